```python
import math
import jax, jax.numpy as jnp
from jax import lax
import numpy as np

D_MODEL = 1024
BATCH = 8
SEQ = 2048
DEPTH = 1

PLE_DIM = 256
D_FF = 2816
CONV_CH = 512
CONV_K = 3
SSM_WIDTH = 512
SSM_GROUP = 16
SSM_GROUPS = SSM_WIDTH // SSM_GROUP
SSM_STATE = 64
ALPHA = (2.0 * DEPTH) ** 0.25
BETA = (8.0 * DEPTH) ** -0.25
LN_EPS = 1e-5
IN_COLS = 3 * CONV_CH + SSM_WIDTH + 2 * D_MODEL

kernel_name = "hybrid_conv_s5_macaron_deepnorm_block"


def layer_norm(x, g, b):
    xf = x.astype(jnp.float32)
    mu = jnp.mean(xf, axis=-1, keepdims=True)
    xc = xf - mu
    var = jnp.mean(xc * xc, axis=-1, keepdims=True)
    y = xc * lax.rsqrt(var + LN_EPS) * g.astype(jnp.float32) + b.astype(jnp.float32)
    return y.astype(x.dtype)


def swiglu(x, w_in, w_out):
    gate, up = jnp.split(x @ w_in, 2, axis=-1)
    return (jax.nn.silu(gate) * up) @ w_out


def causal_depthwise_conv(z, w, b):
    c = z.shape[-1]
    y = lax.conv_general_dilated(
        z, w[:, None, :].astype(z.dtype), window_strides=(1,),
        padding=[(CONV_K - 1, 0)], dimension_numbers=("NWC", "WIO", "NWC"),
        feature_group_count=c)
    return y + b


def s5_scan(u, lam_re, lam_im, log_step, b_re, b_im, c_re, c_im, d_skip):
    f32 = jnp.float32
    lam = lax.complex(lam_re.astype(f32), lam_im.astype(f32))
    dt = jnp.exp(log_step.astype(f32))[:, None]
    lam_bar = jnp.exp(lam * dt)
    b_c = lax.complex(b_re.astype(f32), b_im.astype(f32))
    c_c = lax.complex(c_re.astype(f32), c_im.astype(f32))
    b_bar = ((lam_bar - 1.0) / lam)[..., None] * b_c
    uf = u.astype(f32)
    bu = jnp.einsum("blgi,gni->blgn", uf.astype(jnp.complex64), b_bar)
    a = jnp.broadcast_to(lam_bar, bu.shape)

    def combine(left, right):
        a1, s1 = left
        a2, s2 = right
        return a1 * a2, a2 * s1 + s2

    _, states = lax.associative_scan(combine, (a, bu), axis=1)
    y = jnp.einsum("gin,blgn->blgi", c_c, states).real + d_skip.astype(f32) * uf
    bsz, seq = u.shape[0], u.shape[1]
    return y.reshape(bsz, seq, SSM_WIDTH).astype(u.dtype)


def token_mixer(h, w_in, conv_w, conv_b, conv_w_out, lam_re, lam_im, log_step,
                b_re, b_im, c_re, c_im, d_skip, w_glu, w_out):
    bsz, seq, _ = h.shape
    proj = h @ w_in
    cb, cc, ch, su, g_conv, g_ssm = jnp.split(
        proj, [CONV_CH, 2 * CONV_CH, 3 * CONV_CH, 3 * CONV_CH + SSM_WIDTH,
               3 * CONV_CH + SSM_WIDTH + D_MODEL], axis=-1)
    z = causal_depthwise_conv(cc * ch, conv_w, conv_b)
    y_conv = (cb * z) @ conv_w_out
    s = s5_scan(su.reshape(bsz, seq, SSM_GROUPS, SSM_GROUP), lam_re, lam_im, log_step,
                b_re, b_im, c_re, c_im, d_skip)
    s = jax.nn.gelu(s)
    ga, gb = jnp.split(s @ w_glu, 2, axis=-1)
    y_ssm = ga * jax.nn.sigmoid(gb)
    merged = jax.nn.sigmoid(g_conv) * y_conv + jax.nn.sigmoid(g_ssm) * y_ssm
    return merged @ w_out


def _fwd_setup_inputs(seed: int = 0) -> dict:
    key = jax.random.key(seed)
    ks = jax.random.split(key, 40)
    f32 = jnp.float32
    nrm = lambda k, shape, s: (jax.random.normal(k, shape, f32) * s)
    L = DEPTH

    def gain(k):
        return 1.0 + nrm(k, (L, D_MODEL), 0.01)

    def bias(k, n=D_MODEL):
        return nrm(k, (L, n), 0.01)

    n_idx = jnp.arange(SSM_STATE, dtype=f32)
    lam_re = -0.5 + nrm(ks[20], (L, SSM_GROUPS, SSM_STATE), 0.01)
    lam_im = math.pi * n_idx[None, None, :] + nrm(ks[21], (L, SSM_GROUPS, SSM_STATE), 0.01)
    log_step = jax.random.uniform(ks[22], (L, SSM_GROUPS), f32,
                                  math.log(0.001), math.log(0.1))
    return {
        "x": jax.random.normal(ks[0], (BATCH, SEQ, D_MODEL), f32),
        "p": jax.random.normal(ks[1], (DEPTH, BATCH, SEQ, PLE_DIM), f32),
        "ffn1_w_in": nrm(ks[2], (L, D_MODEL, 2 * D_FF), D_MODEL ** -0.5),
        "ffn1_w_out": nrm(ks[3], (L, D_FF, D_MODEL), BETA * D_FF ** -0.5),
        "ln1_g": gain(ks[4]),
        "ln1_b": bias(ks[5]),
        "mix_w_in": nrm(ks[6], (L, D_MODEL, IN_COLS), D_MODEL ** -0.5),
        "conv_w": nrm(ks[7], (L, CONV_K, CONV_CH), CONV_K ** -0.5),
        "conv_b": bias(ks[8], CONV_CH),
        "conv_w_out": nrm(ks[9], (L, CONV_CH, D_MODEL), BETA * CONV_CH ** -0.5),
        "ssm_lam_re": lam_re,
        "ssm_lam_im": lam_im,
        "ssm_log_step": log_step,
        "ssm_b_re": nrm(ks[23], (L, SSM_GROUPS, SSM_STATE, SSM_GROUP), (2.0 * SSM_GROUP) ** -0.5),
        "ssm_b_im": nrm(ks[24], (L, SSM_GROUPS, SSM_STATE, SSM_GROUP), (2.0 * SSM_GROUP) ** -0.5),
        "ssm_c_re": nrm(ks[25], (L, SSM_GROUPS, SSM_GROUP, SSM_STATE), (2.0 * SSM_STATE) ** -0.5),
        "ssm_c_im": nrm(ks[26], (L, SSM_GROUPS, SSM_GROUP, SSM_STATE), (2.0 * SSM_STATE) ** -0.5),
        "ssm_d": nrm(ks[27], (L, SSM_GROUPS, SSM_GROUP), 1.0),
        "ssm_w_glu": nrm(ks[28], (L, SSM_WIDTH, 2 * D_MODEL), BETA * SSM_WIDTH ** -0.5),
        "mix_w_out": nrm(ks[29], (L, D_MODEL, D_MODEL), BETA * D_MODEL ** -0.5),
        "ln2_g": gain(ks[10]),
        "ln2_b": bias(ks[11]),
        "ffn2_w_in": nrm(ks[12], (L, D_MODEL, 2 * D_FF), D_MODEL ** -0.5),
        "ffn2_w_out": nrm(ks[13], (L, D_FF, D_MODEL), BETA * D_FF ** -0.5),
        "ln3_g": gain(ks[14]),
        "ln3_b": bias(ks[15]),
        "ple_w_in": nrm(ks[16], (L, PLE_DIM, D_MODEL), BETA * PLE_DIM ** -0.5),
        "ple_w_gate": nrm(ks[17], (L, D_MODEL, D_MODEL), D_MODEL ** -0.5),
        "ln4_g": gain(ks[18]),
        "ln4_b": bias(ks[19]),
    }


def _fwd_reference(x, p, ffn1_w_in, ffn1_w_out, ln1_g, ln1_b, mix_w_in, conv_w, conv_b,
              conv_w_out, ssm_lam_re, ssm_lam_im, ssm_log_step, ssm_b_re, ssm_b_im,
              ssm_c_re, ssm_c_im, ssm_d, ssm_w_glu, mix_w_out, ln2_g, ln2_b,
              ffn2_w_in, ffn2_w_out, ln3_g, ln3_b, ple_w_in, ple_w_gate, ln4_g, ln4_b):
    for i in range(DEPTH):
        x = layer_norm(ALPHA * x + 0.5 * swiglu(x, ffn1_w_in[i], ffn1_w_out[i]),
                       ln1_g[i], ln1_b[i])
        mix = token_mixer(x, mix_w_in[i], conv_w[i], conv_b[i], conv_w_out[i],
                          ssm_lam_re[i], ssm_lam_im[i], ssm_log_step[i],
                          ssm_b_re[i], ssm_b_im[i], ssm_c_re[i], ssm_c_im[i], ssm_d[i],
                          ssm_w_glu[i], mix_w_out[i])
        x = layer_norm(ALPHA * x + mix, ln2_g[i], ln2_b[i])
        x = layer_norm(ALPHA * x + 0.5 * swiglu(x, ffn2_w_in[i], ffn2_w_out[i]),
                       ln3_g[i], ln3_b[i])
        e = (p[i] @ ple_w_in[i]) * jax.nn.sigmoid(x @ ple_w_gate[i])
        x = layer_norm(ALPHA * x + e, ln4_g[i], ln4_b[i])
    return x


import jax as _jax
import jax.numpy as _jnp

TWIN_FORMAT = 'train_step'
FWD_PARAMS = ['x', 'p', 'ffn1_w_in', 'ffn1_w_out', 'ln1_g', 'ln1_b', 'mix_w_in', 'conv_w', 'conv_b', 'conv_w_out', 'ssm_lam_re', 'ssm_lam_im', 'ssm_log_step', 'ssm_b_re', 'ssm_b_im', 'ssm_c_re', 'ssm_c_im', 'ssm_d', 'ssm_w_glu', 'mix_w_out', 'ln2_g', 'ln2_b', 'ffn2_w_in', 'ffn2_w_out', 'ln3_g', 'ln3_b', 'ple_w_in', 'ple_w_gate', 'ln4_g', 'ln4_b']
TWIN_WEIGHTS = ['ffn1_w_in', 'ffn1_w_out', 'ln1_g', 'ln1_b', 'mix_w_in', 'conv_w', 'conv_b', 'conv_w_out', 'ssm_lam_re', 'ssm_lam_im', 'ssm_log_step', 'ssm_b_re', 'ssm_b_im', 'ssm_c_re', 'ssm_c_im', 'ssm_d', 'ssm_w_glu', 'mix_w_out', 'ln2_g', 'ln2_b', 'ffn2_w_in', 'ffn2_w_out', 'ln3_g', 'ln3_b', 'ple_w_in', 'ple_w_gate', 'ln4_g', 'ln4_b']
TWIN_DIFF_INPUT = 'x'
TWIN_INPUTS = ['x', 'p', 'ffn1_w_in', 'ffn1_w_out', 'ln1_g', 'ln1_b', 'mix_w_in', 'conv_w', 'conv_b', 'conv_w_out', 'ssm_lam_re', 'ssm_lam_im', 'ssm_log_step', 'ssm_b_re', 'ssm_b_im', 'ssm_c_re', 'ssm_c_im', 'ssm_d', 'ssm_w_glu', 'mix_w_out', 'ln2_g', 'ln2_b', 'ffn2_w_in', 'ffn2_w_out', 'ln3_g', 'ln3_b', 'ple_w_in', 'ple_w_gate', 'ln4_g', 'ln4_b', 'loss_target', 'm_ffn1_w_in', 'm_ffn1_w_out', 'm_ln1_g', 'm_ln1_b', 'm_mix_w_in', 'm_conv_w', 'm_conv_b', 'm_conv_w_out', 'm_ssm_lam_re', 'm_ssm_lam_im', 'm_ssm_log_step', 'm_ssm_b_re', 'm_ssm_b_im', 'm_ssm_c_re', 'm_ssm_c_im', 'm_ssm_d', 'm_ssm_w_glu', 'm_mix_w_out', 'm_ln2_g', 'm_ln2_b', 'm_ffn2_w_in', 'm_ffn2_w_out', 'm_ln3_g', 'm_ln3_b', 'm_ple_w_in', 'm_ple_w_gate', 'm_ln4_g', 'm_ln4_b', 'v_ffn1_w_in', 'v_ffn1_w_out', 'v_ln1_g', 'v_ln1_b', 'v_mix_w_in', 'v_conv_w', 'v_conv_b', 'v_conv_w_out', 'v_ssm_lam_re', 'v_ssm_lam_im', 'v_ssm_log_step', 'v_ssm_b_re', 'v_ssm_b_im', 'v_ssm_c_re', 'v_ssm_c_im', 'v_ssm_d', 'v_ssm_w_glu', 'v_mix_w_out', 'v_ln2_g', 'v_ln2_b', 'v_ffn2_w_in', 'v_ffn2_w_out', 'v_ln3_g', 'v_ln3_b', 'v_ple_w_in', 'v_ple_w_gate', 'v_ln4_g', 'v_ln4_b']
TWIN_OUTPUTS = ['loss', 'grad_x', 'grad_ffn1_w_in', 'grad_ffn1_w_out', 'grad_ln1_g', 'grad_ln1_b', 'grad_mix_w_in', 'grad_conv_w', 'grad_conv_b', 'grad_conv_w_out', 'grad_ssm_lam_re', 'grad_ssm_lam_im', 'grad_ssm_log_step', 'grad_ssm_b_re', 'grad_ssm_b_im', 'grad_ssm_c_re', 'grad_ssm_c_im', 'grad_ssm_d', 'grad_ssm_w_glu', 'grad_mix_w_out', 'grad_ln2_g', 'grad_ln2_b', 'grad_ffn2_w_in', 'grad_ffn2_w_out', 'grad_ln3_g', 'grad_ln3_b', 'grad_ple_w_in', 'grad_ple_w_gate', 'grad_ln4_g', 'grad_ln4_b', 'delta_ffn1_w_in', 'delta_ffn1_w_out', 'delta_ln1_g', 'delta_ln1_b', 'delta_mix_w_in', 'delta_conv_w', 'delta_conv_b', 'delta_conv_w_out', 'delta_ssm_lam_re', 'delta_ssm_lam_im', 'delta_ssm_log_step', 'delta_ssm_b_re', 'delta_ssm_b_im', 'delta_ssm_c_re', 'delta_ssm_c_im', 'delta_ssm_d', 'delta_ssm_w_glu', 'delta_mix_w_out', 'delta_ln2_g', 'delta_ln2_b', 'delta_ffn2_w_in', 'delta_ffn2_w_out', 'delta_ln3_g', 'delta_ln3_b', 'delta_ple_w_in', 'delta_ple_w_gate', 'delta_ln4_g', 'delta_ln4_b', 'new_m_ffn1_w_in', 'new_m_ffn1_w_out', 'new_m_ln1_g', 'new_m_ln1_b', 'new_m_mix_w_in', 'new_m_conv_w', 'new_m_conv_b', 'new_m_conv_w_out', 'new_m_ssm_lam_re', 'new_m_ssm_lam_im', 'new_m_ssm_log_step', 'new_m_ssm_b_re', 'new_m_ssm_b_im', 'new_m_ssm_c_re', 'new_m_ssm_c_im', 'new_m_ssm_d', 'new_m_ssm_w_glu', 'new_m_mix_w_out', 'new_m_ln2_g', 'new_m_ln2_b', 'new_m_ffn2_w_in', 'new_m_ffn2_w_out', 'new_m_ln3_g', 'new_m_ln3_b', 'new_m_ple_w_in', 'new_m_ple_w_gate', 'new_m_ln4_g', 'new_m_ln4_b', 'new_v_ffn1_w_in', 'new_v_ffn1_w_out', 'new_v_ln1_g', 'new_v_ln1_b', 'new_v_mix_w_in', 'new_v_conv_w', 'new_v_conv_b', 'new_v_conv_w_out', 'new_v_ssm_lam_re', 'new_v_ssm_lam_im', 'new_v_ssm_log_step', 'new_v_ssm_b_re', 'new_v_ssm_b_im', 'new_v_ssm_c_re', 'new_v_ssm_c_im', 'new_v_ssm_d', 'new_v_ssm_w_glu', 'new_v_mix_w_out', 'new_v_ln2_g', 'new_v_ln2_b', 'new_v_ffn2_w_in', 'new_v_ffn2_w_out', 'new_v_ln3_g', 'new_v_ln3_b', 'new_v_ple_w_in', 'new_v_ple_w_gate', 'new_v_ln4_g', 'new_v_ln4_b']
TWIN_LEAF_KINDS = {'loss': 'loss', 'grad_x': 'grad_x', 'grad_ffn1_w_in': 'grad_w', 'grad_ffn1_w_out': 'grad_w', 'grad_ln1_g': 'grad_w', 'grad_ln1_b': 'grad_w', 'grad_mix_w_in': 'grad_w', 'grad_conv_w': 'grad_w', 'grad_conv_b': 'grad_w', 'grad_conv_w_out': 'grad_w', 'grad_ssm_lam_re': 'grad_w', 'grad_ssm_lam_im': 'grad_w', 'grad_ssm_log_step': 'grad_w', 'grad_ssm_b_re': 'grad_w', 'grad_ssm_b_im': 'grad_w', 'grad_ssm_c_re': 'grad_w', 'grad_ssm_c_im': 'grad_w', 'grad_ssm_d': 'grad_w', 'grad_ssm_w_glu': 'grad_w', 'grad_mix_w_out': 'grad_w', 'grad_ln2_g': 'grad_w', 'grad_ln2_b': 'grad_w', 'grad_ffn2_w_in': 'grad_w', 'grad_ffn2_w_out': 'grad_w', 'grad_ln3_g': 'grad_w', 'grad_ln3_b': 'grad_w', 'grad_ple_w_in': 'grad_w', 'grad_ple_w_gate': 'grad_w', 'grad_ln4_g': 'grad_w', 'grad_ln4_b': 'grad_w', 'delta_ffn1_w_in': 'delta_w', 'delta_ffn1_w_out': 'delta_w', 'delta_ln1_g': 'delta_w', 'delta_ln1_b': 'delta_w', 'delta_mix_w_in': 'delta_w', 'delta_conv_w': 'delta_w', 'delta_conv_b': 'delta_w', 'delta_conv_w_out': 'delta_w', 'delta_ssm_lam_re': 'delta_w', 'delta_ssm_lam_im': 'delta_w', 'delta_ssm_log_step': 'delta_w', 'delta_ssm_b_re': 'delta_w', 'delta_ssm_b_im': 'delta_w', 'delta_ssm_c_re': 'delta_w', 'delta_ssm_c_im': 'delta_w', 'delta_ssm_d': 'delta_w', 'delta_ssm_w_glu': 'delta_w', 'delta_mix_w_out': 'delta_w', 'delta_ln2_g': 'delta_w', 'delta_ln2_b': 'delta_w', 'delta_ffn2_w_in': 'delta_w', 'delta_ffn2_w_out': 'delta_w', 'delta_ln3_g': 'delta_w', 'delta_ln3_b': 'delta_w', 'delta_ple_w_in': 'delta_w', 'delta_ple_w_gate': 'delta_w', 'delta_ln4_g': 'delta_w', 'delta_ln4_b': 'delta_w', 'new_m_ffn1_w_in': 'new_m', 'new_m_ffn1_w_out': 'new_m', 'new_m_ln1_g': 'new_m', 'new_m_ln1_b': 'new_m', 'new_m_mix_w_in': 'new_m', 'new_m_conv_w': 'new_m', 'new_m_conv_b': 'new_m', 'new_m_conv_w_out': 'new_m', 'new_m_ssm_lam_re': 'new_m', 'new_m_ssm_lam_im': 'new_m', 'new_m_ssm_log_step': 'new_m', 'new_m_ssm_b_re': 'new_m', 'new_m_ssm_b_im': 'new_m', 'new_m_ssm_c_re': 'new_m', 'new_m_ssm_c_im': 'new_m', 'new_m_ssm_d': 'new_m', 'new_m_ssm_w_glu': 'new_m', 'new_m_mix_w_out': 'new_m', 'new_m_ln2_g': 'new_m', 'new_m_ln2_b': 'new_m', 'new_m_ffn2_w_in': 'new_m', 'new_m_ffn2_w_out': 'new_m', 'new_m_ln3_g': 'new_m', 'new_m_ln3_b': 'new_m', 'new_m_ple_w_in': 'new_m', 'new_m_ple_w_gate': 'new_m', 'new_m_ln4_g': 'new_m', 'new_m_ln4_b': 'new_m', 'new_v_ffn1_w_in': 'new_v', 'new_v_ffn1_w_out': 'new_v', 'new_v_ln1_g': 'new_v', 'new_v_ln1_b': 'new_v', 'new_v_mix_w_in': 'new_v', 'new_v_conv_w': 'new_v', 'new_v_conv_b': 'new_v', 'new_v_conv_w_out': 'new_v', 'new_v_ssm_lam_re': 'new_v', 'new_v_ssm_lam_im': 'new_v', 'new_v_ssm_log_step': 'new_v', 'new_v_ssm_b_re': 'new_v', 'new_v_ssm_b_im': 'new_v', 'new_v_ssm_c_re': 'new_v', 'new_v_ssm_c_im': 'new_v', 'new_v_ssm_d': 'new_v', 'new_v_ssm_w_glu': 'new_v', 'new_v_mix_w_out': 'new_v', 'new_v_ln2_g': 'new_v', 'new_v_ln2_b': 'new_v', 'new_v_ffn2_w_in': 'new_v', 'new_v_ffn2_w_out': 'new_v', 'new_v_ln3_g': 'new_v', 'new_v_ln3_b': 'new_v', 'new_v_ple_w_in': 'new_v', 'new_v_ple_w_gate': 'new_v', 'new_v_ln4_g': 'new_v', 'new_v_ln4_b': 'new_v'}


def _forward(args):
    return _fwd_reference(*[args[k] for k in FWD_PARAMS])


def _output_shape():
    out = _jax.eval_shape(lambda: _forward(_fwd_setup_inputs(0)))
    return out.shape, out.dtype

N_MICROBATCH = 1
ADAM_LR = 0.001
ADAM_B1 = 0.9
ADAM_B2 = 0.999
ADAM_EPS = 1e-08
ADAM_WD = 0.01
ADAM_STEP = 10
PER_EXAMPLE_BATCH_AXIS = {'x': 0, 'p': 1, 'loss_target': 0}
SHARED_INPUTS = []
_WEIGHT_DTYPES = {'ffn1_w_in': _jnp.float32, 'ffn1_w_out': _jnp.float32, 'ln1_g': _jnp.float32, 'ln1_b': _jnp.float32, 'mix_w_in': _jnp.float32, 'conv_w': _jnp.float32, 'conv_b': _jnp.float32, 'conv_w_out': _jnp.float32, 'ssm_lam_re': _jnp.float32, 'ssm_lam_im': _jnp.float32, 'ssm_log_step': _jnp.float32, 'ssm_b_re': _jnp.float32, 'ssm_b_im': _jnp.float32, 'ssm_c_re': _jnp.float32, 'ssm_c_im': _jnp.float32, 'ssm_d': _jnp.float32, 'ssm_w_glu': _jnp.float32, 'mix_w_out': _jnp.float32, 'ln2_g': _jnp.float32, 'ln2_b': _jnp.float32, 'ffn2_w_in': _jnp.float32, 'ffn2_w_out': _jnp.float32, 'ln3_g': _jnp.float32, 'ln3_b': _jnp.float32, 'ple_w_in': _jnp.float32, 'ple_w_gate': _jnp.float32, 'ln4_g': _jnp.float32, 'ln4_b': _jnp.float32}
MOMENT_SCALE = {'ffn1_w_in': 1.141344e-02, 'ffn1_w_out': 3.130289e-02, 'ln1_g': 2.574547e-01, 'ln1_b': 1.554615e-01, 'mix_w_in': 1.763167e-02, 'conv_w': 2.802431e-02, 'conv_b': 2.985511e-02, 'conv_w_out': 3.256065e-02, 'ssm_lam_re': 3.844132e-04, 'ssm_lam_im': 4.527088e-04, 'ssm_log_step': 2.811757e-01, 'ssm_b_re': 2.848860e-04, 'ssm_b_im': 2.885468e-04, 'ssm_c_re': 5.845631e-04, 'ssm_c_im': 5.783806e-04, 'ssm_d': 1.083278e-02, 'ssm_w_glu': 8.099066e-03, 'mix_w_out': 3.426867e-02, 'ln2_g': 2.638524e-01, 'ln2_b': 1.585432e-01, 'ffn2_w_in': 1.098794e-02, 'ffn2_w_out': 3.010735e-02, 'ln3_g': 2.690462e-01, 'ln3_b': 1.586683e-01, 'ple_w_in': 5.519528e-02, 'ple_w_gate': 1.277151e-02, 'ln4_g': 1.600323e+01, 'ln4_b': 3.959698e-01}


def _to_microbatches(a, axis):
    t = _jnp.moveaxis(a, axis, 0)
    t = t.reshape((N_MICROBATCH, t.shape[0] // N_MICROBATCH) + t.shape[1:])
    return _jnp.moveaxis(t, 1, axis + 1)


def setup_inputs(seed: int = 0) -> dict:
    inp = _fwd_setup_inputs(seed)
    key = _jax.random.fold_in(_jax.random.key(seed), 7919)
    shape, _ = _output_shape()
    out = dict(inp)
    out["loss_target"] = _jax.random.normal(_jax.random.fold_in(key, 0), shape, _jnp.float32)
    for i, name in enumerate(TWIN_WEIGHTS):
        w = inp[name].astype(_jnp.float32)
        if MOMENT_SCALE is None:
            s = _jnp.sqrt(_jnp.mean(_jnp.square(w)) + 1e-30)
        else:
            s = MOMENT_SCALE[name]
        km, kv = _jax.random.split(_jax.random.fold_in(key, i + 1))
        out[name] = w
        out["m_" + name] = s * _jax.random.normal(km, w.shape, _jnp.float32)
        out["v_" + name] = (s * s) * _jax.random.uniform(kv, w.shape, _jnp.float32, 0.5, 1.5)
    if N_MICROBATCH > 1:
        for name, axis in PER_EXAMPLE_BATCH_AXIS.items():
            out[name] = _to_microbatches(out[name], axis)
    return {'x': out['x'], 'p': out['p'], 'ffn1_w_in': out['ffn1_w_in'], 'ffn1_w_out': out['ffn1_w_out'], 'ln1_g': out['ln1_g'], 'ln1_b': out['ln1_b'], 'mix_w_in': out['mix_w_in'], 'conv_w': out['conv_w'], 'conv_b': out['conv_b'], 'conv_w_out': out['conv_w_out'], 'ssm_lam_re': out['ssm_lam_re'], 'ssm_lam_im': out['ssm_lam_im'], 'ssm_log_step': out['ssm_log_step'], 'ssm_b_re': out['ssm_b_re'], 'ssm_b_im': out['ssm_b_im'], 'ssm_c_re': out['ssm_c_re'], 'ssm_c_im': out['ssm_c_im'], 'ssm_d': out['ssm_d'], 'ssm_w_glu': out['ssm_w_glu'], 'mix_w_out': out['mix_w_out'], 'ln2_g': out['ln2_g'], 'ln2_b': out['ln2_b'], 'ffn2_w_in': out['ffn2_w_in'], 'ffn2_w_out': out['ffn2_w_out'], 'ln3_g': out['ln3_g'], 'ln3_b': out['ln3_b'], 'ple_w_in': out['ple_w_in'], 'ple_w_gate': out['ple_w_gate'], 'ln4_g': out['ln4_g'], 'ln4_b': out['ln4_b'], 'loss_target': out['loss_target'], 'm_ffn1_w_in': out['m_ffn1_w_in'], 'm_ffn1_w_out': out['m_ffn1_w_out'], 'm_ln1_g': out['m_ln1_g'], 'm_ln1_b': out['m_ln1_b'], 'm_mix_w_in': out['m_mix_w_in'], 'm_conv_w': out['m_conv_w'], 'm_conv_b': out['m_conv_b'], 'm_conv_w_out': out['m_conv_w_out'], 'm_ssm_lam_re': out['m_ssm_lam_re'], 'm_ssm_lam_im': out['m_ssm_lam_im'], 'm_ssm_log_step': out['m_ssm_log_step'], 'm_ssm_b_re': out['m_ssm_b_re'], 'm_ssm_b_im': out['m_ssm_b_im'], 'm_ssm_c_re': out['m_ssm_c_re'], 'm_ssm_c_im': out['m_ssm_c_im'], 'm_ssm_d': out['m_ssm_d'], 'm_ssm_w_glu': out['m_ssm_w_glu'], 'm_mix_w_out': out['m_mix_w_out'], 'm_ln2_g': out['m_ln2_g'], 'm_ln2_b': out['m_ln2_b'], 'm_ffn2_w_in': out['m_ffn2_w_in'], 'm_ffn2_w_out': out['m_ffn2_w_out'], 'm_ln3_g': out['m_ln3_g'], 'm_ln3_b': out['m_ln3_b'], 'm_ple_w_in': out['m_ple_w_in'], 'm_ple_w_gate': out['m_ple_w_gate'], 'm_ln4_g': out['m_ln4_g'], 'm_ln4_b': out['m_ln4_b'], 'v_ffn1_w_in': out['v_ffn1_w_in'], 'v_ffn1_w_out': out['v_ffn1_w_out'], 'v_ln1_g': out['v_ln1_g'], 'v_ln1_b': out['v_ln1_b'], 'v_mix_w_in': out['v_mix_w_in'], 'v_conv_w': out['v_conv_w'], 'v_conv_b': out['v_conv_b'], 'v_conv_w_out': out['v_conv_w_out'], 'v_ssm_lam_re': out['v_ssm_lam_re'], 'v_ssm_lam_im': out['v_ssm_lam_im'], 'v_ssm_log_step': out['v_ssm_log_step'], 'v_ssm_b_re': out['v_ssm_b_re'], 'v_ssm_b_im': out['v_ssm_b_im'], 'v_ssm_c_re': out['v_ssm_c_re'], 'v_ssm_c_im': out['v_ssm_c_im'], 'v_ssm_d': out['v_ssm_d'], 'v_ssm_w_glu': out['v_ssm_w_glu'], 'v_mix_w_out': out['v_mix_w_out'], 'v_ln2_g': out['v_ln2_g'], 'v_ln2_b': out['v_ln2_b'], 'v_ffn2_w_in': out['v_ffn2_w_in'], 'v_ffn2_w_out': out['v_ffn2_w_out'], 'v_ln3_g': out['v_ln3_g'], 'v_ln3_b': out['v_ln3_b'], 'v_ple_w_in': out['v_ple_w_in'], 'v_ple_w_gate': out['v_ple_w_gate'], 'v_ln4_g': out['v_ln4_g'], 'v_ln4_b': out['v_ln4_b']}


def _loss(weights, diff, rest, loss_target):
    with _jax.named_scope("forward"):
        args = {**rest, TWIN_DIFF_INPUT: diff, **{k: w.astype(_WEIGHT_DTYPES[k]) for k, w in weights.items()}}
        y = _forward(args)
    with _jax.named_scope("loss_head"):
        err = _jnp.square(y.astype(_jnp.float32) - loss_target)
        return 0.5 * _jnp.sum(_jnp.mean(err, axis=-1)) if err.ndim else 0.5 * err


def _adamw(w, g, m, v):
    m = ADAM_B1 * m + (1.0 - ADAM_B1) * g
    v = ADAM_B2 * v + (1.0 - ADAM_B2) * _jnp.square(g)
    m_hat = m / (1.0 - ADAM_B1 ** ADAM_STEP)
    v_hat = v / (1.0 - ADAM_B2 ** ADAM_STEP)
    delta = -ADAM_LR * (m_hat / (_jnp.sqrt(v_hat) + ADAM_EPS) + ADAM_WD * w)
    return delta, m, v


def reference(x, p, ffn1_w_in, ffn1_w_out, ln1_g, ln1_b, mix_w_in, conv_w, conv_b, conv_w_out, ssm_lam_re, ssm_lam_im, ssm_log_step, ssm_b_re, ssm_b_im, ssm_c_re, ssm_c_im, ssm_d, ssm_w_glu, mix_w_out, ln2_g, ln2_b, ffn2_w_in, ffn2_w_out, ln3_g, ln3_b, ple_w_in, ple_w_gate, ln4_g, ln4_b, loss_target, m_ffn1_w_in, m_ffn1_w_out, m_ln1_g, m_ln1_b, m_mix_w_in, m_conv_w, m_conv_b, m_conv_w_out, m_ssm_lam_re, m_ssm_lam_im, m_ssm_log_step, m_ssm_b_re, m_ssm_b_im, m_ssm_c_re, m_ssm_c_im, m_ssm_d, m_ssm_w_glu, m_mix_w_out, m_ln2_g, m_ln2_b, m_ffn2_w_in, m_ffn2_w_out, m_ln3_g, m_ln3_b, m_ple_w_in, m_ple_w_gate, m_ln4_g, m_ln4_b, v_ffn1_w_in, v_ffn1_w_out, v_ln1_g, v_ln1_b, v_mix_w_in, v_conv_w, v_conv_b, v_conv_w_out, v_ssm_lam_re, v_ssm_lam_im, v_ssm_log_step, v_ssm_b_re, v_ssm_b_im, v_ssm_c_re, v_ssm_c_im, v_ssm_d, v_ssm_w_glu, v_mix_w_out, v_ln2_g, v_ln2_b, v_ffn2_w_in, v_ffn2_w_out, v_ln3_g, v_ln3_b, v_ple_w_in, v_ple_w_gate, v_ln4_g, v_ln4_b):
    given = dict(x=x, p=p, ffn1_w_in=ffn1_w_in, ffn1_w_out=ffn1_w_out, ln1_g=ln1_g, ln1_b=ln1_b, mix_w_in=mix_w_in, conv_w=conv_w, conv_b=conv_b, conv_w_out=conv_w_out, ssm_lam_re=ssm_lam_re, ssm_lam_im=ssm_lam_im, ssm_log_step=ssm_log_step, ssm_b_re=ssm_b_re, ssm_b_im=ssm_b_im, ssm_c_re=ssm_c_re, ssm_c_im=ssm_c_im, ssm_d=ssm_d, ssm_w_glu=ssm_w_glu, mix_w_out=mix_w_out, ln2_g=ln2_g, ln2_b=ln2_b, ffn2_w_in=ffn2_w_in, ffn2_w_out=ffn2_w_out, ln3_g=ln3_g, ln3_b=ln3_b, ple_w_in=ple_w_in, ple_w_gate=ple_w_gate, ln4_g=ln4_g, ln4_b=ln4_b, loss_target=loss_target, m_ffn1_w_in=m_ffn1_w_in, m_ffn1_w_out=m_ffn1_w_out, m_ln1_g=m_ln1_g, m_ln1_b=m_ln1_b, m_mix_w_in=m_mix_w_in, m_conv_w=m_conv_w, m_conv_b=m_conv_b, m_conv_w_out=m_conv_w_out, m_ssm_lam_re=m_ssm_lam_re, m_ssm_lam_im=m_ssm_lam_im, m_ssm_log_step=m_ssm_log_step, m_ssm_b_re=m_ssm_b_re, m_ssm_b_im=m_ssm_b_im, m_ssm_c_re=m_ssm_c_re, m_ssm_c_im=m_ssm_c_im, m_ssm_d=m_ssm_d, m_ssm_w_glu=m_ssm_w_glu, m_mix_w_out=m_mix_w_out, m_ln2_g=m_ln2_g, m_ln2_b=m_ln2_b, m_ffn2_w_in=m_ffn2_w_in, m_ffn2_w_out=m_ffn2_w_out, m_ln3_g=m_ln3_g, m_ln3_b=m_ln3_b, m_ple_w_in=m_ple_w_in, m_ple_w_gate=m_ple_w_gate, m_ln4_g=m_ln4_g, m_ln4_b=m_ln4_b, v_ffn1_w_in=v_ffn1_w_in, v_ffn1_w_out=v_ffn1_w_out, v_ln1_g=v_ln1_g, v_ln1_b=v_ln1_b, v_mix_w_in=v_mix_w_in, v_conv_w=v_conv_w, v_conv_b=v_conv_b, v_conv_w_out=v_conv_w_out, v_ssm_lam_re=v_ssm_lam_re, v_ssm_lam_im=v_ssm_lam_im, v_ssm_log_step=v_ssm_log_step, v_ssm_b_re=v_ssm_b_re, v_ssm_b_im=v_ssm_b_im, v_ssm_c_re=v_ssm_c_re, v_ssm_c_im=v_ssm_c_im, v_ssm_d=v_ssm_d, v_ssm_w_glu=v_ssm_w_glu, v_mix_w_out=v_mix_w_out, v_ln2_g=v_ln2_g, v_ln2_b=v_ln2_b, v_ffn2_w_in=v_ffn2_w_in, v_ffn2_w_out=v_ffn2_w_out, v_ln3_g=v_ln3_g, v_ln3_b=v_ln3_b, v_ple_w_in=v_ple_w_in, v_ple_w_gate=v_ple_w_gate, v_ln4_g=v_ln4_g, v_ln4_b=v_ln4_b)
    weights = {n: given[n] for n in TWIN_WEIGHTS}
    shared = {n: given[n] for n in SHARED_INPUTS}
    per_example = {n: given[n] for n in ['x', 'p']}
    grad_fn = _jax.value_and_grad(_loss, argnums=(0, 1))

    def one_microbatch(ex, loss_target):
        ex = dict(ex)
        diff = ex.pop(TWIN_DIFF_INPUT)
        return grad_fn(weights, diff, {**shared, **ex}, loss_target)

    if N_MICROBATCH == 1:
        loss, (grad_w, grad_x) = one_microbatch(per_example, given["loss_target"])
    else:
        def body(carry, xs):
            loss_sum, grad_sum = carry
            l_k, (gw_k, gx_k) = one_microbatch(xs[0], xs[1])
            with _jax.named_scope("update"):
                return (loss_sum + l_k, _jax.tree.map(_jnp.add, grad_sum, gw_k)), gx_k

        init = (_jnp.zeros((), _jnp.float32), _jax.tree.map(_jnp.zeros_like, weights))
        (loss, grad_w), grad_x = _jax.lax.scan(body, init, (per_example, given["loss_target"]))
    with _jax.named_scope("update"):
        delta_w, new_m, new_v = {}, {}, {}
        for n in TWIN_WEIGHTS:
            delta_w[n], new_m[n], new_v[n] = _adamw(weights[n], grad_w[n], given["m_" + n], given["v_" + n])
    return (loss, grad_x, *[grad_w[n] for n in TWIN_WEIGHTS], *[delta_w[n] for n in TWIN_WEIGHTS],
            *[new_m[n] for n in TWIN_WEIGHTS], *[new_v[n] for n in TWIN_WEIGHTS])
```

```python
import functools
import math

import jax
import jax.numpy as jnp
import numpy as np
from jax import lax
from jax.experimental import pallas as pl
from jax.experimental.pallas import tpu as pltpu

F32, BF16 = jnp.float32, jnp.bfloat16
D = 1024
FF = 2816
FFH = FF // 2
CONV = 512
SSM = 512
GROUPS = 32
STATE = 64
LANES = GROUPS * STATE
SCAN_W = 256
SCAN_R = 256
ALPHA = 2.0 ** 0.25
LN_EPS = 1e-5
GELU_C = math.sqrt(2.0 / math.pi)
B1, B2, LR, EPS, WD, STEP = 0.9, 0.999, 0.001, 1e-8, 0.01, 10
MESH = pl.DeviceIdType.MESH
ANY = pl.BlockSpec(memory_space=pl.ANY)
VMEM_FULL = pl.BlockSpec(memory_space=pltpu.VMEM)


def _cp(vmem_mb=48, n_axes=1):
    return pltpu.CompilerParams(vmem_limit_bytes=vmem_mb << 20,
                                dimension_semantics=("arbitrary",) * n_axes)


def _nn(a, b):
    return jnp.dot(a, b, preferred_element_type=F32)


def _nt(a, b):
    return lax.dot_general(a, b, (((1,), (1,)), ((), ())), preferred_element_type=F32)


def _tn(a, b):
    return lax.dot_general(a, b, (((0,), (0,)), ((), ())), preferred_element_type=F32)


def _sig(v):
    return jax.nn.sigmoid(v)


def _ln_stats(r):
    mu = jnp.mean(r, axis=-1, keepdims=True)
    xc = r - mu
    var = jnp.mean(xc * xc, axis=-1, keepdims=True)
    rstd = lax.rsqrt(var + LN_EPS)
    return xc * rstd, rstd


def _ln_bwd(dy, r, g):
    xhat, rstd = _ln_stats(r)
    dyg = dy * g
    m1 = jnp.mean(dyg, axis=-1, keepdims=True)
    m2 = jnp.mean(dyg * xhat, axis=-1, keepdims=True)
    return rstd * (dyg - m1 - xhat * m2), xhat


def _rowsum(v):
    return jnp.sum(v, axis=0, keepdims=True)


def _ffn_fwd(x, xb, w_in4, w_out2, g, b, tm, name):
    T = x.shape[0]

    def body(x_ref, xb_ref, wg_ref, wu_ref, wo_ref, g_ref, b_ref, h_ref, r_ref, xo_ref, xob_ref, acc):
        k = pl.program_id(1)

        @pl.when(k == 0)
        def _():
            acc[...] = jnp.zeros_like(acc)

        xv = xb_ref[...]
        gt = _nn(xv, wg_ref[...])
        up = _nn(xv, wu_ref[...])
        a = (gt * _sig(gt) * up).astype(BF16)
        for kk in range(2):
            @pl.when(k == kk)
            def _():
                h_ref[:, kk * FFH:(kk + 1) * FFH] = gt.astype(BF16)
                h_ref[:, (kk + 2) * FFH:(kk + 3) * FFH] = up.astype(BF16)
        acc[...] += _nn(a, wo_ref[...])

        @pl.when(k == 1)
        def _():
            r = ALPHA * x_ref[...] + 0.5 * acc[...]
            xhat, _ = _ln_stats(r)
            xo = xhat * g_ref[...] + b_ref[...]
            r_ref[...] = r
            xo_ref[...] = xo
            xob_ref[...] = xo.astype(BF16)

    tok = pl.BlockSpec((tm, D), lambda i, k: (i, 0))
    vec = pl.BlockSpec((1, D), lambda i, k: (0, 0))
    return pl.pallas_call(
        body, name=name, grid=(T // tm, 2),
        in_specs=[tok, tok,
                  pl.BlockSpec((None, D, FFH), lambda i, k: (k, 0, 0)),
                  pl.BlockSpec((None, D, FFH), lambda i, k: (k + 2, 0, 0)),
                  pl.BlockSpec((None, FFH, D), lambda i, k: (k, 0, 0)),
                  vec, vec],
        out_specs=[pl.BlockSpec((tm, 2 * FF), lambda i, k: (i, 0)), tok, tok, tok],
        out_shape=[jax.ShapeDtypeStruct((T, 2 * FF), BF16), jax.ShapeDtypeStruct((T, D), F32),
                   jax.ShapeDtypeStruct((T, D), F32), jax.ShapeDtypeStruct((T, D), BF16)],
        scratch_shapes=[pltpu.VMEM((tm, D), F32)],
        compiler_params=_cp(56, 2),
    )(x, xb, w_in4, w_in4, w_out2, g, b)


def _ffn_bwd(dy, r, g, h, w_in4, w_out2, tm, name):
    T = dy.shape[0]

    def body(dy_ref, r_ref, g_ref, h_ref, wg_ref, wu_ref, wo_ref,
             dx_ref, dh_ref, a_ref, df_ref, dg_ref, db_ref, acc, dr_s, dfb_s):
        i, k = pl.program_id(0), pl.program_id(1)

        @pl.when(k == 0)
        def _():
            dyv = dy_ref[...]
            dr, xhat = _ln_bwd(dyv, r_ref[...], g_ref[...])
            pg, pb = _rowsum(dyv * xhat), _rowsum(dyv)

            @pl.when(i == 0)
            def _():
                dg_ref[...] = pg
                db_ref[...] = pb

            @pl.when(i > 0)
            def _():
                dg_ref[...] += pg
                db_ref[...] += pb

            dr_s[...] = dr
            dfb = (0.5 * dr).astype(BF16)
            dfb_s[...] = dfb
            df_ref[...] = dfb
            acc[...] = jnp.zeros_like(acc)

        da = _nt(dfb_s[...], wo_ref[...])
        for kk in range(2):
            @pl.when(k == kk)
            def _():
                gt = h_ref[:, kk * FFH:(kk + 1) * FFH].astype(F32)
                up = h_ref[:, (kk + 2) * FFH:(kk + 3) * FFH].astype(F32)
                sg = _sig(gt)
                silu = gt * sg
                dgate = (da * up * (sg * (1.0 + gt * (1.0 - sg)))).astype(BF16)
                dup = (da * silu).astype(BF16)
                a_ref[...] = (silu * up).astype(BF16)
                dh_ref[:, kk * FFH:(kk + 1) * FFH] = dgate
                dh_ref[:, (kk + 2) * FFH:(kk + 3) * FFH] = dup
                acc[...] += _nt(dgate, wg_ref[...]) + _nt(dup, wu_ref[...])

        @pl.when(k == 1)
        def _():
            dx_ref[...] = ALPHA * dr_s[...] + acc[...]

    tok = pl.BlockSpec((tm, D), lambda i, k: (i, 0))
    vec = pl.BlockSpec((1, D), lambda i, k: (0, 0))
    wide = pl.BlockSpec((tm, 2 * FF), lambda i, k: (i, 0))
    return pl.pallas_call(
        body, name=name, grid=(T // tm, 2),
        in_specs=[tok, tok, vec, wide,
                  pl.BlockSpec((None, D, FFH), lambda i, k: (k, 0, 0)),
                  pl.BlockSpec((None, D, FFH), lambda i, k: (k + 2, 0, 0)),
                  pl.BlockSpec((None, FFH, D), lambda i, k: (k, 0, 0))],
        out_specs=[tok, wide, pl.BlockSpec((tm, FFH), lambda i, k: (i, k)), tok, vec, vec],
        out_shape=[jax.ShapeDtypeStruct((T, D), F32), jax.ShapeDtypeStruct((T, 2 * FF), BF16),
                   jax.ShapeDtypeStruct((T, FF), BF16), jax.ShapeDtypeStruct((T, D), BF16),
                   jax.ShapeDtypeStruct((1, D), F32), jax.ShapeDtypeStruct((1, D), F32)],
        scratch_shapes=[pltpu.VMEM((tm, D), F32), pltpu.VMEM((tm, D), F32), pltpu.VMEM((tm, D), BF16)],
        compiler_params=_cp(56, 2),
    )(dy, r, g, h, w_in4, w_in4, w_out2)


def _mm_tn(a, b, tk, tn, tt, name, shard_cols=None):
    T, K = a.shape
    N = b.shape[1]

    def body(a_ref, b_ref, o_ref):
        part = _tn(a_ref[...], b_ref[...])

        @pl.when(pl.program_id(2) == 0)
        def _():
            o_ref[...] = part

        @pl.when(pl.program_id(2) > 0)
        def _():
            o_ref[...] += part

    if shard_cols is None:
        out_shape = jax.ShapeDtypeStruct((K, N), F32)
        out_spec = pl.BlockSpec((tk, tn), lambda ki, nj, t: (ki, nj))
    else:
        per = shard_cols // tn
        out_shape = jax.ShapeDtypeStruct((N // shard_cols, K, shard_cols), F32)
        out_spec = pl.BlockSpec((None, tk, tn), lambda ki, nj, t: (nj // per, ki, nj % per))
    return pl.pallas_call(
        body, name=name, grid=(K // tk, N // tn, T // tt),
        in_specs=[pl.BlockSpec((tt, tk), lambda ki, nj, t: (t, ki)),
                  pl.BlockSpec((tt, tn), lambda ki, nj, t: (t, nj))],
        out_specs=out_spec, out_shape=out_shape,
        compiler_params=_cp(48, 3),
    )(a, b)


def _mix_fwd_a(xb, w_mix4, conv_w, conv_b, w_co4, tm):
    T = xb.shape[0]

    def body(xb_ref, w_ref, cw_ref, cb_ref, wco_ref,
             pc_ref, z_ref, yin_ref, su_ref, sub_ref, gc_ref, gs_ref, yc_ref, qbuf):
        @pl.when(pl.program_id(0) == 0)
        def _():
            qbuf[pl.ds(0, 8), :] = jnp.zeros((8, CONV), F32)

        xv = xb_ref[...]
        p0 = _nn(xv, w_ref[0])
        p1 = _nn(xv, w_ref[1])
        gc_ref[...] = _nn(xv, w_ref[2])
        gs_ref[...] = _nn(xv, w_ref[3])
        cbv, ccv = p0[:, :CONV], p0[:, CONV:]
        chv, suv = p1[:, :CONV], p1[:, CONV:]
        q = ccv * chv
        qbuf[pl.ds(8, tm), :] = q
        cw = cw_ref[...]
        z = (cw[2:3] * q + cw[1:2] * qbuf[pl.ds(7, tm), :] + cw[0:1] * qbuf[pl.ds(6, tm), :]
             + cb_ref[...])
        qbuf[pl.ds(0, 8), :] = q[tm - 8:tm]
        yin = (cbv * z).astype(BF16)
        pc_ref[:, 0:CONV] = cbv.astype(BF16)
        pc_ref[:, CONV:2 * CONV] = ccv.astype(BF16)
        pc_ref[:, 2 * CONV:3 * CONV] = chv.astype(BF16)
        z_ref[...] = z.astype(BF16)
        yin_ref[...] = yin
        su_ref[...] = suv
        sub_ref[...] = suv.astype(BF16)
        for k in range(4):
            yc_ref[:, 256 * k:256 * (k + 1)] = _nn(yin, wco_ref[k])

    def tok(n):
        return pl.BlockSpec((tm, n), lambda i: (i, 0))

    def full(shape):
        return pl.BlockSpec(shape, lambda i: (0,) * len(shape))

    return pl.pallas_call(
        body, name="mix_fwd_a", grid=(T // tm,),
        in_specs=[tok(D), full((4, D, D)), full((3, CONV)), full((1, CONV)), full((4, CONV, 256))],
        out_specs=[tok(3 * CONV), tok(CONV), tok(CONV), tok(SSM), tok(SSM), tok(D), tok(D), tok(D)],
        out_shape=[jax.ShapeDtypeStruct((T, 3 * CONV), BF16), jax.ShapeDtypeStruct((T, CONV), BF16),
                   jax.ShapeDtypeStruct((T, CONV), BF16), jax.ShapeDtypeStruct((T, SSM), F32),
                   jax.ShapeDtypeStruct((T, SSM), BF16), jax.ShapeDtypeStruct((T, D), F32),
                   jax.ShapeDtypeStruct((T, D), F32), jax.ShapeDtypeStruct((T, D), F32)],
        scratch_shapes=[pltpu.VMEM((tm + 8, CONV), F32)],
        compiler_params=_cp(56, 1),
    )(xb, w_mix4, conv_w, conv_b, w_co4)


def _scan_inplace(bre, bim, ar, ai, T, rev):
    R = SCAN_R
    if rev:
        ai = -ai
    d = 1
    while d < T:
        if d < 8:
            def step(i, _, d=d, ar=ar, ai=ai):
                c = i if rev else T // R - 1 - i
                t0 = pl.multiple_of(c * R, R)
                if rev:
                    wr = bre[pl.ds(t0 + 8, R + 8), :]
                    wi = bim[pl.ds(t0 + 8, R + 8), :]
                    shr = pltpu.roll(wr, R + 8 - d, 0)[0:R]
                    shi = pltpu.roll(wi, R + 8 - d, 0)[0:R]
                    cr, ci = wr[0:R], wi[0:R]
                else:
                    wr = bre[pl.ds(t0, R + 8), :]
                    wi = bim[pl.ds(t0, R + 8), :]
                    shr = pltpu.roll(wr, d, 0)[8:8 + R]
                    shi = pltpu.roll(wi, d, 0)[8:8 + R]
                    cr, ci = wr[8:8 + R], wi[8:8 + R]
                bre[pl.ds(t0 + 8, R), :] = cr + ar * shr - ai * shi
                bim[pl.ds(t0 + 8, R), :] = ci + ar * shi + ai * shr
                return 0

            lax.fori_loop(0, T // R, step, 0)
        else:
            def upd(lo, n, d=d, ar=ar, ai=ai):
                src = lo + d if rev else lo - d
                if not isinstance(lo, int):
                    lo, src = pl.multiple_of(lo + 8, 8), pl.multiple_of(src + 8, 8)
                else:
                    lo, src = lo + 8, src + 8
                cr = bre[pl.ds(lo, n), :]
                ci = bim[pl.ds(lo, n), :]
                shr = bre[pl.ds(src, n), :]
                shi = bim[pl.ds(src, n), :]
                bre[pl.ds(lo, n), :] = cr + ar * shr - ai * shi
                bim[pl.ds(lo, n), :] = ci + ar * shi + ai * shr

            nfull = (T - d) // R if d >= R else T // R - 1

            def step(i, _, upd=upd, d=d):
                if rev:
                    t0 = i * R
                else:
                    t0 = T - (i + 1) * R
                upd(t0, R)
                return 0

            if nfull > 0:
                lax.fori_loop(0, nfull, step, 0)
            if d < R:
                if rev:
                    upd(T - R, R - d)
                else:
                    upd(d, R - d)
        ar, ai = ar * ar - ai * ai, 2.0 * ar * ai
        d *= 2


def _scan_specs(T):
    W = SCAN_W
    lane = pl.BlockSpec((T, W), lambda j: (0, j))
    col = pl.BlockSpec((T, 128), lambda j: (0, j // 2))
    wb = pl.BlockSpec((None, 128, W), lambda j: (j, 0, 0))
    wc = pl.BlockSpec((None, W, 128), lambda j: (j, 0, 0))
    vec = pl.BlockSpec((1, W), lambda j: (0, j))
    return lane, col, wb, wc, vec


def _s5_scan_fwd(su_b, wb_re, wb_im, a_re, a_im):
    T = su_b.shape[0]
    W = SCAN_W

    def body(su_ref, wbr_ref, wbi_ref, ar_ref, ai_ref, sr_ref, si_ref, bre, bim):
        zero = jnp.zeros((8, W), F32)
        for buf in (bre, bim):
            buf[pl.ds(0, 8), :] = zero
            buf[pl.ds(T + 8, 8), :] = zero
        su = su_ref[...]
        bre[pl.ds(8, T), :] = _nn(su, wbr_ref[...])
        bim[pl.ds(8, T), :] = _nn(su, wbi_ref[...])
        _scan_inplace(bre, bim, ar_ref[...], ai_ref[...], T, rev=False)
        sr_ref[...] = bre[pl.ds(8, T), :]
        si_ref[...] = bim[pl.ds(8, T), :]

    lane, col, wb, wc, vec = _scan_specs(T)
    return pl.pallas_call(
        body, name="s5_scan_fwd", grid=(LANES // W,),
        in_specs=[col, wb, wb, vec, vec],
        out_specs=[lane, lane],
        out_shape=[jax.ShapeDtypeStruct((T, LANES), F32)] * 2,
        scratch_shapes=[pltpu.VMEM((T + 16, W), F32)] * 2,
        compiler_params=_cp(48, 1),
    )(su_b, wb_re, wb_im, a_re, a_im)


def _gelu(s):
    th = jnp.tanh(GELU_C * (s + 0.044715 * s * s * s))
    return 0.5 * s * (1.0 + th), th


def _mix_fwd_b(st_re, st_im, wc_re4, wc_im4, su, dvec, w_glu4, g_conv, g_ssm, y_conv, w_mo, x1, g, b, tm):
    T = su.shape[0]

    def body(sr_ref, si_ref, wcr_ref, wci_ref, su_ref, d_ref, wg_ref, gc_ref, gs_ref, yc_ref, wmo_ref,
             x_ref, g_ref, b_ref, s_ref, sgb_ref, ga_ref, gb_ref, mb_ref, r_ref, xo_ref, xob_ref):
        srb = sr_ref[...].astype(BF16)
        sib = si_ref[...].astype(BF16)
        ys = [_nn(srb[:, 512 * J:512 * (J + 1)], wcr_ref[J]) + _nn(sib[:, 512 * J:512 * (J + 1)], wci_ref[J])
              for J in range(4)]
        s = jnp.concatenate(ys, axis=1) + d_ref[...] * su_ref[...]
        sg, _ = _gelu(s)
        sgb = sg.astype(BF16)
        ga = jnp.concatenate([_nn(sgb, wg_ref[0]), _nn(sgb, wg_ref[1])], axis=1)
        gb = jnp.concatenate([_nn(sgb, wg_ref[2]), _nn(sgb, wg_ref[3])], axis=1)
        merged = _sig(gc_ref[...]) * yc_ref[...] + _sig(gs_ref[...]) * (ga * _sig(gb))
        mb = merged.astype(BF16)
        r = ALPHA * x_ref[...] + _nn(mb, wmo_ref[...])
        xhat, _ = _ln_stats(r)
        xo = xhat * g_ref[...] + b_ref[...]
        s_ref[...] = s
        sgb_ref[...] = sgb
        ga_ref[...] = ga
        gb_ref[...] = gb
        mb_ref[...] = mb
        r_ref[...] = r
        xo_ref[...] = xo
        xob_ref[...] = xo.astype(BF16)

    def tok(n):
        return pl.BlockSpec((tm, n), lambda i: (i, 0))

    def full(shape):
        return pl.BlockSpec(shape, lambda i: (0,) * len(shape))

    return pl.pallas_call(
        body, name="mix_fwd_b", grid=(T // tm,),
        in_specs=[tok(LANES), tok(LANES), full((4, 512, 128)), full((4, 512, 128)), tok(SSM), full((1, SSM)),
                  full((4, SSM, 512)), tok(D), tok(D), tok(D), full((D, D)), tok(D), full((1, D)), full((1, D))],
        out_specs=[tok(SSM), tok(SSM), tok(D), tok(D), tok(D), tok(D), tok(D), tok(D)],
        out_shape=[jax.ShapeDtypeStruct((T, SSM), F32), jax.ShapeDtypeStruct((T, SSM), BF16),
                   jax.ShapeDtypeStruct((T, D), F32), jax.ShapeDtypeStruct((T, D), F32),
                   jax.ShapeDtypeStruct((T, D), BF16), jax.ShapeDtypeStruct((T, D), F32),
                   jax.ShapeDtypeStruct((T, D), F32), jax.ShapeDtypeStruct((T, D), BF16)],
        compiler_params=_cp(56, 1),
    )(st_re, st_im, wc_re4, wc_im4, su, dvec, w_glu4, g_conv, g_ssm, y_conv, w_mo, x1, g, b)


def _ple_loss(x3, x3b, p, w_pi4, w_pg, g, b, target, tm):
    T = x3.shape[0]
    PD = p.shape[1]

    def body(x_ref, xb_ref, p_ref, wpi_ref, wpg_ref, g_ref, b_ref, t_ref,
             loss_ref, dx_ref, pb_ref, dpw_ref, dgt_ref, dg_ref, db_ref):
        i = pl.program_id(0)
        pb = p_ref[...].astype(BF16)
        pw = jnp.concatenate([_nn(pb, wpi_ref[k]) for k in range(4)], axis=1)
        gt = _nn(xb_ref[...], wpg_ref[...])
        sg = _sig(gt)
        r = ALPHA * x_ref[...] + pw * sg
        gv = g_ref[...]
        xhat, rstd = _ln_stats(r)
        err = xhat * gv + b_ref[...] - t_ref[...]
        lpart = jnp.zeros((1, 128), F32) + 0.5 * jnp.sum(jnp.mean(err * err, axis=-1, keepdims=True))
        dy = err * (1.0 / D)
        dyg = dy * gv
        m1 = jnp.mean(dyg, axis=-1, keepdims=True)
        m2 = jnp.mean(dyg * xhat, axis=-1, keepdims=True)
        dr = rstd * (dyg - m1 - xhat * m2)
        pg, pbias = _rowsum(dy * xhat), _rowsum(dy)

        @pl.when(i == 0)
        def _():
            loss_ref[...] = lpart
            dg_ref[...] = pg
            db_ref[...] = pbias

        @pl.when(i > 0)
        def _():
            loss_ref[...] += lpart
            dg_ref[...] += pg
            db_ref[...] += pbias

        dgt = (dr * pw * sg * (1.0 - sg)).astype(BF16)
        pb_ref[...] = pb
        dpw_ref[...] = (dr * sg).astype(BF16)
        dgt_ref[...] = dgt
        dx_ref[...] = ALPHA * dr + _nt(dgt, wpg_ref[...])

    def tok(n):
        return pl.BlockSpec((tm, n), lambda i: (i, 0))

    def full(shape):
        return pl.BlockSpec(shape, lambda i: (0,) * len(shape))

    return pl.pallas_call(
        body, name="ple_loss", grid=(T // tm,),
        in_specs=[tok(D), tok(D), tok(PD), full((4, PD, 256)), full((D, D)), full((1, D)), full((1, D)), tok(D)],
        out_specs=[full((1, 128)), tok(D), tok(PD), tok(D), tok(D), full((1, D)), full((1, D))],
        out_shape=[jax.ShapeDtypeStruct((1, 128), F32), jax.ShapeDtypeStruct((T, D), F32),
                   jax.ShapeDtypeStruct((T, PD), BF16), jax.ShapeDtypeStruct((T, D), BF16),
                   jax.ShapeDtypeStruct((T, D), BF16), jax.ShapeDtypeStruct((1, D), F32),
                   jax.ShapeDtypeStruct((1, D), F32)],
        compiler_params=_cp(48, 1),
    )(x3, x3b, p, w_pi4, w_pg, g, b, target)


def _mix_bwd_b(dy, r2, g, w_mo, g_conv, g_ssm, y_conv, ga, gb, s, su, dvec, w_glu4, wc_re4, wc_im4, tm):
    T = dy.shape[0]

    def body(dy_ref, r_ref, g_ref, wmo_ref, gc_ref, gs_ref, yc_ref, ga_ref, gb_ref, s_ref, su_ref, d_ref,
             wg_ref, wcr_ref, wci_ref,
             dres_ref, dmix_ref, dgl_ref, dsb_ref, dud_ref, gsr_ref, gsi_ref, dyc_ref, dp_ref,
             dg_ref, db_ref, dd_ref):
        i = pl.program_id(0)
        dyv = dy_ref[...]
        dr, xhat = _ln_bwd(dyv, r_ref[...], g_ref[...])
        dmix = dr.astype(BF16)
        dmerged = _nt(dmix, wmo_ref[...])
        sc, ss, sgb = _sig(gc_ref[...]), _sig(gs_ref[...]), _sig(gb_ref[...])
        gav = ga_ref[...]
        yssm = gav * sgb
        dgc = dmerged * yc_ref[...] * sc * (1.0 - sc)
        dgss = dmerged * yssm * ss * (1.0 - ss)
        dyssm = dmerged * ss
        dgl = jnp.concatenate([dyssm * sgb, dyssm * gav * sgb * (1.0 - sgb)], axis=1).astype(BF16)
        dsg = (_nt(dgl[:, 0:512], wg_ref[0]) + _nt(dgl[:, 512:1024], wg_ref[1])
               + _nt(dgl[:, 1024:1536], wg_ref[2]) + _nt(dgl[:, 1536:2048], wg_ref[3]))
        sv = s_ref[...]
        _, th = _gelu(sv)
        dgelu = 0.5 * (1.0 + th) + 0.5 * sv * (1.0 - th * th) * GELU_C * (1.0 + 3.0 * 0.044715 * sv * sv)
        ds = dsg * dgelu
        dsb = ds.astype(BF16)
        pg, pb, pd = _rowsum(dyv * xhat), _rowsum(dyv), _rowsum(ds * su_ref[...])

        @pl.when(i == 0)
        def _():
            dg_ref[...] = pg
            db_ref[...] = pb
            dd_ref[...] = pd

        @pl.when(i > 0)
        def _():
            dg_ref[...] += pg
            db_ref[...] += pb
            dd_ref[...] += pd

        dres_ref[...] = ALPHA * dr
        dmix_ref[...] = dmix
        dgl_ref[...] = dgl
        dsb_ref[...] = dsb
        dud_ref[...] = ds * d_ref[...]
        for J in range(4):
            gsr_ref[:, 512 * J:512 * (J + 1)] = _nt(dsb[:, 128 * J:128 * (J + 1)], wcr_ref[J])
            gsi_ref[:, 512 * J:512 * (J + 1)] = _nt(dsb[:, 128 * J:128 * (J + 1)], wci_ref[J])
        dyc_ref[...] = (dmerged * sc).astype(BF16)
        dp_ref[:, 0:D] = dgc.astype(BF16)
        dp_ref[:, D:2 * D] = dgss.astype(BF16)

    def tok(n):
        return pl.BlockSpec((tm, n), lambda i: (i, 0))

    def full(shape):
        return pl.BlockSpec(shape, lambda i: (0,) * len(shape))

    return pl.pallas_call(
        body, name="mix_bwd_b", grid=(T // tm,),
        in_specs=[tok(D), tok(D), full((1, D)), full((D, D)), tok(D), tok(D), tok(D), tok(D), tok(D),
                  tok(SSM), tok(SSM), full((1, SSM)), full((4, SSM, 512)), full((4, 512, 128)), full((4, 512, 128))],
        out_specs=[tok(D), tok(D), tok(2 * D), tok(SSM), tok(SSM), tok(LANES), tok(LANES), tok(D),
                   pl.BlockSpec((tm, 2 * D), lambda i: (i, 1)), full((1, D)), full((1, D)), full((1, SSM))],
        out_shape=[jax.ShapeDtypeStruct((T, D), F32), jax.ShapeDtypeStruct((T, D), BF16),
                   jax.ShapeDtypeStruct((T, 2 * D), BF16), jax.ShapeDtypeStruct((T, SSM), BF16),
                   jax.ShapeDtypeStruct((T, SSM), F32), jax.ShapeDtypeStruct((T, LANES), F32),
                   jax.ShapeDtypeStruct((T, LANES), F32), jax.ShapeDtypeStruct((T, D), BF16),
                   jax.ShapeDtypeStruct((T, 4 * D), BF16), jax.ShapeDtypeStruct((1, D), F32),
                   jax.ShapeDtypeStruct((1, D), F32), jax.ShapeDtypeStruct((1, SSM), F32)],
        compiler_params=_cp(56, 1),
    )(dy, r2, g, w_mo, g_conv, g_ssm, y_conv, ga, gb, s, su, dvec, w_glu4, wc_re4, wc_im4)


def _s5_scan_bwd(gs_re, gs_im, st_re, st_im, su_b, ds_b, wb_re, wb_im, a_re, a_im):
    T = su_b.shape[0]
    W = SCAN_W
    R = SCAN_R

    def body(gr_ref, gi_ref, sr_ref, si_ref, su_ref, ds_ref, wbr_ref, wbi_ref, ar_ref, ai_ref,
             dsu_ref, dwbr_ref, dwbi_ref, dwcr_ref, dwci_ref, dar_ref, dai_ref, gre, gim):
        j = pl.program_id(0)
        zero = jnp.zeros((8, W), F32)
        for buf in (gre, gim):
            buf[pl.ds(0, 8), :] = zero
            buf[pl.ds(T + 8, 8), :] = zero
        gre[pl.ds(8, T), :] = gr_ref[...]
        gim[pl.ds(8, T), :] = gi_ref[...]
        _scan_inplace(gre, gim, ar_ref[...], ai_ref[...], T, rev=True)
        grb = gre[pl.ds(8, T), :].astype(BF16)
        gib = gim[pl.ds(8, T), :].astype(BF16)
        part = _nt(grb, wbr_ref[...]) + _nt(gib, wbi_ref[...])

        @pl.when(j % 2 == 0)
        def _():
            dsu_ref[...] = part

        @pl.when(j % 2 == 1)
        def _():
            dsu_ref[...] += part

        su = su_ref[...]
        dwbr_ref[...] = _tn(su, grb)
        dwbi_ref[...] = _tn(su, gib)
        dsv = ds_ref[...]
        dwcr_ref[...] = _tn(sr_ref[...].astype(BF16), dsv)
        dwci_ref[...] = _tn(si_ref[...].astype(BF16), dsv)
        dar = jnp.zeros((1, W), F32)
        dai = jnp.zeros((1, W), F32)
        for c in range(T // R):
            xr = sr_ref[pl.ds(c * R, R), :]
            xi = si_ref[pl.ds(c * R, R), :]
            g1r = gre[pl.ds(c * R + 9, R), :]
            g1i = gim[pl.ds(c * R + 9, R), :]
            dar = dar + _rowsum(g1r * xr + g1i * xi)
            dai = dai + _rowsum(g1i * xr - g1r * xi)
        dar_ref[...] = dar
        dai_ref[...] = dai

    lane, col, wb, wc, vec = _scan_specs(T)
    return pl.pallas_call(
        body, name="s5_scan_bwd", grid=(LANES // W,),
        in_specs=[lane, lane, lane, lane, col, col, wb, wb, vec, vec],
        out_specs=[col, wb, wb, wc, wc, vec, vec],
        out_shape=[jax.ShapeDtypeStruct((T, SSM), F32),
                   jax.ShapeDtypeStruct((LANES // W, 128, W), F32), jax.ShapeDtypeStruct((LANES // W, 128, W), F32),
                   jax.ShapeDtypeStruct((LANES // W, W, 128), F32), jax.ShapeDtypeStruct((LANES // W, W, 128), F32),
                   jax.ShapeDtypeStruct((1, LANES), F32), jax.ShapeDtypeStruct((1, LANES), F32)],
        scratch_shapes=[pltpu.VMEM((T + 16, W), F32)] * 2,
        compiler_params=_cp(56, 1),
    )(gs_re, gs_im, st_re, st_im, su_b, ds_b, wb_re, wb_im, a_re, a_im)


def _mix_bwd_a(dyc_b, w_co4, pc, z_b, conv_w, dsu_ssm, du_dir, dproj, dres, w_mix4, tm):
    T = dres.shape[0]
    nt = T // tm

    def body(dyc_ref, wco_ref, pc_ref, halo_ref, z_ref, cw_ref, dsu_ref, dud_ref, dpin_ref, dres_ref, w_ref,
             dp_ref, dx_ref, dcw_ref, dcb_ref, dzbuf, qbuf):
        i = pl.program_id(0)
        ii = nt - 1 - i

        @pl.when(i == 0)
        def _():
            dzbuf[pl.ds(tm, 8), :] = jnp.zeros((8, CONV), F32)

        dyc = dyc_ref[...]
        dyin = (_nt(dyc[:, 0:256], wco_ref[0]) + _nt(dyc[:, 256:512], wco_ref[1])
                + _nt(dyc[:, 512:768], wco_ref[2]) + _nt(dyc[:, 768:1024], wco_ref[3]))
        cbv = pc_ref[:, 0:CONV].astype(F32)
        ccv = pc_ref[:, CONV:2 * CONV].astype(F32)
        chv = pc_ref[:, 2 * CONV:3 * CONV].astype(F32)
        dcbv = dyin * z_ref[...].astype(F32)
        dz = dyin * cbv
        dzbuf[pl.ds(0, tm), :] = dz
        cw = cw_ref[...]
        dq = cw[2:3] * dz + cw[1:2] * dzbuf[pl.ds(1, tm), :] + cw[0:1] * dzbuf[pl.ds(2, tm), :]
        dzbuf[pl.ds(tm, 8), :] = dz[0:8]
        q = ccv * chv
        hq = halo_ref[:, CONV:2 * CONV].astype(F32) * halo_ref[:, 2 * CONV:3 * CONV].astype(F32)
        qbuf[pl.ds(0, 8), :] = jnp.where(ii > 0, hq, jnp.zeros_like(hq))
        qbuf[pl.ds(8, tm), :] = q
        pw = jnp.concatenate([_rowsum(dz * qbuf[pl.ds(6, tm), :]), _rowsum(dz * qbuf[pl.ds(7, tm), :]),
                              _rowsum(dz * q), jnp.zeros((5, CONV), F32)], axis=0)
        pbias = _rowsum(dz)

        @pl.when(i == 0)
        def _():
            dcw_ref[...] = pw
            dcb_ref[...] = pbias

        @pl.when(i > 0)
        def _():
            dcw_ref[...] += pw
            dcb_ref[...] += pbias

        dp0 = jnp.concatenate([dcbv, dq * chv], axis=1).astype(BF16)
        dp1 = jnp.concatenate([dq * ccv, dsu_ref[...] + dud_ref[...]], axis=1).astype(BF16)
        dp_ref[:, 0:D] = dp0
        dp_ref[:, D:2 * D] = dp1
        dx_ref[...] = (dres_ref[...] + _nt(dp0, w_ref[0]) + _nt(dp1, w_ref[1])
                       + _nt(dpin_ref[:, 0:D], w_ref[2]) + _nt(dpin_ref[:, D:2 * D], w_ref[3]))

    def tok(n):
        return pl.BlockSpec((tm, n), lambda i: (nt - 1 - i, 0))

    def full(shape):
        return pl.BlockSpec(shape, lambda i: (0,) * len(shape))

    halo = pl.BlockSpec((8, 3 * CONV), lambda i: (jnp.maximum((nt - 1 - i) * (tm // 8) - 1, 0), 0))
    return pl.pallas_call(
        body, name="mix_bwd_a", grid=(nt,),
        in_specs=[tok(D), full((4, CONV, 256)), tok(3 * CONV), halo, tok(CONV), full((3, CONV)),
                  tok(SSM), tok(SSM), pl.BlockSpec((tm, 2 * D), lambda i: (nt - 1 - i, 1)), tok(D),
                  full((4, D, D))],
        out_specs=[pl.BlockSpec((tm, 2 * D), lambda i: (nt - 1 - i, 0)), tok(D), full((8, CONV)), full((1, CONV))],
        out_shape=[jax.ShapeDtypeStruct((T, 4 * D), BF16), jax.ShapeDtypeStruct((T, D), F32),
                   jax.ShapeDtypeStruct((8, CONV), F32), jax.ShapeDtypeStruct((1, CONV), F32)],
        scratch_shapes=[pltpu.VMEM((tm + 8, CONV), F32), pltpu.VMEM((tm + 8, CONV), F32)],
        input_output_aliases={8: 0},
        compiler_params=_cp(56, 1),
    )(dyc_b, w_co4, pc, pc, z_b, conv_w, dsu_ssm, du_dir, dproj, dres, w_mix4)


def _zoh(lam_re, lam_im, log_step, b_re, b_im):
    dt = jnp.exp(log_step)[:, None]
    mag = jnp.exp(lam_re * dt)
    abr, abi = mag * jnp.cos(lam_im * dt), mag * jnp.sin(lam_im * dt)
    nr, ni = abr - 1.0, abi
    den = lam_re * lam_re + lam_im * lam_im
    cr = (nr * lam_re + ni * lam_im) / den
    ci = (ni * lam_re - nr * lam_im) / den
    bbr = cr[..., None] * b_re - ci[..., None] * b_im
    bbi = cr[..., None] * b_im + ci[..., None] * b_re
    return abr, abi, bbr, bbi


def _wb_blocks(bb):
    eye = jnp.eye(GROUPS, dtype=F32)
    full = jnp.einsum("gni,gh->gihn", bb, eye).reshape(4, 128, 8, SCAN_W)
    return jnp.stack([full[j // 2, :, j, :] for j in range(8)]).astype(BF16)


def _wc_blocks(cc):
    eye = jnp.eye(GROUPS, dtype=F32)
    full = jnp.einsum("gin,gh->gnhi", cc, eye).reshape(4, 512, 4, 128)
    return jnp.stack([full[J, :, J, :] for J in range(4)]).astype(BF16)


_G = np.arange(GROUPS)


def _wb_diag(dwb8):
    d5 = dwb8.reshape(8, 8, 16, 4, 64)
    return d5[_G // 4, _G % 8, :, _G % 4, :].transpose(0, 2, 1)


def _wc_diag(dwc8):
    d5 = dwc8.reshape(8, 4, 64, 8, 16)
    return d5[_G // 4, _G % 4, :, _G % 8, :].transpose(0, 2, 1)


def _where():
    x, y, c = lax.axis_index("x"), lax.axis_index("y"), lax.axis_index("c")
    return x, y, c, 2 * x + y


def _chip_dev(k, c):
    return (k // 2, k % 2, c)


def _gather_weights(shards):
    n = len(shards)

    def body(*refs):
        ins, outs = refs[:n], refs[n:2 * n]
        ici_s, ici_r, fw_s, fw_r, loc = refs[2 * n:]
        x, y, c, me = _where()
        sib = (x, y, 1 - c)
        local = [pltpu.make_async_copy(ins[w], outs[w].at[me], loc.at[w]) for w in range(n)]
        for cp in local:
            cp.start()

        def half(w, k, cc):
            h = shards[w].shape[0] // 2
            return outs[w].at[k, pl.ds(cc * h, h)]

        def my_half(w):
            h = shards[w].shape[0] // 2
            return ins[w].at[pl.ds(c * h, h)]

        sends = []
        for w in range(n):
            for s in range(3):
                k = (me + 1 + s) % 4
                sends.append(pltpu.make_async_remote_copy(
                    src_ref=my_half(w), dst_ref=half(w, me, c), send_sem=ici_s.at[3 * w + s],
                    recv_sem=ici_r.at[3 * w + s], device_id=_chip_dev(k, c), device_id_type=MESH))
        for cp in sends:
            cp.start()
        fwd = []
        for w in range(n):
            for s in range(3):
                j = (me + 3 - s) % 4
                pltpu.make_async_remote_copy(
                    src_ref=my_half(w), dst_ref=half(w, j, c), send_sem=ici_s.at[3 * w + s],
                    recv_sem=ici_r.at[3 * w + s], device_id=sib, device_id_type=MESH).wait_recv()
                f = pltpu.make_async_remote_copy(
                    src_ref=half(w, j, c), dst_ref=half(w, j, c), send_sem=fw_s.at[3 * w + s],
                    recv_sem=fw_r.at[3 * w + s], device_id=sib, device_id_type=MESH)
                f.start()
                fwd.append(f)
        for w in range(n):
            for s in range(3):
                j = (me + 3 - s) % 4
                pltpu.make_async_remote_copy(
                    src_ref=half(w, j, 1 - c), dst_ref=half(w, j, 1 - c), send_sem=fw_s.at[3 * w + s],
                    recv_sem=fw_r.at[3 * w + s], device_id=sib, device_id_type=MESH).wait_recv()
        for cp in sends + fwd:
            cp.wait_send()
        for cp in local:
            cp.wait()

    return pl.pallas_call(
        body, name="gather_weights",
        in_specs=[ANY] * n, out_specs=[ANY] * n,
        out_shape=[jax.ShapeDtypeStruct((4,) + s.shape, s.dtype) for s in shards],
        scratch_shapes=[pltpu.SemaphoreType.DMA((3 * n,))] * 4 + [pltpu.SemaphoreType.DMA((n,))],
        compiler_params=pltpu.CompilerParams(has_side_effects=True),
    )(*shards)


def _pair_swap_halves(g4s):
    n = len(g4s)

    def body(*refs):
        ins, outs = refs[:n], refs[n:2 * n]
        ss, rs = refs[2 * n:]
        x, y, c, me = _where()
        sib = (x, y, 1 - c)
        cps = []
        for w in range(n):
            h = g4s[w].shape[1] // 2
            cps.append(pltpu.make_async_remote_copy(
                src_ref=ins[w].at[:, pl.ds((1 - c) * h, h)], dst_ref=outs[w], send_sem=ss.at[w],
                recv_sem=rs.at[w], device_id=sib, device_id_type=MESH))
        for cp in cps:
            cp.start()
        for cp in cps:
            cp.wait()

    return pl.pallas_call(
        body, name="pair_swap_halves",
        in_specs=[ANY] * n, out_specs=[ANY] * n,
        out_shape=[jax.ShapeDtypeStruct((4, g.shape[1] // 2, g.shape[2]), g.dtype) for g in g4s],
        scratch_shapes=[pltpu.SemaphoreType.DMA((n,))] * 2,
        compiler_params=pltpu.CompilerParams(has_side_effects=True),
    )(*g4s)


def _chip_exchange(pbs):
    n = len(pbs)

    def body(*refs):
        ins, outs = refs[:n], refs[n:2 * n]
        ss, rs = refs[2 * n:]
        x, y, c, me = _where()
        cps = []
        for w in range(n):
            for s in range(3):
                k = (me + 1 + s) % 4
                cps.append(pltpu.make_async_remote_copy(
                    src_ref=ins[w].at[k], dst_ref=outs[w].at[2 - s], send_sem=ss.at[3 * w + s],
                    recv_sem=rs.at[3 * w + s], device_id=_chip_dev(k, c), device_id_type=MESH))
        for cp in cps:
            cp.start()
        for cp in cps:
            cp.wait()

    return pl.pallas_call(
        body, name="chip_exchange",
        in_specs=[ANY] * n, out_specs=[ANY] * n,
        out_shape=[jax.ShapeDtypeStruct((3,) + p.shape[1:], p.dtype) for p in pbs],
        scratch_shapes=[pltpu.SemaphoreType.DMA((3 * n,))] * 2,
        compiler_params=pltpu.CompilerParams(has_side_effects=True),
    )(*pbs)


def _pair_join_halves(halves):
    n = len(halves)

    def body(*refs):
        ins, outs = refs[:n], refs[n:2 * n]
        ss, rs, loc = refs[2 * n:]
        x, y, c, me = _where()
        sib = (x, y, 1 - c)
        cps, local = [], []
        for w in range(n):
            h = halves[w].shape[0]
            mine = outs[w].at[pl.ds(c * h, h)]
            local.append(pltpu.make_async_copy(ins[w], mine, loc.at[w]))
            cps.append(pltpu.make_async_remote_copy(
                src_ref=ins[w], dst_ref=mine, send_sem=ss.at[w], recv_sem=rs.at[w],
                device_id=sib, device_id_type=MESH))
        for cp in local + cps:
            cp.start()
        for w in range(n):
            h = halves[w].shape[0]
            theirs = outs[w].at[pl.ds((1 - c) * h, h)]
            pltpu.make_async_remote_copy(
                src_ref=ins[w], dst_ref=theirs, send_sem=ss.at[w], recv_sem=rs.at[w],
                device_id=sib, device_id_type=MESH).wait_recv()
        for cp in cps:
            cp.wait_send()
        for cp in local:
            cp.wait()

    return pl.pallas_call(
        body, name="pair_join_halves",
        in_specs=[ANY] * n, out_specs=[ANY] * n,
        out_shape=[jax.ShapeDtypeStruct((2 * a.shape[0],) + a.shape[1:], a.dtype) for a in halves],
        scratch_shapes=[pltpu.SemaphoreType.DMA((n,))] * 3,
        compiler_params=pltpu.CompilerParams(has_side_effects=True),
    )(*halves)


def _allreduce_small(v):
    P = v.shape[0]

    def body(v_ref, o_ref, buf, ss, rs):
        x, y, c, me = _where()
        lin = 4 * x + 2 * y + c
        buf[pl.ds(lin, 1)] = v_ref[...][None]
        cps = []
        for o in range(1, 8):
            t = (lin + o) % 8
            cps.append(pltpu.make_async_remote_copy(
                src_ref=v_ref, dst_ref=buf.at[lin], send_sem=ss.at[o - 1], recv_sem=rs.at[o - 1],
                device_id=(t // 4, (t // 2) % 2, t % 2), device_id_type=MESH))
        for cp in cps:
            cp.start()
        for o in range(1, 8):
            src = (lin + 8 - o) % 8
            pltpu.make_async_remote_copy(
                src_ref=v_ref, dst_ref=buf.at[src], send_sem=ss.at[o - 1], recv_sem=rs.at[o - 1],
                device_id=(x, y, c), device_id_type=MESH).wait_recv()
        for cp in cps:
            cp.wait_send()
        acc = buf[0]
        for d in range(1, 8):
            acc = acc + buf[d]
        o_ref[...] = acc

    return pl.pallas_call(
        body, name="allreduce_small",
        in_specs=[VMEM_FULL], out_specs=VMEM_FULL,
        out_shape=jax.ShapeDtypeStruct(v.shape, F32),
        scratch_shapes=[pltpu.VMEM((8, P, 1024), F32), pltpu.SemaphoreType.DMA((7,)), pltpu.SemaphoreType.DMA((7,))],
        compiler_params=pltpu.CompilerParams(vmem_limit_bytes=32 << 20, has_side_effects=True),
    )(v)


def _row_tile(h):
    for t in (256, 176, 128, 64, 32, 16, 8):
        if h % t == 0:
            return t
    raise ValueError(h)


def _pair_sum(cidx, g4, got, name):
    _, R, C = g4.shape
    h = R // 2
    th = _row_tile(h)

    def body(c_ref, a_ref, b_ref, o_ref, ob_ref):
        sm = a_ref[...] + b_ref[...]
        o_ref[...] = sm
        ob_ref[...] = sm.astype(BF16)

    blk = pl.BlockSpec((None, th, C), lambda k, i, c: (k, i, 0))
    gs = pltpu.PrefetchScalarGridSpec(
        num_scalar_prefetch=1, grid=(4, h // th),
        in_specs=[pl.BlockSpec((None, None, th, C), lambda k, i, c: (k, c[0], i, 0)), blk],
        out_specs=[blk, blk])
    return pl.pallas_call(
        body, name=name, grid_spec=gs,
        out_shape=[jax.ShapeDtypeStruct((4, h, C), F32), jax.ShapeDtypeStruct((4, h, C), BF16)],
        compiler_params=_cp(32, 2),
    )(cidx, g4.reshape(4, 2, h, C), got)


def _chip_sum(meidx, p32, got, name):
    _, h, C = p32.shape
    th = _row_tile(h)

    def body(m_ref, a_ref, b_ref, o_ref):
        o_ref[...] = ((a_ref[...] + b_ref[0].astype(F32)) + b_ref[1].astype(F32)) + b_ref[2].astype(F32)

    gs = pltpu.PrefetchScalarGridSpec(
        num_scalar_prefetch=1, grid=(h // th,),
        in_specs=[pl.BlockSpec((None, th, C), lambda i, m: (m[0], i, 0)),
                  pl.BlockSpec((3, th, C), lambda i, m: (0, i, 0))],
        out_specs=pl.BlockSpec((th, C), lambda i, m: (i, 0)))
    return pl.pallas_call(
        body, name=name, grid_spec=gs, out_shape=jax.ShapeDtypeStruct((h, C), F32),
        compiler_params=_cp(32, 1),
    )(meidx, p32, got)


def _adamw_math(w, g, m, v):
    m2 = B1 * m + (1.0 - B1) * g
    v2 = B2 * v + (1.0 - B2) * (g * g)
    m_hat = m2 / (1.0 - B1 ** STEP)
    v_hat = v2 / (1.0 - B2 ** STEP)
    delta = -LR * (m_hat / (jnp.sqrt(v_hat) + EPS) + WD * w)
    return delta, m2, v2


def _adamw(w, g, m, v, name):
    R, C = w.shape
    tr = _row_tile(R)

    def body(w_ref, g_ref, m_ref, v_ref, d_ref, mo_ref, vo_ref):
        d, m2, v2 = _adamw_math(w_ref[...], g_ref[...], m_ref[...], v_ref[...])
        d_ref[...] = d
        mo_ref[...] = m2
        vo_ref[...] = v2

    blk = pl.BlockSpec((tr, C), lambda i: (i, 0))
    return pl.pallas_call(
        body, name=name, grid=(R // tr,), in_specs=[blk] * 4, out_specs=[blk] * 3,
        out_shape=[jax.ShapeDtypeStruct((R, C), F32)] * 3,
        compiler_params=_cp(32, 1),
    )(w, g, m, v)


def _pack(arrs):
    flat = jnp.concatenate([a.reshape(-1).astype(F32) for a in arrs])
    rows = -(-flat.shape[0] // 1024)
    rows = -(-rows // 8) * 8
    return jnp.pad(flat, (0, rows * 1024 - flat.shape[0])).reshape(rows, 1024)


def _unpack(packed, shapes):
    flat = packed.reshape(-1)
    out, off = [], 0
    for s in shapes:
        n = math.prod(s)
        out.append(flat[off:off + n].reshape(s))
        off += n
    return out


BIG = ["ffn1_w_in", "ffn1_w_out", "mix_w_in", "conv_w_out", "ssm_w_glu", "mix_w_out",
       "ffn2_w_in", "ffn2_w_out", "ple_w_in", "ple_w_gate"]
SMALL = ["ln1_g", "ln1_b", "conv_w", "conv_b", "ssm_lam_re", "ssm_lam_im", "ssm_log_step", "ssm_b_re", "ssm_b_im",
         "ssm_c_re", "ssm_c_im", "ssm_d", "ln2_g", "ln2_b", "ln3_g", "ln3_b", "ln4_g", "ln4_b"]
WEIGHTS = ["ffn1_w_in", "ffn1_w_out", "ln1_g", "ln1_b", "mix_w_in", "conv_w", "conv_b", "conv_w_out",
           "ssm_lam_re", "ssm_lam_im", "ssm_log_step", "ssm_b_re", "ssm_b_im", "ssm_c_re", "ssm_c_im", "ssm_d",
           "ssm_w_glu", "mix_w_out", "ln2_g", "ln2_b", "ffn2_w_in", "ffn2_w_out", "ln3_g", "ln3_b",
           "ple_w_in", "ple_w_gate", "ln4_g", "ln4_b"]


def _local_step(x, p, target, W, sp, tm_ffn, tm_mix):
    T = x.shape[0]
    abr, abi, bbr, bbi = _zoh(sp["ssm_lam_re"], sp["ssm_lam_im"], sp["ssm_log_step"], sp["ssm_b_re"], sp["ssm_b_im"])
    wb_re, wb_im = _wb_blocks(bbr), _wb_blocks(bbi)
    wc_re4, wc_im4 = _wc_blocks(sp["ssm_c_re"]), _wc_blocks(-sp["ssm_c_im"])
    a_re, a_im = abr.reshape(1, LANES), abi.reshape(1, LANES)
    dvec = sp["ssm_d"].reshape(1, SSM)

    w1o2 = W["ffn1_w_out"].reshape(2, FFH, D)
    w2o2 = W["ffn2_w_out"].reshape(2, FFH, D)
    w_mo = W["mix_w_out"].reshape(D, D)
    w_pg = W["ple_w_gate"].reshape(D, D)

    xb = x.astype(BF16)
    h1, r1, x1, x1b = _ffn_fwd(x, xb, W["ffn1_w_in"], w1o2, sp["ln1_g"], sp["ln1_b"], tm_ffn, "ffn1_fwd")
    pc, z_b, yin_b, su, su_b, g_conv, g_ssm, y_conv = _mix_fwd_a(
        x1b, W["mix_w_in"], sp["conv_w"], sp["conv_b"], W["conv_w_out"], tm_mix)
    st_re, st_im = _s5_scan_fwd(su_b, wb_re, wb_im, a_re, a_im)
    s, sg_b, ga, gb, merged_b, r2, x2, x2b = _mix_fwd_b(
        st_re, st_im, wc_re4, wc_im4, su, dvec, W["ssm_w_glu"], g_conv, g_ssm, y_conv, w_mo, x1,
        sp["ln2_g"], sp["ln2_b"], tm_mix)
    h2, r3, x3, x3b = _ffn_fwd(x2, x2b, W["ffn2_w_in"], w2o2, sp["ln3_g"], sp["ln3_b"], tm_ffn, "ffn2_fwd")
    loss_part, dx3, p_b, dpw_b, dgt_b, dg4, db4 = _ple_loss(
        x3, x3b, p, W["ple_w_in"], w_pg, sp["ln4_g"], sp["ln4_b"], target, tm_mix)

    G = {}
    G["ple_w_gate"] = _mm_tn(x3b, dgt_b, 512, 1024, 512, "dw_ple_gate").reshape(4, 256, D)
    G["ple_w_in"] = _mm_tn(p_b, dpw_b, 256, 256, 512, "dw_ple_in", shard_cols=256)
    dx2, dh2, a2_b, df2_b, dg3, db3 = _ffn_bwd(dx3, r3, sp["ln3_g"], h2, W["ffn2_w_in"], w2o2, tm_ffn, "ffn2_bwd")
    G["ffn2_w_in"] = _mm_tn(x2b, dh2, 512, FFH, 512, "dw_ffn2_in", shard_cols=FFH)
    G["ffn2_w_out"] = _mm_tn(a2_b, df2_b, FFH, 1024, 512, "dw_ffn2_out").reshape(4, FF // 4, D)
    (dres, dmix_b, dgl_b, ds_b, du_dir, gs_re, gs_im, dyc_b, dproj, dg2, db2, dd) = _mix_bwd_b(
        dx2, r2, sp["ln2_g"], w_mo, g_conv, g_ssm, y_conv, ga, gb, s, su, dvec, W["ssm_w_glu"], wc_re4, wc_im4,
        tm_mix)
    G["mix_w_out"] = _mm_tn(merged_b, dmix_b, 512, 1024, 512, "dw_mix_out").reshape(4, 256, D)
    G["ssm_w_glu"] = _mm_tn(sg_b, dgl_b, 512, 512, 512, "dw_glu", shard_cols=512)
    dsu_ssm, dwb_re, dwb_im, dwc_re, dwc_im, da_re, da_im = _s5_scan_bwd(
        gs_re, gs_im, st_re, st_im, su_b, ds_b, wb_re, wb_im, a_re, a_im)
    G["conv_w_out"] = _mm_tn(yin_b, dyc_b, 512, 256, 512, "dw_conv_out", shard_cols=256)
    dproj, dx1, dcw8, dcb = _mix_bwd_a(dyc_b, W["conv_w_out"], pc, z_b, sp["conv_w"], dsu_ssm, du_dir, dproj, dres,
                                       W["mix_w_in"], tm_mix)
    G["mix_w_in"] = _mm_tn(x1b, dproj, 512, 1024, 512, "dw_mix_in", shard_cols=1024)
    dx0, dh1, a1_b, df1_b, dg1, db1 = _ffn_bwd(dx1, r1, sp["ln1_g"], h1, W["ffn1_w_in"], w1o2, tm_ffn, "ffn1_bwd")
    G["ffn1_w_in"] = _mm_tn(xb, dh1, 512, FFH, 512, "dw_ffn1_in", shard_cols=FFH)
    G["ffn1_w_out"] = _mm_tn(a1_b, df1_b, FFH, 1024, 512, "dw_ffn1_out").reshape(4, FF // 4, D)

    raw = dict(ln1_g=dg1, ln1_b=db1, ln2_g=dg2, ln2_b=db2, ln3_g=dg3, ln3_b=db3, ln4_g=dg4, ln4_b=db4,
               conv_w=dcw8[0:3], conv_b=dcb,
               a_re=da_re.reshape(GROUPS, STATE), a_im=da_im.reshape(GROUPS, STATE),
               bb_re=_wb_diag(dwb_re), bb_im=_wb_diag(dwb_im),
               ssm_c_re=_wc_diag(dwc_re), ssm_c_im=-_wc_diag(dwc_im), ssm_d=dd.reshape(GROUPS, 16))
    return loss_part[0, 0], dx0, G, raw


RAW_ORDER = ["ln1_g", "ln1_b", "ln2_g", "ln2_b", "ln3_g", "ln3_b", "ln4_g", "ln4_b", "conv_w", "conv_b",
             "a_re", "a_im", "bb_re", "bb_im", "ssm_c_re", "ssm_c_im", "ssm_d"]


def _small_grads(raw_sum, sp):
    _, vjp = jax.vjp(_zoh, sp["ssm_lam_re"], sp["ssm_lam_im"], sp["ssm_log_step"], sp["ssm_b_re"], sp["ssm_b_im"])
    d_lre, d_lim, d_ls, d_bre, d_bim = vjp((raw_sum["a_re"], raw_sum["a_im"], raw_sum["bb_re"], raw_sum["bb_im"]))
    g = {k: raw_sum[k] for k in ("ln1_g", "ln1_b", "ln2_g", "ln2_b", "ln3_g", "ln3_b", "ln4_g", "ln4_b",
                                 "conv_w", "conv_b", "ssm_c_re", "ssm_c_im", "ssm_d")}
    g.update(ssm_lam_re=d_lre, ssm_lam_im=d_lim, ssm_log_step=d_ls, ssm_b_re=d_bre, ssm_b_im=d_bim)
    return g


def kernel(x, p, ffn1_w_in, ffn1_w_out, ln1_g, ln1_b, mix_w_in, conv_w, conv_b, conv_w_out, ssm_lam_re, ssm_lam_im, ssm_log_step, ssm_b_re, ssm_b_im, ssm_c_re, ssm_c_im, ssm_d, ssm_w_glu, mix_w_out, ln2_g, ln2_b, ffn2_w_in, ffn2_w_out, ln3_g, ln3_b, ple_w_in, ple_w_gate, ln4_g, ln4_b, loss_target, m_ffn1_w_in, m_ffn1_w_out, m_ln1_g, m_ln1_b, m_mix_w_in, m_conv_w, m_conv_b, m_conv_w_out, m_ssm_lam_re, m_ssm_lam_im, m_ssm_log_step, m_ssm_b_re, m_ssm_b_im, m_ssm_c_re, m_ssm_c_im, m_ssm_d, m_ssm_w_glu, m_mix_w_out, m_ln2_g, m_ln2_b, m_ffn2_w_in, m_ffn2_w_out, m_ln3_g, m_ln3_b, m_ple_w_in, m_ple_w_gate, m_ln4_g, m_ln4_b, v_ffn1_w_in, v_ffn1_w_out, v_ln1_g, v_ln1_b, v_mix_w_in, v_conv_w, v_conv_b, v_conv_w_out, v_ssm_lam_re, v_ssm_lam_im, v_ssm_log_step, v_ssm_b_re, v_ssm_b_im, v_ssm_c_re, v_ssm_c_im, v_ssm_d, v_ssm_w_glu, v_mix_w_out, v_ln2_g, v_ln2_b, v_ffn2_w_in, v_ffn2_w_out, v_ln3_g, v_ln3_b, v_ple_w_in, v_ple_w_gate, v_ln4_g, v_ln4_b):
    args = dict(locals())
    w = {n: args[n] for n in WEIGHTS}
    m = {n: args["m_" + n] for n in WEIGHTS}
    v = {n: args["v_" + n] for n in WEIGHTS}
    T = x.shape[1]
    _, _, c, me = _where()
    cidx = jnp.reshape(c, (1,)).astype(jnp.int32)
    meidx = jnp.reshape(me, (1,)).astype(jnp.int32)

    shards = [w[n][0].astype(BF16) for n in BIG]
    cw_slab = jnp.pad(conv_w[0], ((0, 13), (0, 0)))
    gathered = _gather_weights(shards + [cw_slab])
    W = dict(zip(BIG, gathered[:-1]))
    conv_w_full = gathered[-1][:, 0:3, :].transpose(1, 0, 2).reshape(3, CONV)

    sp = {n: (w[n] if w[n].ndim == 2 and n != "ssm_log_step" else w[n][0]) for n in SMALL}
    sp["conv_w"] = conv_w_full
    loss_part, dx0, G, raw = _local_step(x[0], p[0, 0], loss_target[0], W, sp, 256, 256)
    loss = lax.psum(loss_part, ("x", "y", "c"))

    got1 = _pair_swap_halves([G[n] for n in BIG])
    p32, pbf = {}, {}
    for n, g1 in zip(BIG, got1):
        p32[n], pbf[n] = _pair_sum(cidx, G[n], g1, "pair_sum_" + n)
    got2 = _chip_exchange([pbf[n] for n in BIG])
    halves = [_chip_sum(meidx, p32[n], g2, "chip_sum_" + n) for n, g2 in zip(BIG, got2)]
    grads = dict(zip(BIG, _pair_join_halves(halves)))

    raw_shapes = [raw[k].shape for k in RAW_ORDER]
    raw_sum = dict(zip(RAW_ORDER, _unpack(_allreduce_small(_pack([raw[k] for k in RAW_ORDER])), raw_shapes)))
    sg = _small_grads(raw_sum, sp)
    sg["conv_w"] = lax.dynamic_slice_in_dim(sg["conv_w"], me * 128, 128, axis=1)
    small_shapes = [w[n].shape for n in SMALL]
    gp = _pack([sg[n] for n in SMALL])
    d_s, m_s, v_s = _adamw(_pack([w[n] for n in SMALL]), gp, _pack([m[n] for n in SMALL]),
                           _pack([v[n] for n in SMALL]), "adamw_small")

    out_g, out_d, out_m, out_v = {}, {}, {}, {}
    for n, a, b_, c_, d_ in zip(SMALL, _unpack(gp, small_shapes), _unpack(d_s, small_shapes),
                                _unpack(m_s, small_shapes), _unpack(v_s, small_shapes)):
        out_g[n], out_d[n], out_m[n], out_v[n] = a, b_, c_, d_
    for n in BIG:
        g = grads[n]
        dl, mn, vn = _adamw(w[n][0], g, m[n][0], v[n][0], "adamw_" + n)
        out_g[n], out_d[n], out_m[n], out_v[n] = g[None], dl[None], mn[None], vn[None]

    return (loss, dx0[None], *[out_g[n] for n in WEIGHTS], *[out_d[n] for n in WEIGHTS],
            *[out_m[n] for n in WEIGHTS], *[out_v[n] for n in WEIGHTS])
```

```python
import functools
import math

import jax
import jax.numpy as jnp
import numpy as np
from jax import lax
from jax.experimental import pallas as pl
from jax.experimental.pallas import tpu as pltpu

F32, BF16 = jnp.float32, jnp.bfloat16
D = 1024
FF = 2816
FFH = FF // 2
CONV = 512
SSM = 512
GROUPS = 32
STATE = 64
LANES = GROUPS * STATE
SCAN_W = 256
SCAN_R = 256
ALPHA = 2.0 ** 0.25
LN_EPS = 1e-5
GELU_C = math.sqrt(2.0 / math.pi)
B1, B2, LR, EPS, WD, STEP = 0.9, 0.999, 0.001, 1e-8, 0.01, 10
MESH = pl.DeviceIdType.MESH
ANY = pl.BlockSpec(memory_space=pl.ANY)
VMEM_FULL = pl.BlockSpec(memory_space=pltpu.VMEM)


def _cp(vmem_mb=48, n_axes=1):
    return pltpu.CompilerParams(vmem_limit_bytes=vmem_mb << 20,
                                dimension_semantics=("arbitrary",) * n_axes)


def _nn(a, b):
    return jnp.dot(a, b, preferred_element_type=F32)


def _nt(a, b):
    return lax.dot_general(a, b, (((1,), (1,)), ((), ())), preferred_element_type=F32)


def _tn(a, b):
    return lax.dot_general(a, b, (((0,), (0,)), ((), ())), preferred_element_type=F32)


def _sig(v):
    return jax.nn.sigmoid(v)


def _ln_stats(r):
    mu = jnp.mean(r, axis=-1, keepdims=True)
    xc = r - mu
    var = jnp.mean(xc * xc, axis=-1, keepdims=True)
    rstd = lax.rsqrt(var + LN_EPS)
    return xc * rstd, rstd


def _ln_bwd(dy, r, g):
    xhat, rstd = _ln_stats(r)
    dyg = dy * g
    m1 = jnp.mean(dyg, axis=-1, keepdims=True)
    m2 = jnp.mean(dyg * xhat, axis=-1, keepdims=True)
    return rstd * (dyg - m1 - xhat * m2), xhat


def _rowsum(v):
    return jnp.sum(v, axis=0, keepdims=True)


class _Payload:
    def __init__(self, operands, outs, aliases, sems, start, finish):
        self.operands, self.outs, self.aliases, self.sems = list(operands), list(outs), dict(aliases), list(sems)
        self.start, self.finish = start, finish


def _split(flat, comm, attr):
    out, i = [], 0
    for p in comm:
        n = len(getattr(p, attr))
        out.append(list(flat[i:i + n]))
        i += n
    return out


def _run_comm(comm, which, cin, cout, csem):
    for p, a, b, s in zip(comm, _split(cin, comm, "operands"), _split(cout, comm, "outs"), _split(csem, comm, "sems")):
        getattr(p, which)(a, b, s)


def _pcall(body, *, name, grid, in_specs, out_specs, out_shape, operands, scratch=(), vmem_mb=48, aliases=None,
           comm=()):
    ni, no, ns = len(in_specs), len(out_specs), len(scratch)
    c_ops = [a for p in comm for a in p.operands]
    c_outs = [s for p in comm for s in p.outs]
    c_sems = [s for p in comm for s in p.sems]
    io = dict(aliases or {})
    off_i, off_o = ni, no
    for p in comm:
        for a, b in p.aliases.items():
            io[off_i + a] = off_o + b
        off_i += len(p.operands)
        off_o += len(p.outs)

    def wrapped(*refs):
        ins, cin = refs[:ni], refs[ni:ni + len(c_ops)]
        o0 = ni + len(c_ops)
        outs, cout = refs[o0:o0 + no], refs[o0 + no:o0 + no + len(c_outs)]
        s0 = o0 + no + len(c_outs)
        scr, csem = refs[s0:s0 + ns], refs[s0 + ns:]
        if comm:
            first = functools.reduce(jnp.logical_and, [pl.program_id(a) == 0 for a in range(len(grid))])
            pl.when(first)(lambda: _run_comm(comm, "start", cin, cout, csem))
        body(*ins, *outs, *scr)
        if comm:
            last = functools.reduce(jnp.logical_and, [pl.program_id(a) == grid[a] - 1 for a in range(len(grid))])
            pl.when(last)(lambda: _run_comm(comm, "finish", cin, cout, csem))

    res = pl.pallas_call(
        wrapped, name=name, grid=grid,
        in_specs=list(in_specs) + [ANY] * len(c_ops), out_specs=list(out_specs) + [ANY] * len(c_outs),
        out_shape=list(out_shape) + c_outs, scratch_shapes=list(scratch) + c_sems, input_output_aliases=io,
        compiler_params=pltpu.CompilerParams(vmem_limit_bytes=vmem_mb << 20,
                                             dimension_semantics=("arbitrary",) * len(grid),
                                             has_side_effects=bool(comm)),
    )(*operands, *c_ops)
    return list(res[:no]), _split(res[no:], comm, "outs")


def _comm_call(name, comm):
    c_ops = [a for p in comm for a in p.operands]
    c_outs = [s for p in comm for s in p.outs]
    c_sems = [s for p in comm for s in p.sems]
    io, off_i, off_o = {}, 0, 0
    for p in comm:
        for a, b in p.aliases.items():
            io[off_i + a] = off_o + b
        off_i += len(p.operands)
        off_o += len(p.outs)

    def body(*refs):
        cin, cout = refs[:len(c_ops)], refs[len(c_ops):len(c_ops) + len(c_outs)]
        csem = refs[len(c_ops) + len(c_outs):]
        _run_comm(comm, "start", cin, cout, csem)
        _run_comm(comm, "finish", cin, cout, csem)

    res = pl.pallas_call(
        body, name=name, in_specs=[ANY] * len(c_ops), out_specs=[ANY] * len(c_outs), out_shape=c_outs,
        scratch_shapes=c_sems, input_output_aliases=io,
        compiler_params=pltpu.CompilerParams(has_side_effects=True),
    )(*c_ops)
    return _split(res, comm, "outs")


def _ffn_fwd(x, xb, w_in4, w_out2, g, b, tm, name, comm=()):
    T = x.shape[0]

    def body(x_ref, xb_ref, wg_ref, wu_ref, wo_ref, g_ref, b_ref, h_ref, r_ref, xo_ref, xob_ref, acc):
        k = pl.program_id(1)

        @pl.when(k == 0)
        def _():
            acc[...] = jnp.zeros_like(acc)

        xv = xb_ref[...]
        gt = _nn(xv, wg_ref[...])
        up = _nn(xv, wu_ref[...])
        a = (gt * _sig(gt) * up).astype(BF16)
        for kk in range(2):
            @pl.when(k == kk)
            def _():
                h_ref[:, kk * FFH:(kk + 1) * FFH] = gt.astype(BF16)
                h_ref[:, (kk + 2) * FFH:(kk + 3) * FFH] = up.astype(BF16)
        acc[...] += _nn(a, wo_ref[...])

        @pl.when(k == 1)
        def _():
            r = ALPHA * x_ref[...] + 0.5 * acc[...]
            xhat, _ = _ln_stats(r)
            xo = xhat * g_ref[...] + b_ref[...]
            r_ref[...] = r
            xo_ref[...] = xo
            xob_ref[...] = xo.astype(BF16)

    tok = pl.BlockSpec((tm, D), lambda i, k: (i, 0))
    vec = pl.BlockSpec((1, D), lambda i, k: (0, 0))
    return _pcall(
        body, name=name, grid=(T // tm, 2),
        in_specs=[tok, tok,
                  pl.BlockSpec((None, D, FFH), lambda i, k: (k, 0, 0)),
                  pl.BlockSpec((None, D, FFH), lambda i, k: (k + 2, 0, 0)),
                  pl.BlockSpec((None, FFH, D), lambda i, k: (k, 0, 0)),
                  vec, vec],
        out_specs=[pl.BlockSpec((tm, 2 * FF), lambda i, k: (i, 0)), tok, tok, tok],
        out_shape=[jax.ShapeDtypeStruct((T, 2 * FF), BF16), jax.ShapeDtypeStruct((T, D), F32),
                   jax.ShapeDtypeStruct((T, D), F32), jax.ShapeDtypeStruct((T, D), BF16)],
        scratch=[pltpu.VMEM((tm, D), F32)], vmem_mb=56, comm=comm,
        operands=(x, xb, w_in4, w_in4, w_out2, g, b))


def _ffn_bwd(dy, r, g, h, w_in4, w_out2, tm, name, comm=()):
    T = dy.shape[0]

    def body(dy_ref, r_ref, g_ref, h_ref, wg_ref, wu_ref, wo_ref,
             dx_ref, dh_ref, a_ref, df_ref, dg_ref, db_ref, acc, dr_s, dfb_s):
        i, k = pl.program_id(0), pl.program_id(1)

        @pl.when(k == 0)
        def _():
            dyv = dy_ref[...]
            dr, xhat = _ln_bwd(dyv, r_ref[...], g_ref[...])
            pg, pb = _rowsum(dyv * xhat), _rowsum(dyv)

            @pl.when(i == 0)
            def _():
                dg_ref[...] = pg
                db_ref[...] = pb

            @pl.when(i > 0)
            def _():
                dg_ref[...] += pg
                db_ref[...] += pb

            dr_s[...] = dr
            dfb = (0.5 * dr).astype(BF16)
            dfb_s[...] = dfb
            df_ref[...] = dfb
            acc[...] = jnp.zeros_like(acc)

        da = _nt(dfb_s[...], wo_ref[...])
        for kk in range(2):
            @pl.when(k == kk)
            def _():
                gt = h_ref[:, kk * FFH:(kk + 1) * FFH].astype(F32)
                up = h_ref[:, (kk + 2) * FFH:(kk + 3) * FFH].astype(F32)
                sg = _sig(gt)
                silu = gt * sg
                dgate = (da * up * (sg * (1.0 + gt * (1.0 - sg)))).astype(BF16)
                dup = (da * silu).astype(BF16)
                a_ref[...] = (silu * up).astype(BF16)
                dh_ref[:, kk * FFH:(kk + 1) * FFH] = dgate
                dh_ref[:, (kk + 2) * FFH:(kk + 3) * FFH] = dup
                acc[...] += _nt(dgate, wg_ref[...]) + _nt(dup, wu_ref[...])

        @pl.when(k == 1)
        def _():
            dx_ref[...] = ALPHA * dr_s[...] + acc[...]

    tok = pl.BlockSpec((tm, D), lambda i, k: (i, 0))
    vec = pl.BlockSpec((1, D), lambda i, k: (0, 0))
    wide = pl.BlockSpec((tm, 2 * FF), lambda i, k: (i, 0))
    return _pcall(
        body, name=name, grid=(T // tm, 2),
        in_specs=[tok, tok, vec, wide,
                  pl.BlockSpec((None, D, FFH), lambda i, k: (k, 0, 0)),
                  pl.BlockSpec((None, D, FFH), lambda i, k: (k + 2, 0, 0)),
                  pl.BlockSpec((None, FFH, D), lambda i, k: (k, 0, 0))],
        out_specs=[tok, wide, pl.BlockSpec((tm, FFH), lambda i, k: (i, k)), tok, vec, vec],
        out_shape=[jax.ShapeDtypeStruct((T, D), F32), jax.ShapeDtypeStruct((T, 2 * FF), BF16),
                   jax.ShapeDtypeStruct((T, FF), BF16), jax.ShapeDtypeStruct((T, D), BF16),
                   jax.ShapeDtypeStruct((1, D), F32), jax.ShapeDtypeStruct((1, D), F32)],
        scratch=[pltpu.VMEM((tm, D), F32), pltpu.VMEM((tm, D), F32), pltpu.VMEM((tm, D), BF16)],
        vmem_mb=56, comm=comm, operands=(dy, r, g, h, w_in4, w_in4, w_out2))


def _mm_tn(a, b, tk, tn, tt, name, shard_cols=None, comm=()):
    T, K = a.shape
    N = b.shape[1]

    def body(a_ref, b_ref, o_ref):
        part = _tn(a_ref[...], b_ref[...])

        @pl.when(pl.program_id(2) == 0)
        def _():
            o_ref[...] = part

        @pl.when(pl.program_id(2) > 0)
        def _():
            o_ref[...] += part

    if shard_cols is None:
        out_shape = jax.ShapeDtypeStruct((K, N), F32)
        out_spec = pl.BlockSpec((tk, tn), lambda ki, nj, t: (ki, nj))
    else:
        per = shard_cols // tn
        out_shape = jax.ShapeDtypeStruct((N // shard_cols, K, shard_cols), F32)
        out_spec = pl.BlockSpec((None, tk, tn), lambda ki, nj, t: (nj // per, ki, nj % per))
    (out,), got = _pcall(
        body, name=name, grid=(K // tk, N // tn, T // tt),
        in_specs=[pl.BlockSpec((tt, tk), lambda ki, nj, t: (t, ki)),
                  pl.BlockSpec((tt, tn), lambda ki, nj, t: (t, nj))],
        out_specs=[out_spec], out_shape=[out_shape], comm=comm, operands=(a, b))
    return out, got


def _mix_fwd_a(xb, w_mix4, conv_w, conv_b, w_co4, tm):
    T = xb.shape[0]

    def body(xb_ref, w_ref, cw_ref, cb_ref, wco_ref,
             pc_ref, z_ref, yin_ref, su_ref, sub_ref, gc_ref, gs_ref, yc_ref, qbuf):
        @pl.when(pl.program_id(0) == 0)
        def _():
            qbuf[pl.ds(0, 8), :] = jnp.zeros((8, CONV), F32)

        xv = xb_ref[...]
        p0 = _nn(xv, w_ref[0])
        p1 = _nn(xv, w_ref[1])
        gc_ref[...] = _nn(xv, w_ref[2])
        gs_ref[...] = _nn(xv, w_ref[3])
        cbv, ccv = p0[:, :CONV], p0[:, CONV:]
        chv, suv = p1[:, :CONV], p1[:, CONV:]
        q = ccv * chv
        qbuf[pl.ds(8, tm), :] = q
        cw = cw_ref[...]
        z = (cw[2:3] * q + cw[1:2] * qbuf[pl.ds(7, tm), :] + cw[0:1] * qbuf[pl.ds(6, tm), :]
             + cb_ref[...])
        qbuf[pl.ds(0, 8), :] = q[tm - 8:tm]
        yin = (cbv * z).astype(BF16)
        pc_ref[:, 0:CONV] = cbv.astype(BF16)
        pc_ref[:, CONV:2 * CONV] = ccv.astype(BF16)
        pc_ref[:, 2 * CONV:3 * CONV] = chv.astype(BF16)
        z_ref[...] = z.astype(BF16)
        yin_ref[...] = yin
        su_ref[...] = suv
        sub_ref[...] = suv.astype(BF16)
        for k in range(4):
            yc_ref[:, 256 * k:256 * (k + 1)] = _nn(yin, wco_ref[k])

    def tok(n):
        return pl.BlockSpec((tm, n), lambda i: (i, 0))

    def full(shape):
        return pl.BlockSpec(shape, lambda i: (0,) * len(shape))

    return pl.pallas_call(
        body, name="mix_fwd_a", grid=(T // tm,),
        in_specs=[tok(D), full((4, D, D)), full((3, CONV)), full((1, CONV)), full((4, CONV, 256))],
        out_specs=[tok(3 * CONV), tok(CONV), tok(CONV), tok(SSM), tok(SSM), tok(D), tok(D), tok(D)],
        out_shape=[jax.ShapeDtypeStruct((T, 3 * CONV), BF16), jax.ShapeDtypeStruct((T, CONV), BF16),
                   jax.ShapeDtypeStruct((T, CONV), BF16), jax.ShapeDtypeStruct((T, SSM), F32),
                   jax.ShapeDtypeStruct((T, SSM), BF16), jax.ShapeDtypeStruct((T, D), F32),
                   jax.ShapeDtypeStruct((T, D), F32), jax.ShapeDtypeStruct((T, D), F32)],
        scratch_shapes=[pltpu.VMEM((tm + 8, CONV), F32)],
        compiler_params=_cp(56, 1),
    )(xb, w_mix4, conv_w, conv_b, w_co4)


def _scan_inplace(bre, bim, ar, ai, T, rev):
    R = SCAN_R
    if rev:
        ai = -ai
    d = 1
    while d < T:
        if d < 8:
            def step(i, _, d=d, ar=ar, ai=ai):
                c = i if rev else T // R - 1 - i
                t0 = pl.multiple_of(c * R, R)
                if rev:
                    wr = bre[pl.ds(t0 + 8, R + 8), :]
                    wi = bim[pl.ds(t0 + 8, R + 8), :]
                    shr = pltpu.roll(wr, R + 8 - d, 0)[0:R]
                    shi = pltpu.roll(wi, R + 8 - d, 0)[0:R]
                    cr, ci = wr[0:R], wi[0:R]
                else:
                    wr = bre[pl.ds(t0, R + 8), :]
                    wi = bim[pl.ds(t0, R + 8), :]
                    shr = pltpu.roll(wr, d, 0)[8:8 + R]
                    shi = pltpu.roll(wi, d, 0)[8:8 + R]
                    cr, ci = wr[8:8 + R], wi[8:8 + R]
                bre[pl.ds(t0 + 8, R), :] = cr + ar * shr - ai * shi
                bim[pl.ds(t0 + 8, R), :] = ci + ar * shi + ai * shr
                return 0

            lax.fori_loop(0, T // R, step, 0)
        else:
            def upd(lo, n, d=d, ar=ar, ai=ai):
                src = lo + d if rev else lo - d
                if not isinstance(lo, int):
                    lo, src = pl.multiple_of(lo + 8, 8), pl.multiple_of(src + 8, 8)
                else:
                    lo, src = lo + 8, src + 8
                cr = bre[pl.ds(lo, n), :]
                ci = bim[pl.ds(lo, n), :]
                shr = bre[pl.ds(src, n), :]
                shi = bim[pl.ds(src, n), :]
                bre[pl.ds(lo, n), :] = cr + ar * shr - ai * shi
                bim[pl.ds(lo, n), :] = ci + ar * shi + ai * shr

            nfull = (T - d) // R if d >= R else T // R - 1

            def step(i, _, upd=upd, d=d):
                if rev:
                    t0 = i * R
                else:
                    t0 = T - (i + 1) * R
                upd(t0, R)
                return 0

            if nfull > 0:
                lax.fori_loop(0, nfull, step, 0)
            if d < R:
                if rev:
                    upd(T - R, R - d)
                else:
                    upd(d, R - d)
        ar, ai = ar * ar - ai * ai, 2.0 * ar * ai
        d *= 2


def _scan_specs(T):
    W = SCAN_W
    lane = pl.BlockSpec((T, W), lambda j: (0, j))
    col = pl.BlockSpec((T, 128), lambda j: (0, j // 2))
    wb = pl.BlockSpec((None, 128, W), lambda j: (j, 0, 0))
    wc = pl.BlockSpec((None, W, 128), lambda j: (j, 0, 0))
    vec = pl.BlockSpec((1, W), lambda j: (0, j))
    return lane, col, wb, wc, vec


def _s5_scan_fwd(su_b, wb_re, wb_im, a_re, a_im, comm=()):
    T = su_b.shape[0]
    W = SCAN_W

    def body(su_ref, wbr_ref, wbi_ref, ar_ref, ai_ref, sr_ref, si_ref, bre, bim):
        zero = jnp.zeros((8, W), F32)
        for buf in (bre, bim):
            buf[pl.ds(0, 8), :] = zero
            buf[pl.ds(T + 8, 8), :] = zero
        su = su_ref[...]
        bre[pl.ds(8, T), :] = _nn(su, wbr_ref[...])
        bim[pl.ds(8, T), :] = _nn(su, wbi_ref[...])
        _scan_inplace(bre, bim, ar_ref[...], ai_ref[...], T, rev=False)
        sr_ref[...] = bre[pl.ds(8, T), :]
        si_ref[...] = bim[pl.ds(8, T), :]

    lane, col, wb, wc, vec = _scan_specs(T)
    return _pcall(
        body, name="s5_scan_fwd", grid=(LANES // W,),
        in_specs=[col, wb, wb, vec, vec],
        out_specs=[lane, lane],
        out_shape=[jax.ShapeDtypeStruct((T, LANES), F32)] * 2,
        scratch=[pltpu.VMEM((T + 16, W), F32)] * 2, comm=comm,
        operands=(su_b, wb_re, wb_im, a_re, a_im))


def _gelu(s):
    th = jnp.tanh(GELU_C * (s + 0.044715 * s * s * s))
    return 0.5 * s * (1.0 + th), th


def _mix_fwd_b(st_re, st_im, wc_re4, wc_im4, su, dvec, w_glu4, g_conv, g_ssm, y_conv, w_mo, x1, g, b, tm, comm=()):
    T = su.shape[0]

    def body(sr_ref, si_ref, wcr_ref, wci_ref, su_ref, d_ref, wg_ref, gc_ref, gs_ref, yc_ref, wmo_ref,
             x_ref, g_ref, b_ref, s_ref, sgb_ref, ga_ref, gb_ref, mb_ref, r_ref, xo_ref, xob_ref):
        srb = sr_ref[...].astype(BF16)
        sib = si_ref[...].astype(BF16)
        ys = [_nn(srb[:, 512 * J:512 * (J + 1)], wcr_ref[J]) + _nn(sib[:, 512 * J:512 * (J + 1)], wci_ref[J])
              for J in range(4)]
        s = jnp.concatenate(ys, axis=1) + d_ref[...] * su_ref[...]
        sg, _ = _gelu(s)
        sgb = sg.astype(BF16)
        ga = jnp.concatenate([_nn(sgb, wg_ref[0]), _nn(sgb, wg_ref[1])], axis=1)
        gb = jnp.concatenate([_nn(sgb, wg_ref[2]), _nn(sgb, wg_ref[3])], axis=1)
        merged = _sig(gc_ref[...]) * yc_ref[...] + _sig(gs_ref[...]) * (ga * _sig(gb))
        mb = merged.astype(BF16)
        r = ALPHA * x_ref[...] + _nn(mb, wmo_ref[...])
        xhat, _ = _ln_stats(r)
        xo = xhat * g_ref[...] + b_ref[...]
        s_ref[...] = s
        sgb_ref[...] = sgb
        ga_ref[...] = ga
        gb_ref[...] = gb
        mb_ref[...] = mb
        r_ref[...] = r
        xo_ref[...] = xo
        xob_ref[...] = xo.astype(BF16)

    def tok(n):
        return pl.BlockSpec((tm, n), lambda i: (i, 0))

    def full(shape):
        return pl.BlockSpec(shape, lambda i: (0,) * len(shape))

    return _pcall(
        body, name="mix_fwd_b", grid=(T // tm,),
        in_specs=[tok(LANES), tok(LANES), full((4, 512, 128)), full((4, 512, 128)), tok(SSM), full((1, SSM)),
                  full((4, SSM, 512)), tok(D), tok(D), tok(D), full((D, D)), tok(D), full((1, D)), full((1, D))],
        out_specs=[tok(SSM), tok(SSM), tok(D), tok(D), tok(D), tok(D), tok(D), tok(D)],
        out_shape=[jax.ShapeDtypeStruct((T, SSM), F32), jax.ShapeDtypeStruct((T, SSM), BF16),
                   jax.ShapeDtypeStruct((T, D), F32), jax.ShapeDtypeStruct((T, D), F32),
                   jax.ShapeDtypeStruct((T, D), BF16), jax.ShapeDtypeStruct((T, D), F32),
                   jax.ShapeDtypeStruct((T, D), F32), jax.ShapeDtypeStruct((T, D), BF16)],
        vmem_mb=56, comm=comm,
        operands=(st_re, st_im, wc_re4, wc_im4, su, dvec, w_glu4, g_conv, g_ssm, y_conv, w_mo, x1, g, b))


def _ple_loss(x3, x3b, p, w_pi4, w_pg, g, b, target, tm):
    T = x3.shape[0]
    PD = p.shape[1]

    def body(x_ref, xb_ref, p_ref, wpi_ref, wpg_ref, g_ref, b_ref, t_ref,
             loss_ref, dx_ref, pb_ref, dpw_ref, dgt_ref, dg_ref, db_ref):
        i = pl.program_id(0)
        pb = p_ref[...].astype(BF16)
        pw = jnp.concatenate([_nn(pb, wpi_ref[k]) for k in range(4)], axis=1)
        gt = _nn(xb_ref[...], wpg_ref[...])
        sg = _sig(gt)
        r = ALPHA * x_ref[...] + pw * sg
        gv = g_ref[...]
        xhat, rstd = _ln_stats(r)
        err = xhat * gv + b_ref[...] - t_ref[...]
        lpart = jnp.zeros((1, 128), F32) + 0.5 * jnp.sum(jnp.mean(err * err, axis=-1, keepdims=True))
        dy = err * (1.0 / D)
        dyg = dy * gv
        m1 = jnp.mean(dyg, axis=-1, keepdims=True)
        m2 = jnp.mean(dyg * xhat, axis=-1, keepdims=True)
        dr = rstd * (dyg - m1 - xhat * m2)
        pg, pbias = _rowsum(dy * xhat), _rowsum(dy)

        @pl.when(i == 0)
        def _():
            loss_ref[...] = lpart
            dg_ref[...] = pg
            db_ref[...] = pbias

        @pl.when(i > 0)
        def _():
            loss_ref[...] += lpart
            dg_ref[...] += pg
            db_ref[...] += pbias

        dgt = (dr * pw * sg * (1.0 - sg)).astype(BF16)
        pb_ref[...] = pb
        dpw_ref[...] = (dr * sg).astype(BF16)
        dgt_ref[...] = dgt
        dx_ref[...] = ALPHA * dr + _nt(dgt, wpg_ref[...])

    def tok(n):
        return pl.BlockSpec((tm, n), lambda i: (i, 0))

    def full(shape):
        return pl.BlockSpec(shape, lambda i: (0,) * len(shape))

    return pl.pallas_call(
        body, name="ple_loss", grid=(T // tm,),
        in_specs=[tok(D), tok(D), tok(PD), full((4, PD, 256)), full((D, D)), full((1, D)), full((1, D)), tok(D)],
        out_specs=[full((1, 128)), tok(D), tok(PD), tok(D), tok(D), full((1, D)), full((1, D))],
        out_shape=[jax.ShapeDtypeStruct((1, 128), F32), jax.ShapeDtypeStruct((T, D), F32),
                   jax.ShapeDtypeStruct((T, PD), BF16), jax.ShapeDtypeStruct((T, D), BF16),
                   jax.ShapeDtypeStruct((T, D), BF16), jax.ShapeDtypeStruct((1, D), F32),
                   jax.ShapeDtypeStruct((1, D), F32)],
        compiler_params=_cp(48, 1),
    )(x3, x3b, p, w_pi4, w_pg, g, b, target)


def _mix_bwd_b(dy, r2, g, w_mo, g_conv, g_ssm, y_conv, ga, gb, s, su, dvec, w_glu4, wc_re4, wc_im4, tm, comm=()):
    T = dy.shape[0]

    def body(dy_ref, r_ref, g_ref, wmo_ref, gc_ref, gs_ref, yc_ref, ga_ref, gb_ref, s_ref, su_ref, d_ref,
             wg_ref, wcr_ref, wci_ref,
             dres_ref, dmix_ref, dgl_ref, dsb_ref, dud_ref, gsr_ref, gsi_ref, dyc_ref, dp_ref,
             dg_ref, db_ref, dd_ref):
        i = pl.program_id(0)
        dyv = dy_ref[...]
        dr, xhat = _ln_bwd(dyv, r_ref[...], g_ref[...])
        dmix = dr.astype(BF16)
        dmerged = _nt(dmix, wmo_ref[...])
        sc, ss, sgb = _sig(gc_ref[...]), _sig(gs_ref[...]), _sig(gb_ref[...])
        gav = ga_ref[...]
        yssm = gav * sgb
        dgc = dmerged * yc_ref[...] * sc * (1.0 - sc)
        dgss = dmerged * yssm * ss * (1.0 - ss)
        dyssm = dmerged * ss
        dgl = jnp.concatenate([dyssm * sgb, dyssm * gav * sgb * (1.0 - sgb)], axis=1).astype(BF16)
        dsg = (_nt(dgl[:, 0:512], wg_ref[0]) + _nt(dgl[:, 512:1024], wg_ref[1])
               + _nt(dgl[:, 1024:1536], wg_ref[2]) + _nt(dgl[:, 1536:2048], wg_ref[3]))
        sv = s_ref[...]
        _, th = _gelu(sv)
        dgelu = 0.5 * (1.0 + th) + 0.5 * sv * (1.0 - th * th) * GELU_C * (1.0 + 3.0 * 0.044715 * sv * sv)
        ds = dsg * dgelu
        dsb = ds.astype(BF16)
        pg, pb, pd = _rowsum(dyv * xhat), _rowsum(dyv), _rowsum(ds * su_ref[...])

        @pl.when(i == 0)
        def _():
            dg_ref[...] = pg
            db_ref[...] = pb
            dd_ref[...] = pd

        @pl.when(i > 0)
        def _():
            dg_ref[...] += pg
            db_ref[...] += pb
            dd_ref[...] += pd

        dres_ref[...] = ALPHA * dr
        dmix_ref[...] = dmix
        dgl_ref[...] = dgl
        dsb_ref[...] = dsb
        dud_ref[...] = ds * d_ref[...]
        for J in range(4):
            gsr_ref[:, 512 * J:512 * (J + 1)] = _nt(dsb[:, 128 * J:128 * (J + 1)], wcr_ref[J])
            gsi_ref[:, 512 * J:512 * (J + 1)] = _nt(dsb[:, 128 * J:128 * (J + 1)], wci_ref[J])
        dyc_ref[...] = (dmerged * sc).astype(BF16)
        dp_ref[:, 0:D] = dgc.astype(BF16)
        dp_ref[:, D:2 * D] = dgss.astype(BF16)

    def tok(n):
        return pl.BlockSpec((tm, n), lambda i: (i, 0))

    def full(shape):
        return pl.BlockSpec(shape, lambda i: (0,) * len(shape))

    return _pcall(
        body, name="mix_bwd_b", grid=(T // tm,),
        in_specs=[tok(D), tok(D), full((1, D)), full((D, D)), tok(D), tok(D), tok(D), tok(D), tok(D),
                  tok(SSM), tok(SSM), full((1, SSM)), full((4, SSM, 512)), full((4, 512, 128)), full((4, 512, 128))],
        out_specs=[tok(D), tok(D), tok(2 * D), tok(SSM), tok(SSM), tok(LANES), tok(LANES), tok(D),
                   pl.BlockSpec((tm, 2 * D), lambda i: (i, 1)), full((1, D)), full((1, D)), full((1, SSM))],
        out_shape=[jax.ShapeDtypeStruct((T, D), F32), jax.ShapeDtypeStruct((T, D), BF16),
                   jax.ShapeDtypeStruct((T, 2 * D), BF16), jax.ShapeDtypeStruct((T, SSM), BF16),
                   jax.ShapeDtypeStruct((T, SSM), F32), jax.ShapeDtypeStruct((T, LANES), F32),
                   jax.ShapeDtypeStruct((T, LANES), F32), jax.ShapeDtypeStruct((T, D), BF16),
                   jax.ShapeDtypeStruct((T, 4 * D), BF16), jax.ShapeDtypeStruct((1, D), F32),
                   jax.ShapeDtypeStruct((1, D), F32), jax.ShapeDtypeStruct((1, SSM), F32)],
        vmem_mb=56, comm=comm,
        operands=(dy, r2, g, w_mo, g_conv, g_ssm, y_conv, ga, gb, s, su, dvec, w_glu4, wc_re4, wc_im4))


def _s5_scan_bwd(gs_re, gs_im, st_re, st_im, su_b, ds_b, wb_re, wb_im, a_re, a_im, comm=()):
    T = su_b.shape[0]
    W = SCAN_W
    R = SCAN_R

    def body(gr_ref, gi_ref, sr_ref, si_ref, su_ref, ds_ref, wbr_ref, wbi_ref, ar_ref, ai_ref,
             dsu_ref, dwbr_ref, dwbi_ref, dwcr_ref, dwci_ref, dar_ref, dai_ref, gre, gim):
        j = pl.program_id(0)
        zero = jnp.zeros((8, W), F32)
        for buf in (gre, gim):
            buf[pl.ds(0, 8), :] = zero
            buf[pl.ds(T + 8, 8), :] = zero
        gre[pl.ds(8, T), :] = gr_ref[...]
        gim[pl.ds(8, T), :] = gi_ref[...]
        _scan_inplace(gre, gim, ar_ref[...], ai_ref[...], T, rev=True)
        grb = gre[pl.ds(8, T), :].astype(BF16)
        gib = gim[pl.ds(8, T), :].astype(BF16)
        part = _nt(grb, wbr_ref[...]) + _nt(gib, wbi_ref[...])

        @pl.when(j % 2 == 0)
        def _():
            dsu_ref[...] = part

        @pl.when(j % 2 == 1)
        def _():
            dsu_ref[...] += part

        su = su_ref[...]
        dwbr_ref[...] = _tn(su, grb)
        dwbi_ref[...] = _tn(su, gib)
        dsv = ds_ref[...]
        dwcr_ref[...] = _tn(sr_ref[...].astype(BF16), dsv)
        dwci_ref[...] = _tn(si_ref[...].astype(BF16), dsv)
        dar = jnp.zeros((1, W), F32)
        dai = jnp.zeros((1, W), F32)
        for c in range(T // R):
            xr = sr_ref[pl.ds(c * R, R), :]
            xi = si_ref[pl.ds(c * R, R), :]
            g1r = gre[pl.ds(c * R + 9, R), :]
            g1i = gim[pl.ds(c * R + 9, R), :]
            dar = dar + _rowsum(g1r * xr + g1i * xi)
            dai = dai + _rowsum(g1i * xr - g1r * xi)
        dar_ref[...] = dar
        dai_ref[...] = dai

    lane, col, wb, wc, vec = _scan_specs(T)
    return _pcall(
        body, name="s5_scan_bwd", grid=(LANES // W,),
        in_specs=[lane, lane, lane, lane, col, col, wb, wb, vec, vec],
        out_specs=[col, wb, wb, wc, wc, vec, vec],
        out_shape=[jax.ShapeDtypeStruct((T, SSM), F32),
                   jax.ShapeDtypeStruct((LANES // W, 128, W), F32), jax.ShapeDtypeStruct((LANES // W, 128, W), F32),
                   jax.ShapeDtypeStruct((LANES // W, W, 128), F32), jax.ShapeDtypeStruct((LANES // W, W, 128), F32),
                   jax.ShapeDtypeStruct((1, LANES), F32), jax.ShapeDtypeStruct((1, LANES), F32)],
        scratch=[pltpu.VMEM((T + 16, W), F32)] * 2, vmem_mb=56, comm=comm,
        operands=(gs_re, gs_im, st_re, st_im, su_b, ds_b, wb_re, wb_im, a_re, a_im))


def _mix_bwd_a(dyc_b, w_co4, pc, z_b, conv_w, dsu_ssm, du_dir, dproj, dres, w_mix4, tm, comm=()):
    T = dres.shape[0]
    nt = T // tm

    def body(dyc_ref, wco_ref, pc_ref, halo_ref, z_ref, cw_ref, dsu_ref, dud_ref, dpin_ref, dres_ref, w_ref,
             dp_ref, dx_ref, dcw_ref, dcb_ref, dzbuf, qbuf):
        i = pl.program_id(0)
        ii = nt - 1 - i

        @pl.when(i == 0)
        def _():
            dzbuf[pl.ds(tm, 8), :] = jnp.zeros((8, CONV), F32)

        dyc = dyc_ref[...]
        dyin = (_nt(dyc[:, 0:256], wco_ref[0]) + _nt(dyc[:, 256:512], wco_ref[1])
                + _nt(dyc[:, 512:768], wco_ref[2]) + _nt(dyc[:, 768:1024], wco_ref[3]))
        cbv = pc_ref[:, 0:CONV].astype(F32)
        ccv = pc_ref[:, CONV:2 * CONV].astype(F32)
        chv = pc_ref[:, 2 * CONV:3 * CONV].astype(F32)
        dcbv = dyin * z_ref[...].astype(F32)
        dz = dyin * cbv
        dzbuf[pl.ds(0, tm), :] = dz
        cw = cw_ref[...]
        dq = cw[2:3] * dz + cw[1:2] * dzbuf[pl.ds(1, tm), :] + cw[0:1] * dzbuf[pl.ds(2, tm), :]
        dzbuf[pl.ds(tm, 8), :] = dz[0:8]
        q = ccv * chv
        hq = halo_ref[:, CONV:2 * CONV].astype(F32) * halo_ref[:, 2 * CONV:3 * CONV].astype(F32)
        qbuf[pl.ds(0, 8), :] = jnp.where(ii > 0, hq, jnp.zeros_like(hq))
        qbuf[pl.ds(8, tm), :] = q
        pw = jnp.concatenate([_rowsum(dz * qbuf[pl.ds(6, tm), :]), _rowsum(dz * qbuf[pl.ds(7, tm), :]),
                              _rowsum(dz * q), jnp.zeros((5, CONV), F32)], axis=0)
        pbias = _rowsum(dz)

        @pl.when(i == 0)
        def _():
            dcw_ref[...] = pw
            dcb_ref[...] = pbias

        @pl.when(i > 0)
        def _():
            dcw_ref[...] += pw
            dcb_ref[...] += pbias

        dp0 = jnp.concatenate([dcbv, dq * chv], axis=1).astype(BF16)
        dp1 = jnp.concatenate([dq * ccv, dsu_ref[...] + dud_ref[...]], axis=1).astype(BF16)
        dp_ref[:, 0:D] = dp0
        dp_ref[:, D:2 * D] = dp1
        dx_ref[...] = (dres_ref[...] + _nt(dp0, w_ref[0]) + _nt(dp1, w_ref[1])
                       + _nt(dpin_ref[:, 0:D], w_ref[2]) + _nt(dpin_ref[:, D:2 * D], w_ref[3]))

    def tok(n):
        return pl.BlockSpec((tm, n), lambda i: (nt - 1 - i, 0))

    def full(shape):
        return pl.BlockSpec(shape, lambda i: (0,) * len(shape))

    halo = pl.BlockSpec((8, 3 * CONV), lambda i: (jnp.maximum((nt - 1 - i) * (tm // 8) - 1, 0), 0))
    return _pcall(
        body, name="mix_bwd_a", grid=(nt,),
        in_specs=[tok(D), full((4, CONV, 256)), tok(3 * CONV), halo, tok(CONV), full((3, CONV)),
                  tok(SSM), tok(SSM), pl.BlockSpec((tm, 2 * D), lambda i: (nt - 1 - i, 1)), tok(D),
                  full((4, D, D))],
        out_specs=[pl.BlockSpec((tm, 2 * D), lambda i: (nt - 1 - i, 0)), tok(D), full((8, CONV)), full((1, CONV))],
        out_shape=[jax.ShapeDtypeStruct((T, 4 * D), BF16), jax.ShapeDtypeStruct((T, D), F32),
                   jax.ShapeDtypeStruct((8, CONV), F32), jax.ShapeDtypeStruct((1, CONV), F32)],
        scratch=[pltpu.VMEM((tm + 8, CONV), F32), pltpu.VMEM((tm + 8, CONV), F32)],
        aliases={8: 0}, vmem_mb=56, comm=comm,
        operands=(dyc_b, w_co4, pc, pc, z_b, conv_w, dsu_ssm, du_dir, dproj, dres, w_mix4))


def _zoh(lam_re, lam_im, log_step, b_re, b_im):
    dt = jnp.exp(log_step)[:, None]
    mag = jnp.exp(lam_re * dt)
    abr, abi = mag * jnp.cos(lam_im * dt), mag * jnp.sin(lam_im * dt)
    nr, ni = abr - 1.0, abi
    den = lam_re * lam_re + lam_im * lam_im
    cr = (nr * lam_re + ni * lam_im) / den
    ci = (ni * lam_re - nr * lam_im) / den
    bbr = cr[..., None] * b_re - ci[..., None] * b_im
    bbi = cr[..., None] * b_im + ci[..., None] * b_re
    return abr, abi, bbr, bbi


def _wb_blocks(bb):
    eye = jnp.eye(GROUPS, dtype=F32)
    full = jnp.einsum("gni,gh->gihn", bb, eye).reshape(4, 128, 8, SCAN_W)
    return jnp.stack([full[j // 2, :, j, :] for j in range(8)]).astype(BF16)


def _wc_blocks(cc):
    eye = jnp.eye(GROUPS, dtype=F32)
    full = jnp.einsum("gin,gh->gnhi", cc, eye).reshape(4, 512, 4, 128)
    return jnp.stack([full[J, :, J, :] for J in range(4)]).astype(BF16)


_G = np.arange(GROUPS)


def _wb_diag(dwb8):
    d5 = dwb8.reshape(8, 8, 16, 4, 64)
    return d5[_G // 4, _G % 8, :, _G % 4, :].transpose(0, 2, 1)


def _wc_diag(dwc8):
    d5 = dwc8.reshape(8, 4, 64, 8, 16)
    return d5[_G // 4, _G % 4, :, _G % 8, :].transpose(0, 2, 1)


def _where():
    x, y, c = lax.axis_index("x"), lax.axis_index("y"), lax.axis_index("c")
    return x, y, c, 2 * x + y


def _chip_dev(k, c):
    return (k // 2, k % 2, c)


def _slot_cast(meidx, w, dtype, name):
    R, C = w.shape
    tr = _row_tile(R)

    def body(m_ref, w_ref, o_ref):
        o_ref[...] = w_ref[...].astype(dtype)

    gs = pltpu.PrefetchScalarGridSpec(
        num_scalar_prefetch=1, grid=(R // tr,),
        in_specs=[pl.BlockSpec((tr, C), lambda i, m: (i, 0))],
        out_specs=pl.BlockSpec((None, tr, C), lambda i, m: (m[0], i, 0)))
    return pl.pallas_call(
        body, name=name, grid_spec=gs, out_shape=jax.ShapeDtypeStruct((4, R, C), dtype),
        compiler_params=_cp(32, 1),
    )(meidx, w)


def _gather_payload(bufs):
    n = len(bufs)

    def half(ref, w, k, cc):
        h = bufs[w].shape[1] // 2
        return ref.at[k, pl.ds(cc * h, h)]

    def ici(ins, outs, sems, w, s):
        x, y, c, me = _where()
        k = (me + 1 + s) % 4
        return pltpu.make_async_remote_copy(
            src_ref=half(ins[w], w, me, c), dst_ref=half(outs[w], w, me, c), send_sem=sems[0].at[3 * w + s],
            recv_sem=sems[1].at[3 * w + s], device_id=_chip_dev(k, c), device_id_type=MESH)

    def landed(outs, sems, w, s):
        x, y, c, me = _where()
        j = (me + 3 - s) % 4
        return pltpu.make_async_remote_copy(
            src_ref=half(outs[w], w, j, c), dst_ref=half(outs[w], w, j, c), send_sem=sems[0].at[3 * w + s],
            recv_sem=sems[1].at[3 * w + s], device_id=(x, y, 1 - c), device_id_type=MESH)

    def passed(outs, sems, w, s, cc):
        x, y, c, me = _where()
        j = (me + 3 - s) % 4
        return pltpu.make_async_remote_copy(
            src_ref=half(outs[w], w, j, cc), dst_ref=half(outs[w], w, j, cc), send_sem=sems[2].at[3 * w + s],
            recv_sem=sems[3].at[3 * w + s], device_id=(x, y, 1 - c), device_id_type=MESH)

    pairs = [(w, s) for w in range(n) for s in range(3)]

    def start(ins, outs, sems):
        for w, s in pairs:
            ici(ins, outs, sems, w, s).start()

    def finish(ins, outs, sems):
        _, _, c, _ = _where()
        for w, s in pairs:
            landed(outs, sems, w, s).wait_recv()
            passed(outs, sems, w, s, c).start()
        for w, s in pairs:
            passed(outs, sems, w, s, 1 - c).wait_recv()
        for w, s in pairs:
            ici(ins, outs, sems, w, s).wait_send()
            passed(outs, sems, w, s, c).wait_send()

    return _Payload(bufs, [jax.ShapeDtypeStruct(b.shape, b.dtype) for b in bufs], {w: w for w in range(n)},
                    [pltpu.SemaphoreType.DMA((3 * n,))] * 4, start, finish)


def _sym_payload(operands, outs, copies, n_copies):
    def start(ins, outs_, sems):
        for cp in copies(ins, outs_, sems[0], sems[1]):
            cp.start()

    def finish(ins, outs_, sems):
        for cp in copies(ins, outs_, sems[0], sems[1]):
            cp.wait()

    return _Payload(operands, outs, {}, [pltpu.SemaphoreType.DMA((n_copies,))] * 2, start, finish)


def _swap_payload(g4s):
    def copies(ins, outs, ss, rs):
        x, y, c, me = _where()
        cps = []
        for w, g in enumerate(g4s):
            h = g.shape[1] // 2
            cps.append(pltpu.make_async_remote_copy(
                src_ref=ins[w].at[:, pl.ds((1 - c) * h, h)], dst_ref=outs[w], send_sem=ss.at[w],
                recv_sem=rs.at[w], device_id=(x, y, 1 - c), device_id_type=MESH))
        return cps

    outs = [jax.ShapeDtypeStruct((4, g.shape[1] // 2, g.shape[2]), g.dtype) for g in g4s]
    return _sym_payload(g4s, outs, copies, len(g4s))


def _exchange_payload(pbs):
    def copies(ins, outs, ss, rs):
        x, y, c, me = _where()
        cps = []
        for w in range(len(pbs)):
            for s in range(3):
                k = (me + 1 + s) % 4
                cps.append(pltpu.make_async_remote_copy(
                    src_ref=ins[w].at[k], dst_ref=outs[w].at[2 - s], send_sem=ss.at[3 * w + s],
                    recv_sem=rs.at[3 * w + s], device_id=_chip_dev(k, c), device_id_type=MESH))
        return cps

    outs = [jax.ShapeDtypeStruct((3,) + p.shape[1:], p.dtype) for p in pbs]
    return _sym_payload(pbs, outs, copies, 3 * len(pbs))


def _join_payload(halves):
    def copies(ins, outs, ss, rs):
        x, y, c, me = _where()
        return [pltpu.make_async_remote_copy(
            src_ref=ins[w], dst_ref=outs[w], send_sem=ss.at[w], recv_sem=rs.at[w],
            device_id=(x, y, 1 - c), device_id_type=MESH) for w in range(len(halves))]

    outs = [jax.ShapeDtypeStruct(a.shape, a.dtype) for a in halves]
    return _sym_payload(halves, outs, copies, len(halves))


def _allgather_payload(v):
    def copies(ins, outs, ss, rs):
        x, y, c, me = _where()
        lin = 4 * x + 2 * y + c
        cps = []
        cps = [pltpu.make_async_copy(ins[0], outs[0].at[lin], ss.at[0])]
        for o in range(1, 8):
            t = (lin + o) % 8
            cps.append(pltpu.make_async_remote_copy(
                src_ref=ins[0], dst_ref=outs[0].at[lin], send_sem=ss.at[o], recv_sem=rs.at[o],
                device_id=(t // 4, (t // 2) % 2, t % 2), device_id_type=MESH))
        return cps

    return _sym_payload([v], [jax.ShapeDtypeStruct((8,) + v.shape, v.dtype)], copies, 8)


def _sum8(buf):
    _, P, C = buf.shape

    def body(b_ref, o_ref):
        acc = b_ref[0]
        for d in range(1, 8):
            acc = acc + b_ref[d]
        o_ref[...] = acc

    return pl.pallas_call(
        body, name="sum8", in_specs=[VMEM_FULL], out_specs=VMEM_FULL,
        out_shape=jax.ShapeDtypeStruct((P, C), F32),
        compiler_params=pltpu.CompilerParams(vmem_limit_bytes=32 << 20),
    )(buf)


def _row_tile(h):
    for t in (256, 176, 128, 64, 32, 16, 8):
        if h % t == 0:
            return t
    raise ValueError(h)


def _pair_sum(cidx, g4, got, name):
    _, R, C = g4.shape
    h = R // 2
    th = _row_tile(h)

    def body(c_ref, a_ref, b_ref, o_ref, ob_ref):
        sm = a_ref[...] + b_ref[...]
        o_ref[...] = sm
        ob_ref[...] = sm.astype(BF16)

    blk = pl.BlockSpec((None, th, C), lambda k, i, c: (k, i, 0))
    gs = pltpu.PrefetchScalarGridSpec(
        num_scalar_prefetch=1, grid=(4, h // th),
        in_specs=[pl.BlockSpec((None, None, th, C), lambda k, i, c: (k, c[0], i, 0)), blk],
        out_specs=[blk, blk])
    return pl.pallas_call(
        body, name=name, grid_spec=gs,
        out_shape=[jax.ShapeDtypeStruct((4, h, C), F32), jax.ShapeDtypeStruct((4, h, C), BF16)],
        compiler_params=_cp(32, 2),
    )(cidx, g4.reshape(4, 2, h, C), got)


def _chip_sum(meidx, p32, got, name):
    _, h, C = p32.shape
    th = _row_tile(h)

    def body(m_ref, a_ref, b_ref, o_ref):
        o_ref[...] = ((a_ref[...] + b_ref[0].astype(F32)) + b_ref[1].astype(F32)) + b_ref[2].astype(F32)

    gs = pltpu.PrefetchScalarGridSpec(
        num_scalar_prefetch=1, grid=(h // th,),
        in_specs=[pl.BlockSpec((None, th, C), lambda i, m: (m[0], i, 0)),
                  pl.BlockSpec((3, th, C), lambda i, m: (0, i, 0))],
        out_specs=pl.BlockSpec((th, C), lambda i, m: (i, 0)))
    return pl.pallas_call(
        body, name=name, grid_spec=gs, out_shape=jax.ShapeDtypeStruct((h, C), F32),
        compiler_params=_cp(32, 1),
    )(meidx, p32, got)


def _adamw_math(w, g, m, v):
    m2 = B1 * m + (1.0 - B1) * g
    v2 = B2 * v + (1.0 - B2) * (g * g)
    m_hat = m2 / (1.0 - B1 ** STEP)
    v_hat = v2 / (1.0 - B2 ** STEP)
    delta = -LR * (m_hat / (jnp.sqrt(v_hat) + EPS) + WD * w)
    return delta, m2, v2


def _adamw_pair(cidx, w, mine, theirs, m, v, name):
    R, C = w.shape
    h = R // 2
    tr = _row_tile(h)
    nh = h // tr

    def body(c_ref, w_ref, a_ref, b_ref, m_ref, v_ref, g_ref, d_ref, mo_ref, vo_ref):
        own = (pl.program_id(0) // nh) == c_ref[0]
        g = jnp.where(own, a_ref[...], b_ref[...])
        d, m2, v2 = _adamw_math(w_ref[...], g, m_ref[...], v_ref[...])
        g_ref[...] = g
        d_ref[...] = d
        mo_ref[...] = m2
        vo_ref[...] = v2

    blk = pl.BlockSpec((tr, C), lambda i, c: (i, 0))
    hblk = pl.BlockSpec((tr, C), lambda i, c: (i % nh, 0))
    gs = pltpu.PrefetchScalarGridSpec(
        num_scalar_prefetch=1, grid=(R // tr,),
        in_specs=[blk, hblk, hblk, blk, blk], out_specs=[blk] * 4)
    return pl.pallas_call(
        body, name=name, grid_spec=gs, out_shape=[jax.ShapeDtypeStruct((R, C), F32)] * 4,
        compiler_params=_cp(32, 1),
    )(cidx, w, mine, theirs, m, v)


def _adamw(w, g, m, v, name):
    R, C = w.shape
    tr = _row_tile(R)

    def body(w_ref, g_ref, m_ref, v_ref, d_ref, mo_ref, vo_ref):
        d, m2, v2 = _adamw_math(w_ref[...], g_ref[...], m_ref[...], v_ref[...])
        d_ref[...] = d
        mo_ref[...] = m2
        vo_ref[...] = v2

    blk = pl.BlockSpec((tr, C), lambda i: (i, 0))
    return pl.pallas_call(
        body, name=name, grid=(R // tr,), in_specs=[blk] * 4, out_specs=[blk] * 3,
        out_shape=[jax.ShapeDtypeStruct((R, C), F32)] * 3,
        compiler_params=_cp(32, 1),
    )(w, g, m, v)


def _pack(arrs):
    flat = jnp.concatenate([a.reshape(-1).astype(F32) for a in arrs])
    rows = -(-flat.shape[0] // 1024)
    rows = -(-rows // 8) * 8
    return jnp.pad(flat, (0, rows * 1024 - flat.shape[0])).reshape(rows, 1024)


def _unpack(packed, shapes):
    flat = packed.reshape(-1)
    out, off = [], 0
    for s in shapes:
        n = math.prod(s)
        out.append(flat[off:off + n].reshape(s))
        off += n
    return out


BIG = ["ffn1_w_in", "ffn1_w_out", "mix_w_in", "conv_w_out", "ssm_w_glu", "mix_w_out",
       "ffn2_w_in", "ffn2_w_out", "ple_w_in", "ple_w_gate"]
SMALL = ["ln1_g", "ln1_b", "conv_w", "conv_b", "ssm_lam_re", "ssm_lam_im", "ssm_log_step", "ssm_b_re", "ssm_b_im",
         "ssm_c_re", "ssm_c_im", "ssm_d", "ln2_g", "ln2_b", "ln3_g", "ln3_b", "ln4_g", "ln4_b"]
WEIGHTS = ["ffn1_w_in", "ffn1_w_out", "ln1_g", "ln1_b", "mix_w_in", "conv_w", "conv_b", "conv_w_out",
           "ssm_lam_re", "ssm_lam_im", "ssm_log_step", "ssm_b_re", "ssm_b_im", "ssm_c_re", "ssm_c_im", "ssm_d",
           "ssm_w_glu", "mix_w_out", "ln2_g", "ln2_b", "ffn2_w_in", "ffn2_w_out", "ln3_g", "ln3_b",
           "ple_w_in", "ple_w_gate", "ln4_g", "ln4_b"]


class _NoComm:
    def __init__(self, W):
        self.W, self.G, self.raw = dict(W), {}, None

    def carry(self, name):
        return ()

    def landed(self, name, got):
        pass

    def grad(self, name, g4):
        self.G[name] = g4

    def small(self, raw):
        self.raw = raw


def _local_step(x, p, target, sp, sched, tm_ffn, tm_mix):
    W = sched.W
    abr, abi, bbr, bbi = _zoh(sp["ssm_lam_re"], sp["ssm_lam_im"], sp["ssm_log_step"], sp["ssm_b_re"], sp["ssm_b_im"])
    wb_re, wb_im = _wb_blocks(bbr), _wb_blocks(bbi)
    wc_re4, wc_im4 = _wc_blocks(sp["ssm_c_re"]), _wc_blocks(-sp["ssm_c_im"])
    a_re, a_im = abr.reshape(1, LANES), abi.reshape(1, LANES)
    dvec = sp["ssm_d"].reshape(1, SSM)

    def run(fn, name, *args, **kw):
        outs, got = fn(*args, comm=sched.carry(name), **kw)
        sched.landed(name, got)
        return outs

    def dw(name, wname, a, b, tk, tn, shape4, shard_cols=None):
        out, got = _mm_tn(a, b, tk, tn, 512, name, shard_cols=shard_cols, comm=sched.carry(name))
        sched.landed(name, got)
        sched.grad(wname, out.reshape(shape4))

    xb = x.astype(BF16)
    h1, r1, x1, x1b = run(_ffn_fwd, "ffn1_fwd", x, xb, W["ffn1_w_in"], W["ffn1_w_out"].reshape(2, FFH, D),
                          sp["ln1_g"], sp["ln1_b"], tm_ffn, "ffn1_fwd")
    conv_w = W["conv_w"][:, 0:3, :].transpose(1, 0, 2).reshape(3, CONV)
    pc, z_b, yin_b, su, su_b, g_conv, g_ssm, y_conv = _mix_fwd_a(
        x1b, W["mix_w_in"], conv_w, sp["conv_b"], W["conv_w_out"], tm_mix)
    st_re, st_im = run(_s5_scan_fwd, "s5_scan_fwd", su_b, wb_re, wb_im, a_re, a_im)
    w_mo = W["mix_w_out"].reshape(D, D)
    s, sg_b, ga, gb, merged_b, r2, x2, x2b = run(
        _mix_fwd_b, "mix_fwd_b", st_re, st_im, wc_re4, wc_im4, su, dvec, W["ssm_w_glu"], g_conv, g_ssm, y_conv,
        w_mo, x1, sp["ln2_g"], sp["ln2_b"], tm_mix)
    w2o2 = W["ffn2_w_out"].reshape(2, FFH, D)
    h2, r3, x3, x3b = run(_ffn_fwd, "ffn2_fwd", x2, x2b, W["ffn2_w_in"], w2o2, sp["ln3_g"], sp["ln3_b"], tm_ffn,
                          "ffn2_fwd")
    loss_part, dx3, p_b, dpw_b, dgt_b, dg4, db4 = _ple_loss(
        x3, x3b, p, W["ple_w_in"], W["ple_w_gate"].reshape(D, D), sp["ln4_g"], sp["ln4_b"], target, tm_mix)

    dw("dw_ple_gate", "ple_w_gate", x3b, dgt_b, 512, 1024, (4, 256, D))
    dw("dw_ple_in", "ple_w_in", p_b, dpw_b, 256, 256, (4, 256, 256), shard_cols=256)
    dx2, dh2, a2_b, df2_b, dg3, db3 = run(_ffn_bwd, "ffn2_bwd", dx3, r3, sp["ln3_g"], h2, W["ffn2_w_in"], w2o2,
                                          tm_ffn, "ffn2_bwd")
    dw("dw_ffn2_in", "ffn2_w_in", x2b, dh2, 512, FFH, (4, D, FFH), shard_cols=FFH)
    dw("dw_ffn2_out", "ffn2_w_out", a2_b, df2_b, FFH, 1024, (4, FF // 4, D))
    (dres, dmix_b, dgl_b, ds_b, du_dir, gs_re, gs_im, dyc_b, dproj, dg2, db2, dd) = run(
        _mix_bwd_b, "mix_bwd_b", dx2, r2, sp["ln2_g"], w_mo, g_conv, g_ssm, y_conv, ga, gb, s, su, dvec,
        W["ssm_w_glu"], wc_re4, wc_im4, tm_mix)
    dw("dw_mix_out", "mix_w_out", merged_b, dmix_b, 512, 1024, (4, 256, D))
    dw("dw_glu", "ssm_w_glu", sg_b, dgl_b, 512, 512, (4, SSM, 512), shard_cols=512)
    dsu_ssm, dwb_re, dwb_im, dwc_re, dwc_im, da_re, da_im = run(
        _s5_scan_bwd, "s5_scan_bwd", gs_re, gs_im, st_re, st_im, su_b, ds_b, wb_re, wb_im, a_re, a_im)
    dw("dw_conv_out", "conv_w_out", yin_b, dyc_b, 512, 256, (4, CONV, 256), shard_cols=256)
    dproj, dx1, dcw8, dcb = run(_mix_bwd_a, "mix_bwd_a", dyc_b, W["conv_w_out"], pc, z_b, conv_w, dsu_ssm,
                                du_dir, dproj, dres, W["mix_w_in"], tm_mix)
    dw("dw_mix_in", "mix_w_in", x1b, dproj, 512, 1024, (4, D, D), shard_cols=1024)
    dx0, dh1, a1_b, df1_b, dg1, db1 = run(_ffn_bwd, "ffn1_bwd", dx1, r1, sp["ln1_g"], h1, W["ffn1_w_in"],
                                          W["ffn1_w_out"].reshape(2, FFH, D), tm_ffn, "ffn1_bwd")
    sched.small(dict(
        ln1_g=dg1, ln1_b=db1, ln2_g=dg2, ln2_b=db2, ln3_g=dg3, ln3_b=db3, ln4_g=dg4, ln4_b=db4,
        conv_w=dcw8[0:3], conv_b=dcb,
        a_re=da_re.reshape(GROUPS, STATE), a_im=da_im.reshape(GROUPS, STATE),
        bb_re=_wb_diag(dwb_re), bb_im=_wb_diag(dwb_im),
        ssm_c_re=_wc_diag(dwc_re), ssm_c_im=-_wc_diag(dwc_im), ssm_d=dd.reshape(GROUPS, 16)))
    dw("dw_ffn1_in", "ffn1_w_in", xb, dh1, 512, FFH, (4, D, FFH), shard_cols=FFH)
    dw("dw_ffn1_out", "ffn1_w_out", a1_b, df1_b, FFH, 1024, (4, FF // 4, D))
    return loss_part[0, 0], dx0


RAW_ORDER = ["ln1_g", "ln1_b", "ln2_g", "ln2_b", "ln3_g", "ln3_b", "ln4_g", "ln4_b", "conv_w", "conv_b",
             "a_re", "a_im", "bb_re", "bb_im", "ssm_c_re", "ssm_c_im", "ssm_d"]

GATHER_FIRST = ["ffn1_w_in", "ffn1_w_out"]
GATHER_AT = {"ffn1_fwd": ["mix_w_in", "conv_w_out", "conv_w", "ssm_w_glu", "mix_w_out"],
             "s5_scan_fwd": ["ffn2_w_in"], "mix_fwd_b": ["ffn2_w_out"], "ffn2_fwd": ["ple_w_in", "ple_w_gate"]}
REDUCE_GROUP = {"ple": ["ple_w_gate", "ple_w_in"], "ffn2": ["ffn2_w_in", "ffn2_w_out"],
                "mix": ["mix_w_out", "ssm_w_glu", "conv_w_out", "mix_w_in"], "ffn1": ["ffn1_w_in", "ffn1_w_out"]}
REDUCE_AT = {"ffn2_bwd": [("swap", "ple")], "dw_ffn2_in": [("exchange", "ple")],
             "mix_bwd_b": [("swap", "ffn2"), ("join", "ple")], "s5_scan_bwd": [("exchange", "ffn2")],
             "mix_bwd_a": [("join", "ffn2")], "ffn1_bwd": [("swap", "mix")],
             "dw_ffn1_in": [("exchange", "mix"), ("small", None)]}
REDUCE_TAIL = [[("swap", "ffn1"), ("join", "mix")], [("exchange", "ffn1")], [("join", "ffn1")]]


class _Sched:
    def __init__(self, bufs, cidx, meidx):
        self.bufs, self.cidx, self.meidx = bufs, cidx, meidx
        self.W, self.G, self.raw, self.small_buf = {}, {}, None, None
        self.got1, self.p32, self.pbf, self.got2, self.half, self.theirs = {}, {}, {}, {}, {}, {}
        self._open = []
        self._standalone("gather_ffn1", [("gather", GATHER_FIRST)])

    def _payload(self, stage, key):
        if stage == "gather":
            return _gather_payload([self.bufs[n] for n in key])
        if stage == "small":
            return _allgather_payload(_pack([self.raw[k] for k in RAW_ORDER]))
        names = REDUCE_GROUP[key]
        if stage == "swap":
            return _swap_payload([self.G[n] for n in names])
        if stage == "exchange":
            for n in names:
                self.p32[n], self.pbf[n] = _pair_sum(self.cidx, self.G[n], self.got1[n], "pair_sum_" + n)
            return _exchange_payload([self.pbf[n] for n in names])
        for n in names:
            self.half[n] = _chip_sum(self.meidx, self.p32[n], self.got2[n], "chip_sum_" + n)
        return _join_payload([self.half[n] for n in names])

    def _store(self, stages, got):
        for (stage, key), outs in zip(stages, got):
            if stage == "gather":
                self.W.update(zip(key, outs))
            elif stage == "small":
                self.small_buf = outs[0]
            else:
                {"swap": self.got1, "exchange": self.got2, "join": self.theirs}[stage].update(
                    zip(REDUCE_GROUP[key], outs))

    def _standalone(self, name, stages):
        self._store(stages, _comm_call(name, [self._payload(s, k) for s, k in stages]))

    def carry(self, name):
        self._open = [("gather", GATHER_AT[name])] if name in GATHER_AT else []
        self._open += REDUCE_AT.get(name, [])
        return tuple(self._payload(s, k) for s, k in self._open)

    def landed(self, name, got):
        self._store(self._open, got)

    def grad(self, name, g4):
        self.G[name] = g4

    def small(self, raw):
        self.raw = raw

    def tail(self):
        for i, stages in enumerate(REDUCE_TAIL):
            self._standalone("reduce_tail_%d" % i, stages)


def _small_grads(raw_sum, sp):
    _, vjp = jax.vjp(_zoh, sp["ssm_lam_re"], sp["ssm_lam_im"], sp["ssm_log_step"], sp["ssm_b_re"], sp["ssm_b_im"])
    d_lre, d_lim, d_ls, d_bre, d_bim = vjp((raw_sum["a_re"], raw_sum["a_im"], raw_sum["bb_re"], raw_sum["bb_im"]))
    g = {k: raw_sum[k] for k in ("ln1_g", "ln1_b", "ln2_g", "ln2_b", "ln3_g", "ln3_b", "ln4_g", "ln4_b",
                                 "conv_w", "conv_b", "ssm_c_re", "ssm_c_im", "ssm_d")}
    g.update(ssm_lam_re=d_lre, ssm_lam_im=d_lim, ssm_log_step=d_ls, ssm_b_re=d_bre, ssm_b_im=d_bim)
    return g


def kernel(x, p, ffn1_w_in, ffn1_w_out, ln1_g, ln1_b, mix_w_in, conv_w, conv_b, conv_w_out, ssm_lam_re, ssm_lam_im, ssm_log_step, ssm_b_re, ssm_b_im, ssm_c_re, ssm_c_im, ssm_d, ssm_w_glu, mix_w_out, ln2_g, ln2_b, ffn2_w_in, ffn2_w_out, ln3_g, ln3_b, ple_w_in, ple_w_gate, ln4_g, ln4_b, loss_target, m_ffn1_w_in, m_ffn1_w_out, m_ln1_g, m_ln1_b, m_mix_w_in, m_conv_w, m_conv_b, m_conv_w_out, m_ssm_lam_re, m_ssm_lam_im, m_ssm_log_step, m_ssm_b_re, m_ssm_b_im, m_ssm_c_re, m_ssm_c_im, m_ssm_d, m_ssm_w_glu, m_mix_w_out, m_ln2_g, m_ln2_b, m_ffn2_w_in, m_ffn2_w_out, m_ln3_g, m_ln3_b, m_ple_w_in, m_ple_w_gate, m_ln4_g, m_ln4_b, v_ffn1_w_in, v_ffn1_w_out, v_ln1_g, v_ln1_b, v_mix_w_in, v_conv_w, v_conv_b, v_conv_w_out, v_ssm_lam_re, v_ssm_lam_im, v_ssm_log_step, v_ssm_b_re, v_ssm_b_im, v_ssm_c_re, v_ssm_c_im, v_ssm_d, v_ssm_w_glu, v_mix_w_out, v_ln2_g, v_ln2_b, v_ffn2_w_in, v_ffn2_w_out, v_ln3_g, v_ln3_b, v_ple_w_in, v_ple_w_gate, v_ln4_g, v_ln4_b):
    args = dict(locals())
    w = {n: args[n] for n in WEIGHTS}
    m = {n: args["m_" + n] for n in WEIGHTS}
    v = {n: args["v_" + n] for n in WEIGHTS}
    _, _, c, me = _where()
    cidx = jnp.reshape(c, (1,)).astype(jnp.int32)
    meidx = jnp.reshape(me, (1,)).astype(jnp.int32)

    bufs = {n: _slot_cast(meidx, w[n][0], BF16, "cast_" + n) for n in BIG}
    bufs["conv_w"] = _slot_cast(meidx, jnp.pad(conv_w[0], ((0, 13), (0, 0))), F32, "cast_conv_w")
    sched = _Sched(bufs, cidx, meidx)

    sp = {n: (w[n] if w[n].ndim == 2 and n != "ssm_log_step" else w[n][0]) for n in SMALL if n != "conv_w"}
    loss_part, dx0 = _local_step(x[0], p[0, 0], loss_target[0], sp, sched, 256, 256)
    loss = lax.psum(loss_part, ("x", "y", "c"))
    sched.tail()

    raw_shapes = [sched.raw[k].shape for k in RAW_ORDER]
    raw_sum = dict(zip(RAW_ORDER, _unpack(_sum8(sched.small_buf), raw_shapes)))
    sg = _small_grads(raw_sum, sp)
    sg["conv_w"] = lax.dynamic_slice_in_dim(sg["conv_w"], me * 128, 128, axis=1)
    small_shapes = [w[n].shape for n in SMALL]
    gp = _pack([sg[n] for n in SMALL])
    d_s, m_s, v_s = _adamw(_pack([w[n] for n in SMALL]), gp, _pack([m[n] for n in SMALL]),
                           _pack([v[n] for n in SMALL]), "adamw_small")

    out_g, out_d, out_m, out_v = {}, {}, {}, {}
    for n, a, b_, c_, d_ in zip(SMALL, _unpack(gp, small_shapes), _unpack(d_s, small_shapes),
                                _unpack(m_s, small_shapes), _unpack(v_s, small_shapes)):
        out_g[n], out_d[n], out_m[n], out_v[n] = a, b_, c_, d_
    for n in BIG:
        g, dl, mn, vn = _adamw_pair(cidx, w[n][0], sched.half[n], sched.theirs[n], m[n][0], v[n][0], "adamw_" + n)
        out_g[n], out_d[n], out_m[n], out_v[n] = g[None], dl[None], mn[None], vn[None]

    return (loss, dx0[None], *[out_g[n] for n in WEIGHTS], *[out_d[n] for n in WEIGHTS],
            *[out_m[n] for n in WEIGHTS], *[out_v[n] for n in WEIGHTS])
```

```python
import functools
import math

import jax
import jax.numpy as jnp
import numpy as np
from jax import lax
from jax.experimental import pallas as pl
from jax.experimental.pallas import tpu as pltpu

F32, BF16 = jnp.float32, jnp.bfloat16
D = 1024
FF = 2816
FFH = FF // 2
CONV = 512
SSM = 512
GROUPS = 32
STATE = 64
LANES = GROUPS * STATE
SCAN_W = 256
SCAN_R = 256
ALPHA = 2.0 ** 0.25
LN_EPS = 1e-5
GELU_C = math.sqrt(2.0 / math.pi)
B1, B2, LR, EPS, WD, STEP = 0.9, 0.999, 0.001, 1e-8, 0.01, 10
MESH = pl.DeviceIdType.MESH
ANY = pl.BlockSpec(memory_space=pl.ANY)
VMEM_FULL = pl.BlockSpec(memory_space=pltpu.VMEM)


def _cp(vmem_mb=48, n_axes=1):
    return pltpu.CompilerParams(vmem_limit_bytes=vmem_mb << 20,
                                dimension_semantics=("arbitrary",) * n_axes)


def _hbm(*arrs):
    return [pltpu.with_memory_space_constraint(a, pltpu.HBM) for a in arrs]


def _hbm_out(shapes):
    if isinstance(shapes, (list, tuple)):
        return [pltpu.HBM(s.shape, s.dtype) for s in shapes]
    return pltpu.HBM(shapes.shape, shapes.dtype)


def _nn(a, b):
    return jnp.dot(a, b, preferred_element_type=F32)


def _nt(a, b):
    return lax.dot_general(a, b, (((1,), (1,)), ((), ())), preferred_element_type=F32)


def _tn(a, b):
    return lax.dot_general(a, b, (((0,), (0,)), ((), ())), preferred_element_type=F32)


def _sig(v):
    return jax.nn.sigmoid(v)


def _ln_stats(r):
    mu = jnp.mean(r, axis=-1, keepdims=True)
    xc = r - mu
    var = jnp.mean(xc * xc, axis=-1, keepdims=True)
    rstd = lax.rsqrt(var + LN_EPS)
    return xc * rstd, rstd


def _ln_bwd(dy, r, g):
    xhat, rstd = _ln_stats(r)
    dyg = dy * g
    m1 = jnp.mean(dyg, axis=-1, keepdims=True)
    m2 = jnp.mean(dyg * xhat, axis=-1, keepdims=True)
    return rstd * (dyg - m1 - xhat * m2), xhat


def _rowsum(v):
    return jnp.sum(v, axis=0, keepdims=True)


class _Payload:
    def __init__(self, operands, outs, aliases, sems, start, finish):
        self.operands, self.outs, self.aliases, self.sems = list(operands), list(outs), dict(aliases), list(sems)
        self.start, self.finish = start, finish


def _split(flat, comm, attr):
    out, i = [], 0
    for p in comm:
        n = len(getattr(p, attr))
        out.append(list(flat[i:i + n]))
        i += n
    return out


def _run_comm(comm, which, cin, cout, csem):
    for p, a, b, s in zip(comm, _split(cin, comm, "operands"), _split(cout, comm, "outs"), _split(csem, comm, "sems")):
        getattr(p, which)(a, b, s)


def _pcall(body, *, name, grid, in_specs, out_specs, out_shape, operands, scratch=(), vmem_mb=48, aliases=None,
           comm=()):
    ni, no, ns = len(in_specs), len(out_specs), len(scratch)
    c_ops = [a for p in comm for a in p.operands]
    c_outs = [s for p in comm for s in p.outs]
    c_sems = [s for p in comm for s in p.sems]
    io = dict(aliases or {})
    off_i, off_o = ni, no
    for p in comm:
        for a, b in p.aliases.items():
            io[off_i + a] = off_o + b
        off_i += len(p.operands)
        off_o += len(p.outs)

    def wrapped(*refs):
        ins, cin = refs[:ni], refs[ni:ni + len(c_ops)]
        o0 = ni + len(c_ops)
        outs, cout = refs[o0:o0 + no], refs[o0 + no:o0 + no + len(c_outs)]
        s0 = o0 + no + len(c_outs)
        scr, csem = refs[s0:s0 + ns], refs[s0 + ns:]
        if comm:
            first = functools.reduce(jnp.logical_and, [pl.program_id(a) == 0 for a in range(len(grid))])
            pl.when(first)(lambda: _run_comm(comm, "start", cin, cout, csem))
        body(*ins, *outs, *scr)
        if comm:
            last = functools.reduce(jnp.logical_and, [pl.program_id(a) == grid[a] - 1 for a in range(len(grid))])
            pl.when(last)(lambda: _run_comm(comm, "finish", cin, cout, csem))

    res = pl.pallas_call(
        wrapped, name=name, grid=grid,
        in_specs=list(in_specs) + [ANY] * len(c_ops), out_specs=list(out_specs) + [ANY] * len(c_outs),
        out_shape=_hbm_out(list(out_shape) + c_outs), scratch_shapes=list(scratch) + c_sems,
        input_output_aliases=io,
        compiler_params=pltpu.CompilerParams(vmem_limit_bytes=vmem_mb << 20,
                                             dimension_semantics=("arbitrary",) * len(grid),
                                             has_side_effects=bool(comm)),
    )(*_hbm(*operands, *c_ops))
    return list(res[:no]), _split(res[no:], comm, "outs")


def _comm_call(name, comm):
    c_ops = [a for p in comm for a in p.operands]
    c_outs = [s for p in comm for s in p.outs]
    c_sems = [s for p in comm for s in p.sems]
    io, off_i, off_o = {}, 0, 0
    for p in comm:
        for a, b in p.aliases.items():
            io[off_i + a] = off_o + b
        off_i += len(p.operands)
        off_o += len(p.outs)

    def body(*refs):
        cin, cout = refs[:len(c_ops)], refs[len(c_ops):len(c_ops) + len(c_outs)]
        csem = refs[len(c_ops) + len(c_outs):]
        _run_comm(comm, "start", cin, cout, csem)
        _run_comm(comm, "finish", cin, cout, csem)

    res = pl.pallas_call(
        body, name=name, in_specs=[ANY] * len(c_ops), out_specs=[ANY] * len(c_outs), out_shape=_hbm_out(c_outs),
        scratch_shapes=c_sems, input_output_aliases=io,
        compiler_params=pltpu.CompilerParams(has_side_effects=True),
    )(*_hbm(*c_ops))
    return _split(res, comm, "outs")


def _ffn_fwd(x, xb, w_in4, w_out2, g, b, tm, name, comm=()):
    T = x.shape[0]

    def body(x_ref, xb_ref, wg_ref, wu_ref, wo_ref, g_ref, b_ref, h_ref, r_ref, xo_ref, xob_ref, acc):
        k = pl.program_id(1)
        xv = xb_ref[...]
        gt = _nn(xv, wg_ref[...])
        up = _nn(xv, wu_ref[...])
        a = (gt * _sig(gt) * up).astype(BF16)
        h_ref[:, 0:FFH] = gt.astype(BF16)
        h_ref[:, FFH:2 * FFH] = up.astype(BF16)
        acc[...] = jnp.where(k == 0, 0.0, acc[...]) + _nn(a, wo_ref[...])

        @pl.when(k == 1)
        def _():
            r = ALPHA * x_ref[...] + 0.5 * acc[...]
            xhat, _ = _ln_stats(r)
            xo = xhat * g_ref[...] + b_ref[...]
            r_ref[...] = r
            xo_ref[...] = xo
            xob_ref[...] = xo.astype(BF16)

    tok = pl.BlockSpec((tm, D), lambda i, k: (i, 0))
    vec = pl.BlockSpec((1, D), lambda i, k: (0, 0))
    return _pcall(
        body, name=name, grid=(T // tm, 2),
        in_specs=[tok, tok,
                  pl.BlockSpec((None, D, FFH), lambda i, k: (k, 0, 0)),
                  pl.BlockSpec((None, D, FFH), lambda i, k: (k + 2, 0, 0)),
                  pl.BlockSpec((None, FFH, D), lambda i, k: (k, 0, 0)),
                  vec, vec],
        out_specs=[pl.BlockSpec((tm, FF), lambda i, k: (i, k)), tok, tok, tok],
        out_shape=[jax.ShapeDtypeStruct((T, 2 * FF), BF16), jax.ShapeDtypeStruct((T, D), F32),
                   jax.ShapeDtypeStruct((T, D), F32), jax.ShapeDtypeStruct((T, D), BF16)],
        scratch=[pltpu.VMEM((tm, D), F32)], vmem_mb=56, comm=comm,
        operands=(x, xb, w_in4, w_in4, w_out2, g, b))


def _ffn_bwd(dy, r, g, h, w_in4, w_out2, tm, name, comm=()):
    T = dy.shape[0]

    def body(dy_ref, r_ref, g_ref, h_ref, wg_ref, wu_ref, wo_ref,
             dx_ref, dh_ref, a_ref, df_ref, dg_ref, db_ref, acc, dr_s, dfb_s):
        i, k = pl.program_id(0), pl.program_id(1)

        @pl.when(k == 0)
        def _():
            dyv = dy_ref[...]
            dr, xhat = _ln_bwd(dyv, r_ref[...], g_ref[...])
            pg, pb = _rowsum(dyv * xhat), _rowsum(dyv)

            @pl.when(i == 0)
            def _():
                dg_ref[...] = pg
                db_ref[...] = pb

            @pl.when(i > 0)
            def _():
                dg_ref[...] += pg
                db_ref[...] += pb

            dr_s[...] = dr
            dfb = (0.5 * dr).astype(BF16)
            dfb_s[...] = dfb
            df_ref[...] = dfb

        da = _nt(dfb_s[...], wo_ref[...])
        gt = h_ref[:, 0:FFH].astype(F32)
        up = h_ref[:, FFH:2 * FFH].astype(F32)
        sg = _sig(gt)
        silu = gt * sg
        dgate = (da * up * (sg * (1.0 + gt * (1.0 - sg)))).astype(BF16)
        dup = (da * silu).astype(BF16)
        a_ref[...] = (silu * up).astype(BF16)
        dh_ref[:, 0:FFH] = dgate
        dh_ref[:, FFH:2 * FFH] = dup
        acc[...] = jnp.where(k == 0, 0.0, acc[...]) + _nt(dgate, wg_ref[...]) + _nt(dup, wu_ref[...])

        @pl.when(k == 1)
        def _():
            dx_ref[...] = ALPHA * dr_s[...] + acc[...]

    tok = pl.BlockSpec((tm, D), lambda i, k: (i, 0))
    vec = pl.BlockSpec((1, D), lambda i, k: (0, 0))
    wide = pl.BlockSpec((tm, FF), lambda i, k: (i, k))
    return _pcall(
        body, name=name, grid=(T // tm, 2),
        in_specs=[tok, tok, vec, wide,
                  pl.BlockSpec((None, D, FFH), lambda i, k: (k, 0, 0)),
                  pl.BlockSpec((None, D, FFH), lambda i, k: (k + 2, 0, 0)),
                  pl.BlockSpec((None, FFH, D), lambda i, k: (k, 0, 0))],
        out_specs=[tok, wide, pl.BlockSpec((tm, FFH), lambda i, k: (i, k)), tok, vec, vec],
        out_shape=[jax.ShapeDtypeStruct((T, D), F32), jax.ShapeDtypeStruct((T, 2 * FF), BF16),
                   jax.ShapeDtypeStruct((T, FF), BF16), jax.ShapeDtypeStruct((T, D), BF16),
                   jax.ShapeDtypeStruct((1, D), F32), jax.ShapeDtypeStruct((1, D), F32)],
        scratch=[pltpu.VMEM((tm, D), F32), pltpu.VMEM((tm, D), F32), pltpu.VMEM((tm, D), BF16)],
        vmem_mb=56, comm=comm, operands=(dy, r, g, h, w_in4, w_in4, w_out2))


def _mm_tn(a, b, tk, tn, name, shard_cols=None, interleaved=False, comm=()):
    T, K = a.shape
    N = b.shape[1]

    def body(a_ref, b_ref, o_ref):
        o_ref[...] = _tn(a_ref[...], b_ref[...])

    if shard_cols is None:
        out_shape = jax.ShapeDtypeStruct((K, N), F32)
        out_spec = pl.BlockSpec((tk, tn), lambda ki, nj: (ki, nj))
    else:
        per = shard_cols // tn

        def shard(nj):
            blk = nj // per
            return (blk % 2) * 2 + blk // 2 if interleaved else blk

        out_shape = jax.ShapeDtypeStruct((N // shard_cols, K, shard_cols), F32)
        out_spec = pl.BlockSpec((None, tk, tn), lambda ki, nj: (shard(nj), ki, nj % per))
    (out,), got = _pcall(
        body, name=name, grid=(K // tk, N // tn),
        in_specs=[pl.BlockSpec((T, tk), lambda ki, nj: (0, ki)), pl.BlockSpec((T, tn), lambda ki, nj: (0, nj))],
        out_specs=[out_spec], out_shape=[out_shape], comm=comm, operands=(a, b))
    return out, got


def _mix_fwd_a(xb, w_mix4, conv_w, conv_b, w_co4, tm):
    T = xb.shape[0]

    def body(xb_ref, w_ref, cw_ref, cb_ref, wco_ref,
             pc_ref, z_ref, yin_ref, su_ref, sub_ref, gc_ref, gs_ref, yc_ref, qbuf):
        @pl.when(pl.program_id(0) == 0)
        def _():
            qbuf[pl.ds(0, 8), :] = jnp.zeros((8, CONV), F32)

        xv = xb_ref[...]
        p0 = _nn(xv, w_ref[0])
        p1 = _nn(xv, w_ref[1])
        gc_ref[...] = _nn(xv, w_ref[2])
        gs_ref[...] = _nn(xv, w_ref[3])
        cbv, ccv = p0[:, :CONV], p0[:, CONV:]
        chv, suv = p1[:, :CONV], p1[:, CONV:]
        q = ccv * chv
        qbuf[pl.ds(8, tm), :] = q
        cw = cw_ref[...]
        z = (cw[2:3] * q + cw[1:2] * qbuf[pl.ds(7, tm), :] + cw[0:1] * qbuf[pl.ds(6, tm), :]
             + cb_ref[...])
        qbuf[pl.ds(0, 8), :] = q[tm - 8:tm]
        yin = (cbv * z).astype(BF16)
        pc_ref[:, 0:CONV] = cbv.astype(BF16)
        pc_ref[:, CONV:2 * CONV] = ccv.astype(BF16)
        pc_ref[:, 2 * CONV:3 * CONV] = chv.astype(BF16)
        z_ref[...] = z.astype(BF16)
        yin_ref[...] = yin
        su_ref[...] = suv
        sub_ref[...] = suv.astype(BF16)
        for k in range(4):
            yc_ref[:, 256 * k:256 * (k + 1)] = _nn(yin, wco_ref[k])

    def tok(n):
        return pl.BlockSpec((tm, n), lambda i: (i, 0))

    def full(shape):
        return pl.BlockSpec(shape, lambda i: (0,) * len(shape))

    return pl.pallas_call(
        body, name="mix_fwd_a", grid=(T // tm,),
        in_specs=[tok(D), full((4, D, D)), full((3, CONV)), full((1, CONV)), full((4, CONV, 256))],
        out_specs=[tok(3 * CONV), tok(CONV), tok(CONV), tok(SSM), tok(SSM), tok(D), tok(D), tok(D)],
        out_shape=_hbm_out([jax.ShapeDtypeStruct((T, 3 * CONV), BF16), jax.ShapeDtypeStruct((T, CONV), BF16),
                            jax.ShapeDtypeStruct((T, CONV), BF16), jax.ShapeDtypeStruct((T, SSM), F32),
                            jax.ShapeDtypeStruct((T, SSM), BF16), jax.ShapeDtypeStruct((T, D), F32),
                            jax.ShapeDtypeStruct((T, D), F32), jax.ShapeDtypeStruct((T, D), F32)]),
        scratch_shapes=[pltpu.VMEM((tm + 8, CONV), F32)],
        compiler_params=_cp(56, 1),
    )(*_hbm(xb, w_mix4, conv_w, conv_b, w_co4))


def _scan_inplace(bre, bim, ar, ai, T, rev):
    R = SCAN_R
    if rev:
        ai = -ai
    d = 1
    while d < T:
        if d < 8:
            def step(i, _, d=d, ar=ar, ai=ai):
                c = i if rev else T // R - 1 - i
                t0 = pl.multiple_of(c * R, R)
                if rev:
                    wr = bre[pl.ds(t0 + 8, R + 8), :]
                    wi = bim[pl.ds(t0 + 8, R + 8), :]
                    shr = pltpu.roll(wr, R + 8 - d, 0)[0:R]
                    shi = pltpu.roll(wi, R + 8 - d, 0)[0:R]
                    cr, ci = wr[0:R], wi[0:R]
                else:
                    wr = bre[pl.ds(t0, R + 8), :]
                    wi = bim[pl.ds(t0, R + 8), :]
                    shr = pltpu.roll(wr, d, 0)[8:8 + R]
                    shi = pltpu.roll(wi, d, 0)[8:8 + R]
                    cr, ci = wr[8:8 + R], wi[8:8 + R]
                bre[pl.ds(t0 + 8, R), :] = cr + ar * shr - ai * shi
                bim[pl.ds(t0 + 8, R), :] = ci + ar * shi + ai * shr
                return 0

            lax.fori_loop(0, T // R, step, 0)
        else:
            def upd(lo, n, d=d, ar=ar, ai=ai):
                src = lo + d if rev else lo - d
                if not isinstance(lo, int):
                    lo, src = pl.multiple_of(lo + 8, 8), pl.multiple_of(src + 8, 8)
                else:
                    lo, src = lo + 8, src + 8
                cr = bre[pl.ds(lo, n), :]
                ci = bim[pl.ds(lo, n), :]
                shr = bre[pl.ds(src, n), :]
                shi = bim[pl.ds(src, n), :]
                bre[pl.ds(lo, n), :] = cr + ar * shr - ai * shi
                bim[pl.ds(lo, n), :] = ci + ar * shi + ai * shr

            nfull = (T - d) // R if d >= R else T // R - 1

            def step(i, _, upd=upd, d=d):
                if rev:
                    t0 = i * R
                else:
                    t0 = T - (i + 1) * R
                upd(t0, R)
                return 0

            if nfull > 0:
                lax.fori_loop(0, nfull, step, 0)
            if d < R:
                if rev:
                    upd(T - R, R - d)
                else:
                    upd(d, R - d)
        ar, ai = ar * ar - ai * ai, 2.0 * ar * ai
        d *= 2


def _scan_specs(T):
    W = SCAN_W
    lane = pl.BlockSpec((T, W), lambda j: (0, j))
    col = pl.BlockSpec((T, 128), lambda j: (0, j // 2))
    wb = pl.BlockSpec((None, 128, W), lambda j: (j, 0, 0))
    wc = pl.BlockSpec((None, W, 128), lambda j: (j, 0, 0))
    vec = pl.BlockSpec((1, W), lambda j: (0, j))
    return lane, col, wb, wc, vec


def _s5_scan_fwd(su_b, wb_re, wb_im, a_re, a_im, comm=()):
    T = su_b.shape[0]
    W = SCAN_W

    def body(su_ref, wbr_ref, wbi_ref, ar_ref, ai_ref, sr_ref, si_ref, bre, bim):
        zero = jnp.zeros((8, W), F32)
        for buf in (bre, bim):
            buf[pl.ds(0, 8), :] = zero
            buf[pl.ds(T + 8, 8), :] = zero
        su = su_ref[...]
        bre[pl.ds(8, T), :] = _nn(su, wbr_ref[...])
        bim[pl.ds(8, T), :] = _nn(su, wbi_ref[...])
        _scan_inplace(bre, bim, ar_ref[...], ai_ref[...], T, rev=False)
        sr_ref[...] = bre[pl.ds(8, T), :]
        si_ref[...] = bim[pl.ds(8, T), :]

    lane, col, wb, wc, vec = _scan_specs(T)
    return _pcall(
        body, name="s5_scan_fwd", grid=(LANES // W,),
        in_specs=[col, wb, wb, vec, vec],
        out_specs=[lane, lane],
        out_shape=[jax.ShapeDtypeStruct((T, LANES), F32)] * 2,
        scratch=[pltpu.VMEM((T + 16, W), F32)] * 2, comm=comm,
        operands=(su_b, wb_re, wb_im, a_re, a_im))


def _gelu(s):
    th = jnp.tanh(GELU_C * (s + 0.044715 * s * s * s))
    return 0.5 * s * (1.0 + th), th


def _mix_fwd_b(st_re, st_im, wc_re4, wc_im4, su, dvec, w_glu4, g_conv, g_ssm, y_conv, w_mo, x1, g, b, tm, comm=()):
    T = su.shape[0]

    def body(sr_ref, si_ref, wcr_ref, wci_ref, su_ref, d_ref, wg_ref, gc_ref, gs_ref, yc_ref, wmo_ref,
             x_ref, g_ref, b_ref, s_ref, sgb_ref, ga_ref, gb_ref, mb_ref, r_ref, xo_ref, xob_ref):
        srb = sr_ref[...].astype(BF16)
        sib = si_ref[...].astype(BF16)
        ys = [_nn(srb[:, 512 * J:512 * (J + 1)], wcr_ref[J]) + _nn(sib[:, 512 * J:512 * (J + 1)], wci_ref[J])
              for J in range(4)]
        s = jnp.concatenate(ys, axis=1) + d_ref[...] * su_ref[...]
        sg, _ = _gelu(s)
        sgb = sg.astype(BF16)
        ga = jnp.concatenate([_nn(sgb, wg_ref[0]), _nn(sgb, wg_ref[1])], axis=1)
        gb = jnp.concatenate([_nn(sgb, wg_ref[2]), _nn(sgb, wg_ref[3])], axis=1)
        merged = _sig(gc_ref[...]) * yc_ref[...] + _sig(gs_ref[...]) * (ga * _sig(gb))
        mb = merged.astype(BF16)
        r = ALPHA * x_ref[...] + _nn(mb, wmo_ref[...])
        xhat, _ = _ln_stats(r)
        xo = xhat * g_ref[...] + b_ref[...]
        s_ref[...] = s
        sgb_ref[...] = sgb
        ga_ref[...] = ga
        gb_ref[...] = gb
        mb_ref[...] = mb
        r_ref[...] = r
        xo_ref[...] = xo
        xob_ref[...] = xo.astype(BF16)

    def tok(n):
        return pl.BlockSpec((tm, n), lambda i: (i, 0))

    def full(shape):
        return pl.BlockSpec(shape, lambda i: (0,) * len(shape))

    return _pcall(
        body, name="mix_fwd_b", grid=(T // tm,),
        in_specs=[tok(LANES), tok(LANES), full((4, 512, 128)), full((4, 512, 128)), tok(SSM), full((1, SSM)),
                  full((4, SSM, 512)), tok(D), tok(D), tok(D), full((D, D)), tok(D), full((1, D)), full((1, D))],
        out_specs=[tok(SSM), tok(SSM), tok(D), tok(D), tok(D), tok(D), tok(D), tok(D)],
        out_shape=[jax.ShapeDtypeStruct((T, SSM), F32), jax.ShapeDtypeStruct((T, SSM), BF16),
                   jax.ShapeDtypeStruct((T, D), F32), jax.ShapeDtypeStruct((T, D), F32),
                   jax.ShapeDtypeStruct((T, D), BF16), jax.ShapeDtypeStruct((T, D), F32),
                   jax.ShapeDtypeStruct((T, D), F32), jax.ShapeDtypeStruct((T, D), BF16)],
        vmem_mb=56, comm=comm,
        operands=(st_re, st_im, wc_re4, wc_im4, su, dvec, w_glu4, g_conv, g_ssm, y_conv, w_mo, x1, g, b))


def _ple_loss(x3, x3b, p, w_pi4, w_pg, g, b, target, tm):
    T = x3.shape[0]
    PD = p.shape[1]

    def body(x_ref, xb_ref, p_ref, wpi_ref, wpg_ref, g_ref, b_ref, t_ref,
             loss_ref, dx_ref, pb_ref, dpw_ref, dgt_ref, dg_ref, db_ref):
        i = pl.program_id(0)
        pb = p_ref[...].astype(BF16)
        pw = jnp.concatenate([_nn(pb, wpi_ref[k]) for k in range(4)], axis=1)
        gt = _nn(xb_ref[...], wpg_ref[...])
        sg = _sig(gt)
        r = ALPHA * x_ref[...] + pw * sg
        gv = g_ref[...]
        xhat, rstd = _ln_stats(r)
        err = xhat * gv + b_ref[...] - t_ref[...]
        lpart = jnp.zeros((1, 128), F32) + 0.5 * jnp.sum(jnp.mean(err * err, axis=-1, keepdims=True))
        dy = err * (1.0 / D)
        dyg = dy * gv
        m1 = jnp.mean(dyg, axis=-1, keepdims=True)
        m2 = jnp.mean(dyg * xhat, axis=-1, keepdims=True)
        dr = rstd * (dyg - m1 - xhat * m2)
        pg, pbias = _rowsum(dy * xhat), _rowsum(dy)

        @pl.when(i == 0)
        def _():
            loss_ref[...] = lpart
            dg_ref[...] = pg
            db_ref[...] = pbias

        @pl.when(i > 0)
        def _():
            loss_ref[...] += lpart
            dg_ref[...] += pg
            db_ref[...] += pbias

        dgt = (dr * pw * sg * (1.0 - sg)).astype(BF16)
        pb_ref[...] = pb
        dpw_ref[...] = (dr * sg).astype(BF16)
        dgt_ref[...] = dgt
        dx_ref[...] = ALPHA * dr + _nt(dgt, wpg_ref[...])

    def tok(n):
        return pl.BlockSpec((tm, n), lambda i: (i, 0))

    def full(shape):
        return pl.BlockSpec(shape, lambda i: (0,) * len(shape))

    return pl.pallas_call(
        body, name="ple_loss", grid=(T // tm,),
        in_specs=[tok(D), tok(D), tok(PD), full((4, PD, 256)), full((D, D)), full((1, D)), full((1, D)), tok(D)],
        out_specs=[full((1, 128)), tok(D), tok(PD), tok(D), tok(D), full((1, D)), full((1, D))],
        out_shape=_hbm_out([jax.ShapeDtypeStruct((1, 128), F32), jax.ShapeDtypeStruct((T, D), F32),
                            jax.ShapeDtypeStruct((T, PD), BF16), jax.ShapeDtypeStruct((T, D), BF16),
                            jax.ShapeDtypeStruct((T, D), BF16), jax.ShapeDtypeStruct((1, D), F32),
                            jax.ShapeDtypeStruct((1, D), F32)]),
        compiler_params=_cp(48, 1),
    )(*_hbm(x3, x3b, p, w_pi4, w_pg, g, b, target))


def _mix_bwd_b(dy, r2, g, w_mo, g_conv, g_ssm, y_conv, ga, gb, s, su, dvec, w_glu4, wc_re4, wc_im4, tm, comm=()):
    T = dy.shape[0]

    def body(dy_ref, r_ref, g_ref, wmo_ref, gc_ref, gs_ref, yc_ref, ga_ref, gb_ref, s_ref, su_ref, d_ref,
             wg_ref, wcr_ref, wci_ref,
             dres_ref, dmix_ref, dgl_ref, dsb_ref, dud_ref, gsr_ref, gsi_ref, dyc_ref, dp_ref,
             dg_ref, db_ref, dd_ref):
        i = pl.program_id(0)
        dyv = dy_ref[...]
        dr, xhat = _ln_bwd(dyv, r_ref[...], g_ref[...])
        dmix = dr.astype(BF16)
        dmerged = _nt(dmix, wmo_ref[...])
        sc, ss, sgb = _sig(gc_ref[...]), _sig(gs_ref[...]), _sig(gb_ref[...])
        gav = ga_ref[...]
        yssm = gav * sgb
        dgc = dmerged * yc_ref[...] * sc * (1.0 - sc)
        dgss = dmerged * yssm * ss * (1.0 - ss)
        dyssm = dmerged * ss
        dgl = jnp.concatenate([dyssm * sgb, dyssm * gav * sgb * (1.0 - sgb)], axis=1).astype(BF16)
        dsg = (_nt(dgl[:, 0:512], wg_ref[0]) + _nt(dgl[:, 512:1024], wg_ref[1])
               + _nt(dgl[:, 1024:1536], wg_ref[2]) + _nt(dgl[:, 1536:2048], wg_ref[3]))
        sv = s_ref[...]
        _, th = _gelu(sv)
        dgelu = 0.5 * (1.0 + th) + 0.5 * sv * (1.0 - th * th) * GELU_C * (1.0 + 3.0 * 0.044715 * sv * sv)
        ds = dsg * dgelu
        dsb = ds.astype(BF16)
        pg, pb, pd = _rowsum(dyv * xhat), _rowsum(dyv), _rowsum(ds * su_ref[...])

        @pl.when(i == 0)
        def _():
            dg_ref[...] = pg
            db_ref[...] = pb
            dd_ref[...] = pd

        @pl.when(i > 0)
        def _():
            dg_ref[...] += pg
            db_ref[...] += pb
            dd_ref[...] += pd

        dres_ref[...] = ALPHA * dr
        dmix_ref[...] = dmix
        dgl_ref[...] = dgl
        dsb_ref[...] = dsb
        dud_ref[...] = ds * d_ref[...]
        for J in range(4):
            gsr_ref[:, 512 * J:512 * (J + 1)] = _nt(dsb[:, 128 * J:128 * (J + 1)], wcr_ref[J])
            gsi_ref[:, 512 * J:512 * (J + 1)] = _nt(dsb[:, 128 * J:128 * (J + 1)], wci_ref[J])
        dyc_ref[...] = (dmerged * sc).astype(BF16)
        dp_ref[:, 0:D] = dgc.astype(BF16)
        dp_ref[:, D:2 * D] = dgss.astype(BF16)

    def tok(n):
        return pl.BlockSpec((tm, n), lambda i: (i, 0))

    def full(shape):
        return pl.BlockSpec(shape, lambda i: (0,) * len(shape))

    return _pcall(
        body, name="mix_bwd_b", grid=(T // tm,),
        in_specs=[tok(D), tok(D), full((1, D)), full((D, D)), tok(D), tok(D), tok(D), tok(D), tok(D),
                  tok(SSM), tok(SSM), full((1, SSM)), full((4, SSM, 512)), full((4, 512, 128)), full((4, 512, 128))],
        out_specs=[tok(D), tok(D), tok(2 * D), tok(SSM), tok(SSM), tok(LANES), tok(LANES), tok(D),
                   pl.BlockSpec((tm, 2 * D), lambda i: (i, 1)), full((1, D)), full((1, D)), full((1, SSM))],
        out_shape=[jax.ShapeDtypeStruct((T, D), F32), jax.ShapeDtypeStruct((T, D), BF16),
                   jax.ShapeDtypeStruct((T, 2 * D), BF16), jax.ShapeDtypeStruct((T, SSM), BF16),
                   jax.ShapeDtypeStruct((T, SSM), F32), jax.ShapeDtypeStruct((T, LANES), F32),
                   jax.ShapeDtypeStruct((T, LANES), F32), jax.ShapeDtypeStruct((T, D), BF16),
                   jax.ShapeDtypeStruct((T, 4 * D), BF16), jax.ShapeDtypeStruct((1, D), F32),
                   jax.ShapeDtypeStruct((1, D), F32), jax.ShapeDtypeStruct((1, SSM), F32)],
        vmem_mb=56, comm=comm,
        operands=(dy, r2, g, w_mo, g_conv, g_ssm, y_conv, ga, gb, s, su, dvec, w_glu4, wc_re4, wc_im4))


def _s5_scan_bwd(gs_re, gs_im, st_re, st_im, su_b, ds_b, wb_re, wb_im, a_re, a_im, comm=()):
    T = su_b.shape[0]
    W = SCAN_W
    R = SCAN_R

    def body(gr_ref, gi_ref, sr_ref, si_ref, su_ref, ds_ref, wbr_ref, wbi_ref, ar_ref, ai_ref,
             dsu_ref, dwbr_ref, dwbi_ref, dwcr_ref, dwci_ref, dar_ref, dai_ref, gre, gim):
        j = pl.program_id(0)
        zero = jnp.zeros((8, W), F32)
        for buf in (gre, gim):
            buf[pl.ds(0, 8), :] = zero
            buf[pl.ds(T + 8, 8), :] = zero
        gre[pl.ds(8, T), :] = gr_ref[...]
        gim[pl.ds(8, T), :] = gi_ref[...]
        _scan_inplace(gre, gim, ar_ref[...], ai_ref[...], T, rev=True)
        grb = gre[pl.ds(8, T), :].astype(BF16)
        gib = gim[pl.ds(8, T), :].astype(BF16)
        part = _nt(grb, wbr_ref[...]) + _nt(gib, wbi_ref[...])

        @pl.when(j % 2 == 0)
        def _():
            dsu_ref[...] = part

        @pl.when(j % 2 == 1)
        def _():
            dsu_ref[...] += part

        su = su_ref[...]
        dwbr_ref[...] = _tn(su, grb)
        dwbi_ref[...] = _tn(su, gib)
        dsv = ds_ref[...]
        dwcr_ref[...] = _tn(sr_ref[...].astype(BF16), dsv)
        dwci_ref[...] = _tn(si_ref[...].astype(BF16), dsv)
        dar = jnp.zeros((1, W), F32)
        dai = jnp.zeros((1, W), F32)
        for c in range(T // R):
            xr = sr_ref[pl.ds(c * R, R), :]
            xi = si_ref[pl.ds(c * R, R), :]
            g1r = gre[pl.ds(c * R + 9, R), :]
            g1i = gim[pl.ds(c * R + 9, R), :]
            dar = dar + _rowsum(g1r * xr + g1i * xi)
            dai = dai + _rowsum(g1i * xr - g1r * xi)
        dar_ref[...] = dar
        dai_ref[...] = dai

    lane, col, wb, wc, vec = _scan_specs(T)
    return _pcall(
        body, name="s5_scan_bwd", grid=(LANES // W,),
        in_specs=[lane, lane, lane, lane, col, col, wb, wb, vec, vec],
        out_specs=[col, wb, wb, wc, wc, vec, vec],
        out_shape=[jax.ShapeDtypeStruct((T, SSM), F32),
                   jax.ShapeDtypeStruct((LANES // W, 128, W), F32), jax.ShapeDtypeStruct((LANES // W, 128, W), F32),
                   jax.ShapeDtypeStruct((LANES // W, W, 128), F32), jax.ShapeDtypeStruct((LANES // W, W, 128), F32),
                   jax.ShapeDtypeStruct((1, LANES), F32), jax.ShapeDtypeStruct((1, LANES), F32)],
        scratch=[pltpu.VMEM((T + 16, W), F32)] * 2, vmem_mb=56, comm=comm,
        operands=(gs_re, gs_im, st_re, st_im, su_b, ds_b, wb_re, wb_im, a_re, a_im))


def _mix_bwd_a(dyc_b, w_co4, pc, z_b, conv_w, dsu_ssm, du_dir, dproj, dres, w_mix4, tm, comm=()):
    T = dres.shape[0]
    nt = T // tm

    def body(dyc_ref, wco_ref, pc_ref, halo_ref, z_ref, cw_ref, dsu_ref, dud_ref, dpin_ref, dres_ref, w_ref,
             dp_ref, dx_ref, dcw_ref, dcb_ref, dzbuf, qbuf):
        i = pl.program_id(0)
        ii = nt - 1 - i

        @pl.when(i == 0)
        def _():
            dzbuf[pl.ds(tm, 8), :] = jnp.zeros((8, CONV), F32)

        dyc = dyc_ref[...]
        dyin = (_nt(dyc[:, 0:256], wco_ref[0]) + _nt(dyc[:, 256:512], wco_ref[1])
                + _nt(dyc[:, 512:768], wco_ref[2]) + _nt(dyc[:, 768:1024], wco_ref[3]))
        cbv = pc_ref[:, 0:CONV].astype(F32)
        ccv = pc_ref[:, CONV:2 * CONV].astype(F32)
        chv = pc_ref[:, 2 * CONV:3 * CONV].astype(F32)
        dcbv = dyin * z_ref[...].astype(F32)
        dz = dyin * cbv
        dzbuf[pl.ds(0, tm), :] = dz
        cw = cw_ref[...]
        dq = cw[2:3] * dz + cw[1:2] * dzbuf[pl.ds(1, tm), :] + cw[0:1] * dzbuf[pl.ds(2, tm), :]
        dzbuf[pl.ds(tm, 8), :] = dz[0:8]
        q = ccv * chv
        hq = halo_ref[:, CONV:2 * CONV].astype(F32) * halo_ref[:, 2 * CONV:3 * CONV].astype(F32)
        qbuf[pl.ds(0, 8), :] = jnp.where(ii > 0, hq, jnp.zeros_like(hq))
        qbuf[pl.ds(8, tm), :] = q
        pw = jnp.concatenate([_rowsum(dz * qbuf[pl.ds(6, tm), :]), _rowsum(dz * qbuf[pl.ds(7, tm), :]),
                              _rowsum(dz * q), jnp.zeros((5, CONV), F32)], axis=0)
        pbias = _rowsum(dz)

        @pl.when(i == 0)
        def _():
            dcw_ref[...] = pw
            dcb_ref[...] = pbias

        @pl.when(i > 0)
        def _():
            dcw_ref[...] += pw
            dcb_ref[...] += pbias

        dp0 = jnp.concatenate([dcbv, dq * chv], axis=1).astype(BF16)
        dp1 = jnp.concatenate([dq * ccv, dsu_ref[...] + dud_ref[...]], axis=1).astype(BF16)
        dp_ref[:, 0:D] = dp0
        dp_ref[:, D:2 * D] = dp1
        dx_ref[...] = (dres_ref[...] + _nt(dp0, w_ref[0]) + _nt(dp1, w_ref[1])
                       + _nt(dpin_ref[:, 0:D], w_ref[2]) + _nt(dpin_ref[:, D:2 * D], w_ref[3]))

    def tok(n):
        return pl.BlockSpec((tm, n), lambda i: (nt - 1 - i, 0))

    def full(shape):
        return pl.BlockSpec(shape, lambda i: (0,) * len(shape))

    halo = pl.BlockSpec((8, 3 * CONV), lambda i: (jnp.maximum((nt - 1 - i) * (tm // 8) - 1, 0), 0))
    return _pcall(
        body, name="mix_bwd_a", grid=(nt,),
        in_specs=[tok(D), full((4, CONV, 256)), tok(3 * CONV), halo, tok(CONV), full((3, CONV)),
                  tok(SSM), tok(SSM), pl.BlockSpec((tm, 2 * D), lambda i: (nt - 1 - i, 1)), tok(D),
                  full((4, D, D))],
        out_specs=[pl.BlockSpec((tm, 2 * D), lambda i: (nt - 1 - i, 0)), tok(D), full((8, CONV)), full((1, CONV))],
        out_shape=[jax.ShapeDtypeStruct((T, 4 * D), BF16), jax.ShapeDtypeStruct((T, D), F32),
                   jax.ShapeDtypeStruct((8, CONV), F32), jax.ShapeDtypeStruct((1, CONV), F32)],
        scratch=[pltpu.VMEM((tm + 8, CONV), F32), pltpu.VMEM((tm + 8, CONV), F32)],
        aliases={8: 0}, vmem_mb=56, comm=comm,
        operands=(dyc_b, w_co4, pc, pc, z_b, conv_w, dsu_ssm, du_dir, dproj, dres, w_mix4))


def _zoh(lam_re, lam_im, log_step, b_re, b_im):
    dt = jnp.exp(log_step)[:, None]
    mag = jnp.exp(lam_re * dt)
    abr, abi = mag * jnp.cos(lam_im * dt), mag * jnp.sin(lam_im * dt)
    nr, ni = abr - 1.0, abi
    den = lam_re * lam_re + lam_im * lam_im
    cr = (nr * lam_re + ni * lam_im) / den
    ci = (ni * lam_re - nr * lam_im) / den
    bbr = cr[..., None] * b_re - ci[..., None] * b_im
    bbi = cr[..., None] * b_im + ci[..., None] * b_re
    return abr, abi, bbr, bbi


def _wb_blocks(bb):
    eye = jnp.eye(GROUPS, dtype=F32)
    full = jnp.einsum("gni,gh->gihn", bb, eye).reshape(4, 128, 8, SCAN_W)
    return jnp.stack([full[j // 2, :, j, :] for j in range(8)]).astype(BF16)


def _wc_blocks(cc):
    eye = jnp.eye(GROUPS, dtype=F32)
    full = jnp.einsum("gin,gh->gnhi", cc, eye).reshape(4, 512, 4, 128)
    return jnp.stack([full[J, :, J, :] for J in range(4)]).astype(BF16)


_G = np.arange(GROUPS)


def _wb_diag(dwb8):
    d5 = dwb8.reshape(8, 8, 16, 4, 64)
    return d5[_G // 4, _G % 8, :, _G % 4, :].transpose(0, 2, 1)


def _wc_diag(dwc8):
    d5 = dwc8.reshape(8, 4, 64, 8, 16)
    return d5[_G // 4, _G % 4, :, _G % 8, :].transpose(0, 2, 1)


def _where():
    x, y, c = lax.axis_index("x"), lax.axis_index("y"), lax.axis_index("c")
    return x, y, c, 2 * x + y


def _chip_dev(k, c):
    return (k // 2, k % 2, c)


def _slot_cast(meidx, w, dtype, name):
    R, C = w.shape
    tr = _row_tile(R)

    def body(m_ref, w_ref, o_ref):
        o_ref[...] = w_ref[...].astype(dtype)

    gs = pltpu.PrefetchScalarGridSpec(
        num_scalar_prefetch=1, grid=(R // tr,),
        in_specs=[pl.BlockSpec((tr, C), lambda i, m: (i, 0))],
        out_specs=pl.BlockSpec((None, tr, C), lambda i, m: (m[0], i, 0)))
    return pl.pallas_call(
        body, name=name, grid_spec=gs, out_shape=_hbm_out(jax.ShapeDtypeStruct((4, R, C), dtype)),
        compiler_params=_cp(32, 1),
    )(meidx, *_hbm(w))


def _gather_payload(bufs):
    n = len(bufs)

    def half(ref, w, k, cc):
        h = bufs[w].shape[1] // 2
        return ref.at[k, pl.ds(cc * h, h)]

    def ici(ins, outs, sems, w, s):
        x, y, c, me = _where()
        k = (me + 1 + s) % 4
        return pltpu.make_async_remote_copy(
            src_ref=half(ins[w], w, me, c), dst_ref=half(outs[w], w, me, c), send_sem=sems[0].at[3 * w + s],
            recv_sem=sems[1].at[3 * w + s], device_id=_chip_dev(k, c), device_id_type=MESH)

    def landed(outs, sems, w, s):
        x, y, c, me = _where()
        j = (me + 3 - s) % 4
        return pltpu.make_async_remote_copy(
            src_ref=half(outs[w], w, j, c), dst_ref=half(outs[w], w, j, c), send_sem=sems[0].at[3 * w + s],
            recv_sem=sems[1].at[3 * w + s], device_id=(x, y, 1 - c), device_id_type=MESH)

    def passed(outs, sems, w, s, cc):
        x, y, c, me = _where()
        j = (me + 3 - s) % 4
        return pltpu.make_async_remote_copy(
            src_ref=half(outs[w], w, j, cc), dst_ref=half(outs[w], w, j, cc), send_sem=sems[2].at[3 * w + s],
            recv_sem=sems[3].at[3 * w + s], device_id=(x, y, 1 - c), device_id_type=MESH)

    pairs = [(w, s) for w in range(n) for s in range(3)]

    def start(ins, outs, sems):
        for w, s in pairs:
            ici(ins, outs, sems, w, s).start()

    def finish(ins, outs, sems):
        _, _, c, _ = _where()
        for w, s in pairs:
            landed(outs, sems, w, s).wait_recv()
            passed(outs, sems, w, s, c).start()
        for w, s in pairs:
            passed(outs, sems, w, s, 1 - c).wait_recv()
        for w, s in pairs:
            ici(ins, outs, sems, w, s).wait_send()
            passed(outs, sems, w, s, c).wait_send()

    return _Payload(bufs, [jax.ShapeDtypeStruct(b.shape, b.dtype) for b in bufs], {w: w for w in range(n)},
                    [pltpu.SemaphoreType.DMA((3 * n,))] * 4, start, finish)


def _sym_payload(operands, outs, copies, n_copies):
    def start(ins, outs_, sems):
        for cp in copies(ins, outs_, sems[0], sems[1]):
            cp.start()

    def finish(ins, outs_, sems):
        for cp in copies(ins, outs_, sems[0], sems[1]):
            cp.wait()

    return _Payload(operands, outs, {}, [pltpu.SemaphoreType.DMA((n_copies,))] * 2, start, finish)


def _swap_payload(g4s):
    def copies(ins, outs, ss, rs):
        x, y, c, me = _where()
        cps = []
        for w, g in enumerate(g4s):
            h = g.shape[1] // 2
            cps.append(pltpu.make_async_remote_copy(
                src_ref=ins[w].at[:, pl.ds((1 - c) * h, h)], dst_ref=outs[w], send_sem=ss.at[w],
                recv_sem=rs.at[w], device_id=(x, y, 1 - c), device_id_type=MESH))
        return cps

    outs = [jax.ShapeDtypeStruct((4, g.shape[1] // 2, g.shape[2]), g.dtype) for g in g4s]
    return _sym_payload(g4s, outs, copies, len(g4s))


def _exchange_payload(pbs):
    def copies(ins, outs, ss, rs):
        x, y, c, me = _where()
        cps = []
        for w in range(len(pbs)):
            for s in range(3):
                k = (me + 1 + s) % 4
                cps.append(pltpu.make_async_remote_copy(
                    src_ref=ins[w].at[k], dst_ref=outs[w].at[2 - s], send_sem=ss.at[3 * w + s],
                    recv_sem=rs.at[3 * w + s], device_id=_chip_dev(k, c), device_id_type=MESH))
        return cps

    outs = [jax.ShapeDtypeStruct((3,) + p.shape[1:], p.dtype) for p in pbs]
    return _sym_payload(pbs, outs, copies, 3 * len(pbs))


def _join_payload(halves):
    def copies(ins, outs, ss, rs):
        x, y, c, me = _where()
        return [pltpu.make_async_remote_copy(
            src_ref=ins[w], dst_ref=outs[w], send_sem=ss.at[w], recv_sem=rs.at[w],
            device_id=(x, y, 1 - c), device_id_type=MESH) for w in range(len(halves))]

    outs = [jax.ShapeDtypeStruct(a.shape, a.dtype) for a in halves]
    return _sym_payload(halves, outs, copies, len(halves))


def _allgather_payload(v):
    def copies(ins, outs, ss, rs):
        x, y, c, me = _where()
        lin = 4 * x + 2 * y + c
        cps = []
        cps = [pltpu.make_async_copy(ins[0], outs[0].at[lin], ss.at[0])]
        for o in range(1, 8):
            t = (lin + o) % 8
            cps.append(pltpu.make_async_remote_copy(
                src_ref=ins[0], dst_ref=outs[0].at[lin], send_sem=ss.at[o], recv_sem=rs.at[o],
                device_id=(t // 4, (t // 2) % 2, t % 2), device_id_type=MESH))
        return cps

    return _sym_payload([v], [jax.ShapeDtypeStruct((8,) + v.shape, v.dtype)], copies, 8)


def _sum8(buf):
    _, P, C = buf.shape

    def body(b_ref, o_ref):
        acc = b_ref[0]
        for d in range(1, 8):
            acc = acc + b_ref[d]
        o_ref[...] = acc

    return pl.pallas_call(
        body, name="sum8", in_specs=[VMEM_FULL], out_specs=VMEM_FULL,
        out_shape=jax.ShapeDtypeStruct((P, C), F32),
        compiler_params=pltpu.CompilerParams(vmem_limit_bytes=32 << 20),
    )(buf)


def _row_tile(h):
    for t in (256, 176, 128, 64, 32, 16, 8):
        if h % t == 0:
            return t
    raise ValueError(h)


def _pair_sum(cidx, g4, got, name):
    _, R, C = g4.shape
    h = R // 2
    th = _row_tile(h)

    def body(c_ref, a_ref, b_ref, o_ref, ob_ref):
        sm = a_ref[...] + b_ref[...]
        o_ref[...] = sm
        ob_ref[...] = sm.astype(BF16)

    blk = pl.BlockSpec((None, th, C), lambda k, i, c: (k, i, 0))
    gs = pltpu.PrefetchScalarGridSpec(
        num_scalar_prefetch=1, grid=(4, h // th),
        in_specs=[pl.BlockSpec((None, None, th, C), lambda k, i, c: (k, c[0], i, 0)), blk],
        out_specs=[blk, blk])
    return pl.pallas_call(
        body, name=name, grid_spec=gs,
        out_shape=_hbm_out([jax.ShapeDtypeStruct((4, h, C), F32), jax.ShapeDtypeStruct((4, h, C), BF16)]),
        compiler_params=_cp(32, 2),
    )(cidx, *_hbm(g4.reshape(4, 2, h, C), got))


def _chip_sum(meidx, p32, got, name):
    _, h, C = p32.shape
    th = _row_tile(h)

    def body(m_ref, a_ref, b_ref, o_ref):
        o_ref[...] = ((a_ref[...] + b_ref[0].astype(F32)) + b_ref[1].astype(F32)) + b_ref[2].astype(F32)

    gs = pltpu.PrefetchScalarGridSpec(
        num_scalar_prefetch=1, grid=(h // th,),
        in_specs=[pl.BlockSpec((None, th, C), lambda i, m: (m[0], i, 0)),
                  pl.BlockSpec((3, th, C), lambda i, m: (0, i, 0))],
        out_specs=pl.BlockSpec((th, C), lambda i, m: (i, 0)))
    return pl.pallas_call(
        body, name=name, grid_spec=gs, out_shape=_hbm_out(jax.ShapeDtypeStruct((h, C), F32)),
        compiler_params=_cp(32, 1),
    )(meidx, *_hbm(p32, got))


def _adamw_math(w, g, m, v):
    m2 = B1 * m + (1.0 - B1) * g
    v2 = B2 * v + (1.0 - B2) * (g * g)
    m_hat = m2 / (1.0 - B1 ** STEP)
    v_hat = v2 / (1.0 - B2 ** STEP)
    delta = -LR * (m_hat / (jnp.sqrt(v_hat) + EPS) + WD * w)
    return delta, m2, v2


def _adamw_pair(cidx, w, mine, theirs, m, v, name):
    R, C = w.shape
    h = R // 2
    tr = _row_tile(h)
    nh = h // tr

    def body(c_ref, w_ref, a_ref, b_ref, m_ref, v_ref, g_ref, d_ref, mo_ref, vo_ref):
        own = (pl.program_id(0) // nh) == c_ref[0]
        g = jnp.where(own, a_ref[...], b_ref[...])
        d, m2, v2 = _adamw_math(w_ref[...], g, m_ref[...], v_ref[...])
        g_ref[...] = g
        d_ref[...] = d
        mo_ref[...] = m2
        vo_ref[...] = v2

    blk = pl.BlockSpec((tr, C), lambda i, c: (i, 0))
    hblk = pl.BlockSpec((tr, C), lambda i, c: (i % nh, 0))
    gs = pltpu.PrefetchScalarGridSpec(
        num_scalar_prefetch=1, grid=(R // tr,),
        in_specs=[blk, hblk, hblk, blk, blk], out_specs=[blk] * 4)
    return pl.pallas_call(
        body, name=name, grid_spec=gs, out_shape=_hbm_out([jax.ShapeDtypeStruct((R, C), F32)] * 4),
        compiler_params=_cp(32, 1),
    )(cidx, *_hbm(w, mine, theirs, m, v))


def _adamw(w, g, m, v, name):
    R, C = w.shape
    tr = _row_tile(R)

    def body(w_ref, g_ref, m_ref, v_ref, d_ref, mo_ref, vo_ref):
        d, m2, v2 = _adamw_math(w_ref[...], g_ref[...], m_ref[...], v_ref[...])
        d_ref[...] = d
        mo_ref[...] = m2
        vo_ref[...] = v2

    blk = pl.BlockSpec((tr, C), lambda i: (i, 0))
    return pl.pallas_call(
        body, name=name, grid=(R // tr,), in_specs=[blk] * 4, out_specs=[blk] * 3,
        out_shape=_hbm_out([jax.ShapeDtypeStruct((R, C), F32)] * 3),
        compiler_params=_cp(32, 1),
    )(*_hbm(w, g, m, v))


def _pack(arrs):
    flat = jnp.concatenate([a.reshape(-1).astype(F32) for a in arrs])
    rows = -(-flat.shape[0] // 1024)
    rows = -(-rows // 8) * 8
    return jnp.pad(flat, (0, rows * 1024 - flat.shape[0])).reshape(rows, 1024)


def _unpack(packed, shapes):
    flat = packed.reshape(-1)
    out, off = [], 0
    for s in shapes:
        n = math.prod(s)
        out.append(flat[off:off + n].reshape(s))
        off += n
    return out


BIG = ["ffn1_w_in", "ffn1_w_out", "mix_w_in", "conv_w_out", "ssm_w_glu", "mix_w_out",
       "ffn2_w_in", "ffn2_w_out", "ple_w_in", "ple_w_gate"]
SMALL = ["ln1_g", "ln1_b", "conv_w", "conv_b", "ssm_lam_re", "ssm_lam_im", "ssm_log_step", "ssm_b_re", "ssm_b_im",
         "ssm_c_re", "ssm_c_im", "ssm_d", "ln2_g", "ln2_b", "ln3_g", "ln3_b", "ln4_g", "ln4_b"]
WEIGHTS = ["ffn1_w_in", "ffn1_w_out", "ln1_g", "ln1_b", "mix_w_in", "conv_w", "conv_b", "conv_w_out",
           "ssm_lam_re", "ssm_lam_im", "ssm_log_step", "ssm_b_re", "ssm_b_im", "ssm_c_re", "ssm_c_im", "ssm_d",
           "ssm_w_glu", "mix_w_out", "ln2_g", "ln2_b", "ffn2_w_in", "ffn2_w_out", "ln3_g", "ln3_b",
           "ple_w_in", "ple_w_gate", "ln4_g", "ln4_b"]


class _NoComm:
    def __init__(self, W):
        self.W, self.G, self.raw = dict(W), {}, None

    def carry(self, name):
        return ()

    def landed(self, name, got):
        pass

    def grad(self, name, g4):
        self.G[name] = g4

    def small(self, raw):
        self.raw = raw


def _local_step(x, p, target, sp, sched, tm_ffn, tm_mix):
    W = sched.W
    abr, abi, bbr, bbi = _zoh(sp["ssm_lam_re"], sp["ssm_lam_im"], sp["ssm_log_step"], sp["ssm_b_re"], sp["ssm_b_im"])
    wb_re, wb_im = _wb_blocks(bbr), _wb_blocks(bbi)
    wc_re4, wc_im4 = _wc_blocks(sp["ssm_c_re"]), _wc_blocks(-sp["ssm_c_im"])
    a_re, a_im = abr.reshape(1, LANES), abi.reshape(1, LANES)
    dvec = sp["ssm_d"].reshape(1, SSM)

    def run(fn, name, *args, **kw):
        outs, got = fn(*args, comm=sched.carry(name), **kw)
        sched.landed(name, got)
        return outs

    def dw(name, wname, a, b, tk, tn, shape4, shard_cols=None, interleaved=False):
        out, got = _mm_tn(a, b, tk, tn, name, shard_cols=shard_cols, interleaved=interleaved,
                          comm=sched.carry(name))
        sched.landed(name, got)
        sched.grad(wname, out.reshape(shape4))

    xb = x.astype(BF16)
    h1, r1, x1, x1b = run(_ffn_fwd, "ffn1_fwd", x, xb, W["ffn1_w_in"], W["ffn1_w_out"].reshape(2, FFH, D),
                          sp["ln1_g"], sp["ln1_b"], tm_ffn, "ffn1_fwd")
    conv_w = W["conv_w"][:, 0:3, :].transpose(1, 0, 2).reshape(3, CONV)
    pc, z_b, yin_b, su, su_b, g_conv, g_ssm, y_conv = _mix_fwd_a(
        x1b, W["mix_w_in"], conv_w, sp["conv_b"], W["conv_w_out"], tm_mix)
    st_re, st_im = run(_s5_scan_fwd, "s5_scan_fwd", su_b, wb_re, wb_im, a_re, a_im)
    w_mo = W["mix_w_out"].reshape(D, D)
    s, sg_b, ga, gb, merged_b, r2, x2, x2b = run(
        _mix_fwd_b, "mix_fwd_b", st_re, st_im, wc_re4, wc_im4, su, dvec, W["ssm_w_glu"], g_conv, g_ssm, y_conv,
        w_mo, x1, sp["ln2_g"], sp["ln2_b"], tm_mix)
    w2o2 = W["ffn2_w_out"].reshape(2, FFH, D)
    h2, r3, x3, x3b = run(_ffn_fwd, "ffn2_fwd", x2, x2b, W["ffn2_w_in"], w2o2, sp["ln3_g"], sp["ln3_b"], tm_ffn,
                          "ffn2_fwd")
    loss_part, dx3, p_b, dpw_b, dgt_b, dg4, db4 = _ple_loss(
        x3, x3b, p, W["ple_w_in"], W["ple_w_gate"].reshape(D, D), sp["ln4_g"], sp["ln4_b"], target, tm_mix)

    dw("dw_ple_gate", "ple_w_gate", x3b, dgt_b, 512, 1024, (4, 256, D))
    dw("dw_ple_in", "ple_w_in", p_b, dpw_b, 256, 256, (4, 256, 256), shard_cols=256)
    dx2, dh2, a2_b, df2_b, dg3, db3 = run(_ffn_bwd, "ffn2_bwd", dx3, r3, sp["ln3_g"], h2, W["ffn2_w_in"], w2o2,
                                          tm_ffn, "ffn2_bwd")
    dw("dw_ffn2_in", "ffn2_w_in", x2b, dh2, 512, FFH, (4, D, FFH), shard_cols=FFH, interleaved=True)
    dw("dw_ffn2_out", "ffn2_w_out", a2_b, df2_b, FFH, 1024, (4, FF // 4, D))
    (dres, dmix_b, dgl_b, ds_b, du_dir, gs_re, gs_im, dyc_b, dproj, dg2, db2, dd) = run(
        _mix_bwd_b, "mix_bwd_b", dx2, r2, sp["ln2_g"], w_mo, g_conv, g_ssm, y_conv, ga, gb, s, su, dvec,
        W["ssm_w_glu"], wc_re4, wc_im4, tm_mix)
    dw("dw_mix_out", "mix_w_out", merged_b, dmix_b, 512, 1024, (4, 256, D))
    dw("dw_glu", "ssm_w_glu", sg_b, dgl_b, 512, 512, (4, SSM, 512), shard_cols=512)
    dsu_ssm, dwb_re, dwb_im, dwc_re, dwc_im, da_re, da_im = run(
        _s5_scan_bwd, "s5_scan_bwd", gs_re, gs_im, st_re, st_im, su_b, ds_b, wb_re, wb_im, a_re, a_im)
    dw("dw_conv_out", "conv_w_out", yin_b, dyc_b, 512, 256, (4, CONV, 256), shard_cols=256)
    dproj, dx1, dcw8, dcb = run(_mix_bwd_a, "mix_bwd_a", dyc_b, W["conv_w_out"], pc, z_b, conv_w, dsu_ssm,
                                du_dir, dproj, dres, W["mix_w_in"], tm_mix)
    dw("dw_mix_in", "mix_w_in", x1b, dproj, 512, 1024, (4, D, D), shard_cols=1024)
    dx0, dh1, a1_b, df1_b, dg1, db1 = run(_ffn_bwd, "ffn1_bwd", dx1, r1, sp["ln1_g"], h1, W["ffn1_w_in"],
                                          W["ffn1_w_out"].reshape(2, FFH, D), tm_ffn, "ffn1_bwd")
    sched.small(dict(
        ln1_g=dg1, ln1_b=db1, ln2_g=dg2, ln2_b=db2, ln3_g=dg3, ln3_b=db3, ln4_g=dg4, ln4_b=db4,
        conv_w=dcw8[0:3], conv_b=dcb,
        a_re=da_re.reshape(GROUPS, STATE), a_im=da_im.reshape(GROUPS, STATE),
        bb_re=_wb_diag(dwb_re), bb_im=_wb_diag(dwb_im),
        ssm_c_re=_wc_diag(dwc_re), ssm_c_im=-_wc_diag(dwc_im), ssm_d=dd.reshape(GROUPS, 16),
        loss=loss_part[0:1, 0]))
    dw("dw_ffn1_in", "ffn1_w_in", xb, dh1, 512, FFH, (4, D, FFH), shard_cols=FFH, interleaved=True)
    dw("dw_ffn1_out", "ffn1_w_out", a1_b, df1_b, FFH, 1024, (4, FF // 4, D))
    return loss_part[0, 0], dx0


RAW_ORDER = ["ln1_g", "ln1_b", "ln2_g", "ln2_b", "ln3_g", "ln3_b", "ln4_g", "ln4_b", "conv_w", "conv_b",
             "a_re", "a_im", "bb_re", "bb_im", "ssm_c_re", "ssm_c_im", "ssm_d", "loss"]

GATHER_FIRST = ["ffn1_w_in", "ffn1_w_out"]
GATHER_AT = {"ffn1_fwd": ["mix_w_in", "conv_w_out", "conv_w", "ssm_w_glu", "mix_w_out"],
             "s5_scan_fwd": ["ffn2_w_in"], "mix_fwd_b": ["ffn2_w_out"], "ffn2_fwd": ["ple_w_in", "ple_w_gate"]}
REDUCE_GROUP = {"ple": ["ple_w_gate", "ple_w_in"], "ffn2": ["ffn2_w_in", "ffn2_w_out"],
                "mix": ["mix_w_out", "ssm_w_glu", "conv_w_out", "mix_w_in"], "ffn1": ["ffn1_w_in", "ffn1_w_out"]}
REDUCE_AT = {"ffn2_bwd": [("swap", "ple")], "dw_ffn2_in": [("exchange", "ple")],
             "mix_bwd_b": [("swap", "ffn2"), ("join", "ple")], "s5_scan_bwd": [("exchange", "ffn2")],
             "mix_bwd_a": [("join", "ffn2")], "ffn1_bwd": [("swap", "mix")],
             "dw_ffn1_in": [("exchange", "mix"), ("small", None)]}
REDUCE_TAIL = [[("swap", "ffn1"), ("join", "mix")], [("exchange", "ffn1")], [("join", "ffn1")]]


class _Sched:
    def __init__(self, bufs, cidx, meidx):
        self.bufs, self.cidx, self.meidx = bufs, cidx, meidx
        self.W, self.G, self.raw, self.small_buf = {}, {}, None, None
        self.got1, self.p32, self.pbf, self.got2, self.half, self.theirs = {}, {}, {}, {}, {}, {}
        self._open = []
        self._standalone("gather_ffn1", [("gather", GATHER_FIRST)])

    def _payload(self, stage, key):
        if stage == "gather":
            return _gather_payload([self.bufs[n] for n in key])
        if stage == "small":
            return _allgather_payload(_pack([self.raw[k] for k in RAW_ORDER]))
        names = REDUCE_GROUP[key]
        if stage == "swap":
            return _swap_payload([self.G[n] for n in names])
        if stage == "exchange":
            for n in names:
                self.p32[n], self.pbf[n] = _pair_sum(self.cidx, self.G[n], self.got1[n], "pair_sum_" + n)
            return _exchange_payload([self.pbf[n] for n in names])
        for n in names:
            self.half[n] = _chip_sum(self.meidx, self.p32[n], self.got2[n], "chip_sum_" + n)
        return _join_payload([self.half[n] for n in names])

    def _store(self, stages, got):
        for (stage, key), outs in zip(stages, got):
            if stage == "gather":
                self.W.update(zip(key, outs))
            elif stage == "small":
                self.small_buf = outs[0]
            else:
                {"swap": self.got1, "exchange": self.got2, "join": self.theirs}[stage].update(
                    zip(REDUCE_GROUP[key], outs))

    def _standalone(self, name, stages):
        self._store(stages, _comm_call(name, [self._payload(s, k) for s, k in stages]))

    def carry(self, name):
        self._open = [("gather", GATHER_AT[name])] if name in GATHER_AT else []
        self._open += REDUCE_AT.get(name, [])
        return tuple(self._payload(s, k) for s, k in self._open)

    def landed(self, name, got):
        self._store(self._open, got)

    def grad(self, name, g4):
        self.G[name] = g4

    def small(self, raw):
        self.raw = raw

    def tail(self):
        for i, stages in enumerate(REDUCE_TAIL):
            self._standalone("reduce_tail_%d" % i, stages)


def _small_grads(raw_sum, sp):
    _, vjp = jax.vjp(_zoh, sp["ssm_lam_re"], sp["ssm_lam_im"], sp["ssm_log_step"], sp["ssm_b_re"], sp["ssm_b_im"])
    d_lre, d_lim, d_ls, d_bre, d_bim = vjp((raw_sum["a_re"], raw_sum["a_im"], raw_sum["bb_re"], raw_sum["bb_im"]))
    g = {k: raw_sum[k] for k in ("ln1_g", "ln1_b", "ln2_g", "ln2_b", "ln3_g", "ln3_b", "ln4_g", "ln4_b",
                                 "conv_w", "conv_b", "ssm_c_re", "ssm_c_im", "ssm_d")}
    g.update(ssm_lam_re=d_lre, ssm_lam_im=d_lim, ssm_log_step=d_ls, ssm_b_re=d_bre, ssm_b_im=d_bim)
    return g


def kernel(x, p, ffn1_w_in, ffn1_w_out, ln1_g, ln1_b, mix_w_in, conv_w, conv_b, conv_w_out, ssm_lam_re, ssm_lam_im, ssm_log_step, ssm_b_re, ssm_b_im, ssm_c_re, ssm_c_im, ssm_d, ssm_w_glu, mix_w_out, ln2_g, ln2_b, ffn2_w_in, ffn2_w_out, ln3_g, ln3_b, ple_w_in, ple_w_gate, ln4_g, ln4_b, loss_target, m_ffn1_w_in, m_ffn1_w_out, m_ln1_g, m_ln1_b, m_mix_w_in, m_conv_w, m_conv_b, m_conv_w_out, m_ssm_lam_re, m_ssm_lam_im, m_ssm_log_step, m_ssm_b_re, m_ssm_b_im, m_ssm_c_re, m_ssm_c_im, m_ssm_d, m_ssm_w_glu, m_mix_w_out, m_ln2_g, m_ln2_b, m_ffn2_w_in, m_ffn2_w_out, m_ln3_g, m_ln3_b, m_ple_w_in, m_ple_w_gate, m_ln4_g, m_ln4_b, v_ffn1_w_in, v_ffn1_w_out, v_ln1_g, v_ln1_b, v_mix_w_in, v_conv_w, v_conv_b, v_conv_w_out, v_ssm_lam_re, v_ssm_lam_im, v_ssm_log_step, v_ssm_b_re, v_ssm_b_im, v_ssm_c_re, v_ssm_c_im, v_ssm_d, v_ssm_w_glu, v_mix_w_out, v_ln2_g, v_ln2_b, v_ffn2_w_in, v_ffn2_w_out, v_ln3_g, v_ln3_b, v_ple_w_in, v_ple_w_gate, v_ln4_g, v_ln4_b):
    args = dict(locals())
    w = {n: args[n] for n in WEIGHTS}
    m = {n: args["m_" + n] for n in WEIGHTS}
    v = {n: args["v_" + n] for n in WEIGHTS}
    _, _, c, me = _where()
    cidx = jnp.reshape(c, (1,)).astype(jnp.int32)
    meidx = jnp.reshape(me, (1,)).astype(jnp.int32)

    bufs = {n: _slot_cast(meidx, w[n][0], BF16, "cast_" + n) for n in BIG}
    bufs["conv_w"] = _slot_cast(meidx, jnp.pad(conv_w[0], ((0, 13), (0, 0))), F32, "cast_conv_w")
    sched = _Sched(bufs, cidx, meidx)

    sp = {n: (w[n] if w[n].ndim == 2 and n != "ssm_log_step" else w[n][0]) for n in SMALL if n != "conv_w"}
    loss_part, dx0 = _local_step(x[0], p[0, 0], loss_target[0], sp, sched, 256, 256)
    sched.tail()

    raw_shapes = [sched.raw[k].shape for k in RAW_ORDER]
    raw_sum = dict(zip(RAW_ORDER, _unpack(_sum8(sched.small_buf), raw_shapes)))
    loss = raw_sum["loss"][0]
    sg = _small_grads(raw_sum, sp)
    sg["conv_w"] = lax.dynamic_slice_in_dim(sg["conv_w"], me * 128, 128, axis=1)
    small_shapes = [w[n].shape for n in SMALL]
    gp = _pack([sg[n] for n in SMALL])
    d_s, m_s, v_s = _adamw(_pack([w[n] for n in SMALL]), gp, _pack([m[n] for n in SMALL]),
                           _pack([v[n] for n in SMALL]), "adamw_small")

    out_g, out_d, out_m, out_v = {}, {}, {}, {}
    for n, a, b_, c_, d_ in zip(SMALL, _unpack(gp, small_shapes), _unpack(d_s, small_shapes),
                                _unpack(m_s, small_shapes), _unpack(v_s, small_shapes)):
        out_g[n], out_d[n], out_m[n], out_v[n] = a, b_, c_, d_
    for n in BIG:
        g, dl, mn, vn = _adamw_pair(cidx, w[n][0], sched.half[n], sched.theirs[n], m[n][0], v[n][0], "adamw_" + n)
        out_g[n], out_d[n], out_m[n], out_v[n] = g[None], dl[None], mn[None], vn[None]

    return (loss, dx0[None], *[out_g[n] for n in WEIGHTS], *[out_d[n] for n in WEIGHTS],
            *[out_m[n] for n in WEIGHTS], *[out_v[n] for n in WEIGHTS])
```

```python
import functools
import math

import jax
import jax.numpy as jnp
import numpy as np
from jax import lax
from jax.experimental import pallas as pl
from jax.experimental.pallas import tpu as pltpu

F32, BF16 = jnp.float32, jnp.bfloat16
D = 1024
FF = 2816
FFH = FF // 2
CONV = 512
SSM = 512
GROUPS = 32
STATE = 64
LANES = GROUPS * STATE
SCAN_W = 256
SCAN_R = 256
ALPHA = 2.0 ** 0.25
LN_EPS = 1e-5
GELU_C = math.sqrt(2.0 / math.pi)
B1, B2, LR, EPS, WD, STEP = 0.9, 0.999, 0.001, 1e-8, 0.01, 10
MESH = pl.DeviceIdType.MESH
ANY = pl.BlockSpec(memory_space=pl.ANY)
VMEM_FULL = pl.BlockSpec(memory_space=pltpu.VMEM)


def _cp(vmem_mb=48, n_axes=1):
    return pltpu.CompilerParams(vmem_limit_bytes=vmem_mb << 20,
                                dimension_semantics=("arbitrary",) * n_axes)


def _hbm(*arrs):
    return [pltpu.with_memory_space_constraint(a, pltpu.HBM) for a in arrs]


def _hbm_out(shapes):
    if isinstance(shapes, (list, tuple)):
        return [pltpu.HBM(s.shape, s.dtype) for s in shapes]
    return pltpu.HBM(shapes.shape, shapes.dtype)


def _nn(a, b):
    return jnp.dot(a, b, preferred_element_type=F32)


def _nt(a, b):
    return lax.dot_general(a, b, (((1,), (1,)), ((), ())), preferred_element_type=F32)


def _tn(a, b):
    return lax.dot_general(a, b, (((0,), (0,)), ((), ())), preferred_element_type=F32)


def _sig(v):
    return jax.nn.sigmoid(v)


def _ln_stats(r):
    mu = jnp.mean(r, axis=-1, keepdims=True)
    xc = r - mu
    var = jnp.mean(xc * xc, axis=-1, keepdims=True)
    rstd = lax.rsqrt(var + LN_EPS)
    return xc * rstd, rstd


def _ln_bwd(dy, r, g):
    xhat, rstd = _ln_stats(r)
    dyg = dy * g
    m1 = jnp.mean(dyg, axis=-1, keepdims=True)
    m2 = jnp.mean(dyg * xhat, axis=-1, keepdims=True)
    return rstd * (dyg - m1 - xhat * m2), xhat


def _rowsum(v):
    return jnp.sum(v, axis=0, keepdims=True)


class _Payload:
    def __init__(self, operands, outs, aliases, sems, start, finish):
        self.operands, self.outs, self.aliases, self.sems = list(operands), list(outs), dict(aliases), list(sems)
        self.start, self.finish = start, finish


def _split(flat, comm, attr):
    out, i = [], 0
    for p in comm:
        n = len(getattr(p, attr))
        out.append(list(flat[i:i + n]))
        i += n
    return out


def _run_comm(comm, which, cin, cout, csem):
    for p, a, b, s in zip(comm, _split(cin, comm, "operands"), _split(cout, comm, "outs"), _split(csem, comm, "sems")):
        getattr(p, which)(a, b, s)


def _pcall(body, *, name, grid, in_specs, out_specs, out_shape, operands, scratch=(), vmem_mb=48, aliases=None,
           comm=()):
    ni, no, ns = len(in_specs), len(out_specs), len(scratch)
    c_ops = [a for p in comm for a in p.operands]
    c_outs = [s for p in comm for s in p.outs]
    c_sems = [s for p in comm for s in p.sems]
    io = dict(aliases or {})
    off_i, off_o = ni, no
    for p in comm:
        for a, b in p.aliases.items():
            io[off_i + a] = off_o + b
        off_i += len(p.operands)
        off_o += len(p.outs)

    def wrapped(*refs):
        ins, cin = refs[:ni], refs[ni:ni + len(c_ops)]
        o0 = ni + len(c_ops)
        outs, cout = refs[o0:o0 + no], refs[o0 + no:o0 + no + len(c_outs)]
        s0 = o0 + no + len(c_outs)
        scr, csem = refs[s0:s0 + ns], refs[s0 + ns:]
        if comm:
            first = functools.reduce(jnp.logical_and, [pl.program_id(a) == 0 for a in range(len(grid))])
            pl.when(first)(lambda: _run_comm(comm, "start", cin, cout, csem))
        body(*ins, *outs, *scr)
        if comm:
            last = functools.reduce(jnp.logical_and, [pl.program_id(a) == grid[a] - 1 for a in range(len(grid))])
            pl.when(last)(lambda: _run_comm(comm, "finish", cin, cout, csem))

    res = pl.pallas_call(
        wrapped, name=name, grid=grid,
        in_specs=list(in_specs) + [ANY] * len(c_ops), out_specs=list(out_specs) + [ANY] * len(c_outs),
        out_shape=_hbm_out(list(out_shape) + c_outs), scratch_shapes=list(scratch) + c_sems,
        input_output_aliases=io,
        compiler_params=pltpu.CompilerParams(vmem_limit_bytes=vmem_mb << 20,
                                             dimension_semantics=("arbitrary",) * len(grid),
                                             has_side_effects=bool(comm)),
    )(*_hbm(*operands, *c_ops))
    return list(res[:no]), _split(res[no:], comm, "outs")


def _comm_call(name, comm):
    c_ops = [a for p in comm for a in p.operands]
    c_outs = [s for p in comm for s in p.outs]
    c_sems = [s for p in comm for s in p.sems]
    io, off_i, off_o = {}, 0, 0
    for p in comm:
        for a, b in p.aliases.items():
            io[off_i + a] = off_o + b
        off_i += len(p.operands)
        off_o += len(p.outs)

    def body(*refs):
        cin, cout = refs[:len(c_ops)], refs[len(c_ops):len(c_ops) + len(c_outs)]
        csem = refs[len(c_ops) + len(c_outs):]
        _run_comm(comm, "start", cin, cout, csem)
        _run_comm(comm, "finish", cin, cout, csem)

    res = pl.pallas_call(
        body, name=name, in_specs=[ANY] * len(c_ops), out_specs=[ANY] * len(c_outs), out_shape=_hbm_out(c_outs),
        scratch_shapes=c_sems, input_output_aliases=io,
        compiler_params=pltpu.CompilerParams(has_side_effects=True),
    )(*_hbm(*c_ops))
    return _split(res, comm, "outs")


def _ffn_fwd(x, xb, w_in4, w_out2, g, b, tm, name, comm=()):
    T = x.shape[0]

    def body(x_ref, xb_ref, wg_ref, wu_ref, wo_ref, g_ref, b_ref, h_ref, r_ref, xo_ref, xob_ref, acc):
        k = pl.program_id(1)
        xv = xb_ref[...]
        gt = _nn(xv, wg_ref[...])
        up = _nn(xv, wu_ref[...])
        a = (gt * _sig(gt) * up).astype(BF16)
        h_ref[:, 0:FFH] = gt.astype(BF16)
        h_ref[:, FFH:2 * FFH] = up.astype(BF16)
        acc[...] = jnp.where(k == 0, 0.0, acc[...]) + _nn(a, wo_ref[...])

        @pl.when(k == 1)
        def _():
            r = ALPHA * x_ref[...] + 0.5 * acc[...]
            xhat, _ = _ln_stats(r)
            xo = xhat * g_ref[...] + b_ref[...]
            r_ref[...] = r
            xo_ref[...] = xo
            xob_ref[...] = xo.astype(BF16)

    tok = pl.BlockSpec((tm, D), lambda i, k: (i, 0))
    vec = pl.BlockSpec((1, D), lambda i, k: (0, 0))
    return _pcall(
        body, name=name, grid=(T // tm, 2),
        in_specs=[tok, tok,
                  pl.BlockSpec((None, D, FFH), lambda i, k: (k, 0, 0)),
                  pl.BlockSpec((None, D, FFH), lambda i, k: (k + 2, 0, 0)),
                  pl.BlockSpec((None, FFH, D), lambda i, k: (k, 0, 0)),
                  vec, vec],
        out_specs=[pl.BlockSpec((tm, FF), lambda i, k: (i, k)), tok, tok, tok],
        out_shape=[jax.ShapeDtypeStruct((T, 2 * FF), BF16), jax.ShapeDtypeStruct((T, D), F32),
                   jax.ShapeDtypeStruct((T, D), F32), jax.ShapeDtypeStruct((T, D), BF16)],
        scratch=[pltpu.VMEM((tm, D), F32)], vmem_mb=56, comm=comm,
        operands=(x, xb, w_in4, w_in4, w_out2, g, b))


def _ffn_bwd(dy, r, g, h, w_in4, w_out2, tm, name, comm=()):
    T = dy.shape[0]

    def body(dy_ref, r_ref, g_ref, h_ref, wg_ref, wu_ref, wo_ref,
             dx_ref, dh_ref, a_ref, df_ref, dg_ref, db_ref, acc, dr_s, dfb_s):
        i, k = pl.program_id(0), pl.program_id(1)

        @pl.when(k == 0)
        def _():
            dyv = dy_ref[...]
            dr, xhat = _ln_bwd(dyv, r_ref[...], g_ref[...])
            pg, pb = _rowsum(dyv * xhat), _rowsum(dyv)

            @pl.when(i == 0)
            def _():
                dg_ref[...] = pg
                db_ref[...] = pb

            @pl.when(i > 0)
            def _():
                dg_ref[...] += pg
                db_ref[...] += pb

            dr_s[...] = dr
            dfb = (0.5 * dr).astype(BF16)
            dfb_s[...] = dfb
            df_ref[...] = dfb

        da = _nt(dfb_s[...], wo_ref[...])
        gt = h_ref[:, 0:FFH].astype(F32)
        up = h_ref[:, FFH:2 * FFH].astype(F32)
        sg = _sig(gt)
        silu = gt * sg
        dgate = (da * up * (sg * (1.0 + gt * (1.0 - sg)))).astype(BF16)
        dup = (da * silu).astype(BF16)
        a_ref[...] = (silu * up).astype(BF16)
        dh_ref[:, 0:FFH] = dgate
        dh_ref[:, FFH:2 * FFH] = dup
        acc[...] = jnp.where(k == 0, 0.0, acc[...]) + _nt(dgate, wg_ref[...]) + _nt(dup, wu_ref[...])

        @pl.when(k == 1)
        def _():
            dx_ref[...] = ALPHA * dr_s[...] + acc[...]

    tok = pl.BlockSpec((tm, D), lambda i, k: (i, 0))
    vec = pl.BlockSpec((1, D), lambda i, k: (0, 0))
    wide = pl.BlockSpec((tm, FF), lambda i, k: (i, k))
    return _pcall(
        body, name=name, grid=(T // tm, 2),
        in_specs=[tok, tok, vec, wide,
                  pl.BlockSpec((None, D, FFH), lambda i, k: (k, 0, 0)),
                  pl.BlockSpec((None, D, FFH), lambda i, k: (k + 2, 0, 0)),
                  pl.BlockSpec((None, FFH, D), lambda i, k: (k, 0, 0))],
        out_specs=[tok, wide, pl.BlockSpec((tm, FFH), lambda i, k: (i, k)), tok, vec, vec],
        out_shape=[jax.ShapeDtypeStruct((T, D), F32), jax.ShapeDtypeStruct((T, 2 * FF), BF16),
                   jax.ShapeDtypeStruct((T, FF), BF16), jax.ShapeDtypeStruct((T, D), BF16),
                   jax.ShapeDtypeStruct((1, D), F32), jax.ShapeDtypeStruct((1, D), F32)],
        scratch=[pltpu.VMEM((tm, D), F32), pltpu.VMEM((tm, D), F32), pltpu.VMEM((tm, D), BF16)],
        vmem_mb=56, comm=comm, operands=(dy, r, g, h, w_in4, w_in4, w_out2))


def _mm_tn(a, b, tk, tn, name, shard_cols=None, interleaved=False, comm=()):
    T, K = a.shape
    N = b.shape[1]

    def body(a_ref, b_ref, o_ref):
        o_ref[...] = _tn(a_ref[...], b_ref[...])

    if shard_cols is None:
        out_shape = jax.ShapeDtypeStruct((K, N), F32)
        out_spec = pl.BlockSpec((tk, tn), lambda ki, nj: (ki, nj))
    else:
        per = shard_cols // tn

        def shard(nj):
            blk = nj // per
            return (blk % 2) * 2 + blk // 2 if interleaved else blk

        out_shape = jax.ShapeDtypeStruct((N // shard_cols, K, shard_cols), F32)
        out_spec = pl.BlockSpec((None, tk, tn), lambda ki, nj: (shard(nj), ki, nj % per))
    (out,), got = _pcall(
        body, name=name, grid=(K // tk, N // tn),
        in_specs=[pl.BlockSpec((T, tk), lambda ki, nj: (0, ki)), pl.BlockSpec((T, tn), lambda ki, nj: (0, nj))],
        out_specs=[out_spec], out_shape=[out_shape], comm=comm, operands=(a, b))
    return out, got


def _mix_fwd_a(xb, w_mix4, conv_w, conv_b, w_co4, tm):
    T = xb.shape[0]

    def body(xb_ref, w_ref, cw_ref, cb_ref, wco_ref,
             pc_ref, z_ref, yin_ref, su_ref, sub_ref, gc_ref, gs_ref, yc_ref, qbuf):
        @pl.when(pl.program_id(0) == 0)
        def _():
            qbuf[pl.ds(0, 8), :] = jnp.zeros((8, CONV), F32)

        xv = xb_ref[...]
        p0 = _nn(xv, w_ref[0])
        p1 = _nn(xv, w_ref[1])
        gc_ref[...] = _nn(xv, w_ref[2])
        gs_ref[...] = _nn(xv, w_ref[3])
        cbv, ccv = p0[:, :CONV], p0[:, CONV:]
        chv, suv = p1[:, :CONV], p1[:, CONV:]
        q = ccv * chv
        qbuf[pl.ds(8, tm), :] = q
        cw = cw_ref[...]
        z = (cw[2:3] * q + cw[1:2] * qbuf[pl.ds(7, tm), :] + cw[0:1] * qbuf[pl.ds(6, tm), :]
             + cb_ref[...])
        qbuf[pl.ds(0, 8), :] = q[tm - 8:tm]
        yin = (cbv * z).astype(BF16)
        pc_ref[:, 0:CONV] = cbv.astype(BF16)
        pc_ref[:, CONV:2 * CONV] = ccv.astype(BF16)
        pc_ref[:, 2 * CONV:3 * CONV] = chv.astype(BF16)
        z_ref[...] = z.astype(BF16)
        yin_ref[...] = yin
        su_ref[...] = suv
        sub_ref[...] = suv.astype(BF16)
        for k in range(4):
            yc_ref[:, 256 * k:256 * (k + 1)] = _nn(yin, wco_ref[k])

    def tok(n):
        return pl.BlockSpec((tm, n), lambda i: (i, 0))

    def full(shape):
        return pl.BlockSpec(shape, lambda i: (0,) * len(shape))

    return pl.pallas_call(
        body, name="mix_fwd_a", grid=(T // tm,),
        in_specs=[tok(D), full((4, D, D)), full((3, CONV)), full((1, CONV)), full((4, CONV, 256))],
        out_specs=[tok(3 * CONV), tok(CONV), tok(CONV), tok(SSM), tok(SSM), tok(D), tok(D), tok(D)],
        out_shape=_hbm_out([jax.ShapeDtypeStruct((T, 3 * CONV), BF16), jax.ShapeDtypeStruct((T, CONV), BF16),
                            jax.ShapeDtypeStruct((T, CONV), BF16), jax.ShapeDtypeStruct((T, SSM), F32),
                            jax.ShapeDtypeStruct((T, SSM), BF16), jax.ShapeDtypeStruct((T, D), F32),
                            jax.ShapeDtypeStruct((T, D), F32), jax.ShapeDtypeStruct((T, D), F32)]),
        scratch_shapes=[pltpu.VMEM((tm + 8, CONV), F32)],
        compiler_params=_cp(56, 1),
    )(*_hbm(xb, w_mix4, conv_w, conv_b, w_co4))


def _scan_inplace(bre, bim, ar, ai, T, rev):
    R = SCAN_R
    if rev:
        ai = -ai
    d = 1
    while d < T:
        if d < 8:
            def step(i, _, d=d, ar=ar, ai=ai):
                c = i if rev else T // R - 1 - i
                t0 = pl.multiple_of(c * R, R)
                if rev:
                    wr = bre[pl.ds(t0 + 8, R + 8), :]
                    wi = bim[pl.ds(t0 + 8, R + 8), :]
                    shr = pltpu.roll(wr, R + 8 - d, 0)[0:R]
                    shi = pltpu.roll(wi, R + 8 - d, 0)[0:R]
                    cr, ci = wr[0:R], wi[0:R]
                else:
                    wr = bre[pl.ds(t0, R + 8), :]
                    wi = bim[pl.ds(t0, R + 8), :]
                    shr = pltpu.roll(wr, d, 0)[8:8 + R]
                    shi = pltpu.roll(wi, d, 0)[8:8 + R]
                    cr, ci = wr[8:8 + R], wi[8:8 + R]
                bre[pl.ds(t0 + 8, R), :] = cr + ar * shr - ai * shi
                bim[pl.ds(t0 + 8, R), :] = ci + ar * shi + ai * shr
                return 0

            lax.fori_loop(0, T // R, step, 0)
        else:
            def upd(lo, n, d=d, ar=ar, ai=ai):
                src = lo + d if rev else lo - d
                if not isinstance(lo, int):
                    lo, src = pl.multiple_of(lo + 8, 8), pl.multiple_of(src + 8, 8)
                else:
                    lo, src = lo + 8, src + 8
                cr = bre[pl.ds(lo, n), :]
                ci = bim[pl.ds(lo, n), :]
                shr = bre[pl.ds(src, n), :]
                shi = bim[pl.ds(src, n), :]
                bre[pl.ds(lo, n), :] = cr + ar * shr - ai * shi
                bim[pl.ds(lo, n), :] = ci + ar * shi + ai * shr

            nfull = (T - d) // R if d >= R else T // R - 1

            def step(i, _, upd=upd, d=d):
                if rev:
                    t0 = i * R
                else:
                    t0 = T - (i + 1) * R
                upd(t0, R)
                return 0

            if nfull > 0:
                lax.fori_loop(0, nfull, step, 0)
            if d < R:
                if rev:
                    upd(T - R, R - d)
                else:
                    upd(d, R - d)
        ar, ai = ar * ar - ai * ai, 2.0 * ar * ai
        d *= 2


def _scan_specs(T):
    W = SCAN_W
    lane = pl.BlockSpec((T, W), lambda j: (0, j))
    col = pl.BlockSpec((T, 128), lambda j: (0, j // 2))
    wb = pl.BlockSpec((None, 128, W), lambda j: (j, 0, 0))
    wc = pl.BlockSpec((None, W, 128), lambda j: (j, 0, 0))
    vec = pl.BlockSpec((1, W), lambda j: (0, j))
    return lane, col, wb, wc, vec


def _s5_scan_fwd(su_b, wb_re, wb_im, a_re, a_im, comm=()):
    T = su_b.shape[0]
    W = SCAN_W

    def body(su_ref, wbr_ref, wbi_ref, ar_ref, ai_ref, sr_ref, si_ref, bre, bim):
        zero = jnp.zeros((8, W), F32)
        for buf in (bre, bim):
            buf[pl.ds(0, 8), :] = zero
            buf[pl.ds(T + 8, 8), :] = zero
        su = su_ref[...]
        bre[pl.ds(8, T), :] = _nn(su, wbr_ref[...])
        bim[pl.ds(8, T), :] = _nn(su, wbi_ref[...])
        _scan_inplace(bre, bim, ar_ref[...], ai_ref[...], T, rev=False)
        sr_ref[...] = bre[pl.ds(8, T), :]
        si_ref[...] = bim[pl.ds(8, T), :]

    lane, col, wb, wc, vec = _scan_specs(T)
    return _pcall(
        body, name="s5_scan_fwd", grid=(LANES // W,),
        in_specs=[col, wb, wb, vec, vec],
        out_specs=[lane, lane],
        out_shape=[jax.ShapeDtypeStruct((T, LANES), F32)] * 2,
        scratch=[pltpu.VMEM((T + 16, W), F32)] * 2, comm=comm,
        operands=(su_b, wb_re, wb_im, a_re, a_im))


def _gelu(s):
    th = jnp.tanh(GELU_C * (s + 0.044715 * s * s * s))
    return 0.5 * s * (1.0 + th), th


def _mix_fwd_b(st_re, st_im, wc_re4, wc_im4, su, dvec, w_glu4, g_conv, g_ssm, y_conv, w_mo, x1, g, b, tm, comm=()):
    T = su.shape[0]

    def body(sr_ref, si_ref, wcr_ref, wci_ref, su_ref, d_ref, wg_ref, gc_ref, gs_ref, yc_ref, wmo_ref,
             x_ref, g_ref, b_ref, s_ref, sgb_ref, ga_ref, gb_ref, mb_ref, r_ref, xo_ref, xob_ref):
        srb = sr_ref[...].astype(BF16)
        sib = si_ref[...].astype(BF16)
        ys = [_nn(srb[:, 512 * J:512 * (J + 1)], wcr_ref[J]) + _nn(sib[:, 512 * J:512 * (J + 1)], wci_ref[J])
              for J in range(4)]
        s = jnp.concatenate(ys, axis=1) + d_ref[...] * su_ref[...]
        sg, _ = _gelu(s)
        sgb = sg.astype(BF16)
        ga = jnp.concatenate([_nn(sgb, wg_ref[0]), _nn(sgb, wg_ref[1])], axis=1)
        gb = jnp.concatenate([_nn(sgb, wg_ref[2]), _nn(sgb, wg_ref[3])], axis=1)
        merged = _sig(gc_ref[...]) * yc_ref[...] + _sig(gs_ref[...]) * (ga * _sig(gb))
        mb = merged.astype(BF16)
        r = ALPHA * x_ref[...] + _nn(mb, wmo_ref[...])
        xhat, _ = _ln_stats(r)
        xo = xhat * g_ref[...] + b_ref[...]
        s_ref[...] = s
        sgb_ref[...] = sgb
        ga_ref[...] = ga
        gb_ref[...] = gb
        mb_ref[...] = mb
        r_ref[...] = r
        xo_ref[...] = xo
        xob_ref[...] = xo.astype(BF16)

    def tok(n):
        return pl.BlockSpec((tm, n), lambda i: (i, 0))

    def full(shape):
        return pl.BlockSpec(shape, lambda i: (0,) * len(shape))

    return _pcall(
        body, name="mix_fwd_b", grid=(T // tm,),
        in_specs=[tok(LANES), tok(LANES), full((4, 512, 128)), full((4, 512, 128)), tok(SSM), full((1, SSM)),
                  full((4, SSM, 512)), tok(D), tok(D), tok(D), full((D, D)), tok(D), full((1, D)), full((1, D))],
        out_specs=[tok(SSM), tok(SSM), tok(D), tok(D), tok(D), tok(D), tok(D), tok(D)],
        out_shape=[jax.ShapeDtypeStruct((T, SSM), F32), jax.ShapeDtypeStruct((T, SSM), BF16),
                   jax.ShapeDtypeStruct((T, D), F32), jax.ShapeDtypeStruct((T, D), F32),
                   jax.ShapeDtypeStruct((T, D), BF16), jax.ShapeDtypeStruct((T, D), F32),
                   jax.ShapeDtypeStruct((T, D), F32), jax.ShapeDtypeStruct((T, D), BF16)],
        vmem_mb=56, comm=comm,
        operands=(st_re, st_im, wc_re4, wc_im4, su, dvec, w_glu4, g_conv, g_ssm, y_conv, w_mo, x1, g, b))


def _ple_loss(x3, x3b, p, w_pi4, w_pg, g, b, target, tm):
    T = x3.shape[0]
    PD = p.shape[1]

    def body(x_ref, xb_ref, p_ref, wpi_ref, wpg_ref, g_ref, b_ref, t_ref,
             loss_ref, dx_ref, pb_ref, dpw_ref, dgt_ref, dg_ref, db_ref):
        i = pl.program_id(0)
        pb = p_ref[...].astype(BF16)
        pw = jnp.concatenate([_nn(pb, wpi_ref[k]) for k in range(4)], axis=1)
        gt = _nn(xb_ref[...], wpg_ref[...])
        sg = _sig(gt)
        r = ALPHA * x_ref[...] + pw * sg
        gv = g_ref[...]
        xhat, rstd = _ln_stats(r)
        err = xhat * gv + b_ref[...] - t_ref[...]
        lpart = jnp.zeros((1, 128), F32) + 0.5 * jnp.sum(jnp.mean(err * err, axis=-1, keepdims=True))
        dy = err * (1.0 / D)
        dyg = dy * gv
        m1 = jnp.mean(dyg, axis=-1, keepdims=True)
        m2 = jnp.mean(dyg * xhat, axis=-1, keepdims=True)
        dr = rstd * (dyg - m1 - xhat * m2)
        pg, pbias = _rowsum(dy * xhat), _rowsum(dy)

        @pl.when(i == 0)
        def _():
            loss_ref[...] = lpart
            dg_ref[...] = pg
            db_ref[...] = pbias

        @pl.when(i > 0)
        def _():
            loss_ref[...] += lpart
            dg_ref[...] += pg
            db_ref[...] += pbias

        dgt = (dr * pw * sg * (1.0 - sg)).astype(BF16)
        pb_ref[...] = pb
        dpw_ref[...] = (dr * sg).astype(BF16)
        dgt_ref[...] = dgt
        dx_ref[...] = ALPHA * dr + _nt(dgt, wpg_ref[...])

    def tok(n):
        return pl.BlockSpec((tm, n), lambda i: (i, 0))

    def full(shape):
        return pl.BlockSpec(shape, lambda i: (0,) * len(shape))

    return pl.pallas_call(
        body, name="ple_loss", grid=(T // tm,),
        in_specs=[tok(D), tok(D), tok(PD), full((4, PD, 256)), full((D, D)), full((1, D)), full((1, D)), tok(D)],
        out_specs=[full((1, 128)), tok(D), tok(PD), tok(D), tok(D), full((1, D)), full((1, D))],
        out_shape=_hbm_out([jax.ShapeDtypeStruct((1, 128), F32), jax.ShapeDtypeStruct((T, D), F32),
                            jax.ShapeDtypeStruct((T, PD), BF16), jax.ShapeDtypeStruct((T, D), BF16),
                            jax.ShapeDtypeStruct((T, D), BF16), jax.ShapeDtypeStruct((1, D), F32),
                            jax.ShapeDtypeStruct((1, D), F32)]),
        compiler_params=_cp(48, 1),
    )(*_hbm(x3, x3b, p, w_pi4, w_pg, g, b, target))


def _mix_bwd_b(dy, r2, g, w_mo, g_conv, g_ssm, y_conv, ga, gb, s, su, dvec, w_glu4, wc_re4, wc_im4, tm, comm=()):
    T = dy.shape[0]

    def body(dy_ref, r_ref, g_ref, wmo_ref, gc_ref, gs_ref, yc_ref, ga_ref, gb_ref, s_ref, su_ref, d_ref,
             wg_ref, wcr_ref, wci_ref,
             dres_ref, dmix_ref, dgl_ref, dsb_ref, dud_ref, gsr_ref, gsi_ref, dyc_ref, dp_ref,
             dg_ref, db_ref, dd_ref):
        i = pl.program_id(0)
        dyv = dy_ref[...]
        dr, xhat = _ln_bwd(dyv, r_ref[...], g_ref[...])
        dmix = dr.astype(BF16)
        dmerged = _nt(dmix, wmo_ref[...])
        sc, ss, sgb = _sig(gc_ref[...]), _sig(gs_ref[...]), _sig(gb_ref[...])
        gav = ga_ref[...]
        yssm = gav * sgb
        dgc = dmerged * yc_ref[...] * sc * (1.0 - sc)
        dgss = dmerged * yssm * ss * (1.0 - ss)
        dyssm = dmerged * ss
        dgl = jnp.concatenate([dyssm * sgb, dyssm * gav * sgb * (1.0 - sgb)], axis=1).astype(BF16)
        dsg = (_nt(dgl[:, 0:512], wg_ref[0]) + _nt(dgl[:, 512:1024], wg_ref[1])
               + _nt(dgl[:, 1024:1536], wg_ref[2]) + _nt(dgl[:, 1536:2048], wg_ref[3]))
        sv = s_ref[...]
        _, th = _gelu(sv)
        dgelu = 0.5 * (1.0 + th) + 0.5 * sv * (1.0 - th * th) * GELU_C * (1.0 + 3.0 * 0.044715 * sv * sv)
        ds = dsg * dgelu
        dsb = ds.astype(BF16)
        pg, pb, pd = _rowsum(dyv * xhat), _rowsum(dyv), _rowsum(ds * su_ref[...])

        @pl.when(i == 0)
        def _():
            dg_ref[...] = pg
            db_ref[...] = pb
            dd_ref[...] = pd

        @pl.when(i > 0)
        def _():
            dg_ref[...] += pg
            db_ref[...] += pb
            dd_ref[...] += pd

        dres_ref[...] = ALPHA * dr
        dmix_ref[...] = dmix
        dgl_ref[...] = dgl
        dsb_ref[...] = dsb
        dud_ref[...] = ds * d_ref[...]
        for J in range(4):
            gsr_ref[:, 512 * J:512 * (J + 1)] = _nt(dsb[:, 128 * J:128 * (J + 1)], wcr_ref[J])
            gsi_ref[:, 512 * J:512 * (J + 1)] = _nt(dsb[:, 128 * J:128 * (J + 1)], wci_ref[J])
        dyc_ref[...] = (dmerged * sc).astype(BF16)
        dp_ref[:, 0:D] = dgc.astype(BF16)
        dp_ref[:, D:2 * D] = dgss.astype(BF16)

    def tok(n):
        return pl.BlockSpec((tm, n), lambda i: (i, 0))

    def full(shape):
        return pl.BlockSpec(shape, lambda i: (0,) * len(shape))

    return _pcall(
        body, name="mix_bwd_b", grid=(T // tm,),
        in_specs=[tok(D), tok(D), full((1, D)), full((D, D)), tok(D), tok(D), tok(D), tok(D), tok(D),
                  tok(SSM), tok(SSM), full((1, SSM)), full((4, SSM, 512)), full((4, 512, 128)), full((4, 512, 128))],
        out_specs=[tok(D), tok(D), tok(2 * D), tok(SSM), tok(SSM), tok(LANES), tok(LANES), tok(D),
                   pl.BlockSpec((tm, 2 * D), lambda i: (i, 1)), full((1, D)), full((1, D)), full((1, SSM))],
        out_shape=[jax.ShapeDtypeStruct((T, D), F32), jax.ShapeDtypeStruct((T, D), BF16),
                   jax.ShapeDtypeStruct((T, 2 * D), BF16), jax.ShapeDtypeStruct((T, SSM), BF16),
                   jax.ShapeDtypeStruct((T, SSM), F32), jax.ShapeDtypeStruct((T, LANES), F32),
                   jax.ShapeDtypeStruct((T, LANES), F32), jax.ShapeDtypeStruct((T, D), BF16),
                   jax.ShapeDtypeStruct((T, 4 * D), BF16), jax.ShapeDtypeStruct((1, D), F32),
                   jax.ShapeDtypeStruct((1, D), F32), jax.ShapeDtypeStruct((1, SSM), F32)],
        vmem_mb=56, comm=comm,
        operands=(dy, r2, g, w_mo, g_conv, g_ssm, y_conv, ga, gb, s, su, dvec, w_glu4, wc_re4, wc_im4))


def _s5_scan_bwd(gs_re, gs_im, st_re, st_im, su_b, ds_b, wb_re, wb_im, a_re, a_im, comm=()):
    T = su_b.shape[0]
    W = SCAN_W
    R = SCAN_R

    def body(gr_ref, gi_ref, sr_ref, si_ref, su_ref, ds_ref, wbr_ref, wbi_ref, ar_ref, ai_ref,
             dsu_ref, dwbr_ref, dwbi_ref, dwcr_ref, dwci_ref, dar_ref, dai_ref, gre, gim):
        j = pl.program_id(0)
        zero = jnp.zeros((8, W), F32)
        for buf in (gre, gim):
            buf[pl.ds(0, 8), :] = zero
            buf[pl.ds(T + 8, 8), :] = zero
        gre[pl.ds(8, T), :] = gr_ref[...]
        gim[pl.ds(8, T), :] = gi_ref[...]
        _scan_inplace(gre, gim, ar_ref[...], ai_ref[...], T, rev=True)
        grb = gre[pl.ds(8, T), :].astype(BF16)
        gib = gim[pl.ds(8, T), :].astype(BF16)
        part = _nt(grb, wbr_ref[...]) + _nt(gib, wbi_ref[...])

        @pl.when(j % 2 == 0)
        def _():
            dsu_ref[...] = part

        @pl.when(j % 2 == 1)
        def _():
            dsu_ref[...] += part

        su = su_ref[...]
        dwbr_ref[...] = _tn(su, grb)
        dwbi_ref[...] = _tn(su, gib)
        dsv = ds_ref[...]
        dwcr_ref[...] = _tn(sr_ref[...].astype(BF16), dsv)
        dwci_ref[...] = _tn(si_ref[...].astype(BF16), dsv)
        dar = jnp.zeros((1, W), F32)
        dai = jnp.zeros((1, W), F32)
        for c in range(T // R):
            xr = sr_ref[pl.ds(c * R, R), :]
            xi = si_ref[pl.ds(c * R, R), :]
            g1r = gre[pl.ds(c * R + 9, R), :]
            g1i = gim[pl.ds(c * R + 9, R), :]
            dar = dar + _rowsum(g1r * xr + g1i * xi)
            dai = dai + _rowsum(g1i * xr - g1r * xi)
        dar_ref[...] = dar
        dai_ref[...] = dai

    lane, col, wb, wc, vec = _scan_specs(T)
    return _pcall(
        body, name="s5_scan_bwd", grid=(LANES // W,),
        in_specs=[lane, lane, lane, lane, col, col, wb, wb, vec, vec],
        out_specs=[col, wb, wb, wc, wc, vec, vec],
        out_shape=[jax.ShapeDtypeStruct((T, SSM), F32),
                   jax.ShapeDtypeStruct((LANES // W, 128, W), F32), jax.ShapeDtypeStruct((LANES // W, 128, W), F32),
                   jax.ShapeDtypeStruct((LANES // W, W, 128), F32), jax.ShapeDtypeStruct((LANES // W, W, 128), F32),
                   jax.ShapeDtypeStruct((1, LANES), F32), jax.ShapeDtypeStruct((1, LANES), F32)],
        scratch=[pltpu.VMEM((T + 16, W), F32)] * 2, vmem_mb=56, comm=comm,
        operands=(gs_re, gs_im, st_re, st_im, su_b, ds_b, wb_re, wb_im, a_re, a_im))


def _mix_bwd_a(dyc_b, w_co4, pc, z_b, conv_w, dsu_ssm, du_dir, dproj, dres, w_mix4, tm, comm=()):
    T = dres.shape[0]
    nt = T // tm

    def body(dyc_ref, wco_ref, pc_ref, halo_ref, z_ref, cw_ref, dsu_ref, dud_ref, dpin_ref, dres_ref, w_ref,
             dp_ref, dx_ref, dcw_ref, dcb_ref, dzbuf, qbuf):
        i = pl.program_id(0)
        ii = nt - 1 - i

        @pl.when(i == 0)
        def _():
            dzbuf[pl.ds(tm, 8), :] = jnp.zeros((8, CONV), F32)

        dyc = dyc_ref[...]
        dyin = (_nt(dyc[:, 0:256], wco_ref[0]) + _nt(dyc[:, 256:512], wco_ref[1])
                + _nt(dyc[:, 512:768], wco_ref[2]) + _nt(dyc[:, 768:1024], wco_ref[3]))
        cbv = pc_ref[:, 0:CONV].astype(F32)
        ccv = pc_ref[:, CONV:2 * CONV].astype(F32)
        chv = pc_ref[:, 2 * CONV:3 * CONV].astype(F32)
        dcbv = dyin * z_ref[...].astype(F32)
        dz = dyin * cbv
        dzbuf[pl.ds(0, tm), :] = dz
        cw = cw_ref[...]
        dq = cw[2:3] * dz + cw[1:2] * dzbuf[pl.ds(1, tm), :] + cw[0:1] * dzbuf[pl.ds(2, tm), :]
        dzbuf[pl.ds(tm, 8), :] = dz[0:8]
        q = ccv * chv
        hq = halo_ref[:, CONV:2 * CONV].astype(F32) * halo_ref[:, 2 * CONV:3 * CONV].astype(F32)
        qbuf[pl.ds(0, 8), :] = jnp.where(ii > 0, hq, jnp.zeros_like(hq))
        qbuf[pl.ds(8, tm), :] = q
        pw = jnp.concatenate([_rowsum(dz * qbuf[pl.ds(6, tm), :]), _rowsum(dz * qbuf[pl.ds(7, tm), :]),
                              _rowsum(dz * q), jnp.zeros((5, CONV), F32)], axis=0)
        pbias = _rowsum(dz)

        @pl.when(i == 0)
        def _():
            dcw_ref[...] = pw
            dcb_ref[...] = pbias

        @pl.when(i > 0)
        def _():
            dcw_ref[...] += pw
            dcb_ref[...] += pbias

        dp0 = jnp.concatenate([dcbv, dq * chv], axis=1).astype(BF16)
        dp1 = jnp.concatenate([dq * ccv, dsu_ref[...] + dud_ref[...]], axis=1).astype(BF16)
        dp_ref[:, 0:D] = dp0
        dp_ref[:, D:2 * D] = dp1
        dx_ref[...] = (dres_ref[...] + _nt(dp0, w_ref[0]) + _nt(dp1, w_ref[1])
                       + _nt(dpin_ref[:, 0:D], w_ref[2]) + _nt(dpin_ref[:, D:2 * D], w_ref[3]))

    def tok(n):
        return pl.BlockSpec((tm, n), lambda i: (nt - 1 - i, 0))

    def full(shape):
        return pl.BlockSpec(shape, lambda i: (0,) * len(shape))

    halo = pl.BlockSpec((8, 3 * CONV), lambda i: (jnp.maximum((nt - 1 - i) * (tm // 8) - 1, 0), 0))
    return _pcall(
        body, name="mix_bwd_a", grid=(nt,),
        in_specs=[tok(D), full((4, CONV, 256)), tok(3 * CONV), halo, tok(CONV), full((3, CONV)),
                  tok(SSM), tok(SSM), pl.BlockSpec((tm, 2 * D), lambda i: (nt - 1 - i, 1)), tok(D),
                  full((4, D, D))],
        out_specs=[pl.BlockSpec((tm, 2 * D), lambda i: (nt - 1 - i, 0)), tok(D), full((8, CONV)), full((1, CONV))],
        out_shape=[jax.ShapeDtypeStruct((T, 4 * D), BF16), jax.ShapeDtypeStruct((T, D), F32),
                   jax.ShapeDtypeStruct((8, CONV), F32), jax.ShapeDtypeStruct((1, CONV), F32)],
        scratch=[pltpu.VMEM((tm + 8, CONV), F32), pltpu.VMEM((tm + 8, CONV), F32)],
        aliases={8: 0}, vmem_mb=56, comm=comm,
        operands=(dyc_b, w_co4, pc, pc, z_b, conv_w, dsu_ssm, du_dir, dproj, dres, w_mix4))


def _zoh(lam_re, lam_im, log_step, b_re, b_im):
    dt = jnp.exp(log_step)[:, None]
    mag = jnp.exp(lam_re * dt)
    abr, abi = mag * jnp.cos(lam_im * dt), mag * jnp.sin(lam_im * dt)
    nr, ni = abr - 1.0, abi
    den = lam_re * lam_re + lam_im * lam_im
    cr = (nr * lam_re + ni * lam_im) / den
    ci = (ni * lam_re - nr * lam_im) / den
    bbr = cr[..., None] * b_re - ci[..., None] * b_im
    bbi = cr[..., None] * b_im + ci[..., None] * b_re
    return abr, abi, bbr, bbi


def _wb_blocks(bb):
    eye = jnp.eye(GROUPS, dtype=F32)
    full = jnp.einsum("gni,gh->gihn", bb, eye).reshape(4, 128, 8, SCAN_W)
    return jnp.stack([full[j // 2, :, j, :] for j in range(8)]).astype(BF16)


def _wc_blocks(cc):
    eye = jnp.eye(GROUPS, dtype=F32)
    full = jnp.einsum("gin,gh->gnhi", cc, eye).reshape(4, 512, 4, 128)
    return jnp.stack([full[J, :, J, :] for J in range(4)]).astype(BF16)


_G = np.arange(GROUPS)


def _wb_diag(dwb8):
    d5 = dwb8.reshape(8, 8, 16, 4, 64)
    return d5[_G // 4, _G % 8, :, _G % 4, :].transpose(0, 2, 1)


def _wc_diag(dwc8):
    d5 = dwc8.reshape(8, 4, 64, 8, 16)
    return d5[_G // 4, _G % 4, :, _G % 8, :].transpose(0, 2, 1)


def _where():
    x, y, c = lax.axis_index("x"), lax.axis_index("y"), lax.axis_index("c")
    return x, y, c, 2 * x + y


def _chip_dev(k, c):
    return (k // 2, k % 2, c)


def _slot_cast(meidx, w, dtype, name):
    R, C = w.shape
    tr = _row_tile(R)

    def body(m_ref, w_ref, o_ref):
        o_ref[...] = w_ref[...].astype(dtype)

    gs = pltpu.PrefetchScalarGridSpec(
        num_scalar_prefetch=1, grid=(R // tr,),
        in_specs=[pl.BlockSpec((tr, C), lambda i, m: (i, 0))],
        out_specs=pl.BlockSpec((None, tr, C), lambda i, m: (m[0], i, 0)))
    return pl.pallas_call(
        body, name=name, grid_spec=gs, out_shape=_hbm_out(jax.ShapeDtypeStruct((4, R, C), dtype)),
        compiler_params=_cp(32, 1),
    )(meidx, *_hbm(w))


def _gather_payload(bufs):
    n = len(bufs)

    def half(ref, w, k, cc):
        h = bufs[w].shape[1] // 2
        return ref.at[k, pl.ds(cc * h, h)]

    def ici(ins, outs, sems, w, s):
        x, y, c, me = _where()
        k = (me + 1 + s) % 4
        return pltpu.make_async_remote_copy(
            src_ref=half(ins[w], w, me, c), dst_ref=half(outs[w], w, me, c), send_sem=sems[0].at[3 * w + s],
            recv_sem=sems[1].at[3 * w + s], device_id=_chip_dev(k, c), device_id_type=MESH)

    def landed(outs, sems, w, s):
        x, y, c, me = _where()
        j = (me + 3 - s) % 4
        return pltpu.make_async_remote_copy(
            src_ref=half(outs[w], w, j, c), dst_ref=half(outs[w], w, j, c), send_sem=sems[0].at[3 * w + s],
            recv_sem=sems[1].at[3 * w + s], device_id=(x, y, 1 - c), device_id_type=MESH)

    def passed(outs, sems, w, s, cc):
        x, y, c, me = _where()
        j = (me + 3 - s) % 4
        return pltpu.make_async_remote_copy(
            src_ref=half(outs[w], w, j, cc), dst_ref=half(outs[w], w, j, cc), send_sem=sems[2].at[3 * w + s],
            recv_sem=sems[3].at[3 * w + s], device_id=(x, y, 1 - c), device_id_type=MESH)

    pairs = [(w, s) for w in range(n) for s in range(3)]

    def start(ins, outs, sems):
        for w, s in pairs:
            ici(ins, outs, sems, w, s).start()

    def finish(ins, outs, sems):
        _, _, c, _ = _where()
        for w, s in pairs:
            landed(outs, sems, w, s).wait_recv()
            passed(outs, sems, w, s, c).start()
        for w, s in pairs:
            passed(outs, sems, w, s, 1 - c).wait_recv()
        for w, s in pairs:
            ici(ins, outs, sems, w, s).wait_send()
            passed(outs, sems, w, s, c).wait_send()

    return _Payload(bufs, [jax.ShapeDtypeStruct(b.shape, b.dtype) for b in bufs], {w: w for w in range(n)},
                    [pltpu.SemaphoreType.DMA((3 * n,))] * 4, start, finish)


def _sym_payload(operands, outs, copies, n_copies):
    def start(ins, outs_, sems):
        for cp in copies(ins, outs_, sems[0], sems[1]):
            cp.start()

    def finish(ins, outs_, sems):
        for cp in copies(ins, outs_, sems[0], sems[1]):
            cp.wait()

    return _Payload(operands, outs, {}, [pltpu.SemaphoreType.DMA((n_copies,))] * 2, start, finish)


def _swap_payload(g4s):
    def copies(ins, outs, ss, rs):
        x, y, c, me = _where()
        cps = []
        for w, g in enumerate(g4s):
            h = g.shape[1] // 2
            cps.append(pltpu.make_async_remote_copy(
                src_ref=ins[w].at[:, pl.ds((1 - c) * h, h)], dst_ref=outs[w], send_sem=ss.at[w],
                recv_sem=rs.at[w], device_id=(x, y, 1 - c), device_id_type=MESH))
        return cps

    outs = [jax.ShapeDtypeStruct((4, g.shape[1] // 2, g.shape[2]), g.dtype) for g in g4s]
    return _sym_payload(g4s, outs, copies, len(g4s))


def _exchange_payload(pbs):
    def copies(ins, outs, ss, rs):
        x, y, c, me = _where()
        cps = []
        for w in range(len(pbs)):
            for s in range(3):
                k = (me + 1 + s) % 4
                cps.append(pltpu.make_async_remote_copy(
                    src_ref=ins[w].at[k], dst_ref=outs[w].at[2 - s], send_sem=ss.at[3 * w + s],
                    recv_sem=rs.at[3 * w + s], device_id=_chip_dev(k, c), device_id_type=MESH))
        return cps

    outs = [jax.ShapeDtypeStruct((3,) + p.shape[1:], p.dtype) for p in pbs]
    return _sym_payload(pbs, outs, copies, 3 * len(pbs))


HBM_REF = pl.BlockSpec(memory_space=pltpu.HBM)
SEM_REF = pl.BlockSpec(memory_space=pltpu.SEMAPHORE)
DATAFLOW = pltpu.SideEffectType.DATAFLOW_SIDE_EFFECTING


def _exchange_copies(src_refs, land_refs, sems):
    x, y, c, me = _where()
    cps = []
    for w in range(len(src_refs)):
        for s in range(3):
            k = (me + 1 + s) % 4
            i = 3 * w + s
            cps.append(pltpu.make_async_remote_copy(
                src_ref=src_refs[w].at[k], dst_ref=land_refs[w].at[2 - s], send_sem=sems[2 * i],
                recv_sem=sems[2 * i + 1], device_id=_chip_dev(k, c), device_id_type=MESH))
    return cps


def _exchange_start(pbs, name):
    n = len(pbs)
    lands = [lax.empty((3,) + p.shape[1:], p.dtype) for p in pbs]

    def body(*refs):
        for cp in _exchange_copies(refs[:n], refs[n:2 * n], refs[2 * n:2 * n + 6 * n]):
            cp.start()
        token = refs[-1]
        token[...] = jnp.zeros_like(token)

    res = pl.pallas_call(
        body, name=name,
        in_specs=[HBM_REF] * (2 * n),
        out_specs=[SEM_REF] * (6 * n) + [HBM_REF] * (2 * n) + [VMEM_FULL],
        out_shape=([pltpu.SemaphoreType.DMA(())] * (6 * n) + _hbm_out(pbs) + _hbm_out(lands)
                   + [jax.ShapeDtypeStruct((8, 128), F32)]),
        input_output_aliases={i: 6 * n + i for i in range(2 * n)},
        compiler_params=pltpu.CompilerParams(has_side_effects=DATAFLOW),
    )(*_hbm(*pbs, *lands))
    return list(res[:6 * n]), list(res[6 * n:7 * n]), list(res[7 * n:8 * n]), res[-1]


def _exchange_wait(sems, srcs, lands, after, name):
    n = len(srcs)

    def body(*refs):
        for cp in _exchange_copies(refs[:n], refs[n:2 * n], refs[2 * n:2 * n + 6 * n]):
            cp.wait_send()
            cp.wait_recv()

    res = pl.pallas_call(
        body, name=name,
        in_specs=[HBM_REF] * (2 * n) + [SEM_REF] * (6 * n) + [ANY] * len(after),
        out_specs=[HBM_REF] * (2 * n), out_shape=_hbm_out(srcs) + _hbm_out(lands),
        input_output_aliases={i: i for i in range(2 * n)},
        compiler_params=pltpu.CompilerParams(has_side_effects=DATAFLOW),
    )(*srcs, *lands, *sems, *after)
    return list(res[n:])


def _join_payload(halves):
    def copies(ins, outs, ss, rs):
        x, y, c, me = _where()
        return [pltpu.make_async_remote_copy(
            src_ref=ins[w], dst_ref=outs[w], send_sem=ss.at[w], recv_sem=rs.at[w],
            device_id=(x, y, 1 - c), device_id_type=MESH) for w in range(len(halves))]

    outs = [jax.ShapeDtypeStruct(a.shape, a.dtype) for a in halves]
    return _sym_payload(halves, outs, copies, len(halves))


def _allgather_payload(v):
    def copies(ins, outs, ss, rs):
        x, y, c, me = _where()
        lin = 4 * x + 2 * y + c
        cps = []
        cps = [pltpu.make_async_copy(ins[0], outs[0].at[lin], ss.at[0])]
        for o in range(1, 8):
            t = (lin + o) % 8
            cps.append(pltpu.make_async_remote_copy(
                src_ref=ins[0], dst_ref=outs[0].at[lin], send_sem=ss.at[o], recv_sem=rs.at[o],
                device_id=(t // 4, (t // 2) % 2, t % 2), device_id_type=MESH))
        return cps

    return _sym_payload([v], [jax.ShapeDtypeStruct((8,) + v.shape, v.dtype)], copies, 8)


def _sum8(buf, token):
    _, P, C = buf.shape

    def body(b_ref, t_ref, o_ref):
        acc = b_ref[0]
        for d in range(1, 8):
            acc = acc + b_ref[d]
        o_ref[...] = acc

    return pl.pallas_call(
        body, name="sum8", in_specs=[VMEM_FULL, VMEM_FULL], out_specs=VMEM_FULL,
        out_shape=jax.ShapeDtypeStruct((P, C), F32),
        compiler_params=pltpu.CompilerParams(vmem_limit_bytes=32 << 20),
    )(buf, token)


def _row_tile(h):
    for t in (256, 176, 128, 64, 32, 16, 8):
        if h % t == 0:
            return t
    raise ValueError(h)


def _pair_sum(cidx, g4, got, name):
    _, R, C = g4.shape
    h = R // 2
    th = _row_tile(h)

    def body(c_ref, a_ref, b_ref, o_ref, ob_ref):
        sm = a_ref[...] + b_ref[...]
        o_ref[...] = sm
        ob_ref[...] = sm.astype(BF16)

    blk = pl.BlockSpec((None, th, C), lambda k, i, c: (k, i, 0))
    gs = pltpu.PrefetchScalarGridSpec(
        num_scalar_prefetch=1, grid=(4, h // th),
        in_specs=[pl.BlockSpec((None, None, th, C), lambda k, i, c: (k, c[0], i, 0)), blk],
        out_specs=[blk, blk])
    return pl.pallas_call(
        body, name=name, grid_spec=gs,
        out_shape=_hbm_out([jax.ShapeDtypeStruct((4, h, C), F32), jax.ShapeDtypeStruct((4, h, C), BF16)]),
        compiler_params=_cp(32, 2),
    )(cidx, *_hbm(g4.reshape(4, 2, h, C), got))


def _chip_sum(meidx, p32, got, name):
    _, h, C = p32.shape
    th = _row_tile(h)

    def body(m_ref, a_ref, b_ref, o_ref):
        o_ref[...] = ((a_ref[...] + b_ref[0].astype(F32)) + b_ref[1].astype(F32)) + b_ref[2].astype(F32)

    gs = pltpu.PrefetchScalarGridSpec(
        num_scalar_prefetch=1, grid=(h // th,),
        in_specs=[pl.BlockSpec((None, th, C), lambda i, m: (m[0], i, 0)),
                  pl.BlockSpec((3, th, C), lambda i, m: (0, i, 0))],
        out_specs=pl.BlockSpec((th, C), lambda i, m: (i, 0)))
    return pl.pallas_call(
        body, name=name, grid_spec=gs, out_shape=_hbm_out(jax.ShapeDtypeStruct((h, C), F32)),
        compiler_params=_cp(32, 1),
    )(meidx, *_hbm(p32, got))


def _adamw_math(w, g, m, v):
    m2 = B1 * m + (1.0 - B1) * g
    v2 = B2 * v + (1.0 - B2) * (g * g)
    m_hat = m2 / (1.0 - B1 ** STEP)
    v_hat = v2 / (1.0 - B2 ** STEP)
    delta = -LR * (m_hat / (jnp.sqrt(v_hat) + EPS) + WD * w)
    return delta, m2, v2


def _adamw_pair(cidx, w, mine, theirs, m, v, token, name):
    R, C = w.shape
    h = R // 2
    tr = _row_tile(h)
    nh = h // tr

    def body(c_ref, w_ref, a_ref, b_ref, m_ref, v_ref, t_ref, g_ref, d_ref, mo_ref, vo_ref):
        own = (pl.program_id(0) // nh) == c_ref[0]
        g = jnp.where(own, a_ref[...], b_ref[...])
        d, m2, v2 = _adamw_math(w_ref[...], g, m_ref[...], v_ref[...])
        g_ref[...] = g
        d_ref[...] = d
        mo_ref[...] = m2
        vo_ref[...] = v2

    blk = pl.BlockSpec((tr, C), lambda i, c: (i, 0))
    hblk = pl.BlockSpec((tr, C), lambda i, c: (i % nh, 0))
    gs = pltpu.PrefetchScalarGridSpec(
        num_scalar_prefetch=1, grid=(R // tr,),
        in_specs=[blk, hblk, hblk, blk, blk, pl.BlockSpec((8, 128), lambda i, c: (0, 0))], out_specs=[blk] * 4)
    return pl.pallas_call(
        body, name=name, grid_spec=gs, out_shape=_hbm_out([jax.ShapeDtypeStruct((R, C), F32)] * 4),
        compiler_params=_cp(32, 1),
    )(cidx, *_hbm(w, mine, theirs, m, v), token)


def _adamw(w, g, m, v, name):
    R, C = w.shape
    tr = _row_tile(R)

    def body(w_ref, g_ref, m_ref, v_ref, d_ref, mo_ref, vo_ref):
        d, m2, v2 = _adamw_math(w_ref[...], g_ref[...], m_ref[...], v_ref[...])
        d_ref[...] = d
        mo_ref[...] = m2
        vo_ref[...] = v2

    blk = pl.BlockSpec((tr, C), lambda i: (i, 0))
    return pl.pallas_call(
        body, name=name, grid=(R // tr,), in_specs=[blk] * 4, out_specs=[blk] * 3,
        out_shape=_hbm_out([jax.ShapeDtypeStruct((R, C), F32)] * 3),
        compiler_params=_cp(32, 1),
    )(*_hbm(w, g, m, v))


def _pack(arrs):
    flat = jnp.concatenate([a.reshape(-1).astype(F32) for a in arrs])
    rows = -(-flat.shape[0] // 1024)
    rows = -(-rows // 8) * 8
    return jnp.pad(flat, (0, rows * 1024 - flat.shape[0])).reshape(rows, 1024)


def _unpack(packed, shapes):
    flat = packed.reshape(-1)
    out, off = [], 0
    for s in shapes:
        n = math.prod(s)
        out.append(flat[off:off + n].reshape(s))
        off += n
    return out


BIG = ["ffn1_w_in", "ffn1_w_out", "mix_w_in", "conv_w_out", "ssm_w_glu", "mix_w_out",
       "ffn2_w_in", "ffn2_w_out", "ple_w_in", "ple_w_gate"]
SMALL = ["ln1_g", "ln1_b", "conv_w", "conv_b", "ssm_lam_re", "ssm_lam_im", "ssm_log_step", "ssm_b_re", "ssm_b_im",
         "ssm_c_re", "ssm_c_im", "ssm_d", "ln2_g", "ln2_b", "ln3_g", "ln3_b", "ln4_g", "ln4_b"]
WEIGHTS = ["ffn1_w_in", "ffn1_w_out", "ln1_g", "ln1_b", "mix_w_in", "conv_w", "conv_b", "conv_w_out",
           "ssm_lam_re", "ssm_lam_im", "ssm_log_step", "ssm_b_re", "ssm_b_im", "ssm_c_re", "ssm_c_im", "ssm_d",
           "ssm_w_glu", "mix_w_out", "ln2_g", "ln2_b", "ffn2_w_in", "ffn2_w_out", "ln3_g", "ln3_b",
           "ple_w_in", "ple_w_gate", "ln4_g", "ln4_b"]


class _NoComm:
    def __init__(self, W):
        self.W, self.G, self.raw = dict(W), {}, None

    def carry(self, name):
        return ()

    def landed(self, name, got):
        pass

    def grad(self, name, g4):
        self.G[name] = g4

    def small(self, raw):
        self.raw = raw


def _local_step(x, p, target, sp, sched, tm_ffn, tm_mix):
    W = sched.W
    abr, abi, bbr, bbi = _zoh(sp["ssm_lam_re"], sp["ssm_lam_im"], sp["ssm_log_step"], sp["ssm_b_re"], sp["ssm_b_im"])
    wb_re, wb_im = _wb_blocks(bbr), _wb_blocks(bbi)
    wc_re4, wc_im4 = _wc_blocks(sp["ssm_c_re"]), _wc_blocks(-sp["ssm_c_im"])
    a_re, a_im = abr.reshape(1, LANES), abi.reshape(1, LANES)
    dvec = sp["ssm_d"].reshape(1, SSM)

    def run(fn, name, *args, **kw):
        outs, got = fn(*args, comm=sched.carry(name), **kw)
        sched.landed(name, got)
        return outs

    def dw(name, wname, a, b, tk, tn, shape4, shard_cols=None, interleaved=False):
        out, got = _mm_tn(a, b, tk, tn, name, shard_cols=shard_cols, interleaved=interleaved,
                          comm=sched.carry(name))
        sched.landed(name, got)
        sched.grad(wname, out.reshape(shape4))

    xb = x.astype(BF16)
    h1, r1, x1, x1b = run(_ffn_fwd, "ffn1_fwd", x, xb, W["ffn1_w_in"], W["ffn1_w_out"].reshape(2, FFH, D),
                          sp["ln1_g"], sp["ln1_b"], tm_ffn, "ffn1_fwd")
    conv_w = W["conv_w"][:, 0:3, :].transpose(1, 0, 2).reshape(3, CONV)
    pc, z_b, yin_b, su, su_b, g_conv, g_ssm, y_conv = _mix_fwd_a(
        x1b, W["mix_w_in"], conv_w, sp["conv_b"], W["conv_w_out"], tm_mix)
    st_re, st_im = run(_s5_scan_fwd, "s5_scan_fwd", su_b, wb_re, wb_im, a_re, a_im)
    w_mo = W["mix_w_out"].reshape(D, D)
    s, sg_b, ga, gb, merged_b, r2, x2, x2b = run(
        _mix_fwd_b, "mix_fwd_b", st_re, st_im, wc_re4, wc_im4, su, dvec, W["ssm_w_glu"], g_conv, g_ssm, y_conv,
        w_mo, x1, sp["ln2_g"], sp["ln2_b"], tm_mix)
    w2o2 = W["ffn2_w_out"].reshape(2, FFH, D)
    h2, r3, x3, x3b = run(_ffn_fwd, "ffn2_fwd", x2, x2b, W["ffn2_w_in"], w2o2, sp["ln3_g"], sp["ln3_b"], tm_ffn,
                          "ffn2_fwd")
    loss_part, dx3, p_b, dpw_b, dgt_b, dg4, db4 = _ple_loss(
        x3, x3b, p, W["ple_w_in"], W["ple_w_gate"].reshape(D, D), sp["ln4_g"], sp["ln4_b"], target, tm_mix)

    dw("dw_ple_gate", "ple_w_gate", x3b, dgt_b, 512, 1024, (4, 256, D))
    dw("dw_ple_in", "ple_w_in", p_b, dpw_b, 256, 256, (4, 256, 256), shard_cols=256)
    dx2, dh2, a2_b, df2_b, dg3, db3 = run(_ffn_bwd, "ffn2_bwd", dx3, r3, sp["ln3_g"], h2, W["ffn2_w_in"], w2o2,
                                          tm_ffn, "ffn2_bwd")
    dw("dw_ffn2_in", "ffn2_w_in", x2b, dh2, 512, FFH, (4, D, FFH), shard_cols=FFH, interleaved=True)
    dw("dw_ffn2_out", "ffn2_w_out", a2_b, df2_b, FFH, 1024, (4, FF // 4, D))
    (dres, dmix_b, dgl_b, ds_b, du_dir, gs_re, gs_im, dyc_b, dproj, dg2, db2, dd) = run(
        _mix_bwd_b, "mix_bwd_b", dx2, r2, sp["ln2_g"], w_mo, g_conv, g_ssm, y_conv, ga, gb, s, su, dvec,
        W["ssm_w_glu"], wc_re4, wc_im4, tm_mix)
    dw("dw_mix_out", "mix_w_out", merged_b, dmix_b, 512, 1024, (4, 256, D))
    dw("dw_glu", "ssm_w_glu", sg_b, dgl_b, 512, 512, (4, SSM, 512), shard_cols=512)
    dsu_ssm, dwb_re, dwb_im, dwc_re, dwc_im, da_re, da_im = run(
        _s5_scan_bwd, "s5_scan_bwd", gs_re, gs_im, st_re, st_im, su_b, ds_b, wb_re, wb_im, a_re, a_im)
    dw("dw_conv_out", "conv_w_out", yin_b, dyc_b, 512, 256, (4, CONV, 256), shard_cols=256)
    dproj, dx1, dcw8, dcb = run(_mix_bwd_a, "mix_bwd_a", dyc_b, W["conv_w_out"], pc, z_b, conv_w, dsu_ssm,
                                du_dir, dproj, dres, W["mix_w_in"], tm_mix)
    dw("dw_mix_in", "mix_w_in", x1b, dproj, 512, 1024, (4, D, D), shard_cols=1024)
    dx0, dh1, a1_b, df1_b, dg1, db1 = run(_ffn_bwd, "ffn1_bwd", dx1, r1, sp["ln1_g"], h1, W["ffn1_w_in"],
                                          W["ffn1_w_out"].reshape(2, FFH, D), tm_ffn, "ffn1_bwd")
    sched.small(dict(
        ln1_g=dg1, ln1_b=db1, ln2_g=dg2, ln2_b=db2, ln3_g=dg3, ln3_b=db3, ln4_g=dg4, ln4_b=db4,
        conv_w=dcw8[0:3], conv_b=dcb,
        a_re=da_re.reshape(GROUPS, STATE), a_im=da_im.reshape(GROUPS, STATE),
        bb_re=_wb_diag(dwb_re), bb_im=_wb_diag(dwb_im),
        ssm_c_re=_wc_diag(dwc_re), ssm_c_im=-_wc_diag(dwc_im), ssm_d=dd.reshape(GROUPS, 16),
        loss=loss_part[0:1, 0]))
    dw("dw_ffn1_in", "ffn1_w_in", xb, dh1, 512, FFH, (4, D, FFH), shard_cols=FFH, interleaved=True)
    dw("dw_ffn1_out", "ffn1_w_out", a1_b, df1_b, FFH, 1024, (4, FF // 4, D))
    return loss_part[0, 0], dx0


RAW_ORDER = ["ln1_g", "ln1_b", "ln2_g", "ln2_b", "ln3_g", "ln3_b", "ln4_g", "ln4_b", "conv_w", "conv_b",
             "a_re", "a_im", "bb_re", "bb_im", "ssm_c_re", "ssm_c_im", "ssm_d", "loss"]

GATHER_FIRST = ["ffn1_w_in", "ffn1_w_out"]
GATHER_AT = {"ffn1_fwd": ["mix_w_in", "conv_w_out", "conv_w", "ssm_w_glu", "mix_w_out"],
             "s5_scan_fwd": ["ffn2_w_in"], "mix_fwd_b": ["ffn2_w_out"], "ffn2_fwd": ["ple_w_in", "ple_w_gate"]}
REDUCE_GROUP = {"ple": ["ple_w_gate", "ple_w_in"], "ffn2": ["ffn2_w_in", "ffn2_w_out"],
                "mix": ["mix_w_out", "ssm_w_glu", "conv_w_out", "mix_w_in"], "ffn1": ["ffn1_w_in", "ffn1_w_out"]}
REDUCE_AT = {"ffn2_bwd": [("swap", "ple")], "dw_ffn2_in": [("exchange", "ple")],
             "mix_bwd_b": [("swap", "ffn2"), ("join", "ple")], "s5_scan_bwd": [("exchange", "ffn2")],
             "mix_bwd_a": [("join", "ffn2")], "ffn1_bwd": [("swap", "mix")],
             "dw_ffn1_in": [("exchange", "mix"), ("small", None)]}
LAST_GROUP = "ffn1"


class _Sched:
    def __init__(self, bufs, cidx, meidx):
        self.bufs, self.cidx, self.meidx = bufs, cidx, meidx
        self.W, self.G, self.raw, self.small_buf = {}, {}, None, None
        self.got1, self.p32, self.pbf, self.got2, self.half, self.theirs = {}, {}, {}, {}, {}, {}
        self._open = []
        self._standalone("gather_ffn1", [("gather", GATHER_FIRST)])

    def _payload(self, stage, key):
        if stage == "gather":
            return _gather_payload([self.bufs[n] for n in key])
        if stage == "small":
            return _allgather_payload(_pack([self.raw[k] for k in RAW_ORDER]))
        names = REDUCE_GROUP[key]
        if stage == "swap":
            return _swap_payload([self.G[n] for n in names])
        if stage == "exchange":
            for n in names:
                self.p32[n], self.pbf[n] = _pair_sum(self.cidx, self.G[n], self.got1[n], "pair_sum_" + n)
            return _exchange_payload([self.pbf[n] for n in names])
        for n in names:
            self.half[n] = _chip_sum(self.meidx, self.p32[n], self.got2[n], "chip_sum_" + n)
        return _join_payload([self.half[n] for n in names])

    def _store(self, stages, got):
        for (stage, key), outs in zip(stages, got):
            if stage == "gather":
                self.W.update(zip(key, outs))
            elif stage == "small":
                self.small_buf = outs[0]
            else:
                {"swap": self.got1, "exchange": self.got2, "join": self.theirs}[stage].update(
                    zip(REDUCE_GROUP[key], outs))

    def _standalone(self, name, stages):
        self._store(stages, _comm_call(name, [self._payload(s, k) for s, k in stages]))

    def carry(self, name):
        self._open = [("gather", GATHER_AT[name])] if name in GATHER_AT else []
        self._open += REDUCE_AT.get(name, [])
        return tuple(self._payload(s, k) for s, k in self._open)

    def landed(self, name, got):
        self._store(self._open, got)

    def grad(self, name, g4):
        self.G[name] = g4

    def small(self, raw):
        self.raw = raw

    def tail_begin(self):
        self._standalone("reduce_tail_swap", [("swap", LAST_GROUP), ("join", "mix")])
        names = REDUCE_GROUP[LAST_GROUP]
        for n in names:
            self.p32[n], self.pbf[n] = _pair_sum(self.cidx, self.G[n], self.got1[n], "pair_sum_" + n)
        self._split = _exchange_start([self.pbf[n] for n in names], "exchange_last_start")
        return self._split[3]

    def tail_end(self, after):
        sems, srcs, lands, _ = self._split
        self.got2.update(zip(REDUCE_GROUP[LAST_GROUP], _exchange_wait(sems, srcs, lands, after, "exchange_last_wait")))
        self._standalone("reduce_tail_join", [("join", LAST_GROUP)])


def _small_grads(raw_sum, sp):
    _, vjp = jax.vjp(_zoh, sp["ssm_lam_re"], sp["ssm_lam_im"], sp["ssm_log_step"], sp["ssm_b_re"], sp["ssm_b_im"])
    d_lre, d_lim, d_ls, d_bre, d_bim = vjp((raw_sum["a_re"], raw_sum["a_im"], raw_sum["bb_re"], raw_sum["bb_im"]))
    g = {k: raw_sum[k] for k in ("ln1_g", "ln1_b", "ln2_g", "ln2_b", "ln3_g", "ln3_b", "ln4_g", "ln4_b",
                                 "conv_w", "conv_b", "ssm_c_re", "ssm_c_im", "ssm_d")}
    g.update(ssm_lam_re=d_lre, ssm_lam_im=d_lim, ssm_log_step=d_ls, ssm_b_re=d_bre, ssm_b_im=d_bim)
    return g


def kernel(x, p, ffn1_w_in, ffn1_w_out, ln1_g, ln1_b, mix_w_in, conv_w, conv_b, conv_w_out, ssm_lam_re, ssm_lam_im, ssm_log_step, ssm_b_re, ssm_b_im, ssm_c_re, ssm_c_im, ssm_d, ssm_w_glu, mix_w_out, ln2_g, ln2_b, ffn2_w_in, ffn2_w_out, ln3_g, ln3_b, ple_w_in, ple_w_gate, ln4_g, ln4_b, loss_target, m_ffn1_w_in, m_ffn1_w_out, m_ln1_g, m_ln1_b, m_mix_w_in, m_conv_w, m_conv_b, m_conv_w_out, m_ssm_lam_re, m_ssm_lam_im, m_ssm_log_step, m_ssm_b_re, m_ssm_b_im, m_ssm_c_re, m_ssm_c_im, m_ssm_d, m_ssm_w_glu, m_mix_w_out, m_ln2_g, m_ln2_b, m_ffn2_w_in, m_ffn2_w_out, m_ln3_g, m_ln3_b, m_ple_w_in, m_ple_w_gate, m_ln4_g, m_ln4_b, v_ffn1_w_in, v_ffn1_w_out, v_ln1_g, v_ln1_b, v_mix_w_in, v_conv_w, v_conv_b, v_conv_w_out, v_ssm_lam_re, v_ssm_lam_im, v_ssm_log_step, v_ssm_b_re, v_ssm_b_im, v_ssm_c_re, v_ssm_c_im, v_ssm_d, v_ssm_w_glu, v_mix_w_out, v_ln2_g, v_ln2_b, v_ffn2_w_in, v_ffn2_w_out, v_ln3_g, v_ln3_b, v_ple_w_in, v_ple_w_gate, v_ln4_g, v_ln4_b):
    args = dict(locals())
    w = {n: args[n] for n in WEIGHTS}
    m = {n: args["m_" + n] for n in WEIGHTS}
    v = {n: args["v_" + n] for n in WEIGHTS}
    _, _, c, me = _where()
    cidx = jnp.reshape(c, (1,)).astype(jnp.int32)
    meidx = jnp.reshape(me, (1,)).astype(jnp.int32)

    bufs = {n: _slot_cast(meidx, w[n][0], BF16, "cast_" + n) for n in BIG}
    bufs["conv_w"] = _slot_cast(meidx, jnp.pad(conv_w[0], ((0, 13), (0, 0))), F32, "cast_conv_w")
    sched = _Sched(bufs, cidx, meidx)

    sp = {n: (w[n] if w[n].ndim == 2 and n != "ssm_log_step" else w[n][0]) for n in SMALL if n != "conv_w"}
    loss_part, dx0 = _local_step(x[0], p[0, 0], loss_target[0], sp, sched, 256, 256)
    token = sched.tail_begin()

    raw_shapes = [sched.raw[k].shape for k in RAW_ORDER]
    raw_sum = dict(zip(RAW_ORDER, _unpack(_sum8(sched.small_buf, token), raw_shapes)))
    loss = raw_sum["loss"][0]
    sg = _small_grads(raw_sum, sp)
    sg["conv_w"] = lax.dynamic_slice_in_dim(sg["conv_w"], me * 128, 128, axis=1)
    small_shapes = [w[n].shape for n in SMALL]
    gp = _pack([sg[n] for n in SMALL])
    d_s, m_s, v_s = _adamw(_pack([w[n] for n in SMALL]), gp, _pack([m[n] for n in SMALL]),
                           _pack([v[n] for n in SMALL]), "adamw_small")

    out_g, out_d, out_m, out_v = {}, {}, {}, {}
    for n, a, b_, c_, d_ in zip(SMALL, _unpack(gp, small_shapes), _unpack(d_s, small_shapes),
                                _unpack(m_s, small_shapes), _unpack(v_s, small_shapes)):
        out_g[n], out_d[n], out_m[n], out_v[n] = a, b_, c_, d_
    def big_adamw(names):
        for n in names:
            g, dl, mn, vn = _adamw_pair(cidx, w[n][0], sched.half[n], sched.theirs[n], m[n][0], v[n][0], token,
                                        "adamw_" + n)
            out_g[n], out_d[n], out_m[n], out_v[n] = g[None], dl[None], mn[None], vn[None]

    early = [n for n in BIG if n not in REDUCE_GROUP[LAST_GROUP]]
    big_adamw(early)
    sched.tail_end([d_s] + [out_v[n] for n in early])
    big_adamw(REDUCE_GROUP[LAST_GROUP])

    return (loss, dx0[None], *[out_g[n] for n in WEIGHTS], *[out_d[n] for n in WEIGHTS],
            *[out_m[n] for n in WEIGHTS], *[out_v[n] for n in WEIGHTS])
```

```python
import functools
import math

import jax
import jax.numpy as jnp
import numpy as np
from jax import lax
from jax.experimental import pallas as pl
from jax.experimental.pallas import tpu as pltpu

F32, BF16 = jnp.float32, jnp.bfloat16
D = 1024
FF = 2816
FFH = FF // 2
CONV = 512
SSM = 512
GROUPS = 32
STATE = 64
LANES = GROUPS * STATE
SCAN_W = 256
SCAN_R = 256
ALPHA = 2.0 ** 0.25
LN_EPS = 1e-5
GELU_C = math.sqrt(2.0 / math.pi)
B1, B2, LR, EPS, WD, STEP = 0.9, 0.999, 0.001, 1e-8, 0.01, 10
MESH = pl.DeviceIdType.MESH
ANY = pl.BlockSpec(memory_space=pl.ANY)
VMEM_FULL = pl.BlockSpec(memory_space=pltpu.VMEM)


def _cp(vmem_mb=48, n_axes=1):
    return pltpu.CompilerParams(vmem_limit_bytes=vmem_mb << 20,
                                dimension_semantics=("arbitrary",) * n_axes)


def _hbm(*arrs):
    return [pltpu.with_memory_space_constraint(a, pltpu.HBM) for a in arrs]


def _hbm_out(shapes):
    if isinstance(shapes, (list, tuple)):
        return [pltpu.HBM(s.shape, s.dtype) for s in shapes]
    return pltpu.HBM(shapes.shape, shapes.dtype)


def _nn(a, b):
    return jnp.dot(a, b, preferred_element_type=F32)


def _nt(a, b):
    return lax.dot_general(a, b, (((1,), (1,)), ((), ())), preferred_element_type=F32)


def _tn(a, b):
    return lax.dot_general(a, b, (((0,), (0,)), ((), ())), preferred_element_type=F32)


def _sig(v):
    return jax.nn.sigmoid(v)


def _ln_stats(r):
    mu = jnp.mean(r, axis=-1, keepdims=True)
    xc = r - mu
    var = jnp.mean(xc * xc, axis=-1, keepdims=True)
    rstd = lax.rsqrt(var + LN_EPS)
    return xc * rstd, rstd


def _ln_bwd(dy, r, g):
    xhat, rstd = _ln_stats(r)
    dyg = dy * g
    m1 = jnp.mean(dyg, axis=-1, keepdims=True)
    m2 = jnp.mean(dyg * xhat, axis=-1, keepdims=True)
    return rstd * (dyg - m1 - xhat * m2), xhat


def _rowsum(v):
    return jnp.sum(v, axis=0, keepdims=True)


class _Payload:
    def __init__(self, operands, outs, aliases, sems, start, finish):
        self.operands, self.outs, self.aliases, self.sems = list(operands), list(outs), dict(aliases), list(sems)
        self.start, self.finish = start, finish


def _split(flat, comm, attr):
    out, i = [], 0
    for p in comm:
        n = len(getattr(p, attr))
        out.append(list(flat[i:i + n]))
        i += n
    return out


def _run_comm(comm, which, cin, cout, csem):
    for p, a, b, s in zip(comm, _split(cin, comm, "operands"), _split(cout, comm, "outs"), _split(csem, comm, "sems")):
        getattr(p, which)(a, b, s)


def _pcall(body, *, name, grid, in_specs, out_specs, out_shape, operands, scratch=(), vmem_mb=48, aliases=None,
           comm=()):
    ni, no, ns = len(in_specs), len(out_specs), len(scratch)
    c_ops = [a for p in comm for a in p.operands]
    c_outs = [s for p in comm for s in p.outs]
    c_sems = [s for p in comm for s in p.sems]
    io = dict(aliases or {})
    off_i, off_o = ni, no
    for p in comm:
        for a, b in p.aliases.items():
            io[off_i + a] = off_o + b
        off_i += len(p.operands)
        off_o += len(p.outs)

    def wrapped(*refs):
        ins, cin = refs[:ni], refs[ni:ni + len(c_ops)]
        o0 = ni + len(c_ops)
        outs, cout = refs[o0:o0 + no], refs[o0 + no:o0 + no + len(c_outs)]
        s0 = o0 + no + len(c_outs)
        scr, csem = refs[s0:s0 + ns], refs[s0 + ns:]
        if comm:
            first = functools.reduce(jnp.logical_and, [pl.program_id(a) == 0 for a in range(len(grid))])
            pl.when(first)(lambda: _run_comm(comm, "start", cin, cout, csem))
        body(*ins, *outs, *scr)
        if comm:
            last = functools.reduce(jnp.logical_and, [pl.program_id(a) == grid[a] - 1 for a in range(len(grid))])
            pl.when(last)(lambda: _run_comm(comm, "finish", cin, cout, csem))

    res = pl.pallas_call(
        wrapped, name=name, grid=grid,
        in_specs=list(in_specs) + [ANY] * len(c_ops), out_specs=list(out_specs) + [ANY] * len(c_outs),
        out_shape=_hbm_out(list(out_shape) + c_outs), scratch_shapes=list(scratch) + c_sems,
        input_output_aliases=io,
        compiler_params=pltpu.CompilerParams(vmem_limit_bytes=vmem_mb << 20,
                                             dimension_semantics=("arbitrary",) * len(grid),
                                             has_side_effects=bool(comm)),
    )(*_hbm(*operands, *c_ops))
    return list(res[:no]), _split(res[no:], comm, "outs")


def _comm_call(name, comm):
    c_ops = [a for p in comm for a in p.operands]
    c_outs = [s for p in comm for s in p.outs]
    c_sems = [s for p in comm for s in p.sems]
    io, off_i, off_o = {}, 0, 0
    for p in comm:
        for a, b in p.aliases.items():
            io[off_i + a] = off_o + b
        off_i += len(p.operands)
        off_o += len(p.outs)

    def body(*refs):
        cin, cout = refs[:len(c_ops)], refs[len(c_ops):len(c_ops) + len(c_outs)]
        csem = refs[len(c_ops) + len(c_outs):]
        _run_comm(comm, "start", cin, cout, csem)
        _run_comm(comm, "finish", cin, cout, csem)

    res = pl.pallas_call(
        body, name=name, in_specs=[ANY] * len(c_ops), out_specs=[ANY] * len(c_outs), out_shape=_hbm_out(c_outs),
        scratch_shapes=c_sems, input_output_aliases=io,
        compiler_params=pltpu.CompilerParams(has_side_effects=True),
    )(*_hbm(*c_ops))
    return _split(res, comm, "outs")


def _ffn_fwd(x, xb, w_in4, w_out2, g, b, tm, name, comm=()):
    T = x.shape[0]

    def body(x_ref, xb_ref, wg_ref, wu_ref, wo_ref, g_ref, b_ref, h_ref, r_ref, xo_ref, xob_ref, acc):
        k = pl.program_id(1)
        xv = xb_ref[...]
        gt = _nn(xv, wg_ref[...])
        up = _nn(xv, wu_ref[...])
        a = (gt * _sig(gt) * up).astype(BF16)
        h_ref[:, 0:FFH] = gt.astype(BF16)
        h_ref[:, FFH:2 * FFH] = up.astype(BF16)
        acc[...] = jnp.where(k == 0, 0.0, acc[...]) + _nn(a, wo_ref[...])

        @pl.when(k == 1)
        def _():
            r = ALPHA * x_ref[...] + 0.5 * acc[...]
            xhat, _ = _ln_stats(r)
            xo = xhat * g_ref[...] + b_ref[...]
            r_ref[...] = r
            xo_ref[...] = xo
            xob_ref[...] = xo.astype(BF16)

    tok = pl.BlockSpec((tm, D), lambda i, k: (i, 0))
    vec = pl.BlockSpec((1, D), lambda i, k: (0, 0))
    return _pcall(
        body, name=name, grid=(T // tm, 2),
        in_specs=[tok, tok,
                  pl.BlockSpec((None, D, FFH), lambda i, k: (k, 0, 0)),
                  pl.BlockSpec((None, D, FFH), lambda i, k: (k + 2, 0, 0)),
                  pl.BlockSpec((None, FFH, D), lambda i, k: (k, 0, 0)),
                  vec, vec],
        out_specs=[pl.BlockSpec((tm, FF), lambda i, k: (i, k)), tok, tok, tok],
        out_shape=[jax.ShapeDtypeStruct((T, 2 * FF), BF16), jax.ShapeDtypeStruct((T, D), F32),
                   jax.ShapeDtypeStruct((T, D), F32), jax.ShapeDtypeStruct((T, D), BF16)],
        scratch=[pltpu.VMEM((tm, D), F32)], vmem_mb=56, comm=comm,
        operands=(x, xb, w_in4, w_in4, w_out2, g, b))


def _ffn_bwd(dy, r, g, h, w_in4, w_out2, tm, name, comm=()):
    T = dy.shape[0]

    def body(dy_ref, r_ref, g_ref, h_ref, wg_ref, wu_ref, wo_ref,
             dx_ref, dh_ref, a_ref, df_ref, dg_ref, db_ref, acc, dr_s, dfb_s):
        i, k = pl.program_id(0), pl.program_id(1)

        @pl.when(k == 0)
        def _():
            dyv = dy_ref[...]
            dr, xhat = _ln_bwd(dyv, r_ref[...], g_ref[...])
            pg, pb = _rowsum(dyv * xhat), _rowsum(dyv)

            @pl.when(i == 0)
            def _():
                dg_ref[...] = pg
                db_ref[...] = pb

            @pl.when(i > 0)
            def _():
                dg_ref[...] += pg
                db_ref[...] += pb

            dr_s[...] = dr
            dfb = (0.5 * dr).astype(BF16)
            dfb_s[...] = dfb
            df_ref[...] = dfb

        da = _nt(dfb_s[...], wo_ref[...])
        gt = h_ref[:, 0:FFH].astype(F32)
        up = h_ref[:, FFH:2 * FFH].astype(F32)
        sg = _sig(gt)
        silu = gt * sg
        dgate = (da * up * (sg * (1.0 + gt * (1.0 - sg)))).astype(BF16)
        dup = (da * silu).astype(BF16)
        a_ref[...] = (silu * up).astype(BF16)
        dh_ref[:, 0:FFH] = dgate
        dh_ref[:, FFH:2 * FFH] = dup
        acc[...] = jnp.where(k == 0, 0.0, acc[...]) + _nt(dgate, wg_ref[...]) + _nt(dup, wu_ref[...])

        @pl.when(k == 1)
        def _():
            dx_ref[...] = ALPHA * dr_s[...] + acc[...]

    tok = pl.BlockSpec((tm, D), lambda i, k: (i, 0))
    vec = pl.BlockSpec((1, D), lambda i, k: (0, 0))
    wide = pl.BlockSpec((tm, FF), lambda i, k: (i, k))
    return _pcall(
        body, name=name, grid=(T // tm, 2),
        in_specs=[tok, tok, vec, wide,
                  pl.BlockSpec((None, D, FFH), lambda i, k: (k, 0, 0)),
                  pl.BlockSpec((None, D, FFH), lambda i, k: (k + 2, 0, 0)),
                  pl.BlockSpec((None, FFH, D), lambda i, k: (k, 0, 0))],
        out_specs=[tok, wide, pl.BlockSpec((tm, FFH), lambda i, k: (i, k)), tok, vec, vec],
        out_shape=[jax.ShapeDtypeStruct((T, D), F32), jax.ShapeDtypeStruct((T, 2 * FF), BF16),
                   jax.ShapeDtypeStruct((T, FF), BF16), jax.ShapeDtypeStruct((T, D), BF16),
                   jax.ShapeDtypeStruct((1, D), F32), jax.ShapeDtypeStruct((1, D), F32)],
        scratch=[pltpu.VMEM((tm, D), F32), pltpu.VMEM((tm, D), F32), pltpu.VMEM((tm, D), BF16)],
        vmem_mb=56, comm=comm, operands=(dy, r, g, h, w_in4, w_in4, w_out2))


def _mm_tn(a, b, tk, tn, name, shard_cols=None, interleaved=False, comm=()):
    T, K = a.shape
    N = b.shape[1]

    def body(a_ref, b_ref, o_ref):
        o_ref[...] = _tn(a_ref[...], b_ref[...])

    if shard_cols is None:
        out_shape = jax.ShapeDtypeStruct((K, N), F32)
        out_spec = pl.BlockSpec((tk, tn), lambda ki, nj: (ki, nj))
    else:
        per = shard_cols // tn

        def shard(nj):
            blk = nj // per
            return (blk % 2) * 2 + blk // 2 if interleaved else blk

        out_shape = jax.ShapeDtypeStruct((N // shard_cols, K, shard_cols), F32)
        out_spec = pl.BlockSpec((None, tk, tn), lambda ki, nj: (shard(nj), ki, nj % per))
    (out,), got = _pcall(
        body, name=name, grid=(K // tk, N // tn),
        in_specs=[pl.BlockSpec((T, tk), lambda ki, nj: (0, ki)), pl.BlockSpec((T, tn), lambda ki, nj: (0, nj))],
        out_specs=[out_spec], out_shape=[out_shape], comm=comm, operands=(a, b))
    return out, got


def _mix_fwd_a(xb, w_mix4, conv_w, conv_b, w_co4, tm):
    T = xb.shape[0]

    def body(xb_ref, w_ref, cw_ref, cb_ref, wco_ref,
             pc_ref, z_ref, yin_ref, su_ref, sub_ref, gc_ref, gs_ref, yc_ref, qbuf):
        @pl.when(pl.program_id(0) == 0)
        def _():
            qbuf[pl.ds(0, 8), :] = jnp.zeros((8, CONV), F32)

        xv = xb_ref[...]
        p0 = _nn(xv, w_ref[0])
        p1 = _nn(xv, w_ref[1])
        gc_ref[...] = _nn(xv, w_ref[2])
        gs_ref[...] = _nn(xv, w_ref[3])
        cbv, ccv = p0[:, :CONV], p0[:, CONV:]
        chv, suv = p1[:, :CONV], p1[:, CONV:]
        q = ccv * chv
        qbuf[pl.ds(8, tm), :] = q
        cw = cw_ref[...]
        z = (cw[2:3] * q + cw[1:2] * qbuf[pl.ds(7, tm), :] + cw[0:1] * qbuf[pl.ds(6, tm), :]
             + cb_ref[...])
        qbuf[pl.ds(0, 8), :] = q[tm - 8:tm]
        yin = (cbv * z).astype(BF16)
        pc_ref[:, 0:CONV] = cbv.astype(BF16)
        pc_ref[:, CONV:2 * CONV] = ccv.astype(BF16)
        pc_ref[:, 2 * CONV:3 * CONV] = chv.astype(BF16)
        z_ref[...] = z.astype(BF16)
        yin_ref[...] = yin
        su_ref[...] = suv
        sub_ref[...] = suv.astype(BF16)
        for k in range(4):
            yc_ref[:, 256 * k:256 * (k + 1)] = _nn(yin, wco_ref[k])

    def tok(n):
        return pl.BlockSpec((tm, n), lambda i: (i, 0))

    def full(shape):
        return pl.BlockSpec(shape, lambda i: (0,) * len(shape))

    return pl.pallas_call(
        body, name="mix_fwd_a", grid=(T // tm,),
        in_specs=[tok(D), full((4, D, D)), full((3, CONV)), full((1, CONV)), full((4, CONV, 256))],
        out_specs=[tok(3 * CONV), tok(CONV), tok(CONV), tok(SSM), tok(SSM), tok(D), tok(D), tok(D)],
        out_shape=_hbm_out([jax.ShapeDtypeStruct((T, 3 * CONV), BF16), jax.ShapeDtypeStruct((T, CONV), BF16),
                            jax.ShapeDtypeStruct((T, CONV), BF16), jax.ShapeDtypeStruct((T, SSM), F32),
                            jax.ShapeDtypeStruct((T, SSM), BF16), jax.ShapeDtypeStruct((T, D), F32),
                            jax.ShapeDtypeStruct((T, D), F32), jax.ShapeDtypeStruct((T, D), F32)]),
        scratch_shapes=[pltpu.VMEM((tm + 8, CONV), F32)],
        compiler_params=_cp(56, 1),
    )(*_hbm(xb, w_mix4, conv_w, conv_b, w_co4))


def _scan_inplace(bre, bim, ar, ai, T, rev):
    R = SCAN_R
    if rev:
        ai = -ai
    d = 1
    while d < T:
        if d < 8:
            def step(i, _, d=d, ar=ar, ai=ai):
                c = i if rev else T // R - 1 - i
                t0 = pl.multiple_of(c * R, R)
                if rev:
                    wr = bre[pl.ds(t0 + 8, R + 8), :]
                    wi = bim[pl.ds(t0 + 8, R + 8), :]
                    shr = pltpu.roll(wr, R + 8 - d, 0)[0:R]
                    shi = pltpu.roll(wi, R + 8 - d, 0)[0:R]
                    cr, ci = wr[0:R], wi[0:R]
                else:
                    wr = bre[pl.ds(t0, R + 8), :]
                    wi = bim[pl.ds(t0, R + 8), :]
                    shr = pltpu.roll(wr, d, 0)[8:8 + R]
                    shi = pltpu.roll(wi, d, 0)[8:8 + R]
                    cr, ci = wr[8:8 + R], wi[8:8 + R]
                bre[pl.ds(t0 + 8, R), :] = cr + ar * shr - ai * shi
                bim[pl.ds(t0 + 8, R), :] = ci + ar * shi + ai * shr
                return 0

            lax.fori_loop(0, T // R, step, 0)
        else:
            def upd(lo, n, d=d, ar=ar, ai=ai):
                src = lo + d if rev else lo - d
                if not isinstance(lo, int):
                    lo, src = pl.multiple_of(lo + 8, 8), pl.multiple_of(src + 8, 8)
                else:
                    lo, src = lo + 8, src + 8
                cr = bre[pl.ds(lo, n), :]
                ci = bim[pl.ds(lo, n), :]
                shr = bre[pl.ds(src, n), :]
                shi = bim[pl.ds(src, n), :]
                bre[pl.ds(lo, n), :] = cr + ar * shr - ai * shi
                bim[pl.ds(lo, n), :] = ci + ar * shi + ai * shr

            nfull = (T - d) // R if d >= R else T // R - 1

            def step(i, _, upd=upd, d=d):
                if rev:
                    t0 = i * R
                else:
                    t0 = T - (i + 1) * R
                upd(t0, R)
                return 0

            if nfull > 0:
                lax.fori_loop(0, nfull, step, 0)
            if d < R:
                if rev:
                    upd(T - R, R - d)
                else:
                    upd(d, R - d)
        ar, ai = ar * ar - ai * ai, 2.0 * ar * ai
        d *= 2


def _scan_specs(T):
    W = SCAN_W
    lane = pl.BlockSpec((T, W), lambda j: (0, j))
    col = pl.BlockSpec((T, 128), lambda j: (0, j // 2))
    wb = pl.BlockSpec((None, 128, W), lambda j: (j, 0, 0))
    wc = pl.BlockSpec((None, W, 128), lambda j: (j, 0, 0))
    vec = pl.BlockSpec((1, W), lambda j: (0, j))
    return lane, col, wb, wc, vec


def _s5_scan_fwd(su_b, wb_re, wb_im, a_re, a_im, comm=()):
    T = su_b.shape[0]
    W = SCAN_W

    def body(su_ref, wbr_ref, wbi_ref, ar_ref, ai_ref, sr_ref, si_ref, bre, bim):
        zero = jnp.zeros((8, W), F32)
        for buf in (bre, bim):
            buf[pl.ds(0, 8), :] = zero
            buf[pl.ds(T + 8, 8), :] = zero
        su = su_ref[...]
        bre[pl.ds(8, T), :] = _nn(su, wbr_ref[...])
        bim[pl.ds(8, T), :] = _nn(su, wbi_ref[...])
        _scan_inplace(bre, bim, ar_ref[...], ai_ref[...], T, rev=False)
        sr_ref[...] = bre[pl.ds(8, T), :]
        si_ref[...] = bim[pl.ds(8, T), :]

    lane, col, wb, wc, vec = _scan_specs(T)
    return _pcall(
        body, name="s5_scan_fwd", grid=(LANES // W,),
        in_specs=[col, wb, wb, vec, vec],
        out_specs=[lane, lane],
        out_shape=[jax.ShapeDtypeStruct((T, LANES), F32)] * 2,
        scratch=[pltpu.VMEM((T + 16, W), F32)] * 2, comm=comm,
        operands=(su_b, wb_re, wb_im, a_re, a_im))


def _gelu(s):
    th = jnp.tanh(GELU_C * (s + 0.044715 * s * s * s))
    return 0.5 * s * (1.0 + th), th


def _mix_fwd_b(st_re, st_im, wc_re4, wc_im4, su, dvec, w_glu4, g_conv, g_ssm, y_conv, w_mo, x1, g, b, tm, comm=()):
    T = su.shape[0]

    def body(sr_ref, si_ref, wcr_ref, wci_ref, su_ref, d_ref, wg_ref, gc_ref, gs_ref, yc_ref, wmo_ref,
             x_ref, g_ref, b_ref, s_ref, sgb_ref, ga_ref, gb_ref, mb_ref, r_ref, xo_ref, xob_ref):
        srb = sr_ref[...].astype(BF16)
        sib = si_ref[...].astype(BF16)
        ys = [_nn(srb[:, 512 * J:512 * (J + 1)], wcr_ref[J]) + _nn(sib[:, 512 * J:512 * (J + 1)], wci_ref[J])
              for J in range(4)]
        s = jnp.concatenate(ys, axis=1) + d_ref[...] * su_ref[...]
        sg, _ = _gelu(s)
        sgb = sg.astype(BF16)
        ga = jnp.concatenate([_nn(sgb, wg_ref[0]), _nn(sgb, wg_ref[1])], axis=1)
        gb = jnp.concatenate([_nn(sgb, wg_ref[2]), _nn(sgb, wg_ref[3])], axis=1)
        merged = _sig(gc_ref[...]) * yc_ref[...] + _sig(gs_ref[...]) * (ga * _sig(gb))
        mb = merged.astype(BF16)
        r = ALPHA * x_ref[...] + _nn(mb, wmo_ref[...])
        xhat, _ = _ln_stats(r)
        xo = xhat * g_ref[...] + b_ref[...]
        s_ref[...] = s
        sgb_ref[...] = sgb
        ga_ref[...] = ga
        gb_ref[...] = gb
        mb_ref[...] = mb
        r_ref[...] = r
        xo_ref[...] = xo
        xob_ref[...] = xo.astype(BF16)

    def tok(n):
        return pl.BlockSpec((tm, n), lambda i: (i, 0))

    def full(shape):
        return pl.BlockSpec(shape, lambda i: (0,) * len(shape))

    return _pcall(
        body, name="mix_fwd_b", grid=(T // tm,),
        in_specs=[tok(LANES), tok(LANES), full((4, 512, 128)), full((4, 512, 128)), tok(SSM), full((1, SSM)),
                  full((4, SSM, 512)), tok(D), tok(D), tok(D), full((D, D)), tok(D), full((1, D)), full((1, D))],
        out_specs=[tok(SSM), tok(SSM), tok(D), tok(D), tok(D), tok(D), tok(D), tok(D)],
        out_shape=[jax.ShapeDtypeStruct((T, SSM), F32), jax.ShapeDtypeStruct((T, SSM), BF16),
                   jax.ShapeDtypeStruct((T, D), F32), jax.ShapeDtypeStruct((T, D), F32),
                   jax.ShapeDtypeStruct((T, D), BF16), jax.ShapeDtypeStruct((T, D), F32),
                   jax.ShapeDtypeStruct((T, D), F32), jax.ShapeDtypeStruct((T, D), BF16)],
        vmem_mb=56, comm=comm,
        operands=(st_re, st_im, wc_re4, wc_im4, su, dvec, w_glu4, g_conv, g_ssm, y_conv, w_mo, x1, g, b))


def _ple_loss(x3, x3b, p, w_pi4, w_pg, g, b, target, tm):
    T = x3.shape[0]
    PD = p.shape[1]

    def body(x_ref, xb_ref, p_ref, wpi_ref, wpg_ref, g_ref, b_ref, t_ref,
             loss_ref, dx_ref, pb_ref, dpw_ref, dgt_ref, dg_ref, db_ref):
        i = pl.program_id(0)
        pb = p_ref[...].astype(BF16)
        pw = jnp.concatenate([_nn(pb, wpi_ref[k]) for k in range(4)], axis=1)
        gt = _nn(xb_ref[...], wpg_ref[...])
        sg = _sig(gt)
        r = ALPHA * x_ref[...] + pw * sg
        gv = g_ref[...]
        xhat, rstd = _ln_stats(r)
        err = xhat * gv + b_ref[...] - t_ref[...]
        lpart = jnp.zeros((1, 128), F32) + 0.5 * jnp.sum(jnp.mean(err * err, axis=-1, keepdims=True))
        dy = err * (1.0 / D)
        dyg = dy * gv
        m1 = jnp.mean(dyg, axis=-1, keepdims=True)
        m2 = jnp.mean(dyg * xhat, axis=-1, keepdims=True)
        dr = rstd * (dyg - m1 - xhat * m2)
        pg, pbias = _rowsum(dy * xhat), _rowsum(dy)

        @pl.when(i == 0)
        def _():
            loss_ref[...] = lpart
            dg_ref[...] = pg
            db_ref[...] = pbias

        @pl.when(i > 0)
        def _():
            loss_ref[...] += lpart
            dg_ref[...] += pg
            db_ref[...] += pbias

        dgt = (dr * pw * sg * (1.0 - sg)).astype(BF16)
        pb_ref[...] = pb
        dpw_ref[...] = (dr * sg).astype(BF16)
        dgt_ref[...] = dgt
        dx_ref[...] = ALPHA * dr + _nt(dgt, wpg_ref[...])

    def tok(n):
        return pl.BlockSpec((tm, n), lambda i: (i, 0))

    def full(shape):
        return pl.BlockSpec(shape, lambda i: (0,) * len(shape))

    return pl.pallas_call(
        body, name="ple_loss", grid=(T // tm,),
        in_specs=[tok(D), tok(D), tok(PD), full((4, PD, 256)), full((D, D)), full((1, D)), full((1, D)), tok(D)],
        out_specs=[full((1, 128)), tok(D), tok(PD), tok(D), tok(D), full((1, D)), full((1, D))],
        out_shape=_hbm_out([jax.ShapeDtypeStruct((1, 128), F32), jax.ShapeDtypeStruct((T, D), F32),
                            jax.ShapeDtypeStruct((T, PD), BF16), jax.ShapeDtypeStruct((T, D), BF16),
                            jax.ShapeDtypeStruct((T, D), BF16), jax.ShapeDtypeStruct((1, D), F32),
                            jax.ShapeDtypeStruct((1, D), F32)]),
        compiler_params=_cp(48, 1),
    )(*_hbm(x3, x3b, p, w_pi4, w_pg, g, b, target))


def _mix_bwd_b(dy, r2, g, w_mo, g_conv, g_ssm, y_conv, ga, gb, s, su, dvec, w_glu4, wc_re4, wc_im4, tm, comm=()):
    T = dy.shape[0]

    def body(dy_ref, r_ref, g_ref, wmo_ref, gc_ref, gs_ref, yc_ref, ga_ref, gb_ref, s_ref, su_ref, d_ref,
             wg_ref, wcr_ref, wci_ref,
             dres_ref, dmix_ref, dgl_ref, dsb_ref, dud_ref, gsr_ref, gsi_ref, dyc_ref, dp_ref,
             dg_ref, db_ref, dd_ref):
        i = pl.program_id(0)
        dyv = dy_ref[...]
        dr, xhat = _ln_bwd(dyv, r_ref[...], g_ref[...])
        dmix = dr.astype(BF16)
        dmerged = _nt(dmix, wmo_ref[...])
        sc, ss, sgb = _sig(gc_ref[...]), _sig(gs_ref[...]), _sig(gb_ref[...])
        gav = ga_ref[...]
        yssm = gav * sgb
        dgc = dmerged * yc_ref[...] * sc * (1.0 - sc)
        dgss = dmerged * yssm * ss * (1.0 - ss)
        dyssm = dmerged * ss
        dgl = jnp.concatenate([dyssm * sgb, dyssm * gav * sgb * (1.0 - sgb)], axis=1).astype(BF16)
        dsg = (_nt(dgl[:, 0:512], wg_ref[0]) + _nt(dgl[:, 512:1024], wg_ref[1])
               + _nt(dgl[:, 1024:1536], wg_ref[2]) + _nt(dgl[:, 1536:2048], wg_ref[3]))
        sv = s_ref[...]
        _, th = _gelu(sv)
        dgelu = 0.5 * (1.0 + th) + 0.5 * sv * (1.0 - th * th) * GELU_C * (1.0 + 3.0 * 0.044715 * sv * sv)
        ds = dsg * dgelu
        dsb = ds.astype(BF16)
        pg, pb, pd = _rowsum(dyv * xhat), _rowsum(dyv), _rowsum(ds * su_ref[...])

        @pl.when(i == 0)
        def _():
            dg_ref[...] = pg
            db_ref[...] = pb
            dd_ref[...] = pd

        @pl.when(i > 0)
        def _():
            dg_ref[...] += pg
            db_ref[...] += pb
            dd_ref[...] += pd

        dres_ref[...] = ALPHA * dr
        dmix_ref[...] = dmix
        dgl_ref[...] = dgl
        dsb_ref[...] = dsb
        dud_ref[...] = ds * d_ref[...]
        for J in range(4):
            gsr_ref[:, 512 * J:512 * (J + 1)] = _nt(dsb[:, 128 * J:128 * (J + 1)], wcr_ref[J])
            gsi_ref[:, 512 * J:512 * (J + 1)] = _nt(dsb[:, 128 * J:128 * (J + 1)], wci_ref[J])
        dyc_ref[...] = (dmerged * sc).astype(BF16)
        dp_ref[:, 0:D] = dgc.astype(BF16)
        dp_ref[:, D:2 * D] = dgss.astype(BF16)

    def tok(n):
        return pl.BlockSpec((tm, n), lambda i: (i, 0))

    def full(shape):
        return pl.BlockSpec(shape, lambda i: (0,) * len(shape))

    return _pcall(
        body, name="mix_bwd_b", grid=(T // tm,),
        in_specs=[tok(D), tok(D), full((1, D)), full((D, D)), tok(D), tok(D), tok(D), tok(D), tok(D),
                  tok(SSM), tok(SSM), full((1, SSM)), full((4, SSM, 512)), full((4, 512, 128)), full((4, 512, 128))],
        out_specs=[tok(D), tok(D), tok(2 * D), tok(SSM), tok(SSM), tok(LANES), tok(LANES), tok(D),
                   pl.BlockSpec((tm, 2 * D), lambda i: (i, 1)), full((1, D)), full((1, D)), full((1, SSM))],
        out_shape=[jax.ShapeDtypeStruct((T, D), F32), jax.ShapeDtypeStruct((T, D), BF16),
                   jax.ShapeDtypeStruct((T, 2 * D), BF16), jax.ShapeDtypeStruct((T, SSM), BF16),
                   jax.ShapeDtypeStruct((T, SSM), F32), jax.ShapeDtypeStruct((T, LANES), F32),
                   jax.ShapeDtypeStruct((T, LANES), F32), jax.ShapeDtypeStruct((T, D), BF16),
                   jax.ShapeDtypeStruct((T, 4 * D), BF16), jax.ShapeDtypeStruct((1, D), F32),
                   jax.ShapeDtypeStruct((1, D), F32), jax.ShapeDtypeStruct((1, SSM), F32)],
        vmem_mb=56, comm=comm,
        operands=(dy, r2, g, w_mo, g_conv, g_ssm, y_conv, ga, gb, s, su, dvec, w_glu4, wc_re4, wc_im4))


def _s5_scan_bwd(gs_re, gs_im, st_re, st_im, su_b, ds_b, wb_re, wb_im, a_re, a_im, comm=()):
    T = su_b.shape[0]
    W = SCAN_W
    R = SCAN_R

    def body(gr_ref, gi_ref, sr_ref, si_ref, su_ref, ds_ref, wbr_ref, wbi_ref, ar_ref, ai_ref,
             dsu_ref, dwbr_ref, dwbi_ref, dwcr_ref, dwci_ref, dar_ref, dai_ref, gre, gim):
        j = pl.program_id(0)
        zero = jnp.zeros((8, W), F32)
        for buf in (gre, gim):
            buf[pl.ds(0, 8), :] = zero
            buf[pl.ds(T + 8, 8), :] = zero
        gre[pl.ds(8, T), :] = gr_ref[...]
        gim[pl.ds(8, T), :] = gi_ref[...]
        _scan_inplace(gre, gim, ar_ref[...], ai_ref[...], T, rev=True)
        grb = gre[pl.ds(8, T), :].astype(BF16)
        gib = gim[pl.ds(8, T), :].astype(BF16)
        part = _nt(grb, wbr_ref[...]) + _nt(gib, wbi_ref[...])

        @pl.when(j % 2 == 0)
        def _():
            dsu_ref[...] = part

        @pl.when(j % 2 == 1)
        def _():
            dsu_ref[...] += part

        su = su_ref[...]
        dwbr_ref[...] = _tn(su, grb)
        dwbi_ref[...] = _tn(su, gib)
        dsv = ds_ref[...]
        dwcr_ref[...] = _tn(sr_ref[...].astype(BF16), dsv)
        dwci_ref[...] = _tn(si_ref[...].astype(BF16), dsv)
        dar = jnp.zeros((1, W), F32)
        dai = jnp.zeros((1, W), F32)
        for c in range(T // R):
            xr = sr_ref[pl.ds(c * R, R), :]
            xi = si_ref[pl.ds(c * R, R), :]
            g1r = gre[pl.ds(c * R + 9, R), :]
            g1i = gim[pl.ds(c * R + 9, R), :]
            dar = dar + _rowsum(g1r * xr + g1i * xi)
            dai = dai + _rowsum(g1i * xr - g1r * xi)
        dar_ref[...] = dar
        dai_ref[...] = dai

    lane, col, wb, wc, vec = _scan_specs(T)
    return _pcall(
        body, name="s5_scan_bwd", grid=(LANES // W,),
        in_specs=[lane, lane, lane, lane, col, col, wb, wb, vec, vec],
        out_specs=[col, wb, wb, wc, wc, vec, vec],
        out_shape=[jax.ShapeDtypeStruct((T, SSM), F32),
                   jax.ShapeDtypeStruct((LANES // W, 128, W), F32), jax.ShapeDtypeStruct((LANES // W, 128, W), F32),
                   jax.ShapeDtypeStruct((LANES // W, W, 128), F32), jax.ShapeDtypeStruct((LANES // W, W, 128), F32),
                   jax.ShapeDtypeStruct((1, LANES), F32), jax.ShapeDtypeStruct((1, LANES), F32)],
        scratch=[pltpu.VMEM((T + 16, W), F32)] * 2, vmem_mb=56, comm=comm,
        operands=(gs_re, gs_im, st_re, st_im, su_b, ds_b, wb_re, wb_im, a_re, a_im))


def _mix_bwd_a(dyc_b, w_co4, pc, z_b, conv_w, dsu_ssm, du_dir, dproj, dres, w_mix4, tm, comm=()):
    T = dres.shape[0]
    nt = T // tm

    def body(dyc_ref, wco_ref, pc_ref, halo_ref, z_ref, cw_ref, dsu_ref, dud_ref, dpin_ref, dres_ref, w_ref,
             dp_ref, dx_ref, dcw_ref, dcb_ref, dzbuf, qbuf):
        i = pl.program_id(0)
        ii = nt - 1 - i

        @pl.when(i == 0)
        def _():
            dzbuf[pl.ds(tm, 8), :] = jnp.zeros((8, CONV), F32)

        dyc = dyc_ref[...]
        dyin = (_nt(dyc[:, 0:256], wco_ref[0]) + _nt(dyc[:, 256:512], wco_ref[1])
                + _nt(dyc[:, 512:768], wco_ref[2]) + _nt(dyc[:, 768:1024], wco_ref[3]))
        cbv = pc_ref[:, 0:CONV].astype(F32)
        ccv = pc_ref[:, CONV:2 * CONV].astype(F32)
        chv = pc_ref[:, 2 * CONV:3 * CONV].astype(F32)
        dcbv = dyin * z_ref[...].astype(F32)
        dz = dyin * cbv
        dzbuf[pl.ds(0, tm), :] = dz
        cw = cw_ref[...]
        dq = cw[2:3] * dz + cw[1:2] * dzbuf[pl.ds(1, tm), :] + cw[0:1] * dzbuf[pl.ds(2, tm), :]
        dzbuf[pl.ds(tm, 8), :] = dz[0:8]
        q = ccv * chv
        hq = halo_ref[:, CONV:2 * CONV].astype(F32) * halo_ref[:, 2 * CONV:3 * CONV].astype(F32)
        qbuf[pl.ds(0, 8), :] = jnp.where(ii > 0, hq, jnp.zeros_like(hq))
        qbuf[pl.ds(8, tm), :] = q
        pw = jnp.concatenate([_rowsum(dz * qbuf[pl.ds(6, tm), :]), _rowsum(dz * qbuf[pl.ds(7, tm), :]),
                              _rowsum(dz * q), jnp.zeros((5, CONV), F32)], axis=0)
        pbias = _rowsum(dz)

        @pl.when(i == 0)
        def _():
            dcw_ref[...] = pw
            dcb_ref[...] = pbias

        @pl.when(i > 0)
        def _():
            dcw_ref[...] += pw
            dcb_ref[...] += pbias

        dp0 = jnp.concatenate([dcbv, dq * chv], axis=1).astype(BF16)
        dp1 = jnp.concatenate([dq * ccv, dsu_ref[...] + dud_ref[...]], axis=1).astype(BF16)
        dp_ref[:, 0:D] = dp0
        dp_ref[:, D:2 * D] = dp1
        dx_ref[...] = (dres_ref[...] + _nt(dp0, w_ref[0]) + _nt(dp1, w_ref[1])
                       + _nt(dpin_ref[:, 0:D], w_ref[2]) + _nt(dpin_ref[:, D:2 * D], w_ref[3]))

    def tok(n):
        return pl.BlockSpec((tm, n), lambda i: (nt - 1 - i, 0))

    def full(shape):
        return pl.BlockSpec(shape, lambda i: (0,) * len(shape))

    halo = pl.BlockSpec((8, 3 * CONV), lambda i: (jnp.maximum((nt - 1 - i) * (tm // 8) - 1, 0), 0))
    return _pcall(
        body, name="mix_bwd_a", grid=(nt,),
        in_specs=[tok(D), full((4, CONV, 256)), tok(3 * CONV), halo, tok(CONV), full((3, CONV)),
                  tok(SSM), tok(SSM), pl.BlockSpec((tm, 2 * D), lambda i: (nt - 1 - i, 1)), tok(D),
                  full((4, D, D))],
        out_specs=[pl.BlockSpec((tm, 2 * D), lambda i: (nt - 1 - i, 0)), tok(D), full((8, CONV)), full((1, CONV))],
        out_shape=[jax.ShapeDtypeStruct((T, 4 * D), BF16), jax.ShapeDtypeStruct((T, D), F32),
                   jax.ShapeDtypeStruct((8, CONV), F32), jax.ShapeDtypeStruct((1, CONV), F32)],
        scratch=[pltpu.VMEM((tm + 8, CONV), F32), pltpu.VMEM((tm + 8, CONV), F32)],
        aliases={8: 0}, vmem_mb=56, comm=comm,
        operands=(dyc_b, w_co4, pc, pc, z_b, conv_w, dsu_ssm, du_dir, dproj, dres, w_mix4))


def _zoh(lam_re, lam_im, log_step, b_re, b_im):
    dt = jnp.exp(log_step)[:, None]
    mag = jnp.exp(lam_re * dt)
    abr, abi = mag * jnp.cos(lam_im * dt), mag * jnp.sin(lam_im * dt)
    nr, ni = abr - 1.0, abi
    den = lam_re * lam_re + lam_im * lam_im
    cr = (nr * lam_re + ni * lam_im) / den
    ci = (ni * lam_re - nr * lam_im) / den
    bbr = cr[..., None] * b_re - ci[..., None] * b_im
    bbi = cr[..., None] * b_im + ci[..., None] * b_re
    return abr, abi, bbr, bbi


def _wb_blocks(bb):
    eye = jnp.eye(GROUPS, dtype=F32)
    full = jnp.einsum("gni,gh->gihn", bb, eye).reshape(4, 128, 8, SCAN_W)
    return jnp.stack([full[j // 2, :, j, :] for j in range(8)]).astype(BF16)


def _wc_blocks(cc):
    eye = jnp.eye(GROUPS, dtype=F32)
    full = jnp.einsum("gin,gh->gnhi", cc, eye).reshape(4, 512, 4, 128)
    return jnp.stack([full[J, :, J, :] for J in range(4)]).astype(BF16)


_G = np.arange(GROUPS)


def _wb_diag(dwb8):
    d5 = dwb8.reshape(8, 8, 16, 4, 64)
    return d5[_G // 4, _G % 8, :, _G % 4, :].transpose(0, 2, 1)


def _wc_diag(dwc8):
    d5 = dwc8.reshape(8, 4, 64, 8, 16)
    return d5[_G // 4, _G % 4, :, _G % 8, :].transpose(0, 2, 1)


def _where():
    x, y, c = lax.axis_index("x"), lax.axis_index("y"), lax.axis_index("c")
    return x, y, c, 2 * x + y


def _chip_dev(k, c):
    return (k // 2, k % 2, c)


def _slot_cast(meidx, w, dtype, name):
    R, C = w.shape
    tr = _row_tile(R)

    def body(m_ref, w_ref, o_ref):
        o_ref[...] = w_ref[...].astype(dtype)

    gs = pltpu.PrefetchScalarGridSpec(
        num_scalar_prefetch=1, grid=(R // tr,),
        in_specs=[pl.BlockSpec((tr, C), lambda i, m: (i, 0))],
        out_specs=pl.BlockSpec((None, tr, C), lambda i, m: (m[0], i, 0)))
    return pl.pallas_call(
        body, name=name, grid_spec=gs, out_shape=_hbm_out(jax.ShapeDtypeStruct((4, R, C), dtype)),
        compiler_params=_cp(32, 1),
    )(meidx, *_hbm(w))


def _gather_payload(bufs):
    n = len(bufs)

    def half(ref, w, k, cc):
        h = bufs[w].shape[1] // 2
        return ref.at[k, pl.ds(cc * h, h)]

    def ici(ins, outs, sems, w, s):
        x, y, c, me = _where()
        k = (me + 1 + s) % 4
        return pltpu.make_async_remote_copy(
            src_ref=half(ins[w], w, me, c), dst_ref=half(outs[w], w, me, c), send_sem=sems[0].at[3 * w + s],
            recv_sem=sems[1].at[3 * w + s], device_id=_chip_dev(k, c), device_id_type=MESH)

    def landed(outs, sems, w, s):
        x, y, c, me = _where()
        j = (me + 3 - s) % 4
        return pltpu.make_async_remote_copy(
            src_ref=half(outs[w], w, j, c), dst_ref=half(outs[w], w, j, c), send_sem=sems[0].at[3 * w + s],
            recv_sem=sems[1].at[3 * w + s], device_id=(x, y, 1 - c), device_id_type=MESH)

    def passed(outs, sems, w, s, cc):
        x, y, c, me = _where()
        j = (me + 3 - s) % 4
        return pltpu.make_async_remote_copy(
            src_ref=half(outs[w], w, j, cc), dst_ref=half(outs[w], w, j, cc), send_sem=sems[2].at[3 * w + s],
            recv_sem=sems[3].at[3 * w + s], device_id=(x, y, 1 - c), device_id_type=MESH)

    pairs = [(w, s) for w in range(n) for s in range(3)]

    def start(ins, outs, sems):
        for w, s in pairs:
            ici(ins, outs, sems, w, s).start()

    def finish(ins, outs, sems):
        _, _, c, _ = _where()
        for w, s in pairs:
            landed(outs, sems, w, s).wait_recv()
            passed(outs, sems, w, s, c).start()
        for w, s in pairs:
            passed(outs, sems, w, s, 1 - c).wait_recv()
        for w, s in pairs:
            ici(ins, outs, sems, w, s).wait_send()
            passed(outs, sems, w, s, c).wait_send()

    return _Payload(bufs, [jax.ShapeDtypeStruct(b.shape, b.dtype) for b in bufs], {w: w for w in range(n)},
                    [pltpu.SemaphoreType.DMA((3 * n,))] * 4, start, finish)


def _sym_payload(operands, outs, copies, n_copies):
    def start(ins, outs_, sems):
        for cp in copies(ins, outs_, sems[0], sems[1]):
            cp.start()

    def finish(ins, outs_, sems):
        for cp in copies(ins, outs_, sems[0], sems[1]):
            cp.wait()

    p = _Payload(operands, outs, {}, [pltpu.SemaphoreType.DMA((n_copies,))] * 2, start, finish)
    p.copies, p.n_copies = copies, n_copies
    return p


def _swap_payload(g4s):
    def copies(ins, outs, ss, rs):
        x, y, c, me = _where()
        cps = []
        for w, g in enumerate(g4s):
            h = g.shape[1] // 2
            cps.append(pltpu.make_async_remote_copy(
                src_ref=ins[w].at[:, pl.ds((1 - c) * h, h)], dst_ref=outs[w], send_sem=ss.at[w],
                recv_sem=rs.at[w], device_id=(x, y, 1 - c), device_id_type=MESH))
        return cps

    outs = [jax.ShapeDtypeStruct((4, g.shape[1] // 2, g.shape[2]), g.dtype) for g in g4s]
    return _sym_payload(g4s, outs, copies, len(g4s))


def _exchange_payload(pbs):
    def copies(ins, outs, ss, rs):
        x, y, c, me = _where()
        cps = []
        for w in range(len(pbs)):
            for s in range(3):
                k = (me + 1 + s) % 4
                cps.append(pltpu.make_async_remote_copy(
                    src_ref=ins[w].at[k], dst_ref=outs[w].at[2 - s], send_sem=ss.at[3 * w + s],
                    recv_sem=rs.at[3 * w + s], device_id=_chip_dev(k, c), device_id_type=MESH))
        return cps

    outs = [jax.ShapeDtypeStruct((3,) + p.shape[1:], p.dtype) for p in pbs]
    return _sym_payload(pbs, outs, copies, 3 * len(pbs))


HBM_REF = pl.BlockSpec(memory_space=pltpu.HBM)
SEM_REF = pl.BlockSpec(memory_space=pltpu.SEMAPHORE)
DATAFLOW = pltpu.SideEffectType.DATAFLOW_SIDE_EFFECTING


class _SemList:
    def __init__(self, refs):
        self.refs = refs

    @property
    def at(self):
        return self.refs


def _split_start(p, name):
    n_in, n_out, nc = len(p.operands), len(p.outs), p.n_copies
    lands = [lax.empty(s.shape, s.dtype) for s in p.outs]

    def body(*refs):
        ins, lnd = refs[:n_in], refs[n_in:n_in + n_out]
        sems = refs[n_in + n_out:n_in + n_out + 2 * nc]
        for cp in p.copies(ins, lnd, _SemList(sems[:nc]), _SemList(sems[nc:])):
            cp.start()
        refs[-1][...] = jnp.zeros((8, 128), F32)

    res = pl.pallas_call(
        body, name=name,
        in_specs=[HBM_REF] * (n_in + n_out),
        out_specs=[SEM_REF] * (2 * nc) + [HBM_REF] * (n_in + n_out) + [VMEM_FULL],
        out_shape=([pltpu.SemaphoreType.DMA(())] * (2 * nc) + _hbm_out(p.operands) + _hbm_out(lands)
                   + [jax.ShapeDtypeStruct((8, 128), F32)]),
        input_output_aliases={i: 2 * nc + i for i in range(n_in + n_out)},
        compiler_params=pltpu.CompilerParams(has_side_effects=DATAFLOW),
    )(*_hbm(*p.operands, *lands))
    k = 2 * nc
    return list(res[:k]), list(res[k:k + n_in]), list(res[k + n_in:k + n_in + n_out]), res[-1]


def _split_wait(p, handle, after, name):
    sems, srcs, lands, _ = handle
    n_in, n_out, nc = len(srcs), len(lands), p.n_copies

    def body(*refs):
        ins, lnd = refs[:n_in], refs[n_in:n_in + n_out]
        sm = refs[n_in + n_out:n_in + n_out + 2 * nc]
        for cp in p.copies(ins, lnd, _SemList(sm[:nc]), _SemList(sm[nc:])):
            cp.wait_send()
            cp.wait_recv()

    res = pl.pallas_call(
        body, name=name,
        in_specs=[HBM_REF] * (n_in + n_out) + [SEM_REF] * (2 * nc) + [ANY] * len(after),
        out_specs=[HBM_REF] * (n_in + n_out), out_shape=_hbm_out(srcs) + _hbm_out(lands),
        input_output_aliases={i: i for i in range(n_in + n_out)},
        compiler_params=pltpu.CompilerParams(has_side_effects=DATAFLOW),
    )(*srcs, *lands, *sems, *after)
    return list(res[n_in:])


def _join_payload(halves):
    def copies(ins, outs, ss, rs):
        x, y, c, me = _where()
        return [pltpu.make_async_remote_copy(
            src_ref=ins[w], dst_ref=outs[w], send_sem=ss.at[w], recv_sem=rs.at[w],
            device_id=(x, y, 1 - c), device_id_type=MESH) for w in range(len(halves))]

    outs = [jax.ShapeDtypeStruct(a.shape, a.dtype) for a in halves]
    return _sym_payload(halves, outs, copies, len(halves))


def _allgather_payload(v):
    def copies(ins, outs, ss, rs):
        x, y, c, me = _where()
        lin = 4 * x + 2 * y + c
        cps = []
        cps = [pltpu.make_async_copy(ins[0], outs[0].at[lin], ss.at[0])]
        for o in range(1, 8):
            t = (lin + o) % 8
            cps.append(pltpu.make_async_remote_copy(
                src_ref=ins[0], dst_ref=outs[0].at[lin], send_sem=ss.at[o], recv_sem=rs.at[o],
                device_id=(t // 4, (t // 2) % 2, t % 2), device_id_type=MESH))
        return cps

    return _sym_payload([v], [jax.ShapeDtypeStruct((8,) + v.shape, v.dtype)], copies, 8)


def _sum8(buf, token):
    _, P, C = buf.shape

    def body(b_ref, t_ref, o_ref):
        acc = b_ref[0]
        for d in range(1, 8):
            acc = acc + b_ref[d]
        o_ref[...] = acc

    return pl.pallas_call(
        body, name="sum8", in_specs=[VMEM_FULL, VMEM_FULL], out_specs=VMEM_FULL,
        out_shape=jax.ShapeDtypeStruct((P, C), F32),
        compiler_params=pltpu.CompilerParams(vmem_limit_bytes=32 << 20),
    )(buf, token)


def _row_tile(h):
    for t in (256, 176, 128, 64, 32, 16, 8):
        if h % t == 0:
            return t
    raise ValueError(h)


def _pair_sum(cidx, g4, got, name):
    _, R, C = g4.shape
    h = R // 2
    th = _row_tile(h)

    def body(c_ref, a_ref, b_ref, o_ref, ob_ref):
        sm = a_ref[...] + b_ref[...]
        o_ref[...] = sm
        ob_ref[...] = sm.astype(BF16)

    blk = pl.BlockSpec((None, th, C), lambda k, i, c: (k, i, 0))
    gs = pltpu.PrefetchScalarGridSpec(
        num_scalar_prefetch=1, grid=(4, h // th),
        in_specs=[pl.BlockSpec((None, None, th, C), lambda k, i, c: (k, c[0], i, 0)), blk],
        out_specs=[blk, blk])
    return pl.pallas_call(
        body, name=name, grid_spec=gs,
        out_shape=_hbm_out([jax.ShapeDtypeStruct((4, h, C), F32), jax.ShapeDtypeStruct((4, h, C), BF16)]),
        compiler_params=_cp(32, 2),
    )(cidx, *_hbm(g4.reshape(4, 2, h, C), got))


def _chip_sum(meidx, p32, got, name):
    _, h, C = p32.shape
    th = _row_tile(h)

    def body(m_ref, a_ref, b_ref, o_ref):
        o_ref[...] = ((a_ref[...] + b_ref[0].astype(F32)) + b_ref[1].astype(F32)) + b_ref[2].astype(F32)

    gs = pltpu.PrefetchScalarGridSpec(
        num_scalar_prefetch=1, grid=(h // th,),
        in_specs=[pl.BlockSpec((None, th, C), lambda i, m: (m[0], i, 0)),
                  pl.BlockSpec((3, th, C), lambda i, m: (0, i, 0))],
        out_specs=pl.BlockSpec((th, C), lambda i, m: (i, 0)))
    return pl.pallas_call(
        body, name=name, grid_spec=gs, out_shape=_hbm_out(jax.ShapeDtypeStruct((h, C), F32)),
        compiler_params=_cp(32, 1),
    )(meidx, *_hbm(p32, got))


def _adamw_math(w, g, m, v):
    m2 = B1 * m + (1.0 - B1) * g
    v2 = B2 * v + (1.0 - B2) * (g * g)
    m_hat = m2 / (1.0 - B1 ** STEP)
    v_hat = v2 / (1.0 - B2 ** STEP)
    delta = -LR * (m_hat / (jnp.sqrt(v_hat) + EPS) + WD * w)
    return delta, m2, v2


def _adamw_pair(cidx, w, mine, theirs, m, v, token, name):
    R, C = w.shape
    h = R // 2
    tr = _row_tile(h)
    nh = h // tr

    def body(c_ref, w_ref, a_ref, b_ref, m_ref, v_ref, t_ref, g_ref, d_ref, mo_ref, vo_ref):
        own = (pl.program_id(0) // nh) == c_ref[0]
        g = jnp.where(own, a_ref[...], b_ref[...])
        d, m2, v2 = _adamw_math(w_ref[...], g, m_ref[...], v_ref[...])
        g_ref[...] = g
        d_ref[...] = d
        mo_ref[...] = m2
        vo_ref[...] = v2

    blk = pl.BlockSpec((tr, C), lambda i, c: (i, 0))
    hblk = pl.BlockSpec((tr, C), lambda i, c: (i % nh, 0))
    gs = pltpu.PrefetchScalarGridSpec(
        num_scalar_prefetch=1, grid=(R // tr,),
        in_specs=[blk, hblk, hblk, blk, blk, pl.BlockSpec((8, 128), lambda i, c: (0, 0))], out_specs=[blk] * 4)
    return pl.pallas_call(
        body, name=name, grid_spec=gs, out_shape=_hbm_out([jax.ShapeDtypeStruct((R, C), F32)] * 4),
        compiler_params=_cp(32, 1),
    )(cidx, *_hbm(w, mine, theirs, m, v), token)


def _adamw(w, g, m, v, name):
    R, C = w.shape
    tr = _row_tile(R)

    def body(w_ref, g_ref, m_ref, v_ref, d_ref, mo_ref, vo_ref):
        d, m2, v2 = _adamw_math(w_ref[...], g_ref[...], m_ref[...], v_ref[...])
        d_ref[...] = d
        mo_ref[...] = m2
        vo_ref[...] = v2

    blk = pl.BlockSpec((tr, C), lambda i: (i, 0))
    return pl.pallas_call(
        body, name=name, grid=(R // tr,), in_specs=[blk] * 4, out_specs=[blk] * 3,
        out_shape=_hbm_out([jax.ShapeDtypeStruct((R, C), F32)] * 3),
        compiler_params=_cp(32, 1),
    )(*_hbm(w, g, m, v))


def _pack(arrs):
    flat = jnp.concatenate([a.reshape(-1).astype(F32) for a in arrs])
    rows = -(-flat.shape[0] // 1024)
    rows = -(-rows // 8) * 8
    return jnp.pad(flat, (0, rows * 1024 - flat.shape[0])).reshape(rows, 1024)


def _unpack(packed, shapes):
    flat = packed.reshape(-1)
    out, off = [], 0
    for s in shapes:
        n = math.prod(s)
        out.append(flat[off:off + n].reshape(s))
        off += n
    return out


BIG = ["ffn1_w_in", "ffn1_w_out", "mix_w_in", "conv_w_out", "ssm_w_glu", "mix_w_out",
       "ffn2_w_in", "ffn2_w_out", "ple_w_in", "ple_w_gate"]
SMALL = ["ln1_g", "ln1_b", "conv_w", "conv_b", "ssm_lam_re", "ssm_lam_im", "ssm_log_step", "ssm_b_re", "ssm_b_im",
         "ssm_c_re", "ssm_c_im", "ssm_d", "ln2_g", "ln2_b", "ln3_g", "ln3_b", "ln4_g", "ln4_b"]
WEIGHTS = ["ffn1_w_in", "ffn1_w_out", "ln1_g", "ln1_b", "mix_w_in", "conv_w", "conv_b", "conv_w_out",
           "ssm_lam_re", "ssm_lam_im", "ssm_log_step", "ssm_b_re", "ssm_b_im", "ssm_c_re", "ssm_c_im", "ssm_d",
           "ssm_w_glu", "mix_w_out", "ln2_g", "ln2_b", "ffn2_w_in", "ffn2_w_out", "ln3_g", "ln3_b",
           "ple_w_in", "ple_w_gate", "ln4_g", "ln4_b"]


class _NoComm:
    def __init__(self, W):
        self.W, self.G, self.raw = dict(W), {}, None

    def carry(self, name):
        return ()

    def landed(self, name, got):
        pass

    def grad(self, name, g4):
        self.G[name] = g4

    def small(self, raw):
        self.raw = raw


def _local_step(x, p, target, sp, sched, tm_ffn, tm_mix):
    W = sched.W
    abr, abi, bbr, bbi = _zoh(sp["ssm_lam_re"], sp["ssm_lam_im"], sp["ssm_log_step"], sp["ssm_b_re"], sp["ssm_b_im"])
    wb_re, wb_im = _wb_blocks(bbr), _wb_blocks(bbi)
    wc_re4, wc_im4 = _wc_blocks(sp["ssm_c_re"]), _wc_blocks(-sp["ssm_c_im"])
    a_re, a_im = abr.reshape(1, LANES), abi.reshape(1, LANES)
    dvec = sp["ssm_d"].reshape(1, SSM)

    def run(fn, name, *args, **kw):
        outs, got = fn(*args, comm=sched.carry(name), **kw)
        sched.landed(name, got)
        return outs

    def dw(name, wname, a, b, tk, tn, shape4, shard_cols=None, interleaved=False):
        out, got = _mm_tn(a, b, tk, tn, name, shard_cols=shard_cols, interleaved=interleaved,
                          comm=sched.carry(name))
        sched.landed(name, got)
        sched.grad(wname, out.reshape(shape4))

    xb = x.astype(BF16)
    h1, r1, x1, x1b = run(_ffn_fwd, "ffn1_fwd", x, xb, W["ffn1_w_in"], W["ffn1_w_out"].reshape(2, FFH, D),
                          sp["ln1_g"], sp["ln1_b"], tm_ffn, "ffn1_fwd")
    conv_w = W["conv_w"][:, 0:3, :].transpose(1, 0, 2).reshape(3, CONV)
    pc, z_b, yin_b, su, su_b, g_conv, g_ssm, y_conv = _mix_fwd_a(
        x1b, W["mix_w_in"], conv_w, sp["conv_b"], W["conv_w_out"], tm_mix)
    st_re, st_im = run(_s5_scan_fwd, "s5_scan_fwd", su_b, wb_re, wb_im, a_re, a_im)
    w_mo = W["mix_w_out"].reshape(D, D)
    s, sg_b, ga, gb, merged_b, r2, x2, x2b = run(
        _mix_fwd_b, "mix_fwd_b", st_re, st_im, wc_re4, wc_im4, su, dvec, W["ssm_w_glu"], g_conv, g_ssm, y_conv,
        w_mo, x1, sp["ln2_g"], sp["ln2_b"], tm_mix)
    w2o2 = W["ffn2_w_out"].reshape(2, FFH, D)
    h2, r3, x3, x3b = run(_ffn_fwd, "ffn2_fwd", x2, x2b, W["ffn2_w_in"], w2o2, sp["ln3_g"], sp["ln3_b"], tm_ffn,
                          "ffn2_fwd")
    loss_part, dx3, p_b, dpw_b, dgt_b, dg4, db4 = _ple_loss(
        x3, x3b, p, W["ple_w_in"], W["ple_w_gate"].reshape(D, D), sp["ln4_g"], sp["ln4_b"], target, tm_mix)

    dw("dw_ple_gate", "ple_w_gate", x3b, dgt_b, 512, 1024, (4, 256, D))
    dw("dw_ple_in", "ple_w_in", p_b, dpw_b, 256, 256, (4, 256, 256), shard_cols=256)
    dx2, dh2, a2_b, df2_b, dg3, db3 = run(_ffn_bwd, "ffn2_bwd", dx3, r3, sp["ln3_g"], h2, W["ffn2_w_in"], w2o2,
                                          tm_ffn, "ffn2_bwd")
    dw("dw_ffn2_in", "ffn2_w_in", x2b, dh2, 512, FFH, (4, D, FFH), shard_cols=FFH, interleaved=True)
    dw("dw_ffn2_out", "ffn2_w_out", a2_b, df2_b, FFH, 1024, (4, FF // 4, D))
    (dres, dmix_b, dgl_b, ds_b, du_dir, gs_re, gs_im, dyc_b, dproj, dg2, db2, dd) = run(
        _mix_bwd_b, "mix_bwd_b", dx2, r2, sp["ln2_g"], w_mo, g_conv, g_ssm, y_conv, ga, gb, s, su, dvec,
        W["ssm_w_glu"], wc_re4, wc_im4, tm_mix)
    dw("dw_mix_out", "mix_w_out", merged_b, dmix_b, 512, 1024, (4, 256, D))
    dw("dw_glu", "ssm_w_glu", sg_b, dgl_b, 512, 512, (4, SSM, 512), shard_cols=512)
    dsu_ssm, dwb_re, dwb_im, dwc_re, dwc_im, da_re, da_im = run(
        _s5_scan_bwd, "s5_scan_bwd", gs_re, gs_im, st_re, st_im, su_b, ds_b, wb_re, wb_im, a_re, a_im)
    dw("dw_conv_out", "conv_w_out", yin_b, dyc_b, 512, 256, (4, CONV, 256), shard_cols=256)
    dproj, dx1, dcw8, dcb = run(_mix_bwd_a, "mix_bwd_a", dyc_b, W["conv_w_out"], pc, z_b, conv_w, dsu_ssm,
                                du_dir, dproj, dres, W["mix_w_in"], tm_mix)
    dw("dw_mix_in", "mix_w_in", x1b, dproj, 512, 1024, (4, D, D), shard_cols=1024)
    dx0, dh1, a1_b, df1_b, dg1, db1 = run(_ffn_bwd, "ffn1_bwd", dx1, r1, sp["ln1_g"], h1, W["ffn1_w_in"],
                                          W["ffn1_w_out"].reshape(2, FFH, D), tm_ffn, "ffn1_bwd")
    sched.small(dict(
        ln1_g=dg1, ln1_b=db1, ln2_g=dg2, ln2_b=db2, ln3_g=dg3, ln3_b=db3, ln4_g=dg4, ln4_b=db4,
        conv_w=dcw8[0:3], conv_b=dcb,
        a_re=da_re.reshape(GROUPS, STATE), a_im=da_im.reshape(GROUPS, STATE),
        bb_re=_wb_diag(dwb_re), bb_im=_wb_diag(dwb_im),
        ssm_c_re=_wc_diag(dwc_re), ssm_c_im=-_wc_diag(dwc_im), ssm_d=dd.reshape(GROUPS, 16),
        loss=loss_part[0:1, 0]))
    dw("dw_ffn1_in", "ffn1_w_in", xb, dh1, 512, FFH, (4, D, FFH), shard_cols=FFH, interleaved=True)
    dw("dw_ffn1_out", "ffn1_w_out", a1_b, df1_b, FFH, 1024, (4, FF // 4, D))
    return loss_part[0, 0], dx0


RAW_ORDER = ["ln1_g", "ln1_b", "ln2_g", "ln2_b", "ln3_g", "ln3_b", "ln4_g", "ln4_b", "conv_w", "conv_b",
             "a_re", "a_im", "bb_re", "bb_im", "ssm_c_re", "ssm_c_im", "ssm_d", "loss"]

GATHER_FIRST = ["ffn1_w_in", "ffn1_w_out"]
GATHER_AT = {"ffn1_fwd": ["mix_w_in", "conv_w_out", "conv_w", "ssm_w_glu", "mix_w_out"],
             "s5_scan_fwd": ["ffn2_w_in"], "mix_fwd_b": ["ffn2_w_out"], "ffn2_fwd": ["ple_w_in", "ple_w_gate"]}
REDUCE_GROUP = {"ple": ["ple_w_gate", "ple_w_in"], "ffn2": ["ffn2_w_in", "ffn2_w_out"],
                "mix": ["mix_w_out", "ssm_w_glu", "conv_w_out", "mix_w_in"], "ffn1": ["ffn1_w_in", "ffn1_w_out"]}
REDUCE_AT = {"ffn2_bwd": [("swap", "ple")], "dw_ffn2_in": [("exchange", "ple")],
             "mix_bwd_b": [("swap", "ffn2"), ("join", "ple")], "s5_scan_bwd": [("exchange", "ffn2")],
             "mix_bwd_a": [("join", "ffn2")], "ffn1_bwd": [("swap", "mix")],
             "dw_ffn1_in": [("small", None)]}
BEGIN_AT = {"dw_ffn1_in": [("exchange", "mix")]}
LAST_GROUP = "ffn1"


class _Sched:
    def __init__(self, bufs, cidx, meidx):
        self.bufs, self.cidx, self.meidx = bufs, cidx, meidx
        self.W, self.G, self.raw, self.small_buf = {}, {}, None, None
        self.got1, self.p32, self.pbf, self.got2, self.half, self.theirs = {}, {}, {}, {}, {}, {}
        self._open, self._split = [], {}
        self._standalone("gather_ffn1", [("gather", GATHER_FIRST)])

    def _payload(self, stage, key):
        if stage == "gather":
            return _gather_payload([self.bufs[n] for n in key])
        if stage == "small":
            return _allgather_payload(_pack([self.raw[k] for k in RAW_ORDER]))
        names = REDUCE_GROUP[key]
        if stage == "swap":
            return _swap_payload([self.G[n] for n in names])
        if stage == "exchange":
            for n in names:
                self.p32[n], self.pbf[n] = _pair_sum(self.cidx, self.G[n], self.got1[n], "pair_sum_" + n)
            return _exchange_payload([self.pbf[n] for n in names])
        for n in names:
            self.half[n] = _chip_sum(self.meidx, self.p32[n], self.got2[n], "chip_sum_" + n)
        return _join_payload([self.half[n] for n in names])

    def _store(self, stages, got):
        for (stage, key), outs in zip(stages, got):
            if stage == "gather":
                self.W.update(zip(key, outs))
            elif stage == "small":
                self.small_buf = outs[0]
            else:
                {"swap": self.got1, "exchange": self.got2, "join": self.theirs}[stage].update(
                    zip(REDUCE_GROUP[key], outs))

    def _standalone(self, name, stages):
        self._store(stages, _comm_call(name, [self._payload(s, k) for s, k in stages]))

    def carry(self, name):
        for stage, key in BEGIN_AT.get(name, []):
            self._begin(stage, key)
        self._open = [("gather", GATHER_AT[name])] if name in GATHER_AT else []
        self._open += REDUCE_AT.get(name, [])
        return tuple(self._payload(s, k) for s, k in self._open)

    def landed(self, name, got):
        self._store(self._open, got)

    def grad(self, name, g4):
        self.G[name] = g4

    def small(self, raw):
        self.raw = raw

    def _begin(self, stage, key):
        p = self._payload(stage, key)
        self._split[stage, key] = (p, _split_start(p, "%s_%s_start" % (stage, key)))
        return self._split[stage, key][1][3]

    def _end(self, stage, key, after):
        p, handle = self._split.pop((stage, key))
        self._store([(stage, key)], [_split_wait(p, handle, after, "%s_%s_wait" % (stage, key))])

    def tail_begin(self):
        return self._begin("swap", LAST_GROUP)

    def tail_mid(self, after):
        self._end("swap", LAST_GROUP, after)
        token = self._begin("exchange", LAST_GROUP)
        self._end("exchange", "mix", [token])
        self._standalone("reduce_tail_join_mix", [("join", "mix")])
        return token

    def tail_end(self, after):
        self._end("exchange", LAST_GROUP, after)
        self._standalone("reduce_tail_join", [("join", LAST_GROUP)])


def _small_grads(raw_sum, sp):
    _, vjp = jax.vjp(_zoh, sp["ssm_lam_re"], sp["ssm_lam_im"], sp["ssm_log_step"], sp["ssm_b_re"], sp["ssm_b_im"])
    d_lre, d_lim, d_ls, d_bre, d_bim = vjp((raw_sum["a_re"], raw_sum["a_im"], raw_sum["bb_re"], raw_sum["bb_im"]))
    g = {k: raw_sum[k] for k in ("ln1_g", "ln1_b", "ln2_g", "ln2_b", "ln3_g", "ln3_b", "ln4_g", "ln4_b",
                                 "conv_w", "conv_b", "ssm_c_re", "ssm_c_im", "ssm_d")}
    g.update(ssm_lam_re=d_lre, ssm_lam_im=d_lim, ssm_log_step=d_ls, ssm_b_re=d_bre, ssm_b_im=d_bim)
    return g


def kernel(x, p, ffn1_w_in, ffn1_w_out, ln1_g, ln1_b, mix_w_in, conv_w, conv_b, conv_w_out, ssm_lam_re, ssm_lam_im, ssm_log_step, ssm_b_re, ssm_b_im, ssm_c_re, ssm_c_im, ssm_d, ssm_w_glu, mix_w_out, ln2_g, ln2_b, ffn2_w_in, ffn2_w_out, ln3_g, ln3_b, ple_w_in, ple_w_gate, ln4_g, ln4_b, loss_target, m_ffn1_w_in, m_ffn1_w_out, m_ln1_g, m_ln1_b, m_mix_w_in, m_conv_w, m_conv_b, m_conv_w_out, m_ssm_lam_re, m_ssm_lam_im, m_ssm_log_step, m_ssm_b_re, m_ssm_b_im, m_ssm_c_re, m_ssm_c_im, m_ssm_d, m_ssm_w_glu, m_mix_w_out, m_ln2_g, m_ln2_b, m_ffn2_w_in, m_ffn2_w_out, m_ln3_g, m_ln3_b, m_ple_w_in, m_ple_w_gate, m_ln4_g, m_ln4_b, v_ffn1_w_in, v_ffn1_w_out, v_ln1_g, v_ln1_b, v_mix_w_in, v_conv_w, v_conv_b, v_conv_w_out, v_ssm_lam_re, v_ssm_lam_im, v_ssm_log_step, v_ssm_b_re, v_ssm_b_im, v_ssm_c_re, v_ssm_c_im, v_ssm_d, v_ssm_w_glu, v_mix_w_out, v_ln2_g, v_ln2_b, v_ffn2_w_in, v_ffn2_w_out, v_ln3_g, v_ln3_b, v_ple_w_in, v_ple_w_gate, v_ln4_g, v_ln4_b):
    args = dict(locals())
    w = {n: args[n] for n in WEIGHTS}
    m = {n: args["m_" + n] for n in WEIGHTS}
    v = {n: args["v_" + n] for n in WEIGHTS}
    _, _, c, me = _where()
    cidx = jnp.reshape(c, (1,)).astype(jnp.int32)
    meidx = jnp.reshape(me, (1,)).astype(jnp.int32)

    bufs = {n: _slot_cast(meidx, w[n][0], BF16, "cast_" + n) for n in BIG}
    bufs["conv_w"] = _slot_cast(meidx, jnp.pad(conv_w[0], ((0, 13), (0, 0))), F32, "cast_conv_w")
    sched = _Sched(bufs, cidx, meidx)

    sp = {n: (w[n] if w[n].ndim == 2 and n != "ssm_log_step" else w[n][0]) for n in SMALL if n != "conv_w"}
    loss_part, dx0 = _local_step(x[0], p[0, 0], loss_target[0], sp, sched, 256, 256)
    out_g, out_d, out_m, out_v = {}, {}, {}, {}

    def big_adamw(names, token):
        for n in names:
            g, dl, mn, vn = _adamw_pair(cidx, w[n][0], sched.half[n], sched.theirs[n], m[n][0], v[n][0], token,
                                        "adamw_" + n)
            out_g[n], out_d[n], out_m[n], out_v[n] = g[None], dl[None], mn[None], vn[None]

    first = REDUCE_GROUP["ple"] + REDUCE_GROUP["ffn2"]
    big_adamw(first, sched.tail_begin())
    token = sched.tail_mid([out_v[n] for n in first])

    raw_shapes = [sched.raw[k].shape for k in RAW_ORDER]
    raw_sum = dict(zip(RAW_ORDER, _unpack(_sum8(sched.small_buf, token), raw_shapes)))
    loss = raw_sum["loss"][0]
    sg = _small_grads(raw_sum, sp)
    sg["conv_w"] = lax.dynamic_slice_in_dim(sg["conv_w"], me * 128, 128, axis=1)
    small_shapes = [w[n].shape for n in SMALL]
    gp = _pack([sg[n] for n in SMALL])
    d_s, m_s, v_s = _adamw(_pack([w[n] for n in SMALL]), gp, _pack([m[n] for n in SMALL]),
                           _pack([v[n] for n in SMALL]), "adamw_small")

    for n, a, b_, c_, d_ in zip(SMALL, _unpack(gp, small_shapes), _unpack(d_s, small_shapes),
                                _unpack(m_s, small_shapes), _unpack(v_s, small_shapes)):
        out_g[n], out_d[n], out_m[n], out_v[n] = a, b_, c_, d_
    big_adamw(REDUCE_GROUP["mix"], token)
    sched.tail_end([d_s] + [out_v[n] for n in REDUCE_GROUP["mix"]])
    big_adamw(REDUCE_GROUP[LAST_GROUP], token)

    return (loss, dx0[None], *[out_g[n] for n in WEIGHTS], *[out_d[n] for n in WEIGHTS],
            *[out_m[n] for n in WEIGHTS], *[out_v[n] for n in WEIGHTS])
```

```python
import functools
import math

import jax
import jax.numpy as jnp
import numpy as np
from jax import lax
from jax.experimental import pallas as pl
from jax.experimental.pallas import tpu as pltpu

F32, BF16 = jnp.float32, jnp.bfloat16
D = 1024
FF = 2816
FFH = FF // 2
CONV = 512
SSM = 512
GROUPS = 32
STATE = 64
LANES = GROUPS * STATE
SCAN_W = 256
SCAN_R = 256
ALPHA = 2.0 ** 0.25
LN_EPS = 1e-5
GELU_C = math.sqrt(2.0 / math.pi)
B1, B2, LR, EPS, WD, STEP = 0.9, 0.999, 0.001, 1e-8, 0.01, 10
MESH = pl.DeviceIdType.MESH
ANY = pl.BlockSpec(memory_space=pl.ANY)
VMEM_FULL = pl.BlockSpec(memory_space=pltpu.VMEM)


def _cp(vmem_mb=48, n_axes=1):
    return pltpu.CompilerParams(vmem_limit_bytes=vmem_mb << 20,
                                dimension_semantics=("arbitrary",) * n_axes)


def _hbm(*arrs):
    return [pltpu.with_memory_space_constraint(a, pltpu.HBM) for a in arrs]


def _hbm_out(shapes):
    if isinstance(shapes, (list, tuple)):
        return [pltpu.HBM(s.shape, s.dtype) for s in shapes]
    return pltpu.HBM(shapes.shape, shapes.dtype)


def _nn(a, b):
    return jnp.dot(a, b, preferred_element_type=F32)


def _nt(a, b):
    return lax.dot_general(a, b, (((1,), (1,)), ((), ())), preferred_element_type=F32)


def _tn(a, b):
    return lax.dot_general(a, b, (((0,), (0,)), ((), ())), preferred_element_type=F32)


def _sig(v):
    return jax.nn.sigmoid(v)


def _ln_stats(r):
    mu = jnp.mean(r, axis=-1, keepdims=True)
    xc = r - mu
    var = jnp.mean(xc * xc, axis=-1, keepdims=True)
    rstd = lax.rsqrt(var + LN_EPS)
    return xc * rstd, rstd


def _ln_bwd(dy, r, g):
    xhat, rstd = _ln_stats(r)
    dyg = dy * g
    m1 = jnp.mean(dyg, axis=-1, keepdims=True)
    m2 = jnp.mean(dyg * xhat, axis=-1, keepdims=True)
    return rstd * (dyg - m1 - xhat * m2), xhat


def _rowsum(v):
    return jnp.sum(v, axis=0, keepdims=True)


class _Payload:
    def __init__(self, operands, outs, aliases, sems, start, finish):
        self.operands, self.outs, self.aliases, self.sems = list(operands), list(outs), dict(aliases), list(sems)
        self.start, self.finish = start, finish


def _split(flat, comm, attr):
    out, i = [], 0
    for p in comm:
        n = len(getattr(p, attr))
        out.append(list(flat[i:i + n]))
        i += n
    return out


def _run_comm(comm, which, cin, cout, csem):
    for p, a, b, s in zip(comm, _split(cin, comm, "operands"), _split(cout, comm, "outs"), _split(csem, comm, "sems")):
        getattr(p, which)(a, b, s)


def _pcall(body, *, name, grid, in_specs, out_specs, out_shape, operands, scratch=(), vmem_mb=48, aliases=None,
           comm=()):
    ni, no, ns = len(in_specs), len(out_specs), len(scratch)
    c_ops = [a for p in comm for a in p.operands]
    c_outs = [s for p in comm for s in p.outs]
    c_sems = [s for p in comm for s in p.sems]
    io = dict(aliases or {})
    off_i, off_o = ni, no
    for p in comm:
        for a, b in p.aliases.items():
            io[off_i + a] = off_o + b
        off_i += len(p.operands)
        off_o += len(p.outs)

    def wrapped(*refs):
        ins, cin = refs[:ni], refs[ni:ni + len(c_ops)]
        o0 = ni + len(c_ops)
        outs, cout = refs[o0:o0 + no], refs[o0 + no:o0 + no + len(c_outs)]
        s0 = o0 + no + len(c_outs)
        scr, csem = refs[s0:s0 + ns], refs[s0 + ns:]
        if comm:
            first = functools.reduce(jnp.logical_and, [pl.program_id(a) == 0 for a in range(len(grid))])
            pl.when(first)(lambda: _run_comm(comm, "start", cin, cout, csem))
        body(*ins, *outs, *scr)
        if comm:
            last = functools.reduce(jnp.logical_and, [pl.program_id(a) == grid[a] - 1 for a in range(len(grid))])
            pl.when(last)(lambda: _run_comm(comm, "finish", cin, cout, csem))

    res = pl.pallas_call(
        wrapped, name=name, grid=grid,
        in_specs=list(in_specs) + [ANY] * len(c_ops), out_specs=list(out_specs) + [ANY] * len(c_outs),
        out_shape=_hbm_out(list(out_shape) + c_outs), scratch_shapes=list(scratch) + c_sems,
        input_output_aliases=io,
        compiler_params=pltpu.CompilerParams(vmem_limit_bytes=vmem_mb << 20,
                                             dimension_semantics=("arbitrary",) * len(grid),
                                             has_side_effects=bool(comm)),
    )(*_hbm(*operands, *c_ops))
    return list(res[:no]), _split(res[no:], comm, "outs")


def _comm_call(name, comm):
    c_ops = [a for p in comm for a in p.operands]
    c_outs = [s for p in comm for s in p.outs]
    c_sems = [s for p in comm for s in p.sems]
    io, off_i, off_o = {}, 0, 0
    for p in comm:
        for a, b in p.aliases.items():
            io[off_i + a] = off_o + b
        off_i += len(p.operands)
        off_o += len(p.outs)

    def body(*refs):
        cin, cout = refs[:len(c_ops)], refs[len(c_ops):len(c_ops) + len(c_outs)]
        csem = refs[len(c_ops) + len(c_outs):]
        _run_comm(comm, "start", cin, cout, csem)
        _run_comm(comm, "finish", cin, cout, csem)

    res = pl.pallas_call(
        body, name=name, in_specs=[ANY] * len(c_ops), out_specs=[ANY] * len(c_outs), out_shape=_hbm_out(c_outs),
        scratch_shapes=c_sems, input_output_aliases=io,
        compiler_params=pltpu.CompilerParams(has_side_effects=True),
    )(*_hbm(*c_ops))
    return _split(res, comm, "outs")


def _ffn_fwd(x, xb, w_in4, w_out2, g, b, tm, name, comm=()):
    T = x.shape[0]

    def body(x_ref, xb_ref, wg_ref, wu_ref, wo_ref, g_ref, b_ref, h_ref, r_ref, xo_ref, xob_ref, acc):
        k = pl.program_id(1)
        xv = xb_ref[...]
        gt = _nn(xv, wg_ref[...])
        up = _nn(xv, wu_ref[...])
        a = (gt * _sig(gt) * up).astype(BF16)
        h_ref[:, 0:FFH] = gt.astype(BF16)
        h_ref[:, FFH:2 * FFH] = up.astype(BF16)
        acc[...] = jnp.where(k == 0, 0.0, acc[...]) + _nn(a, wo_ref[...])

        @pl.when(k == 1)
        def _():
            r = ALPHA * x_ref[...] + 0.5 * acc[...]
            xhat, _ = _ln_stats(r)
            xo = xhat * g_ref[...] + b_ref[...]
            r_ref[...] = r
            xo_ref[...] = xo
            xob_ref[...] = xo.astype(BF16)

    tok = pl.BlockSpec((tm, D), lambda i, k: (i, 0))
    vec = pl.BlockSpec((1, D), lambda i, k: (0, 0))
    return _pcall(
        body, name=name, grid=(T // tm, 2),
        in_specs=[tok, tok,
                  pl.BlockSpec((None, D, FFH), lambda i, k: (k, 0, 0)),
                  pl.BlockSpec((None, D, FFH), lambda i, k: (k + 2, 0, 0)),
                  pl.BlockSpec((None, FFH, D), lambda i, k: (k, 0, 0)),
                  vec, vec],
        out_specs=[pl.BlockSpec((tm, FF), lambda i, k: (i, k)), tok, tok, tok],
        out_shape=[jax.ShapeDtypeStruct((T, 2 * FF), BF16), jax.ShapeDtypeStruct((T, D), F32),
                   jax.ShapeDtypeStruct((T, D), F32), jax.ShapeDtypeStruct((T, D), BF16)],
        scratch=[pltpu.VMEM((tm, D), F32)], vmem_mb=56, comm=comm,
        operands=(x, xb, w_in4, w_in4, w_out2, g, b))


def _ffn_bwd(dy, r, g, h, w_in4, w_out2, tm, name, comm=()):
    T = dy.shape[0]

    def body(dy_ref, r_ref, g_ref, h_ref, wg_ref, wu_ref, wo_ref,
             dx_ref, dh_ref, a_ref, df_ref, dg_ref, db_ref, acc, dr_s, dfb_s):
        i, k = pl.program_id(0), pl.program_id(1)

        @pl.when(k == 0)
        def _():
            dyv = dy_ref[...]
            dr, xhat = _ln_bwd(dyv, r_ref[...], g_ref[...])
            pg, pb = _rowsum(dyv * xhat), _rowsum(dyv)

            @pl.when(i == 0)
            def _():
                dg_ref[...] = pg
                db_ref[...] = pb

            @pl.when(i > 0)
            def _():
                dg_ref[...] += pg
                db_ref[...] += pb

            dr_s[...] = dr
            dfb = (0.5 * dr).astype(BF16)
            dfb_s[...] = dfb
            df_ref[...] = dfb

        da = _nt(dfb_s[...], wo_ref[...])
        gt = h_ref[:, 0:FFH].astype(F32)
        up = h_ref[:, FFH:2 * FFH].astype(F32)
        sg = _sig(gt)
        silu = gt * sg
        dgate = (da * up * (sg * (1.0 + gt * (1.0 - sg)))).astype(BF16)
        dup = (da * silu).astype(BF16)
        a_ref[...] = (silu * up).astype(BF16)
        dh_ref[:, 0:FFH] = dgate
        dh_ref[:, FFH:2 * FFH] = dup
        acc[...] = jnp.where(k == 0, 0.0, acc[...]) + _nt(dgate, wg_ref[...]) + _nt(dup, wu_ref[...])

        @pl.when(k == 1)
        def _():
            dx_ref[...] = ALPHA * dr_s[...] + acc[...]

    tok = pl.BlockSpec((tm, D), lambda i, k: (i, 0))
    vec = pl.BlockSpec((1, D), lambda i, k: (0, 0))
    wide = pl.BlockSpec((tm, FF), lambda i, k: (i, k))
    return _pcall(
        body, name=name, grid=(T // tm, 2),
        in_specs=[tok, tok, vec, wide,
                  pl.BlockSpec((None, D, FFH), lambda i, k: (k, 0, 0)),
                  pl.BlockSpec((None, D, FFH), lambda i, k: (k + 2, 0, 0)),
                  pl.BlockSpec((None, FFH, D), lambda i, k: (k, 0, 0))],
        out_specs=[tok, wide, pl.BlockSpec((tm, FFH), lambda i, k: (i, k)), tok, vec, vec],
        out_shape=[jax.ShapeDtypeStruct((T, D), F32), jax.ShapeDtypeStruct((T, 2 * FF), BF16),
                   jax.ShapeDtypeStruct((T, FF), BF16), jax.ShapeDtypeStruct((T, D), BF16),
                   jax.ShapeDtypeStruct((1, D), F32), jax.ShapeDtypeStruct((1, D), F32)],
        scratch=[pltpu.VMEM((tm, D), F32), pltpu.VMEM((tm, D), F32), pltpu.VMEM((tm, D), BF16)],
        vmem_mb=56, comm=comm, operands=(dy, r, g, h, w_in4, w_in4, w_out2))


def _mm_tn(a, b, tk, tn, name, shard_cols=None, interleaved=False, comm=()):
    T, K = a.shape
    N = b.shape[1]

    def body(a_ref, b_ref, o_ref):
        o_ref[...] = _tn(a_ref[...], b_ref[...])

    if shard_cols is None:
        out_shape = jax.ShapeDtypeStruct((K, N), F32)
        out_spec = pl.BlockSpec((tk, tn), lambda ki, nj: (ki, nj))
    else:
        per = shard_cols // tn

        def shard(nj):
            blk = nj // per
            return (blk % 2) * 2 + blk // 2 if interleaved else blk

        out_shape = jax.ShapeDtypeStruct((N // shard_cols, K, shard_cols), F32)
        out_spec = pl.BlockSpec((None, tk, tn), lambda ki, nj: (shard(nj), ki, nj % per))
    (out,), got = _pcall(
        body, name=name, grid=(K // tk, N // tn),
        in_specs=[pl.BlockSpec((T, tk), lambda ki, nj: (0, ki)), pl.BlockSpec((T, tn), lambda ki, nj: (0, nj))],
        out_specs=[out_spec], out_shape=[out_shape], comm=comm, operands=(a, b))
    return out, got


def _mix_fwd_a(xb, w_mix4, conv_w, conv_b, w_co4, tm):
    T = xb.shape[0]

    def body(xb_ref, w_ref, cw_ref, cb_ref, wco_ref,
             pc_ref, z_ref, yin_ref, su_ref, sub_ref, gc_ref, gs_ref, yc_ref, qbuf):
        @pl.when(pl.program_id(0) == 0)
        def _():
            qbuf[pl.ds(0, 8), :] = jnp.zeros((8, CONV), F32)

        xv = xb_ref[...]
        p0 = _nn(xv, w_ref[0])
        p1 = _nn(xv, w_ref[1])
        gc_ref[...] = _nn(xv, w_ref[2])
        gs_ref[...] = _nn(xv, w_ref[3])
        cbv, ccv = p0[:, :CONV], p0[:, CONV:]
        chv, suv = p1[:, :CONV], p1[:, CONV:]
        q = ccv * chv
        qbuf[pl.ds(8, tm), :] = q
        cw = cw_ref[...]
        z = (cw[2:3] * q + cw[1:2] * qbuf[pl.ds(7, tm), :] + cw[0:1] * qbuf[pl.ds(6, tm), :]
             + cb_ref[...])
        qbuf[pl.ds(0, 8), :] = q[tm - 8:tm]
        yin = (cbv * z).astype(BF16)
        pc_ref[:, 0:CONV] = cbv.astype(BF16)
        pc_ref[:, CONV:2 * CONV] = ccv.astype(BF16)
        pc_ref[:, 2 * CONV:3 * CONV] = chv.astype(BF16)
        z_ref[...] = z.astype(BF16)
        yin_ref[...] = yin
        su_ref[...] = suv
        sub_ref[...] = suv.astype(BF16)
        for k in range(4):
            yc_ref[:, 256 * k:256 * (k + 1)] = _nn(yin, wco_ref[k])

    def tok(n):
        return pl.BlockSpec((tm, n), lambda i: (i, 0))

    def full(shape):
        return pl.BlockSpec(shape, lambda i: (0,) * len(shape))

    return pl.pallas_call(
        body, name="mix_fwd_a", grid=(T // tm,),
        in_specs=[tok(D), full((4, D, D)), full((3, CONV)), full((1, CONV)), full((4, CONV, 256))],
        out_specs=[tok(3 * CONV), tok(CONV), tok(CONV), tok(SSM), tok(SSM), tok(D), tok(D), tok(D)],
        out_shape=_hbm_out([jax.ShapeDtypeStruct((T, 3 * CONV), BF16), jax.ShapeDtypeStruct((T, CONV), BF16),
                            jax.ShapeDtypeStruct((T, CONV), BF16), jax.ShapeDtypeStruct((T, SSM), F32),
                            jax.ShapeDtypeStruct((T, SSM), BF16), jax.ShapeDtypeStruct((T, D), F32),
                            jax.ShapeDtypeStruct((T, D), F32), jax.ShapeDtypeStruct((T, D), F32)]),
        scratch_shapes=[pltpu.VMEM((tm + 8, CONV), F32)],
        compiler_params=_cp(56, 1),
    )(*_hbm(xb, w_mix4, conv_w, conv_b, w_co4))


def _scan_inplace(bre, bim, ar, ai, T, rev):
    R = SCAN_R
    if rev:
        ai = -ai
    d = 1
    while d < T:
        if d < 8:
            def step(i, _, d=d, ar=ar, ai=ai):
                c = i if rev else T // R - 1 - i
                t0 = pl.multiple_of(c * R, R)
                if rev:
                    wr = bre[pl.ds(t0 + 8, R + 8), :]
                    wi = bim[pl.ds(t0 + 8, R + 8), :]
                    shr = pltpu.roll(wr, R + 8 - d, 0)[0:R]
                    shi = pltpu.roll(wi, R + 8 - d, 0)[0:R]
                    cr, ci = wr[0:R], wi[0:R]
                else:
                    wr = bre[pl.ds(t0, R + 8), :]
                    wi = bim[pl.ds(t0, R + 8), :]
                    shr = pltpu.roll(wr, d, 0)[8:8 + R]
                    shi = pltpu.roll(wi, d, 0)[8:8 + R]
                    cr, ci = wr[8:8 + R], wi[8:8 + R]
                bre[pl.ds(t0 + 8, R), :] = cr + ar * shr - ai * shi
                bim[pl.ds(t0 + 8, R), :] = ci + ar * shi + ai * shr
                return 0

            lax.fori_loop(0, T // R, step, 0)
        else:
            def upd(lo, n, d=d, ar=ar, ai=ai):
                src = lo + d if rev else lo - d
                if not isinstance(lo, int):
                    lo, src = pl.multiple_of(lo + 8, 8), pl.multiple_of(src + 8, 8)
                else:
                    lo, src = lo + 8, src + 8
                cr = bre[pl.ds(lo, n), :]
                ci = bim[pl.ds(lo, n), :]
                shr = bre[pl.ds(src, n), :]
                shi = bim[pl.ds(src, n), :]
                bre[pl.ds(lo, n), :] = cr + ar * shr - ai * shi
                bim[pl.ds(lo, n), :] = ci + ar * shi + ai * shr

            nfull = (T - d) // R if d >= R else T // R - 1

            def step(i, _, upd=upd, d=d):
                if rev:
                    t0 = i * R
                else:
                    t0 = T - (i + 1) * R
                upd(t0, R)
                return 0

            if nfull > 0:
                lax.fori_loop(0, nfull, step, 0)
            if d < R:
                if rev:
                    upd(T - R, R - d)
                else:
                    upd(d, R - d)
        ar, ai = ar * ar - ai * ai, 2.0 * ar * ai
        d *= 2


def _scan_specs(T):
    W = SCAN_W
    lane = pl.BlockSpec((T, W), lambda j: (0, j))
    col = pl.BlockSpec((T, 128), lambda j: (0, j // 2))
    wb = pl.BlockSpec((None, 128, W), lambda j: (j, 0, 0))
    wc = pl.BlockSpec((None, W, 128), lambda j: (j, 0, 0))
    vec = pl.BlockSpec((1, W), lambda j: (0, j))
    return lane, col, wb, wc, vec


def _s5_scan_fwd(su_b, wb_re, wb_im, a_re, a_im, comm=()):
    T = su_b.shape[0]
    W = SCAN_W

    def body(su_ref, wbr_ref, wbi_ref, ar_ref, ai_ref, sr_ref, si_ref, bre, bim):
        zero = jnp.zeros((8, W), F32)
        for buf in (bre, bim):
            buf[pl.ds(0, 8), :] = zero
            buf[pl.ds(T + 8, 8), :] = zero
        su = su_ref[...]
        bre[pl.ds(8, T), :] = _nn(su, wbr_ref[...])
        bim[pl.ds(8, T), :] = _nn(su, wbi_ref[...])
        _scan_inplace(bre, bim, ar_ref[...], ai_ref[...], T, rev=False)
        sr_ref[...] = bre[pl.ds(8, T), :]
        si_ref[...] = bim[pl.ds(8, T), :]

    lane, col, wb, wc, vec = _scan_specs(T)
    return _pcall(
        body, name="s5_scan_fwd", grid=(LANES // W,),
        in_specs=[col, wb, wb, vec, vec],
        out_specs=[lane, lane],
        out_shape=[jax.ShapeDtypeStruct((T, LANES), F32)] * 2,
        scratch=[pltpu.VMEM((T + 16, W), F32)] * 2, comm=comm,
        operands=(su_b, wb_re, wb_im, a_re, a_im))


def _gelu(s):
    th = jnp.tanh(GELU_C * (s + 0.044715 * s * s * s))
    return 0.5 * s * (1.0 + th), th


def _mix_fwd_b(st_re, st_im, wc_re4, wc_im4, su, dvec, w_glu4, g_conv, g_ssm, y_conv, w_mo, x1, g, b, tm, comm=()):
    T = su.shape[0]

    def body(sr_ref, si_ref, wcr_ref, wci_ref, su_ref, d_ref, wg_ref, gc_ref, gs_ref, yc_ref, wmo_ref,
             x_ref, g_ref, b_ref, s_ref, sgb_ref, ga_ref, gb_ref, mb_ref, r_ref, xo_ref, xob_ref):
        srb = sr_ref[...].astype(BF16)
        sib = si_ref[...].astype(BF16)
        ys = [_nn(srb[:, 512 * J:512 * (J + 1)], wcr_ref[J]) + _nn(sib[:, 512 * J:512 * (J + 1)], wci_ref[J])
              for J in range(4)]
        s = jnp.concatenate(ys, axis=1) + d_ref[...] * su_ref[...]
        sg, _ = _gelu(s)
        sgb = sg.astype(BF16)
        ga = jnp.concatenate([_nn(sgb, wg_ref[0]), _nn(sgb, wg_ref[1])], axis=1)
        gb = jnp.concatenate([_nn(sgb, wg_ref[2]), _nn(sgb, wg_ref[3])], axis=1)
        merged = _sig(gc_ref[...]) * yc_ref[...] + _sig(gs_ref[...]) * (ga * _sig(gb))
        mb = merged.astype(BF16)
        r = ALPHA * x_ref[...] + _nn(mb, wmo_ref[...])
        xhat, _ = _ln_stats(r)
        xo = xhat * g_ref[...] + b_ref[...]
        s_ref[...] = s
        sgb_ref[...] = sgb
        ga_ref[...] = ga
        gb_ref[...] = gb
        mb_ref[...] = mb
        r_ref[...] = r
        xo_ref[...] = xo
        xob_ref[...] = xo.astype(BF16)

    def tok(n):
        return pl.BlockSpec((tm, n), lambda i: (i, 0))

    def full(shape):
        return pl.BlockSpec(shape, lambda i: (0,) * len(shape))

    return _pcall(
        body, name="mix_fwd_b", grid=(T // tm,),
        in_specs=[tok(LANES), tok(LANES), full((4, 512, 128)), full((4, 512, 128)), tok(SSM), full((1, SSM)),
                  full((4, SSM, 512)), tok(D), tok(D), tok(D), full((D, D)), tok(D), full((1, D)), full((1, D))],
        out_specs=[tok(SSM), tok(SSM), tok(D), tok(D), tok(D), tok(D), tok(D), tok(D)],
        out_shape=[jax.ShapeDtypeStruct((T, SSM), F32), jax.ShapeDtypeStruct((T, SSM), BF16),
                   jax.ShapeDtypeStruct((T, D), F32), jax.ShapeDtypeStruct((T, D), F32),
                   jax.ShapeDtypeStruct((T, D), BF16), jax.ShapeDtypeStruct((T, D), F32),
                   jax.ShapeDtypeStruct((T, D), F32), jax.ShapeDtypeStruct((T, D), BF16)],
        vmem_mb=56, comm=comm,
        operands=(st_re, st_im, wc_re4, wc_im4, su, dvec, w_glu4, g_conv, g_ssm, y_conv, w_mo, x1, g, b))


def _ple_loss(x3, x3b, p, w_pi4, w_pg, g, b, target, tm):
    T = x3.shape[0]
    PD = p.shape[1]

    def body(x_ref, xb_ref, p_ref, wpi_ref, wpg_ref, g_ref, b_ref, t_ref,
             loss_ref, dx_ref, pb_ref, dpw_ref, dgt_ref, dg_ref, db_ref):
        i = pl.program_id(0)
        pb = p_ref[...].astype(BF16)
        pw = jnp.concatenate([_nn(pb, wpi_ref[k]) for k in range(4)], axis=1)
        gt = _nn(xb_ref[...], wpg_ref[...])
        sg = _sig(gt)
        r = ALPHA * x_ref[...] + pw * sg
        gv = g_ref[...]
        xhat, rstd = _ln_stats(r)
        err = xhat * gv + b_ref[...] - t_ref[...]
        lpart = jnp.zeros((1, 128), F32) + 0.5 * jnp.sum(jnp.mean(err * err, axis=-1, keepdims=True))
        dy = err * (1.0 / D)
        dyg = dy * gv
        m1 = jnp.mean(dyg, axis=-1, keepdims=True)
        m2 = jnp.mean(dyg * xhat, axis=-1, keepdims=True)
        dr = rstd * (dyg - m1 - xhat * m2)
        pg, pbias = _rowsum(dy * xhat), _rowsum(dy)

        @pl.when(i == 0)
        def _():
            loss_ref[...] = lpart
            dg_ref[...] = pg
            db_ref[...] = pbias

        @pl.when(i > 0)
        def _():
            loss_ref[...] += lpart
            dg_ref[...] += pg
            db_ref[...] += pbias

        dgt = (dr * pw * sg * (1.0 - sg)).astype(BF16)
        pb_ref[...] = pb
        dpw_ref[...] = (dr * sg).astype(BF16)
        dgt_ref[...] = dgt
        dx_ref[...] = ALPHA * dr + _nt(dgt, wpg_ref[...])

    def tok(n):
        return pl.BlockSpec((tm, n), lambda i: (i, 0))

    def full(shape):
        return pl.BlockSpec(shape, lambda i: (0,) * len(shape))

    return pl.pallas_call(
        body, name="ple_loss", grid=(T // tm,),
        in_specs=[tok(D), tok(D), tok(PD), full((4, PD, 256)), full((D, D)), full((1, D)), full((1, D)), tok(D)],
        out_specs=[full((1, 128)), tok(D), tok(PD), tok(D), tok(D), full((1, D)), full((1, D))],
        out_shape=_hbm_out([jax.ShapeDtypeStruct((1, 128), F32), jax.ShapeDtypeStruct((T, D), F32),
                            jax.ShapeDtypeStruct((T, PD), BF16), jax.ShapeDtypeStruct((T, D), BF16),
                            jax.ShapeDtypeStruct((T, D), BF16), jax.ShapeDtypeStruct((1, D), F32),
                            jax.ShapeDtypeStruct((1, D), F32)]),
        compiler_params=_cp(48, 1),
    )(*_hbm(x3, x3b, p, w_pi4, w_pg, g, b, target))


def _mix_bwd_b(dy, r2, g, w_mo, g_conv, g_ssm, y_conv, ga, gb, s, su, dvec, w_glu4, wc_re4, wc_im4, tm, comm=()):
    T = dy.shape[0]

    def body(dy_ref, r_ref, g_ref, wmo_ref, gc_ref, gs_ref, yc_ref, ga_ref, gb_ref, s_ref, su_ref, d_ref,
             wg_ref, wcr_ref, wci_ref,
             dres_ref, dmix_ref, dgl_ref, dsb_ref, dud_ref, gsr_ref, gsi_ref, dyc_ref, dp_ref,
             dg_ref, db_ref, dd_ref):
        i = pl.program_id(0)
        dyv = dy_ref[...]
        dr, xhat = _ln_bwd(dyv, r_ref[...], g_ref[...])
        dmix = dr.astype(BF16)
        dmerged = _nt(dmix, wmo_ref[...])
        sc, ss, sgb = _sig(gc_ref[...]), _sig(gs_ref[...]), _sig(gb_ref[...])
        gav = ga_ref[...]
        yssm = gav * sgb
        dgc = dmerged * yc_ref[...] * sc * (1.0 - sc)
        dgss = dmerged * yssm * ss * (1.0 - ss)
        dyssm = dmerged * ss
        dgl = jnp.concatenate([dyssm * sgb, dyssm * gav * sgb * (1.0 - sgb)], axis=1).astype(BF16)
        dsg = (_nt(dgl[:, 0:512], wg_ref[0]) + _nt(dgl[:, 512:1024], wg_ref[1])
               + _nt(dgl[:, 1024:1536], wg_ref[2]) + _nt(dgl[:, 1536:2048], wg_ref[3]))
        sv = s_ref[...]
        _, th = _gelu(sv)
        dgelu = 0.5 * (1.0 + th) + 0.5 * sv * (1.0 - th * th) * GELU_C * (1.0 + 3.0 * 0.044715 * sv * sv)
        ds = dsg * dgelu
        dsb = ds.astype(BF16)
        pg, pb, pd = _rowsum(dyv * xhat), _rowsum(dyv), _rowsum(ds * su_ref[...])

        @pl.when(i == 0)
        def _():
            dg_ref[...] = pg
            db_ref[...] = pb
            dd_ref[...] = pd

        @pl.when(i > 0)
        def _():
            dg_ref[...] += pg
            db_ref[...] += pb
            dd_ref[...] += pd

        dres_ref[...] = ALPHA * dr
        dmix_ref[...] = dmix
        dgl_ref[...] = dgl
        dsb_ref[...] = dsb
        dud_ref[...] = ds * d_ref[...]
        for J in range(4):
            gsr_ref[:, 512 * J:512 * (J + 1)] = _nt(dsb[:, 128 * J:128 * (J + 1)], wcr_ref[J])
            gsi_ref[:, 512 * J:512 * (J + 1)] = _nt(dsb[:, 128 * J:128 * (J + 1)], wci_ref[J])
        dyc_ref[...] = (dmerged * sc).astype(BF16)
        dp_ref[:, 0:D] = dgc.astype(BF16)
        dp_ref[:, D:2 * D] = dgss.astype(BF16)

    def tok(n):
        return pl.BlockSpec((tm, n), lambda i: (i, 0))

    def full(shape):
        return pl.BlockSpec(shape, lambda i: (0,) * len(shape))

    return _pcall(
        body, name="mix_bwd_b", grid=(T // tm,),
        in_specs=[tok(D), tok(D), full((1, D)), full((D, D)), tok(D), tok(D), tok(D), tok(D), tok(D),
                  tok(SSM), tok(SSM), full((1, SSM)), full((4, SSM, 512)), full((4, 512, 128)), full((4, 512, 128))],
        out_specs=[tok(D), tok(D), tok(2 * D), tok(SSM), tok(SSM), tok(LANES), tok(LANES), tok(D),
                   pl.BlockSpec((tm, 2 * D), lambda i: (i, 1)), full((1, D)), full((1, D)), full((1, SSM))],
        out_shape=[jax.ShapeDtypeStruct((T, D), F32), jax.ShapeDtypeStruct((T, D), BF16),
                   jax.ShapeDtypeStruct((T, 2 * D), BF16), jax.ShapeDtypeStruct((T, SSM), BF16),
                   jax.ShapeDtypeStruct((T, SSM), F32), jax.ShapeDtypeStruct((T, LANES), F32),
                   jax.ShapeDtypeStruct((T, LANES), F32), jax.ShapeDtypeStruct((T, D), BF16),
                   jax.ShapeDtypeStruct((T, 4 * D), BF16), jax.ShapeDtypeStruct((1, D), F32),
                   jax.ShapeDtypeStruct((1, D), F32), jax.ShapeDtypeStruct((1, SSM), F32)],
        vmem_mb=56, comm=comm,
        operands=(dy, r2, g, w_mo, g_conv, g_ssm, y_conv, ga, gb, s, su, dvec, w_glu4, wc_re4, wc_im4))


def _s5_scan_bwd(gs_re, gs_im, st_re, st_im, su_b, ds_b, wb_re, wb_im, a_re, a_im, comm=()):
    T = su_b.shape[0]
    W = SCAN_W
    R = SCAN_R

    def body(gr_ref, gi_ref, sr_ref, si_ref, su_ref, ds_ref, wbr_ref, wbi_ref, ar_ref, ai_ref,
             dsu_ref, dwbr_ref, dwbi_ref, dwcr_ref, dwci_ref, dar_ref, dai_ref, gre, gim):
        j = pl.program_id(0)
        zero = jnp.zeros((8, W), F32)
        for buf in (gre, gim):
            buf[pl.ds(0, 8), :] = zero
            buf[pl.ds(T + 8, 8), :] = zero
        gre[pl.ds(8, T), :] = gr_ref[...]
        gim[pl.ds(8, T), :] = gi_ref[...]
        _scan_inplace(gre, gim, ar_ref[...], ai_ref[...], T, rev=True)
        grb = gre[pl.ds(8, T), :].astype(BF16)
        gib = gim[pl.ds(8, T), :].astype(BF16)
        part = _nt(grb, wbr_ref[...]) + _nt(gib, wbi_ref[...])

        @pl.when(j % 2 == 0)
        def _():
            dsu_ref[...] = part

        @pl.when(j % 2 == 1)
        def _():
            dsu_ref[...] += part

        su = su_ref[...]
        dwbr_ref[...] = _tn(su, grb)
        dwbi_ref[...] = _tn(su, gib)
        dsv = ds_ref[...]
        dwcr_ref[...] = _tn(sr_ref[...].astype(BF16), dsv)
        dwci_ref[...] = _tn(si_ref[...].astype(BF16), dsv)
        dar = jnp.zeros((1, W), F32)
        dai = jnp.zeros((1, W), F32)
        for c in range(T // R):
            xr = sr_ref[pl.ds(c * R, R), :]
            xi = si_ref[pl.ds(c * R, R), :]
            g1r = gre[pl.ds(c * R + 9, R), :]
            g1i = gim[pl.ds(c * R + 9, R), :]
            dar = dar + _rowsum(g1r * xr + g1i * xi)
            dai = dai + _rowsum(g1i * xr - g1r * xi)
        dar_ref[...] = dar
        dai_ref[...] = dai

    lane, col, wb, wc, vec = _scan_specs(T)
    return _pcall(
        body, name="s5_scan_bwd", grid=(LANES // W,),
        in_specs=[lane, lane, lane, lane, col, col, wb, wb, vec, vec],
        out_specs=[col, wb, wb, wc, wc, vec, vec],
        out_shape=[jax.ShapeDtypeStruct((T, SSM), F32),
                   jax.ShapeDtypeStruct((LANES // W, 128, W), F32), jax.ShapeDtypeStruct((LANES // W, 128, W), F32),
                   jax.ShapeDtypeStruct((LANES // W, W, 128), F32), jax.ShapeDtypeStruct((LANES // W, W, 128), F32),
                   jax.ShapeDtypeStruct((1, LANES), F32), jax.ShapeDtypeStruct((1, LANES), F32)],
        scratch=[pltpu.VMEM((T + 16, W), F32)] * 2, vmem_mb=56, comm=comm,
        operands=(gs_re, gs_im, st_re, st_im, su_b, ds_b, wb_re, wb_im, a_re, a_im))


def _mix_bwd_a(dyc_b, w_co4, pc, z_b, conv_w, dsu_ssm, du_dir, dproj, dres, w_mix4, tm, comm=()):
    T = dres.shape[0]
    nt = T // tm

    def body(dyc_ref, wco_ref, pc_ref, halo_ref, z_ref, cw_ref, dsu_ref, dud_ref, dpin_ref, dres_ref, w_ref,
             dp_ref, dx_ref, dcw_ref, dcb_ref, dzbuf, qbuf):
        i = pl.program_id(0)
        ii = nt - 1 - i

        @pl.when(i == 0)
        def _():
            dzbuf[pl.ds(tm, 8), :] = jnp.zeros((8, CONV), F32)

        dyc = dyc_ref[...]
        dyin = (_nt(dyc[:, 0:256], wco_ref[0]) + _nt(dyc[:, 256:512], wco_ref[1])
                + _nt(dyc[:, 512:768], wco_ref[2]) + _nt(dyc[:, 768:1024], wco_ref[3]))
        cbv = pc_ref[:, 0:CONV].astype(F32)
        ccv = pc_ref[:, CONV:2 * CONV].astype(F32)
        chv = pc_ref[:, 2 * CONV:3 * CONV].astype(F32)
        dcbv = dyin * z_ref[...].astype(F32)
        dz = dyin * cbv
        dzbuf[pl.ds(0, tm), :] = dz
        cw = cw_ref[...]
        dq = cw[2:3] * dz + cw[1:2] * dzbuf[pl.ds(1, tm), :] + cw[0:1] * dzbuf[pl.ds(2, tm), :]
        dzbuf[pl.ds(tm, 8), :] = dz[0:8]
        q = ccv * chv
        hq = halo_ref[:, CONV:2 * CONV].astype(F32) * halo_ref[:, 2 * CONV:3 * CONV].astype(F32)
        qbuf[pl.ds(0, 8), :] = jnp.where(ii > 0, hq, jnp.zeros_like(hq))
        qbuf[pl.ds(8, tm), :] = q
        pw = jnp.concatenate([_rowsum(dz * qbuf[pl.ds(6, tm), :]), _rowsum(dz * qbuf[pl.ds(7, tm), :]),
                              _rowsum(dz * q), jnp.zeros((5, CONV), F32)], axis=0)
        pbias = _rowsum(dz)

        @pl.when(i == 0)
        def _():
            dcw_ref[...] = pw
            dcb_ref[...] = pbias

        @pl.when(i > 0)
        def _():
            dcw_ref[...] += pw
            dcb_ref[...] += pbias

        dp0 = jnp.concatenate([dcbv, dq * chv], axis=1).astype(BF16)
        dp1 = jnp.concatenate([dq * ccv, dsu_ref[...] + dud_ref[...]], axis=1).astype(BF16)
        dp_ref[:, 0:D] = dp0
        dp_ref[:, D:2 * D] = dp1
        dx_ref[...] = (dres_ref[...] + _nt(dp0, w_ref[0]) + _nt(dp1, w_ref[1])
                       + _nt(dpin_ref[:, 0:D], w_ref[2]) + _nt(dpin_ref[:, D:2 * D], w_ref[3]))

    def tok(n):
        return pl.BlockSpec((tm, n), lambda i: (nt - 1 - i, 0))

    def full(shape):
        return pl.BlockSpec(shape, lambda i: (0,) * len(shape))

    halo = pl.BlockSpec((8, 3 * CONV), lambda i: (jnp.maximum((nt - 1 - i) * (tm // 8) - 1, 0), 0))
    return _pcall(
        body, name="mix_bwd_a", grid=(nt,),
        in_specs=[tok(D), full((4, CONV, 256)), tok(3 * CONV), halo, tok(CONV), full((3, CONV)),
                  tok(SSM), tok(SSM), pl.BlockSpec((tm, 2 * D), lambda i: (nt - 1 - i, 1)), tok(D),
                  full((4, D, D))],
        out_specs=[pl.BlockSpec((tm, 2 * D), lambda i: (nt - 1 - i, 0)), tok(D), full((8, CONV)), full((1, CONV))],
        out_shape=[jax.ShapeDtypeStruct((T, 4 * D), BF16), jax.ShapeDtypeStruct((T, D), F32),
                   jax.ShapeDtypeStruct((8, CONV), F32), jax.ShapeDtypeStruct((1, CONV), F32)],
        scratch=[pltpu.VMEM((tm + 8, CONV), F32), pltpu.VMEM((tm + 8, CONV), F32)],
        aliases={8: 0}, vmem_mb=56, comm=comm,
        operands=(dyc_b, w_co4, pc, pc, z_b, conv_w, dsu_ssm, du_dir, dproj, dres, w_mix4))


def _zoh(lam_re, lam_im, log_step, b_re, b_im):
    dt = jnp.exp(log_step)[:, None]
    mag = jnp.exp(lam_re * dt)
    abr, abi = mag * jnp.cos(lam_im * dt), mag * jnp.sin(lam_im * dt)
    nr, ni = abr - 1.0, abi
    den = lam_re * lam_re + lam_im * lam_im
    cr = (nr * lam_re + ni * lam_im) / den
    ci = (ni * lam_re - nr * lam_im) / den
    bbr = cr[..., None] * b_re - ci[..., None] * b_im
    bbi = cr[..., None] * b_im + ci[..., None] * b_re
    return abr, abi, bbr, bbi


def _wb_blocks(bb):
    eye = jnp.eye(GROUPS, dtype=F32)
    full = jnp.einsum("gni,gh->gihn", bb, eye).reshape(4, 128, 8, SCAN_W)
    return jnp.stack([full[j // 2, :, j, :] for j in range(8)]).astype(BF16)


def _wc_blocks(cc):
    eye = jnp.eye(GROUPS, dtype=F32)
    full = jnp.einsum("gin,gh->gnhi", cc, eye).reshape(4, 512, 4, 128)
    return jnp.stack([full[J, :, J, :] for J in range(4)]).astype(BF16)


_G = np.arange(GROUPS)


def _wb_diag(dwb8):
    d5 = dwb8.reshape(8, 8, 16, 4, 64)
    return d5[_G // 4, _G % 8, :, _G % 4, :].transpose(0, 2, 1)


def _wc_diag(dwc8):
    d5 = dwc8.reshape(8, 4, 64, 8, 16)
    return d5[_G // 4, _G % 4, :, _G % 8, :].transpose(0, 2, 1)


def _where():
    x, y, c = lax.axis_index("x"), lax.axis_index("y"), lax.axis_index("c")
    return x, y, c, 2 * x + y


def _chip_dev(k, c):
    return (k // 2, k % 2, c)


def _slot_cast(meidx, w, dtype, name):
    R, C = w.shape
    tr = _row_tile(R)

    def body(m_ref, w_ref, o_ref):
        o_ref[...] = w_ref[...].astype(dtype)

    gs = pltpu.PrefetchScalarGridSpec(
        num_scalar_prefetch=1, grid=(R // tr,),
        in_specs=[pl.BlockSpec((tr, C), lambda i, m: (i, 0))],
        out_specs=pl.BlockSpec((None, tr, C), lambda i, m: (m[0], i, 0)))
    return pl.pallas_call(
        body, name=name, grid_spec=gs, out_shape=_hbm_out(jax.ShapeDtypeStruct((4, R, C), dtype)),
        compiler_params=_cp(32, 1),
    )(meidx, *_hbm(w))


def _gather_payload(bufs):
    n = len(bufs)

    def half(ref, w, k, cc):
        h = bufs[w].shape[1] // 2
        return ref.at[k, pl.ds(cc * h, h)]

    def ici(ins, outs, sems, w, s):
        x, y, c, me = _where()
        k = (me + 1 + s) % 4
        return pltpu.make_async_remote_copy(
            src_ref=half(ins[w], w, me, c), dst_ref=half(outs[w], w, me, c), send_sem=sems[0].at[3 * w + s],
            recv_sem=sems[1].at[3 * w + s], device_id=_chip_dev(k, c), device_id_type=MESH)

    def landed(outs, sems, w, s):
        x, y, c, me = _where()
        j = (me + 3 - s) % 4
        return pltpu.make_async_remote_copy(
            src_ref=half(outs[w], w, j, c), dst_ref=half(outs[w], w, j, c), send_sem=sems[0].at[3 * w + s],
            recv_sem=sems[1].at[3 * w + s], device_id=(x, y, 1 - c), device_id_type=MESH)

    def passed(outs, sems, w, s, cc):
        x, y, c, me = _where()
        j = (me + 3 - s) % 4
        return pltpu.make_async_remote_copy(
            src_ref=half(outs[w], w, j, cc), dst_ref=half(outs[w], w, j, cc), send_sem=sems[2].at[3 * w + s],
            recv_sem=sems[3].at[3 * w + s], device_id=(x, y, 1 - c), device_id_type=MESH)

    pairs = [(w, s) for w in range(n) for s in range(3)]

    def start(ins, outs, sems):
        for w, s in pairs:
            ici(ins, outs, sems, w, s).start()

    def finish(ins, outs, sems):
        _, _, c, _ = _where()
        for w, s in pairs:
            landed(outs, sems, w, s).wait_recv()
            passed(outs, sems, w, s, c).start()
        for w, s in pairs:
            passed(outs, sems, w, s, 1 - c).wait_recv()
        for w, s in pairs:
            ici(ins, outs, sems, w, s).wait_send()
            passed(outs, sems, w, s, c).wait_send()

    return _Payload(bufs, [jax.ShapeDtypeStruct(b.shape, b.dtype) for b in bufs], {w: w for w in range(n)},
                    [pltpu.SemaphoreType.DMA((3 * n,))] * 4, start, finish)


def _sym_payload(operands, outs, copies, n_copies):
    def start(ins, outs_, sems):
        for cp in copies(ins, outs_, sems[0], sems[1]):
            cp.start()

    def finish(ins, outs_, sems):
        for cp in copies(ins, outs_, sems[0], sems[1]):
            cp.wait()

    p = _Payload(operands, outs, {}, [pltpu.SemaphoreType.DMA((n_copies,))] * 2, start, finish)
    p.copies, p.n_copies = copies, n_copies
    return p


def _swap_payload(g4s):
    def copies(ins, outs, ss, rs):
        x, y, c, me = _where()
        cps = []
        for w, g in enumerate(g4s):
            h = g.shape[1] // 2
            cps.append(pltpu.make_async_remote_copy(
                src_ref=ins[w].at[:, pl.ds((1 - c) * h, h)], dst_ref=outs[w], send_sem=ss.at[w],
                recv_sem=rs.at[w], device_id=(x, y, 1 - c), device_id_type=MESH))
        return cps

    outs = [jax.ShapeDtypeStruct((4, g.shape[1] // 2, g.shape[2]), g.dtype) for g in g4s]
    return _sym_payload(g4s, outs, copies, len(g4s))


def _exchange_payload(pbs):
    def copies(ins, outs, ss, rs):
        x, y, c, me = _where()
        cps = []
        for w in range(len(pbs)):
            for s in range(3):
                k = (me + 1 + s) % 4
                cps.append(pltpu.make_async_remote_copy(
                    src_ref=ins[w].at[k], dst_ref=outs[w].at[2 - s], send_sem=ss.at[3 * w + s],
                    recv_sem=rs.at[3 * w + s], device_id=_chip_dev(k, c), device_id_type=MESH))
        return cps

    outs = [jax.ShapeDtypeStruct((3,) + p.shape[1:], p.dtype) for p in pbs]
    return _sym_payload(pbs, outs, copies, 3 * len(pbs))


HBM_REF = pl.BlockSpec(memory_space=pltpu.HBM)
SEM_REF = pl.BlockSpec(memory_space=pltpu.SEMAPHORE)
DATAFLOW = pltpu.SideEffectType.DATAFLOW_SIDE_EFFECTING


class _SemList:
    def __init__(self, refs):
        self.refs = refs

    @property
    def at(self):
        return self.refs


def _split_start(p, name):
    n_in, n_out, nc = len(p.operands), len(p.outs), p.n_copies
    lands = [lax.empty(s.shape, s.dtype) for s in p.outs]

    def body(*refs):
        ins, lnd = refs[:n_in], refs[n_in:n_in + n_out]
        sems = refs[n_in + n_out:n_in + n_out + 2 * nc]
        for cp in p.copies(ins, lnd, _SemList(sems[:nc]), _SemList(sems[nc:])):
            cp.start()
        refs[-1][...] = jnp.zeros((8, 128), F32)

    res = pl.pallas_call(
        body, name=name,
        in_specs=[HBM_REF] * (n_in + n_out),
        out_specs=[SEM_REF] * (2 * nc) + [HBM_REF] * (n_in + n_out) + [VMEM_FULL],
        out_shape=([pltpu.SemaphoreType.DMA(())] * (2 * nc) + _hbm_out(p.operands) + _hbm_out(lands)
                   + [jax.ShapeDtypeStruct((8, 128), F32)]),
        input_output_aliases={i: 2 * nc + i for i in range(n_in + n_out)},
        compiler_params=pltpu.CompilerParams(has_side_effects=DATAFLOW),
    )(*_hbm(*p.operands, *lands))
    k = 2 * nc
    return list(res[:k]), list(res[k:k + n_in]), list(res[k + n_in:k + n_in + n_out]), res[-1]


def _split_wait(p, handle, after, name):
    sems, srcs, lands, _ = handle
    n_in, n_out, nc = len(srcs), len(lands), p.n_copies

    def body(*refs):
        ins, lnd = refs[:n_in], refs[n_in:n_in + n_out]
        sm = refs[n_in + n_out:n_in + n_out + 2 * nc]
        for cp in p.copies(ins, lnd, _SemList(sm[:nc]), _SemList(sm[nc:])):
            cp.wait_send()
            cp.wait_recv()

    res = pl.pallas_call(
        body, name=name,
        in_specs=[HBM_REF] * (n_in + n_out) + [SEM_REF] * (2 * nc) + [ANY] * len(after),
        out_specs=[HBM_REF] * (n_in + n_out), out_shape=_hbm_out(srcs) + _hbm_out(lands),
        input_output_aliases={i: i for i in range(n_in + n_out)},
        compiler_params=pltpu.CompilerParams(has_side_effects=DATAFLOW),
    )(*srcs, *lands, *sems, *after)
    return list(res[:n_in]), list(res[n_in:])


def _join_payload(halves):
    def copies(ins, outs, ss, rs):
        x, y, c, me = _where()
        return [pltpu.make_async_remote_copy(
            src_ref=ins[w], dst_ref=outs[w], send_sem=ss.at[w], recv_sem=rs.at[w],
            device_id=(x, y, 1 - c), device_id_type=MESH) for w in range(len(halves))]

    outs = [jax.ShapeDtypeStruct(a.shape, a.dtype) for a in halves]
    return _sym_payload(halves, outs, copies, len(halves))


def _allgather_payload(v):
    def copies(ins, outs, ss, rs):
        x, y, c, me = _where()
        lin = 4 * x + 2 * y + c
        cps = []
        cps = [pltpu.make_async_copy(ins[0], outs[0].at[lin], ss.at[0])]
        for o in range(1, 8):
            t = (lin + o) % 8
            cps.append(pltpu.make_async_remote_copy(
                src_ref=ins[0], dst_ref=outs[0].at[lin], send_sem=ss.at[o], recv_sem=rs.at[o],
                device_id=(t // 4, (t // 2) % 2, t % 2), device_id_type=MESH))
        return cps

    return _sym_payload([v], [jax.ShapeDtypeStruct((8,) + v.shape, v.dtype)], copies, 8)


def _sum8(buf, token):
    _, P, C = buf.shape

    def body(b_ref, t_ref, o_ref):
        acc = b_ref[0]
        for d in range(1, 8):
            acc = acc + b_ref[d]
        o_ref[...] = acc

    return pl.pallas_call(
        body, name="sum8", in_specs=[VMEM_FULL, VMEM_FULL], out_specs=VMEM_FULL,
        out_shape=jax.ShapeDtypeStruct((P, C), F32),
        compiler_params=pltpu.CompilerParams(vmem_limit_bytes=32 << 20),
    )(buf, token)


def _row_tile(h):
    for t in (256, 176, 128, 64, 32, 16, 8):
        if h % t == 0:
            return t
    raise ValueError(h)


def _pair_sum(cidx, g4, got, name):
    _, R, C = g4.shape
    h = R // 2
    th = _row_tile(h)

    def body(c_ref, a_ref, b_ref, o_ref, ob_ref):
        sm = a_ref[...] + b_ref[...]
        o_ref[...] = sm
        ob_ref[...] = sm.astype(BF16)

    blk = pl.BlockSpec((None, th, C), lambda k, i, c: (k, i, 0))
    gs = pltpu.PrefetchScalarGridSpec(
        num_scalar_prefetch=1, grid=(4, h // th),
        in_specs=[pl.BlockSpec((None, None, th, C), lambda k, i, c: (k, c[0], i, 0)), blk],
        out_specs=[blk, blk])
    return pl.pallas_call(
        body, name=name, grid_spec=gs,
        out_shape=_hbm_out([jax.ShapeDtypeStruct((4, h, C), F32), jax.ShapeDtypeStruct((4, h, C), BF16)]),
        compiler_params=_cp(32, 2),
    )(cidx, *_hbm(g4.reshape(4, 2, h, C), got))


def _chip_sum(meidx, p32, got, name):
    _, h, C = p32.shape
    th = _row_tile(h)

    def body(m_ref, a_ref, b_ref, o_ref):
        o_ref[...] = ((a_ref[...] + b_ref[0].astype(F32)) + b_ref[1].astype(F32)) + b_ref[2].astype(F32)

    gs = pltpu.PrefetchScalarGridSpec(
        num_scalar_prefetch=1, grid=(h // th,),
        in_specs=[pl.BlockSpec((None, th, C), lambda i, m: (m[0], i, 0)),
                  pl.BlockSpec((3, th, C), lambda i, m: (0, i, 0))],
        out_specs=pl.BlockSpec((th, C), lambda i, m: (i, 0)))
    return pl.pallas_call(
        body, name=name, grid_spec=gs, out_shape=_hbm_out(jax.ShapeDtypeStruct((h, C), F32)),
        compiler_params=_cp(32, 1),
    )(meidx, *_hbm(p32, got))


def _adamw_math(w, g, m, v):
    m2 = B1 * m + (1.0 - B1) * g
    v2 = B2 * v + (1.0 - B2) * (g * g)
    m_hat = m2 / (1.0 - B1 ** STEP)
    v_hat = v2 / (1.0 - B2 ** STEP)
    delta = -LR * (m_hat / (jnp.sqrt(v_hat) + EPS) + WD * w)
    return delta, m2, v2


def _adamw_pair(cidx, w, mine, theirs, m, v, token, name):
    R, C = w.shape
    h = R // 2
    tr = _row_tile(h)
    nh = h // tr

    def body(c_ref, w_ref, a_ref, b_ref, m_ref, v_ref, t_ref, g_ref, d_ref, mo_ref, vo_ref):
        own = (pl.program_id(0) // nh) == c_ref[0]
        g = jnp.where(own, a_ref[...], b_ref[...])
        d, m2, v2 = _adamw_math(w_ref[...], g, m_ref[...], v_ref[...])
        g_ref[...] = g
        d_ref[...] = d
        mo_ref[...] = m2
        vo_ref[...] = v2

    blk = pl.BlockSpec((tr, C), lambda i, c: (i, 0))
    hblk = pl.BlockSpec((tr, C), lambda i, c: (i % nh, 0))
    gs = pltpu.PrefetchScalarGridSpec(
        num_scalar_prefetch=1, grid=(R // tr,),
        in_specs=[blk, hblk, hblk, blk, blk, pl.BlockSpec((8, 128), lambda i, c: (0, 0))], out_specs=[blk] * 4)
    return pl.pallas_call(
        body, name=name, grid_spec=gs, out_shape=_hbm_out([jax.ShapeDtypeStruct((R, C), F32)] * 4),
        compiler_params=_cp(32, 1),
    )(cidx, *_hbm(w, mine, theirs, m, v), token)


def _adamw(w, g, m, v, name):
    R, C = w.shape
    tr = _row_tile(R)

    def body(w_ref, g_ref, m_ref, v_ref, d_ref, mo_ref, vo_ref):
        d, m2, v2 = _adamw_math(w_ref[...], g_ref[...], m_ref[...], v_ref[...])
        d_ref[...] = d
        mo_ref[...] = m2
        vo_ref[...] = v2

    blk = pl.BlockSpec((tr, C), lambda i: (i, 0))
    return pl.pallas_call(
        body, name=name, grid=(R // tr,), in_specs=[blk] * 4, out_specs=[blk] * 3,
        out_shape=_hbm_out([jax.ShapeDtypeStruct((R, C), F32)] * 3),
        compiler_params=_cp(32, 1),
    )(*_hbm(w, g, m, v))


def _pack(arrs):
    flat = jnp.concatenate([a.reshape(-1).astype(F32) for a in arrs])
    rows = -(-flat.shape[0] // 1024)
    rows = -(-rows // 8) * 8
    return jnp.pad(flat, (0, rows * 1024 - flat.shape[0])).reshape(rows, 1024)


def _unpack(packed, shapes):
    flat = packed.reshape(-1)
    out, off = [], 0
    for s in shapes:
        n = math.prod(s)
        out.append(flat[off:off + n].reshape(s))
        off += n
    return out


BIG = ["ffn1_w_in", "ffn1_w_out", "mix_w_in", "conv_w_out", "ssm_w_glu", "mix_w_out",
       "ffn2_w_in", "ffn2_w_out", "ple_w_in", "ple_w_gate"]
SMALL = ["ln1_g", "ln1_b", "conv_w", "conv_b", "ssm_lam_re", "ssm_lam_im", "ssm_log_step", "ssm_b_re", "ssm_b_im",
         "ssm_c_re", "ssm_c_im", "ssm_d", "ln2_g", "ln2_b", "ln3_g", "ln3_b", "ln4_g", "ln4_b"]
WEIGHTS = ["ffn1_w_in", "ffn1_w_out", "ln1_g", "ln1_b", "mix_w_in", "conv_w", "conv_b", "conv_w_out",
           "ssm_lam_re", "ssm_lam_im", "ssm_log_step", "ssm_b_re", "ssm_b_im", "ssm_c_re", "ssm_c_im", "ssm_d",
           "ssm_w_glu", "mix_w_out", "ln2_g", "ln2_b", "ffn2_w_in", "ffn2_w_out", "ln3_g", "ln3_b",
           "ple_w_in", "ple_w_gate", "ln4_g", "ln4_b"]


class _NoComm:
    def __init__(self, W):
        self.W, self.G, self.raw = dict(W), {}, None

    def carry(self, name):
        return ()

    def landed(self, name, got):
        pass

    def grad(self, name, g4):
        self.G[name] = g4

    def small(self, raw):
        self.raw = raw


def _local_step(x, p, target, sp, sched, tm_ffn, tm_mix):
    W = sched.W
    abr, abi, bbr, bbi = _zoh(sp["ssm_lam_re"], sp["ssm_lam_im"], sp["ssm_log_step"], sp["ssm_b_re"], sp["ssm_b_im"])
    wb_re, wb_im = _wb_blocks(bbr), _wb_blocks(bbi)
    wc_re4, wc_im4 = _wc_blocks(sp["ssm_c_re"]), _wc_blocks(-sp["ssm_c_im"])
    a_re, a_im = abr.reshape(1, LANES), abi.reshape(1, LANES)
    dvec = sp["ssm_d"].reshape(1, SSM)

    def run(fn, name, *args, **kw):
        outs, got = fn(*args, comm=sched.carry(name), **kw)
        sched.landed(name, got)
        return outs

    def dw(name, wname, a, b, tk, tn, shape4, shard_cols=None, interleaved=False):
        out, got = _mm_tn(a, b, tk, tn, name, shard_cols=shard_cols, interleaved=interleaved,
                          comm=sched.carry(name))
        sched.landed(name, got)
        sched.grad(wname, out.reshape(shape4))

    xb = x.astype(BF16)
    h1, r1, x1, x1b = run(_ffn_fwd, "ffn1_fwd", x, xb, W["ffn1_w_in"], W["ffn1_w_out"].reshape(2, FFH, D),
                          sp["ln1_g"], sp["ln1_b"], tm_ffn, "ffn1_fwd")
    conv_w = W["conv_w"][:, 0:3, :].transpose(1, 0, 2).reshape(3, CONV)
    pc, z_b, yin_b, su, su_b, g_conv, g_ssm, y_conv = _mix_fwd_a(
        x1b, W["mix_w_in"], conv_w, sp["conv_b"], W["conv_w_out"], tm_mix)
    st_re, st_im = run(_s5_scan_fwd, "s5_scan_fwd", su_b, wb_re, wb_im, a_re, a_im)
    w_mo = W["mix_w_out"].reshape(D, D)
    s, sg_b, ga, gb, merged_b, r2, x2, x2b = run(
        _mix_fwd_b, "mix_fwd_b", st_re, st_im, wc_re4, wc_im4, su, dvec, W["ssm_w_glu"], g_conv, g_ssm, y_conv,
        w_mo, x1, sp["ln2_g"], sp["ln2_b"], tm_mix)
    w2o2 = W["ffn2_w_out"].reshape(2, FFH, D)
    h2, r3, x3, x3b = run(_ffn_fwd, "ffn2_fwd", x2, x2b, W["ffn2_w_in"], w2o2, sp["ln3_g"], sp["ln3_b"], tm_ffn,
                          "ffn2_fwd")
    loss_part, dx3, p_b, dpw_b, dgt_b, dg4, db4 = _ple_loss(
        x3, x3b, p, W["ple_w_in"], W["ple_w_gate"].reshape(D, D), sp["ln4_g"], sp["ln4_b"], target, tm_mix)

    dw("dw_ple_gate", "ple_w_gate", x3b, dgt_b, 512, 1024, (4, 256, D))
    dw("dw_ple_in", "ple_w_in", p_b, dpw_b, 256, 256, (4, 256, 256), shard_cols=256)
    dx2, dh2, a2_b, df2_b, dg3, db3 = run(_ffn_bwd, "ffn2_bwd", dx3, r3, sp["ln3_g"], h2, W["ffn2_w_in"], w2o2,
                                          tm_ffn, "ffn2_bwd")
    dw("dw_ffn2_in", "ffn2_w_in", x2b, dh2, 512, FFH, (4, D, FFH), shard_cols=FFH, interleaved=True)
    dw("dw_ffn2_out", "ffn2_w_out", a2_b, df2_b, FFH, 1024, (4, FF // 4, D))
    (dres, dmix_b, dgl_b, ds_b, du_dir, gs_re, gs_im, dyc_b, dproj, dg2, db2, dd) = run(
        _mix_bwd_b, "mix_bwd_b", dx2, r2, sp["ln2_g"], w_mo, g_conv, g_ssm, y_conv, ga, gb, s, su, dvec,
        W["ssm_w_glu"], wc_re4, wc_im4, tm_mix)
    dw("dw_mix_out", "mix_w_out", merged_b, dmix_b, 512, 1024, (4, 256, D))
    dw("dw_glu", "ssm_w_glu", sg_b, dgl_b, 512, 512, (4, SSM, 512), shard_cols=512)
    dsu_ssm, dwb_re, dwb_im, dwc_re, dwc_im, da_re, da_im = run(
        _s5_scan_bwd, "s5_scan_bwd", gs_re, gs_im, st_re, st_im, su_b, ds_b, wb_re, wb_im, a_re, a_im)
    dw("dw_conv_out", "conv_w_out", yin_b, dyc_b, 512, 256, (4, CONV, 256), shard_cols=256)
    dproj, dx1, dcw8, dcb = run(_mix_bwd_a, "mix_bwd_a", dyc_b, W["conv_w_out"], pc, z_b, conv_w, dsu_ssm,
                                du_dir, dproj, dres, W["mix_w_in"], tm_mix)
    dw("dw_mix_in", "mix_w_in", x1b, dproj, 512, 1024, (4, D, D), shard_cols=1024)
    dx0, dh1, a1_b, df1_b, dg1, db1 = run(_ffn_bwd, "ffn1_bwd", dx1, r1, sp["ln1_g"], h1, W["ffn1_w_in"],
                                          W["ffn1_w_out"].reshape(2, FFH, D), tm_ffn, "ffn1_bwd")
    sched.small(dict(
        ln1_g=dg1, ln1_b=db1, ln2_g=dg2, ln2_b=db2, ln3_g=dg3, ln3_b=db3, ln4_g=dg4, ln4_b=db4,
        conv_w=dcw8[0:3], conv_b=dcb,
        a_re=da_re.reshape(GROUPS, STATE), a_im=da_im.reshape(GROUPS, STATE),
        bb_re=_wb_diag(dwb_re), bb_im=_wb_diag(dwb_im),
        ssm_c_re=_wc_diag(dwc_re), ssm_c_im=-_wc_diag(dwc_im), ssm_d=dd.reshape(GROUPS, 16),
        loss=loss_part[0:1, 0]))
    dw("dw_ffn1_in", "ffn1_w_in", xb, dh1, 512, FFH, (4, D, FFH), shard_cols=FFH, interleaved=True)
    dw("dw_ffn1_out", "ffn1_w_out", a1_b, df1_b, FFH, 1024, (4, FF // 4, D))
    return loss_part[0, 0], dx0


RAW_ORDER = ["ln1_g", "ln1_b", "ln2_g", "ln2_b", "ln3_g", "ln3_b", "ln4_g", "ln4_b", "conv_w", "conv_b",
             "a_re", "a_im", "bb_re", "bb_im", "ssm_c_re", "ssm_c_im", "ssm_d", "loss"]

GATHER_FIRST = ["ffn1_w_in", "ffn1_w_out"]
GATHER_AT = {"ffn1_fwd": ["mix_w_in", "conv_w_out", "conv_w", "ssm_w_glu", "mix_w_out"],
             "s5_scan_fwd": ["ffn2_w_in"], "mix_fwd_b": ["ffn2_w_out"], "ffn2_fwd": ["ple_w_in", "ple_w_gate"]}
REDUCE_GROUP = {"ple": ["ple_w_gate", "ple_w_in"], "ffn2": ["ffn2_w_in", "ffn2_w_out"],
                "mix": ["mix_w_out", "ssm_w_glu", "conv_w_out", "mix_w_in"], "ffn1": ["ffn1_w_in", "ffn1_w_out"]}
REDUCE_AT = {"ffn2_bwd": [("swap", "ple")], "dw_ffn2_in": [("exchange", "ple")],
             "mix_bwd_b": [("swap", "ffn2"), ("join", "ple")], "s5_scan_bwd": [("exchange", "ffn2")],
             "mix_bwd_a": [("join", "ffn2")], "ffn1_bwd": [("swap", "mix")],
             "dw_ffn1_in": [("small", None)]}
BEGIN_AT = {"dw_ffn1_in": [("exchange", "mix")]}
LAST_GROUP = "ffn1"


class _Sched:
    def __init__(self, bufs, cidx, meidx):
        self.bufs, self.cidx, self.meidx = bufs, cidx, meidx
        self.W, self.G, self.raw, self.small_buf = {}, {}, None, None
        self.got1, self.p32, self.pbf, self.got2, self.half, self.theirs = {}, {}, {}, {}, {}, {}
        self._open, self._split = [], {}
        self._standalone("gather_ffn1", [("gather", GATHER_FIRST)])

    def _payload(self, stage, key):
        if stage == "gather":
            return _gather_payload([self.bufs[n] for n in key])
        if stage == "small":
            return _allgather_payload(_pack([self.raw[k] for k in RAW_ORDER]))
        names = REDUCE_GROUP[key]
        if stage == "swap":
            return _swap_payload([self.G[n] for n in names])
        if stage == "exchange":
            for n in names:
                self.p32[n], self.pbf[n] = _pair_sum(self.cidx, self.G[n], self.got1[n], "pair_sum_" + n)
            return _exchange_payload([self.pbf[n] for n in names])
        for n in names:
            self.half[n] = _chip_sum(self.meidx, self.p32[n], self.got2[n], "chip_sum_" + n)
        return _join_payload([self.half[n] for n in names])

    def _store(self, stages, got):
        for (stage, key), outs in zip(stages, got):
            if stage == "gather":
                self.W.update(zip(key, outs))
            elif stage == "small":
                self.small_buf = outs[0]
            else:
                {"swap": self.got1, "exchange": self.got2, "join": self.theirs}[stage].update(
                    zip(REDUCE_GROUP[key], outs))

    def _standalone(self, name, stages):
        self._store(stages, _comm_call(name, [self._payload(s, k) for s, k in stages]))

    def carry(self, name):
        tokens = [self._begin(stage, key) for stage, key in BEGIN_AT.get(name, [])]
        self._open = [("gather", GATHER_AT[name])] if name in GATHER_AT else []
        self._open += REDUCE_AT.get(name, [])
        comm = [self._payload(s, k) for s, k in self._open]
        if tokens:
            comm.append(_Payload(tokens, [], {}, [], lambda *a: None, lambda *a: None))
        return tuple(comm)

    def landed(self, name, got):
        self._store(self._open, got)

    def grad(self, name, g4):
        self.G[name] = g4

    def small(self, raw):
        self.raw = raw

    def _begin(self, stage, key):
        p = self._payload(stage, key)
        self._split[stage, key] = (p, _split_start(p, "%s_%s_start" % (stage, key)))
        return self._split[stage, key][1][3]

    def _end(self, stage, key, after):
        p, handle = self._split.pop((stage, key))
        srcs, lands = _split_wait(p, handle, after, "%s_%s_wait" % (stage, key))
        if stage == "swap":
            self.G.update(zip(REDUCE_GROUP[key], srcs))
        self._store([(stage, key)], [lands])

    def tail_begin(self):
        return self._begin("swap", LAST_GROUP)

    def tail_mid(self, after):
        self._end("swap", LAST_GROUP, after)
        token = self._begin("exchange", LAST_GROUP)
        self._end("exchange", "mix", [token])
        self._standalone("reduce_tail_join_mix", [("join", "mix")])
        return token

    def tail_end(self, after):
        self._end("exchange", LAST_GROUP, after)
        self._standalone("reduce_tail_join", [("join", LAST_GROUP)])


def _small_grads(raw_sum, sp):
    _, vjp = jax.vjp(_zoh, sp["ssm_lam_re"], sp["ssm_lam_im"], sp["ssm_log_step"], sp["ssm_b_re"], sp["ssm_b_im"])
    d_lre, d_lim, d_ls, d_bre, d_bim = vjp((raw_sum["a_re"], raw_sum["a_im"], raw_sum["bb_re"], raw_sum["bb_im"]))
    g = {k: raw_sum[k] for k in ("ln1_g", "ln1_b", "ln2_g", "ln2_b", "ln3_g", "ln3_b", "ln4_g", "ln4_b",
                                 "conv_w", "conv_b", "ssm_c_re", "ssm_c_im", "ssm_d")}
    g.update(ssm_lam_re=d_lre, ssm_lam_im=d_lim, ssm_log_step=d_ls, ssm_b_re=d_bre, ssm_b_im=d_bim)
    return g


def kernel(x, p, ffn1_w_in, ffn1_w_out, ln1_g, ln1_b, mix_w_in, conv_w, conv_b, conv_w_out, ssm_lam_re, ssm_lam_im, ssm_log_step, ssm_b_re, ssm_b_im, ssm_c_re, ssm_c_im, ssm_d, ssm_w_glu, mix_w_out, ln2_g, ln2_b, ffn2_w_in, ffn2_w_out, ln3_g, ln3_b, ple_w_in, ple_w_gate, ln4_g, ln4_b, loss_target, m_ffn1_w_in, m_ffn1_w_out, m_ln1_g, m_ln1_b, m_mix_w_in, m_conv_w, m_conv_b, m_conv_w_out, m_ssm_lam_re, m_ssm_lam_im, m_ssm_log_step, m_ssm_b_re, m_ssm_b_im, m_ssm_c_re, m_ssm_c_im, m_ssm_d, m_ssm_w_glu, m_mix_w_out, m_ln2_g, m_ln2_b, m_ffn2_w_in, m_ffn2_w_out, m_ln3_g, m_ln3_b, m_ple_w_in, m_ple_w_gate, m_ln4_g, m_ln4_b, v_ffn1_w_in, v_ffn1_w_out, v_ln1_g, v_ln1_b, v_mix_w_in, v_conv_w, v_conv_b, v_conv_w_out, v_ssm_lam_re, v_ssm_lam_im, v_ssm_log_step, v_ssm_b_re, v_ssm_b_im, v_ssm_c_re, v_ssm_c_im, v_ssm_d, v_ssm_w_glu, v_mix_w_out, v_ln2_g, v_ln2_b, v_ffn2_w_in, v_ffn2_w_out, v_ln3_g, v_ln3_b, v_ple_w_in, v_ple_w_gate, v_ln4_g, v_ln4_b):
    args = dict(locals())
    w = {n: args[n] for n in WEIGHTS}
    m = {n: args["m_" + n] for n in WEIGHTS}
    v = {n: args["v_" + n] for n in WEIGHTS}
    _, _, c, me = _where()
    cidx = jnp.reshape(c, (1,)).astype(jnp.int32)
    meidx = jnp.reshape(me, (1,)).astype(jnp.int32)

    bufs = {n: _slot_cast(meidx, w[n][0], BF16, "cast_" + n) for n in BIG}
    bufs["conv_w"] = _slot_cast(meidx, jnp.pad(conv_w[0], ((0, 13), (0, 0))), F32, "cast_conv_w")
    sched = _Sched(bufs, cidx, meidx)

    sp = {n: (w[n] if w[n].ndim == 2 and n != "ssm_log_step" else w[n][0]) for n in SMALL if n != "conv_w"}
    loss_part, dx0 = _local_step(x[0], p[0, 0], loss_target[0], sp, sched, 256, 256)
    out_g, out_d, out_m, out_v = {}, {}, {}, {}

    def big_adamw(names, token):
        for n in names:
            g, dl, mn, vn = _adamw_pair(cidx, w[n][0], sched.half[n], sched.theirs[n], m[n][0], v[n][0], token,
                                        "adamw_" + n)
            out_g[n], out_d[n], out_m[n], out_v[n] = g[None], dl[None], mn[None], vn[None]

    first = REDUCE_GROUP["ple"] + REDUCE_GROUP["ffn2"]
    big_adamw(first, sched.tail_begin())
    token = sched.tail_mid([out_v[n] for n in first])

    raw_shapes = [sched.raw[k].shape for k in RAW_ORDER]
    raw_sum = dict(zip(RAW_ORDER, _unpack(_sum8(sched.small_buf, token), raw_shapes)))
    loss = raw_sum["loss"][0]
    sg = _small_grads(raw_sum, sp)
    sg["conv_w"] = lax.dynamic_slice_in_dim(sg["conv_w"], me * 128, 128, axis=1)
    small_shapes = [w[n].shape for n in SMALL]
    gp = _pack([sg[n] for n in SMALL])
    d_s, m_s, v_s = _adamw(_pack([w[n] for n in SMALL]), gp, _pack([m[n] for n in SMALL]),
                           _pack([v[n] for n in SMALL]), "adamw_small")

    for n, a, b_, c_, d_ in zip(SMALL, _unpack(gp, small_shapes), _unpack(d_s, small_shapes),
                                _unpack(m_s, small_shapes), _unpack(v_s, small_shapes)):
        out_g[n], out_d[n], out_m[n], out_v[n] = a, b_, c_, d_
    big_adamw(REDUCE_GROUP["mix"], token)
    sched.tail_end([d_s] + [out_v[n] for n in REDUCE_GROUP["mix"]])
    big_adamw(REDUCE_GROUP[LAST_GROUP], token)

    return (loss, dx0[None], *[out_g[n] for n in WEIGHTS], *[out_d[n] for n in WEIGHTS],
            *[out_m[n] for n in WEIGHTS], *[out_v[n] for n in WEIGHTS])
```

```python
import functools
import math

import jax
import jax.numpy as jnp
import numpy as np
from jax import lax
from jax.experimental import pallas as pl
from jax.experimental.pallas import tpu as pltpu

F32, BF16 = jnp.float32, jnp.bfloat16
D = 1024
FF = 2816
FFH = FF // 2
CONV = 512
SSM = 512
GROUPS = 32
STATE = 64
LANES = GROUPS * STATE
SCAN_W = 256
SCAN_R = 256
ALPHA = 2.0 ** 0.25
LN_EPS = 1e-5
GELU_C = math.sqrt(2.0 / math.pi)
B1, B2, LR, EPS, WD, STEP = 0.9, 0.999, 0.001, 1e-8, 0.01, 10
MESH = pl.DeviceIdType.MESH
ANY = pl.BlockSpec(memory_space=pl.ANY)
VMEM_FULL = pl.BlockSpec(memory_space=pltpu.VMEM)


def _cp(vmem_mb=48, n_axes=1):
    return pltpu.CompilerParams(vmem_limit_bytes=vmem_mb << 20,
                                dimension_semantics=("arbitrary",) * n_axes)


def _hbm(*arrs):
    return [pltpu.with_memory_space_constraint(a, pltpu.HBM) for a in arrs]


def _hbm_out(shapes):
    if isinstance(shapes, (list, tuple)):
        return [pltpu.HBM(s.shape, s.dtype) for s in shapes]
    return pltpu.HBM(shapes.shape, shapes.dtype)


def _nn(a, b):
    return jnp.dot(a, b, preferred_element_type=F32)


def _nt(a, b):
    return lax.dot_general(a, b, (((1,), (1,)), ((), ())), preferred_element_type=F32)


def _tn(a, b):
    return lax.dot_general(a, b, (((0,), (0,)), ((), ())), preferred_element_type=F32)


def _sig(v):
    return jax.nn.sigmoid(v)


def _ln_stats(r):
    mu = jnp.mean(r, axis=-1, keepdims=True)
    xc = r - mu
    var = jnp.mean(xc * xc, axis=-1, keepdims=True)
    rstd = lax.rsqrt(var + LN_EPS)
    return xc * rstd, rstd


def _ln_bwd(dy, r, g):
    xhat, rstd = _ln_stats(r)
    dyg = dy * g
    m1 = jnp.mean(dyg, axis=-1, keepdims=True)
    m2 = jnp.mean(dyg * xhat, axis=-1, keepdims=True)
    return rstd * (dyg - m1 - xhat * m2), xhat


def _rowsum(v):
    return jnp.sum(v, axis=0, keepdims=True)


class _Payload:
    def __init__(self, operands, outs, aliases, sems, start, finish):
        self.operands, self.outs, self.aliases, self.sems = list(operands), list(outs), dict(aliases), list(sems)
        self.start, self.finish = start, finish


def _split(flat, comm, attr):
    out, i = [], 0
    for p in comm:
        n = len(getattr(p, attr))
        out.append(list(flat[i:i + n]))
        i += n
    return out


def _run_comm(comm, which, cin, cout, csem):
    for p, a, b, s in zip(comm, _split(cin, comm, "operands"), _split(cout, comm, "outs"), _split(csem, comm, "sems")):
        getattr(p, which)(a, b, s)


def _pcall(body, *, name, grid, in_specs, out_specs, out_shape, operands, scratch=(), vmem_mb=48, aliases=None,
           comm=()):
    ni, no, ns = len(in_specs), len(out_specs), len(scratch)
    c_ops = [a for p in comm for a in p.operands]
    c_outs = [s for p in comm for s in p.outs]
    c_sems = [s for p in comm for s in p.sems]
    io = dict(aliases or {})
    off_i, off_o = ni, no
    for p in comm:
        for a, b in p.aliases.items():
            io[off_i + a] = off_o + b
        off_i += len(p.operands)
        off_o += len(p.outs)

    def wrapped(*refs):
        ins, cin = refs[:ni], refs[ni:ni + len(c_ops)]
        o0 = ni + len(c_ops)
        outs, cout = refs[o0:o0 + no], refs[o0 + no:o0 + no + len(c_outs)]
        s0 = o0 + no + len(c_outs)
        scr, csem = refs[s0:s0 + ns], refs[s0 + ns:]
        if comm:
            first = functools.reduce(jnp.logical_and, [pl.program_id(a) == 0 for a in range(len(grid))])
            pl.when(first)(lambda: _run_comm(comm, "start", cin, cout, csem))
        body(*ins, *outs, *scr)
        if comm:
            last = functools.reduce(jnp.logical_and, [pl.program_id(a) == grid[a] - 1 for a in range(len(grid))])
            pl.when(last)(lambda: _run_comm(comm, "finish", cin, cout, csem))

    res = pl.pallas_call(
        wrapped, name=name, grid=grid,
        in_specs=list(in_specs) + [ANY] * len(c_ops), out_specs=list(out_specs) + [ANY] * len(c_outs),
        out_shape=_hbm_out(list(out_shape) + c_outs), scratch_shapes=list(scratch) + c_sems,
        input_output_aliases=io,
        compiler_params=pltpu.CompilerParams(vmem_limit_bytes=vmem_mb << 20,
                                             dimension_semantics=("arbitrary",) * len(grid),
                                             has_side_effects=bool(comm)),
    )(*_hbm(*operands, *c_ops))
    return list(res[:no]), _split(res[no:], comm, "outs")


def _comm_call(name, comm):
    c_ops = [a for p in comm for a in p.operands]
    c_outs = [s for p in comm for s in p.outs]
    c_sems = [s for p in comm for s in p.sems]
    io, off_i, off_o = {}, 0, 0
    for p in comm:
        for a, b in p.aliases.items():
            io[off_i + a] = off_o + b
        off_i += len(p.operands)
        off_o += len(p.outs)

    def body(*refs):
        cin, cout = refs[:len(c_ops)], refs[len(c_ops):len(c_ops) + len(c_outs)]
        csem = refs[len(c_ops) + len(c_outs):]
        _run_comm(comm, "start", cin, cout, csem)
        _run_comm(comm, "finish", cin, cout, csem)

    res = pl.pallas_call(
        body, name=name, in_specs=[ANY] * len(c_ops), out_specs=[ANY] * len(c_outs), out_shape=_hbm_out(c_outs),
        scratch_shapes=c_sems, input_output_aliases=io,
        compiler_params=pltpu.CompilerParams(has_side_effects=True),
    )(*_hbm(*c_ops))
    return _split(res, comm, "outs")


def _ffn_fwd(x, xb, w_in4, w_out2, g, b, tm, name, comm=()):
    T = x.shape[0]

    def body(x_ref, xb_ref, wg_ref, wu_ref, wo_ref, g_ref, b_ref, h_ref, r_ref, xo_ref, xob_ref, acc):
        k = pl.program_id(1)
        xv = xb_ref[...]
        gt = _nn(xv, wg_ref[...])
        up = _nn(xv, wu_ref[...])
        a = (gt * _sig(gt) * up).astype(BF16)
        h_ref[:, 0:FFH] = gt.astype(BF16)
        h_ref[:, FFH:2 * FFH] = up.astype(BF16)
        acc[...] = jnp.where(k == 0, 0.0, acc[...]) + _nn(a, wo_ref[...])

        @pl.when(k == 1)
        def _():
            r = ALPHA * x_ref[...] + 0.5 * acc[...]
            xhat, _ = _ln_stats(r)
            xo = xhat * g_ref[...] + b_ref[...]
            r_ref[...] = r
            xo_ref[...] = xo
            xob_ref[...] = xo.astype(BF16)

    tok = pl.BlockSpec((tm, D), lambda i, k: (i, 0))
    vec = pl.BlockSpec((1, D), lambda i, k: (0, 0))
    return _pcall(
        body, name=name, grid=(T // tm, 2),
        in_specs=[tok, tok,
                  pl.BlockSpec((None, D, FFH), lambda i, k: (k, 0, 0)),
                  pl.BlockSpec((None, D, FFH), lambda i, k: (k + 2, 0, 0)),
                  pl.BlockSpec((None, FFH, D), lambda i, k: (k, 0, 0)),
                  vec, vec],
        out_specs=[pl.BlockSpec((tm, FF), lambda i, k: (i, k)), tok, tok, tok],
        out_shape=[jax.ShapeDtypeStruct((T, 2 * FF), BF16), jax.ShapeDtypeStruct((T, D), F32),
                   jax.ShapeDtypeStruct((T, D), F32), jax.ShapeDtypeStruct((T, D), BF16)],
        scratch=[pltpu.VMEM((tm, D), F32)], vmem_mb=56, comm=comm,
        operands=(x, xb, w_in4, w_in4, w_out2, g, b))


def _ffn_bwd(dy, r, g, h, w_in4, w_out2, tm, name, comm=()):
    T = dy.shape[0]

    def body(dy_ref, r_ref, g_ref, h_ref, wg_ref, wu_ref, wo_ref,
             dx_ref, dh_ref, a_ref, df_ref, dg_ref, db_ref, acc, dr_s, dfb_s):
        i, k = pl.program_id(0), pl.program_id(1)

        @pl.when(k == 0)
        def _():
            dyv = dy_ref[...]
            dr, xhat = _ln_bwd(dyv, r_ref[...], g_ref[...])
            pg, pb = _rowsum(dyv * xhat), _rowsum(dyv)

            @pl.when(i == 0)
            def _():
                dg_ref[...] = pg
                db_ref[...] = pb

            @pl.when(i > 0)
            def _():
                dg_ref[...] += pg
                db_ref[...] += pb

            dr_s[...] = dr
            dfb = (0.5 * dr).astype(BF16)
            dfb_s[...] = dfb
            df_ref[...] = dfb

        da = _nt(dfb_s[...], wo_ref[...])
        gt = h_ref[:, 0:FFH].astype(F32)
        up = h_ref[:, FFH:2 * FFH].astype(F32)
        sg = _sig(gt)
        silu = gt * sg
        dgate = (da * up * (sg * (1.0 + gt * (1.0 - sg)))).astype(BF16)
        dup = (da * silu).astype(BF16)
        a_ref[...] = (silu * up).astype(BF16)
        dh_ref[:, 0:FFH] = dgate
        dh_ref[:, FFH:2 * FFH] = dup
        acc[...] = jnp.where(k == 0, 0.0, acc[...]) + _nt(dgate, wg_ref[...]) + _nt(dup, wu_ref[...])

        @pl.when(k == 1)
        def _():
            dx_ref[...] = ALPHA * dr_s[...] + acc[...]

    tok = pl.BlockSpec((tm, D), lambda i, k: (i, 0))
    vec = pl.BlockSpec((1, D), lambda i, k: (0, 0))
    wide = pl.BlockSpec((tm, FF), lambda i, k: (i, k))
    return _pcall(
        body, name=name, grid=(T // tm, 2),
        in_specs=[tok, tok, vec, wide,
                  pl.BlockSpec((None, D, FFH), lambda i, k: (k, 0, 0)),
                  pl.BlockSpec((None, D, FFH), lambda i, k: (k + 2, 0, 0)),
                  pl.BlockSpec((None, FFH, D), lambda i, k: (k, 0, 0))],
        out_specs=[tok, wide, pl.BlockSpec((tm, FFH), lambda i, k: (i, k)), tok, vec, vec],
        out_shape=[jax.ShapeDtypeStruct((T, D), F32), jax.ShapeDtypeStruct((T, 2 * FF), BF16),
                   jax.ShapeDtypeStruct((T, FF), BF16), jax.ShapeDtypeStruct((T, D), BF16),
                   jax.ShapeDtypeStruct((1, D), F32), jax.ShapeDtypeStruct((1, D), F32)],
        scratch=[pltpu.VMEM((tm, D), F32), pltpu.VMEM((tm, D), F32), pltpu.VMEM((tm, D), BF16)],
        vmem_mb=56, comm=comm, operands=(dy, r, g, h, w_in4, w_in4, w_out2))


def _mm_tn(a, b, tk, tn, name, shard_cols=None, interleaved=False, comm=()):
    T, K = a.shape
    N = b.shape[1]

    def body(a_ref, b_ref, o_ref):
        o_ref[...] = _tn(a_ref[...], b_ref[...])

    if shard_cols is None:
        out_shape = jax.ShapeDtypeStruct((K, N), F32)
        out_spec = pl.BlockSpec((tk, tn), lambda ki, nj: (ki, nj))
    else:
        per = shard_cols // tn

        def shard(nj):
            blk = nj // per
            return (blk % 2) * 2 + blk // 2 if interleaved else blk

        out_shape = jax.ShapeDtypeStruct((N // shard_cols, K, shard_cols), F32)
        out_spec = pl.BlockSpec((None, tk, tn), lambda ki, nj: (shard(nj), ki, nj % per))
    (out,), got = _pcall(
        body, name=name, grid=(K // tk, N // tn),
        in_specs=[pl.BlockSpec((T, tk), lambda ki, nj: (0, ki)), pl.BlockSpec((T, tn), lambda ki, nj: (0, nj))],
        out_specs=[out_spec], out_shape=[out_shape], comm=comm, operands=(a, b))
    return out, got


def _mix_fwd_a(xb, w_mix4, conv_w, conv_b, w_co4, tm):
    T = xb.shape[0]

    def body(xb_ref, w_ref, cw_ref, cb_ref, wco_ref,
             pc_ref, z_ref, yin_ref, su_ref, sub_ref, gc_ref, gs_ref, yc_ref, qbuf):
        @pl.when(pl.program_id(0) == 0)
        def _():
            qbuf[pl.ds(0, 8), :] = jnp.zeros((8, CONV), F32)

        xv = xb_ref[...]
        p0 = _nn(xv, w_ref[0])
        p1 = _nn(xv, w_ref[1])
        gc_ref[...] = _nn(xv, w_ref[2])
        gs_ref[...] = _nn(xv, w_ref[3])
        cbv, ccv = p0[:, :CONV], p0[:, CONV:]
        chv, suv = p1[:, :CONV], p1[:, CONV:]
        q = ccv * chv
        qbuf[pl.ds(8, tm), :] = q
        cw = cw_ref[...]
        z = (cw[2:3] * q + cw[1:2] * qbuf[pl.ds(7, tm), :] + cw[0:1] * qbuf[pl.ds(6, tm), :]
             + cb_ref[...])
        qbuf[pl.ds(0, 8), :] = q[tm - 8:tm]
        yin = (cbv * z).astype(BF16)
        pc_ref[:, 0:CONV] = cbv.astype(BF16)
        pc_ref[:, CONV:2 * CONV] = ccv.astype(BF16)
        pc_ref[:, 2 * CONV:3 * CONV] = chv.astype(BF16)
        z_ref[...] = z.astype(BF16)
        yin_ref[...] = yin
        su_ref[...] = suv
        sub_ref[...] = suv.astype(BF16)
        for k in range(4):
            yc_ref[:, 256 * k:256 * (k + 1)] = _nn(yin, wco_ref[k])

    def tok(n):
        return pl.BlockSpec((tm, n), lambda i: (i, 0))

    def full(shape):
        return pl.BlockSpec(shape, lambda i: (0,) * len(shape))

    return pl.pallas_call(
        body, name="mix_fwd_a", grid=(T // tm,),
        in_specs=[tok(D), full((4, D, D)), full((3, CONV)), full((1, CONV)), full((4, CONV, 256))],
        out_specs=[tok(3 * CONV), tok(CONV), tok(CONV), tok(SSM), tok(SSM), tok(D), tok(D), tok(D)],
        out_shape=_hbm_out([jax.ShapeDtypeStruct((T, 3 * CONV), BF16), jax.ShapeDtypeStruct((T, CONV), BF16),
                            jax.ShapeDtypeStruct((T, CONV), BF16), jax.ShapeDtypeStruct((T, SSM), F32),
                            jax.ShapeDtypeStruct((T, SSM), BF16), jax.ShapeDtypeStruct((T, D), F32),
                            jax.ShapeDtypeStruct((T, D), F32), jax.ShapeDtypeStruct((T, D), F32)]),
        scratch_shapes=[pltpu.VMEM((tm + 8, CONV), F32)],
        compiler_params=_cp(56, 1),
    )(*_hbm(xb, w_mix4, conv_w, conv_b, w_co4))


def _scan_inplace(bre, bim, ar, ai, T, rev):
    R = SCAN_R
    if rev:
        ai = -ai
    d = 1
    while d < T:
        if d < 8:
            def step(i, _, d=d, ar=ar, ai=ai):
                c = i if rev else T // R - 1 - i
                t0 = pl.multiple_of(c * R, R)
                if rev:
                    wr = bre[pl.ds(t0 + 8, R + 8), :]
                    wi = bim[pl.ds(t0 + 8, R + 8), :]
                    shr = pltpu.roll(wr, R + 8 - d, 0)[0:R]
                    shi = pltpu.roll(wi, R + 8 - d, 0)[0:R]
                    cr, ci = wr[0:R], wi[0:R]
                else:
                    wr = bre[pl.ds(t0, R + 8), :]
                    wi = bim[pl.ds(t0, R + 8), :]
                    shr = pltpu.roll(wr, d, 0)[8:8 + R]
                    shi = pltpu.roll(wi, d, 0)[8:8 + R]
                    cr, ci = wr[8:8 + R], wi[8:8 + R]
                bre[pl.ds(t0 + 8, R), :] = cr + ar * shr - ai * shi
                bim[pl.ds(t0 + 8, R), :] = ci + ar * shi + ai * shr
                return 0

            lax.fori_loop(0, T // R, step, 0)
        else:
            def upd(lo, n, d=d, ar=ar, ai=ai):
                src = lo + d if rev else lo - d
                if not isinstance(lo, int):
                    lo, src = pl.multiple_of(lo + 8, 8), pl.multiple_of(src + 8, 8)
                else:
                    lo, src = lo + 8, src + 8
                cr = bre[pl.ds(lo, n), :]
                ci = bim[pl.ds(lo, n), :]
                shr = bre[pl.ds(src, n), :]
                shi = bim[pl.ds(src, n), :]
                bre[pl.ds(lo, n), :] = cr + ar * shr - ai * shi
                bim[pl.ds(lo, n), :] = ci + ar * shi + ai * shr

            nfull = (T - d) // R if d >= R else T // R - 1

            def step(i, _, upd=upd, d=d):
                if rev:
                    t0 = i * R
                else:
                    t0 = T - (i + 1) * R
                upd(t0, R)
                return 0

            if nfull > 0:
                lax.fori_loop(0, nfull, step, 0)
            if d < R:
                if rev:
                    upd(T - R, R - d)
                else:
                    upd(d, R - d)
        ar, ai = ar * ar - ai * ai, 2.0 * ar * ai
        d *= 2


def _scan_specs(T):
    W = SCAN_W
    lane = pl.BlockSpec((T, W), lambda j: (0, j))
    col = pl.BlockSpec((T, 128), lambda j: (0, j // 2))
    wb = pl.BlockSpec((None, 128, W), lambda j: (j, 0, 0))
    wc = pl.BlockSpec((None, W, 128), lambda j: (j, 0, 0))
    vec = pl.BlockSpec((1, W), lambda j: (0, j))
    return lane, col, wb, wc, vec


def _s5_scan_fwd(su_b, wb_re, wb_im, a_re, a_im, comm=()):
    T = su_b.shape[0]
    W = SCAN_W

    def body(su_ref, wbr_ref, wbi_ref, ar_ref, ai_ref, sr_ref, si_ref, bre, bim):
        zero = jnp.zeros((8, W), F32)
        for buf in (bre, bim):
            buf[pl.ds(0, 8), :] = zero
            buf[pl.ds(T + 8, 8), :] = zero
        su = su_ref[...]
        bre[pl.ds(8, T), :] = _nn(su, wbr_ref[...])
        bim[pl.ds(8, T), :] = _nn(su, wbi_ref[...])
        _scan_inplace(bre, bim, ar_ref[...], ai_ref[...], T, rev=False)
        sr_ref[...] = bre[pl.ds(8, T), :]
        si_ref[...] = bim[pl.ds(8, T), :]

    lane, col, wb, wc, vec = _scan_specs(T)
    return _pcall(
        body, name="s5_scan_fwd", grid=(LANES // W,),
        in_specs=[col, wb, wb, vec, vec],
        out_specs=[lane, lane],
        out_shape=[jax.ShapeDtypeStruct((T, LANES), F32)] * 2,
        scratch=[pltpu.VMEM((T + 16, W), F32)] * 2, comm=comm,
        operands=(su_b, wb_re, wb_im, a_re, a_im))


def _gelu(s):
    th = jnp.tanh(GELU_C * (s + 0.044715 * s * s * s))
    return 0.5 * s * (1.0 + th), th


def _mix_fwd_b(st_re, st_im, wc_re4, wc_im4, su, dvec, w_glu4, g_conv, g_ssm, y_conv, w_mo, x1, g, b, tm, comm=()):
    T = su.shape[0]

    def body(sr_ref, si_ref, wcr_ref, wci_ref, su_ref, d_ref, wg_ref, gc_ref, gs_ref, yc_ref, wmo_ref,
             x_ref, g_ref, b_ref, s_ref, sgb_ref, ga_ref, gb_ref, mb_ref, r_ref, xo_ref, xob_ref):
        srb = sr_ref[...].astype(BF16)
        sib = si_ref[...].astype(BF16)
        ys = [_nn(srb[:, 512 * J:512 * (J + 1)], wcr_ref[J]) + _nn(sib[:, 512 * J:512 * (J + 1)], wci_ref[J])
              for J in range(4)]
        s = jnp.concatenate(ys, axis=1) + d_ref[...] * su_ref[...]
        sg, _ = _gelu(s)
        sgb = sg.astype(BF16)
        ga = jnp.concatenate([_nn(sgb, wg_ref[0]), _nn(sgb, wg_ref[1])], axis=1)
        gb = jnp.concatenate([_nn(sgb, wg_ref[2]), _nn(sgb, wg_ref[3])], axis=1)
        merged = _sig(gc_ref[...]) * yc_ref[...] + _sig(gs_ref[...]) * (ga * _sig(gb))
        mb = merged.astype(BF16)
        r = ALPHA * x_ref[...] + _nn(mb, wmo_ref[...])
        xhat, _ = _ln_stats(r)
        xo = xhat * g_ref[...] + b_ref[...]
        s_ref[...] = s
        sgb_ref[...] = sgb
        ga_ref[...] = ga
        gb_ref[...] = gb
        mb_ref[...] = mb
        r_ref[...] = r
        xo_ref[...] = xo
        xob_ref[...] = xo.astype(BF16)

    def tok(n):
        return pl.BlockSpec((tm, n), lambda i: (i, 0))

    def full(shape):
        return pl.BlockSpec(shape, lambda i: (0,) * len(shape))

    return _pcall(
        body, name="mix_fwd_b", grid=(T // tm,),
        in_specs=[tok(LANES), tok(LANES), full((4, 512, 128)), full((4, 512, 128)), tok(SSM), full((1, SSM)),
                  full((4, SSM, 512)), tok(D), tok(D), tok(D), full((D, D)), tok(D), full((1, D)), full((1, D))],
        out_specs=[tok(SSM), tok(SSM), tok(D), tok(D), tok(D), tok(D), tok(D), tok(D)],
        out_shape=[jax.ShapeDtypeStruct((T, SSM), F32), jax.ShapeDtypeStruct((T, SSM), BF16),
                   jax.ShapeDtypeStruct((T, D), F32), jax.ShapeDtypeStruct((T, D), F32),
                   jax.ShapeDtypeStruct((T, D), BF16), jax.ShapeDtypeStruct((T, D), F32),
                   jax.ShapeDtypeStruct((T, D), F32), jax.ShapeDtypeStruct((T, D), BF16)],
        vmem_mb=56, comm=comm,
        operands=(st_re, st_im, wc_re4, wc_im4, su, dvec, w_glu4, g_conv, g_ssm, y_conv, w_mo, x1, g, b))


def _ple_loss(x3, x3b, p, w_pi4, w_pg, g, b, target, tm):
    T = x3.shape[0]
    PD = p.shape[1]

    def body(x_ref, xb_ref, p_ref, wpi_ref, wpg_ref, g_ref, b_ref, t_ref,
             loss_ref, dx_ref, pb_ref, dpw_ref, dgt_ref, dg_ref, db_ref):
        i = pl.program_id(0)
        pb = p_ref[...].astype(BF16)
        pw = jnp.concatenate([_nn(pb, wpi_ref[k]) for k in range(4)], axis=1)
        gt = _nn(xb_ref[...], wpg_ref[...])
        sg = _sig(gt)
        r = ALPHA * x_ref[...] + pw * sg
        gv = g_ref[...]
        xhat, rstd = _ln_stats(r)
        err = xhat * gv + b_ref[...] - t_ref[...]
        lpart = jnp.zeros((1, 128), F32) + 0.5 * jnp.sum(jnp.mean(err * err, axis=-1, keepdims=True))
        dy = err * (1.0 / D)
        dyg = dy * gv
        m1 = jnp.mean(dyg, axis=-1, keepdims=True)
        m2 = jnp.mean(dyg * xhat, axis=-1, keepdims=True)
        dr = rstd * (dyg - m1 - xhat * m2)
        pg, pbias = _rowsum(dy * xhat), _rowsum(dy)

        @pl.when(i == 0)
        def _():
            loss_ref[...] = lpart
            dg_ref[...] = pg
            db_ref[...] = pbias

        @pl.when(i > 0)
        def _():
            loss_ref[...] += lpart
            dg_ref[...] += pg
            db_ref[...] += pbias

        dgt = (dr * pw * sg * (1.0 - sg)).astype(BF16)
        pb_ref[...] = pb
        dpw_ref[...] = (dr * sg).astype(BF16)
        dgt_ref[...] = dgt
        dx_ref[...] = ALPHA * dr + _nt(dgt, wpg_ref[...])

    def tok(n):
        return pl.BlockSpec((tm, n), lambda i: (i, 0))

    def full(shape):
        return pl.BlockSpec(shape, lambda i: (0,) * len(shape))

    return pl.pallas_call(
        body, name="ple_loss", grid=(T // tm,),
        in_specs=[tok(D), tok(D), tok(PD), full((4, PD, 256)), full((D, D)), full((1, D)), full((1, D)), tok(D)],
        out_specs=[full((1, 128)), tok(D), tok(PD), tok(D), tok(D), full((1, D)), full((1, D))],
        out_shape=_hbm_out([jax.ShapeDtypeStruct((1, 128), F32), jax.ShapeDtypeStruct((T, D), F32),
                            jax.ShapeDtypeStruct((T, PD), BF16), jax.ShapeDtypeStruct((T, D), BF16),
                            jax.ShapeDtypeStruct((T, D), BF16), jax.ShapeDtypeStruct((1, D), F32),
                            jax.ShapeDtypeStruct((1, D), F32)]),
        compiler_params=_cp(48, 1),
    )(*_hbm(x3, x3b, p, w_pi4, w_pg, g, b, target))


def _mix_bwd_b(dy, r2, g, w_mo, g_conv, g_ssm, y_conv, ga, gb, s, su, dvec, w_glu4, wc_re4, wc_im4, tm, comm=()):
    T = dy.shape[0]

    def body(dy_ref, r_ref, g_ref, wmo_ref, gc_ref, gs_ref, yc_ref, ga_ref, gb_ref, s_ref, su_ref, d_ref,
             wg_ref, wcr_ref, wci_ref,
             dres_ref, dmix_ref, dgl_ref, dsb_ref, dud_ref, gsr_ref, gsi_ref, dyc_ref, dp_ref,
             dg_ref, db_ref, dd_ref):
        i = pl.program_id(0)
        dyv = dy_ref[...]
        dr, xhat = _ln_bwd(dyv, r_ref[...], g_ref[...])
        dmix = dr.astype(BF16)
        dmerged = _nt(dmix, wmo_ref[...])
        sc, ss, sgb = _sig(gc_ref[...]), _sig(gs_ref[...]), _sig(gb_ref[...])
        gav = ga_ref[...]
        yssm = gav * sgb
        dgc = dmerged * yc_ref[...] * sc * (1.0 - sc)
        dgss = dmerged * yssm * ss * (1.0 - ss)
        dyssm = dmerged * ss
        dgl = jnp.concatenate([dyssm * sgb, dyssm * gav * sgb * (1.0 - sgb)], axis=1).astype(BF16)
        dsg = (_nt(dgl[:, 0:512], wg_ref[0]) + _nt(dgl[:, 512:1024], wg_ref[1])
               + _nt(dgl[:, 1024:1536], wg_ref[2]) + _nt(dgl[:, 1536:2048], wg_ref[3]))
        sv = s_ref[...]
        _, th = _gelu(sv)
        dgelu = 0.5 * (1.0 + th) + 0.5 * sv * (1.0 - th * th) * GELU_C * (1.0 + 3.0 * 0.044715 * sv * sv)
        ds = dsg * dgelu
        dsb = ds.astype(BF16)
        pg, pb, pd = _rowsum(dyv * xhat), _rowsum(dyv), _rowsum(ds * su_ref[...])

        @pl.when(i == 0)
        def _():
            dg_ref[...] = pg
            db_ref[...] = pb
            dd_ref[...] = pd

        @pl.when(i > 0)
        def _():
            dg_ref[...] += pg
            db_ref[...] += pb
            dd_ref[...] += pd

        dres_ref[...] = ALPHA * dr
        dmix_ref[...] = dmix
        dgl_ref[...] = dgl
        dsb_ref[...] = dsb
        dud_ref[...] = ds * d_ref[...]
        for J in range(4):
            gsr_ref[:, 512 * J:512 * (J + 1)] = _nt(dsb[:, 128 * J:128 * (J + 1)], wcr_ref[J])
            gsi_ref[:, 512 * J:512 * (J + 1)] = _nt(dsb[:, 128 * J:128 * (J + 1)], wci_ref[J])
        dyc_ref[...] = (dmerged * sc).astype(BF16)
        dp_ref[:, 0:D] = dgc.astype(BF16)
        dp_ref[:, D:2 * D] = dgss.astype(BF16)

    def tok(n):
        return pl.BlockSpec((tm, n), lambda i: (i, 0))

    def full(shape):
        return pl.BlockSpec(shape, lambda i: (0,) * len(shape))

    return _pcall(
        body, name="mix_bwd_b", grid=(T // tm,),
        in_specs=[tok(D), tok(D), full((1, D)), full((D, D)), tok(D), tok(D), tok(D), tok(D), tok(D),
                  tok(SSM), tok(SSM), full((1, SSM)), full((4, SSM, 512)), full((4, 512, 128)), full((4, 512, 128))],
        out_specs=[tok(D), tok(D), tok(2 * D), tok(SSM), tok(SSM), tok(LANES), tok(LANES), tok(D),
                   pl.BlockSpec((tm, 2 * D), lambda i: (i, 1)), full((1, D)), full((1, D)), full((1, SSM))],
        out_shape=[jax.ShapeDtypeStruct((T, D), F32), jax.ShapeDtypeStruct((T, D), BF16),
                   jax.ShapeDtypeStruct((T, 2 * D), BF16), jax.ShapeDtypeStruct((T, SSM), BF16),
                   jax.ShapeDtypeStruct((T, SSM), F32), jax.ShapeDtypeStruct((T, LANES), F32),
                   jax.ShapeDtypeStruct((T, LANES), F32), jax.ShapeDtypeStruct((T, D), BF16),
                   jax.ShapeDtypeStruct((T, 4 * D), BF16), jax.ShapeDtypeStruct((1, D), F32),
                   jax.ShapeDtypeStruct((1, D), F32), jax.ShapeDtypeStruct((1, SSM), F32)],
        vmem_mb=56, comm=comm,
        operands=(dy, r2, g, w_mo, g_conv, g_ssm, y_conv, ga, gb, s, su, dvec, w_glu4, wc_re4, wc_im4))


def _s5_scan_bwd(gs_re, gs_im, st_re, st_im, su_b, ds_b, wb_re, wb_im, a_re, a_im, comm=()):
    T = su_b.shape[0]
    W = SCAN_W
    R = SCAN_R

    def body(gr_ref, gi_ref, sr_ref, si_ref, su_ref, ds_ref, wbr_ref, wbi_ref, ar_ref, ai_ref,
             dsu_ref, dwbr_ref, dwbi_ref, dwcr_ref, dwci_ref, dar_ref, dai_ref, gre, gim):
        j = pl.program_id(0)
        zero = jnp.zeros((8, W), F32)
        for buf in (gre, gim):
            buf[pl.ds(0, 8), :] = zero
            buf[pl.ds(T + 8, 8), :] = zero
        gre[pl.ds(8, T), :] = gr_ref[...]
        gim[pl.ds(8, T), :] = gi_ref[...]
        _scan_inplace(gre, gim, ar_ref[...], ai_ref[...], T, rev=True)
        grb = gre[pl.ds(8, T), :].astype(BF16)
        gib = gim[pl.ds(8, T), :].astype(BF16)
        part = _nt(grb, wbr_ref[...]) + _nt(gib, wbi_ref[...])

        @pl.when(j % 2 == 0)
        def _():
            dsu_ref[...] = part

        @pl.when(j % 2 == 1)
        def _():
            dsu_ref[...] += part

        su = su_ref[...]
        dwbr_ref[...] = _tn(su, grb)
        dwbi_ref[...] = _tn(su, gib)
        dsv = ds_ref[...]
        dwcr_ref[...] = _tn(sr_ref[...].astype(BF16), dsv)
        dwci_ref[...] = _tn(si_ref[...].astype(BF16), dsv)
        dar = jnp.zeros((1, W), F32)
        dai = jnp.zeros((1, W), F32)
        for c in range(T // R):
            xr = sr_ref[pl.ds(c * R, R), :]
            xi = si_ref[pl.ds(c * R, R), :]
            g1r = gre[pl.ds(c * R + 9, R), :]
            g1i = gim[pl.ds(c * R + 9, R), :]
            dar = dar + _rowsum(g1r * xr + g1i * xi)
            dai = dai + _rowsum(g1i * xr - g1r * xi)
        dar_ref[...] = dar
        dai_ref[...] = dai

    lane, col, wb, wc, vec = _scan_specs(T)
    return _pcall(
        body, name="s5_scan_bwd", grid=(LANES // W,),
        in_specs=[lane, lane, lane, lane, col, col, wb, wb, vec, vec],
        out_specs=[col, wb, wb, wc, wc, vec, vec],
        out_shape=[jax.ShapeDtypeStruct((T, SSM), F32),
                   jax.ShapeDtypeStruct((LANES // W, 128, W), F32), jax.ShapeDtypeStruct((LANES // W, 128, W), F32),
                   jax.ShapeDtypeStruct((LANES // W, W, 128), F32), jax.ShapeDtypeStruct((LANES // W, W, 128), F32),
                   jax.ShapeDtypeStruct((1, LANES), F32), jax.ShapeDtypeStruct((1, LANES), F32)],
        scratch=[pltpu.VMEM((T + 16, W), F32)] * 2, vmem_mb=56, comm=comm,
        operands=(gs_re, gs_im, st_re, st_im, su_b, ds_b, wb_re, wb_im, a_re, a_im))


def _mix_bwd_a(dyc_b, w_co4, pc, z_b, conv_w, dsu_ssm, du_dir, dproj, dres, w_mix4, tm, comm=()):
    T = dres.shape[0]
    nt = T // tm

    def body(dyc_ref, wco_ref, pc_ref, halo_ref, z_ref, cw_ref, dsu_ref, dud_ref, dpin_ref, dres_ref, w_ref,
             dp_ref, dx_ref, dcw_ref, dcb_ref, dzbuf, qbuf):
        i = pl.program_id(0)
        ii = nt - 1 - i

        @pl.when(i == 0)
        def _():
            dzbuf[pl.ds(tm, 8), :] = jnp.zeros((8, CONV), F32)

        dyc = dyc_ref[...]
        dyin = (_nt(dyc[:, 0:256], wco_ref[0]) + _nt(dyc[:, 256:512], wco_ref[1])
                + _nt(dyc[:, 512:768], wco_ref[2]) + _nt(dyc[:, 768:1024], wco_ref[3]))
        cbv = pc_ref[:, 0:CONV].astype(F32)
        ccv = pc_ref[:, CONV:2 * CONV].astype(F32)
        chv = pc_ref[:, 2 * CONV:3 * CONV].astype(F32)
        dcbv = dyin * z_ref[...].astype(F32)
        dz = dyin * cbv
        dzbuf[pl.ds(0, tm), :] = dz
        cw = cw_ref[...]
        dq = cw[2:3] * dz + cw[1:2] * dzbuf[pl.ds(1, tm), :] + cw[0:1] * dzbuf[pl.ds(2, tm), :]
        dzbuf[pl.ds(tm, 8), :] = dz[0:8]
        q = ccv * chv
        hq = halo_ref[:, CONV:2 * CONV].astype(F32) * halo_ref[:, 2 * CONV:3 * CONV].astype(F32)
        qbuf[pl.ds(0, 8), :] = jnp.where(ii > 0, hq, jnp.zeros_like(hq))
        qbuf[pl.ds(8, tm), :] = q
        pw = jnp.concatenate([_rowsum(dz * qbuf[pl.ds(6, tm), :]), _rowsum(dz * qbuf[pl.ds(7, tm), :]),
                              _rowsum(dz * q), jnp.zeros((5, CONV), F32)], axis=0)
        pbias = _rowsum(dz)

        @pl.when(i == 0)
        def _():
            dcw_ref[...] = pw
            dcb_ref[...] = pbias

        @pl.when(i > 0)
        def _():
            dcw_ref[...] += pw
            dcb_ref[...] += pbias

        dp0 = jnp.concatenate([dcbv, dq * chv], axis=1).astype(BF16)
        dp1 = jnp.concatenate([dq * ccv, dsu_ref[...] + dud_ref[...]], axis=1).astype(BF16)
        dp_ref[:, 0:D] = dp0
        dp_ref[:, D:2 * D] = dp1
        dx_ref[...] = (dres_ref[...] + _nt(dp0, w_ref[0]) + _nt(dp1, w_ref[1])
                       + _nt(dpin_ref[:, 0:D], w_ref[2]) + _nt(dpin_ref[:, D:2 * D], w_ref[3]))

    def tok(n):
        return pl.BlockSpec((tm, n), lambda i: (nt - 1 - i, 0))

    def full(shape):
        return pl.BlockSpec(shape, lambda i: (0,) * len(shape))

    halo = pl.BlockSpec((8, 3 * CONV), lambda i: (jnp.maximum((nt - 1 - i) * (tm // 8) - 1, 0), 0))
    return _pcall(
        body, name="mix_bwd_a", grid=(nt,),
        in_specs=[tok(D), full((4, CONV, 256)), tok(3 * CONV), halo, tok(CONV), full((3, CONV)),
                  tok(SSM), tok(SSM), pl.BlockSpec((tm, 2 * D), lambda i: (nt - 1 - i, 1)), tok(D),
                  full((4, D, D))],
        out_specs=[pl.BlockSpec((tm, 2 * D), lambda i: (nt - 1 - i, 0)), tok(D), full((8, CONV)), full((1, CONV))],
        out_shape=[jax.ShapeDtypeStruct((T, 4 * D), BF16), jax.ShapeDtypeStruct((T, D), F32),
                   jax.ShapeDtypeStruct((8, CONV), F32), jax.ShapeDtypeStruct((1, CONV), F32)],
        scratch=[pltpu.VMEM((tm + 8, CONV), F32), pltpu.VMEM((tm + 8, CONV), F32)],
        aliases={8: 0}, vmem_mb=56, comm=comm,
        operands=(dyc_b, w_co4, pc, pc, z_b, conv_w, dsu_ssm, du_dir, dproj, dres, w_mix4))


def _zoh(lam_re, lam_im, log_step, b_re, b_im):
    dt = jnp.exp(log_step)[:, None]
    mag = jnp.exp(lam_re * dt)
    abr, abi = mag * jnp.cos(lam_im * dt), mag * jnp.sin(lam_im * dt)
    nr, ni = abr - 1.0, abi
    den = lam_re * lam_re + lam_im * lam_im
    cr = (nr * lam_re + ni * lam_im) / den
    ci = (ni * lam_re - nr * lam_im) / den
    bbr = cr[..., None] * b_re - ci[..., None] * b_im
    bbi = cr[..., None] * b_im + ci[..., None] * b_re
    return abr, abi, bbr, bbi


def _wb_blocks(bb):
    eye = jnp.eye(GROUPS, dtype=F32)
    full = jnp.einsum("gni,gh->gihn", bb, eye).reshape(4, 128, 8, SCAN_W)
    return jnp.stack([full[j // 2, :, j, :] for j in range(8)]).astype(BF16)


def _wc_blocks(cc):
    eye = jnp.eye(GROUPS, dtype=F32)
    full = jnp.einsum("gin,gh->gnhi", cc, eye).reshape(4, 512, 4, 128)
    return jnp.stack([full[J, :, J, :] for J in range(4)]).astype(BF16)


_G = np.arange(GROUPS)


def _wb_diag(dwb8):
    d5 = dwb8.reshape(8, 8, 16, 4, 64)
    return d5[_G // 4, _G % 8, :, _G % 4, :].transpose(0, 2, 1)


def _wc_diag(dwc8):
    d5 = dwc8.reshape(8, 4, 64, 8, 16)
    return d5[_G // 4, _G % 4, :, _G % 8, :].transpose(0, 2, 1)


def _where():
    x, y, c = lax.axis_index("x"), lax.axis_index("y"), lax.axis_index("c")
    return x, y, c, 2 * x + y


def _chip_dev(k, c):
    return (k // 2, k % 2, c)


def _slot_cast(meidx, w, dtype, name):
    R, C = w.shape
    tr = _row_tile(R)

    def body(m_ref, w_ref, o_ref):
        o_ref[...] = w_ref[...].astype(dtype)

    gs = pltpu.PrefetchScalarGridSpec(
        num_scalar_prefetch=1, grid=(R // tr,),
        in_specs=[pl.BlockSpec((tr, C), lambda i, m: (i, 0))],
        out_specs=pl.BlockSpec((None, tr, C), lambda i, m: (m[0], i, 0)))
    return pl.pallas_call(
        body, name=name, grid_spec=gs, out_shape=_hbm_out(jax.ShapeDtypeStruct((4, R, C), dtype)),
        compiler_params=_cp(32, 1),
    )(meidx, *_hbm(w))


def _gather_payload(bufs):
    n = len(bufs)

    def half(ref, w, k, cc):
        h = bufs[w].shape[1] // 2
        return ref.at[k, pl.ds(cc * h, h)]

    def ici(ins, outs, sems, w, s):
        x, y, c, me = _where()
        k = (me + 1 + s) % 4
        return pltpu.make_async_remote_copy(
            src_ref=half(ins[w], w, me, c), dst_ref=half(outs[w], w, me, c), send_sem=sems[0].at[3 * w + s],
            recv_sem=sems[1].at[3 * w + s], device_id=_chip_dev(k, c), device_id_type=MESH)

    def landed(outs, sems, w, s):
        x, y, c, me = _where()
        j = (me + 3 - s) % 4
        return pltpu.make_async_remote_copy(
            src_ref=half(outs[w], w, j, c), dst_ref=half(outs[w], w, j, c), send_sem=sems[0].at[3 * w + s],
            recv_sem=sems[1].at[3 * w + s], device_id=(x, y, 1 - c), device_id_type=MESH)

    def passed(outs, sems, w, s, cc):
        x, y, c, me = _where()
        j = (me + 3 - s) % 4
        return pltpu.make_async_remote_copy(
            src_ref=half(outs[w], w, j, cc), dst_ref=half(outs[w], w, j, cc), send_sem=sems[2].at[3 * w + s],
            recv_sem=sems[3].at[3 * w + s], device_id=(x, y, 1 - c), device_id_type=MESH)

    pairs = [(w, s) for w in range(n) for s in range(3)]

    def start(ins, outs, sems):
        for w, s in pairs:
            ici(ins, outs, sems, w, s).start()

    def finish(ins, outs, sems):
        _, _, c, _ = _where()
        for w, s in pairs:
            landed(outs, sems, w, s).wait_recv()
            passed(outs, sems, w, s, c).start()
        for w, s in pairs:
            passed(outs, sems, w, s, 1 - c).wait_recv()
        for w, s in pairs:
            ici(ins, outs, sems, w, s).wait_send()
            passed(outs, sems, w, s, c).wait_send()

    return _Payload(bufs, [jax.ShapeDtypeStruct(b.shape, b.dtype) for b in bufs], {w: w for w in range(n)},
                    [pltpu.SemaphoreType.DMA((3 * n,))] * 4, start, finish)


def _sym_payload(operands, outs, copies, n_copies):
    def start(ins, outs_, sems):
        for cp in copies(ins, outs_, sems[0], sems[1]):
            cp.start()

    def finish(ins, outs_, sems):
        for cp in copies(ins, outs_, sems[0], sems[1]):
            cp.wait()

    p = _Payload(operands, outs, {}, [pltpu.SemaphoreType.DMA((n_copies,))] * 2, start, finish)
    p.copies, p.n_copies = copies, n_copies
    return p


def _swap_payload(g4s):
    def copies(ins, outs, ss, rs):
        x, y, c, me = _where()
        cps = []
        for w, g in enumerate(g4s):
            h = g.shape[1] // 2
            cps.append(pltpu.make_async_remote_copy(
                src_ref=ins[w].at[:, pl.ds((1 - c) * h, h)], dst_ref=outs[w], send_sem=ss.at[w],
                recv_sem=rs.at[w], device_id=(x, y, 1 - c), device_id_type=MESH))
        return cps

    outs = [jax.ShapeDtypeStruct((4, g.shape[1] // 2, g.shape[2]), g.dtype) for g in g4s]
    return _sym_payload(g4s, outs, copies, len(g4s))


def _exchange_payload(pbs):
    def copies(ins, outs, ss, rs):
        x, y, c, me = _where()
        cps = []
        for w in range(len(pbs)):
            for s in range(3):
                k = (me + 1 + s) % 4
                cps.append(pltpu.make_async_remote_copy(
                    src_ref=ins[w].at[k], dst_ref=outs[w].at[2 - s], send_sem=ss.at[3 * w + s],
                    recv_sem=rs.at[3 * w + s], device_id=_chip_dev(k, c), device_id_type=MESH))
        return cps

    outs = [jax.ShapeDtypeStruct((3,) + p.shape[1:], p.dtype) for p in pbs]
    return _sym_payload(pbs, outs, copies, 3 * len(pbs))


HBM_REF = pl.BlockSpec(memory_space=pltpu.HBM)
SEM_REF = pl.BlockSpec(memory_space=pltpu.SEMAPHORE)
DATAFLOW = pltpu.SideEffectType.DATAFLOW_SIDE_EFFECTING


class _SemList:
    def __init__(self, refs):
        self.refs = refs

    @property
    def at(self):
        return self.refs


def _split_start(p, name):
    n_in, n_out, nc = len(p.operands), len(p.outs), p.n_copies
    lands = getattr(p, "lands", None) or [lax.empty(s.shape, s.dtype) for s in p.outs]

    def body(*refs):
        ins, lnd = refs[:n_in], refs[n_in:n_in + n_out]
        sems = refs[n_in + n_out:n_in + n_out + 2 * nc]
        for cp in p.copies(ins, lnd, _SemList(sems[:nc]), _SemList(sems[nc:])):
            cp.start()
        refs[-1][...] = jnp.zeros((8, 128), F32)

    res = pl.pallas_call(
        body, name=name,
        in_specs=[HBM_REF] * (n_in + n_out),
        out_specs=[SEM_REF] * (2 * nc) + [HBM_REF] * (n_in + n_out) + [VMEM_FULL],
        out_shape=([pltpu.SemaphoreType.DMA(())] * (2 * nc) + _hbm_out(p.operands) + _hbm_out(lands)
                   + [jax.ShapeDtypeStruct((8, 128), F32)]),
        input_output_aliases={i: 2 * nc + i for i in range(n_in + n_out)},
        compiler_params=pltpu.CompilerParams(has_side_effects=DATAFLOW),
    )(*_hbm(*p.operands, *lands))
    k = 2 * nc
    return list(res[:k]), list(res[k:k + n_in]), list(res[k + n_in:k + n_in + n_out]), res[-1]


def _split_wait(p, handle, after, name):
    sems, srcs, lands, _ = handle
    n_in, n_out, nc = len(srcs), len(lands), p.n_copies

    def body(*refs):
        ins, lnd = refs[:n_in], refs[n_in:n_in + n_out]
        sm = refs[n_in + n_out:n_in + n_out + 2 * nc]
        for cp in p.copies(ins, lnd, _SemList(sm[:nc]), _SemList(sm[nc:])):
            cp.wait_send()
            cp.wait_recv()

    res = pl.pallas_call(
        body, name=name,
        in_specs=[HBM_REF] * (n_in + n_out) + [SEM_REF] * (2 * nc) + [ANY] * len(after),
        out_specs=[HBM_REF] * (n_in + n_out), out_shape=_hbm_out(srcs) + _hbm_out(lands),
        input_output_aliases={i: i for i in range(n_in + n_out)},
        compiler_params=pltpu.CompilerParams(has_side_effects=DATAFLOW),
    )(*srcs, *lands, *sems, *after)
    return list(res[:n_in]), list(res[n_in:])


def _join_payload(halves):
    def copies(ins, outs, ss, rs):
        x, y, c, me = _where()
        return [pltpu.make_async_remote_copy(
            src_ref=ins[w], dst_ref=outs[w], send_sem=ss.at[w], recv_sem=rs.at[w],
            device_id=(x, y, 1 - c), device_id_type=MESH) for w in range(len(halves))]

    outs = [jax.ShapeDtypeStruct(a.shape, a.dtype) for a in halves]
    return _sym_payload(halves, outs, copies, len(halves))


def _allgather_payload(v):
    def copies(ins, outs, ss, rs):
        x, y, c, me = _where()
        lin = 4 * x + 2 * y + c
        cps = []
        for o in range(1, 8):
            t = (lin + o) % 8
            cps.append(pltpu.make_async_remote_copy(
                src_ref=ins[0], dst_ref=outs[0].at[lin], send_sem=ss.at[o - 1], recv_sem=rs.at[o - 1],
                device_id=(t // 4, (t // 2) % 2, t % 2), device_id_type=MESH))
        return cps

    p = _sym_payload([v], [jax.ShapeDtypeStruct((8,) + v.shape, v.dtype)], copies, 7)
    x, y, c, _ = _where()
    p.lands = [lax.dynamic_update_slice(jnp.zeros((8,) + v.shape, v.dtype), v[None], (4 * x + 2 * y + c, 0, 0))]
    return p


def _sum8(buf, token):
    _, P, C = buf.shape

    def body(b_ref, t_ref, o_ref):
        acc = b_ref[0]
        for d in range(1, 8):
            acc = acc + b_ref[d]
        o_ref[...] = acc

    return pl.pallas_call(
        body, name="sum8", in_specs=[VMEM_FULL, VMEM_FULL], out_specs=VMEM_FULL,
        out_shape=jax.ShapeDtypeStruct((P, C), F32),
        compiler_params=pltpu.CompilerParams(vmem_limit_bytes=32 << 20),
    )(buf, token)


def _row_tile(h):
    for t in (256, 176, 128, 64, 32, 16, 8):
        if h % t == 0:
            return t
    raise ValueError(h)


def _pair_sum(cidx, g4, got, name):
    _, R, C = g4.shape
    h = R // 2
    th = _row_tile(h)

    def body(c_ref, a_ref, b_ref, o_ref, ob_ref):
        sm = a_ref[...] + b_ref[...]
        o_ref[...] = sm
        ob_ref[...] = sm.astype(BF16)

    blk = pl.BlockSpec((None, th, C), lambda k, i, c: (k, i, 0))
    gs = pltpu.PrefetchScalarGridSpec(
        num_scalar_prefetch=1, grid=(4, h // th),
        in_specs=[pl.BlockSpec((None, None, th, C), lambda k, i, c: (k, c[0], i, 0)), blk],
        out_specs=[blk, blk])
    return pl.pallas_call(
        body, name=name, grid_spec=gs,
        out_shape=_hbm_out([jax.ShapeDtypeStruct((4, h, C), F32), jax.ShapeDtypeStruct((4, h, C), BF16)]),
        compiler_params=_cp(32, 2),
    )(cidx, *_hbm(g4.reshape(4, 2, h, C), got))


def _chip_sum(meidx, p32, got, name):
    _, h, C = p32.shape
    th = _row_tile(h)

    def body(m_ref, a_ref, b_ref, o_ref):
        o_ref[...] = ((a_ref[...] + b_ref[0].astype(F32)) + b_ref[1].astype(F32)) + b_ref[2].astype(F32)

    gs = pltpu.PrefetchScalarGridSpec(
        num_scalar_prefetch=1, grid=(h // th,),
        in_specs=[pl.BlockSpec((None, th, C), lambda i, m: (m[0], i, 0)),
                  pl.BlockSpec((3, th, C), lambda i, m: (0, i, 0))],
        out_specs=pl.BlockSpec((th, C), lambda i, m: (i, 0)))
    return pl.pallas_call(
        body, name=name, grid_spec=gs, out_shape=_hbm_out(jax.ShapeDtypeStruct((h, C), F32)),
        compiler_params=_cp(32, 1),
    )(meidx, *_hbm(p32, got))


def _adamw_math(w, g, m, v):
    m2 = B1 * m + (1.0 - B1) * g
    v2 = B2 * v + (1.0 - B2) * (g * g)
    m_hat = m2 / (1.0 - B1 ** STEP)
    v_hat = v2 / (1.0 - B2 ** STEP)
    delta = -LR * (m_hat / (jnp.sqrt(v_hat) + EPS) + WD * w)
    return delta, m2, v2


def _adamw_pair(cidx, w, mine, theirs, m, v, token, name):
    R, C = w.shape
    h = R // 2
    tr = _row_tile(h)
    nh = h // tr

    def body(c_ref, w_ref, a_ref, b_ref, m_ref, v_ref, t_ref, g_ref, d_ref, mo_ref, vo_ref):
        own = (pl.program_id(0) // nh) == c_ref[0]
        g = jnp.where(own, a_ref[...], b_ref[...])
        d, m2, v2 = _adamw_math(w_ref[...], g, m_ref[...], v_ref[...])
        g_ref[...] = g
        d_ref[...] = d
        mo_ref[...] = m2
        vo_ref[...] = v2

    blk = pl.BlockSpec((tr, C), lambda i, c: (i, 0))
    hblk = pl.BlockSpec((tr, C), lambda i, c: (i % nh, 0))
    gs = pltpu.PrefetchScalarGridSpec(
        num_scalar_prefetch=1, grid=(R // tr,),
        in_specs=[blk, hblk, hblk, blk, blk, pl.BlockSpec((8, 128), lambda i, c: (0, 0))], out_specs=[blk] * 4)
    return pl.pallas_call(
        body, name=name, grid_spec=gs, out_shape=_hbm_out([jax.ShapeDtypeStruct((R, C), F32)] * 4),
        compiler_params=_cp(32, 1),
    )(cidx, *_hbm(w, mine, theirs, m, v), token)


def _adamw(w, g, m, v, name):
    R, C = w.shape
    tr = _row_tile(R)

    def body(w_ref, g_ref, m_ref, v_ref, d_ref, mo_ref, vo_ref):
        d, m2, v2 = _adamw_math(w_ref[...], g_ref[...], m_ref[...], v_ref[...])
        d_ref[...] = d
        mo_ref[...] = m2
        vo_ref[...] = v2

    blk = pl.BlockSpec((tr, C), lambda i: (i, 0))
    return pl.pallas_call(
        body, name=name, grid=(R // tr,), in_specs=[blk] * 4, out_specs=[blk] * 3,
        out_shape=_hbm_out([jax.ShapeDtypeStruct((R, C), F32)] * 3),
        compiler_params=_cp(32, 1),
    )(*_hbm(w, g, m, v))


def _pack(arrs):
    flat = jnp.concatenate([a.reshape(-1).astype(F32) for a in arrs])
    rows = -(-flat.shape[0] // 1024)
    rows = -(-rows // 8) * 8
    return jnp.pad(flat, (0, rows * 1024 - flat.shape[0])).reshape(rows, 1024)


def _unpack(packed, shapes):
    flat = packed.reshape(-1)
    out, off = [], 0
    for s in shapes:
        n = math.prod(s)
        out.append(flat[off:off + n].reshape(s))
        off += n
    return out


BIG = ["ffn1_w_in", "ffn1_w_out", "mix_w_in", "conv_w_out", "ssm_w_glu", "mix_w_out",
       "ffn2_w_in", "ffn2_w_out", "ple_w_in", "ple_w_gate"]
SMALL = ["ln1_g", "ln1_b", "conv_w", "conv_b", "ssm_lam_re", "ssm_lam_im", "ssm_log_step", "ssm_b_re", "ssm_b_im",
         "ssm_c_re", "ssm_c_im", "ssm_d", "ln2_g", "ln2_b", "ln3_g", "ln3_b", "ln4_g", "ln4_b"]
WEIGHTS = ["ffn1_w_in", "ffn1_w_out", "ln1_g", "ln1_b", "mix_w_in", "conv_w", "conv_b", "conv_w_out",
           "ssm_lam_re", "ssm_lam_im", "ssm_log_step", "ssm_b_re", "ssm_b_im", "ssm_c_re", "ssm_c_im", "ssm_d",
           "ssm_w_glu", "mix_w_out", "ln2_g", "ln2_b", "ffn2_w_in", "ffn2_w_out", "ln3_g", "ln3_b",
           "ple_w_in", "ple_w_gate", "ln4_g", "ln4_b"]


class _NoComm:
    def __init__(self, W):
        self.W, self.G, self.raw = dict(W), {}, None

    def carry(self, name):
        return ()

    def landed(self, name, got):
        pass

    def grad(self, name, g4):
        self.G[name] = g4

    def small(self, raw):
        self.raw = raw


def _local_step(x, p, target, sp, sched, tm_ffn, tm_mix):
    W = sched.W
    abr, abi, bbr, bbi = _zoh(sp["ssm_lam_re"], sp["ssm_lam_im"], sp["ssm_log_step"], sp["ssm_b_re"], sp["ssm_b_im"])
    wb_re, wb_im = _wb_blocks(bbr), _wb_blocks(bbi)
    wc_re4, wc_im4 = _wc_blocks(sp["ssm_c_re"]), _wc_blocks(-sp["ssm_c_im"])
    a_re, a_im = abr.reshape(1, LANES), abi.reshape(1, LANES)
    dvec = sp["ssm_d"].reshape(1, SSM)

    def run(fn, name, *args, **kw):
        outs, got = fn(*args, comm=sched.carry(name), **kw)
        sched.landed(name, got)
        return outs

    def dw(name, wname, a, b, tk, tn, shape4, shard_cols=None, interleaved=False):
        out, got = _mm_tn(a, b, tk, tn, name, shard_cols=shard_cols, interleaved=interleaved,
                          comm=sched.carry(name))
        sched.landed(name, got)
        sched.grad(wname, out.reshape(shape4))

    xb = x.astype(BF16)
    h1, r1, x1, x1b = run(_ffn_fwd, "ffn1_fwd", x, xb, W["ffn1_w_in"], W["ffn1_w_out"].reshape(2, FFH, D),
                          sp["ln1_g"], sp["ln1_b"], tm_ffn, "ffn1_fwd")
    conv_w = W["conv_w"][:, 0:3, :].transpose(1, 0, 2).reshape(3, CONV)
    pc, z_b, yin_b, su, su_b, g_conv, g_ssm, y_conv = _mix_fwd_a(
        x1b, W["mix_w_in"], conv_w, sp["conv_b"], W["conv_w_out"], tm_mix)
    st_re, st_im = run(_s5_scan_fwd, "s5_scan_fwd", su_b, wb_re, wb_im, a_re, a_im)
    w_mo = W["mix_w_out"].reshape(D, D)
    s, sg_b, ga, gb, merged_b, r2, x2, x2b = run(
        _mix_fwd_b, "mix_fwd_b", st_re, st_im, wc_re4, wc_im4, su, dvec, W["ssm_w_glu"], g_conv, g_ssm, y_conv,
        w_mo, x1, sp["ln2_g"], sp["ln2_b"], tm_mix)
    w2o2 = W["ffn2_w_out"].reshape(2, FFH, D)
    h2, r3, x3, x3b = run(_ffn_fwd, "ffn2_fwd", x2, x2b, W["ffn2_w_in"], w2o2, sp["ln3_g"], sp["ln3_b"], tm_ffn,
                          "ffn2_fwd")
    loss_part, dx3, p_b, dpw_b, dgt_b, dg4, db4 = _ple_loss(
        x3, x3b, p, W["ple_w_in"], W["ple_w_gate"].reshape(D, D), sp["ln4_g"], sp["ln4_b"], target, tm_mix)

    dw("dw_ple_gate", "ple_w_gate", x3b, dgt_b, 512, 1024, (4, 256, D))
    dw("dw_ple_in", "ple_w_in", p_b, dpw_b, 256, 256, (4, 256, 256), shard_cols=256)
    dx2, dh2, a2_b, df2_b, dg3, db3 = run(_ffn_bwd, "ffn2_bwd", dx3, r3, sp["ln3_g"], h2, W["ffn2_w_in"], w2o2,
                                          tm_ffn, "ffn2_bwd")
    dw("dw_ffn2_in", "ffn2_w_in", x2b, dh2, 512, FFH, (4, D, FFH), shard_cols=FFH, interleaved=True)
    dw("dw_ffn2_out", "ffn2_w_out", a2_b, df2_b, FFH, 1024, (4, FF // 4, D))
    (dres, dmix_b, dgl_b, ds_b, du_dir, gs_re, gs_im, dyc_b, dproj, dg2, db2, dd) = run(
        _mix_bwd_b, "mix_bwd_b", dx2, r2, sp["ln2_g"], w_mo, g_conv, g_ssm, y_conv, ga, gb, s, su, dvec,
        W["ssm_w_glu"], wc_re4, wc_im4, tm_mix)
    dw("dw_mix_out", "mix_w_out", merged_b, dmix_b, 512, 1024, (4, 256, D))
    dw("dw_glu", "ssm_w_glu", sg_b, dgl_b, 512, 512, (4, SSM, 512), shard_cols=512)
    dsu_ssm, dwb_re, dwb_im, dwc_re, dwc_im, da_re, da_im = run(
        _s5_scan_bwd, "s5_scan_bwd", gs_re, gs_im, st_re, st_im, su_b, ds_b, wb_re, wb_im, a_re, a_im)
    dw("dw_conv_out", "conv_w_out", yin_b, dyc_b, 512, 256, (4, CONV, 256), shard_cols=256)
    dproj, dx1, dcw8, dcb = run(_mix_bwd_a, "mix_bwd_a", dyc_b, W["conv_w_out"], pc, z_b, conv_w, dsu_ssm,
                                du_dir, dproj, dres, W["mix_w_in"], tm_mix)
    dw("dw_mix_in", "mix_w_in", x1b, dproj, 512, 1024, (4, D, D), shard_cols=1024)
    dx0, dh1, a1_b, df1_b, dg1, db1 = run(_ffn_bwd, "ffn1_bwd", dx1, r1, sp["ln1_g"], h1, W["ffn1_w_in"],
                                          W["ffn1_w_out"].reshape(2, FFH, D), tm_ffn, "ffn1_bwd")
    sched.small(dict(
        ln1_g=dg1, ln1_b=db1, ln2_g=dg2, ln2_b=db2, ln3_g=dg3, ln3_b=db3, ln4_g=dg4, ln4_b=db4,
        conv_w=dcw8[0:3], conv_b=dcb,
        a_re=da_re.reshape(GROUPS, STATE), a_im=da_im.reshape(GROUPS, STATE),
        bb_re=_wb_diag(dwb_re), bb_im=_wb_diag(dwb_im),
        ssm_c_re=_wc_diag(dwc_re), ssm_c_im=-_wc_diag(dwc_im), ssm_d=dd.reshape(GROUPS, 16),
        loss=loss_part[0:1, 0]))
    dw("dw_ffn1_in", "ffn1_w_in", xb, dh1, 512, FFH, (4, D, FFH), shard_cols=FFH, interleaved=True)
    dw("dw_ffn1_out", "ffn1_w_out", a1_b, df1_b, FFH, 1024, (4, FF // 4, D))
    return loss_part[0, 0], dx0


RAW_ORDER = ["ln1_g", "ln1_b", "ln2_g", "ln2_b", "ln3_g", "ln3_b", "ln4_g", "ln4_b", "conv_w", "conv_b",
             "a_re", "a_im", "bb_re", "bb_im", "ssm_c_re", "ssm_c_im", "ssm_d", "loss"]

GATHER_FIRST = ["ffn1_w_in", "ffn1_w_out"]
GATHER_AT = {"ffn1_fwd": ["mix_w_in", "conv_w_out", "conv_w", "ssm_w_glu", "mix_w_out"],
             "s5_scan_fwd": ["ffn2_w_in"], "mix_fwd_b": ["ffn2_w_out"], "ffn2_fwd": ["ple_w_in", "ple_w_gate"]}
REDUCE_GROUP = {"ple": ["ple_w_gate", "ple_w_in"], "ffn2": ["ffn2_w_in", "ffn2_w_out"],
                "mix": ["mix_w_out", "ssm_w_glu", "conv_w_out", "mix_w_in"], "ffn1": ["ffn1_w_in", "ffn1_w_out"]}
REDUCE_AT = {"ffn2_bwd": [("swap", "ple")], "dw_ffn2_in": [("exchange", "ple")],
             "mix_bwd_b": [("swap", "ffn2"), ("join", "ple")], "s5_scan_bwd": [("exchange", "ffn2")],
             "mix_bwd_a": [("join", "ffn2")], "ffn1_bwd": [("swap", "mix")]}
BEGIN_AT = {"dw_ffn1_in": [("small", None), ("exchange", "mix")]}
LAST_GROUP = "ffn1"


class _Sched:
    def __init__(self, bufs, cidx, meidx):
        self.bufs, self.cidx, self.meidx = bufs, cidx, meidx
        self.W, self.G, self.raw, self.small_buf = {}, {}, None, None
        self.got1, self.p32, self.pbf, self.got2, self.half, self.theirs = {}, {}, {}, {}, {}, {}
        self._open, self._split = [], {}
        self._standalone("gather_ffn1", [("gather", GATHER_FIRST)])

    def _payload(self, stage, key):
        if stage == "gather":
            return _gather_payload([self.bufs[n] for n in key])
        if stage == "small":
            return _allgather_payload(_pack([self.raw[k] for k in RAW_ORDER]))
        names = REDUCE_GROUP[key]
        if stage == "swap":
            return _swap_payload([self.G[n] for n in names])
        if stage == "exchange":
            for n in names:
                self.p32[n], self.pbf[n] = _pair_sum(self.cidx, self.G[n], self.got1[n], "pair_sum_" + n)
            return _exchange_payload([self.pbf[n] for n in names])
        for n in names:
            self.half[n] = _chip_sum(self.meidx, self.p32[n], self.got2[n], "chip_sum_" + n)
        return _join_payload([self.half[n] for n in names])

    def _store(self, stages, got):
        for (stage, key), outs in zip(stages, got):
            if stage == "gather":
                self.W.update(zip(key, outs))
            elif stage == "small":
                self.small_buf = outs[0]
            else:
                {"swap": self.got1, "exchange": self.got2, "join": self.theirs}[stage].update(
                    zip(REDUCE_GROUP[key], outs))

    def _standalone(self, name, stages):
        self._store(stages, _comm_call(name, [self._payload(s, k) for s, k in stages]))

    def carry(self, name):
        tokens = [self._begin(stage, key) for stage, key in BEGIN_AT.get(name, [])]
        self._open = [("gather", GATHER_AT[name])] if name in GATHER_AT else []
        self._open += REDUCE_AT.get(name, [])
        comm = [self._payload(s, k) for s, k in self._open]
        if tokens:
            comm.append(_Payload(tokens, [], {}, [], lambda *a: None, lambda *a: None))
        return tuple(comm)

    def landed(self, name, got):
        self._store(self._open, got)

    def grad(self, name, g4):
        self.G[name] = g4

    def small(self, raw):
        self.raw = raw

    def _begin(self, stage, key):
        p = self._payload(stage, key)
        self._split[stage, key] = (p, _split_start(p, "%s_%s_start" % (stage, key)))
        return self._split[stage, key][1][3]

    def _end(self, stage, key, after):
        p, handle = self._split.pop((stage, key))
        srcs, lands = _split_wait(p, handle, after, "%s_%s_wait" % (stage, key))
        if stage == "swap":
            self.G.update(zip(REDUCE_GROUP[key], srcs))
        self._store([(stage, key)], [lands])

    def tail_begin(self):
        return self._begin("swap", LAST_GROUP)

    def tail_mid(self, after):
        self._end("swap", LAST_GROUP, after)
        token = self._begin("exchange", LAST_GROUP)
        self._end("small", None, [token])
        self._end("exchange", "mix", [token])
        self._standalone("reduce_tail_join_mix", [("join", "mix")])
        return token

    def tail_end(self, after):
        self._end("exchange", LAST_GROUP, after)
        self._standalone("reduce_tail_join", [("join", LAST_GROUP)])


def _small_grads(raw_sum, sp):
    _, vjp = jax.vjp(_zoh, sp["ssm_lam_re"], sp["ssm_lam_im"], sp["ssm_log_step"], sp["ssm_b_re"], sp["ssm_b_im"])
    d_lre, d_lim, d_ls, d_bre, d_bim = vjp((raw_sum["a_re"], raw_sum["a_im"], raw_sum["bb_re"], raw_sum["bb_im"]))
    g = {k: raw_sum[k] for k in ("ln1_g", "ln1_b", "ln2_g", "ln2_b", "ln3_g", "ln3_b", "ln4_g", "ln4_b",
                                 "conv_w", "conv_b", "ssm_c_re", "ssm_c_im", "ssm_d")}
    g.update(ssm_lam_re=d_lre, ssm_lam_im=d_lim, ssm_log_step=d_ls, ssm_b_re=d_bre, ssm_b_im=d_bim)
    return g


def kernel(x, p, ffn1_w_in, ffn1_w_out, ln1_g, ln1_b, mix_w_in, conv_w, conv_b, conv_w_out, ssm_lam_re, ssm_lam_im, ssm_log_step, ssm_b_re, ssm_b_im, ssm_c_re, ssm_c_im, ssm_d, ssm_w_glu, mix_w_out, ln2_g, ln2_b, ffn2_w_in, ffn2_w_out, ln3_g, ln3_b, ple_w_in, ple_w_gate, ln4_g, ln4_b, loss_target, m_ffn1_w_in, m_ffn1_w_out, m_ln1_g, m_ln1_b, m_mix_w_in, m_conv_w, m_conv_b, m_conv_w_out, m_ssm_lam_re, m_ssm_lam_im, m_ssm_log_step, m_ssm_b_re, m_ssm_b_im, m_ssm_c_re, m_ssm_c_im, m_ssm_d, m_ssm_w_glu, m_mix_w_out, m_ln2_g, m_ln2_b, m_ffn2_w_in, m_ffn2_w_out, m_ln3_g, m_ln3_b, m_ple_w_in, m_ple_w_gate, m_ln4_g, m_ln4_b, v_ffn1_w_in, v_ffn1_w_out, v_ln1_g, v_ln1_b, v_mix_w_in, v_conv_w, v_conv_b, v_conv_w_out, v_ssm_lam_re, v_ssm_lam_im, v_ssm_log_step, v_ssm_b_re, v_ssm_b_im, v_ssm_c_re, v_ssm_c_im, v_ssm_d, v_ssm_w_glu, v_mix_w_out, v_ln2_g, v_ln2_b, v_ffn2_w_in, v_ffn2_w_out, v_ln3_g, v_ln3_b, v_ple_w_in, v_ple_w_gate, v_ln4_g, v_ln4_b):
    args = dict(locals())
    w = {n: args[n] for n in WEIGHTS}
    m = {n: args["m_" + n] for n in WEIGHTS}
    v = {n: args["v_" + n] for n in WEIGHTS}
    _, _, c, me = _where()
    cidx = jnp.reshape(c, (1,)).astype(jnp.int32)
    meidx = jnp.reshape(me, (1,)).astype(jnp.int32)

    bufs = {n: _slot_cast(meidx, w[n][0], BF16, "cast_" + n) for n in BIG}
    bufs["conv_w"] = _slot_cast(meidx, jnp.pad(conv_w[0], ((0, 13), (0, 0))), F32, "cast_conv_w")
    sched = _Sched(bufs, cidx, meidx)

    sp = {n: (w[n] if w[n].ndim == 2 and n != "ssm_log_step" else w[n][0]) for n in SMALL if n != "conv_w"}
    loss_part, dx0 = _local_step(x[0], p[0, 0], loss_target[0], sp, sched, 256, 256)
    out_g, out_d, out_m, out_v = {}, {}, {}, {}

    def big_adamw(names, token):
        for n in names:
            g, dl, mn, vn = _adamw_pair(cidx, w[n][0], sched.half[n], sched.theirs[n], m[n][0], v[n][0], token,
                                        "adamw_" + n)
            out_g[n], out_d[n], out_m[n], out_v[n] = g[None], dl[None], mn[None], vn[None]

    first = REDUCE_GROUP["ple"] + REDUCE_GROUP["ffn2"]
    big_adamw(first, sched.tail_begin())
    token = sched.tail_mid([out_v[n] for n in first])

    raw_shapes = [sched.raw[k].shape for k in RAW_ORDER]
    raw_sum = dict(zip(RAW_ORDER, _unpack(_sum8(sched.small_buf, token), raw_shapes)))
    loss = raw_sum["loss"][0]
    sg = _small_grads(raw_sum, sp)
    sg["conv_w"] = lax.dynamic_slice_in_dim(sg["conv_w"], me * 128, 128, axis=1)
    small_shapes = [w[n].shape for n in SMALL]
    gp = _pack([sg[n] for n in SMALL])
    d_s, m_s, v_s = _adamw(_pack([w[n] for n in SMALL]), gp, _pack([m[n] for n in SMALL]),
                           _pack([v[n] for n in SMALL]), "adamw_small")

    for n, a, b_, c_, d_ in zip(SMALL, _unpack(gp, small_shapes), _unpack(d_s, small_shapes),
                                _unpack(m_s, small_shapes), _unpack(v_s, small_shapes)):
        out_g[n], out_d[n], out_m[n], out_v[n] = a, b_, c_, d_
    big_adamw(REDUCE_GROUP["mix"], token)
    sched.tail_end([d_s] + [out_v[n] for n in REDUCE_GROUP["mix"]])
    big_adamw(REDUCE_GROUP[LAST_GROUP], token)

    return (loss, dx0[None], *[out_g[n] for n in WEIGHTS], *[out_d[n] for n in WEIGHTS],
            *[out_m[n] for n in WEIGHTS], *[out_v[n] for n in WEIGHTS])
```

```python
import functools
import math

import jax
import jax.numpy as jnp
import numpy as np
from jax import lax
from jax.experimental import pallas as pl
from jax.experimental.pallas import tpu as pltpu

F32, BF16 = jnp.float32, jnp.bfloat16
D = 1024
FF = 2816
FFH = FF // 2
CONV = 512
SSM = 512
GROUPS = 32
STATE = 64
LANES = GROUPS * STATE
SCAN_W = 256
SCAN_R = 256
ALPHA = 2.0 ** 0.25
LN_EPS = 1e-5
GELU_C = math.sqrt(2.0 / math.pi)
B1, B2, LR, EPS, WD, STEP = 0.9, 0.999, 0.001, 1e-8, 0.01, 10
MESH = pl.DeviceIdType.MESH
ANY = pl.BlockSpec(memory_space=pl.ANY)
VMEM_FULL = pl.BlockSpec(memory_space=pltpu.VMEM)


def _cp(vmem_mb=48, n_axes=1):
    return pltpu.CompilerParams(vmem_limit_bytes=vmem_mb << 20,
                                dimension_semantics=("arbitrary",) * n_axes)


def _hbm(*arrs):
    return [pltpu.with_memory_space_constraint(a, pltpu.HBM) for a in arrs]


def _hbm_out(shapes):
    if isinstance(shapes, (list, tuple)):
        return [pltpu.HBM(s.shape, s.dtype) for s in shapes]
    return pltpu.HBM(shapes.shape, shapes.dtype)


def _nn(a, b):
    return jnp.dot(a, b, preferred_element_type=F32)


def _nt(a, b):
    return lax.dot_general(a, b, (((1,), (1,)), ((), ())), preferred_element_type=F32)


def _tn(a, b):
    return lax.dot_general(a, b, (((0,), (0,)), ((), ())), preferred_element_type=F32)


def _sig(v):
    return jax.nn.sigmoid(v)


def _ln_stats(r):
    mu = jnp.mean(r, axis=-1, keepdims=True)
    xc = r - mu
    var = jnp.mean(xc * xc, axis=-1, keepdims=True)
    rstd = lax.rsqrt(var + LN_EPS)
    return xc * rstd, rstd


def _ln_bwd(dy, r, g):
    xhat, rstd = _ln_stats(r)
    dyg = dy * g
    m1 = jnp.mean(dyg, axis=-1, keepdims=True)
    m2 = jnp.mean(dyg * xhat, axis=-1, keepdims=True)
    return rstd * (dyg - m1 - xhat * m2), xhat


def _rowsum(v):
    return jnp.sum(v, axis=0, keepdims=True)


class _Payload:
    def __init__(self, operands, outs, aliases, sems, start, finish):
        self.operands, self.outs, self.aliases, self.sems = list(operands), list(outs), dict(aliases), list(sems)
        self.start, self.finish = start, finish


def _split(flat, comm, attr):
    out, i = [], 0
    for p in comm:
        n = len(getattr(p, attr))
        out.append(list(flat[i:i + n]))
        i += n
    return out


def _run_comm(comm, which, cin, cout, csem):
    for p, a, b, s in zip(comm, _split(cin, comm, "operands"), _split(cout, comm, "outs"), _split(csem, comm, "sems")):
        getattr(p, which)(a, b, s)


def _pcall(body, *, name, grid, in_specs, out_specs, out_shape, operands, scratch=(), vmem_mb=48, aliases=None,
           comm=()):
    ni, no, ns = len(in_specs), len(out_specs), len(scratch)
    c_ops = [a for p in comm for a in p.operands]
    c_outs = [s for p in comm for s in p.outs]
    c_sems = [s for p in comm for s in p.sems]
    io = dict(aliases or {})
    off_i, off_o = ni, no
    for p in comm:
        for a, b in p.aliases.items():
            io[off_i + a] = off_o + b
        off_i += len(p.operands)
        off_o += len(p.outs)

    def wrapped(*refs):
        ins, cin = refs[:ni], refs[ni:ni + len(c_ops)]
        o0 = ni + len(c_ops)
        outs, cout = refs[o0:o0 + no], refs[o0 + no:o0 + no + len(c_outs)]
        s0 = o0 + no + len(c_outs)
        scr, csem = refs[s0:s0 + ns], refs[s0 + ns:]
        if comm:
            first = functools.reduce(jnp.logical_and, [pl.program_id(a) == 0 for a in range(len(grid))])
            pl.when(first)(lambda: _run_comm(comm, "start", cin, cout, csem))
        body(*ins, *outs, *scr)
        if comm:
            last = functools.reduce(jnp.logical_and, [pl.program_id(a) == grid[a] - 1 for a in range(len(grid))])
            pl.when(last)(lambda: _run_comm(comm, "finish", cin, cout, csem))

    res = pl.pallas_call(
        wrapped, name=name, grid=grid,
        in_specs=list(in_specs) + [ANY] * len(c_ops), out_specs=list(out_specs) + [ANY] * len(c_outs),
        out_shape=_hbm_out(list(out_shape) + c_outs), scratch_shapes=list(scratch) + c_sems,
        input_output_aliases=io,
        compiler_params=pltpu.CompilerParams(vmem_limit_bytes=vmem_mb << 20,
                                             dimension_semantics=("arbitrary",) * len(grid),
                                             has_side_effects=bool(comm)),
    )(*_hbm(*operands, *c_ops))
    return list(res[:no]), _split(res[no:], comm, "outs")


def _comm_call(name, comm):
    c_ops = [a for p in comm for a in p.operands]
    c_outs = [s for p in comm for s in p.outs]
    c_sems = [s for p in comm for s in p.sems]
    io, off_i, off_o = {}, 0, 0
    for p in comm:
        for a, b in p.aliases.items():
            io[off_i + a] = off_o + b
        off_i += len(p.operands)
        off_o += len(p.outs)

    def body(*refs):
        cin, cout = refs[:len(c_ops)], refs[len(c_ops):len(c_ops) + len(c_outs)]
        csem = refs[len(c_ops) + len(c_outs):]
        _run_comm(comm, "start", cin, cout, csem)
        _run_comm(comm, "finish", cin, cout, csem)

    res = pl.pallas_call(
        body, name=name, in_specs=[ANY] * len(c_ops), out_specs=[ANY] * len(c_outs), out_shape=_hbm_out(c_outs),
        scratch_shapes=c_sems, input_output_aliases=io,
        compiler_params=pltpu.CompilerParams(has_side_effects=True),
    )(*_hbm(*c_ops))
    return _split(res, comm, "outs")


def _ffn_fwd(x, xb, w_in4, w_out2, g, b, tm, name, comm=()):
    T = x.shape[0]

    def body(x_ref, xb_ref, wg_ref, wu_ref, wo_ref, g_ref, b_ref, h_ref, r_ref, xo_ref, xob_ref, acc):
        k = pl.program_id(1)
        xv = xb_ref[...]
        gt = _nn(xv, wg_ref[...])
        up = _nn(xv, wu_ref[...])
        a = (gt * _sig(gt) * up).astype(BF16)
        h_ref[:, 0:FFH] = gt.astype(BF16)
        h_ref[:, FFH:2 * FFH] = up.astype(BF16)
        acc[...] = jnp.where(k == 0, 0.0, acc[...]) + _nn(a, wo_ref[...])

        @pl.when(k == 1)
        def _():
            r = ALPHA * x_ref[...] + 0.5 * acc[...]
            xhat, _ = _ln_stats(r)
            xo = xhat * g_ref[...] + b_ref[...]
            r_ref[...] = r
            xo_ref[...] = xo
            xob_ref[...] = xo.astype(BF16)

    tok = pl.BlockSpec((tm, D), lambda i, k: (i, 0))
    vec = pl.BlockSpec((1, D), lambda i, k: (0, 0))
    return _pcall(
        body, name=name, grid=(T // tm, 2),
        in_specs=[tok, tok,
                  pl.BlockSpec((None, D, FFH), lambda i, k: (k, 0, 0)),
                  pl.BlockSpec((None, D, FFH), lambda i, k: (k + 2, 0, 0)),
                  pl.BlockSpec((None, FFH, D), lambda i, k: (k, 0, 0)),
                  vec, vec],
        out_specs=[pl.BlockSpec((tm, FF), lambda i, k: (i, k)), tok, tok, tok],
        out_shape=[jax.ShapeDtypeStruct((T, 2 * FF), BF16), jax.ShapeDtypeStruct((T, D), F32),
                   jax.ShapeDtypeStruct((T, D), F32), jax.ShapeDtypeStruct((T, D), BF16)],
        scratch=[pltpu.VMEM((tm, D), F32)], vmem_mb=56, comm=comm,
        operands=(x, xb, w_in4, w_in4, w_out2, g, b))


def _ffn_bwd(dy, r, g, h, w_in4, w_out2, tm, name, comm=()):
    T = dy.shape[0]

    def body(dy_ref, r_ref, g_ref, h_ref, wg_ref, wu_ref, wo_ref,
             dx_ref, dh_ref, a_ref, df_ref, dg_ref, db_ref, acc, dr_s, dfb_s):
        i, k = pl.program_id(0), pl.program_id(1)

        @pl.when(k == 0)
        def _():
            dyv = dy_ref[...]
            dr, xhat = _ln_bwd(dyv, r_ref[...], g_ref[...])
            pg, pb = _rowsum(dyv * xhat), _rowsum(dyv)

            @pl.when(i == 0)
            def _():
                dg_ref[...] = pg
                db_ref[...] = pb

            @pl.when(i > 0)
            def _():
                dg_ref[...] += pg
                db_ref[...] += pb

            dr_s[...] = dr
            dfb = (0.5 * dr).astype(BF16)
            dfb_s[...] = dfb
            df_ref[...] = dfb

        da = _nt(dfb_s[...], wo_ref[...])
        gt = h_ref[:, 0:FFH].astype(F32)
        up = h_ref[:, FFH:2 * FFH].astype(F32)
        sg = _sig(gt)
        silu = gt * sg
        dgate = (da * up * (sg * (1.0 + gt * (1.0 - sg)))).astype(BF16)
        dup = (da * silu).astype(BF16)
        a_ref[...] = (silu * up).astype(BF16)
        dh_ref[:, 0:FFH] = dgate
        dh_ref[:, FFH:2 * FFH] = dup
        acc[...] = jnp.where(k == 0, 0.0, acc[...]) + _nt(dgate, wg_ref[...]) + _nt(dup, wu_ref[...])

        @pl.when(k == 1)
        def _():
            dx_ref[...] = ALPHA * dr_s[...] + acc[...]

    tok = pl.BlockSpec((tm, D), lambda i, k: (i, 0))
    vec = pl.BlockSpec((1, D), lambda i, k: (0, 0))
    wide = pl.BlockSpec((tm, FF), lambda i, k: (i, k))
    return _pcall(
        body, name=name, grid=(T // tm, 2),
        in_specs=[tok, tok, vec, wide,
                  pl.BlockSpec((None, D, FFH), lambda i, k: (k, 0, 0)),
                  pl.BlockSpec((None, D, FFH), lambda i, k: (k + 2, 0, 0)),
                  pl.BlockSpec((None, FFH, D), lambda i, k: (k, 0, 0))],
        out_specs=[tok, wide, pl.BlockSpec((tm, FFH), lambda i, k: (i, k)), tok, vec, vec],
        out_shape=[jax.ShapeDtypeStruct((T, D), F32), jax.ShapeDtypeStruct((T, 2 * FF), BF16),
                   jax.ShapeDtypeStruct((T, FF), BF16), jax.ShapeDtypeStruct((T, D), BF16),
                   jax.ShapeDtypeStruct((1, D), F32), jax.ShapeDtypeStruct((1, D), F32)],
        scratch=[pltpu.VMEM((tm, D), F32), pltpu.VMEM((tm, D), F32), pltpu.VMEM((tm, D), BF16)],
        vmem_mb=56, comm=comm, operands=(dy, r, g, h, w_in4, w_in4, w_out2))


def _mm_tn(a, b, tk, tn, name, shard_cols=None, interleaved=False, comm=()):
    T, K = a.shape
    N = b.shape[1]

    def body(a_ref, b_ref, o_ref):
        o_ref[...] = _tn(a_ref[...], b_ref[...])

    if shard_cols is None:
        out_shape = jax.ShapeDtypeStruct((K, N), F32)
        out_spec = pl.BlockSpec((tk, tn), lambda ki, nj: (ki, nj))
    else:
        per = shard_cols // tn

        def shard(nj):
            blk = nj // per
            return (blk % 2) * 2 + blk // 2 if interleaved else blk

        out_shape = jax.ShapeDtypeStruct((N // shard_cols, K, shard_cols), F32)
        out_spec = pl.BlockSpec((None, tk, tn), lambda ki, nj: (shard(nj), ki, nj % per))
    (out,), got = _pcall(
        body, name=name, grid=(K // tk, N // tn),
        in_specs=[pl.BlockSpec((T, tk), lambda ki, nj: (0, ki)), pl.BlockSpec((T, tn), lambda ki, nj: (0, nj))],
        out_specs=[out_spec], out_shape=[out_shape], comm=comm, operands=(a, b))
    return out, got


def _mix_fwd_a(xb, w_mix4, conv_w, conv_b, w_co4, tm):
    T = xb.shape[0]

    def body(xb_ref, w_ref, cw_ref, cb_ref, wco_ref,
             pc_ref, z_ref, yin_ref, su_ref, sub_ref, gc_ref, gs_ref, yc_ref, qbuf):
        @pl.when(pl.program_id(0) == 0)
        def _():
            qbuf[pl.ds(0, 8), :] = jnp.zeros((8, CONV), F32)

        xv = xb_ref[...]
        p0 = _nn(xv, w_ref[0])
        p1 = _nn(xv, w_ref[1])
        gc_ref[...] = _nn(xv, w_ref[2])
        gs_ref[...] = _nn(xv, w_ref[3])
        cbv, ccv = p0[:, :CONV], p0[:, CONV:]
        chv, suv = p1[:, :CONV], p1[:, CONV:]
        q = ccv * chv
        qbuf[pl.ds(8, tm), :] = q
        cw = cw_ref[...]
        z = (cw[2:3] * q + cw[1:2] * qbuf[pl.ds(7, tm), :] + cw[0:1] * qbuf[pl.ds(6, tm), :]
             + cb_ref[...])
        qbuf[pl.ds(0, 8), :] = q[tm - 8:tm]
        yin = (cbv * z).astype(BF16)
        pc_ref[:, 0:CONV] = cbv.astype(BF16)
        pc_ref[:, CONV:2 * CONV] = ccv.astype(BF16)
        pc_ref[:, 2 * CONV:3 * CONV] = chv.astype(BF16)
        z_ref[...] = z.astype(BF16)
        yin_ref[...] = yin
        su_ref[...] = suv
        sub_ref[...] = suv.astype(BF16)
        for k in range(4):
            yc_ref[:, 256 * k:256 * (k + 1)] = _nn(yin, wco_ref[k])

    def tok(n):
        return pl.BlockSpec((tm, n), lambda i: (i, 0))

    def full(shape):
        return pl.BlockSpec(shape, lambda i: (0,) * len(shape))

    return pl.pallas_call(
        body, name="mix_fwd_a", grid=(T // tm,),
        in_specs=[tok(D), full((4, D, D)), full((3, CONV)), full((1, CONV)), full((4, CONV, 256))],
        out_specs=[tok(3 * CONV), tok(CONV), tok(CONV), tok(SSM), tok(SSM), tok(D), tok(D), tok(D)],
        out_shape=_hbm_out([jax.ShapeDtypeStruct((T, 3 * CONV), BF16), jax.ShapeDtypeStruct((T, CONV), BF16),
                            jax.ShapeDtypeStruct((T, CONV), BF16), jax.ShapeDtypeStruct((T, SSM), F32),
                            jax.ShapeDtypeStruct((T, SSM), BF16), jax.ShapeDtypeStruct((T, D), F32),
                            jax.ShapeDtypeStruct((T, D), F32), jax.ShapeDtypeStruct((T, D), F32)]),
        scratch_shapes=[pltpu.VMEM((tm + 8, CONV), F32)],
        compiler_params=_cp(56, 1),
    )(*_hbm(xb, w_mix4, conv_w, conv_b, w_co4))


def _scan_inplace(bre, bim, ar, ai, T, rev):
    R = SCAN_R
    if rev:
        ai = -ai
    d = 1
    while d < T:
        if d < 8:
            def step(i, _, d=d, ar=ar, ai=ai):
                c = i if rev else T // R - 1 - i
                t0 = pl.multiple_of(c * R, R)
                if rev:
                    wr = bre[pl.ds(t0 + 8, R + 8), :]
                    wi = bim[pl.ds(t0 + 8, R + 8), :]
                    shr = pltpu.roll(wr, R + 8 - d, 0)[0:R]
                    shi = pltpu.roll(wi, R + 8 - d, 0)[0:R]
                    cr, ci = wr[0:R], wi[0:R]
                else:
                    wr = bre[pl.ds(t0, R + 8), :]
                    wi = bim[pl.ds(t0, R + 8), :]
                    shr = pltpu.roll(wr, d, 0)[8:8 + R]
                    shi = pltpu.roll(wi, d, 0)[8:8 + R]
                    cr, ci = wr[8:8 + R], wi[8:8 + R]
                bre[pl.ds(t0 + 8, R), :] = cr + ar * shr - ai * shi
                bim[pl.ds(t0 + 8, R), :] = ci + ar * shi + ai * shr
                return 0

            lax.fori_loop(0, T // R, step, 0)
        else:
            def upd(lo, n, d=d, ar=ar, ai=ai):
                src = lo + d if rev else lo - d
                if not isinstance(lo, int):
                    lo, src = pl.multiple_of(lo + 8, 8), pl.multiple_of(src + 8, 8)
                else:
                    lo, src = lo + 8, src + 8
                cr = bre[pl.ds(lo, n), :]
                ci = bim[pl.ds(lo, n), :]
                shr = bre[pl.ds(src, n), :]
                shi = bim[pl.ds(src, n), :]
                bre[pl.ds(lo, n), :] = cr + ar * shr - ai * shi
                bim[pl.ds(lo, n), :] = ci + ar * shi + ai * shr

            nfull = (T - d) // R if d >= R else T // R - 1

            def step(i, _, upd=upd, d=d):
                if rev:
                    t0 = i * R
                else:
                    t0 = T - (i + 1) * R
                upd(t0, R)
                return 0

            if nfull > 0:
                lax.fori_loop(0, nfull, step, 0)
            if d < R:
                if rev:
                    upd(T - R, R - d)
                else:
                    upd(d, R - d)
        ar, ai = ar * ar - ai * ai, 2.0 * ar * ai
        d *= 2


def _scan_specs(T):
    W = SCAN_W
    lane = pl.BlockSpec((T, W), lambda j: (0, j))
    col = pl.BlockSpec((T, 128), lambda j: (0, j // 2))
    wb = pl.BlockSpec((None, 128, W), lambda j: (j, 0, 0))
    wc = pl.BlockSpec((None, W, 128), lambda j: (j, 0, 0))
    vec = pl.BlockSpec((1, W), lambda j: (0, j))
    return lane, col, wb, wc, vec


def _s5_scan_fwd(su_b, wb_re, wb_im, a_re, a_im, comm=()):
    T = su_b.shape[0]
    W = SCAN_W

    def body(su_ref, wbr_ref, wbi_ref, ar_ref, ai_ref, sr_ref, si_ref, bre, bim):
        zero = jnp.zeros((8, W), F32)
        for buf in (bre, bim):
            buf[pl.ds(0, 8), :] = zero
            buf[pl.ds(T + 8, 8), :] = zero
        su = su_ref[...]
        bre[pl.ds(8, T), :] = _nn(su, wbr_ref[...])
        bim[pl.ds(8, T), :] = _nn(su, wbi_ref[...])
        _scan_inplace(bre, bim, ar_ref[...], ai_ref[...], T, rev=False)
        sr_ref[...] = bre[pl.ds(8, T), :]
        si_ref[...] = bim[pl.ds(8, T), :]

    lane, col, wb, wc, vec = _scan_specs(T)
    return _pcall(
        body, name="s5_scan_fwd", grid=(LANES // W,),
        in_specs=[col, wb, wb, vec, vec],
        out_specs=[lane, lane],
        out_shape=[jax.ShapeDtypeStruct((T, LANES), F32)] * 2,
        scratch=[pltpu.VMEM((T + 16, W), F32)] * 2, comm=comm,
        operands=(su_b, wb_re, wb_im, a_re, a_im))


def _gelu(s):
    th = jnp.tanh(GELU_C * (s + 0.044715 * s * s * s))
    return 0.5 * s * (1.0 + th), th


def _mix_fwd_b(st_re, st_im, wc_re4, wc_im4, su, dvec, w_glu4, g_conv, g_ssm, y_conv, w_mo, x1, g, b, tm, comm=()):
    T = su.shape[0]

    def body(sr_ref, si_ref, wcr_ref, wci_ref, su_ref, d_ref, wg_ref, gc_ref, gs_ref, yc_ref, wmo_ref,
             x_ref, g_ref, b_ref, s_ref, sgb_ref, ga_ref, gb_ref, mb_ref, r_ref, xo_ref, xob_ref):
        srb = sr_ref[...].astype(BF16)
        sib = si_ref[...].astype(BF16)
        ys = [_nn(srb[:, 512 * J:512 * (J + 1)], wcr_ref[J]) + _nn(sib[:, 512 * J:512 * (J + 1)], wci_ref[J])
              for J in range(4)]
        s = jnp.concatenate(ys, axis=1) + d_ref[...] * su_ref[...]
        sg, _ = _gelu(s)
        sgb = sg.astype(BF16)
        ga = jnp.concatenate([_nn(sgb, wg_ref[0]), _nn(sgb, wg_ref[1])], axis=1)
        gb = jnp.concatenate([_nn(sgb, wg_ref[2]), _nn(sgb, wg_ref[3])], axis=1)
        merged = _sig(gc_ref[...]) * yc_ref[...] + _sig(gs_ref[...]) * (ga * _sig(gb))
        mb = merged.astype(BF16)
        r = ALPHA * x_ref[...] + _nn(mb, wmo_ref[...])
        xhat, _ = _ln_stats(r)
        xo = xhat * g_ref[...] + b_ref[...]
        s_ref[...] = s
        sgb_ref[...] = sgb
        ga_ref[...] = ga
        gb_ref[...] = gb
        mb_ref[...] = mb
        r_ref[...] = r
        xo_ref[...] = xo
        xob_ref[...] = xo.astype(BF16)

    def tok(n):
        return pl.BlockSpec((tm, n), lambda i: (i, 0))

    def full(shape):
        return pl.BlockSpec(shape, lambda i: (0,) * len(shape))

    return _pcall(
        body, name="mix_fwd_b", grid=(T // tm,),
        in_specs=[tok(LANES), tok(LANES), full((4, 512, 128)), full((4, 512, 128)), tok(SSM), full((1, SSM)),
                  full((4, SSM, 512)), tok(D), tok(D), tok(D), full((D, D)), tok(D), full((1, D)), full((1, D))],
        out_specs=[tok(SSM), tok(SSM), tok(D), tok(D), tok(D), tok(D), tok(D), tok(D)],
        out_shape=[jax.ShapeDtypeStruct((T, SSM), F32), jax.ShapeDtypeStruct((T, SSM), BF16),
                   jax.ShapeDtypeStruct((T, D), F32), jax.ShapeDtypeStruct((T, D), F32),
                   jax.ShapeDtypeStruct((T, D), BF16), jax.ShapeDtypeStruct((T, D), F32),
                   jax.ShapeDtypeStruct((T, D), F32), jax.ShapeDtypeStruct((T, D), BF16)],
        vmem_mb=56, comm=comm,
        operands=(st_re, st_im, wc_re4, wc_im4, su, dvec, w_glu4, g_conv, g_ssm, y_conv, w_mo, x1, g, b))


def _ple_loss(x3, x3b, p, w_pi4, w_pg, g, b, target, tm):
    T = x3.shape[0]
    PD = p.shape[1]

    def body(x_ref, xb_ref, p_ref, wpi_ref, wpg_ref, g_ref, b_ref, t_ref,
             loss_ref, dx_ref, pb_ref, dpw_ref, dgt_ref, dg_ref, db_ref):
        i = pl.program_id(0)
        pb = p_ref[...].astype(BF16)
        pw = jnp.concatenate([_nn(pb, wpi_ref[k]) for k in range(4)], axis=1)
        gt = _nn(xb_ref[...], wpg_ref[...])
        sg = _sig(gt)
        r = ALPHA * x_ref[...] + pw * sg
        gv = g_ref[...]
        xhat, rstd = _ln_stats(r)
        err = xhat * gv + b_ref[...] - t_ref[...]
        lpart = jnp.zeros((1, 128), F32) + 0.5 * jnp.sum(jnp.mean(err * err, axis=-1, keepdims=True))
        dy = err * (1.0 / D)
        dyg = dy * gv
        m1 = jnp.mean(dyg, axis=-1, keepdims=True)
        m2 = jnp.mean(dyg * xhat, axis=-1, keepdims=True)
        dr = rstd * (dyg - m1 - xhat * m2)
        pg, pbias = _rowsum(dy * xhat), _rowsum(dy)

        @pl.when(i == 0)
        def _():
            loss_ref[...] = lpart
            dg_ref[...] = pg
            db_ref[...] = pbias

        @pl.when(i > 0)
        def _():
            loss_ref[...] += lpart
            dg_ref[...] += pg
            db_ref[...] += pbias

        dgt = (dr * pw * sg * (1.0 - sg)).astype(BF16)
        pb_ref[...] = pb
        dpw_ref[...] = (dr * sg).astype(BF16)
        dgt_ref[...] = dgt
        dx_ref[...] = ALPHA * dr + _nt(dgt, wpg_ref[...])

    def tok(n):
        return pl.BlockSpec((tm, n), lambda i: (i, 0))

    def full(shape):
        return pl.BlockSpec(shape, lambda i: (0,) * len(shape))

    return pl.pallas_call(
        body, name="ple_loss", grid=(T // tm,),
        in_specs=[tok(D), tok(D), tok(PD), full((4, PD, 256)), full((D, D)), full((1, D)), full((1, D)), tok(D)],
        out_specs=[full((1, 128)), tok(D), tok(PD), tok(D), tok(D), full((1, D)), full((1, D))],
        out_shape=_hbm_out([jax.ShapeDtypeStruct((1, 128), F32), jax.ShapeDtypeStruct((T, D), F32),
                            jax.ShapeDtypeStruct((T, PD), BF16), jax.ShapeDtypeStruct((T, D), BF16),
                            jax.ShapeDtypeStruct((T, D), BF16), jax.ShapeDtypeStruct((1, D), F32),
                            jax.ShapeDtypeStruct((1, D), F32)]),
        compiler_params=_cp(48, 1),
    )(*_hbm(x3, x3b, p, w_pi4, w_pg, g, b, target))


def _mix_bwd_b(dy, r2, g, w_mo, g_conv, g_ssm, y_conv, ga, gb, s, su, dvec, w_glu4, wc_re4, wc_im4, tm, comm=()):
    T = dy.shape[0]

    def body(dy_ref, r_ref, g_ref, wmo_ref, gc_ref, gs_ref, yc_ref, ga_ref, gb_ref, s_ref, su_ref, d_ref,
             wg_ref, wcr_ref, wci_ref,
             dres_ref, dmix_ref, dgl_ref, dsb_ref, dud_ref, gsr_ref, gsi_ref, dyc_ref, dp_ref,
             dg_ref, db_ref, dd_ref):
        i = pl.program_id(0)
        dyv = dy_ref[...]
        dr, xhat = _ln_bwd(dyv, r_ref[...], g_ref[...])
        dmix = dr.astype(BF16)
        dmerged = _nt(dmix, wmo_ref[...])
        sc, ss, sgb = _sig(gc_ref[...]), _sig(gs_ref[...]), _sig(gb_ref[...])
        gav = ga_ref[...]
        yssm = gav * sgb
        dgc = dmerged * yc_ref[...] * sc * (1.0 - sc)
        dgss = dmerged * yssm * ss * (1.0 - ss)
        dyssm = dmerged * ss
        dgl = jnp.concatenate([dyssm * sgb, dyssm * gav * sgb * (1.0 - sgb)], axis=1).astype(BF16)
        dsg = (_nt(dgl[:, 0:512], wg_ref[0]) + _nt(dgl[:, 512:1024], wg_ref[1])
               + _nt(dgl[:, 1024:1536], wg_ref[2]) + _nt(dgl[:, 1536:2048], wg_ref[3]))
        sv = s_ref[...]
        _, th = _gelu(sv)
        dgelu = 0.5 * (1.0 + th) + 0.5 * sv * (1.0 - th * th) * GELU_C * (1.0 + 3.0 * 0.044715 * sv * sv)
        ds = dsg * dgelu
        dsb = ds.astype(BF16)
        pg, pb, pd = _rowsum(dyv * xhat), _rowsum(dyv), _rowsum(ds * su_ref[...])

        @pl.when(i == 0)
        def _():
            dg_ref[...] = pg
            db_ref[...] = pb
            dd_ref[...] = pd

        @pl.when(i > 0)
        def _():
            dg_ref[...] += pg
            db_ref[...] += pb
            dd_ref[...] += pd

        dres_ref[...] = ALPHA * dr
        dmix_ref[...] = dmix
        dgl_ref[...] = dgl
        dsb_ref[...] = dsb
        dud_ref[...] = ds * d_ref[...]
        for J in range(4):
            gsr_ref[:, 512 * J:512 * (J + 1)] = _nt(dsb[:, 128 * J:128 * (J + 1)], wcr_ref[J])
            gsi_ref[:, 512 * J:512 * (J + 1)] = _nt(dsb[:, 128 * J:128 * (J + 1)], wci_ref[J])
        dyc_ref[...] = (dmerged * sc).astype(BF16)
        dp_ref[:, 0:D] = dgc.astype(BF16)
        dp_ref[:, D:2 * D] = dgss.astype(BF16)

    def tok(n):
        return pl.BlockSpec((tm, n), lambda i: (i, 0))

    def full(shape):
        return pl.BlockSpec(shape, lambda i: (0,) * len(shape))

    return _pcall(
        body, name="mix_bwd_b", grid=(T // tm,),
        in_specs=[tok(D), tok(D), full((1, D)), full((D, D)), tok(D), tok(D), tok(D), tok(D), tok(D),
                  tok(SSM), tok(SSM), full((1, SSM)), full((4, SSM, 512)), full((4, 512, 128)), full((4, 512, 128))],
        out_specs=[tok(D), tok(D), tok(2 * D), tok(SSM), tok(SSM), tok(LANES), tok(LANES), tok(D),
                   pl.BlockSpec((tm, 2 * D), lambda i: (i, 1)), full((1, D)), full((1, D)), full((1, SSM))],
        out_shape=[jax.ShapeDtypeStruct((T, D), F32), jax.ShapeDtypeStruct((T, D), BF16),
                   jax.ShapeDtypeStruct((T, 2 * D), BF16), jax.ShapeDtypeStruct((T, SSM), BF16),
                   jax.ShapeDtypeStruct((T, SSM), F32), jax.ShapeDtypeStruct((T, LANES), F32),
                   jax.ShapeDtypeStruct((T, LANES), F32), jax.ShapeDtypeStruct((T, D), BF16),
                   jax.ShapeDtypeStruct((T, 4 * D), BF16), jax.ShapeDtypeStruct((1, D), F32),
                   jax.ShapeDtypeStruct((1, D), F32), jax.ShapeDtypeStruct((1, SSM), F32)],
        vmem_mb=56, comm=comm,
        operands=(dy, r2, g, w_mo, g_conv, g_ssm, y_conv, ga, gb, s, su, dvec, w_glu4, wc_re4, wc_im4))


def _s5_scan_bwd(gs_re, gs_im, st_re, st_im, su_b, ds_b, wb_re, wb_im, a_re, a_im, comm=()):
    T = su_b.shape[0]
    W = SCAN_W
    R = SCAN_R

    def body(gr_ref, gi_ref, sr_ref, si_ref, su_ref, ds_ref, wbr_ref, wbi_ref, ar_ref, ai_ref,
             dsu_ref, dwbr_ref, dwbi_ref, dwcr_ref, dwci_ref, dar_ref, dai_ref, gre, gim):
        j = pl.program_id(0)
        zero = jnp.zeros((8, W), F32)
        for buf in (gre, gim):
            buf[pl.ds(0, 8), :] = zero
            buf[pl.ds(T + 8, 8), :] = zero
        gre[pl.ds(8, T), :] = gr_ref[...]
        gim[pl.ds(8, T), :] = gi_ref[...]
        _scan_inplace(gre, gim, ar_ref[...], ai_ref[...], T, rev=True)
        grb = gre[pl.ds(8, T), :].astype(BF16)
        gib = gim[pl.ds(8, T), :].astype(BF16)
        part = _nt(grb, wbr_ref[...]) + _nt(gib, wbi_ref[...])

        @pl.when(j % 2 == 0)
        def _():
            dsu_ref[...] = part

        @pl.when(j % 2 == 1)
        def _():
            dsu_ref[...] += part

        su = su_ref[...]
        dwbr_ref[...] = _tn(su, grb)
        dwbi_ref[...] = _tn(su, gib)
        dsv = ds_ref[...]
        dwcr_ref[...] = _tn(sr_ref[...].astype(BF16), dsv)
        dwci_ref[...] = _tn(si_ref[...].astype(BF16), dsv)
        dar = jnp.zeros((1, W), F32)
        dai = jnp.zeros((1, W), F32)
        for c in range(T // R):
            xr = sr_ref[pl.ds(c * R, R), :]
            xi = si_ref[pl.ds(c * R, R), :]
            g1r = gre[pl.ds(c * R + 9, R), :]
            g1i = gim[pl.ds(c * R + 9, R), :]
            dar = dar + _rowsum(g1r * xr + g1i * xi)
            dai = dai + _rowsum(g1i * xr - g1r * xi)
        dar_ref[...] = dar
        dai_ref[...] = dai

    lane, col, wb, wc, vec = _scan_specs(T)
    return _pcall(
        body, name="s5_scan_bwd", grid=(LANES // W,),
        in_specs=[lane, lane, lane, lane, col, col, wb, wb, vec, vec],
        out_specs=[col, wb, wb, wc, wc, vec, vec],
        out_shape=[jax.ShapeDtypeStruct((T, SSM), F32),
                   jax.ShapeDtypeStruct((LANES // W, 128, W), F32), jax.ShapeDtypeStruct((LANES // W, 128, W), F32),
                   jax.ShapeDtypeStruct((LANES // W, W, 128), F32), jax.ShapeDtypeStruct((LANES // W, W, 128), F32),
                   jax.ShapeDtypeStruct((1, LANES), F32), jax.ShapeDtypeStruct((1, LANES), F32)],
        scratch=[pltpu.VMEM((T + 16, W), F32)] * 2, vmem_mb=56, comm=comm,
        operands=(gs_re, gs_im, st_re, st_im, su_b, ds_b, wb_re, wb_im, a_re, a_im))


def _mix_bwd_a(dyc_b, w_co4, pc, z_b, conv_w, dsu_ssm, du_dir, dproj, dres, w_mix4, tm, comm=()):
    T = dres.shape[0]
    nt = T // tm

    def body(dyc_ref, wco_ref, pc_ref, halo_ref, z_ref, cw_ref, dsu_ref, dud_ref, dpin_ref, dres_ref, w_ref,
             dp_ref, dx_ref, dcw_ref, dcb_ref, dzbuf, qbuf):
        i = pl.program_id(0)
        ii = nt - 1 - i

        @pl.when(i == 0)
        def _():
            dzbuf[pl.ds(tm, 8), :] = jnp.zeros((8, CONV), F32)

        dyc = dyc_ref[...]
        dyin = (_nt(dyc[:, 0:256], wco_ref[0]) + _nt(dyc[:, 256:512], wco_ref[1])
                + _nt(dyc[:, 512:768], wco_ref[2]) + _nt(dyc[:, 768:1024], wco_ref[3]))
        cbv = pc_ref[:, 0:CONV].astype(F32)
        ccv = pc_ref[:, CONV:2 * CONV].astype(F32)
        chv = pc_ref[:, 2 * CONV:3 * CONV].astype(F32)
        dcbv = dyin * z_ref[...].astype(F32)
        dz = dyin * cbv
        dzbuf[pl.ds(0, tm), :] = dz
        cw = cw_ref[...]
        dq = cw[2:3] * dz + cw[1:2] * dzbuf[pl.ds(1, tm), :] + cw[0:1] * dzbuf[pl.ds(2, tm), :]
        dzbuf[pl.ds(tm, 8), :] = dz[0:8]
        q = ccv * chv
        hq = halo_ref[:, CONV:2 * CONV].astype(F32) * halo_ref[:, 2 * CONV:3 * CONV].astype(F32)
        qbuf[pl.ds(0, 8), :] = jnp.where(ii > 0, hq, jnp.zeros_like(hq))
        qbuf[pl.ds(8, tm), :] = q
        pw = jnp.concatenate([_rowsum(dz * qbuf[pl.ds(6, tm), :]), _rowsum(dz * qbuf[pl.ds(7, tm), :]),
                              _rowsum(dz * q), jnp.zeros((5, CONV), F32)], axis=0)
        pbias = _rowsum(dz)

        @pl.when(i == 0)
        def _():
            dcw_ref[...] = pw
            dcb_ref[...] = pbias

        @pl.when(i > 0)
        def _():
            dcw_ref[...] += pw
            dcb_ref[...] += pbias

        dp0 = jnp.concatenate([dcbv, dq * chv], axis=1).astype(BF16)
        dp1 = jnp.concatenate([dq * ccv, dsu_ref[...] + dud_ref[...]], axis=1).astype(BF16)
        dp_ref[:, 0:D] = dp0
        dp_ref[:, D:2 * D] = dp1
        dx_ref[...] = (dres_ref[...] + _nt(dp0, w_ref[0]) + _nt(dp1, w_ref[1])
                       + _nt(dpin_ref[:, 0:D], w_ref[2]) + _nt(dpin_ref[:, D:2 * D], w_ref[3]))

    def tok(n):
        return pl.BlockSpec((tm, n), lambda i: (nt - 1 - i, 0))

    def full(shape):
        return pl.BlockSpec(shape, lambda i: (0,) * len(shape))

    halo = pl.BlockSpec((8, 3 * CONV), lambda i: (jnp.maximum((nt - 1 - i) * (tm // 8) - 1, 0), 0))
    return _pcall(
        body, name="mix_bwd_a", grid=(nt,),
        in_specs=[tok(D), full((4, CONV, 256)), tok(3 * CONV), halo, tok(CONV), full((3, CONV)),
                  tok(SSM), tok(SSM), pl.BlockSpec((tm, 2 * D), lambda i: (nt - 1 - i, 1)), tok(D),
                  full((4, D, D))],
        out_specs=[pl.BlockSpec((tm, 2 * D), lambda i: (nt - 1 - i, 0)), tok(D), full((8, CONV)), full((1, CONV))],
        out_shape=[jax.ShapeDtypeStruct((T, 4 * D), BF16), jax.ShapeDtypeStruct((T, D), F32),
                   jax.ShapeDtypeStruct((8, CONV), F32), jax.ShapeDtypeStruct((1, CONV), F32)],
        scratch=[pltpu.VMEM((tm + 8, CONV), F32), pltpu.VMEM((tm + 8, CONV), F32)],
        aliases={8: 0}, vmem_mb=56, comm=comm,
        operands=(dyc_b, w_co4, pc, pc, z_b, conv_w, dsu_ssm, du_dir, dproj, dres, w_mix4))


def _zoh(lam_re, lam_im, log_step, b_re, b_im):
    dt = jnp.exp(log_step)[:, None]
    mag = jnp.exp(lam_re * dt)
    abr, abi = mag * jnp.cos(lam_im * dt), mag * jnp.sin(lam_im * dt)
    nr, ni = abr - 1.0, abi
    den = lam_re * lam_re + lam_im * lam_im
    cr = (nr * lam_re + ni * lam_im) / den
    ci = (ni * lam_re - nr * lam_im) / den
    bbr = cr[..., None] * b_re - ci[..., None] * b_im
    bbi = cr[..., None] * b_im + ci[..., None] * b_re
    return abr, abi, bbr, bbi


_WB_MASK = (np.arange(8)[None, :, None] == 4 * np.arange(2)[:, None, None] + np.arange(4)[None, None, :]
            ).astype(np.float32)
_EYE8 = np.eye(8, dtype=np.float32)


def _wb_blocks(bb):
    bt = bb.transpose(0, 2, 1).reshape(4, 1, 8, 16, 1, STATE)
    full = bt * _WB_MASK[None, :, :, None, :, None]
    return full.reshape(8, 128, SCAN_W).astype(BF16)


def _wc_blocks(cc):
    ct = cc.transpose(0, 2, 1).reshape(4, 8, STATE, 1, 16)
    full = ct * _EYE8[None, :, None, :, None]
    return full.reshape(4, 512, 128).astype(BF16)


def _wb_diag(dwb8):
    d6 = dwb8.reshape(4, 2, 8, 16, 4, STATE) * _WB_MASK[None, :, :, None, :, None]
    return d6.sum(axis=(1, 4)).reshape(GROUPS, 16, STATE).transpose(0, 2, 1)


def _wc_diag(dwc8):
    mask = _WB_MASK.transpose(0, 2, 1)
    d6 = dwc8.reshape(4, 2, 4, STATE, 8, 16) * mask[None, :, :, None, :, None]
    return d6.sum(axis=4).reshape(GROUPS, STATE, 16).transpose(0, 2, 1)


def _where():
    x, y, c = lax.axis_index("x"), lax.axis_index("y"), lax.axis_index("c")
    return x, y, c, 2 * x + y


def _chip_dev(k, c):
    return (k // 2, k % 2, c)


def _slot_cast(meidx, w, dtype, name):
    R, C = w.shape
    tr = _row_tile(R)

    def body(m_ref, w_ref, o_ref):
        o_ref[...] = w_ref[...].astype(dtype)

    gs = pltpu.PrefetchScalarGridSpec(
        num_scalar_prefetch=1, grid=(R // tr,),
        in_specs=[pl.BlockSpec((tr, C), lambda i, m: (i, 0))],
        out_specs=pl.BlockSpec((None, tr, C), lambda i, m: (m[0], i, 0)))
    return pl.pallas_call(
        body, name=name, grid_spec=gs, out_shape=_hbm_out(jax.ShapeDtypeStruct((4, R, C), dtype)),
        compiler_params=_cp(32, 1),
    )(meidx, *_hbm(w))


def _gather_payload(bufs):
    n = len(bufs)

    def half(ref, w, k, cc):
        h = bufs[w].shape[1] // 2
        return ref.at[k, pl.ds(cc * h, h)]

    def ici(ins, outs, sems, w, s):
        x, y, c, me = _where()
        k = (me + 1 + s) % 4
        return pltpu.make_async_remote_copy(
            src_ref=half(ins[w], w, me, c), dst_ref=half(outs[w], w, me, c), send_sem=sems[0].at[3 * w + s],
            recv_sem=sems[1].at[3 * w + s], device_id=_chip_dev(k, c), device_id_type=MESH)

    def landed(outs, sems, w, s):
        x, y, c, me = _where()
        j = (me + 3 - s) % 4
        return pltpu.make_async_remote_copy(
            src_ref=half(outs[w], w, j, c), dst_ref=half(outs[w], w, j, c), send_sem=sems[0].at[3 * w + s],
            recv_sem=sems[1].at[3 * w + s], device_id=(x, y, 1 - c), device_id_type=MESH)

    def passed(outs, sems, w, s, cc):
        x, y, c, me = _where()
        j = (me + 3 - s) % 4
        return pltpu.make_async_remote_copy(
            src_ref=half(outs[w], w, j, cc), dst_ref=half(outs[w], w, j, cc), send_sem=sems[2].at[3 * w + s],
            recv_sem=sems[3].at[3 * w + s], device_id=(x, y, 1 - c), device_id_type=MESH)

    pairs = [(w, s) for w in range(n) for s in range(3)]

    def start(ins, outs, sems):
        for w, s in pairs:
            ici(ins, outs, sems, w, s).start()

    def finish(ins, outs, sems):
        _, _, c, _ = _where()
        for w, s in pairs:
            landed(outs, sems, w, s).wait_recv()
            passed(outs, sems, w, s, c).start()
        for w, s in pairs:
            passed(outs, sems, w, s, 1 - c).wait_recv()
        for w, s in pairs:
            ici(ins, outs, sems, w, s).wait_send()
            passed(outs, sems, w, s, c).wait_send()

    return _Payload(bufs, [jax.ShapeDtypeStruct(b.shape, b.dtype) for b in bufs], {w: w for w in range(n)},
                    [pltpu.SemaphoreType.DMA((3 * n,))] * 4, start, finish)


def _sym_payload(operands, outs, copies, n_copies):
    def start(ins, outs_, sems):
        for cp in copies(ins, outs_, sems[0], sems[1]):
            cp.start()

    def finish(ins, outs_, sems):
        for cp in copies(ins, outs_, sems[0], sems[1]):
            cp.wait()

    p = _Payload(operands, outs, {}, [pltpu.SemaphoreType.DMA((n_copies,))] * 2, start, finish)
    p.copies, p.n_copies = copies, n_copies
    return p


def _swap_payload(g4s):
    def copies(ins, outs, ss, rs):
        x, y, c, me = _where()
        cps = []
        for w, g in enumerate(g4s):
            h = g.shape[1] // 2
            cps.append(pltpu.make_async_remote_copy(
                src_ref=ins[w].at[:, pl.ds((1 - c) * h, h)], dst_ref=outs[w], send_sem=ss.at[w],
                recv_sem=rs.at[w], device_id=(x, y, 1 - c), device_id_type=MESH))
        return cps

    outs = [jax.ShapeDtypeStruct((4, g.shape[1] // 2, g.shape[2]), g.dtype) for g in g4s]
    return _sym_payload(g4s, outs, copies, len(g4s))


def _exchange_payload(pbs):
    def copies(ins, outs, ss, rs):
        x, y, c, me = _where()
        cps = []
        for w in range(len(pbs)):
            for s in range(3):
                k = (me + 1 + s) % 4
                cps.append(pltpu.make_async_remote_copy(
                    src_ref=ins[w].at[k], dst_ref=outs[w].at[2 - s], send_sem=ss.at[3 * w + s],
                    recv_sem=rs.at[3 * w + s], device_id=_chip_dev(k, c), device_id_type=MESH))
        return cps

    outs = [jax.ShapeDtypeStruct((3,) + p.shape[1:], p.dtype) for p in pbs]
    return _sym_payload(pbs, outs, copies, 3 * len(pbs))


HBM_REF = pl.BlockSpec(memory_space=pltpu.HBM)
SEM_REF = pl.BlockSpec(memory_space=pltpu.SEMAPHORE)
DATAFLOW = pltpu.SideEffectType.DATAFLOW_SIDE_EFFECTING


class _SemList:
    def __init__(self, refs):
        self.refs = refs

    @property
    def at(self):
        return self.refs


def _split_start(p, name):
    n_in, n_out, nc = len(p.operands), len(p.outs), p.n_copies
    lands = getattr(p, "lands", None) or [lax.empty(s.shape, s.dtype) for s in p.outs]

    def body(*refs):
        ins, lnd = refs[:n_in], refs[n_in:n_in + n_out]
        sems = refs[n_in + n_out:n_in + n_out + 2 * nc]
        for cp in p.copies(ins, lnd, _SemList(sems[:nc]), _SemList(sems[nc:])):
            cp.start()
        refs[-1][...] = jnp.zeros((8, 128), F32)

    res = pl.pallas_call(
        body, name=name,
        in_specs=[HBM_REF] * (n_in + n_out),
        out_specs=[SEM_REF] * (2 * nc) + [HBM_REF] * (n_in + n_out) + [VMEM_FULL],
        out_shape=([pltpu.SemaphoreType.DMA(())] * (2 * nc) + _hbm_out(p.operands) + _hbm_out(lands)
                   + [jax.ShapeDtypeStruct((8, 128), F32)]),
        input_output_aliases={i: 2 * nc + i for i in range(n_in + n_out)},
        compiler_params=pltpu.CompilerParams(has_side_effects=DATAFLOW),
    )(*_hbm(*p.operands, *lands))
    k = 2 * nc
    return list(res[:k]), list(res[k:k + n_in]), list(res[k + n_in:k + n_in + n_out]), res[-1]


def _split_wait(p, handle, after, name):
    sems, srcs, lands, _ = handle
    n_in, n_out, nc = len(srcs), len(lands), p.n_copies

    def body(*refs):
        ins, lnd = refs[:n_in], refs[n_in:n_in + n_out]
        sm = refs[n_in + n_out:n_in + n_out + 2 * nc]
        for cp in p.copies(ins, lnd, _SemList(sm[:nc]), _SemList(sm[nc:])):
            cp.wait_send()
            cp.wait_recv()

    res = pl.pallas_call(
        body, name=name,
        in_specs=[HBM_REF] * (n_in + n_out) + [SEM_REF] * (2 * nc) + [ANY] * len(after),
        out_specs=[HBM_REF] * (n_in + n_out), out_shape=_hbm_out(srcs) + _hbm_out(lands),
        input_output_aliases={i: i for i in range(n_in + n_out)},
        compiler_params=pltpu.CompilerParams(has_side_effects=DATAFLOW),
    )(*srcs, *lands, *sems, *after)
    return list(res[:n_in]), list(res[n_in:])


def _join_payload(halves):
    def copies(ins, outs, ss, rs):
        x, y, c, me = _where()
        return [pltpu.make_async_remote_copy(
            src_ref=ins[w], dst_ref=outs[w], send_sem=ss.at[w], recv_sem=rs.at[w],
            device_id=(x, y, 1 - c), device_id_type=MESH) for w in range(len(halves))]

    outs = [jax.ShapeDtypeStruct(a.shape, a.dtype) for a in halves]
    return _sym_payload(halves, outs, copies, len(halves))


def _allgather_payload(v):
    def copies(ins, outs, ss, rs):
        x, y, c, me = _where()
        lin = 4 * x + 2 * y + c
        cps = []
        for o in range(1, 8):
            t = (lin + o) % 8
            cps.append(pltpu.make_async_remote_copy(
                src_ref=ins[0], dst_ref=outs[0].at[lin], send_sem=ss.at[o - 1], recv_sem=rs.at[o - 1],
                device_id=(t // 4, (t // 2) % 2, t % 2), device_id_type=MESH))
        return cps

    p = _sym_payload([v], [jax.ShapeDtypeStruct((8,) + v.shape, v.dtype)], copies, 7)
    x, y, c, _ = _where()
    p.lands = [lax.dynamic_update_slice(jnp.zeros((8,) + v.shape, v.dtype), v[None], (4 * x + 2 * y + c, 0, 0))]
    return p


def _sum8(buf, token):
    _, P, C = buf.shape

    def body(b_ref, t_ref, o_ref):
        acc = b_ref[0]
        for d in range(1, 8):
            acc = acc + b_ref[d]
        o_ref[...] = acc

    return pl.pallas_call(
        body, name="sum8", in_specs=[VMEM_FULL, VMEM_FULL], out_specs=VMEM_FULL,
        out_shape=jax.ShapeDtypeStruct((P, C), F32),
        compiler_params=pltpu.CompilerParams(vmem_limit_bytes=32 << 20),
    )(buf, token)


def _row_tile(h):
    for t in (256, 176, 128, 64, 32, 16, 8):
        if h % t == 0:
            return t
    raise ValueError(h)


def _pair_sum(cmidx, g4, got, name):
    _, R, C = g4.shape
    h = R // 2
    th = _row_tile(h)

    def body(cm_ref, a_ref, b_ref, o_ref, ob_ref):
        sm = a_ref[...] + b_ref[...]
        ob_ref[...] = sm.astype(BF16)

        @pl.when(pl.program_id(1) == cm_ref[1])
        def _():
            o_ref[...] = sm

    blk = pl.BlockSpec((None, th, C), lambda i, k, cm: (k, i, 0))
    gs = pltpu.PrefetchScalarGridSpec(
        num_scalar_prefetch=1, grid=(h // th, 4),
        in_specs=[pl.BlockSpec((None, None, th, C), lambda i, k, cm: (k, cm[0], i, 0)), blk],
        out_specs=[pl.BlockSpec((th, C), lambda i, k, cm: (i, 0)), blk])
    return pl.pallas_call(
        body, name=name, grid_spec=gs,
        out_shape=_hbm_out([jax.ShapeDtypeStruct((h, C), F32), jax.ShapeDtypeStruct((4, h, C), BF16)]),
        compiler_params=_cp(32, 2),
    )(cmidx, *_hbm(g4.reshape(4, 2, h, C), got))


def _chip_sum(own, got, name):
    h, C = own.shape
    th = _row_tile(h)

    def body(a_ref, b_ref, o_ref):
        o_ref[...] = ((a_ref[...] + b_ref[0].astype(F32)) + b_ref[1].astype(F32)) + b_ref[2].astype(F32)

    return pl.pallas_call(
        body, name=name, grid=(h // th,),
        in_specs=[pl.BlockSpec((th, C), lambda i: (i, 0)), pl.BlockSpec((3, th, C), lambda i: (0, i, 0))],
        out_specs=pl.BlockSpec((th, C), lambda i: (i, 0)),
        out_shape=_hbm_out(jax.ShapeDtypeStruct((h, C), F32)),
        compiler_params=_cp(32, 1),
    )(*_hbm(own, got))


def _adamw_math(w, g, m, v):
    m2 = B1 * m + (1.0 - B1) * g
    v2 = B2 * v + (1.0 - B2) * (g * g)
    m_hat = m2 / (1.0 - B1 ** STEP)
    v_hat = v2 / (1.0 - B2 ** STEP)
    delta = -LR * (m_hat / (jnp.sqrt(v_hat) + EPS) + WD * w)
    return delta, m2, v2


def _adamw_pair(cidx, w, mine, theirs, m, v, token, name):
    R, C = w.shape
    h = R // 2
    tr = _row_tile(h)
    nh = h // tr

    def body(c_ref, w_ref, a_ref, b_ref, m_ref, v_ref, t_ref, g_ref, d_ref, mo_ref, vo_ref):
        own = (pl.program_id(0) // nh) == c_ref[0]
        g = jnp.where(own, a_ref[...], b_ref[...])
        d, m2, v2 = _adamw_math(w_ref[...], g, m_ref[...], v_ref[...])
        g_ref[...] = g
        d_ref[...] = d
        mo_ref[...] = m2
        vo_ref[...] = v2

    blk = pl.BlockSpec((tr, C), lambda i, c: (i, 0))
    mine_blk = pl.BlockSpec((tr, C), lambda i, c: (jnp.clip(i - c[0] * nh, 0, nh - 1), 0))
    theirs_blk = pl.BlockSpec((tr, C), lambda i, c: (jnp.clip(i - (1 - c[0]) * nh, 0, nh - 1), 0))
    gs = pltpu.PrefetchScalarGridSpec(
        num_scalar_prefetch=1, grid=(R // tr,),
        in_specs=[blk, mine_blk, theirs_blk, blk, blk, pl.BlockSpec((8, 128), lambda i, c: (0, 0))],
        out_specs=[blk] * 4)
    return pl.pallas_call(
        body, name=name, grid_spec=gs, out_shape=_hbm_out([jax.ShapeDtypeStruct((R, C), F32)] * 4),
        compiler_params=_cp(32, 1),
    )(cidx, *_hbm(w, mine, theirs, m, v), token)


def _adamw(w, g, m, v, name):
    R, C = w.shape
    tr = _row_tile(R)

    def body(w_ref, g_ref, m_ref, v_ref, d_ref, mo_ref, vo_ref):
        d, m2, v2 = _adamw_math(w_ref[...], g_ref[...], m_ref[...], v_ref[...])
        d_ref[...] = d
        mo_ref[...] = m2
        vo_ref[...] = v2

    blk = pl.BlockSpec((tr, C), lambda i: (i, 0))
    return pl.pallas_call(
        body, name=name, grid=(R // tr,), in_specs=[blk] * 4, out_specs=[blk] * 3,
        out_shape=_hbm_out([jax.ShapeDtypeStruct((R, C), F32)] * 3),
        compiler_params=_cp(32, 1),
    )(*_hbm(w, g, m, v))


def _pack(arrs):
    flat = jnp.concatenate([a.reshape(-1).astype(F32) for a in arrs])
    rows = -(-flat.shape[0] // 1024)
    rows = -(-rows // 8) * 8
    return jnp.pad(flat, (0, rows * 1024 - flat.shape[0])).reshape(rows, 1024)


def _unpack(packed, shapes):
    flat = packed.reshape(-1)
    out, off = [], 0
    for s in shapes:
        n = math.prod(s)
        out.append(flat[off:off + n].reshape(s))
        off += n
    return out


BIG = ["ffn1_w_in", "ffn1_w_out", "mix_w_in", "conv_w_out", "ssm_w_glu", "mix_w_out",
       "ffn2_w_in", "ffn2_w_out", "ple_w_in", "ple_w_gate"]
SMALL = ["ln1_g", "ln1_b", "conv_w", "conv_b", "ssm_lam_re", "ssm_lam_im", "ssm_log_step", "ssm_b_re", "ssm_b_im",
         "ssm_c_re", "ssm_c_im", "ssm_d", "ln2_g", "ln2_b", "ln3_g", "ln3_b", "ln4_g", "ln4_b"]
WEIGHTS = ["ffn1_w_in", "ffn1_w_out", "ln1_g", "ln1_b", "mix_w_in", "conv_w", "conv_b", "conv_w_out",
           "ssm_lam_re", "ssm_lam_im", "ssm_log_step", "ssm_b_re", "ssm_b_im", "ssm_c_re", "ssm_c_im", "ssm_d",
           "ssm_w_glu", "mix_w_out", "ln2_g", "ln2_b", "ffn2_w_in", "ffn2_w_out", "ln3_g", "ln3_b",
           "ple_w_in", "ple_w_gate", "ln4_g", "ln4_b"]


class _NoComm:
    def __init__(self, W):
        self.W, self.G, self.raw = dict(W), {}, None

    def carry(self, name):
        return ()

    def landed(self, name, got):
        pass

    def grad(self, name, g4):
        self.G[name] = g4

    def small(self, raw):
        self.raw = raw


def _local_step(x, p, target, sp, sched, tm_ffn, tm_mix):
    W = sched.W
    abr, abi, bbr, bbi = _zoh(sp["ssm_lam_re"], sp["ssm_lam_im"], sp["ssm_log_step"], sp["ssm_b_re"], sp["ssm_b_im"])
    wb_re, wb_im = _wb_blocks(bbr), _wb_blocks(bbi)
    wc_re4, wc_im4 = _wc_blocks(sp["ssm_c_re"]), _wc_blocks(-sp["ssm_c_im"])
    a_re, a_im = abr.reshape(1, LANES), abi.reshape(1, LANES)
    dvec = sp["ssm_d"].reshape(1, SSM)

    def run(fn, name, *args, **kw):
        outs, got = fn(*args, comm=sched.carry(name), **kw)
        sched.landed(name, got)
        return outs

    def dw(name, wname, a, b, tk, tn, shape4, shard_cols=None, interleaved=False):
        out, got = _mm_tn(a, b, tk, tn, name, shard_cols=shard_cols, interleaved=interleaved,
                          comm=sched.carry(name))
        sched.landed(name, got)
        sched.grad(wname, out.reshape(shape4))

    xb = x.astype(BF16)
    h1, r1, x1, x1b = run(_ffn_fwd, "ffn1_fwd", x, xb, W["ffn1_w_in"], W["ffn1_w_out"].reshape(2, FFH, D),
                          sp["ln1_g"], sp["ln1_b"], tm_ffn, "ffn1_fwd")
    conv_w = W["conv_w"][:, 0:3, :].transpose(1, 0, 2).reshape(3, CONV)
    pc, z_b, yin_b, su, su_b, g_conv, g_ssm, y_conv = _mix_fwd_a(
        x1b, W["mix_w_in"], conv_w, sp["conv_b"], W["conv_w_out"], tm_mix)
    st_re, st_im = run(_s5_scan_fwd, "s5_scan_fwd", su_b, wb_re, wb_im, a_re, a_im)
    w_mo = W["mix_w_out"].reshape(D, D)
    s, sg_b, ga, gb, merged_b, r2, x2, x2b = run(
        _mix_fwd_b, "mix_fwd_b", st_re, st_im, wc_re4, wc_im4, su, dvec, W["ssm_w_glu"], g_conv, g_ssm, y_conv,
        w_mo, x1, sp["ln2_g"], sp["ln2_b"], tm_mix)
    w2o2 = W["ffn2_w_out"].reshape(2, FFH, D)
    h2, r3, x3, x3b = run(_ffn_fwd, "ffn2_fwd", x2, x2b, W["ffn2_w_in"], w2o2, sp["ln3_g"], sp["ln3_b"], tm_ffn,
                          "ffn2_fwd")
    loss_part, dx3, p_b, dpw_b, dgt_b, dg4, db4 = _ple_loss(
        x3, x3b, p, W["ple_w_in"], W["ple_w_gate"].reshape(D, D), sp["ln4_g"], sp["ln4_b"], target, tm_mix)

    dw("dw_ple_gate", "ple_w_gate", x3b, dgt_b, 512, 1024, (4, 256, D))
    dw("dw_ple_in", "ple_w_in", p_b, dpw_b, 256, 256, (4, 256, 256), shard_cols=256)
    dx2, dh2, a2_b, df2_b, dg3, db3 = run(_ffn_bwd, "ffn2_bwd", dx3, r3, sp["ln3_g"], h2, W["ffn2_w_in"], w2o2,
                                          tm_mix, "ffn2_bwd")
    dw("dw_ffn2_in", "ffn2_w_in", x2b, dh2, 512, FFH, (4, D, FFH), shard_cols=FFH, interleaved=True)
    dw("dw_ffn2_out", "ffn2_w_out", a2_b, df2_b, FFH, 1024, (4, FF // 4, D))
    (dres, dmix_b, dgl_b, ds_b, du_dir, gs_re, gs_im, dyc_b, dproj, dg2, db2, dd) = run(
        _mix_bwd_b, "mix_bwd_b", dx2, r2, sp["ln2_g"], w_mo, g_conv, g_ssm, y_conv, ga, gb, s, su, dvec,
        W["ssm_w_glu"], wc_re4, wc_im4, tm_mix)
    dw("dw_mix_out", "mix_w_out", merged_b, dmix_b, 512, 1024, (4, 256, D))
    dw("dw_glu", "ssm_w_glu", sg_b, dgl_b, 512, 512, (4, SSM, 512), shard_cols=512)
    dsu_ssm, dwb_re, dwb_im, dwc_re, dwc_im, da_re, da_im = run(
        _s5_scan_bwd, "s5_scan_bwd", gs_re, gs_im, st_re, st_im, su_b, ds_b, wb_re, wb_im, a_re, a_im)
    dw("dw_conv_out", "conv_w_out", yin_b, dyc_b, 512, 256, (4, CONV, 256), shard_cols=256)
    dproj, dx1, dcw8, dcb = run(_mix_bwd_a, "mix_bwd_a", dyc_b, W["conv_w_out"], pc, z_b, conv_w, dsu_ssm,
                                du_dir, dproj, dres, W["mix_w_in"], tm_mix)
    dw("dw_mix_in", "mix_w_in", x1b, dproj, 512, 1024, (4, D, D), shard_cols=1024)
    dx0, dh1, a1_b, df1_b, dg1, db1 = run(_ffn_bwd, "ffn1_bwd", dx1, r1, sp["ln1_g"], h1, W["ffn1_w_in"],
                                          W["ffn1_w_out"].reshape(2, FFH, D), tm_mix, "ffn1_bwd")
    sched.small(dict(
        ln1_g=dg1, ln1_b=db1, ln2_g=dg2, ln2_b=db2, ln3_g=dg3, ln3_b=db3, ln4_g=dg4, ln4_b=db4,
        conv_w=dcw8[0:3], conv_b=dcb,
        a_re=da_re.reshape(GROUPS, STATE), a_im=da_im.reshape(GROUPS, STATE),
        bb_re=_wb_diag(dwb_re), bb_im=_wb_diag(dwb_im),
        ssm_c_re=_wc_diag(dwc_re), ssm_c_im=-_wc_diag(dwc_im), ssm_d=dd.reshape(GROUPS, 16),
        loss=loss_part[0:1, 0]))
    dw("dw_ffn1_in", "ffn1_w_in", xb, dh1, 512, FFH, (4, D, FFH), shard_cols=FFH, interleaved=True)
    dw("dw_ffn1_out", "ffn1_w_out", a1_b, df1_b, FFH, 1024, (4, FF // 4, D))
    return loss_part[0, 0], dx0


RAW_ORDER = ["ln1_g", "ln1_b", "ln2_g", "ln2_b", "ln3_g", "ln3_b", "ln4_g", "ln4_b", "conv_w", "conv_b",
             "a_re", "a_im", "bb_re", "bb_im", "ssm_c_re", "ssm_c_im", "ssm_d", "loss"]

GATHER_FIRST = ["ffn1_w_in", "ffn1_w_out"]
GATHER_AT = {"ffn1_fwd": ["mix_w_in", "conv_w_out", "conv_w", "ssm_w_glu", "mix_w_out"],
             "s5_scan_fwd": ["ffn2_w_in"], "mix_fwd_b": ["ffn2_w_out"], "ffn2_fwd": ["ple_w_in", "ple_w_gate"]}
REDUCE_GROUP = {"ple": ["ple_w_gate", "ple_w_in"], "ffn2": ["ffn2_w_in", "ffn2_w_out"],
                "mix": ["mix_w_out", "ssm_w_glu", "conv_w_out", "mix_w_in"], "ffn1": ["ffn1_w_in", "ffn1_w_out"]}
REDUCE_AT = {"ffn2_bwd": [("swap", "ple")], "dw_ffn2_in": [("exchange", "ple")],
             "mix_bwd_b": [("swap", "ffn2"), ("join", "ple")], "s5_scan_bwd": [("exchange", "ffn2")],
             "mix_bwd_a": [("join", "ffn2")], "ffn1_bwd": [("swap", "mix")]}
BEGIN_AT = {"dw_ffn1_in": [("small", None), ("exchange", "mix")]}
LAST_GROUP = "ffn1"


class _Sched:
    def __init__(self, bufs, cmidx):
        self.bufs, self.cmidx = bufs, cmidx
        self.W, self.G, self.raw, self.small_buf = {}, {}, None, None
        self.got1, self.p32, self.pbf, self.got2, self.half, self.theirs = {}, {}, {}, {}, {}, {}
        self._open, self._split = [], {}
        self._standalone("gather_ffn1", [("gather", GATHER_FIRST)])

    def _payload(self, stage, key):
        if stage == "gather":
            return _gather_payload([self.bufs[n] for n in key])
        if stage == "small":
            return _allgather_payload(_pack([self.raw[k] for k in RAW_ORDER]))
        names = REDUCE_GROUP[key]
        if stage == "swap":
            return _swap_payload([self.G[n] for n in names])
        if stage == "exchange":
            for n in names:
                self.p32[n], self.pbf[n] = _pair_sum(self.cmidx, self.G[n], self.got1[n], "pair_sum_" + n)
            return _exchange_payload([self.pbf[n] for n in names])
        for n in names:
            self.half[n] = _chip_sum(self.p32[n], self.got2[n], "chip_sum_" + n)
        return _join_payload([self.half[n] for n in names])

    def _store(self, stages, got):
        for (stage, key), outs in zip(stages, got):
            if stage == "gather":
                self.W.update(zip(key, outs))
            elif stage == "small":
                self.small_buf = outs[0]
            else:
                {"swap": self.got1, "exchange": self.got2, "join": self.theirs}[stage].update(
                    zip(REDUCE_GROUP[key], outs))

    def _standalone(self, name, stages):
        self._store(stages, _comm_call(name, [self._payload(s, k) for s, k in stages]))

    def carry(self, name):
        tokens = [self._begin(stage, key) for stage, key in BEGIN_AT.get(name, [])]
        self._open = [("gather", GATHER_AT[name])] if name in GATHER_AT else []
        self._open += REDUCE_AT.get(name, [])
        comm = [self._payload(s, k) for s, k in self._open]
        if tokens:
            comm.append(_Payload(tokens, [], {}, [], lambda *a: None, lambda *a: None))
        return tuple(comm)

    def landed(self, name, got):
        self._store(self._open, got)

    def grad(self, name, g4):
        self.G[name] = g4

    def small(self, raw):
        self.raw = raw

    def _begin(self, stage, key):
        p = self._payload(stage, key)
        self._split[stage, key] = (p, _split_start(p, "%s_%s_start" % (stage, key)))
        return self._split[stage, key][1][3]

    def _end(self, stage, key, after):
        p, handle = self._split.pop((stage, key))
        srcs, lands = _split_wait(p, handle, after, "%s_%s_wait" % (stage, key))
        if stage == "swap":
            self.G.update(zip(REDUCE_GROUP[key], srcs))
        self._store([(stage, key)], [lands])

    def tail_begin(self):
        return self._begin("swap", LAST_GROUP)

    def tail_mid(self, after):
        self._end("swap", LAST_GROUP, after)
        token = self._begin("exchange", LAST_GROUP)
        self._end("small", None, [token])
        self._end("exchange", "mix", [token])
        self._standalone("reduce_tail_join_mix", [("join", "mix")])
        return token

    def tail_end(self, after):
        self._end("exchange", LAST_GROUP, after)
        self._standalone("reduce_tail_join", [("join", LAST_GROUP)])


def _small_grads(raw_sum, sp):
    _, vjp = jax.vjp(_zoh, sp["ssm_lam_re"], sp["ssm_lam_im"], sp["ssm_log_step"], sp["ssm_b_re"], sp["ssm_b_im"])
    d_lre, d_lim, d_ls, d_bre, d_bim = vjp((raw_sum["a_re"], raw_sum["a_im"], raw_sum["bb_re"], raw_sum["bb_im"]))
    g = {k: raw_sum[k] for k in ("ln1_g", "ln1_b", "ln2_g", "ln2_b", "ln3_g", "ln3_b", "ln4_g", "ln4_b",
                                 "conv_w", "conv_b", "ssm_c_re", "ssm_c_im", "ssm_d")}
    g.update(ssm_lam_re=d_lre, ssm_lam_im=d_lim, ssm_log_step=d_ls, ssm_b_re=d_bre, ssm_b_im=d_bim)
    return g


def kernel(x, p, ffn1_w_in, ffn1_w_out, ln1_g, ln1_b, mix_w_in, conv_w, conv_b, conv_w_out, ssm_lam_re, ssm_lam_im, ssm_log_step, ssm_b_re, ssm_b_im, ssm_c_re, ssm_c_im, ssm_d, ssm_w_glu, mix_w_out, ln2_g, ln2_b, ffn2_w_in, ffn2_w_out, ln3_g, ln3_b, ple_w_in, ple_w_gate, ln4_g, ln4_b, loss_target, m_ffn1_w_in, m_ffn1_w_out, m_ln1_g, m_ln1_b, m_mix_w_in, m_conv_w, m_conv_b, m_conv_w_out, m_ssm_lam_re, m_ssm_lam_im, m_ssm_log_step, m_ssm_b_re, m_ssm_b_im, m_ssm_c_re, m_ssm_c_im, m_ssm_d, m_ssm_w_glu, m_mix_w_out, m_ln2_g, m_ln2_b, m_ffn2_w_in, m_ffn2_w_out, m_ln3_g, m_ln3_b, m_ple_w_in, m_ple_w_gate, m_ln4_g, m_ln4_b, v_ffn1_w_in, v_ffn1_w_out, v_ln1_g, v_ln1_b, v_mix_w_in, v_conv_w, v_conv_b, v_conv_w_out, v_ssm_lam_re, v_ssm_lam_im, v_ssm_log_step, v_ssm_b_re, v_ssm_b_im, v_ssm_c_re, v_ssm_c_im, v_ssm_d, v_ssm_w_glu, v_mix_w_out, v_ln2_g, v_ln2_b, v_ffn2_w_in, v_ffn2_w_out, v_ln3_g, v_ln3_b, v_ple_w_in, v_ple_w_gate, v_ln4_g, v_ln4_b):
    args = dict(locals())
    w = {n: args[n] for n in WEIGHTS}
    m = {n: args["m_" + n] for n in WEIGHTS}
    v = {n: args["v_" + n] for n in WEIGHTS}
    _, _, c, me = _where()
    cidx = jnp.stack([c, me]).astype(jnp.int32)
    meidx = jnp.reshape(me, (1,)).astype(jnp.int32)

    bufs = {n: _slot_cast(meidx, w[n][0], BF16, "cast_" + n) for n in BIG}
    bufs["conv_w"] = _slot_cast(meidx, jnp.pad(conv_w[0], ((0, 13), (0, 0))), F32, "cast_conv_w")
    sched = _Sched(bufs, cidx)

    sp = {n: (w[n] if w[n].ndim == 2 and n != "ssm_log_step" else w[n][0]) for n in SMALL if n != "conv_w"}
    loss_part, dx0 = _local_step(x[0], p[0, 0], loss_target[0], sp, sched, 512, 256)
    out_g, out_d, out_m, out_v = {}, {}, {}, {}

    def big_adamw(names, token):
        for n in names:
            g, dl, mn, vn = _adamw_pair(cidx, w[n][0], sched.half[n], sched.theirs[n], m[n][0], v[n][0], token,
                                        "adamw_" + n)
            out_g[n], out_d[n], out_m[n], out_v[n] = g[None], dl[None], mn[None], vn[None]

    first = REDUCE_GROUP["ple"] + REDUCE_GROUP["ffn2"]
    big_adamw(first, sched.tail_begin())
    token = sched.tail_mid([out_v[n] for n in first])

    raw_shapes = [sched.raw[k].shape for k in RAW_ORDER]
    raw_sum = dict(zip(RAW_ORDER, _unpack(_sum8(sched.small_buf, token), raw_shapes)))
    loss = raw_sum["loss"][0]
    sg = _small_grads(raw_sum, sp)
    sg["conv_w"] = lax.dynamic_slice_in_dim(sg["conv_w"], me * 128, 128, axis=1)
    small_shapes = [w[n].shape for n in SMALL]
    gp = _pack([sg[n] for n in SMALL])
    d_s, m_s, v_s = _adamw(_pack([w[n] for n in SMALL]), gp, _pack([m[n] for n in SMALL]),
                           _pack([v[n] for n in SMALL]), "adamw_small")

    for n, a, b_, c_, d_ in zip(SMALL, _unpack(gp, small_shapes), _unpack(d_s, small_shapes),
                                _unpack(m_s, small_shapes), _unpack(v_s, small_shapes)):
        out_g[n], out_d[n], out_m[n], out_v[n] = a, b_, c_, d_
    big_adamw(REDUCE_GROUP["mix"], token)
    sched.tail_end([d_s] + [out_v[n] for n in REDUCE_GROUP["mix"]])
    big_adamw(REDUCE_GROUP[LAST_GROUP], token)

    return (loss, dx0[None], *[out_g[n] for n in WEIGHTS], *[out_d[n] for n in WEIGHTS],
            *[out_m[n] for n in WEIGHTS], *[out_v[n] for n in WEIGHTS])
```

```python
import functools
import math

import jax
import jax.numpy as jnp
import numpy as np
from jax import lax
from jax.experimental import pallas as pl
from jax.experimental.pallas import tpu as pltpu

F32, BF16 = jnp.float32, jnp.bfloat16
D = 1024
FF = 2816
FFH = FF // 2
CONV = 512
SSM = 512
GROUPS = 32
STATE = 64
LANES = GROUPS * STATE
SCAN_W = 256
SCAN_R = 256
ALPHA = 2.0 ** 0.25
LN_EPS = 1e-5
GELU_C = math.sqrt(2.0 / math.pi)
B1, B2, LR, EPS, WD, STEP = 0.9, 0.999, 0.001, 1e-8, 0.01, 10
MESH = pl.DeviceIdType.MESH
ANY = pl.BlockSpec(memory_space=pl.ANY)
VMEM_FULL = pl.BlockSpec(memory_space=pltpu.VMEM)


def _cp(vmem_mb=48, n_axes=1):
    return pltpu.CompilerParams(vmem_limit_bytes=vmem_mb << 20,
                                dimension_semantics=("arbitrary",) * n_axes)


def _hbm(*arrs):
    return [pltpu.with_memory_space_constraint(a, pltpu.HBM) for a in arrs]


def _hbm_out(shapes):
    if isinstance(shapes, (list, tuple)):
        return [pltpu.HBM(s.shape, s.dtype) for s in shapes]
    return pltpu.HBM(shapes.shape, shapes.dtype)


def _nn(a, b):
    return jnp.dot(a, b, preferred_element_type=F32)


def _nt(a, b):
    return lax.dot_general(a, b, (((1,), (1,)), ((), ())), preferred_element_type=F32)


def _tn(a, b):
    return lax.dot_general(a, b, (((0,), (0,)), ((), ())), preferred_element_type=F32)


def _sig(v):
    return jax.nn.sigmoid(v)


def _ln_stats(r):
    mu = jnp.mean(r, axis=-1, keepdims=True)
    xc = r - mu
    var = jnp.mean(xc * xc, axis=-1, keepdims=True)
    rstd = lax.rsqrt(var + LN_EPS)
    return xc * rstd, rstd


def _ln_bwd(dy, r, g):
    xhat, rstd = _ln_stats(r)
    dyg = dy * g
    m1 = jnp.mean(dyg, axis=-1, keepdims=True)
    m2 = jnp.mean(dyg * xhat, axis=-1, keepdims=True)
    return rstd * (dyg - m1 - xhat * m2), xhat


def _rowsum(v):
    return jnp.sum(v, axis=0, keepdims=True)


class _Payload:
    def __init__(self, operands, outs, aliases, sems, start, finish):
        self.operands, self.outs, self.aliases, self.sems = list(operands), list(outs), dict(aliases), list(sems)
        self.start, self.finish = start, finish


def _split(flat, comm, attr):
    out, i = [], 0
    for p in comm:
        n = len(getattr(p, attr))
        out.append(list(flat[i:i + n]))
        i += n
    return out


def _run_comm(comm, which, cin, cout, csem):
    for p, a, b, s in zip(comm, _split(cin, comm, "operands"), _split(cout, comm, "outs"), _split(csem, comm, "sems")):
        getattr(p, which)(a, b, s)


def _pcall(body, *, name, grid, in_specs, out_specs, out_shape, operands, scratch=(), vmem_mb=48, aliases=None,
           comm=()):
    ni, no, ns = len(in_specs), len(out_specs), len(scratch)
    c_ops = [a for p in comm for a in p.operands]
    c_outs = [s for p in comm for s in p.outs]
    c_sems = [s for p in comm for s in p.sems]
    io = dict(aliases or {})
    off_i, off_o = ni, no
    for p in comm:
        for a, b in p.aliases.items():
            io[off_i + a] = off_o + b
        off_i += len(p.operands)
        off_o += len(p.outs)

    def wrapped(*refs):
        ins, cin = refs[:ni], refs[ni:ni + len(c_ops)]
        o0 = ni + len(c_ops)
        outs, cout = refs[o0:o0 + no], refs[o0 + no:o0 + no + len(c_outs)]
        s0 = o0 + no + len(c_outs)
        scr, csem = refs[s0:s0 + ns], refs[s0 + ns:]
        if comm:
            first = functools.reduce(jnp.logical_and, [pl.program_id(a) == 0 for a in range(len(grid))])
            pl.when(first)(lambda: _run_comm(comm, "start", cin, cout, csem))
        body(*ins, *outs, *scr)
        if comm:
            last = functools.reduce(jnp.logical_and, [pl.program_id(a) == grid[a] - 1 for a in range(len(grid))])
            pl.when(last)(lambda: _run_comm(comm, "finish", cin, cout, csem))

    res = pl.pallas_call(
        wrapped, name=name, grid=grid,
        in_specs=list(in_specs) + [ANY] * len(c_ops), out_specs=list(out_specs) + [ANY] * len(c_outs),
        out_shape=_hbm_out(list(out_shape) + c_outs), scratch_shapes=list(scratch) + c_sems,
        input_output_aliases=io,
        compiler_params=pltpu.CompilerParams(vmem_limit_bytes=vmem_mb << 20,
                                             dimension_semantics=("arbitrary",) * len(grid),
                                             has_side_effects=bool(comm)),
    )(*_hbm(*operands, *c_ops))
    return list(res[:no]), _split(res[no:], comm, "outs")


def _comm_call(name, comm):
    c_ops = [a for p in comm for a in p.operands]
    c_outs = [s for p in comm for s in p.outs]
    c_sems = [s for p in comm for s in p.sems]
    io, off_i, off_o = {}, 0, 0
    for p in comm:
        for a, b in p.aliases.items():
            io[off_i + a] = off_o + b
        off_i += len(p.operands)
        off_o += len(p.outs)

    def body(*refs):
        cin, cout = refs[:len(c_ops)], refs[len(c_ops):len(c_ops) + len(c_outs)]
        csem = refs[len(c_ops) + len(c_outs):]
        _run_comm(comm, "start", cin, cout, csem)
        _run_comm(comm, "finish", cin, cout, csem)

    res = pl.pallas_call(
        body, name=name, in_specs=[ANY] * len(c_ops), out_specs=[ANY] * len(c_outs), out_shape=_hbm_out(c_outs),
        scratch_shapes=c_sems, input_output_aliases=io,
        compiler_params=pltpu.CompilerParams(has_side_effects=True),
    )(*_hbm(*c_ops))
    return _split(res, comm, "outs")


def _ffn_fwd(x, xb, w_in4, w_out2, g, b, tm, name, comm=()):
    T = x.shape[0]

    def body(x_ref, xb_ref, wg_ref, wu_ref, wo_ref, g_ref, b_ref, h_ref, r_ref, xo_ref, xob_ref, acc):
        k = pl.program_id(1)
        xv = xb_ref[...]
        gt = _nn(xv, wg_ref[...])
        up = _nn(xv, wu_ref[...])
        a = (gt * _sig(gt) * up).astype(BF16)
        h_ref[:, 0:FFH] = gt.astype(BF16)
        h_ref[:, FFH:2 * FFH] = up.astype(BF16)
        acc[...] = jnp.where(k == 0, 0.0, acc[...]) + _nn(a, wo_ref[...])

        @pl.when(k == 1)
        def _():
            r = ALPHA * x_ref[...] + 0.5 * acc[...]
            xhat, _ = _ln_stats(r)
            xo = xhat * g_ref[...] + b_ref[...]
            r_ref[...] = r
            xo_ref[...] = xo
            xob_ref[...] = xo.astype(BF16)

    tok = pl.BlockSpec((tm, D), lambda i, k: (i, 0))
    vec = pl.BlockSpec((1, D), lambda i, k: (0, 0))
    return _pcall(
        body, name=name, grid=(T // tm, 2),
        in_specs=[tok, tok,
                  pl.BlockSpec((None, D, FFH), lambda i, k: (k, 0, 0)),
                  pl.BlockSpec((None, D, FFH), lambda i, k: (k + 2, 0, 0)),
                  pl.BlockSpec((None, FFH, D), lambda i, k: (k, 0, 0)),
                  vec, vec],
        out_specs=[pl.BlockSpec((tm, FF), lambda i, k: (i, k)), tok, tok, tok],
        out_shape=[jax.ShapeDtypeStruct((T, 2 * FF), BF16), jax.ShapeDtypeStruct((T, D), F32),
                   jax.ShapeDtypeStruct((T, D), F32), jax.ShapeDtypeStruct((T, D), BF16)],
        scratch=[pltpu.VMEM((tm, D), F32)], vmem_mb=56, comm=comm,
        operands=(x, xb, w_in4, w_in4, w_out2, g, b))


def _ffn_bwd(dy, r, g, h, w_in4, w_out2, tm, name, comm=()):
    T = dy.shape[0]

    def body(dy_ref, r_ref, g_ref, h_ref, wg_ref, wu_ref, wo_ref,
             dx_ref, dh_ref, a_ref, df_ref, dg_ref, db_ref, acc, dr_s, dfb_s):
        i, k = pl.program_id(0), pl.program_id(1)

        @pl.when(k == 0)
        def _():
            dyv = dy_ref[...]
            dr, xhat = _ln_bwd(dyv, r_ref[...], g_ref[...])
            pg, pb = _rowsum(dyv * xhat), _rowsum(dyv)

            @pl.when(i == 0)
            def _():
                dg_ref[...] = pg
                db_ref[...] = pb

            @pl.when(i > 0)
            def _():
                dg_ref[...] += pg
                db_ref[...] += pb

            dr_s[...] = dr
            dfb = (0.5 * dr).astype(BF16)
            dfb_s[...] = dfb
            df_ref[...] = dfb

        da = _nt(dfb_s[...], wo_ref[...])
        gt = h_ref[:, 0:FFH].astype(F32)
        up = h_ref[:, FFH:2 * FFH].astype(F32)
        sg = _sig(gt)
        silu = gt * sg
        dgate = (da * up * (sg * (1.0 + gt * (1.0 - sg)))).astype(BF16)
        dup = (da * silu).astype(BF16)
        a_ref[...] = (silu * up).astype(BF16)
        dh_ref[:, 0:FFH] = dgate
        dh_ref[:, FFH:2 * FFH] = dup
        acc[...] = jnp.where(k == 0, 0.0, acc[...]) + _nt(dgate, wg_ref[...]) + _nt(dup, wu_ref[...])

        @pl.when(k == 1)
        def _():
            dx_ref[...] = ALPHA * dr_s[...] + acc[...]

    tok = pl.BlockSpec((tm, D), lambda i, k: (i, 0))
    vec = pl.BlockSpec((1, D), lambda i, k: (0, 0))
    wide = pl.BlockSpec((tm, FF), lambda i, k: (i, k))
    return _pcall(
        body, name=name, grid=(T // tm, 2),
        in_specs=[tok, tok, vec, wide,
                  pl.BlockSpec((None, D, FFH), lambda i, k: (k, 0, 0)),
                  pl.BlockSpec((None, D, FFH), lambda i, k: (k + 2, 0, 0)),
                  pl.BlockSpec((None, FFH, D), lambda i, k: (k, 0, 0))],
        out_specs=[tok, wide, pl.BlockSpec((tm, FFH), lambda i, k: (i, k)), tok, vec, vec],
        out_shape=[jax.ShapeDtypeStruct((T, D), F32), jax.ShapeDtypeStruct((T, 2 * FF), BF16),
                   jax.ShapeDtypeStruct((T, FF), BF16), jax.ShapeDtypeStruct((T, D), BF16),
                   jax.ShapeDtypeStruct((1, D), F32), jax.ShapeDtypeStruct((1, D), F32)],
        scratch=[pltpu.VMEM((tm, D), F32), pltpu.VMEM((tm, D), F32), pltpu.VMEM((tm, D), BF16)],
        vmem_mb=56, comm=comm, operands=(dy, r, g, h, w_in4, w_in4, w_out2))


def _mm_tn(a, b, tk, tn, name, shard_cols=None, interleaved=False, comm=()):
    T, K = a.shape
    N = b.shape[1]

    def body(a_ref, b_ref, o_ref):
        o_ref[...] = _tn(a_ref[...], b_ref[...])

    if shard_cols is None:
        out_shape = jax.ShapeDtypeStruct((K, N), F32)
        out_spec = pl.BlockSpec((tk, tn), lambda ki, nj: (ki, nj))
    else:
        per = shard_cols // tn

        def shard(nj):
            blk = nj // per
            return (blk % 2) * 2 + blk // 2 if interleaved else blk

        out_shape = jax.ShapeDtypeStruct((N // shard_cols, K, shard_cols), F32)
        out_spec = pl.BlockSpec((None, tk, tn), lambda ki, nj: (shard(nj), ki, nj % per))
    (out,), got = _pcall(
        body, name=name, grid=(K // tk, N // tn),
        in_specs=[pl.BlockSpec((T, tk), lambda ki, nj: (0, ki)), pl.BlockSpec((T, tn), lambda ki, nj: (0, nj))],
        out_specs=[out_spec], out_shape=[out_shape], comm=comm, operands=(a, b))
    return out, got


def _mix_fwd_a(xb, w_mix4, conv_w, conv_b, w_co4, tm):
    T = xb.shape[0]

    def body(xb_ref, w_ref, cw_ref, cb_ref, wco_ref,
             pc_ref, z_ref, yin_ref, su_ref, sub_ref, gc_ref, gs_ref, yc_ref, qbuf):
        @pl.when(pl.program_id(0) == 0)
        def _():
            qbuf[pl.ds(0, 8), :] = jnp.zeros((8, CONV), F32)

        xv = xb_ref[...]
        p0 = _nn(xv, w_ref[0])
        p1 = _nn(xv, w_ref[1])
        gc_ref[...] = _nn(xv, w_ref[2])
        gs_ref[...] = _nn(xv, w_ref[3])
        cbv, ccv = p0[:, :CONV], p0[:, CONV:]
        chv, suv = p1[:, :CONV], p1[:, CONV:]
        q = ccv * chv
        qbuf[pl.ds(8, tm), :] = q
        cw = cw_ref[...]
        z = (cw[2:3] * q + cw[1:2] * qbuf[pl.ds(7, tm), :] + cw[0:1] * qbuf[pl.ds(6, tm), :]
             + cb_ref[...])
        qbuf[pl.ds(0, 8), :] = q[tm - 8:tm]
        yin = (cbv * z).astype(BF16)
        pc_ref[:, 0:CONV] = cbv.astype(BF16)
        pc_ref[:, CONV:2 * CONV] = ccv.astype(BF16)
        pc_ref[:, 2 * CONV:3 * CONV] = chv.astype(BF16)
        z_ref[...] = z.astype(BF16)
        yin_ref[...] = yin
        su_ref[...] = suv
        sub_ref[...] = suv.astype(BF16)
        for k in range(4):
            yc_ref[:, 256 * k:256 * (k + 1)] = _nn(yin, wco_ref[k])

    def tok(n):
        return pl.BlockSpec((tm, n), lambda i: (i, 0))

    def full(shape):
        return pl.BlockSpec(shape, lambda i: (0,) * len(shape))

    return pl.pallas_call(
        body, name="mix_fwd_a", grid=(T // tm,),
        in_specs=[tok(D), full((4, D, D)), full((3, CONV)), full((1, CONV)), full((4, CONV, 256))],
        out_specs=[tok(3 * CONV), tok(CONV), tok(CONV), tok(SSM), tok(SSM), tok(D), tok(D), tok(D)],
        out_shape=_hbm_out([jax.ShapeDtypeStruct((T, 3 * CONV), BF16), jax.ShapeDtypeStruct((T, CONV), BF16),
                            jax.ShapeDtypeStruct((T, CONV), BF16), jax.ShapeDtypeStruct((T, SSM), F32),
                            jax.ShapeDtypeStruct((T, SSM), BF16), jax.ShapeDtypeStruct((T, D), F32),
                            jax.ShapeDtypeStruct((T, D), F32), jax.ShapeDtypeStruct((T, D), F32)]),
        scratch_shapes=[pltpu.VMEM((tm + 8, CONV), F32)],
        compiler_params=_cp(56, 1),
    )(*_hbm(xb, w_mix4, conv_w, conv_b, w_co4))


def _scan_inplace(bre, bim, ar, ai, T, rev):
    R = SCAN_R
    if rev:
        ai = -ai
    d = 1
    while d < T:
        if d < 8:
            def step(i, _, d=d, ar=ar, ai=ai):
                c = i if rev else T // R - 1 - i
                t0 = pl.multiple_of(c * R, R)
                if rev:
                    wr = bre[pl.ds(t0 + 8, R + 8), :]
                    wi = bim[pl.ds(t0 + 8, R + 8), :]
                    shr = pltpu.roll(wr, R + 8 - d, 0)[0:R]
                    shi = pltpu.roll(wi, R + 8 - d, 0)[0:R]
                    cr, ci = wr[0:R], wi[0:R]
                else:
                    wr = bre[pl.ds(t0, R + 8), :]
                    wi = bim[pl.ds(t0, R + 8), :]
                    shr = pltpu.roll(wr, d, 0)[8:8 + R]
                    shi = pltpu.roll(wi, d, 0)[8:8 + R]
                    cr, ci = wr[8:8 + R], wi[8:8 + R]
                bre[pl.ds(t0 + 8, R), :] = cr + ar * shr - ai * shi
                bim[pl.ds(t0 + 8, R), :] = ci + ar * shi + ai * shr
                return 0

            lax.fori_loop(0, T // R, step, 0)
        else:
            def upd(lo, n, d=d, ar=ar, ai=ai):
                src = lo + d if rev else lo - d
                if not isinstance(lo, int):
                    lo, src = pl.multiple_of(lo + 8, 8), pl.multiple_of(src + 8, 8)
                else:
                    lo, src = lo + 8, src + 8
                cr = bre[pl.ds(lo, n), :]
                ci = bim[pl.ds(lo, n), :]
                shr = bre[pl.ds(src, n), :]
                shi = bim[pl.ds(src, n), :]
                bre[pl.ds(lo, n), :] = cr + ar * shr - ai * shi
                bim[pl.ds(lo, n), :] = ci + ar * shi + ai * shr

            nfull = (T - d) // R if d >= R else T // R - 1

            def step(i, _, upd=upd, d=d):
                if rev:
                    t0 = i * R
                else:
                    t0 = T - (i + 1) * R
                upd(t0, R)
                return 0

            if nfull > 0:
                lax.fori_loop(0, nfull, step, 0)
            if d < R:
                if rev:
                    upd(T - R, R - d)
                else:
                    upd(d, R - d)
        ar, ai = ar * ar - ai * ai, 2.0 * ar * ai
        d *= 2


def _scan_specs(T):
    W = SCAN_W
    lane = pl.BlockSpec((T, W), lambda j: (0, j))
    col = pl.BlockSpec((T, 128), lambda j: (0, j // 2))
    wb = pl.BlockSpec((None, 128, W), lambda j: (j, 0, 0))
    wc = pl.BlockSpec((None, W, 128), lambda j: (j, 0, 0))
    vec = pl.BlockSpec((1, W), lambda j: (0, j))
    return lane, col, wb, wc, vec


def _s5_scan_fwd(su_b, wb_re, wb_im, a_re, a_im, comm=()):
    T = su_b.shape[0]
    W = SCAN_W

    def body(su_ref, wbr_ref, wbi_ref, ar_ref, ai_ref, sr_ref, si_ref, bre, bim):
        zero = jnp.zeros((8, W), F32)
        for buf in (bre, bim):
            buf[pl.ds(0, 8), :] = zero
            buf[pl.ds(T + 8, 8), :] = zero
        su = su_ref[...]
        bre[pl.ds(8, T), :] = _nn(su, wbr_ref[...])
        bim[pl.ds(8, T), :] = _nn(su, wbi_ref[...])
        _scan_inplace(bre, bim, ar_ref[...], ai_ref[...], T, rev=False)
        sr_ref[...] = bre[pl.ds(8, T), :]
        si_ref[...] = bim[pl.ds(8, T), :]

    lane, col, wb, wc, vec = _scan_specs(T)
    return _pcall(
        body, name="s5_scan_fwd", grid=(LANES // W,),
        in_specs=[col, wb, wb, vec, vec],
        out_specs=[lane, lane],
        out_shape=[jax.ShapeDtypeStruct((T, LANES), F32)] * 2,
        scratch=[pltpu.VMEM((T + 16, W), F32)] * 2, comm=comm,
        operands=(su_b, wb_re, wb_im, a_re, a_im))


def _gelu(s):
    th = jnp.tanh(GELU_C * (s + 0.044715 * s * s * s))
    return 0.5 * s * (1.0 + th), th


def _mix_fwd_b(st_re, st_im, wc_re4, wc_im4, su, dvec, w_glu4, g_conv, g_ssm, y_conv, w_mo, x1, g, b, tm, comm=()):
    T = su.shape[0]

    def body(sr_ref, si_ref, wcr_ref, wci_ref, su_ref, d_ref, wg_ref, gc_ref, gs_ref, yc_ref, wmo_ref,
             x_ref, g_ref, b_ref, s_ref, sgb_ref, ga_ref, gb_ref, mb_ref, r_ref, xo_ref, xob_ref):
        srb = sr_ref[...].astype(BF16)
        sib = si_ref[...].astype(BF16)
        ys = [_nn(srb[:, 512 * J:512 * (J + 1)], wcr_ref[J]) + _nn(sib[:, 512 * J:512 * (J + 1)], wci_ref[J])
              for J in range(4)]
        s = jnp.concatenate(ys, axis=1) + d_ref[...] * su_ref[...]
        sg, _ = _gelu(s)
        sgb = sg.astype(BF16)
        ga = jnp.concatenate([_nn(sgb, wg_ref[0]), _nn(sgb, wg_ref[1])], axis=1)
        gb = jnp.concatenate([_nn(sgb, wg_ref[2]), _nn(sgb, wg_ref[3])], axis=1)
        merged = _sig(gc_ref[...]) * yc_ref[...] + _sig(gs_ref[...]) * (ga * _sig(gb))
        mb = merged.astype(BF16)
        r = ALPHA * x_ref[...] + _nn(mb, wmo_ref[...])
        xhat, _ = _ln_stats(r)
        xo = xhat * g_ref[...] + b_ref[...]
        s_ref[...] = s
        sgb_ref[...] = sgb
        ga_ref[...] = ga
        gb_ref[...] = gb
        mb_ref[...] = mb
        r_ref[...] = r
        xo_ref[...] = xo
        xob_ref[...] = xo.astype(BF16)

    def tok(n):
        return pl.BlockSpec((tm, n), lambda i: (i, 0))

    def full(shape):
        return pl.BlockSpec(shape, lambda i: (0,) * len(shape))

    return _pcall(
        body, name="mix_fwd_b", grid=(T // tm,),
        in_specs=[tok(LANES), tok(LANES), full((4, 512, 128)), full((4, 512, 128)), tok(SSM), full((1, SSM)),
                  full((4, SSM, 512)), tok(D), tok(D), tok(D), full((D, D)), tok(D), full((1, D)), full((1, D))],
        out_specs=[tok(SSM), tok(SSM), tok(D), tok(D), tok(D), tok(D), tok(D), tok(D)],
        out_shape=[jax.ShapeDtypeStruct((T, SSM), F32), jax.ShapeDtypeStruct((T, SSM), BF16),
                   jax.ShapeDtypeStruct((T, D), F32), jax.ShapeDtypeStruct((T, D), F32),
                   jax.ShapeDtypeStruct((T, D), BF16), jax.ShapeDtypeStruct((T, D), F32),
                   jax.ShapeDtypeStruct((T, D), F32), jax.ShapeDtypeStruct((T, D), BF16)],
        vmem_mb=56, comm=comm,
        operands=(st_re, st_im, wc_re4, wc_im4, su, dvec, w_glu4, g_conv, g_ssm, y_conv, w_mo, x1, g, b))


def _ple_loss(x3, x3b, p, w_pi4, w_pg, g, b, target, tm):
    T = x3.shape[0]
    PD = p.shape[1]

    def body(x_ref, xb_ref, p_ref, wpi_ref, wpg_ref, g_ref, b_ref, t_ref,
             loss_ref, dx_ref, pb_ref, dpw_ref, dgt_ref, dg_ref, db_ref):
        i = pl.program_id(0)
        pb = p_ref[...].astype(BF16)
        pw = jnp.concatenate([_nn(pb, wpi_ref[k]) for k in range(4)], axis=1)
        gt = _nn(xb_ref[...], wpg_ref[...])
        sg = _sig(gt)
        r = ALPHA * x_ref[...] + pw * sg
        gv = g_ref[...]
        xhat, rstd = _ln_stats(r)
        err = xhat * gv + b_ref[...] - t_ref[...]
        lpart = jnp.zeros((1, 128), F32) + 0.5 * jnp.sum(jnp.mean(err * err, axis=-1, keepdims=True))
        dy = err * (1.0 / D)
        dyg = dy * gv
        m1 = jnp.mean(dyg, axis=-1, keepdims=True)
        m2 = jnp.mean(dyg * xhat, axis=-1, keepdims=True)
        dr = rstd * (dyg - m1 - xhat * m2)
        pg, pbias = _rowsum(dy * xhat), _rowsum(dy)

        @pl.when(i == 0)
        def _():
            loss_ref[...] = lpart
            dg_ref[...] = pg
            db_ref[...] = pbias

        @pl.when(i > 0)
        def _():
            loss_ref[...] += lpart
            dg_ref[...] += pg
            db_ref[...] += pbias

        dgt = (dr * pw * sg * (1.0 - sg)).astype(BF16)
        pb_ref[...] = pb
        dpw_ref[...] = (dr * sg).astype(BF16)
        dgt_ref[...] = dgt
        dx_ref[...] = ALPHA * dr + _nt(dgt, wpg_ref[...])

    def tok(n):
        return pl.BlockSpec((tm, n), lambda i: (i, 0))

    def full(shape):
        return pl.BlockSpec(shape, lambda i: (0,) * len(shape))

    return pl.pallas_call(
        body, name="ple_loss", grid=(T // tm,),
        in_specs=[tok(D), tok(D), tok(PD), full((4, PD, 256)), full((D, D)), full((1, D)), full((1, D)), tok(D)],
        out_specs=[full((1, 128)), tok(D), tok(PD), tok(D), tok(D), full((1, D)), full((1, D))],
        out_shape=_hbm_out([jax.ShapeDtypeStruct((1, 128), F32), jax.ShapeDtypeStruct((T, D), F32),
                            jax.ShapeDtypeStruct((T, PD), BF16), jax.ShapeDtypeStruct((T, D), BF16),
                            jax.ShapeDtypeStruct((T, D), BF16), jax.ShapeDtypeStruct((1, D), F32),
                            jax.ShapeDtypeStruct((1, D), F32)]),
        compiler_params=_cp(48, 1),
    )(*_hbm(x3, x3b, p, w_pi4, w_pg, g, b, target))


def _mix_bwd_b(dy, r2, g, w_mo, g_conv, g_ssm, y_conv, ga, gb, s, su, dvec, w_glu4, wc_re4, wc_im4, tm, comm=()):
    T = dy.shape[0]

    def body(dy_ref, r_ref, g_ref, wmo_ref, gc_ref, gs_ref, yc_ref, ga_ref, gb_ref, s_ref, su_ref, d_ref,
             wg_ref, wcr_ref, wci_ref,
             dres_ref, dmix_ref, dgl_ref, dsb_ref, dud_ref, gsr_ref, gsi_ref, dyc_ref, dp_ref,
             dg_ref, db_ref, dd_ref):
        i = pl.program_id(0)
        dyv = dy_ref[...]
        dr, xhat = _ln_bwd(dyv, r_ref[...], g_ref[...])
        dmix = dr.astype(BF16)
        dmerged = _nt(dmix, wmo_ref[...])
        sc, ss, sgb = _sig(gc_ref[...]), _sig(gs_ref[...]), _sig(gb_ref[...])
        gav = ga_ref[...]
        yssm = gav * sgb
        dgc = dmerged * yc_ref[...] * sc * (1.0 - sc)
        dgss = dmerged * yssm * ss * (1.0 - ss)
        dyssm = dmerged * ss
        dgl = jnp.concatenate([dyssm * sgb, dyssm * gav * sgb * (1.0 - sgb)], axis=1).astype(BF16)
        dsg = (_nt(dgl[:, 0:512], wg_ref[0]) + _nt(dgl[:, 512:1024], wg_ref[1])
               + _nt(dgl[:, 1024:1536], wg_ref[2]) + _nt(dgl[:, 1536:2048], wg_ref[3]))
        sv = s_ref[...]
        _, th = _gelu(sv)
        dgelu = 0.5 * (1.0 + th) + 0.5 * sv * (1.0 - th * th) * GELU_C * (1.0 + 3.0 * 0.044715 * sv * sv)
        ds = dsg * dgelu
        dsb = ds.astype(BF16)
        pg, pb, pd = _rowsum(dyv * xhat), _rowsum(dyv), _rowsum(ds * su_ref[...])

        @pl.when(i == 0)
        def _():
            dg_ref[...] = pg
            db_ref[...] = pb
            dd_ref[...] = pd

        @pl.when(i > 0)
        def _():
            dg_ref[...] += pg
            db_ref[...] += pb
            dd_ref[...] += pd

        dres_ref[...] = ALPHA * dr
        dmix_ref[...] = dmix
        dgl_ref[...] = dgl
        dsb_ref[...] = dsb
        dud_ref[...] = ds * d_ref[...]
        for J in range(4):
            gsr_ref[:, 512 * J:512 * (J + 1)] = _nt(dsb[:, 128 * J:128 * (J + 1)], wcr_ref[J])
            gsi_ref[:, 512 * J:512 * (J + 1)] = _nt(dsb[:, 128 * J:128 * (J + 1)], wci_ref[J])
        dyc_ref[...] = (dmerged * sc).astype(BF16)
        dp_ref[:, 0:D] = dgc.astype(BF16)
        dp_ref[:, D:2 * D] = dgss.astype(BF16)

    def tok(n):
        return pl.BlockSpec((tm, n), lambda i: (i, 0))

    def full(shape):
        return pl.BlockSpec(shape, lambda i: (0,) * len(shape))

    return _pcall(
        body, name="mix_bwd_b", grid=(T // tm,),
        in_specs=[tok(D), tok(D), full((1, D)), full((D, D)), tok(D), tok(D), tok(D), tok(D), tok(D),
                  tok(SSM), tok(SSM), full((1, SSM)), full((4, SSM, 512)), full((4, 512, 128)), full((4, 512, 128))],
        out_specs=[tok(D), tok(D), tok(2 * D), tok(SSM), tok(SSM), tok(LANES), tok(LANES), tok(D),
                   pl.BlockSpec((tm, 2 * D), lambda i: (i, 1)), full((1, D)), full((1, D)), full((1, SSM))],
        out_shape=[jax.ShapeDtypeStruct((T, D), F32), jax.ShapeDtypeStruct((T, D), BF16),
                   jax.ShapeDtypeStruct((T, 2 * D), BF16), jax.ShapeDtypeStruct((T, SSM), BF16),
                   jax.ShapeDtypeStruct((T, SSM), F32), jax.ShapeDtypeStruct((T, LANES), F32),
                   jax.ShapeDtypeStruct((T, LANES), F32), jax.ShapeDtypeStruct((T, D), BF16),
                   jax.ShapeDtypeStruct((T, 4 * D), BF16), jax.ShapeDtypeStruct((1, D), F32),
                   jax.ShapeDtypeStruct((1, D), F32), jax.ShapeDtypeStruct((1, SSM), F32)],
        vmem_mb=56, comm=comm,
        operands=(dy, r2, g, w_mo, g_conv, g_ssm, y_conv, ga, gb, s, su, dvec, w_glu4, wc_re4, wc_im4))


def _s5_scan_bwd(gs_re, gs_im, st_re, st_im, su_b, ds_b, wb_re, wb_im, a_re, a_im, comm=()):
    T = su_b.shape[0]
    W = SCAN_W
    R = SCAN_R

    def body(gr_ref, gi_ref, sr_ref, si_ref, su_ref, ds_ref, wbr_ref, wbi_ref, ar_ref, ai_ref,
             dsu_ref, dwbr_ref, dwbi_ref, dwcr_ref, dwci_ref, dar_ref, dai_ref, gre, gim):
        j = pl.program_id(0)
        zero = jnp.zeros((8, W), F32)
        for buf in (gre, gim):
            buf[pl.ds(0, 8), :] = zero
            buf[pl.ds(T + 8, 8), :] = zero
        gre[pl.ds(8, T), :] = gr_ref[...]
        gim[pl.ds(8, T), :] = gi_ref[...]
        _scan_inplace(gre, gim, ar_ref[...], ai_ref[...], T, rev=True)
        grb = gre[pl.ds(8, T), :].astype(BF16)
        gib = gim[pl.ds(8, T), :].astype(BF16)
        part = _nt(grb, wbr_ref[...]) + _nt(gib, wbi_ref[...])

        @pl.when(j % 2 == 0)
        def _():
            dsu_ref[...] = part

        @pl.when(j % 2 == 1)
        def _():
            dsu_ref[...] += part

        su = su_ref[...]
        dwbr_ref[...] = _tn(su, grb)
        dwbi_ref[...] = _tn(su, gib)
        dsv = ds_ref[...]
        dwcr_ref[...] = _tn(sr_ref[...].astype(BF16), dsv)
        dwci_ref[...] = _tn(si_ref[...].astype(BF16), dsv)
        dar = jnp.zeros((1, W), F32)
        dai = jnp.zeros((1, W), F32)
        for c in range(T // R):
            xr = sr_ref[pl.ds(c * R, R), :]
            xi = si_ref[pl.ds(c * R, R), :]
            g1r = gre[pl.ds(c * R + 9, R), :]
            g1i = gim[pl.ds(c * R + 9, R), :]
            dar = dar + _rowsum(g1r * xr + g1i * xi)
            dai = dai + _rowsum(g1i * xr - g1r * xi)
        dar_ref[...] = dar
        dai_ref[...] = dai

    lane, col, wb, wc, vec = _scan_specs(T)
    return _pcall(
        body, name="s5_scan_bwd", grid=(LANES // W,),
        in_specs=[lane, lane, lane, lane, col, col, wb, wb, vec, vec],
        out_specs=[col, wb, wb, wc, wc, vec, vec],
        out_shape=[jax.ShapeDtypeStruct((T, SSM), F32),
                   jax.ShapeDtypeStruct((LANES // W, 128, W), F32), jax.ShapeDtypeStruct((LANES // W, 128, W), F32),
                   jax.ShapeDtypeStruct((LANES // W, W, 128), F32), jax.ShapeDtypeStruct((LANES // W, W, 128), F32),
                   jax.ShapeDtypeStruct((1, LANES), F32), jax.ShapeDtypeStruct((1, LANES), F32)],
        scratch=[pltpu.VMEM((T + 16, W), F32)] * 2, vmem_mb=56, comm=comm,
        operands=(gs_re, gs_im, st_re, st_im, su_b, ds_b, wb_re, wb_im, a_re, a_im))


def _mix_bwd_a(dyc_b, w_co4, pc, z_b, conv_w, dsu_ssm, du_dir, dproj, dres, w_mix4, tm, comm=()):
    T = dres.shape[0]
    nt = T // tm

    def body(dyc_ref, wco_ref, pc_ref, halo_ref, z_ref, cw_ref, dsu_ref, dud_ref, dpin_ref, dres_ref, w_ref,
             dp_ref, dx_ref, dcw_ref, dcb_ref, dzbuf, qbuf):
        i = pl.program_id(0)
        ii = nt - 1 - i

        @pl.when(i == 0)
        def _():
            dzbuf[pl.ds(tm, 8), :] = jnp.zeros((8, CONV), F32)

        dyc = dyc_ref[...]
        dyin = (_nt(dyc[:, 0:256], wco_ref[0]) + _nt(dyc[:, 256:512], wco_ref[1])
                + _nt(dyc[:, 512:768], wco_ref[2]) + _nt(dyc[:, 768:1024], wco_ref[3]))
        cbv = pc_ref[:, 0:CONV].astype(F32)
        ccv = pc_ref[:, CONV:2 * CONV].astype(F32)
        chv = pc_ref[:, 2 * CONV:3 * CONV].astype(F32)
        dcbv = dyin * z_ref[...].astype(F32)
        dz = dyin * cbv
        dzbuf[pl.ds(0, tm), :] = dz
        cw = cw_ref[...]
        dq = cw[2:3] * dz + cw[1:2] * dzbuf[pl.ds(1, tm), :] + cw[0:1] * dzbuf[pl.ds(2, tm), :]
        dzbuf[pl.ds(tm, 8), :] = dz[0:8]
        q = ccv * chv
        hq = halo_ref[:, CONV:2 * CONV].astype(F32) * halo_ref[:, 2 * CONV:3 * CONV].astype(F32)
        qbuf[pl.ds(0, 8), :] = jnp.where(ii > 0, hq, jnp.zeros_like(hq))
        qbuf[pl.ds(8, tm), :] = q
        pw = jnp.concatenate([_rowsum(dz * qbuf[pl.ds(6, tm), :]), _rowsum(dz * qbuf[pl.ds(7, tm), :]),
                              _rowsum(dz * q), jnp.zeros((5, CONV), F32)], axis=0)
        pbias = _rowsum(dz)

        @pl.when(i == 0)
        def _():
            dcw_ref[...] = pw
            dcb_ref[...] = pbias

        @pl.when(i > 0)
        def _():
            dcw_ref[...] += pw
            dcb_ref[...] += pbias

        dp0 = jnp.concatenate([dcbv, dq * chv], axis=1).astype(BF16)
        dp1 = jnp.concatenate([dq * ccv, dsu_ref[...] + dud_ref[...]], axis=1).astype(BF16)
        dp_ref[:, 0:D] = dp0
        dp_ref[:, D:2 * D] = dp1
        dx_ref[...] = (dres_ref[...] + _nt(dp0, w_ref[0]) + _nt(dp1, w_ref[1])
                       + _nt(dpin_ref[:, 0:D], w_ref[2]) + _nt(dpin_ref[:, D:2 * D], w_ref[3]))

    def tok(n):
        return pl.BlockSpec((tm, n), lambda i: (nt - 1 - i, 0))

    def full(shape):
        return pl.BlockSpec(shape, lambda i: (0,) * len(shape))

    halo = pl.BlockSpec((8, 3 * CONV), lambda i: (jnp.maximum((nt - 1 - i) * (tm // 8) - 1, 0), 0))
    return _pcall(
        body, name="mix_bwd_a", grid=(nt,),
        in_specs=[tok(D), full((4, CONV, 256)), tok(3 * CONV), halo, tok(CONV), full((3, CONV)),
                  tok(SSM), tok(SSM), pl.BlockSpec((tm, 2 * D), lambda i: (nt - 1 - i, 1)), tok(D),
                  full((4, D, D))],
        out_specs=[pl.BlockSpec((tm, 2 * D), lambda i: (nt - 1 - i, 0)), tok(D), full((8, CONV)), full((1, CONV))],
        out_shape=[jax.ShapeDtypeStruct((T, 4 * D), BF16), jax.ShapeDtypeStruct((T, D), F32),
                   jax.ShapeDtypeStruct((8, CONV), F32), jax.ShapeDtypeStruct((1, CONV), F32)],
        scratch=[pltpu.VMEM((tm + 8, CONV), F32), pltpu.VMEM((tm + 8, CONV), F32)],
        aliases={8: 0}, vmem_mb=56, comm=comm,
        operands=(dyc_b, w_co4, pc, pc, z_b, conv_w, dsu_ssm, du_dir, dproj, dres, w_mix4))


def _zoh(lam_re, lam_im, log_step, b_re, b_im):
    dt = jnp.exp(log_step)[:, None]
    mag = jnp.exp(lam_re * dt)
    abr, abi = mag * jnp.cos(lam_im * dt), mag * jnp.sin(lam_im * dt)
    nr, ni = abr - 1.0, abi
    den = lam_re * lam_re + lam_im * lam_im
    cr = (nr * lam_re + ni * lam_im) / den
    ci = (ni * lam_re - nr * lam_im) / den
    bbr = cr[..., None] * b_re - ci[..., None] * b_im
    bbi = cr[..., None] * b_im + ci[..., None] * b_re
    return abr, abi, bbr, bbi


_WB_MASK = (np.arange(8)[None, :, None] == 4 * np.arange(2)[:, None, None] + np.arange(4)[None, None, :]
            ).astype(np.float32)
_EYE8 = np.eye(8, dtype=np.float32)


def _wb_blocks(bb):
    bt = bb.transpose(0, 2, 1).reshape(4, 1, 8, 16, 1, STATE)
    full = bt * _WB_MASK[None, :, :, None, :, None]
    return full.reshape(8, 128, SCAN_W).astype(BF16)


def _wc_blocks(cc):
    ct = cc.transpose(0, 2, 1).reshape(4, 8, STATE, 1, 16)
    full = ct * _EYE8[None, :, None, :, None]
    return full.reshape(4, 512, 128).astype(BF16)


def _wb_diag(dwb8):
    d6 = dwb8.reshape(4, 2, 8, 16, 4, STATE) * _WB_MASK[None, :, :, None, :, None]
    return d6.sum(axis=(1, 4)).reshape(GROUPS, 16, STATE).transpose(0, 2, 1)


def _wc_diag(dwc8):
    mask = _WB_MASK.transpose(0, 2, 1)
    d6 = dwc8.reshape(4, 2, 4, STATE, 8, 16) * mask[None, :, :, None, :, None]
    return d6.sum(axis=4).reshape(GROUPS, STATE, 16).transpose(0, 2, 1)


def _where():
    x, y, c = lax.axis_index("x"), lax.axis_index("y"), lax.axis_index("c")
    return x, y, c, 2 * x + y


def _chip_dev(k, c):
    return (k // 2, k % 2, c)


def _slot_cast(meidx, w, dtype, name, token=()):
    R, C = w.shape
    tr = _row_tile(R)

    def body(m_ref, w_ref, *rest):
        rest[-1][...] = w_ref[...].astype(dtype)

    gs = pltpu.PrefetchScalarGridSpec(
        num_scalar_prefetch=1, grid=(R // tr,),
        in_specs=[pl.BlockSpec((tr, C), lambda i, m: (i, 0))] + [pl.BlockSpec((8, 128), lambda i, m: (0, 0))] * len(token),
        out_specs=pl.BlockSpec((None, tr, C), lambda i, m: (m[0], i, 0)))
    return pl.pallas_call(
        body, name=name, grid_spec=gs, out_shape=_hbm_out(jax.ShapeDtypeStruct((4, R, C), dtype)),
        compiler_params=_cp(32, 1),
    )(meidx, *_hbm(w), *token)


def _gather_ici_payload(bufs):
    def copies(ins, lnd, ss, rs):
        x, y, c, me = _where()
        cps = []
        for w, b in enumerate(bufs):
            h = b.shape[1] // 2
            mine = lnd[w].at[me, pl.ds(c * h, h)]
            for s in range(3):
                k = (me + 1 + s) % 4
                cps.append(pltpu.make_async_remote_copy(
                    src_ref=mine, dst_ref=mine, send_sem=ss.at[3 * w + s], recv_sem=rs.at[3 * w + s],
                    device_id=_chip_dev(k, c), device_id_type=MESH))
        return cps

    p = _sym_payload([], [jax.ShapeDtypeStruct(b.shape, b.dtype) for b in bufs], copies, 3 * len(bufs))
    p.lands = list(bufs)
    return p


def _gather_pass_payload(bufs):
    def copies(ins, outs, ss, rs):
        x, y, c, me = _where()
        cps = []
        for w, b in enumerate(bufs):
            h = b.shape[1] // 2
            for s in range(3):
                j = (me + 1 + s) % 4
                cps.append(pltpu.make_async_remote_copy(
                    src_ref=ins[w].at[j, pl.ds(c * h, h)], dst_ref=outs[w].at[j, pl.ds(c * h, h)],
                    send_sem=ss.at[3 * w + s], recv_sem=rs.at[3 * w + s], device_id=(x, y, 1 - c),
                    device_id_type=MESH))
        return cps

    p = _sym_payload(bufs, [jax.ShapeDtypeStruct(b.shape, b.dtype) for b in bufs], copies, 3 * len(bufs))
    p.aliases = {w: w for w in range(len(bufs))}
    return p


def _gather_payload(bufs):
    n = len(bufs)

    def half(ref, w, k, cc):
        h = bufs[w].shape[1] // 2
        return ref.at[k, pl.ds(cc * h, h)]

    def ici(ins, outs, sems, w, s):
        x, y, c, me = _where()
        k = (me + 1 + s) % 4
        return pltpu.make_async_remote_copy(
            src_ref=half(ins[w], w, me, c), dst_ref=half(outs[w], w, me, c), send_sem=sems[0].at[3 * w + s],
            recv_sem=sems[1].at[3 * w + s], device_id=_chip_dev(k, c), device_id_type=MESH)

    def landed(outs, sems, w, s):
        x, y, c, me = _where()
        j = (me + 3 - s) % 4
        return pltpu.make_async_remote_copy(
            src_ref=half(outs[w], w, j, c), dst_ref=half(outs[w], w, j, c), send_sem=sems[0].at[3 * w + s],
            recv_sem=sems[1].at[3 * w + s], device_id=(x, y, 1 - c), device_id_type=MESH)

    def passed(outs, sems, w, s, cc):
        x, y, c, me = _where()
        j = (me + 3 - s) % 4
        return pltpu.make_async_remote_copy(
            src_ref=half(outs[w], w, j, cc), dst_ref=half(outs[w], w, j, cc), send_sem=sems[2].at[3 * w + s],
            recv_sem=sems[3].at[3 * w + s], device_id=(x, y, 1 - c), device_id_type=MESH)

    pairs = [(w, s) for w in range(n) for s in range(3)]

    def start(ins, outs, sems):
        for w, s in pairs:
            ici(ins, outs, sems, w, s).start()

    def finish(ins, outs, sems):
        _, _, c, _ = _where()
        for w, s in pairs:
            landed(outs, sems, w, s).wait_recv()
            passed(outs, sems, w, s, c).start()
        for w, s in pairs:
            passed(outs, sems, w, s, 1 - c).wait_recv()
        for w, s in pairs:
            ici(ins, outs, sems, w, s).wait_send()
            passed(outs, sems, w, s, c).wait_send()

    return _Payload(bufs, [jax.ShapeDtypeStruct(b.shape, b.dtype) for b in bufs], {w: w for w in range(n)},
                    [pltpu.SemaphoreType.DMA((3 * n,))] * 4, start, finish)


def _sym_payload(operands, outs, copies, n_copies):
    def start(ins, outs_, sems):
        for cp in copies(ins, outs_, sems[0], sems[1]):
            cp.start()

    def finish(ins, outs_, sems):
        for cp in copies(ins, outs_, sems[0], sems[1]):
            cp.wait()

    p = _Payload(operands, outs, {}, [pltpu.SemaphoreType.DMA((n_copies,))] * 2, start, finish)
    p.copies, p.n_copies = copies, n_copies
    return p


def _swap_payload(g4s):
    def copies(ins, outs, ss, rs):
        x, y, c, me = _where()
        cps = []
        for w, g in enumerate(g4s):
            h = g.shape[1] // 2
            cps.append(pltpu.make_async_remote_copy(
                src_ref=ins[w].at[:, pl.ds((1 - c) * h, h)], dst_ref=outs[w], send_sem=ss.at[w],
                recv_sem=rs.at[w], device_id=(x, y, 1 - c), device_id_type=MESH))
        return cps

    outs = [jax.ShapeDtypeStruct((4, g.shape[1] // 2, g.shape[2]), g.dtype) for g in g4s]
    return _sym_payload(g4s, outs, copies, len(g4s))


def _exchange_payload(pbs):
    def copies(ins, outs, ss, rs):
        x, y, c, me = _where()
        cps = []
        for w in range(len(pbs)):
            for s in range(3):
                k = (me + 1 + s) % 4
                cps.append(pltpu.make_async_remote_copy(
                    src_ref=ins[w].at[k], dst_ref=outs[w].at[2 - s], send_sem=ss.at[3 * w + s],
                    recv_sem=rs.at[3 * w + s], device_id=_chip_dev(k, c), device_id_type=MESH))
        return cps

    outs = [jax.ShapeDtypeStruct((3,) + p.shape[1:], p.dtype) for p in pbs]
    return _sym_payload(pbs, outs, copies, 3 * len(pbs))


HBM_REF = pl.BlockSpec(memory_space=pltpu.HBM)
SEM_REF = pl.BlockSpec(memory_space=pltpu.SEMAPHORE)
DATAFLOW = pltpu.SideEffectType.DATAFLOW_SIDE_EFFECTING


class _SemList:
    def __init__(self, refs):
        self.refs = refs

    @property
    def at(self):
        return self.refs


def _split_start(p, name):
    n_in, n_out, nc = len(p.operands), len(p.outs), p.n_copies
    lands = getattr(p, "lands", None) or [lax.empty(s.shape, s.dtype) for s in p.outs]

    def body(*refs):
        ins, lnd = refs[:n_in], refs[n_in:n_in + n_out]
        sems = refs[n_in + n_out:n_in + n_out + 2 * nc]
        for cp in p.copies(ins, lnd, _SemList(sems[:nc]), _SemList(sems[nc:])):
            cp.start()
        refs[-1][...] = jnp.zeros((8, 128), F32)

    res = pl.pallas_call(
        body, name=name,
        in_specs=[HBM_REF] * (n_in + n_out),
        out_specs=[SEM_REF] * (2 * nc) + [HBM_REF] * (n_in + n_out) + [VMEM_FULL],
        out_shape=([pltpu.SemaphoreType.DMA(())] * (2 * nc) + _hbm_out(p.operands) + _hbm_out(lands)
                   + [jax.ShapeDtypeStruct((8, 128), F32)]),
        input_output_aliases={i: 2 * nc + i for i in range(n_in + n_out)},
        compiler_params=pltpu.CompilerParams(has_side_effects=DATAFLOW),
    )(*_hbm(*p.operands, *lands))
    k = 2 * nc
    return list(res[:k]), list(res[k:k + n_in]), list(res[k + n_in:k + n_in + n_out]), res[-1]


def _split_wait(p, handle, after, name):
    sems, srcs, lands, _ = handle
    n_in, n_out, nc = len(srcs), len(lands), p.n_copies

    def body(*refs):
        ins, lnd = refs[:n_in], refs[n_in:n_in + n_out]
        sm = refs[n_in + n_out:n_in + n_out + 2 * nc]
        for cp in p.copies(ins, lnd, _SemList(sm[:nc]), _SemList(sm[nc:])):
            cp.wait_send()
            cp.wait_recv()

    res = pl.pallas_call(
        body, name=name,
        in_specs=[HBM_REF] * (n_in + n_out) + [SEM_REF] * (2 * nc) + [ANY] * len(after),
        out_specs=[HBM_REF] * (n_in + n_out), out_shape=_hbm_out(srcs) + _hbm_out(lands),
        input_output_aliases={i: i for i in range(n_in + n_out)},
        compiler_params=pltpu.CompilerParams(has_side_effects=DATAFLOW),
    )(*srcs, *lands, *sems, *after)
    return list(res[:n_in]), list(res[n_in:])


def _join_payload(halves):
    def copies(ins, outs, ss, rs):
        x, y, c, me = _where()
        return [pltpu.make_async_remote_copy(
            src_ref=ins[w], dst_ref=outs[w], send_sem=ss.at[w], recv_sem=rs.at[w],
            device_id=(x, y, 1 - c), device_id_type=MESH) for w in range(len(halves))]

    outs = [jax.ShapeDtypeStruct(a.shape, a.dtype) for a in halves]
    return _sym_payload(halves, outs, copies, len(halves))


def _allgather_payload(v):
    def copies(ins, outs, ss, rs):
        x, y, c, me = _where()
        lin = 4 * x + 2 * y + c
        cps = []
        for o in range(1, 8):
            t = (lin + o) % 8
            cps.append(pltpu.make_async_remote_copy(
                src_ref=ins[0], dst_ref=outs[0].at[lin], send_sem=ss.at[o - 1], recv_sem=rs.at[o - 1],
                device_id=(t // 4, (t // 2) % 2, t % 2), device_id_type=MESH))
        return cps

    p = _sym_payload([v], [jax.ShapeDtypeStruct((8,) + v.shape, v.dtype)], copies, 7)
    x, y, c, _ = _where()
    p.lands = [lax.dynamic_update_slice(jnp.zeros((8,) + v.shape, v.dtype), v[None], (4 * x + 2 * y + c, 0, 0))]
    return p


def _sum8(buf, token):
    _, P, C = buf.shape

    def body(b_ref, t_ref, o_ref):
        acc = b_ref[0]
        for d in range(1, 8):
            acc = acc + b_ref[d]
        o_ref[...] = acc

    return pl.pallas_call(
        body, name="sum8", in_specs=[VMEM_FULL, VMEM_FULL], out_specs=VMEM_FULL,
        out_shape=jax.ShapeDtypeStruct((P, C), F32),
        compiler_params=pltpu.CompilerParams(vmem_limit_bytes=32 << 20),
    )(buf, token)


def _row_tile(h):
    for t in (256, 176, 128, 64, 32, 16, 8):
        if h % t == 0:
            return t
    raise ValueError(h)


def _pair_sum(cmidx, g4, got, name):
    _, R, C = g4.shape
    h = R // 2
    th = _row_tile(h)

    def body(cm_ref, a_ref, b_ref, o_ref, ob_ref):
        sm = a_ref[...] + b_ref[...]
        ob_ref[...] = sm.astype(BF16)

        @pl.when(pl.program_id(1) == cm_ref[1])
        def _():
            o_ref[...] = sm

    blk = pl.BlockSpec((None, th, C), lambda i, k, cm: (k, i, 0))
    gs = pltpu.PrefetchScalarGridSpec(
        num_scalar_prefetch=1, grid=(h // th, 4),
        in_specs=[pl.BlockSpec((None, None, th, C), lambda i, k, cm: (k, cm[0], i, 0)), blk],
        out_specs=[pl.BlockSpec((th, C), lambda i, k, cm: (i, 0)), blk])
    return pl.pallas_call(
        body, name=name, grid_spec=gs,
        out_shape=_hbm_out([jax.ShapeDtypeStruct((h, C), F32), jax.ShapeDtypeStruct((4, h, C), BF16)]),
        compiler_params=_cp(32, 2),
    )(cmidx, *_hbm(g4.reshape(4, 2, h, C), got))


def _chip_sum(own, got, name):
    h, C = own.shape
    th = _row_tile(h)

    def body(a_ref, b_ref, o_ref):
        o_ref[...] = ((a_ref[...] + b_ref[0].astype(F32)) + b_ref[1].astype(F32)) + b_ref[2].astype(F32)

    return pl.pallas_call(
        body, name=name, grid=(h // th,),
        in_specs=[pl.BlockSpec((th, C), lambda i: (i, 0)), pl.BlockSpec((3, th, C), lambda i: (0, i, 0))],
        out_specs=pl.BlockSpec((th, C), lambda i: (i, 0)),
        out_shape=_hbm_out(jax.ShapeDtypeStruct((h, C), F32)),
        compiler_params=_cp(32, 1),
    )(*_hbm(own, got))


def _adamw_math(w, g, m, v):
    m2 = B1 * m + (1.0 - B1) * g
    v2 = B2 * v + (1.0 - B2) * (g * g)
    m_hat = m2 / (1.0 - B1 ** STEP)
    v_hat = v2 / (1.0 - B2 ** STEP)
    delta = -LR * (m_hat / (jnp.sqrt(v_hat) + EPS) + WD * w)
    return delta, m2, v2


def _adamw_pair(cidx, w, mine, theirs, m, v, token, name):
    R, C = w.shape
    h = R // 2
    tr = _row_tile(h)
    nh = h // tr

    def body(c_ref, w_ref, a_ref, b_ref, m_ref, v_ref, t_ref, g_ref, d_ref, mo_ref, vo_ref):
        own = (pl.program_id(0) // nh) == c_ref[0]
        g = jnp.where(own, a_ref[...], b_ref[...])
        d, m2, v2 = _adamw_math(w_ref[...], g, m_ref[...], v_ref[...])
        g_ref[...] = g
        d_ref[...] = d
        mo_ref[...] = m2
        vo_ref[...] = v2

    blk = pl.BlockSpec((tr, C), lambda i, c: (i, 0))
    mine_blk = pl.BlockSpec((tr, C), lambda i, c: (jnp.clip(i - c[0] * nh, 0, nh - 1), 0))
    theirs_blk = pl.BlockSpec((tr, C), lambda i, c: (jnp.clip(i - (1 - c[0]) * nh, 0, nh - 1), 0))
    gs = pltpu.PrefetchScalarGridSpec(
        num_scalar_prefetch=1, grid=(R // tr,),
        in_specs=[blk, mine_blk, theirs_blk, blk, blk, pl.BlockSpec((8, 128), lambda i, c: (0, 0))],
        out_specs=[blk] * 4)
    return pl.pallas_call(
        body, name=name, grid_spec=gs, out_shape=_hbm_out([jax.ShapeDtypeStruct((R, C), F32)] * 4),
        compiler_params=_cp(32, 1),
    )(cidx, *_hbm(w, mine, theirs, m, v), token)


def _adamw(w, g, m, v, name):
    R, C = w.shape
    tr = _row_tile(R)

    def body(w_ref, g_ref, m_ref, v_ref, d_ref, mo_ref, vo_ref):
        d, m2, v2 = _adamw_math(w_ref[...], g_ref[...], m_ref[...], v_ref[...])
        d_ref[...] = d
        mo_ref[...] = m2
        vo_ref[...] = v2

    blk = pl.BlockSpec((tr, C), lambda i: (i, 0))
    return pl.pallas_call(
        body, name=name, grid=(R // tr,), in_specs=[blk] * 4, out_specs=[blk] * 3,
        out_shape=_hbm_out([jax.ShapeDtypeStruct((R, C), F32)] * 3),
        compiler_params=_cp(32, 1),
    )(*_hbm(w, g, m, v))


def _pack(arrs):
    flat = jnp.concatenate([a.reshape(-1).astype(F32) for a in arrs])
    rows = -(-flat.shape[0] // 1024)
    rows = -(-rows // 8) * 8
    return jnp.pad(flat, (0, rows * 1024 - flat.shape[0])).reshape(rows, 1024)


def _unpack(packed, shapes):
    flat = packed.reshape(-1)
    out, off = [], 0
    for s in shapes:
        n = math.prod(s)
        out.append(flat[off:off + n].reshape(s))
        off += n
    return out


BIG = ["ffn1_w_in", "ffn1_w_out", "mix_w_in", "conv_w_out", "ssm_w_glu", "mix_w_out",
       "ffn2_w_in", "ffn2_w_out", "ple_w_in", "ple_w_gate"]
SMALL = ["ln1_g", "ln1_b", "conv_w", "conv_b", "ssm_lam_re", "ssm_lam_im", "ssm_log_step", "ssm_b_re", "ssm_b_im",
         "ssm_c_re", "ssm_c_im", "ssm_d", "ln2_g", "ln2_b", "ln3_g", "ln3_b", "ln4_g", "ln4_b"]
WEIGHTS = ["ffn1_w_in", "ffn1_w_out", "ln1_g", "ln1_b", "mix_w_in", "conv_w", "conv_b", "conv_w_out",
           "ssm_lam_re", "ssm_lam_im", "ssm_log_step", "ssm_b_re", "ssm_b_im", "ssm_c_re", "ssm_c_im", "ssm_d",
           "ssm_w_glu", "mix_w_out", "ln2_g", "ln2_b", "ffn2_w_in", "ffn2_w_out", "ln3_g", "ln3_b",
           "ple_w_in", "ple_w_gate", "ln4_g", "ln4_b"]


class _NoComm:
    def __init__(self, W):
        self.W, self.G, self.raw = dict(W), {}, None

    def carry(self, name):
        return ()

    def landed(self, name, got):
        pass

    def grad(self, name, g4):
        self.G[name] = g4

    def small(self, raw):
        self.raw = raw


def _s5_operands(sp):
    abr, abi, bbr, bbi = _zoh(sp["ssm_lam_re"], sp["ssm_lam_im"], sp["ssm_log_step"], sp["ssm_b_re"], sp["ssm_b_im"])
    return (_wb_blocks(bbr), _wb_blocks(bbi), _wc_blocks(sp["ssm_c_re"]), _wc_blocks(-sp["ssm_c_im"]),
            abr.reshape(1, LANES), abi.reshape(1, LANES), sp["ssm_d"].reshape(1, SSM))


def _local_step(x, p, target, sp, sched, tm_ffn, tm_mix, ops=None):
    W = sched.W
    wb_re, wb_im, wc_re4, wc_im4, a_re, a_im, dvec = ops if ops is not None else _s5_operands(sp)

    def run(fn, name, *args, **kw):
        outs, got = fn(*args, comm=sched.carry(name), **kw)
        sched.landed(name, got)
        return outs

    def dw(name, wname, a, b, tk, tn, shape4, shard_cols=None, interleaved=False):
        out, got = _mm_tn(a, b, tk, tn, name, shard_cols=shard_cols, interleaved=interleaved,
                          comm=sched.carry(name))
        sched.landed(name, got)
        sched.grad(wname, out.reshape(shape4))

    xb = x.astype(BF16)
    h1, r1, x1, x1b = run(_ffn_fwd, "ffn1_fwd", x, xb, W["ffn1_w_in"], W["ffn1_w_out"].reshape(2, FFH, D),
                          sp["ln1_g"], sp["ln1_b"], tm_ffn, "ffn1_fwd")
    conv_w = W["conv_w"][:, 0:3, :].transpose(1, 0, 2).reshape(3, CONV)
    pc, z_b, yin_b, su, su_b, g_conv, g_ssm, y_conv = _mix_fwd_a(
        x1b, W["mix_w_in"], conv_w, sp["conv_b"], W["conv_w_out"], tm_mix)
    st_re, st_im = run(_s5_scan_fwd, "s5_scan_fwd", su_b, wb_re, wb_im, a_re, a_im)
    w_mo = W["mix_w_out"].reshape(D, D)
    s, sg_b, ga, gb, merged_b, r2, x2, x2b = run(
        _mix_fwd_b, "mix_fwd_b", st_re, st_im, wc_re4, wc_im4, su, dvec, W["ssm_w_glu"], g_conv, g_ssm, y_conv,
        w_mo, x1, sp["ln2_g"], sp["ln2_b"], tm_mix)
    w2o2 = W["ffn2_w_out"].reshape(2, FFH, D)
    h2, r3, x3, x3b = run(_ffn_fwd, "ffn2_fwd", x2, x2b, W["ffn2_w_in"], w2o2, sp["ln3_g"], sp["ln3_b"], tm_ffn,
                          "ffn2_fwd")
    loss_part, dx3, p_b, dpw_b, dgt_b, dg4, db4 = _ple_loss(
        x3, x3b, p, W["ple_w_in"], W["ple_w_gate"].reshape(D, D), sp["ln4_g"], sp["ln4_b"], target, tm_mix)

    dw("dw_ple_gate", "ple_w_gate", x3b, dgt_b, 512, 1024, (4, 256, D))
    dw("dw_ple_in", "ple_w_in", p_b, dpw_b, 256, 256, (4, 256, 256), shard_cols=256)
    dx2, dh2, a2_b, df2_b, dg3, db3 = run(_ffn_bwd, "ffn2_bwd", dx3, r3, sp["ln3_g"], h2, W["ffn2_w_in"], w2o2,
                                          tm_mix, "ffn2_bwd")
    dw("dw_ffn2_in", "ffn2_w_in", x2b, dh2, 512, FFH, (4, D, FFH), shard_cols=FFH, interleaved=True)
    dw("dw_ffn2_out", "ffn2_w_out", a2_b, df2_b, FFH, 1024, (4, FF // 4, D))
    (dres, dmix_b, dgl_b, ds_b, du_dir, gs_re, gs_im, dyc_b, dproj, dg2, db2, dd) = run(
        _mix_bwd_b, "mix_bwd_b", dx2, r2, sp["ln2_g"], w_mo, g_conv, g_ssm, y_conv, ga, gb, s, su, dvec,
        W["ssm_w_glu"], wc_re4, wc_im4, tm_mix)
    dw("dw_mix_out", "mix_w_out", merged_b, dmix_b, 512, 1024, (4, 256, D))
    dw("dw_glu", "ssm_w_glu", sg_b, dgl_b, 512, 512, (4, SSM, 512), shard_cols=512)
    dsu_ssm, dwb_re, dwb_im, dwc_re, dwc_im, da_re, da_im = run(
        _s5_scan_bwd, "s5_scan_bwd", gs_re, gs_im, st_re, st_im, su_b, ds_b, wb_re, wb_im, a_re, a_im)
    dw("dw_conv_out", "conv_w_out", yin_b, dyc_b, 512, 256, (4, CONV, 256), shard_cols=256)
    dproj, dx1, dcw8, dcb = run(_mix_bwd_a, "mix_bwd_a", dyc_b, W["conv_w_out"], pc, z_b, conv_w, dsu_ssm,
                                du_dir, dproj, dres, W["mix_w_in"], tm_mix)
    dw("dw_mix_in", "mix_w_in", x1b, dproj, 512, 1024, (4, D, D), shard_cols=1024)
    dx0, dh1, a1_b, df1_b, dg1, db1 = run(_ffn_bwd, "ffn1_bwd", dx1, r1, sp["ln1_g"], h1, W["ffn1_w_in"],
                                          W["ffn1_w_out"].reshape(2, FFH, D), tm_mix, "ffn1_bwd")
    sched.small(dict(
        ln1_g=dg1, ln1_b=db1, ln2_g=dg2, ln2_b=db2, ln3_g=dg3, ln3_b=db3, ln4_g=dg4, ln4_b=db4,
        conv_w=dcw8[0:3], conv_b=dcb,
        a_re=da_re.reshape(GROUPS, STATE), a_im=da_im.reshape(GROUPS, STATE),
        bb_re=_wb_diag(dwb_re), bb_im=_wb_diag(dwb_im),
        ssm_c_re=_wc_diag(dwc_re), ssm_c_im=-_wc_diag(dwc_im), ssm_d=dd.reshape(GROUPS, 16),
        loss=loss_part[0:1, 0]))
    dw("dw_ffn1_in", "ffn1_w_in", xb, dh1, 512, FFH, (4, D, FFH), shard_cols=FFH, interleaved=True)
    dw("dw_ffn1_out", "ffn1_w_out", a1_b, df1_b, FFH, 1024, (4, FF // 4, D))
    return loss_part[0, 0], dx0


RAW_ORDER = ["ln1_g", "ln1_b", "ln2_g", "ln2_b", "ln3_g", "ln3_b", "ln4_g", "ln4_b", "conv_w", "conv_b",
             "a_re", "a_im", "bb_re", "bb_im", "ssm_c_re", "ssm_c_im", "ssm_d", "loss"]

GATHER_FIRST = ["ffn1_w_in", "ffn1_w_out"]
GATHER_AT = {"ffn1_fwd": ["mix_w_in", "conv_w_out", "conv_w", "ssm_w_glu", "mix_w_out"],
             "s5_scan_fwd": ["ffn2_w_in"], "mix_fwd_b": ["ffn2_w_out"], "ffn2_fwd": ["ple_w_in", "ple_w_gate"]}
REDUCE_GROUP = {"ple": ["ple_w_gate", "ple_w_in"], "ffn2": ["ffn2_w_in", "ffn2_w_out"],
                "mix": ["mix_w_out", "ssm_w_glu", "conv_w_out", "mix_w_in"], "ffn1": ["ffn1_w_in", "ffn1_w_out"]}
REDUCE_AT = {"ffn2_bwd": [("swap", "ple")], "dw_ffn2_in": [("exchange", "ple")],
             "mix_bwd_b": [("swap", "ffn2"), ("join", "ple")], "s5_scan_bwd": [("exchange", "ffn2")],
             "mix_bwd_a": [("join", "ffn2")], "ffn1_bwd": [("swap", "mix")]}
BEGIN_AT = {"dw_ffn1_in": [("small", None), ("exchange", "mix")]}
LAST_GROUP = "ffn1"


class _Sched:
    def __init__(self, cmidx):
        self.bufs, self.cmidx = {}, cmidx
        self.W, self.G, self.raw, self.small_buf = {}, {}, None, None
        self.got1, self.p32, self.pbf, self.got2, self.half, self.theirs = {}, {}, {}, {}, {}, {}
        self._open, self._split = [], {}

    def first_begin(self, bufs):
        self.bufs.update(bufs)
        p = _gather_ici_payload([bufs[n] for n in GATHER_FIRST])
        self._first = (p, _split_start(p, "gather_first_start"))
        return self._first[1][3]

    def first_end(self, bufs, after):
        self.bufs.update(bufs)
        p, handle = self._first
        _, landed = _split_wait(p, handle, after, "gather_first_wait")
        (outs,) = _comm_call("gather_first_pass", [_gather_pass_payload(landed)])
        self.W.update(zip(GATHER_FIRST, outs))

    def _payload(self, stage, key):
        if stage == "gather":
            return _gather_payload([self.bufs[n] for n in key])
        if stage == "small":
            return _allgather_payload(_pack([self.raw[k] for k in RAW_ORDER]))
        names = REDUCE_GROUP[key]
        if stage == "swap":
            return _swap_payload([self.G[n] for n in names])
        if stage == "exchange":
            for n in names:
                self.p32[n], self.pbf[n] = _pair_sum(self.cmidx, self.G[n], self.got1[n], "pair_sum_" + n)
            return _exchange_payload([self.pbf[n] for n in names])
        for n in names:
            self.half[n] = _chip_sum(self.p32[n], self.got2[n], "chip_sum_" + n)
        return _join_payload([self.half[n] for n in names])

    def _store(self, stages, got):
        for (stage, key), outs in zip(stages, got):
            if stage == "gather":
                self.W.update(zip(key, outs))
            elif stage == "small":
                self.small_buf = outs[0]
            else:
                {"swap": self.got1, "exchange": self.got2, "join": self.theirs}[stage].update(
                    zip(REDUCE_GROUP[key], outs))

    def _standalone(self, name, stages):
        self._store(stages, _comm_call(name, [self._payload(s, k) for s, k in stages]))

    def carry(self, name):
        tokens = [self._begin(stage, key) for stage, key in BEGIN_AT.get(name, [])]
        self._open = [("gather", GATHER_AT[name])] if name in GATHER_AT else []
        self._open += REDUCE_AT.get(name, [])
        comm = [self._payload(s, k) for s, k in self._open]
        if tokens:
            comm.append(_Payload(tokens, [], {}, [], lambda *a: None, lambda *a: None))
        return tuple(comm)

    def landed(self, name, got):
        self._store(self._open, got)

    def grad(self, name, g4):
        self.G[name] = g4

    def small(self, raw):
        self.raw = raw

    def _begin(self, stage, key):
        p = self._payload(stage, key)
        self._split[stage, key] = (p, _split_start(p, "%s_%s_start" % (stage, key)))
        return self._split[stage, key][1][3]

    def _end(self, stage, key, after):
        p, handle = self._split.pop((stage, key))
        srcs, lands = _split_wait(p, handle, after, "%s_%s_wait" % (stage, key))
        if stage == "swap":
            self.G.update(zip(REDUCE_GROUP[key], srcs))
        self._store([(stage, key)], [lands])

    def tail_begin(self):
        return self._begin("swap", LAST_GROUP)

    def tail_mid(self, after):
        self._end("swap", LAST_GROUP, after)
        token = self._begin("exchange", LAST_GROUP)
        self._end("small", None, [token])
        self._end("exchange", "mix", [token])
        self._standalone("reduce_tail_join_mix", [("join", "mix")])
        return token

    def tail_end(self, after):
        self._end("exchange", LAST_GROUP, after)
        self._standalone("reduce_tail_join", [("join", LAST_GROUP)])


def _small_grads(raw_sum, sp):
    _, vjp = jax.vjp(_zoh, sp["ssm_lam_re"], sp["ssm_lam_im"], sp["ssm_log_step"], sp["ssm_b_re"], sp["ssm_b_im"])
    d_lre, d_lim, d_ls, d_bre, d_bim = vjp((raw_sum["a_re"], raw_sum["a_im"], raw_sum["bb_re"], raw_sum["bb_im"]))
    g = {k: raw_sum[k] for k in ("ln1_g", "ln1_b", "ln2_g", "ln2_b", "ln3_g", "ln3_b", "ln4_g", "ln4_b",
                                 "conv_w", "conv_b", "ssm_c_re", "ssm_c_im", "ssm_d")}
    g.update(ssm_lam_re=d_lre, ssm_lam_im=d_lim, ssm_log_step=d_ls, ssm_b_re=d_bre, ssm_b_im=d_bim)
    return g


def kernel(x, p, ffn1_w_in, ffn1_w_out, ln1_g, ln1_b, mix_w_in, conv_w, conv_b, conv_w_out, ssm_lam_re, ssm_lam_im, ssm_log_step, ssm_b_re, ssm_b_im, ssm_c_re, ssm_c_im, ssm_d, ssm_w_glu, mix_w_out, ln2_g, ln2_b, ffn2_w_in, ffn2_w_out, ln3_g, ln3_b, ple_w_in, ple_w_gate, ln4_g, ln4_b, loss_target, m_ffn1_w_in, m_ffn1_w_out, m_ln1_g, m_ln1_b, m_mix_w_in, m_conv_w, m_conv_b, m_conv_w_out, m_ssm_lam_re, m_ssm_lam_im, m_ssm_log_step, m_ssm_b_re, m_ssm_b_im, m_ssm_c_re, m_ssm_c_im, m_ssm_d, m_ssm_w_glu, m_mix_w_out, m_ln2_g, m_ln2_b, m_ffn2_w_in, m_ffn2_w_out, m_ln3_g, m_ln3_b, m_ple_w_in, m_ple_w_gate, m_ln4_g, m_ln4_b, v_ffn1_w_in, v_ffn1_w_out, v_ln1_g, v_ln1_b, v_mix_w_in, v_conv_w, v_conv_b, v_conv_w_out, v_ssm_lam_re, v_ssm_lam_im, v_ssm_log_step, v_ssm_b_re, v_ssm_b_im, v_ssm_c_re, v_ssm_c_im, v_ssm_d, v_ssm_w_glu, v_mix_w_out, v_ln2_g, v_ln2_b, v_ffn2_w_in, v_ffn2_w_out, v_ln3_g, v_ln3_b, v_ple_w_in, v_ple_w_gate, v_ln4_g, v_ln4_b):
    args = dict(locals())
    w = {n: args[n] for n in WEIGHTS}
    m = {n: args["m_" + n] for n in WEIGHTS}
    v = {n: args["v_" + n] for n in WEIGHTS}
    _, _, c, me = _where()
    cidx = jnp.stack([c, me]).astype(jnp.int32)
    meidx = jnp.reshape(me, (1,)).astype(jnp.int32)

    sched = _Sched(cidx)
    token = sched.first_begin({n: _slot_cast(meidx, w[n][0], BF16, "cast_" + n) for n in GATHER_FIRST})
    rest = {n: _slot_cast(meidx, w[n][0], BF16, "cast_" + n, (token,)) for n in BIG if n not in GATHER_FIRST}
    rest["conv_w"] = _slot_cast(meidx, jnp.pad(conv_w[0], ((0, 13), (0, 0))), F32, "cast_conv_w", (token,))
    sp = {n: (w[n] if w[n].ndim == 2 and n != "ssm_log_step" else w[n][0]) for n in SMALL if n != "conv_w"}
    ops = _s5_operands({**sp, "ssm_lam_re": sp["ssm_lam_re"] + token[0, 0]})
    sched.first_end(rest, list(rest.values()) + list(ops))
    loss_part, dx0 = _local_step(x[0], p[0, 0], loss_target[0], sp, sched, 512, 256, ops)
    out_g, out_d, out_m, out_v = {}, {}, {}, {}

    def big_adamw(names, token):
        for n in names:
            g, dl, mn, vn = _adamw_pair(cidx, w[n][0], sched.half[n], sched.theirs[n], m[n][0], v[n][0], token,
                                        "adamw_" + n)
            out_g[n], out_d[n], out_m[n], out_v[n] = g[None], dl[None], mn[None], vn[None]

    first = REDUCE_GROUP["ple"] + REDUCE_GROUP["ffn2"]
    big_adamw(first, sched.tail_begin())
    token = sched.tail_mid([out_v[n] for n in first])

    raw_shapes = [sched.raw[k].shape for k in RAW_ORDER]
    raw_sum = dict(zip(RAW_ORDER, _unpack(_sum8(sched.small_buf, token), raw_shapes)))
    loss = raw_sum["loss"][0]
    sg = _small_grads(raw_sum, sp)
    sg["conv_w"] = lax.dynamic_slice_in_dim(sg["conv_w"], me * 128, 128, axis=1)
    small_shapes = [w[n].shape for n in SMALL]
    gp = _pack([sg[n] for n in SMALL])
    d_s, m_s, v_s = _adamw(_pack([w[n] for n in SMALL]), gp, _pack([m[n] for n in SMALL]),
                           _pack([v[n] for n in SMALL]), "adamw_small")

    for n, a, b_, c_, d_ in zip(SMALL, _unpack(gp, small_shapes), _unpack(d_s, small_shapes),
                                _unpack(m_s, small_shapes), _unpack(v_s, small_shapes)):
        out_g[n], out_d[n], out_m[n], out_v[n] = a, b_, c_, d_
    big_adamw(REDUCE_GROUP["mix"], token)
    sched.tail_end([d_s] + [out_v[n] for n in REDUCE_GROUP["mix"]])
    big_adamw(REDUCE_GROUP[LAST_GROUP], token)

    return (loss, dx0[None], *[out_g[n] for n in WEIGHTS], *[out_d[n] for n in WEIGHTS],
            *[out_m[n] for n in WEIGHTS], *[out_v[n] for n in WEIGHTS])
```

```python
import functools
import math

import jax
import jax.numpy as jnp
import numpy as np
from jax import lax
from jax.experimental import pallas as pl
from jax.experimental.pallas import tpu as pltpu

F32, BF16 = jnp.float32, jnp.bfloat16
D = 1024
FF = 2816
FFH = FF // 2
CONV = 512
SSM = 512
GROUPS = 32
STATE = 64
LANES = GROUPS * STATE
SCAN_W = 128
SCAN_PER = 512 // SCAN_W
SCAN_GR = SCAN_W // STATE
SCAN_R = 256
ALPHA = 2.0 ** 0.25
LN_EPS = 1e-5
GELU_C = math.sqrt(2.0 / math.pi)
B1, B2, LR, EPS, WD, STEP = 0.9, 0.999, 0.001, 1e-8, 0.01, 10
MESH = pl.DeviceIdType.MESH
ANY = pl.BlockSpec(memory_space=pl.ANY)
VMEM_FULL = pl.BlockSpec(memory_space=pltpu.VMEM)


def _cp(vmem_mb=48, n_axes=1):
    return pltpu.CompilerParams(vmem_limit_bytes=vmem_mb << 20,
                                dimension_semantics=("arbitrary",) * n_axes)


def _hbm(*arrs):
    return [pltpu.with_memory_space_constraint(a, pltpu.HBM) for a in arrs]


def _hbm_out(shapes):
    if isinstance(shapes, (list, tuple)):
        return [pltpu.HBM(s.shape, s.dtype) for s in shapes]
    return pltpu.HBM(shapes.shape, shapes.dtype)


def _nn(a, b):
    return jnp.dot(a, b, preferred_element_type=F32)


def _nt(a, b):
    return lax.dot_general(a, b, (((1,), (1,)), ((), ())), preferred_element_type=F32)


def _tn(a, b):
    return lax.dot_general(a, b, (((0,), (0,)), ((), ())), preferred_element_type=F32)


def _sig(v):
    return jax.nn.sigmoid(v)


def _ln_stats(r):
    mu = jnp.mean(r, axis=-1, keepdims=True)
    xc = r - mu
    var = jnp.mean(xc * xc, axis=-1, keepdims=True)
    rstd = lax.rsqrt(var + LN_EPS)
    return xc * rstd, rstd


def _ln_bwd(dy, r, g):
    xhat, rstd = _ln_stats(r)
    dyg = dy * g
    m1 = jnp.mean(dyg, axis=-1, keepdims=True)
    m2 = jnp.mean(dyg * xhat, axis=-1, keepdims=True)
    return rstd * (dyg - m1 - xhat * m2), xhat


def _rowsum(v):
    return jnp.sum(v, axis=0, keepdims=True)


class _Payload:
    def __init__(self, operands, outs, aliases, sems, start, finish):
        self.operands, self.outs, self.aliases, self.sems = list(operands), list(outs), dict(aliases), list(sems)
        self.start, self.finish = start, finish


def _split(flat, comm, attr):
    out, i = [], 0
    for p in comm:
        n = len(getattr(p, attr))
        out.append(list(flat[i:i + n]))
        i += n
    return out


def _run_comm(comm, which, cin, cout, csem):
    for p, a, b, s in zip(comm, _split(cin, comm, "operands"), _split(cout, comm, "outs"), _split(csem, comm, "sems")):
        getattr(p, which)(a, b, s)


def _pcall(body, *, name, grid, in_specs, out_specs, out_shape, operands, scratch=(), vmem_mb=48, aliases=None,
           comm=()):
    ni, no, ns = len(in_specs), len(out_specs), len(scratch)
    c_ops = [a for p in comm for a in p.operands]
    c_outs = [s for p in comm for s in p.outs]
    c_sems = [s for p in comm for s in p.sems]
    io = dict(aliases or {})
    off_i, off_o = ni, no
    for p in comm:
        for a, b in p.aliases.items():
            io[off_i + a] = off_o + b
        off_i += len(p.operands)
        off_o += len(p.outs)

    def wrapped(*refs):
        ins, cin = refs[:ni], refs[ni:ni + len(c_ops)]
        o0 = ni + len(c_ops)
        outs, cout = refs[o0:o0 + no], refs[o0 + no:o0 + no + len(c_outs)]
        s0 = o0 + no + len(c_outs)
        scr, csem = refs[s0:s0 + ns], refs[s0 + ns:]
        if comm:
            first = functools.reduce(jnp.logical_and, [pl.program_id(a) == 0 for a in range(len(grid))])
            pl.when(first)(lambda: _run_comm(comm, "start", cin, cout, csem))
        body(*ins, *outs, *scr)
        if comm:
            last = functools.reduce(jnp.logical_and, [pl.program_id(a) == grid[a] - 1 for a in range(len(grid))])
            pl.when(last)(lambda: _run_comm(comm, "finish", cin, cout, csem))

    res = pl.pallas_call(
        wrapped, name=name, grid=grid,
        in_specs=list(in_specs) + [ANY] * len(c_ops), out_specs=list(out_specs) + [ANY] * len(c_outs),
        out_shape=_hbm_out(list(out_shape) + c_outs), scratch_shapes=list(scratch) + c_sems,
        input_output_aliases=io,
        compiler_params=pltpu.CompilerParams(vmem_limit_bytes=vmem_mb << 20,
                                             dimension_semantics=("arbitrary",) * len(grid),
                                             has_side_effects=bool(comm)),
    )(*_hbm(*operands, *c_ops))
    return list(res[:no]), _split(res[no:], comm, "outs")


def _comm_call(name, comm):
    c_ops = [a for p in comm for a in p.operands]
    c_outs = [s for p in comm for s in p.outs]
    c_sems = [s for p in comm for s in p.sems]
    io, off_i, off_o = {}, 0, 0
    for p in comm:
        for a, b in p.aliases.items():
            io[off_i + a] = off_o + b
        off_i += len(p.operands)
        off_o += len(p.outs)

    def body(*refs):
        cin, cout = refs[:len(c_ops)], refs[len(c_ops):len(c_ops) + len(c_outs)]
        csem = refs[len(c_ops) + len(c_outs):]
        _run_comm(comm, "start", cin, cout, csem)
        _run_comm(comm, "finish", cin, cout, csem)

    res = pl.pallas_call(
        body, name=name, in_specs=[ANY] * len(c_ops), out_specs=[ANY] * len(c_outs), out_shape=_hbm_out(c_outs),
        scratch_shapes=c_sems, input_output_aliases=io,
        compiler_params=pltpu.CompilerParams(has_side_effects=True),
    )(*_hbm(*c_ops))
    return _split(res, comm, "outs")


def _ffn_fwd(x, xb, w_in4, w_out2, g, b, tm, name, comm=()):
    T = x.shape[0]

    def body(x_ref, xb_ref, wg_ref, wu_ref, wo_ref, g_ref, b_ref, h_ref, r_ref, xo_ref, xob_ref, acc):
        k = pl.program_id(1)
        xv = xb_ref[...]
        gt = _nn(xv, wg_ref[...])
        up = _nn(xv, wu_ref[...])
        a = (gt * _sig(gt) * up).astype(BF16)
        h_ref[:, 0:FFH] = gt.astype(BF16)
        h_ref[:, FFH:2 * FFH] = up.astype(BF16)
        acc[...] = jnp.where(k == 0, 0.0, acc[...]) + _nn(a, wo_ref[...])

        @pl.when(k == 1)
        def _():
            r = ALPHA * x_ref[...] + 0.5 * acc[...]
            xhat, _ = _ln_stats(r)
            xo = xhat * g_ref[...] + b_ref[...]
            r_ref[...] = r
            xo_ref[...] = xo
            xob_ref[...] = xo.astype(BF16)

    tok = pl.BlockSpec((tm, D), lambda i, k: (i, 0))
    vec = pl.BlockSpec((1, D), lambda i, k: (0, 0))
    return _pcall(
        body, name=name, grid=(T // tm, 2),
        in_specs=[tok, tok,
                  pl.BlockSpec((None, D, FFH), lambda i, k: (k, 0, 0)),
                  pl.BlockSpec((None, D, FFH), lambda i, k: (k + 2, 0, 0)),
                  pl.BlockSpec((None, FFH, D), lambda i, k: (k, 0, 0)),
                  vec, vec],
        out_specs=[pl.BlockSpec((tm, FF), lambda i, k: (i, k)), tok, tok, tok],
        out_shape=[jax.ShapeDtypeStruct((T, 2 * FF), BF16), jax.ShapeDtypeStruct((T, D), F32),
                   jax.ShapeDtypeStruct((T, D), F32), jax.ShapeDtypeStruct((T, D), BF16)],
        scratch=[pltpu.VMEM((tm, D), F32)], vmem_mb=56, comm=comm,
        operands=(x, xb, w_in4, w_in4, w_out2, g, b))


def _ffn_bwd(dy, r, g, h, w_in4, w_out2, tm, name, comm=()):
    T = dy.shape[0]

    def body(dy_ref, r_ref, g_ref, h_ref, wg_ref, wu_ref, wo_ref,
             dx_ref, dh_ref, a_ref, df_ref, dg_ref, db_ref, acc, dr_s, dfb_s):
        i, k = pl.program_id(0), pl.program_id(1)

        @pl.when(k == 0)
        def _():
            dyv = dy_ref[...]
            dr, xhat = _ln_bwd(dyv, r_ref[...], g_ref[...])
            pg, pb = _rowsum(dyv * xhat), _rowsum(dyv)

            @pl.when(i == 0)
            def _():
                dg_ref[...] = pg
                db_ref[...] = pb

            @pl.when(i > 0)
            def _():
                dg_ref[...] += pg
                db_ref[...] += pb

            dr_s[...] = dr
            dfb = (0.5 * dr).astype(BF16)
            dfb_s[...] = dfb
            df_ref[...] = dfb

        da = _nt(dfb_s[...], wo_ref[...])
        gt = h_ref[:, 0:FFH].astype(F32)
        up = h_ref[:, FFH:2 * FFH].astype(F32)
        sg = _sig(gt)
        silu = gt * sg
        dgate = (da * up * (sg * (1.0 + gt * (1.0 - sg)))).astype(BF16)
        dup = (da * silu).astype(BF16)
        a_ref[...] = (silu * up).astype(BF16)
        dh_ref[:, 0:FFH] = dgate
        dh_ref[:, FFH:2 * FFH] = dup
        acc[...] = jnp.where(k == 0, 0.0, acc[...]) + _nt(dgate, wg_ref[...]) + _nt(dup, wu_ref[...])

        @pl.when(k == 1)
        def _():
            dx_ref[...] = ALPHA * dr_s[...] + acc[...]

    tok = pl.BlockSpec((tm, D), lambda i, k: (i, 0))
    vec = pl.BlockSpec((1, D), lambda i, k: (0, 0))
    wide = pl.BlockSpec((tm, FF), lambda i, k: (i, k))
    return _pcall(
        body, name=name, grid=(T // tm, 2),
        in_specs=[tok, tok, vec, wide,
                  pl.BlockSpec((None, D, FFH), lambda i, k: (k, 0, 0)),
                  pl.BlockSpec((None, D, FFH), lambda i, k: (k + 2, 0, 0)),
                  pl.BlockSpec((None, FFH, D), lambda i, k: (k, 0, 0))],
        out_specs=[tok, wide, pl.BlockSpec((tm, FFH), lambda i, k: (i, k)), tok, vec, vec],
        out_shape=[jax.ShapeDtypeStruct((T, D), F32), jax.ShapeDtypeStruct((T, 2 * FF), BF16),
                   jax.ShapeDtypeStruct((T, FF), BF16), jax.ShapeDtypeStruct((T, D), BF16),
                   jax.ShapeDtypeStruct((1, D), F32), jax.ShapeDtypeStruct((1, D), F32)],
        scratch=[pltpu.VMEM((tm, D), F32), pltpu.VMEM((tm, D), F32), pltpu.VMEM((tm, D), BF16)],
        vmem_mb=56, comm=comm, operands=(dy, r, g, h, w_in4, w_in4, w_out2))


def _mm_tn(a, b, tk, tn, name, shard_cols=None, interleaved=False, comm=()):
    T, K = a.shape
    N = b.shape[1]

    def body(a_ref, b_ref, o_ref):
        o_ref[...] = _tn(a_ref[...], b_ref[...])

    if shard_cols is None:
        out_shape = jax.ShapeDtypeStruct((K, N), F32)
        out_spec = pl.BlockSpec((tk, tn), lambda ki, nj: (ki, nj))
    else:
        per = shard_cols // tn

        def shard(nj):
            blk = nj // per
            return (blk % 2) * 2 + blk // 2 if interleaved else blk

        out_shape = jax.ShapeDtypeStruct((N // shard_cols, K, shard_cols), F32)
        out_spec = pl.BlockSpec((None, tk, tn), lambda ki, nj: (shard(nj), ki, nj % per))
    (out,), got = _pcall(
        body, name=name, grid=(K // tk, N // tn),
        in_specs=[pl.BlockSpec((T, tk), lambda ki, nj: (0, ki)), pl.BlockSpec((T, tn), lambda ki, nj: (0, nj))],
        out_specs=[out_spec], out_shape=[out_shape], comm=comm, operands=(a, b))
    return out, got


def _mix_fwd_a(xb, w_mix4, conv_w, conv_b, w_co4, tm, comm=()):
    T = xb.shape[0]

    def body(xb_ref, w_ref, cw_ref, cb_ref, wco_ref,
             pc_ref, z_ref, yin_ref, su_ref, sub_ref, gc_ref, gs_ref, yc_ref, qbuf):
        @pl.when(pl.program_id(0) == 0)
        def _():
            qbuf[pl.ds(0, 8), :] = jnp.zeros((8, CONV), F32)

        xv = xb_ref[...]
        p0 = _nn(xv, w_ref[0])
        p1 = _nn(xv, w_ref[1])
        gc_ref[...] = _nn(xv, w_ref[2])
        gs_ref[...] = _nn(xv, w_ref[3])
        cbv, ccv = p0[:, :CONV], p0[:, CONV:]
        chv, suv = p1[:, :CONV], p1[:, CONV:]
        q = ccv * chv
        qbuf[pl.ds(8, tm), :] = q
        cw = cw_ref[...]
        z = (cw[2:3] * q + cw[1:2] * qbuf[pl.ds(7, tm), :] + cw[0:1] * qbuf[pl.ds(6, tm), :]
             + cb_ref[...])
        qbuf[pl.ds(0, 8), :] = q[tm - 8:tm]
        yin = (cbv * z).astype(BF16)
        pc_ref[:, 0:CONV] = cbv.astype(BF16)
        pc_ref[:, CONV:2 * CONV] = ccv.astype(BF16)
        pc_ref[:, 2 * CONV:3 * CONV] = chv.astype(BF16)
        z_ref[...] = z.astype(BF16)
        yin_ref[...] = yin
        su_ref[...] = suv
        sub_ref[...] = suv.astype(BF16)
        for k in range(4):
            yc_ref[:, 256 * k:256 * (k + 1)] = _nn(yin, wco_ref[k])

    def tok(n):
        return pl.BlockSpec((tm, n), lambda i: (i, 0))

    def full(shape):
        return pl.BlockSpec(shape, lambda i: (0,) * len(shape))

    return _pcall(
        body, name="mix_fwd_a", grid=(T // tm,),
        in_specs=[tok(D), full((4, D, D)), full((3, CONV)), full((1, CONV)), full((4, CONV, 256))],
        out_specs=[tok(3 * CONV), tok(CONV), tok(CONV), tok(SSM), tok(SSM), tok(D), tok(D), tok(D)],
        out_shape=[jax.ShapeDtypeStruct((T, 3 * CONV), BF16), jax.ShapeDtypeStruct((T, CONV), BF16),
                   jax.ShapeDtypeStruct((T, CONV), BF16), jax.ShapeDtypeStruct((T, SSM), F32),
                   jax.ShapeDtypeStruct((T, SSM), BF16), jax.ShapeDtypeStruct((T, D), F32),
                   jax.ShapeDtypeStruct((T, D), F32), jax.ShapeDtypeStruct((T, D), F32)],
        scratch=[pltpu.VMEM((tm + 8, CONV), F32)], vmem_mb=56, comm=comm,
        operands=(xb, w_mix4, conv_w, conv_b, w_co4))


def _scan_inplace(bre, bim, ar, ai, T, rev):
    R, W, G = SCAN_R, bre.shape[1], T // 8
    if rev:
        ai = -ai

    def cmul(pr, pi, xr, xi):
        return pr * xr - pi * xi, pr * xi + pi * xr

    pw = [(ar, ai)]
    for _ in range(7):
        pw.append(cmul(ar, ai, *pw[-1]))

    def shifted(v, d, axis, n, idx):
        if rev:
            return jnp.where(idx < n - d, pltpu.roll(v, n - d, axis), 0.0)
        return jnp.where(idx >= d, pltpu.roll(v, d, axis), 0.0)

    def step(i, _):
        t0 = pl.multiple_of(i * R, R)
        vr = bre[pl.ds(t0 + 8, R), :].reshape(R // 8, 8, W)
        vi = bim[pl.ds(t0 + 8, R), :].reshape(R // 8, 8, W)
        sub = lax.broadcasted_iota(jnp.int32, (R // 8, 8, W), 1)
        for d in (1, 2, 4):
            pr, pi = pw[d - 1]
            dr, di = cmul(pr[None], pi[None], shifted(vr, d, 1, 8, sub), shifted(vi, d, 1, 8, sub))
            vr, vi = vr + dr, vi + di
        bre[pl.ds(t0 + 8, R), :] = vr.reshape(R, W)
        bim[pl.ds(t0 + 8, R), :] = vi.reshape(R, W)
        return 0

    lax.fori_loop(0, T // R, step, 0)

    edge = 0 if rev else 7
    cr = bre[pl.ds(8 + edge, G, stride=8), :]
    ci = bim[pl.ds(8 + edge, G, stride=8), :]
    row = lax.broadcasted_iota(jnp.int32, (G, W), 0)
    qr, qi = pw[7]
    d = 1
    while d < G:
        dr, di = cmul(qr, qi, shifted(cr, d, 0, G, row), shifted(ci, d, 0, G, row))
        cr, ci = cr + dr, ci + di
        qr, qi = qr * qr - qi * qi, 2.0 * qr * qi
        d *= 2

    nr, ni = shifted(cr, 1, 0, G, row), shifted(ci, 1, 0, G, row)
    for r in range(8):
        pr, pi = pw[7 - r] if rev else pw[r]
        dr, di = cmul(pr, pi, nr, ni)
        bre[pl.ds(8 + r, G, stride=8), :] = bre[pl.ds(8 + r, G, stride=8), :] + dr
        bim[pl.ds(8 + r, G, stride=8), :] = bim[pl.ds(8 + r, G, stride=8), :] + di


def _scan_specs(T):
    W = SCAN_W
    lane = pl.BlockSpec((T, W), lambda j: (0, j))
    col = pl.BlockSpec((T, 128), lambda j: (0, j // SCAN_PER))
    wb = pl.BlockSpec((None, 128, W), lambda j: (j, 0, 0))
    wc = pl.BlockSpec((None, W, 128), lambda j: (j, 0, 0))
    vec = pl.BlockSpec((1, W), lambda j: (0, j))
    return lane, col, wb, wc, vec


def _s5_scan_fwd(su_b, wb_re, wb_im, a_re, a_im, comm=()):
    T = su_b.shape[0]
    W = SCAN_W

    def body(su_ref, wbr_ref, wbi_ref, ar_ref, ai_ref, sr_ref, si_ref, bre, bim):
        zero = jnp.zeros((8, W), F32)
        for buf in (bre, bim):
            buf[pl.ds(0, 8), :] = zero
            buf[pl.ds(T + 8, 8), :] = zero
        su = su_ref[...]
        bre[pl.ds(8, T), :] = _nn(su, wbr_ref[...])
        bim[pl.ds(8, T), :] = _nn(su, wbi_ref[...])
        _scan_inplace(bre, bim, ar_ref[...], ai_ref[...], T, rev=False)
        sr_ref[...] = bre[pl.ds(8, T), :]
        si_ref[...] = bim[pl.ds(8, T), :]

    lane, col, wb, wc, vec = _scan_specs(T)
    return _pcall(
        body, name="s5_scan_fwd", grid=(LANES // W,),
        in_specs=[col, wb, wb, vec, vec],
        out_specs=[lane, lane],
        out_shape=[jax.ShapeDtypeStruct((T, LANES), F32)] * 2,
        scratch=[pltpu.VMEM((T + 16, W), F32)] * 2, comm=comm,
        operands=(su_b, wb_re, wb_im, a_re, a_im))


def _gelu(s):
    th = jnp.tanh(GELU_C * (s + 0.044715 * s * s * s))
    return 0.5 * s * (1.0 + th), th


def _mix_fwd_b(st_re, st_im, wc_re4, wc_im4, su, dvec, w_glu4, g_conv, g_ssm, y_conv, w_mo, x1, g, b, tm, comm=()):
    T = su.shape[0]

    def body(sr_ref, si_ref, wcr_ref, wci_ref, su_ref, d_ref, wg_ref, gc_ref, gs_ref, yc_ref, wmo_ref,
             x_ref, g_ref, b_ref, s_ref, sgb_ref, ga_ref, gb_ref, mb_ref, r_ref, xo_ref, xob_ref):
        srb = sr_ref[...].astype(BF16)
        sib = si_ref[...].astype(BF16)
        ys = [_nn(srb[:, 512 * J:512 * (J + 1)], wcr_ref[J]) + _nn(sib[:, 512 * J:512 * (J + 1)], wci_ref[J])
              for J in range(4)]
        s = jnp.concatenate(ys, axis=1) + d_ref[...] * su_ref[...]
        sg, _ = _gelu(s)
        sgb = sg.astype(BF16)
        ga = jnp.concatenate([_nn(sgb, wg_ref[0]), _nn(sgb, wg_ref[1])], axis=1)
        gb = jnp.concatenate([_nn(sgb, wg_ref[2]), _nn(sgb, wg_ref[3])], axis=1)
        merged = _sig(gc_ref[...]) * yc_ref[...] + _sig(gs_ref[...]) * (ga * _sig(gb))
        mb = merged.astype(BF16)
        r = ALPHA * x_ref[...] + _nn(mb, wmo_ref[...])
        xhat, _ = _ln_stats(r)
        xo = xhat * g_ref[...] + b_ref[...]
        s_ref[...] = s
        sgb_ref[...] = sgb
        ga_ref[...] = ga
        gb_ref[...] = gb
        mb_ref[...] = mb
        r_ref[...] = r
        xo_ref[...] = xo
        xob_ref[...] = xo.astype(BF16)

    def tok(n):
        return pl.BlockSpec((tm, n), lambda i: (i, 0))

    def full(shape):
        return pl.BlockSpec(shape, lambda i: (0,) * len(shape))

    return _pcall(
        body, name="mix_fwd_b", grid=(T // tm,),
        in_specs=[tok(LANES), tok(LANES), full((4, 512, 128)), full((4, 512, 128)), tok(SSM), full((1, SSM)),
                  full((4, SSM, 512)), tok(D), tok(D), tok(D), full((D, D)), tok(D), full((1, D)), full((1, D))],
        out_specs=[tok(SSM), tok(SSM), tok(D), tok(D), tok(D), tok(D), tok(D), tok(D)],
        out_shape=[jax.ShapeDtypeStruct((T, SSM), F32), jax.ShapeDtypeStruct((T, SSM), BF16),
                   jax.ShapeDtypeStruct((T, D), F32), jax.ShapeDtypeStruct((T, D), F32),
                   jax.ShapeDtypeStruct((T, D), BF16), jax.ShapeDtypeStruct((T, D), F32),
                   jax.ShapeDtypeStruct((T, D), F32), jax.ShapeDtypeStruct((T, D), BF16)],
        vmem_mb=56, comm=comm,
        operands=(st_re, st_im, wc_re4, wc_im4, su, dvec, w_glu4, g_conv, g_ssm, y_conv, w_mo, x1, g, b))


def _ple_loss(x3, x3b, p, w_pi4, w_pg, g, b, target, tm):
    T = x3.shape[0]
    PD = p.shape[1]

    def body(x_ref, xb_ref, p_ref, wpi_ref, wpg_ref, g_ref, b_ref, t_ref,
             loss_ref, dx_ref, pb_ref, dpw_ref, dgt_ref, dg_ref, db_ref):
        i = pl.program_id(0)
        pb = p_ref[...].astype(BF16)
        pw = jnp.concatenate([_nn(pb, wpi_ref[k]) for k in range(4)], axis=1)
        gt = _nn(xb_ref[...], wpg_ref[...])
        sg = _sig(gt)
        r = ALPHA * x_ref[...] + pw * sg
        gv = g_ref[...]
        xhat, rstd = _ln_stats(r)
        err = xhat * gv + b_ref[...] - t_ref[...]
        lpart = jnp.zeros((1, 128), F32) + 0.5 * jnp.sum(jnp.mean(err * err, axis=-1, keepdims=True))
        dy = err * (1.0 / D)
        dyg = dy * gv
        m1 = jnp.mean(dyg, axis=-1, keepdims=True)
        m2 = jnp.mean(dyg * xhat, axis=-1, keepdims=True)
        dr = rstd * (dyg - m1 - xhat * m2)
        pg, pbias = _rowsum(dy * xhat), _rowsum(dy)

        @pl.when(i == 0)
        def _():
            loss_ref[...] = lpart
            dg_ref[...] = pg
            db_ref[...] = pbias

        @pl.when(i > 0)
        def _():
            loss_ref[...] += lpart
            dg_ref[...] += pg
            db_ref[...] += pbias

        dgt = (dr * pw * sg * (1.0 - sg)).astype(BF16)
        pb_ref[...] = pb
        dpw_ref[...] = (dr * sg).astype(BF16)
        dgt_ref[...] = dgt
        dx_ref[...] = ALPHA * dr + _nt(dgt, wpg_ref[...])

    def tok(n):
        return pl.BlockSpec((tm, n), lambda i: (i, 0))

    def full(shape):
        return pl.BlockSpec(shape, lambda i: (0,) * len(shape))

    return pl.pallas_call(
        body, name="ple_loss", grid=(T // tm,),
        in_specs=[tok(D), tok(D), tok(PD), full((4, PD, 256)), full((D, D)), full((1, D)), full((1, D)), tok(D)],
        out_specs=[full((1, 128)), tok(D), tok(PD), tok(D), tok(D), full((1, D)), full((1, D))],
        out_shape=_hbm_out([jax.ShapeDtypeStruct((1, 128), F32), jax.ShapeDtypeStruct((T, D), F32),
                            jax.ShapeDtypeStruct((T, PD), BF16), jax.ShapeDtypeStruct((T, D), BF16),
                            jax.ShapeDtypeStruct((T, D), BF16), jax.ShapeDtypeStruct((1, D), F32),
                            jax.ShapeDtypeStruct((1, D), F32)]),
        compiler_params=_cp(48, 1),
    )(*_hbm(x3, x3b, p, w_pi4, w_pg, g, b, target))


def _mix_bwd_b(dy, r2, g, w_mo, g_conv, g_ssm, y_conv, ga, gb, s, su, dvec, w_glu4, wc_re4, wc_im4, tm, comm=()):
    T = dy.shape[0]

    def body(dy_ref, r_ref, g_ref, wmo_ref, gc_ref, gs_ref, yc_ref, ga_ref, gb_ref, s_ref, su_ref, d_ref,
             wg_ref, wcr_ref, wci_ref,
             dres_ref, dmix_ref, dgl_ref, dsb_ref, dud_ref, gsr_ref, gsi_ref, dyc_ref, dp_ref,
             dg_ref, db_ref, dd_ref):
        i = pl.program_id(0)
        dyv = dy_ref[...]
        dr, xhat = _ln_bwd(dyv, r_ref[...], g_ref[...])
        dmix = dr.astype(BF16)
        dmerged = _nt(dmix, wmo_ref[...])
        sc, ss, sgb = _sig(gc_ref[...]), _sig(gs_ref[...]), _sig(gb_ref[...])
        gav = ga_ref[...]
        yssm = gav * sgb
        dgc = dmerged * yc_ref[...] * sc * (1.0 - sc)
        dgss = dmerged * yssm * ss * (1.0 - ss)
        dyssm = dmerged * ss
        dgl = jnp.concatenate([dyssm * sgb, dyssm * gav * sgb * (1.0 - sgb)], axis=1).astype(BF16)
        dsg = (_nt(dgl[:, 0:512], wg_ref[0]) + _nt(dgl[:, 512:1024], wg_ref[1])
               + _nt(dgl[:, 1024:1536], wg_ref[2]) + _nt(dgl[:, 1536:2048], wg_ref[3]))
        sv = s_ref[...]
        _, th = _gelu(sv)
        dgelu = 0.5 * (1.0 + th) + 0.5 * sv * (1.0 - th * th) * GELU_C * (1.0 + 3.0 * 0.044715 * sv * sv)
        ds = dsg * dgelu
        dsb = ds.astype(BF16)
        pg, pb, pd = _rowsum(dyv * xhat), _rowsum(dyv), _rowsum(ds * su_ref[...])

        @pl.when(i == 0)
        def _():
            dg_ref[...] = pg
            db_ref[...] = pb
            dd_ref[...] = pd

        @pl.when(i > 0)
        def _():
            dg_ref[...] += pg
            db_ref[...] += pb
            dd_ref[...] += pd

        dres_ref[...] = ALPHA * dr
        dmix_ref[...] = dmix
        dgl_ref[...] = dgl
        dsb_ref[...] = dsb
        dud_ref[...] = ds * d_ref[...]
        for J in range(4):
            gsr_ref[:, 512 * J:512 * (J + 1)] = _nt(dsb[:, 128 * J:128 * (J + 1)], wcr_ref[J])
            gsi_ref[:, 512 * J:512 * (J + 1)] = _nt(dsb[:, 128 * J:128 * (J + 1)], wci_ref[J])
        dyc_ref[...] = (dmerged * sc).astype(BF16)
        dp_ref[:, 0:D] = dgc.astype(BF16)
        dp_ref[:, D:2 * D] = dgss.astype(BF16)

    def tok(n):
        return pl.BlockSpec((tm, n), lambda i: (i, 0))

    def full(shape):
        return pl.BlockSpec(shape, lambda i: (0,) * len(shape))

    return _pcall(
        body, name="mix_bwd_b", grid=(T // tm,),
        in_specs=[tok(D), tok(D), full((1, D)), full((D, D)), tok(D), tok(D), tok(D), tok(D), tok(D),
                  tok(SSM), tok(SSM), full((1, SSM)), full((4, SSM, 512)), full((4, 512, 128)), full((4, 512, 128))],
        out_specs=[tok(D), tok(D), tok(2 * D), tok(SSM), tok(SSM), tok(LANES), tok(LANES), tok(D),
                   pl.BlockSpec((tm, 2 * D), lambda i: (i, 1)), full((1, D)), full((1, D)), full((1, SSM))],
        out_shape=[jax.ShapeDtypeStruct((T, D), F32), jax.ShapeDtypeStruct((T, D), BF16),
                   jax.ShapeDtypeStruct((T, 2 * D), BF16), jax.ShapeDtypeStruct((T, SSM), BF16),
                   jax.ShapeDtypeStruct((T, SSM), F32), jax.ShapeDtypeStruct((T, LANES), F32),
                   jax.ShapeDtypeStruct((T, LANES), F32), jax.ShapeDtypeStruct((T, D), BF16),
                   jax.ShapeDtypeStruct((T, 4 * D), BF16), jax.ShapeDtypeStruct((1, D), F32),
                   jax.ShapeDtypeStruct((1, D), F32), jax.ShapeDtypeStruct((1, SSM), F32)],
        vmem_mb=56, comm=comm,
        operands=(dy, r2, g, w_mo, g_conv, g_ssm, y_conv, ga, gb, s, su, dvec, w_glu4, wc_re4, wc_im4))


def _s5_scan_bwd(gs_re, gs_im, st_re, st_im, su_b, ds_b, wb_re, wb_im, a_re, a_im, comm=()):
    T = su_b.shape[0]
    W = SCAN_W
    R = SCAN_R

    def body(gr_ref, gi_ref, sr_ref, si_ref, su_ref, ds_ref, wbr_ref, wbi_ref, ar_ref, ai_ref,
             dsu_ref, dwbr_ref, dwbi_ref, dwcr_ref, dwci_ref, dar_ref, dai_ref, gre, gim):
        j = pl.program_id(0)
        zero = jnp.zeros((8, W), F32)
        for buf in (gre, gim):
            buf[pl.ds(0, 8), :] = zero
            buf[pl.ds(T + 8, 8), :] = zero
        gre[pl.ds(8, T), :] = gr_ref[...]
        gim[pl.ds(8, T), :] = gi_ref[...]
        _scan_inplace(gre, gim, ar_ref[...], ai_ref[...], T, rev=True)
        grb = gre[pl.ds(8, T), :].astype(BF16)
        gib = gim[pl.ds(8, T), :].astype(BF16)
        part = _nt(grb, wbr_ref[...]) + _nt(gib, wbi_ref[...])

        @pl.when(j % SCAN_PER == 0)
        def _():
            dsu_ref[...] = part

        @pl.when(j % SCAN_PER > 0)
        def _():
            dsu_ref[...] += part

        su = su_ref[...]
        dwbr_ref[...] = _tn(su, grb)
        dwbi_ref[...] = _tn(su, gib)
        dsv = ds_ref[...]
        dwcr_ref[...] = _tn(sr_ref[...].astype(BF16), dsv)
        dwci_ref[...] = _tn(si_ref[...].astype(BF16), dsv)
        dar = jnp.zeros((1, W), F32)
        dai = jnp.zeros((1, W), F32)
        for c in range(T // R):
            xr = sr_ref[pl.ds(c * R, R), :]
            xi = si_ref[pl.ds(c * R, R), :]
            g1r = gre[pl.ds(c * R + 9, R), :]
            g1i = gim[pl.ds(c * R + 9, R), :]
            dar = dar + _rowsum(g1r * xr + g1i * xi)
            dai = dai + _rowsum(g1i * xr - g1r * xi)
        dar_ref[...] = dar
        dai_ref[...] = dai

    lane, col, wb, wc, vec = _scan_specs(T)
    return _pcall(
        body, name="s5_scan_bwd", grid=(LANES // W,),
        in_specs=[lane, lane, lane, lane, col, col, wb, wb, vec, vec],
        out_specs=[col, wb, wb, wc, wc, vec, vec],
        out_shape=[jax.ShapeDtypeStruct((T, SSM), F32),
                   jax.ShapeDtypeStruct((LANES // W, 128, W), F32), jax.ShapeDtypeStruct((LANES // W, 128, W), F32),
                   jax.ShapeDtypeStruct((LANES // W, W, 128), F32), jax.ShapeDtypeStruct((LANES // W, W, 128), F32),
                   jax.ShapeDtypeStruct((1, LANES), F32), jax.ShapeDtypeStruct((1, LANES), F32)],
        scratch=[pltpu.VMEM((T + 16, W), F32)] * 2, vmem_mb=56, comm=comm,
        operands=(gs_re, gs_im, st_re, st_im, su_b, ds_b, wb_re, wb_im, a_re, a_im))


def _mix_bwd_a(dyc_b, w_co4, pc, z_b, conv_w, dsu_ssm, du_dir, dproj, dres, w_mix4, tm, comm=()):
    T = dres.shape[0]
    nt = T // tm

    def body(dyc_ref, wco_ref, pc_ref, halo_ref, z_ref, cw_ref, dsu_ref, dud_ref, dpin_ref, dres_ref, w_ref,
             dp_ref, dx_ref, dcw_ref, dcb_ref, dzbuf, qbuf):
        i = pl.program_id(0)
        ii = nt - 1 - i

        @pl.when(i == 0)
        def _():
            dzbuf[pl.ds(tm, 8), :] = jnp.zeros((8, CONV), F32)

        dyc = dyc_ref[...]
        dyin = (_nt(dyc[:, 0:256], wco_ref[0]) + _nt(dyc[:, 256:512], wco_ref[1])
                + _nt(dyc[:, 512:768], wco_ref[2]) + _nt(dyc[:, 768:1024], wco_ref[3]))
        cbv = pc_ref[:, 0:CONV].astype(F32)
        ccv = pc_ref[:, CONV:2 * CONV].astype(F32)
        chv = pc_ref[:, 2 * CONV:3 * CONV].astype(F32)
        dcbv = dyin * z_ref[...].astype(F32)
        dz = dyin * cbv
        dzbuf[pl.ds(0, tm), :] = dz
        cw = cw_ref[...]
        dq = cw[2:3] * dz + cw[1:2] * dzbuf[pl.ds(1, tm), :] + cw[0:1] * dzbuf[pl.ds(2, tm), :]
        dzbuf[pl.ds(tm, 8), :] = dz[0:8]
        q = ccv * chv
        hq = halo_ref[:, CONV:2 * CONV].astype(F32) * halo_ref[:, 2 * CONV:3 * CONV].astype(F32)
        qbuf[pl.ds(0, 8), :] = jnp.where(ii > 0, hq, jnp.zeros_like(hq))
        qbuf[pl.ds(8, tm), :] = q
        pw = jnp.concatenate([_rowsum(dz * qbuf[pl.ds(6, tm), :]), _rowsum(dz * qbuf[pl.ds(7, tm), :]),
                              _rowsum(dz * q), jnp.zeros((5, CONV), F32)], axis=0)
        pbias = _rowsum(dz)

        @pl.when(i == 0)
        def _():
            dcw_ref[...] = pw
            dcb_ref[...] = pbias

        @pl.when(i > 0)
        def _():
            dcw_ref[...] += pw
            dcb_ref[...] += pbias

        dp0 = jnp.concatenate([dcbv, dq * chv], axis=1).astype(BF16)
        dp1 = jnp.concatenate([dq * ccv, dsu_ref[...] + dud_ref[...]], axis=1).astype(BF16)
        dp_ref[:, 0:D] = dp0
        dp_ref[:, D:2 * D] = dp1
        dx_ref[...] = (dres_ref[...] + _nt(dp0, w_ref[0]) + _nt(dp1, w_ref[1])
                       + _nt(dpin_ref[:, 0:D], w_ref[2]) + _nt(dpin_ref[:, D:2 * D], w_ref[3]))

    def tok(n):
        return pl.BlockSpec((tm, n), lambda i: (nt - 1 - i, 0))

    def full(shape):
        return pl.BlockSpec(shape, lambda i: (0,) * len(shape))

    halo = pl.BlockSpec((8, 3 * CONV), lambda i: (jnp.maximum((nt - 1 - i) * (tm // 8) - 1, 0), 0))
    return _pcall(
        body, name="mix_bwd_a", grid=(nt,),
        in_specs=[tok(D), full((4, CONV, 256)), tok(3 * CONV), halo, tok(CONV), full((3, CONV)),
                  tok(SSM), tok(SSM), pl.BlockSpec((tm, 2 * D), lambda i: (nt - 1 - i, 1)), tok(D),
                  full((4, D, D))],
        out_specs=[pl.BlockSpec((tm, 2 * D), lambda i: (nt - 1 - i, 0)), tok(D), full((8, CONV)), full((1, CONV))],
        out_shape=[jax.ShapeDtypeStruct((T, 4 * D), BF16), jax.ShapeDtypeStruct((T, D), F32),
                   jax.ShapeDtypeStruct((8, CONV), F32), jax.ShapeDtypeStruct((1, CONV), F32)],
        scratch=[pltpu.VMEM((tm + 8, CONV), F32), pltpu.VMEM((tm + 8, CONV), F32)],
        aliases={8: 0}, vmem_mb=56, comm=comm,
        operands=(dyc_b, w_co4, pc, pc, z_b, conv_w, dsu_ssm, du_dir, dproj, dres, w_mix4))


def _zoh(lam_re, lam_im, log_step, b_re, b_im):
    dt = jnp.exp(log_step)[:, None]
    mag = jnp.exp(lam_re * dt)
    abr, abi = mag * jnp.cos(lam_im * dt), mag * jnp.sin(lam_im * dt)
    nr, ni = abr - 1.0, abi
    den = lam_re * lam_re + lam_im * lam_im
    cr = (nr * lam_re + ni * lam_im) / den
    ci = (ni * lam_re - nr * lam_im) / den
    bbr = cr[..., None] * b_re - ci[..., None] * b_im
    bbi = cr[..., None] * b_im + ci[..., None] * b_re
    return abr, abi, bbr, bbi


_WB_MASK = (np.arange(8)[None, :, None]
            == SCAN_GR * np.arange(SCAN_PER)[:, None, None] + np.arange(SCAN_GR)[None, None, :]).astype(np.float32)
_EYE8 = np.eye(8, dtype=np.float32)


def _wb_blocks(bb):
    bt = bb.transpose(0, 2, 1).reshape(4, 1, 8, 16, 1, STATE)
    full = bt * _WB_MASK[None, :, :, None, :, None]
    return full.reshape(LANES // SCAN_W, 128, SCAN_W).astype(BF16)


def _wc_blocks(cc):
    ct = cc.transpose(0, 2, 1).reshape(4, 8, STATE, 1, 16)
    full = ct * _EYE8[None, :, None, :, None]
    return full.reshape(4, 512, 128).astype(BF16)


def _wb_diag(dwb):
    d6 = dwb.reshape(4, SCAN_PER, 8, 16, SCAN_GR, STATE) * _WB_MASK[None, :, :, None, :, None]
    return d6.sum(axis=(1, 4)).reshape(GROUPS, 16, STATE).transpose(0, 2, 1)


def _wc_diag(dwc):
    mask = _WB_MASK.transpose(0, 2, 1)
    d6 = dwc.reshape(4, SCAN_PER, SCAN_GR, STATE, 8, 16) * mask[None, :, :, None, :, None]
    return d6.sum(axis=4).reshape(GROUPS, STATE, 16).transpose(0, 2, 1)


def _where():
    x, y, c = lax.axis_index("x"), lax.axis_index("y"), lax.axis_index("c")
    return x, y, c, 2 * x + y


def _chip_dev(k, c):
    return (k // 2, k % 2, c)


def _slot_cast(meidx, w, dtype, name, token=()):
    R, C = w.shape
    tr = _row_tile(R)

    def body(m_ref, w_ref, *rest):
        rest[-1][...] = w_ref[...].astype(dtype)

    gs = pltpu.PrefetchScalarGridSpec(
        num_scalar_prefetch=1, grid=(R // tr,),
        in_specs=[pl.BlockSpec((tr, C), lambda i, m: (i, 0))] + [pl.BlockSpec((8, 128), lambda i, m: (0, 0))] * len(token),
        out_specs=pl.BlockSpec((None, tr, C), lambda i, m: (m[0], i, 0)))
    return pl.pallas_call(
        body, name=name, grid_spec=gs, out_shape=_hbm_out(jax.ShapeDtypeStruct((4, R, C), dtype)),
        compiler_params=_cp(32, 1),
    )(meidx, *_hbm(w), *token)


def _gather_ici_payload(bufs):
    def copies(ins, lnd, ss, rs):
        x, y, c, me = _where()
        cps = []
        for w, b in enumerate(bufs):
            h = b.shape[1] // 2
            mine = lnd[w].at[me, pl.ds(c * h, h)]
            for s in range(3):
                k = (me + 1 + s) % 4
                cps.append(pltpu.make_async_remote_copy(
                    src_ref=mine, dst_ref=mine, send_sem=ss.at[3 * w + s], recv_sem=rs.at[3 * w + s],
                    device_id=_chip_dev(k, c), device_id_type=MESH))
        return cps

    p = _sym_payload([], [jax.ShapeDtypeStruct(b.shape, b.dtype) for b in bufs], copies, 3 * len(bufs))
    p.lands = list(bufs)
    return p


def _gather_pass_payload(bufs):
    def copies(ins, outs, ss, rs):
        x, y, c, me = _where()
        cps = []
        for w, b in enumerate(bufs):
            h = b.shape[1] // 2
            for s in range(3):
                j = (me + 1 + s) % 4
                cps.append(pltpu.make_async_remote_copy(
                    src_ref=ins[w].at[j, pl.ds(c * h, h)], dst_ref=outs[w].at[j, pl.ds(c * h, h)],
                    send_sem=ss.at[3 * w + s], recv_sem=rs.at[3 * w + s], device_id=(x, y, 1 - c),
                    device_id_type=MESH))
        return cps

    p = _sym_payload(bufs, [jax.ShapeDtypeStruct(b.shape, b.dtype) for b in bufs], copies, 3 * len(bufs))
    p.aliases = {w: w for w in range(len(bufs))}
    return p


def _gather_payload(bufs):
    n = len(bufs)

    def half(ref, w, k, cc):
        h = bufs[w].shape[1] // 2
        return ref.at[k, pl.ds(cc * h, h)]

    def ici(ins, outs, sems, w, s):
        x, y, c, me = _where()
        k = (me + 1 + s) % 4
        return pltpu.make_async_remote_copy(
            src_ref=half(ins[w], w, me, c), dst_ref=half(outs[w], w, me, c), send_sem=sems[0].at[3 * w + s],
            recv_sem=sems[1].at[3 * w + s], device_id=_chip_dev(k, c), device_id_type=MESH)

    def landed(outs, sems, w, s):
        x, y, c, me = _where()
        j = (me + 3 - s) % 4
        return pltpu.make_async_remote_copy(
            src_ref=half(outs[w], w, j, c), dst_ref=half(outs[w], w, j, c), send_sem=sems[0].at[3 * w + s],
            recv_sem=sems[1].at[3 * w + s], device_id=(x, y, 1 - c), device_id_type=MESH)

    def passed(outs, sems, w, s, cc):
        x, y, c, me = _where()
        j = (me + 3 - s) % 4
        return pltpu.make_async_remote_copy(
            src_ref=half(outs[w], w, j, cc), dst_ref=half(outs[w], w, j, cc), send_sem=sems[2].at[3 * w + s],
            recv_sem=sems[3].at[3 * w + s], device_id=(x, y, 1 - c), device_id_type=MESH)

    pairs = [(w, s) for w in range(n) for s in range(3)]

    def start(ins, outs, sems):
        for w, s in pairs:
            ici(ins, outs, sems, w, s).start()

    def finish(ins, outs, sems):
        _, _, c, _ = _where()
        for w, s in pairs:
            landed(outs, sems, w, s).wait_recv()
            passed(outs, sems, w, s, c).start()
        for w, s in pairs:
            passed(outs, sems, w, s, 1 - c).wait_recv()
        for w, s in pairs:
            ici(ins, outs, sems, w, s).wait_send()
            passed(outs, sems, w, s, c).wait_send()

    return _Payload(bufs, [jax.ShapeDtypeStruct(b.shape, b.dtype) for b in bufs], {w: w for w in range(n)},
                    [pltpu.SemaphoreType.DMA((3 * n,))] * 4, start, finish)


def _sym_payload(operands, outs, copies, n_copies):
    def start(ins, outs_, sems):
        for cp in copies(ins, outs_, sems[0], sems[1]):
            cp.start()

    def finish(ins, outs_, sems):
        for cp in copies(ins, outs_, sems[0], sems[1]):
            cp.wait()

    p = _Payload(operands, outs, {}, [pltpu.SemaphoreType.DMA((n_copies,))] * 2, start, finish)
    p.copies, p.n_copies = copies, n_copies
    return p


def _swap_payload(g4s):
    def copies(ins, outs, ss, rs):
        x, y, c, me = _where()
        cps = []
        for w, g in enumerate(g4s):
            h = g.shape[1] // 2
            cps.append(pltpu.make_async_remote_copy(
                src_ref=ins[w].at[:, pl.ds((1 - c) * h, h)], dst_ref=outs[w], send_sem=ss.at[w],
                recv_sem=rs.at[w], device_id=(x, y, 1 - c), device_id_type=MESH))
        return cps

    outs = [jax.ShapeDtypeStruct((4, g.shape[1] // 2, g.shape[2]), g.dtype) for g in g4s]
    return _sym_payload(g4s, outs, copies, len(g4s))


def _exchange_payload(pbs):
    def copies(ins, outs, ss, rs):
        x, y, c, me = _where()
        cps = []
        for w in range(len(pbs)):
            for s in range(3):
                k = (me + 1 + s) % 4
                cps.append(pltpu.make_async_remote_copy(
                    src_ref=ins[w].at[k], dst_ref=outs[w].at[2 - s], send_sem=ss.at[3 * w + s],
                    recv_sem=rs.at[3 * w + s], device_id=_chip_dev(k, c), device_id_type=MESH))
        return cps

    outs = [jax.ShapeDtypeStruct((3,) + p.shape[1:], p.dtype) for p in pbs]
    return _sym_payload(pbs, outs, copies, 3 * len(pbs))


HBM_REF = pl.BlockSpec(memory_space=pltpu.HBM)
SEM_REF = pl.BlockSpec(memory_space=pltpu.SEMAPHORE)
DATAFLOW = pltpu.SideEffectType.DATAFLOW_SIDE_EFFECTING


class _SemList:
    def __init__(self, refs):
        self.refs = refs

    @property
    def at(self):
        return self.refs


def _split_start(p, name):
    n_in, n_out, nc = len(p.operands), len(p.outs), p.n_copies
    lands = getattr(p, "lands", None) or [lax.empty(s.shape, s.dtype) for s in p.outs]

    def body(*refs):
        ins, lnd = refs[:n_in], refs[n_in:n_in + n_out]
        sems = refs[n_in + n_out:n_in + n_out + 2 * nc]
        for cp in p.copies(ins, lnd, _SemList(sems[:nc]), _SemList(sems[nc:])):
            cp.start()
        refs[-1][...] = jnp.zeros((8, 128), F32)

    res = pl.pallas_call(
        body, name=name,
        in_specs=[HBM_REF] * (n_in + n_out),
        out_specs=[SEM_REF] * (2 * nc) + [HBM_REF] * (n_in + n_out) + [VMEM_FULL],
        out_shape=([pltpu.SemaphoreType.DMA(())] * (2 * nc) + _hbm_out(p.operands) + _hbm_out(lands)
                   + [jax.ShapeDtypeStruct((8, 128), F32)]),
        input_output_aliases={i: 2 * nc + i for i in range(n_in + n_out)},
        compiler_params=pltpu.CompilerParams(has_side_effects=DATAFLOW),
    )(*_hbm(*p.operands, *lands))
    k = 2 * nc
    return list(res[:k]), list(res[k:k + n_in]), list(res[k + n_in:k + n_in + n_out]), res[-1]


def _split_wait(p, handle, after, name):
    sems, srcs, lands, _ = handle
    n_in, n_out, nc = len(srcs), len(lands), p.n_copies

    def body(*refs):
        ins, lnd = refs[:n_in], refs[n_in:n_in + n_out]
        sm = refs[n_in + n_out:n_in + n_out + 2 * nc]
        for cp in p.copies(ins, lnd, _SemList(sm[:nc]), _SemList(sm[nc:])):
            cp.wait_send()
            cp.wait_recv()

    res = pl.pallas_call(
        body, name=name,
        in_specs=[HBM_REF] * (n_in + n_out) + [SEM_REF] * (2 * nc) + [ANY] * len(after),
        out_specs=[HBM_REF] * (n_in + n_out), out_shape=_hbm_out(srcs) + _hbm_out(lands),
        input_output_aliases={i: i for i in range(n_in + n_out)},
        compiler_params=pltpu.CompilerParams(has_side_effects=DATAFLOW),
    )(*srcs, *lands, *sems, *after)
    return list(res[:n_in]), list(res[n_in:])


def _join_payload(halves):
    def copies(ins, outs, ss, rs):
        x, y, c, me = _where()
        return [pltpu.make_async_remote_copy(
            src_ref=ins[w], dst_ref=outs[w], send_sem=ss.at[w], recv_sem=rs.at[w],
            device_id=(x, y, 1 - c), device_id_type=MESH) for w in range(len(halves))]

    outs = [jax.ShapeDtypeStruct(a.shape, a.dtype) for a in halves]
    return _sym_payload(halves, outs, copies, len(halves))


def _allgather_payload(v):
    def copies(ins, outs, ss, rs):
        x, y, c, me = _where()
        lin = 4 * x + 2 * y + c
        cps = []
        for o in range(1, 8):
            t = (lin + o) % 8
            cps.append(pltpu.make_async_remote_copy(
                src_ref=ins[0], dst_ref=outs[0].at[lin], send_sem=ss.at[o - 1], recv_sem=rs.at[o - 1],
                device_id=(t // 4, (t // 2) % 2, t % 2), device_id_type=MESH))
        return cps

    p = _sym_payload([v], [jax.ShapeDtypeStruct((8,) + v.shape, v.dtype)], copies, 7)
    x, y, c, _ = _where()
    p.lands = [lax.dynamic_update_slice(jnp.zeros((8,) + v.shape, v.dtype), v[None], (4 * x + 2 * y + c, 0, 0))]
    return p


def _sum8(buf, token):
    _, P, C = buf.shape

    def body(b_ref, t_ref, o_ref):
        acc = b_ref[0]
        for d in range(1, 8):
            acc = acc + b_ref[d]
        o_ref[...] = acc

    return pl.pallas_call(
        body, name="sum8", in_specs=[VMEM_FULL, VMEM_FULL], out_specs=VMEM_FULL,
        out_shape=jax.ShapeDtypeStruct((P, C), F32),
        compiler_params=pltpu.CompilerParams(vmem_limit_bytes=32 << 20),
    )(buf, token)


def _row_tile(h):
    for t in (256, 176, 128, 64, 32, 16, 8):
        if h % t == 0:
            return t
    raise ValueError(h)


def _pair_sum(cmidx, g4, got, name):
    _, R, C = g4.shape
    h = R // 2
    th = _row_tile(h)

    def body(cm_ref, a_ref, b_ref, o_ref, ob_ref):
        sm = a_ref[...] + b_ref[...]
        ob_ref[...] = sm.astype(BF16)

        @pl.when(pl.program_id(1) == cm_ref[1])
        def _():
            o_ref[...] = sm

    blk = pl.BlockSpec((None, th, C), lambda i, k, cm: (k, i, 0))
    gs = pltpu.PrefetchScalarGridSpec(
        num_scalar_prefetch=1, grid=(h // th, 4),
        in_specs=[pl.BlockSpec((None, None, th, C), lambda i, k, cm: (k, cm[0], i, 0)), blk],
        out_specs=[pl.BlockSpec((th, C), lambda i, k, cm: (i, 0)), blk])
    return pl.pallas_call(
        body, name=name, grid_spec=gs,
        out_shape=_hbm_out([jax.ShapeDtypeStruct((h, C), F32), jax.ShapeDtypeStruct((4, h, C), BF16)]),
        compiler_params=_cp(32, 2),
    )(cmidx, *_hbm(g4.reshape(4, 2, h, C), got))


def _chip_sum(own, got, name):
    h, C = own.shape
    th = _row_tile(h)

    def body(a_ref, b_ref, o_ref):
        o_ref[...] = ((a_ref[...] + b_ref[0].astype(F32)) + b_ref[1].astype(F32)) + b_ref[2].astype(F32)

    return pl.pallas_call(
        body, name=name, grid=(h // th,),
        in_specs=[pl.BlockSpec((th, C), lambda i: (i, 0)), pl.BlockSpec((3, th, C), lambda i: (0, i, 0))],
        out_specs=pl.BlockSpec((th, C), lambda i: (i, 0)),
        out_shape=_hbm_out(jax.ShapeDtypeStruct((h, C), F32)),
        compiler_params=_cp(32, 1),
    )(*_hbm(own, got))


def _adamw_math(w, g, m, v):
    m2 = B1 * m + (1.0 - B1) * g
    v2 = B2 * v + (1.0 - B2) * (g * g)
    m_hat = m2 / (1.0 - B1 ** STEP)
    v_hat = v2 / (1.0 - B2 ** STEP)
    delta = -LR * (m_hat / (jnp.sqrt(v_hat) + EPS) + WD * w)
    return delta, m2, v2


def _adamw_pair(cidx, w, mine, theirs, m, v, token, name):
    R, C = w.shape
    h = R // 2
    tr = _row_tile(h)
    nh = h // tr

    def body(c_ref, w_ref, a_ref, b_ref, m_ref, v_ref, t_ref, g_ref, d_ref, mo_ref, vo_ref):
        own = (pl.program_id(0) // nh) == c_ref[0]
        g = jnp.where(own, a_ref[...], b_ref[...])
        d, m2, v2 = _adamw_math(w_ref[...], g, m_ref[...], v_ref[...])
        g_ref[...] = g
        d_ref[...] = d
        mo_ref[...] = m2
        vo_ref[...] = v2

    blk = pl.BlockSpec((tr, C), lambda i, c: (i, 0))
    mine_blk = pl.BlockSpec((tr, C), lambda i, c: (jnp.clip(i - c[0] * nh, 0, nh - 1), 0))
    theirs_blk = pl.BlockSpec((tr, C), lambda i, c: (jnp.clip(i - (1 - c[0]) * nh, 0, nh - 1), 0))
    gs = pltpu.PrefetchScalarGridSpec(
        num_scalar_prefetch=1, grid=(R // tr,),
        in_specs=[blk, mine_blk, theirs_blk, blk, blk, pl.BlockSpec((8, 128), lambda i, c: (0, 0))],
        out_specs=[blk] * 4)
    return pl.pallas_call(
        body, name=name, grid_spec=gs, out_shape=_hbm_out([jax.ShapeDtypeStruct((R, C), F32)] * 4),
        compiler_params=_cp(32, 1),
    )(cidx, *_hbm(w, mine, theirs, m, v), token)


def _adamw(w, g, m, v, name):
    R, C = w.shape
    tr = _row_tile(R)

    def body(w_ref, g_ref, m_ref, v_ref, d_ref, mo_ref, vo_ref):
        d, m2, v2 = _adamw_math(w_ref[...], g_ref[...], m_ref[...], v_ref[...])
        d_ref[...] = d
        mo_ref[...] = m2
        vo_ref[...] = v2

    blk = pl.BlockSpec((tr, C), lambda i: (i, 0))
    return pl.pallas_call(
        body, name=name, grid=(R // tr,), in_specs=[blk] * 4, out_specs=[blk] * 3,
        out_shape=_hbm_out([jax.ShapeDtypeStruct((R, C), F32)] * 3),
        compiler_params=_cp(32, 1),
    )(*_hbm(w, g, m, v))


def _pack(arrs):
    flat = jnp.concatenate([a.reshape(-1).astype(F32) for a in arrs])
    rows = -(-flat.shape[0] // 1024)
    rows = -(-rows // 8) * 8
    return jnp.pad(flat, (0, rows * 1024 - flat.shape[0])).reshape(rows, 1024)


def _unpack(packed, shapes):
    flat = packed.reshape(-1)
    out, off = [], 0
    for s in shapes:
        n = math.prod(s)
        out.append(flat[off:off + n].reshape(s))
        off += n
    return out


BIG = ["ffn1_w_in", "ffn1_w_out", "mix_w_in", "conv_w_out", "ssm_w_glu", "mix_w_out",
       "ffn2_w_in", "ffn2_w_out", "ple_w_in", "ple_w_gate"]
SMALL = ["ln1_g", "ln1_b", "conv_w", "conv_b", "ssm_lam_re", "ssm_lam_im", "ssm_log_step", "ssm_b_re", "ssm_b_im",
         "ssm_c_re", "ssm_c_im", "ssm_d", "ln2_g", "ln2_b", "ln3_g", "ln3_b", "ln4_g", "ln4_b"]
WEIGHTS = ["ffn1_w_in", "ffn1_w_out", "ln1_g", "ln1_b", "mix_w_in", "conv_w", "conv_b", "conv_w_out",
           "ssm_lam_re", "ssm_lam_im", "ssm_log_step", "ssm_b_re", "ssm_b_im", "ssm_c_re", "ssm_c_im", "ssm_d",
           "ssm_w_glu", "mix_w_out", "ln2_g", "ln2_b", "ffn2_w_in", "ffn2_w_out", "ln3_g", "ln3_b",
           "ple_w_in", "ple_w_gate", "ln4_g", "ln4_b"]


class _NoComm:
    def __init__(self, W):
        self.W, self.G, self.raw = dict(W), {}, None

    def carry(self, name):
        return ()

    def landed(self, name, got):
        pass

    def grad(self, name, g4):
        self.G[name] = g4

    def small(self, raw):
        self.raw = raw


def _s5_operands(sp):
    abr, abi, bbr, bbi = _zoh(sp["ssm_lam_re"], sp["ssm_lam_im"], sp["ssm_log_step"], sp["ssm_b_re"], sp["ssm_b_im"])
    return (_wb_blocks(bbr), _wb_blocks(bbi), _wc_blocks(sp["ssm_c_re"]), _wc_blocks(-sp["ssm_c_im"]),
            abr.reshape(1, LANES), abi.reshape(1, LANES), sp["ssm_d"].reshape(1, SSM))


def _local_step(x, p, target, sp, sched, tm_ffn, tm_mix, ops=None):
    W = sched.W
    wb_re, wb_im, wc_re4, wc_im4, a_re, a_im, dvec = ops if ops is not None else _s5_operands(sp)

    def run(fn, name, *args, **kw):
        outs, got = fn(*args, comm=sched.carry(name), **kw)
        sched.landed(name, got)
        return outs

    def dw(name, wname, a, b, tk, tn, shape4, shard_cols=None, interleaved=False):
        out, got = _mm_tn(a, b, tk, tn, name, shard_cols=shard_cols, interleaved=interleaved,
                          comm=sched.carry(name))
        sched.landed(name, got)
        sched.grad(wname, out.reshape(shape4))

    xb = x.astype(BF16)
    h1, r1, x1, x1b = run(_ffn_fwd, "ffn1_fwd", x, xb, W["ffn1_w_in"], W["ffn1_w_out"].reshape(2, FFH, D),
                          sp["ln1_g"], sp["ln1_b"], tm_ffn, "ffn1_fwd")
    conv_w = W["conv_w"][:, 0:3, :].transpose(1, 0, 2).reshape(3, CONV)
    pc, z_b, yin_b, su, su_b, g_conv, g_ssm, y_conv = run(
        _mix_fwd_a, "mix_fwd_a", x1b, W["mix_w_in"], conv_w, sp["conv_b"], W["conv_w_out"], tm_mix)
    st_re, st_im = run(_s5_scan_fwd, "s5_scan_fwd", su_b, wb_re, wb_im, a_re, a_im)
    w_mo = W["mix_w_out"].reshape(D, D)
    s, sg_b, ga, gb, merged_b, r2, x2, x2b = run(
        _mix_fwd_b, "mix_fwd_b", st_re, st_im, wc_re4, wc_im4, su, dvec, W["ssm_w_glu"], g_conv, g_ssm, y_conv,
        w_mo, x1, sp["ln2_g"], sp["ln2_b"], tm_mix)
    w2o2 = W["ffn2_w_out"].reshape(2, FFH, D)
    h2, r3, x3, x3b = run(_ffn_fwd, "ffn2_fwd", x2, x2b, W["ffn2_w_in"], w2o2, sp["ln3_g"], sp["ln3_b"], tm_ffn,
                          "ffn2_fwd")
    loss_part, dx3, p_b, dpw_b, dgt_b, dg4, db4 = _ple_loss(
        x3, x3b, p, W["ple_w_in"], W["ple_w_gate"].reshape(D, D), sp["ln4_g"], sp["ln4_b"], target, tm_mix)

    dw("dw_ple_gate", "ple_w_gate", x3b, dgt_b, 512, 1024, (4, 256, D))
    dw("dw_ple_in", "ple_w_in", p_b, dpw_b, 256, 256, (4, 256, 256), shard_cols=256)
    dx2, dh2, a2_b, df2_b, dg3, db3 = run(_ffn_bwd, "ffn2_bwd", dx3, r3, sp["ln3_g"], h2, W["ffn2_w_in"], w2o2,
                                          tm_mix, "ffn2_bwd")
    dw("dw_ffn2_in", "ffn2_w_in", x2b, dh2, 512, FFH, (4, D, FFH), shard_cols=FFH, interleaved=True)
    dw("dw_ffn2_out", "ffn2_w_out", a2_b, df2_b, FFH, 1024, (4, FF // 4, D))
    (dres, dmix_b, dgl_b, ds_b, du_dir, gs_re, gs_im, dyc_b, dproj, dg2, db2, dd) = run(
        _mix_bwd_b, "mix_bwd_b", dx2, r2, sp["ln2_g"], w_mo, g_conv, g_ssm, y_conv, ga, gb, s, su, dvec,
        W["ssm_w_glu"], wc_re4, wc_im4, tm_mix)
    dw("dw_mix_out", "mix_w_out", merged_b, dmix_b, 512, 1024, (4, 256, D))
    dw("dw_glu", "ssm_w_glu", sg_b, dgl_b, 512, 512, (4, SSM, 512), shard_cols=512)
    dsu_ssm, dwb_re, dwb_im, dwc_re, dwc_im, da_re, da_im = run(
        _s5_scan_bwd, "s5_scan_bwd", gs_re, gs_im, st_re, st_im, su_b, ds_b, wb_re, wb_im, a_re, a_im)
    dw("dw_conv_out", "conv_w_out", yin_b, dyc_b, 512, 256, (4, CONV, 256), shard_cols=256)
    dproj, dx1, dcw8, dcb = run(_mix_bwd_a, "mix_bwd_a", dyc_b, W["conv_w_out"], pc, z_b, conv_w, dsu_ssm,
                                du_dir, dproj, dres, W["mix_w_in"], tm_mix)
    dw("dw_mix_in", "mix_w_in", x1b, dproj, 512, 1024, (4, D, D), shard_cols=1024)
    dx0, dh1, a1_b, df1_b, dg1, db1 = run(_ffn_bwd, "ffn1_bwd", dx1, r1, sp["ln1_g"], h1, W["ffn1_w_in"],
                                          W["ffn1_w_out"].reshape(2, FFH, D), tm_mix, "ffn1_bwd")
    sched.small(dict(
        ln1_g=dg1, ln1_b=db1, ln2_g=dg2, ln2_b=db2, ln3_g=dg3, ln3_b=db3, ln4_g=dg4, ln4_b=db4,
        conv_w=dcw8[0:3], conv_b=dcb,
        a_re=da_re.reshape(GROUPS, STATE), a_im=da_im.reshape(GROUPS, STATE),
        bb_re=_wb_diag(dwb_re), bb_im=_wb_diag(dwb_im),
        ssm_c_re=_wc_diag(dwc_re), ssm_c_im=-_wc_diag(dwc_im), ssm_d=dd.reshape(GROUPS, 16),
        loss=loss_part[0:1, 0]))
    dw("dw_ffn1_in", "ffn1_w_in", xb, dh1, 512, FFH, (4, D, FFH), shard_cols=FFH, interleaved=True)
    dw("dw_ffn1_out", "ffn1_w_out", a1_b, df1_b, FFH, 1024, (4, FF // 4, D))
    return loss_part[0, 0], dx0


RAW_ORDER = ["ln1_g", "ln1_b", "ln2_g", "ln2_b", "ln3_g", "ln3_b", "ln4_g", "ln4_b", "conv_w", "conv_b",
             "a_re", "a_im", "bb_re", "bb_im", "ssm_c_re", "ssm_c_im", "ssm_d", "loss"]

GATHER_FIRST = ["ffn1_w_in", "ffn1_w_out"]
GATHER_AT = {"ffn1_fwd": ["mix_w_in", "conv_w_out", "conv_w"], "mix_fwd_a": ["ssm_w_glu", "mix_w_out"],
             "s5_scan_fwd": ["ffn2_w_in"], "mix_fwd_b": ["ffn2_w_out"], "ffn2_fwd": ["ple_w_in", "ple_w_gate"]}
REDUCE_GROUP = {"ple": ["ple_w_gate", "ple_w_in"], "ffn2": ["ffn2_w_in", "ffn2_w_out"],
                "mix": ["mix_w_out", "ssm_w_glu", "conv_w_out", "mix_w_in"], "ffn1": ["ffn1_w_in", "ffn1_w_out"]}
REDUCE_AT = {"ffn2_bwd": [("swap", "ple")], "dw_ffn2_in": [("exchange", "ple")],
             "mix_bwd_b": [("swap", "ffn2"), ("join", "ple")], "s5_scan_bwd": [("exchange", "ffn2")],
             "mix_bwd_a": [("join", "ffn2")], "ffn1_bwd": [("swap", "mix")]}
BEGIN_AT = {"dw_ffn1_in": [("small", None), ("exchange", "mix")]}
LAST_GROUP = "ffn1"


class _Sched:
    def __init__(self, cmidx):
        self.bufs, self.cmidx = {}, cmidx
        self.W, self.G, self.raw, self.small_buf = {}, {}, None, None
        self.got1, self.p32, self.pbf, self.got2, self.half, self.theirs = {}, {}, {}, {}, {}, {}
        self._open, self._split = [], {}

    def first_begin(self, bufs):
        self.bufs.update(bufs)
        p = _gather_ici_payload([bufs[n] for n in GATHER_FIRST])
        self._first = (p, _split_start(p, "gather_first_start"))
        return self._first[1][3]

    def first_end(self, bufs, after):
        self.bufs.update(bufs)
        p, handle = self._first
        _, landed = _split_wait(p, handle, after, "gather_first_wait")
        (outs,) = _comm_call("gather_first_pass", [_gather_pass_payload(landed)])
        self.W.update(zip(GATHER_FIRST, outs))

    def _payload(self, stage, key):
        if stage == "gather":
            return _gather_payload([self.bufs[n] for n in key])
        if stage == "small":
            return _allgather_payload(_pack([self.raw[k] for k in RAW_ORDER]))
        names = REDUCE_GROUP[key]
        if stage == "swap":
            return _swap_payload([self.G[n] for n in names])
        if stage == "exchange":
            for n in names:
                self.p32[n], self.pbf[n] = _pair_sum(self.cmidx, self.G[n], self.got1[n], "pair_sum_" + n)
            return _exchange_payload([self.pbf[n] for n in names])
        for n in names:
            self.half[n] = _chip_sum(self.p32[n], self.got2[n], "chip_sum_" + n)
        return _join_payload([self.half[n] for n in names])

    def _store(self, stages, got):
        for (stage, key), outs in zip(stages, got):
            if stage == "gather":
                self.W.update(zip(key, outs))
            elif stage == "small":
                self.small_buf = outs[0]
            else:
                {"swap": self.got1, "exchange": self.got2, "join": self.theirs}[stage].update(
                    zip(REDUCE_GROUP[key], outs))

    def _standalone(self, name, stages):
        self._store(stages, _comm_call(name, [self._payload(s, k) for s, k in stages]))

    def carry(self, name):
        tokens = [self._begin(stage, key) for stage, key in BEGIN_AT.get(name, [])]
        self._open = [("gather", GATHER_AT[name])] if name in GATHER_AT else []
        self._open += REDUCE_AT.get(name, [])
        comm = [self._payload(s, k) for s, k in self._open]
        if tokens:
            comm.append(_Payload(tokens, [], {}, [], lambda *a: None, lambda *a: None))
        return tuple(comm)

    def landed(self, name, got):
        self._store(self._open, got)

    def grad(self, name, g4):
        self.G[name] = g4

    def small(self, raw):
        self.raw = raw

    def _begin(self, stage, key):
        p = self._payload(stage, key)
        self._split[stage, key] = (p, _split_start(p, "%s_%s_start" % (stage, key)))
        return self._split[stage, key][1][3]

    def _end(self, stage, key, after):
        p, handle = self._split.pop((stage, key))
        srcs, lands = _split_wait(p, handle, after, "%s_%s_wait" % (stage, key))
        if stage == "swap":
            self.G.update(zip(REDUCE_GROUP[key], srcs))
        self._store([(stage, key)], [lands])

    def tail_begin(self):
        return self._begin("swap", LAST_GROUP)

    def tail_mid(self, after):
        self._end("swap", LAST_GROUP, after)
        token = self._begin("exchange", LAST_GROUP)
        self._end("small", None, [token])
        self._end("exchange", "mix", [token])
        self._standalone("reduce_tail_join_mix", [("join", "mix")])
        return token

    def tail_end(self, after):
        self._end("exchange", LAST_GROUP, after)
        self._standalone("reduce_tail_join", [("join", LAST_GROUP)])


def _small_grads(raw_sum, sp):
    _, vjp = jax.vjp(_zoh, sp["ssm_lam_re"], sp["ssm_lam_im"], sp["ssm_log_step"], sp["ssm_b_re"], sp["ssm_b_im"])
    d_lre, d_lim, d_ls, d_bre, d_bim = vjp((raw_sum["a_re"], raw_sum["a_im"], raw_sum["bb_re"], raw_sum["bb_im"]))
    g = {k: raw_sum[k] for k in ("ln1_g", "ln1_b", "ln2_g", "ln2_b", "ln3_g", "ln3_b", "ln4_g", "ln4_b",
                                 "conv_w", "conv_b", "ssm_c_re", "ssm_c_im", "ssm_d")}
    g.update(ssm_lam_re=d_lre, ssm_lam_im=d_lim, ssm_log_step=d_ls, ssm_b_re=d_bre, ssm_b_im=d_bim)
    return g


def kernel(x, p, ffn1_w_in, ffn1_w_out, ln1_g, ln1_b, mix_w_in, conv_w, conv_b, conv_w_out, ssm_lam_re, ssm_lam_im, ssm_log_step, ssm_b_re, ssm_b_im, ssm_c_re, ssm_c_im, ssm_d, ssm_w_glu, mix_w_out, ln2_g, ln2_b, ffn2_w_in, ffn2_w_out, ln3_g, ln3_b, ple_w_in, ple_w_gate, ln4_g, ln4_b, loss_target, m_ffn1_w_in, m_ffn1_w_out, m_ln1_g, m_ln1_b, m_mix_w_in, m_conv_w, m_conv_b, m_conv_w_out, m_ssm_lam_re, m_ssm_lam_im, m_ssm_log_step, m_ssm_b_re, m_ssm_b_im, m_ssm_c_re, m_ssm_c_im, m_ssm_d, m_ssm_w_glu, m_mix_w_out, m_ln2_g, m_ln2_b, m_ffn2_w_in, m_ffn2_w_out, m_ln3_g, m_ln3_b, m_ple_w_in, m_ple_w_gate, m_ln4_g, m_ln4_b, v_ffn1_w_in, v_ffn1_w_out, v_ln1_g, v_ln1_b, v_mix_w_in, v_conv_w, v_conv_b, v_conv_w_out, v_ssm_lam_re, v_ssm_lam_im, v_ssm_log_step, v_ssm_b_re, v_ssm_b_im, v_ssm_c_re, v_ssm_c_im, v_ssm_d, v_ssm_w_glu, v_mix_w_out, v_ln2_g, v_ln2_b, v_ffn2_w_in, v_ffn2_w_out, v_ln3_g, v_ln3_b, v_ple_w_in, v_ple_w_gate, v_ln4_g, v_ln4_b):
    args = dict(locals())
    w = {n: args[n] for n in WEIGHTS}
    m = {n: args["m_" + n] for n in WEIGHTS}
    v = {n: args["v_" + n] for n in WEIGHTS}
    _, _, c, me = _where()
    cidx = jnp.stack([c, me]).astype(jnp.int32)
    meidx = jnp.reshape(me, (1,)).astype(jnp.int32)

    sched = _Sched(cidx)
    token = sched.first_begin({n: _slot_cast(meidx, w[n][0], BF16, "cast_" + n) for n in GATHER_FIRST})
    rest = {n: _slot_cast(meidx, w[n][0], BF16, "cast_" + n, (token,)) for n in BIG if n not in GATHER_FIRST}
    rest["conv_w"] = _slot_cast(meidx, jnp.pad(conv_w[0], ((0, 13), (0, 0))), F32, "cast_conv_w", (token,))
    sp = {n: (w[n] if w[n].ndim == 2 and n != "ssm_log_step" else w[n][0]) for n in SMALL if n != "conv_w"}
    ops = _s5_operands({**sp, "ssm_lam_re": sp["ssm_lam_re"] + token[0, 0]})
    sched.first_end(rest, list(rest.values()) + list(ops))
    loss_part, dx0 = _local_step(x[0], p[0, 0], loss_target[0], sp, sched, 512, 256, ops)
    out_g, out_d, out_m, out_v = {}, {}, {}, {}

    def big_adamw(names, token):
        for n in names:
            g, dl, mn, vn = _adamw_pair(cidx, w[n][0], sched.half[n], sched.theirs[n], m[n][0], v[n][0], token,
                                        "adamw_" + n)
            out_g[n], out_d[n], out_m[n], out_v[n] = g[None], dl[None], mn[None], vn[None]

    first = REDUCE_GROUP["ple"] + REDUCE_GROUP["ffn2"]
    big_adamw(first, sched.tail_begin())
    token = sched.tail_mid([out_v[n] for n in first])

    raw_shapes = [sched.raw[k].shape for k in RAW_ORDER]
    raw_sum = dict(zip(RAW_ORDER, _unpack(_sum8(sched.small_buf, token), raw_shapes)))
    loss = raw_sum["loss"][0]
    sg = _small_grads(raw_sum, sp)
    sg["conv_w"] = lax.dynamic_slice_in_dim(sg["conv_w"], me * 128, 128, axis=1)
    small_shapes = [w[n].shape for n in SMALL]
    gp = _pack([sg[n] for n in SMALL])
    d_s, m_s, v_s = _adamw(_pack([w[n] for n in SMALL]), gp, _pack([m[n] for n in SMALL]),
                           _pack([v[n] for n in SMALL]), "adamw_small")

    for n, a, b_, c_, d_ in zip(SMALL, _unpack(gp, small_shapes), _unpack(d_s, small_shapes),
                                _unpack(m_s, small_shapes), _unpack(v_s, small_shapes)):
        out_g[n], out_d[n], out_m[n], out_v[n] = a, b_, c_, d_
    big_adamw(REDUCE_GROUP["mix"], token)
    sched.tail_end([d_s] + [out_v[n] for n in REDUCE_GROUP["mix"]])
    big_adamw(REDUCE_GROUP[LAST_GROUP], token)

    return (loss, dx0[None], *[out_g[n] for n in WEIGHTS], *[out_d[n] for n in WEIGHTS],
            *[out_m[n] for n in WEIGHTS], *[out_v[n] for n in WEIGHTS])
```

```python
import functools
import math

import jax
import jax.numpy as jnp
import numpy as np
from jax import lax
from jax.experimental import pallas as pl
from jax.experimental.pallas import tpu as pltpu

F32, BF16 = jnp.float32, jnp.bfloat16
D = 1024
FF = 2816
FFH = FF // 2
CONV = 512
SSM = 512
GROUPS = 32
STATE = 64
LANES = GROUPS * STATE
SCAN_W = 128
SCAN_PER = 512 // SCAN_W
SCAN_GR = SCAN_W // STATE
SCAN_R = 256
ALPHA = 2.0 ** 0.25
LN_EPS = 1e-5
GELU_C = math.sqrt(2.0 / math.pi)
B1, B2, LR, EPS, WD, STEP = 0.9, 0.999, 0.001, 1e-8, 0.01, 10
MESH = pl.DeviceIdType.MESH
ANY = pl.BlockSpec(memory_space=pl.ANY)
VMEM_FULL = pl.BlockSpec(memory_space=pltpu.VMEM)


def _cp(vmem_mb=48, n_axes=1):
    return pltpu.CompilerParams(vmem_limit_bytes=vmem_mb << 20,
                                dimension_semantics=("arbitrary",) * n_axes)


def _hbm(*arrs):
    return [pltpu.with_memory_space_constraint(a, pltpu.HBM) for a in arrs]


def _hbm_out(shapes):
    if isinstance(shapes, (list, tuple)):
        return [pltpu.HBM(s.shape, s.dtype) for s in shapes]
    return pltpu.HBM(shapes.shape, shapes.dtype)


def _nn(a, b):
    return jnp.dot(a, b, preferred_element_type=F32)


def _nt(a, b):
    return lax.dot_general(a, b, (((1,), (1,)), ((), ())), preferred_element_type=F32)


def _tn(a, b):
    return lax.dot_general(a, b, (((0,), (0,)), ((), ())), preferred_element_type=F32)


def _sig(v):
    return jax.nn.sigmoid(v)


def _ln_stats(r):
    mu = jnp.mean(r, axis=-1, keepdims=True)
    xc = r - mu
    var = jnp.mean(xc * xc, axis=-1, keepdims=True)
    rstd = lax.rsqrt(var + LN_EPS)
    return xc * rstd, rstd


def _ln_bwd(dy, r, g):
    xhat, rstd = _ln_stats(r)
    dyg = dy * g
    m1 = jnp.mean(dyg, axis=-1, keepdims=True)
    m2 = jnp.mean(dyg * xhat, axis=-1, keepdims=True)
    return rstd * (dyg - m1 - xhat * m2), xhat


def _rowsum(v):
    return jnp.sum(v, axis=0, keepdims=True)


class _Payload:
    def __init__(self, operands, outs, aliases, sems, start, finish):
        self.operands, self.outs, self.aliases, self.sems = list(operands), list(outs), dict(aliases), list(sems)
        self.start, self.finish = start, finish


def _split(flat, comm, attr):
    out, i = [], 0
    for p in comm:
        n = len(getattr(p, attr))
        out.append(list(flat[i:i + n]))
        i += n
    return out


def _run_comm(comm, which, cin, cout, csem):
    for p, a, b, s in zip(comm, _split(cin, comm, "operands"), _split(cout, comm, "outs"), _split(csem, comm, "sems")):
        getattr(p, which)(a, b, s)


def _pcall(body, *, name, grid, in_specs, out_specs, out_shape, operands, scratch=(), vmem_mb=48, aliases=None,
           comm=()):
    ni, no, ns = len(in_specs), len(out_specs), len(scratch)
    c_ops = [a for p in comm for a in p.operands]
    c_outs = [s for p in comm for s in p.outs]
    c_sems = [s for p in comm for s in p.sems]
    io = dict(aliases or {})
    off_i, off_o = ni, no
    for p in comm:
        for a, b in p.aliases.items():
            io[off_i + a] = off_o + b
        off_i += len(p.operands)
        off_o += len(p.outs)

    def wrapped(*refs):
        ins, cin = refs[:ni], refs[ni:ni + len(c_ops)]
        o0 = ni + len(c_ops)
        outs, cout = refs[o0:o0 + no], refs[o0 + no:o0 + no + len(c_outs)]
        s0 = o0 + no + len(c_outs)
        scr, csem = refs[s0:s0 + ns], refs[s0 + ns:]
        if comm:
            first = functools.reduce(jnp.logical_and, [pl.program_id(a) == 0 for a in range(len(grid))])
            pl.when(first)(lambda: _run_comm(comm, "start", cin, cout, csem))
        body(*ins, *outs, *scr)
        if comm:
            last = functools.reduce(jnp.logical_and, [pl.program_id(a) == grid[a] - 1 for a in range(len(grid))])
            pl.when(last)(lambda: _run_comm(comm, "finish", cin, cout, csem))

    res = pl.pallas_call(
        wrapped, name=name, grid=grid,
        in_specs=list(in_specs) + [ANY] * len(c_ops), out_specs=list(out_specs) + [ANY] * len(c_outs),
        out_shape=_hbm_out(list(out_shape) + c_outs), scratch_shapes=list(scratch) + c_sems,
        input_output_aliases=io,
        compiler_params=pltpu.CompilerParams(vmem_limit_bytes=vmem_mb << 20,
                                             dimension_semantics=("arbitrary",) * len(grid),
                                             has_side_effects=bool(comm)),
    )(*_hbm(*operands, *c_ops))
    return list(res[:no]), _split(res[no:], comm, "outs")


def _comm_call(name, comm):
    c_ops = [a for p in comm for a in p.operands]
    c_outs = [s for p in comm for s in p.outs]
    c_sems = [s for p in comm for s in p.sems]
    io, off_i, off_o = {}, 0, 0
    for p in comm:
        for a, b in p.aliases.items():
            io[off_i + a] = off_o + b
        off_i += len(p.operands)
        off_o += len(p.outs)

    def body(*refs):
        cin, cout = refs[:len(c_ops)], refs[len(c_ops):len(c_ops) + len(c_outs)]
        csem = refs[len(c_ops) + len(c_outs):]
        _run_comm(comm, "start", cin, cout, csem)
        _run_comm(comm, "finish", cin, cout, csem)

    res = pl.pallas_call(
        body, name=name, in_specs=[ANY] * len(c_ops), out_specs=[ANY] * len(c_outs), out_shape=_hbm_out(c_outs),
        scratch_shapes=c_sems, input_output_aliases=io,
        compiler_params=pltpu.CompilerParams(has_side_effects=True),
    )(*_hbm(*c_ops))
    return _split(res, comm, "outs")


def _ffn_fwd(x, xb, w_in4, w_out2, g, b, tm, name, comm=()):
    T = x.shape[0]

    def body(x_ref, xb_ref, wg_ref, wu_ref, wo_ref, g_ref, b_ref, h_ref, r_ref, xo_ref, xob_ref, acc):
        k = pl.program_id(1)
        xv = xb_ref[...]
        gt = _nn(xv, wg_ref[...])
        up = _nn(xv, wu_ref[...])
        a = (gt * _sig(gt) * up).astype(BF16)
        h_ref[:, 0:FFH] = gt.astype(BF16)
        h_ref[:, FFH:2 * FFH] = up.astype(BF16)
        acc[...] = jnp.where(k == 0, 0.0, acc[...]) + _nn(a, wo_ref[...])

        @pl.when(k == 1)
        def _():
            r = ALPHA * x_ref[...] + 0.5 * acc[...]
            xhat, _ = _ln_stats(r)
            xo = xhat * g_ref[...] + b_ref[...]
            r_ref[...] = r
            xo_ref[...] = xo
            xob_ref[...] = xo.astype(BF16)

    tok = pl.BlockSpec((tm, D), lambda i, k: (i, 0))
    vec = pl.BlockSpec((1, D), lambda i, k: (0, 0))
    return _pcall(
        body, name=name, grid=(T // tm, 2),
        in_specs=[tok, tok,
                  pl.BlockSpec((None, D, FFH), lambda i, k: (k, 0, 0)),
                  pl.BlockSpec((None, D, FFH), lambda i, k: (k + 2, 0, 0)),
                  pl.BlockSpec((None, FFH, D), lambda i, k: (k, 0, 0)),
                  vec, vec],
        out_specs=[pl.BlockSpec((tm, FF), lambda i, k: (i, k)), tok, tok, tok],
        out_shape=[jax.ShapeDtypeStruct((T, 2 * FF), BF16), jax.ShapeDtypeStruct((T, D), F32),
                   jax.ShapeDtypeStruct((T, D), F32), jax.ShapeDtypeStruct((T, D), BF16)],
        scratch=[pltpu.VMEM((tm, D), F32)], vmem_mb=56, comm=comm,
        operands=(x, xb, w_in4, w_in4, w_out2, g, b))


def _ffn_bwd(dy, r, g, h, w_in4, w_out2, tm, name, comm=()):
    T = dy.shape[0]

    def body(dy_ref, r_ref, g_ref, h_ref, wg_ref, wu_ref, wo_ref,
             dx_ref, dh_ref, a_ref, df_ref, dg_ref, db_ref, acc, dr_s, dfb_s):
        i, k = pl.program_id(0), pl.program_id(1)

        @pl.when(k == 0)
        def _():
            dyv = dy_ref[...]
            dr, xhat = _ln_bwd(dyv, r_ref[...], g_ref[...])
            pg, pb = _rowsum(dyv * xhat), _rowsum(dyv)

            @pl.when(i == 0)
            def _():
                dg_ref[...] = pg
                db_ref[...] = pb

            @pl.when(i > 0)
            def _():
                dg_ref[...] += pg
                db_ref[...] += pb

            dr_s[...] = dr
            dfb = (0.5 * dr).astype(BF16)
            dfb_s[...] = dfb
            df_ref[...] = dfb

        da = _nt(dfb_s[...], wo_ref[...])
        gt = h_ref[:, 0:FFH].astype(F32)
        up = h_ref[:, FFH:2 * FFH].astype(F32)
        sg = _sig(gt)
        silu = gt * sg
        dgate = (da * up * (sg * (1.0 + gt * (1.0 - sg)))).astype(BF16)
        dup = (da * silu).astype(BF16)
        a_ref[...] = (silu * up).astype(BF16)
        dh_ref[:, 0:FFH] = dgate
        dh_ref[:, FFH:2 * FFH] = dup
        acc[...] = jnp.where(k == 0, 0.0, acc[...]) + _nt(dgate, wg_ref[...]) + _nt(dup, wu_ref[...])

        @pl.when(k == 1)
        def _():
            dx_ref[...] = ALPHA * dr_s[...] + acc[...]

    tok = pl.BlockSpec((tm, D), lambda i, k: (i, 0))
    vec = pl.BlockSpec((1, D), lambda i, k: (0, 0))
    wide = pl.BlockSpec((tm, FF), lambda i, k: (i, k))
    return _pcall(
        body, name=name, grid=(T // tm, 2),
        in_specs=[tok, tok, vec, wide,
                  pl.BlockSpec((None, D, FFH), lambda i, k: (k, 0, 0)),
                  pl.BlockSpec((None, D, FFH), lambda i, k: (k + 2, 0, 0)),
                  pl.BlockSpec((None, FFH, D), lambda i, k: (k, 0, 0))],
        out_specs=[tok, wide, pl.BlockSpec((tm, FFH), lambda i, k: (i, k)), tok, vec, vec],
        out_shape=[jax.ShapeDtypeStruct((T, D), F32), jax.ShapeDtypeStruct((T, 2 * FF), BF16),
                   jax.ShapeDtypeStruct((T, FF), BF16), jax.ShapeDtypeStruct((T, D), BF16),
                   jax.ShapeDtypeStruct((1, D), F32), jax.ShapeDtypeStruct((1, D), F32)],
        scratch=[pltpu.VMEM((tm, D), F32), pltpu.VMEM((tm, D), F32), pltpu.VMEM((tm, D), BF16)],
        vmem_mb=56, comm=comm, operands=(dy, r, g, h, w_in4, w_in4, w_out2))


def _mm_tn(a, b, tk, tn, name, shard_cols=None, interleaved=False, comm=()):
    T, K = a.shape
    N = b.shape[1]

    def body(a_ref, b_ref, o_ref):
        o_ref[...] = _tn(a_ref[...], b_ref[...])

    if shard_cols is None:
        out_shape = jax.ShapeDtypeStruct((K, N), F32)
        out_spec = pl.BlockSpec((tk, tn), lambda ki, nj: (ki, nj))
    else:
        per = shard_cols // tn

        def shard(nj):
            blk = nj // per
            return (blk % 2) * 2 + blk // 2 if interleaved else blk

        out_shape = jax.ShapeDtypeStruct((N // shard_cols, K, shard_cols), F32)
        out_spec = pl.BlockSpec((None, tk, tn), lambda ki, nj: (shard(nj), ki, nj % per))
    (out,), got = _pcall(
        body, name=name, grid=(K // tk, N // tn),
        in_specs=[pl.BlockSpec((T, tk), lambda ki, nj: (0, ki)), pl.BlockSpec((T, tn), lambda ki, nj: (0, nj))],
        out_specs=[out_spec], out_shape=[out_shape], comm=comm, operands=(a, b))
    return out, got


def _mix_fwd_a(xb, w_mix4, conv_w, conv_b, w_co4, tm, comm=()):
    T = xb.shape[0]

    def body(xb_ref, w_ref, cw_ref, cb_ref, wco_ref,
             pc_ref, z_ref, yin_ref, su_ref, sub_ref, gc_ref, gs_ref, yc_ref, qbuf):
        @pl.when(pl.program_id(0) == 0)
        def _():
            qbuf[pl.ds(0, 8), :] = jnp.zeros((8, CONV), F32)

        xv = xb_ref[...]
        p0 = _nn(xv, w_ref[0])
        p1 = _nn(xv, w_ref[1])
        gc_ref[...] = _nn(xv, w_ref[2])
        gs_ref[...] = _nn(xv, w_ref[3])
        cbv, ccv = p0[:, :CONV], p0[:, CONV:]
        chv, suv = p1[:, :CONV], p1[:, CONV:]
        q = ccv * chv
        qbuf[pl.ds(8, tm), :] = q
        cw = cw_ref[...]
        z = (cw[2:3] * q + cw[1:2] * qbuf[pl.ds(7, tm), :] + cw[0:1] * qbuf[pl.ds(6, tm), :]
             + cb_ref[...])
        qbuf[pl.ds(0, 8), :] = q[tm - 8:tm]
        yin = (cbv * z).astype(BF16)
        pc_ref[:, 0:CONV] = cbv.astype(BF16)
        pc_ref[:, CONV:2 * CONV] = ccv.astype(BF16)
        pc_ref[:, 2 * CONV:3 * CONV] = chv.astype(BF16)
        z_ref[...] = z.astype(BF16)
        yin_ref[...] = yin
        su_ref[...] = suv
        sub_ref[...] = suv.astype(BF16)
        for k in range(4):
            yc_ref[:, 256 * k:256 * (k + 1)] = _nn(yin, wco_ref[k])

    def tok(n):
        return pl.BlockSpec((tm, n), lambda i: (i, 0))

    def full(shape):
        return pl.BlockSpec(shape, lambda i: (0,) * len(shape))

    return _pcall(
        body, name="mix_fwd_a", grid=(T // tm,),
        in_specs=[tok(D), full((4, D, D)), full((3, CONV)), full((1, CONV)), full((4, CONV, 256))],
        out_specs=[tok(3 * CONV), tok(CONV), tok(CONV), tok(SSM), tok(SSM), tok(D), tok(D), tok(D)],
        out_shape=[jax.ShapeDtypeStruct((T, 3 * CONV), BF16), jax.ShapeDtypeStruct((T, CONV), BF16),
                   jax.ShapeDtypeStruct((T, CONV), BF16), jax.ShapeDtypeStruct((T, SSM), F32),
                   jax.ShapeDtypeStruct((T, SSM), BF16), jax.ShapeDtypeStruct((T, D), F32),
                   jax.ShapeDtypeStruct((T, D), F32), jax.ShapeDtypeStruct((T, D), F32)],
        scratch=[pltpu.VMEM((tm + 8, CONV), F32)], vmem_mb=56, comm=comm,
        operands=(xb, w_mix4, conv_w, conv_b, w_co4))


def _scan_rows(bre, bim, ar, ai, T, rev, load, out=None):
    R, W, G = SCAN_R, bre.shape[1], T // 8
    if rev:
        ai = -ai

    def cmul(pr, pi, xr, xi):
        return pr * xr - pi * xi, pr * xi + pi * xr

    pw = [(ar, ai)]
    for _ in range(7):
        pw.append(cmul(ar, ai, *pw[-1]))

    def shifted(v, d, axis, n, idx):
        if rev:
            return jnp.where(idx < n - d, pltpu.roll(v, n - d, axis), 0.0)
        return jnp.where(idx >= d, pltpu.roll(v, d, axis), 0.0)

    sub8 = lax.broadcasted_iota(jnp.int32, (8, W), 0)
    inside = {d: (sub8 < 8 - d) if rev else (sub8 >= d) for d in (1, 2, 4)}
    pm = {d: (jnp.where(inside[d], pw[d - 1][0], 0.0)[None], jnp.where(inside[d], pw[d - 1][1], 0.0)[None])
          for d in (1, 2, 4)}

    def step(i, _):
        t0 = pl.multiple_of(i * R, R)
        vr, vi = load(t0)
        vr, vi = vr.reshape(R // 8, 8, W), vi.reshape(R // 8, 8, W)
        for d in (1, 2, 4):
            sh = (8 - d) if rev else d
            dr, di = cmul(pm[d][0], pm[d][1], pltpu.roll(vr, sh, 1), pltpu.roll(vi, sh, 1))
            vr, vi = vr + dr, vi + di
        bre[pl.ds(t0 + 8, R), :] = vr.reshape(R, W)
        bim[pl.ds(t0 + 8, R), :] = vi.reshape(R, W)
        return 0

    lax.fori_loop(0, T // R, step, 0)

    edge = 0 if rev else 7
    cr = bre[pl.ds(8 + edge, G, stride=8), :]
    ci = bim[pl.ds(8 + edge, G, stride=8), :]
    row = lax.broadcasted_iota(jnp.int32, (G, W), 0)
    qr, qi = pw[7]
    d = 1
    while d < G:
        dr, di = cmul(qr, qi, shifted(cr, d, 0, G, row), shifted(ci, d, 0, G, row))
        cr, ci = cr + dr, ci + di
        qr, qi = qr * qr - qi * qi, 2.0 * qr * qi
        d *= 2

    nr, ni = shifted(cr, 1, 0, G, row), shifted(ci, 1, 0, G, row)
    for r in range(8):
        pr, pi = pw[7 - r] if rev else pw[r]
        dr, di = cmul(pr, pi, nr, ni)
        xr = bre[pl.ds(8 + r, G, stride=8), :] + dr
        xi = bim[pl.ds(8 + r, G, stride=8), :] + di
        if out is None:
            bre[pl.ds(8 + r, G, stride=8), :] = xr
            bim[pl.ds(8 + r, G, stride=8), :] = xi
        else:
            out[0][pl.ds(r, G, stride=8), :] = xr
            out[1][pl.ds(r, G, stride=8), :] = xi


def _scan_specs(T):
    W = SCAN_W
    lane = pl.BlockSpec((T, W), lambda j: (0, j))
    col = pl.BlockSpec((T, 128), lambda j: (0, j // SCAN_PER))
    wb = pl.BlockSpec((None, 128, W), lambda j: (j, 0, 0))
    wc = pl.BlockSpec((None, W, 128), lambda j: (j, 0, 0))
    vec = pl.BlockSpec((1, W), lambda j: (0, j))
    return lane, col, wb, wc, vec


def _s5_scan_fwd(su_b, wb_re, wb_im, a_re, a_im, comm=()):
    T = su_b.shape[0]
    W = SCAN_W

    def body(su_ref, wbr_ref, wbi_ref, ar_ref, ai_ref, sr_ref, si_ref, bre, bim):
        def load(t0):
            su = su_ref[pl.ds(t0, SCAN_R), :]
            return _nn(su, wbr_ref[...]), _nn(su, wbi_ref[...])

        _scan_rows(bre, bim, ar_ref[...], ai_ref[...], T, False, load, out=(sr_ref, si_ref))

    lane, col, wb, wc, vec = _scan_specs(T)
    return _pcall(
        body, name="s5_scan_fwd", grid=(LANES // W,),
        in_specs=[col, wb, wb, vec, vec],
        out_specs=[lane, lane],
        out_shape=[jax.ShapeDtypeStruct((T, LANES), F32)] * 2,
        scratch=[pltpu.VMEM((T + 16, W), F32)] * 2, comm=comm,
        operands=(su_b, wb_re, wb_im, a_re, a_im))


def _gelu(s):
    th = jnp.tanh(GELU_C * (s + 0.044715 * s * s * s))
    return 0.5 * s * (1.0 + th), th


def _mix_fwd_b(st_re, st_im, wc_re4, wc_im4, su, dvec, w_glu4, g_conv, g_ssm, y_conv, w_mo, x1, g, b, tm, comm=()):
    T = su.shape[0]

    def body(sr_ref, si_ref, wcr_ref, wci_ref, su_ref, d_ref, wg_ref, gc_ref, gs_ref, yc_ref, wmo_ref,
             x_ref, g_ref, b_ref, s_ref, sgb_ref, ga_ref, gb_ref, mb_ref, r_ref, xo_ref, xob_ref):
        srb = sr_ref[...].astype(BF16)
        sib = si_ref[...].astype(BF16)
        ys = [_nn(srb[:, 512 * J:512 * (J + 1)], wcr_ref[J]) + _nn(sib[:, 512 * J:512 * (J + 1)], wci_ref[J])
              for J in range(4)]
        s = jnp.concatenate(ys, axis=1) + d_ref[...] * su_ref[...]
        sg, _ = _gelu(s)
        sgb = sg.astype(BF16)
        ga = jnp.concatenate([_nn(sgb, wg_ref[0]), _nn(sgb, wg_ref[1])], axis=1)
        gb = jnp.concatenate([_nn(sgb, wg_ref[2]), _nn(sgb, wg_ref[3])], axis=1)
        merged = _sig(gc_ref[...]) * yc_ref[...] + _sig(gs_ref[...]) * (ga * _sig(gb))
        mb = merged.astype(BF16)
        r = ALPHA * x_ref[...] + _nn(mb, wmo_ref[...])
        xhat, _ = _ln_stats(r)
        xo = xhat * g_ref[...] + b_ref[...]
        s_ref[...] = s
        sgb_ref[...] = sgb
        ga_ref[...] = ga
        gb_ref[...] = gb
        mb_ref[...] = mb
        r_ref[...] = r
        xo_ref[...] = xo
        xob_ref[...] = xo.astype(BF16)

    def tok(n):
        return pl.BlockSpec((tm, n), lambda i: (i, 0))

    def full(shape):
        return pl.BlockSpec(shape, lambda i: (0,) * len(shape))

    return _pcall(
        body, name="mix_fwd_b", grid=(T // tm,),
        in_specs=[tok(LANES), tok(LANES), full((4, 512, 128)), full((4, 512, 128)), tok(SSM), full((1, SSM)),
                  full((4, SSM, 512)), tok(D), tok(D), tok(D), full((D, D)), tok(D), full((1, D)), full((1, D))],
        out_specs=[tok(SSM), tok(SSM), tok(D), tok(D), tok(D), tok(D), tok(D), tok(D)],
        out_shape=[jax.ShapeDtypeStruct((T, SSM), F32), jax.ShapeDtypeStruct((T, SSM), BF16),
                   jax.ShapeDtypeStruct((T, D), F32), jax.ShapeDtypeStruct((T, D), F32),
                   jax.ShapeDtypeStruct((T, D), BF16), jax.ShapeDtypeStruct((T, D), F32),
                   jax.ShapeDtypeStruct((T, D), F32), jax.ShapeDtypeStruct((T, D), BF16)],
        vmem_mb=56, comm=comm,
        operands=(st_re, st_im, wc_re4, wc_im4, su, dvec, w_glu4, g_conv, g_ssm, y_conv, w_mo, x1, g, b))


def _ple_loss(x3, x3b, p, w_pi4, w_pg, g, b, target, tm):
    T = x3.shape[0]
    PD = p.shape[1]

    def body(x_ref, xb_ref, p_ref, wpi_ref, wpg_ref, g_ref, b_ref, t_ref,
             loss_ref, dx_ref, pb_ref, dpw_ref, dgt_ref, dg_ref, db_ref):
        i = pl.program_id(0)
        pb = p_ref[...].astype(BF16)
        pw = jnp.concatenate([_nn(pb, wpi_ref[k]) for k in range(4)], axis=1)
        gt = _nn(xb_ref[...], wpg_ref[...])
        sg = _sig(gt)
        r = ALPHA * x_ref[...] + pw * sg
        gv = g_ref[...]
        xhat, rstd = _ln_stats(r)
        err = xhat * gv + b_ref[...] - t_ref[...]
        lpart = jnp.zeros((1, 128), F32) + 0.5 * jnp.sum(jnp.mean(err * err, axis=-1, keepdims=True))
        dy = err * (1.0 / D)
        dyg = dy * gv
        m1 = jnp.mean(dyg, axis=-1, keepdims=True)
        m2 = jnp.mean(dyg * xhat, axis=-1, keepdims=True)
        dr = rstd * (dyg - m1 - xhat * m2)
        pg, pbias = _rowsum(dy * xhat), _rowsum(dy)

        @pl.when(i == 0)
        def _():
            loss_ref[...] = lpart
            dg_ref[...] = pg
            db_ref[...] = pbias

        @pl.when(i > 0)
        def _():
            loss_ref[...] += lpart
            dg_ref[...] += pg
            db_ref[...] += pbias

        dgt = (dr * pw * sg * (1.0 - sg)).astype(BF16)
        pb_ref[...] = pb
        dpw_ref[...] = (dr * sg).astype(BF16)
        dgt_ref[...] = dgt
        dx_ref[...] = ALPHA * dr + _nt(dgt, wpg_ref[...])

    def tok(n):
        return pl.BlockSpec((tm, n), lambda i: (i, 0))

    def full(shape):
        return pl.BlockSpec(shape, lambda i: (0,) * len(shape))

    return pl.pallas_call(
        body, name="ple_loss", grid=(T // tm,),
        in_specs=[tok(D), tok(D), tok(PD), full((4, PD, 256)), full((D, D)), full((1, D)), full((1, D)), tok(D)],
        out_specs=[full((1, 128)), tok(D), tok(PD), tok(D), tok(D), full((1, D)), full((1, D))],
        out_shape=_hbm_out([jax.ShapeDtypeStruct((1, 128), F32), jax.ShapeDtypeStruct((T, D), F32),
                            jax.ShapeDtypeStruct((T, PD), BF16), jax.ShapeDtypeStruct((T, D), BF16),
                            jax.ShapeDtypeStruct((T, D), BF16), jax.ShapeDtypeStruct((1, D), F32),
                            jax.ShapeDtypeStruct((1, D), F32)]),
        compiler_params=_cp(48, 1),
    )(*_hbm(x3, x3b, p, w_pi4, w_pg, g, b, target))


def _mix_bwd_b(dy, r2, g, w_mo, g_conv, g_ssm, y_conv, ga, gb, s, su, dvec, w_glu4, wc_re4, wc_im4, tm, comm=()):
    T = dy.shape[0]

    def body(dy_ref, r_ref, g_ref, wmo_ref, gc_ref, gs_ref, yc_ref, ga_ref, gb_ref, s_ref, su_ref, d_ref,
             wg_ref, wcr_ref, wci_ref,
             dres_ref, dmix_ref, dgl_ref, dsb_ref, dud_ref, gsr_ref, gsi_ref, dyc_ref, dp_ref,
             dg_ref, db_ref, dd_ref):
        i = pl.program_id(0)
        dyv = dy_ref[...]
        dr, xhat = _ln_bwd(dyv, r_ref[...], g_ref[...])
        dmix = dr.astype(BF16)
        dmerged = _nt(dmix, wmo_ref[...])
        sc, ss, sgb = _sig(gc_ref[...]), _sig(gs_ref[...]), _sig(gb_ref[...])
        gav = ga_ref[...]
        yssm = gav * sgb
        dgc = dmerged * yc_ref[...] * sc * (1.0 - sc)
        dgss = dmerged * yssm * ss * (1.0 - ss)
        dyssm = dmerged * ss
        dgl = jnp.concatenate([dyssm * sgb, dyssm * gav * sgb * (1.0 - sgb)], axis=1).astype(BF16)
        dsg = (_nt(dgl[:, 0:512], wg_ref[0]) + _nt(dgl[:, 512:1024], wg_ref[1])
               + _nt(dgl[:, 1024:1536], wg_ref[2]) + _nt(dgl[:, 1536:2048], wg_ref[3]))
        sv = s_ref[...]
        _, th = _gelu(sv)
        dgelu = 0.5 * (1.0 + th) + 0.5 * sv * (1.0 - th * th) * GELU_C * (1.0 + 3.0 * 0.044715 * sv * sv)
        ds = dsg * dgelu
        dsb = ds.astype(BF16)
        pg, pb, pd = _rowsum(dyv * xhat), _rowsum(dyv), _rowsum(ds * su_ref[...])

        @pl.when(i == 0)
        def _():
            dg_ref[...] = pg
            db_ref[...] = pb
            dd_ref[...] = pd

        @pl.when(i > 0)
        def _():
            dg_ref[...] += pg
            db_ref[...] += pb
            dd_ref[...] += pd

        dres_ref[...] = ALPHA * dr
        dmix_ref[...] = dmix
        dgl_ref[...] = dgl
        dsb_ref[...] = dsb
        dud_ref[...] = ds * d_ref[...]
        for J in range(4):
            gsr_ref[:, 512 * J:512 * (J + 1)] = _nt(dsb[:, 128 * J:128 * (J + 1)], wcr_ref[J])
            gsi_ref[:, 512 * J:512 * (J + 1)] = _nt(dsb[:, 128 * J:128 * (J + 1)], wci_ref[J])
        dyc_ref[...] = (dmerged * sc).astype(BF16)
        dp_ref[:, 0:D] = dgc.astype(BF16)
        dp_ref[:, D:2 * D] = dgss.astype(BF16)

    def tok(n):
        return pl.BlockSpec((tm, n), lambda i: (i, 0))

    def full(shape):
        return pl.BlockSpec(shape, lambda i: (0,) * len(shape))

    return _pcall(
        body, name="mix_bwd_b", grid=(T // tm,),
        in_specs=[tok(D), tok(D), full((1, D)), full((D, D)), tok(D), tok(D), tok(D), tok(D), tok(D),
                  tok(SSM), tok(SSM), full((1, SSM)), full((4, SSM, 512)), full((4, 512, 128)), full((4, 512, 128))],
        out_specs=[tok(D), tok(D), tok(2 * D), tok(SSM), tok(SSM), tok(LANES), tok(LANES), tok(D),
                   pl.BlockSpec((tm, 2 * D), lambda i: (i, 1)), full((1, D)), full((1, D)), full((1, SSM))],
        out_shape=[jax.ShapeDtypeStruct((T, D), F32), jax.ShapeDtypeStruct((T, D), BF16),
                   jax.ShapeDtypeStruct((T, 2 * D), BF16), jax.ShapeDtypeStruct((T, SSM), BF16),
                   jax.ShapeDtypeStruct((T, SSM), F32), jax.ShapeDtypeStruct((T, LANES), F32),
                   jax.ShapeDtypeStruct((T, LANES), F32), jax.ShapeDtypeStruct((T, D), BF16),
                   jax.ShapeDtypeStruct((T, 4 * D), BF16), jax.ShapeDtypeStruct((1, D), F32),
                   jax.ShapeDtypeStruct((1, D), F32), jax.ShapeDtypeStruct((1, SSM), F32)],
        vmem_mb=56, comm=comm,
        operands=(dy, r2, g, w_mo, g_conv, g_ssm, y_conv, ga, gb, s, su, dvec, w_glu4, wc_re4, wc_im4))


def _s5_scan_bwd(gs_re, gs_im, st_re, st_im, su_b, ds_b, wb_re, wb_im, a_re, a_im, comm=()):
    T = su_b.shape[0]
    W = SCAN_W
    R = SCAN_R

    def body(gr_ref, gi_ref, sr_ref, si_ref, su_ref, ds_ref, wbr_ref, wbi_ref, ar_ref, ai_ref,
             dsu_ref, dwbr_ref, dwbi_ref, dwcr_ref, dwci_ref, dar_ref, dai_ref, gre, gim):
        j = pl.program_id(0)
        zero = jnp.zeros((8, W), F32)
        for buf in (gre, gim):
            buf[pl.ds(T + 8, 8), :] = zero
        _scan_rows(gre, gim, ar_ref[...], ai_ref[...], T, True,
                   lambda t0: (gr_ref[pl.ds(t0, R), :], gi_ref[pl.ds(t0, R), :]))
        grb = gre[pl.ds(8, T), :].astype(BF16)
        gib = gim[pl.ds(8, T), :].astype(BF16)
        part = _nt(grb, wbr_ref[...]) + _nt(gib, wbi_ref[...])

        @pl.when(j % SCAN_PER == 0)
        def _():
            dsu_ref[...] = part

        @pl.when(j % SCAN_PER > 0)
        def _():
            dsu_ref[...] += part

        su = su_ref[...]
        dwbr_ref[...] = _tn(su, grb)
        dwbi_ref[...] = _tn(su, gib)
        dsv = ds_ref[...]
        dwcr_ref[...] = _tn(sr_ref[...].astype(BF16), dsv)
        dwci_ref[...] = _tn(si_ref[...].astype(BF16), dsv)
        dar = jnp.zeros((1, W), F32)
        dai = jnp.zeros((1, W), F32)
        for c in range(T // R):
            xr = sr_ref[pl.ds(c * R, R), :]
            xi = si_ref[pl.ds(c * R, R), :]
            g1r = gre[pl.ds(c * R + 9, R), :]
            g1i = gim[pl.ds(c * R + 9, R), :]
            dar = dar + _rowsum(g1r * xr + g1i * xi)
            dai = dai + _rowsum(g1i * xr - g1r * xi)
        dar_ref[...] = dar
        dai_ref[...] = dai

    lane, col, wb, wc, vec = _scan_specs(T)
    return _pcall(
        body, name="s5_scan_bwd", grid=(LANES // W,),
        in_specs=[lane, lane, lane, lane, col, col, wb, wb, vec, vec],
        out_specs=[col, wb, wb, wc, wc, vec, vec],
        out_shape=[jax.ShapeDtypeStruct((T, SSM), F32),
                   jax.ShapeDtypeStruct((LANES // W, 128, W), F32), jax.ShapeDtypeStruct((LANES // W, 128, W), F32),
                   jax.ShapeDtypeStruct((LANES // W, W, 128), F32), jax.ShapeDtypeStruct((LANES // W, W, 128), F32),
                   jax.ShapeDtypeStruct((1, LANES), F32), jax.ShapeDtypeStruct((1, LANES), F32)],
        scratch=[pltpu.VMEM((T + 16, W), F32)] * 2, vmem_mb=56, comm=comm,
        operands=(gs_re, gs_im, st_re, st_im, su_b, ds_b, wb_re, wb_im, a_re, a_im))


def _mix_bwd_a(dyc_b, w_co4, pc, z_b, conv_w, dsu_ssm, du_dir, dproj, dres, w_mix4, tm, comm=()):
    T = dres.shape[0]
    nt = T // tm

    def body(dyc_ref, wco_ref, pc_ref, halo_ref, z_ref, cw_ref, dsu_ref, dud_ref, dpin_ref, dres_ref, w_ref,
             dp_ref, dx_ref, dcw_ref, dcb_ref, dzbuf, qbuf):
        i = pl.program_id(0)
        ii = nt - 1 - i

        @pl.when(i == 0)
        def _():
            dzbuf[pl.ds(tm, 8), :] = jnp.zeros((8, CONV), F32)

        dyc = dyc_ref[...]
        dyin = (_nt(dyc[:, 0:256], wco_ref[0]) + _nt(dyc[:, 256:512], wco_ref[1])
                + _nt(dyc[:, 512:768], wco_ref[2]) + _nt(dyc[:, 768:1024], wco_ref[3]))
        cbv = pc_ref[:, 0:CONV].astype(F32)
        ccv = pc_ref[:, CONV:2 * CONV].astype(F32)
        chv = pc_ref[:, 2 * CONV:3 * CONV].astype(F32)
        dcbv = dyin * z_ref[...].astype(F32)
        dz = dyin * cbv
        dzbuf[pl.ds(0, tm), :] = dz
        cw = cw_ref[...]
        dq = cw[2:3] * dz + cw[1:2] * dzbuf[pl.ds(1, tm), :] + cw[0:1] * dzbuf[pl.ds(2, tm), :]
        dzbuf[pl.ds(tm, 8), :] = dz[0:8]
        q = ccv * chv
        hq = halo_ref[:, CONV:2 * CONV].astype(F32) * halo_ref[:, 2 * CONV:3 * CONV].astype(F32)
        qbuf[pl.ds(0, 8), :] = jnp.where(ii > 0, hq, jnp.zeros_like(hq))
        qbuf[pl.ds(8, tm), :] = q
        pw = jnp.concatenate([_rowsum(dz * qbuf[pl.ds(6, tm), :]), _rowsum(dz * qbuf[pl.ds(7, tm), :]),
                              _rowsum(dz * q), jnp.zeros((5, CONV), F32)], axis=0)
        pbias = _rowsum(dz)

        @pl.when(i == 0)
        def _():
            dcw_ref[...] = pw
            dcb_ref[...] = pbias

        @pl.when(i > 0)
        def _():
            dcw_ref[...] += pw
            dcb_ref[...] += pbias

        dp0 = jnp.concatenate([dcbv, dq * chv], axis=1).astype(BF16)
        dp1 = jnp.concatenate([dq * ccv, dsu_ref[...] + dud_ref[...]], axis=1).astype(BF16)
        dp_ref[:, 0:D] = dp0
        dp_ref[:, D:2 * D] = dp1
        dx_ref[...] = (dres_ref[...] + _nt(dp0, w_ref[0]) + _nt(dp1, w_ref[1])
                       + _nt(dpin_ref[:, 0:D], w_ref[2]) + _nt(dpin_ref[:, D:2 * D], w_ref[3]))

    def tok(n):
        return pl.BlockSpec((tm, n), lambda i: (nt - 1 - i, 0))

    def full(shape):
        return pl.BlockSpec(shape, lambda i: (0,) * len(shape))

    halo = pl.BlockSpec((8, 3 * CONV), lambda i: (jnp.maximum((nt - 1 - i) * (tm // 8) - 1, 0), 0))
    return _pcall(
        body, name="mix_bwd_a", grid=(nt,),
        in_specs=[tok(D), full((4, CONV, 256)), tok(3 * CONV), halo, tok(CONV), full((3, CONV)),
                  tok(SSM), tok(SSM), pl.BlockSpec((tm, 2 * D), lambda i: (nt - 1 - i, 1)), tok(D),
                  full((4, D, D))],
        out_specs=[pl.BlockSpec((tm, 2 * D), lambda i: (nt - 1 - i, 0)), tok(D), full((8, CONV)), full((1, CONV))],
        out_shape=[jax.ShapeDtypeStruct((T, 4 * D), BF16), jax.ShapeDtypeStruct((T, D), F32),
                   jax.ShapeDtypeStruct((8, CONV), F32), jax.ShapeDtypeStruct((1, CONV), F32)],
        scratch=[pltpu.VMEM((tm + 8, CONV), F32), pltpu.VMEM((tm + 8, CONV), F32)],
        aliases={8: 0}, vmem_mb=56, comm=comm,
        operands=(dyc_b, w_co4, pc, pc, z_b, conv_w, dsu_ssm, du_dir, dproj, dres, w_mix4))


def _zoh(lam_re, lam_im, log_step, b_re, b_im):
    dt = jnp.exp(log_step)[:, None]
    mag = jnp.exp(lam_re * dt)
    abr, abi = mag * jnp.cos(lam_im * dt), mag * jnp.sin(lam_im * dt)
    nr, ni = abr - 1.0, abi
    den = lam_re * lam_re + lam_im * lam_im
    cr = (nr * lam_re + ni * lam_im) / den
    ci = (ni * lam_re - nr * lam_im) / den
    bbr = cr[..., None] * b_re - ci[..., None] * b_im
    bbi = cr[..., None] * b_im + ci[..., None] * b_re
    return abr, abi, bbr, bbi


_WB_MASK = (np.arange(8)[None, :, None]
            == SCAN_GR * np.arange(SCAN_PER)[:, None, None] + np.arange(SCAN_GR)[None, None, :]).astype(np.float32)
_EYE8 = np.eye(8, dtype=np.float32)


def _wb_blocks(bb):
    bt = bb.transpose(0, 2, 1).reshape(4, 1, 8, 16, 1, STATE)
    full = bt * _WB_MASK[None, :, :, None, :, None]
    return full.reshape(LANES // SCAN_W, 128, SCAN_W).astype(BF16)


def _wc_blocks(cc):
    ct = cc.transpose(0, 2, 1).reshape(4, 8, STATE, 1, 16)
    full = ct * _EYE8[None, :, None, :, None]
    return full.reshape(4, 512, 128).astype(BF16)


def _wb_diag(dwb):
    d6 = dwb.reshape(4, SCAN_PER, 8, 16, SCAN_GR, STATE) * _WB_MASK[None, :, :, None, :, None]
    return d6.sum(axis=(1, 4)).reshape(GROUPS, 16, STATE).transpose(0, 2, 1)


def _wc_diag(dwc):
    mask = _WB_MASK.transpose(0, 2, 1)
    d6 = dwc.reshape(4, SCAN_PER, SCAN_GR, STATE, 8, 16) * mask[None, :, :, None, :, None]
    return d6.sum(axis=4).reshape(GROUPS, STATE, 16).transpose(0, 2, 1)


def _where():
    x, y, c = lax.axis_index("x"), lax.axis_index("y"), lax.axis_index("c")
    return x, y, c, 2 * x + y


def _chip_dev(k, c):
    return (k // 2, k % 2, c)


def _slot_cast(meidx, w, dtype, name, token=()):
    R, C = w.shape
    tr = _row_tile(R)

    def body(m_ref, w_ref, *rest):
        rest[-1][...] = w_ref[...].astype(dtype)

    gs = pltpu.PrefetchScalarGridSpec(
        num_scalar_prefetch=1, grid=(R // tr,),
        in_specs=[pl.BlockSpec((tr, C), lambda i, m: (i, 0))] + [pl.BlockSpec((8, 128), lambda i, m: (0, 0))] * len(token),
        out_specs=pl.BlockSpec((None, tr, C), lambda i, m: (m[0], i, 0)))
    return pl.pallas_call(
        body, name=name, grid_spec=gs, out_shape=_hbm_out(jax.ShapeDtypeStruct((4, R, C), dtype)),
        compiler_params=_cp(32, 1),
    )(meidx, *_hbm(w), *token)


def _gather_ici_payload(bufs):
    def copies(ins, lnd, ss, rs):
        x, y, c, me = _where()
        cps = []
        for w, b in enumerate(bufs):
            h = b.shape[1] // 2
            mine = lnd[w].at[me, pl.ds(c * h, h)]
            for s in range(3):
                k = (me + 1 + s) % 4
                cps.append(pltpu.make_async_remote_copy(
                    src_ref=mine, dst_ref=mine, send_sem=ss.at[3 * w + s], recv_sem=rs.at[3 * w + s],
                    device_id=_chip_dev(k, c), device_id_type=MESH))
        return cps

    p = _sym_payload([], [jax.ShapeDtypeStruct(b.shape, b.dtype) for b in bufs], copies, 3 * len(bufs))
    p.lands = list(bufs)
    return p


def _gather_pass_payload(bufs):
    def copies(ins, outs, ss, rs):
        x, y, c, me = _where()
        cps = []
        for w, b in enumerate(bufs):
            h = b.shape[1] // 2
            for s in range(3):
                j = (me + 1 + s) % 4
                cps.append(pltpu.make_async_remote_copy(
                    src_ref=ins[w].at[j, pl.ds(c * h, h)], dst_ref=outs[w].at[j, pl.ds(c * h, h)],
                    send_sem=ss.at[3 * w + s], recv_sem=rs.at[3 * w + s], device_id=(x, y, 1 - c),
                    device_id_type=MESH))
        return cps

    p = _sym_payload(bufs, [jax.ShapeDtypeStruct(b.shape, b.dtype) for b in bufs], copies, 3 * len(bufs))
    p.aliases = {w: w for w in range(len(bufs))}
    return p


def _gather_payload(bufs):
    n = len(bufs)

    def half(ref, w, k, cc):
        h = bufs[w].shape[1] // 2
        return ref.at[k, pl.ds(cc * h, h)]

    def ici(ins, outs, sems, w, s):
        x, y, c, me = _where()
        k = (me + 1 + s) % 4
        return pltpu.make_async_remote_copy(
            src_ref=half(ins[w], w, me, c), dst_ref=half(outs[w], w, me, c), send_sem=sems[0].at[3 * w + s],
            recv_sem=sems[1].at[3 * w + s], device_id=_chip_dev(k, c), device_id_type=MESH)

    def landed(outs, sems, w, s):
        x, y, c, me = _where()
        j = (me + 3 - s) % 4
        return pltpu.make_async_remote_copy(
            src_ref=half(outs[w], w, j, c), dst_ref=half(outs[w], w, j, c), send_sem=sems[0].at[3 * w + s],
            recv_sem=sems[1].at[3 * w + s], device_id=(x, y, 1 - c), device_id_type=MESH)

    def passed(outs, sems, w, s, cc):
        x, y, c, me = _where()
        j = (me + 3 - s) % 4
        return pltpu.make_async_remote_copy(
            src_ref=half(outs[w], w, j, cc), dst_ref=half(outs[w], w, j, cc), send_sem=sems[2].at[3 * w + s],
            recv_sem=sems[3].at[3 * w + s], device_id=(x, y, 1 - c), device_id_type=MESH)

    pairs = [(w, s) for w in range(n) for s in range(3)]

    def start(ins, outs, sems):
        for w, s in pairs:
            ici(ins, outs, sems, w, s).start()

    def finish(ins, outs, sems):
        _, _, c, _ = _where()
        for w, s in pairs:
            landed(outs, sems, w, s).wait_recv()
            passed(outs, sems, w, s, c).start()
        for w, s in pairs:
            passed(outs, sems, w, s, 1 - c).wait_recv()
        for w, s in pairs:
            ici(ins, outs, sems, w, s).wait_send()
            passed(outs, sems, w, s, c).wait_send()

    return _Payload(bufs, [jax.ShapeDtypeStruct(b.shape, b.dtype) for b in bufs], {w: w for w in range(n)},
                    [pltpu.SemaphoreType.DMA((3 * n,))] * 4, start, finish)


def _sym_payload(operands, outs, copies, n_copies):
    def start(ins, outs_, sems):
        for cp in copies(ins, outs_, sems[0], sems[1]):
            cp.start()

    def finish(ins, outs_, sems):
        for cp in copies(ins, outs_, sems[0], sems[1]):
            cp.wait()

    p = _Payload(operands, outs, {}, [pltpu.SemaphoreType.DMA((n_copies,))] * 2, start, finish)
    p.copies, p.n_copies = copies, n_copies
    return p


def _swap_payload(g4s):
    def copies(ins, outs, ss, rs):
        x, y, c, me = _where()
        cps = []
        for w, g in enumerate(g4s):
            h = g.shape[1] // 2
            cps.append(pltpu.make_async_remote_copy(
                src_ref=ins[w].at[:, pl.ds((1 - c) * h, h)], dst_ref=outs[w], send_sem=ss.at[w],
                recv_sem=rs.at[w], device_id=(x, y, 1 - c), device_id_type=MESH))
        return cps

    outs = [jax.ShapeDtypeStruct((4, g.shape[1] // 2, g.shape[2]), g.dtype) for g in g4s]
    return _sym_payload(g4s, outs, copies, len(g4s))


def _exchange_payload(pbs):
    def copies(ins, outs, ss, rs):
        x, y, c, me = _where()
        cps = []
        for w in range(len(pbs)):
            for s in range(3):
                k = (me + 1 + s) % 4
                cps.append(pltpu.make_async_remote_copy(
                    src_ref=ins[w].at[k], dst_ref=outs[w].at[2 - s], send_sem=ss.at[3 * w + s],
                    recv_sem=rs.at[3 * w + s], device_id=_chip_dev(k, c), device_id_type=MESH))
        return cps

    outs = [jax.ShapeDtypeStruct((3,) + p.shape[1:], p.dtype) for p in pbs]
    return _sym_payload(pbs, outs, copies, 3 * len(pbs))


HBM_REF = pl.BlockSpec(memory_space=pltpu.HBM)
SEM_REF = pl.BlockSpec(memory_space=pltpu.SEMAPHORE)
DATAFLOW = pltpu.SideEffectType.DATAFLOW_SIDE_EFFECTING


class _SemList:
    def __init__(self, refs):
        self.refs = refs

    @property
    def at(self):
        return self.refs


def _split_start(p, name):
    n_in, n_out, nc = len(p.operands), len(p.outs), p.n_copies
    lands = getattr(p, "lands", None) or [lax.empty(s.shape, s.dtype) for s in p.outs]

    def body(*refs):
        ins, lnd = refs[:n_in], refs[n_in:n_in + n_out]
        sems = refs[n_in + n_out:n_in + n_out + 2 * nc]
        for cp in p.copies(ins, lnd, _SemList(sems[:nc]), _SemList(sems[nc:])):
            cp.start()
        refs[-1][...] = jnp.zeros((8, 128), F32)

    res = pl.pallas_call(
        body, name=name,
        in_specs=[HBM_REF] * (n_in + n_out),
        out_specs=[SEM_REF] * (2 * nc) + [HBM_REF] * (n_in + n_out) + [VMEM_FULL],
        out_shape=([pltpu.SemaphoreType.DMA(())] * (2 * nc) + _hbm_out(p.operands) + _hbm_out(lands)
                   + [jax.ShapeDtypeStruct((8, 128), F32)]),
        input_output_aliases={i: 2 * nc + i for i in range(n_in + n_out)},
        compiler_params=pltpu.CompilerParams(has_side_effects=DATAFLOW),
    )(*_hbm(*p.operands, *lands))
    k = 2 * nc
    return list(res[:k]), list(res[k:k + n_in]), list(res[k + n_in:k + n_in + n_out]), res[-1]


def _split_wait(p, handle, after, name):
    sems, srcs, lands, _ = handle
    n_in, n_out, nc = len(srcs), len(lands), p.n_copies

    def body(*refs):
        ins, lnd = refs[:n_in], refs[n_in:n_in + n_out]
        sm = refs[n_in + n_out:n_in + n_out + 2 * nc]
        for cp in p.copies(ins, lnd, _SemList(sm[:nc]), _SemList(sm[nc:])):
            cp.wait_send()
            cp.wait_recv()

    res = pl.pallas_call(
        body, name=name,
        in_specs=[HBM_REF] * (n_in + n_out) + [SEM_REF] * (2 * nc) + [ANY] * len(after),
        out_specs=[HBM_REF] * (n_in + n_out), out_shape=_hbm_out(srcs) + _hbm_out(lands),
        input_output_aliases={i: i for i in range(n_in + n_out)},
        compiler_params=pltpu.CompilerParams(has_side_effects=DATAFLOW),
    )(*srcs, *lands, *sems, *after)
    return list(res[:n_in]), list(res[n_in:])


def _join_payload(halves):
    def copies(ins, outs, ss, rs):
        x, y, c, me = _where()
        return [pltpu.make_async_remote_copy(
            src_ref=ins[w], dst_ref=outs[w], send_sem=ss.at[w], recv_sem=rs.at[w],
            device_id=(x, y, 1 - c), device_id_type=MESH) for w in range(len(halves))]

    outs = [jax.ShapeDtypeStruct(a.shape, a.dtype) for a in halves]
    return _sym_payload(halves, outs, copies, len(halves))


def _allgather_payload(v):
    def copies(ins, outs, ss, rs):
        x, y, c, me = _where()
        lin = 4 * x + 2 * y + c
        cps = []
        for o in range(1, 8):
            t = (lin + o) % 8
            cps.append(pltpu.make_async_remote_copy(
                src_ref=ins[0], dst_ref=outs[0].at[lin], send_sem=ss.at[o - 1], recv_sem=rs.at[o - 1],
                device_id=(t // 4, (t // 2) % 2, t % 2), device_id_type=MESH))
        return cps

    p = _sym_payload([v], [jax.ShapeDtypeStruct((8,) + v.shape, v.dtype)], copies, 7)
    x, y, c, _ = _where()
    p.lands = [lax.dynamic_update_slice(jnp.zeros((8,) + v.shape, v.dtype), v[None], (4 * x + 2 * y + c, 0, 0))]
    return p


def _sum8(buf, token):
    _, P, C = buf.shape

    def body(b_ref, t_ref, o_ref):
        acc = b_ref[0]
        for d in range(1, 8):
            acc = acc + b_ref[d]
        o_ref[...] = acc

    return pl.pallas_call(
        body, name="sum8", in_specs=[VMEM_FULL, VMEM_FULL], out_specs=VMEM_FULL,
        out_shape=jax.ShapeDtypeStruct((P, C), F32),
        compiler_params=pltpu.CompilerParams(vmem_limit_bytes=32 << 20),
    )(buf, token)


def _row_tile(h):
    for t in (256, 176, 128, 64, 32, 16, 8):
        if h % t == 0:
            return t
    raise ValueError(h)


def _pair_sum(cmidx, g4, got, name):
    _, R, C = g4.shape
    h = R // 2
    th = _row_tile(h)

    def body(cm_ref, a_ref, b_ref, o_ref, ob_ref):
        sm = a_ref[...] + b_ref[...]
        ob_ref[...] = sm.astype(BF16)

        @pl.when(pl.program_id(1) == cm_ref[1])
        def _():
            o_ref[...] = sm

    blk = pl.BlockSpec((None, th, C), lambda i, k, cm: (k, i, 0))
    gs = pltpu.PrefetchScalarGridSpec(
        num_scalar_prefetch=1, grid=(h // th, 4),
        in_specs=[pl.BlockSpec((None, None, th, C), lambda i, k, cm: (k, cm[0], i, 0)), blk],
        out_specs=[pl.BlockSpec((th, C), lambda i, k, cm: (i, 0)), blk])
    return pl.pallas_call(
        body, name=name, grid_spec=gs,
        out_shape=_hbm_out([jax.ShapeDtypeStruct((h, C), F32), jax.ShapeDtypeStruct((4, h, C), BF16)]),
        compiler_params=_cp(32, 2),
    )(cmidx, *_hbm(g4.reshape(4, 2, h, C), got))


def _chip_sum(own, got, name):
    h, C = own.shape
    th = _row_tile(h)

    def body(a_ref, b_ref, o_ref):
        o_ref[...] = ((a_ref[...] + b_ref[0].astype(F32)) + b_ref[1].astype(F32)) + b_ref[2].astype(F32)

    return pl.pallas_call(
        body, name=name, grid=(h // th,),
        in_specs=[pl.BlockSpec((th, C), lambda i: (i, 0)), pl.BlockSpec((3, th, C), lambda i: (0, i, 0))],
        out_specs=pl.BlockSpec((th, C), lambda i: (i, 0)),
        out_shape=_hbm_out(jax.ShapeDtypeStruct((h, C), F32)),
        compiler_params=_cp(32, 1),
    )(*_hbm(own, got))


def _adamw_math(w, g, m, v):
    m2 = B1 * m + (1.0 - B1) * g
    v2 = B2 * v + (1.0 - B2) * (g * g)
    m_hat = m2 / (1.0 - B1 ** STEP)
    v_hat = v2 / (1.0 - B2 ** STEP)
    delta = -LR * (m_hat / (jnp.sqrt(v_hat) + EPS) + WD * w)
    return delta, m2, v2


def _adamw_pair(cidx, w, mine, theirs, m, v, token, name):
    R, C = w.shape
    h = R // 2
    tr = _row_tile(h)
    nh = h // tr

    def body(c_ref, w_ref, a_ref, b_ref, m_ref, v_ref, t_ref, g_ref, d_ref, mo_ref, vo_ref):
        own = (pl.program_id(0) // nh) == c_ref[0]
        g = jnp.where(own, a_ref[...], b_ref[...])
        d, m2, v2 = _adamw_math(w_ref[...], g, m_ref[...], v_ref[...])
        g_ref[...] = g
        d_ref[...] = d
        mo_ref[...] = m2
        vo_ref[...] = v2

    blk = pl.BlockSpec((tr, C), lambda i, c: (i, 0))
    mine_blk = pl.BlockSpec((tr, C), lambda i, c: (jnp.clip(i - c[0] * nh, 0, nh - 1), 0))
    theirs_blk = pl.BlockSpec((tr, C), lambda i, c: (jnp.clip(i - (1 - c[0]) * nh, 0, nh - 1), 0))
    gs = pltpu.PrefetchScalarGridSpec(
        num_scalar_prefetch=1, grid=(R // tr,),
        in_specs=[blk, mine_blk, theirs_blk, blk, blk, pl.BlockSpec((8, 128), lambda i, c: (0, 0))],
        out_specs=[blk] * 4)
    return pl.pallas_call(
        body, name=name, grid_spec=gs, out_shape=_hbm_out([jax.ShapeDtypeStruct((R, C), F32)] * 4),
        compiler_params=_cp(32, 1),
    )(cidx, *_hbm(w, mine, theirs, m, v), token)


def _adamw(w, g, m, v, name):
    R, C = w.shape
    tr = _row_tile(R)

    def body(w_ref, g_ref, m_ref, v_ref, d_ref, mo_ref, vo_ref):
        d, m2, v2 = _adamw_math(w_ref[...], g_ref[...], m_ref[...], v_ref[...])
        d_ref[...] = d
        mo_ref[...] = m2
        vo_ref[...] = v2

    blk = pl.BlockSpec((tr, C), lambda i: (i, 0))
    return pl.pallas_call(
        body, name=name, grid=(R // tr,), in_specs=[blk] * 4, out_specs=[blk] * 3,
        out_shape=_hbm_out([jax.ShapeDtypeStruct((R, C), F32)] * 3),
        compiler_params=_cp(32, 1),
    )(*_hbm(w, g, m, v))


def _pack(arrs):
    flat = jnp.concatenate([a.reshape(-1).astype(F32) for a in arrs])
    rows = -(-flat.shape[0] // 1024)
    rows = -(-rows // 8) * 8
    return jnp.pad(flat, (0, rows * 1024 - flat.shape[0])).reshape(rows, 1024)


def _unpack(packed, shapes):
    flat = packed.reshape(-1)
    out, off = [], 0
    for s in shapes:
        n = math.prod(s)
        out.append(flat[off:off + n].reshape(s))
        off += n
    return out


BIG = ["ffn1_w_in", "ffn1_w_out", "mix_w_in", "conv_w_out", "ssm_w_glu", "mix_w_out",
       "ffn2_w_in", "ffn2_w_out", "ple_w_in", "ple_w_gate"]
SMALL = ["ln1_g", "ln1_b", "conv_w", "conv_b", "ssm_lam_re", "ssm_lam_im", "ssm_log_step", "ssm_b_re", "ssm_b_im",
         "ssm_c_re", "ssm_c_im", "ssm_d", "ln2_g", "ln2_b", "ln3_g", "ln3_b", "ln4_g", "ln4_b"]
WEIGHTS = ["ffn1_w_in", "ffn1_w_out", "ln1_g", "ln1_b", "mix_w_in", "conv_w", "conv_b", "conv_w_out",
           "ssm_lam_re", "ssm_lam_im", "ssm_log_step", "ssm_b_re", "ssm_b_im", "ssm_c_re", "ssm_c_im", "ssm_d",
           "ssm_w_glu", "mix_w_out", "ln2_g", "ln2_b", "ffn2_w_in", "ffn2_w_out", "ln3_g", "ln3_b",
           "ple_w_in", "ple_w_gate", "ln4_g", "ln4_b"]


class _NoComm:
    def __init__(self, W):
        self.W, self.G, self.raw = dict(W), {}, None

    def carry(self, name):
        return ()

    def landed(self, name, got):
        pass

    def grad(self, name, g4):
        self.G[name] = g4

    def small(self, raw):
        self.raw = raw


def _s5_operands(sp):
    abr, abi, bbr, bbi = _zoh(sp["ssm_lam_re"], sp["ssm_lam_im"], sp["ssm_log_step"], sp["ssm_b_re"], sp["ssm_b_im"])
    return (_wb_blocks(bbr), _wb_blocks(bbi), _wc_blocks(sp["ssm_c_re"]), _wc_blocks(-sp["ssm_c_im"]),
            abr.reshape(1, LANES), abi.reshape(1, LANES), sp["ssm_d"].reshape(1, SSM))


def _local_step(x, p, target, sp, sched, tm_ffn, tm_mix, ops=None):
    W = sched.W
    wb_re, wb_im, wc_re4, wc_im4, a_re, a_im, dvec = ops if ops is not None else _s5_operands(sp)

    def run(fn, name, *args, **kw):
        outs, got = fn(*args, comm=sched.carry(name), **kw)
        sched.landed(name, got)
        return outs

    def dw(name, wname, a, b, tk, tn, shape4, shard_cols=None, interleaved=False):
        out, got = _mm_tn(a, b, tk, tn, name, shard_cols=shard_cols, interleaved=interleaved,
                          comm=sched.carry(name))
        sched.landed(name, got)
        sched.grad(wname, out.reshape(shape4))

    xb = x.astype(BF16)
    h1, r1, x1, x1b = run(_ffn_fwd, "ffn1_fwd", x, xb, W["ffn1_w_in"], W["ffn1_w_out"].reshape(2, FFH, D),
                          sp["ln1_g"], sp["ln1_b"], tm_ffn, "ffn1_fwd")
    conv_w = W["conv_w"][:, 0:3, :].transpose(1, 0, 2).reshape(3, CONV)
    pc, z_b, yin_b, su, su_b, g_conv, g_ssm, y_conv = run(
        _mix_fwd_a, "mix_fwd_a", x1b, W["mix_w_in"], conv_w, sp["conv_b"], W["conv_w_out"], tm_mix)
    st_re, st_im = run(_s5_scan_fwd, "s5_scan_fwd", su_b, wb_re, wb_im, a_re, a_im)
    w_mo = W["mix_w_out"].reshape(D, D)
    s, sg_b, ga, gb, merged_b, r2, x2, x2b = run(
        _mix_fwd_b, "mix_fwd_b", st_re, st_im, wc_re4, wc_im4, su, dvec, W["ssm_w_glu"], g_conv, g_ssm, y_conv,
        w_mo, x1, sp["ln2_g"], sp["ln2_b"], tm_mix)
    w2o2 = W["ffn2_w_out"].reshape(2, FFH, D)
    h2, r3, x3, x3b = run(_ffn_fwd, "ffn2_fwd", x2, x2b, W["ffn2_w_in"], w2o2, sp["ln3_g"], sp["ln3_b"], tm_ffn,
                          "ffn2_fwd")
    loss_part, dx3, p_b, dpw_b, dgt_b, dg4, db4 = _ple_loss(
        x3, x3b, p, W["ple_w_in"], W["ple_w_gate"].reshape(D, D), sp["ln4_g"], sp["ln4_b"], target, tm_mix)

    dw("dw_ple_gate", "ple_w_gate", x3b, dgt_b, 512, 1024, (4, 256, D))
    dw("dw_ple_in", "ple_w_in", p_b, dpw_b, 256, 256, (4, 256, 256), shard_cols=256)
    dx2, dh2, a2_b, df2_b, dg3, db3 = run(_ffn_bwd, "ffn2_bwd", dx3, r3, sp["ln3_g"], h2, W["ffn2_w_in"], w2o2,
                                          tm_mix, "ffn2_bwd")
    dw("dw_ffn2_in", "ffn2_w_in", x2b, dh2, 512, FFH, (4, D, FFH), shard_cols=FFH, interleaved=True)
    dw("dw_ffn2_out", "ffn2_w_out", a2_b, df2_b, FFH, 1024, (4, FF // 4, D))
    (dres, dmix_b, dgl_b, ds_b, du_dir, gs_re, gs_im, dyc_b, dproj, dg2, db2, dd) = run(
        _mix_bwd_b, "mix_bwd_b", dx2, r2, sp["ln2_g"], w_mo, g_conv, g_ssm, y_conv, ga, gb, s, su, dvec,
        W["ssm_w_glu"], wc_re4, wc_im4, tm_mix)
    dw("dw_mix_out", "mix_w_out", merged_b, dmix_b, 512, 1024, (4, 256, D))
    dw("dw_glu", "ssm_w_glu", sg_b, dgl_b, 512, 512, (4, SSM, 512), shard_cols=512)
    dsu_ssm, dwb_re, dwb_im, dwc_re, dwc_im, da_re, da_im = run(
        _s5_scan_bwd, "s5_scan_bwd", gs_re, gs_im, st_re, st_im, su_b, ds_b, wb_re, wb_im, a_re, a_im)
    dw("dw_conv_out", "conv_w_out", yin_b, dyc_b, 512, 256, (4, CONV, 256), shard_cols=256)
    dproj, dx1, dcw8, dcb = run(_mix_bwd_a, "mix_bwd_a", dyc_b, W["conv_w_out"], pc, z_b, conv_w, dsu_ssm,
                                du_dir, dproj, dres, W["mix_w_in"], tm_mix)
    dw("dw_mix_in", "mix_w_in", x1b, dproj, 512, 1024, (4, D, D), shard_cols=1024)
    dx0, dh1, a1_b, df1_b, dg1, db1 = run(_ffn_bwd, "ffn1_bwd", dx1, r1, sp["ln1_g"], h1, W["ffn1_w_in"],
                                          W["ffn1_w_out"].reshape(2, FFH, D), tm_mix, "ffn1_bwd")
    sched.small(dict(
        ln1_g=dg1, ln1_b=db1, ln2_g=dg2, ln2_b=db2, ln3_g=dg3, ln3_b=db3, ln4_g=dg4, ln4_b=db4,
        conv_w=dcw8[0:3], conv_b=dcb,
        a_re=da_re.reshape(GROUPS, STATE), a_im=da_im.reshape(GROUPS, STATE),
        bb_re=_wb_diag(dwb_re), bb_im=_wb_diag(dwb_im),
        ssm_c_re=_wc_diag(dwc_re), ssm_c_im=-_wc_diag(dwc_im), ssm_d=dd.reshape(GROUPS, 16),
        loss=loss_part[0:1, 0]))
    dw("dw_ffn1_in", "ffn1_w_in", xb, dh1, 512, FFH, (4, D, FFH), shard_cols=FFH, interleaved=True)
    dw("dw_ffn1_out", "ffn1_w_out", a1_b, df1_b, FFH, 1024, (4, FF // 4, D))
    return loss_part[0, 0], dx0


RAW_ORDER = ["ln1_g", "ln1_b", "ln2_g", "ln2_b", "ln3_g", "ln3_b", "ln4_g", "ln4_b", "conv_w", "conv_b",
             "a_re", "a_im", "bb_re", "bb_im", "ssm_c_re", "ssm_c_im", "ssm_d", "loss"]

GATHER_FIRST = ["ffn1_w_in", "ffn1_w_out"]
GATHER_AT = {"ffn1_fwd": ["mix_w_in", "conv_w_out", "conv_w"], "mix_fwd_a": ["ssm_w_glu", "mix_w_out"],
             "s5_scan_fwd": ["ffn2_w_in"], "mix_fwd_b": ["ffn2_w_out"], "ffn2_fwd": ["ple_w_in", "ple_w_gate"]}
REDUCE_GROUP = {"ple": ["ple_w_gate", "ple_w_in"], "ffn2": ["ffn2_w_in", "ffn2_w_out"],
                "mix": ["mix_w_out", "ssm_w_glu", "conv_w_out", "mix_w_in"], "ffn1": ["ffn1_w_in", "ffn1_w_out"]}
REDUCE_AT = {"ffn2_bwd": [("swap", "ple")], "dw_ffn2_in": [("exchange", "ple")],
             "mix_bwd_b": [("swap", "ffn2"), ("join", "ple")],
             "mix_bwd_a": [("join", "ffn2")], "ffn1_bwd": [("swap", "mix")]}
BEGIN_AT = {"dw_mix_out": [("exchange", "ffn2")], "dw_ffn1_in": [("small", None), ("exchange", "mix")]}
END_AT = {"mix_bwd_a": [("exchange", "ffn2", "conv_w_out")]}
LAST_GROUP = "ffn1"


class _Sched:
    def __init__(self, cmidx):
        self.bufs, self.cmidx = {}, cmidx
        self.W, self.G, self.raw, self.small_buf = {}, {}, None, None
        self.got1, self.p32, self.pbf, self.got2, self.half, self.theirs = {}, {}, {}, {}, {}, {}
        self._open, self._split = [], {}

    def first_begin(self, bufs):
        self.bufs.update(bufs)
        p = _gather_ici_payload([bufs[n] for n in GATHER_FIRST])
        self._first = (p, _split_start(p, "gather_first_start"))
        return self._first[1][3]

    def first_end(self, bufs, after):
        self.bufs.update(bufs)
        p, handle = self._first
        _, landed = _split_wait(p, handle, after, "gather_first_wait")
        (outs,) = _comm_call("gather_first_pass", [_gather_pass_payload(landed)])
        self.W.update(zip(GATHER_FIRST, outs))

    def _payload(self, stage, key):
        if stage == "gather":
            return _gather_payload([self.bufs[n] for n in key])
        if stage == "small":
            return _allgather_payload(_pack([self.raw[k] for k in RAW_ORDER]))
        names = REDUCE_GROUP[key]
        if stage == "swap":
            return _swap_payload([self.G[n] for n in names])
        if stage == "exchange":
            for n in names:
                self.p32[n], self.pbf[n] = _pair_sum(self.cmidx, self.G[n], self.got1[n], "pair_sum_" + n)
            return _exchange_payload([self.pbf[n] for n in names])
        for n in names:
            self.half[n] = _chip_sum(self.p32[n], self.got2[n], "chip_sum_" + n)
        return _join_payload([self.half[n] for n in names])

    def _store(self, stages, got):
        for (stage, key), outs in zip(stages, got):
            if stage == "gather":
                self.W.update(zip(key, outs))
            elif stage == "small":
                self.small_buf = outs[0]
            else:
                {"swap": self.got1, "exchange": self.got2, "join": self.theirs}[stage].update(
                    zip(REDUCE_GROUP[key], outs))

    def _standalone(self, name, stages):
        self._store(stages, _comm_call(name, [self._payload(s, k) for s, k in stages]))

    def carry(self, name):
        for stage, key, behind in END_AT.get(name, []):
            self._end(stage, key, [self.G[behind]])
        tokens = [self._begin(stage, key) for stage, key in BEGIN_AT.get(name, [])]
        self._open = [("gather", GATHER_AT[name])] if name in GATHER_AT else []
        self._open += REDUCE_AT.get(name, [])
        comm = [self._payload(s, k) for s, k in self._open]
        if tokens:
            comm.append(_Payload(tokens, [], {}, [], lambda *a: None, lambda *a: None))
        return tuple(comm)

    def landed(self, name, got):
        self._store(self._open, got)

    def grad(self, name, g4):
        self.G[name] = g4

    def small(self, raw):
        self.raw = raw

    def _begin(self, stage, key):
        p = self._payload(stage, key)
        self._split[stage, key] = (p, _split_start(p, "%s_%s_start" % (stage, key)))
        return self._split[stage, key][1][3]

    def _end(self, stage, key, after):
        p, handle = self._split.pop((stage, key))
        srcs, lands = _split_wait(p, handle, after, "%s_%s_wait" % (stage, key))
        if stage == "swap":
            self.G.update(zip(REDUCE_GROUP[key], srcs))
        self._store([(stage, key)], [lands])

    def tail_begin(self):
        return self._begin("swap", LAST_GROUP)

    def tail_mid(self, after):
        self._end("swap", LAST_GROUP, after)
        token = self._begin("exchange", LAST_GROUP)
        self._end("small", None, [token])
        self._end("exchange", "mix", [token])
        self._standalone("reduce_tail_join_mix", [("join", "mix")])
        return token

    def tail_end(self, after):
        self._end("exchange", LAST_GROUP, after)
        self._standalone("reduce_tail_join", [("join", LAST_GROUP)])


def _small_grads(raw_sum, sp):
    _, vjp = jax.vjp(_zoh, sp["ssm_lam_re"], sp["ssm_lam_im"], sp["ssm_log_step"], sp["ssm_b_re"], sp["ssm_b_im"])
    d_lre, d_lim, d_ls, d_bre, d_bim = vjp((raw_sum["a_re"], raw_sum["a_im"], raw_sum["bb_re"], raw_sum["bb_im"]))
    g = {k: raw_sum[k] for k in ("ln1_g", "ln1_b", "ln2_g", "ln2_b", "ln3_g", "ln3_b", "ln4_g", "ln4_b",
                                 "conv_w", "conv_b", "ssm_c_re", "ssm_c_im", "ssm_d")}
    g.update(ssm_lam_re=d_lre, ssm_lam_im=d_lim, ssm_log_step=d_ls, ssm_b_re=d_bre, ssm_b_im=d_bim)
    return g


def kernel(x, p, ffn1_w_in, ffn1_w_out, ln1_g, ln1_b, mix_w_in, conv_w, conv_b, conv_w_out, ssm_lam_re, ssm_lam_im, ssm_log_step, ssm_b_re, ssm_b_im, ssm_c_re, ssm_c_im, ssm_d, ssm_w_glu, mix_w_out, ln2_g, ln2_b, ffn2_w_in, ffn2_w_out, ln3_g, ln3_b, ple_w_in, ple_w_gate, ln4_g, ln4_b, loss_target, m_ffn1_w_in, m_ffn1_w_out, m_ln1_g, m_ln1_b, m_mix_w_in, m_conv_w, m_conv_b, m_conv_w_out, m_ssm_lam_re, m_ssm_lam_im, m_ssm_log_step, m_ssm_b_re, m_ssm_b_im, m_ssm_c_re, m_ssm_c_im, m_ssm_d, m_ssm_w_glu, m_mix_w_out, m_ln2_g, m_ln2_b, m_ffn2_w_in, m_ffn2_w_out, m_ln3_g, m_ln3_b, m_ple_w_in, m_ple_w_gate, m_ln4_g, m_ln4_b, v_ffn1_w_in, v_ffn1_w_out, v_ln1_g, v_ln1_b, v_mix_w_in, v_conv_w, v_conv_b, v_conv_w_out, v_ssm_lam_re, v_ssm_lam_im, v_ssm_log_step, v_ssm_b_re, v_ssm_b_im, v_ssm_c_re, v_ssm_c_im, v_ssm_d, v_ssm_w_glu, v_mix_w_out, v_ln2_g, v_ln2_b, v_ffn2_w_in, v_ffn2_w_out, v_ln3_g, v_ln3_b, v_ple_w_in, v_ple_w_gate, v_ln4_g, v_ln4_b):
    args = dict(locals())
    w = {n: args[n] for n in WEIGHTS}
    m = {n: args["m_" + n] for n in WEIGHTS}
    v = {n: args["v_" + n] for n in WEIGHTS}
    _, _, c, me = _where()
    cidx = jnp.stack([c, me]).astype(jnp.int32)
    meidx = jnp.reshape(me, (1,)).astype(jnp.int32)

    sched = _Sched(cidx)
    token = sched.first_begin({n: _slot_cast(meidx, w[n][0], BF16, "cast_" + n) for n in GATHER_FIRST})
    rest = {n: _slot_cast(meidx, w[n][0], BF16, "cast_" + n, (token,)) for n in BIG if n not in GATHER_FIRST}
    rest["conv_w"] = _slot_cast(meidx, jnp.pad(conv_w[0], ((0, 13), (0, 0))), F32, "cast_conv_w", (token,))
    sp = {n: (w[n] if w[n].ndim == 2 and n != "ssm_log_step" else w[n][0]) for n in SMALL if n != "conv_w"}
    ops = _s5_operands({**sp, "ssm_lam_re": sp["ssm_lam_re"] + token[0, 0]})
    sched.first_end(rest, list(rest.values()) + list(ops))
    loss_part, dx0 = _local_step(x[0], p[0, 0], loss_target[0], sp, sched, 512, 256, ops)
    out_g, out_d, out_m, out_v = {}, {}, {}, {}

    def big_adamw(names, token):
        for n in names:
            g, dl, mn, vn = _adamw_pair(cidx, w[n][0], sched.half[n], sched.theirs[n], m[n][0], v[n][0], token,
                                        "adamw_" + n)
            out_g[n], out_d[n], out_m[n], out_v[n] = g[None], dl[None], mn[None], vn[None]

    first = REDUCE_GROUP["ple"] + REDUCE_GROUP["ffn2"]
    big_adamw(first, sched.tail_begin())
    token = sched.tail_mid([out_v[n] for n in first])

    raw_shapes = [sched.raw[k].shape for k in RAW_ORDER]
    raw_sum = dict(zip(RAW_ORDER, _unpack(_sum8(sched.small_buf, token), raw_shapes)))
    loss = raw_sum["loss"][0]
    sg = _small_grads(raw_sum, sp)
    sg["conv_w"] = lax.dynamic_slice_in_dim(sg["conv_w"], me * 128, 128, axis=1)
    small_shapes = [w[n].shape for n in SMALL]
    gp = _pack([sg[n] for n in SMALL])
    d_s, m_s, v_s = _adamw(_pack([w[n] for n in SMALL]), gp, _pack([m[n] for n in SMALL]),
                           _pack([v[n] for n in SMALL]), "adamw_small")

    for n, a, b_, c_, d_ in zip(SMALL, _unpack(gp, small_shapes), _unpack(d_s, small_shapes),
                                _unpack(m_s, small_shapes), _unpack(v_s, small_shapes)):
        out_g[n], out_d[n], out_m[n], out_v[n] = a, b_, c_, d_
    big_adamw(REDUCE_GROUP["mix"], token)
    sched.tail_end([d_s] + [out_v[n] for n in REDUCE_GROUP["mix"]])
    big_adamw(REDUCE_GROUP[LAST_GROUP], token)

    return (loss, dx0[None], *[out_g[n] for n in WEIGHTS], *[out_d[n] for n in WEIGHTS],
            *[out_m[n] for n in WEIGHTS], *[out_v[n] for n in WEIGHTS])
```

```python
import functools
import math

import jax
import jax.numpy as jnp
import numpy as np
from jax import lax
from jax.experimental import pallas as pl
from jax.experimental.pallas import tpu as pltpu

F32, BF16 = jnp.float32, jnp.bfloat16
D = 1024
FF = 2816
FFH = FF // 2
CONV = 512
SSM = 512
GROUPS = 32
STATE = 64
LANES = GROUPS * STATE
SCAN_W = 128
SCAN_PER = 512 // SCAN_W
SCAN_GR = SCAN_W // STATE
SCAN_R = 256
ALPHA = 2.0 ** 0.25
LN_EPS = 1e-5
GELU_C = math.sqrt(2.0 / math.pi)
B1, B2, LR, EPS, WD, STEP = 0.9, 0.999, 0.001, 1e-8, 0.01, 10
MESH = pl.DeviceIdType.MESH
ANY = pl.BlockSpec(memory_space=pl.ANY)
VMEM_FULL = pl.BlockSpec(memory_space=pltpu.VMEM)


def _cp(vmem_mb=48, n_axes=1):
    return pltpu.CompilerParams(vmem_limit_bytes=vmem_mb << 20,
                                dimension_semantics=("arbitrary",) * n_axes)


def _hbm(*arrs):
    return [pltpu.with_memory_space_constraint(a, pltpu.HBM) for a in arrs]


def _hbm_out(shapes):
    if isinstance(shapes, (list, tuple)):
        return [pltpu.HBM(s.shape, s.dtype) for s in shapes]
    return pltpu.HBM(shapes.shape, shapes.dtype)


def _nn(a, b):
    return jnp.dot(a, b, preferred_element_type=F32)


def _nt(a, b):
    return lax.dot_general(a, b, (((1,), (1,)), ((), ())), preferred_element_type=F32)


def _tn(a, b):
    return lax.dot_general(a, b, (((0,), (0,)), ((), ())), preferred_element_type=F32)


def _sig(v):
    return jax.nn.sigmoid(v)


def _ln_stats(r):
    mu = jnp.mean(r, axis=-1, keepdims=True)
    xc = r - mu
    var = jnp.mean(xc * xc, axis=-1, keepdims=True)
    rstd = lax.rsqrt(var + LN_EPS)
    return xc * rstd, rstd


def _ln_bwd(dy, r, g):
    xhat, rstd = _ln_stats(r)
    dyg = dy * g
    m1 = jnp.mean(dyg, axis=-1, keepdims=True)
    m2 = jnp.mean(dyg * xhat, axis=-1, keepdims=True)
    return rstd * (dyg - m1 - xhat * m2), xhat


def _rowsum(v):
    return jnp.sum(v, axis=0, keepdims=True)


class _Payload:
    def __init__(self, operands, outs, aliases, sems, start, finish):
        self.operands, self.outs, self.aliases, self.sems = list(operands), list(outs), dict(aliases), list(sems)
        self.start, self.finish = start, finish


def _split(flat, comm, attr):
    out, i = [], 0
    for p in comm:
        n = len(getattr(p, attr))
        out.append(list(flat[i:i + n]))
        i += n
    return out


def _run_comm(comm, which, cin, cout, csem):
    for p, a, b, s in zip(comm, _split(cin, comm, "operands"), _split(cout, comm, "outs"), _split(csem, comm, "sems")):
        getattr(p, which)(a, b, s)


def _pcall(body, *, name, grid, in_specs, out_specs, out_shape, operands, scratch=(), vmem_mb=48, aliases=None,
           comm=()):
    ni, no, ns = len(in_specs), len(out_specs), len(scratch)
    c_ops = [a for p in comm for a in p.operands]
    c_outs = [s for p in comm for s in p.outs]
    c_sems = [s for p in comm for s in p.sems]
    io = dict(aliases or {})
    off_i, off_o = ni, no
    for p in comm:
        for a, b in p.aliases.items():
            io[off_i + a] = off_o + b
        off_i += len(p.operands)
        off_o += len(p.outs)

    def wrapped(*refs):
        ins, cin = refs[:ni], refs[ni:ni + len(c_ops)]
        o0 = ni + len(c_ops)
        outs, cout = refs[o0:o0 + no], refs[o0 + no:o0 + no + len(c_outs)]
        s0 = o0 + no + len(c_outs)
        scr, csem = refs[s0:s0 + ns], refs[s0 + ns:]
        if comm:
            first = functools.reduce(jnp.logical_and, [pl.program_id(a) == 0 for a in range(len(grid))])
            pl.when(first)(lambda: _run_comm(comm, "start", cin, cout, csem))
        body(*ins, *outs, *scr)
        if comm:
            last = functools.reduce(jnp.logical_and, [pl.program_id(a) == grid[a] - 1 for a in range(len(grid))])
            pl.when(last)(lambda: _run_comm(comm, "finish", cin, cout, csem))

    res = pl.pallas_call(
        wrapped, name=name, grid=grid,
        in_specs=list(in_specs) + [ANY] * len(c_ops), out_specs=list(out_specs) + [ANY] * len(c_outs),
        out_shape=_hbm_out(list(out_shape) + c_outs), scratch_shapes=list(scratch) + c_sems,
        input_output_aliases=io,
        compiler_params=pltpu.CompilerParams(vmem_limit_bytes=vmem_mb << 20,
                                             dimension_semantics=("arbitrary",) * len(grid),
                                             has_side_effects=bool(comm)),
    )(*_hbm(*operands, *c_ops))
    return list(res[:no]), _split(res[no:], comm, "outs")


def _comm_call(name, comm):
    c_ops = [a for p in comm for a in p.operands]
    c_outs = [s for p in comm for s in p.outs]
    c_sems = [s for p in comm for s in p.sems]
    io, off_i, off_o = {}, 0, 0
    for p in comm:
        for a, b in p.aliases.items():
            io[off_i + a] = off_o + b
        off_i += len(p.operands)
        off_o += len(p.outs)

    def body(*refs):
        cin, cout = refs[:len(c_ops)], refs[len(c_ops):len(c_ops) + len(c_outs)]
        csem = refs[len(c_ops) + len(c_outs):]
        _run_comm(comm, "start", cin, cout, csem)
        _run_comm(comm, "finish", cin, cout, csem)

    res = pl.pallas_call(
        body, name=name, in_specs=[ANY] * len(c_ops), out_specs=[ANY] * len(c_outs), out_shape=_hbm_out(c_outs),
        scratch_shapes=c_sems, input_output_aliases=io,
        compiler_params=pltpu.CompilerParams(has_side_effects=True),
    )(*_hbm(*c_ops))
    return _split(res, comm, "outs")


def _ffn_fwd(x, xb, w_in4, w_out2, g, b, tm, name, comm=()):
    T = x.shape[0]

    def body(x_ref, xb_ref, wg_ref, wu_ref, wo_ref, g_ref, b_ref, h_ref, r_ref, xo_ref, xob_ref, acc):
        k = pl.program_id(1)
        xv = xb_ref[...]
        gt = _nn(xv, wg_ref[...])
        up = _nn(xv, wu_ref[...])
        a = (gt * _sig(gt) * up).astype(BF16)
        h_ref[:, 0:FFH] = gt.astype(BF16)
        h_ref[:, FFH:2 * FFH] = up.astype(BF16)
        acc[...] = jnp.where(k == 0, 0.0, acc[...]) + _nn(a, wo_ref[...])

        @pl.when(k == 1)
        def _():
            r = ALPHA * x_ref[...] + 0.5 * acc[...]
            xhat, _ = _ln_stats(r)
            xo = xhat * g_ref[...] + b_ref[...]
            r_ref[...] = r
            xo_ref[...] = xo
            xob_ref[...] = xo.astype(BF16)

    tok = pl.BlockSpec((tm, D), lambda i, k: (i, 0))
    vec = pl.BlockSpec((1, D), lambda i, k: (0, 0))
    return _pcall(
        body, name=name, grid=(T // tm, 2),
        in_specs=[tok, tok,
                  pl.BlockSpec((None, D, FFH), lambda i, k: (k, 0, 0)),
                  pl.BlockSpec((None, D, FFH), lambda i, k: (k + 2, 0, 0)),
                  pl.BlockSpec((None, FFH, D), lambda i, k: (k, 0, 0)),
                  vec, vec],
        out_specs=[pl.BlockSpec((tm, FF), lambda i, k: (i, k)), tok, tok, tok],
        out_shape=[jax.ShapeDtypeStruct((T, 2 * FF), BF16), jax.ShapeDtypeStruct((T, D), F32),
                   jax.ShapeDtypeStruct((T, D), F32), jax.ShapeDtypeStruct((T, D), BF16)],
        scratch=[pltpu.VMEM((tm, D), F32)], vmem_mb=56, comm=comm,
        operands=(x, xb, w_in4, w_in4, w_out2, g, b))


def _ffn_bwd(dy, r, g, h, w_in4, w_out2, tm, name, comm=()):
    T = dy.shape[0]

    def body(dy_ref, r_ref, g_ref, h_ref, wg_ref, wu_ref, wo_ref,
             dx_ref, dh_ref, a_ref, df_ref, dg_ref, db_ref, acc, dr_s, dfb_s):
        i, k = pl.program_id(0), pl.program_id(1)

        @pl.when(k == 0)
        def _():
            dyv = dy_ref[...]
            dr, xhat = _ln_bwd(dyv, r_ref[...], g_ref[...])
            pg, pb = _rowsum(dyv * xhat), _rowsum(dyv)

            @pl.when(i == 0)
            def _():
                dg_ref[...] = pg
                db_ref[...] = pb

            @pl.when(i > 0)
            def _():
                dg_ref[...] += pg
                db_ref[...] += pb

            dr_s[...] = dr
            dfb = (0.5 * dr).astype(BF16)
            dfb_s[...] = dfb
            df_ref[...] = dfb

        da = _nt(dfb_s[...], wo_ref[...])
        gt = h_ref[:, 0:FFH].astype(F32)
        up = h_ref[:, FFH:2 * FFH].astype(F32)
        sg = _sig(gt)
        silu = gt * sg
        dgate = (da * up * (sg * (1.0 + gt * (1.0 - sg)))).astype(BF16)
        dup = (da * silu).astype(BF16)
        a_ref[...] = (silu * up).astype(BF16)
        dh_ref[:, 0:FFH] = dgate
        dh_ref[:, FFH:2 * FFH] = dup
        acc[...] = jnp.where(k == 0, 0.0, acc[...]) + _nt(dgate, wg_ref[...]) + _nt(dup, wu_ref[...])

        @pl.when(k == 1)
        def _():
            dx_ref[...] = ALPHA * dr_s[...] + acc[...]

    tok = pl.BlockSpec((tm, D), lambda i, k: (i, 0))
    vec = pl.BlockSpec((1, D), lambda i, k: (0, 0))
    wide = pl.BlockSpec((tm, FF), lambda i, k: (i, k))
    return _pcall(
        body, name=name, grid=(T // tm, 2),
        in_specs=[tok, tok, vec, wide,
                  pl.BlockSpec((None, D, FFH), lambda i, k: (k, 0, 0)),
                  pl.BlockSpec((None, D, FFH), lambda i, k: (k + 2, 0, 0)),
                  pl.BlockSpec((None, FFH, D), lambda i, k: (k, 0, 0))],
        out_specs=[tok, wide, pl.BlockSpec((tm, FFH), lambda i, k: (i, k)), tok, vec, vec],
        out_shape=[jax.ShapeDtypeStruct((T, D), F32), jax.ShapeDtypeStruct((T, 2 * FF), BF16),
                   jax.ShapeDtypeStruct((T, FF), BF16), jax.ShapeDtypeStruct((T, D), BF16),
                   jax.ShapeDtypeStruct((1, D), F32), jax.ShapeDtypeStruct((1, D), F32)],
        scratch=[pltpu.VMEM((tm, D), F32), pltpu.VMEM((tm, D), F32), pltpu.VMEM((tm, D), BF16)],
        vmem_mb=56, comm=comm, operands=(dy, r, g, h, w_in4, w_in4, w_out2))


def _mm_tn(a, b, tk, tn, name, shard_cols=None, interleaved=False, comm=()):
    T, K = a.shape
    N = b.shape[1]

    def body(a_ref, b_ref, o_ref):
        o_ref[...] = _tn(a_ref[...], b_ref[...])

    if shard_cols is None:
        out_shape = jax.ShapeDtypeStruct((K, N), F32)
        out_spec = pl.BlockSpec((tk, tn), lambda ki, nj: (ki, nj))
    else:
        per = shard_cols // tn

        def shard(nj):
            blk = nj // per
            return (blk % 2) * 2 + blk // 2 if interleaved else blk

        out_shape = jax.ShapeDtypeStruct((N // shard_cols, K, shard_cols), F32)
        out_spec = pl.BlockSpec((None, tk, tn), lambda ki, nj: (shard(nj), ki, nj % per))
    (out,), got = _pcall(
        body, name=name, grid=(K // tk, N // tn),
        in_specs=[pl.BlockSpec((T, tk), lambda ki, nj: (0, ki)), pl.BlockSpec((T, tn), lambda ki, nj: (0, nj))],
        out_specs=[out_spec], out_shape=[out_shape], comm=comm, operands=(a, b))
    return out, got


def _mix_fwd_a(xb, w_mix4, conv_w, conv_b, w_co4, tm, comm=()):
    T = xb.shape[0]

    def body(xb_ref, w_ref, cw_ref, cb_ref, wco_ref,
             pc_ref, z_ref, yin_ref, su_ref, sub_ref, gc_ref, gs_ref, yc_ref, qbuf):
        @pl.when(pl.program_id(0) == 0)
        def _():
            qbuf[pl.ds(0, 8), :] = jnp.zeros((8, CONV), F32)

        xv = xb_ref[...]
        p0 = _nn(xv, w_ref[0])
        p1 = _nn(xv, w_ref[1])
        gc_ref[...] = _nn(xv, w_ref[2])
        gs_ref[...] = _nn(xv, w_ref[3])
        cbv, ccv = p0[:, :CONV], p0[:, CONV:]
        chv, suv = p1[:, :CONV], p1[:, CONV:]
        q = ccv * chv
        qbuf[pl.ds(8, tm), :] = q
        cw = cw_ref[...]
        z = (cw[2:3] * q + cw[1:2] * qbuf[pl.ds(7, tm), :] + cw[0:1] * qbuf[pl.ds(6, tm), :]
             + cb_ref[...])
        qbuf[pl.ds(0, 8), :] = q[tm - 8:tm]
        yin = (cbv * z).astype(BF16)
        pc_ref[:, 0:CONV] = cbv.astype(BF16)
        pc_ref[:, CONV:2 * CONV] = ccv.astype(BF16)
        pc_ref[:, 2 * CONV:3 * CONV] = chv.astype(BF16)
        z_ref[...] = z.astype(BF16)
        yin_ref[...] = yin
        su_ref[...] = suv
        sub_ref[...] = suv.astype(BF16)
        for k in range(4):
            yc_ref[:, 256 * k:256 * (k + 1)] = _nn(yin, wco_ref[k])

    def tok(n):
        return pl.BlockSpec((tm, n), lambda i: (i, 0))

    def full(shape):
        return pl.BlockSpec(shape, lambda i: (0,) * len(shape))

    return _pcall(
        body, name="mix_fwd_a", grid=(T // tm,),
        in_specs=[tok(D), full((4, D, D)), full((3, CONV)), full((1, CONV)), full((4, CONV, 256))],
        out_specs=[tok(3 * CONV), tok(CONV), tok(CONV), tok(SSM), tok(SSM), tok(D), tok(D), tok(D)],
        out_shape=[jax.ShapeDtypeStruct((T, 3 * CONV), BF16), jax.ShapeDtypeStruct((T, CONV), BF16),
                   jax.ShapeDtypeStruct((T, CONV), BF16), jax.ShapeDtypeStruct((T, SSM), F32),
                   jax.ShapeDtypeStruct((T, SSM), BF16), jax.ShapeDtypeStruct((T, D), F32),
                   jax.ShapeDtypeStruct((T, D), F32), jax.ShapeDtypeStruct((T, D), F32)],
        scratch=[pltpu.VMEM((tm + 8, CONV), F32)], vmem_mb=56, comm=comm,
        operands=(xb, w_mix4, conv_w, conv_b, w_co4))


def _scan_rows(bre, bim, ar, ai, T, rev, load, out=None):
    R, W, G = SCAN_R, bre.shape[1], T // 8
    if rev:
        ai = -ai

    def cmul(pr, pi, xr, xi):
        return pr * xr - pi * xi, pr * xi + pi * xr

    pw = [(ar, ai)]
    for _ in range(7):
        pw.append(cmul(ar, ai, *pw[-1]))

    def shifted(v, d, axis, n, idx):
        if rev:
            return jnp.where(idx < n - d, pltpu.roll(v, n - d, axis), 0.0)
        return jnp.where(idx >= d, pltpu.roll(v, d, axis), 0.0)

    sub8 = lax.broadcasted_iota(jnp.int32, (8, W), 0)
    inside = {d: (sub8 < 8 - d) if rev else (sub8 >= d) for d in (1, 2, 4)}
    pm = {d: (jnp.where(inside[d], pw[d - 1][0], 0.0)[None], jnp.where(inside[d], pw[d - 1][1], 0.0)[None])
          for d in (1, 2, 4)}

    def step(i, _):
        t0 = pl.multiple_of(i * R, R)
        vr, vi = load(t0)
        vr, vi = vr.reshape(R // 8, 8, W), vi.reshape(R // 8, 8, W)
        for d in (1, 2, 4):
            sh = (8 - d) if rev else d
            dr, di = cmul(pm[d][0], pm[d][1], pltpu.roll(vr, sh, 1), pltpu.roll(vi, sh, 1))
            vr, vi = vr + dr, vi + di
        bre[pl.ds(t0 + 8, R), :] = vr.reshape(R, W)
        bim[pl.ds(t0 + 8, R), :] = vi.reshape(R, W)
        return 0

    lax.fori_loop(0, T // R, step, 0)

    edge = 0 if rev else 7
    cr = bre[pl.ds(8 + edge, G, stride=8), :]
    ci = bim[pl.ds(8 + edge, G, stride=8), :]
    row = lax.broadcasted_iota(jnp.int32, (G, W), 0)
    qr, qi = pw[7]
    d = 1
    while d < G:
        dr, di = cmul(qr, qi, shifted(cr, d, 0, G, row), shifted(ci, d, 0, G, row))
        cr, ci = cr + dr, ci + di
        qr, qi = qr * qr - qi * qi, 2.0 * qr * qi
        d *= 2

    nr, ni = shifted(cr, 1, 0, G, row), shifted(ci, 1, 0, G, row)
    for r in range(8):
        pr, pi = pw[7 - r] if rev else pw[r]
        dr, di = cmul(pr, pi, nr, ni)
        xr = bre[pl.ds(8 + r, G, stride=8), :] + dr
        xi = bim[pl.ds(8 + r, G, stride=8), :] + di
        if out is None:
            bre[pl.ds(8 + r, G, stride=8), :] = xr
            bim[pl.ds(8 + r, G, stride=8), :] = xi
        else:
            out[0][pl.ds(r, G, stride=8), :] = xr
            out[1][pl.ds(r, G, stride=8), :] = xi


def _scan_specs(T):
    W = SCAN_W
    lane = pl.BlockSpec((T, W), lambda j: (0, j))
    col = pl.BlockSpec((T, 128), lambda j: (0, j // SCAN_PER))
    wb = pl.BlockSpec((None, 128, W), lambda j: (j, 0, 0))
    wc = pl.BlockSpec((None, W, 128), lambda j: (j, 0, 0))
    vec = pl.BlockSpec((1, W), lambda j: (0, j))
    return lane, col, wb, wc, vec


def _s5_scan_fwd(su_b, wb_re, wb_im, a_re, a_im, comm=()):
    T = su_b.shape[0]
    W = SCAN_W

    def body(su_ref, wbr_ref, wbi_ref, ar_ref, ai_ref, sr_ref, si_ref, bre, bim):
        def load(t0):
            su = su_ref[pl.ds(t0, SCAN_R), :]
            return _nn(su, wbr_ref[...]), _nn(su, wbi_ref[...])

        _scan_rows(bre, bim, ar_ref[...], ai_ref[...], T, False, load, out=(sr_ref, si_ref))

    lane, col, wb, wc, vec = _scan_specs(T)
    return _pcall(
        body, name="s5_scan_fwd", grid=(LANES // W,),
        in_specs=[col, wb, wb, vec, vec],
        out_specs=[lane, lane],
        out_shape=[jax.ShapeDtypeStruct((T, LANES), F32)] * 2,
        scratch=[pltpu.VMEM((T + 16, W), F32)] * 2, comm=comm,
        operands=(su_b, wb_re, wb_im, a_re, a_im))


def _gelu(s):
    th = jnp.tanh(GELU_C * (s + 0.044715 * s * s * s))
    return 0.5 * s * (1.0 + th), th


def _mix_fwd_b(st_re, st_im, wc_re4, wc_im4, su, dvec, w_glu4, g_conv, g_ssm, y_conv, w_mo, x1, g, b, tm, comm=()):
    T = su.shape[0]

    def body(sr_ref, si_ref, wcr_ref, wci_ref, su_ref, d_ref, wg_ref, gc_ref, gs_ref, yc_ref, wmo_ref,
             x_ref, g_ref, b_ref, s_ref, sgb_ref, ga_ref, gb_ref, mb_ref, r_ref, xo_ref, xob_ref):
        srb = sr_ref[...].astype(BF16)
        sib = si_ref[...].astype(BF16)
        ys = [_nn(srb[:, 512 * J:512 * (J + 1)], wcr_ref[J]) + _nn(sib[:, 512 * J:512 * (J + 1)], wci_ref[J])
              for J in range(4)]
        s = jnp.concatenate(ys, axis=1) + d_ref[...] * su_ref[...]
        sg, _ = _gelu(s)
        sgb = sg.astype(BF16)
        ga = jnp.concatenate([_nn(sgb, wg_ref[0]), _nn(sgb, wg_ref[1])], axis=1)
        gb = jnp.concatenate([_nn(sgb, wg_ref[2]), _nn(sgb, wg_ref[3])], axis=1)
        merged = _sig(gc_ref[...]) * yc_ref[...] + _sig(gs_ref[...]) * (ga * _sig(gb))
        mb = merged.astype(BF16)
        r = ALPHA * x_ref[...] + _nn(mb, wmo_ref[...])
        xhat, _ = _ln_stats(r)
        xo = xhat * g_ref[...] + b_ref[...]
        s_ref[...] = s
        sgb_ref[...] = sgb
        ga_ref[...] = ga
        gb_ref[...] = gb
        mb_ref[...] = mb
        r_ref[...] = r
        xo_ref[...] = xo
        xob_ref[...] = xo.astype(BF16)

    def tok(n):
        return pl.BlockSpec((tm, n), lambda i: (i, 0))

    def full(shape):
        return pl.BlockSpec(shape, lambda i: (0,) * len(shape))

    return _pcall(
        body, name="mix_fwd_b", grid=(T // tm,),
        in_specs=[tok(LANES), tok(LANES), full((4, 512, 128)), full((4, 512, 128)), tok(SSM), full((1, SSM)),
                  full((4, SSM, 512)), tok(D), tok(D), tok(D), full((D, D)), tok(D), full((1, D)), full((1, D))],
        out_specs=[tok(SSM), tok(SSM), tok(D), tok(D), tok(D), tok(D), tok(D), tok(D)],
        out_shape=[jax.ShapeDtypeStruct((T, SSM), F32), jax.ShapeDtypeStruct((T, SSM), BF16),
                   jax.ShapeDtypeStruct((T, D), F32), jax.ShapeDtypeStruct((T, D), F32),
                   jax.ShapeDtypeStruct((T, D), BF16), jax.ShapeDtypeStruct((T, D), F32),
                   jax.ShapeDtypeStruct((T, D), F32), jax.ShapeDtypeStruct((T, D), BF16)],
        vmem_mb=56, comm=comm,
        operands=(st_re, st_im, wc_re4, wc_im4, su, dvec, w_glu4, g_conv, g_ssm, y_conv, w_mo, x1, g, b))


def _ple_loss(x3, x3b, p, w_pi4, w_pg, g, b, target, tm):
    T = x3.shape[0]
    PD = p.shape[1]

    def body(x_ref, xb_ref, p_ref, wpi_ref, wpg_ref, g_ref, b_ref, t_ref,
             loss_ref, dx_ref, pb_ref, dpw_ref, dgt_ref, dg_ref, db_ref):
        i = pl.program_id(0)
        pb = p_ref[...].astype(BF16)
        pw = jnp.concatenate([_nn(pb, wpi_ref[k]) for k in range(4)], axis=1)
        gt = _nn(xb_ref[...], wpg_ref[...])
        sg = _sig(gt)
        r = ALPHA * x_ref[...] + pw * sg
        gv = g_ref[...]
        xhat, rstd = _ln_stats(r)
        err = xhat * gv + b_ref[...] - t_ref[...]
        lpart = jnp.zeros((1, 128), F32) + 0.5 * jnp.sum(jnp.mean(err * err, axis=-1, keepdims=True))
        dy = err * (1.0 / D)
        dyg = dy * gv
        m1 = jnp.mean(dyg, axis=-1, keepdims=True)
        m2 = jnp.mean(dyg * xhat, axis=-1, keepdims=True)
        dr = rstd * (dyg - m1 - xhat * m2)
        pg, pbias = _rowsum(dy * xhat), _rowsum(dy)

        @pl.when(i == 0)
        def _():
            loss_ref[...] = lpart
            dg_ref[...] = pg
            db_ref[...] = pbias

        @pl.when(i > 0)
        def _():
            loss_ref[...] += lpart
            dg_ref[...] += pg
            db_ref[...] += pbias

        dgt = (dr * pw * sg * (1.0 - sg)).astype(BF16)
        pb_ref[...] = pb
        dpw_ref[...] = (dr * sg).astype(BF16)
        dgt_ref[...] = dgt
        dx_ref[...] = ALPHA * dr + _nt(dgt, wpg_ref[...])

    def tok(n):
        return pl.BlockSpec((tm, n), lambda i: (i, 0))

    def full(shape):
        return pl.BlockSpec(shape, lambda i: (0,) * len(shape))

    return pl.pallas_call(
        body, name="ple_loss", grid=(T // tm,),
        in_specs=[tok(D), tok(D), tok(PD), full((4, PD, 256)), full((D, D)), full((1, D)), full((1, D)), tok(D)],
        out_specs=[full((1, 128)), tok(D), tok(PD), tok(D), tok(D), full((1, D)), full((1, D))],
        out_shape=_hbm_out([jax.ShapeDtypeStruct((1, 128), F32), jax.ShapeDtypeStruct((T, D), F32),
                            jax.ShapeDtypeStruct((T, PD), BF16), jax.ShapeDtypeStruct((T, D), BF16),
                            jax.ShapeDtypeStruct((T, D), BF16), jax.ShapeDtypeStruct((1, D), F32),
                            jax.ShapeDtypeStruct((1, D), F32)]),
        compiler_params=_cp(48, 1),
    )(*_hbm(x3, x3b, p, w_pi4, w_pg, g, b, target))


def _mix_bwd_b(dy, r2, g, w_mo, g_conv, g_ssm, y_conv, ga, gb, s, su, dvec, w_glu4, wc_re4, wc_im4, tm, comm=()):
    T = dy.shape[0]

    def body(dy_ref, r_ref, g_ref, wmo_ref, gc_ref, gs_ref, yc_ref, ga_ref, gb_ref, s_ref, su_ref, d_ref,
             wg_ref, wcr_ref, wci_ref,
             dres_ref, dmix_ref, dgl_ref, dsb_ref, dud_ref, gsr_ref, gsi_ref, dyc_ref, dp_ref,
             dg_ref, db_ref, dd_ref):
        i = pl.program_id(0)
        dyv = dy_ref[...]
        dr, xhat = _ln_bwd(dyv, r_ref[...], g_ref[...])
        dmix = dr.astype(BF16)
        dmerged = _nt(dmix, wmo_ref[...])
        sc, ss, sgb = _sig(gc_ref[...]), _sig(gs_ref[...]), _sig(gb_ref[...])
        gav = ga_ref[...]
        yssm = gav * sgb
        dgc = dmerged * yc_ref[...] * sc * (1.0 - sc)
        dgss = dmerged * yssm * ss * (1.0 - ss)
        dyssm = dmerged * ss
        dgl = jnp.concatenate([dyssm * sgb, dyssm * gav * sgb * (1.0 - sgb)], axis=1).astype(BF16)
        dsg = (_nt(dgl[:, 0:512], wg_ref[0]) + _nt(dgl[:, 512:1024], wg_ref[1])
               + _nt(dgl[:, 1024:1536], wg_ref[2]) + _nt(dgl[:, 1536:2048], wg_ref[3]))
        sv = s_ref[...]
        _, th = _gelu(sv)
        dgelu = 0.5 * (1.0 + th) + 0.5 * sv * (1.0 - th * th) * GELU_C * (1.0 + 3.0 * 0.044715 * sv * sv)
        ds = dsg * dgelu
        dsb = ds.astype(BF16)
        pg, pb, pd = _rowsum(dyv * xhat), _rowsum(dyv), _rowsum(ds * su_ref[...])

        @pl.when(i == 0)
        def _():
            dg_ref[...] = pg
            db_ref[...] = pb
            dd_ref[...] = pd

        @pl.when(i > 0)
        def _():
            dg_ref[...] += pg
            db_ref[...] += pb
            dd_ref[...] += pd

        dres_ref[...] = ALPHA * dr
        dmix_ref[...] = dmix
        dgl_ref[...] = dgl
        dsb_ref[...] = dsb
        dud_ref[...] = ds * d_ref[...]
        for J in range(4):
            gsr_ref[:, 512 * J:512 * (J + 1)] = _nt(dsb[:, 128 * J:128 * (J + 1)], wcr_ref[J])
            gsi_ref[:, 512 * J:512 * (J + 1)] = _nt(dsb[:, 128 * J:128 * (J + 1)], wci_ref[J])
        dyc_ref[...] = (dmerged * sc).astype(BF16)
        dp_ref[:, 0:D] = dgc.astype(BF16)
        dp_ref[:, D:2 * D] = dgss.astype(BF16)

    def tok(n):
        return pl.BlockSpec((tm, n), lambda i: (i, 0))

    def full(shape):
        return pl.BlockSpec(shape, lambda i: (0,) * len(shape))

    return _pcall(
        body, name="mix_bwd_b", grid=(T // tm,),
        in_specs=[tok(D), tok(D), full((1, D)), full((D, D)), tok(D), tok(D), tok(D), tok(D), tok(D),
                  tok(SSM), tok(SSM), full((1, SSM)), full((4, SSM, 512)), full((4, 512, 128)), full((4, 512, 128))],
        out_specs=[tok(D), tok(D), tok(2 * D), tok(SSM), tok(SSM), tok(LANES), tok(LANES), tok(D),
                   pl.BlockSpec((tm, 2 * D), lambda i: (i, 1)), full((1, D)), full((1, D)), full((1, SSM))],
        out_shape=[jax.ShapeDtypeStruct((T, D), F32), jax.ShapeDtypeStruct((T, D), BF16),
                   jax.ShapeDtypeStruct((T, 2 * D), BF16), jax.ShapeDtypeStruct((T, SSM), BF16),
                   jax.ShapeDtypeStruct((T, SSM), F32), jax.ShapeDtypeStruct((T, LANES), F32),
                   jax.ShapeDtypeStruct((T, LANES), F32), jax.ShapeDtypeStruct((T, D), BF16),
                   jax.ShapeDtypeStruct((T, 4 * D), BF16), jax.ShapeDtypeStruct((1, D), F32),
                   jax.ShapeDtypeStruct((1, D), F32), jax.ShapeDtypeStruct((1, SSM), F32)],
        vmem_mb=56, comm=comm,
        operands=(dy, r2, g, w_mo, g_conv, g_ssm, y_conv, ga, gb, s, su, dvec, w_glu4, wc_re4, wc_im4))


def _s5_scan_bwd(gs_re, gs_im, st_re, st_im, su_b, ds_b, wb_re, wb_im, a_re, a_im, comm=()):
    T = su_b.shape[0]
    W = SCAN_W
    R = SCAN_R

    def body(gr_ref, gi_ref, sr_ref, si_ref, su_ref, ds_ref, wbr_ref, wbi_ref, ar_ref, ai_ref,
             dsu_ref, dwbr_ref, dwbi_ref, dwcr_ref, dwci_ref, dar_ref, dai_ref, gre, gim):
        j = pl.program_id(0)
        zero = jnp.zeros((8, W), F32)
        for buf in (gre, gim):
            buf[pl.ds(T + 8, 8), :] = zero
        _scan_rows(gre, gim, ar_ref[...], ai_ref[...], T, True,
                   lambda t0: (gr_ref[pl.ds(t0, R), :], gi_ref[pl.ds(t0, R), :]))
        grb = gre[pl.ds(8, T), :].astype(BF16)
        gib = gim[pl.ds(8, T), :].astype(BF16)
        part = _nt(grb, wbr_ref[...]) + _nt(gib, wbi_ref[...])

        @pl.when(j % SCAN_PER == 0)
        def _():
            dsu_ref[...] = part

        @pl.when(j % SCAN_PER > 0)
        def _():
            dsu_ref[...] += part

        su = su_ref[...]
        dwbr_ref[...] = _tn(su, grb)
        dwbi_ref[...] = _tn(su, gib)
        dsv = ds_ref[...]
        dwcr_ref[...] = _tn(sr_ref[...].astype(BF16), dsv)
        dwci_ref[...] = _tn(si_ref[...].astype(BF16), dsv)
        dar = jnp.zeros((1, W), F32)
        dai = jnp.zeros((1, W), F32)
        for c in range(T // R):
            xr = sr_ref[pl.ds(c * R, R), :]
            xi = si_ref[pl.ds(c * R, R), :]
            g1r = gre[pl.ds(c * R + 9, R), :]
            g1i = gim[pl.ds(c * R + 9, R), :]
            dar = dar + _rowsum(g1r * xr + g1i * xi)
            dai = dai + _rowsum(g1i * xr - g1r * xi)
        dar_ref[...] = dar
        dai_ref[...] = dai

    lane, col, wb, wc, vec = _scan_specs(T)
    return _pcall(
        body, name="s5_scan_bwd", grid=(LANES // W,),
        in_specs=[lane, lane, lane, lane, col, col, wb, wb, vec, vec],
        out_specs=[col, wb, wb, wc, wc, vec, vec],
        out_shape=[jax.ShapeDtypeStruct((T, SSM), F32),
                   jax.ShapeDtypeStruct((LANES // W, 128, W), F32), jax.ShapeDtypeStruct((LANES // W, 128, W), F32),
                   jax.ShapeDtypeStruct((LANES // W, W, 128), F32), jax.ShapeDtypeStruct((LANES // W, W, 128), F32),
                   jax.ShapeDtypeStruct((1, LANES), F32), jax.ShapeDtypeStruct((1, LANES), F32)],
        scratch=[pltpu.VMEM((T + 16, W), F32)] * 2, vmem_mb=56, comm=comm,
        operands=(gs_re, gs_im, st_re, st_im, su_b, ds_b, wb_re, wb_im, a_re, a_im))


def _mix_bwd_a(dyc_b, w_co4, pc, z_b, conv_w, dsu_ssm, du_dir, dproj, dres, w_mix4, tm, comm=()):
    T = dres.shape[0]
    nt = T // tm

    def body(dyc_ref, wco_ref, pc_ref, halo_ref, z_ref, cw_ref, dsu_ref, dud_ref, dpin_ref, dres_ref, w_ref,
             dp_ref, dx_ref, dcw_ref, dcb_ref, dzbuf, qbuf):
        i = pl.program_id(0)
        ii = nt - 1 - i

        @pl.when(i == 0)
        def _():
            dzbuf[pl.ds(tm, 8), :] = jnp.zeros((8, CONV), F32)

        dyc = dyc_ref[...]
        dyin = (_nt(dyc[:, 0:256], wco_ref[0]) + _nt(dyc[:, 256:512], wco_ref[1])
                + _nt(dyc[:, 512:768], wco_ref[2]) + _nt(dyc[:, 768:1024], wco_ref[3]))
        cbv = pc_ref[:, 0:CONV].astype(F32)
        ccv = pc_ref[:, CONV:2 * CONV].astype(F32)
        chv = pc_ref[:, 2 * CONV:3 * CONV].astype(F32)
        dcbv = dyin * z_ref[...].astype(F32)
        dz = dyin * cbv
        dzbuf[pl.ds(0, tm), :] = dz
        cw = cw_ref[...]
        dq = cw[2:3] * dz + cw[1:2] * dzbuf[pl.ds(1, tm), :] + cw[0:1] * dzbuf[pl.ds(2, tm), :]
        dzbuf[pl.ds(tm, 8), :] = dz[0:8]
        q = ccv * chv
        hq = halo_ref[:, CONV:2 * CONV].astype(F32) * halo_ref[:, 2 * CONV:3 * CONV].astype(F32)
        qbuf[pl.ds(0, 8), :] = jnp.where(ii > 0, hq, jnp.zeros_like(hq))
        qbuf[pl.ds(8, tm), :] = q
        pw = jnp.concatenate([_rowsum(dz * qbuf[pl.ds(6, tm), :]), _rowsum(dz * qbuf[pl.ds(7, tm), :]),
                              _rowsum(dz * q), jnp.zeros((5, CONV), F32)], axis=0)
        pbias = _rowsum(dz)

        @pl.when(i == 0)
        def _():
            dcw_ref[...] = pw
            dcb_ref[...] = pbias

        @pl.when(i > 0)
        def _():
            dcw_ref[...] += pw
            dcb_ref[...] += pbias

        dp0 = jnp.concatenate([dcbv, dq * chv], axis=1).astype(BF16)
        dp1 = jnp.concatenate([dq * ccv, dsu_ref[...] + dud_ref[...]], axis=1).astype(BF16)
        dp_ref[:, 0:D] = dp0
        dp_ref[:, D:2 * D] = dp1
        dx_ref[...] = (dres_ref[...] + _nt(dp0, w_ref[0]) + _nt(dp1, w_ref[1])
                       + _nt(dpin_ref[:, 0:D], w_ref[2]) + _nt(dpin_ref[:, D:2 * D], w_ref[3]))

    def tok(n):
        return pl.BlockSpec((tm, n), lambda i: (nt - 1 - i, 0))

    def full(shape):
        return pl.BlockSpec(shape, lambda i: (0,) * len(shape))

    halo = pl.BlockSpec((8, 3 * CONV), lambda i: (jnp.maximum((nt - 1 - i) * (tm // 8) - 1, 0), 0))
    return _pcall(
        body, name="mix_bwd_a", grid=(nt,),
        in_specs=[tok(D), full((4, CONV, 256)), tok(3 * CONV), halo, tok(CONV), full((3, CONV)),
                  tok(SSM), tok(SSM), pl.BlockSpec((tm, 2 * D), lambda i: (nt - 1 - i, 1)), tok(D),
                  full((4, D, D))],
        out_specs=[pl.BlockSpec((tm, 2 * D), lambda i: (nt - 1 - i, 0)), tok(D), full((8, CONV)), full((1, CONV))],
        out_shape=[jax.ShapeDtypeStruct((T, 4 * D), BF16), jax.ShapeDtypeStruct((T, D), F32),
                   jax.ShapeDtypeStruct((8, CONV), F32), jax.ShapeDtypeStruct((1, CONV), F32)],
        scratch=[pltpu.VMEM((tm + 8, CONV), F32), pltpu.VMEM((tm + 8, CONV), F32)],
        aliases={8: 0}, vmem_mb=56, comm=comm,
        operands=(dyc_b, w_co4, pc, pc, z_b, conv_w, dsu_ssm, du_dir, dproj, dres, w_mix4))


def _zoh(lam_re, lam_im, log_step, b_re, b_im):
    dt = jnp.exp(log_step)[:, None]
    mag = jnp.exp(lam_re * dt)
    abr, abi = mag * jnp.cos(lam_im * dt), mag * jnp.sin(lam_im * dt)
    nr, ni = abr - 1.0, abi
    den = lam_re * lam_re + lam_im * lam_im
    cr = (nr * lam_re + ni * lam_im) / den
    ci = (ni * lam_re - nr * lam_im) / den
    bbr = cr[..., None] * b_re - ci[..., None] * b_im
    bbi = cr[..., None] * b_im + ci[..., None] * b_re
    return abr, abi, bbr, bbi


_WB_MASK = (np.arange(8)[None, :, None]
            == SCAN_GR * np.arange(SCAN_PER)[:, None, None] + np.arange(SCAN_GR)[None, None, :]).astype(np.float32)
_EYE8 = np.eye(8, dtype=np.float32)


def _wb_blocks(bb):
    bt = bb.transpose(0, 2, 1).reshape(4, 1, 8, 16, 1, STATE)
    full = bt * _WB_MASK[None, :, :, None, :, None]
    return full.reshape(LANES // SCAN_W, 128, SCAN_W).astype(BF16)


def _wc_blocks(cc):
    ct = cc.transpose(0, 2, 1).reshape(4, 8, STATE, 1, 16)
    full = ct * _EYE8[None, :, None, :, None]
    return full.reshape(4, 512, 128).astype(BF16)


def _wb_diag(dwb):
    d6 = dwb.reshape(4, SCAN_PER, 8, 16, SCAN_GR, STATE) * _WB_MASK[None, :, :, None, :, None]
    return d6.sum(axis=(1, 4)).reshape(GROUPS, 16, STATE).transpose(0, 2, 1)


def _wc_diag(dwc):
    mask = _WB_MASK.transpose(0, 2, 1)
    d6 = dwc.reshape(4, SCAN_PER, SCAN_GR, STATE, 8, 16) * mask[None, :, :, None, :, None]
    return d6.sum(axis=4).reshape(GROUPS, STATE, 16).transpose(0, 2, 1)


def _where():
    x, y, c = lax.axis_index("x"), lax.axis_index("y"), lax.axis_index("c")
    return x, y, c, 2 * x + y


def _chip_dev(k, c):
    return (k // 2, k % 2, c)


def _slot_cast(meidx, w, dtype, name, token=()):
    R, C = w.shape
    tr = _row_tile(R)

    def body(m_ref, w_ref, *rest):
        rest[-1][...] = w_ref[...].astype(dtype)

    gs = pltpu.PrefetchScalarGridSpec(
        num_scalar_prefetch=1, grid=(R // tr,),
        in_specs=[pl.BlockSpec((tr, C), lambda i, m: (i, 0))] + [pl.BlockSpec((8, 128), lambda i, m: (0, 0))] * len(token),
        out_specs=pl.BlockSpec((None, tr, C), lambda i, m: (m[0], i, 0)))
    return pl.pallas_call(
        body, name=name, grid_spec=gs, out_shape=_hbm_out(jax.ShapeDtypeStruct((4, R, C), dtype)),
        compiler_params=_cp(32, 1),
    )(meidx, *_hbm(w), *token)


def _gather_ici_payload(bufs):
    def copies(ins, lnd, ss, rs):
        x, y, c, me = _where()
        cps = []
        for w, b in enumerate(bufs):
            h = b.shape[1] // 2
            mine = lnd[w].at[me, pl.ds(c * h, h)]
            for s in range(3):
                k = (me + 1 + s) % 4
                cps.append(pltpu.make_async_remote_copy(
                    src_ref=mine, dst_ref=mine, send_sem=ss.at[3 * w + s], recv_sem=rs.at[3 * w + s],
                    device_id=_chip_dev(k, c), device_id_type=MESH))
        return cps

    p = _sym_payload([], [jax.ShapeDtypeStruct(b.shape, b.dtype) for b in bufs], copies, 3 * len(bufs))
    p.lands = list(bufs)
    return p


def _gather_pass_payload(bufs):
    def copies(ins, outs, ss, rs):
        x, y, c, me = _where()
        cps = []
        for w, b in enumerate(bufs):
            h = b.shape[1] // 2
            for s in range(3):
                j = (me + 1 + s) % 4
                cps.append(pltpu.make_async_remote_copy(
                    src_ref=ins[w].at[j, pl.ds(c * h, h)], dst_ref=outs[w].at[j, pl.ds(c * h, h)],
                    send_sem=ss.at[3 * w + s], recv_sem=rs.at[3 * w + s], device_id=(x, y, 1 - c),
                    device_id_type=MESH))
        return cps

    p = _sym_payload(bufs, [jax.ShapeDtypeStruct(b.shape, b.dtype) for b in bufs], copies, 3 * len(bufs))
    p.aliases = {w: w for w in range(len(bufs))}
    return p


def _gather_payload(bufs):
    n = len(bufs)

    def half(ref, w, k, cc):
        h = bufs[w].shape[1] // 2
        return ref.at[k, pl.ds(cc * h, h)]

    def ici(ins, outs, sems, w, s):
        x, y, c, me = _where()
        k = (me + 1 + s) % 4
        return pltpu.make_async_remote_copy(
            src_ref=half(ins[w], w, me, c), dst_ref=half(outs[w], w, me, c), send_sem=sems[0].at[3 * w + s],
            recv_sem=sems[1].at[3 * w + s], device_id=_chip_dev(k, c), device_id_type=MESH)

    def landed(outs, sems, w, s):
        x, y, c, me = _where()
        j = (me + 3 - s) % 4
        return pltpu.make_async_remote_copy(
            src_ref=half(outs[w], w, j, c), dst_ref=half(outs[w], w, j, c), send_sem=sems[0].at[3 * w + s],
            recv_sem=sems[1].at[3 * w + s], device_id=(x, y, 1 - c), device_id_type=MESH)

    def passed(outs, sems, w, s, cc):
        x, y, c, me = _where()
        j = (me + 3 - s) % 4
        return pltpu.make_async_remote_copy(
            src_ref=half(outs[w], w, j, cc), dst_ref=half(outs[w], w, j, cc), send_sem=sems[2].at[3 * w + s],
            recv_sem=sems[3].at[3 * w + s], device_id=(x, y, 1 - c), device_id_type=MESH)

    pairs = [(w, s) for w in range(n) for s in range(3)]

    def start(ins, outs, sems):
        for w, s in pairs:
            ici(ins, outs, sems, w, s).start()

    def finish(ins, outs, sems):
        _, _, c, _ = _where()
        for w, s in pairs:
            landed(outs, sems, w, s).wait_recv()
            passed(outs, sems, w, s, c).start()
        for w, s in pairs:
            passed(outs, sems, w, s, 1 - c).wait_recv()
        for w, s in pairs:
            ici(ins, outs, sems, w, s).wait_send()
            passed(outs, sems, w, s, c).wait_send()

    return _Payload(bufs, [jax.ShapeDtypeStruct(b.shape, b.dtype) for b in bufs], {w: w for w in range(n)},
                    [pltpu.SemaphoreType.DMA((3 * n,))] * 4, start, finish)


def _sym_payload(operands, outs, copies, n_copies):
    def start(ins, outs_, sems):
        for cp in copies(ins, outs_, sems[0], sems[1]):
            cp.start()

    def finish(ins, outs_, sems):
        for cp in copies(ins, outs_, sems[0], sems[1]):
            cp.wait()

    p = _Payload(operands, outs, {}, [pltpu.SemaphoreType.DMA((n_copies,))] * 2, start, finish)
    p.copies, p.n_copies = copies, n_copies
    return p


def _swap_payload(g4s):
    def copies(ins, outs, ss, rs):
        x, y, c, me = _where()
        cps = []
        for w, g in enumerate(g4s):
            h = g.shape[1] // 2
            cps.append(pltpu.make_async_remote_copy(
                src_ref=ins[w].at[:, pl.ds((1 - c) * h, h)], dst_ref=outs[w], send_sem=ss.at[w],
                recv_sem=rs.at[w], device_id=(x, y, 1 - c), device_id_type=MESH))
        return cps

    outs = [jax.ShapeDtypeStruct((4, g.shape[1] // 2, g.shape[2]), g.dtype) for g in g4s]
    return _sym_payload(g4s, outs, copies, len(g4s))


def _exchange_payload(pbs):
    def copies(ins, outs, ss, rs):
        x, y, c, me = _where()
        cps = []
        for w in range(len(pbs)):
            for s in range(3):
                k = (me + 1 + s) % 4
                cps.append(pltpu.make_async_remote_copy(
                    src_ref=ins[w].at[k], dst_ref=outs[w].at[2 - s], send_sem=ss.at[3 * w + s],
                    recv_sem=rs.at[3 * w + s], device_id=_chip_dev(k, c), device_id_type=MESH))
        return cps

    outs = [jax.ShapeDtypeStruct((3,) + p.shape[1:], p.dtype) for p in pbs]
    return _sym_payload(pbs, outs, copies, 3 * len(pbs))


HBM_REF = pl.BlockSpec(memory_space=pltpu.HBM)
SEM_REF = pl.BlockSpec(memory_space=pltpu.SEMAPHORE)
DATAFLOW = pltpu.SideEffectType.DATAFLOW_SIDE_EFFECTING


class _SemList:
    def __init__(self, refs):
        self.refs = refs

    @property
    def at(self):
        return self.refs


def _split_start(p, name):
    n_in, n_out, nc = len(p.operands), len(p.outs), p.n_copies
    lands = getattr(p, "lands", None) or [lax.empty(s.shape, s.dtype) for s in p.outs]

    def body(*refs):
        ins, lnd = refs[:n_in], refs[n_in:n_in + n_out]
        sems = refs[n_in + n_out:n_in + n_out + 2 * nc]
        for cp in p.copies(ins, lnd, _SemList(sems[:nc]), _SemList(sems[nc:])):
            cp.start()
        refs[-1][...] = jnp.zeros((8, 128), F32)

    res = pl.pallas_call(
        body, name=name,
        in_specs=[HBM_REF] * (n_in + n_out),
        out_specs=[SEM_REF] * (2 * nc) + [HBM_REF] * (n_in + n_out) + [VMEM_FULL],
        out_shape=([pltpu.SemaphoreType.DMA(())] * (2 * nc) + _hbm_out(p.operands) + _hbm_out(lands)
                   + [jax.ShapeDtypeStruct((8, 128), F32)]),
        input_output_aliases={i: 2 * nc + i for i in range(n_in + n_out)},
        compiler_params=pltpu.CompilerParams(has_side_effects=DATAFLOW),
    )(*_hbm(*p.operands, *lands))
    k = 2 * nc
    return list(res[:k]), list(res[k:k + n_in]), list(res[k + n_in:k + n_in + n_out]), res[-1]


def _split_wait(p, handle, after, name):
    sems, srcs, lands, _ = handle
    n_in, n_out, nc = len(srcs), len(lands), p.n_copies

    def body(*refs):
        ins, lnd = refs[:n_in], refs[n_in:n_in + n_out]
        sm = refs[n_in + n_out:n_in + n_out + 2 * nc]
        for cp in p.copies(ins, lnd, _SemList(sm[:nc]), _SemList(sm[nc:])):
            cp.wait_send()
            cp.wait_recv()

    res = pl.pallas_call(
        body, name=name,
        in_specs=[HBM_REF] * (n_in + n_out) + [SEM_REF] * (2 * nc) + [ANY] * len(after),
        out_specs=[HBM_REF] * (n_in + n_out), out_shape=_hbm_out(srcs) + _hbm_out(lands),
        input_output_aliases={i: i for i in range(n_in + n_out)},
        compiler_params=pltpu.CompilerParams(has_side_effects=DATAFLOW),
    )(*srcs, *lands, *sems, *after)
    return list(res[:n_in]), list(res[n_in:])


def _join_payload(halves):
    def copies(ins, outs, ss, rs):
        x, y, c, me = _where()
        return [pltpu.make_async_remote_copy(
            src_ref=ins[w], dst_ref=outs[w], send_sem=ss.at[w], recv_sem=rs.at[w],
            device_id=(x, y, 1 - c), device_id_type=MESH) for w in range(len(halves))]

    outs = [jax.ShapeDtypeStruct(a.shape, a.dtype) for a in halves]
    return _sym_payload(halves, outs, copies, len(halves))


def _allgather_payload(v):
    def copies(ins, outs, ss, rs):
        x, y, c, me = _where()
        lin = 4 * x + 2 * y + c
        cps = []
        for o in range(1, 8):
            t = (lin + o) % 8
            cps.append(pltpu.make_async_remote_copy(
                src_ref=ins[0], dst_ref=outs[0].at[lin], send_sem=ss.at[o - 1], recv_sem=rs.at[o - 1],
                device_id=(t // 4, (t // 2) % 2, t % 2), device_id_type=MESH))
        return cps

    p = _sym_payload([v], [jax.ShapeDtypeStruct((8,) + v.shape, v.dtype)], copies, 7)
    x, y, c, _ = _where()
    p.lands = [lax.dynamic_update_slice(jnp.zeros((8,) + v.shape, v.dtype), v[None], (4 * x + 2 * y + c, 0, 0))]
    return p


def _sum8(buf, token):
    _, P, C = buf.shape

    def body(b_ref, t_ref, o_ref):
        acc = b_ref[0]
        for d in range(1, 8):
            acc = acc + b_ref[d]
        o_ref[...] = acc

    return pl.pallas_call(
        body, name="sum8", in_specs=[VMEM_FULL, VMEM_FULL], out_specs=VMEM_FULL,
        out_shape=jax.ShapeDtypeStruct((P, C), F32),
        compiler_params=pltpu.CompilerParams(vmem_limit_bytes=32 << 20),
    )(buf, token)


def _row_tile(h):
    for t in (256, 176, 128, 64, 32, 16, 8):
        if h % t == 0:
            return t
    raise ValueError(h)


def _pair_sum(cmidx, g4, got, name):
    _, R, C = g4.shape
    h = R // 2
    th = _row_tile(h)

    def body(cm_ref, a_ref, b_ref, o_ref, ob_ref):
        sm = a_ref[...] + b_ref[...]
        ob_ref[...] = sm.astype(BF16)

        @pl.when(pl.program_id(1) == cm_ref[1])
        def _():
            o_ref[...] = sm

    blk = pl.BlockSpec((None, th, C), lambda i, k, cm: (k, i, 0))
    gs = pltpu.PrefetchScalarGridSpec(
        num_scalar_prefetch=1, grid=(h // th, 4),
        in_specs=[pl.BlockSpec((None, None, th, C), lambda i, k, cm: (k, cm[0], i, 0)), blk],
        out_specs=[pl.BlockSpec((th, C), lambda i, k, cm: (i, 0)), blk])
    return pl.pallas_call(
        body, name=name, grid_spec=gs,
        out_shape=_hbm_out([jax.ShapeDtypeStruct((h, C), F32), jax.ShapeDtypeStruct((4, h, C), BF16)]),
        compiler_params=_cp(32, 2),
    )(cmidx, *_hbm(g4.reshape(4, 2, h, C), got))


def _chip_sum(own, got, name):
    h, C = own.shape
    th = _row_tile(h)

    def body(a_ref, b_ref, o_ref):
        o_ref[...] = ((a_ref[...] + b_ref[0].astype(F32)) + b_ref[1].astype(F32)) + b_ref[2].astype(F32)

    return pl.pallas_call(
        body, name=name, grid=(h // th,),
        in_specs=[pl.BlockSpec((th, C), lambda i: (i, 0)), pl.BlockSpec((3, th, C), lambda i: (0, i, 0))],
        out_specs=pl.BlockSpec((th, C), lambda i: (i, 0)),
        out_shape=_hbm_out(jax.ShapeDtypeStruct((h, C), F32)),
        compiler_params=_cp(32, 1),
    )(*_hbm(own, got))


def _adamw_math(w, g, m, v):
    m2 = B1 * m + (1.0 - B1) * g
    v2 = B2 * v + (1.0 - B2) * (g * g)
    m_hat = m2 / (1.0 - B1 ** STEP)
    v_hat = v2 / (1.0 - B2 ** STEP)
    delta = -LR * (m_hat / (jnp.sqrt(v_hat) + EPS) + WD * w)
    return delta, m2, v2


def _adamw_pair(cidx, w, mine, theirs, m, v, token, name):
    R, C = w.shape
    h = R // 2
    tr = _row_tile(h)
    nh = h // tr

    def body(c_ref, w_ref, a_ref, b_ref, m_ref, v_ref, t_ref, g_ref, d_ref, mo_ref, vo_ref):
        own = (pl.program_id(0) // nh) == c_ref[0]
        g = jnp.where(own, a_ref[...], b_ref[...])
        d, m2, v2 = _adamw_math(w_ref[...], g, m_ref[...], v_ref[...])
        g_ref[...] = g
        d_ref[...] = d
        mo_ref[...] = m2
        vo_ref[...] = v2

    blk = pl.BlockSpec((tr, C), lambda i, c: (i, 0))
    mine_blk = pl.BlockSpec((tr, C), lambda i, c: (jnp.clip(i - c[0] * nh, 0, nh - 1), 0))
    theirs_blk = pl.BlockSpec((tr, C), lambda i, c: (jnp.clip(i - (1 - c[0]) * nh, 0, nh - 1), 0))
    gs = pltpu.PrefetchScalarGridSpec(
        num_scalar_prefetch=1, grid=(R // tr,),
        in_specs=[blk, mine_blk, theirs_blk, blk, blk, pl.BlockSpec((8, 128), lambda i, c: (0, 0))],
        out_specs=[blk] * 4)
    return pl.pallas_call(
        body, name=name, grid_spec=gs, out_shape=_hbm_out([jax.ShapeDtypeStruct((R, C), F32)] * 4),
        compiler_params=_cp(32, 1),
    )(cidx, *_hbm(w, mine, theirs, m, v), token)


def _adamw(w, g, m, v, name):
    R, C = w.shape
    tr = _row_tile(R)

    def body(w_ref, g_ref, m_ref, v_ref, d_ref, mo_ref, vo_ref):
        d, m2, v2 = _adamw_math(w_ref[...], g_ref[...], m_ref[...], v_ref[...])
        d_ref[...] = d
        mo_ref[...] = m2
        vo_ref[...] = v2

    blk = pl.BlockSpec((tr, C), lambda i: (i, 0))
    return pl.pallas_call(
        body, name=name, grid=(R // tr,), in_specs=[blk] * 4, out_specs=[blk] * 3,
        out_shape=_hbm_out([jax.ShapeDtypeStruct((R, C), F32)] * 3),
        compiler_params=_cp(32, 1),
    )(*_hbm(w, g, m, v))


def _pack(arrs):
    flat = jnp.concatenate([a.reshape(-1).astype(F32) for a in arrs])
    rows = -(-flat.shape[0] // 1024)
    rows = -(-rows // 8) * 8
    return jnp.pad(flat, (0, rows * 1024 - flat.shape[0])).reshape(rows, 1024)


def _unpack(packed, shapes):
    flat = packed.reshape(-1)
    out, off = [], 0
    for s in shapes:
        n = math.prod(s)
        out.append(flat[off:off + n].reshape(s))
        off += n
    return out


BIG = ["ffn1_w_in", "ffn1_w_out", "mix_w_in", "conv_w_out", "ssm_w_glu", "mix_w_out",
       "ffn2_w_in", "ffn2_w_out", "ple_w_in", "ple_w_gate"]
SMALL = ["ln1_g", "ln1_b", "conv_w", "conv_b", "ssm_lam_re", "ssm_lam_im", "ssm_log_step", "ssm_b_re", "ssm_b_im",
         "ssm_c_re", "ssm_c_im", "ssm_d", "ln2_g", "ln2_b", "ln3_g", "ln3_b", "ln4_g", "ln4_b"]
WEIGHTS = ["ffn1_w_in", "ffn1_w_out", "ln1_g", "ln1_b", "mix_w_in", "conv_w", "conv_b", "conv_w_out",
           "ssm_lam_re", "ssm_lam_im", "ssm_log_step", "ssm_b_re", "ssm_b_im", "ssm_c_re", "ssm_c_im", "ssm_d",
           "ssm_w_glu", "mix_w_out", "ln2_g", "ln2_b", "ffn2_w_in", "ffn2_w_out", "ln3_g", "ln3_b",
           "ple_w_in", "ple_w_gate", "ln4_g", "ln4_b"]


class _NoComm:
    def __init__(self, W):
        self.W, self.G, self.raw, self.done = dict(W), {}, None, {}

    def carry(self, name):
        return ()

    def landed(self, name, got):
        pass

    def grad(self, name, g4):
        self.G[name] = g4

    def small(self, raw):
        self.raw = raw


def _s5_operands(sp):
    abr, abi, bbr, bbi = _zoh(sp["ssm_lam_re"], sp["ssm_lam_im"], sp["ssm_log_step"], sp["ssm_b_re"], sp["ssm_b_im"])
    return (_wb_blocks(bbr), _wb_blocks(bbi), _wc_blocks(sp["ssm_c_re"]), _wc_blocks(-sp["ssm_c_im"]),
            abr.reshape(1, LANES), abi.reshape(1, LANES), sp["ssm_d"].reshape(1, SSM))


def _local_step(x, p, target, sp, sched, tm_ffn, tm_mix, ops=None):
    W = sched.W
    wb_re, wb_im, wc_re4, wc_im4, a_re, a_im, dvec = ops if ops is not None else _s5_operands(sp)

    def run(fn, name, *args, **kw):
        outs, got = fn(*args, comm=sched.carry(name), **kw)
        sched.landed(name, got)
        sched.done[name] = outs[0]
        return outs

    def dw(name, wname, a, b, tk, tn, shape4, shard_cols=None, interleaved=False):
        out, got = _mm_tn(a, b, tk, tn, name, shard_cols=shard_cols, interleaved=interleaved,
                          comm=sched.carry(name))
        sched.landed(name, got)
        sched.done[name] = out
        sched.grad(wname, out.reshape(shape4))

    xb = x.astype(BF16)
    h1, r1, x1, x1b = run(_ffn_fwd, "ffn1_fwd", x, xb, W["ffn1_w_in"], W["ffn1_w_out"].reshape(2, FFH, D),
                          sp["ln1_g"], sp["ln1_b"], tm_ffn, "ffn1_fwd")
    conv_w = W["conv_w"][:, 0:3, :].transpose(1, 0, 2).reshape(3, CONV)
    pc, z_b, yin_b, su, su_b, g_conv, g_ssm, y_conv = run(
        _mix_fwd_a, "mix_fwd_a", x1b, W["mix_w_in"], conv_w, sp["conv_b"], W["conv_w_out"], tm_mix)
    st_re, st_im = run(_s5_scan_fwd, "s5_scan_fwd", su_b, wb_re, wb_im, a_re, a_im)
    w_mo = W["mix_w_out"].reshape(D, D)
    s, sg_b, ga, gb, merged_b, r2, x2, x2b = run(
        _mix_fwd_b, "mix_fwd_b", st_re, st_im, wc_re4, wc_im4, su, dvec, W["ssm_w_glu"], g_conv, g_ssm, y_conv,
        w_mo, x1, sp["ln2_g"], sp["ln2_b"], tm_mix)
    w2o2 = W["ffn2_w_out"].reshape(2, FFH, D)
    h2, r3, x3, x3b = run(_ffn_fwd, "ffn2_fwd", x2, x2b, W["ffn2_w_in"], w2o2, sp["ln3_g"], sp["ln3_b"], tm_ffn,
                          "ffn2_fwd")
    loss_part, dx3, p_b, dpw_b, dgt_b, dg4, db4 = _ple_loss(
        x3, x3b, p, W["ple_w_in"], W["ple_w_gate"].reshape(D, D), sp["ln4_g"], sp["ln4_b"], target, tm_mix)

    dw("dw_ple_gate", "ple_w_gate", x3b, dgt_b, 512, 1024, (4, 256, D))
    dw("dw_ple_in", "ple_w_in", p_b, dpw_b, 256, 256, (4, 256, 256), shard_cols=256)
    dx2, dh2, a2_b, df2_b, dg3, db3 = run(_ffn_bwd, "ffn2_bwd", dx3, r3, sp["ln3_g"], h2, W["ffn2_w_in"], w2o2,
                                          tm_mix, "ffn2_bwd")
    dw("dw_ffn2_in", "ffn2_w_in", x2b, dh2, 512, FFH, (4, D, FFH), shard_cols=FFH, interleaved=True)
    dw("dw_ffn2_out", "ffn2_w_out", a2_b, df2_b, FFH, 1024, (4, FF // 4, D))
    (dres, dmix_b, dgl_b, ds_b, du_dir, gs_re, gs_im, dyc_b, dproj, dg2, db2, dd) = run(
        _mix_bwd_b, "mix_bwd_b", dx2, r2, sp["ln2_g"], w_mo, g_conv, g_ssm, y_conv, ga, gb, s, su, dvec,
        W["ssm_w_glu"], wc_re4, wc_im4, tm_mix)
    dw("dw_mix_out", "mix_w_out", merged_b, dmix_b, 512, 1024, (4, 256, D))
    dw("dw_glu", "ssm_w_glu", sg_b, dgl_b, 512, 512, (4, SSM, 512), shard_cols=512)
    dsu_ssm, dwb_re, dwb_im, dwc_re, dwc_im, da_re, da_im = run(
        _s5_scan_bwd, "s5_scan_bwd", gs_re, gs_im, st_re, st_im, su_b, ds_b, wb_re, wb_im, a_re, a_im)
    dw("dw_conv_out", "conv_w_out", yin_b, dyc_b, 512, 256, (4, CONV, 256), shard_cols=256)
    dproj, dx1, dcw8, dcb = run(_mix_bwd_a, "mix_bwd_a", dyc_b, W["conv_w_out"], pc, z_b, conv_w, dsu_ssm,
                                du_dir, dproj, dres, W["mix_w_in"], tm_mix)
    dw("dw_mix_in", "mix_w_in", x1b, dproj, 512, 1024, (4, D, D), shard_cols=1024)
    dx0, dh1, a1_b, df1_b, dg1, db1 = run(_ffn_bwd, "ffn1_bwd", dx1, r1, sp["ln1_g"], h1, W["ffn1_w_in"],
                                          W["ffn1_w_out"].reshape(2, FFH, D), tm_mix, "ffn1_bwd")
    sched.small(dict(
        ln1_g=dg1, ln1_b=db1, ln2_g=dg2, ln2_b=db2, ln3_g=dg3, ln3_b=db3, ln4_g=dg4, ln4_b=db4,
        conv_w=dcw8[0:3], conv_b=dcb,
        a_re=da_re.reshape(GROUPS, STATE), a_im=da_im.reshape(GROUPS, STATE),
        bb_re=_wb_diag(dwb_re), bb_im=_wb_diag(dwb_im),
        ssm_c_re=_wc_diag(dwc_re), ssm_c_im=-_wc_diag(dwc_im), ssm_d=dd.reshape(GROUPS, 16),
        loss=loss_part[0:1, 0]))
    dw("dw_ffn1_in", "ffn1_w_in", xb, dh1, 512, FFH, (4, D, FFH), shard_cols=FFH, interleaved=True)
    dw("dw_ffn1_out", "ffn1_w_out", a1_b, df1_b, FFH, 1024, (4, FF // 4, D))
    return loss_part[0, 0], dx0


RAW_ORDER = ["ln1_g", "ln1_b", "ln2_g", "ln2_b", "ln3_g", "ln3_b", "ln4_g", "ln4_b", "conv_w", "conv_b",
             "a_re", "a_im", "bb_re", "bb_im", "ssm_c_re", "ssm_c_im", "ssm_d", "loss"]

GATHER_FIRST = ["ffn1_w_in", "ffn1_w_out"]
GATHER_AT = {"ffn1_fwd": ["mix_w_in", "conv_w_out", "conv_w"], "mix_fwd_a": ["ssm_w_glu", "mix_w_out"],
             "s5_scan_fwd": ["ffn2_w_in"], "mix_fwd_b": ["ffn2_w_out"], "ffn2_fwd": ["ple_w_in", "ple_w_gate"]}
REDUCE_GROUP = {"ple": ["ple_w_gate", "ple_w_in"], "ffn2": ["ffn2_w_in", "ffn2_w_out"],
                "mix": ["mix_w_out", "ssm_w_glu", "conv_w_out", "mix_w_in"], "ffn1": ["ffn1_w_in", "ffn1_w_out"]}
REDUCE_AT = {"ffn2_bwd": [("swap", "ple")], "dw_ffn2_in": [("exchange", "ple")],
             "mix_bwd_b": [("swap", "ffn2"), ("join", "ple")],
             "mix_bwd_a": [("join", "ffn2")], "ffn1_bwd": [("swap", "mix")]}
BEGIN_AT = {"dw_mix_out": [("exchange", "ffn2")], "dw_ffn1_in": [("small", None), ("exchange", "mix")]}
BEHIND = {"dw_glu": [("exchange", "ffn2")], "s5_scan_bwd": [("exchange", "ffn2")]}
END_AT = {"mix_bwd_a": [("exchange", "ffn2", ["dw_mix_out", "dw_glu", "s5_scan_bwd"])]}
LAST_GROUP = "ffn1"


class _Sched:
    def __init__(self, cmidx):
        self.bufs, self.cmidx = {}, cmidx
        self.W, self.G, self.raw, self.small_buf = {}, {}, None, None
        self.got1, self.p32, self.pbf, self.got2, self.half, self.theirs = {}, {}, {}, {}, {}, {}
        self._open, self._split, self.done = [], {}, {}

    def first_begin(self, bufs):
        self.bufs.update(bufs)
        p = _gather_ici_payload([bufs[n] for n in GATHER_FIRST])
        self._first = (p, _split_start(p, "gather_first_start"))
        return self._first[1][3]

    def first_end(self, bufs, after):
        self.bufs.update(bufs)
        p, handle = self._first
        _, landed = _split_wait(p, handle, after, "gather_first_wait")
        (outs,) = _comm_call("gather_first_pass", [_gather_pass_payload(landed)])
        self.W.update(zip(GATHER_FIRST, outs))

    def _payload(self, stage, key):
        if stage == "gather":
            return _gather_payload([self.bufs[n] for n in key])
        if stage == "small":
            return _allgather_payload(_pack([self.raw[k] for k in RAW_ORDER]))
        names = REDUCE_GROUP[key]
        if stage == "swap":
            return _swap_payload([self.G[n] for n in names])
        if stage == "exchange":
            for n in names:
                self.p32[n], self.pbf[n] = _pair_sum(self.cmidx, self.G[n], self.got1[n], "pair_sum_" + n)
            return _exchange_payload([self.pbf[n] for n in names])
        for n in names:
            self.half[n] = _chip_sum(self.p32[n], self.got2[n], "chip_sum_" + n)
        return _join_payload([self.half[n] for n in names])

    def _store(self, stages, got):
        for (stage, key), outs in zip(stages, got):
            if stage == "gather":
                self.W.update(zip(key, outs))
            elif stage == "small":
                self.small_buf = outs[0]
            else:
                {"swap": self.got1, "exchange": self.got2, "join": self.theirs}[stage].update(
                    zip(REDUCE_GROUP[key], outs))

    def _standalone(self, name, stages):
        self._store(stages, _comm_call(name, [self._payload(s, k) for s, k in stages]))

    def carry(self, name):
        for stage, key, behind in END_AT.get(name, []):
            self._end(stage, key, [self.done[b] for b in behind])
        tokens = [self._begin(stage, key) for stage, key in BEGIN_AT.get(name, [])]
        tokens += [self._split[sk][1][3] for sk in BEHIND.get(name, [])]
        self._open = [("gather", GATHER_AT[name])] if name in GATHER_AT else []
        self._open += REDUCE_AT.get(name, [])
        comm = [self._payload(s, k) for s, k in self._open]
        if tokens:
            comm.append(_Payload(tokens, [], {}, [], lambda *a: None, lambda *a: None))
        return tuple(comm)

    def landed(self, name, got):
        self._store(self._open, got)

    def grad(self, name, g4):
        self.G[name] = g4

    def small(self, raw):
        self.raw = raw

    def _begin(self, stage, key):
        p = self._payload(stage, key)
        self._split[stage, key] = (p, _split_start(p, "%s_%s_start" % (stage, key)))
        return self._split[stage, key][1][3]

    def _end(self, stage, key, after):
        p, handle = self._split.pop((stage, key))
        srcs, lands = _split_wait(p, handle, after, "%s_%s_wait" % (stage, key))
        if stage == "swap":
            self.G.update(zip(REDUCE_GROUP[key], srcs))
        self._store([(stage, key)], [lands])

    def tail_begin(self):
        return self._begin("swap", LAST_GROUP)

    def tail_mid(self, after):
        self._end("swap", LAST_GROUP, after)
        token = self._begin("exchange", LAST_GROUP)
        self._end("small", None, [token])
        self._end("exchange", "mix", [token])
        self._standalone("reduce_tail_join_mix", [("join", "mix")])
        return token

    def tail_end(self, after):
        self._end("exchange", LAST_GROUP, after)
        self._standalone("reduce_tail_join", [("join", LAST_GROUP)])


def _small_grads(raw_sum, sp):
    _, vjp = jax.vjp(_zoh, sp["ssm_lam_re"], sp["ssm_lam_im"], sp["ssm_log_step"], sp["ssm_b_re"], sp["ssm_b_im"])
    d_lre, d_lim, d_ls, d_bre, d_bim = vjp((raw_sum["a_re"], raw_sum["a_im"], raw_sum["bb_re"], raw_sum["bb_im"]))
    g = {k: raw_sum[k] for k in ("ln1_g", "ln1_b", "ln2_g", "ln2_b", "ln3_g", "ln3_b", "ln4_g", "ln4_b",
                                 "conv_w", "conv_b", "ssm_c_re", "ssm_c_im", "ssm_d")}
    g.update(ssm_lam_re=d_lre, ssm_lam_im=d_lim, ssm_log_step=d_ls, ssm_b_re=d_bre, ssm_b_im=d_bim)
    return g


def kernel(x, p, ffn1_w_in, ffn1_w_out, ln1_g, ln1_b, mix_w_in, conv_w, conv_b, conv_w_out, ssm_lam_re, ssm_lam_im, ssm_log_step, ssm_b_re, ssm_b_im, ssm_c_re, ssm_c_im, ssm_d, ssm_w_glu, mix_w_out, ln2_g, ln2_b, ffn2_w_in, ffn2_w_out, ln3_g, ln3_b, ple_w_in, ple_w_gate, ln4_g, ln4_b, loss_target, m_ffn1_w_in, m_ffn1_w_out, m_ln1_g, m_ln1_b, m_mix_w_in, m_conv_w, m_conv_b, m_conv_w_out, m_ssm_lam_re, m_ssm_lam_im, m_ssm_log_step, m_ssm_b_re, m_ssm_b_im, m_ssm_c_re, m_ssm_c_im, m_ssm_d, m_ssm_w_glu, m_mix_w_out, m_ln2_g, m_ln2_b, m_ffn2_w_in, m_ffn2_w_out, m_ln3_g, m_ln3_b, m_ple_w_in, m_ple_w_gate, m_ln4_g, m_ln4_b, v_ffn1_w_in, v_ffn1_w_out, v_ln1_g, v_ln1_b, v_mix_w_in, v_conv_w, v_conv_b, v_conv_w_out, v_ssm_lam_re, v_ssm_lam_im, v_ssm_log_step, v_ssm_b_re, v_ssm_b_im, v_ssm_c_re, v_ssm_c_im, v_ssm_d, v_ssm_w_glu, v_mix_w_out, v_ln2_g, v_ln2_b, v_ffn2_w_in, v_ffn2_w_out, v_ln3_g, v_ln3_b, v_ple_w_in, v_ple_w_gate, v_ln4_g, v_ln4_b):
    args = dict(locals())
    w = {n: args[n] for n in WEIGHTS}
    m = {n: args["m_" + n] for n in WEIGHTS}
    v = {n: args["v_" + n] for n in WEIGHTS}
    _, _, c, me = _where()
    cidx = jnp.stack([c, me]).astype(jnp.int32)
    meidx = jnp.reshape(me, (1,)).astype(jnp.int32)

    sched = _Sched(cidx)
    token = sched.first_begin({n: _slot_cast(meidx, w[n][0], BF16, "cast_" + n) for n in GATHER_FIRST})
    rest = {n: _slot_cast(meidx, w[n][0], BF16, "cast_" + n, (token,)) for n in BIG if n not in GATHER_FIRST}
    rest["conv_w"] = _slot_cast(meidx, jnp.pad(conv_w[0], ((0, 13), (0, 0))), F32, "cast_conv_w", (token,))
    sp = {n: (w[n] if w[n].ndim == 2 and n != "ssm_log_step" else w[n][0]) for n in SMALL if n != "conv_w"}
    ops = _s5_operands({**sp, "ssm_lam_re": sp["ssm_lam_re"] + token[0, 0]})
    sched.first_end(rest, list(rest.values()) + list(ops))
    loss_part, dx0 = _local_step(x[0], p[0, 0], loss_target[0], sp, sched, 512, 256, ops)
    out_g, out_d, out_m, out_v = {}, {}, {}, {}

    def big_adamw(names, token):
        for n in names:
            g, dl, mn, vn = _adamw_pair(cidx, w[n][0], sched.half[n], sched.theirs[n], m[n][0], v[n][0], token,
                                        "adamw_" + n)
            out_g[n], out_d[n], out_m[n], out_v[n] = g[None], dl[None], mn[None], vn[None]

    first = REDUCE_GROUP["ple"] + REDUCE_GROUP["ffn2"]
    big_adamw(first, sched.tail_begin())
    token = sched.tail_mid([out_v[n] for n in first])

    raw_shapes = [sched.raw[k].shape for k in RAW_ORDER]
    raw_sum = dict(zip(RAW_ORDER, _unpack(_sum8(sched.small_buf, token), raw_shapes)))
    loss = raw_sum["loss"][0]
    sg = _small_grads(raw_sum, sp)
    sg["conv_w"] = lax.dynamic_slice_in_dim(sg["conv_w"], me * 128, 128, axis=1)
    small_shapes = [w[n].shape for n in SMALL]
    gp = _pack([sg[n] for n in SMALL])
    d_s, m_s, v_s = _adamw(_pack([w[n] for n in SMALL]), gp, _pack([m[n] for n in SMALL]),
                           _pack([v[n] for n in SMALL]), "adamw_small")

    for n, a, b_, c_, d_ in zip(SMALL, _unpack(gp, small_shapes), _unpack(d_s, small_shapes),
                                _unpack(m_s, small_shapes), _unpack(v_s, small_shapes)):
        out_g[n], out_d[n], out_m[n], out_v[n] = a, b_, c_, d_
    big_adamw(REDUCE_GROUP["mix"], token)
    sched.tail_end([d_s] + [out_v[n] for n in REDUCE_GROUP["mix"]])
    big_adamw(REDUCE_GROUP[LAST_GROUP], token)

    return (loss, dx0[None], *[out_g[n] for n in WEIGHTS], *[out_d[n] for n in WEIGHTS],
            *[out_m[n] for n in WEIGHTS], *[out_v[n] for n in WEIGHTS])
```

```python
import functools
import math

import jax
import jax.numpy as jnp
import numpy as np
from jax import lax
from jax.experimental import pallas as pl
from jax.experimental.pallas import tpu as pltpu

F32, BF16 = jnp.float32, jnp.bfloat16
D = 1024
FF = 2816
FFH = FF // 2
CONV = 512
SSM = 512
GROUPS = 32
STATE = 64
LANES = GROUPS * STATE
SCAN_W = 128
SCAN_PER = 512 // SCAN_W
SCAN_GR = SCAN_W // STATE
SCAN_R = 256
ALPHA = 2.0 ** 0.25
LN_EPS = 1e-5
GELU_C = math.sqrt(2.0 / math.pi)
B1, B2, LR, EPS, WD, STEP = 0.9, 0.999, 0.001, 1e-8, 0.01, 10
MESH = pl.DeviceIdType.MESH
ANY = pl.BlockSpec(memory_space=pl.ANY)
VMEM_FULL = pl.BlockSpec(memory_space=pltpu.VMEM)


def _cp(vmem_mb=48, n_axes=1):
    return pltpu.CompilerParams(vmem_limit_bytes=vmem_mb << 20,
                                dimension_semantics=("arbitrary",) * n_axes)


def _hbm(*arrs):
    return [pltpu.with_memory_space_constraint(a, pltpu.HBM) for a in arrs]


def _hbm_out(shapes):
    if isinstance(shapes, (list, tuple)):
        return [pltpu.HBM(s.shape, s.dtype) for s in shapes]
    return pltpu.HBM(shapes.shape, shapes.dtype)


def _nn(a, b):
    return jnp.dot(a, b, preferred_element_type=F32)


def _nt(a, b):
    return lax.dot_general(a, b, (((1,), (1,)), ((), ())), preferred_element_type=F32)


def _tn(a, b):
    return lax.dot_general(a, b, (((0,), (0,)), ((), ())), preferred_element_type=F32)


def _sig(v):
    return jax.nn.sigmoid(v)


def _ln_stats(r):
    mu = jnp.mean(r, axis=-1, keepdims=True)
    xc = r - mu
    var = jnp.mean(xc * xc, axis=-1, keepdims=True)
    rstd = lax.rsqrt(var + LN_EPS)
    return xc * rstd, rstd


def _ln_bwd(dy, r, g):
    xhat, rstd = _ln_stats(r)
    dyg = dy * g
    m1 = jnp.mean(dyg, axis=-1, keepdims=True)
    m2 = jnp.mean(dyg * xhat, axis=-1, keepdims=True)
    return rstd * (dyg - m1 - xhat * m2), xhat


def _rowsum(v):
    return jnp.sum(v, axis=0, keepdims=True)


class _Payload:
    def __init__(self, operands, outs, aliases, sems, start, finish):
        self.operands, self.outs, self.aliases, self.sems = list(operands), list(outs), dict(aliases), list(sems)
        self.start, self.finish = start, finish


def _split(flat, comm, attr):
    out, i = [], 0
    for p in comm:
        n = len(getattr(p, attr))
        out.append(list(flat[i:i + n]))
        i += n
    return out


def _run_comm(comm, which, cin, cout, csem):
    for p, a, b, s in zip(comm, _split(cin, comm, "operands"), _split(cout, comm, "outs"), _split(csem, comm, "sems")):
        getattr(p, which)(a, b, s)


def _pcall(body, *, name, grid, in_specs, out_specs, out_shape, operands, scratch=(), vmem_mb=48, aliases=None,
           comm=()):
    ni, no, ns = len(in_specs), len(out_specs), len(scratch)
    c_ops = [a for p in comm for a in p.operands]
    c_outs = [s for p in comm for s in p.outs]
    c_sems = [s for p in comm for s in p.sems]
    io = dict(aliases or {})
    off_i, off_o = ni, no
    for p in comm:
        for a, b in p.aliases.items():
            io[off_i + a] = off_o + b
        off_i += len(p.operands)
        off_o += len(p.outs)

    def wrapped(*refs):
        ins, cin = refs[:ni], refs[ni:ni + len(c_ops)]
        o0 = ni + len(c_ops)
        outs, cout = refs[o0:o0 + no], refs[o0 + no:o0 + no + len(c_outs)]
        s0 = o0 + no + len(c_outs)
        scr, csem = refs[s0:s0 + ns], refs[s0 + ns:]
        if comm:
            first = functools.reduce(jnp.logical_and, [pl.program_id(a) == 0 for a in range(len(grid))])
            pl.when(first)(lambda: _run_comm(comm, "start", cin, cout, csem))
        body(*ins, *outs, *scr)
        if comm:
            last = functools.reduce(jnp.logical_and, [pl.program_id(a) == grid[a] - 1 for a in range(len(grid))])
            pl.when(last)(lambda: _run_comm(comm, "finish", cin, cout, csem))

    res = pl.pallas_call(
        wrapped, name=name, grid=grid,
        in_specs=list(in_specs) + [ANY] * len(c_ops), out_specs=list(out_specs) + [ANY] * len(c_outs),
        out_shape=_hbm_out(list(out_shape) + c_outs), scratch_shapes=list(scratch) + c_sems,
        input_output_aliases=io,
        compiler_params=pltpu.CompilerParams(vmem_limit_bytes=vmem_mb << 20,
                                             dimension_semantics=("arbitrary",) * len(grid),
                                             has_side_effects=bool(comm)),
    )(*_hbm(*operands, *c_ops))
    return list(res[:no]), _split(res[no:], comm, "outs")


def _comm_call(name, comm):
    c_ops = [a for p in comm for a in p.operands]
    c_outs = [s for p in comm for s in p.outs]
    c_sems = [s for p in comm for s in p.sems]
    io, off_i, off_o = {}, 0, 0
    for p in comm:
        for a, b in p.aliases.items():
            io[off_i + a] = off_o + b
        off_i += len(p.operands)
        off_o += len(p.outs)

    def body(*refs):
        cin, cout = refs[:len(c_ops)], refs[len(c_ops):len(c_ops) + len(c_outs)]
        csem = refs[len(c_ops) + len(c_outs):]
        _run_comm(comm, "start", cin, cout, csem)
        _run_comm(comm, "finish", cin, cout, csem)

    res = pl.pallas_call(
        body, name=name, in_specs=[ANY] * len(c_ops), out_specs=[ANY] * len(c_outs), out_shape=_hbm_out(c_outs),
        scratch_shapes=c_sems, input_output_aliases=io,
        compiler_params=pltpu.CompilerParams(has_side_effects=True),
    )(*_hbm(*c_ops))
    return _split(res, comm, "outs")


def _ffn_fwd(x, xb, w_in4, w_out2, g, b, tm, name, comm=()):
    T = x.shape[0]

    def body(x_ref, xb_ref, win_ref, wo_ref, g_ref, b_ref, h_ref, r_ref, xo_ref, xob_ref):
        xv = xb_ref[...]
        acc = ALPHA * x_ref[...]
        for k in range(2):
            gt = _nn(xv, win_ref[k])
            up = _nn(xv, win_ref[k + 2])
            a = (gt * _sig(gt) * up).astype(BF16)
            h_ref[:, 2 * k * FFH:(2 * k + 1) * FFH] = gt.astype(BF16)
            h_ref[:, (2 * k + 1) * FFH:(2 * k + 2) * FFH] = up.astype(BF16)
            acc = acc + 0.5 * _nn(a, wo_ref[k])
        xhat, _ = _ln_stats(acc)
        xo = xhat * g_ref[...] + b_ref[...]
        r_ref[...] = acc
        xo_ref[...] = xo
        xob_ref[...] = xo.astype(BF16)

    tok = pl.BlockSpec((tm, D), lambda i: (i, 0))
    vec = pl.BlockSpec((1, D), lambda i: (0, 0))
    return _pcall(
        body, name=name, grid=(T // tm,),
        in_specs=[tok, tok,
                  pl.BlockSpec((4, D, FFH), lambda i: (0, 0, 0), pipeline_mode=pl.Buffered(1)),
                  pl.BlockSpec((2, FFH, D), lambda i: (0, 0, 0), pipeline_mode=pl.Buffered(1)),
                  vec, vec],
        out_specs=[pl.BlockSpec((tm, 2 * FF), lambda i: (i, 0)), tok, tok, tok],
        out_shape=[jax.ShapeDtypeStruct((T, 2 * FF), BF16), jax.ShapeDtypeStruct((T, D), F32),
                   jax.ShapeDtypeStruct((T, D), F32), jax.ShapeDtypeStruct((T, D), BF16)],
        vmem_mb=58, comm=comm, operands=(x, xb, w_in4, w_out2, g, b))


def _ffn_bwd(dy, r, g, h, w_in4, w_out2, tm, name, comm=()):
    T = dy.shape[0]

    def body(dy_ref, r_ref, g_ref, h_ref, win_ref, wo_ref, dx_ref, dh_ref, a_ref, df_ref, dg_ref, db_ref):
        i = pl.program_id(0)
        dyv = dy_ref[...]
        dr, xhat = _ln_bwd(dyv, r_ref[...], g_ref[...])
        dg_ref[...] = jnp.where(i == 0, 0.0, dg_ref[...]) + _rowsum(dyv * xhat)
        db_ref[...] = jnp.where(i == 0, 0.0, db_ref[...]) + _rowsum(dyv)
        dfb = (0.5 * dr).astype(BF16)
        df_ref[...] = dfb
        acc = ALPHA * dr
        for k in range(2):
            da = _nt(dfb, wo_ref[k])
            gt = h_ref[:, 2 * k * FFH:(2 * k + 1) * FFH].astype(F32)
            up = h_ref[:, (2 * k + 1) * FFH:(2 * k + 2) * FFH].astype(F32)
            sg = _sig(gt)
            silu = gt * sg
            dgate = (da * up * (sg * (1.0 + gt * (1.0 - sg)))).astype(BF16)
            dup = (da * silu).astype(BF16)
            a_ref[:, k * FFH:(k + 1) * FFH] = (silu * up).astype(BF16)
            dh_ref[:, 2 * k * FFH:(2 * k + 1) * FFH] = dgate
            dh_ref[:, (2 * k + 1) * FFH:(2 * k + 2) * FFH] = dup
            acc = acc + _nt(dgate, win_ref[k]) + _nt(dup, win_ref[k + 2])
        dx_ref[...] = acc

    tok = pl.BlockSpec((tm, D), lambda i: (i, 0))
    vec = pl.BlockSpec((1, D), lambda i: (0, 0))
    wide = pl.BlockSpec((tm, 2 * FF), lambda i: (i, 0))
    return _pcall(
        body, name=name, grid=(T // tm,),
        in_specs=[tok, tok, vec, wide,
                  pl.BlockSpec((4, D, FFH), lambda i: (0, 0, 0), pipeline_mode=pl.Buffered(1)),
                  pl.BlockSpec((2, FFH, D), lambda i: (0, 0, 0), pipeline_mode=pl.Buffered(1))],
        out_specs=[tok, wide, pl.BlockSpec((tm, FF), lambda i: (i, 0)), tok, vec, vec],
        out_shape=[jax.ShapeDtypeStruct((T, D), F32), jax.ShapeDtypeStruct((T, 2 * FF), BF16),
                   jax.ShapeDtypeStruct((T, FF), BF16), jax.ShapeDtypeStruct((T, D), BF16),
                   jax.ShapeDtypeStruct((1, D), F32), jax.ShapeDtypeStruct((1, D), F32)],
        vmem_mb=58, comm=comm, operands=(dy, r, g, h, w_in4, w_out2))


def _mm_tn(a, b, tk, tn, name, shard_cols=None, interleaved=False, comm=()):
    T, K = a.shape
    N = b.shape[1]

    def body(a_ref, b_ref, o_ref):
        o_ref[...] = _tn(a_ref[...], b_ref[...])

    if shard_cols is None:
        out_shape = jax.ShapeDtypeStruct((K, N), F32)
        out_spec = pl.BlockSpec((tk, tn), lambda ki, nj: (ki, nj))
    else:
        per = shard_cols // tn

        def shard(nj):
            blk = nj // per
            return (blk % 2) * 2 + blk // 2 if interleaved else blk

        out_shape = jax.ShapeDtypeStruct((N // shard_cols, K, shard_cols), F32)
        out_spec = pl.BlockSpec((None, tk, tn), lambda ki, nj: (shard(nj), ki, nj % per))
    (out,), got = _pcall(
        body, name=name, grid=(K // tk, N // tn),
        in_specs=[pl.BlockSpec((T, tk), lambda ki, nj: (0, ki)), pl.BlockSpec((T, tn), lambda ki, nj: (0, nj))],
        out_specs=[out_spec], out_shape=[out_shape], comm=comm, operands=(a, b))
    return out, got


def _mix_fwd_a(xb, w_mix4, conv_w, conv_b, w_co4, tm, comm=()):
    T = xb.shape[0]

    def body(xb_ref, w_ref, cw_ref, cb_ref, wco_ref,
             pc_ref, z_ref, yin_ref, su_ref, sub_ref, gc_ref, gs_ref, yc_ref, qbuf):
        @pl.when(pl.program_id(0) == 0)
        def _():
            qbuf[pl.ds(0, 8), :] = jnp.zeros((8, CONV), F32)

        xv = xb_ref[...]
        p0 = _nn(xv, w_ref[0])
        p1 = _nn(xv, w_ref[1])
        gc_ref[...] = _nn(xv, w_ref[2])
        gs_ref[...] = _nn(xv, w_ref[3])
        cbv, ccv = p0[:, :CONV], p0[:, CONV:]
        chv, suv = p1[:, :CONV], p1[:, CONV:]
        q = ccv * chv
        qbuf[pl.ds(8, tm), :] = q
        cw = cw_ref[...]
        z = (cw[2:3] * q + cw[1:2] * qbuf[pl.ds(7, tm), :] + cw[0:1] * qbuf[pl.ds(6, tm), :]
             + cb_ref[...])
        qbuf[pl.ds(0, 8), :] = q[tm - 8:tm]
        yin = (cbv * z).astype(BF16)
        pc_ref[:, 0:CONV] = cbv.astype(BF16)
        pc_ref[:, CONV:2 * CONV] = ccv.astype(BF16)
        pc_ref[:, 2 * CONV:3 * CONV] = chv.astype(BF16)
        z_ref[...] = z.astype(BF16)
        yin_ref[...] = yin
        su_ref[...] = suv
        sub_ref[...] = suv.astype(BF16)
        for k in range(4):
            yc_ref[:, 256 * k:256 * (k + 1)] = _nn(yin, wco_ref[k])

    def tok(n):
        return pl.BlockSpec((tm, n), lambda i: (i, 0))

    def full(shape):
        return pl.BlockSpec(shape, lambda i: (0,) * len(shape))

    return _pcall(
        body, name="mix_fwd_a", grid=(T // tm,),
        in_specs=[tok(D), full((4, D, D)), full((3, CONV)), full((1, CONV)), full((4, CONV, 256))],
        out_specs=[tok(3 * CONV), tok(CONV), tok(CONV), tok(SSM), tok(SSM), tok(D), tok(D), tok(D)],
        out_shape=[jax.ShapeDtypeStruct((T, 3 * CONV), BF16), jax.ShapeDtypeStruct((T, CONV), BF16),
                   jax.ShapeDtypeStruct((T, CONV), BF16), jax.ShapeDtypeStruct((T, SSM), F32),
                   jax.ShapeDtypeStruct((T, SSM), BF16), jax.ShapeDtypeStruct((T, D), F32),
                   jax.ShapeDtypeStruct((T, D), F32), jax.ShapeDtypeStruct((T, D), F32)],
        scratch=[pltpu.VMEM((tm + 8, CONV), F32)], vmem_mb=56, comm=comm,
        operands=(xb, w_mix4, conv_w, conv_b, w_co4))


def _scan_rows(bre, bim, ar, ai, T, rev, load, out=None):
    R, W, G = SCAN_R, bre.shape[1], T // 8
    if rev:
        ai = -ai

    def cmul(pr, pi, xr, xi):
        return pr * xr - pi * xi, pr * xi + pi * xr

    pw = [(ar, ai)]
    for _ in range(7):
        pw.append(cmul(ar, ai, *pw[-1]))

    def shifted(v, d, axis, n, idx):
        if rev:
            return jnp.where(idx < n - d, pltpu.roll(v, n - d, axis), 0.0)
        return jnp.where(idx >= d, pltpu.roll(v, d, axis), 0.0)

    sub8 = lax.broadcasted_iota(jnp.int32, (8, W), 0)
    inside = {d: (sub8 < 8 - d) if rev else (sub8 >= d) for d in (1, 2, 4)}
    pm = {d: (jnp.where(inside[d], pw[d - 1][0], 0.0)[None], jnp.where(inside[d], pw[d - 1][1], 0.0)[None])
          for d in (1, 2, 4)}

    def step(i, _):
        t0 = pl.multiple_of(i * R, R)
        vr, vi = load(t0)
        vr, vi = vr.reshape(R // 8, 8, W), vi.reshape(R // 8, 8, W)
        for d in (1, 2, 4):
            sh = (8 - d) if rev else d
            dr, di = cmul(pm[d][0], pm[d][1], pltpu.roll(vr, sh, 1), pltpu.roll(vi, sh, 1))
            vr, vi = vr + dr, vi + di
        bre[pl.ds(t0 + 8, R), :] = vr.reshape(R, W)
        bim[pl.ds(t0 + 8, R), :] = vi.reshape(R, W)
        return 0

    lax.fori_loop(0, T // R, step, 0)

    edge = 0 if rev else 7
    cr = bre[pl.ds(8 + edge, G, stride=8), :]
    ci = bim[pl.ds(8 + edge, G, stride=8), :]
    row = lax.broadcasted_iota(jnp.int32, (G, W), 0)
    qr, qi = pw[7]
    d = 1
    while d < G:
        dr, di = cmul(qr, qi, shifted(cr, d, 0, G, row), shifted(ci, d, 0, G, row))
        cr, ci = cr + dr, ci + di
        qr, qi = qr * qr - qi * qi, 2.0 * qr * qi
        d *= 2

    nr, ni = shifted(cr, 1, 0, G, row), shifted(ci, 1, 0, G, row)
    for r in range(8):
        pr, pi = pw[7 - r] if rev else pw[r]
        dr, di = cmul(pr, pi, nr, ni)
        xr = bre[pl.ds(8 + r, G, stride=8), :] + dr
        xi = bim[pl.ds(8 + r, G, stride=8), :] + di
        if out is None:
            bre[pl.ds(8 + r, G, stride=8), :] = xr
            bim[pl.ds(8 + r, G, stride=8), :] = xi
        else:
            out[0][pl.ds(r, G, stride=8), :] = xr
            out[1][pl.ds(r, G, stride=8), :] = xi


def _scan_specs(T):
    W = SCAN_W
    lane = pl.BlockSpec((T, W), lambda j: (0, j))
    col = pl.BlockSpec((T, 128), lambda j: (0, j // SCAN_PER))
    wb = pl.BlockSpec((None, 128, W), lambda j: (j, 0, 0))
    wc = pl.BlockSpec((None, W, 128), lambda j: (j, 0, 0))
    vec = pl.BlockSpec((1, W), lambda j: (0, j))
    return lane, col, wb, wc, vec


def _s5_scan_fwd(su_b, wb_re, wb_im, a_re, a_im, comm=()):
    T = su_b.shape[0]
    W = SCAN_W

    def body(su_ref, wbr_ref, wbi_ref, ar_ref, ai_ref, sr_ref, si_ref, bre, bim):
        def load(t0):
            su = su_ref[pl.ds(t0, SCAN_R), :]
            return _nn(su, wbr_ref[...]), _nn(su, wbi_ref[...])

        _scan_rows(bre, bim, ar_ref[...], ai_ref[...], T, False, load, out=(sr_ref, si_ref))

    lane, col, wb, wc, vec = _scan_specs(T)
    return _pcall(
        body, name="s5_scan_fwd", grid=(LANES // W,),
        in_specs=[col, wb, wb, vec, vec],
        out_specs=[lane, lane],
        out_shape=[jax.ShapeDtypeStruct((T, LANES), F32)] * 2,
        scratch=[pltpu.VMEM((T + 16, W), F32)] * 2, comm=comm,
        operands=(su_b, wb_re, wb_im, a_re, a_im))


def _gelu(s):
    th = jnp.tanh(GELU_C * (s + 0.044715 * s * s * s))
    return 0.5 * s * (1.0 + th), th


def _mix_fwd_b(st_re, st_im, wc_re4, wc_im4, su, dvec, w_glu4, g_conv, g_ssm, y_conv, w_mo, x1, g, b, tm, comm=()):
    T = su.shape[0]

    def body(sr_ref, si_ref, wcr_ref, wci_ref, su_ref, d_ref, wg_ref, gc_ref, gs_ref, yc_ref, wmo_ref,
             x_ref, g_ref, b_ref, s_ref, sgb_ref, ga_ref, gb_ref, mb_ref, r_ref, xo_ref, xob_ref):
        srb = sr_ref[...].astype(BF16)
        sib = si_ref[...].astype(BF16)
        ys = [_nn(srb[:, 512 * J:512 * (J + 1)], wcr_ref[J]) + _nn(sib[:, 512 * J:512 * (J + 1)], wci_ref[J])
              for J in range(4)]
        s = jnp.concatenate(ys, axis=1) + d_ref[...] * su_ref[...]
        sg, _ = _gelu(s)
        sgb = sg.astype(BF16)
        ga = jnp.concatenate([_nn(sgb, wg_ref[0]), _nn(sgb, wg_ref[1])], axis=1)
        gb = jnp.concatenate([_nn(sgb, wg_ref[2]), _nn(sgb, wg_ref[3])], axis=1)
        merged = _sig(gc_ref[...]) * yc_ref[...] + _sig(gs_ref[...]) * (ga * _sig(gb))
        mb = merged.astype(BF16)
        r = ALPHA * x_ref[...] + _nn(mb, wmo_ref[...])
        xhat, _ = _ln_stats(r)
        xo = xhat * g_ref[...] + b_ref[...]
        s_ref[...] = s
        sgb_ref[...] = sgb
        ga_ref[...] = ga
        gb_ref[...] = gb
        mb_ref[...] = mb
        r_ref[...] = r
        xo_ref[...] = xo
        xob_ref[...] = xo.astype(BF16)

    def tok(n):
        return pl.BlockSpec((tm, n), lambda i: (i, 0))

    def full(shape):
        return pl.BlockSpec(shape, lambda i: (0,) * len(shape))

    return _pcall(
        body, name="mix_fwd_b", grid=(T // tm,),
        in_specs=[tok(LANES), tok(LANES), full((4, 512, 128)), full((4, 512, 128)), tok(SSM), full((1, SSM)),
                  full((4, SSM, 512)), tok(D), tok(D), tok(D), full((D, D)), tok(D), full((1, D)), full((1, D))],
        out_specs=[tok(SSM), tok(SSM), tok(D), tok(D), tok(D), tok(D), tok(D), tok(D)],
        out_shape=[jax.ShapeDtypeStruct((T, SSM), F32), jax.ShapeDtypeStruct((T, SSM), BF16),
                   jax.ShapeDtypeStruct((T, D), F32), jax.ShapeDtypeStruct((T, D), F32),
                   jax.ShapeDtypeStruct((T, D), BF16), jax.ShapeDtypeStruct((T, D), F32),
                   jax.ShapeDtypeStruct((T, D), F32), jax.ShapeDtypeStruct((T, D), BF16)],
        vmem_mb=56, comm=comm,
        operands=(st_re, st_im, wc_re4, wc_im4, su, dvec, w_glu4, g_conv, g_ssm, y_conv, w_mo, x1, g, b))


def _ple_loss(x3, x3b, p, w_pi4, w_pg, g, b, target, tm):
    T = x3.shape[0]
    PD = p.shape[1]

    def body(x_ref, xb_ref, p_ref, wpi_ref, wpg_ref, g_ref, b_ref, t_ref,
             loss_ref, dx_ref, pb_ref, dpw_ref, dgt_ref, dg_ref, db_ref):
        i = pl.program_id(0)
        pb = p_ref[...].astype(BF16)
        pw = jnp.concatenate([_nn(pb, wpi_ref[k]) for k in range(4)], axis=1)
        gt = _nn(xb_ref[...], wpg_ref[...])
        sg = _sig(gt)
        r = ALPHA * x_ref[...] + pw * sg
        gv = g_ref[...]
        xhat, rstd = _ln_stats(r)
        err = xhat * gv + b_ref[...] - t_ref[...]
        lpart = jnp.zeros((1, 128), F32) + 0.5 * jnp.sum(jnp.mean(err * err, axis=-1, keepdims=True))
        dy = err * (1.0 / D)
        dyg = dy * gv
        m1 = jnp.mean(dyg, axis=-1, keepdims=True)
        m2 = jnp.mean(dyg * xhat, axis=-1, keepdims=True)
        dr = rstd * (dyg - m1 - xhat * m2)
        pg, pbias = _rowsum(dy * xhat), _rowsum(dy)

        @pl.when(i == 0)
        def _():
            loss_ref[...] = lpart
            dg_ref[...] = pg
            db_ref[...] = pbias

        @pl.when(i > 0)
        def _():
            loss_ref[...] += lpart
            dg_ref[...] += pg
            db_ref[...] += pbias

        dgt = (dr * pw * sg * (1.0 - sg)).astype(BF16)
        pb_ref[...] = pb
        dpw_ref[...] = (dr * sg).astype(BF16)
        dgt_ref[...] = dgt
        dx_ref[...] = ALPHA * dr + _nt(dgt, wpg_ref[...])

    def tok(n):
        return pl.BlockSpec((tm, n), lambda i: (i, 0))

    def full(shape):
        return pl.BlockSpec(shape, lambda i: (0,) * len(shape))

    return pl.pallas_call(
        body, name="ple_loss", grid=(T // tm,),
        in_specs=[tok(D), tok(D), tok(PD), full((4, PD, 256)), full((D, D)), full((1, D)), full((1, D)), tok(D)],
        out_specs=[full((1, 128)), tok(D), tok(PD), tok(D), tok(D), full((1, D)), full((1, D))],
        out_shape=_hbm_out([jax.ShapeDtypeStruct((1, 128), F32), jax.ShapeDtypeStruct((T, D), F32),
                            jax.ShapeDtypeStruct((T, PD), BF16), jax.ShapeDtypeStruct((T, D), BF16),
                            jax.ShapeDtypeStruct((T, D), BF16), jax.ShapeDtypeStruct((1, D), F32),
                            jax.ShapeDtypeStruct((1, D), F32)]),
        compiler_params=_cp(48, 1),
    )(*_hbm(x3, x3b, p, w_pi4, w_pg, g, b, target))


def _mix_bwd_b(dy, r2, g, w_mo, g_conv, g_ssm, y_conv, ga, gb, s, su, dvec, w_glu4, wc_re4, wc_im4, tm, comm=()):
    T = dy.shape[0]

    def body(dy_ref, r_ref, g_ref, wmo_ref, gc_ref, gs_ref, yc_ref, ga_ref, gb_ref, s_ref, su_ref, d_ref,
             wg_ref, wcr_ref, wci_ref,
             dres_ref, dmix_ref, dgl_ref, dsb_ref, dud_ref, gsr_ref, gsi_ref, dyc_ref, dp_ref,
             dg_ref, db_ref, dd_ref):
        i = pl.program_id(0)
        dyv = dy_ref[...]
        dr, xhat = _ln_bwd(dyv, r_ref[...], g_ref[...])
        dmix = dr.astype(BF16)
        dmerged = _nt(dmix, wmo_ref[...])
        sc, ss, sgb = _sig(gc_ref[...]), _sig(gs_ref[...]), _sig(gb_ref[...])
        gav = ga_ref[...]
        yssm = gav * sgb
        dgc = dmerged * yc_ref[...] * sc * (1.0 - sc)
        dgss = dmerged * yssm * ss * (1.0 - ss)
        dyssm = dmerged * ss
        dgl = jnp.concatenate([dyssm * sgb, dyssm * gav * sgb * (1.0 - sgb)], axis=1).astype(BF16)
        dsg = (_nt(dgl[:, 0:512], wg_ref[0]) + _nt(dgl[:, 512:1024], wg_ref[1])
               + _nt(dgl[:, 1024:1536], wg_ref[2]) + _nt(dgl[:, 1536:2048], wg_ref[3]))
        sv = s_ref[...]
        _, th = _gelu(sv)
        dgelu = 0.5 * (1.0 + th) + 0.5 * sv * (1.0 - th * th) * GELU_C * (1.0 + 3.0 * 0.044715 * sv * sv)
        ds = dsg * dgelu
        dsb = ds.astype(BF16)
        pg, pb, pd = _rowsum(dyv * xhat), _rowsum(dyv), _rowsum(ds * su_ref[...])

        @pl.when(i == 0)
        def _():
            dg_ref[...] = pg
            db_ref[...] = pb
            dd_ref[...] = pd

        @pl.when(i > 0)
        def _():
            dg_ref[...] += pg
            db_ref[...] += pb
            dd_ref[...] += pd

        dres_ref[...] = ALPHA * dr
        dmix_ref[...] = dmix
        dgl_ref[...] = dgl
        dsb_ref[...] = dsb
        dud_ref[...] = ds * d_ref[...]
        for J in range(4):
            gsr_ref[:, 512 * J:512 * (J + 1)] = _nt(dsb[:, 128 * J:128 * (J + 1)], wcr_ref[J])
            gsi_ref[:, 512 * J:512 * (J + 1)] = _nt(dsb[:, 128 * J:128 * (J + 1)], wci_ref[J])
        dyc_ref[...] = (dmerged * sc).astype(BF16)
        dp_ref[:, 0:D] = dgc.astype(BF16)
        dp_ref[:, D:2 * D] = dgss.astype(BF16)

    def tok(n):
        return pl.BlockSpec((tm, n), lambda i: (i, 0))

    def full(shape):
        return pl.BlockSpec(shape, lambda i: (0,) * len(shape))

    return _pcall(
        body, name="mix_bwd_b", grid=(T // tm,),
        in_specs=[tok(D), tok(D), full((1, D)), full((D, D)), tok(D), tok(D), tok(D), tok(D), tok(D),
                  tok(SSM), tok(SSM), full((1, SSM)), full((4, SSM, 512)), full((4, 512, 128)), full((4, 512, 128))],
        out_specs=[tok(D), tok(D), tok(2 * D), tok(SSM), tok(SSM), tok(LANES), tok(LANES), tok(D),
                   pl.BlockSpec((tm, 2 * D), lambda i: (i, 1)), full((1, D)), full((1, D)), full((1, SSM))],
        out_shape=[jax.ShapeDtypeStruct((T, D), F32), jax.ShapeDtypeStruct((T, D), BF16),
                   jax.ShapeDtypeStruct((T, 2 * D), BF16), jax.ShapeDtypeStruct((T, SSM), BF16),
                   jax.ShapeDtypeStruct((T, SSM), F32), jax.ShapeDtypeStruct((T, LANES), F32),
                   jax.ShapeDtypeStruct((T, LANES), F32), jax.ShapeDtypeStruct((T, D), BF16),
                   jax.ShapeDtypeStruct((T, 4 * D), BF16), jax.ShapeDtypeStruct((1, D), F32),
                   jax.ShapeDtypeStruct((1, D), F32), jax.ShapeDtypeStruct((1, SSM), F32)],
        vmem_mb=56, comm=comm,
        operands=(dy, r2, g, w_mo, g_conv, g_ssm, y_conv, ga, gb, s, su, dvec, w_glu4, wc_re4, wc_im4))


def _s5_scan_bwd(gs_re, gs_im, st_re, st_im, su_b, ds_b, wb_re, wb_im, a_re, a_im, comm=()):
    T = su_b.shape[0]
    W = SCAN_W
    R = SCAN_R

    def body(gr_ref, gi_ref, sr_ref, si_ref, su_ref, ds_ref, wbr_ref, wbi_ref, ar_ref, ai_ref,
             dsu_ref, dwbr_ref, dwbi_ref, dwcr_ref, dwci_ref, dar_ref, dai_ref, gre, gim):
        j = pl.program_id(0)
        zero = jnp.zeros((8, W), F32)
        for buf in (gre, gim):
            buf[pl.ds(T + 8, 8), :] = zero
        _scan_rows(gre, gim, ar_ref[...], ai_ref[...], T, True,
                   lambda t0: (gr_ref[pl.ds(t0, R), :], gi_ref[pl.ds(t0, R), :]))
        grb = gre[pl.ds(8, T), :].astype(BF16)
        gib = gim[pl.ds(8, T), :].astype(BF16)
        part = _nt(grb, wbr_ref[...]) + _nt(gib, wbi_ref[...])

        @pl.when(j % SCAN_PER == 0)
        def _():
            dsu_ref[...] = part

        @pl.when(j % SCAN_PER > 0)
        def _():
            dsu_ref[...] += part

        su = su_ref[...]
        dwbr_ref[...] = _tn(su, grb)
        dwbi_ref[...] = _tn(su, gib)
        dsv = ds_ref[...]
        dwcr_ref[...] = _tn(sr_ref[...].astype(BF16), dsv)
        dwci_ref[...] = _tn(si_ref[...].astype(BF16), dsv)
        dar = jnp.zeros((1, W), F32)
        dai = jnp.zeros((1, W), F32)
        for c in range(T // R):
            xr = sr_ref[pl.ds(c * R, R), :]
            xi = si_ref[pl.ds(c * R, R), :]
            g1r = gre[pl.ds(c * R + 9, R), :]
            g1i = gim[pl.ds(c * R + 9, R), :]
            dar = dar + _rowsum(g1r * xr + g1i * xi)
            dai = dai + _rowsum(g1i * xr - g1r * xi)
        dar_ref[...] = dar
        dai_ref[...] = dai

    lane, col, wb, wc, vec = _scan_specs(T)
    return _pcall(
        body, name="s5_scan_bwd", grid=(LANES // W,),
        in_specs=[lane, lane, lane, lane, col, col, wb, wb, vec, vec],
        out_specs=[col, wb, wb, wc, wc, vec, vec],
        out_shape=[jax.ShapeDtypeStruct((T, SSM), F32),
                   jax.ShapeDtypeStruct((LANES // W, 128, W), F32), jax.ShapeDtypeStruct((LANES // W, 128, W), F32),
                   jax.ShapeDtypeStruct((LANES // W, W, 128), F32), jax.ShapeDtypeStruct((LANES // W, W, 128), F32),
                   jax.ShapeDtypeStruct((1, LANES), F32), jax.ShapeDtypeStruct((1, LANES), F32)],
        scratch=[pltpu.VMEM((T + 16, W), F32)] * 2, vmem_mb=56, comm=comm,
        operands=(gs_re, gs_im, st_re, st_im, su_b, ds_b, wb_re, wb_im, a_re, a_im))


def _mix_bwd_a(dyc_b, w_co4, pc, z_b, conv_w, dsu_ssm, du_dir, dproj, dres, w_mix4, tm, comm=()):
    T = dres.shape[0]
    nt = T // tm

    def body(dyc_ref, wco_ref, pc_ref, halo_ref, z_ref, cw_ref, dsu_ref, dud_ref, dpin_ref, dres_ref, w_ref,
             dp_ref, dx_ref, dcw_ref, dcb_ref, dzbuf, qbuf):
        i = pl.program_id(0)
        ii = nt - 1 - i

        @pl.when(i == 0)
        def _():
            dzbuf[pl.ds(tm, 8), :] = jnp.zeros((8, CONV), F32)

        dyc = dyc_ref[...]
        dyin = (_nt(dyc[:, 0:256], wco_ref[0]) + _nt(dyc[:, 256:512], wco_ref[1])
                + _nt(dyc[:, 512:768], wco_ref[2]) + _nt(dyc[:, 768:1024], wco_ref[3]))
        cbv = pc_ref[:, 0:CONV].astype(F32)
        ccv = pc_ref[:, CONV:2 * CONV].astype(F32)
        chv = pc_ref[:, 2 * CONV:3 * CONV].astype(F32)
        dcbv = dyin * z_ref[...].astype(F32)
        dz = dyin * cbv
        dzbuf[pl.ds(0, tm), :] = dz
        cw = cw_ref[...]
        dq = cw[2:3] * dz + cw[1:2] * dzbuf[pl.ds(1, tm), :] + cw[0:1] * dzbuf[pl.ds(2, tm), :]
        dzbuf[pl.ds(tm, 8), :] = dz[0:8]
        q = ccv * chv
        hq = halo_ref[:, CONV:2 * CONV].astype(F32) * halo_ref[:, 2 * CONV:3 * CONV].astype(F32)
        qbuf[pl.ds(0, 8), :] = jnp.where(ii > 0, hq, jnp.zeros_like(hq))
        qbuf[pl.ds(8, tm), :] = q
        pw = jnp.concatenate([_rowsum(dz * qbuf[pl.ds(6, tm), :]), _rowsum(dz * qbuf[pl.ds(7, tm), :]),
                              _rowsum(dz * q), jnp.zeros((5, CONV), F32)], axis=0)
        pbias = _rowsum(dz)

        @pl.when(i == 0)
        def _():
            dcw_ref[...] = pw
            dcb_ref[...] = pbias

        @pl.when(i > 0)
        def _():
            dcw_ref[...] += pw
            dcb_ref[...] += pbias

        dp0 = jnp.concatenate([dcbv, dq * chv], axis=1).astype(BF16)
        dp1 = jnp.concatenate([dq * ccv, dsu_ref[...] + dud_ref[...]], axis=1).astype(BF16)
        dp_ref[:, 0:D] = dp0
        dp_ref[:, D:2 * D] = dp1
        dx_ref[...] = (dres_ref[...] + _nt(dp0, w_ref[0]) + _nt(dp1, w_ref[1])
                       + _nt(dpin_ref[:, 0:D], w_ref[2]) + _nt(dpin_ref[:, D:2 * D], w_ref[3]))

    def tok(n):
        return pl.BlockSpec((tm, n), lambda i: (nt - 1 - i, 0))

    def full(shape):
        return pl.BlockSpec(shape, lambda i: (0,) * len(shape))

    halo = pl.BlockSpec((8, 3 * CONV), lambda i: (jnp.maximum((nt - 1 - i) * (tm // 8) - 1, 0), 0))
    return _pcall(
        body, name="mix_bwd_a", grid=(nt,),
        in_specs=[tok(D), full((4, CONV, 256)), tok(3 * CONV), halo, tok(CONV), full((3, CONV)),
                  tok(SSM), tok(SSM), pl.BlockSpec((tm, 2 * D), lambda i: (nt - 1 - i, 1)), tok(D),
                  full((4, D, D))],
        out_specs=[pl.BlockSpec((tm, 2 * D), lambda i: (nt - 1 - i, 0)), tok(D), full((8, CONV)), full((1, CONV))],
        out_shape=[jax.ShapeDtypeStruct((T, 4 * D), BF16), jax.ShapeDtypeStruct((T, D), F32),
                   jax.ShapeDtypeStruct((8, CONV), F32), jax.ShapeDtypeStruct((1, CONV), F32)],
        scratch=[pltpu.VMEM((tm + 8, CONV), F32), pltpu.VMEM((tm + 8, CONV), F32)],
        aliases={8: 0}, vmem_mb=56, comm=comm,
        operands=(dyc_b, w_co4, pc, pc, z_b, conv_w, dsu_ssm, du_dir, dproj, dres, w_mix4))


def _zoh(lam_re, lam_im, log_step, b_re, b_im):
    dt = jnp.exp(log_step)[:, None]
    mag = jnp.exp(lam_re * dt)
    abr, abi = mag * jnp.cos(lam_im * dt), mag * jnp.sin(lam_im * dt)
    nr, ni = abr - 1.0, abi
    den = lam_re * lam_re + lam_im * lam_im
    cr = (nr * lam_re + ni * lam_im) / den
    ci = (ni * lam_re - nr * lam_im) / den
    bbr = cr[..., None] * b_re - ci[..., None] * b_im
    bbi = cr[..., None] * b_im + ci[..., None] * b_re
    return abr, abi, bbr, bbi


_WB_MASK = (np.arange(8)[None, :, None]
            == SCAN_GR * np.arange(SCAN_PER)[:, None, None] + np.arange(SCAN_GR)[None, None, :]).astype(np.float32)
_EYE8 = np.eye(8, dtype=np.float32)


def _wb_blocks(bb):
    bt = bb.transpose(0, 2, 1).reshape(4, 1, 8, 16, 1, STATE)
    full = bt * _WB_MASK[None, :, :, None, :, None]
    return full.reshape(LANES // SCAN_W, 128, SCAN_W).astype(BF16)


def _wc_blocks(cc):
    ct = cc.transpose(0, 2, 1).reshape(4, 8, STATE, 1, 16)
    full = ct * _EYE8[None, :, None, :, None]
    return full.reshape(4, 512, 128).astype(BF16)


def _wb_diag(dwb):
    d6 = dwb.reshape(4, SCAN_PER, 8, 16, SCAN_GR, STATE) * _WB_MASK[None, :, :, None, :, None]
    return d6.sum(axis=(1, 4)).reshape(GROUPS, 16, STATE).transpose(0, 2, 1)


def _wc_diag(dwc):
    mask = _WB_MASK.transpose(0, 2, 1)
    d6 = dwc.reshape(4, SCAN_PER, SCAN_GR, STATE, 8, 16) * mask[None, :, :, None, :, None]
    return d6.sum(axis=4).reshape(GROUPS, STATE, 16).transpose(0, 2, 1)


def _where():
    x, y, c = lax.axis_index("x"), lax.axis_index("y"), lax.axis_index("c")
    return x, y, c, 2 * x + y


def _chip_dev(k, c):
    return (k // 2, k % 2, c)


def _slot_cast(meidx, w, dtype, name, token=()):
    R, C = w.shape
    tr = _row_tile(R)

    def body(m_ref, w_ref, *rest):
        rest[-1][...] = w_ref[...].astype(dtype)

    gs = pltpu.PrefetchScalarGridSpec(
        num_scalar_prefetch=1, grid=(R // tr,),
        in_specs=[pl.BlockSpec((tr, C), lambda i, m: (i, 0))] + [pl.BlockSpec((8, 128), lambda i, m: (0, 0))] * len(token),
        out_specs=pl.BlockSpec((None, tr, C), lambda i, m: (m[0], i, 0)))
    return pl.pallas_call(
        body, name=name, grid_spec=gs, out_shape=_hbm_out(jax.ShapeDtypeStruct((4, R, C), dtype)),
        compiler_params=_cp(32, 1),
    )(meidx, *_hbm(w), *token)


def _gather_ici_payload(bufs):
    def copies(ins, lnd, ss, rs):
        x, y, c, me = _where()
        cps = []
        for w, b in enumerate(bufs):
            h = b.shape[1] // 2
            mine = lnd[w].at[me, pl.ds(c * h, h)]
            for s in range(3):
                k = (me + 1 + s) % 4
                cps.append(pltpu.make_async_remote_copy(
                    src_ref=mine, dst_ref=mine, send_sem=ss.at[3 * w + s], recv_sem=rs.at[3 * w + s],
                    device_id=_chip_dev(k, c), device_id_type=MESH))
        return cps

    p = _sym_payload([], [jax.ShapeDtypeStruct(b.shape, b.dtype) for b in bufs], copies, 3 * len(bufs))
    p.lands = list(bufs)
    return p


def _gather_pass_payload(bufs):
    def copies(ins, outs, ss, rs):
        x, y, c, me = _where()
        cps = []
        for w, b in enumerate(bufs):
            h = b.shape[1] // 2
            for s in range(3):
                j = (me + 1 + s) % 4
                cps.append(pltpu.make_async_remote_copy(
                    src_ref=ins[w].at[j, pl.ds(c * h, h)], dst_ref=outs[w].at[j, pl.ds(c * h, h)],
                    send_sem=ss.at[3 * w + s], recv_sem=rs.at[3 * w + s], device_id=(x, y, 1 - c),
                    device_id_type=MESH))
        return cps

    p = _sym_payload(bufs, [jax.ShapeDtypeStruct(b.shape, b.dtype) for b in bufs], copies, 3 * len(bufs))
    p.aliases = {w: w for w in range(len(bufs))}
    return p


def _gather_payload(bufs):
    n = len(bufs)

    def half(ref, w, k, cc):
        h = bufs[w].shape[1] // 2
        return ref.at[k, pl.ds(cc * h, h)]

    def ici(ins, outs, sems, w, s):
        x, y, c, me = _where()
        k = (me + 1 + s) % 4
        return pltpu.make_async_remote_copy(
            src_ref=half(ins[w], w, me, c), dst_ref=half(outs[w], w, me, c), send_sem=sems[0].at[3 * w + s],
            recv_sem=sems[1].at[3 * w + s], device_id=_chip_dev(k, c), device_id_type=MESH)

    def landed(outs, sems, w, s):
        x, y, c, me = _where()
        j = (me + 3 - s) % 4
        return pltpu.make_async_remote_copy(
            src_ref=half(outs[w], w, j, c), dst_ref=half(outs[w], w, j, c), send_sem=sems[0].at[3 * w + s],
            recv_sem=sems[1].at[3 * w + s], device_id=(x, y, 1 - c), device_id_type=MESH)

    def passed(outs, sems, w, s, cc):
        x, y, c, me = _where()
        j = (me + 3 - s) % 4
        return pltpu.make_async_remote_copy(
            src_ref=half(outs[w], w, j, cc), dst_ref=half(outs[w], w, j, cc), send_sem=sems[2].at[3 * w + s],
            recv_sem=sems[3].at[3 * w + s], device_id=(x, y, 1 - c), device_id_type=MESH)

    pairs = [(w, s) for w in range(n) for s in range(3)]

    def start(ins, outs, sems):
        for w, s in pairs:
            ici(ins, outs, sems, w, s).start()

    def finish(ins, outs, sems):
        _, _, c, _ = _where()
        for w, s in pairs:
            landed(outs, sems, w, s).wait_recv()
            passed(outs, sems, w, s, c).start()
        for w, s in pairs:
            passed(outs, sems, w, s, 1 - c).wait_recv()
        for w, s in pairs:
            ici(ins, outs, sems, w, s).wait_send()
            passed(outs, sems, w, s, c).wait_send()

    return _Payload(bufs, [jax.ShapeDtypeStruct(b.shape, b.dtype) for b in bufs], {w: w for w in range(n)},
                    [pltpu.SemaphoreType.DMA((3 * n,))] * 4, start, finish)


def _sym_payload(operands, outs, copies, n_copies):
    def start(ins, outs_, sems):
        for cp in copies(ins, outs_, sems[0], sems[1]):
            cp.start()

    def finish(ins, outs_, sems):
        for cp in copies(ins, outs_, sems[0], sems[1]):
            cp.wait()

    p = _Payload(operands, outs, {}, [pltpu.SemaphoreType.DMA((n_copies,))] * 2, start, finish)
    p.copies, p.n_copies = copies, n_copies
    return p


def _swap_payload(g4s):
    def copies(ins, outs, ss, rs):
        x, y, c, me = _where()
        cps = []
        for w, g in enumerate(g4s):
            h = g.shape[1] // 2
            cps.append(pltpu.make_async_remote_copy(
                src_ref=ins[w].at[:, pl.ds((1 - c) * h, h)], dst_ref=outs[w], send_sem=ss.at[w],
                recv_sem=rs.at[w], device_id=(x, y, 1 - c), device_id_type=MESH))
        return cps

    outs = [jax.ShapeDtypeStruct((4, g.shape[1] // 2, g.shape[2]), g.dtype) for g in g4s]
    return _sym_payload(g4s, outs, copies, len(g4s))


def _exchange_payload(pbs):
    def copies(ins, outs, ss, rs):
        x, y, c, me = _where()
        cps = []
        for w in range(len(pbs)):
            for s in range(3):
                k = (me + 1 + s) % 4
                cps.append(pltpu.make_async_remote_copy(
                    src_ref=ins[w].at[k], dst_ref=outs[w].at[2 - s], send_sem=ss.at[3 * w + s],
                    recv_sem=rs.at[3 * w + s], device_id=_chip_dev(k, c), device_id_type=MESH))
        return cps

    outs = [jax.ShapeDtypeStruct((3,) + p.shape[1:], p.dtype) for p in pbs]
    return _sym_payload(pbs, outs, copies, 3 * len(pbs))


HBM_REF = pl.BlockSpec(memory_space=pltpu.HBM)
SEM_REF = pl.BlockSpec(memory_space=pltpu.SEMAPHORE)
DATAFLOW = pltpu.SideEffectType.DATAFLOW_SIDE_EFFECTING


class _SemList:
    def __init__(self, refs):
        self.refs = refs

    @property
    def at(self):
        return self.refs


def _split_start(p, name):
    n_in, n_out, nc = len(p.operands), len(p.outs), p.n_copies
    lands = getattr(p, "lands", None) or [lax.empty(s.shape, s.dtype) for s in p.outs]

    def body(*refs):
        ins, lnd = refs[:n_in], refs[n_in:n_in + n_out]
        sems = refs[n_in + n_out:n_in + n_out + 2 * nc]
        for cp in p.copies(ins, lnd, _SemList(sems[:nc]), _SemList(sems[nc:])):
            cp.start()
        refs[-1][...] = jnp.zeros((8, 128), F32)

    res = pl.pallas_call(
        body, name=name,
        in_specs=[HBM_REF] * (n_in + n_out),
        out_specs=[SEM_REF] * (2 * nc) + [HBM_REF] * (n_in + n_out) + [VMEM_FULL],
        out_shape=([pltpu.SemaphoreType.DMA(())] * (2 * nc) + _hbm_out(p.operands) + _hbm_out(lands)
                   + [jax.ShapeDtypeStruct((8, 128), F32)]),
        input_output_aliases={i: 2 * nc + i for i in range(n_in + n_out)},
        compiler_params=pltpu.CompilerParams(has_side_effects=DATAFLOW),
    )(*_hbm(*p.operands, *lands))
    k = 2 * nc
    return list(res[:k]), list(res[k:k + n_in]), list(res[k + n_in:k + n_in + n_out]), res[-1]


def _split_wait(p, handle, after, name):
    sems, srcs, lands, _ = handle
    n_in, n_out, nc = len(srcs), len(lands), p.n_copies

    def body(*refs):
        ins, lnd = refs[:n_in], refs[n_in:n_in + n_out]
        sm = refs[n_in + n_out:n_in + n_out + 2 * nc]
        for cp in p.copies(ins, lnd, _SemList(sm[:nc]), _SemList(sm[nc:])):
            cp.wait_send()
            cp.wait_recv()

    res = pl.pallas_call(
        body, name=name,
        in_specs=[HBM_REF] * (n_in + n_out) + [SEM_REF] * (2 * nc) + [ANY] * len(after),
        out_specs=[HBM_REF] * (n_in + n_out), out_shape=_hbm_out(srcs) + _hbm_out(lands),
        input_output_aliases={i: i for i in range(n_in + n_out)},
        compiler_params=pltpu.CompilerParams(has_side_effects=DATAFLOW),
    )(*srcs, *lands, *sems, *after)
    return list(res[:n_in]), list(res[n_in:])


def _join_payload(halves):
    def copies(ins, outs, ss, rs):
        x, y, c, me = _where()
        return [pltpu.make_async_remote_copy(
            src_ref=ins[w], dst_ref=outs[w], send_sem=ss.at[w], recv_sem=rs.at[w],
            device_id=(x, y, 1 - c), device_id_type=MESH) for w in range(len(halves))]

    outs = [jax.ShapeDtypeStruct(a.shape, a.dtype) for a in halves]
    return _sym_payload(halves, outs, copies, len(halves))


def _allgather_payload(v):
    def copies(ins, outs, ss, rs):
        x, y, c, me = _where()
        lin = 4 * x + 2 * y + c
        cps = []
        for o in range(1, 8):
            t = (lin + o) % 8
            cps.append(pltpu.make_async_remote_copy(
                src_ref=ins[0], dst_ref=outs[0].at[lin], send_sem=ss.at[o - 1], recv_sem=rs.at[o - 1],
                device_id=(t // 4, (t // 2) % 2, t % 2), device_id_type=MESH))
        return cps

    p = _sym_payload([v], [jax.ShapeDtypeStruct((8,) + v.shape, v.dtype)], copies, 7)
    x, y, c, _ = _where()
    p.lands = [lax.dynamic_update_slice(jnp.zeros((8,) + v.shape, v.dtype), v[None], (4 * x + 2 * y + c, 0, 0))]
    return p


def _sum8(buf, token):
    _, P, C = buf.shape

    def body(b_ref, t_ref, o_ref):
        acc = b_ref[0]
        for d in range(1, 8):
            acc = acc + b_ref[d]
        o_ref[...] = acc

    return pl.pallas_call(
        body, name="sum8", in_specs=[VMEM_FULL, VMEM_FULL], out_specs=VMEM_FULL,
        out_shape=jax.ShapeDtypeStruct((P, C), F32),
        compiler_params=pltpu.CompilerParams(vmem_limit_bytes=32 << 20),
    )(buf, token)


def _row_tile(h):
    for t in (256, 176, 128, 64, 32, 16, 8):
        if h % t == 0:
            return t
    raise ValueError(h)


def _pair_sum(cmidx, g4, got, name):
    _, R, C = g4.shape
    h = R // 2
    th = _row_tile(h)

    def body(cm_ref, a_ref, b_ref, o_ref, ob_ref):
        sm = a_ref[...] + b_ref[...]
        ob_ref[...] = sm.astype(BF16)

        @pl.when(pl.program_id(1) == cm_ref[1])
        def _():
            o_ref[...] = sm

    blk = pl.BlockSpec((None, th, C), lambda i, k, cm: (k, i, 0))
    gs = pltpu.PrefetchScalarGridSpec(
        num_scalar_prefetch=1, grid=(h // th, 4),
        in_specs=[pl.BlockSpec((None, None, th, C), lambda i, k, cm: (k, cm[0], i, 0)), blk],
        out_specs=[pl.BlockSpec((th, C), lambda i, k, cm: (i, 0)), blk])
    return pl.pallas_call(
        body, name=name, grid_spec=gs,
        out_shape=_hbm_out([jax.ShapeDtypeStruct((h, C), F32), jax.ShapeDtypeStruct((4, h, C), BF16)]),
        compiler_params=_cp(32, 2),
    )(cmidx, *_hbm(g4.reshape(4, 2, h, C), got))


def _chip_sum(own, got, name):
    h, C = own.shape
    th = _row_tile(h)

    def body(a_ref, b_ref, o_ref):
        o_ref[...] = ((a_ref[...] + b_ref[0].astype(F32)) + b_ref[1].astype(F32)) + b_ref[2].astype(F32)

    return pl.pallas_call(
        body, name=name, grid=(h // th,),
        in_specs=[pl.BlockSpec((th, C), lambda i: (i, 0)), pl.BlockSpec((3, th, C), lambda i: (0, i, 0))],
        out_specs=pl.BlockSpec((th, C), lambda i: (i, 0)),
        out_shape=_hbm_out(jax.ShapeDtypeStruct((h, C), F32)),
        compiler_params=_cp(32, 1),
    )(*_hbm(own, got))


def _adamw_math(w, g, m, v):
    m2 = B1 * m + (1.0 - B1) * g
    v2 = B2 * v + (1.0 - B2) * (g * g)
    m_hat = m2 / (1.0 - B1 ** STEP)
    v_hat = v2 / (1.0 - B2 ** STEP)
    delta = -LR * (m_hat / (jnp.sqrt(v_hat) + EPS) + WD * w)
    return delta, m2, v2


def _adamw_pair(cidx, w, mine, theirs, m, v, token, name):
    R, C = w.shape
    h = R // 2
    tr = _row_tile(h)
    nh = h // tr

    def body(c_ref, w_ref, a_ref, b_ref, m_ref, v_ref, t_ref, g_ref, d_ref, mo_ref, vo_ref):
        own = (pl.program_id(0) // nh) == c_ref[0]
        g = jnp.where(own, a_ref[...], b_ref[...])
        d, m2, v2 = _adamw_math(w_ref[...], g, m_ref[...], v_ref[...])
        g_ref[...] = g
        d_ref[...] = d
        mo_ref[...] = m2
        vo_ref[...] = v2

    blk = pl.BlockSpec((tr, C), lambda i, c: (i, 0))
    mine_blk = pl.BlockSpec((tr, C), lambda i, c: (jnp.clip(i - c[0] * nh, 0, nh - 1), 0))
    theirs_blk = pl.BlockSpec((tr, C), lambda i, c: (jnp.clip(i - (1 - c[0]) * nh, 0, nh - 1), 0))
    gs = pltpu.PrefetchScalarGridSpec(
        num_scalar_prefetch=1, grid=(R // tr,),
        in_specs=[blk, mine_blk, theirs_blk, blk, blk, pl.BlockSpec((8, 128), lambda i, c: (0, 0))],
        out_specs=[blk] * 4)
    return pl.pallas_call(
        body, name=name, grid_spec=gs, out_shape=_hbm_out([jax.ShapeDtypeStruct((R, C), F32)] * 4),
        compiler_params=_cp(32, 1),
    )(cidx, *_hbm(w, mine, theirs, m, v), token)


def _adamw(w, g, m, v, name):
    R, C = w.shape
    tr = _row_tile(R)

    def body(w_ref, g_ref, m_ref, v_ref, d_ref, mo_ref, vo_ref):
        d, m2, v2 = _adamw_math(w_ref[...], g_ref[...], m_ref[...], v_ref[...])
        d_ref[...] = d
        mo_ref[...] = m2
        vo_ref[...] = v2

    blk = pl.BlockSpec((tr, C), lambda i: (i, 0))
    return pl.pallas_call(
        body, name=name, grid=(R // tr,), in_specs=[blk] * 4, out_specs=[blk] * 3,
        out_shape=_hbm_out([jax.ShapeDtypeStruct((R, C), F32)] * 3),
        compiler_params=_cp(32, 1),
    )(*_hbm(w, g, m, v))


def _pack(arrs):
    flat = jnp.concatenate([a.reshape(-1).astype(F32) for a in arrs])
    rows = -(-flat.shape[0] // 1024)
    rows = -(-rows // 8) * 8
    return jnp.pad(flat, (0, rows * 1024 - flat.shape[0])).reshape(rows, 1024)


def _unpack(packed, shapes):
    flat = packed.reshape(-1)
    out, off = [], 0
    for s in shapes:
        n = math.prod(s)
        out.append(flat[off:off + n].reshape(s))
        off += n
    return out


BIG = ["ffn1_w_in", "ffn1_w_out", "mix_w_in", "conv_w_out", "ssm_w_glu", "mix_w_out",
       "ffn2_w_in", "ffn2_w_out", "ple_w_in", "ple_w_gate"]
SMALL = ["ln1_g", "ln1_b", "conv_w", "conv_b", "ssm_lam_re", "ssm_lam_im", "ssm_log_step", "ssm_b_re", "ssm_b_im",
         "ssm_c_re", "ssm_c_im", "ssm_d", "ln2_g", "ln2_b", "ln3_g", "ln3_b", "ln4_g", "ln4_b"]
WEIGHTS = ["ffn1_w_in", "ffn1_w_out", "ln1_g", "ln1_b", "mix_w_in", "conv_w", "conv_b", "conv_w_out",
           "ssm_lam_re", "ssm_lam_im", "ssm_log_step", "ssm_b_re", "ssm_b_im", "ssm_c_re", "ssm_c_im", "ssm_d",
           "ssm_w_glu", "mix_w_out", "ln2_g", "ln2_b", "ffn2_w_in", "ffn2_w_out", "ln3_g", "ln3_b",
           "ple_w_in", "ple_w_gate", "ln4_g", "ln4_b"]


class _NoComm:
    def __init__(self, W):
        self.W, self.G, self.raw, self.done = dict(W), {}, None, {}

    def carry(self, name):
        return ()

    def landed(self, name, got):
        pass

    def grad(self, name, g4):
        self.G[name] = g4

    def small(self, raw):
        self.raw = raw


def _s5_operands(sp):
    abr, abi, bbr, bbi = _zoh(sp["ssm_lam_re"], sp["ssm_lam_im"], sp["ssm_log_step"], sp["ssm_b_re"], sp["ssm_b_im"])
    return (_wb_blocks(bbr), _wb_blocks(bbi), _wc_blocks(sp["ssm_c_re"]), _wc_blocks(-sp["ssm_c_im"]),
            abr.reshape(1, LANES), abi.reshape(1, LANES), sp["ssm_d"].reshape(1, SSM))


def _local_step(x, p, target, sp, sched, tm_ffn, tm_mix, ops=None):
    W = sched.W
    wb_re, wb_im, wc_re4, wc_im4, a_re, a_im, dvec = ops if ops is not None else _s5_operands(sp)

    def run(fn, name, *args, **kw):
        outs, got = fn(*args, comm=sched.carry(name), **kw)
        sched.landed(name, got)
        sched.done[name] = outs[0]
        return outs

    def dw(name, wname, a, b, tk, tn, shape4, shard_cols=None, interleaved=False):
        out, got = _mm_tn(a, b, tk, tn, name, shard_cols=shard_cols, interleaved=interleaved,
                          comm=sched.carry(name))
        sched.landed(name, got)
        sched.done[name] = out
        sched.grad(wname, out.reshape(shape4))

    xb = x.astype(BF16)
    h1, r1, x1, x1b = run(_ffn_fwd, "ffn1_fwd", x, xb, W["ffn1_w_in"], W["ffn1_w_out"].reshape(2, FFH, D),
                          sp["ln1_g"], sp["ln1_b"], tm_ffn, "ffn1_fwd")
    conv_w = W["conv_w"][:, 0:3, :].transpose(1, 0, 2).reshape(3, CONV)
    pc, z_b, yin_b, su, su_b, g_conv, g_ssm, y_conv = run(
        _mix_fwd_a, "mix_fwd_a", x1b, W["mix_w_in"], conv_w, sp["conv_b"], W["conv_w_out"], tm_mix)
    st_re, st_im = run(_s5_scan_fwd, "s5_scan_fwd", su_b, wb_re, wb_im, a_re, a_im)
    w_mo = W["mix_w_out"].reshape(D, D)
    s, sg_b, ga, gb, merged_b, r2, x2, x2b = run(
        _mix_fwd_b, "mix_fwd_b", st_re, st_im, wc_re4, wc_im4, su, dvec, W["ssm_w_glu"], g_conv, g_ssm, y_conv,
        w_mo, x1, sp["ln2_g"], sp["ln2_b"], tm_mix)
    w2o2 = W["ffn2_w_out"].reshape(2, FFH, D)
    h2, r3, x3, x3b = run(_ffn_fwd, "ffn2_fwd", x2, x2b, W["ffn2_w_in"], w2o2, sp["ln3_g"], sp["ln3_b"], tm_ffn,
                          "ffn2_fwd")
    loss_part, dx3, p_b, dpw_b, dgt_b, dg4, db4 = _ple_loss(
        x3, x3b, p, W["ple_w_in"], W["ple_w_gate"].reshape(D, D), sp["ln4_g"], sp["ln4_b"], target, tm_mix)

    dw("dw_ple_gate", "ple_w_gate", x3b, dgt_b, 512, 1024, (4, 256, D))
    dw("dw_ple_in", "ple_w_in", p_b, dpw_b, 256, 256, (4, 256, 256), shard_cols=256)
    dx2, dh2, a2_b, df2_b, dg3, db3 = run(_ffn_bwd, "ffn2_bwd", dx3, r3, sp["ln3_g"], h2, W["ffn2_w_in"], w2o2,
                                          tm_mix, "ffn2_bwd")
    dw("dw_ffn2_in", "ffn2_w_in", x2b, dh2, 512, FFH, (4, D, FFH), shard_cols=FFH, interleaved=True)
    dw("dw_ffn2_out", "ffn2_w_out", a2_b, df2_b, FFH, 1024, (4, FF // 4, D))
    (dres, dmix_b, dgl_b, ds_b, du_dir, gs_re, gs_im, dyc_b, dproj, dg2, db2, dd) = run(
        _mix_bwd_b, "mix_bwd_b", dx2, r2, sp["ln2_g"], w_mo, g_conv, g_ssm, y_conv, ga, gb, s, su, dvec,
        W["ssm_w_glu"], wc_re4, wc_im4, tm_mix)
    dw("dw_mix_out", "mix_w_out", merged_b, dmix_b, 512, 1024, (4, 256, D))
    dw("dw_glu", "ssm_w_glu", sg_b, dgl_b, 512, 512, (4, SSM, 512), shard_cols=512)
    dsu_ssm, dwb_re, dwb_im, dwc_re, dwc_im, da_re, da_im = run(
        _s5_scan_bwd, "s5_scan_bwd", gs_re, gs_im, st_re, st_im, su_b, ds_b, wb_re, wb_im, a_re, a_im)
    dw("dw_conv_out", "conv_w_out", yin_b, dyc_b, 512, 256, (4, CONV, 256), shard_cols=256)
    dproj, dx1, dcw8, dcb = run(_mix_bwd_a, "mix_bwd_a", dyc_b, W["conv_w_out"], pc, z_b, conv_w, dsu_ssm,
                                du_dir, dproj, dres, W["mix_w_in"], tm_mix)
    dw("dw_mix_in", "mix_w_in", x1b, dproj, 512, 1024, (4, D, D), shard_cols=1024)
    dx0, dh1, a1_b, df1_b, dg1, db1 = run(_ffn_bwd, "ffn1_bwd", dx1, r1, sp["ln1_g"], h1, W["ffn1_w_in"],
                                          W["ffn1_w_out"].reshape(2, FFH, D), tm_mix, "ffn1_bwd")
    sched.small(dict(
        ln1_g=dg1, ln1_b=db1, ln2_g=dg2, ln2_b=db2, ln3_g=dg3, ln3_b=db3, ln4_g=dg4, ln4_b=db4,
        conv_w=dcw8[0:3], conv_b=dcb,
        a_re=da_re.reshape(GROUPS, STATE), a_im=da_im.reshape(GROUPS, STATE),
        bb_re=_wb_diag(dwb_re), bb_im=_wb_diag(dwb_im),
        ssm_c_re=_wc_diag(dwc_re), ssm_c_im=-_wc_diag(dwc_im), ssm_d=dd.reshape(GROUPS, 16),
        loss=loss_part[0:1, 0]))
    dw("dw_ffn1_in", "ffn1_w_in", xb, dh1, 512, FFH, (4, D, FFH), shard_cols=FFH, interleaved=True)
    dw("dw_ffn1_out", "ffn1_w_out", a1_b, df1_b, FFH, 1024, (4, FF // 4, D))
    return loss_part[0, 0], dx0


RAW_ORDER = ["ln1_g", "ln1_b", "ln2_g", "ln2_b", "ln3_g", "ln3_b", "ln4_g", "ln4_b", "conv_w", "conv_b",
             "a_re", "a_im", "bb_re", "bb_im", "ssm_c_re", "ssm_c_im", "ssm_d", "loss"]

GATHER_FIRST = ["ffn1_w_in", "ffn1_w_out"]
GATHER_AT = {"ffn1_fwd": ["mix_w_in", "conv_w_out", "conv_w"], "mix_fwd_a": ["ssm_w_glu", "mix_w_out"],
             "s5_scan_fwd": ["ffn2_w_in"], "mix_fwd_b": ["ffn2_w_out"], "ffn2_fwd": ["ple_w_in", "ple_w_gate"]}
REDUCE_GROUP = {"ple": ["ple_w_gate", "ple_w_in"], "ffn2": ["ffn2_w_in", "ffn2_w_out"],
                "mix": ["mix_w_out", "ssm_w_glu", "conv_w_out", "mix_w_in"], "ffn1": ["ffn1_w_in", "ffn1_w_out"]}
REDUCE_AT = {"ffn2_bwd": [("swap", "ple")], "dw_ffn2_in": [("exchange", "ple")],
             "mix_bwd_b": [("swap", "ffn2"), ("join", "ple")],
             "mix_bwd_a": [("join", "ffn2")], "ffn1_bwd": [("swap", "mix")]}
BEGIN_AT = {"dw_mix_out": [("exchange", "ffn2")], "dw_ffn1_in": [("small", None), ("exchange", "mix")]}
BEHIND = {"dw_glu": [("exchange", "ffn2")], "s5_scan_bwd": [("exchange", "ffn2")]}
END_AT = {"mix_bwd_a": [("exchange", "ffn2", ["dw_mix_out", "dw_glu", "s5_scan_bwd"])]}
LAST_GROUP = "ffn1"


class _Sched:
    def __init__(self, cmidx):
        self.bufs, self.cmidx = {}, cmidx
        self.W, self.G, self.raw, self.small_buf = {}, {}, None, None
        self.got1, self.p32, self.pbf, self.got2, self.half, self.theirs = {}, {}, {}, {}, {}, {}
        self._open, self._split, self.done = [], {}, {}

    def first_begin(self, bufs):
        self.bufs.update(bufs)
        p = _gather_ici_payload([bufs[n] for n in GATHER_FIRST])
        self._first = (p, _split_start(p, "gather_first_start"))
        return self._first[1][3]

    def first_end(self, bufs, after):
        self.bufs.update(bufs)
        p, handle = self._first
        _, landed = _split_wait(p, handle, after, "gather_first_wait")
        (outs,) = _comm_call("gather_first_pass", [_gather_pass_payload(landed)])
        self.W.update(zip(GATHER_FIRST, outs))

    def _payload(self, stage, key):
        if stage == "gather":
            return _gather_payload([self.bufs[n] for n in key])
        if stage == "small":
            return _allgather_payload(_pack([self.raw[k] for k in RAW_ORDER]))
        names = REDUCE_GROUP[key]
        if stage == "swap":
            return _swap_payload([self.G[n] for n in names])
        if stage == "exchange":
            for n in names:
                self.p32[n], self.pbf[n] = _pair_sum(self.cmidx, self.G[n], self.got1[n], "pair_sum_" + n)
            return _exchange_payload([self.pbf[n] for n in names])
        for n in names:
            self.half[n] = _chip_sum(self.p32[n], self.got2[n], "chip_sum_" + n)
        return _join_payload([self.half[n] for n in names])

    def _store(self, stages, got):
        for (stage, key), outs in zip(stages, got):
            if stage == "gather":
                self.W.update(zip(key, outs))
            elif stage == "small":
                self.small_buf = outs[0]
            else:
                {"swap": self.got1, "exchange": self.got2, "join": self.theirs}[stage].update(
                    zip(REDUCE_GROUP[key], outs))

    def _standalone(self, name, stages):
        self._store(stages, _comm_call(name, [self._payload(s, k) for s, k in stages]))

    def carry(self, name):
        for stage, key, behind in END_AT.get(name, []):
            self._end(stage, key, [self.done[b] for b in behind])
        tokens = [self._begin(stage, key) for stage, key in BEGIN_AT.get(name, [])]
        tokens += [self._split[sk][1][3] for sk in BEHIND.get(name, [])]
        self._open = [("gather", GATHER_AT[name])] if name in GATHER_AT else []
        self._open += REDUCE_AT.get(name, [])
        comm = [self._payload(s, k) for s, k in self._open]
        if tokens:
            comm.append(_Payload(tokens, [], {}, [], lambda *a: None, lambda *a: None))
        return tuple(comm)

    def landed(self, name, got):
        self._store(self._open, got)

    def grad(self, name, g4):
        self.G[name] = g4

    def small(self, raw):
        self.raw = raw

    def _begin(self, stage, key):
        p = self._payload(stage, key)
        self._split[stage, key] = (p, _split_start(p, "%s_%s_start" % (stage, key)))
        return self._split[stage, key][1][3]

    def _end(self, stage, key, after):
        p, handle = self._split.pop((stage, key))
        srcs, lands = _split_wait(p, handle, after, "%s_%s_wait" % (stage, key))
        if stage == "swap":
            self.G.update(zip(REDUCE_GROUP[key], srcs))
        self._store([(stage, key)], [lands])

    def tail_begin(self):
        return self._begin("swap", LAST_GROUP)

    def tail_mid(self, after):
        self._end("swap", LAST_GROUP, after)
        token = self._begin("exchange", LAST_GROUP)
        self._end("small", None, [token])
        self._end("exchange", "mix", [token])
        self._standalone("reduce_tail_join_mix", [("join", "mix")])
        return token

    def tail_end(self, after):
        self._end("exchange", LAST_GROUP, after)
        self._standalone("reduce_tail_join", [("join", LAST_GROUP)])


def _small_grads(raw_sum, sp):
    _, vjp = jax.vjp(_zoh, sp["ssm_lam_re"], sp["ssm_lam_im"], sp["ssm_log_step"], sp["ssm_b_re"], sp["ssm_b_im"])
    d_lre, d_lim, d_ls, d_bre, d_bim = vjp((raw_sum["a_re"], raw_sum["a_im"], raw_sum["bb_re"], raw_sum["bb_im"]))
    g = {k: raw_sum[k] for k in ("ln1_g", "ln1_b", "ln2_g", "ln2_b", "ln3_g", "ln3_b", "ln4_g", "ln4_b",
                                 "conv_w", "conv_b", "ssm_c_re", "ssm_c_im", "ssm_d")}
    g.update(ssm_lam_re=d_lre, ssm_lam_im=d_lim, ssm_log_step=d_ls, ssm_b_re=d_bre, ssm_b_im=d_bim)
    return g


def kernel(x, p, ffn1_w_in, ffn1_w_out, ln1_g, ln1_b, mix_w_in, conv_w, conv_b, conv_w_out, ssm_lam_re, ssm_lam_im, ssm_log_step, ssm_b_re, ssm_b_im, ssm_c_re, ssm_c_im, ssm_d, ssm_w_glu, mix_w_out, ln2_g, ln2_b, ffn2_w_in, ffn2_w_out, ln3_g, ln3_b, ple_w_in, ple_w_gate, ln4_g, ln4_b, loss_target, m_ffn1_w_in, m_ffn1_w_out, m_ln1_g, m_ln1_b, m_mix_w_in, m_conv_w, m_conv_b, m_conv_w_out, m_ssm_lam_re, m_ssm_lam_im, m_ssm_log_step, m_ssm_b_re, m_ssm_b_im, m_ssm_c_re, m_ssm_c_im, m_ssm_d, m_ssm_w_glu, m_mix_w_out, m_ln2_g, m_ln2_b, m_ffn2_w_in, m_ffn2_w_out, m_ln3_g, m_ln3_b, m_ple_w_in, m_ple_w_gate, m_ln4_g, m_ln4_b, v_ffn1_w_in, v_ffn1_w_out, v_ln1_g, v_ln1_b, v_mix_w_in, v_conv_w, v_conv_b, v_conv_w_out, v_ssm_lam_re, v_ssm_lam_im, v_ssm_log_step, v_ssm_b_re, v_ssm_b_im, v_ssm_c_re, v_ssm_c_im, v_ssm_d, v_ssm_w_glu, v_mix_w_out, v_ln2_g, v_ln2_b, v_ffn2_w_in, v_ffn2_w_out, v_ln3_g, v_ln3_b, v_ple_w_in, v_ple_w_gate, v_ln4_g, v_ln4_b):
    args = dict(locals())
    w = {n: args[n] for n in WEIGHTS}
    m = {n: args["m_" + n] for n in WEIGHTS}
    v = {n: args["v_" + n] for n in WEIGHTS}
    _, _, c, me = _where()
    cidx = jnp.stack([c, me]).astype(jnp.int32)
    meidx = jnp.reshape(me, (1,)).astype(jnp.int32)

    sched = _Sched(cidx)
    token = sched.first_begin({n: _slot_cast(meidx, w[n][0], BF16, "cast_" + n) for n in GATHER_FIRST})
    rest = {n: _slot_cast(meidx, w[n][0], BF16, "cast_" + n, (token,)) for n in BIG if n not in GATHER_FIRST}
    rest["conv_w"] = _slot_cast(meidx, jnp.pad(conv_w[0], ((0, 13), (0, 0))), F32, "cast_conv_w", (token,))
    sp = {n: (w[n] if w[n].ndim == 2 and n != "ssm_log_step" else w[n][0]) for n in SMALL if n != "conv_w"}
    ops = _s5_operands({**sp, "ssm_lam_re": sp["ssm_lam_re"] + token[0, 0]})
    sched.first_end(rest, list(rest.values()) + list(ops))
    loss_part, dx0 = _local_step(x[0], p[0, 0], loss_target[0], sp, sched, 256, 256, ops)
    out_g, out_d, out_m, out_v = {}, {}, {}, {}

    def big_adamw(names, token):
        for n in names:
            g, dl, mn, vn = _adamw_pair(cidx, w[n][0], sched.half[n], sched.theirs[n], m[n][0], v[n][0], token,
                                        "adamw_" + n)
            out_g[n], out_d[n], out_m[n], out_v[n] = g[None], dl[None], mn[None], vn[None]

    first = REDUCE_GROUP["ple"] + REDUCE_GROUP["ffn2"]
    big_adamw(first, sched.tail_begin())
    token = sched.tail_mid([out_v[n] for n in first])

    raw_shapes = [sched.raw[k].shape for k in RAW_ORDER]
    raw_sum = dict(zip(RAW_ORDER, _unpack(_sum8(sched.small_buf, token), raw_shapes)))
    loss = raw_sum["loss"][0]
    sg = _small_grads(raw_sum, sp)
    sg["conv_w"] = lax.dynamic_slice_in_dim(sg["conv_w"], me * 128, 128, axis=1)
    small_shapes = [w[n].shape for n in SMALL]
    gp = _pack([sg[n] for n in SMALL])
    d_s, m_s, v_s = _adamw(_pack([w[n] for n in SMALL]), gp, _pack([m[n] for n in SMALL]),
                           _pack([v[n] for n in SMALL]), "adamw_small")

    for n, a, b_, c_, d_ in zip(SMALL, _unpack(gp, small_shapes), _unpack(d_s, small_shapes),
                                _unpack(m_s, small_shapes), _unpack(v_s, small_shapes)):
        out_g[n], out_d[n], out_m[n], out_v[n] = a, b_, c_, d_
    big_adamw(REDUCE_GROUP["mix"], token)
    sched.tail_end([d_s] + [out_v[n] for n in REDUCE_GROUP["mix"]])
    big_adamw(REDUCE_GROUP[LAST_GROUP], token)

    return (loss, dx0[None], *[out_g[n] for n in WEIGHTS], *[out_d[n] for n in WEIGHTS],
            *[out_m[n] for n in WEIGHTS], *[out_v[n] for n in WEIGHTS])
```

```python
import functools
import math

import jax
import jax.numpy as jnp
import numpy as np
from jax import lax
from jax.experimental import pallas as pl
from jax.experimental.pallas import tpu as pltpu

F32, BF16 = jnp.float32, jnp.bfloat16
D = 1024
FF = 2816
FFH = FF // 2
CONV = 512
SSM = 512
GROUPS = 32
STATE = 64
LANES = GROUPS * STATE
SCAN_W = 128
SCAN_PER = 512 // SCAN_W
SCAN_GR = SCAN_W // STATE
SCAN_R = 256
ALPHA = 2.0 ** 0.25
LN_EPS = 1e-5
GELU_C = math.sqrt(2.0 / math.pi)
B1, B2, LR, EPS, WD, STEP = 0.9, 0.999, 0.001, 1e-8, 0.01, 10
MESH = pl.DeviceIdType.MESH
ANY = pl.BlockSpec(memory_space=pl.ANY)
VMEM_FULL = pl.BlockSpec(memory_space=pltpu.VMEM)


def _cp(vmem_mb=48, n_axes=1):
    return pltpu.CompilerParams(vmem_limit_bytes=vmem_mb << 20,
                                dimension_semantics=("arbitrary",) * n_axes)


def _hbm(*arrs):
    return [pltpu.with_memory_space_constraint(a, pltpu.HBM) for a in arrs]


def _hbm_out(shapes):
    if isinstance(shapes, (list, tuple)):
        return [pltpu.HBM(s.shape, s.dtype) for s in shapes]
    return pltpu.HBM(shapes.shape, shapes.dtype)


def _nn(a, b):
    return jnp.dot(a, b, preferred_element_type=F32)


def _nt(a, b):
    return lax.dot_general(a, b, (((1,), (1,)), ((), ())), preferred_element_type=F32)


def _tn(a, b):
    return lax.dot_general(a, b, (((0,), (0,)), ((), ())), preferred_element_type=F32)


def _sig(v):
    return jax.nn.sigmoid(v)


def _ln_stats(r):
    mu = jnp.mean(r, axis=-1, keepdims=True)
    xc = r - mu
    var = jnp.mean(xc * xc, axis=-1, keepdims=True)
    rstd = lax.rsqrt(var + LN_EPS)
    return xc * rstd, rstd


def _ln_bwd(dy, r, g):
    xhat, rstd = _ln_stats(r)
    dyg = dy * g
    m1 = jnp.mean(dyg, axis=-1, keepdims=True)
    m2 = jnp.mean(dyg * xhat, axis=-1, keepdims=True)
    return rstd * (dyg - m1 - xhat * m2), xhat


def _rowsum(v):
    return jnp.sum(v, axis=0, keepdims=True)


class _Payload:
    def __init__(self, operands, outs, aliases, sems, start, finish):
        self.operands, self.outs, self.aliases, self.sems = list(operands), list(outs), dict(aliases), list(sems)
        self.start, self.finish = start, finish


def _split(flat, comm, attr):
    out, i = [], 0
    for p in comm:
        n = len(getattr(p, attr))
        out.append(list(flat[i:i + n]))
        i += n
    return out


def _run_comm(comm, which, cin, cout, csem):
    for p, a, b, s in zip(comm, _split(cin, comm, "operands"), _split(cout, comm, "outs"), _split(csem, comm, "sems")):
        getattr(p, which)(a, b, s)


def _pcall(body, *, name, grid, in_specs, out_specs, out_shape, operands, scratch=(), vmem_mb=48, aliases=None,
           comm=()):
    ni, no, ns = len(in_specs), len(out_specs), len(scratch)
    c_ops = [a for p in comm for a in p.operands]
    c_outs = [s for p in comm for s in p.outs]
    c_sems = [s for p in comm for s in p.sems]
    io = dict(aliases or {})
    off_i, off_o = ni, no
    for p in comm:
        for a, b in p.aliases.items():
            io[off_i + a] = off_o + b
        off_i += len(p.operands)
        off_o += len(p.outs)

    def wrapped(*refs):
        ins, cin = refs[:ni], refs[ni:ni + len(c_ops)]
        o0 = ni + len(c_ops)
        outs, cout = refs[o0:o0 + no], refs[o0 + no:o0 + no + len(c_outs)]
        s0 = o0 + no + len(c_outs)
        scr, csem = refs[s0:s0 + ns], refs[s0 + ns:]
        if comm:
            first = functools.reduce(jnp.logical_and, [pl.program_id(a) == 0 for a in range(len(grid))])
            pl.when(first)(lambda: _run_comm(comm, "start", cin, cout, csem))
        body(*ins, *outs, *scr)
        if comm:
            last = functools.reduce(jnp.logical_and, [pl.program_id(a) == grid[a] - 1 for a in range(len(grid))])
            pl.when(last)(lambda: _run_comm(comm, "finish", cin, cout, csem))

    res = pl.pallas_call(
        wrapped, name=name, grid=grid,
        in_specs=list(in_specs) + [ANY] * len(c_ops), out_specs=list(out_specs) + [ANY] * len(c_outs),
        out_shape=_hbm_out(list(out_shape) + c_outs), scratch_shapes=list(scratch) + c_sems,
        input_output_aliases=io,
        compiler_params=pltpu.CompilerParams(vmem_limit_bytes=vmem_mb << 20,
                                             dimension_semantics=("arbitrary",) * len(grid),
                                             has_side_effects=bool(comm)),
    )(*_hbm(*operands, *c_ops))
    return list(res[:no]), _split(res[no:], comm, "outs")


def _comm_call(name, comm):
    c_ops = [a for p in comm for a in p.operands]
    c_outs = [s for p in comm for s in p.outs]
    c_sems = [s for p in comm for s in p.sems]
    io, off_i, off_o = {}, 0, 0
    for p in comm:
        for a, b in p.aliases.items():
            io[off_i + a] = off_o + b
        off_i += len(p.operands)
        off_o += len(p.outs)

    def body(*refs):
        cin, cout = refs[:len(c_ops)], refs[len(c_ops):len(c_ops) + len(c_outs)]
        csem = refs[len(c_ops) + len(c_outs):]
        _run_comm(comm, "start", cin, cout, csem)
        _run_comm(comm, "finish", cin, cout, csem)

    res = pl.pallas_call(
        body, name=name, in_specs=[ANY] * len(c_ops), out_specs=[ANY] * len(c_outs), out_shape=_hbm_out(c_outs),
        scratch_shapes=c_sems, input_output_aliases=io,
        compiler_params=pltpu.CompilerParams(has_side_effects=True),
    )(*_hbm(*c_ops))
    return _split(res, comm, "outs")


def _ffn_fwd(x, xb, w_in4, w_out2, g, b, tm, name, comm=()):
    T = x.shape[0]

    def body(x_ref, xb_ref, win_ref, wo_ref, g_ref, b_ref, h_ref, r_ref, xo_ref, xob_ref):
        xv = xb_ref[...]
        acc = ALPHA * x_ref[...]
        for k in range(2):
            gt = _nn(xv, win_ref[k])
            up = _nn(xv, win_ref[k + 2])
            a = (gt * _sig(gt) * up).astype(BF16)
            h_ref[:, 2 * k * FFH:(2 * k + 1) * FFH] = gt.astype(BF16)
            h_ref[:, (2 * k + 1) * FFH:(2 * k + 2) * FFH] = up.astype(BF16)
            acc = acc + 0.5 * _nn(a, wo_ref[k])
        xhat, _ = _ln_stats(acc)
        xo = xhat * g_ref[...] + b_ref[...]
        r_ref[...] = acc
        xo_ref[...] = xo
        xob_ref[...] = xo.astype(BF16)

    tok = pl.BlockSpec((tm, D), lambda i: (i, 0))
    vec = pl.BlockSpec((1, D), lambda i: (0, 0))
    return _pcall(
        body, name=name, grid=(T // tm,),
        in_specs=[tok, tok,
                  pl.BlockSpec((4, D, FFH), lambda i: (0, 0, 0), pipeline_mode=pl.Buffered(1)),
                  pl.BlockSpec((2, FFH, D), lambda i: (0, 0, 0), pipeline_mode=pl.Buffered(1)),
                  vec, vec],
        out_specs=[pl.BlockSpec((tm, 2 * FF), lambda i: (i, 0)), tok, tok, tok],
        out_shape=[jax.ShapeDtypeStruct((T, 2 * FF), BF16), jax.ShapeDtypeStruct((T, D), F32),
                   jax.ShapeDtypeStruct((T, D), F32), jax.ShapeDtypeStruct((T, D), BF16)],
        vmem_mb=58, comm=comm, operands=(x, xb, w_in4, w_out2, g, b))


def _ffn_bwd(dy, r, g, h, w_in4, w_out2, tm, name, comm=()):
    T = dy.shape[0]

    def body(dy_ref, r_ref, g_ref, h_ref, win_ref, wo_ref, dx_ref, dh_ref, a_ref, df_ref, dg_ref, db_ref):
        i = pl.program_id(0)
        dyv = dy_ref[...]
        dr, xhat = _ln_bwd(dyv, r_ref[...], g_ref[...])
        dg_ref[...] = jnp.where(i == 0, 0.0, dg_ref[...]) + _rowsum(dyv * xhat)
        db_ref[...] = jnp.where(i == 0, 0.0, db_ref[...]) + _rowsum(dyv)
        dfb = (0.5 * dr).astype(BF16)
        df_ref[...] = dfb
        acc = ALPHA * dr
        for k in range(2):
            da = _nt(dfb, wo_ref[k])
            gt = h_ref[:, 2 * k * FFH:(2 * k + 1) * FFH].astype(F32)
            up = h_ref[:, (2 * k + 1) * FFH:(2 * k + 2) * FFH].astype(F32)
            sg = _sig(gt)
            silu = gt * sg
            dgate = (da * up * (sg * (1.0 + gt * (1.0 - sg)))).astype(BF16)
            dup = (da * silu).astype(BF16)
            a_ref[:, k * FFH:(k + 1) * FFH] = (silu * up).astype(BF16)
            dh_ref[:, 2 * k * FFH:(2 * k + 1) * FFH] = dgate
            dh_ref[:, (2 * k + 1) * FFH:(2 * k + 2) * FFH] = dup
            acc = acc + _nt(dgate, win_ref[k]) + _nt(dup, win_ref[k + 2])
        dx_ref[...] = acc

    tok = pl.BlockSpec((tm, D), lambda i: (i, 0))
    vec = pl.BlockSpec((1, D), lambda i: (0, 0))
    wide = pl.BlockSpec((tm, 2 * FF), lambda i: (i, 0))
    return _pcall(
        body, name=name, grid=(T // tm,),
        in_specs=[tok, tok, vec, wide,
                  pl.BlockSpec((4, D, FFH), lambda i: (0, 0, 0), pipeline_mode=pl.Buffered(1)),
                  pl.BlockSpec((2, FFH, D), lambda i: (0, 0, 0), pipeline_mode=pl.Buffered(1))],
        out_specs=[tok, wide, pl.BlockSpec((tm, FF), lambda i: (i, 0)), tok, vec, vec],
        out_shape=[jax.ShapeDtypeStruct((T, D), F32), jax.ShapeDtypeStruct((T, 2 * FF), BF16),
                   jax.ShapeDtypeStruct((T, FF), BF16), jax.ShapeDtypeStruct((T, D), BF16),
                   jax.ShapeDtypeStruct((1, D), F32), jax.ShapeDtypeStruct((1, D), F32)],
        vmem_mb=58, comm=comm, operands=(dy, r, g, h, w_in4, w_out2))


def _mm_tn(a, b, tk, tn, name, shard_cols=None, interleaved=False, comm=()):
    T, K = a.shape
    N = b.shape[1]

    def body(a_ref, b_ref, o_ref):
        o_ref[...] = _tn(a_ref[...], b_ref[...])

    if shard_cols is None:
        out_shape = jax.ShapeDtypeStruct((K, N), F32)
        out_spec = pl.BlockSpec((tk, tn), lambda ki, nj: (ki, nj))
    else:
        per = shard_cols // tn

        def shard(nj):
            blk = nj // per
            return (blk % 2) * 2 + blk // 2 if interleaved else blk

        out_shape = jax.ShapeDtypeStruct((N // shard_cols, K, shard_cols), F32)
        out_spec = pl.BlockSpec((None, tk, tn), lambda ki, nj: (shard(nj), ki, nj % per))
    (out,), got = _pcall(
        body, name=name, grid=(K // tk, N // tn),
        in_specs=[pl.BlockSpec((T, tk), lambda ki, nj: (0, ki)), pl.BlockSpec((T, tn), lambda ki, nj: (0, nj))],
        out_specs=[out_spec], out_shape=[out_shape], comm=comm, operands=(a, b))
    return out, got


def _mix_fwd_a(xb, w_mix4, conv_w, conv_b, w_co4, tm, comm=()):
    T = xb.shape[0]

    def body(xb_ref, w_ref, cw_ref, cb_ref, wco_ref,
             pc_ref, z_ref, yin_ref, su_ref, sub_ref, gc_ref, gs_ref, yc_ref, qbuf):
        @pl.when(pl.program_id(0) == 0)
        def _():
            qbuf[pl.ds(0, 8), :] = jnp.zeros((8, CONV), F32)

        xv = xb_ref[...]
        p0 = _nn(xv, w_ref[0])
        p1 = _nn(xv, w_ref[1])
        gc_ref[...] = _nn(xv, w_ref[2])
        gs_ref[...] = _nn(xv, w_ref[3])
        cbv, ccv = p0[:, :CONV], p0[:, CONV:]
        chv, suv = p1[:, :CONV], p1[:, CONV:]
        q = ccv * chv
        qbuf[pl.ds(8, tm), :] = q
        cw = cw_ref[...]
        z = (cw[2:3] * q + cw[1:2] * qbuf[pl.ds(7, tm), :] + cw[0:1] * qbuf[pl.ds(6, tm), :]
             + cb_ref[...])
        qbuf[pl.ds(0, 8), :] = q[tm - 8:tm]
        yin = (cbv * z).astype(BF16)
        pc_ref[:, 0:CONV] = cbv.astype(BF16)
        pc_ref[:, CONV:2 * CONV] = ccv.astype(BF16)
        pc_ref[:, 2 * CONV:3 * CONV] = chv.astype(BF16)
        z_ref[...] = z.astype(BF16)
        yin_ref[...] = yin
        su_ref[...] = suv
        sub_ref[...] = suv.astype(BF16)
        for k in range(4):
            yc_ref[:, 256 * k:256 * (k + 1)] = _nn(yin, wco_ref[k])

    def tok(n):
        return pl.BlockSpec((tm, n), lambda i: (i, 0))

    def full(shape):
        return pl.BlockSpec(shape, lambda i: (0,) * len(shape))

    return _pcall(
        body, name="mix_fwd_a", grid=(T // tm,),
        in_specs=[tok(D), full((4, D, D)), full((3, CONV)), full((1, CONV)), full((4, CONV, 256))],
        out_specs=[tok(3 * CONV), tok(CONV), tok(CONV), tok(SSM), tok(SSM), tok(D), tok(D), tok(D)],
        out_shape=[jax.ShapeDtypeStruct((T, 3 * CONV), BF16), jax.ShapeDtypeStruct((T, CONV), BF16),
                   jax.ShapeDtypeStruct((T, CONV), BF16), jax.ShapeDtypeStruct((T, SSM), F32),
                   jax.ShapeDtypeStruct((T, SSM), BF16), jax.ShapeDtypeStruct((T, D), F32),
                   jax.ShapeDtypeStruct((T, D), F32), jax.ShapeDtypeStruct((T, D), F32)],
        scratch=[pltpu.VMEM((tm + 8, CONV), F32)], vmem_mb=56, comm=comm,
        operands=(xb, w_mix4, conv_w, conv_b, w_co4))


def _scan_rows(bre, bim, ar, ai, T, rev, load, out=None):
    R, W, G = SCAN_R, bre.shape[1], T // 8
    if rev:
        ai = -ai

    def cmul(pr, pi, xr, xi):
        return pr * xr - pi * xi, pr * xi + pi * xr

    pw = [(ar, ai)]
    for _ in range(7):
        pw.append(cmul(ar, ai, *pw[-1]))

    def shifted(v, d, axis, n, idx):
        if rev:
            return jnp.where(idx < n - d, pltpu.roll(v, n - d, axis), 0.0)
        return jnp.where(idx >= d, pltpu.roll(v, d, axis), 0.0)

    sub8 = lax.broadcasted_iota(jnp.int32, (8, W), 0)
    inside = {d: (sub8 < 8 - d) if rev else (sub8 >= d) for d in (1, 2, 4)}
    pm = {d: (jnp.where(inside[d], pw[d - 1][0], 0.0)[None], jnp.where(inside[d], pw[d - 1][1], 0.0)[None])
          for d in (1, 2, 4)}

    def step(i, _):
        t0 = pl.multiple_of(i * R, R)
        vr, vi = load(t0)
        vr, vi = vr.reshape(R // 8, 8, W), vi.reshape(R // 8, 8, W)
        for d in (1, 2, 4):
            sh = (8 - d) if rev else d
            dr, di = cmul(pm[d][0], pm[d][1], pltpu.roll(vr, sh, 1), pltpu.roll(vi, sh, 1))
            vr, vi = vr + dr, vi + di
        bre[pl.ds(t0 + 8, R), :] = vr.reshape(R, W)
        bim[pl.ds(t0 + 8, R), :] = vi.reshape(R, W)
        return 0

    lax.fori_loop(0, T // R, step, 0)

    edge = 0 if rev else 7
    cr = bre[pl.ds(8 + edge, G, stride=8), :]
    ci = bim[pl.ds(8 + edge, G, stride=8), :]
    row = lax.broadcasted_iota(jnp.int32, (G, W), 0)
    qr, qi = pw[7]
    d = 1
    while d < G:
        dr, di = cmul(qr, qi, shifted(cr, d, 0, G, row), shifted(ci, d, 0, G, row))
        cr, ci = cr + dr, ci + di
        qr, qi = qr * qr - qi * qi, 2.0 * qr * qi
        d *= 2

    nr, ni = shifted(cr, 1, 0, G, row), shifted(ci, 1, 0, G, row)
    for r in range(8):
        pr, pi = pw[7 - r] if rev else pw[r]
        dr, di = cmul(pr, pi, nr, ni)
        xr = bre[pl.ds(8 + r, G, stride=8), :] + dr
        xi = bim[pl.ds(8 + r, G, stride=8), :] + di
        if out is None:
            bre[pl.ds(8 + r, G, stride=8), :] = xr
            bim[pl.ds(8 + r, G, stride=8), :] = xi
        else:
            out[0][pl.ds(r, G, stride=8), :] = xr
            out[1][pl.ds(r, G, stride=8), :] = xi


def _scan_specs(T):
    W = SCAN_W
    lane = pl.BlockSpec((T, W), lambda j: (0, j))
    col = pl.BlockSpec((T, 128), lambda j: (0, j // SCAN_PER))
    wb = pl.BlockSpec((None, 128, W), lambda j: (j, 0, 0))
    wc = pl.BlockSpec((None, W, 128), lambda j: (j, 0, 0))
    vec = pl.BlockSpec((1, W), lambda j: (0, j))
    return lane, col, wb, wc, vec


def _s5_scan_fwd(su_b, wb_re, wb_im, a_re, a_im, comm=()):
    T = su_b.shape[0]
    W = SCAN_W

    def body(su_ref, wbr_ref, wbi_ref, ar_ref, ai_ref, sr_ref, si_ref, bre, bim):
        def load(t0):
            su = su_ref[pl.ds(t0, SCAN_R), :]
            return _nn(su, wbr_ref[...]), _nn(su, wbi_ref[...])

        _scan_rows(bre, bim, ar_ref[...], ai_ref[...], T, False, load, out=(sr_ref, si_ref))

    lane, col, wb, wc, vec = _scan_specs(T)
    return _pcall(
        body, name="s5_scan_fwd", grid=(LANES // W,),
        in_specs=[col, wb, wb, vec, vec],
        out_specs=[lane, lane],
        out_shape=[jax.ShapeDtypeStruct((T, LANES), F32)] * 2,
        scratch=[pltpu.VMEM((T + 16, W), F32)] * 2, comm=comm,
        operands=(su_b, wb_re, wb_im, a_re, a_im))


def _gelu(s):
    th = jnp.tanh(GELU_C * (s + 0.044715 * s * s * s))
    return 0.5 * s * (1.0 + th), th


def _mix_fwd_b(st_re, st_im, wc_re4, wc_im4, su, dvec, w_glu4, g_conv, g_ssm, y_conv, w_mo, x1, g, b, tm, comm=()):
    T = su.shape[0]

    def body(sr_ref, si_ref, wcr_ref, wci_ref, su_ref, d_ref, wg_ref, gc_ref, gs_ref, yc_ref, wmo_ref,
             x_ref, g_ref, b_ref, s_ref, sgb_ref, ga_ref, gb_ref, mb_ref, r_ref, xo_ref, xob_ref):
        srb = sr_ref[...].astype(BF16)
        sib = si_ref[...].astype(BF16)
        ys = [_nn(srb[:, 512 * J:512 * (J + 1)], wcr_ref[J]) + _nn(sib[:, 512 * J:512 * (J + 1)], wci_ref[J])
              for J in range(4)]
        s = jnp.concatenate(ys, axis=1) + d_ref[...] * su_ref[...]
        sg, _ = _gelu(s)
        sgb = sg.astype(BF16)
        ga = jnp.concatenate([_nn(sgb, wg_ref[0]), _nn(sgb, wg_ref[1])], axis=1)
        gb = jnp.concatenate([_nn(sgb, wg_ref[2]), _nn(sgb, wg_ref[3])], axis=1)
        merged = _sig(gc_ref[...]) * yc_ref[...] + _sig(gs_ref[...]) * (ga * _sig(gb))
        mb = merged.astype(BF16)
        r = ALPHA * x_ref[...] + _nn(mb, wmo_ref[...])
        xhat, _ = _ln_stats(r)
        xo = xhat * g_ref[...] + b_ref[...]
        s_ref[...] = s
        sgb_ref[...] = sgb
        ga_ref[...] = ga
        gb_ref[...] = gb
        mb_ref[...] = mb
        r_ref[...] = r
        xo_ref[...] = xo
        xob_ref[...] = xo.astype(BF16)

    def tok(n):
        return pl.BlockSpec((tm, n), lambda i: (i, 0))

    def full(shape):
        return pl.BlockSpec(shape, lambda i: (0,) * len(shape))

    return _pcall(
        body, name="mix_fwd_b", grid=(T // tm,),
        in_specs=[tok(LANES), tok(LANES), full((4, 512, 128)), full((4, 512, 128)), tok(SSM), full((1, SSM)),
                  full((4, SSM, 512)), tok(D), tok(D), tok(D), full((D, D)), tok(D), full((1, D)), full((1, D))],
        out_specs=[tok(SSM), tok(SSM), tok(D), tok(D), tok(D), tok(D), tok(D), tok(D)],
        out_shape=[jax.ShapeDtypeStruct((T, SSM), F32), jax.ShapeDtypeStruct((T, SSM), BF16),
                   jax.ShapeDtypeStruct((T, D), F32), jax.ShapeDtypeStruct((T, D), F32),
                   jax.ShapeDtypeStruct((T, D), BF16), jax.ShapeDtypeStruct((T, D), F32),
                   jax.ShapeDtypeStruct((T, D), F32), jax.ShapeDtypeStruct((T, D), BF16)],
        vmem_mb=56, comm=comm,
        operands=(st_re, st_im, wc_re4, wc_im4, su, dvec, w_glu4, g_conv, g_ssm, y_conv, w_mo, x1, g, b))


def _ple_loss(x3, x3b, p, w_pi4, w_pg, g, b, target, tm):
    T = x3.shape[0]
    PD = p.shape[1]

    def body(x_ref, xb_ref, p_ref, wpi_ref, wpg_ref, g_ref, b_ref, t_ref,
             loss_ref, dx_ref, pb_ref, dpw_ref, dgt_ref, dg_ref, db_ref):
        i = pl.program_id(0)
        pb = p_ref[...].astype(BF16)
        pw = jnp.concatenate([_nn(pb, wpi_ref[k]) for k in range(4)], axis=1)
        gt = _nn(xb_ref[...], wpg_ref[...])
        sg = _sig(gt)
        r = ALPHA * x_ref[...] + pw * sg
        gv = g_ref[...]
        xhat, rstd = _ln_stats(r)
        err = xhat * gv + b_ref[...] - t_ref[...]
        lpart = jnp.zeros((1, 128), F32) + 0.5 * jnp.sum(jnp.mean(err * err, axis=-1, keepdims=True))
        dy = err * (1.0 / D)
        dyg = dy * gv
        m1 = jnp.mean(dyg, axis=-1, keepdims=True)
        m2 = jnp.mean(dyg * xhat, axis=-1, keepdims=True)
        dr = rstd * (dyg - m1 - xhat * m2)
        pg, pbias = _rowsum(dy * xhat), _rowsum(dy)

        @pl.when(i == 0)
        def _():
            loss_ref[...] = lpart
            dg_ref[...] = pg
            db_ref[...] = pbias

        @pl.when(i > 0)
        def _():
            loss_ref[...] += lpart
            dg_ref[...] += pg
            db_ref[...] += pbias

        dgt = (dr * pw * sg * (1.0 - sg)).astype(BF16)
        pb_ref[...] = pb
        dpw_ref[...] = (dr * sg).astype(BF16)
        dgt_ref[...] = dgt
        dx_ref[...] = ALPHA * dr + _nt(dgt, wpg_ref[...])

    def tok(n):
        return pl.BlockSpec((tm, n), lambda i: (i, 0))

    def full(shape):
        return pl.BlockSpec(shape, lambda i: (0,) * len(shape))

    return pl.pallas_call(
        body, name="ple_loss", grid=(T // tm,),
        in_specs=[tok(D), tok(D), tok(PD), full((4, PD, 256)), full((D, D)), full((1, D)), full((1, D)), tok(D)],
        out_specs=[full((1, 128)), tok(D), tok(PD), tok(D), tok(D), full((1, D)), full((1, D))],
        out_shape=_hbm_out([jax.ShapeDtypeStruct((1, 128), F32), jax.ShapeDtypeStruct((T, D), F32),
                            jax.ShapeDtypeStruct((T, PD), BF16), jax.ShapeDtypeStruct((T, D), BF16),
                            jax.ShapeDtypeStruct((T, D), BF16), jax.ShapeDtypeStruct((1, D), F32),
                            jax.ShapeDtypeStruct((1, D), F32)]),
        compiler_params=_cp(48, 1),
    )(*_hbm(x3, x3b, p, w_pi4, w_pg, g, b, target))


def _mix_bwd_b(dy, r2, g, w_mo, g_conv, g_ssm, y_conv, ga, gb, s, su, dvec, w_glu4, wc_re4, wc_im4, tm, comm=()):
    T = dy.shape[0]

    def body(dy_ref, r_ref, g_ref, wmo_ref, gc_ref, gs_ref, yc_ref, ga_ref, gb_ref, s_ref, su_ref, d_ref,
             wg_ref, wcr_ref, wci_ref,
             dres_ref, dmix_ref, dgl_ref, dsb_ref, dud_ref, gsr_ref, gsi_ref, dyc_ref, dp_ref,
             dg_ref, db_ref, dd_ref):
        i = pl.program_id(0)
        dyv = dy_ref[...]
        dr, xhat = _ln_bwd(dyv, r_ref[...], g_ref[...])
        dmix = dr.astype(BF16)
        dmerged = _nt(dmix, wmo_ref[...])
        sc, ss, sgb = _sig(gc_ref[...]), _sig(gs_ref[...]), _sig(gb_ref[...])
        gav = ga_ref[...]
        yssm = gav * sgb
        dgc = dmerged * yc_ref[...] * sc * (1.0 - sc)
        dgss = dmerged * yssm * ss * (1.0 - ss)
        dyssm = dmerged * ss
        dgl = jnp.concatenate([dyssm * sgb, dyssm * gav * sgb * (1.0 - sgb)], axis=1).astype(BF16)
        dsg = (_nt(dgl[:, 0:512], wg_ref[0]) + _nt(dgl[:, 512:1024], wg_ref[1])
               + _nt(dgl[:, 1024:1536], wg_ref[2]) + _nt(dgl[:, 1536:2048], wg_ref[3]))
        sv = s_ref[...]
        _, th = _gelu(sv)
        dgelu = 0.5 * (1.0 + th) + 0.5 * sv * (1.0 - th * th) * GELU_C * (1.0 + 3.0 * 0.044715 * sv * sv)
        ds = dsg * dgelu
        dsb = ds.astype(BF16)
        pg, pb, pd = _rowsum(dyv * xhat), _rowsum(dyv), _rowsum(ds * su_ref[...])

        @pl.when(i == 0)
        def _():
            dg_ref[...] = pg
            db_ref[...] = pb
            dd_ref[...] = pd

        @pl.when(i > 0)
        def _():
            dg_ref[...] += pg
            db_ref[...] += pb
            dd_ref[...] += pd

        dres_ref[...] = ALPHA * dr
        dmix_ref[...] = dmix
        dgl_ref[...] = dgl
        dsb_ref[...] = dsb
        dud_ref[...] = ds * d_ref[...]
        for J in range(4):
            gsr_ref[:, 512 * J:512 * (J + 1)] = _nt(dsb[:, 128 * J:128 * (J + 1)], wcr_ref[J])
            gsi_ref[:, 512 * J:512 * (J + 1)] = _nt(dsb[:, 128 * J:128 * (J + 1)], wci_ref[J])
        dyc_ref[...] = (dmerged * sc).astype(BF16)
        dp_ref[:, 0:D] = dgc.astype(BF16)
        dp_ref[:, D:2 * D] = dgss.astype(BF16)

    def tok(n):
        return pl.BlockSpec((tm, n), lambda i: (i, 0))

    def full(shape):
        return pl.BlockSpec(shape, lambda i: (0,) * len(shape))

    return _pcall(
        body, name="mix_bwd_b", grid=(T // tm,),
        in_specs=[tok(D), tok(D), full((1, D)), full((D, D)), tok(D), tok(D), tok(D), tok(D), tok(D),
                  tok(SSM), tok(SSM), full((1, SSM)), full((4, SSM, 512)), full((4, 512, 128)), full((4, 512, 128))],
        out_specs=[tok(D), tok(D), tok(2 * D), tok(SSM), tok(SSM), tok(LANES), tok(LANES), tok(D),
                   pl.BlockSpec((tm, 2 * D), lambda i: (i, 1)), full((1, D)), full((1, D)), full((1, SSM))],
        out_shape=[jax.ShapeDtypeStruct((T, D), F32), jax.ShapeDtypeStruct((T, D), BF16),
                   jax.ShapeDtypeStruct((T, 2 * D), BF16), jax.ShapeDtypeStruct((T, SSM), BF16),
                   jax.ShapeDtypeStruct((T, SSM), F32), jax.ShapeDtypeStruct((T, LANES), F32),
                   jax.ShapeDtypeStruct((T, LANES), F32), jax.ShapeDtypeStruct((T, D), BF16),
                   jax.ShapeDtypeStruct((T, 4 * D), BF16), jax.ShapeDtypeStruct((1, D), F32),
                   jax.ShapeDtypeStruct((1, D), F32), jax.ShapeDtypeStruct((1, SSM), F32)],
        vmem_mb=56, comm=comm,
        operands=(dy, r2, g, w_mo, g_conv, g_ssm, y_conv, ga, gb, s, su, dvec, w_glu4, wc_re4, wc_im4))


def _s5_scan_bwd(gs_re, gs_im, st_re, st_im, su_b, ds_b, wb_re, wb_im, a_re, a_im, comm=()):
    T = su_b.shape[0]
    W = SCAN_W
    R = SCAN_R

    def body(gr_ref, gi_ref, sr_ref, si_ref, su_ref, ds_ref, wbr_ref, wbi_ref, ar_ref, ai_ref,
             dsu_ref, dwbr_ref, dwbi_ref, dwcr_ref, dwci_ref, dar_ref, dai_ref, gre, gim):
        j = pl.program_id(0)
        zero = jnp.zeros((8, W), F32)
        for buf in (gre, gim):
            buf[pl.ds(T + 8, 8), :] = zero
        _scan_rows(gre, gim, ar_ref[...], ai_ref[...], T, True,
                   lambda t0: (gr_ref[pl.ds(t0, R), :], gi_ref[pl.ds(t0, R), :]))
        grb = gre[pl.ds(8, T), :].astype(BF16)
        gib = gim[pl.ds(8, T), :].astype(BF16)
        part = _nt(grb, wbr_ref[...]) + _nt(gib, wbi_ref[...])

        @pl.when(j % SCAN_PER == 0)
        def _():
            dsu_ref[...] = part

        @pl.when(j % SCAN_PER > 0)
        def _():
            dsu_ref[...] += part

        su = su_ref[...]
        dwbr_ref[...] = _tn(su, grb)
        dwbi_ref[...] = _tn(su, gib)
        dsv = ds_ref[...]
        dwcr_ref[...] = _tn(sr_ref[...].astype(BF16), dsv)
        dwci_ref[...] = _tn(si_ref[...].astype(BF16), dsv)
        dar = jnp.zeros((1, W), F32)
        dai = jnp.zeros((1, W), F32)
        for c in range(T // R):
            xr = sr_ref[pl.ds(c * R, R), :]
            xi = si_ref[pl.ds(c * R, R), :]
            g1r = gre[pl.ds(c * R + 9, R), :]
            g1i = gim[pl.ds(c * R + 9, R), :]
            dar = dar + _rowsum(g1r * xr + g1i * xi)
            dai = dai + _rowsum(g1i * xr - g1r * xi)
        dar_ref[...] = dar
        dai_ref[...] = dai

    lane, col, wb, wc, vec = _scan_specs(T)
    return _pcall(
        body, name="s5_scan_bwd", grid=(LANES // W,),
        in_specs=[lane, lane, lane, lane, col, col, wb, wb, vec, vec],
        out_specs=[col, wb, wb, wc, wc, vec, vec],
        out_shape=[jax.ShapeDtypeStruct((T, SSM), F32),
                   jax.ShapeDtypeStruct((LANES // W, 128, W), F32), jax.ShapeDtypeStruct((LANES // W, 128, W), F32),
                   jax.ShapeDtypeStruct((LANES // W, W, 128), F32), jax.ShapeDtypeStruct((LANES // W, W, 128), F32),
                   jax.ShapeDtypeStruct((1, LANES), F32), jax.ShapeDtypeStruct((1, LANES), F32)],
        scratch=[pltpu.VMEM((T + 16, W), F32)] * 2, vmem_mb=56, comm=comm,
        operands=(gs_re, gs_im, st_re, st_im, su_b, ds_b, wb_re, wb_im, a_re, a_im))


def _mix_bwd_a(dyc_b, w_co4, pc, z_b, conv_w, dsu_ssm, du_dir, dproj, dres, w_mix4, tm, comm=()):
    T = dres.shape[0]
    nt = T // tm

    def body(dyc_ref, wco_ref, pc_ref, halo_ref, z_ref, cw_ref, dsu_ref, dud_ref, dpin_ref, dres_ref, w_ref,
             dp_ref, dx_ref, dcw_ref, dcb_ref, dzbuf, qbuf):
        i = pl.program_id(0)
        ii = nt - 1 - i

        @pl.when(i == 0)
        def _():
            dzbuf[pl.ds(tm, 8), :] = jnp.zeros((8, CONV), F32)

        dyc = dyc_ref[...]
        dyin = (_nt(dyc[:, 0:256], wco_ref[0]) + _nt(dyc[:, 256:512], wco_ref[1])
                + _nt(dyc[:, 512:768], wco_ref[2]) + _nt(dyc[:, 768:1024], wco_ref[3]))
        cbv = pc_ref[:, 0:CONV].astype(F32)
        ccv = pc_ref[:, CONV:2 * CONV].astype(F32)
        chv = pc_ref[:, 2 * CONV:3 * CONV].astype(F32)
        dcbv = dyin * z_ref[...].astype(F32)
        dz = dyin * cbv
        dzbuf[pl.ds(0, tm), :] = dz
        cw = cw_ref[...]
        dq = cw[2:3] * dz + cw[1:2] * dzbuf[pl.ds(1, tm), :] + cw[0:1] * dzbuf[pl.ds(2, tm), :]
        dzbuf[pl.ds(tm, 8), :] = dz[0:8]
        q = ccv * chv
        hq = halo_ref[:, CONV:2 * CONV].astype(F32) * halo_ref[:, 2 * CONV:3 * CONV].astype(F32)
        qbuf[pl.ds(0, 8), :] = jnp.where(ii > 0, hq, jnp.zeros_like(hq))
        qbuf[pl.ds(8, tm), :] = q
        pw = jnp.concatenate([_rowsum(dz * qbuf[pl.ds(6, tm), :]), _rowsum(dz * qbuf[pl.ds(7, tm), :]),
                              _rowsum(dz * q), jnp.zeros((5, CONV), F32)], axis=0)
        pbias = _rowsum(dz)

        @pl.when(i == 0)
        def _():
            dcw_ref[...] = pw
            dcb_ref[...] = pbias

        @pl.when(i > 0)
        def _():
            dcw_ref[...] += pw
            dcb_ref[...] += pbias

        dp0 = jnp.concatenate([dcbv, dq * chv], axis=1).astype(BF16)
        dp1 = jnp.concatenate([dq * ccv, dsu_ref[...] + dud_ref[...]], axis=1).astype(BF16)
        dp_ref[:, 0:D] = dp0
        dp_ref[:, D:2 * D] = dp1
        dx_ref[...] = (dres_ref[...] + _nt(dp0, w_ref[0]) + _nt(dp1, w_ref[1])
                       + _nt(dpin_ref[:, 0:D], w_ref[2]) + _nt(dpin_ref[:, D:2 * D], w_ref[3]))

    def tok(n):
        return pl.BlockSpec((tm, n), lambda i: (nt - 1 - i, 0))

    def full(shape):
        return pl.BlockSpec(shape, lambda i: (0,) * len(shape))

    halo = pl.BlockSpec((8, 3 * CONV), lambda i: (jnp.maximum((nt - 1 - i) * (tm // 8) - 1, 0), 0))
    return _pcall(
        body, name="mix_bwd_a", grid=(nt,),
        in_specs=[tok(D), full((4, CONV, 256)), tok(3 * CONV), halo, tok(CONV), full((3, CONV)),
                  tok(SSM), tok(SSM), pl.BlockSpec((tm, 2 * D), lambda i: (nt - 1 - i, 1)), tok(D),
                  full((4, D, D))],
        out_specs=[pl.BlockSpec((tm, 2 * D), lambda i: (nt - 1 - i, 0)), tok(D), full((8, CONV)), full((1, CONV))],
        out_shape=[jax.ShapeDtypeStruct((T, 4 * D), BF16), jax.ShapeDtypeStruct((T, D), F32),
                   jax.ShapeDtypeStruct((8, CONV), F32), jax.ShapeDtypeStruct((1, CONV), F32)],
        scratch=[pltpu.VMEM((tm + 8, CONV), F32), pltpu.VMEM((tm + 8, CONV), F32)],
        aliases={8: 0}, vmem_mb=56, comm=comm,
        operands=(dyc_b, w_co4, pc, pc, z_b, conv_w, dsu_ssm, du_dir, dproj, dres, w_mix4))


def _zoh(lam_re, lam_im, log_step, b_re, b_im):
    dt = jnp.exp(log_step)[:, None]
    mag = jnp.exp(lam_re * dt)
    abr, abi = mag * jnp.cos(lam_im * dt), mag * jnp.sin(lam_im * dt)
    nr, ni = abr - 1.0, abi
    den = lam_re * lam_re + lam_im * lam_im
    cr = (nr * lam_re + ni * lam_im) / den
    ci = (ni * lam_re - nr * lam_im) / den
    bbr = cr[..., None] * b_re - ci[..., None] * b_im
    bbi = cr[..., None] * b_im + ci[..., None] * b_re
    return abr, abi, bbr, bbi


_WB_MASK = (np.arange(8)[None, :, None]
            == SCAN_GR * np.arange(SCAN_PER)[:, None, None] + np.arange(SCAN_GR)[None, None, :]).astype(np.float32)
_EYE8 = np.eye(8, dtype=np.float32)


def _wb_blocks(bb):
    bt = bb.transpose(0, 2, 1).reshape(4, 1, 8, 16, 1, STATE)
    full = bt * _WB_MASK[None, :, :, None, :, None]
    return full.reshape(LANES // SCAN_W, 128, SCAN_W).astype(BF16)


def _wc_blocks(cc):
    ct = cc.transpose(0, 2, 1).reshape(4, 8, STATE, 1, 16)
    full = ct * _EYE8[None, :, None, :, None]
    return full.reshape(4, 512, 128).astype(BF16)


def _wb_diag(dwb):
    d6 = dwb.reshape(4, SCAN_PER, 8, 16, SCAN_GR, STATE) * _WB_MASK[None, :, :, None, :, None]
    return d6.sum(axis=(1, 4)).reshape(GROUPS, 16, STATE).transpose(0, 2, 1)


def _wc_diag(dwc):
    mask = _WB_MASK.transpose(0, 2, 1)
    d6 = dwc.reshape(4, SCAN_PER, SCAN_GR, STATE, 8, 16) * mask[None, :, :, None, :, None]
    return d6.sum(axis=4).reshape(GROUPS, STATE, 16).transpose(0, 2, 1)


def _where():
    x, y, c = lax.axis_index("x"), lax.axis_index("y"), lax.axis_index("c")
    return x, y, c, 2 * x + y


def _chip_dev(k, c):
    return (k // 2, k % 2, c)


def _slot_cast(meidx, w, dtype, name, token=()):
    R, C = w.shape
    tr = _row_tile(R)

    def body(m_ref, w_ref, *rest):
        rest[-1][...] = w_ref[...].astype(dtype)

    gs = pltpu.PrefetchScalarGridSpec(
        num_scalar_prefetch=1, grid=(R // tr,),
        in_specs=[pl.BlockSpec((tr, C), lambda i, m: (i, 0))] + [pl.BlockSpec((8, 128), lambda i, m: (0, 0))] * len(token),
        out_specs=pl.BlockSpec((None, tr, C), lambda i, m: (m[0], i, 0)))
    return pl.pallas_call(
        body, name=name, grid_spec=gs, out_shape=_hbm_out(jax.ShapeDtypeStruct((4, R, C), dtype)),
        compiler_params=_cp(32, 1),
    )(meidx, *_hbm(w), *token)


def _gather_ici_payload(bufs):
    def copies(ins, lnd, ss, rs):
        x, y, c, me = _where()
        cps = []
        for w, b in enumerate(bufs):
            h = b.shape[1] // 2
            mine = lnd[w].at[me, pl.ds(c * h, h)]
            for s in range(3):
                k = (me + 1 + s) % 4
                cps.append(pltpu.make_async_remote_copy(
                    src_ref=mine, dst_ref=mine, send_sem=ss.at[3 * w + s], recv_sem=rs.at[3 * w + s],
                    device_id=_chip_dev(k, c), device_id_type=MESH))
        return cps

    p = _sym_payload([], [jax.ShapeDtypeStruct(b.shape, b.dtype) for b in bufs], copies, 3 * len(bufs))
    p.lands = list(bufs)
    return p


def _gather_pass_payload(bufs):
    def copies(ins, outs, ss, rs):
        x, y, c, me = _where()
        cps = []
        for w, b in enumerate(bufs):
            h = b.shape[1] // 2
            for s in range(3):
                j = (me + 1 + s) % 4
                cps.append(pltpu.make_async_remote_copy(
                    src_ref=ins[w].at[j, pl.ds(c * h, h)], dst_ref=outs[w].at[j, pl.ds(c * h, h)],
                    send_sem=ss.at[3 * w + s], recv_sem=rs.at[3 * w + s], device_id=(x, y, 1 - c),
                    device_id_type=MESH))
        return cps

    p = _sym_payload(bufs, [jax.ShapeDtypeStruct(b.shape, b.dtype) for b in bufs], copies, 3 * len(bufs))
    p.aliases = {w: w for w in range(len(bufs))}
    return p


def _gather_payload(bufs):
    n = len(bufs)

    def half(ref, w, k, cc):
        h = bufs[w].shape[1] // 2
        return ref.at[k, pl.ds(cc * h, h)]

    def ici(ins, outs, sems, w, s):
        x, y, c, me = _where()
        k = (me + 1 + s) % 4
        return pltpu.make_async_remote_copy(
            src_ref=half(ins[w], w, me, c), dst_ref=half(outs[w], w, me, c), send_sem=sems[0].at[3 * w + s],
            recv_sem=sems[1].at[3 * w + s], device_id=_chip_dev(k, c), device_id_type=MESH)

    def landed(outs, sems, w, s):
        x, y, c, me = _where()
        j = (me + 3 - s) % 4
        return pltpu.make_async_remote_copy(
            src_ref=half(outs[w], w, j, c), dst_ref=half(outs[w], w, j, c), send_sem=sems[0].at[3 * w + s],
            recv_sem=sems[1].at[3 * w + s], device_id=(x, y, 1 - c), device_id_type=MESH)

    def passed(outs, sems, w, s, cc):
        x, y, c, me = _where()
        j = (me + 3 - s) % 4
        return pltpu.make_async_remote_copy(
            src_ref=half(outs[w], w, j, cc), dst_ref=half(outs[w], w, j, cc), send_sem=sems[2].at[3 * w + s],
            recv_sem=sems[3].at[3 * w + s], device_id=(x, y, 1 - c), device_id_type=MESH)

    pairs = [(w, s) for w in range(n) for s in range(3)]

    def start(ins, outs, sems):
        for w, s in pairs:
            ici(ins, outs, sems, w, s).start()

    def finish(ins, outs, sems):
        _, _, c, _ = _where()
        for w, s in pairs:
            landed(outs, sems, w, s).wait_recv()
            passed(outs, sems, w, s, c).start()
        for w, s in pairs:
            passed(outs, sems, w, s, 1 - c).wait_recv()
        for w, s in pairs:
            ici(ins, outs, sems, w, s).wait_send()
            passed(outs, sems, w, s, c).wait_send()

    return _Payload(bufs, [jax.ShapeDtypeStruct(b.shape, b.dtype) for b in bufs], {w: w for w in range(n)},
                    [pltpu.SemaphoreType.DMA((3 * n,))] * 4, start, finish)


def _sym_payload(operands, outs, copies, n_copies):
    def start(ins, outs_, sems):
        for cp in copies(ins, outs_, sems[0], sems[1]):
            cp.start()

    def finish(ins, outs_, sems):
        for cp in copies(ins, outs_, sems[0], sems[1]):
            cp.wait()

    p = _Payload(operands, outs, {}, [pltpu.SemaphoreType.DMA((n_copies,))] * 2, start, finish)
    p.copies, p.n_copies = copies, n_copies
    return p


def _swap_payload(g4s):
    def copies(ins, outs, ss, rs):
        x, y, c, me = _where()
        cps = []
        for w, g in enumerate(g4s):
            h = g.shape[1] // 2
            cps.append(pltpu.make_async_remote_copy(
                src_ref=ins[w].at[:, pl.ds((1 - c) * h, h)], dst_ref=outs[w], send_sem=ss.at[w],
                recv_sem=rs.at[w], device_id=(x, y, 1 - c), device_id_type=MESH))
        return cps

    outs = [jax.ShapeDtypeStruct((4, g.shape[1] // 2, g.shape[2]), g.dtype) for g in g4s]
    return _sym_payload(g4s, outs, copies, len(g4s))


def _exchange_payload(pbs):
    def copies(ins, outs, ss, rs):
        x, y, c, me = _where()
        cps = []
        for w in range(len(pbs)):
            for s in range(3):
                k = (me + 1 + s) % 4
                cps.append(pltpu.make_async_remote_copy(
                    src_ref=ins[w].at[k], dst_ref=outs[w].at[2 - s], send_sem=ss.at[3 * w + s],
                    recv_sem=rs.at[3 * w + s], device_id=_chip_dev(k, c), device_id_type=MESH))
        return cps

    outs = [jax.ShapeDtypeStruct((3,) + p.shape[1:], p.dtype) for p in pbs]
    return _sym_payload(pbs, outs, copies, 3 * len(pbs))


HBM_REF = pl.BlockSpec(memory_space=pltpu.HBM)
SEM_REF = pl.BlockSpec(memory_space=pltpu.SEMAPHORE)
DATAFLOW = pltpu.SideEffectType.DATAFLOW_SIDE_EFFECTING


class _SemList:
    def __init__(self, refs):
        self.refs = refs

    @property
    def at(self):
        return self.refs


def _split_start(p, name):
    n_in, n_out, nc = len(p.operands), len(p.outs), p.n_copies
    lands = getattr(p, "lands", None) or [lax.empty(s.shape, s.dtype) for s in p.outs]

    def body(*refs):
        ins, lnd = refs[:n_in], refs[n_in:n_in + n_out]
        sems = refs[n_in + n_out:n_in + n_out + 2 * nc]
        for cp in p.copies(ins, lnd, _SemList(sems[:nc]), _SemList(sems[nc:])):
            cp.start()
        refs[-1][...] = jnp.zeros((8, 128), F32)

    res = pl.pallas_call(
        body, name=name,
        in_specs=[HBM_REF] * (n_in + n_out),
        out_specs=[SEM_REF] * (2 * nc) + [HBM_REF] * (n_in + n_out) + [VMEM_FULL],
        out_shape=([pltpu.SemaphoreType.DMA(())] * (2 * nc) + _hbm_out(p.operands) + _hbm_out(lands)
                   + [jax.ShapeDtypeStruct((8, 128), F32)]),
        input_output_aliases={i: 2 * nc + i for i in range(n_in + n_out)},
        compiler_params=pltpu.CompilerParams(has_side_effects=DATAFLOW),
    )(*_hbm(*p.operands, *lands))
    k = 2 * nc
    return list(res[:k]), list(res[k:k + n_in]), list(res[k + n_in:k + n_in + n_out]), res[-1]


def _split_wait(p, handle, after, name):
    sems, srcs, lands, _ = handle
    n_in, n_out, nc = len(srcs), len(lands), p.n_copies

    def body(*refs):
        ins, lnd = refs[:n_in], refs[n_in:n_in + n_out]
        sm = refs[n_in + n_out:n_in + n_out + 2 * nc]
        for cp in p.copies(ins, lnd, _SemList(sm[:nc]), _SemList(sm[nc:])):
            cp.wait_send()
            cp.wait_recv()

    res = pl.pallas_call(
        body, name=name,
        in_specs=[HBM_REF] * (n_in + n_out) + [SEM_REF] * (2 * nc) + [ANY] * len(after),
        out_specs=[HBM_REF] * (n_in + n_out), out_shape=_hbm_out(srcs) + _hbm_out(lands),
        input_output_aliases={i: i for i in range(n_in + n_out)},
        compiler_params=pltpu.CompilerParams(has_side_effects=DATAFLOW),
    )(*srcs, *lands, *sems, *after)
    return list(res[:n_in]), list(res[n_in:])


def _join_payload(halves):
    def copies(ins, outs, ss, rs):
        x, y, c, me = _where()
        return [pltpu.make_async_remote_copy(
            src_ref=ins[w], dst_ref=outs[w], send_sem=ss.at[w], recv_sem=rs.at[w],
            device_id=(x, y, 1 - c), device_id_type=MESH) for w in range(len(halves))]

    outs = [jax.ShapeDtypeStruct(a.shape, a.dtype) for a in halves]
    return _sym_payload(halves, outs, copies, len(halves))


def _allgather_payload(v):
    def copies(ins, outs, ss, rs):
        x, y, c, me = _where()
        lin = 4 * x + 2 * y + c
        cps = []
        for o in range(1, 8):
            t = (lin + o) % 8
            cps.append(pltpu.make_async_remote_copy(
                src_ref=ins[0], dst_ref=outs[0].at[lin], send_sem=ss.at[o - 1], recv_sem=rs.at[o - 1],
                device_id=(t // 4, (t // 2) % 2, t % 2), device_id_type=MESH))
        return cps

    p = _sym_payload([v], [jax.ShapeDtypeStruct((8,) + v.shape, v.dtype)], copies, 7)
    x, y, c, _ = _where()
    p.lands = [lax.dynamic_update_slice(jnp.zeros((8,) + v.shape, v.dtype), v[None], (4 * x + 2 * y + c, 0, 0))]
    return p


def _sum8(buf, token):
    _, P, C = buf.shape

    def body(b_ref, t_ref, o_ref):
        acc = b_ref[0]
        for d in range(1, 8):
            acc = acc + b_ref[d]
        o_ref[...] = acc

    return pl.pallas_call(
        body, name="sum8", in_specs=[VMEM_FULL, VMEM_FULL], out_specs=VMEM_FULL,
        out_shape=jax.ShapeDtypeStruct((P, C), F32),
        compiler_params=pltpu.CompilerParams(vmem_limit_bytes=32 << 20),
    )(buf, token)


def _row_tile(h):
    for t in (256, 176, 128, 64, 32, 16, 8):
        if h % t == 0:
            return t
    raise ValueError(h)


def _pair_sum(cmidx, g4, got, name):
    _, R, C = g4.shape
    h = R // 2
    th = _row_tile(h)

    def body(cm_ref, a_ref, b_ref, o_ref, ob_ref):
        sm = a_ref[...] + b_ref[...]
        ob_ref[...] = sm.astype(BF16)

        @pl.when(pl.program_id(1) == cm_ref[1])
        def _():
            o_ref[...] = sm

    blk = pl.BlockSpec((None, th, C), lambda i, k, cm: (k, i, 0))
    gs = pltpu.PrefetchScalarGridSpec(
        num_scalar_prefetch=1, grid=(h // th, 4),
        in_specs=[pl.BlockSpec((None, None, th, C), lambda i, k, cm: (k, cm[0], i, 0)), blk],
        out_specs=[pl.BlockSpec((th, C), lambda i, k, cm: (i, 0)), blk])
    return pl.pallas_call(
        body, name=name, grid_spec=gs,
        out_shape=_hbm_out([jax.ShapeDtypeStruct((h, C), F32), jax.ShapeDtypeStruct((4, h, C), BF16)]),
        compiler_params=_cp(32, 2),
    )(cmidx, *_hbm(g4.reshape(4, 2, h, C), got))


def _chip_sum(own, got, name):
    h, C = own.shape
    th = _row_tile(h)

    def body(a_ref, b_ref, o_ref):
        o_ref[...] = ((a_ref[...] + b_ref[0].astype(F32)) + b_ref[1].astype(F32)) + b_ref[2].astype(F32)

    return pl.pallas_call(
        body, name=name, grid=(h // th,),
        in_specs=[pl.BlockSpec((th, C), lambda i: (i, 0)), pl.BlockSpec((3, th, C), lambda i: (0, i, 0))],
        out_specs=pl.BlockSpec((th, C), lambda i: (i, 0)),
        out_shape=_hbm_out(jax.ShapeDtypeStruct((h, C), F32)),
        compiler_params=_cp(32, 1),
    )(*_hbm(own, got))


def _adamw_math(w, g, m, v):
    m2 = B1 * m + (1.0 - B1) * g
    v2 = B2 * v + (1.0 - B2) * (g * g)
    m_hat = m2 / (1.0 - B1 ** STEP)
    v_hat = v2 / (1.0 - B2 ** STEP)
    delta = -LR * (m_hat / (jnp.sqrt(v_hat) + EPS) + WD * w)
    return delta, m2, v2


def _adamw_pair(cidx, w, mine, theirs, m, v, token, name):
    R, C = w.shape
    h = R // 2
    tr = _row_tile(h)
    nh = h // tr

    def body(c_ref, w_ref, a_ref, b_ref, m_ref, v_ref, t_ref, g_ref, d_ref, mo_ref, vo_ref):
        own = (pl.program_id(0) // nh) == c_ref[0]
        g = jnp.where(own, a_ref[...], b_ref[...])
        d, m2, v2 = _adamw_math(w_ref[...], g, m_ref[...], v_ref[...])
        g_ref[...] = g
        d_ref[...] = d
        mo_ref[...] = m2
        vo_ref[...] = v2

    blk = pl.BlockSpec((tr, C), lambda i, c: (i, 0))
    mine_blk = pl.BlockSpec((tr, C), lambda i, c: (jnp.clip(i - c[0] * nh, 0, nh - 1), 0))
    theirs_blk = pl.BlockSpec((tr, C), lambda i, c: (jnp.clip(i - (1 - c[0]) * nh, 0, nh - 1), 0))
    gs = pltpu.PrefetchScalarGridSpec(
        num_scalar_prefetch=1, grid=(R // tr,),
        in_specs=[blk, mine_blk, theirs_blk, blk, blk, pl.BlockSpec((8, 128), lambda i, c: (0, 0))],
        out_specs=[blk] * 4)
    return pl.pallas_call(
        body, name=name, grid_spec=gs, out_shape=_hbm_out([jax.ShapeDtypeStruct((R, C), F32)] * 4),
        compiler_params=_cp(32, 1),
    )(cidx, *_hbm(w, mine, theirs, m, v), token)


def _adamw(w, g, m, v, name):
    R, C = w.shape
    tr = _row_tile(R)

    def body(w_ref, g_ref, m_ref, v_ref, d_ref, mo_ref, vo_ref):
        d, m2, v2 = _adamw_math(w_ref[...], g_ref[...], m_ref[...], v_ref[...])
        d_ref[...] = d
        mo_ref[...] = m2
        vo_ref[...] = v2

    blk = pl.BlockSpec((tr, C), lambda i: (i, 0))
    return pl.pallas_call(
        body, name=name, grid=(R // tr,), in_specs=[blk] * 4, out_specs=[blk] * 3,
        out_shape=_hbm_out([jax.ShapeDtypeStruct((R, C), F32)] * 3),
        compiler_params=_cp(32, 1),
    )(*_hbm(w, g, m, v))


def _pack(arrs):
    flat = jnp.concatenate([a.reshape(-1).astype(F32) for a in arrs])
    rows = -(-flat.shape[0] // 1024)
    rows = -(-rows // 8) * 8
    return jnp.pad(flat, (0, rows * 1024 - flat.shape[0])).reshape(rows, 1024)


def _unpack(packed, shapes):
    flat = packed.reshape(-1)
    out, off = [], 0
    for s in shapes:
        n = math.prod(s)
        out.append(flat[off:off + n].reshape(s))
        off += n
    return out


BIG = ["ffn1_w_in", "ffn1_w_out", "mix_w_in", "conv_w_out", "ssm_w_glu", "mix_w_out",
       "ffn2_w_in", "ffn2_w_out", "ple_w_in", "ple_w_gate"]
SMALL = ["ln1_g", "ln1_b", "conv_w", "conv_b", "ssm_lam_re", "ssm_lam_im", "ssm_log_step", "ssm_b_re", "ssm_b_im",
         "ssm_c_re", "ssm_c_im", "ssm_d", "ln2_g", "ln2_b", "ln3_g", "ln3_b", "ln4_g", "ln4_b"]
WEIGHTS = ["ffn1_w_in", "ffn1_w_out", "ln1_g", "ln1_b", "mix_w_in", "conv_w", "conv_b", "conv_w_out",
           "ssm_lam_re", "ssm_lam_im", "ssm_log_step", "ssm_b_re", "ssm_b_im", "ssm_c_re", "ssm_c_im", "ssm_d",
           "ssm_w_glu", "mix_w_out", "ln2_g", "ln2_b", "ffn2_w_in", "ffn2_w_out", "ln3_g", "ln3_b",
           "ple_w_in", "ple_w_gate", "ln4_g", "ln4_b"]


class _NoComm:
    def __init__(self, W):
        self.W, self.G, self.raw, self.done = dict(W), {}, None, {}

    def carry(self, name):
        return ()

    def landed(self, name, got):
        pass

    def grad(self, name, g4):
        self.G[name] = g4

    def small(self, raw):
        self.raw = raw


def _s5_operands(sp):
    abr, abi, bbr, bbi = _zoh(sp["ssm_lam_re"], sp["ssm_lam_im"], sp["ssm_log_step"], sp["ssm_b_re"], sp["ssm_b_im"])
    return (_wb_blocks(bbr), _wb_blocks(bbi), _wc_blocks(sp["ssm_c_re"]), _wc_blocks(-sp["ssm_c_im"]),
            abr.reshape(1, LANES), abi.reshape(1, LANES), sp["ssm_d"].reshape(1, SSM))


def _local_step(x, p, target, sp, sched, tm_ffn, tm_mix, ops=None):
    W = sched.W
    wb_re, wb_im, wc_re4, wc_im4, a_re, a_im, dvec = ops if ops is not None else _s5_operands(sp)

    def run(fn, name, *args, **kw):
        outs, got = fn(*args, comm=sched.carry(name), **kw)
        sched.landed(name, got)
        sched.done[name] = outs[0]
        return outs

    def dw(name, wname, a, b, tk, tn, shape4, shard_cols=None, interleaved=False):
        out, got = _mm_tn(a, b, tk, tn, name, shard_cols=shard_cols, interleaved=interleaved,
                          comm=sched.carry(name))
        sched.landed(name, got)
        sched.done[name] = out
        sched.grad(wname, out.reshape(shape4))

    xb = x.astype(BF16)
    h1, r1, x1, x1b = run(_ffn_fwd, "ffn1_fwd", x, xb, W["ffn1_w_in"], W["ffn1_w_out"].reshape(2, FFH, D),
                          sp["ln1_g"], sp["ln1_b"], tm_ffn, "ffn1_fwd")
    conv_w = W["conv_w"][:, 0:3, :].transpose(1, 0, 2).reshape(3, CONV)
    pc, z_b, yin_b, su, su_b, g_conv, g_ssm, y_conv = run(
        _mix_fwd_a, "mix_fwd_a", x1b, W["mix_w_in"], conv_w, sp["conv_b"], W["conv_w_out"], tm_mix)
    st_re, st_im = run(_s5_scan_fwd, "s5_scan_fwd", su_b, wb_re, wb_im, a_re, a_im)
    w_mo = W["mix_w_out"].reshape(D, D)
    s, sg_b, ga, gb, merged_b, r2, x2, x2b = run(
        _mix_fwd_b, "mix_fwd_b", st_re, st_im, wc_re4, wc_im4, su, dvec, W["ssm_w_glu"], g_conv, g_ssm, y_conv,
        w_mo, x1, sp["ln2_g"], sp["ln2_b"], tm_mix)
    w2o2 = W["ffn2_w_out"].reshape(2, FFH, D)
    h2, r3, x3, x3b = run(_ffn_fwd, "ffn2_fwd", x2, x2b, W["ffn2_w_in"], w2o2, sp["ln3_g"], sp["ln3_b"], tm_ffn,
                          "ffn2_fwd")
    loss_part, dx3, p_b, dpw_b, dgt_b, dg4, db4 = _ple_loss(
        x3, x3b, p, W["ple_w_in"], W["ple_w_gate"].reshape(D, D), sp["ln4_g"], sp["ln4_b"], target, tm_mix)

    dw("dw_ple_gate", "ple_w_gate", x3b, dgt_b, 512, 1024, (4, 256, D))
    dw("dw_ple_in", "ple_w_in", p_b, dpw_b, 256, 256, (4, 256, 256), shard_cols=256)
    dx2, dh2, a2_b, df2_b, dg3, db3 = run(_ffn_bwd, "ffn2_bwd", dx3, r3, sp["ln3_g"], h2, W["ffn2_w_in"], w2o2,
                                          tm_mix, "ffn2_bwd")
    dw("dw_ffn2_in", "ffn2_w_in", x2b, dh2, 512, FFH, (4, D, FFH), shard_cols=FFH, interleaved=True)
    dw("dw_ffn2_out", "ffn2_w_out", a2_b, df2_b, FFH, 1024, (4, FF // 4, D))
    (dres, dmix_b, dgl_b, ds_b, du_dir, gs_re, gs_im, dyc_b, dproj, dg2, db2, dd) = run(
        _mix_bwd_b, "mix_bwd_b", dx2, r2, sp["ln2_g"], w_mo, g_conv, g_ssm, y_conv, ga, gb, s, su, dvec,
        W["ssm_w_glu"], wc_re4, wc_im4, tm_mix)
    dw("dw_mix_out", "mix_w_out", merged_b, dmix_b, 512, 1024, (4, 256, D))
    dw("dw_glu", "ssm_w_glu", sg_b, dgl_b, 512, 512, (4, SSM, 512), shard_cols=512)
    dsu_ssm, dwb_re, dwb_im, dwc_re, dwc_im, da_re, da_im = run(
        _s5_scan_bwd, "s5_scan_bwd", gs_re, gs_im, st_re, st_im, su_b, ds_b, wb_re, wb_im, a_re, a_im)
    dw("dw_conv_out", "conv_w_out", yin_b, dyc_b, 512, 256, (4, CONV, 256), shard_cols=256)
    dproj, dx1, dcw8, dcb = run(_mix_bwd_a, "mix_bwd_a", dyc_b, W["conv_w_out"], pc, z_b, conv_w, dsu_ssm,
                                du_dir, dproj, dres, W["mix_w_in"], tm_mix)
    dw("dw_mix_in", "mix_w_in", x1b, dproj, 512, 1024, (4, D, D), shard_cols=1024)
    dx0, dh1, a1_b, df1_b, dg1, db1 = run(_ffn_bwd, "ffn1_bwd", dx1, r1, sp["ln1_g"], h1, W["ffn1_w_in"],
                                          W["ffn1_w_out"].reshape(2, FFH, D), tm_mix, "ffn1_bwd")
    sched.small(dict(
        ln1_g=dg1, ln1_b=db1, ln2_g=dg2, ln2_b=db2, ln3_g=dg3, ln3_b=db3, ln4_g=dg4, ln4_b=db4,
        conv_w=dcw8[0:3], conv_b=dcb,
        a_re=da_re.reshape(GROUPS, STATE), a_im=da_im.reshape(GROUPS, STATE),
        bb_re=_wb_diag(dwb_re), bb_im=_wb_diag(dwb_im),
        ssm_c_re=_wc_diag(dwc_re), ssm_c_im=-_wc_diag(dwc_im), ssm_d=dd.reshape(GROUPS, 16),
        loss=loss_part[0:1, 0]))
    dw("dw_ffn1_in", "ffn1_w_in", xb, dh1, 512, FFH, (4, D, FFH), shard_cols=FFH, interleaved=True)
    dw("dw_ffn1_out", "ffn1_w_out", a1_b, df1_b, FFH, 1024, (4, FF // 4, D))
    return loss_part[0, 0], dx0


RAW_ORDER = ["ln1_g", "ln1_b", "ln2_g", "ln2_b", "ln3_g", "ln3_b", "ln4_g", "ln4_b", "conv_w", "conv_b",
             "a_re", "a_im", "bb_re", "bb_im", "ssm_c_re", "ssm_c_im", "ssm_d", "loss"]

GATHER_FIRST = ["ffn1_w_in", "ffn1_w_out"]
GATHER_AT = {"ffn1_fwd": ["mix_w_in", "conv_w_out", "conv_w"], "mix_fwd_a": ["ssm_w_glu", "mix_w_out"],
             "s5_scan_fwd": ["ffn2_w_in"], "mix_fwd_b": ["ffn2_w_out"], "ffn2_fwd": ["ple_w_in", "ple_w_gate"]}
REDUCE_GROUP = {"ple": ["ple_w_gate", "ple_w_in"], "ffn2": ["ffn2_w_in", "ffn2_w_out"],
                "mix": ["mix_w_out", "ssm_w_glu", "conv_w_out", "mix_w_in"], "ffn1": ["ffn1_w_in", "ffn1_w_out"]}
REDUCE_AT = {"ffn2_bwd": [("swap", "ple")], "dw_ffn2_in": [("exchange", "ple")],
             "mix_bwd_b": [("swap", "ffn2"), ("join", "ple")],
             "mix_bwd_a": [("join", "ffn2")], "ffn1_bwd": [("swap", "mix")]}
BEGIN_AT = {"dw_mix_out": [("exchange", "ffn2")], "dw_ffn1_in": [("small", None), ("exchange", "mix")]}
BEHIND = {"dw_glu": [("exchange", "ffn2")], "s5_scan_bwd": [("exchange", "ffn2")]}
END_AT = {"mix_bwd_a": [("exchange", "ffn2", ["dw_mix_out", "dw_glu", "s5_scan_bwd"])]}
LAST_GROUP = "ffn1"


class _Sched:
    def __init__(self, cmidx):
        self.bufs, self.cmidx = {}, cmidx
        self.W, self.G, self.raw, self.small_buf = {}, {}, None, None
        self.got1, self.p32, self.pbf, self.got2, self.half, self.theirs = {}, {}, {}, {}, {}, {}
        self._open, self._split, self.done = [], {}, {}

    def first_begin(self, bufs):
        self.bufs.update(bufs)
        p = _gather_ici_payload([bufs[n] for n in GATHER_FIRST])
        self._first = (p, _split_start(p, "gather_first_start"))
        return self._first[1][3]

    def first_end(self, bufs, after):
        self.bufs.update(bufs)
        p, handle = self._first
        _, landed = _split_wait(p, handle, after, "gather_first_wait")
        (outs,) = _comm_call("gather_first_pass", [_gather_pass_payload(landed)])
        self.W.update(zip(GATHER_FIRST, outs))

    def _payload(self, stage, key):
        if stage == "gather":
            return _gather_payload([self.bufs[n] for n in key])
        if stage == "small":
            return _allgather_payload(_pack([self.raw[k] for k in RAW_ORDER]))
        names = REDUCE_GROUP[key]
        if stage == "swap":
            return _swap_payload([self.G[n] for n in names])
        if stage == "exchange":
            for n in names:
                self.p32[n], self.pbf[n] = _pair_sum(self.cmidx, self.G[n], self.got1[n], "pair_sum_" + n)
            return _exchange_payload([self.pbf[n] for n in names])
        for n in names:
            self.half[n] = _chip_sum(self.p32[n], self.got2[n], "chip_sum_" + n)
        return _join_payload([self.half[n] for n in names])

    def _store(self, stages, got):
        for (stage, key), outs in zip(stages, got):
            if stage == "gather":
                self.W.update(zip(key, outs))
            elif stage == "small":
                self.small_buf = outs[0]
            else:
                {"swap": self.got1, "exchange": self.got2, "join": self.theirs}[stage].update(
                    zip(REDUCE_GROUP[key], outs))

    def _standalone(self, name, stages):
        self._store(stages, _comm_call(name, [self._payload(s, k) for s, k in stages]))

    def carry(self, name):
        for stage, key, behind in END_AT.get(name, []):
            self._end(stage, key, [self.done[b] for b in behind])
        tokens = [self._begin(stage, key) for stage, key in BEGIN_AT.get(name, [])]
        tokens += [self._split[sk][1][3] for sk in BEHIND.get(name, [])]
        self._open = [("gather", GATHER_AT[name])] if name in GATHER_AT else []
        self._open += REDUCE_AT.get(name, [])
        comm = [self._payload(s, k) for s, k in self._open]
        if tokens:
            comm.append(_Payload(tokens, [], {}, [], lambda *a: None, lambda *a: None))
        return tuple(comm)

    def landed(self, name, got):
        self._store(self._open, got)

    def grad(self, name, g4):
        self.G[name] = g4

    def small(self, raw):
        self.raw = raw

    def _begin(self, stage, key):
        p = self._payload(stage, key)
        self._split[stage, key] = (p, _split_start(p, "%s_%s_start" % (stage, key)))
        return self._split[stage, key][1][3]

    def _end(self, stage, key, after):
        p, handle = self._split.pop((stage, key))
        srcs, lands = _split_wait(p, handle, after, "%s_%s_wait" % (stage, key))
        if stage == "swap":
            self.G.update(zip(REDUCE_GROUP[key], srcs))
        self._store([(stage, key)], [lands])

    def tail_begin(self):
        return self._begin("swap", LAST_GROUP)

    def tail_mid(self, after):
        self._end("swap", LAST_GROUP, after)
        token = self._begin("exchange", LAST_GROUP)
        self._end("small", None, [token])
        self._end("exchange", "mix", [token])
        self._standalone("reduce_tail_join_mix", [("join", "mix")])
        return token

    def tail_end(self, after):
        self._end("exchange", LAST_GROUP, after)
        self._standalone("reduce_tail_join", [("join", LAST_GROUP)])


def _small_grads(raw_sum, sp):
    _, vjp = jax.vjp(_zoh, sp["ssm_lam_re"], sp["ssm_lam_im"], sp["ssm_log_step"], sp["ssm_b_re"], sp["ssm_b_im"])
    d_lre, d_lim, d_ls, d_bre, d_bim = vjp((raw_sum["a_re"], raw_sum["a_im"], raw_sum["bb_re"], raw_sum["bb_im"]))
    g = {k: raw_sum[k] for k in ("ln1_g", "ln1_b", "ln2_g", "ln2_b", "ln3_g", "ln3_b", "ln4_g", "ln4_b",
                                 "conv_w", "conv_b", "ssm_c_re", "ssm_c_im", "ssm_d")}
    g.update(ssm_lam_re=d_lre, ssm_lam_im=d_lim, ssm_log_step=d_ls, ssm_b_re=d_bre, ssm_b_im=d_bim)
    return g


def kernel(x, p, ffn1_w_in, ffn1_w_out, ln1_g, ln1_b, mix_w_in, conv_w, conv_b, conv_w_out, ssm_lam_re, ssm_lam_im, ssm_log_step, ssm_b_re, ssm_b_im, ssm_c_re, ssm_c_im, ssm_d, ssm_w_glu, mix_w_out, ln2_g, ln2_b, ffn2_w_in, ffn2_w_out, ln3_g, ln3_b, ple_w_in, ple_w_gate, ln4_g, ln4_b, loss_target, m_ffn1_w_in, m_ffn1_w_out, m_ln1_g, m_ln1_b, m_mix_w_in, m_conv_w, m_conv_b, m_conv_w_out, m_ssm_lam_re, m_ssm_lam_im, m_ssm_log_step, m_ssm_b_re, m_ssm_b_im, m_ssm_c_re, m_ssm_c_im, m_ssm_d, m_ssm_w_glu, m_mix_w_out, m_ln2_g, m_ln2_b, m_ffn2_w_in, m_ffn2_w_out, m_ln3_g, m_ln3_b, m_ple_w_in, m_ple_w_gate, m_ln4_g, m_ln4_b, v_ffn1_w_in, v_ffn1_w_out, v_ln1_g, v_ln1_b, v_mix_w_in, v_conv_w, v_conv_b, v_conv_w_out, v_ssm_lam_re, v_ssm_lam_im, v_ssm_log_step, v_ssm_b_re, v_ssm_b_im, v_ssm_c_re, v_ssm_c_im, v_ssm_d, v_ssm_w_glu, v_mix_w_out, v_ln2_g, v_ln2_b, v_ffn2_w_in, v_ffn2_w_out, v_ln3_g, v_ln3_b, v_ple_w_in, v_ple_w_gate, v_ln4_g, v_ln4_b):
    args = dict(locals())
    w = {n: args[n] for n in WEIGHTS}
    m = {n: args["m_" + n] for n in WEIGHTS}
    v = {n: args["v_" + n] for n in WEIGHTS}
    _, _, c, me = _where()
    cidx = jnp.stack([c, me]).astype(jnp.int32)
    meidx = jnp.reshape(me, (1,)).astype(jnp.int32)

    sched = _Sched(cidx)
    token = sched.first_begin({n: _slot_cast(meidx, w[n][0], BF16, "cast_" + n) for n in GATHER_FIRST})
    rest = {n: _slot_cast(meidx, w[n][0], BF16, "cast_" + n, (token,)) for n in BIG if n not in GATHER_FIRST}
    rest["conv_w"] = _slot_cast(meidx, jnp.pad(conv_w[0], ((0, 13), (0, 0))), F32, "cast_conv_w", (token,))
    sp = {n: (w[n] if w[n].ndim == 2 and n != "ssm_log_step" else w[n][0]) for n in SMALL if n != "conv_w"}
    ops = _s5_operands({**sp, "ssm_lam_re": sp["ssm_lam_re"] + token[0, 0]})
    sched.first_end(rest, list(rest.values()) + list(ops))
    loss_part, dx0 = _local_step(x[0], p[0, 0], loss_target[0], sp, sched, 256, 256, ops)
    out_g, out_d, out_m, out_v = {}, {}, {}, {}

    def big_adamw(names, token):
        for n in names:
            g, dl, mn, vn = _adamw_pair(cidx, w[n][0], sched.half[n], sched.theirs[n], m[n][0], v[n][0], token,
                                        "adamw_" + n)
            out_g[n], out_d[n], out_m[n], out_v[n] = g[None], dl[None], mn[None], vn[None]

    first = REDUCE_GROUP["ple"] + ["ffn2_w_in"]
    big_adamw(first, sched.tail_begin())
    token = sched.tail_mid([out_v[n] for n in first])
    big_adamw(["ffn2_w_out"], token)

    raw_shapes = [sched.raw[k].shape for k in RAW_ORDER]
    raw_sum = dict(zip(RAW_ORDER, _unpack(_sum8(sched.small_buf, token), raw_shapes)))
    loss = raw_sum["loss"][0]
    sg = _small_grads(raw_sum, sp)
    sg["conv_w"] = lax.dynamic_slice_in_dim(sg["conv_w"], me * 128, 128, axis=1)
    small_shapes = [w[n].shape for n in SMALL]
    gp = _pack([sg[n] for n in SMALL])
    d_s, m_s, v_s = _adamw(_pack([w[n] for n in SMALL]), gp, _pack([m[n] for n in SMALL]),
                           _pack([v[n] for n in SMALL]), "adamw_small")

    for n, a, b_, c_, d_ in zip(SMALL, _unpack(gp, small_shapes), _unpack(d_s, small_shapes),
                                _unpack(m_s, small_shapes), _unpack(v_s, small_shapes)):
        out_g[n], out_d[n], out_m[n], out_v[n] = a, b_, c_, d_
    big_adamw(REDUCE_GROUP["mix"], token)
    sched.tail_end([d_s, out_v["ffn2_w_out"]] + [out_v[n] for n in REDUCE_GROUP["mix"]])
    big_adamw(REDUCE_GROUP[LAST_GROUP], token)

    return (loss, dx0[None], *[out_g[n] for n in WEIGHTS], *[out_d[n] for n in WEIGHTS],
            *[out_m[n] for n in WEIGHTS], *[out_v[n] for n in WEIGHTS])
```

```python
import functools
import math

import jax
import jax.numpy as jnp
import numpy as np
from jax import lax
from jax.experimental import pallas as pl
from jax.experimental.pallas import tpu as pltpu

F32, BF16 = jnp.float32, jnp.bfloat16
D = 1024
FF = 2816
FFH = FF // 2
CONV = 512
SSM = 512
GROUPS = 32
STATE = 64
LANES = GROUPS * STATE
SCAN_W = 128
SCAN_PER = 512 // SCAN_W
SCAN_GR = SCAN_W // STATE
SCAN_R = 256
ALPHA = 2.0 ** 0.25
LN_EPS = 1e-5
GELU_C = math.sqrt(2.0 / math.pi)
B1, B2, LR, EPS, WD, STEP = 0.9, 0.999, 0.001, 1e-8, 0.01, 10
MESH = pl.DeviceIdType.MESH
ANY = pl.BlockSpec(memory_space=pl.ANY)
VMEM_FULL = pl.BlockSpec(memory_space=pltpu.VMEM)


def _cp(vmem_mb=48, n_axes=1):
    return pltpu.CompilerParams(vmem_limit_bytes=vmem_mb << 20,
                                dimension_semantics=("arbitrary",) * n_axes)


def _hbm(*arrs):
    return [pltpu.with_memory_space_constraint(a, pltpu.HBM) for a in arrs]


def _hbm_out(shapes):
    if isinstance(shapes, (list, tuple)):
        return [pltpu.HBM(s.shape, s.dtype) for s in shapes]
    return pltpu.HBM(shapes.shape, shapes.dtype)


def _nn(a, b):
    return jnp.dot(a, b, preferred_element_type=F32)


def _nt(a, b):
    return lax.dot_general(a, b, (((1,), (1,)), ((), ())), preferred_element_type=F32)


def _tn(a, b):
    return lax.dot_general(a, b, (((0,), (0,)), ((), ())), preferred_element_type=F32)


def _sig(v):
    return jax.nn.sigmoid(v)


def _ln_stats(r):
    mu = jnp.mean(r, axis=-1, keepdims=True)
    xc = r - mu
    var = jnp.mean(xc * xc, axis=-1, keepdims=True)
    rstd = lax.rsqrt(var + LN_EPS)
    return xc * rstd, rstd


def _ln_bwd(dy, r, g):
    xhat, rstd = _ln_stats(r)
    dyg = dy * g
    m1 = jnp.mean(dyg, axis=-1, keepdims=True)
    m2 = jnp.mean(dyg * xhat, axis=-1, keepdims=True)
    return rstd * (dyg - m1 - xhat * m2), xhat


def _rowsum(v):
    return jnp.sum(v, axis=0, keepdims=True)


class _Payload:
    def __init__(self, operands, outs, aliases, sems, start, finish):
        self.operands, self.outs, self.aliases, self.sems = list(operands), list(outs), dict(aliases), list(sems)
        self.start, self.finish = start, finish


def _split(flat, comm, attr):
    out, i = [], 0
    for p in comm:
        n = len(getattr(p, attr))
        out.append(list(flat[i:i + n]))
        i += n
    return out


def _run_comm(comm, which, cin, cout, csem):
    for p, a, b, s in zip(comm, _split(cin, comm, "operands"), _split(cout, comm, "outs"), _split(csem, comm, "sems")):
        getattr(p, which)(a, b, s)


def _pcall(body, *, name, grid, in_specs, out_specs, out_shape, operands, scratch=(), vmem_mb=48, aliases=None,
           comm=()):
    ni, no, ns = len(in_specs), len(out_specs), len(scratch)
    c_ops = [a for p in comm for a in p.operands]
    c_outs = [s for p in comm for s in p.outs]
    c_sems = [s for p in comm for s in p.sems]
    io = dict(aliases or {})
    off_i, off_o = ni, no
    for p in comm:
        for a, b in p.aliases.items():
            io[off_i + a] = off_o + b
        off_i += len(p.operands)
        off_o += len(p.outs)

    def wrapped(*refs):
        ins, cin = refs[:ni], refs[ni:ni + len(c_ops)]
        o0 = ni + len(c_ops)
        outs, cout = refs[o0:o0 + no], refs[o0 + no:o0 + no + len(c_outs)]
        s0 = o0 + no + len(c_outs)
        scr, csem = refs[s0:s0 + ns], refs[s0 + ns:]
        if comm:
            first = functools.reduce(jnp.logical_and, [pl.program_id(a) == 0 for a in range(len(grid))])
            pl.when(first)(lambda: _run_comm(comm, "start", cin, cout, csem))
        body(*ins, *outs, *scr)
        if comm:
            last = functools.reduce(jnp.logical_and, [pl.program_id(a) == grid[a] - 1 for a in range(len(grid))])
            pl.when(last)(lambda: _run_comm(comm, "finish", cin, cout, csem))

    res = pl.pallas_call(
        wrapped, name=name, grid=grid,
        in_specs=list(in_specs) + [ANY] * len(c_ops), out_specs=list(out_specs) + [ANY] * len(c_outs),
        out_shape=_hbm_out(list(out_shape) + c_outs), scratch_shapes=list(scratch) + c_sems,
        input_output_aliases=io,
        compiler_params=pltpu.CompilerParams(vmem_limit_bytes=vmem_mb << 20,
                                             dimension_semantics=("arbitrary",) * len(grid),
                                             has_side_effects=bool(comm)),
    )(*_hbm(*operands, *c_ops))
    return list(res[:no]), _split(res[no:], comm, "outs")


def _comm_call(name, comm):
    c_ops = [a for p in comm for a in p.operands]
    c_outs = [s for p in comm for s in p.outs]
    c_sems = [s for p in comm for s in p.sems]
    io, off_i, off_o = {}, 0, 0
    for p in comm:
        for a, b in p.aliases.items():
            io[off_i + a] = off_o + b
        off_i += len(p.operands)
        off_o += len(p.outs)

    def body(*refs):
        cin, cout = refs[:len(c_ops)], refs[len(c_ops):len(c_ops) + len(c_outs)]
        csem = refs[len(c_ops) + len(c_outs):]
        _run_comm(comm, "start", cin, cout, csem)
        _run_comm(comm, "finish", cin, cout, csem)

    res = pl.pallas_call(
        body, name=name, in_specs=[ANY] * len(c_ops), out_specs=[ANY] * len(c_outs), out_shape=_hbm_out(c_outs),
        scratch_shapes=c_sems, input_output_aliases=io,
        compiler_params=pltpu.CompilerParams(has_side_effects=True),
    )(*_hbm(*c_ops))
    return _split(res, comm, "outs")


def _ffn_fwd(x, xb, w_in4, w_out2, g, b, tm, name, comm=()):
    T = x.shape[0]

    def body(x_ref, xb_ref, win_ref, wo_ref, g_ref, b_ref, h_ref, r_ref, xo_ref, xob_ref):
        xv = xb_ref[...]
        acc = ALPHA * x_ref[...]
        for k in range(2):
            gt = _nn(xv, win_ref[k])
            up = _nn(xv, win_ref[k + 2])
            a = (gt * _sig(gt) * up).astype(BF16)
            h_ref[:, 2 * k * FFH:(2 * k + 1) * FFH] = gt.astype(BF16)
            h_ref[:, (2 * k + 1) * FFH:(2 * k + 2) * FFH] = up.astype(BF16)
            acc = acc + 0.5 * _nn(a, wo_ref[k])
        xhat, _ = _ln_stats(acc)
        xo = xhat * g_ref[...] + b_ref[...]
        r_ref[...] = acc
        xo_ref[...] = xo
        xob_ref[...] = xo.astype(BF16)

    tok = pl.BlockSpec((tm, D), lambda i: (i, 0))
    vec = pl.BlockSpec((1, D), lambda i: (0, 0))
    return _pcall(
        body, name=name, grid=(T // tm,),
        in_specs=[tok, tok,
                  pl.BlockSpec((4, D, FFH), lambda i: (0, 0, 0), pipeline_mode=pl.Buffered(1)),
                  pl.BlockSpec((2, FFH, D), lambda i: (0, 0, 0), pipeline_mode=pl.Buffered(1)),
                  vec, vec],
        out_specs=[pl.BlockSpec((tm, 2 * FF), lambda i: (i, 0)), tok, tok, tok],
        out_shape=[jax.ShapeDtypeStruct((T, 2 * FF), BF16), jax.ShapeDtypeStruct((T, D), F32),
                   jax.ShapeDtypeStruct((T, D), F32), jax.ShapeDtypeStruct((T, D), BF16)],
        vmem_mb=58, comm=comm, operands=(x, xb, w_in4, w_out2, g, b))


def _ffn_bwd(dy, r, g, h, w_in4, w_out2, tm, name, comm=()):
    T = dy.shape[0]

    def body(dy_ref, r_ref, g_ref, h_ref, win_ref, wo_ref, dx_ref, dh_ref, a_ref, df_ref, dg_ref, db_ref):
        i = pl.program_id(0)
        dyv = dy_ref[...]
        dr, xhat = _ln_bwd(dyv, r_ref[...], g_ref[...])
        dg_ref[...] = jnp.where(i == 0, 0.0, dg_ref[...]) + _rowsum(dyv * xhat)
        db_ref[...] = jnp.where(i == 0, 0.0, db_ref[...]) + _rowsum(dyv)
        dfb = (0.5 * dr).astype(BF16)
        df_ref[...] = dfb
        acc = ALPHA * dr
        for k in range(2):
            da = _nt(dfb, wo_ref[k])
            gt = h_ref[:, 2 * k * FFH:(2 * k + 1) * FFH].astype(F32)
            up = h_ref[:, (2 * k + 1) * FFH:(2 * k + 2) * FFH].astype(F32)
            sg = _sig(gt)
            silu = gt * sg
            dgate = (da * up * (sg * (1.0 + gt * (1.0 - sg)))).astype(BF16)
            dup = (da * silu).astype(BF16)
            a_ref[:, k * FFH:(k + 1) * FFH] = (silu * up).astype(BF16)
            dh_ref[:, 2 * k * FFH:(2 * k + 1) * FFH] = dgate
            dh_ref[:, (2 * k + 1) * FFH:(2 * k + 2) * FFH] = dup
            acc = acc + _nt(dgate, win_ref[k]) + _nt(dup, win_ref[k + 2])
        dx_ref[...] = acc

    tok = pl.BlockSpec((tm, D), lambda i: (i, 0))
    vec = pl.BlockSpec((1, D), lambda i: (0, 0))
    wide = pl.BlockSpec((tm, 2 * FF), lambda i: (i, 0))
    return _pcall(
        body, name=name, grid=(T // tm,),
        in_specs=[tok, tok, vec, wide,
                  pl.BlockSpec((4, D, FFH), lambda i: (0, 0, 0), pipeline_mode=pl.Buffered(1)),
                  pl.BlockSpec((2, FFH, D), lambda i: (0, 0, 0), pipeline_mode=pl.Buffered(1))],
        out_specs=[tok, wide, pl.BlockSpec((tm, FF), lambda i: (i, 0)), tok, vec, vec],
        out_shape=[jax.ShapeDtypeStruct((T, D), F32), jax.ShapeDtypeStruct((T, 2 * FF), BF16),
                   jax.ShapeDtypeStruct((T, FF), BF16), jax.ShapeDtypeStruct((T, D), BF16),
                   jax.ShapeDtypeStruct((1, D), F32), jax.ShapeDtypeStruct((1, D), F32)],
        vmem_mb=58, comm=comm, operands=(dy, r, g, h, w_in4, w_out2))


def _mm_tn(a, b, tk, tn, name, shard_cols=None, interleaved=False, comm=()):
    T, K = a.shape
    N = b.shape[1]

    def body(a_ref, b_ref, o_ref):
        o_ref[...] = _tn(a_ref[...], b_ref[...])

    if shard_cols is None:
        out_shape = jax.ShapeDtypeStruct((K, N), F32)
        out_spec = pl.BlockSpec((tk, tn), lambda ki, nj: (ki, nj))
    else:
        per = shard_cols // tn

        def shard(nj):
            blk = nj // per
            return (blk % 2) * 2 + blk // 2 if interleaved else blk

        out_shape = jax.ShapeDtypeStruct((N // shard_cols, K, shard_cols), F32)
        out_spec = pl.BlockSpec((None, tk, tn), lambda ki, nj: (shard(nj), ki, nj % per))
    (out,), got = _pcall(
        body, name=name, grid=(K // tk, N // tn),
        in_specs=[pl.BlockSpec((T, tk), lambda ki, nj: (0, ki)), pl.BlockSpec((T, tn), lambda ki, nj: (0, nj))],
        out_specs=[out_spec], out_shape=[out_shape], comm=comm, operands=(a, b))
    return out, got


def _mix_fwd_a(xb, w_mix4, conv_w, conv_b, w_co4, tm, comm=()):
    T = xb.shape[0]

    def body(xb_ref, w_ref, cw_ref, cb_ref, wco_ref,
             pc_ref, z_ref, yin_ref, su_ref, sub_ref, gc_ref, gs_ref, yc_ref, qbuf):
        @pl.when(pl.program_id(0) == 0)
        def _():
            qbuf[pl.ds(0, 8), :] = jnp.zeros((8, CONV), F32)

        xv = xb_ref[...]
        p0 = _nn(xv, w_ref[0])
        p1 = _nn(xv, w_ref[1])
        gc_ref[...] = _nn(xv, w_ref[2]).astype(BF16)
        gs_ref[...] = _nn(xv, w_ref[3]).astype(BF16)
        cbv, ccv = p0[:, :CONV], p0[:, CONV:]
        chv, suv = p1[:, :CONV], p1[:, CONV:]
        q = ccv * chv
        qbuf[pl.ds(8, tm), :] = q
        cw = cw_ref[...]
        z = (cw[2:3] * q + cw[1:2] * qbuf[pl.ds(7, tm), :] + cw[0:1] * qbuf[pl.ds(6, tm), :]
             + cb_ref[...])
        qbuf[pl.ds(0, 8), :] = q[tm - 8:tm]
        yin = (cbv * z).astype(BF16)
        pc_ref[:, 0:CONV] = cbv.astype(BF16)
        pc_ref[:, CONV:2 * CONV] = ccv.astype(BF16)
        pc_ref[:, 2 * CONV:3 * CONV] = chv.astype(BF16)
        z_ref[...] = z.astype(BF16)
        yin_ref[...] = yin
        su_ref[...] = suv
        sub_ref[...] = suv.astype(BF16)
        for k in range(4):
            yc_ref[:, 256 * k:256 * (k + 1)] = _nn(yin, wco_ref[k]).astype(BF16)

    def tok(n):
        return pl.BlockSpec((tm, n), lambda i: (i, 0))

    def full(shape):
        return pl.BlockSpec(shape, lambda i: (0,) * len(shape))

    return _pcall(
        body, name="mix_fwd_a", grid=(T // tm,),
        in_specs=[tok(D), full((4, D, D)), full((3, CONV)), full((1, CONV)), full((4, CONV, 256))],
        out_specs=[tok(3 * CONV), tok(CONV), tok(CONV), tok(SSM), tok(SSM), tok(D), tok(D), tok(D)],
        out_shape=[jax.ShapeDtypeStruct((T, 3 * CONV), BF16), jax.ShapeDtypeStruct((T, CONV), BF16),
                   jax.ShapeDtypeStruct((T, CONV), BF16), jax.ShapeDtypeStruct((T, SSM), F32),
                   jax.ShapeDtypeStruct((T, SSM), BF16), jax.ShapeDtypeStruct((T, D), BF16),
                   jax.ShapeDtypeStruct((T, D), BF16), jax.ShapeDtypeStruct((T, D), BF16)],
        scratch=[pltpu.VMEM((tm + 8, CONV), F32)], vmem_mb=56, comm=comm,
        operands=(xb, w_mix4, conv_w, conv_b, w_co4))


def _scan_rows(bre, bim, ar, ai, T, rev, load, out=None):
    R, W, G = SCAN_R, bre.shape[1], T // 8
    if rev:
        ai = -ai

    def cmul(pr, pi, xr, xi):
        return pr * xr - pi * xi, pr * xi + pi * xr

    pw = [(ar, ai)]
    for _ in range(7):
        pw.append(cmul(ar, ai, *pw[-1]))

    def shifted(v, d, axis, n, idx):
        if rev:
            return jnp.where(idx < n - d, pltpu.roll(v, n - d, axis), 0.0)
        return jnp.where(idx >= d, pltpu.roll(v, d, axis), 0.0)

    sub8 = lax.broadcasted_iota(jnp.int32, (8, W), 0)
    inside = {d: (sub8 < 8 - d) if rev else (sub8 >= d) for d in (1, 2, 4)}
    pm = {d: (jnp.where(inside[d], pw[d - 1][0], 0.0)[None], jnp.where(inside[d], pw[d - 1][1], 0.0)[None])
          for d in (1, 2, 4)}

    def step(i, _):
        t0 = pl.multiple_of(i * R, R)
        vr, vi = load(t0)
        vr, vi = vr.reshape(R // 8, 8, W), vi.reshape(R // 8, 8, W)
        for d in (1, 2, 4):
            sh = (8 - d) if rev else d
            dr, di = cmul(pm[d][0], pm[d][1], pltpu.roll(vr, sh, 1), pltpu.roll(vi, sh, 1))
            vr, vi = vr + dr, vi + di
        bre[pl.ds(t0 + 8, R), :] = vr.reshape(R, W)
        bim[pl.ds(t0 + 8, R), :] = vi.reshape(R, W)
        return 0

    lax.fori_loop(0, T // R, step, 0)

    edge = 0 if rev else 7
    cr = bre[pl.ds(8 + edge, G, stride=8), :]
    ci = bim[pl.ds(8 + edge, G, stride=8), :]
    row = lax.broadcasted_iota(jnp.int32, (G, W), 0)
    qr, qi = pw[7]
    d = 1
    while d < G:
        dr, di = cmul(qr, qi, shifted(cr, d, 0, G, row), shifted(ci, d, 0, G, row))
        cr, ci = cr + dr, ci + di
        qr, qi = qr * qr - qi * qi, 2.0 * qr * qi
        d *= 2

    nr, ni = shifted(cr, 1, 0, G, row), shifted(ci, 1, 0, G, row)
    for r in range(8):
        pr, pi = pw[7 - r] if rev else pw[r]
        dr, di = cmul(pr, pi, nr, ni)
        xr = bre[pl.ds(8 + r, G, stride=8), :] + dr
        xi = bim[pl.ds(8 + r, G, stride=8), :] + di
        if out is None:
            bre[pl.ds(8 + r, G, stride=8), :] = xr
            bim[pl.ds(8 + r, G, stride=8), :] = xi
        else:
            out[0][pl.ds(r, G, stride=8), :] = xr
            out[1][pl.ds(r, G, stride=8), :] = xi


def _scan_specs(T):
    W = SCAN_W
    lane = pl.BlockSpec((T, W), lambda j: (0, j))
    col = pl.BlockSpec((T, 128), lambda j: (0, j // SCAN_PER))
    wb = pl.BlockSpec((None, 128, W), lambda j: (j, 0, 0))
    wc = pl.BlockSpec((None, W, 128), lambda j: (j, 0, 0))
    vec = pl.BlockSpec((1, W), lambda j: (0, j))
    return lane, col, wb, wc, vec


def _s5_scan_fwd(su_b, wb_re, wb_im, a_re, a_im, comm=()):
    T = su_b.shape[0]
    W = SCAN_W

    def body(su_ref, wbr_ref, wbi_ref, ar_ref, ai_ref, sr_ref, si_ref, bre, bim):
        def load(t0):
            su = su_ref[pl.ds(t0, SCAN_R), :]
            return _nn(su, wbr_ref[...]), _nn(su, wbi_ref[...])

        _scan_rows(bre, bim, ar_ref[...], ai_ref[...], T, False, load)
        sr_ref[...] = bre[pl.ds(8, T), :].astype(BF16)
        si_ref[...] = bim[pl.ds(8, T), :].astype(BF16)

    lane, col, wb, wc, vec = _scan_specs(T)
    return _pcall(
        body, name="s5_scan_fwd", grid=(LANES // W,),
        in_specs=[col, wb, wb, vec, vec],
        out_specs=[lane, lane],
        out_shape=[jax.ShapeDtypeStruct((T, LANES), BF16)] * 2,
        scratch=[pltpu.VMEM((T + 16, W), F32)] * 2, comm=comm,
        operands=(su_b, wb_re, wb_im, a_re, a_im))


def _gelu(s):
    th = jnp.tanh(GELU_C * (s + 0.044715 * s * s * s))
    return 0.5 * s * (1.0 + th), th


def _mix_fwd_b(st_re, st_im, wc_re4, wc_im4, su, dvec, w_glu4, g_conv, g_ssm, y_conv, w_mo, x1, g, b, tm, comm=()):
    T = su.shape[0]

    def body(sr_ref, si_ref, wcr_ref, wci_ref, su_ref, d_ref, wg_ref, gc_ref, gs_ref, yc_ref, wmo_ref,
             x_ref, g_ref, b_ref, s_ref, sgb_ref, ga_ref, gb_ref, mb_ref, r_ref, xo_ref, xob_ref):
        srb = sr_ref[...]
        sib = si_ref[...]
        ys = [_nn(srb[:, 512 * J:512 * (J + 1)], wcr_ref[J]) + _nn(sib[:, 512 * J:512 * (J + 1)], wci_ref[J])
              for J in range(4)]
        s = jnp.concatenate(ys, axis=1) + d_ref[...] * su_ref[...]
        sg, _ = _gelu(s)
        sgb = sg.astype(BF16)
        ga = jnp.concatenate([_nn(sgb, wg_ref[0]), _nn(sgb, wg_ref[1])], axis=1)
        gb = jnp.concatenate([_nn(sgb, wg_ref[2]), _nn(sgb, wg_ref[3])], axis=1)
        merged = (_sig(gc_ref[...].astype(F32)) * yc_ref[...].astype(F32)
                  + _sig(gs_ref[...].astype(F32)) * (ga * _sig(gb)))
        mb = merged.astype(BF16)
        r = ALPHA * x_ref[...] + _nn(mb, wmo_ref[...])
        xhat, _ = _ln_stats(r)
        xo = xhat * g_ref[...] + b_ref[...]
        s_ref[...] = s
        sgb_ref[...] = sgb
        ga_ref[...] = ga.astype(BF16)
        gb_ref[...] = gb.astype(BF16)
        mb_ref[...] = mb
        r_ref[...] = r
        xo_ref[...] = xo
        xob_ref[...] = xo.astype(BF16)

    def tok(n):
        return pl.BlockSpec((tm, n), lambda i: (i, 0))

    def full(shape):
        return pl.BlockSpec(shape, lambda i: (0,) * len(shape))

    return _pcall(
        body, name="mix_fwd_b", grid=(T // tm,),
        in_specs=[tok(LANES), tok(LANES), full((4, 512, 128)), full((4, 512, 128)), tok(SSM), full((1, SSM)),
                  full((4, SSM, 512)), tok(D), tok(D), tok(D), full((D, D)), tok(D), full((1, D)), full((1, D))],
        out_specs=[tok(SSM), tok(SSM), tok(D), tok(D), tok(D), tok(D), tok(D), tok(D)],
        out_shape=[jax.ShapeDtypeStruct((T, SSM), F32), jax.ShapeDtypeStruct((T, SSM), BF16),
                   jax.ShapeDtypeStruct((T, D), BF16), jax.ShapeDtypeStruct((T, D), BF16),
                   jax.ShapeDtypeStruct((T, D), BF16), jax.ShapeDtypeStruct((T, D), F32),
                   jax.ShapeDtypeStruct((T, D), F32), jax.ShapeDtypeStruct((T, D), BF16)],
        vmem_mb=56, comm=comm,
        operands=(st_re, st_im, wc_re4, wc_im4, su, dvec, w_glu4, g_conv, g_ssm, y_conv, w_mo, x1, g, b))


def _ple_loss(x3, x3b, p, w_pi4, w_pg, g, b, target, tm):
    T = x3.shape[0]
    PD = p.shape[1]

    def body(x_ref, xb_ref, p_ref, wpi_ref, wpg_ref, g_ref, b_ref, t_ref,
             loss_ref, dx_ref, pb_ref, dpw_ref, dgt_ref, dg_ref, db_ref):
        i = pl.program_id(0)
        pb = p_ref[...].astype(BF16)
        pw = jnp.concatenate([_nn(pb, wpi_ref[k]) for k in range(4)], axis=1)
        gt = _nn(xb_ref[...], wpg_ref[...])
        sg = _sig(gt)
        r = ALPHA * x_ref[...] + pw * sg
        gv = g_ref[...]
        xhat, rstd = _ln_stats(r)
        err = xhat * gv + b_ref[...] - t_ref[...]
        lpart = jnp.zeros((1, 128), F32) + 0.5 * jnp.sum(jnp.mean(err * err, axis=-1, keepdims=True))
        dy = err * (1.0 / D)
        dyg = dy * gv
        m1 = jnp.mean(dyg, axis=-1, keepdims=True)
        m2 = jnp.mean(dyg * xhat, axis=-1, keepdims=True)
        dr = rstd * (dyg - m1 - xhat * m2)
        pg, pbias = _rowsum(dy * xhat), _rowsum(dy)

        @pl.when(i == 0)
        def _():
            loss_ref[...] = lpart
            dg_ref[...] = pg
            db_ref[...] = pbias

        @pl.when(i > 0)
        def _():
            loss_ref[...] += lpart
            dg_ref[...] += pg
            db_ref[...] += pbias

        dgt = (dr * pw * sg * (1.0 - sg)).astype(BF16)
        pb_ref[...] = pb
        dpw_ref[...] = (dr * sg).astype(BF16)
        dgt_ref[...] = dgt
        dx_ref[...] = ALPHA * dr + _nt(dgt, wpg_ref[...])

    def tok(n):
        return pl.BlockSpec((tm, n), lambda i: (i, 0))

    def full(shape):
        return pl.BlockSpec(shape, lambda i: (0,) * len(shape))

    return pl.pallas_call(
        body, name="ple_loss", grid=(T // tm,),
        in_specs=[tok(D), tok(D), tok(PD), full((4, PD, 256)), full((D, D)), full((1, D)), full((1, D)), tok(D)],
        out_specs=[full((1, 128)), tok(D), tok(PD), tok(D), tok(D), full((1, D)), full((1, D))],
        out_shape=_hbm_out([jax.ShapeDtypeStruct((1, 128), F32), jax.ShapeDtypeStruct((T, D), F32),
                            jax.ShapeDtypeStruct((T, PD), BF16), jax.ShapeDtypeStruct((T, D), BF16),
                            jax.ShapeDtypeStruct((T, D), BF16), jax.ShapeDtypeStruct((1, D), F32),
                            jax.ShapeDtypeStruct((1, D), F32)]),
        compiler_params=_cp(48, 1),
    )(*_hbm(x3, x3b, p, w_pi4, w_pg, g, b, target))


def _mix_bwd_b(dy, r2, g, w_mo, g_conv, g_ssm, y_conv, ga, gb, s, su, dvec, w_glu4, wc_re4, wc_im4, tm, comm=()):
    T = dy.shape[0]

    def body(dy_ref, r_ref, g_ref, wmo_ref, gc_ref, gs_ref, yc_ref, ga_ref, gb_ref, s_ref, su_ref, d_ref,
             wg_ref, wcr_ref, wci_ref,
             dres_ref, dmix_ref, dgl_ref, dsb_ref, dud_ref, gsr_ref, gsi_ref, dyc_ref, dp_ref,
             dg_ref, db_ref, dd_ref):
        i = pl.program_id(0)
        dyv = dy_ref[...]
        dr, xhat = _ln_bwd(dyv, r_ref[...], g_ref[...])
        dmix = dr.astype(BF16)
        dmerged = _nt(dmix, wmo_ref[...])
        sc, ss, sgb = (_sig(gc_ref[...].astype(F32)), _sig(gs_ref[...].astype(F32)),
                       _sig(gb_ref[...].astype(F32)))
        gav = ga_ref[...].astype(F32)
        yssm = gav * sgb
        dgc = dmerged * yc_ref[...].astype(F32) * sc * (1.0 - sc)
        dgss = dmerged * yssm * ss * (1.0 - ss)
        dyssm = dmerged * ss
        dgl = jnp.concatenate([dyssm * sgb, dyssm * gav * sgb * (1.0 - sgb)], axis=1).astype(BF16)
        dsg = (_nt(dgl[:, 0:512], wg_ref[0]) + _nt(dgl[:, 512:1024], wg_ref[1])
               + _nt(dgl[:, 1024:1536], wg_ref[2]) + _nt(dgl[:, 1536:2048], wg_ref[3]))
        sv = s_ref[...]
        _, th = _gelu(sv)
        dgelu = 0.5 * (1.0 + th) + 0.5 * sv * (1.0 - th * th) * GELU_C * (1.0 + 3.0 * 0.044715 * sv * sv)
        ds = dsg * dgelu
        dsb = ds.astype(BF16)
        pg, pb, pd = _rowsum(dyv * xhat), _rowsum(dyv), _rowsum(ds * su_ref[...])

        @pl.when(i == 0)
        def _():
            dg_ref[...] = pg
            db_ref[...] = pb
            dd_ref[...] = pd

        @pl.when(i > 0)
        def _():
            dg_ref[...] += pg
            db_ref[...] += pb
            dd_ref[...] += pd

        dres_ref[...] = ALPHA * dr
        dmix_ref[...] = dmix
        dgl_ref[...] = dgl
        dsb_ref[...] = dsb
        dud_ref[...] = ds * d_ref[...]
        for J in range(4):
            gsr_ref[:, 512 * J:512 * (J + 1)] = _nt(dsb[:, 128 * J:128 * (J + 1)], wcr_ref[J]).astype(BF16)
            gsi_ref[:, 512 * J:512 * (J + 1)] = _nt(dsb[:, 128 * J:128 * (J + 1)], wci_ref[J]).astype(BF16)
        dyc_ref[...] = (dmerged * sc).astype(BF16)
        dp_ref[:, 0:D] = dgc.astype(BF16)
        dp_ref[:, D:2 * D] = dgss.astype(BF16)

    def tok(n):
        return pl.BlockSpec((tm, n), lambda i: (i, 0))

    def full(shape):
        return pl.BlockSpec(shape, lambda i: (0,) * len(shape))

    return _pcall(
        body, name="mix_bwd_b", grid=(T // tm,),
        in_specs=[tok(D), tok(D), full((1, D)), full((D, D)), tok(D), tok(D), tok(D), tok(D), tok(D),
                  tok(SSM), tok(SSM), full((1, SSM)), full((4, SSM, 512)), full((4, 512, 128)), full((4, 512, 128))],
        out_specs=[tok(D), tok(D), tok(2 * D), tok(SSM), tok(SSM), tok(LANES), tok(LANES), tok(D),
                   pl.BlockSpec((tm, 2 * D), lambda i: (i, 1)), full((1, D)), full((1, D)), full((1, SSM))],
        out_shape=[jax.ShapeDtypeStruct((T, D), F32), jax.ShapeDtypeStruct((T, D), BF16),
                   jax.ShapeDtypeStruct((T, 2 * D), BF16), jax.ShapeDtypeStruct((T, SSM), BF16),
                   jax.ShapeDtypeStruct((T, SSM), F32), jax.ShapeDtypeStruct((T, LANES), BF16),
                   jax.ShapeDtypeStruct((T, LANES), BF16), jax.ShapeDtypeStruct((T, D), BF16),
                   jax.ShapeDtypeStruct((T, 4 * D), BF16), jax.ShapeDtypeStruct((1, D), F32),
                   jax.ShapeDtypeStruct((1, D), F32), jax.ShapeDtypeStruct((1, SSM), F32)],
        vmem_mb=56, comm=comm,
        operands=(dy, r2, g, w_mo, g_conv, g_ssm, y_conv, ga, gb, s, su, dvec, w_glu4, wc_re4, wc_im4))


def _s5_scan_bwd(gs_re, gs_im, st_re, st_im, su_b, ds_b, wb_re, wb_im, a_re, a_im, comm=()):
    T = su_b.shape[0]
    W = SCAN_W
    R = SCAN_R

    def body(gr_ref, gi_ref, sr_ref, si_ref, su_ref, ds_ref, wbr_ref, wbi_ref, ar_ref, ai_ref,
             dsu_ref, dwbr_ref, dwbi_ref, dwcr_ref, dwci_ref, dar_ref, dai_ref, gre, gim):
        j = pl.program_id(0)
        zero = jnp.zeros((8, W), F32)
        for buf in (gre, gim):
            buf[pl.ds(T + 8, 8), :] = zero
        _scan_rows(gre, gim, ar_ref[...], ai_ref[...], T, True,
                   lambda t0: (gr_ref[pl.ds(t0, R), :].astype(F32), gi_ref[pl.ds(t0, R), :].astype(F32)))
        grb = gre[pl.ds(8, T), :].astype(BF16)
        gib = gim[pl.ds(8, T), :].astype(BF16)
        part = _nt(grb, wbr_ref[...]) + _nt(gib, wbi_ref[...])

        @pl.when(j % SCAN_PER == 0)
        def _():
            dsu_ref[...] = part

        @pl.when(j % SCAN_PER > 0)
        def _():
            dsu_ref[...] += part

        su = su_ref[...]
        dwbr_ref[...] = _tn(su, grb)
        dwbi_ref[...] = _tn(su, gib)
        dsv = ds_ref[...]
        dwcr_ref[...] = _tn(sr_ref[...], dsv)
        dwci_ref[...] = _tn(si_ref[...], dsv)
        dar = jnp.zeros((1, W), F32)
        dai = jnp.zeros((1, W), F32)
        for c in range(T // R):
            xr = sr_ref[pl.ds(c * R, R), :].astype(F32)
            xi = si_ref[pl.ds(c * R, R), :].astype(F32)
            g1r = gre[pl.ds(c * R + 9, R), :]
            g1i = gim[pl.ds(c * R + 9, R), :]
            dar = dar + _rowsum(g1r * xr + g1i * xi)
            dai = dai + _rowsum(g1i * xr - g1r * xi)
        dar_ref[...] = dar
        dai_ref[...] = dai

    lane, col, wb, wc, vec = _scan_specs(T)
    return _pcall(
        body, name="s5_scan_bwd", grid=(LANES // W,),
        in_specs=[lane, lane, lane, lane, col, col, wb, wb, vec, vec],
        out_specs=[col, wb, wb, wc, wc, vec, vec],
        out_shape=[jax.ShapeDtypeStruct((T, SSM), F32),
                   jax.ShapeDtypeStruct((LANES // W, 128, W), F32), jax.ShapeDtypeStruct((LANES // W, 128, W), F32),
                   jax.ShapeDtypeStruct((LANES // W, W, 128), F32), jax.ShapeDtypeStruct((LANES // W, W, 128), F32),
                   jax.ShapeDtypeStruct((1, LANES), F32), jax.ShapeDtypeStruct((1, LANES), F32)],
        scratch=[pltpu.VMEM((T + 16, W), F32)] * 2, vmem_mb=56, comm=comm,
        operands=(gs_re, gs_im, st_re, st_im, su_b, ds_b, wb_re, wb_im, a_re, a_im))


def _mix_bwd_a(dyc_b, w_co4, pc, z_b, conv_w, dsu_ssm, du_dir, dproj, dres, w_mix4, tm, comm=()):
    T = dres.shape[0]
    nt = T // tm

    def body(dyc_ref, wco_ref, pc_ref, halo_ref, z_ref, cw_ref, dsu_ref, dud_ref, dpin_ref, dres_ref, w_ref,
             dp_ref, dx_ref, dcw_ref, dcb_ref, dzbuf, qbuf):
        i = pl.program_id(0)
        ii = nt - 1 - i

        @pl.when(i == 0)
        def _():
            dzbuf[pl.ds(tm, 8), :] = jnp.zeros((8, CONV), F32)

        dyc = dyc_ref[...]
        dyin = (_nt(dyc[:, 0:256], wco_ref[0]) + _nt(dyc[:, 256:512], wco_ref[1])
                + _nt(dyc[:, 512:768], wco_ref[2]) + _nt(dyc[:, 768:1024], wco_ref[3]))
        cbv = pc_ref[:, 0:CONV].astype(F32)
        ccv = pc_ref[:, CONV:2 * CONV].astype(F32)
        chv = pc_ref[:, 2 * CONV:3 * CONV].astype(F32)
        dcbv = dyin * z_ref[...].astype(F32)
        dz = dyin * cbv
        dzbuf[pl.ds(0, tm), :] = dz
        cw = cw_ref[...]
        dq = cw[2:3] * dz + cw[1:2] * dzbuf[pl.ds(1, tm), :] + cw[0:1] * dzbuf[pl.ds(2, tm), :]
        dzbuf[pl.ds(tm, 8), :] = dz[0:8]
        q = ccv * chv
        hq = halo_ref[:, CONV:2 * CONV].astype(F32) * halo_ref[:, 2 * CONV:3 * CONV].astype(F32)
        qbuf[pl.ds(0, 8), :] = jnp.where(ii > 0, hq, jnp.zeros_like(hq))
        qbuf[pl.ds(8, tm), :] = q
        pw = jnp.concatenate([_rowsum(dz * qbuf[pl.ds(6, tm), :]), _rowsum(dz * qbuf[pl.ds(7, tm), :]),
                              _rowsum(dz * q), jnp.zeros((5, CONV), F32)], axis=0)
        pbias = _rowsum(dz)

        @pl.when(i == 0)
        def _():
            dcw_ref[...] = pw
            dcb_ref[...] = pbias

        @pl.when(i > 0)
        def _():
            dcw_ref[...] += pw
            dcb_ref[...] += pbias

        dp0 = jnp.concatenate([dcbv, dq * chv], axis=1).astype(BF16)
        dp1 = jnp.concatenate([dq * ccv, dsu_ref[...] + dud_ref[...]], axis=1).astype(BF16)
        dp_ref[:, 0:D] = dp0
        dp_ref[:, D:2 * D] = dp1
        dx_ref[...] = (dres_ref[...] + _nt(dp0, w_ref[0]) + _nt(dp1, w_ref[1])
                       + _nt(dpin_ref[:, 0:D], w_ref[2]) + _nt(dpin_ref[:, D:2 * D], w_ref[3]))

    def tok(n):
        return pl.BlockSpec((tm, n), lambda i: (nt - 1 - i, 0))

    def full(shape):
        return pl.BlockSpec(shape, lambda i: (0,) * len(shape))

    halo = pl.BlockSpec((8, 3 * CONV), lambda i: (jnp.maximum((nt - 1 - i) * (tm // 8) - 1, 0), 0))
    return _pcall(
        body, name="mix_bwd_a", grid=(nt,),
        in_specs=[tok(D), full((4, CONV, 256)), tok(3 * CONV), halo, tok(CONV), full((3, CONV)),
                  tok(SSM), tok(SSM), pl.BlockSpec((tm, 2 * D), lambda i: (nt - 1 - i, 1)), tok(D),
                  full((4, D, D))],
        out_specs=[pl.BlockSpec((tm, 2 * D), lambda i: (nt - 1 - i, 0)), tok(D), full((8, CONV)), full((1, CONV))],
        out_shape=[jax.ShapeDtypeStruct((T, 4 * D), BF16), jax.ShapeDtypeStruct((T, D), F32),
                   jax.ShapeDtypeStruct((8, CONV), F32), jax.ShapeDtypeStruct((1, CONV), F32)],
        scratch=[pltpu.VMEM((tm + 8, CONV), F32), pltpu.VMEM((tm + 8, CONV), F32)],
        aliases={8: 0}, vmem_mb=56, comm=comm,
        operands=(dyc_b, w_co4, pc, pc, z_b, conv_w, dsu_ssm, du_dir, dproj, dres, w_mix4))


def _zoh(lam_re, lam_im, log_step, b_re, b_im):
    dt = jnp.exp(log_step)[:, None]
    mag = jnp.exp(lam_re * dt)
    abr, abi = mag * jnp.cos(lam_im * dt), mag * jnp.sin(lam_im * dt)
    nr, ni = abr - 1.0, abi
    den = lam_re * lam_re + lam_im * lam_im
    cr = (nr * lam_re + ni * lam_im) / den
    ci = (ni * lam_re - nr * lam_im) / den
    bbr = cr[..., None] * b_re - ci[..., None] * b_im
    bbi = cr[..., None] * b_im + ci[..., None] * b_re
    return abr, abi, bbr, bbi


_WB_MASK = (np.arange(8)[None, :, None]
            == SCAN_GR * np.arange(SCAN_PER)[:, None, None] + np.arange(SCAN_GR)[None, None, :]).astype(np.float32)
_EYE8 = np.eye(8, dtype=np.float32)


def _wb_blocks(bb):
    bt = bb.transpose(0, 2, 1).reshape(4, 1, 8, 16, 1, STATE)
    full = bt * _WB_MASK[None, :, :, None, :, None]
    return full.reshape(LANES // SCAN_W, 128, SCAN_W).astype(BF16)


def _wc_blocks(cc):
    ct = cc.transpose(0, 2, 1).reshape(4, 8, STATE, 1, 16)
    full = ct * _EYE8[None, :, None, :, None]
    return full.reshape(4, 512, 128).astype(BF16)


def _wb_diag(dwb):
    d6 = dwb.reshape(4, SCAN_PER, 8, 16, SCAN_GR, STATE) * _WB_MASK[None, :, :, None, :, None]
    return d6.sum(axis=(1, 4)).reshape(GROUPS, 16, STATE).transpose(0, 2, 1)


def _wc_diag(dwc):
    mask = _WB_MASK.transpose(0, 2, 1)
    d6 = dwc.reshape(4, SCAN_PER, SCAN_GR, STATE, 8, 16) * mask[None, :, :, None, :, None]
    return d6.sum(axis=4).reshape(GROUPS, STATE, 16).transpose(0, 2, 1)


def _where():
    x, y, c = lax.axis_index("x"), lax.axis_index("y"), lax.axis_index("c")
    return x, y, c, 2 * x + y


def _chip_dev(k, c):
    return (k // 2, k % 2, c)


def _slot_cast(meidx, w, dtype, name, token=()):
    R, C = w.shape
    tr = _row_tile(R)

    def body(m_ref, w_ref, *rest):
        rest[-1][...] = w_ref[...].astype(dtype)

    gs = pltpu.PrefetchScalarGridSpec(
        num_scalar_prefetch=1, grid=(R // tr,),
        in_specs=[pl.BlockSpec((tr, C), lambda i, m: (i, 0))] + [pl.BlockSpec((8, 128), lambda i, m: (0, 0))] * len(token),
        out_specs=pl.BlockSpec((None, tr, C), lambda i, m: (m[0], i, 0)))
    return pl.pallas_call(
        body, name=name, grid_spec=gs, out_shape=_hbm_out(jax.ShapeDtypeStruct((4, R, C), dtype)),
        compiler_params=_cp(32, 1),
    )(meidx, *_hbm(w), *token)


def _gather_ici_payload(bufs):
    def copies(ins, lnd, ss, rs):
        x, y, c, me = _where()
        cps = []
        for w, b in enumerate(bufs):
            h = b.shape[1] // 2
            mine = lnd[w].at[me, pl.ds(c * h, h)]
            for s in range(3):
                k = (me + 1 + s) % 4
                cps.append(pltpu.make_async_remote_copy(
                    src_ref=mine, dst_ref=mine, send_sem=ss.at[3 * w + s], recv_sem=rs.at[3 * w + s],
                    device_id=_chip_dev(k, c), device_id_type=MESH))
        return cps

    p = _sym_payload([], [jax.ShapeDtypeStruct(b.shape, b.dtype) for b in bufs], copies, 3 * len(bufs))
    p.lands = list(bufs)
    return p


def _gather_pass_payload(bufs):
    def copies(ins, outs, ss, rs):
        x, y, c, me = _where()
        cps = []
        for w, b in enumerate(bufs):
            h = b.shape[1] // 2
            for s in range(3):
                j = (me + 1 + s) % 4
                cps.append(pltpu.make_async_remote_copy(
                    src_ref=ins[w].at[j, pl.ds(c * h, h)], dst_ref=outs[w].at[j, pl.ds(c * h, h)],
                    send_sem=ss.at[3 * w + s], recv_sem=rs.at[3 * w + s], device_id=(x, y, 1 - c),
                    device_id_type=MESH))
        return cps

    p = _sym_payload(bufs, [jax.ShapeDtypeStruct(b.shape, b.dtype) for b in bufs], copies, 3 * len(bufs))
    p.aliases = {w: w for w in range(len(bufs))}
    return p


def _gather_payload(bufs):
    n = len(bufs)

    def half(ref, w, k, cc):
        h = bufs[w].shape[1] // 2
        return ref.at[k, pl.ds(cc * h, h)]

    def ici(ins, outs, sems, w, s):
        x, y, c, me = _where()
        k = (me + 1 + s) % 4
        return pltpu.make_async_remote_copy(
            src_ref=half(ins[w], w, me, c), dst_ref=half(outs[w], w, me, c), send_sem=sems[0].at[3 * w + s],
            recv_sem=sems[1].at[3 * w + s], device_id=_chip_dev(k, c), device_id_type=MESH)

    def landed(outs, sems, w, s):
        x, y, c, me = _where()
        j = (me + 3 - s) % 4
        return pltpu.make_async_remote_copy(
            src_ref=half(outs[w], w, j, c), dst_ref=half(outs[w], w, j, c), send_sem=sems[0].at[3 * w + s],
            recv_sem=sems[1].at[3 * w + s], device_id=(x, y, 1 - c), device_id_type=MESH)

    def passed(outs, sems, w, s, cc):
        x, y, c, me = _where()
        j = (me + 3 - s) % 4
        return pltpu.make_async_remote_copy(
            src_ref=half(outs[w], w, j, cc), dst_ref=half(outs[w], w, j, cc), send_sem=sems[2].at[3 * w + s],
            recv_sem=sems[3].at[3 * w + s], device_id=(x, y, 1 - c), device_id_type=MESH)

    pairs = [(w, s) for w in range(n) for s in range(3)]

    def start(ins, outs, sems):
        for w, s in pairs:
            ici(ins, outs, sems, w, s).start()

    def finish(ins, outs, sems):
        _, _, c, _ = _where()
        for w, s in pairs:
            landed(outs, sems, w, s).wait_recv()
            passed(outs, sems, w, s, c).start()
        for w, s in pairs:
            passed(outs, sems, w, s, 1 - c).wait_recv()
        for w, s in pairs:
            ici(ins, outs, sems, w, s).wait_send()
            passed(outs, sems, w, s, c).wait_send()

    return _Payload(bufs, [jax.ShapeDtypeStruct(b.shape, b.dtype) for b in bufs], {w: w for w in range(n)},
                    [pltpu.SemaphoreType.DMA((3 * n,))] * 4, start, finish)


def _sym_payload(operands, outs, copies, n_copies):
    def start(ins, outs_, sems):
        for cp in copies(ins, outs_, sems[0], sems[1]):
            cp.start()

    def finish(ins, outs_, sems):
        for cp in copies(ins, outs_, sems[0], sems[1]):
            cp.wait()

    p = _Payload(operands, outs, {}, [pltpu.SemaphoreType.DMA((n_copies,))] * 2, start, finish)
    p.copies, p.n_copies = copies, n_copies
    return p


def _swap_payload(g4s):
    def copies(ins, outs, ss, rs):
        x, y, c, me = _where()
        cps = []
        for w, g in enumerate(g4s):
            h = g.shape[1] // 2
            cps.append(pltpu.make_async_remote_copy(
                src_ref=ins[w].at[:, pl.ds((1 - c) * h, h)], dst_ref=outs[w], send_sem=ss.at[w],
                recv_sem=rs.at[w], device_id=(x, y, 1 - c), device_id_type=MESH))
        return cps

    outs = [jax.ShapeDtypeStruct((4, g.shape[1] // 2, g.shape[2]), g.dtype) for g in g4s]
    return _sym_payload(g4s, outs, copies, len(g4s))


def _exchange_payload(pbs):
    def copies(ins, outs, ss, rs):
        x, y, c, me = _where()
        cps = []
        for w in range(len(pbs)):
            for s in range(3):
                k = (me + 1 + s) % 4
                cps.append(pltpu.make_async_remote_copy(
                    src_ref=ins[w].at[k], dst_ref=outs[w].at[2 - s], send_sem=ss.at[3 * w + s],
                    recv_sem=rs.at[3 * w + s], device_id=_chip_dev(k, c), device_id_type=MESH))
        return cps

    outs = [jax.ShapeDtypeStruct((3,) + p.shape[1:], p.dtype) for p in pbs]
    return _sym_payload(pbs, outs, copies, 3 * len(pbs))


HBM_REF = pl.BlockSpec(memory_space=pltpu.HBM)
SEM_REF = pl.BlockSpec(memory_space=pltpu.SEMAPHORE)
DATAFLOW = pltpu.SideEffectType.DATAFLOW_SIDE_EFFECTING


class _SemList:
    def __init__(self, refs):
        self.refs = refs

    @property
    def at(self):
        return self.refs


def _split_start(p, name):
    n_in, n_out, nc = len(p.operands), len(p.outs), p.n_copies
    lands = getattr(p, "lands", None) or [lax.empty(s.shape, s.dtype) for s in p.outs]

    def body(*refs):
        ins, lnd = refs[:n_in], refs[n_in:n_in + n_out]
        sems = refs[n_in + n_out:n_in + n_out + 2 * nc]
        for cp in p.copies(ins, lnd, _SemList(sems[:nc]), _SemList(sems[nc:])):
            cp.start()
        refs[-1][...] = jnp.zeros((8, 128), F32)

    res = pl.pallas_call(
        body, name=name,
        in_specs=[HBM_REF] * (n_in + n_out),
        out_specs=[SEM_REF] * (2 * nc) + [HBM_REF] * (n_in + n_out) + [VMEM_FULL],
        out_shape=([pltpu.SemaphoreType.DMA(())] * (2 * nc) + _hbm_out(p.operands) + _hbm_out(lands)
                   + [jax.ShapeDtypeStruct((8, 128), F32)]),
        input_output_aliases={i: 2 * nc + i for i in range(n_in + n_out)},
        compiler_params=pltpu.CompilerParams(has_side_effects=DATAFLOW),
    )(*_hbm(*p.operands, *lands))
    k = 2 * nc
    return list(res[:k]), list(res[k:k + n_in]), list(res[k + n_in:k + n_in + n_out]), res[-1]


def _split_wait(p, handle, after, name):
    sems, srcs, lands, _ = handle
    n_in, n_out, nc = len(srcs), len(lands), p.n_copies

    def body(*refs):
        ins, lnd = refs[:n_in], refs[n_in:n_in + n_out]
        sm = refs[n_in + n_out:n_in + n_out + 2 * nc]
        for cp in p.copies(ins, lnd, _SemList(sm[:nc]), _SemList(sm[nc:])):
            cp.wait_send()
            cp.wait_recv()

    res = pl.pallas_call(
        body, name=name,
        in_specs=[HBM_REF] * (n_in + n_out) + [SEM_REF] * (2 * nc) + [ANY] * len(after),
        out_specs=[HBM_REF] * (n_in + n_out), out_shape=_hbm_out(srcs) + _hbm_out(lands),
        input_output_aliases={i: i for i in range(n_in + n_out)},
        compiler_params=pltpu.CompilerParams(has_side_effects=DATAFLOW),
    )(*srcs, *lands, *sems, *after)
    return list(res[:n_in]), list(res[n_in:])


def _join_payload(halves):
    def copies(ins, outs, ss, rs):
        x, y, c, me = _where()
        return [pltpu.make_async_remote_copy(
            src_ref=ins[w], dst_ref=outs[w], send_sem=ss.at[w], recv_sem=rs.at[w],
            device_id=(x, y, 1 - c), device_id_type=MESH) for w in range(len(halves))]

    outs = [jax.ShapeDtypeStruct(a.shape, a.dtype) for a in halves]
    return _sym_payload(halves, outs, copies, len(halves))


def _allgather_payload(v):
    def copies(ins, outs, ss, rs):
        x, y, c, me = _where()
        lin = 4 * x + 2 * y + c
        cps = []
        for o in range(1, 8):
            t = (lin + o) % 8
            cps.append(pltpu.make_async_remote_copy(
                src_ref=ins[0], dst_ref=outs[0].at[lin], send_sem=ss.at[o - 1], recv_sem=rs.at[o - 1],
                device_id=(t // 4, (t // 2) % 2, t % 2), device_id_type=MESH))
        return cps

    p = _sym_payload([v], [jax.ShapeDtypeStruct((8,) + v.shape, v.dtype)], copies, 7)
    x, y, c, _ = _where()
    p.lands = [lax.dynamic_update_slice(jnp.zeros((8,) + v.shape, v.dtype), v[None], (4 * x + 2 * y + c, 0, 0))]
    return p


def _sum8(buf, token):
    _, P, C = buf.shape

    def body(b_ref, t_ref, o_ref):
        acc = b_ref[0]
        for d in range(1, 8):
            acc = acc + b_ref[d]
        o_ref[...] = acc

    return pl.pallas_call(
        body, name="sum8", in_specs=[VMEM_FULL, VMEM_FULL], out_specs=VMEM_FULL,
        out_shape=jax.ShapeDtypeStruct((P, C), F32),
        compiler_params=pltpu.CompilerParams(vmem_limit_bytes=32 << 20),
    )(buf, token)


def _row_tile(h):
    for t in (256, 176, 128, 64, 32, 16, 8):
        if h % t == 0:
            return t
    raise ValueError(h)


def _pair_sum(cmidx, g4, got, name):
    _, R, C = g4.shape
    h = R // 2
    th = _row_tile(h)

    def body(cm_ref, a_ref, b_ref, o_ref, ob_ref):
        sm = a_ref[...] + b_ref[...]
        ob_ref[...] = sm.astype(BF16)

        @pl.when(pl.program_id(1) == cm_ref[1])
        def _():
            o_ref[...] = sm

    blk = pl.BlockSpec((None, th, C), lambda i, k, cm: (k, i, 0))
    gs = pltpu.PrefetchScalarGridSpec(
        num_scalar_prefetch=1, grid=(h // th, 4),
        in_specs=[pl.BlockSpec((None, None, th, C), lambda i, k, cm: (k, cm[0], i, 0)), blk],
        out_specs=[pl.BlockSpec((th, C), lambda i, k, cm: (i, 0)), blk])
    return pl.pallas_call(
        body, name=name, grid_spec=gs,
        out_shape=_hbm_out([jax.ShapeDtypeStruct((h, C), F32), jax.ShapeDtypeStruct((4, h, C), BF16)]),
        compiler_params=_cp(32, 2),
    )(cmidx, *_hbm(g4.reshape(4, 2, h, C), got))


def _chip_sum(own, got, name):
    h, C = own.shape
    th = _row_tile(h)

    def body(a_ref, b_ref, o_ref):
        o_ref[...] = ((a_ref[...] + b_ref[0].astype(F32)) + b_ref[1].astype(F32)) + b_ref[2].astype(F32)

    return pl.pallas_call(
        body, name=name, grid=(h // th,),
        in_specs=[pl.BlockSpec((th, C), lambda i: (i, 0)), pl.BlockSpec((3, th, C), lambda i: (0, i, 0))],
        out_specs=pl.BlockSpec((th, C), lambda i: (i, 0)),
        out_shape=_hbm_out(jax.ShapeDtypeStruct((h, C), F32)),
        compiler_params=_cp(32, 1),
    )(*_hbm(own, got))


def _adamw_math(w, g, m, v):
    m2 = B1 * m + (1.0 - B1) * g
    v2 = B2 * v + (1.0 - B2) * (g * g)
    m_hat = m2 / (1.0 - B1 ** STEP)
    v_hat = v2 / (1.0 - B2 ** STEP)
    delta = -LR * (m_hat / (jnp.sqrt(v_hat) + EPS) + WD * w)
    return delta, m2, v2


def _adamw_pair(cidx, w, mine, theirs, m, v, token, name):
    R, C = w.shape
    h = R // 2
    tr = _row_tile(h)
    nh = h // tr

    def body(c_ref, w_ref, a_ref, b_ref, m_ref, v_ref, t_ref, g_ref, d_ref, mo_ref, vo_ref):
        own = (pl.program_id(0) // nh) == c_ref[0]
        g = jnp.where(own, a_ref[...], b_ref[...])
        d, m2, v2 = _adamw_math(w_ref[...], g, m_ref[...], v_ref[...])
        g_ref[...] = g
        d_ref[...] = d
        mo_ref[...] = m2
        vo_ref[...] = v2

    blk = pl.BlockSpec((tr, C), lambda i, c: (i, 0))
    mine_blk = pl.BlockSpec((tr, C), lambda i, c: (jnp.clip(i - c[0] * nh, 0, nh - 1), 0))
    theirs_blk = pl.BlockSpec((tr, C), lambda i, c: (jnp.clip(i - (1 - c[0]) * nh, 0, nh - 1), 0))
    gs = pltpu.PrefetchScalarGridSpec(
        num_scalar_prefetch=1, grid=(R // tr,),
        in_specs=[blk, mine_blk, theirs_blk, blk, blk, pl.BlockSpec((8, 128), lambda i, c: (0, 0))],
        out_specs=[blk] * 4)
    return pl.pallas_call(
        body, name=name, grid_spec=gs, out_shape=_hbm_out([jax.ShapeDtypeStruct((R, C), F32)] * 4),
        compiler_params=_cp(32, 1),
    )(cidx, *_hbm(w, mine, theirs, m, v), token)


def _adamw(w, g, m, v, name):
    R, C = w.shape
    tr = _row_tile(R)

    def body(w_ref, g_ref, m_ref, v_ref, d_ref, mo_ref, vo_ref):
        d, m2, v2 = _adamw_math(w_ref[...], g_ref[...], m_ref[...], v_ref[...])
        d_ref[...] = d
        mo_ref[...] = m2
        vo_ref[...] = v2

    blk = pl.BlockSpec((tr, C), lambda i: (i, 0))
    return pl.pallas_call(
        body, name=name, grid=(R // tr,), in_specs=[blk] * 4, out_specs=[blk] * 3,
        out_shape=_hbm_out([jax.ShapeDtypeStruct((R, C), F32)] * 3),
        compiler_params=_cp(32, 1),
    )(*_hbm(w, g, m, v))


def _pack(arrs):
    flat = jnp.concatenate([a.reshape(-1).astype(F32) for a in arrs])
    rows = -(-flat.shape[0] // 1024)
    rows = -(-rows // 8) * 8
    return jnp.pad(flat, (0, rows * 1024 - flat.shape[0])).reshape(rows, 1024)


def _unpack(packed, shapes):
    flat = packed.reshape(-1)
    out, off = [], 0
    for s in shapes:
        n = math.prod(s)
        out.append(flat[off:off + n].reshape(s))
        off += n
    return out


BIG = ["ffn1_w_in", "ffn1_w_out", "mix_w_in", "conv_w_out", "ssm_w_glu", "mix_w_out",
       "ffn2_w_in", "ffn2_w_out", "ple_w_in", "ple_w_gate"]
SMALL = ["ln1_g", "ln1_b", "conv_w", "conv_b", "ssm_lam_re", "ssm_lam_im", "ssm_log_step", "ssm_b_re", "ssm_b_im",
         "ssm_c_re", "ssm_c_im", "ssm_d", "ln2_g", "ln2_b", "ln3_g", "ln3_b", "ln4_g", "ln4_b"]
WEIGHTS = ["ffn1_w_in", "ffn1_w_out", "ln1_g", "ln1_b", "mix_w_in", "conv_w", "conv_b", "conv_w_out",
           "ssm_lam_re", "ssm_lam_im", "ssm_log_step", "ssm_b_re", "ssm_b_im", "ssm_c_re", "ssm_c_im", "ssm_d",
           "ssm_w_glu", "mix_w_out", "ln2_g", "ln2_b", "ffn2_w_in", "ffn2_w_out", "ln3_g", "ln3_b",
           "ple_w_in", "ple_w_gate", "ln4_g", "ln4_b"]


class _NoComm:
    def __init__(self, W):
        self.W, self.G, self.raw, self.done = dict(W), {}, None, {}

    def carry(self, name):
        return ()

    def landed(self, name, got):
        pass

    def grad(self, name, g4):
        self.G[name] = g4

    def small(self, raw):
        self.raw = raw


def _s5_operands(sp):
    abr, abi, bbr, bbi = _zoh(sp["ssm_lam_re"], sp["ssm_lam_im"], sp["ssm_log_step"], sp["ssm_b_re"], sp["ssm_b_im"])
    return (_wb_blocks(bbr), _wb_blocks(bbi), _wc_blocks(sp["ssm_c_re"]), _wc_blocks(-sp["ssm_c_im"]),
            abr.reshape(1, LANES), abi.reshape(1, LANES), sp["ssm_d"].reshape(1, SSM))


def _local_step(x, p, target, sp, sched, tm_ffn, tm_mix, ops=None):
    W = sched.W
    wb_re, wb_im, wc_re4, wc_im4, a_re, a_im, dvec = ops if ops is not None else _s5_operands(sp)

    def run(fn, name, *args, **kw):
        outs, got = fn(*args, comm=sched.carry(name), **kw)
        sched.landed(name, got)
        sched.done[name] = outs[0]
        return outs

    def dw(name, wname, a, b, tk, tn, shape4, shard_cols=None, interleaved=False):
        out, got = _mm_tn(a, b, tk, tn, name, shard_cols=shard_cols, interleaved=interleaved,
                          comm=sched.carry(name))
        sched.landed(name, got)
        sched.done[name] = out
        sched.grad(wname, out.reshape(shape4))

    xb = x.astype(BF16)
    h1, r1, x1, x1b = run(_ffn_fwd, "ffn1_fwd", x, xb, W["ffn1_w_in"], W["ffn1_w_out"].reshape(2, FFH, D),
                          sp["ln1_g"], sp["ln1_b"], tm_ffn, "ffn1_fwd")
    conv_w = W["conv_w"][:, 0:3, :].transpose(1, 0, 2).reshape(3, CONV)
    pc, z_b, yin_b, su, su_b, g_conv, g_ssm, y_conv = run(
        _mix_fwd_a, "mix_fwd_a", x1b, W["mix_w_in"], conv_w, sp["conv_b"], W["conv_w_out"], tm_mix)
    st_re, st_im = run(_s5_scan_fwd, "s5_scan_fwd", su_b, wb_re, wb_im, a_re, a_im)
    w_mo = W["mix_w_out"].reshape(D, D)
    s, sg_b, ga, gb, merged_b, r2, x2, x2b = run(
        _mix_fwd_b, "mix_fwd_b", st_re, st_im, wc_re4, wc_im4, su, dvec, W["ssm_w_glu"], g_conv, g_ssm, y_conv,
        w_mo, x1, sp["ln2_g"], sp["ln2_b"], tm_mix)
    w2o2 = W["ffn2_w_out"].reshape(2, FFH, D)
    h2, r3, x3, x3b = run(_ffn_fwd, "ffn2_fwd", x2, x2b, W["ffn2_w_in"], w2o2, sp["ln3_g"], sp["ln3_b"], tm_ffn,
                          "ffn2_fwd")
    loss_part, dx3, p_b, dpw_b, dgt_b, dg4, db4 = _ple_loss(
        x3, x3b, p, W["ple_w_in"], W["ple_w_gate"].reshape(D, D), sp["ln4_g"], sp["ln4_b"], target, tm_mix)

    dw("dw_ple_gate", "ple_w_gate", x3b, dgt_b, 512, 1024, (4, 256, D))
    dw("dw_ple_in", "ple_w_in", p_b, dpw_b, 256, 256, (4, 256, 256), shard_cols=256)
    dx2, dh2, a2_b, df2_b, dg3, db3 = run(_ffn_bwd, "ffn2_bwd", dx3, r3, sp["ln3_g"], h2, W["ffn2_w_in"], w2o2,
                                          tm_mix, "ffn2_bwd")
    dw("dw_ffn2_in", "ffn2_w_in", x2b, dh2, 512, FFH, (4, D, FFH), shard_cols=FFH, interleaved=True)
    dw("dw_ffn2_out", "ffn2_w_out", a2_b, df2_b, FFH, 1024, (4, FF // 4, D))
    (dres, dmix_b, dgl_b, ds_b, du_dir, gs_re, gs_im, dyc_b, dproj, dg2, db2, dd) = run(
        _mix_bwd_b, "mix_bwd_b", dx2, r2, sp["ln2_g"], w_mo, g_conv, g_ssm, y_conv, ga, gb, s, su, dvec,
        W["ssm_w_glu"], wc_re4, wc_im4, tm_mix)
    dw("dw_mix_out", "mix_w_out", merged_b, dmix_b, 512, 1024, (4, 256, D))
    dw("dw_glu", "ssm_w_glu", sg_b, dgl_b, 512, 512, (4, SSM, 512), shard_cols=512)
    dsu_ssm, dwb_re, dwb_im, dwc_re, dwc_im, da_re, da_im = run(
        _s5_scan_bwd, "s5_scan_bwd", gs_re, gs_im, st_re, st_im, su_b, ds_b, wb_re, wb_im, a_re, a_im)
    dw("dw_conv_out", "conv_w_out", yin_b, dyc_b, 512, 256, (4, CONV, 256), shard_cols=256)
    dproj, dx1, dcw8, dcb = run(_mix_bwd_a, "mix_bwd_a", dyc_b, W["conv_w_out"], pc, z_b, conv_w, dsu_ssm,
                                du_dir, dproj, dres, W["mix_w_in"], tm_mix)
    dw("dw_mix_in", "mix_w_in", x1b, dproj, 512, 1024, (4, D, D), shard_cols=1024)
    dx0, dh1, a1_b, df1_b, dg1, db1 = run(_ffn_bwd, "ffn1_bwd", dx1, r1, sp["ln1_g"], h1, W["ffn1_w_in"],
                                          W["ffn1_w_out"].reshape(2, FFH, D), tm_mix, "ffn1_bwd")
    sched.small(dict(
        ln1_g=dg1, ln1_b=db1, ln2_g=dg2, ln2_b=db2, ln3_g=dg3, ln3_b=db3, ln4_g=dg4, ln4_b=db4,
        conv_w=dcw8[0:3], conv_b=dcb,
        a_re=da_re.reshape(GROUPS, STATE), a_im=da_im.reshape(GROUPS, STATE),
        bb_re=_wb_diag(dwb_re), bb_im=_wb_diag(dwb_im),
        ssm_c_re=_wc_diag(dwc_re), ssm_c_im=-_wc_diag(dwc_im), ssm_d=dd.reshape(GROUPS, 16),
        loss=loss_part[0:1, 0]))
    dw("dw_ffn1_in", "ffn1_w_in", xb, dh1, 512, FFH, (4, D, FFH), shard_cols=FFH, interleaved=True)
    dw("dw_ffn1_out", "ffn1_w_out", a1_b, df1_b, FFH, 1024, (4, FF // 4, D))
    return loss_part[0, 0], dx0


RAW_ORDER = ["ln1_g", "ln1_b", "ln2_g", "ln2_b", "ln3_g", "ln3_b", "ln4_g", "ln4_b", "conv_w", "conv_b",
             "a_re", "a_im", "bb_re", "bb_im", "ssm_c_re", "ssm_c_im", "ssm_d", "loss"]

GATHER_FIRST = ["ffn1_w_in", "ffn1_w_out"]
GATHER_AT = {"ffn1_fwd": ["mix_w_in", "conv_w_out", "conv_w"], "mix_fwd_a": ["ssm_w_glu", "mix_w_out"],
             "s5_scan_fwd": ["ffn2_w_in"], "mix_fwd_b": ["ffn2_w_out"], "ffn2_fwd": ["ple_w_in", "ple_w_gate"]}
REDUCE_GROUP = {"ple": ["ple_w_gate", "ple_w_in"], "ffn2": ["ffn2_w_in", "ffn2_w_out"],
                "mix": ["mix_w_out", "ssm_w_glu", "conv_w_out", "mix_w_in"], "ffn1": ["ffn1_w_in", "ffn1_w_out"]}
REDUCE_AT = {"ffn2_bwd": [("swap", "ple")], "dw_ffn2_in": [("exchange", "ple")],
             "mix_bwd_b": [("swap", "ffn2"), ("join", "ple")],
             "mix_bwd_a": [("join", "ffn2")], "ffn1_bwd": [("swap", "mix")]}
BEGIN_AT = {"dw_mix_out": [("exchange", "ffn2")], "dw_ffn1_in": [("small", None), ("exchange", "mix")]}
BEHIND = {"dw_glu": [("exchange", "ffn2")], "s5_scan_bwd": [("exchange", "ffn2")]}
END_AT = {"mix_bwd_a": [("exchange", "ffn2", ["dw_mix_out", "dw_glu", "s5_scan_bwd"])]}
LAST_GROUP = "ffn1"


class _Sched:
    def __init__(self, cmidx):
        self.bufs, self.cmidx = {}, cmidx
        self.W, self.G, self.raw, self.small_buf = {}, {}, None, None
        self.got1, self.p32, self.pbf, self.got2, self.half, self.theirs = {}, {}, {}, {}, {}, {}
        self._open, self._split, self.done = [], {}, {}

    def first_begin(self, bufs):
        self.bufs.update(bufs)
        p = _gather_ici_payload([bufs[n] for n in GATHER_FIRST])
        self._first = (p, _split_start(p, "gather_first_start"))
        return self._first[1][3]

    def first_end(self, bufs, after):
        self.bufs.update(bufs)
        p, handle = self._first
        _, landed = _split_wait(p, handle, after, "gather_first_wait")
        (outs,) = _comm_call("gather_first_pass", [_gather_pass_payload(landed)])
        self.W.update(zip(GATHER_FIRST, outs))

    def _payload(self, stage, key):
        if stage == "gather":
            return _gather_payload([self.bufs[n] for n in key])
        if stage == "small":
            return _allgather_payload(_pack([self.raw[k] for k in RAW_ORDER]))
        names = REDUCE_GROUP[key]
        if stage == "swap":
            return _swap_payload([self.G[n] for n in names])
        if stage == "exchange":
            for n in names:
                self.p32[n], self.pbf[n] = _pair_sum(self.cmidx, self.G[n], self.got1[n], "pair_sum_" + n)
            return _exchange_payload([self.pbf[n] for n in names])
        for n in names:
            self.half[n] = _chip_sum(self.p32[n], self.got2[n], "chip_sum_" + n)
        return _join_payload([self.half[n] for n in names])

    def _store(self, stages, got):
        for (stage, key), outs in zip(stages, got):
            if stage == "gather":
                self.W.update(zip(key, outs))
            elif stage == "small":
                self.small_buf = outs[0]
            else:
                {"swap": self.got1, "exchange": self.got2, "join": self.theirs}[stage].update(
                    zip(REDUCE_GROUP[key], outs))

    def _standalone(self, name, stages):
        self._store(stages, _comm_call(name, [self._payload(s, k) for s, k in stages]))

    def carry(self, name):
        for stage, key, behind in END_AT.get(name, []):
            self._end(stage, key, [self.done[b] for b in behind])
        tokens = [self._begin(stage, key) for stage, key in BEGIN_AT.get(name, [])]
        tokens += [self._split[sk][1][3] for sk in BEHIND.get(name, [])]
        self._open = [("gather", GATHER_AT[name])] if name in GATHER_AT else []
        self._open += REDUCE_AT.get(name, [])
        comm = [self._payload(s, k) for s, k in self._open]
        if tokens:
            comm.append(_Payload(tokens, [], {}, [], lambda *a: None, lambda *a: None))
        return tuple(comm)

    def landed(self, name, got):
        self._store(self._open, got)

    def grad(self, name, g4):
        self.G[name] = g4

    def small(self, raw):
        self.raw = raw

    def _begin(self, stage, key):
        p = self._payload(stage, key)
        self._split[stage, key] = (p, _split_start(p, "%s_%s_start" % (stage, key)))
        return self._split[stage, key][1][3]

    def _end(self, stage, key, after):
        p, handle = self._split.pop((stage, key))
        srcs, lands = _split_wait(p, handle, after, "%s_%s_wait" % (stage, key))
        if stage == "swap":
            self.G.update(zip(REDUCE_GROUP[key], srcs))
        self._store([(stage, key)], [lands])

    def tail_begin(self):
        return self._begin("swap", LAST_GROUP)

    def tail_mid(self, after):
        self._end("swap", LAST_GROUP, after)
        token = self._begin("exchange", LAST_GROUP)
        self._end("small", None, [token])
        self._end("exchange", "mix", [token])
        self._standalone("reduce_tail_join_mix", [("join", "mix")])
        return token

    def tail_end(self, after):
        self._end("exchange", LAST_GROUP, after)
        self._standalone("reduce_tail_join", [("join", LAST_GROUP)])


def _small_grads(raw_sum, sp):
    _, vjp = jax.vjp(_zoh, sp["ssm_lam_re"], sp["ssm_lam_im"], sp["ssm_log_step"], sp["ssm_b_re"], sp["ssm_b_im"])
    d_lre, d_lim, d_ls, d_bre, d_bim = vjp((raw_sum["a_re"], raw_sum["a_im"], raw_sum["bb_re"], raw_sum["bb_im"]))
    g = {k: raw_sum[k] for k in ("ln1_g", "ln1_b", "ln2_g", "ln2_b", "ln3_g", "ln3_b", "ln4_g", "ln4_b",
                                 "conv_w", "conv_b", "ssm_c_re", "ssm_c_im", "ssm_d")}
    g.update(ssm_lam_re=d_lre, ssm_lam_im=d_lim, ssm_log_step=d_ls, ssm_b_re=d_bre, ssm_b_im=d_bim)
    return g


def kernel(x, p, ffn1_w_in, ffn1_w_out, ln1_g, ln1_b, mix_w_in, conv_w, conv_b, conv_w_out, ssm_lam_re, ssm_lam_im, ssm_log_step, ssm_b_re, ssm_b_im, ssm_c_re, ssm_c_im, ssm_d, ssm_w_glu, mix_w_out, ln2_g, ln2_b, ffn2_w_in, ffn2_w_out, ln3_g, ln3_b, ple_w_in, ple_w_gate, ln4_g, ln4_b, loss_target, m_ffn1_w_in, m_ffn1_w_out, m_ln1_g, m_ln1_b, m_mix_w_in, m_conv_w, m_conv_b, m_conv_w_out, m_ssm_lam_re, m_ssm_lam_im, m_ssm_log_step, m_ssm_b_re, m_ssm_b_im, m_ssm_c_re, m_ssm_c_im, m_ssm_d, m_ssm_w_glu, m_mix_w_out, m_ln2_g, m_ln2_b, m_ffn2_w_in, m_ffn2_w_out, m_ln3_g, m_ln3_b, m_ple_w_in, m_ple_w_gate, m_ln4_g, m_ln4_b, v_ffn1_w_in, v_ffn1_w_out, v_ln1_g, v_ln1_b, v_mix_w_in, v_conv_w, v_conv_b, v_conv_w_out, v_ssm_lam_re, v_ssm_lam_im, v_ssm_log_step, v_ssm_b_re, v_ssm_b_im, v_ssm_c_re, v_ssm_c_im, v_ssm_d, v_ssm_w_glu, v_mix_w_out, v_ln2_g, v_ln2_b, v_ffn2_w_in, v_ffn2_w_out, v_ln3_g, v_ln3_b, v_ple_w_in, v_ple_w_gate, v_ln4_g, v_ln4_b):
    args = dict(locals())
    w = {n: args[n] for n in WEIGHTS}
    m = {n: args["m_" + n] for n in WEIGHTS}
    v = {n: args["v_" + n] for n in WEIGHTS}
    _, _, c, me = _where()
    cidx = jnp.stack([c, me]).astype(jnp.int32)
    meidx = jnp.reshape(me, (1,)).astype(jnp.int32)

    sched = _Sched(cidx)
    token = sched.first_begin({n: _slot_cast(meidx, w[n][0], BF16, "cast_" + n) for n in GATHER_FIRST})
    rest = {n: _slot_cast(meidx, w[n][0], BF16, "cast_" + n, (token,)) for n in BIG if n not in GATHER_FIRST}
    rest["conv_w"] = _slot_cast(meidx, jnp.pad(conv_w[0], ((0, 13), (0, 0))), F32, "cast_conv_w", (token,))
    sp = {n: (w[n] if w[n].ndim == 2 and n != "ssm_log_step" else w[n][0]) for n in SMALL if n != "conv_w"}
    ops = _s5_operands({**sp, "ssm_lam_re": sp["ssm_lam_re"] + token[0, 0]})
    sched.first_end(rest, list(rest.values()) + list(ops))
    loss_part, dx0 = _local_step(x[0], p[0, 0], loss_target[0], sp, sched, 256, 256, ops)
    out_g, out_d, out_m, out_v = {}, {}, {}, {}

    def big_adamw(names, token):
        for n in names:
            g, dl, mn, vn = _adamw_pair(cidx, w[n][0], sched.half[n], sched.theirs[n], m[n][0], v[n][0], token,
                                        "adamw_" + n)
            out_g[n], out_d[n], out_m[n], out_v[n] = g[None], dl[None], mn[None], vn[None]

    first = REDUCE_GROUP["ple"] + ["ffn2_w_in"]
    big_adamw(first, sched.tail_begin())
    token = sched.tail_mid([out_v[n] for n in first])
    big_adamw(["ffn2_w_out"], token)

    raw_shapes = [sched.raw[k].shape for k in RAW_ORDER]
    raw_sum = dict(zip(RAW_ORDER, _unpack(_sum8(sched.small_buf, token), raw_shapes)))
    loss = raw_sum["loss"][0]
    sg = _small_grads(raw_sum, sp)
    sg["conv_w"] = lax.dynamic_slice_in_dim(sg["conv_w"], me * 128, 128, axis=1)
    small_shapes = [w[n].shape for n in SMALL]
    gp = _pack([sg[n] for n in SMALL])
    d_s, m_s, v_s = _adamw(_pack([w[n] for n in SMALL]), gp, _pack([m[n] for n in SMALL]),
                           _pack([v[n] for n in SMALL]), "adamw_small")

    for n, a, b_, c_, d_ in zip(SMALL, _unpack(gp, small_shapes), _unpack(d_s, small_shapes),
                                _unpack(m_s, small_shapes), _unpack(v_s, small_shapes)):
        out_g[n], out_d[n], out_m[n], out_v[n] = a, b_, c_, d_
    big_adamw(REDUCE_GROUP["mix"], token)
    sched.tail_end([d_s, out_v["ffn2_w_out"]] + [out_v[n] for n in REDUCE_GROUP["mix"]])
    big_adamw(REDUCE_GROUP[LAST_GROUP], token)

    return (loss, dx0[None], *[out_g[n] for n in WEIGHTS], *[out_d[n] for n in WEIGHTS],
            *[out_m[n] for n in WEIGHTS], *[out_v[n] for n in WEIGHTS])
```

```python
import functools
import math

import jax
import jax.numpy as jnp
import numpy as np
from jax import lax
from jax.experimental import pallas as pl
from jax.experimental.pallas import tpu as pltpu

F32, BF16 = jnp.float32, jnp.bfloat16
D = 1024
FF = 2816
FFH = FF // 2
CONV = 512
SSM = 512
GROUPS = 32
STATE = 64
LANES = GROUPS * STATE
SCAN_W = 128
SCAN_PER = 512 // SCAN_W
SCAN_GR = SCAN_W // STATE
SCAN_R = 256
ALPHA = 2.0 ** 0.25
LN_EPS = 1e-5
GELU_C = math.sqrt(2.0 / math.pi)
B1, B2, LR, EPS, WD, STEP = 0.9, 0.999, 0.001, 1e-8, 0.01, 10
MESH = pl.DeviceIdType.MESH
ANY = pl.BlockSpec(memory_space=pl.ANY)
VMEM_FULL = pl.BlockSpec(memory_space=pltpu.VMEM)


def _cp(vmem_mb=48, n_axes=1):
    return pltpu.CompilerParams(vmem_limit_bytes=vmem_mb << 20,
                                dimension_semantics=("arbitrary",) * n_axes)


def _hbm(*arrs):
    return [pltpu.with_memory_space_constraint(a, pltpu.HBM) for a in arrs]


def _hbm_out(shapes):
    if isinstance(shapes, (list, tuple)):
        return [pltpu.HBM(s.shape, s.dtype) for s in shapes]
    return pltpu.HBM(shapes.shape, shapes.dtype)


def _nn(a, b):
    return jnp.dot(a, b, preferred_element_type=F32)


def _nt(a, b):
    return lax.dot_general(a, b, (((1,), (1,)), ((), ())), preferred_element_type=F32)


def _tn(a, b):
    return lax.dot_general(a, b, (((0,), (0,)), ((), ())), preferred_element_type=F32)


def _sig(v):
    return jax.nn.sigmoid(v)


def _ln_stats(r):
    mu = jnp.mean(r, axis=-1, keepdims=True)
    xc = r - mu
    var = jnp.mean(xc * xc, axis=-1, keepdims=True)
    rstd = lax.rsqrt(var + LN_EPS)
    return xc * rstd, rstd


def _ln_bwd(dy, r, g):
    xhat, rstd = _ln_stats(r)
    dyg = dy * g
    m1 = jnp.mean(dyg, axis=-1, keepdims=True)
    m2 = jnp.mean(dyg * xhat, axis=-1, keepdims=True)
    return rstd * (dyg - m1 - xhat * m2), xhat


def _rowsum(v):
    return jnp.sum(v, axis=0, keepdims=True)


class _Payload:
    def __init__(self, operands, outs, aliases, sems, start, finish):
        self.operands, self.outs, self.aliases, self.sems = list(operands), list(outs), dict(aliases), list(sems)
        self.start, self.finish = start, finish


def _split(flat, comm, attr):
    out, i = [], 0
    for p in comm:
        n = len(getattr(p, attr))
        out.append(list(flat[i:i + n]))
        i += n
    return out


def _run_comm(comm, which, cin, cout, csem):
    for p, a, b, s in zip(comm, _split(cin, comm, "operands"), _split(cout, comm, "outs"), _split(csem, comm, "sems")):
        getattr(p, which)(a, b, s)


def _pcall(body, *, name, grid, in_specs, out_specs, out_shape, operands, scratch=(), vmem_mb=48, aliases=None,
           comm=()):
    ni, no, ns = len(in_specs), len(out_specs), len(scratch)
    c_ops = [a for p in comm for a in p.operands]
    c_outs = [s for p in comm for s in p.outs]
    c_sems = [s for p in comm for s in p.sems]
    io = dict(aliases or {})
    off_i, off_o = ni, no
    for p in comm:
        for a, b in p.aliases.items():
            io[off_i + a] = off_o + b
        off_i += len(p.operands)
        off_o += len(p.outs)

    def wrapped(*refs):
        ins, cin = refs[:ni], refs[ni:ni + len(c_ops)]
        o0 = ni + len(c_ops)
        outs, cout = refs[o0:o0 + no], refs[o0 + no:o0 + no + len(c_outs)]
        s0 = o0 + no + len(c_outs)
        scr, csem = refs[s0:s0 + ns], refs[s0 + ns:]
        if comm:
            first = functools.reduce(jnp.logical_and, [pl.program_id(a) == 0 for a in range(len(grid))])
            pl.when(first)(lambda: _run_comm(comm, "start", cin, cout, csem))
        body(*ins, *outs, *scr)
        if comm:
            last = functools.reduce(jnp.logical_and, [pl.program_id(a) == grid[a] - 1 for a in range(len(grid))])
            pl.when(last)(lambda: _run_comm(comm, "finish", cin, cout, csem))

    res = pl.pallas_call(
        wrapped, name=name, grid=grid,
        in_specs=list(in_specs) + [ANY] * len(c_ops), out_specs=list(out_specs) + [ANY] * len(c_outs),
        out_shape=_hbm_out(list(out_shape) + c_outs), scratch_shapes=list(scratch) + c_sems,
        input_output_aliases=io,
        compiler_params=pltpu.CompilerParams(vmem_limit_bytes=vmem_mb << 20,
                                             dimension_semantics=("arbitrary",) * len(grid),
                                             has_side_effects=bool(comm)),
    )(*_hbm(*operands, *c_ops))
    return list(res[:no]), _split(res[no:], comm, "outs")


def _comm_call(name, comm):
    c_ops = [a for p in comm for a in p.operands]
    c_outs = [s for p in comm for s in p.outs]
    c_sems = [s for p in comm for s in p.sems]
    io, off_i, off_o = {}, 0, 0
    for p in comm:
        for a, b in p.aliases.items():
            io[off_i + a] = off_o + b
        off_i += len(p.operands)
        off_o += len(p.outs)

    def body(*refs):
        cin, cout = refs[:len(c_ops)], refs[len(c_ops):len(c_ops) + len(c_outs)]
        csem = refs[len(c_ops) + len(c_outs):]
        _run_comm(comm, "start", cin, cout, csem)
        _run_comm(comm, "finish", cin, cout, csem)

    res = pl.pallas_call(
        body, name=name, in_specs=[ANY] * len(c_ops), out_specs=[ANY] * len(c_outs), out_shape=_hbm_out(c_outs),
        scratch_shapes=c_sems, input_output_aliases=io,
        compiler_params=pltpu.CompilerParams(has_side_effects=True),
    )(*_hbm(*c_ops))
    return _split(res, comm, "outs")


def _ffn_fwd(x, xb, w_in4, w_out2, g, b, tm, name, comm=()):
    T = x.shape[0]

    def body(x_ref, xb_ref, win_ref, wo_ref, g_ref, b_ref, h_ref, r_ref, xo_ref, xob_ref):
        xv = xb_ref[...]
        acc = ALPHA * x_ref[...]
        for k in range(2):
            gt = _nn(xv, win_ref[k])
            up = _nn(xv, win_ref[k + 2])
            a = (gt * _sig(gt) * up).astype(BF16)
            h_ref[:, 2 * k * FFH:(2 * k + 1) * FFH] = gt.astype(BF16)
            h_ref[:, (2 * k + 1) * FFH:(2 * k + 2) * FFH] = up.astype(BF16)
            acc = acc + 0.5 * _nn(a, wo_ref[k])
        xhat, _ = _ln_stats(acc)
        xo = xhat * g_ref[...] + b_ref[...]
        r_ref[...] = acc
        xo_ref[...] = xo
        xob_ref[...] = xo.astype(BF16)

    tok = pl.BlockSpec((tm, D), lambda i: (i, 0))
    vec = pl.BlockSpec((1, D), lambda i: (0, 0))
    return _pcall(
        body, name=name, grid=(T // tm,),
        in_specs=[tok, tok,
                  pl.BlockSpec((4, D, FFH), lambda i: (0, 0, 0), pipeline_mode=pl.Buffered(1)),
                  pl.BlockSpec((2, FFH, D), lambda i: (0, 0, 0), pipeline_mode=pl.Buffered(1)),
                  vec, vec],
        out_specs=[pl.BlockSpec((tm, 2 * FF), lambda i: (i, 0)), tok, tok, tok],
        out_shape=[jax.ShapeDtypeStruct((T, 2 * FF), BF16), jax.ShapeDtypeStruct((T, D), F32),
                   jax.ShapeDtypeStruct((T, D), F32), jax.ShapeDtypeStruct((T, D), BF16)],
        vmem_mb=58, comm=comm, operands=(x, xb, w_in4, w_out2, g, b))


def _ffn_bwd(dy, r, g, h, w_in4, w_out2, tm, name, comm=()):
    T = dy.shape[0]

    def body(dy_ref, r_ref, g_ref, h_ref, win_ref, wo_ref, dx_ref, dh_ref, a_ref, df_ref, dg_ref, db_ref):
        i = pl.program_id(0)
        dyv = dy_ref[...]
        dr, xhat = _ln_bwd(dyv, r_ref[...], g_ref[...])
        dg_ref[...] = jnp.where(i == 0, 0.0, dg_ref[...]) + _rowsum(dyv * xhat)
        db_ref[...] = jnp.where(i == 0, 0.0, db_ref[...]) + _rowsum(dyv)
        dfb = (0.5 * dr).astype(BF16)
        df_ref[...] = dfb
        acc = ALPHA * dr
        for k in range(2):
            da = _nt(dfb, wo_ref[k])
            gt = h_ref[:, 2 * k * FFH:(2 * k + 1) * FFH].astype(F32)
            up = h_ref[:, (2 * k + 1) * FFH:(2 * k + 2) * FFH].astype(F32)
            sg = _sig(gt)
            silu = gt * sg
            dgate = (da * up * (sg * (1.0 + gt * (1.0 - sg)))).astype(BF16)
            dup = (da * silu).astype(BF16)
            a_ref[:, k * FFH:(k + 1) * FFH] = (silu * up).astype(BF16)
            dh_ref[:, 2 * k * FFH:(2 * k + 1) * FFH] = dgate
            dh_ref[:, (2 * k + 1) * FFH:(2 * k + 2) * FFH] = dup
            acc = acc + _nt(dgate, win_ref[k]) + _nt(dup, win_ref[k + 2])
        dx_ref[...] = acc

    tok = pl.BlockSpec((tm, D), lambda i: (i, 0))
    vec = pl.BlockSpec((1, D), lambda i: (0, 0))
    wide = pl.BlockSpec((tm, 2 * FF), lambda i: (i, 0))
    return _pcall(
        body, name=name, grid=(T // tm,),
        in_specs=[tok, tok, vec, wide,
                  pl.BlockSpec((4, D, FFH), lambda i: (0, 0, 0), pipeline_mode=pl.Buffered(1)),
                  pl.BlockSpec((2, FFH, D), lambda i: (0, 0, 0), pipeline_mode=pl.Buffered(1))],
        out_specs=[tok, wide, pl.BlockSpec((tm, FF), lambda i: (i, 0)), tok, vec, vec],
        out_shape=[jax.ShapeDtypeStruct((T, D), F32), jax.ShapeDtypeStruct((T, 2 * FF), BF16),
                   jax.ShapeDtypeStruct((T, FF), BF16), jax.ShapeDtypeStruct((T, D), BF16),
                   jax.ShapeDtypeStruct((1, D), F32), jax.ShapeDtypeStruct((1, D), F32)],
        vmem_mb=58, comm=comm, operands=(dy, r, g, h, w_in4, w_out2))


def _mm_tn(a, b, tk, tn, name, shard_cols=None, interleaved=False, comm=()):
    T, K = a.shape
    N = b.shape[1]

    def body(a_ref, b_ref, o_ref):
        o_ref[...] = _tn(a_ref[...], b_ref[...])

    if shard_cols is None:
        out_shape = jax.ShapeDtypeStruct((K, N), F32)
        out_spec = pl.BlockSpec((tk, tn), lambda ki, nj: (ki, nj))
    else:
        per = shard_cols // tn

        def shard(nj):
            blk = nj // per
            return (blk % 2) * 2 + blk // 2 if interleaved else blk

        out_shape = jax.ShapeDtypeStruct((N // shard_cols, K, shard_cols), F32)
        out_spec = pl.BlockSpec((None, tk, tn), lambda ki, nj: (shard(nj), ki, nj % per))
    (out,), got = _pcall(
        body, name=name, grid=(K // tk, N // tn),
        in_specs=[pl.BlockSpec((T, tk), lambda ki, nj: (0, ki)), pl.BlockSpec((T, tn), lambda ki, nj: (0, nj))],
        out_specs=[out_spec], out_shape=[out_shape], comm=comm, operands=(a, b))
    return out, got


def _mix_fwd_a(xb, w_mix4, conv_w, conv_b, w_co4, tm, comm=()):
    T = xb.shape[0]

    def body(xb_ref, w_ref, cw_ref, cb_ref, wco_ref,
             pc_ref, z_ref, yin_ref, su_ref, sub_ref, gc_ref, gs_ref, yc_ref, qbuf):
        @pl.when(pl.program_id(0) == 0)
        def _():
            qbuf[pl.ds(0, 8), :] = jnp.zeros((8, CONV), F32)

        xv = xb_ref[...]
        p0 = _nn(xv, w_ref[0])
        p1 = _nn(xv, w_ref[1])
        gc_ref[...] = _nn(xv, w_ref[2]).astype(BF16)
        gs_ref[...] = _nn(xv, w_ref[3]).astype(BF16)
        cbv, ccv = p0[:, :CONV], p0[:, CONV:]
        chv, suv = p1[:, :CONV], p1[:, CONV:]
        q = ccv * chv
        qbuf[pl.ds(8, tm), :] = q
        cw = cw_ref[...]
        z = (cw[2:3] * q + cw[1:2] * qbuf[pl.ds(7, tm), :] + cw[0:1] * qbuf[pl.ds(6, tm), :]
             + cb_ref[...])
        qbuf[pl.ds(0, 8), :] = q[tm - 8:tm]
        yin = (cbv * z).astype(BF16)
        pc_ref[:, 0:CONV] = cbv.astype(BF16)
        pc_ref[:, CONV:2 * CONV] = ccv.astype(BF16)
        pc_ref[:, 2 * CONV:3 * CONV] = chv.astype(BF16)
        z_ref[...] = z.astype(BF16)
        yin_ref[...] = yin
        su_ref[...] = suv
        sub_ref[...] = suv.astype(BF16)
        for k in range(4):
            yc_ref[:, 256 * k:256 * (k + 1)] = _nn(yin, wco_ref[k]).astype(BF16)

    def tok(n):
        return pl.BlockSpec((tm, n), lambda i: (i, 0))

    def full(shape):
        return pl.BlockSpec(shape, lambda i: (0,) * len(shape))

    return _pcall(
        body, name="mix_fwd_a", grid=(T // tm,),
        in_specs=[tok(D), full((4, D, D)), full((3, CONV)), full((1, CONV)), full((4, CONV, 256))],
        out_specs=[tok(3 * CONV), tok(CONV), tok(CONV), tok(SSM), tok(SSM), tok(D), tok(D), tok(D)],
        out_shape=[jax.ShapeDtypeStruct((T, 3 * CONV), BF16), jax.ShapeDtypeStruct((T, CONV), BF16),
                   jax.ShapeDtypeStruct((T, CONV), BF16), jax.ShapeDtypeStruct((T, SSM), F32),
                   jax.ShapeDtypeStruct((T, SSM), BF16), jax.ShapeDtypeStruct((T, D), BF16),
                   jax.ShapeDtypeStruct((T, D), BF16), jax.ShapeDtypeStruct((T, D), BF16)],
        scratch=[pltpu.VMEM((tm + 8, CONV), F32)], vmem_mb=56, comm=comm,
        operands=(xb, w_mix4, conv_w, conv_b, w_co4))


def _scan_rows(bre, bim, ar, ai, T, rev, load):
    R, W, G = SCAN_R, bre.shape[1], T // 8
    if rev:
        ai = -ai

    def cmul(pr, pi, xr, xi):
        return pr * xr - pi * xi, pr * xi + pi * xr

    pw = [(ar, ai)]
    for _ in range(7):
        pw.append(cmul(ar, ai, *pw[-1]))

    def shifted(v, d, axis, n, idx):
        if rev:
            return jnp.where(idx < n - d, pltpu.roll(v, n - d, axis), 0.0)
        return jnp.where(idx >= d, pltpu.roll(v, d, axis), 0.0)

    sub8 = lax.broadcasted_iota(jnp.int32, (8, W), 0)
    inside = {d: (sub8 < 8 - d) if rev else (sub8 >= d) for d in (1, 2, 4)}
    pm = {d: (jnp.where(inside[d], pw[d - 1][0], 0.0)[None], jnp.where(inside[d], pw[d - 1][1], 0.0)[None])
          for d in (1, 2, 4)}

    def step(i, _):
        t0 = pl.multiple_of(i * R, R)
        vr, vi = load(t0)
        vr, vi = vr.reshape(R // 8, 8, W), vi.reshape(R // 8, 8, W)
        for d in (1, 2, 4):
            sh = (8 - d) if rev else d
            dr, di = cmul(pm[d][0], pm[d][1], pltpu.roll(vr, sh, 1), pltpu.roll(vi, sh, 1))
            vr, vi = vr + dr, vi + di
        bre[pl.ds(t0 + 8, R), :] = vr.reshape(R, W)
        bim[pl.ds(t0 + 8, R), :] = vi.reshape(R, W)
        return 0

    lax.fori_loop(0, T // R, step, 0)

    edge = 0 if rev else 7
    cr = bre[pl.ds(8 + edge, G, stride=8), :]
    ci = bim[pl.ds(8 + edge, G, stride=8), :]
    row = lax.broadcasted_iota(jnp.int32, (G, W), 0)
    qr, qi = pw[7]
    d = 1
    while d < G:
        dr, di = cmul(qr, qi, shifted(cr, d, 0, G, row), shifted(ci, d, 0, G, row))
        cr, ci = cr + dr, ci + di
        qr, qi = qr * qr - qi * qi, 2.0 * qr * qi
        d *= 2

    nr, ni = shifted(cr, 1, 0, G, row), shifted(ci, 1, 0, G, row)
    for r in range(8):
        pr, pi = pw[7 - r] if rev else pw[r]
        dr, di = cmul(pr, pi, nr, ni)
        bre[pl.ds(8 + r, G, stride=8), :] = bre[pl.ds(8 + r, G, stride=8), :] + dr
        bim[pl.ds(8 + r, G, stride=8), :] = bim[pl.ds(8 + r, G, stride=8), :] + di


def _scan_specs(T):
    W = SCAN_W
    lane = pl.BlockSpec((T, W), lambda j: (0, j))
    col = pl.BlockSpec((T, 128), lambda j: (0, j // SCAN_PER))
    wb = pl.BlockSpec((None, 128, W), lambda j: (j, 0, 0))
    wc = pl.BlockSpec((None, W, 128), lambda j: (j, 0, 0))
    vec = pl.BlockSpec((1, W), lambda j: (0, j))
    return lane, col, wb, wc, vec


def _s5_scan_fwd(su_b, wb_re, wb_im, a_re, a_im, comm=()):
    T = su_b.shape[0]
    W = SCAN_W

    def body(su_ref, wbr_ref, wbi_ref, ar_ref, ai_ref, sr_ref, si_ref, bre, bim):
        su = su_ref[...]
        bre[pl.ds(8, T), :] = _nn(su, wbr_ref[...])
        bim[pl.ds(8, T), :] = _nn(su, wbi_ref[...])
        _scan_rows(bre, bim, ar_ref[...], ai_ref[...], T, False,
                   lambda t0: (bre[pl.ds(t0 + 8, SCAN_R), :], bim[pl.ds(t0 + 8, SCAN_R), :]))
        sr_ref[...] = bre[pl.ds(8, T), :].astype(BF16)
        si_ref[...] = bim[pl.ds(8, T), :].astype(BF16)

    lane, col, wb, wc, vec = _scan_specs(T)
    return _pcall(
        body, name="s5_scan_fwd", grid=(LANES // W,),
        in_specs=[col, wb, wb, vec, vec],
        out_specs=[lane, lane],
        out_shape=[jax.ShapeDtypeStruct((T, LANES), BF16)] * 2,
        scratch=[pltpu.VMEM((T + 16, W), F32)] * 2, comm=comm,
        operands=(su_b, wb_re, wb_im, a_re, a_im))


def _gelu(s):
    th = jnp.tanh(GELU_C * (s + 0.044715 * s * s * s))
    return 0.5 * s * (1.0 + th), th


def _mix_fwd_b(st_re, st_im, wc_re4, wc_im4, su, dvec, w_glu4, g_conv, g_ssm, y_conv, w_mo, x1, g, b, tm, comm=()):
    T = su.shape[0]

    def body(sr_ref, si_ref, wcr_ref, wci_ref, su_ref, d_ref, wg_ref, gc_ref, gs_ref, yc_ref, wmo_ref,
             x_ref, g_ref, b_ref, s_ref, sgb_ref, ga_ref, gb_ref, mb_ref, r_ref, xo_ref, xob_ref):
        srb = sr_ref[...]
        sib = si_ref[...]
        ys = [_nn(srb[:, 512 * J:512 * (J + 1)], wcr_ref[J]) + _nn(sib[:, 512 * J:512 * (J + 1)], wci_ref[J])
              for J in range(4)]
        s = jnp.concatenate(ys, axis=1) + d_ref[...] * su_ref[...]
        sg, _ = _gelu(s)
        sgb = sg.astype(BF16)
        ga = jnp.concatenate([_nn(sgb, wg_ref[0]), _nn(sgb, wg_ref[1])], axis=1)
        gb = jnp.concatenate([_nn(sgb, wg_ref[2]), _nn(sgb, wg_ref[3])], axis=1)
        merged = (_sig(gc_ref[...].astype(F32)) * yc_ref[...].astype(F32)
                  + _sig(gs_ref[...].astype(F32)) * (ga * _sig(gb)))
        mb = merged.astype(BF16)
        r = ALPHA * x_ref[...] + _nn(mb, wmo_ref[...])
        xhat, _ = _ln_stats(r)
        xo = xhat * g_ref[...] + b_ref[...]
        s_ref[...] = s
        sgb_ref[...] = sgb
        ga_ref[...] = ga.astype(BF16)
        gb_ref[...] = gb.astype(BF16)
        mb_ref[...] = mb
        r_ref[...] = r
        xo_ref[...] = xo
        xob_ref[...] = xo.astype(BF16)

    def tok(n):
        return pl.BlockSpec((tm, n), lambda i: (i, 0))

    def full(shape):
        return pl.BlockSpec(shape, lambda i: (0,) * len(shape))

    return _pcall(
        body, name="mix_fwd_b", grid=(T // tm,),
        in_specs=[tok(LANES), tok(LANES), full((4, 512, 128)), full((4, 512, 128)), tok(SSM), full((1, SSM)),
                  full((4, SSM, 512)), tok(D), tok(D), tok(D), full((D, D)), tok(D), full((1, D)), full((1, D))],
        out_specs=[tok(SSM), tok(SSM), tok(D), tok(D), tok(D), tok(D), tok(D), tok(D)],
        out_shape=[jax.ShapeDtypeStruct((T, SSM), F32), jax.ShapeDtypeStruct((T, SSM), BF16),
                   jax.ShapeDtypeStruct((T, D), BF16), jax.ShapeDtypeStruct((T, D), BF16),
                   jax.ShapeDtypeStruct((T, D), BF16), jax.ShapeDtypeStruct((T, D), F32),
                   jax.ShapeDtypeStruct((T, D), F32), jax.ShapeDtypeStruct((T, D), BF16)],
        vmem_mb=56, comm=comm,
        operands=(st_re, st_im, wc_re4, wc_im4, su, dvec, w_glu4, g_conv, g_ssm, y_conv, w_mo, x1, g, b))


def _ple_loss(x3, x3b, p, w_pi4, w_pg, g, b, target, tm):
    T = x3.shape[0]
    PD = p.shape[1]

    def body(x_ref, xb_ref, p_ref, wpi_ref, wpg_ref, g_ref, b_ref, t_ref,
             loss_ref, dx_ref, pb_ref, dpw_ref, dgt_ref, dg_ref, db_ref):
        i = pl.program_id(0)
        pb = p_ref[...].astype(BF16)
        pw = jnp.concatenate([_nn(pb, wpi_ref[k]) for k in range(4)], axis=1)
        gt = _nn(xb_ref[...], wpg_ref[...])
        sg = _sig(gt)
        r = ALPHA * x_ref[...] + pw * sg
        gv = g_ref[...]
        xhat, rstd = _ln_stats(r)
        err = xhat * gv + b_ref[...] - t_ref[...]
        lpart = jnp.zeros((1, 128), F32) + 0.5 * jnp.sum(jnp.mean(err * err, axis=-1, keepdims=True))
        dy = err * (1.0 / D)
        dyg = dy * gv
        m1 = jnp.mean(dyg, axis=-1, keepdims=True)
        m2 = jnp.mean(dyg * xhat, axis=-1, keepdims=True)
        dr = rstd * (dyg - m1 - xhat * m2)
        pg, pbias = _rowsum(dy * xhat), _rowsum(dy)

        @pl.when(i == 0)
        def _():
            loss_ref[...] = lpart
            dg_ref[...] = pg
            db_ref[...] = pbias

        @pl.when(i > 0)
        def _():
            loss_ref[...] += lpart
            dg_ref[...] += pg
            db_ref[...] += pbias

        dgt = (dr * pw * sg * (1.0 - sg)).astype(BF16)
        pb_ref[...] = pb
        dpw_ref[...] = (dr * sg).astype(BF16)
        dgt_ref[...] = dgt
        dx_ref[...] = ALPHA * dr + _nt(dgt, wpg_ref[...])

    def tok(n):
        return pl.BlockSpec((tm, n), lambda i: (i, 0))

    def full(shape):
        return pl.BlockSpec(shape, lambda i: (0,) * len(shape))

    return pl.pallas_call(
        body, name="ple_loss", grid=(T // tm,),
        in_specs=[tok(D), tok(D), tok(PD), full((4, PD, 256)), full((D, D)), full((1, D)), full((1, D)), tok(D)],
        out_specs=[full((1, 128)), tok(D), tok(PD), tok(D), tok(D), full((1, D)), full((1, D))],
        out_shape=_hbm_out([jax.ShapeDtypeStruct((1, 128), F32), jax.ShapeDtypeStruct((T, D), F32),
                            jax.ShapeDtypeStruct((T, PD), BF16), jax.ShapeDtypeStruct((T, D), BF16),
                            jax.ShapeDtypeStruct((T, D), BF16), jax.ShapeDtypeStruct((1, D), F32),
                            jax.ShapeDtypeStruct((1, D), F32)]),
        compiler_params=_cp(48, 1),
    )(*_hbm(x3, x3b, p, w_pi4, w_pg, g, b, target))


def _mix_bwd_b(dy, r2, g, w_mo, g_conv, g_ssm, y_conv, ga, gb, s, su, dvec, w_glu4, wc_re4, wc_im4, tm, comm=()):
    T = dy.shape[0]

    def body(dy_ref, r_ref, g_ref, wmo_ref, gc_ref, gs_ref, yc_ref, ga_ref, gb_ref, s_ref, su_ref, d_ref,
             wg_ref, wcr_ref, wci_ref,
             dres_ref, dmix_ref, dgl_ref, dsb_ref, dud_ref, gsr_ref, gsi_ref, dyc_ref, dp_ref,
             dg_ref, db_ref, dd_ref):
        i = pl.program_id(0)
        dyv = dy_ref[...]
        dr, xhat = _ln_bwd(dyv, r_ref[...], g_ref[...])
        dmix = dr.astype(BF16)
        dmerged = _nt(dmix, wmo_ref[...])
        sc, ss, sgb = (_sig(gc_ref[...].astype(F32)), _sig(gs_ref[...].astype(F32)),
                       _sig(gb_ref[...].astype(F32)))
        gav = ga_ref[...].astype(F32)
        yssm = gav * sgb
        dgc = dmerged * yc_ref[...].astype(F32) * sc * (1.0 - sc)
        dgss = dmerged * yssm * ss * (1.0 - ss)
        dyssm = dmerged * ss
        dgl = jnp.concatenate([dyssm * sgb, dyssm * gav * sgb * (1.0 - sgb)], axis=1).astype(BF16)
        dsg = (_nt(dgl[:, 0:512], wg_ref[0]) + _nt(dgl[:, 512:1024], wg_ref[1])
               + _nt(dgl[:, 1024:1536], wg_ref[2]) + _nt(dgl[:, 1536:2048], wg_ref[3]))
        sv = s_ref[...]
        _, th = _gelu(sv)
        dgelu = 0.5 * (1.0 + th) + 0.5 * sv * (1.0 - th * th) * GELU_C * (1.0 + 3.0 * 0.044715 * sv * sv)
        ds = dsg * dgelu
        dsb = ds.astype(BF16)
        pg, pb, pd = _rowsum(dyv * xhat), _rowsum(dyv), _rowsum(ds * su_ref[...])

        @pl.when(i == 0)
        def _():
            dg_ref[...] = pg
            db_ref[...] = pb
            dd_ref[...] = pd

        @pl.when(i > 0)
        def _():
            dg_ref[...] += pg
            db_ref[...] += pb
            dd_ref[...] += pd

        dres_ref[...] = ALPHA * dr
        dmix_ref[...] = dmix
        dgl_ref[...] = dgl
        dsb_ref[...] = dsb
        dud_ref[...] = ds * d_ref[...]
        for J in range(4):
            gsr_ref[:, 512 * J:512 * (J + 1)] = _nt(dsb[:, 128 * J:128 * (J + 1)], wcr_ref[J]).astype(BF16)
            gsi_ref[:, 512 * J:512 * (J + 1)] = _nt(dsb[:, 128 * J:128 * (J + 1)], wci_ref[J]).astype(BF16)
        dyc_ref[...] = (dmerged * sc).astype(BF16)
        dp_ref[:, 0:D] = dgc.astype(BF16)
        dp_ref[:, D:2 * D] = dgss.astype(BF16)

    def tok(n):
        return pl.BlockSpec((tm, n), lambda i: (i, 0))

    def full(shape):
        return pl.BlockSpec(shape, lambda i: (0,) * len(shape))

    return _pcall(
        body, name="mix_bwd_b", grid=(T // tm,),
        in_specs=[tok(D), tok(D), full((1, D)), full((D, D)), tok(D), tok(D), tok(D), tok(D), tok(D),
                  tok(SSM), tok(SSM), full((1, SSM)), full((4, SSM, 512)), full((4, 512, 128)), full((4, 512, 128))],
        out_specs=[tok(D), tok(D), tok(2 * D), tok(SSM), tok(SSM), tok(LANES), tok(LANES), tok(D),
                   pl.BlockSpec((tm, 2 * D), lambda i: (i, 1)), full((1, D)), full((1, D)), full((1, SSM))],
        out_shape=[jax.ShapeDtypeStruct((T, D), F32), jax.ShapeDtypeStruct((T, D), BF16),
                   jax.ShapeDtypeStruct((T, 2 * D), BF16), jax.ShapeDtypeStruct((T, SSM), BF16),
                   jax.ShapeDtypeStruct((T, SSM), F32), jax.ShapeDtypeStruct((T, LANES), BF16),
                   jax.ShapeDtypeStruct((T, LANES), BF16), jax.ShapeDtypeStruct((T, D), BF16),
                   jax.ShapeDtypeStruct((T, 4 * D), BF16), jax.ShapeDtypeStruct((1, D), F32),
                   jax.ShapeDtypeStruct((1, D), F32), jax.ShapeDtypeStruct((1, SSM), F32)],
        vmem_mb=56, comm=comm,
        operands=(dy, r2, g, w_mo, g_conv, g_ssm, y_conv, ga, gb, s, su, dvec, w_glu4, wc_re4, wc_im4))


def _s5_scan_bwd(gs_re, gs_im, st_re, st_im, su_b, ds_b, wb_re, wb_im, a_re, a_im, comm=()):
    T = su_b.shape[0]
    W = SCAN_W
    R = SCAN_R

    def body(gr_ref, gi_ref, sr_ref, si_ref, su_ref, ds_ref, wbr_ref, wbi_ref, ar_ref, ai_ref,
             dsu_ref, dwbr_ref, dwbi_ref, dwcr_ref, dwci_ref, dar_ref, dai_ref, gre, gim):
        j = pl.program_id(0)
        zero = jnp.zeros((8, W), F32)
        for buf in (gre, gim):
            buf[pl.ds(T + 8, 8), :] = zero
        _scan_rows(gre, gim, ar_ref[...], ai_ref[...], T, True,
                   lambda t0: (gr_ref[pl.ds(t0, R), :].astype(F32), gi_ref[pl.ds(t0, R), :].astype(F32)))
        grb = gre[pl.ds(8, T), :].astype(BF16)
        gib = gim[pl.ds(8, T), :].astype(BF16)
        part = _nt(grb, wbr_ref[...]) + _nt(gib, wbi_ref[...])

        @pl.when(j % SCAN_PER == 0)
        def _():
            dsu_ref[...] = part

        @pl.when(j % SCAN_PER > 0)
        def _():
            dsu_ref[...] += part

        su = su_ref[...]
        dwbr_ref[...] = _tn(su, grb)
        dwbi_ref[...] = _tn(su, gib)
        dsv = ds_ref[...]
        dwcr_ref[...] = _tn(sr_ref[...], dsv)
        dwci_ref[...] = _tn(si_ref[...], dsv)
        dar = jnp.zeros((1, W), F32)
        dai = jnp.zeros((1, W), F32)
        for c in range(T // R):
            xr = sr_ref[pl.ds(c * R, R), :].astype(F32)
            xi = si_ref[pl.ds(c * R, R), :].astype(F32)
            g1r = gre[pl.ds(c * R + 9, R), :]
            g1i = gim[pl.ds(c * R + 9, R), :]
            dar = dar + _rowsum(g1r * xr + g1i * xi)
            dai = dai + _rowsum(g1i * xr - g1r * xi)
        dar_ref[...] = dar
        dai_ref[...] = dai

    lane, col, wb, wc, vec = _scan_specs(T)
    return _pcall(
        body, name="s5_scan_bwd", grid=(LANES // W,),
        in_specs=[lane, lane, lane, lane, col, col, wb, wb, vec, vec],
        out_specs=[col, wb, wb, wc, wc, vec, vec],
        out_shape=[jax.ShapeDtypeStruct((T, SSM), F32),
                   jax.ShapeDtypeStruct((LANES // W, 128, W), F32), jax.ShapeDtypeStruct((LANES // W, 128, W), F32),
                   jax.ShapeDtypeStruct((LANES // W, W, 128), F32), jax.ShapeDtypeStruct((LANES // W, W, 128), F32),
                   jax.ShapeDtypeStruct((1, LANES), F32), jax.ShapeDtypeStruct((1, LANES), F32)],
        scratch=[pltpu.VMEM((T + 16, W), F32)] * 2, vmem_mb=56, comm=comm,
        operands=(gs_re, gs_im, st_re, st_im, su_b, ds_b, wb_re, wb_im, a_re, a_im))


def _mix_bwd_a(dyc_b, w_co4, pc, z_b, conv_w, dsu_ssm, du_dir, dproj, dres, w_mix4, tm, comm=()):
    T = dres.shape[0]
    nt = T // tm

    def body(dyc_ref, wco_ref, pc_ref, halo_ref, z_ref, cw_ref, dsu_ref, dud_ref, dpin_ref, dres_ref, w_ref,
             dp_ref, dx_ref, dcw_ref, dcb_ref, dzbuf, qbuf):
        i = pl.program_id(0)
        ii = nt - 1 - i

        @pl.when(i == 0)
        def _():
            dzbuf[pl.ds(tm, 8), :] = jnp.zeros((8, CONV), F32)

        dyc = dyc_ref[...]
        dyin = (_nt(dyc[:, 0:256], wco_ref[0]) + _nt(dyc[:, 256:512], wco_ref[1])
                + _nt(dyc[:, 512:768], wco_ref[2]) + _nt(dyc[:, 768:1024], wco_ref[3]))
        cbv = pc_ref[:, 0:CONV].astype(F32)
        ccv = pc_ref[:, CONV:2 * CONV].astype(F32)
        chv = pc_ref[:, 2 * CONV:3 * CONV].astype(F32)
        dcbv = dyin * z_ref[...].astype(F32)
        dz = dyin * cbv
        dzbuf[pl.ds(0, tm), :] = dz
        cw = cw_ref[...]
        dq = cw[2:3] * dz + cw[1:2] * dzbuf[pl.ds(1, tm), :] + cw[0:1] * dzbuf[pl.ds(2, tm), :]
        dzbuf[pl.ds(tm, 8), :] = dz[0:8]
        q = ccv * chv
        hq = halo_ref[:, CONV:2 * CONV].astype(F32) * halo_ref[:, 2 * CONV:3 * CONV].astype(F32)
        qbuf[pl.ds(0, 8), :] = jnp.where(ii > 0, hq, jnp.zeros_like(hq))
        qbuf[pl.ds(8, tm), :] = q
        pw = jnp.concatenate([_rowsum(dz * qbuf[pl.ds(6, tm), :]), _rowsum(dz * qbuf[pl.ds(7, tm), :]),
                              _rowsum(dz * q), jnp.zeros((5, CONV), F32)], axis=0)
        pbias = _rowsum(dz)

        @pl.when(i == 0)
        def _():
            dcw_ref[...] = pw
            dcb_ref[...] = pbias

        @pl.when(i > 0)
        def _():
            dcw_ref[...] += pw
            dcb_ref[...] += pbias

        dp0 = jnp.concatenate([dcbv, dq * chv], axis=1).astype(BF16)
        dp1 = jnp.concatenate([dq * ccv, dsu_ref[...] + dud_ref[...]], axis=1).astype(BF16)
        dp_ref[:, 0:D] = dp0
        dp_ref[:, D:2 * D] = dp1
        dx_ref[...] = (dres_ref[...] + _nt(dp0, w_ref[0]) + _nt(dp1, w_ref[1])
                       + _nt(dpin_ref[:, 0:D], w_ref[2]) + _nt(dpin_ref[:, D:2 * D], w_ref[3]))

    def tok(n):
        return pl.BlockSpec((tm, n), lambda i: (nt - 1 - i, 0))

    def full(shape):
        return pl.BlockSpec(shape, lambda i: (0,) * len(shape))

    halo = pl.BlockSpec((8, 3 * CONV), lambda i: (jnp.maximum((nt - 1 - i) * (tm // 8) - 1, 0), 0))
    return _pcall(
        body, name="mix_bwd_a", grid=(nt,),
        in_specs=[tok(D), full((4, CONV, 256)), tok(3 * CONV), halo, tok(CONV), full((3, CONV)),
                  tok(SSM), tok(SSM), pl.BlockSpec((tm, 2 * D), lambda i: (nt - 1 - i, 1)), tok(D),
                  full((4, D, D))],
        out_specs=[pl.BlockSpec((tm, 2 * D), lambda i: (nt - 1 - i, 0)), tok(D), full((8, CONV)), full((1, CONV))],
        out_shape=[jax.ShapeDtypeStruct((T, 4 * D), BF16), jax.ShapeDtypeStruct((T, D), F32),
                   jax.ShapeDtypeStruct((8, CONV), F32), jax.ShapeDtypeStruct((1, CONV), F32)],
        scratch=[pltpu.VMEM((tm + 8, CONV), F32), pltpu.VMEM((tm + 8, CONV), F32)],
        aliases={8: 0}, vmem_mb=56, comm=comm,
        operands=(dyc_b, w_co4, pc, pc, z_b, conv_w, dsu_ssm, du_dir, dproj, dres, w_mix4))


def _zoh(lam_re, lam_im, log_step, b_re, b_im):
    dt = jnp.exp(log_step)[:, None]
    mag = jnp.exp(lam_re * dt)
    abr, abi = mag * jnp.cos(lam_im * dt), mag * jnp.sin(lam_im * dt)
    nr, ni = abr - 1.0, abi
    den = lam_re * lam_re + lam_im * lam_im
    cr = (nr * lam_re + ni * lam_im) / den
    ci = (ni * lam_re - nr * lam_im) / den
    bbr = cr[..., None] * b_re - ci[..., None] * b_im
    bbi = cr[..., None] * b_im + ci[..., None] * b_re
    return abr, abi, bbr, bbi


_WB_MASK = (np.arange(8)[None, :, None]
            == SCAN_GR * np.arange(SCAN_PER)[:, None, None] + np.arange(SCAN_GR)[None, None, :]).astype(np.float32)
_EYE8 = np.eye(8, dtype=np.float32)


def _wb_blocks(bb):
    bt = bb.transpose(0, 2, 1).reshape(4, 1, 8, 16, 1, STATE)
    full = bt * _WB_MASK[None, :, :, None, :, None]
    return full.reshape(LANES // SCAN_W, 128, SCAN_W).astype(BF16)


def _wc_blocks(cc):
    ct = cc.transpose(0, 2, 1).reshape(4, 8, STATE, 1, 16)
    full = ct * _EYE8[None, :, None, :, None]
    return full.reshape(4, 512, 128).astype(BF16)


def _wb_diag(dwb):
    d6 = dwb.reshape(4, SCAN_PER, 8, 16, SCAN_GR, STATE) * _WB_MASK[None, :, :, None, :, None]
    return d6.sum(axis=(1, 4)).reshape(GROUPS, 16, STATE).transpose(0, 2, 1)


def _wc_diag(dwc):
    mask = _WB_MASK.transpose(0, 2, 1)
    d6 = dwc.reshape(4, SCAN_PER, SCAN_GR, STATE, 8, 16) * mask[None, :, :, None, :, None]
    return d6.sum(axis=4).reshape(GROUPS, STATE, 16).transpose(0, 2, 1)


def _where():
    x, y, c = lax.axis_index("x"), lax.axis_index("y"), lax.axis_index("c")
    return x, y, c, 2 * x + y


def _chip_dev(k, c):
    return (k // 2, k % 2, c)


def _slot_cast(meidx, w, dtype, name, token=()):
    R, C = w.shape
    tr = _row_tile(R)

    def body(m_ref, w_ref, *rest):
        rest[-1][...] = w_ref[...].astype(dtype)

    gs = pltpu.PrefetchScalarGridSpec(
        num_scalar_prefetch=1, grid=(R // tr,),
        in_specs=[pl.BlockSpec((tr, C), lambda i, m: (i, 0))] + [pl.BlockSpec((8, 128), lambda i, m: (0, 0))] * len(token),
        out_specs=pl.BlockSpec((None, tr, C), lambda i, m: (m[0], i, 0)))
    return pl.pallas_call(
        body, name=name, grid_spec=gs, out_shape=_hbm_out(jax.ShapeDtypeStruct((4, R, C), dtype)),
        compiler_params=_cp(32, 1),
    )(meidx, *_hbm(w), *token)


def _gather_ici_payload(bufs):
    def copies(ins, lnd, ss, rs):
        x, y, c, me = _where()
        cps = []
        for w, b in enumerate(bufs):
            h = b.shape[1] // 2
            mine = lnd[w].at[me, pl.ds(c * h, h)]
            for s in range(3):
                k = (me + 1 + s) % 4
                cps.append(pltpu.make_async_remote_copy(
                    src_ref=mine, dst_ref=mine, send_sem=ss.at[3 * w + s], recv_sem=rs.at[3 * w + s],
                    device_id=_chip_dev(k, c), device_id_type=MESH))
        return cps

    p = _sym_payload([], [jax.ShapeDtypeStruct(b.shape, b.dtype) for b in bufs], copies, 3 * len(bufs))
    p.lands = list(bufs)
    return p


def _gather_pass_payload(bufs):
    def copies(ins, outs, ss, rs):
        x, y, c, me = _where()
        cps = []
        for w, b in enumerate(bufs):
            h = b.shape[1] // 2
            for s in range(3):
                j = (me + 1 + s) % 4
                cps.append(pltpu.make_async_remote_copy(
                    src_ref=ins[w].at[j, pl.ds(c * h, h)], dst_ref=outs[w].at[j, pl.ds(c * h, h)],
                    send_sem=ss.at[3 * w + s], recv_sem=rs.at[3 * w + s], device_id=(x, y, 1 - c),
                    device_id_type=MESH))
        return cps

    p = _sym_payload(bufs, [jax.ShapeDtypeStruct(b.shape, b.dtype) for b in bufs], copies, 3 * len(bufs))
    p.aliases = {w: w for w in range(len(bufs))}
    return p


def _gather_payload(bufs):
    n = len(bufs)

    def half(ref, w, k, cc):
        h = bufs[w].shape[1] // 2
        return ref.at[k, pl.ds(cc * h, h)]

    def ici(ins, outs, sems, w, s):
        x, y, c, me = _where()
        k = (me + 1 + s) % 4
        return pltpu.make_async_remote_copy(
            src_ref=half(ins[w], w, me, c), dst_ref=half(outs[w], w, me, c), send_sem=sems[0].at[3 * w + s],
            recv_sem=sems[1].at[3 * w + s], device_id=_chip_dev(k, c), device_id_type=MESH)

    def landed(outs, sems, w, s):
        x, y, c, me = _where()
        j = (me + 3 - s) % 4
        return pltpu.make_async_remote_copy(
            src_ref=half(outs[w], w, j, c), dst_ref=half(outs[w], w, j, c), send_sem=sems[0].at[3 * w + s],
            recv_sem=sems[1].at[3 * w + s], device_id=(x, y, 1 - c), device_id_type=MESH)

    def passed(outs, sems, w, s, cc):
        x, y, c, me = _where()
        j = (me + 3 - s) % 4
        return pltpu.make_async_remote_copy(
            src_ref=half(outs[w], w, j, cc), dst_ref=half(outs[w], w, j, cc), send_sem=sems[2].at[3 * w + s],
            recv_sem=sems[3].at[3 * w + s], device_id=(x, y, 1 - c), device_id_type=MESH)

    pairs = [(w, s) for w in range(n) for s in range(3)]

    def start(ins, outs, sems):
        for w, s in pairs:
            ici(ins, outs, sems, w, s).start()

    def finish(ins, outs, sems):
        _, _, c, _ = _where()
        for w, s in pairs:
            landed(outs, sems, w, s).wait_recv()
            passed(outs, sems, w, s, c).start()
        for w, s in pairs:
            passed(outs, sems, w, s, 1 - c).wait_recv()
        for w, s in pairs:
            ici(ins, outs, sems, w, s).wait_send()
            passed(outs, sems, w, s, c).wait_send()

    return _Payload(bufs, [jax.ShapeDtypeStruct(b.shape, b.dtype) for b in bufs], {w: w for w in range(n)},
                    [pltpu.SemaphoreType.DMA((3 * n,))] * 4, start, finish)


def _sym_payload(operands, outs, copies, n_copies):
    def start(ins, outs_, sems):
        for cp in copies(ins, outs_, sems[0], sems[1]):
            cp.start()

    def finish(ins, outs_, sems):
        for cp in copies(ins, outs_, sems[0], sems[1]):
            cp.wait()

    p = _Payload(operands, outs, {}, [pltpu.SemaphoreType.DMA((n_copies,))] * 2, start, finish)
    p.copies, p.n_copies = copies, n_copies
    return p


def _swap_payload(g4s):
    def copies(ins, outs, ss, rs):
        x, y, c, me = _where()
        cps = []
        for w, g in enumerate(g4s):
            h = g.shape[1] // 2
            cps.append(pltpu.make_async_remote_copy(
                src_ref=ins[w].at[:, pl.ds((1 - c) * h, h)], dst_ref=outs[w], send_sem=ss.at[w],
                recv_sem=rs.at[w], device_id=(x, y, 1 - c), device_id_type=MESH))
        return cps

    outs = [jax.ShapeDtypeStruct((4, g.shape[1] // 2, g.shape[2]), g.dtype) for g in g4s]
    return _sym_payload(g4s, outs, copies, len(g4s))


def _exchange_payload(pbs):
    def copies(ins, outs, ss, rs):
        x, y, c, me = _where()
        cps = []
        for w in range(len(pbs)):
            for s in range(3):
                k = (me + 1 + s) % 4
                cps.append(pltpu.make_async_remote_copy(
                    src_ref=ins[w].at[k], dst_ref=outs[w].at[2 - s], send_sem=ss.at[3 * w + s],
                    recv_sem=rs.at[3 * w + s], device_id=_chip_dev(k, c), device_id_type=MESH))
        return cps

    outs = [jax.ShapeDtypeStruct((3,) + p.shape[1:], p.dtype) for p in pbs]
    return _sym_payload(pbs, outs, copies, 3 * len(pbs))


HBM_REF = pl.BlockSpec(memory_space=pltpu.HBM)
SEM_REF = pl.BlockSpec(memory_space=pltpu.SEMAPHORE)
DATAFLOW = pltpu.SideEffectType.DATAFLOW_SIDE_EFFECTING


class _SemList:
    def __init__(self, refs):
        self.refs = refs

    @property
    def at(self):
        return self.refs


def _split_start(p, name):
    n_in, n_out, nc = len(p.operands), len(p.outs), p.n_copies
    lands = getattr(p, "lands", None) or [lax.empty(s.shape, s.dtype) for s in p.outs]

    def body(*refs):
        ins, lnd = refs[:n_in], refs[n_in:n_in + n_out]
        sems = refs[n_in + n_out:n_in + n_out + 2 * nc]
        for cp in p.copies(ins, lnd, _SemList(sems[:nc]), _SemList(sems[nc:])):
            cp.start()
        refs[-1][...] = jnp.zeros((8, 128), F32)

    res = pl.pallas_call(
        body, name=name,
        in_specs=[HBM_REF] * (n_in + n_out),
        out_specs=[SEM_REF] * (2 * nc) + [HBM_REF] * (n_in + n_out) + [VMEM_FULL],
        out_shape=([pltpu.SemaphoreType.DMA(())] * (2 * nc) + _hbm_out(p.operands) + _hbm_out(lands)
                   + [jax.ShapeDtypeStruct((8, 128), F32)]),
        input_output_aliases={i: 2 * nc + i for i in range(n_in + n_out)},
        compiler_params=pltpu.CompilerParams(has_side_effects=DATAFLOW),
    )(*_hbm(*p.operands, *lands))
    k = 2 * nc
    return list(res[:k]), list(res[k:k + n_in]), list(res[k + n_in:k + n_in + n_out]), res[-1]


def _split_wait(p, handle, after, name):
    sems, srcs, lands, _ = handle
    n_in, n_out, nc = len(srcs), len(lands), p.n_copies

    def body(*refs):
        ins, lnd = refs[:n_in], refs[n_in:n_in + n_out]
        sm = refs[n_in + n_out:n_in + n_out + 2 * nc]
        for cp in p.copies(ins, lnd, _SemList(sm[:nc]), _SemList(sm[nc:])):
            cp.wait_send()
            cp.wait_recv()

    res = pl.pallas_call(
        body, name=name,
        in_specs=[HBM_REF] * (n_in + n_out) + [SEM_REF] * (2 * nc) + [ANY] * len(after),
        out_specs=[HBM_REF] * (n_in + n_out), out_shape=_hbm_out(srcs) + _hbm_out(lands),
        input_output_aliases={i: i for i in range(n_in + n_out)},
        compiler_params=pltpu.CompilerParams(has_side_effects=DATAFLOW),
    )(*srcs, *lands, *sems, *after)
    return list(res[:n_in]), list(res[n_in:])


def _join_payload(halves):
    def copies(ins, outs, ss, rs):
        x, y, c, me = _where()
        return [pltpu.make_async_remote_copy(
            src_ref=ins[w], dst_ref=outs[w], send_sem=ss.at[w], recv_sem=rs.at[w],
            device_id=(x, y, 1 - c), device_id_type=MESH) for w in range(len(halves))]

    outs = [jax.ShapeDtypeStruct(a.shape, a.dtype) for a in halves]
    return _sym_payload(halves, outs, copies, len(halves))


def _allgather_payload(v):
    def copies(ins, outs, ss, rs):
        x, y, c, me = _where()
        lin = 4 * x + 2 * y + c
        cps = []
        for o in range(1, 8):
            t = (lin + o) % 8
            cps.append(pltpu.make_async_remote_copy(
                src_ref=ins[0], dst_ref=outs[0].at[lin], send_sem=ss.at[o - 1], recv_sem=rs.at[o - 1],
                device_id=(t // 4, (t // 2) % 2, t % 2), device_id_type=MESH))
        return cps

    p = _sym_payload([v], [jax.ShapeDtypeStruct((8,) + v.shape, v.dtype)], copies, 7)
    x, y, c, _ = _where()
    p.lands = [lax.dynamic_update_slice(jnp.zeros((8,) + v.shape, v.dtype), v[None], (4 * x + 2 * y + c, 0, 0))]
    return p


def _sum8(buf, token):
    _, P, C = buf.shape

    def body(b_ref, t_ref, o_ref):
        acc = b_ref[0]
        for d in range(1, 8):
            acc = acc + b_ref[d]
        o_ref[...] = acc

    return pl.pallas_call(
        body, name="sum8", in_specs=[VMEM_FULL, VMEM_FULL], out_specs=VMEM_FULL,
        out_shape=jax.ShapeDtypeStruct((P, C), F32),
        compiler_params=pltpu.CompilerParams(vmem_limit_bytes=32 << 20),
    )(buf, token)


def _row_tile(h):
    for t in (256, 176, 128, 64, 32, 16, 8):
        if h % t == 0:
            return t
    raise ValueError(h)


def _pair_sum(cmidx, g4, got, name):
    _, R, C = g4.shape
    h = R // 2
    th = _row_tile(h)

    def body(cm_ref, a_ref, b_ref, o_ref, ob_ref):
        sm = a_ref[...] + b_ref[...]
        ob_ref[...] = sm.astype(BF16)

        @pl.when(pl.program_id(1) == cm_ref[1])
        def _():
            o_ref[...] = sm

    blk = pl.BlockSpec((None, th, C), lambda i, k, cm: (k, i, 0))
    gs = pltpu.PrefetchScalarGridSpec(
        num_scalar_prefetch=1, grid=(h // th, 4),
        in_specs=[pl.BlockSpec((None, None, th, C), lambda i, k, cm: (k, cm[0], i, 0)), blk],
        out_specs=[pl.BlockSpec((th, C), lambda i, k, cm: (i, 0)), blk])
    return pl.pallas_call(
        body, name=name, grid_spec=gs,
        out_shape=_hbm_out([jax.ShapeDtypeStruct((h, C), F32), jax.ShapeDtypeStruct((4, h, C), BF16)]),
        compiler_params=_cp(32, 2),
    )(cmidx, *_hbm(g4.reshape(4, 2, h, C), got))


def _chip_sum(own, got, name):
    h, C = own.shape
    th = _row_tile(h)

    def body(a_ref, b_ref, o_ref):
        o_ref[...] = ((a_ref[...] + b_ref[0].astype(F32)) + b_ref[1].astype(F32)) + b_ref[2].astype(F32)

    return pl.pallas_call(
        body, name=name, grid=(h // th,),
        in_specs=[pl.BlockSpec((th, C), lambda i: (i, 0)), pl.BlockSpec((3, th, C), lambda i: (0, i, 0))],
        out_specs=pl.BlockSpec((th, C), lambda i: (i, 0)),
        out_shape=_hbm_out(jax.ShapeDtypeStruct((h, C), F32)),
        compiler_params=_cp(32, 1),
    )(*_hbm(own, got))


def _adamw_math(w, g, m, v):
    m2 = B1 * m + (1.0 - B1) * g
    v2 = B2 * v + (1.0 - B2) * (g * g)
    m_hat = m2 / (1.0 - B1 ** STEP)
    v_hat = v2 / (1.0 - B2 ** STEP)
    delta = -LR * (m_hat / (jnp.sqrt(v_hat) + EPS) + WD * w)
    return delta, m2, v2


def _adamw_pair(cidx, w, mine, theirs, m, v, token, name):
    R, C = w.shape
    h = R // 2
    tr = _row_tile(h)
    nh = h // tr

    def body(c_ref, w_ref, a_ref, b_ref, m_ref, v_ref, t_ref, g_ref, d_ref, mo_ref, vo_ref):
        own = (pl.program_id(0) // nh) == c_ref[0]
        g = jnp.where(own, a_ref[...], b_ref[...])
        d, m2, v2 = _adamw_math(w_ref[...], g, m_ref[...], v_ref[...])
        g_ref[...] = g
        d_ref[...] = d
        mo_ref[...] = m2
        vo_ref[...] = v2

    blk = pl.BlockSpec((tr, C), lambda i, c: (i, 0))
    mine_blk = pl.BlockSpec((tr, C), lambda i, c: (jnp.clip(i - c[0] * nh, 0, nh - 1), 0))
    theirs_blk = pl.BlockSpec((tr, C), lambda i, c: (jnp.clip(i - (1 - c[0]) * nh, 0, nh - 1), 0))
    gs = pltpu.PrefetchScalarGridSpec(
        num_scalar_prefetch=1, grid=(R // tr,),
        in_specs=[blk, mine_blk, theirs_blk, blk, blk, pl.BlockSpec((8, 128), lambda i, c: (0, 0))],
        out_specs=[blk] * 4)
    return pl.pallas_call(
        body, name=name, grid_spec=gs, out_shape=_hbm_out([jax.ShapeDtypeStruct((R, C), F32)] * 4),
        compiler_params=_cp(32, 1),
    )(cidx, *_hbm(w, mine, theirs, m, v), token)


def _adamw(w, g, m, v, name):
    R, C = w.shape
    tr = _row_tile(R)

    def body(w_ref, g_ref, m_ref, v_ref, d_ref, mo_ref, vo_ref):
        d, m2, v2 = _adamw_math(w_ref[...], g_ref[...], m_ref[...], v_ref[...])
        d_ref[...] = d
        mo_ref[...] = m2
        vo_ref[...] = v2

    blk = pl.BlockSpec((tr, C), lambda i: (i, 0))
    return pl.pallas_call(
        body, name=name, grid=(R // tr,), in_specs=[blk] * 4, out_specs=[blk] * 3,
        out_shape=_hbm_out([jax.ShapeDtypeStruct((R, C), F32)] * 3),
        compiler_params=_cp(32, 1),
    )(*_hbm(w, g, m, v))


def _pack(arrs):
    flat = jnp.concatenate([a.reshape(-1).astype(F32) for a in arrs])
    rows = -(-flat.shape[0] // 1024)
    rows = -(-rows // 8) * 8
    return jnp.pad(flat, (0, rows * 1024 - flat.shape[0])).reshape(rows, 1024)


def _unpack(packed, shapes):
    flat = packed.reshape(-1)
    out, off = [], 0
    for s in shapes:
        n = math.prod(s)
        out.append(flat[off:off + n].reshape(s))
        off += n
    return out


BIG = ["ffn1_w_in", "ffn1_w_out", "mix_w_in", "conv_w_out", "ssm_w_glu", "mix_w_out",
       "ffn2_w_in", "ffn2_w_out", "ple_w_in", "ple_w_gate"]
SMALL = ["ln1_g", "ln1_b", "conv_w", "conv_b", "ssm_lam_re", "ssm_lam_im", "ssm_log_step", "ssm_b_re", "ssm_b_im",
         "ssm_c_re", "ssm_c_im", "ssm_d", "ln2_g", "ln2_b", "ln3_g", "ln3_b", "ln4_g", "ln4_b"]
WEIGHTS = ["ffn1_w_in", "ffn1_w_out", "ln1_g", "ln1_b", "mix_w_in", "conv_w", "conv_b", "conv_w_out",
           "ssm_lam_re", "ssm_lam_im", "ssm_log_step", "ssm_b_re", "ssm_b_im", "ssm_c_re", "ssm_c_im", "ssm_d",
           "ssm_w_glu", "mix_w_out", "ln2_g", "ln2_b", "ffn2_w_in", "ffn2_w_out", "ln3_g", "ln3_b",
           "ple_w_in", "ple_w_gate", "ln4_g", "ln4_b"]


class _NoComm:
    def __init__(self, W):
        self.W, self.G, self.raw, self.done = dict(W), {}, None, {}

    def carry(self, name):
        return ()

    def landed(self, name, got):
        pass

    def grad(self, name, g4):
        self.G[name] = g4

    def small(self, raw):
        self.raw = raw


def _s5_operands(sp):
    abr, abi, bbr, bbi = _zoh(sp["ssm_lam_re"], sp["ssm_lam_im"], sp["ssm_log_step"], sp["ssm_b_re"], sp["ssm_b_im"])
    return (_wb_blocks(bbr), _wb_blocks(bbi), _wc_blocks(sp["ssm_c_re"]), _wc_blocks(-sp["ssm_c_im"]),
            abr.reshape(1, LANES), abi.reshape(1, LANES), sp["ssm_d"].reshape(1, SSM))


def _local_step(x, p, target, sp, sched, tm_ffn, tm_mix, ops=None):
    W = sched.W
    wb_re, wb_im, wc_re4, wc_im4, a_re, a_im, dvec = ops if ops is not None else _s5_operands(sp)

    def run(fn, name, *args, **kw):
        outs, got = fn(*args, comm=sched.carry(name), **kw)
        sched.landed(name, got)
        sched.done[name] = outs[0]
        return outs

    def dw(name, wname, a, b, tk, tn, shape4, shard_cols=None, interleaved=False):
        out, got = _mm_tn(a, b, tk, tn, name, shard_cols=shard_cols, interleaved=interleaved,
                          comm=sched.carry(name))
        sched.landed(name, got)
        sched.done[name] = out
        sched.grad(wname, out.reshape(shape4))

    xb = x.astype(BF16)
    h1, r1, x1, x1b = run(_ffn_fwd, "ffn1_fwd", x, xb, W["ffn1_w_in"], W["ffn1_w_out"].reshape(2, FFH, D),
                          sp["ln1_g"], sp["ln1_b"], tm_ffn, "ffn1_fwd")
    conv_w = W["conv_w"][:, 0:3, :].transpose(1, 0, 2).reshape(3, CONV)
    pc, z_b, yin_b, su, su_b, g_conv, g_ssm, y_conv = run(
        _mix_fwd_a, "mix_fwd_a", x1b, W["mix_w_in"], conv_w, sp["conv_b"], W["conv_w_out"], tm_mix)
    st_re, st_im = run(_s5_scan_fwd, "s5_scan_fwd", su_b, wb_re, wb_im, a_re, a_im)
    w_mo = W["mix_w_out"].reshape(D, D)
    s, sg_b, ga, gb, merged_b, r2, x2, x2b = run(
        _mix_fwd_b, "mix_fwd_b", st_re, st_im, wc_re4, wc_im4, su, dvec, W["ssm_w_glu"], g_conv, g_ssm, y_conv,
        w_mo, x1, sp["ln2_g"], sp["ln2_b"], tm_mix)
    w2o2 = W["ffn2_w_out"].reshape(2, FFH, D)
    h2, r3, x3, x3b = run(_ffn_fwd, "ffn2_fwd", x2, x2b, W["ffn2_w_in"], w2o2, sp["ln3_g"], sp["ln3_b"], tm_ffn,
                          "ffn2_fwd")
    loss_part, dx3, p_b, dpw_b, dgt_b, dg4, db4 = _ple_loss(
        x3, x3b, p, W["ple_w_in"], W["ple_w_gate"].reshape(D, D), sp["ln4_g"], sp["ln4_b"], target, tm_mix)

    dw("dw_ple_gate", "ple_w_gate", x3b, dgt_b, 512, 1024, (4, 256, D))
    dw("dw_ple_in", "ple_w_in", p_b, dpw_b, 256, 256, (4, 256, 256), shard_cols=256)
    dx2, dh2, a2_b, df2_b, dg3, db3 = run(_ffn_bwd, "ffn2_bwd", dx3, r3, sp["ln3_g"], h2, W["ffn2_w_in"], w2o2,
                                          tm_mix, "ffn2_bwd")
    dw("dw_ffn2_in", "ffn2_w_in", x2b, dh2, 512, FFH, (4, D, FFH), shard_cols=FFH, interleaved=True)
    dw("dw_ffn2_out", "ffn2_w_out", a2_b, df2_b, FFH, 1024, (4, FF // 4, D))
    (dres, dmix_b, dgl_b, ds_b, du_dir, gs_re, gs_im, dyc_b, dproj, dg2, db2, dd) = run(
        _mix_bwd_b, "mix_bwd_b", dx2, r2, sp["ln2_g"], w_mo, g_conv, g_ssm, y_conv, ga, gb, s, su, dvec,
        W["ssm_w_glu"], wc_re4, wc_im4, tm_mix)
    dw("dw_mix_out", "mix_w_out", merged_b, dmix_b, 512, 1024, (4, 256, D))
    dw("dw_glu", "ssm_w_glu", sg_b, dgl_b, 512, 512, (4, SSM, 512), shard_cols=512)
    dsu_ssm, dwb_re, dwb_im, dwc_re, dwc_im, da_re, da_im = run(
        _s5_scan_bwd, "s5_scan_bwd", gs_re, gs_im, st_re, st_im, su_b, ds_b, wb_re, wb_im, a_re, a_im)
    dw("dw_conv_out", "conv_w_out", yin_b, dyc_b, 512, 256, (4, CONV, 256), shard_cols=256)
    dproj, dx1, dcw8, dcb = run(_mix_bwd_a, "mix_bwd_a", dyc_b, W["conv_w_out"], pc, z_b, conv_w, dsu_ssm,
                                du_dir, dproj, dres, W["mix_w_in"], tm_mix)
    dw("dw_mix_in", "mix_w_in", x1b, dproj, 512, 1024, (4, D, D), shard_cols=1024)
    dx0, dh1, a1_b, df1_b, dg1, db1 = run(_ffn_bwd, "ffn1_bwd", dx1, r1, sp["ln1_g"], h1, W["ffn1_w_in"],
                                          W["ffn1_w_out"].reshape(2, FFH, D), tm_mix, "ffn1_bwd")
    sched.small(dict(
        ln1_g=dg1, ln1_b=db1, ln2_g=dg2, ln2_b=db2, ln3_g=dg3, ln3_b=db3, ln4_g=dg4, ln4_b=db4,
        conv_w=dcw8[0:3], conv_b=dcb,
        a_re=da_re.reshape(GROUPS, STATE), a_im=da_im.reshape(GROUPS, STATE),
        bb_re=_wb_diag(dwb_re), bb_im=_wb_diag(dwb_im),
        ssm_c_re=_wc_diag(dwc_re), ssm_c_im=-_wc_diag(dwc_im), ssm_d=dd.reshape(GROUPS, 16),
        loss=loss_part[0:1, 0]))
    dw("dw_ffn1_in", "ffn1_w_in", xb, dh1, 512, FFH, (4, D, FFH), shard_cols=FFH, interleaved=True)
    dw("dw_ffn1_out", "ffn1_w_out", a1_b, df1_b, FFH, 1024, (4, FF // 4, D))
    return loss_part[0, 0], dx0


RAW_ORDER = ["ln1_g", "ln1_b", "ln2_g", "ln2_b", "ln3_g", "ln3_b", "ln4_g", "ln4_b", "conv_w", "conv_b",
             "a_re", "a_im", "bb_re", "bb_im", "ssm_c_re", "ssm_c_im", "ssm_d", "loss"]

GATHER_FIRST = ["ffn1_w_in", "ffn1_w_out"]
GATHER_AT = {"ffn1_fwd": ["mix_w_in", "conv_w_out", "conv_w"], "mix_fwd_a": ["ssm_w_glu", "mix_w_out"],
             "s5_scan_fwd": ["ffn2_w_in"], "mix_fwd_b": ["ffn2_w_out"], "ffn2_fwd": ["ple_w_in", "ple_w_gate"]}
REDUCE_GROUP = {"ple": ["ple_w_gate", "ple_w_in"], "ffn2": ["ffn2_w_in", "ffn2_w_out"],
                "mix": ["mix_w_out", "ssm_w_glu", "conv_w_out", "mix_w_in"], "ffn1": ["ffn1_w_in", "ffn1_w_out"]}
REDUCE_AT = {"ffn2_bwd": [("swap", "ple")], "dw_ffn2_in": [("exchange", "ple")],
             "mix_bwd_b": [("swap", "ffn2"), ("join", "ple")],
             "mix_bwd_a": [("join", "ffn2")], "ffn1_bwd": [("swap", "mix")]}
BEGIN_AT = {"dw_mix_out": [("exchange", "ffn2")], "dw_ffn1_in": [("small", None), ("exchange", "mix")]}
BEHIND = {"dw_glu": [("exchange", "ffn2")], "s5_scan_bwd": [("exchange", "ffn2")]}
END_AT = {"mix_bwd_a": [("exchange", "ffn2", ["dw_mix_out", "dw_glu", "s5_scan_bwd"])]}
LAST_GROUP = "ffn1"


class _Sched:
    def __init__(self, cmidx):
        self.bufs, self.cmidx = {}, cmidx
        self.W, self.G, self.raw, self.small_buf = {}, {}, None, None
        self.got1, self.p32, self.pbf, self.got2, self.half, self.theirs = {}, {}, {}, {}, {}, {}
        self._open, self._split, self.done = [], {}, {}

    def first_begin(self, bufs):
        self.bufs.update(bufs)
        p = _gather_ici_payload([bufs[n] for n in GATHER_FIRST])
        self._first = (p, _split_start(p, "gather_first_start"))
        return self._first[1][3]

    def first_end(self, bufs, after):
        self.bufs.update(bufs)
        p, handle = self._first
        _, landed = _split_wait(p, handle, after, "gather_first_wait")
        (outs,) = _comm_call("gather_first_pass", [_gather_pass_payload(landed)])
        self.W.update(zip(GATHER_FIRST, outs))

    def _payload(self, stage, key):
        if stage == "gather":
            return _gather_payload([self.bufs[n] for n in key])
        if stage == "small":
            return _allgather_payload(_pack([self.raw[k] for k in RAW_ORDER]))
        names = REDUCE_GROUP[key]
        if stage == "swap":
            return _swap_payload([self.G[n] for n in names])
        if stage == "exchange":
            for n in names:
                self.p32[n], self.pbf[n] = _pair_sum(self.cmidx, self.G[n], self.got1[n], "pair_sum_" + n)
            return _exchange_payload([self.pbf[n] for n in names])
        for n in names:
            self.half[n] = _chip_sum(self.p32[n], self.got2[n], "chip_sum_" + n)
        return _join_payload([self.half[n] for n in names])

    def _store(self, stages, got):
        for (stage, key), outs in zip(stages, got):
            if stage == "gather":
                self.W.update(zip(key, outs))
            elif stage == "small":
                self.small_buf = outs[0]
            else:
                {"swap": self.got1, "exchange": self.got2, "join": self.theirs}[stage].update(
                    zip(REDUCE_GROUP[key], outs))

    def _standalone(self, name, stages):
        self._store(stages, _comm_call(name, [self._payload(s, k) for s, k in stages]))

    def carry(self, name):
        for stage, key, behind in END_AT.get(name, []):
            self._end(stage, key, [self.done[b] for b in behind])
        tokens = [self._begin(stage, key) for stage, key in BEGIN_AT.get(name, [])]
        tokens += [self._split[sk][1][3] for sk in BEHIND.get(name, [])]
        self._open = [("gather", GATHER_AT[name])] if name in GATHER_AT else []
        self._open += REDUCE_AT.get(name, [])
        comm = [self._payload(s, k) for s, k in self._open]
        if tokens:
            comm.append(_Payload(tokens, [], {}, [], lambda *a: None, lambda *a: None))
        return tuple(comm)

    def landed(self, name, got):
        self._store(self._open, got)

    def grad(self, name, g4):
        self.G[name] = g4

    def small(self, raw):
        self.raw = raw

    def _begin(self, stage, key):
        p = self._payload(stage, key)
        self._split[stage, key] = (p, _split_start(p, "%s_%s_start" % (stage, key)))
        return self._split[stage, key][1][3]

    def _end(self, stage, key, after):
        p, handle = self._split.pop((stage, key))
        srcs, lands = _split_wait(p, handle, after, "%s_%s_wait" % (stage, key))
        if stage == "swap":
            self.G.update(zip(REDUCE_GROUP[key], srcs))
        self._store([(stage, key)], [lands])

    def tail_begin(self):
        return self._begin("swap", LAST_GROUP)

    def tail_mid(self, after):
        self._end("swap", LAST_GROUP, after)
        token = self._begin("exchange", LAST_GROUP)
        self._end("small", None, [token])
        self._end("exchange", "mix", [token])
        self._standalone("reduce_tail_join_mix", [("join", "mix")])
        return token

    def tail_end(self, after):
        self._end("exchange", LAST_GROUP, after)
        self._standalone("reduce_tail_join", [("join", LAST_GROUP)])


def _small_grads(raw_sum, sp):
    _, vjp = jax.vjp(_zoh, sp["ssm_lam_re"], sp["ssm_lam_im"], sp["ssm_log_step"], sp["ssm_b_re"], sp["ssm_b_im"])
    d_lre, d_lim, d_ls, d_bre, d_bim = vjp((raw_sum["a_re"], raw_sum["a_im"], raw_sum["bb_re"], raw_sum["bb_im"]))
    g = {k: raw_sum[k] for k in ("ln1_g", "ln1_b", "ln2_g", "ln2_b", "ln3_g", "ln3_b", "ln4_g", "ln4_b",
                                 "conv_w", "conv_b", "ssm_c_re", "ssm_c_im", "ssm_d")}
    g.update(ssm_lam_re=d_lre, ssm_lam_im=d_lim, ssm_log_step=d_ls, ssm_b_re=d_bre, ssm_b_im=d_bim)
    return g


def kernel(x, p, ffn1_w_in, ffn1_w_out, ln1_g, ln1_b, mix_w_in, conv_w, conv_b, conv_w_out, ssm_lam_re, ssm_lam_im, ssm_log_step, ssm_b_re, ssm_b_im, ssm_c_re, ssm_c_im, ssm_d, ssm_w_glu, mix_w_out, ln2_g, ln2_b, ffn2_w_in, ffn2_w_out, ln3_g, ln3_b, ple_w_in, ple_w_gate, ln4_g, ln4_b, loss_target, m_ffn1_w_in, m_ffn1_w_out, m_ln1_g, m_ln1_b, m_mix_w_in, m_conv_w, m_conv_b, m_conv_w_out, m_ssm_lam_re, m_ssm_lam_im, m_ssm_log_step, m_ssm_b_re, m_ssm_b_im, m_ssm_c_re, m_ssm_c_im, m_ssm_d, m_ssm_w_glu, m_mix_w_out, m_ln2_g, m_ln2_b, m_ffn2_w_in, m_ffn2_w_out, m_ln3_g, m_ln3_b, m_ple_w_in, m_ple_w_gate, m_ln4_g, m_ln4_b, v_ffn1_w_in, v_ffn1_w_out, v_ln1_g, v_ln1_b, v_mix_w_in, v_conv_w, v_conv_b, v_conv_w_out, v_ssm_lam_re, v_ssm_lam_im, v_ssm_log_step, v_ssm_b_re, v_ssm_b_im, v_ssm_c_re, v_ssm_c_im, v_ssm_d, v_ssm_w_glu, v_mix_w_out, v_ln2_g, v_ln2_b, v_ffn2_w_in, v_ffn2_w_out, v_ln3_g, v_ln3_b, v_ple_w_in, v_ple_w_gate, v_ln4_g, v_ln4_b):
    args = dict(locals())
    w = {n: args[n] for n in WEIGHTS}
    m = {n: args["m_" + n] for n in WEIGHTS}
    v = {n: args["v_" + n] for n in WEIGHTS}
    _, _, c, me = _where()
    cidx = jnp.stack([c, me]).astype(jnp.int32)
    meidx = jnp.reshape(me, (1,)).astype(jnp.int32)

    sched = _Sched(cidx)
    token = sched.first_begin({n: _slot_cast(meidx, w[n][0], BF16, "cast_" + n) for n in GATHER_FIRST})
    rest = {n: _slot_cast(meidx, w[n][0], BF16, "cast_" + n, (token,)) for n in BIG if n not in GATHER_FIRST}
    rest["conv_w"] = _slot_cast(meidx, jnp.pad(conv_w[0], ((0, 13), (0, 0))), F32, "cast_conv_w", (token,))
    sp = {n: (w[n] if w[n].ndim == 2 and n != "ssm_log_step" else w[n][0]) for n in SMALL if n != "conv_w"}
    ops = _s5_operands({**sp, "ssm_lam_re": sp["ssm_lam_re"] + token[0, 0]})
    sched.first_end(rest, list(rest.values()) + list(ops))
    loss_part, dx0 = _local_step(x[0], p[0, 0], loss_target[0], sp, sched, 256, 256, ops)
    out_g, out_d, out_m, out_v = {}, {}, {}, {}

    def big_adamw(names, token):
        for n in names:
            g, dl, mn, vn = _adamw_pair(cidx, w[n][0], sched.half[n], sched.theirs[n], m[n][0], v[n][0], token,
                                        "adamw_" + n)
            out_g[n], out_d[n], out_m[n], out_v[n] = g[None], dl[None], mn[None], vn[None]

    first = REDUCE_GROUP["ple"] + ["ffn2_w_in"]
    big_adamw(first, sched.tail_begin())
    token = sched.tail_mid([out_v[n] for n in first])
    big_adamw(["ffn2_w_out"], token)

    raw_shapes = [sched.raw[k].shape for k in RAW_ORDER]
    raw_sum = dict(zip(RAW_ORDER, _unpack(_sum8(sched.small_buf, token), raw_shapes)))
    loss = raw_sum["loss"][0]
    sg = _small_grads(raw_sum, sp)
    sg["conv_w"] = lax.dynamic_slice_in_dim(sg["conv_w"], me * 128, 128, axis=1)
    small_shapes = [w[n].shape for n in SMALL]
    gp = _pack([sg[n] for n in SMALL])
    d_s, m_s, v_s = _adamw(_pack([w[n] for n in SMALL]), gp, _pack([m[n] for n in SMALL]),
                           _pack([v[n] for n in SMALL]), "adamw_small")

    for n, a, b_, c_, d_ in zip(SMALL, _unpack(gp, small_shapes), _unpack(d_s, small_shapes),
                                _unpack(m_s, small_shapes), _unpack(v_s, small_shapes)):
        out_g[n], out_d[n], out_m[n], out_v[n] = a, b_, c_, d_
    big_adamw(REDUCE_GROUP["mix"], token)
    sched.tail_end([d_s, out_v["ffn2_w_out"]] + [out_v[n] for n in REDUCE_GROUP["mix"]])
    big_adamw(REDUCE_GROUP[LAST_GROUP], token)

    return (loss, dx0[None], *[out_g[n] for n in WEIGHTS], *[out_d[n] for n in WEIGHTS],
            *[out_m[n] for n in WEIGHTS], *[out_v[n] for n in WEIGHTS])
```

```python
import functools
import math

import jax
import jax.numpy as jnp
import numpy as np
from jax import lax
from jax.experimental import pallas as pl
from jax.experimental.pallas import tpu as pltpu

F32, BF16 = jnp.float32, jnp.bfloat16
D = 1024
FF = 2816
FFH = FF // 2
CONV = 512
SSM = 512
GROUPS = 32
STATE = 64
LANES = GROUPS * STATE
SCAN_W = 128
SCAN_PER = 512 // SCAN_W
SCAN_GR = SCAN_W // STATE
SCAN_R = 256
TOKEN_TILE = 256
ALPHA = 2.0 ** 0.25
LN_EPS = 1e-5
GELU_C = math.sqrt(2.0 / math.pi)
B1, B2, LR, EPS, WD, STEP = 0.9, 0.999, 0.001, 1e-8, 0.01, 10
MESH = pl.DeviceIdType.MESH
ANY = pl.BlockSpec(memory_space=pl.ANY)
VMEM_FULL = pl.BlockSpec(memory_space=pltpu.VMEM)


def _cp(vmem_mb=48, n_axes=1):
    return pltpu.CompilerParams(vmem_limit_bytes=vmem_mb << 20,
                                dimension_semantics=("arbitrary",) * n_axes)


def _hbm(*arrs):
    return [pltpu.with_memory_space_constraint(a, pltpu.HBM) for a in arrs]


def _hbm_out(shapes):
    if isinstance(shapes, (list, tuple)):
        return [pltpu.HBM(s.shape, s.dtype) for s in shapes]
    return pltpu.HBM(shapes.shape, shapes.dtype)


def _nn(a, b):
    return jnp.dot(a, b, preferred_element_type=F32)


def _nt(a, b):
    return lax.dot_general(a, b, (((1,), (1,)), ((), ())), preferred_element_type=F32)


def _tn(a, b):
    return lax.dot_general(a, b, (((0,), (0,)), ((), ())), preferred_element_type=F32)


def _sig(v):
    return jax.nn.sigmoid(v)


def _ln_stats(r):
    mu = jnp.mean(r, axis=-1, keepdims=True)
    xc = r - mu
    var = jnp.mean(xc * xc, axis=-1, keepdims=True)
    rstd = lax.rsqrt(var + LN_EPS)
    return xc * rstd, rstd


def _ln_bwd(dy, r, g):
    xhat, rstd = _ln_stats(r)
    dyg = dy * g
    m1 = jnp.mean(dyg, axis=-1, keepdims=True)
    m2 = jnp.mean(dyg * xhat, axis=-1, keepdims=True)
    return rstd * (dyg - m1 - xhat * m2), xhat


def _rowsum(v):
    return jnp.sum(v, axis=0, keepdims=True)


class _Payload:
    def __init__(self, operands, outs, aliases, sems, start, finish):
        self.operands, self.outs, self.aliases, self.sems = list(operands), list(outs), dict(aliases), list(sems)
        self.start, self.finish = start, finish


def _split(flat, comm, attr):
    out, i = [], 0
    for p in comm:
        n = len(getattr(p, attr))
        out.append(list(flat[i:i + n]))
        i += n
    return out


def _run_comm(comm, which, cin, cout, csem):
    for p, a, b, s in zip(comm, _split(cin, comm, "operands"), _split(cout, comm, "outs"), _split(csem, comm, "sems")):
        getattr(p, which)(a, b, s)


def _pcall(body, *, name, grid, in_specs, out_specs, out_shape, operands, scratch=(), vmem_mb=48, aliases=None,
           comm=()):
    ni, no, ns = len(in_specs), len(out_specs), len(scratch)
    c_ops = [a for p in comm for a in p.operands]
    c_outs = [s for p in comm for s in p.outs]
    c_sems = [s for p in comm for s in p.sems]
    io = dict(aliases or {})
    off_i, off_o = ni, no
    for p in comm:
        for a, b in p.aliases.items():
            io[off_i + a] = off_o + b
        off_i += len(p.operands)
        off_o += len(p.outs)

    def wrapped(*refs):
        ins, cin = refs[:ni], refs[ni:ni + len(c_ops)]
        o0 = ni + len(c_ops)
        outs, cout = refs[o0:o0 + no], refs[o0 + no:o0 + no + len(c_outs)]
        s0 = o0 + no + len(c_outs)
        scr, csem = refs[s0:s0 + ns], refs[s0 + ns:]
        if comm:
            first = functools.reduce(jnp.logical_and, [pl.program_id(a) == 0 for a in range(len(grid))])
            pl.when(first)(lambda: _run_comm(comm, "start", cin, cout, csem))
        body(*ins, *outs, *scr)
        if comm:
            last = functools.reduce(jnp.logical_and, [pl.program_id(a) == grid[a] - 1 for a in range(len(grid))])
            pl.when(last)(lambda: _run_comm(comm, "finish", cin, cout, csem))

    res = pl.pallas_call(
        wrapped, name=name, grid=grid,
        in_specs=list(in_specs) + [ANY] * len(c_ops), out_specs=list(out_specs) + [ANY] * len(c_outs),
        out_shape=_hbm_out(list(out_shape) + c_outs), scratch_shapes=list(scratch) + c_sems,
        input_output_aliases=io,
        compiler_params=pltpu.CompilerParams(vmem_limit_bytes=vmem_mb << 20,
                                             dimension_semantics=("arbitrary",) * len(grid),
                                             has_side_effects=bool(c_sems)),
    )(*_hbm(*operands, *c_ops))
    return list(res[:no]), _split(res[no:], comm, "outs")


def _comm_call(name, comm):
    c_ops = [a for p in comm for a in p.operands]
    c_outs = [s for p in comm for s in p.outs]
    c_sems = [s for p in comm for s in p.sems]
    io, off_i, off_o = {}, 0, 0
    for p in comm:
        for a, b in p.aliases.items():
            io[off_i + a] = off_o + b
        off_i += len(p.operands)
        off_o += len(p.outs)

    def body(*refs):
        cin, cout = refs[:len(c_ops)], refs[len(c_ops):len(c_ops) + len(c_outs)]
        csem = refs[len(c_ops) + len(c_outs):]
        _run_comm(comm, "start", cin, cout, csem)
        _run_comm(comm, "finish", cin, cout, csem)

    res = pl.pallas_call(
        body, name=name, in_specs=[ANY] * len(c_ops), out_specs=[ANY] * len(c_outs), out_shape=_hbm_out(c_outs),
        scratch_shapes=c_sems, input_output_aliases=io,
        compiler_params=pltpu.CompilerParams(has_side_effects=True),
    )(*_hbm(*c_ops))
    return _split(res, comm, "outs")


def _ffn_fwd(x, w_in4, w_out2, g, b, tm, name, comm=()):
    T = x.shape[0]

    def body(x_ref, win_ref, wo_ref, g_ref, b_ref, h_ref, r_ref, xo_ref, xob_ref, xib_ref):
        xf = x_ref[...]
        xv = xf.astype(BF16)
        xib_ref[...] = xv
        acc = ALPHA * xf
        for k in range(2):
            gt = _nn(xv, win_ref[k])
            up = _nn(xv, win_ref[k + 2])
            a = (gt * _sig(gt) * up).astype(BF16)
            h_ref[:, 2 * k * FFH:(2 * k + 1) * FFH] = gt.astype(BF16)
            h_ref[:, (2 * k + 1) * FFH:(2 * k + 2) * FFH] = up.astype(BF16)
            acc = acc + 0.5 * _nn(a, wo_ref[k])
        xhat, _ = _ln_stats(acc)
        xo = xhat * g_ref[...] + b_ref[...]
        r_ref[...] = acc
        xo_ref[...] = xo
        xob_ref[...] = xo.astype(BF16)

    tok = pl.BlockSpec((tm, D), lambda i: (i, 0))
    vec = pl.BlockSpec((1, D), lambda i: (0, 0))
    return _pcall(
        body, name=name, grid=(T // tm,),
        in_specs=[tok,
                  pl.BlockSpec((4, D, FFH), lambda i: (0, 0, 0), pipeline_mode=pl.Buffered(1)),
                  pl.BlockSpec((2, FFH, D), lambda i: (0, 0, 0), pipeline_mode=pl.Buffered(1)),
                  vec, vec],
        out_specs=[pl.BlockSpec((tm, 2 * FF), lambda i: (i, 0)), tok, tok, tok, tok],
        out_shape=[jax.ShapeDtypeStruct((T, 2 * FF), BF16), jax.ShapeDtypeStruct((T, D), F32),
                   jax.ShapeDtypeStruct((T, D), F32), jax.ShapeDtypeStruct((T, D), BF16),
                   jax.ShapeDtypeStruct((T, D), BF16)],
        vmem_mb=58, comm=comm, operands=(x, w_in4, w_out2, g, b))


def _ffn_bwd(dy, r, g, h, w_in4, w_out2, tm, name, comm=()):
    T = dy.shape[0]

    def body(dy_ref, r_ref, g_ref, h_ref, win_ref, wo_ref, dx_ref, dh_ref, a_ref, df_ref, dg_ref, db_ref):
        i = pl.program_id(0)
        dyv = dy_ref[...]
        dr, xhat = _ln_bwd(dyv, r_ref[...], g_ref[...])
        dg_ref[...] = jnp.where(i == 0, 0.0, dg_ref[...]) + _rowsum(dyv * xhat)
        db_ref[...] = jnp.where(i == 0, 0.0, db_ref[...]) + _rowsum(dyv)
        dfb = (0.5 * dr).astype(BF16)
        df_ref[...] = dfb
        acc = ALPHA * dr
        for k in range(2):
            da = _nt(dfb, wo_ref[k])
            gt = h_ref[:, 2 * k * FFH:(2 * k + 1) * FFH].astype(F32)
            up = h_ref[:, (2 * k + 1) * FFH:(2 * k + 2) * FFH].astype(F32)
            sg = _sig(gt)
            silu = gt * sg
            dgate = (da * up * (sg * (1.0 + gt * (1.0 - sg)))).astype(BF16)
            dup = (da * silu).astype(BF16)
            a_ref[:, k * FFH:(k + 1) * FFH] = (silu * up).astype(BF16)
            dh_ref[:, 2 * k * FFH:(2 * k + 1) * FFH] = dgate
            dh_ref[:, (2 * k + 1) * FFH:(2 * k + 2) * FFH] = dup
            acc = acc + _nt(dgate, win_ref[k]) + _nt(dup, win_ref[k + 2])
        dx_ref[...] = acc

    tok = pl.BlockSpec((tm, D), lambda i: (i, 0))
    vec = pl.BlockSpec((1, D), lambda i: (0, 0))
    wide = pl.BlockSpec((tm, 2 * FF), lambda i: (i, 0))
    return _pcall(
        body, name=name, grid=(T // tm,),
        in_specs=[tok, tok, vec, wide,
                  pl.BlockSpec((4, D, FFH), lambda i: (0, 0, 0), pipeline_mode=pl.Buffered(1)),
                  pl.BlockSpec((2, FFH, D), lambda i: (0, 0, 0), pipeline_mode=pl.Buffered(1))],
        out_specs=[tok, wide, pl.BlockSpec((tm, FF), lambda i: (i, 0)), tok, vec, vec],
        out_shape=[jax.ShapeDtypeStruct((T, D), F32), jax.ShapeDtypeStruct((T, 2 * FF), BF16),
                   jax.ShapeDtypeStruct((T, FF), BF16), jax.ShapeDtypeStruct((T, D), BF16),
                   jax.ShapeDtypeStruct((1, D), F32), jax.ShapeDtypeStruct((1, D), F32)],
        vmem_mb=58, comm=comm, operands=(dy, r, g, h, w_in4, w_out2))


def _mm_tn(a, b, tk, tn, name, shard_cols=None, interleaved=False, comm=()):
    T, K = a.shape
    N = b.shape[1]

    def body(a_ref, b_ref, o_ref):
        o_ref[...] = _tn(a_ref[...], b_ref[...])

    if shard_cols is None:
        out_shape = jax.ShapeDtypeStruct((K, N), F32)
        out_spec = pl.BlockSpec((tk, tn), lambda ki, nj: (ki, nj))
    else:
        per = shard_cols // tn

        def shard(nj):
            blk = nj // per
            return (blk % 2) * 2 + blk // 2 if interleaved else blk

        out_shape = jax.ShapeDtypeStruct((N // shard_cols, K, shard_cols), F32)
        out_spec = pl.BlockSpec((None, tk, tn), lambda ki, nj: (shard(nj), ki, nj % per))
    (out,), got = _pcall(
        body, name=name, grid=(K // tk, N // tn),
        in_specs=[pl.BlockSpec((T, tk), lambda ki, nj: (0, ki)), pl.BlockSpec((T, tn), lambda ki, nj: (0, nj))],
        out_specs=[out_spec], out_shape=[out_shape], comm=comm, operands=(a, b))
    return out, got


def _mix_fwd_a(xb, w_mix4, conv_w, conv_b, w_co4, tm, comm=()):
    T = xb.shape[0]

    def body(xb_ref, w_ref, cw_ref, cb_ref, wco_ref,
             pc_ref, z_ref, yin_ref, su_ref, sub_ref, gc_ref, gs_ref, yc_ref, qbuf):
        @pl.when(pl.program_id(0) == 0)
        def _():
            qbuf[pl.ds(0, 8), :] = jnp.zeros((8, CONV), F32)

        xv = xb_ref[...]
        p0 = _nn(xv, w_ref[0])
        p1 = _nn(xv, w_ref[1])
        gc_ref[...] = _nn(xv, w_ref[2]).astype(BF16)
        gs_ref[...] = _nn(xv, w_ref[3]).astype(BF16)
        cbv, ccv = p0[:, :CONV], p0[:, CONV:]
        chv, suv = p1[:, :CONV], p1[:, CONV:]
        q = ccv * chv
        qbuf[pl.ds(8, tm), :] = q
        cw = cw_ref[...]
        z = (cw[2:3] * q + cw[1:2] * qbuf[pl.ds(7, tm), :] + cw[0:1] * qbuf[pl.ds(6, tm), :]
             + cb_ref[...])
        qbuf[pl.ds(0, 8), :] = q[tm - 8:tm]
        yin = (cbv * z).astype(BF16)
        pc_ref[:, 0:CONV] = cbv.astype(BF16)
        pc_ref[:, CONV:2 * CONV] = ccv.astype(BF16)
        pc_ref[:, 2 * CONV:3 * CONV] = chv.astype(BF16)
        z_ref[...] = z.astype(BF16)
        yin_ref[...] = yin
        su_ref[...] = suv
        sub_ref[...] = suv.astype(BF16)
        for k in range(4):
            yc_ref[:, 256 * k:256 * (k + 1)] = _nn(yin, wco_ref[k]).astype(BF16)

    def tok(n):
        return pl.BlockSpec((tm, n), lambda i: (i, 0))

    def full(shape):
        return pl.BlockSpec(shape, lambda i: (0,) * len(shape))

    return _pcall(
        body, name="mix_fwd_a", grid=(T // tm,),
        in_specs=[tok(D), full((4, D, D)), full((3, CONV)), full((1, CONV)), full((4, CONV, 256))],
        out_specs=[tok(3 * CONV), tok(CONV), tok(CONV), tok(SSM), tok(SSM), tok(D), tok(D), tok(D)],
        out_shape=[jax.ShapeDtypeStruct((T, 3 * CONV), BF16), jax.ShapeDtypeStruct((T, CONV), BF16),
                   jax.ShapeDtypeStruct((T, CONV), BF16), jax.ShapeDtypeStruct((T, SSM), F32),
                   jax.ShapeDtypeStruct((T, SSM), BF16), jax.ShapeDtypeStruct((T, D), BF16),
                   jax.ShapeDtypeStruct((T, D), BF16), jax.ShapeDtypeStruct((T, D), BF16)],
        scratch=[pltpu.VMEM((tm + 8, CONV), F32)], vmem_mb=56, comm=comm,
        operands=(xb, w_mix4, conv_w, conv_b, w_co4))


def _scan_rows(bre, bim, ar, ai, T, rev, load):
    R, W, G = SCAN_R, bre.shape[1], T // 8
    if rev:
        ai = -ai

    def cmul(pr, pi, xr, xi):
        return pr * xr - pi * xi, pr * xi + pi * xr

    pw = [(ar, ai)]
    for _ in range(7):
        pw.append(cmul(ar, ai, *pw[-1]))

    def shifted(v, d, axis, n, idx):
        if rev:
            return jnp.where(idx < n - d, pltpu.roll(v, n - d, axis), 0.0)
        return jnp.where(idx >= d, pltpu.roll(v, d, axis), 0.0)

    sub8 = lax.broadcasted_iota(jnp.int32, (8, W), 0)
    inside = {d: (sub8 < 8 - d) if rev else (sub8 >= d) for d in (1, 2, 4)}
    pm = {d: (jnp.where(inside[d], pw[d - 1][0], 0.0)[None], jnp.where(inside[d], pw[d - 1][1], 0.0)[None])
          for d in (1, 2, 4)}

    def step(i, _):
        t0 = pl.multiple_of(i * R, R)
        vr, vi = load(t0)
        vr, vi = vr.reshape(R // 8, 8, W), vi.reshape(R // 8, 8, W)
        for d in (1, 2, 4):
            sh = (8 - d) if rev else d
            dr, di = cmul(pm[d][0], pm[d][1], pltpu.roll(vr, sh, 1), pltpu.roll(vi, sh, 1))
            vr, vi = vr + dr, vi + di
        bre[pl.ds(t0 + 8, R), :] = vr.reshape(R, W)
        bim[pl.ds(t0 + 8, R), :] = vi.reshape(R, W)
        return 0

    lax.fori_loop(0, T // R, step, 0)

    edge = 0 if rev else 7
    cr = bre[pl.ds(8 + edge, G, stride=8), :]
    ci = bim[pl.ds(8 + edge, G, stride=8), :]
    row = lax.broadcasted_iota(jnp.int32, (G, W), 0)
    qr, qi = pw[7]
    d = 1
    while d < G:
        dr, di = cmul(qr, qi, shifted(cr, d, 0, G, row), shifted(ci, d, 0, G, row))
        cr, ci = cr + dr, ci + di
        qr, qi = qr * qr - qi * qi, 2.0 * qr * qi
        d *= 2

    nr, ni = shifted(cr, 1, 0, G, row), shifted(ci, 1, 0, G, row)
    for r in range(8):
        pr, pi = pw[7 - r] if rev else pw[r]
        dr, di = cmul(pr, pi, nr, ni)
        bre[pl.ds(8 + r, G, stride=8), :] = bre[pl.ds(8 + r, G, stride=8), :] + dr
        bim[pl.ds(8 + r, G, stride=8), :] = bim[pl.ds(8 + r, G, stride=8), :] + di


def _scan_specs(T):
    W = SCAN_W
    lane = pl.BlockSpec((T, W), lambda j: (0, j))
    col = pl.BlockSpec((T, 128), lambda j: (0, j // SCAN_PER))
    wb = pl.BlockSpec((None, 128, W), lambda j: (j, 0, 0))
    wc = pl.BlockSpec((None, W, 128), lambda j: (j, 0, 0))
    vec = pl.BlockSpec((1, W), lambda j: (0, j))
    return lane, col, wb, wc, vec


def _s5_scan_fwd(su_b, wb_re, wb_im, a_re, a_im, comm=()):
    T = su_b.shape[0]
    W = SCAN_W

    def body(su_ref, wbr_ref, wbi_ref, ar_ref, ai_ref, sr_ref, si_ref, bre, bim):
        su = su_ref[...]
        bre[pl.ds(8, T), :] = _nn(su, wbr_ref[...])
        bim[pl.ds(8, T), :] = _nn(su, wbi_ref[...])
        _scan_rows(bre, bim, ar_ref[...], ai_ref[...], T, False,
                   lambda t0: (bre[pl.ds(t0 + 8, SCAN_R), :], bim[pl.ds(t0 + 8, SCAN_R), :]))
        sr_ref[...] = bre[pl.ds(8, T), :].astype(BF16)
        si_ref[...] = bim[pl.ds(8, T), :].astype(BF16)

    lane, col, wb, wc, vec = _scan_specs(T)
    return _pcall(
        body, name="s5_scan_fwd", grid=(LANES // W,),
        in_specs=[col, wb, wb, vec, vec],
        out_specs=[lane, lane],
        out_shape=[jax.ShapeDtypeStruct((T, LANES), BF16)] * 2,
        scratch=[pltpu.VMEM((T + 16, W), F32)] * 2, comm=comm,
        operands=(su_b, wb_re, wb_im, a_re, a_im))


def _gelu(s):
    th = jnp.tanh(GELU_C * (s + 0.044715 * s * s * s))
    return 0.5 * s * (1.0 + th), th


def _mix_fwd_b(st_re, st_im, wc_re4, wc_im4, su, dvec, w_glu4, g_conv, g_ssm, y_conv, w_mo, x1, g, b, tm, comm=()):
    T = su.shape[0]

    def body(sr_ref, si_ref, wcr_ref, wci_ref, su_ref, d_ref, wg_ref, gc_ref, gs_ref, yc_ref, wmo_ref,
             x_ref, g_ref, b_ref, s_ref, sgb_ref, ga_ref, gb_ref, mb_ref, r_ref, xo_ref):
        srb = sr_ref[...]
        sib = si_ref[...]
        ys = [_nn(srb[:, 512 * J:512 * (J + 1)], wcr_ref[J]) + _nn(sib[:, 512 * J:512 * (J + 1)], wci_ref[J])
              for J in range(4)]
        s = jnp.concatenate(ys, axis=1) + d_ref[...] * su_ref[...]
        sg, _ = _gelu(s)
        sgb = sg.astype(BF16)
        ga = jnp.concatenate([_nn(sgb, wg_ref[0]), _nn(sgb, wg_ref[1])], axis=1)
        gb = jnp.concatenate([_nn(sgb, wg_ref[2]), _nn(sgb, wg_ref[3])], axis=1)
        merged = (_sig(gc_ref[...].astype(F32)) * yc_ref[...].astype(F32)
                  + _sig(gs_ref[...].astype(F32)) * (ga * _sig(gb)))
        mb = merged.astype(BF16)
        r = ALPHA * x_ref[...] + _nn(mb, wmo_ref[...])
        xhat, _ = _ln_stats(r)
        xo = xhat * g_ref[...] + b_ref[...]
        s_ref[...] = s
        sgb_ref[...] = sgb
        ga_ref[...] = ga.astype(BF16)
        gb_ref[...] = gb.astype(BF16)
        mb_ref[...] = mb
        r_ref[...] = r
        xo_ref[...] = xo

    def tok(n):
        return pl.BlockSpec((tm, n), lambda i: (i, 0))

    def full(shape):
        return pl.BlockSpec(shape, lambda i: (0,) * len(shape))

    return _pcall(
        body, name="mix_fwd_b", grid=(T // tm,),
        in_specs=[tok(LANES), tok(LANES), full((4, 512, 128)), full((4, 512, 128)), tok(SSM), full((1, SSM)),
                  full((4, SSM, 512)), tok(D), tok(D), tok(D), full((D, D)), tok(D), full((1, D)), full((1, D))],
        out_specs=[tok(SSM), tok(SSM), tok(D), tok(D), tok(D), tok(D), tok(D)],
        out_shape=[jax.ShapeDtypeStruct((T, SSM), F32), jax.ShapeDtypeStruct((T, SSM), BF16),
                   jax.ShapeDtypeStruct((T, D), BF16), jax.ShapeDtypeStruct((T, D), BF16),
                   jax.ShapeDtypeStruct((T, D), BF16), jax.ShapeDtypeStruct((T, D), F32),
                   jax.ShapeDtypeStruct((T, D), F32)],
        vmem_mb=56, comm=comm,
        operands=(st_re, st_im, wc_re4, wc_im4, su, dvec, w_glu4, g_conv, g_ssm, y_conv, w_mo, x1, g, b))


def _ple_loss(x3, x3b, p, w_pi4, w_pg, g, b, target, tm):
    T = x3.shape[0]
    PD = p.shape[1]

    def body(x_ref, xb_ref, p_ref, wpi_ref, wpg_ref, g_ref, b_ref, t_ref,
             loss_ref, dx_ref, pb_ref, dpw_ref, dgt_ref, dg_ref, db_ref):
        i = pl.program_id(0)
        pb = p_ref[...].astype(BF16)
        pw = jnp.concatenate([_nn(pb, wpi_ref[k]) for k in range(4)], axis=1)
        gt = _nn(xb_ref[...], wpg_ref[...])
        sg = _sig(gt)
        r = ALPHA * x_ref[...] + pw * sg
        gv = g_ref[...]
        xhat, rstd = _ln_stats(r)
        err = xhat * gv + b_ref[...] - t_ref[...]
        lpart = jnp.zeros((1, 128), F32) + 0.5 * jnp.sum(jnp.mean(err * err, axis=-1, keepdims=True))
        dy = err * (1.0 / D)
        dyg = dy * gv
        m1 = jnp.mean(dyg, axis=-1, keepdims=True)
        m2 = jnp.mean(dyg * xhat, axis=-1, keepdims=True)
        dr = rstd * (dyg - m1 - xhat * m2)
        pg, pbias = _rowsum(dy * xhat), _rowsum(dy)

        @pl.when(i == 0)
        def _():
            loss_ref[...] = lpart
            dg_ref[...] = pg
            db_ref[...] = pbias

        @pl.when(i > 0)
        def _():
            loss_ref[...] += lpart
            dg_ref[...] += pg
            db_ref[...] += pbias

        dgt = (dr * pw * sg * (1.0 - sg)).astype(BF16)
        pb_ref[...] = pb
        dpw_ref[...] = (dr * sg).astype(BF16)
        dgt_ref[...] = dgt
        dx_ref[...] = ALPHA * dr + _nt(dgt, wpg_ref[...])

    def tok(n):
        return pl.BlockSpec((tm, n), lambda i: (i, 0))

    def full(shape):
        return pl.BlockSpec(shape, lambda i: (0,) * len(shape))

    return pl.pallas_call(
        body, name="ple_loss", grid=(T // tm,),
        in_specs=[tok(D), tok(D), tok(PD), full((4, PD, 256)), full((D, D)), full((1, D)), full((1, D)), tok(D)],
        out_specs=[full((1, 128)), tok(D), tok(PD), tok(D), tok(D), full((1, D)), full((1, D))],
        out_shape=_hbm_out([jax.ShapeDtypeStruct((1, 128), F32), jax.ShapeDtypeStruct((T, D), F32),
                            jax.ShapeDtypeStruct((T, PD), BF16), jax.ShapeDtypeStruct((T, D), BF16),
                            jax.ShapeDtypeStruct((T, D), BF16), jax.ShapeDtypeStruct((1, D), F32),
                            jax.ShapeDtypeStruct((1, D), F32)]),
        compiler_params=_cp(48, 1),
    )(*_hbm(x3, x3b, p, w_pi4, w_pg, g, b, target))


def _mix_bwd_b(dy, r2, g, w_mo, g_conv, g_ssm, y_conv, ga, gb, s, su, dvec, w_glu4, wc_re4, wc_im4, tm, comm=()):
    T = dy.shape[0]

    def body(dy_ref, r_ref, g_ref, wmo_ref, gc_ref, gs_ref, yc_ref, ga_ref, gb_ref, s_ref, su_ref, d_ref,
             wg_ref, wcr_ref, wci_ref,
             dres_ref, dmix_ref, dgl_ref, dsb_ref, dud_ref, gsr_ref, gsi_ref, dyc_ref, dp_ref,
             dg_ref, db_ref, dd_ref):
        i = pl.program_id(0)
        dyv = dy_ref[...]
        dr, xhat = _ln_bwd(dyv, r_ref[...], g_ref[...])
        dmix = dr.astype(BF16)
        dmerged = _nt(dmix, wmo_ref[...])
        sc, ss, sgb = (_sig(gc_ref[...].astype(F32)), _sig(gs_ref[...].astype(F32)),
                       _sig(gb_ref[...].astype(F32)))
        gav = ga_ref[...].astype(F32)
        yssm = gav * sgb
        dgc = dmerged * yc_ref[...].astype(F32) * sc * (1.0 - sc)
        dgss = dmerged * yssm * ss * (1.0 - ss)
        dyssm = dmerged * ss
        dgl = jnp.concatenate([dyssm * sgb, dyssm * gav * sgb * (1.0 - sgb)], axis=1).astype(BF16)
        dsg = (_nt(dgl[:, 0:512], wg_ref[0]) + _nt(dgl[:, 512:1024], wg_ref[1])
               + _nt(dgl[:, 1024:1536], wg_ref[2]) + _nt(dgl[:, 1536:2048], wg_ref[3]))
        sv = s_ref[...]
        _, th = _gelu(sv)
        dgelu = 0.5 * (1.0 + th) + 0.5 * sv * (1.0 - th * th) * GELU_C * (1.0 + 3.0 * 0.044715 * sv * sv)
        ds = dsg * dgelu
        dsb = ds.astype(BF16)
        pg, pb, pd = _rowsum(dyv * xhat), _rowsum(dyv), _rowsum(ds * su_ref[...])

        @pl.when(i == 0)
        def _():
            dg_ref[...] = pg
            db_ref[...] = pb
            dd_ref[...] = pd

        @pl.when(i > 0)
        def _():
            dg_ref[...] += pg
            db_ref[...] += pb
            dd_ref[...] += pd

        dres_ref[...] = ALPHA * dr
        dmix_ref[...] = dmix
        dgl_ref[...] = dgl
        dsb_ref[...] = dsb
        dud_ref[...] = ds * d_ref[...]
        for J in range(4):
            gsr_ref[:, 512 * J:512 * (J + 1)] = _nt(dsb[:, 128 * J:128 * (J + 1)], wcr_ref[J]).astype(BF16)
            gsi_ref[:, 512 * J:512 * (J + 1)] = _nt(dsb[:, 128 * J:128 * (J + 1)], wci_ref[J]).astype(BF16)
        dyc_ref[...] = (dmerged * sc).astype(BF16)
        dp_ref[:, 0:D] = dgc.astype(BF16)
        dp_ref[:, D:2 * D] = dgss.astype(BF16)

    def tok(n):
        return pl.BlockSpec((tm, n), lambda i: (i, 0))

    def full(shape):
        return pl.BlockSpec(shape, lambda i: (0,) * len(shape))

    return _pcall(
        body, name="mix_bwd_b", grid=(T // tm,),
        in_specs=[tok(D), tok(D), full((1, D)), full((D, D)), tok(D), tok(D), tok(D), tok(D), tok(D),
                  tok(SSM), tok(SSM), full((1, SSM)), full((4, SSM, 512)), full((4, 512, 128)), full((4, 512, 128))],
        out_specs=[tok(D), tok(D), tok(2 * D), tok(SSM), tok(SSM), tok(LANES), tok(LANES), tok(D),
                   pl.BlockSpec((tm, 2 * D), lambda i: (i, 1)), full((1, D)), full((1, D)), full((1, SSM))],
        out_shape=[jax.ShapeDtypeStruct((T, D), F32), jax.ShapeDtypeStruct((T, D), BF16),
                   jax.ShapeDtypeStruct((T, 2 * D), BF16), jax.ShapeDtypeStruct((T, SSM), BF16),
                   jax.ShapeDtypeStruct((T, SSM), F32), jax.ShapeDtypeStruct((T, LANES), BF16),
                   jax.ShapeDtypeStruct((T, LANES), BF16), jax.ShapeDtypeStruct((T, D), BF16),
                   jax.ShapeDtypeStruct((T, 4 * D), BF16), jax.ShapeDtypeStruct((1, D), F32),
                   jax.ShapeDtypeStruct((1, D), F32), jax.ShapeDtypeStruct((1, SSM), F32)],
        vmem_mb=56, comm=comm,
        operands=(dy, r2, g, w_mo, g_conv, g_ssm, y_conv, ga, gb, s, su, dvec, w_glu4, wc_re4, wc_im4))


def _s5_scan_bwd(gs_re, gs_im, st_re, st_im, su_b, ds_b, wb_re, wb_im, a_re, a_im, comm=()):
    T = su_b.shape[0]
    W = SCAN_W
    R = SCAN_R

    def body(gr_ref, gi_ref, sr_ref, si_ref, su_ref, ds_ref, wbr_ref, wbi_ref, ar_ref, ai_ref,
             dsu_ref, dwbr_ref, dwbi_ref, dwcr_ref, dwci_ref, dar_ref, dai_ref, gre, gim):
        j = pl.program_id(0)
        zero = jnp.zeros((8, W), F32)
        for buf in (gre, gim):
            buf[pl.ds(T + 8, 8), :] = zero
        _scan_rows(gre, gim, ar_ref[...], ai_ref[...], T, True,
                   lambda t0: (gr_ref[pl.ds(t0, R), :].astype(F32), gi_ref[pl.ds(t0, R), :].astype(F32)))
        grb = gre[pl.ds(8, T), :].astype(BF16)
        gib = gim[pl.ds(8, T), :].astype(BF16)
        part = _nt(grb, wbr_ref[...]) + _nt(gib, wbi_ref[...])

        @pl.when(j % SCAN_PER == 0)
        def _():
            dsu_ref[...] = part

        @pl.when(j % SCAN_PER > 0)
        def _():
            dsu_ref[...] += part

        su = su_ref[...]
        dwbr_ref[...] = _tn(su, grb)
        dwbi_ref[...] = _tn(su, gib)
        dsv = ds_ref[...]
        dwcr_ref[...] = _tn(sr_ref[...], dsv)
        dwci_ref[...] = _tn(si_ref[...], dsv)
        dar = jnp.zeros((1, W), F32)
        dai = jnp.zeros((1, W), F32)
        for c in range(T // R):
            xr = sr_ref[pl.ds(c * R, R), :].astype(F32)
            xi = si_ref[pl.ds(c * R, R), :].astype(F32)
            g1r = gre[pl.ds(c * R + 9, R), :]
            g1i = gim[pl.ds(c * R + 9, R), :]
            dar = dar + _rowsum(g1r * xr + g1i * xi)
            dai = dai + _rowsum(g1i * xr - g1r * xi)
        dar_ref[...] = dar
        dai_ref[...] = dai

    lane, col, wb, wc, vec = _scan_specs(T)
    return _pcall(
        body, name="s5_scan_bwd", grid=(LANES // W,),
        in_specs=[lane, lane, lane, lane, col, col, wb, wb, vec, vec],
        out_specs=[col, wb, wb, wc, wc, vec, vec],
        out_shape=[jax.ShapeDtypeStruct((T, SSM), F32),
                   jax.ShapeDtypeStruct((LANES // W, 128, W), F32), jax.ShapeDtypeStruct((LANES // W, 128, W), F32),
                   jax.ShapeDtypeStruct((LANES // W, W, 128), F32), jax.ShapeDtypeStruct((LANES // W, W, 128), F32),
                   jax.ShapeDtypeStruct((1, LANES), F32), jax.ShapeDtypeStruct((1, LANES), F32)],
        scratch=[pltpu.VMEM((T + 16, W), F32)] * 2, vmem_mb=56, comm=comm,
        operands=(gs_re, gs_im, st_re, st_im, su_b, ds_b, wb_re, wb_im, a_re, a_im))


def _mix_bwd_a(dyc_b, w_co4, pc, z_b, conv_w, dsu_ssm, du_dir, dproj, dres, w_mix4, tm, comm=()):
    T = dres.shape[0]
    nt = T // tm

    def body(dyc_ref, wco_ref, pc_ref, halo_ref, z_ref, cw_ref, dsu_ref, dud_ref, dpin_ref, dres_ref, w_ref,
             dp_ref, dx_ref, dcw_ref, dcb_ref, dzbuf, qbuf):
        i = pl.program_id(0)
        ii = nt - 1 - i

        @pl.when(i == 0)
        def _():
            dzbuf[pl.ds(tm, 8), :] = jnp.zeros((8, CONV), F32)

        dyc = dyc_ref[...]
        dyin = (_nt(dyc[:, 0:256], wco_ref[0]) + _nt(dyc[:, 256:512], wco_ref[1])
                + _nt(dyc[:, 512:768], wco_ref[2]) + _nt(dyc[:, 768:1024], wco_ref[3]))
        cbv = pc_ref[:, 0:CONV].astype(F32)
        ccv = pc_ref[:, CONV:2 * CONV].astype(F32)
        chv = pc_ref[:, 2 * CONV:3 * CONV].astype(F32)
        dcbv = dyin * z_ref[...].astype(F32)
        dz = dyin * cbv
        dzbuf[pl.ds(0, tm), :] = dz
        cw = cw_ref[...]
        dq = cw[2:3] * dz + cw[1:2] * dzbuf[pl.ds(1, tm), :] + cw[0:1] * dzbuf[pl.ds(2, tm), :]
        dzbuf[pl.ds(tm, 8), :] = dz[0:8]
        q = ccv * chv
        hq = halo_ref[:, CONV:2 * CONV].astype(F32) * halo_ref[:, 2 * CONV:3 * CONV].astype(F32)
        qbuf[pl.ds(0, 8), :] = jnp.where(ii > 0, hq, jnp.zeros_like(hq))
        qbuf[pl.ds(8, tm), :] = q
        pw = jnp.concatenate([_rowsum(dz * qbuf[pl.ds(6, tm), :]), _rowsum(dz * qbuf[pl.ds(7, tm), :]),
                              _rowsum(dz * q), jnp.zeros((5, CONV), F32)], axis=0)
        pbias = _rowsum(dz)

        @pl.when(i == 0)
        def _():
            dcw_ref[...] = pw
            dcb_ref[...] = pbias

        @pl.when(i > 0)
        def _():
            dcw_ref[...] += pw
            dcb_ref[...] += pbias

        dp0 = jnp.concatenate([dcbv, dq * chv], axis=1).astype(BF16)
        dp1 = jnp.concatenate([dq * ccv, dsu_ref[...] + dud_ref[...]], axis=1).astype(BF16)
        dp_ref[:, 0:D] = dp0
        dp_ref[:, D:2 * D] = dp1
        dx_ref[...] = (dres_ref[...] + _nt(dp0, w_ref[0]) + _nt(dp1, w_ref[1])
                       + _nt(dpin_ref[:, 0:D], w_ref[2]) + _nt(dpin_ref[:, D:2 * D], w_ref[3]))

    def tok(n):
        return pl.BlockSpec((tm, n), lambda i: (nt - 1 - i, 0))

    def full(shape):
        return pl.BlockSpec(shape, lambda i: (0,) * len(shape))

    halo = pl.BlockSpec((8, 3 * CONV), lambda i: (jnp.maximum((nt - 1 - i) * (tm // 8) - 1, 0), 0))
    return _pcall(
        body, name="mix_bwd_a", grid=(nt,),
        in_specs=[tok(D), full((4, CONV, 256)), tok(3 * CONV), halo, tok(CONV), full((3, CONV)),
                  tok(SSM), tok(SSM), pl.BlockSpec((tm, 2 * D), lambda i: (nt - 1 - i, 1)), tok(D),
                  full((4, D, D))],
        out_specs=[pl.BlockSpec((tm, 2 * D), lambda i: (nt - 1 - i, 0)), tok(D), full((8, CONV)), full((1, CONV))],
        out_shape=[jax.ShapeDtypeStruct((T, 4 * D), BF16), jax.ShapeDtypeStruct((T, D), F32),
                   jax.ShapeDtypeStruct((8, CONV), F32), jax.ShapeDtypeStruct((1, CONV), F32)],
        scratch=[pltpu.VMEM((tm + 8, CONV), F32), pltpu.VMEM((tm + 8, CONV), F32)],
        aliases={8: 0}, vmem_mb=56, comm=comm,
        operands=(dyc_b, w_co4, pc, pc, z_b, conv_w, dsu_ssm, du_dir, dproj, dres, w_mix4))


def _zoh(lam_re, lam_im, log_step, b_re, b_im):
    dt = jnp.exp(log_step)[:, None]
    mag = jnp.exp(lam_re * dt)
    abr, abi = mag * jnp.cos(lam_im * dt), mag * jnp.sin(lam_im * dt)
    nr, ni = abr - 1.0, abi
    den = lam_re * lam_re + lam_im * lam_im
    cr = (nr * lam_re + ni * lam_im) / den
    ci = (ni * lam_re - nr * lam_im) / den
    bbr = cr[..., None] * b_re - ci[..., None] * b_im
    bbi = cr[..., None] * b_im + ci[..., None] * b_re
    return abr, abi, bbr, bbi


_WB_MASK = (np.arange(8)[None, :, None]
            == SCAN_GR * np.arange(SCAN_PER)[:, None, None] + np.arange(SCAN_GR)[None, None, :]).astype(np.float32)
_EYE8 = np.eye(8, dtype=np.float32)


def _wb_blocks(bb):
    bt = bb.transpose(0, 2, 1).reshape(4, 1, 8, 16, 1, STATE)
    full = bt * _WB_MASK[None, :, :, None, :, None]
    return full.reshape(LANES // SCAN_W, 128, SCAN_W).astype(BF16)


def _wc_blocks(cc):
    ct = cc.transpose(0, 2, 1).reshape(4, 8, STATE, 1, 16)
    full = ct * _EYE8[None, :, None, :, None]
    return full.reshape(4, 512, 128).astype(BF16)


def _wb_diag(dwb):
    d6 = dwb.reshape(4, SCAN_PER, 8, 16, SCAN_GR, STATE) * _WB_MASK[None, :, :, None, :, None]
    return d6.sum(axis=(1, 4)).reshape(GROUPS, 16, STATE).transpose(0, 2, 1)


def _wc_diag(dwc):
    mask = _WB_MASK.transpose(0, 2, 1)
    d6 = dwc.reshape(4, SCAN_PER, SCAN_GR, STATE, 8, 16) * mask[None, :, :, None, :, None]
    return d6.sum(axis=4).reshape(GROUPS, STATE, 16).transpose(0, 2, 1)


def _where():
    x, y, c = lax.axis_index("x"), lax.axis_index("y"), lax.axis_index("c")
    return x, y, c, 2 * x + y


def _chip_dev(k, c):
    return (k // 2, k % 2, c)


def _slot_cast(meidx, w, dtype, name, token=()):
    R, C = w.shape
    tr = _row_tile(R)

    def body(m_ref, w_ref, *rest):
        rest[-1][...] = w_ref[...].astype(dtype)

    gs = pltpu.PrefetchScalarGridSpec(
        num_scalar_prefetch=1, grid=(R // tr,),
        in_specs=[pl.BlockSpec((tr, C), lambda i, m: (i, 0))] + [pl.BlockSpec((8, 128), lambda i, m: (0, 0))] * len(token),
        out_specs=pl.BlockSpec((None, tr, C), lambda i, m: (m[0], i, 0)))
    return pl.pallas_call(
        body, name=name, grid_spec=gs, out_shape=_hbm_out(jax.ShapeDtypeStruct((4, R, C), dtype)),
        compiler_params=_cp(32, 1),
    )(meidx, *_hbm(w), *token)


def _gather_ici_payload(bufs):
    def copies(ins, lnd, ss, rs):
        x, y, c, me = _where()
        cps = []
        for w, b in enumerate(bufs):
            h = b.shape[1] // 2
            mine = lnd[w].at[me, pl.ds(c * h, h)]
            for s in range(3):
                k = (me + 1 + s) % 4
                cps.append(pltpu.make_async_remote_copy(
                    src_ref=mine, dst_ref=mine, send_sem=ss.at[3 * w + s], recv_sem=rs.at[3 * w + s],
                    device_id=_chip_dev(k, c), device_id_type=MESH))
        return cps

    p = _sym_payload([], [jax.ShapeDtypeStruct(b.shape, b.dtype) for b in bufs], copies, 3 * len(bufs))
    p.lands = list(bufs)
    return p


def _gather_pass_payload(bufs):
    def copies(ins, outs, ss, rs):
        x, y, c, me = _where()
        cps = []
        for w, b in enumerate(bufs):
            h = b.shape[1] // 2
            for s in range(3):
                j = (me + 1 + s) % 4
                cps.append(pltpu.make_async_remote_copy(
                    src_ref=ins[w].at[j, pl.ds(c * h, h)], dst_ref=outs[w].at[j, pl.ds(c * h, h)],
                    send_sem=ss.at[3 * w + s], recv_sem=rs.at[3 * w + s], device_id=(x, y, 1 - c),
                    device_id_type=MESH))
        return cps

    p = _sym_payload(bufs, [jax.ShapeDtypeStruct(b.shape, b.dtype) for b in bufs], copies, 3 * len(bufs))
    p.aliases = {w: w for w in range(len(bufs))}
    return p


def _gather_payload(bufs):
    n = len(bufs)

    def half(ref, w, k, cc):
        h = bufs[w].shape[1] // 2
        return ref.at[k, pl.ds(cc * h, h)]

    def ici(ins, outs, sems, w, s):
        x, y, c, me = _where()
        k = (me + 1 + s) % 4
        return pltpu.make_async_remote_copy(
            src_ref=half(ins[w], w, me, c), dst_ref=half(outs[w], w, me, c), send_sem=sems[0].at[3 * w + s],
            recv_sem=sems[1].at[3 * w + s], device_id=_chip_dev(k, c), device_id_type=MESH)

    def landed(outs, sems, w, s):
        x, y, c, me = _where()
        j = (me + 3 - s) % 4
        return pltpu.make_async_remote_copy(
            src_ref=half(outs[w], w, j, c), dst_ref=half(outs[w], w, j, c), send_sem=sems[0].at[3 * w + s],
            recv_sem=sems[1].at[3 * w + s], device_id=(x, y, 1 - c), device_id_type=MESH)

    def passed(outs, sems, w, s, cc):
        x, y, c, me = _where()
        j = (me + 3 - s) % 4
        return pltpu.make_async_remote_copy(
            src_ref=half(outs[w], w, j, cc), dst_ref=half(outs[w], w, j, cc), send_sem=sems[2].at[3 * w + s],
            recv_sem=sems[3].at[3 * w + s], device_id=(x, y, 1 - c), device_id_type=MESH)

    pairs = [(w, s) for w in range(n) for s in range(3)]

    def start(ins, outs, sems):
        for w, s in pairs:
            ici(ins, outs, sems, w, s).start()

    def finish(ins, outs, sems):
        _, _, c, _ = _where()
        for w, s in pairs:
            landed(outs, sems, w, s).wait_recv()
            passed(outs, sems, w, s, c).start()
        for w, s in pairs:
            passed(outs, sems, w, s, 1 - c).wait_recv()
        for w, s in pairs:
            ici(ins, outs, sems, w, s).wait_send()
            passed(outs, sems, w, s, c).wait_send()

    return _Payload(bufs, [jax.ShapeDtypeStruct(b.shape, b.dtype) for b in bufs], {w: w for w in range(n)},
                    [pltpu.SemaphoreType.DMA((3 * n,))] * 4, start, finish)


def _sym_payload(operands, outs, copies, n_copies):
    def start(ins, outs_, sems):
        for cp in copies(ins, outs_, sems[0], sems[1]):
            cp.start()

    def finish(ins, outs_, sems):
        for cp in copies(ins, outs_, sems[0], sems[1]):
            cp.wait()

    p = _Payload(operands, outs, {}, [pltpu.SemaphoreType.DMA((n_copies,))] * 2, start, finish)
    p.copies, p.n_copies = copies, n_copies
    return p


def _swap_payload(g4s):
    def copies(ins, outs, ss, rs):
        x, y, c, me = _where()
        cps = []
        for w, g in enumerate(g4s):
            h = g.shape[1] // 2
            cps.append(pltpu.make_async_remote_copy(
                src_ref=ins[w].at[:, pl.ds((1 - c) * h, h)], dst_ref=outs[w], send_sem=ss.at[w],
                recv_sem=rs.at[w], device_id=(x, y, 1 - c), device_id_type=MESH))
        return cps

    outs = [jax.ShapeDtypeStruct((4, g.shape[1] // 2, g.shape[2]), g.dtype) for g in g4s]
    return _sym_payload(g4s, outs, copies, len(g4s))


def _exchange_payload(pbs):
    def copies(ins, outs, ss, rs):
        x, y, c, me = _where()
        cps = []
        for w in range(len(pbs)):
            for s in range(3):
                k = (me + 1 + s) % 4
                cps.append(pltpu.make_async_remote_copy(
                    src_ref=ins[w].at[k], dst_ref=outs[w].at[2 - s], send_sem=ss.at[3 * w + s],
                    recv_sem=rs.at[3 * w + s], device_id=_chip_dev(k, c), device_id_type=MESH))
        return cps

    outs = [jax.ShapeDtypeStruct((3,) + p.shape[1:], p.dtype) for p in pbs]
    return _sym_payload(pbs, outs, copies, 3 * len(pbs))


HBM_REF = pl.BlockSpec(memory_space=pltpu.HBM)
SEM_REF = pl.BlockSpec(memory_space=pltpu.SEMAPHORE)
DATAFLOW = pltpu.SideEffectType.DATAFLOW_SIDE_EFFECTING


class _SemList:
    def __init__(self, refs):
        self.refs = refs

    @property
    def at(self):
        return self.refs


def _split_start(p, name):
    n_in, n_out, nc = len(p.operands), len(p.outs), p.n_copies
    lands = getattr(p, "lands", None) or [lax.empty(s.shape, s.dtype) for s in p.outs]

    def body(*refs):
        ins, lnd = refs[:n_in], refs[n_in:n_in + n_out]
        sems = refs[n_in + n_out:n_in + n_out + 2 * nc]
        for cp in p.copies(ins, lnd, _SemList(sems[:nc]), _SemList(sems[nc:])):
            cp.start()
        refs[-1][...] = jnp.zeros((8, 128), F32)

    res = pl.pallas_call(
        body, name=name,
        in_specs=[HBM_REF] * (n_in + n_out),
        out_specs=[SEM_REF] * (2 * nc) + [HBM_REF] * (n_in + n_out) + [VMEM_FULL],
        out_shape=([pltpu.SemaphoreType.DMA(())] * (2 * nc) + _hbm_out(p.operands) + _hbm_out(lands)
                   + [jax.ShapeDtypeStruct((8, 128), F32)]),
        input_output_aliases={i: 2 * nc + i for i in range(n_in + n_out)},
        compiler_params=pltpu.CompilerParams(has_side_effects=DATAFLOW),
    )(*_hbm(*p.operands, *lands))
    k = 2 * nc
    return list(res[:k]), list(res[k:k + n_in]), list(res[k + n_in:k + n_in + n_out]), res[-1]


def _split_wait(p, handle, after, name):
    sems, srcs, lands, _ = handle
    n_in, n_out, nc = len(srcs), len(lands), p.n_copies

    def body(*refs):
        ins, lnd = refs[:n_in], refs[n_in:n_in + n_out]
        sm = refs[n_in + n_out:n_in + n_out + 2 * nc]
        for cp in p.copies(ins, lnd, _SemList(sm[:nc]), _SemList(sm[nc:])):
            cp.wait_send()
            cp.wait_recv()

    res = pl.pallas_call(
        body, name=name,
        in_specs=[HBM_REF] * (n_in + n_out) + [SEM_REF] * (2 * nc) + [ANY] * len(after),
        out_specs=[HBM_REF] * (n_in + n_out), out_shape=_hbm_out(srcs) + _hbm_out(lands),
        input_output_aliases={i: i for i in range(n_in + n_out)},
        compiler_params=pltpu.CompilerParams(has_side_effects=DATAFLOW),
    )(*srcs, *lands, *sems, *after)
    return list(res[:n_in]), list(res[n_in:])


def _join_payload(halves):
    def copies(ins, outs, ss, rs):
        x, y, c, me = _where()
        return [pltpu.make_async_remote_copy(
            src_ref=ins[w], dst_ref=outs[w], send_sem=ss.at[w], recv_sem=rs.at[w],
            device_id=(x, y, 1 - c), device_id_type=MESH) for w in range(len(halves))]

    outs = [jax.ShapeDtypeStruct(a.shape, a.dtype) for a in halves]
    return _sym_payload(halves, outs, copies, len(halves))


def _allgather_payload(v):
    def copies(ins, outs, ss, rs):
        x, y, c, me = _where()
        lin = 4 * x + 2 * y + c
        cps = []
        for o in range(1, 8):
            t = (lin + o) % 8
            cps.append(pltpu.make_async_remote_copy(
                src_ref=ins[0], dst_ref=outs[0].at[lin], send_sem=ss.at[o - 1], recv_sem=rs.at[o - 1],
                device_id=(t // 4, (t // 2) % 2, t % 2), device_id_type=MESH))
        return cps

    p = _sym_payload([v], [jax.ShapeDtypeStruct((8,) + v.shape, v.dtype)], copies, 7)
    x, y, c, _ = _where()
    p.lands = [lax.dynamic_update_slice(jnp.zeros((8,) + v.shape, v.dtype), v[None], (4 * x + 2 * y + c, 0, 0))]
    return p


def _sum8(buf, token):
    _, P, C = buf.shape

    def body(b_ref, t_ref, o_ref):
        acc = b_ref[0]
        for d in range(1, 8):
            acc = acc + b_ref[d]
        o_ref[...] = acc

    return pl.pallas_call(
        body, name="sum8", in_specs=[VMEM_FULL, VMEM_FULL], out_specs=VMEM_FULL,
        out_shape=jax.ShapeDtypeStruct((P, C), F32),
        compiler_params=pltpu.CompilerParams(vmem_limit_bytes=32 << 20),
    )(buf, token)


def _row_tile(h):
    for t in (256, 176, 128, 64, 32, 16, 8):
        if h % t == 0:
            return t
    raise ValueError(h)


def _pair_sum(cmidx, g4, got, name):
    _, R, C = g4.shape
    h = R // 2
    th = _row_tile(h)

    def body(cm_ref, a_ref, b_ref, o_ref, ob_ref):
        sm = a_ref[...] + b_ref[...]
        ob_ref[...] = sm.astype(BF16)

        @pl.when(pl.program_id(1) == cm_ref[1])
        def _():
            o_ref[...] = sm

    blk = pl.BlockSpec((None, th, C), lambda i, k, cm: (k, i, 0))
    gs = pltpu.PrefetchScalarGridSpec(
        num_scalar_prefetch=1, grid=(h // th, 4),
        in_specs=[pl.BlockSpec((None, None, th, C), lambda i, k, cm: (k, cm[0], i, 0)), blk],
        out_specs=[pl.BlockSpec((th, C), lambda i, k, cm: (i, 0)), blk])
    return pl.pallas_call(
        body, name=name, grid_spec=gs,
        out_shape=_hbm_out([jax.ShapeDtypeStruct((h, C), F32), jax.ShapeDtypeStruct((4, h, C), BF16)]),
        compiler_params=_cp(32, 2),
    )(cmidx, *_hbm(g4.reshape(4, 2, h, C), got))


def _chip_sum(own, got, name):
    h, C = own.shape
    th = _row_tile(h)

    def body(a_ref, b_ref, o_ref):
        o_ref[...] = ((a_ref[...] + b_ref[0].astype(F32)) + b_ref[1].astype(F32)) + b_ref[2].astype(F32)

    return pl.pallas_call(
        body, name=name, grid=(h // th,),
        in_specs=[pl.BlockSpec((th, C), lambda i: (i, 0)), pl.BlockSpec((3, th, C), lambda i: (0, i, 0))],
        out_specs=pl.BlockSpec((th, C), lambda i: (i, 0)),
        out_shape=_hbm_out(jax.ShapeDtypeStruct((h, C), F32)),
        compiler_params=_cp(32, 1),
    )(*_hbm(own, got))


def _adamw_math(w, g, m, v):
    m2 = B1 * m + (1.0 - B1) * g
    v2 = B2 * v + (1.0 - B2) * (g * g)
    m_hat = m2 / (1.0 - B1 ** STEP)
    v_hat = v2 / (1.0 - B2 ** STEP)
    delta = -LR * (m_hat / (jnp.sqrt(v_hat) + EPS) + WD * w)
    return delta, m2, v2


def _adamw_pair(cidx, w, mine, theirs, m, v, token, name):
    R, C = w.shape
    h = R // 2
    tr = _row_tile(h)
    nh = h // tr

    def body(c_ref, w_ref, a_ref, b_ref, m_ref, v_ref, t_ref, g_ref, d_ref, mo_ref, vo_ref):
        own = (pl.program_id(0) // nh) == c_ref[0]
        g = jnp.where(own, a_ref[...], b_ref[...])
        d, m2, v2 = _adamw_math(w_ref[...], g, m_ref[...], v_ref[...])
        g_ref[...] = g
        d_ref[...] = d
        mo_ref[...] = m2
        vo_ref[...] = v2

    blk = pl.BlockSpec((tr, C), lambda i, c: (i, 0))
    mine_blk = pl.BlockSpec((tr, C), lambda i, c: (jnp.clip(i - c[0] * nh, 0, nh - 1), 0))
    theirs_blk = pl.BlockSpec((tr, C), lambda i, c: (jnp.clip(i - (1 - c[0]) * nh, 0, nh - 1), 0))
    gs = pltpu.PrefetchScalarGridSpec(
        num_scalar_prefetch=1, grid=(R // tr,),
        in_specs=[blk, mine_blk, theirs_blk, blk, blk, pl.BlockSpec((8, 128), lambda i, c: (0, 0))],
        out_specs=[blk] * 4)
    return pl.pallas_call(
        body, name=name, grid_spec=gs, out_shape=_hbm_out([jax.ShapeDtypeStruct((R, C), F32)] * 4),
        compiler_params=_cp(32, 1),
    )(cidx, *_hbm(w, mine, theirs, m, v), token)


def _adamw(w, g, m, v, name):
    R, C = w.shape
    tr = _row_tile(R)

    def body(w_ref, g_ref, m_ref, v_ref, d_ref, mo_ref, vo_ref):
        d, m2, v2 = _adamw_math(w_ref[...], g_ref[...], m_ref[...], v_ref[...])
        d_ref[...] = d
        mo_ref[...] = m2
        vo_ref[...] = v2

    blk = pl.BlockSpec((tr, C), lambda i: (i, 0))
    return pl.pallas_call(
        body, name=name, grid=(R // tr,), in_specs=[blk] * 4, out_specs=[blk] * 3,
        out_shape=_hbm_out([jax.ShapeDtypeStruct((R, C), F32)] * 3),
        compiler_params=_cp(32, 1),
    )(*_hbm(w, g, m, v))


def _pack(arrs):
    flat = jnp.concatenate([a.reshape(-1).astype(F32) for a in arrs])
    rows = -(-flat.shape[0] // 1024)
    rows = -(-rows // 8) * 8
    return jnp.pad(flat, (0, rows * 1024 - flat.shape[0])).reshape(rows, 1024)


def _unpack(packed, shapes):
    flat = packed.reshape(-1)
    out, off = [], 0
    for s in shapes:
        n = math.prod(s)
        out.append(flat[off:off + n].reshape(s))
        off += n
    return out


BIG = ["ffn1_w_in", "ffn1_w_out", "mix_w_in", "conv_w_out", "ssm_w_glu", "mix_w_out",
       "ffn2_w_in", "ffn2_w_out", "ple_w_in", "ple_w_gate"]
SMALL = ["ln1_g", "ln1_b", "conv_w", "conv_b", "ssm_lam_re", "ssm_lam_im", "ssm_log_step", "ssm_b_re", "ssm_b_im",
         "ssm_c_re", "ssm_c_im", "ssm_d", "ln2_g", "ln2_b", "ln3_g", "ln3_b", "ln4_g", "ln4_b"]
WEIGHTS = ["ffn1_w_in", "ffn1_w_out", "ln1_g", "ln1_b", "mix_w_in", "conv_w", "conv_b", "conv_w_out",
           "ssm_lam_re", "ssm_lam_im", "ssm_log_step", "ssm_b_re", "ssm_b_im", "ssm_c_re", "ssm_c_im", "ssm_d",
           "ssm_w_glu", "mix_w_out", "ln2_g", "ln2_b", "ffn2_w_in", "ffn2_w_out", "ln3_g", "ln3_b",
           "ple_w_in", "ple_w_gate", "ln4_g", "ln4_b"]


def _s5_operands(sp):
    abr, abi, bbr, bbi = _zoh(sp["ssm_lam_re"], sp["ssm_lam_im"], sp["ssm_log_step"], sp["ssm_b_re"], sp["ssm_b_im"])
    return (_wb_blocks(bbr), _wb_blocks(bbi), _wc_blocks(sp["ssm_c_re"]), _wc_blocks(-sp["ssm_c_im"]),
            abr.reshape(1, LANES), abi.reshape(1, LANES), sp["ssm_d"].reshape(1, SSM))


def _local_step(x, p, target, sp, ops, sched):
    W = sched.W
    wb_re, wb_im, wc_re4, wc_im4, a_re, a_im, dvec = ops
    tm = TOKEN_TILE

    def run(fn, name, *args, **kw):
        outs, got = fn(*args, comm=sched.carry(name), **kw)
        sched.landed(name, got)
        sched.done[name] = outs[0]
        return outs

    def dw(name, wname, a, b, tk, tn, shape4, shard_cols=None, interleaved=False):
        out, got = _mm_tn(a, b, tk, tn, name, shard_cols=shard_cols, interleaved=interleaved,
                          comm=sched.carry(name))
        sched.landed(name, got)
        sched.done[name] = out
        sched.grad(wname, out.reshape(shape4))

    h1, r1, x1, x1b, xb = run(_ffn_fwd, "ffn1_fwd", x, W["ffn1_w_in"], W["ffn1_w_out"].reshape(2, FFH, D),
                              sp["ln1_g"], sp["ln1_b"], tm, "ffn1_fwd")
    conv_w = W["conv_w"][:, 0:3, :].transpose(1, 0, 2).reshape(3, CONV)
    pc, z_b, yin_b, su, su_b, g_conv, g_ssm, y_conv = run(
        _mix_fwd_a, "mix_fwd_a", x1b, W["mix_w_in"], conv_w, sp["conv_b"], W["conv_w_out"], tm)
    st_re, st_im = run(_s5_scan_fwd, "s5_scan_fwd", su_b, wb_re, wb_im, a_re, a_im)
    w_mo = W["mix_w_out"].reshape(D, D)
    s, sg_b, ga, gb, merged_b, r2, x2 = run(
        _mix_fwd_b, "mix_fwd_b", st_re, st_im, wc_re4, wc_im4, su, dvec, W["ssm_w_glu"], g_conv, g_ssm, y_conv,
        w_mo, x1, sp["ln2_g"], sp["ln2_b"], tm)
    w2o2 = W["ffn2_w_out"].reshape(2, FFH, D)
    h2, r3, x3, x3b, x2b = run(_ffn_fwd, "ffn2_fwd", x2, W["ffn2_w_in"], w2o2, sp["ln3_g"], sp["ln3_b"], tm,
                               "ffn2_fwd")
    loss_part, dx3, p_b, dpw_b, dgt_b, dg4, db4 = _ple_loss(
        x3, x3b, p, W["ple_w_in"], W["ple_w_gate"].reshape(D, D), sp["ln4_g"], sp["ln4_b"], target, tm)

    dw("dw_ple_gate", "ple_w_gate", x3b, dgt_b, 512, 1024, (4, 256, D))
    dw("dw_ple_in", "ple_w_in", p_b, dpw_b, 256, 256, (4, 256, 256), shard_cols=256)
    dx2, dh2, a2_b, df2_b, dg3, db3 = run(_ffn_bwd, "ffn2_bwd", dx3, r3, sp["ln3_g"], h2, W["ffn2_w_in"], w2o2,
                                          tm, "ffn2_bwd")
    dw("dw_ffn2_in", "ffn2_w_in", x2b, dh2, 512, FFH, (4, D, FFH), shard_cols=FFH, interleaved=True)
    dw("dw_ffn2_out", "ffn2_w_out", a2_b, df2_b, FFH, 1024, (4, FF // 4, D))
    (dres, dmix_b, dgl_b, ds_b, du_dir, gs_re, gs_im, dyc_b, dproj, dg2, db2, dd) = run(
        _mix_bwd_b, "mix_bwd_b", dx2, r2, sp["ln2_g"], w_mo, g_conv, g_ssm, y_conv, ga, gb, s, su, dvec,
        W["ssm_w_glu"], wc_re4, wc_im4, tm)
    dw("dw_mix_out", "mix_w_out", merged_b, dmix_b, 512, 1024, (4, 256, D))
    dw("dw_glu", "ssm_w_glu", sg_b, dgl_b, 512, 512, (4, SSM, 512), shard_cols=512)
    dsu_ssm, dwb_re, dwb_im, dwc_re, dwc_im, da_re, da_im = run(
        _s5_scan_bwd, "s5_scan_bwd", gs_re, gs_im, st_re, st_im, su_b, ds_b, wb_re, wb_im, a_re, a_im)
    dw("dw_conv_out", "conv_w_out", yin_b, dyc_b, 512, 256, (4, CONV, 256), shard_cols=256)
    dproj, dx1, dcw8, dcb = run(_mix_bwd_a, "mix_bwd_a", dyc_b, W["conv_w_out"], pc, z_b, conv_w, dsu_ssm,
                                du_dir, dproj, dres, W["mix_w_in"], tm)
    dw("dw_mix_in", "mix_w_in", x1b, dproj, 512, 1024, (4, D, D), shard_cols=1024)
    dx0, dh1, a1_b, df1_b, dg1, db1 = run(_ffn_bwd, "ffn1_bwd", dx1, r1, sp["ln1_g"], h1, W["ffn1_w_in"],
                                          W["ffn1_w_out"].reshape(2, FFH, D), tm, "ffn1_bwd")
    sched.small(dict(
        ln1_g=dg1, ln1_b=db1, ln2_g=dg2, ln2_b=db2, ln3_g=dg3, ln3_b=db3, ln4_g=dg4, ln4_b=db4,
        conv_w=dcw8[0:3], conv_b=dcb,
        a_re=da_re.reshape(GROUPS, STATE), a_im=da_im.reshape(GROUPS, STATE),
        bb_re=_wb_diag(dwb_re), bb_im=_wb_diag(dwb_im),
        ssm_c_re=_wc_diag(dwc_re), ssm_c_im=-_wc_diag(dwc_im), ssm_d=dd.reshape(GROUPS, 16),
        loss=loss_part[0:1, 0]))
    dw("dw_ffn1_in", "ffn1_w_in", xb, dh1, 512, FFH, (4, D, FFH), shard_cols=FFH, interleaved=True)
    dw("dw_ffn1_out", "ffn1_w_out", a1_b, df1_b, FFH, 1024, (4, FF // 4, D))
    return loss_part[0, 0], dx0


RAW_ORDER = ["ln1_g", "ln1_b", "ln2_g", "ln2_b", "ln3_g", "ln3_b", "ln4_g", "ln4_b", "conv_w", "conv_b",
             "a_re", "a_im", "bb_re", "bb_im", "ssm_c_re", "ssm_c_im", "ssm_d", "loss"]

GATHER_FIRST = ["ffn1_w_in", "ffn1_w_out"]
GATHER_AT = {"ffn1_fwd": ["mix_w_in", "conv_w_out", "conv_w"], "mix_fwd_a": ["ssm_w_glu", "mix_w_out"],
             "s5_scan_fwd": ["ffn2_w_in"], "mix_fwd_b": ["ffn2_w_out"], "ffn2_fwd": ["ple_w_in", "ple_w_gate"]}
REDUCE_GROUP = {"ple": ["ple_w_gate", "ple_w_in"], "ffn2": ["ffn2_w_in", "ffn2_w_out"],
                "mix": ["mix_w_out", "ssm_w_glu", "conv_w_out", "mix_w_in"], "ffn1": ["ffn1_w_in", "ffn1_w_out"]}
REDUCE_AT = {"ffn2_bwd": [("swap", "ple")], "dw_ffn2_in": [("exchange", "ple")],
             "mix_bwd_b": [("swap", "ffn2"), ("join", "ple")],
             "mix_bwd_a": [("join", "ffn2")]}
BEGIN_AT = {"dw_mix_out": [("exchange", "ffn2")], "ffn1_bwd": [("swap", "mix")],
            "dw_ffn1_in": [("small", None), ("exchange", "mix")]}
BEHIND = {"dw_glu": [("exchange", "ffn2")], "s5_scan_bwd": [("exchange", "ffn2")]}
END_AT = {"mix_bwd_a": [("exchange", "ffn2", ["dw_mix_out", "dw_glu", "s5_scan_bwd"])],
          "dw_ffn1_in": [("swap", "mix", ["ffn1_bwd"])]}
LAST_GROUP = "ffn1"


class _Sched:
    def __init__(self, cmidx):
        self.bufs, self.cmidx = {}, cmidx
        self.W, self.G, self.raw, self.small_buf = {}, {}, None, None
        self.got1, self.p32, self.pbf, self.got2, self.half, self.theirs = {}, {}, {}, {}, {}, {}
        self._open, self._split, self.done = [], {}, {}

    def first_begin(self, bufs):
        self.bufs.update(bufs)
        p = _gather_ici_payload([bufs[n] for n in GATHER_FIRST])
        self._first = (p, _split_start(p, "gather_first_start"))
        return self._first[1][3]

    def first_end(self, bufs, after):
        self.bufs.update(bufs)
        p, handle = self._first
        _, landed = _split_wait(p, handle, after, "gather_first_wait")
        (outs,) = _comm_call("gather_first_pass", [_gather_pass_payload(landed)])
        self.W.update(zip(GATHER_FIRST, outs))

    def _payload(self, stage, key):
        if stage == "gather":
            return _gather_payload([self.bufs[n] for n in key])
        if stage == "small":
            return _allgather_payload(_pack([self.raw[k] for k in RAW_ORDER]))
        names = REDUCE_GROUP[key]
        if stage == "swap":
            return _swap_payload([self.G[n] for n in names])
        if stage == "exchange":
            for n in names:
                self.p32[n], self.pbf[n] = _pair_sum(self.cmidx, self.G[n], self.got1[n], "pair_sum_" + n)
            return _exchange_payload([self.pbf[n] for n in names])
        for n in names:
            self.half[n] = _chip_sum(self.p32[n], self.got2[n], "chip_sum_" + n)
        return _join_payload([self.half[n] for n in names])

    def _store(self, stages, got):
        for (stage, key), outs in zip(stages, got):
            if stage == "gather":
                self.W.update(zip(key, outs))
            elif stage == "small":
                self.small_buf = outs[0]
            else:
                {"swap": self.got1, "exchange": self.got2, "join": self.theirs}[stage].update(
                    zip(REDUCE_GROUP[key], outs))

    def _standalone(self, name, stages):
        self._store(stages, _comm_call(name, [self._payload(s, k) for s, k in stages]))

    def carry(self, name):
        for stage, key, behind in END_AT.get(name, []):
            self._end(stage, key, [self.done[b] for b in behind])
        tokens = [self._begin(stage, key) for stage, key in BEGIN_AT.get(name, [])]
        tokens += [self._split[sk][1][3] for sk in BEHIND.get(name, [])]
        self._open = [("gather", GATHER_AT[name])] if name in GATHER_AT else []
        self._open += REDUCE_AT.get(name, [])
        comm = [self._payload(s, k) for s, k in self._open]
        if tokens:
            comm.append(_Payload(tokens, [], {}, [], lambda *a: None, lambda *a: None))
        return tuple(comm)

    def landed(self, name, got):
        self._store(self._open, got)

    def grad(self, name, g4):
        self.G[name] = g4

    def small(self, raw):
        self.raw = raw

    def _begin(self, stage, key):
        p = self._payload(stage, key)
        self._split[stage, key] = (p, _split_start(p, "%s_%s_start" % (stage, key)))
        return self._split[stage, key][1][3]

    def _end(self, stage, key, after):
        p, handle = self._split.pop((stage, key))
        srcs, lands = _split_wait(p, handle, after, "%s_%s_wait" % (stage, key))
        if stage == "swap":
            self.G.update(zip(REDUCE_GROUP[key], srcs))
        self._store([(stage, key)], [lands])

    def tail_begin(self):
        return self._begin("swap", LAST_GROUP)

    def tail_mid(self, after):
        self._end("swap", LAST_GROUP, after)
        token = self._begin("exchange", LAST_GROUP)
        self._end("small", None, [token])
        self._end("exchange", "mix", [token])
        self._standalone("reduce_tail_join_mix", [("join", "mix")])
        return token

    def tail_end(self, after):
        self._end("exchange", LAST_GROUP, after)
        self._standalone("reduce_tail_join", [("join", LAST_GROUP)])


def _small_grads(raw_sum, sp):
    _, vjp = jax.vjp(_zoh, sp["ssm_lam_re"], sp["ssm_lam_im"], sp["ssm_log_step"], sp["ssm_b_re"], sp["ssm_b_im"])
    d_lre, d_lim, d_ls, d_bre, d_bim = vjp((raw_sum["a_re"], raw_sum["a_im"], raw_sum["bb_re"], raw_sum["bb_im"]))
    g = {k: raw_sum[k] for k in ("ln1_g", "ln1_b", "ln2_g", "ln2_b", "ln3_g", "ln3_b", "ln4_g", "ln4_b",
                                 "conv_w", "conv_b", "ssm_c_re", "ssm_c_im", "ssm_d")}
    g.update(ssm_lam_re=d_lre, ssm_lam_im=d_lim, ssm_log_step=d_ls, ssm_b_re=d_bre, ssm_b_im=d_bim)
    return g


def kernel(x, p, ffn1_w_in, ffn1_w_out, ln1_g, ln1_b, mix_w_in, conv_w, conv_b, conv_w_out, ssm_lam_re, ssm_lam_im, ssm_log_step, ssm_b_re, ssm_b_im, ssm_c_re, ssm_c_im, ssm_d, ssm_w_glu, mix_w_out, ln2_g, ln2_b, ffn2_w_in, ffn2_w_out, ln3_g, ln3_b, ple_w_in, ple_w_gate, ln4_g, ln4_b, loss_target, m_ffn1_w_in, m_ffn1_w_out, m_ln1_g, m_ln1_b, m_mix_w_in, m_conv_w, m_conv_b, m_conv_w_out, m_ssm_lam_re, m_ssm_lam_im, m_ssm_log_step, m_ssm_b_re, m_ssm_b_im, m_ssm_c_re, m_ssm_c_im, m_ssm_d, m_ssm_w_glu, m_mix_w_out, m_ln2_g, m_ln2_b, m_ffn2_w_in, m_ffn2_w_out, m_ln3_g, m_ln3_b, m_ple_w_in, m_ple_w_gate, m_ln4_g, m_ln4_b, v_ffn1_w_in, v_ffn1_w_out, v_ln1_g, v_ln1_b, v_mix_w_in, v_conv_w, v_conv_b, v_conv_w_out, v_ssm_lam_re, v_ssm_lam_im, v_ssm_log_step, v_ssm_b_re, v_ssm_b_im, v_ssm_c_re, v_ssm_c_im, v_ssm_d, v_ssm_w_glu, v_mix_w_out, v_ln2_g, v_ln2_b, v_ffn2_w_in, v_ffn2_w_out, v_ln3_g, v_ln3_b, v_ple_w_in, v_ple_w_gate, v_ln4_g, v_ln4_b):
    args = dict(locals())
    w = {n: args[n] for n in WEIGHTS}
    m = {n: args["m_" + n] for n in WEIGHTS}
    v = {n: args["v_" + n] for n in WEIGHTS}
    _, _, c, me = _where()
    cidx = jnp.stack([c, me]).astype(jnp.int32)
    meidx = jnp.reshape(me, (1,)).astype(jnp.int32)

    sched = _Sched(cidx)
    token = sched.first_begin({n: _slot_cast(meidx, w[n][0], BF16, "cast_" + n) for n in GATHER_FIRST})
    rest = {n: _slot_cast(meidx, w[n][0], BF16, "cast_" + n, (token,)) for n in BIG if n not in GATHER_FIRST}
    rest["conv_w"] = _slot_cast(meidx, jnp.pad(conv_w[0], ((0, 13), (0, 0))), F32, "cast_conv_w", (token,))
    sp = {n: (w[n] if w[n].ndim == 2 and n != "ssm_log_step" else w[n][0]) for n in SMALL if n != "conv_w"}
    ops = _s5_operands({**sp, "ssm_lam_re": sp["ssm_lam_re"] + token[0, 0]})
    sched.first_end(rest, list(rest.values()) + list(ops))
    loss_part, dx0 = _local_step(x[0], p[0, 0], loss_target[0], sp, ops, sched)
    out_g, out_d, out_m, out_v = {}, {}, {}, {}

    def big_adamw(names, token):
        for n in names:
            g, dl, mn, vn = _adamw_pair(cidx, w[n][0], sched.half[n], sched.theirs[n], m[n][0], v[n][0], token,
                                        "adamw_" + n)
            out_g[n], out_d[n], out_m[n], out_v[n] = g[None], dl[None], mn[None], vn[None]

    first = REDUCE_GROUP["ple"] + ["ffn2_w_in"]
    big_adamw(first, sched.tail_begin())
    token = sched.tail_mid([out_v[n] for n in first])
    big_adamw(["ffn2_w_out"], token)

    raw_shapes = [sched.raw[k].shape for k in RAW_ORDER]
    raw_sum = dict(zip(RAW_ORDER, _unpack(_sum8(sched.small_buf, token), raw_shapes)))
    loss = raw_sum["loss"][0]
    sg = _small_grads(raw_sum, sp)
    sg["conv_w"] = lax.dynamic_slice_in_dim(sg["conv_w"], me * 128, 128, axis=1)
    small_shapes = [w[n].shape for n in SMALL]
    gp = _pack([sg[n] for n in SMALL])
    d_s, m_s, v_s = _adamw(_pack([w[n] for n in SMALL]), gp, _pack([m[n] for n in SMALL]),
                           _pack([v[n] for n in SMALL]), "adamw_small")

    for n, a, b_, c_, d_ in zip(SMALL, _unpack(gp, small_shapes), _unpack(d_s, small_shapes),
                                _unpack(m_s, small_shapes), _unpack(v_s, small_shapes)):
        out_g[n], out_d[n], out_m[n], out_v[n] = a, b_, c_, d_
    big_adamw(REDUCE_GROUP["mix"], token)
    sched.tail_end([d_s, out_v["ffn2_w_out"]] + [out_v[n] for n in REDUCE_GROUP["mix"]])
    big_adamw(REDUCE_GROUP[LAST_GROUP], token)

    return (loss, dx0[None], *[out_g[n] for n in WEIGHTS], *[out_d[n] for n in WEIGHTS],
            *[out_m[n] for n in WEIGHTS], *[out_v[n] for n in WEIGHTS])
```

```python
import functools
import math

import jax
import jax.numpy as jnp
import numpy as np
from jax import lax
from jax.experimental import pallas as pl
from jax.experimental.pallas import tpu as pltpu

F32, BF16 = jnp.float32, jnp.bfloat16
D = 1024
FF = 2816
FFH = FF // 2
CONV = 512
SSM = 512
GROUPS = 32
STATE = 64
LANES = GROUPS * STATE
SCAN_W = 128
SCAN_PER = 512 // SCAN_W
SCAN_GR = SCAN_W // STATE
SCAN_R = 256
TOKEN_TILE = 256
ALPHA = 2.0 ** 0.25
LN_EPS = 1e-5
GELU_C = math.sqrt(2.0 / math.pi)
B1, B2, LR, EPS, WD, STEP = 0.9, 0.999, 0.001, 1e-8, 0.01, 10
MESH = pl.DeviceIdType.MESH
ANY = pl.BlockSpec(memory_space=pl.ANY)
VMEM_FULL = pl.BlockSpec(memory_space=pltpu.VMEM)


def _cp(vmem_mb=48, n_axes=1):
    return pltpu.CompilerParams(vmem_limit_bytes=vmem_mb << 20,
                                dimension_semantics=("arbitrary",) * n_axes)


def _hbm(*arrs):
    return [pltpu.with_memory_space_constraint(a, pltpu.HBM) for a in arrs]


def _hbm_out(shapes):
    if isinstance(shapes, (list, tuple)):
        return [pltpu.HBM(s.shape, s.dtype) for s in shapes]
    return pltpu.HBM(shapes.shape, shapes.dtype)


def _nn(a, b):
    return jnp.dot(a, b, preferred_element_type=F32)


def _nt(a, b):
    return lax.dot_general(a, b, (((1,), (1,)), ((), ())), preferred_element_type=F32)


def _tn(a, b):
    return lax.dot_general(a, b, (((0,), (0,)), ((), ())), preferred_element_type=F32)


def _sig(v):
    return jax.nn.sigmoid(v)


def _ln_stats(r):
    mu = jnp.mean(r, axis=-1, keepdims=True)
    xc = r - mu
    var = jnp.mean(xc * xc, axis=-1, keepdims=True)
    rstd = lax.rsqrt(var + LN_EPS)
    return xc * rstd, rstd


def _ln_bwd(dy, r, g):
    xhat, rstd = _ln_stats(r)
    dyg = dy * g
    m1 = jnp.mean(dyg, axis=-1, keepdims=True)
    m2 = jnp.mean(dyg * xhat, axis=-1, keepdims=True)
    return rstd * (dyg - m1 - xhat * m2), xhat


def _rowsum(v):
    return jnp.sum(v, axis=0, keepdims=True)


class _Payload:
    def __init__(self, operands, outs, aliases, sems, start, finish):
        self.operands, self.outs, self.aliases, self.sems = list(operands), list(outs), dict(aliases), list(sems)
        self.start, self.finish = start, finish


def _split(flat, comm, attr):
    out, i = [], 0
    for p in comm:
        n = len(getattr(p, attr))
        out.append(list(flat[i:i + n]))
        i += n
    return out


def _run_comm(comm, which, cin, cout, csem):
    for p, a, b, s in zip(comm, _split(cin, comm, "operands"), _split(cout, comm, "outs"), _split(csem, comm, "sems")):
        getattr(p, which)(a, b, s)


def _pcall(body, *, name, grid, in_specs, out_specs, out_shape, operands, scratch=(), vmem_mb=48, aliases=None,
           comm=()):
    ni, no, ns = len(in_specs), len(out_specs), len(scratch)
    c_ops = [a for p in comm for a in p.operands]
    c_outs = [s for p in comm for s in p.outs]
    c_sems = [s for p in comm for s in p.sems]
    io = dict(aliases or {})
    off_i, off_o = ni, no
    for p in comm:
        for a, b in p.aliases.items():
            io[off_i + a] = off_o + b
        off_i += len(p.operands)
        off_o += len(p.outs)

    def wrapped(*refs):
        ins, cin = refs[:ni], refs[ni:ni + len(c_ops)]
        o0 = ni + len(c_ops)
        outs, cout = refs[o0:o0 + no], refs[o0 + no:o0 + no + len(c_outs)]
        s0 = o0 + no + len(c_outs)
        scr, csem = refs[s0:s0 + ns], refs[s0 + ns:]
        if comm:
            first = functools.reduce(jnp.logical_and, [pl.program_id(a) == 0 for a in range(len(grid))])
            pl.when(first)(lambda: _run_comm(comm, "start", cin, cout, csem))
        body(*ins, *outs, *scr)
        if comm:
            last = functools.reduce(jnp.logical_and, [pl.program_id(a) == grid[a] - 1 for a in range(len(grid))])
            pl.when(last)(lambda: _run_comm(comm, "finish", cin, cout, csem))

    res = pl.pallas_call(
        wrapped, name=name, grid=grid,
        in_specs=list(in_specs) + [ANY] * len(c_ops), out_specs=list(out_specs) + [ANY] * len(c_outs),
        out_shape=_hbm_out(list(out_shape) + c_outs), scratch_shapes=list(scratch) + c_sems,
        input_output_aliases=io,
        compiler_params=pltpu.CompilerParams(vmem_limit_bytes=vmem_mb << 20,
                                             dimension_semantics=("arbitrary",) * len(grid),
                                             has_side_effects=bool(c_sems)),
    )(*_hbm(*operands, *c_ops))
    return list(res[:no]), _split(res[no:], comm, "outs")


def _comm_call(name, comm):
    c_ops = [a for p in comm for a in p.operands]
    c_outs = [s for p in comm for s in p.outs]
    c_sems = [s for p in comm for s in p.sems]
    io, off_i, off_o = {}, 0, 0
    for p in comm:
        for a, b in p.aliases.items():
            io[off_i + a] = off_o + b
        off_i += len(p.operands)
        off_o += len(p.outs)

    def body(*refs):
        cin, cout = refs[:len(c_ops)], refs[len(c_ops):len(c_ops) + len(c_outs)]
        csem = refs[len(c_ops) + len(c_outs):]
        _run_comm(comm, "start", cin, cout, csem)
        _run_comm(comm, "finish", cin, cout, csem)

    res = pl.pallas_call(
        body, name=name, in_specs=[ANY] * len(c_ops), out_specs=[ANY] * len(c_outs), out_shape=_hbm_out(c_outs),
        scratch_shapes=c_sems, input_output_aliases=io,
        compiler_params=pltpu.CompilerParams(has_side_effects=True),
    )(*_hbm(*c_ops))
    return _split(res, comm, "outs")


def _ffn_fwd(x, w_in4, w_out2, g, b, tm, name, comm=()):
    T = x.shape[0]

    def body(x_ref, win_ref, wo_ref, g_ref, b_ref, h_ref, r_ref, xo_ref, xob_ref, xib_ref):
        xf = x_ref[...]
        xv = xf.astype(BF16)
        xib_ref[...] = xv
        acc = ALPHA * xf
        for k in range(2):
            gt = _nn(xv, win_ref[k])
            up = _nn(xv, win_ref[k + 2])
            a = (gt * _sig(gt) * up).astype(BF16)
            h_ref[:, 2 * k * FFH:(2 * k + 1) * FFH] = gt.astype(BF16)
            h_ref[:, (2 * k + 1) * FFH:(2 * k + 2) * FFH] = up.astype(BF16)
            acc = acc + 0.5 * _nn(a, wo_ref[k])
        xhat, _ = _ln_stats(acc)
        xo = xhat * g_ref[...] + b_ref[...]
        r_ref[...] = acc
        xo_ref[...] = xo
        xob_ref[...] = xo.astype(BF16)

    tok = pl.BlockSpec((tm, D), lambda i: (i, 0))
    vec = pl.BlockSpec((1, D), lambda i: (0, 0))
    return _pcall(
        body, name=name, grid=(T // tm,),
        in_specs=[tok,
                  pl.BlockSpec((4, D, FFH), lambda i: (0, 0, 0), pipeline_mode=pl.Buffered(1)),
                  pl.BlockSpec((2, FFH, D), lambda i: (0, 0, 0), pipeline_mode=pl.Buffered(1)),
                  vec, vec],
        out_specs=[pl.BlockSpec((tm, 2 * FF), lambda i: (i, 0)), tok, tok, tok, tok],
        out_shape=[jax.ShapeDtypeStruct((T, 2 * FF), BF16), jax.ShapeDtypeStruct((T, D), F32),
                   jax.ShapeDtypeStruct((T, D), F32), jax.ShapeDtypeStruct((T, D), BF16),
                   jax.ShapeDtypeStruct((T, D), BF16)],
        vmem_mb=58, comm=comm, operands=(x, w_in4, w_out2, g, b))


def _ffn_bwd(dy, r, g, h, w_in4, w_out2, tm, name, comm=()):
    T = dy.shape[0]

    def body(dy_ref, r_ref, g_ref, h_ref, win_ref, wo_ref, dx_ref, dh_ref, a_ref, df_ref, dg_ref, db_ref):
        i = pl.program_id(0)
        dyv = dy_ref[...]
        dr, xhat = _ln_bwd(dyv, r_ref[...], g_ref[...])
        dg_ref[...] = jnp.where(i == 0, 0.0, dg_ref[...]) + _rowsum(dyv * xhat)
        db_ref[...] = jnp.where(i == 0, 0.0, db_ref[...]) + _rowsum(dyv)
        dfb = (0.5 * dr).astype(BF16)
        df_ref[...] = dfb
        acc = ALPHA * dr
        for k in range(2):
            da = _nt(dfb, wo_ref[k])
            gt = h_ref[:, 2 * k * FFH:(2 * k + 1) * FFH].astype(F32)
            up = h_ref[:, (2 * k + 1) * FFH:(2 * k + 2) * FFH].astype(F32)
            sg = _sig(gt)
            silu = gt * sg
            dgate = (da * up * (sg * (1.0 + gt * (1.0 - sg)))).astype(BF16)
            dup = (da * silu).astype(BF16)
            a_ref[:, k * FFH:(k + 1) * FFH] = (silu * up).astype(BF16)
            dh_ref[:, 2 * k * FFH:(2 * k + 1) * FFH] = dgate
            dh_ref[:, (2 * k + 1) * FFH:(2 * k + 2) * FFH] = dup
            acc = acc + _nt(dgate, win_ref[k]) + _nt(dup, win_ref[k + 2])
        dx_ref[...] = acc

    tok = pl.BlockSpec((tm, D), lambda i: (i, 0))
    vec = pl.BlockSpec((1, D), lambda i: (0, 0))
    wide = pl.BlockSpec((tm, 2 * FF), lambda i: (i, 0))
    return _pcall(
        body, name=name, grid=(T // tm,),
        in_specs=[tok, tok, vec, wide,
                  pl.BlockSpec((4, D, FFH), lambda i: (0, 0, 0), pipeline_mode=pl.Buffered(1)),
                  pl.BlockSpec((2, FFH, D), lambda i: (0, 0, 0), pipeline_mode=pl.Buffered(1))],
        out_specs=[tok, wide, pl.BlockSpec((tm, FF), lambda i: (i, 0)), tok, vec, vec],
        out_shape=[jax.ShapeDtypeStruct((T, D), F32), jax.ShapeDtypeStruct((T, 2 * FF), BF16),
                   jax.ShapeDtypeStruct((T, FF), BF16), jax.ShapeDtypeStruct((T, D), BF16),
                   jax.ShapeDtypeStruct((1, D), F32), jax.ShapeDtypeStruct((1, D), F32)],
        vmem_mb=58, comm=comm, operands=(dy, r, g, h, w_in4, w_out2))


def _mm_tn(a, b, tk, tn, name, shard_cols=None, interleaved=False, comm=()):
    T, K = a.shape
    N = b.shape[1]

    def body(a_ref, b_ref, o_ref):
        o_ref[...] = _tn(a_ref[...], b_ref[...])

    if shard_cols is None:
        out_shape = jax.ShapeDtypeStruct((K, N), F32)
        out_spec = pl.BlockSpec((tk, tn), lambda ki, nj: (ki, nj))
    else:
        per = shard_cols // tn

        def shard(nj):
            blk = nj // per
            return (blk % 2) * 2 + blk // 2 if interleaved else blk

        out_shape = jax.ShapeDtypeStruct((N // shard_cols, K, shard_cols), F32)
        out_spec = pl.BlockSpec((None, tk, tn), lambda ki, nj: (shard(nj), ki, nj % per))
    (out,), got = _pcall(
        body, name=name, grid=(K // tk, N // tn),
        in_specs=[pl.BlockSpec((T, tk), lambda ki, nj: (0, ki)), pl.BlockSpec((T, tn), lambda ki, nj: (0, nj))],
        out_specs=[out_spec], out_shape=[out_shape], comm=comm, operands=(a, b))
    return out, got


def _mix_fwd_a(xb, w_mix4, conv_w, conv_b, w_co4, tm, comm=()):
    T = xb.shape[0]

    def body(xb_ref, w_ref, cw_ref, cb_ref, wco_ref,
             pc_ref, z_ref, yin_ref, su_ref, sub_ref, gc_ref, gs_ref, yc_ref, qbuf):
        @pl.when(pl.program_id(0) == 0)
        def _():
            qbuf[pl.ds(0, 8), :] = jnp.zeros((8, CONV), F32)

        xv = xb_ref[...]
        p0 = _nn(xv, w_ref[0])
        p1 = _nn(xv, w_ref[1])
        gc_ref[...] = _nn(xv, w_ref[2]).astype(BF16)
        gs_ref[...] = _nn(xv, w_ref[3]).astype(BF16)
        cbv, ccv = p0[:, :CONV], p0[:, CONV:]
        chv, suv = p1[:, :CONV], p1[:, CONV:]
        q = ccv * chv
        qbuf[pl.ds(8, tm), :] = q
        cw = cw_ref[...]
        z = (cw[2:3] * q + cw[1:2] * qbuf[pl.ds(7, tm), :] + cw[0:1] * qbuf[pl.ds(6, tm), :]
             + cb_ref[...])
        qbuf[pl.ds(0, 8), :] = q[tm - 8:tm]
        yin = (cbv * z).astype(BF16)
        pc_ref[:, 0:CONV] = cbv.astype(BF16)
        pc_ref[:, CONV:2 * CONV] = ccv.astype(BF16)
        pc_ref[:, 2 * CONV:3 * CONV] = chv.astype(BF16)
        z_ref[...] = z.astype(BF16)
        yin_ref[...] = yin
        su_ref[...] = suv
        sub_ref[...] = suv.astype(BF16)
        for k in range(4):
            yc_ref[:, 256 * k:256 * (k + 1)] = _nn(yin, wco_ref[k]).astype(BF16)

    def tok(n):
        return pl.BlockSpec((tm, n), lambda i: (i, 0))

    def full(shape):
        return pl.BlockSpec(shape, lambda i: (0,) * len(shape))

    return _pcall(
        body, name="mix_fwd_a", grid=(T // tm,),
        in_specs=[tok(D), full((4, D, D)), full((3, CONV)), full((1, CONV)), full((4, CONV, 256))],
        out_specs=[tok(3 * CONV), tok(CONV), tok(CONV), tok(SSM), tok(SSM), tok(D), tok(D), tok(D)],
        out_shape=[jax.ShapeDtypeStruct((T, 3 * CONV), BF16), jax.ShapeDtypeStruct((T, CONV), BF16),
                   jax.ShapeDtypeStruct((T, CONV), BF16), jax.ShapeDtypeStruct((T, SSM), F32),
                   jax.ShapeDtypeStruct((T, SSM), BF16), jax.ShapeDtypeStruct((T, D), BF16),
                   jax.ShapeDtypeStruct((T, D), BF16), jax.ShapeDtypeStruct((T, D), BF16)],
        scratch=[pltpu.VMEM((tm + 8, CONV), F32)], vmem_mb=56, comm=comm,
        operands=(xb, w_mix4, conv_w, conv_b, w_co4))


def _scan_rows(bre, bim, ar, ai, T, rev, load):
    R, W, G = SCAN_R, bre.shape[1], T // 8
    if rev:
        ai = -ai

    def cmul(pr, pi, xr, xi):
        return pr * xr - pi * xi, pr * xi + pi * xr

    pw = [(ar, ai)]
    for _ in range(7):
        pw.append(cmul(ar, ai, *pw[-1]))

    def shifted(v, d, axis, n, idx):
        if rev:
            return jnp.where(idx < n - d, pltpu.roll(v, n - d, axis), 0.0)
        return jnp.where(idx >= d, pltpu.roll(v, d, axis), 0.0)

    sub8 = lax.broadcasted_iota(jnp.int32, (8, W), 0)
    inside = {d: (sub8 < 8 - d) if rev else (sub8 >= d) for d in (1, 2, 4)}
    pm = {d: (jnp.where(inside[d], pw[d - 1][0], 0.0)[None], jnp.where(inside[d], pw[d - 1][1], 0.0)[None])
          for d in (1, 2, 4)}

    def step(i, _):
        t0 = pl.multiple_of(i * R, R)
        vr, vi = load(t0)
        vr, vi = vr.reshape(R // 8, 8, W), vi.reshape(R // 8, 8, W)
        for d in (1, 2, 4):
            sh = (8 - d) if rev else d
            dr, di = cmul(pm[d][0], pm[d][1], pltpu.roll(vr, sh, 1), pltpu.roll(vi, sh, 1))
            vr, vi = vr + dr, vi + di
        bre[pl.ds(t0 + 8, R), :] = vr.reshape(R, W)
        bim[pl.ds(t0 + 8, R), :] = vi.reshape(R, W)
        return 0

    lax.fori_loop(0, T // R, step, 0)

    edge = 0 if rev else 7
    cr = bre[pl.ds(8 + edge, G, stride=8), :]
    ci = bim[pl.ds(8 + edge, G, stride=8), :]
    row = lax.broadcasted_iota(jnp.int32, (G, W), 0)
    qr, qi = pw[7]
    d = 1
    while d < G:
        dr, di = cmul(qr, qi, shifted(cr, d, 0, G, row), shifted(ci, d, 0, G, row))
        cr, ci = cr + dr, ci + di
        qr, qi = qr * qr - qi * qi, 2.0 * qr * qi
        d *= 2

    nr, ni = shifted(cr, 1, 0, G, row), shifted(ci, 1, 0, G, row)
    for r in range(8):
        pr, pi = pw[7 - r] if rev else pw[r]
        dr, di = cmul(pr, pi, nr, ni)
        bre[pl.ds(8 + r, G, stride=8), :] = bre[pl.ds(8 + r, G, stride=8), :] + dr
        bim[pl.ds(8 + r, G, stride=8), :] = bim[pl.ds(8 + r, G, stride=8), :] + di


def _scan_specs(T):
    W = SCAN_W
    lane = pl.BlockSpec((T, W), lambda j: (0, j))
    col = pl.BlockSpec((T, 128), lambda j: (0, j // SCAN_PER))
    wb = pl.BlockSpec((None, 128, W), lambda j: (j, 0, 0))
    wc = pl.BlockSpec((None, W, 128), lambda j: (j, 0, 0))
    vec = pl.BlockSpec((1, W), lambda j: (0, j))
    return lane, col, wb, wc, vec


def _s5_scan_fwd(su_b, wb_re, wb_im, a_re, a_im, comm=()):
    T = su_b.shape[0]
    W = SCAN_W

    def body(su_ref, wbr_ref, wbi_ref, ar_ref, ai_ref, sr_ref, si_ref, bre, bim):
        su = su_ref[...]
        bre[pl.ds(8, T), :] = _nn(su, wbr_ref[...])
        bim[pl.ds(8, T), :] = _nn(su, wbi_ref[...])
        _scan_rows(bre, bim, ar_ref[...], ai_ref[...], T, False,
                   lambda t0: (bre[pl.ds(t0 + 8, SCAN_R), :], bim[pl.ds(t0 + 8, SCAN_R), :]))
        sr_ref[...] = bre[pl.ds(8, T), :].astype(BF16)
        si_ref[...] = bim[pl.ds(8, T), :].astype(BF16)

    lane, col, wb, wc, vec = _scan_specs(T)
    return _pcall(
        body, name="s5_scan_fwd", grid=(LANES // W,),
        in_specs=[col, wb, wb, vec, vec],
        out_specs=[lane, lane],
        out_shape=[jax.ShapeDtypeStruct((T, LANES), BF16)] * 2,
        scratch=[pltpu.VMEM((T + 16, W), F32)] * 2, comm=comm,
        operands=(su_b, wb_re, wb_im, a_re, a_im))


def _gelu(s):
    th = jnp.tanh(GELU_C * (s + 0.044715 * s * s * s))
    return 0.5 * s * (1.0 + th), th


def _mix_fwd_b(st_re, st_im, wc_re4, wc_im4, su, dvec, w_glu4, g_conv, g_ssm, y_conv, w_mo, x1, g, b, tm, comm=()):
    T = su.shape[0]

    def body(sr_ref, si_ref, wcr_ref, wci_ref, su_ref, d_ref, wg_ref, gc_ref, gs_ref, yc_ref, wmo_ref,
             x_ref, g_ref, b_ref, s_ref, sgb_ref, ga_ref, gb_ref, mb_ref, r_ref, xo_ref):
        srb = sr_ref[...]
        sib = si_ref[...]
        ys = [_nn(srb[:, 512 * J:512 * (J + 1)], wcr_ref[J]) + _nn(sib[:, 512 * J:512 * (J + 1)], wci_ref[J])
              for J in range(4)]
        s = jnp.concatenate(ys, axis=1) + d_ref[...] * su_ref[...]
        sg, _ = _gelu(s)
        sgb = sg.astype(BF16)
        ga = jnp.concatenate([_nn(sgb, wg_ref[0]), _nn(sgb, wg_ref[1])], axis=1)
        gb = jnp.concatenate([_nn(sgb, wg_ref[2]), _nn(sgb, wg_ref[3])], axis=1)
        merged = (_sig(gc_ref[...].astype(F32)) * yc_ref[...].astype(F32)
                  + _sig(gs_ref[...].astype(F32)) * (ga * _sig(gb)))
        mb = merged.astype(BF16)
        r = ALPHA * x_ref[...] + _nn(mb, wmo_ref[...])
        xhat, _ = _ln_stats(r)
        xo = xhat * g_ref[...] + b_ref[...]
        s_ref[...] = s
        sgb_ref[...] = sgb
        ga_ref[...] = ga.astype(BF16)
        gb_ref[...] = gb.astype(BF16)
        mb_ref[...] = mb
        r_ref[...] = r
        xo_ref[...] = xo

    def tok(n):
        return pl.BlockSpec((tm, n), lambda i: (i, 0))

    def full(shape):
        return pl.BlockSpec(shape, lambda i: (0,) * len(shape))

    return _pcall(
        body, name="mix_fwd_b", grid=(T // tm,),
        in_specs=[tok(LANES), tok(LANES), full((4, 512, 128)), full((4, 512, 128)), tok(SSM), full((1, SSM)),
                  full((4, SSM, 512)), tok(D), tok(D), tok(D), full((D, D)), tok(D), full((1, D)), full((1, D))],
        out_specs=[tok(SSM), tok(SSM), tok(D), tok(D), tok(D), tok(D), tok(D)],
        out_shape=[jax.ShapeDtypeStruct((T, SSM), F32), jax.ShapeDtypeStruct((T, SSM), BF16),
                   jax.ShapeDtypeStruct((T, D), BF16), jax.ShapeDtypeStruct((T, D), BF16),
                   jax.ShapeDtypeStruct((T, D), BF16), jax.ShapeDtypeStruct((T, D), F32),
                   jax.ShapeDtypeStruct((T, D), F32)],
        vmem_mb=56, comm=comm,
        operands=(st_re, st_im, wc_re4, wc_im4, su, dvec, w_glu4, g_conv, g_ssm, y_conv, w_mo, x1, g, b))


def _ple_loss(x3, x3b, p, w_pi4, w_pg, g, b, target, tm):
    T = x3.shape[0]
    PD = p.shape[1]

    def body(x_ref, xb_ref, p_ref, wpi_ref, wpg_ref, g_ref, b_ref, t_ref,
             loss_ref, dx_ref, pb_ref, dpw_ref, dgt_ref, dg_ref, db_ref):
        i = pl.program_id(0)
        pb = p_ref[...].astype(BF16)
        pw = jnp.concatenate([_nn(pb, wpi_ref[k]) for k in range(4)], axis=1)
        gt = _nn(xb_ref[...], wpg_ref[...])
        sg = _sig(gt)
        r = ALPHA * x_ref[...] + pw * sg
        gv = g_ref[...]
        xhat, rstd = _ln_stats(r)
        err = xhat * gv + b_ref[...] - t_ref[...]
        lpart = jnp.zeros((1, 128), F32) + 0.5 * jnp.sum(jnp.mean(err * err, axis=-1, keepdims=True))
        dy = err * (1.0 / D)
        dyg = dy * gv
        m1 = jnp.mean(dyg, axis=-1, keepdims=True)
        m2 = jnp.mean(dyg * xhat, axis=-1, keepdims=True)
        dr = rstd * (dyg - m1 - xhat * m2)
        pg, pbias = _rowsum(dy * xhat), _rowsum(dy)

        @pl.when(i == 0)
        def _():
            loss_ref[...] = lpart
            dg_ref[...] = pg
            db_ref[...] = pbias

        @pl.when(i > 0)
        def _():
            loss_ref[...] += lpart
            dg_ref[...] += pg
            db_ref[...] += pbias

        dgt = (dr * pw * sg * (1.0 - sg)).astype(BF16)
        pb_ref[...] = pb
        dpw_ref[...] = (dr * sg).astype(BF16)
        dgt_ref[...] = dgt
        dx_ref[...] = ALPHA * dr + _nt(dgt, wpg_ref[...])

    def tok(n):
        return pl.BlockSpec((tm, n), lambda i: (i, 0))

    def full(shape):
        return pl.BlockSpec(shape, lambda i: (0,) * len(shape))

    return pl.pallas_call(
        body, name="ple_loss", grid=(T // tm,),
        in_specs=[tok(D), tok(D), tok(PD), full((4, PD, 256)), full((D, D)), full((1, D)), full((1, D)), tok(D)],
        out_specs=[full((1, 128)), tok(D), tok(PD), tok(D), tok(D), full((1, D)), full((1, D))],
        out_shape=_hbm_out([jax.ShapeDtypeStruct((1, 128), F32), jax.ShapeDtypeStruct((T, D), F32),
                            jax.ShapeDtypeStruct((T, PD), BF16), jax.ShapeDtypeStruct((T, D), BF16),
                            jax.ShapeDtypeStruct((T, D), BF16), jax.ShapeDtypeStruct((1, D), F32),
                            jax.ShapeDtypeStruct((1, D), F32)]),
        compiler_params=_cp(48, 1),
    )(*_hbm(x3, x3b, p, w_pi4, w_pg, g, b, target))


def _mix_bwd_b(dy, r2, g, w_mo, g_conv, g_ssm, y_conv, ga, gb, s, su, dvec, w_glu4, wc_re4, wc_im4, tm, comm=()):
    T = dy.shape[0]

    def body(dy_ref, r_ref, g_ref, wmo_ref, gc_ref, gs_ref, yc_ref, ga_ref, gb_ref, s_ref, su_ref, d_ref,
             wg_ref, wcr_ref, wci_ref,
             dres_ref, dmix_ref, dgl_ref, dsb_ref, dud_ref, gsr_ref, gsi_ref, dyc_ref, dp_ref,
             dg_ref, db_ref, dd_ref):
        i = pl.program_id(0)
        dyv = dy_ref[...]
        dr, xhat = _ln_bwd(dyv, r_ref[...], g_ref[...])
        dmix = dr.astype(BF16)
        dmerged = _nt(dmix, wmo_ref[...])
        sc, ss, sgb = (_sig(gc_ref[...].astype(F32)), _sig(gs_ref[...].astype(F32)),
                       _sig(gb_ref[...].astype(F32)))
        gav = ga_ref[...].astype(F32)
        yssm = gav * sgb
        dgc = dmerged * yc_ref[...].astype(F32) * sc * (1.0 - sc)
        dgss = dmerged * yssm * ss * (1.0 - ss)
        dyssm = dmerged * ss
        dgl = jnp.concatenate([dyssm * sgb, dyssm * gav * sgb * (1.0 - sgb)], axis=1).astype(BF16)
        dsg = (_nt(dgl[:, 0:512], wg_ref[0]) + _nt(dgl[:, 512:1024], wg_ref[1])
               + _nt(dgl[:, 1024:1536], wg_ref[2]) + _nt(dgl[:, 1536:2048], wg_ref[3]))
        sv = s_ref[...]
        _, th = _gelu(sv)
        dgelu = 0.5 * (1.0 + th) + 0.5 * sv * (1.0 - th * th) * GELU_C * (1.0 + 3.0 * 0.044715 * sv * sv)
        ds = dsg * dgelu
        dsb = ds.astype(BF16)
        pg, pb, pd = _rowsum(dyv * xhat), _rowsum(dyv), _rowsum(ds * su_ref[...])

        @pl.when(i == 0)
        def _():
            dg_ref[...] = pg
            db_ref[...] = pb
            dd_ref[...] = pd

        @pl.when(i > 0)
        def _():
            dg_ref[...] += pg
            db_ref[...] += pb
            dd_ref[...] += pd

        dres_ref[...] = ALPHA * dr
        dmix_ref[...] = dmix
        dgl_ref[...] = dgl
        dsb_ref[...] = dsb
        dud_ref[...] = ds * d_ref[...]
        for J in range(4):
            gsr_ref[:, 512 * J:512 * (J + 1)] = _nt(dsb[:, 128 * J:128 * (J + 1)], wcr_ref[J]).astype(BF16)
            gsi_ref[:, 512 * J:512 * (J + 1)] = _nt(dsb[:, 128 * J:128 * (J + 1)], wci_ref[J]).astype(BF16)
        dyc_ref[...] = (dmerged * sc).astype(BF16)
        dp_ref[:, 0:D] = dgc.astype(BF16)
        dp_ref[:, D:2 * D] = dgss.astype(BF16)

    def tok(n):
        return pl.BlockSpec((tm, n), lambda i: (i, 0))

    def full(shape):
        return pl.BlockSpec(shape, lambda i: (0,) * len(shape))

    return _pcall(
        body, name="mix_bwd_b", grid=(T // tm,),
        in_specs=[tok(D), tok(D), full((1, D)), full((D, D)), tok(D), tok(D), tok(D), tok(D), tok(D),
                  tok(SSM), tok(SSM), full((1, SSM)), full((4, SSM, 512)), full((4, 512, 128)), full((4, 512, 128))],
        out_specs=[tok(D), tok(D), tok(2 * D), tok(SSM), tok(SSM), tok(LANES), tok(LANES), tok(D),
                   pl.BlockSpec((tm, 2 * D), lambda i: (i, 1)), full((1, D)), full((1, D)), full((1, SSM))],
        out_shape=[jax.ShapeDtypeStruct((T, D), F32), jax.ShapeDtypeStruct((T, D), BF16),
                   jax.ShapeDtypeStruct((T, 2 * D), BF16), jax.ShapeDtypeStruct((T, SSM), BF16),
                   jax.ShapeDtypeStruct((T, SSM), F32), jax.ShapeDtypeStruct((T, LANES), BF16),
                   jax.ShapeDtypeStruct((T, LANES), BF16), jax.ShapeDtypeStruct((T, D), BF16),
                   jax.ShapeDtypeStruct((T, 4 * D), BF16), jax.ShapeDtypeStruct((1, D), F32),
                   jax.ShapeDtypeStruct((1, D), F32), jax.ShapeDtypeStruct((1, SSM), F32)],
        vmem_mb=56, comm=comm,
        operands=(dy, r2, g, w_mo, g_conv, g_ssm, y_conv, ga, gb, s, su, dvec, w_glu4, wc_re4, wc_im4))


def _s5_scan_bwd(gs_re, gs_im, st_re, st_im, su_b, ds_b, wb_re, wb_im, a_re, a_im, comm=()):
    T = su_b.shape[0]
    W = SCAN_W
    R = SCAN_R

    def body(gr_ref, gi_ref, sr_ref, si_ref, su_ref, ds_ref, wbr_ref, wbi_ref, ar_ref, ai_ref,
             dsu_ref, dwbr_ref, dwbi_ref, dwcr_ref, dwci_ref, dar_ref, dai_ref, gre, gim):
        j = pl.program_id(0)
        zero = jnp.zeros((8, W), F32)
        for buf in (gre, gim):
            buf[pl.ds(T + 8, 8), :] = zero
        _scan_rows(gre, gim, ar_ref[...], ai_ref[...], T, True,
                   lambda t0: (gr_ref[pl.ds(t0, R), :].astype(F32), gi_ref[pl.ds(t0, R), :].astype(F32)))
        grb = gre[pl.ds(8, T), :].astype(BF16)
        gib = gim[pl.ds(8, T), :].astype(BF16)
        part = _nt(grb, wbr_ref[...]) + _nt(gib, wbi_ref[...])

        @pl.when(j % SCAN_PER == 0)
        def _():
            dsu_ref[...] = part

        @pl.when(j % SCAN_PER > 0)
        def _():
            dsu_ref[...] += part

        su = su_ref[...]
        dwbr_ref[...] = _tn(su, grb)
        dwbi_ref[...] = _tn(su, gib)
        dsv = ds_ref[...]
        dwcr_ref[...] = _tn(sr_ref[...], dsv)
        dwci_ref[...] = _tn(si_ref[...], dsv)
        dar = jnp.zeros((1, W), F32)
        dai = jnp.zeros((1, W), F32)
        for c in range(T // R):
            xr = sr_ref[pl.ds(c * R, R), :].astype(F32)
            xi = si_ref[pl.ds(c * R, R), :].astype(F32)
            g1r = gre[pl.ds(c * R + 9, R), :]
            g1i = gim[pl.ds(c * R + 9, R), :]
            dar = dar + _rowsum(g1r * xr + g1i * xi)
            dai = dai + _rowsum(g1i * xr - g1r * xi)
        dar_ref[...] = dar
        dai_ref[...] = dai

    lane, col, wb, wc, vec = _scan_specs(T)
    return _pcall(
        body, name="s5_scan_bwd", grid=(LANES // W,),
        in_specs=[lane, lane, lane, lane, col, col, wb, wb, vec, vec],
        out_specs=[col, wb, wb, wc, wc, vec, vec],
        out_shape=[jax.ShapeDtypeStruct((T, SSM), F32),
                   jax.ShapeDtypeStruct((LANES // W, 128, W), F32), jax.ShapeDtypeStruct((LANES // W, 128, W), F32),
                   jax.ShapeDtypeStruct((LANES // W, W, 128), F32), jax.ShapeDtypeStruct((LANES // W, W, 128), F32),
                   jax.ShapeDtypeStruct((1, LANES), F32), jax.ShapeDtypeStruct((1, LANES), F32)],
        scratch=[pltpu.VMEM((T + 16, W), F32)] * 2, vmem_mb=56, comm=comm,
        operands=(gs_re, gs_im, st_re, st_im, su_b, ds_b, wb_re, wb_im, a_re, a_im))


def _mix_bwd_a(dyc_b, w_co4, pc, z_b, conv_w, dsu_ssm, du_dir, dproj, dres, w_mix4, tm, comm=()):
    T = dres.shape[0]
    nt = T // tm

    def body(dyc_ref, wco_ref, pc_ref, halo_ref, z_ref, cw_ref, dsu_ref, dud_ref, dpin_ref, dres_ref, w_ref,
             dp_ref, dx_ref, dcw_ref, dcb_ref, dzbuf, qbuf):
        i = pl.program_id(0)
        ii = nt - 1 - i

        @pl.when(i == 0)
        def _():
            dzbuf[pl.ds(tm, 8), :] = jnp.zeros((8, CONV), F32)

        dyc = dyc_ref[...]
        dyin = (_nt(dyc[:, 0:256], wco_ref[0]) + _nt(dyc[:, 256:512], wco_ref[1])
                + _nt(dyc[:, 512:768], wco_ref[2]) + _nt(dyc[:, 768:1024], wco_ref[3]))
        cbv = pc_ref[:, 0:CONV].astype(F32)
        ccv = pc_ref[:, CONV:2 * CONV].astype(F32)
        chv = pc_ref[:, 2 * CONV:3 * CONV].astype(F32)
        dcbv = dyin * z_ref[...].astype(F32)
        dz = dyin * cbv
        dzbuf[pl.ds(0, tm), :] = dz
        cw = cw_ref[...]
        dq = cw[2:3] * dz + cw[1:2] * dzbuf[pl.ds(1, tm), :] + cw[0:1] * dzbuf[pl.ds(2, tm), :]
        dzbuf[pl.ds(tm, 8), :] = dz[0:8]
        q = ccv * chv
        hq = halo_ref[:, CONV:2 * CONV].astype(F32) * halo_ref[:, 2 * CONV:3 * CONV].astype(F32)
        qbuf[pl.ds(0, 8), :] = jnp.where(ii > 0, hq, jnp.zeros_like(hq))
        qbuf[pl.ds(8, tm), :] = q
        pw = jnp.concatenate([_rowsum(dz * qbuf[pl.ds(6, tm), :]), _rowsum(dz * qbuf[pl.ds(7, tm), :]),
                              _rowsum(dz * q), jnp.zeros((5, CONV), F32)], axis=0)
        pbias = _rowsum(dz)

        @pl.when(i == 0)
        def _():
            dcw_ref[...] = pw
            dcb_ref[...] = pbias

        @pl.when(i > 0)
        def _():
            dcw_ref[...] += pw
            dcb_ref[...] += pbias

        dp0 = jnp.concatenate([dcbv, dq * chv], axis=1).astype(BF16)
        dp1 = jnp.concatenate([dq * ccv, dsu_ref[...] + dud_ref[...]], axis=1).astype(BF16)
        dp_ref[:, 0:D] = dp0
        dp_ref[:, D:2 * D] = dp1
        dx_ref[...] = (dres_ref[...] + _nt(dp0, w_ref[0]) + _nt(dp1, w_ref[1])
                       + _nt(dpin_ref[:, 0:D], w_ref[2]) + _nt(dpin_ref[:, D:2 * D], w_ref[3]))

    def tok(n):
        return pl.BlockSpec((tm, n), lambda i: (nt - 1 - i, 0))

    def full(shape):
        return pl.BlockSpec(shape, lambda i: (0,) * len(shape))

    halo = pl.BlockSpec((8, 3 * CONV), lambda i: (jnp.maximum((nt - 1 - i) * (tm // 8) - 1, 0), 0))
    return _pcall(
        body, name="mix_bwd_a", grid=(nt,),
        in_specs=[tok(D), full((4, CONV, 256)), tok(3 * CONV), halo, tok(CONV), full((3, CONV)),
                  tok(SSM), tok(SSM), pl.BlockSpec((tm, 2 * D), lambda i: (nt - 1 - i, 1)), tok(D),
                  full((4, D, D))],
        out_specs=[pl.BlockSpec((tm, 2 * D), lambda i: (nt - 1 - i, 0)), tok(D), full((8, CONV)), full((1, CONV))],
        out_shape=[jax.ShapeDtypeStruct((T, 4 * D), BF16), jax.ShapeDtypeStruct((T, D), F32),
                   jax.ShapeDtypeStruct((8, CONV), F32), jax.ShapeDtypeStruct((1, CONV), F32)],
        scratch=[pltpu.VMEM((tm + 8, CONV), F32), pltpu.VMEM((tm + 8, CONV), F32)],
        aliases={8: 0}, vmem_mb=56, comm=comm,
        operands=(dyc_b, w_co4, pc, pc, z_b, conv_w, dsu_ssm, du_dir, dproj, dres, w_mix4))


def _zoh(lam_re, lam_im, log_step, b_re, b_im):
    dt = jnp.exp(log_step)[:, None]
    mag = jnp.exp(lam_re * dt)
    abr, abi = mag * jnp.cos(lam_im * dt), mag * jnp.sin(lam_im * dt)
    nr, ni = abr - 1.0, abi
    den = lam_re * lam_re + lam_im * lam_im
    cr = (nr * lam_re + ni * lam_im) / den
    ci = (ni * lam_re - nr * lam_im) / den
    bbr = cr[..., None] * b_re - ci[..., None] * b_im
    bbi = cr[..., None] * b_im + ci[..., None] * b_re
    return abr, abi, bbr, bbi


_WB_MASK = (np.arange(8)[None, :, None]
            == SCAN_GR * np.arange(SCAN_PER)[:, None, None] + np.arange(SCAN_GR)[None, None, :]).astype(np.float32)
_EYE8 = np.eye(8, dtype=np.float32)


def _wb_blocks(bb):
    bt = bb.transpose(0, 2, 1).reshape(4, 1, 8, 16, 1, STATE)
    full = bt * _WB_MASK[None, :, :, None, :, None]
    return full.reshape(LANES // SCAN_W, 128, SCAN_W).astype(BF16)


def _wc_blocks(cc):
    ct = cc.transpose(0, 2, 1).reshape(4, 8, STATE, 1, 16)
    full = ct * _EYE8[None, :, None, :, None]
    return full.reshape(4, 512, 128).astype(BF16)


def _wb_diag(dwb):
    d6 = dwb.reshape(4, SCAN_PER, 8, 16, SCAN_GR, STATE) * _WB_MASK[None, :, :, None, :, None]
    return d6.sum(axis=(1, 4)).reshape(GROUPS, 16, STATE).transpose(0, 2, 1)


def _wc_diag(dwc):
    mask = _WB_MASK.transpose(0, 2, 1)
    d6 = dwc.reshape(4, SCAN_PER, SCAN_GR, STATE, 8, 16) * mask[None, :, :, None, :, None]
    return d6.sum(axis=4).reshape(GROUPS, STATE, 16).transpose(0, 2, 1)


def _where():
    x, y, c = lax.axis_index("x"), lax.axis_index("y"), lax.axis_index("c")
    return x, y, c, 2 * x + y


def _chip_dev(k, c):
    return (k // 2, k % 2, c)


def _slot_cast(meidx, w, dtype, name, token=()):
    R, C = w.shape
    tr = _row_tile(R)

    def body(m_ref, w_ref, *rest):
        rest[-1][...] = w_ref[...].astype(dtype)

    gs = pltpu.PrefetchScalarGridSpec(
        num_scalar_prefetch=1, grid=(R // tr,),
        in_specs=[pl.BlockSpec((tr, C), lambda i, m: (i, 0))] + [pl.BlockSpec((8, 128), lambda i, m: (0, 0))] * len(token),
        out_specs=pl.BlockSpec((None, tr, C), lambda i, m: (m[0], i, 0)))
    return pl.pallas_call(
        body, name=name, grid_spec=gs, out_shape=_hbm_out(jax.ShapeDtypeStruct((4, R, C), dtype)),
        compiler_params=_cp(32, 1),
    )(meidx, *_hbm(w), *token)


def _gather_ici_payload(bufs):
    def copies(ins, lnd, ss, rs):
        x, y, c, me = _where()
        cps = []
        for w, b in enumerate(bufs):
            h = b.shape[1] // 2
            mine = lnd[w].at[me, pl.ds(c * h, h)]
            for s in range(3):
                k = (me + 1 + s) % 4
                cps.append(pltpu.make_async_remote_copy(
                    src_ref=mine, dst_ref=mine, send_sem=ss.at[3 * w + s], recv_sem=rs.at[3 * w + s],
                    device_id=_chip_dev(k, c), device_id_type=MESH))
        return cps

    p = _sym_payload([], [jax.ShapeDtypeStruct(b.shape, b.dtype) for b in bufs], copies, 3 * len(bufs))
    p.lands = list(bufs)
    return p


def _gather_pass_payload(bufs):
    def copies(ins, outs, ss, rs):
        x, y, c, me = _where()
        cps = []
        for w, b in enumerate(bufs):
            h = b.shape[1] // 2
            for s in range(3):
                j = (me + 1 + s) % 4
                cps.append(pltpu.make_async_remote_copy(
                    src_ref=ins[w].at[j, pl.ds(c * h, h)], dst_ref=outs[w].at[j, pl.ds(c * h, h)],
                    send_sem=ss.at[3 * w + s], recv_sem=rs.at[3 * w + s], device_id=(x, y, 1 - c),
                    device_id_type=MESH))
        return cps

    p = _sym_payload(bufs, [jax.ShapeDtypeStruct(b.shape, b.dtype) for b in bufs], copies, 3 * len(bufs))
    p.aliases = {w: w for w in range(len(bufs))}
    return p


def _gather_payload(bufs):
    n = len(bufs)

    def half(ref, w, k, cc):
        h = bufs[w].shape[1] // 2
        return ref.at[k, pl.ds(cc * h, h)]

    def ici(ins, outs, sems, w, s):
        x, y, c, me = _where()
        k = (me + 1 + s) % 4
        return pltpu.make_async_remote_copy(
            src_ref=half(ins[w], w, me, c), dst_ref=half(outs[w], w, me, c), send_sem=sems[0].at[3 * w + s],
            recv_sem=sems[1].at[3 * w + s], device_id=_chip_dev(k, c), device_id_type=MESH)

    def landed(outs, sems, w, s):
        x, y, c, me = _where()
        j = (me + 3 - s) % 4
        return pltpu.make_async_remote_copy(
            src_ref=half(outs[w], w, j, c), dst_ref=half(outs[w], w, j, c), send_sem=sems[0].at[3 * w + s],
            recv_sem=sems[1].at[3 * w + s], device_id=(x, y, 1 - c), device_id_type=MESH)

    def passed(outs, sems, w, s, cc):
        x, y, c, me = _where()
        j = (me + 3 - s) % 4
        return pltpu.make_async_remote_copy(
            src_ref=half(outs[w], w, j, cc), dst_ref=half(outs[w], w, j, cc), send_sem=sems[2].at[3 * w + s],
            recv_sem=sems[3].at[3 * w + s], device_id=(x, y, 1 - c), device_id_type=MESH)

    pairs = [(w, s) for w in range(n) for s in range(3)]

    def start(ins, outs, sems):
        for w, s in pairs:
            ici(ins, outs, sems, w, s).start()

    def finish(ins, outs, sems):
        _, _, c, _ = _where()
        for w, s in pairs:
            landed(outs, sems, w, s).wait_recv()
            passed(outs, sems, w, s, c).start()
        for w, s in pairs:
            passed(outs, sems, w, s, 1 - c).wait_recv()
        for w, s in pairs:
            ici(ins, outs, sems, w, s).wait_send()
            passed(outs, sems, w, s, c).wait_send()

    return _Payload(bufs, [jax.ShapeDtypeStruct(b.shape, b.dtype) for b in bufs], {w: w for w in range(n)},
                    [pltpu.SemaphoreType.DMA((3 * n,))] * 4, start, finish)


def _sym_payload(operands, outs, copies, n_copies):
    def start(ins, outs_, sems):
        for cp in copies(ins, outs_, sems[0], sems[1]):
            cp.start()

    def finish(ins, outs_, sems):
        for cp in copies(ins, outs_, sems[0], sems[1]):
            cp.wait()

    p = _Payload(operands, outs, {}, [pltpu.SemaphoreType.DMA((n_copies,))] * 2, start, finish)
    p.copies, p.n_copies = copies, n_copies
    return p


def _swap_payload(g4s):
    def copies(ins, outs, ss, rs):
        x, y, c, me = _where()
        cps = []
        for w, g in enumerate(g4s):
            h = g.shape[1] // 2
            cps.append(pltpu.make_async_remote_copy(
                src_ref=ins[w].at[:, pl.ds((1 - c) * h, h)], dst_ref=outs[w], send_sem=ss.at[w],
                recv_sem=rs.at[w], device_id=(x, y, 1 - c), device_id_type=MESH))
        return cps

    outs = [jax.ShapeDtypeStruct((4, g.shape[1] // 2, g.shape[2]), g.dtype) for g in g4s]
    return _sym_payload(g4s, outs, copies, len(g4s))


def _exchange_payload(pbs):
    def copies(ins, outs, ss, rs):
        x, y, c, me = _where()
        cps = []
        for w in range(len(pbs)):
            for s in range(3):
                k = (me + 1 + s) % 4
                cps.append(pltpu.make_async_remote_copy(
                    src_ref=ins[w].at[k], dst_ref=outs[w].at[2 - s], send_sem=ss.at[3 * w + s],
                    recv_sem=rs.at[3 * w + s], device_id=_chip_dev(k, c), device_id_type=MESH))
        return cps

    outs = [jax.ShapeDtypeStruct((3,) + p.shape[1:], p.dtype) for p in pbs]
    return _sym_payload(pbs, outs, copies, 3 * len(pbs))


HBM_REF = pl.BlockSpec(memory_space=pltpu.HBM)
SEM_REF = pl.BlockSpec(memory_space=pltpu.SEMAPHORE)
DATAFLOW = pltpu.SideEffectType.DATAFLOW_SIDE_EFFECTING


class _SemList:
    def __init__(self, refs):
        self.refs = refs

    @property
    def at(self):
        return self.refs


def _split_start(p, name):
    n_in, n_out, nc = len(p.operands), len(p.outs), p.n_copies
    lands = getattr(p, "lands", None) or [lax.empty(s.shape, s.dtype) for s in p.outs]

    def body(*refs):
        ins, lnd = refs[:n_in], refs[n_in:n_in + n_out]
        sems = refs[n_in + n_out:n_in + n_out + 2 * nc]
        for cp in p.copies(ins, lnd, _SemList(sems[:nc]), _SemList(sems[nc:])):
            cp.start()
        refs[-1][...] = jnp.zeros((8, 128), F32)

    res = pl.pallas_call(
        body, name=name,
        in_specs=[HBM_REF] * (n_in + n_out),
        out_specs=[SEM_REF] * (2 * nc) + [HBM_REF] * (n_in + n_out) + [VMEM_FULL],
        out_shape=([pltpu.SemaphoreType.DMA(())] * (2 * nc) + _hbm_out(p.operands) + _hbm_out(lands)
                   + [jax.ShapeDtypeStruct((8, 128), F32)]),
        input_output_aliases={i: 2 * nc + i for i in range(n_in + n_out)},
        compiler_params=pltpu.CompilerParams(has_side_effects=DATAFLOW),
    )(*_hbm(*p.operands, *lands))
    k = 2 * nc
    return list(res[:k]), list(res[k:k + n_in]), list(res[k + n_in:k + n_in + n_out]), res[-1]


def _split_wait(p, handle, after, name):
    sems, srcs, lands, _ = handle
    n_in, n_out, nc = len(srcs), len(lands), p.n_copies

    def body(*refs):
        ins, lnd = refs[:n_in], refs[n_in:n_in + n_out]
        sm = refs[n_in + n_out:n_in + n_out + 2 * nc]
        for cp in p.copies(ins, lnd, _SemList(sm[:nc]), _SemList(sm[nc:])):
            cp.wait_send()
            cp.wait_recv()

    res = pl.pallas_call(
        body, name=name,
        in_specs=[HBM_REF] * (n_in + n_out) + [SEM_REF] * (2 * nc) + [ANY] * len(after),
        out_specs=[HBM_REF] * (n_in + n_out), out_shape=_hbm_out(srcs) + _hbm_out(lands),
        input_output_aliases={i: i for i in range(n_in + n_out)},
        compiler_params=pltpu.CompilerParams(has_side_effects=DATAFLOW),
    )(*srcs, *lands, *sems, *after)
    return list(res[:n_in]), list(res[n_in:])


def _join_payload(halves):
    def copies(ins, outs, ss, rs):
        x, y, c, me = _where()
        return [pltpu.make_async_remote_copy(
            src_ref=ins[w], dst_ref=outs[w], send_sem=ss.at[w], recv_sem=rs.at[w],
            device_id=(x, y, 1 - c), device_id_type=MESH) for w in range(len(halves))]

    outs = [jax.ShapeDtypeStruct(a.shape, a.dtype) for a in halves]
    return _sym_payload(halves, outs, copies, len(halves))


def _allgather_payload(v):
    def copies(ins, outs, ss, rs):
        x, y, c, me = _where()
        lin = 4 * x + 2 * y + c
        cps = []
        for o in range(1, 8):
            t = (lin + o) % 8
            cps.append(pltpu.make_async_remote_copy(
                src_ref=ins[0], dst_ref=outs[0].at[lin], send_sem=ss.at[o - 1], recv_sem=rs.at[o - 1],
                device_id=(t // 4, (t // 2) % 2, t % 2), device_id_type=MESH))
        return cps

    p = _sym_payload([v], [jax.ShapeDtypeStruct((8,) + v.shape, v.dtype)], copies, 7)
    x, y, c, _ = _where()
    p.lands = [lax.dynamic_update_slice(jnp.zeros((8,) + v.shape, v.dtype), v[None], (4 * x + 2 * y + c, 0, 0))]
    return p


def _sum8(buf, token):
    _, P, C = buf.shape

    def body(b_ref, t_ref, o_ref):
        acc = b_ref[0]
        for d in range(1, 8):
            acc = acc + b_ref[d]
        o_ref[...] = acc

    return pl.pallas_call(
        body, name="sum8", in_specs=[VMEM_FULL, VMEM_FULL], out_specs=VMEM_FULL,
        out_shape=jax.ShapeDtypeStruct((P, C), F32),
        compiler_params=pltpu.CompilerParams(vmem_limit_bytes=32 << 20),
    )(buf, token)


def _row_tile(h):
    for t in (256, 176, 128, 64, 32, 16, 8):
        if h % t == 0:
            return t
    raise ValueError(h)


def _pair_sum(cmidx, g4, got, name):
    _, R, C = g4.shape
    h = R // 2
    th = _row_tile(h)

    def body(cm_ref, a_ref, b_ref, o_ref, ob_ref):
        sm = a_ref[...] + b_ref[...]
        ob_ref[...] = sm.astype(BF16)

        @pl.when(pl.program_id(1) == cm_ref[1])
        def _():
            o_ref[...] = sm

    blk = pl.BlockSpec((None, th, C), lambda i, k, cm: (k, i, 0))
    gs = pltpu.PrefetchScalarGridSpec(
        num_scalar_prefetch=1, grid=(h // th, 4),
        in_specs=[pl.BlockSpec((None, None, th, C), lambda i, k, cm: (k, cm[0], i, 0)), blk],
        out_specs=[pl.BlockSpec((th, C), lambda i, k, cm: (i, 0)), blk])
    return pl.pallas_call(
        body, name=name, grid_spec=gs,
        out_shape=_hbm_out([jax.ShapeDtypeStruct((h, C), F32), jax.ShapeDtypeStruct((4, h, C), BF16)]),
        compiler_params=_cp(32, 2),
    )(cmidx, *_hbm(g4.reshape(4, 2, h, C), got))


def _chip_sum(own, got, name):
    h, C = own.shape
    th = _row_tile(h)

    def body(a_ref, b_ref, o_ref):
        o_ref[...] = ((a_ref[...] + b_ref[0].astype(F32)) + b_ref[1].astype(F32)) + b_ref[2].astype(F32)

    return pl.pallas_call(
        body, name=name, grid=(h // th,),
        in_specs=[pl.BlockSpec((th, C), lambda i: (i, 0)), pl.BlockSpec((3, th, C), lambda i: (0, i, 0))],
        out_specs=pl.BlockSpec((th, C), lambda i: (i, 0)),
        out_shape=_hbm_out(jax.ShapeDtypeStruct((h, C), F32)),
        compiler_params=_cp(32, 1),
    )(*_hbm(own, got))


def _adamw_math(w, g, m, v):
    m2 = B1 * m + (1.0 - B1) * g
    v2 = B2 * v + (1.0 - B2) * (g * g)
    m_hat = m2 / (1.0 - B1 ** STEP)
    v_hat = v2 / (1.0 - B2 ** STEP)
    delta = -LR * (m_hat / (jnp.sqrt(v_hat) + EPS) + WD * w)
    return delta, m2, v2


def _adamw_pair(cidx, w, mine, theirs, m, v, token, name):
    R, C = w.shape
    h = R // 2
    tr = _row_tile(h)
    nh = h // tr

    def body(c_ref, w_ref, a_ref, b_ref, m_ref, v_ref, t_ref, g_ref, d_ref, mo_ref, vo_ref):
        own = (pl.program_id(0) // nh) == c_ref[0]
        g = jnp.where(own, a_ref[...], b_ref[...])
        d, m2, v2 = _adamw_math(w_ref[...], g, m_ref[...], v_ref[...])
        g_ref[...] = g
        d_ref[...] = d
        mo_ref[...] = m2
        vo_ref[...] = v2

    blk = pl.BlockSpec((tr, C), lambda i, c: (i, 0))
    mine_blk = pl.BlockSpec((tr, C), lambda i, c: (jnp.clip(i - c[0] * nh, 0, nh - 1), 0))
    theirs_blk = pl.BlockSpec((tr, C), lambda i, c: (jnp.clip(i - (1 - c[0]) * nh, 0, nh - 1), 0))
    gs = pltpu.PrefetchScalarGridSpec(
        num_scalar_prefetch=1, grid=(R // tr,),
        in_specs=[blk, mine_blk, theirs_blk, blk, blk, pl.BlockSpec((8, 128), lambda i, c: (0, 0))],
        out_specs=[blk] * 4)
    return pl.pallas_call(
        body, name=name, grid_spec=gs, out_shape=_hbm_out([jax.ShapeDtypeStruct((R, C), F32)] * 4),
        compiler_params=_cp(32, 1),
    )(cidx, *_hbm(w, mine, theirs, m, v), token)


def _adamw(wmv, g, name):
    _, R, C = wmv.shape
    tr = _row_tile(R)

    def body(w_ref, m_ref, v_ref, g_ref, d_ref, mo_ref, vo_ref):
        d, m2, v2 = _adamw_math(w_ref[...], g_ref[...], m_ref[...], v_ref[...])
        d_ref[...] = d
        mo_ref[...] = m2
        vo_ref[...] = v2

    blk = pl.BlockSpec((tr, C), lambda i: (i, 0))
    part = [pl.BlockSpec((None, tr, C), functools.partial(lambda k, i: (k, i, 0), k)) for k in range(3)]
    return pl.pallas_call(
        body, name=name, grid=(R // tr,), in_specs=part + [blk], out_specs=[blk] * 3,
        out_shape=_hbm_out([jax.ShapeDtypeStruct((R, C), F32)] * 3),
        compiler_params=_cp(32, 1),
    )(*_hbm(wmv, wmv, wmv, g))


def _pack(arrs):
    lists = arrs if isinstance(arrs[0], (list, tuple)) else [arrs]
    n = sum(math.prod(a.shape) for a in lists[0])
    rows = -(-(-(-n // 1024)) // 8) * 8
    pad = jnp.zeros((rows * 1024 - n,), F32)
    flat = jnp.concatenate([piece for lst in lists for piece in [a.reshape(-1).astype(F32) for a in lst] + [pad]])
    out = flat.reshape(len(lists), rows, 1024)
    return out if isinstance(arrs[0], (list, tuple)) else out[0]


def _unpack(packed, shapes):
    flat = packed.reshape(-1)
    out, off = [], 0
    for s in shapes:
        n = math.prod(s)
        out.append(flat[off:off + n].reshape(s))
        off += n
    return out


BIG = ["ffn1_w_in", "ffn1_w_out", "mix_w_in", "conv_w_out", "ssm_w_glu", "mix_w_out",
       "ffn2_w_in", "ffn2_w_out", "ple_w_in", "ple_w_gate"]
SMALL = ["ln1_g", "ln1_b", "conv_w", "conv_b", "ssm_lam_re", "ssm_lam_im", "ssm_log_step", "ssm_b_re", "ssm_b_im",
         "ssm_c_re", "ssm_c_im", "ssm_d", "ln2_g", "ln2_b", "ln3_g", "ln3_b", "ln4_g", "ln4_b"]
WEIGHTS = ["ffn1_w_in", "ffn1_w_out", "ln1_g", "ln1_b", "mix_w_in", "conv_w", "conv_b", "conv_w_out",
           "ssm_lam_re", "ssm_lam_im", "ssm_log_step", "ssm_b_re", "ssm_b_im", "ssm_c_re", "ssm_c_im", "ssm_d",
           "ssm_w_glu", "mix_w_out", "ln2_g", "ln2_b", "ffn2_w_in", "ffn2_w_out", "ln3_g", "ln3_b",
           "ple_w_in", "ple_w_gate", "ln4_g", "ln4_b"]


def _s5_operands(sp):
    abr, abi, bbr, bbi = _zoh(sp["ssm_lam_re"], sp["ssm_lam_im"], sp["ssm_log_step"], sp["ssm_b_re"], sp["ssm_b_im"])
    return (_wb_blocks(bbr), _wb_blocks(bbi), _wc_blocks(sp["ssm_c_re"]), _wc_blocks(-sp["ssm_c_im"]),
            abr.reshape(1, LANES), abi.reshape(1, LANES), sp["ssm_d"].reshape(1, SSM))


def _local_step(x, p, target, sp, ops, sched):
    W = sched.W
    wb_re, wb_im, wc_re4, wc_im4, a_re, a_im, dvec = ops
    tm = TOKEN_TILE

    def run(fn, name, *args, **kw):
        outs, got = fn(*args, comm=sched.carry(name), **kw)
        sched.landed(name, got)
        sched.done[name] = outs[0]
        return outs

    def dw(name, wname, a, b, tk, tn, shape4, shard_cols=None, interleaved=False):
        out, got = _mm_tn(a, b, tk, tn, name, shard_cols=shard_cols, interleaved=interleaved,
                          comm=sched.carry(name))
        sched.landed(name, got)
        sched.done[name] = out
        sched.grad(wname, out.reshape(shape4))

    h1, r1, x1, x1b, xb = run(_ffn_fwd, "ffn1_fwd", x, W["ffn1_w_in"], W["ffn1_w_out"].reshape(2, FFH, D),
                              sp["ln1_g"], sp["ln1_b"], tm, "ffn1_fwd")
    conv_w = W["conv_w"][:, 0:3, :].transpose(1, 0, 2).reshape(3, CONV)
    pc, z_b, yin_b, su, su_b, g_conv, g_ssm, y_conv = run(
        _mix_fwd_a, "mix_fwd_a", x1b, W["mix_w_in"], conv_w, sp["conv_b"], W["conv_w_out"], tm)
    st_re, st_im = run(_s5_scan_fwd, "s5_scan_fwd", su_b, wb_re, wb_im, a_re, a_im)
    w_mo = W["mix_w_out"].reshape(D, D)
    s, sg_b, ga, gb, merged_b, r2, x2 = run(
        _mix_fwd_b, "mix_fwd_b", st_re, st_im, wc_re4, wc_im4, su, dvec, W["ssm_w_glu"], g_conv, g_ssm, y_conv,
        w_mo, x1, sp["ln2_g"], sp["ln2_b"], tm)
    w2o2 = W["ffn2_w_out"].reshape(2, FFH, D)
    h2, r3, x3, x3b, x2b = run(_ffn_fwd, "ffn2_fwd", x2, W["ffn2_w_in"], w2o2, sp["ln3_g"], sp["ln3_b"], tm,
                               "ffn2_fwd")
    loss_part, dx3, p_b, dpw_b, dgt_b, dg4, db4 = _ple_loss(
        x3, x3b, p, W["ple_w_in"], W["ple_w_gate"].reshape(D, D), sp["ln4_g"], sp["ln4_b"], target, tm)

    dw("dw_ple_gate", "ple_w_gate", x3b, dgt_b, 512, 1024, (4, 256, D))
    dw("dw_ple_in", "ple_w_in", p_b, dpw_b, 256, 256, (4, 256, 256), shard_cols=256)
    dx2, dh2, a2_b, df2_b, dg3, db3 = run(_ffn_bwd, "ffn2_bwd", dx3, r3, sp["ln3_g"], h2, W["ffn2_w_in"], w2o2,
                                          tm, "ffn2_bwd")
    dw("dw_ffn2_in", "ffn2_w_in", x2b, dh2, 512, FFH, (4, D, FFH), shard_cols=FFH, interleaved=True)
    dw("dw_ffn2_out", "ffn2_w_out", a2_b, df2_b, FFH, 1024, (4, FF // 4, D))
    (dres, dmix_b, dgl_b, ds_b, du_dir, gs_re, gs_im, dyc_b, dproj, dg2, db2, dd) = run(
        _mix_bwd_b, "mix_bwd_b", dx2, r2, sp["ln2_g"], w_mo, g_conv, g_ssm, y_conv, ga, gb, s, su, dvec,
        W["ssm_w_glu"], wc_re4, wc_im4, tm)
    dw("dw_mix_out", "mix_w_out", merged_b, dmix_b, 512, 1024, (4, 256, D))
    dw("dw_glu", "ssm_w_glu", sg_b, dgl_b, 512, 512, (4, SSM, 512), shard_cols=512)
    dsu_ssm, dwb_re, dwb_im, dwc_re, dwc_im, da_re, da_im = run(
        _s5_scan_bwd, "s5_scan_bwd", gs_re, gs_im, st_re, st_im, su_b, ds_b, wb_re, wb_im, a_re, a_im)
    dw("dw_conv_out", "conv_w_out", yin_b, dyc_b, 512, 256, (4, CONV, 256), shard_cols=256)
    dproj, dx1, dcw8, dcb = run(_mix_bwd_a, "mix_bwd_a", dyc_b, W["conv_w_out"], pc, z_b, conv_w, dsu_ssm,
                                du_dir, dproj, dres, W["mix_w_in"], tm)
    dw("dw_mix_in", "mix_w_in", x1b, dproj, 512, 1024, (4, D, D), shard_cols=1024)
    dx0, dh1, a1_b, df1_b, dg1, db1 = run(_ffn_bwd, "ffn1_bwd", dx1, r1, sp["ln1_g"], h1, W["ffn1_w_in"],
                                          W["ffn1_w_out"].reshape(2, FFH, D), tm, "ffn1_bwd")
    sched.small(dict(
        ln1_g=dg1, ln1_b=db1, ln2_g=dg2, ln2_b=db2, ln3_g=dg3, ln3_b=db3, ln4_g=dg4, ln4_b=db4,
        conv_w=dcw8[0:3], conv_b=dcb,
        a_re=da_re.reshape(GROUPS, STATE), a_im=da_im.reshape(GROUPS, STATE),
        bb_re=_wb_diag(dwb_re), bb_im=_wb_diag(dwb_im),
        ssm_c_re=_wc_diag(dwc_re), ssm_c_im=-_wc_diag(dwc_im), ssm_d=dd.reshape(GROUPS, 16),
        loss=loss_part[0:1, 0]))
    dw("dw_ffn1_in", "ffn1_w_in", xb, dh1, 512, FFH, (4, D, FFH), shard_cols=FFH, interleaved=True)
    dw("dw_ffn1_out", "ffn1_w_out", a1_b, df1_b, FFH, 1024, (4, FF // 4, D))
    return loss_part[0, 0], dx0


RAW_ORDER = ["ln1_g", "ln1_b", "ln2_g", "ln2_b", "ln3_g", "ln3_b", "ln4_g", "ln4_b", "conv_w", "conv_b",
             "a_re", "a_im", "bb_re", "bb_im", "ssm_c_re", "ssm_c_im", "ssm_d", "loss"]

GATHER_FIRST = ["ffn1_w_in", "ffn1_w_out"]
GATHER_AT = {"ffn1_fwd": ["mix_w_in", "conv_w_out", "conv_w"], "mix_fwd_a": ["ssm_w_glu", "mix_w_out"],
             "s5_scan_fwd": ["ffn2_w_in"], "mix_fwd_b": ["ffn2_w_out"], "ffn2_fwd": ["ple_w_in", "ple_w_gate"]}
REDUCE_GROUP = {"ple": ["ple_w_gate", "ple_w_in"], "ffn2": ["ffn2_w_in", "ffn2_w_out"],
                "mix": ["mix_w_out", "ssm_w_glu", "conv_w_out", "mix_w_in"], "ffn1": ["ffn1_w_in", "ffn1_w_out"]}
REDUCE_AT = {"ffn2_bwd": [("swap", "ple")], "dw_ffn2_in": [("exchange", "ple")],
             "mix_bwd_b": [("swap", "ffn2"), ("join", "ple")],
             "mix_bwd_a": [("join", "ffn2")]}
BEGIN_AT = {"dw_mix_out": [("exchange", "ffn2")], "ffn1_bwd": [("swap", "mix")],
            "dw_ffn1_in": [("small", None), ("exchange", "mix")]}
BEHIND = {"dw_glu": [("exchange", "ffn2")], "s5_scan_bwd": [("exchange", "ffn2")]}
END_AT = {"mix_bwd_a": [("exchange", "ffn2", ["dw_mix_out", "dw_glu", "s5_scan_bwd"])],
          "dw_ffn1_in": [("swap", "mix", ["ffn1_bwd"])]}
LAST_GROUP = "ffn1"


class _Sched:
    def __init__(self, cmidx):
        self.bufs, self.cmidx = {}, cmidx
        self.W, self.G, self.raw, self.small_buf = {}, {}, None, None
        self.got1, self.p32, self.pbf, self.got2, self.half, self.theirs = {}, {}, {}, {}, {}, {}
        self._open, self._split, self.done = [], {}, {}

    def first_begin(self, bufs):
        self.bufs.update(bufs)
        p = _gather_ici_payload([bufs[n] for n in GATHER_FIRST])
        self._first = (p, _split_start(p, "gather_first_start"))
        return self._first[1][3]

    def first_end(self, bufs, after):
        self.bufs.update(bufs)
        p, handle = self._first
        _, landed = _split_wait(p, handle, after, "gather_first_wait")
        (outs,) = _comm_call("gather_first_pass", [_gather_pass_payload(landed)])
        self.W.update(zip(GATHER_FIRST, outs))

    def _payload(self, stage, key):
        if stage == "gather":
            return _gather_payload([self.bufs[n] for n in key])
        if stage == "small":
            return _allgather_payload(_pack([self.raw[k] for k in RAW_ORDER]))
        names = REDUCE_GROUP[key]
        if stage == "swap":
            return _swap_payload([self.G[n] for n in names])
        if stage == "exchange":
            for n in names:
                self.p32[n], self.pbf[n] = _pair_sum(self.cmidx, self.G[n], self.got1[n], "pair_sum_" + n)
            return _exchange_payload([self.pbf[n] for n in names])
        for n in names:
            self.half[n] = _chip_sum(self.p32[n], self.got2[n], "chip_sum_" + n)
        return _join_payload([self.half[n] for n in names])

    def _store(self, stages, got):
        for (stage, key), outs in zip(stages, got):
            if stage == "gather":
                self.W.update(zip(key, outs))
            elif stage == "small":
                self.small_buf = outs[0]
            else:
                {"swap": self.got1, "exchange": self.got2, "join": self.theirs}[stage].update(
                    zip(REDUCE_GROUP[key], outs))

    def _standalone(self, name, stages):
        self._store(stages, _comm_call(name, [self._payload(s, k) for s, k in stages]))

    def carry(self, name):
        for stage, key, behind in END_AT.get(name, []):
            self._end(stage, key, [self.done[b] for b in behind])
        tokens = [self._begin(stage, key) for stage, key in BEGIN_AT.get(name, [])]
        tokens += [self._split[sk][1][3] for sk in BEHIND.get(name, [])]
        self._open = [("gather", GATHER_AT[name])] if name in GATHER_AT else []
        self._open += REDUCE_AT.get(name, [])
        comm = [self._payload(s, k) for s, k in self._open]
        if tokens:
            comm.append(_Payload(tokens, [], {}, [], lambda *a: None, lambda *a: None))
        return tuple(comm)

    def landed(self, name, got):
        self._store(self._open, got)

    def grad(self, name, g4):
        self.G[name] = g4

    def small(self, raw):
        self.raw = raw

    def _begin(self, stage, key):
        p = self._payload(stage, key)
        self._split[stage, key] = (p, _split_start(p, "%s_%s_start" % (stage, key)))
        return self._split[stage, key][1][3]

    def _end(self, stage, key, after):
        p, handle = self._split.pop((stage, key))
        srcs, lands = _split_wait(p, handle, after, "%s_%s_wait" % (stage, key))
        if stage == "swap":
            self.G.update(zip(REDUCE_GROUP[key], srcs))
        self._store([(stage, key)], [lands])

    def tail_begin(self):
        return self._begin("swap", LAST_GROUP)

    def tail_mid(self, after):
        self._end("swap", LAST_GROUP, after)
        token = self._begin("exchange", LAST_GROUP)
        self._end("small", None, [token])
        self._end("exchange", "mix", [token])
        self._standalone("reduce_tail_join_mix", [("join", "mix")])
        return token

    def tail_end(self, after):
        self._end("exchange", LAST_GROUP, after)
        self._standalone("reduce_tail_join", [("join", LAST_GROUP)])


def _small_grads(raw_sum, sp):
    _, vjp = jax.vjp(_zoh, sp["ssm_lam_re"], sp["ssm_lam_im"], sp["ssm_log_step"], sp["ssm_b_re"], sp["ssm_b_im"])
    d_lre, d_lim, d_ls, d_bre, d_bim = vjp((raw_sum["a_re"], raw_sum["a_im"], raw_sum["bb_re"], raw_sum["bb_im"]))
    g = {k: raw_sum[k] for k in ("ln1_g", "ln1_b", "ln2_g", "ln2_b", "ln3_g", "ln3_b", "ln4_g", "ln4_b",
                                 "conv_w", "conv_b", "ssm_c_re", "ssm_c_im", "ssm_d")}
    g.update(ssm_lam_re=d_lre, ssm_lam_im=d_lim, ssm_log_step=d_ls, ssm_b_re=d_bre, ssm_b_im=d_bim)
    return g


def kernel(x, p, ffn1_w_in, ffn1_w_out, ln1_g, ln1_b, mix_w_in, conv_w, conv_b, conv_w_out, ssm_lam_re, ssm_lam_im, ssm_log_step, ssm_b_re, ssm_b_im, ssm_c_re, ssm_c_im, ssm_d, ssm_w_glu, mix_w_out, ln2_g, ln2_b, ffn2_w_in, ffn2_w_out, ln3_g, ln3_b, ple_w_in, ple_w_gate, ln4_g, ln4_b, loss_target, m_ffn1_w_in, m_ffn1_w_out, m_ln1_g, m_ln1_b, m_mix_w_in, m_conv_w, m_conv_b, m_conv_w_out, m_ssm_lam_re, m_ssm_lam_im, m_ssm_log_step, m_ssm_b_re, m_ssm_b_im, m_ssm_c_re, m_ssm_c_im, m_ssm_d, m_ssm_w_glu, m_mix_w_out, m_ln2_g, m_ln2_b, m_ffn2_w_in, m_ffn2_w_out, m_ln3_g, m_ln3_b, m_ple_w_in, m_ple_w_gate, m_ln4_g, m_ln4_b, v_ffn1_w_in, v_ffn1_w_out, v_ln1_g, v_ln1_b, v_mix_w_in, v_conv_w, v_conv_b, v_conv_w_out, v_ssm_lam_re, v_ssm_lam_im, v_ssm_log_step, v_ssm_b_re, v_ssm_b_im, v_ssm_c_re, v_ssm_c_im, v_ssm_d, v_ssm_w_glu, v_mix_w_out, v_ln2_g, v_ln2_b, v_ffn2_w_in, v_ffn2_w_out, v_ln3_g, v_ln3_b, v_ple_w_in, v_ple_w_gate, v_ln4_g, v_ln4_b):
    args = dict(locals())
    w = {n: args[n] for n in WEIGHTS}
    m = {n: args["m_" + n] for n in WEIGHTS}
    v = {n: args["v_" + n] for n in WEIGHTS}
    _, _, c, me = _where()
    cidx = jnp.stack([c, me]).astype(jnp.int32)
    meidx = jnp.reshape(me, (1,)).astype(jnp.int32)

    sched = _Sched(cidx)
    token = sched.first_begin({n: _slot_cast(meidx, w[n][0], BF16, "cast_" + n) for n in GATHER_FIRST})
    rest = {n: _slot_cast(meidx, w[n][0], BF16, "cast_" + n, (token,)) for n in BIG if n not in GATHER_FIRST}
    rest["conv_w"] = _slot_cast(meidx, jnp.pad(conv_w[0], ((0, 13), (0, 0))), F32, "cast_conv_w", (token,))
    sp = {n: (w[n] if w[n].ndim == 2 and n != "ssm_log_step" else w[n][0]) for n in SMALL if n != "conv_w"}
    ops = _s5_operands({**sp, "ssm_lam_re": sp["ssm_lam_re"] + token[0, 0]})
    sched.first_end(rest, list(rest.values()) + list(ops))
    loss_part, dx0 = _local_step(x[0], p[0, 0], loss_target[0], sp, ops, sched)
    out_g, out_d, out_m, out_v = {}, {}, {}, {}

    def big_adamw(names, token):
        for n in names:
            g, dl, mn, vn = _adamw_pair(cidx, w[n][0], sched.half[n], sched.theirs[n], m[n][0], v[n][0], token,
                                        "adamw_" + n)
            out_g[n], out_d[n], out_m[n], out_v[n] = g[None], dl[None], mn[None], vn[None]

    first = REDUCE_GROUP["ple"] + ["ffn2_w_in"]
    big_adamw(first, sched.tail_begin())
    token = sched.tail_mid([out_v[n] for n in first])
    big_adamw(["ffn2_w_out"], token)

    raw_shapes = [sched.raw[k].shape for k in RAW_ORDER]
    raw_sum = dict(zip(RAW_ORDER, _unpack(_sum8(sched.small_buf, token), raw_shapes)))
    loss = raw_sum["loss"][0]
    sg = _small_grads(raw_sum, sp)
    sg["conv_w"] = lax.dynamic_slice_in_dim(sg["conv_w"], me * 128, 128, axis=1)
    small_shapes = [w[n].shape for n in SMALL]
    gp = _pack([sg[n] for n in SMALL])
    d_s, m_s, v_s = _adamw(_pack([[d[n] for n in SMALL] for d in (w, m, v)]), gp, "adamw_small")

    for n, a, b_, c_, d_ in zip(SMALL, _unpack(gp, small_shapes), _unpack(d_s, small_shapes),
                                _unpack(m_s, small_shapes), _unpack(v_s, small_shapes)):
        out_g[n], out_d[n], out_m[n], out_v[n] = a, b_, c_, d_
    big_adamw(REDUCE_GROUP["mix"], token)
    sched.tail_end([d_s, out_v["ffn2_w_out"]] + [out_v[n] for n in REDUCE_GROUP["mix"]])
    big_adamw(REDUCE_GROUP[LAST_GROUP], token)

    return (loss, dx0[None], *[out_g[n] for n in WEIGHTS], *[out_d[n] for n in WEIGHTS],
            *[out_m[n] for n in WEIGHTS], *[out_v[n] for n in WEIGHTS])
```

```python
import functools
import math

import jax
import jax.numpy as jnp
import numpy as np
from jax import lax
from jax.experimental import pallas as pl
from jax.experimental.pallas import tpu as pltpu

F32, BF16 = jnp.float32, jnp.bfloat16
D = 1024
FF = 2816
FFH = FF // 2
CONV = 512
SSM = 512
GROUPS = 32
STATE = 64
LANES = GROUPS * STATE
SCAN_W = 128
SCAN_PER = 512 // SCAN_W
SCAN_GR = SCAN_W // STATE
SCAN_R = 256
TOKEN_TILE = 256
ALPHA = 2.0 ** 0.25
LN_EPS = 1e-5
GELU_C = math.sqrt(2.0 / math.pi)
B1, B2, LR, EPS, WD, STEP = 0.9, 0.999, 0.001, 1e-8, 0.01, 10
MESH = pl.DeviceIdType.MESH
ANY = pl.BlockSpec(memory_space=pl.ANY)
VMEM_FULL = pl.BlockSpec(memory_space=pltpu.VMEM)


def _cp(vmem_mb=48, n_axes=1):
    return pltpu.CompilerParams(vmem_limit_bytes=vmem_mb << 20,
                                dimension_semantics=("arbitrary",) * n_axes)


def _hbm(*arrs):
    return [pltpu.with_memory_space_constraint(a, pltpu.HBM) for a in arrs]


def _hbm_out(shapes):
    if isinstance(shapes, (list, tuple)):
        return [pltpu.HBM(s.shape, s.dtype) for s in shapes]
    return pltpu.HBM(shapes.shape, shapes.dtype)


def _nn(a, b):
    return jnp.dot(a, b, preferred_element_type=F32)


def _nt(a, b):
    return lax.dot_general(a, b, (((1,), (1,)), ((), ())), preferred_element_type=F32)


def _tn(a, b):
    return lax.dot_general(a, b, (((0,), (0,)), ((), ())), preferred_element_type=F32)


def _sig(v):
    return jax.nn.sigmoid(v)


def _ln_stats(r):
    mu = jnp.mean(r, axis=-1, keepdims=True)
    xc = r - mu
    var = jnp.mean(xc * xc, axis=-1, keepdims=True)
    rstd = lax.rsqrt(var + LN_EPS)
    return xc * rstd, rstd


def _ln_bwd(dy, r, g):
    xhat, rstd = _ln_stats(r)
    dyg = dy * g
    m1 = jnp.mean(dyg, axis=-1, keepdims=True)
    m2 = jnp.mean(dyg * xhat, axis=-1, keepdims=True)
    return rstd * (dyg - m1 - xhat * m2), xhat


def _rowsum(v):
    return jnp.sum(v, axis=0, keepdims=True)


class _Payload:
    def __init__(self, operands, outs, aliases, sems, start, finish):
        self.operands, self.outs, self.aliases, self.sems = list(operands), list(outs), dict(aliases), list(sems)
        self.start, self.finish = start, finish


def _split(flat, comm, attr):
    out, i = [], 0
    for p in comm:
        n = len(getattr(p, attr))
        out.append(list(flat[i:i + n]))
        i += n
    return out


def _run_comm(comm, which, cin, cout, csem):
    for p, a, b, s in zip(comm, _split(cin, comm, "operands"), _split(cout, comm, "outs"), _split(csem, comm, "sems")):
        getattr(p, which)(a, b, s)


def _pcall(body, *, name, grid, in_specs, out_specs, out_shape, operands, scratch=(), vmem_mb=48, aliases=None,
           comm=()):
    ni, no, ns = len(in_specs), len(out_specs), len(scratch)
    c_ops = [a for p in comm for a in p.operands]
    c_outs = [s for p in comm for s in p.outs]
    c_sems = [s for p in comm for s in p.sems]
    io = dict(aliases or {})
    off_i, off_o = ni, no
    for p in comm:
        for a, b in p.aliases.items():
            io[off_i + a] = off_o + b
        off_i += len(p.operands)
        off_o += len(p.outs)

    def wrapped(*refs):
        ins, cin = refs[:ni], refs[ni:ni + len(c_ops)]
        o0 = ni + len(c_ops)
        outs, cout = refs[o0:o0 + no], refs[o0 + no:o0 + no + len(c_outs)]
        s0 = o0 + no + len(c_outs)
        scr, csem = refs[s0:s0 + ns], refs[s0 + ns:]
        if comm:
            first = functools.reduce(jnp.logical_and, [pl.program_id(a) == 0 for a in range(len(grid))])
            pl.when(first)(lambda: _run_comm(comm, "start", cin, cout, csem))
        body(*ins, *outs, *scr)
        if comm:
            last = functools.reduce(jnp.logical_and, [pl.program_id(a) == grid[a] - 1 for a in range(len(grid))])
            pl.when(last)(lambda: _run_comm(comm, "finish", cin, cout, csem))

    res = pl.pallas_call(
        wrapped, name=name, grid=grid,
        in_specs=list(in_specs) + [ANY] * len(c_ops), out_specs=list(out_specs) + [ANY] * len(c_outs),
        out_shape=_hbm_out(list(out_shape) + c_outs), scratch_shapes=list(scratch) + c_sems,
        input_output_aliases=io,
        compiler_params=pltpu.CompilerParams(vmem_limit_bytes=vmem_mb << 20,
                                             dimension_semantics=("arbitrary",) * len(grid),
                                             has_side_effects=bool(c_sems)),
    )(*_hbm(*operands, *c_ops))
    return list(res[:no]), _split(res[no:], comm, "outs")


def _comm_call(name, comm):
    c_ops = [a for p in comm for a in p.operands]
    c_outs = [s for p in comm for s in p.outs]
    c_sems = [s for p in comm for s in p.sems]
    io, off_i, off_o = {}, 0, 0
    for p in comm:
        for a, b in p.aliases.items():
            io[off_i + a] = off_o + b
        off_i += len(p.operands)
        off_o += len(p.outs)

    def body(*refs):
        cin, cout = refs[:len(c_ops)], refs[len(c_ops):len(c_ops) + len(c_outs)]
        csem = refs[len(c_ops) + len(c_outs):]
        _run_comm(comm, "start", cin, cout, csem)
        _run_comm(comm, "finish", cin, cout, csem)

    res = pl.pallas_call(
        body, name=name, in_specs=[ANY] * len(c_ops), out_specs=[ANY] * len(c_outs), out_shape=_hbm_out(c_outs),
        scratch_shapes=c_sems, input_output_aliases=io,
        compiler_params=pltpu.CompilerParams(has_side_effects=True),
    )(*_hbm(*c_ops))
    return _split(res, comm, "outs")


def _ffn_fwd(x, w_in4, w_out2, g, b, tm, name, comm=()):
    T = x.shape[0]

    def body(x_ref, win_ref, wo_ref, g_ref, b_ref, h_ref, r_ref, xo_ref, xob_ref, xib_ref):
        xf = x_ref[...]
        xv = xf.astype(BF16)
        xib_ref[...] = xv
        acc = ALPHA * xf
        for k in range(2):
            gt = _nn(xv, win_ref[k])
            up = _nn(xv, win_ref[k + 2])
            a = (gt * _sig(gt) * up).astype(BF16)
            h_ref[:, 2 * k * FFH:(2 * k + 1) * FFH] = gt.astype(BF16)
            h_ref[:, (2 * k + 1) * FFH:(2 * k + 2) * FFH] = up.astype(BF16)
            acc = acc + 0.5 * _nn(a, wo_ref[k])
        xhat, _ = _ln_stats(acc)
        xo = xhat * g_ref[...] + b_ref[...]
        r_ref[...] = acc
        xo_ref[...] = xo
        xob_ref[...] = xo.astype(BF16)

    tok = pl.BlockSpec((tm, D), lambda i: (i, 0))
    vec = pl.BlockSpec((1, D), lambda i: (0, 0))
    return _pcall(
        body, name=name, grid=(T // tm,),
        in_specs=[tok,
                  pl.BlockSpec((4, D, FFH), lambda i: (0, 0, 0), pipeline_mode=pl.Buffered(1)),
                  pl.BlockSpec((2, FFH, D), lambda i: (0, 0, 0), pipeline_mode=pl.Buffered(1)),
                  vec, vec],
        out_specs=[pl.BlockSpec((tm, 2 * FF), lambda i: (i, 0)), tok, tok, tok, tok],
        out_shape=[jax.ShapeDtypeStruct((T, 2 * FF), BF16), jax.ShapeDtypeStruct((T, D), F32),
                   jax.ShapeDtypeStruct((T, D), F32), jax.ShapeDtypeStruct((T, D), BF16),
                   jax.ShapeDtypeStruct((T, D), BF16)],
        vmem_mb=58, comm=comm, operands=(x, w_in4, w_out2, g, b))


def _ffn_bwd(dy, r, g, h, w_in4, w_out2, tm, name, comm=()):
    T = dy.shape[0]

    def body(dy_ref, r_ref, g_ref, h_ref, win_ref, wo_ref, dx_ref, dh_ref, a_ref, df_ref, dg_ref, db_ref):
        i = pl.program_id(0)
        dyv = dy_ref[...]
        dr, xhat = _ln_bwd(dyv, r_ref[...], g_ref[...])
        dg_ref[...] = jnp.where(i == 0, 0.0, dg_ref[...]) + _rowsum(dyv * xhat)
        db_ref[...] = jnp.where(i == 0, 0.0, db_ref[...]) + _rowsum(dyv)
        dfb = (0.5 * dr).astype(BF16)
        df_ref[...] = dfb
        acc = ALPHA * dr
        for k in range(2):
            da = _nt(dfb, wo_ref[k])
            gt = h_ref[:, 2 * k * FFH:(2 * k + 1) * FFH].astype(F32)
            up = h_ref[:, (2 * k + 1) * FFH:(2 * k + 2) * FFH].astype(F32)
            sg = _sig(gt)
            silu = gt * sg
            dgate = (da * up * (sg * (1.0 + gt * (1.0 - sg)))).astype(BF16)
            dup = (da * silu).astype(BF16)
            a_ref[:, k * FFH:(k + 1) * FFH] = (silu * up).astype(BF16)
            dh_ref[:, 2 * k * FFH:(2 * k + 1) * FFH] = dgate
            dh_ref[:, (2 * k + 1) * FFH:(2 * k + 2) * FFH] = dup
            acc = acc + _nt(dgate, win_ref[k]) + _nt(dup, win_ref[k + 2])
        dx_ref[...] = acc

    tok = pl.BlockSpec((tm, D), lambda i: (i, 0))
    vec = pl.BlockSpec((1, D), lambda i: (0, 0))
    wide = pl.BlockSpec((tm, 2 * FF), lambda i: (i, 0))
    return _pcall(
        body, name=name, grid=(T // tm,),
        in_specs=[tok, tok, vec, wide,
                  pl.BlockSpec((4, D, FFH), lambda i: (0, 0, 0), pipeline_mode=pl.Buffered(1)),
                  pl.BlockSpec((2, FFH, D), lambda i: (0, 0, 0), pipeline_mode=pl.Buffered(1))],
        out_specs=[tok, wide, pl.BlockSpec((tm, FF), lambda i: (i, 0)), tok, vec, vec],
        out_shape=[jax.ShapeDtypeStruct((T, D), F32), jax.ShapeDtypeStruct((T, 2 * FF), BF16),
                   jax.ShapeDtypeStruct((T, FF), BF16), jax.ShapeDtypeStruct((T, D), BF16),
                   jax.ShapeDtypeStruct((1, D), F32), jax.ShapeDtypeStruct((1, D), F32)],
        vmem_mb=58, comm=comm, operands=(dy, r, g, h, w_in4, w_out2))


def _mm_tn(a, b, tk, tn, name, shard_cols=None, interleaved=False, comm=()):
    T, K = a.shape
    N = b.shape[1]

    def body(a_ref, b_ref, o_ref):
        o_ref[...] = _tn(a_ref[...], b_ref[...])

    if shard_cols is None:
        out_shape = jax.ShapeDtypeStruct((K, N), F32)
        out_spec = pl.BlockSpec((tk, tn), lambda ki, nj: (ki, nj))
    else:
        per = shard_cols // tn

        def shard(nj):
            blk = nj // per
            return (blk % 2) * 2 + blk // 2 if interleaved else blk

        out_shape = jax.ShapeDtypeStruct((N // shard_cols, K, shard_cols), F32)
        out_spec = pl.BlockSpec((None, tk, tn), lambda ki, nj: (shard(nj), ki, nj % per))
    (out,), got = _pcall(
        body, name=name, grid=(K // tk, N // tn),
        in_specs=[pl.BlockSpec((T, tk), lambda ki, nj: (0, ki)), pl.BlockSpec((T, tn), lambda ki, nj: (0, nj))],
        out_specs=[out_spec], out_shape=[out_shape], comm=comm, operands=(a, b))
    return out, got


def _mix_fwd_a(xb, w_mix4, conv_w, conv_b, w_co4, tm, comm=()):
    T = xb.shape[0]

    def body(xb_ref, w_ref, cw_ref, cb_ref, wco_ref,
             pc_ref, z_ref, yin_ref, su_ref, sub_ref, gc_ref, gs_ref, yc_ref, qbuf):
        @pl.when(pl.program_id(0) == 0)
        def _():
            qbuf[pl.ds(0, 8), :] = jnp.zeros((8, CONV), F32)

        xv = xb_ref[...]
        p0 = _nn(xv, w_ref[0])
        p1 = _nn(xv, w_ref[1])
        gc_ref[...] = _nn(xv, w_ref[2]).astype(BF16)
        gs_ref[...] = _nn(xv, w_ref[3]).astype(BF16)
        cbv, ccv = p0[:, :CONV], p0[:, CONV:]
        chv, suv = p1[:, :CONV], p1[:, CONV:]
        q = ccv * chv
        qbuf[pl.ds(8, tm), :] = q
        cw = cw_ref[...]
        z = (cw[2:3] * q + cw[1:2] * qbuf[pl.ds(7, tm), :] + cw[0:1] * qbuf[pl.ds(6, tm), :]
             + cb_ref[...])
        qbuf[pl.ds(0, 8), :] = q[tm - 8:tm]
        yin = (cbv * z).astype(BF16)
        pc_ref[:, 0:CONV] = cbv.astype(BF16)
        pc_ref[:, CONV:2 * CONV] = ccv.astype(BF16)
        pc_ref[:, 2 * CONV:3 * CONV] = chv.astype(BF16)
        z_ref[...] = z.astype(BF16)
        yin_ref[...] = yin
        su_ref[...] = suv
        sub_ref[...] = suv.astype(BF16)
        for k in range(4):
            yc_ref[:, 256 * k:256 * (k + 1)] = _nn(yin, wco_ref[k]).astype(BF16)

    def tok(n):
        return pl.BlockSpec((tm, n), lambda i: (i, 0))

    def full(shape):
        return pl.BlockSpec(shape, lambda i: (0,) * len(shape))

    return _pcall(
        body, name="mix_fwd_a", grid=(T // tm,),
        in_specs=[tok(D), full((4, D, D)), full((3, CONV)), full((1, CONV)), full((4, CONV, 256))],
        out_specs=[tok(3 * CONV), tok(CONV), tok(CONV), tok(SSM), tok(SSM), tok(D), tok(D), tok(D)],
        out_shape=[jax.ShapeDtypeStruct((T, 3 * CONV), BF16), jax.ShapeDtypeStruct((T, CONV), BF16),
                   jax.ShapeDtypeStruct((T, CONV), BF16), jax.ShapeDtypeStruct((T, SSM), F32),
                   jax.ShapeDtypeStruct((T, SSM), BF16), jax.ShapeDtypeStruct((T, D), BF16),
                   jax.ShapeDtypeStruct((T, D), BF16), jax.ShapeDtypeStruct((T, D), BF16)],
        scratch=[pltpu.VMEM((tm + 8, CONV), F32)], vmem_mb=56, comm=comm,
        operands=(xb, w_mix4, conv_w, conv_b, w_co4))


def _scan_rows(bre, bim, ar, ai, T, rev, load):
    R, W, G = SCAN_R, bre.shape[1], T // 8
    if rev:
        ai = -ai

    def cmul(pr, pi, xr, xi):
        return pr * xr - pi * xi, pr * xi + pi * xr

    pw = [(ar, ai)]
    for _ in range(7):
        pw.append(cmul(ar, ai, *pw[-1]))

    def shifted(v, d, axis, n, idx):
        if rev:
            return jnp.where(idx < n - d, pltpu.roll(v, n - d, axis), 0.0)
        return jnp.where(idx >= d, pltpu.roll(v, d, axis), 0.0)

    sub8 = lax.broadcasted_iota(jnp.int32, (8, W), 0)
    inside = {d: (sub8 < 8 - d) if rev else (sub8 >= d) for d in (1, 2, 4)}
    pm = {d: (jnp.where(inside[d], pw[d - 1][0], 0.0)[None], jnp.where(inside[d], pw[d - 1][1], 0.0)[None])
          for d in (1, 2, 4)}

    def step(i, _):
        t0 = pl.multiple_of(i * R, R)
        vr, vi = load(t0)
        vr, vi = vr.reshape(R // 8, 8, W), vi.reshape(R // 8, 8, W)
        for d in (1, 2, 4):
            sh = (8 - d) if rev else d
            dr, di = cmul(pm[d][0], pm[d][1], pltpu.roll(vr, sh, 1), pltpu.roll(vi, sh, 1))
            vr, vi = vr + dr, vi + di
        bre[pl.ds(t0 + 8, R), :] = vr.reshape(R, W)
        bim[pl.ds(t0 + 8, R), :] = vi.reshape(R, W)
        return 0

    lax.fori_loop(0, T // R, step, 0)

    edge = 0 if rev else 7
    cr = bre[pl.ds(8 + edge, G, stride=8), :]
    ci = bim[pl.ds(8 + edge, G, stride=8), :]
    row = lax.broadcasted_iota(jnp.int32, (G, W), 0)
    qr, qi = pw[7]
    d = 1
    while d < G:
        dr, di = cmul(qr, qi, shifted(cr, d, 0, G, row), shifted(ci, d, 0, G, row))
        cr, ci = cr + dr, ci + di
        qr, qi = qr * qr - qi * qi, 2.0 * qr * qi
        d *= 2

    nr, ni = shifted(cr, 1, 0, G, row), shifted(ci, 1, 0, G, row)
    for r in range(8):
        pr, pi = pw[7 - r] if rev else pw[r]
        dr, di = cmul(pr, pi, nr, ni)
        bre[pl.ds(8 + r, G, stride=8), :] = bre[pl.ds(8 + r, G, stride=8), :] + dr
        bim[pl.ds(8 + r, G, stride=8), :] = bim[pl.ds(8 + r, G, stride=8), :] + di


def _scan_specs(T):
    W = SCAN_W
    lane = pl.BlockSpec((T, W), lambda j: (0, j))
    col = pl.BlockSpec((T, 128), lambda j: (0, j // SCAN_PER))
    wb = pl.BlockSpec((None, 128, W), lambda j: (j, 0, 0))
    wc = pl.BlockSpec((None, W, 128), lambda j: (j, 0, 0))
    vec = pl.BlockSpec((1, W), lambda j: (0, j))
    return lane, col, wb, wc, vec


def _s5_scan_fwd(su_b, wb_re, wb_im, a_re, a_im, comm=()):
    T = su_b.shape[0]
    W = SCAN_W

    def body(su_ref, wbr_ref, wbi_ref, ar_ref, ai_ref, sr_ref, si_ref, bre, bim):
        su = su_ref[...]
        bre[pl.ds(8, T), :] = _nn(su, wbr_ref[...])
        bim[pl.ds(8, T), :] = _nn(su, wbi_ref[...])
        _scan_rows(bre, bim, ar_ref[...], ai_ref[...], T, False,
                   lambda t0: (bre[pl.ds(t0 + 8, SCAN_R), :], bim[pl.ds(t0 + 8, SCAN_R), :]))
        sr_ref[...] = bre[pl.ds(8, T), :].astype(BF16)
        si_ref[...] = bim[pl.ds(8, T), :].astype(BF16)

    lane, col, wb, wc, vec = _scan_specs(T)
    return _pcall(
        body, name="s5_scan_fwd", grid=(LANES // W,),
        in_specs=[col, wb, wb, vec, vec],
        out_specs=[lane, lane],
        out_shape=[jax.ShapeDtypeStruct((T, LANES), BF16)] * 2,
        scratch=[pltpu.VMEM((T + 16, W), F32)] * 2, comm=comm,
        operands=(su_b, wb_re, wb_im, a_re, a_im))


def _gelu(s):
    th = jnp.tanh(GELU_C * (s + 0.044715 * s * s * s))
    return 0.5 * s * (1.0 + th), th


def _mix_fwd_b(st_re, st_im, wc_re4, wc_im4, su, dvec, w_glu4, g_conv, g_ssm, y_conv, w_mo, x1, g, b, tm, comm=()):
    T = su.shape[0]

    def body(sr_ref, si_ref, wcr_ref, wci_ref, su_ref, d_ref, wg_ref, gc_ref, gs_ref, yc_ref, wmo_ref,
             x_ref, g_ref, b_ref, s_ref, sgb_ref, ga_ref, gb_ref, mb_ref, r_ref, xo_ref):
        srb = sr_ref[...]
        sib = si_ref[...]
        ys = [_nn(srb[:, 512 * J:512 * (J + 1)], wcr_ref[J]) + _nn(sib[:, 512 * J:512 * (J + 1)], wci_ref[J])
              for J in range(4)]
        s = jnp.concatenate(ys, axis=1) + d_ref[...] * su_ref[...]
        sg, _ = _gelu(s)
        sgb = sg.astype(BF16)
        ga = jnp.concatenate([_nn(sgb, wg_ref[0]), _nn(sgb, wg_ref[1])], axis=1)
        gb = jnp.concatenate([_nn(sgb, wg_ref[2]), _nn(sgb, wg_ref[3])], axis=1)
        merged = (_sig(gc_ref[...].astype(F32)) * yc_ref[...].astype(F32)
                  + _sig(gs_ref[...].astype(F32)) * (ga * _sig(gb)))
        mb = merged.astype(BF16)
        r = ALPHA * x_ref[...] + _nn(mb, wmo_ref[...])
        xhat, _ = _ln_stats(r)
        xo = xhat * g_ref[...] + b_ref[...]
        s_ref[...] = s
        sgb_ref[...] = sgb
        ga_ref[...] = ga.astype(BF16)
        gb_ref[...] = gb.astype(BF16)
        mb_ref[...] = mb
        r_ref[...] = r
        xo_ref[...] = xo

    def tok(n):
        return pl.BlockSpec((tm, n), lambda i: (i, 0))

    def full(shape):
        return pl.BlockSpec(shape, lambda i: (0,) * len(shape))

    return _pcall(
        body, name="mix_fwd_b", grid=(T // tm,),
        in_specs=[tok(LANES), tok(LANES), full((4, 512, 128)), full((4, 512, 128)), tok(SSM), full((1, SSM)),
                  full((4, SSM, 512)), tok(D), tok(D), tok(D), full((D, D)), tok(D), full((1, D)), full((1, D))],
        out_specs=[tok(SSM), tok(SSM), tok(D), tok(D), tok(D), tok(D), tok(D)],
        out_shape=[jax.ShapeDtypeStruct((T, SSM), F32), jax.ShapeDtypeStruct((T, SSM), BF16),
                   jax.ShapeDtypeStruct((T, D), BF16), jax.ShapeDtypeStruct((T, D), BF16),
                   jax.ShapeDtypeStruct((T, D), BF16), jax.ShapeDtypeStruct((T, D), F32),
                   jax.ShapeDtypeStruct((T, D), F32)],
        vmem_mb=56, comm=comm,
        operands=(st_re, st_im, wc_re4, wc_im4, su, dvec, w_glu4, g_conv, g_ssm, y_conv, w_mo, x1, g, b))


def _ple_loss(x3, x3b, p, w_pi4, w_pg, g, b, target, tm):
    T = x3.shape[0]
    PD = p.shape[1]

    def body(x_ref, xb_ref, p_ref, wpi_ref, wpg_ref, g_ref, b_ref, t_ref,
             loss_ref, dx_ref, pb_ref, dpw_ref, dgt_ref, dg_ref, db_ref):
        i = pl.program_id(0)
        pb = p_ref[...].astype(BF16)
        pw = jnp.concatenate([_nn(pb, wpi_ref[k]) for k in range(4)], axis=1)
        gt = _nn(xb_ref[...], wpg_ref[...])
        sg = _sig(gt)
        r = ALPHA * x_ref[...] + pw * sg
        gv = g_ref[...]
        xhat, rstd = _ln_stats(r)
        err = xhat * gv + b_ref[...] - t_ref[...]
        lpart = jnp.zeros((1, 128), F32) + 0.5 * jnp.sum(jnp.mean(err * err, axis=-1, keepdims=True))
        dy = err * (1.0 / D)
        dyg = dy * gv
        m1 = jnp.mean(dyg, axis=-1, keepdims=True)
        m2 = jnp.mean(dyg * xhat, axis=-1, keepdims=True)
        dr = rstd * (dyg - m1 - xhat * m2)
        pg, pbias = _rowsum(dy * xhat), _rowsum(dy)

        @pl.when(i == 0)
        def _():
            loss_ref[...] = lpart
            dg_ref[...] = pg
            db_ref[...] = pbias

        @pl.when(i > 0)
        def _():
            loss_ref[...] += lpart
            dg_ref[...] += pg
            db_ref[...] += pbias

        dgt = (dr * pw * sg * (1.0 - sg)).astype(BF16)
        pb_ref[...] = pb
        dpw_ref[...] = (dr * sg).astype(BF16)
        dgt_ref[...] = dgt
        dx_ref[...] = ALPHA * dr + _nt(dgt, wpg_ref[...])

    def tok(n):
        return pl.BlockSpec((tm, n), lambda i: (i, 0))

    def full(shape):
        return pl.BlockSpec(shape, lambda i: (0,) * len(shape))

    return pl.pallas_call(
        body, name="ple_loss", grid=(T // tm,),
        in_specs=[tok(D), tok(D), tok(PD), full((4, PD, 256)), full((D, D)), full((1, D)), full((1, D)), tok(D)],
        out_specs=[full((1, 128)), tok(D), tok(PD), tok(D), tok(D), full((1, D)), full((1, D))],
        out_shape=_hbm_out([jax.ShapeDtypeStruct((1, 128), F32), jax.ShapeDtypeStruct((T, D), F32),
                            jax.ShapeDtypeStruct((T, PD), BF16), jax.ShapeDtypeStruct((T, D), BF16),
                            jax.ShapeDtypeStruct((T, D), BF16), jax.ShapeDtypeStruct((1, D), F32),
                            jax.ShapeDtypeStruct((1, D), F32)]),
        compiler_params=_cp(48, 1),
    )(*_hbm(x3, x3b, p, w_pi4, w_pg, g, b, target))


def _mix_bwd_b(dy, r2, g, w_mo, g_conv, g_ssm, y_conv, ga, gb, s, su, dvec, w_glu4, wc_re4, wc_im4, tm, comm=()):
    T = dy.shape[0]

    def body(dy_ref, r_ref, g_ref, wmo_ref, gc_ref, gs_ref, yc_ref, ga_ref, gb_ref, s_ref, su_ref, d_ref,
             wg_ref, wcr_ref, wci_ref,
             dres_ref, dmix_ref, dgl_ref, dsb_ref, dud_ref, gsr_ref, gsi_ref, dyc_ref, dp_ref,
             dg_ref, db_ref, dd_ref):
        i = pl.program_id(0)
        dyv = dy_ref[...]
        dr, xhat = _ln_bwd(dyv, r_ref[...], g_ref[...])
        dmix = dr.astype(BF16)
        dmerged = _nt(dmix, wmo_ref[...])
        sc, ss, sgb = (_sig(gc_ref[...].astype(F32)), _sig(gs_ref[...].astype(F32)),
                       _sig(gb_ref[...].astype(F32)))
        gav = ga_ref[...].astype(F32)
        yssm = gav * sgb
        dgc = dmerged * yc_ref[...].astype(F32) * sc * (1.0 - sc)
        dgss = dmerged * yssm * ss * (1.0 - ss)
        dyssm = dmerged * ss
        dgl = jnp.concatenate([dyssm * sgb, dyssm * gav * sgb * (1.0 - sgb)], axis=1).astype(BF16)
        dsg = (_nt(dgl[:, 0:512], wg_ref[0]) + _nt(dgl[:, 512:1024], wg_ref[1])
               + _nt(dgl[:, 1024:1536], wg_ref[2]) + _nt(dgl[:, 1536:2048], wg_ref[3]))
        sv = s_ref[...]
        _, th = _gelu(sv)
        dgelu = 0.5 * (1.0 + th) + 0.5 * sv * (1.0 - th * th) * GELU_C * (1.0 + 3.0 * 0.044715 * sv * sv)
        ds = dsg * dgelu
        dsb = ds.astype(BF16)
        pg, pb, pd = _rowsum(dyv * xhat), _rowsum(dyv), _rowsum(ds * su_ref[...])

        @pl.when(i == 0)
        def _():
            dg_ref[...] = pg
            db_ref[...] = pb
            dd_ref[...] = pd

        @pl.when(i > 0)
        def _():
            dg_ref[...] += pg
            db_ref[...] += pb
            dd_ref[...] += pd

        dres_ref[...] = ALPHA * dr
        dmix_ref[...] = dmix
        dgl_ref[...] = dgl
        dsb_ref[...] = dsb
        dud_ref[...] = ds * d_ref[...]
        for J in range(4):
            gsr_ref[:, 512 * J:512 * (J + 1)] = _nt(dsb[:, 128 * J:128 * (J + 1)], wcr_ref[J]).astype(BF16)
            gsi_ref[:, 512 * J:512 * (J + 1)] = _nt(dsb[:, 128 * J:128 * (J + 1)], wci_ref[J]).astype(BF16)
        dyc_ref[...] = (dmerged * sc).astype(BF16)
        dp_ref[:, 0:D] = dgc.astype(BF16)
        dp_ref[:, D:2 * D] = dgss.astype(BF16)

    def tok(n):
        return pl.BlockSpec((tm, n), lambda i: (i, 0))

    def full(shape):
        return pl.BlockSpec(shape, lambda i: (0,) * len(shape))

    return _pcall(
        body, name="mix_bwd_b", grid=(T // tm,),
        in_specs=[tok(D), tok(D), full((1, D)), full((D, D)), tok(D), tok(D), tok(D), tok(D), tok(D),
                  tok(SSM), tok(SSM), full((1, SSM)), full((4, SSM, 512)), full((4, 512, 128)), full((4, 512, 128))],
        out_specs=[tok(D), tok(D), tok(2 * D), tok(SSM), tok(SSM), tok(LANES), tok(LANES), tok(D),
                   pl.BlockSpec((tm, 2 * D), lambda i: (i, 1)), full((1, D)), full((1, D)), full((1, SSM))],
        out_shape=[jax.ShapeDtypeStruct((T, D), F32), jax.ShapeDtypeStruct((T, D), BF16),
                   jax.ShapeDtypeStruct((T, 2 * D), BF16), jax.ShapeDtypeStruct((T, SSM), BF16),
                   jax.ShapeDtypeStruct((T, SSM), F32), jax.ShapeDtypeStruct((T, LANES), BF16),
                   jax.ShapeDtypeStruct((T, LANES), BF16), jax.ShapeDtypeStruct((T, D), BF16),
                   jax.ShapeDtypeStruct((T, 4 * D), BF16), jax.ShapeDtypeStruct((1, D), F32),
                   jax.ShapeDtypeStruct((1, D), F32), jax.ShapeDtypeStruct((1, SSM), F32)],
        vmem_mb=56, comm=comm,
        operands=(dy, r2, g, w_mo, g_conv, g_ssm, y_conv, ga, gb, s, su, dvec, w_glu4, wc_re4, wc_im4))


def _s5_scan_bwd(gs_re, gs_im, st_re, st_im, su_b, ds_b, wb_re, wb_im, a_re, a_im, comm=()):
    T = su_b.shape[0]
    W = SCAN_W
    R = SCAN_R

    def body(gr_ref, gi_ref, sr_ref, si_ref, su_ref, ds_ref, wbr_ref, wbi_ref, ar_ref, ai_ref,
             dsu_ref, dwbr_ref, dwbi_ref, dwcr_ref, dwci_ref, dar_ref, dai_ref, gre, gim):
        j = pl.program_id(0)
        zero = jnp.zeros((8, W), F32)
        for buf in (gre, gim):
            buf[pl.ds(T + 8, 8), :] = zero
        _scan_rows(gre, gim, ar_ref[...], ai_ref[...], T, True,
                   lambda t0: (gr_ref[pl.ds(t0, R), :].astype(F32), gi_ref[pl.ds(t0, R), :].astype(F32)))
        grb = gre[pl.ds(8, T), :].astype(BF16)
        gib = gim[pl.ds(8, T), :].astype(BF16)
        part = _nt(grb, wbr_ref[...]) + _nt(gib, wbi_ref[...])

        @pl.when(j % SCAN_PER == 0)
        def _():
            dsu_ref[...] = part

        @pl.when(j % SCAN_PER > 0)
        def _():
            dsu_ref[...] += part

        su = su_ref[...]
        dwbr_ref[...] = _tn(su, grb)
        dwbi_ref[...] = _tn(su, gib)
        dsv = ds_ref[...]
        dwcr_ref[...] = _tn(sr_ref[...], dsv)
        dwci_ref[...] = _tn(si_ref[...], dsv)
        dar = jnp.zeros((1, W), F32)
        dai = jnp.zeros((1, W), F32)
        for c in range(T // R):
            xr = sr_ref[pl.ds(c * R, R), :].astype(F32)
            xi = si_ref[pl.ds(c * R, R), :].astype(F32)
            g1r = gre[pl.ds(c * R + 9, R), :]
            g1i = gim[pl.ds(c * R + 9, R), :]
            dar = dar + _rowsum(g1r * xr + g1i * xi)
            dai = dai + _rowsum(g1i * xr - g1r * xi)
        dar_ref[...] = dar
        dai_ref[...] = dai

    lane, col, wb, wc, vec = _scan_specs(T)
    return _pcall(
        body, name="s5_scan_bwd", grid=(LANES // W,),
        in_specs=[lane, lane, lane, lane, col, col, wb, wb, vec, vec],
        out_specs=[col, wb, wb, wc, wc, vec, vec],
        out_shape=[jax.ShapeDtypeStruct((T, SSM), F32),
                   jax.ShapeDtypeStruct((LANES // W, 128, W), F32), jax.ShapeDtypeStruct((LANES // W, 128, W), F32),
                   jax.ShapeDtypeStruct((LANES // W, W, 128), F32), jax.ShapeDtypeStruct((LANES // W, W, 128), F32),
                   jax.ShapeDtypeStruct((1, LANES), F32), jax.ShapeDtypeStruct((1, LANES), F32)],
        scratch=[pltpu.VMEM((T + 16, W), F32)] * 2, vmem_mb=56, comm=comm,
        operands=(gs_re, gs_im, st_re, st_im, su_b, ds_b, wb_re, wb_im, a_re, a_im))


def _mix_bwd_a(dyc_b, w_co4, pc, z_b, conv_w, dsu_ssm, du_dir, dproj, dres, w_mix4, tm, comm=()):
    T = dres.shape[0]
    nt = T // tm

    def body(dyc_ref, wco_ref, pc_ref, halo_ref, z_ref, cw_ref, dsu_ref, dud_ref, dpin_ref, dres_ref, w_ref,
             dp_ref, dx_ref, dcw_ref, dcb_ref, dzbuf, qbuf):
        i = pl.program_id(0)
        ii = nt - 1 - i

        @pl.when(i == 0)
        def _():
            dzbuf[pl.ds(tm, 8), :] = jnp.zeros((8, CONV), F32)

        dyc = dyc_ref[...]
        dyin = (_nt(dyc[:, 0:256], wco_ref[0]) + _nt(dyc[:, 256:512], wco_ref[1])
                + _nt(dyc[:, 512:768], wco_ref[2]) + _nt(dyc[:, 768:1024], wco_ref[3]))
        cbv = pc_ref[:, 0:CONV].astype(F32)
        ccv = pc_ref[:, CONV:2 * CONV].astype(F32)
        chv = pc_ref[:, 2 * CONV:3 * CONV].astype(F32)
        dcbv = dyin * z_ref[...].astype(F32)
        dz = dyin * cbv
        dzbuf[pl.ds(0, tm), :] = dz
        cw = cw_ref[...]
        dq = cw[2:3] * dz + cw[1:2] * dzbuf[pl.ds(1, tm), :] + cw[0:1] * dzbuf[pl.ds(2, tm), :]
        dzbuf[pl.ds(tm, 8), :] = dz[0:8]
        q = ccv * chv
        hq = halo_ref[:, CONV:2 * CONV].astype(F32) * halo_ref[:, 2 * CONV:3 * CONV].astype(F32)
        qbuf[pl.ds(0, 8), :] = jnp.where(ii > 0, hq, jnp.zeros_like(hq))
        qbuf[pl.ds(8, tm), :] = q
        pw = jnp.concatenate([_rowsum(dz * qbuf[pl.ds(6, tm), :]), _rowsum(dz * qbuf[pl.ds(7, tm), :]),
                              _rowsum(dz * q), jnp.zeros((5, CONV), F32)], axis=0)
        pbias = _rowsum(dz)

        @pl.when(i == 0)
        def _():
            dcw_ref[...] = pw
            dcb_ref[...] = pbias

        @pl.when(i > 0)
        def _():
            dcw_ref[...] += pw
            dcb_ref[...] += pbias

        dp0 = jnp.concatenate([dcbv, dq * chv], axis=1).astype(BF16)
        dp1 = jnp.concatenate([dq * ccv, dsu_ref[...] + dud_ref[...]], axis=1).astype(BF16)
        dp_ref[:, 0:D] = dp0
        dp_ref[:, D:2 * D] = dp1
        dx_ref[...] = (dres_ref[...] + _nt(dp0, w_ref[0]) + _nt(dp1, w_ref[1])
                       + _nt(dpin_ref[:, 0:D], w_ref[2]) + _nt(dpin_ref[:, D:2 * D], w_ref[3]))

    def tok(n):
        return pl.BlockSpec((tm, n), lambda i: (nt - 1 - i, 0))

    def full(shape):
        return pl.BlockSpec(shape, lambda i: (0,) * len(shape))

    halo = pl.BlockSpec((8, 3 * CONV), lambda i: (jnp.maximum((nt - 1 - i) * (tm // 8) - 1, 0), 0))
    return _pcall(
        body, name="mix_bwd_a", grid=(nt,),
        in_specs=[tok(D), full((4, CONV, 256)), tok(3 * CONV), halo, tok(CONV), full((3, CONV)),
                  tok(SSM), tok(SSM), pl.BlockSpec((tm, 2 * D), lambda i: (nt - 1 - i, 1)), tok(D),
                  full((4, D, D))],
        out_specs=[pl.BlockSpec((tm, 2 * D), lambda i: (nt - 1 - i, 0)), tok(D), full((8, CONV)), full((1, CONV))],
        out_shape=[jax.ShapeDtypeStruct((T, 4 * D), BF16), jax.ShapeDtypeStruct((T, D), F32),
                   jax.ShapeDtypeStruct((8, CONV), F32), jax.ShapeDtypeStruct((1, CONV), F32)],
        scratch=[pltpu.VMEM((tm + 8, CONV), F32), pltpu.VMEM((tm + 8, CONV), F32)],
        aliases={8: 0}, vmem_mb=56, comm=comm,
        operands=(dyc_b, w_co4, pc, pc, z_b, conv_w, dsu_ssm, du_dir, dproj, dres, w_mix4))


def _zoh(lam_re, lam_im, log_step, b_re, b_im):
    dt = jnp.exp(log_step)[:, None]
    mag = jnp.exp(lam_re * dt)
    abr, abi = mag * jnp.cos(lam_im * dt), mag * jnp.sin(lam_im * dt)
    nr, ni = abr - 1.0, abi
    den = lam_re * lam_re + lam_im * lam_im
    cr = (nr * lam_re + ni * lam_im) / den
    ci = (ni * lam_re - nr * lam_im) / den
    bbr = cr[..., None] * b_re - ci[..., None] * b_im
    bbi = cr[..., None] * b_im + ci[..., None] * b_re
    return abr, abi, bbr, bbi


_WB_MASK = (np.arange(8)[None, :, None]
            == SCAN_GR * np.arange(SCAN_PER)[:, None, None] + np.arange(SCAN_GR)[None, None, :]).astype(np.float32)
_EYE8 = np.eye(8, dtype=np.float32)


def _wb_blocks(bb):
    bt = bb.transpose(0, 2, 1).reshape(4, 1, 8, 16, 1, STATE)
    full = bt * _WB_MASK[None, :, :, None, :, None]
    return full.reshape(LANES // SCAN_W, 128, SCAN_W).astype(BF16)


def _wc_blocks(cc):
    ct = cc.transpose(0, 2, 1).reshape(4, 8, STATE, 1, 16)
    full = ct * _EYE8[None, :, None, :, None]
    return full.reshape(4, 512, 128).astype(BF16)


def _wb_diag(dwb):
    d6 = dwb.reshape(4, SCAN_PER, 8, 16, SCAN_GR, STATE) * _WB_MASK[None, :, :, None, :, None]
    return d6.sum(axis=(1, 4)).reshape(GROUPS, 16, STATE).transpose(0, 2, 1)


def _wc_diag(dwc):
    mask = _WB_MASK.transpose(0, 2, 1)
    d6 = dwc.reshape(4, SCAN_PER, SCAN_GR, STATE, 8, 16) * mask[None, :, :, None, :, None]
    return d6.sum(axis=4).reshape(GROUPS, STATE, 16).transpose(0, 2, 1)


def _where():
    x, y, c = lax.axis_index("x"), lax.axis_index("y"), lax.axis_index("c")
    return x, y, c, 2 * x + y


def _chip_dev(k, c):
    return (k // 2, k % 2, c)


def _slot_cast(meidx, w, dtype, name, token=()):
    R, C = w.shape
    tr = _row_tile(R)

    def body(m_ref, w_ref, *rest):
        rest[-1][...] = w_ref[...].astype(dtype)

    gs = pltpu.PrefetchScalarGridSpec(
        num_scalar_prefetch=1, grid=(R // tr,),
        in_specs=[pl.BlockSpec((tr, C), lambda i, m: (i, 0))] + [pl.BlockSpec((8, 128), lambda i, m: (0, 0))] * len(token),
        out_specs=pl.BlockSpec((None, tr, C), lambda i, m: (m[0], i, 0)))
    return pl.pallas_call(
        body, name=name, grid_spec=gs, out_shape=_hbm_out(jax.ShapeDtypeStruct((4, R, C), dtype)),
        compiler_params=_cp(32, 1),
    )(meidx, *_hbm(w), *token)


def _gather_ici_payload(bufs):
    def copies(ins, lnd, ss, rs):
        x, y, c, me = _where()
        cps = []
        for w, b in enumerate(bufs):
            h = b.shape[1] // 2
            mine = lnd[w].at[me, pl.ds(c * h, h)]
            for s in range(3):
                k = (me + 1 + s) % 4
                cps.append(pltpu.make_async_remote_copy(
                    src_ref=mine, dst_ref=mine, send_sem=ss.at[3 * w + s], recv_sem=rs.at[3 * w + s],
                    device_id=_chip_dev(k, c), device_id_type=MESH))
        return cps

    p = _sym_payload([], [jax.ShapeDtypeStruct(b.shape, b.dtype) for b in bufs], copies, 3 * len(bufs))
    p.lands = list(bufs)
    return p


def _gather_pass_payload(bufs):
    def copies(ins, outs, ss, rs):
        x, y, c, me = _where()
        cps = []
        for w, b in enumerate(bufs):
            h = b.shape[1] // 2
            for s in range(3):
                j = (me + 1 + s) % 4
                cps.append(pltpu.make_async_remote_copy(
                    src_ref=ins[w].at[j, pl.ds(c * h, h)], dst_ref=outs[w].at[j, pl.ds(c * h, h)],
                    send_sem=ss.at[3 * w + s], recv_sem=rs.at[3 * w + s], device_id=(x, y, 1 - c),
                    device_id_type=MESH))
        return cps

    p = _sym_payload(bufs, [jax.ShapeDtypeStruct(b.shape, b.dtype) for b in bufs], copies, 3 * len(bufs))
    p.aliases = {w: w for w in range(len(bufs))}
    return p


def _gather_payload(bufs):
    n = len(bufs)

    def half(ref, w, k, cc):
        h = bufs[w].shape[1] // 2
        return ref.at[k, pl.ds(cc * h, h)]

    def ici(ins, outs, sems, w, s):
        x, y, c, me = _where()
        k = (me + 1 + s) % 4
        return pltpu.make_async_remote_copy(
            src_ref=half(ins[w], w, me, c), dst_ref=half(outs[w], w, me, c), send_sem=sems[0].at[3 * w + s],
            recv_sem=sems[1].at[3 * w + s], device_id=_chip_dev(k, c), device_id_type=MESH)

    def landed(outs, sems, w, s):
        x, y, c, me = _where()
        j = (me + 3 - s) % 4
        return pltpu.make_async_remote_copy(
            src_ref=half(outs[w], w, j, c), dst_ref=half(outs[w], w, j, c), send_sem=sems[0].at[3 * w + s],
            recv_sem=sems[1].at[3 * w + s], device_id=(x, y, 1 - c), device_id_type=MESH)

    def passed(outs, sems, w, s, cc):
        x, y, c, me = _where()
        j = (me + 3 - s) % 4
        return pltpu.make_async_remote_copy(
            src_ref=half(outs[w], w, j, cc), dst_ref=half(outs[w], w, j, cc), send_sem=sems[2].at[3 * w + s],
            recv_sem=sems[3].at[3 * w + s], device_id=(x, y, 1 - c), device_id_type=MESH)

    pairs = [(w, s) for w in range(n) for s in range(3)]

    def start(ins, outs, sems):
        for w, s in pairs:
            ici(ins, outs, sems, w, s).start()

    def finish(ins, outs, sems):
        _, _, c, _ = _where()
        for w, s in pairs:
            landed(outs, sems, w, s).wait_recv()
            passed(outs, sems, w, s, c).start()
        for w, s in pairs:
            passed(outs, sems, w, s, 1 - c).wait_recv()
        for w, s in pairs:
            ici(ins, outs, sems, w, s).wait_send()
            passed(outs, sems, w, s, c).wait_send()

    return _Payload(bufs, [jax.ShapeDtypeStruct(b.shape, b.dtype) for b in bufs], {w: w for w in range(n)},
                    [pltpu.SemaphoreType.DMA((3 * n,))] * 4, start, finish)


def _sym_payload(operands, outs, copies, n_copies):
    def start(ins, outs_, sems):
        for cp in copies(ins, outs_, sems[0], sems[1]):
            cp.start()

    def finish(ins, outs_, sems):
        for cp in copies(ins, outs_, sems[0], sems[1]):
            cp.wait()

    p = _Payload(operands, outs, {}, [pltpu.SemaphoreType.DMA((n_copies,))] * 2, start, finish)
    p.copies, p.n_copies = copies, n_copies
    return p


def _swap_payload(g4s):
    def copies(ins, outs, ss, rs):
        x, y, c, me = _where()
        cps = []
        for w, g in enumerate(g4s):
            h = g.shape[1] // 2
            cps.append(pltpu.make_async_remote_copy(
                src_ref=ins[w].at[:, pl.ds((1 - c) * h, h)], dst_ref=outs[w], send_sem=ss.at[w],
                recv_sem=rs.at[w], device_id=(x, y, 1 - c), device_id_type=MESH))
        return cps

    outs = [jax.ShapeDtypeStruct((4, g.shape[1] // 2, g.shape[2]), g.dtype) for g in g4s]
    return _sym_payload(g4s, outs, copies, len(g4s))


def _exchange_payload(pbs):
    def copies(ins, outs, ss, rs):
        x, y, c, me = _where()
        cps = []
        for w in range(len(pbs)):
            for s in range(3):
                k = (me + 1 + s) % 4
                cps.append(pltpu.make_async_remote_copy(
                    src_ref=ins[w].at[k], dst_ref=outs[w].at[2 - s], send_sem=ss.at[3 * w + s],
                    recv_sem=rs.at[3 * w + s], device_id=_chip_dev(k, c), device_id_type=MESH))
        return cps

    outs = [jax.ShapeDtypeStruct((3,) + p.shape[1:], p.dtype) for p in pbs]
    return _sym_payload(pbs, outs, copies, 3 * len(pbs))


HBM_REF = pl.BlockSpec(memory_space=pltpu.HBM)
SEM_REF = pl.BlockSpec(memory_space=pltpu.SEMAPHORE)
DATAFLOW = pltpu.SideEffectType.DATAFLOW_SIDE_EFFECTING


class _SemList:
    def __init__(self, refs):
        self.refs = refs

    @property
    def at(self):
        return self.refs


def _split_start(p, name):
    n_in, n_out, nc = len(p.operands), len(p.outs), p.n_copies
    lands = getattr(p, "lands", None) or [lax.empty(s.shape, s.dtype) for s in p.outs]

    def body(*refs):
        ins, lnd = refs[:n_in], refs[n_in:n_in + n_out]
        sems = refs[n_in + n_out:n_in + n_out + 2 * nc]
        for cp in p.copies(ins, lnd, _SemList(sems[:nc]), _SemList(sems[nc:])):
            cp.start()
        refs[-1][...] = jnp.zeros((8, 128), F32)

    res = pl.pallas_call(
        body, name=name,
        in_specs=[HBM_REF] * (n_in + n_out),
        out_specs=[SEM_REF] * (2 * nc) + [HBM_REF] * (n_in + n_out) + [VMEM_FULL],
        out_shape=([pltpu.SemaphoreType.DMA(())] * (2 * nc) + _hbm_out(p.operands) + _hbm_out(lands)
                   + [jax.ShapeDtypeStruct((8, 128), F32)]),
        input_output_aliases={i: 2 * nc + i for i in range(n_in + n_out)},
        compiler_params=pltpu.CompilerParams(has_side_effects=DATAFLOW),
    )(*_hbm(*p.operands, *lands))
    k = 2 * nc
    return list(res[:k]), list(res[k:k + n_in]), list(res[k + n_in:k + n_in + n_out]), res[-1]


def _split_wait(p, handle, after, name):
    sems, srcs, lands, _ = handle
    n_in, n_out, nc = len(srcs), len(lands), p.n_copies

    def body(*refs):
        ins, lnd = refs[:n_in], refs[n_in:n_in + n_out]
        sm = refs[n_in + n_out:n_in + n_out + 2 * nc]
        for cp in p.copies(ins, lnd, _SemList(sm[:nc]), _SemList(sm[nc:])):
            cp.wait_send()
            cp.wait_recv()

    res = pl.pallas_call(
        body, name=name,
        in_specs=[HBM_REF] * (n_in + n_out) + [SEM_REF] * (2 * nc) + [ANY] * len(after),
        out_specs=[HBM_REF] * (n_in + n_out), out_shape=_hbm_out(srcs) + _hbm_out(lands),
        input_output_aliases={i: i for i in range(n_in + n_out)},
        compiler_params=pltpu.CompilerParams(has_side_effects=DATAFLOW),
    )(*srcs, *lands, *sems, *after)
    return list(res[:n_in]), list(res[n_in:])


def _join_payload(halves):
    def copies(ins, outs, ss, rs):
        x, y, c, me = _where()
        return [pltpu.make_async_remote_copy(
            src_ref=ins[w], dst_ref=outs[w], send_sem=ss.at[w], recv_sem=rs.at[w],
            device_id=(x, y, 1 - c), device_id_type=MESH) for w in range(len(halves))]

    outs = [jax.ShapeDtypeStruct(a.shape, a.dtype) for a in halves]
    return _sym_payload(halves, outs, copies, len(halves))


def _allgather_payload(v):
    def copies(ins, outs, ss, rs):
        x, y, c, me = _where()
        lin = 4 * x + 2 * y + c
        cps = []
        for o in range(1, 8):
            t = (lin + o) % 8
            cps.append(pltpu.make_async_remote_copy(
                src_ref=ins[0], dst_ref=outs[0].at[lin], send_sem=ss.at[o - 1], recv_sem=rs.at[o - 1],
                device_id=(t // 4, (t // 2) % 2, t % 2), device_id_type=MESH))
        return cps

    p = _sym_payload([v], [jax.ShapeDtypeStruct((8,) + v.shape, v.dtype)], copies, 7)
    x, y, c, _ = _where()
    p.lands = [lax.dynamic_update_slice(jnp.zeros((8,) + v.shape, v.dtype), v[None], (4 * x + 2 * y + c, 0, 0))]
    return p


def _sum8(buf, token):
    _, P, C = buf.shape

    def body(b_ref, t_ref, o_ref):
        acc = b_ref[0]
        for d in range(1, 8):
            acc = acc + b_ref[d]
        o_ref[...] = acc

    return pl.pallas_call(
        body, name="sum8", in_specs=[VMEM_FULL, VMEM_FULL], out_specs=VMEM_FULL,
        out_shape=jax.ShapeDtypeStruct((P, C), F32),
        compiler_params=pltpu.CompilerParams(vmem_limit_bytes=32 << 20),
    )(buf, token)


def _row_tile(h):
    for t in (256, 176, 128, 64, 32, 16, 8):
        if h % t == 0:
            return t
    raise ValueError(h)


def _pair_sum(cmidx, g4, got, name):
    _, R, C = g4.shape
    h = R // 2
    th = _row_tile(h)

    def body(cm_ref, a_ref, b_ref, o_ref, ob_ref):
        sm = a_ref[...] + b_ref[...]
        ob_ref[...] = sm.astype(BF16)

        @pl.when(pl.program_id(1) == cm_ref[1])
        def _():
            o_ref[...] = sm

    blk = pl.BlockSpec((None, th, C), lambda i, k, cm: (k, i, 0))
    gs = pltpu.PrefetchScalarGridSpec(
        num_scalar_prefetch=1, grid=(h // th, 4),
        in_specs=[pl.BlockSpec((None, None, th, C), lambda i, k, cm: (k, cm[0], i, 0)), blk],
        out_specs=[pl.BlockSpec((th, C), lambda i, k, cm: (i, 0)), blk])
    return pl.pallas_call(
        body, name=name, grid_spec=gs,
        out_shape=_hbm_out([jax.ShapeDtypeStruct((h, C), F32), jax.ShapeDtypeStruct((4, h, C), BF16)]),
        compiler_params=_cp(32, 2),
    )(cmidx, *_hbm(g4.reshape(4, 2, h, C), got))


def _chip_sum(own, got, name):
    h, C = own.shape
    th = _row_tile(h)

    def body(a_ref, b_ref, o_ref):
        o_ref[...] = ((a_ref[...] + b_ref[0].astype(F32)) + b_ref[1].astype(F32)) + b_ref[2].astype(F32)

    return pl.pallas_call(
        body, name=name, grid=(h // th,),
        in_specs=[pl.BlockSpec((th, C), lambda i: (i, 0)), pl.BlockSpec((3, th, C), lambda i: (0, i, 0))],
        out_specs=pl.BlockSpec((th, C), lambda i: (i, 0)),
        out_shape=_hbm_out(jax.ShapeDtypeStruct((h, C), F32)),
        compiler_params=_cp(32, 1),
    )(*_hbm(own, got))


def _adamw_math(w, g, m, v):
    m2 = B1 * m + (1.0 - B1) * g
    v2 = B2 * v + (1.0 - B2) * (g * g)
    m_hat = m2 / (1.0 - B1 ** STEP)
    v_hat = v2 / (1.0 - B2 ** STEP)
    delta = -LR * (m_hat / (jnp.sqrt(v_hat) + EPS) + WD * w)
    return delta, m2, v2


def _adamw_pair(cidx, w, mine, theirs, m, v, token, name):
    R, C = w.shape
    h = R // 2
    tr = _row_tile(h)
    nh = h // tr

    def body(c_ref, w_ref, a_ref, b_ref, m_ref, v_ref, t_ref, g_ref, d_ref, mo_ref, vo_ref):
        own = (pl.program_id(0) // nh) == c_ref[0]
        g = jnp.where(own, a_ref[...], b_ref[...])
        d, m2, v2 = _adamw_math(w_ref[...], g, m_ref[...], v_ref[...])
        g_ref[...] = g
        d_ref[...] = d
        mo_ref[...] = m2
        vo_ref[...] = v2

    blk = pl.BlockSpec((tr, C), lambda i, c: (i, 0))
    mine_blk = pl.BlockSpec((tr, C), lambda i, c: (jnp.clip(i - c[0] * nh, 0, nh - 1), 0))
    theirs_blk = pl.BlockSpec((tr, C), lambda i, c: (jnp.clip(i - (1 - c[0]) * nh, 0, nh - 1), 0))
    gs = pltpu.PrefetchScalarGridSpec(
        num_scalar_prefetch=1, grid=(R // tr,),
        in_specs=[blk, mine_blk, theirs_blk, blk, blk, pl.BlockSpec((8, 128), lambda i, c: (0, 0))],
        out_specs=[blk] * 4)
    return pl.pallas_call(
        body, name=name, grid_spec=gs, out_shape=_hbm_out([jax.ShapeDtypeStruct((R, C), F32)] * 4),
        compiler_params=_cp(32, 1),
    )(cidx, *_hbm(w, mine, theirs, m, v), token)


def _adamw(w, g, m, v, name):
    R, C = w.shape
    tr = _row_tile(R)

    def body(w_ref, g_ref, m_ref, v_ref, d_ref, mo_ref, vo_ref):
        d, m2, v2 = _adamw_math(w_ref[...], g_ref[...], m_ref[...], v_ref[...])
        d_ref[...] = d
        mo_ref[...] = m2
        vo_ref[...] = v2

    blk = pl.BlockSpec((tr, C), lambda i: (i, 0))
    return pl.pallas_call(
        body, name=name, grid=(R // tr,), in_specs=[blk] * 4, out_specs=[blk] * 3,
        out_shape=_hbm_out([jax.ShapeDtypeStruct((R, C), F32)] * 3),
        compiler_params=_cp(32, 1),
    )(*_hbm(w, g, m, v))


def _pack(arrs):
    flat = jnp.concatenate([a.reshape(-1).astype(F32) for a in arrs])
    rows = -(-flat.shape[0] // 1024)
    rows = -(-rows // 8) * 8
    return jnp.pad(flat, (0, rows * 1024 - flat.shape[0])).reshape(rows, 1024)


def _unpack(packed, shapes):
    flat = packed.reshape(-1)
    out, off = [], 0
    for s in shapes:
        n = math.prod(s)
        out.append(flat[off:off + n].reshape(s))
        off += n
    return out


BIG = ["ffn1_w_in", "ffn1_w_out", "mix_w_in", "conv_w_out", "ssm_w_glu", "mix_w_out",
       "ffn2_w_in", "ffn2_w_out", "ple_w_in", "ple_w_gate"]
SMALL = ["ln1_g", "ln1_b", "conv_w", "conv_b", "ssm_lam_re", "ssm_lam_im", "ssm_log_step", "ssm_b_re", "ssm_b_im",
         "ssm_c_re", "ssm_c_im", "ssm_d", "ln2_g", "ln2_b", "ln3_g", "ln3_b", "ln4_g", "ln4_b"]
WEIGHTS = ["ffn1_w_in", "ffn1_w_out", "ln1_g", "ln1_b", "mix_w_in", "conv_w", "conv_b", "conv_w_out",
           "ssm_lam_re", "ssm_lam_im", "ssm_log_step", "ssm_b_re", "ssm_b_im", "ssm_c_re", "ssm_c_im", "ssm_d",
           "ssm_w_glu", "mix_w_out", "ln2_g", "ln2_b", "ffn2_w_in", "ffn2_w_out", "ln3_g", "ln3_b",
           "ple_w_in", "ple_w_gate", "ln4_g", "ln4_b"]


def _s5_operands(sp):
    abr, abi, bbr, bbi = _zoh(sp["ssm_lam_re"], sp["ssm_lam_im"], sp["ssm_log_step"], sp["ssm_b_re"], sp["ssm_b_im"])
    return (_wb_blocks(bbr), _wb_blocks(bbi), _wc_blocks(sp["ssm_c_re"]), _wc_blocks(-sp["ssm_c_im"]),
            abr.reshape(1, LANES), abi.reshape(1, LANES), sp["ssm_d"].reshape(1, SSM))


def _local_step(x, p, target, sp, ops, sched):
    W = sched.W
    wb_re, wb_im, wc_re4, wc_im4, a_re, a_im, dvec = ops
    tm = TOKEN_TILE

    def run(fn, name, *args, **kw):
        outs, got = fn(*args, comm=sched.carry(name), **kw)
        sched.landed(name, got)
        sched.done[name] = outs[0]
        return outs

    def dw(name, wname, a, b, tk, tn, shape4, shard_cols=None, interleaved=False):
        out, got = _mm_tn(a, b, tk, tn, name, shard_cols=shard_cols, interleaved=interleaved,
                          comm=sched.carry(name))
        sched.landed(name, got)
        sched.done[name] = out
        sched.grad(wname, out.reshape(shape4))

    h1, r1, x1, x1b, xb = run(_ffn_fwd, "ffn1_fwd", x, W["ffn1_w_in"], W["ffn1_w_out"].reshape(2, FFH, D),
                              sp["ln1_g"], sp["ln1_b"], tm, "ffn1_fwd")
    conv_w = W["conv_w"][:, 0:3, :].transpose(1, 0, 2).reshape(3, CONV)
    pc, z_b, yin_b, su, su_b, g_conv, g_ssm, y_conv = run(
        _mix_fwd_a, "mix_fwd_a", x1b, W["mix_w_in"], conv_w, sp["conv_b"], W["conv_w_out"], tm)
    st_re, st_im = run(_s5_scan_fwd, "s5_scan_fwd", su_b, wb_re, wb_im, a_re, a_im)
    w_mo = W["mix_w_out"].reshape(D, D)
    s, sg_b, ga, gb, merged_b, r2, x2 = run(
        _mix_fwd_b, "mix_fwd_b", st_re, st_im, wc_re4, wc_im4, su, dvec, W["ssm_w_glu"], g_conv, g_ssm, y_conv,
        w_mo, x1, sp["ln2_g"], sp["ln2_b"], tm)
    w2o2 = W["ffn2_w_out"].reshape(2, FFH, D)
    h2, r3, x3, x3b, x2b = run(_ffn_fwd, "ffn2_fwd", x2, W["ffn2_w_in"], w2o2, sp["ln3_g"], sp["ln3_b"], tm,
                               "ffn2_fwd")
    loss_part, dx3, p_b, dpw_b, dgt_b, dg4, db4 = _ple_loss(
        x3, x3b, p, W["ple_w_in"], W["ple_w_gate"].reshape(D, D), sp["ln4_g"], sp["ln4_b"], target, tm)

    dw("dw_ple_gate", "ple_w_gate", x3b, dgt_b, 512, 1024, (4, 256, D))
    dw("dw_ple_in", "ple_w_in", p_b, dpw_b, 256, 256, (4, 256, 256), shard_cols=256)
    dx2, dh2, a2_b, df2_b, dg3, db3 = run(_ffn_bwd, "ffn2_bwd", dx3, r3, sp["ln3_g"], h2, W["ffn2_w_in"], w2o2,
                                          tm, "ffn2_bwd")
    dw("dw_ffn2_in", "ffn2_w_in", x2b, dh2, 512, FFH, (4, D, FFH), shard_cols=FFH, interleaved=True)
    dw("dw_ffn2_out", "ffn2_w_out", a2_b, df2_b, FFH, 1024, (4, FF // 4, D))
    (dres, dmix_b, dgl_b, ds_b, du_dir, gs_re, gs_im, dyc_b, dproj, dg2, db2, dd) = run(
        _mix_bwd_b, "mix_bwd_b", dx2, r2, sp["ln2_g"], w_mo, g_conv, g_ssm, y_conv, ga, gb, s, su, dvec,
        W["ssm_w_glu"], wc_re4, wc_im4, tm)
    dw("dw_mix_out", "mix_w_out", merged_b, dmix_b, 512, 1024, (4, 256, D))
    dw("dw_glu", "ssm_w_glu", sg_b, dgl_b, 512, 512, (4, SSM, 512), shard_cols=512)
    dsu_ssm, dwb_re, dwb_im, dwc_re, dwc_im, da_re, da_im = run(
        _s5_scan_bwd, "s5_scan_bwd", gs_re, gs_im, st_re, st_im, su_b, ds_b, wb_re, wb_im, a_re, a_im)
    dw("dw_conv_out", "conv_w_out", yin_b, dyc_b, 512, 256, (4, CONV, 256), shard_cols=256)
    dproj, dx1, dcw8, dcb = run(_mix_bwd_a, "mix_bwd_a", dyc_b, W["conv_w_out"], pc, z_b, conv_w, dsu_ssm,
                                du_dir, dproj, dres, W["mix_w_in"], tm)
    dw("dw_mix_in", "mix_w_in", x1b, dproj, 512, 1024, (4, D, D), shard_cols=1024)
    dx0, dh1, a1_b, df1_b, dg1, db1 = run(_ffn_bwd, "ffn1_bwd", dx1, r1, sp["ln1_g"], h1, W["ffn1_w_in"],
                                          W["ffn1_w_out"].reshape(2, FFH, D), tm, "ffn1_bwd")
    sched.small(dict(
        ln1_g=dg1, ln1_b=db1, ln2_g=dg2, ln2_b=db2, ln3_g=dg3, ln3_b=db3, ln4_g=dg4, ln4_b=db4,
        conv_w=dcw8[0:3], conv_b=dcb,
        a_re=da_re.reshape(GROUPS, STATE), a_im=da_im.reshape(GROUPS, STATE),
        bb_re=_wb_diag(dwb_re), bb_im=_wb_diag(dwb_im),
        ssm_c_re=_wc_diag(dwc_re), ssm_c_im=-_wc_diag(dwc_im), ssm_d=dd.reshape(GROUPS, 16),
        loss=loss_part[0:1, 0]))
    dw("dw_ffn1_in", "ffn1_w_in", xb, dh1, 512, FFH, (4, D, FFH), shard_cols=FFH, interleaved=True)
    dw("dw_ffn1_out", "ffn1_w_out", a1_b, df1_b, FFH, 1024, (4, FF // 4, D))
    return loss_part[0, 0], dx0


RAW_ORDER = ["ln1_g", "ln1_b", "ln2_g", "ln2_b", "ln3_g", "ln3_b", "ln4_g", "ln4_b", "conv_w", "conv_b",
             "a_re", "a_im", "bb_re", "bb_im", "ssm_c_re", "ssm_c_im", "ssm_d", "loss"]

GATHER_FIRST = ["ffn1_w_in", "ffn1_w_out"]
GATHER_AT = {"ffn1_fwd": ["mix_w_in", "conv_w_out", "conv_w"], "mix_fwd_a": ["ssm_w_glu", "mix_w_out"],
             "s5_scan_fwd": ["ffn2_w_in"], "mix_fwd_b": ["ffn2_w_out"], "ffn2_fwd": ["ple_w_in", "ple_w_gate"]}
REDUCE_GROUP = {"ffn2": ["ple_w_gate", "ple_w_in", "ffn2_w_in", "ffn2_w_out"],
                "mix": ["mix_w_out", "ssm_w_glu", "conv_w_out", "mix_w_in"], "ffn1": ["ffn1_w_in", "ffn1_w_out"]}
REDUCE_AT = {"mix_bwd_b": [("swap", "ffn2")], "mix_bwd_a": [("join", "ffn2")]}
BEGIN_AT = {"dw_mix_out": [("exchange", "ffn2")], "ffn1_bwd": [("swap", "mix")],
            "dw_ffn1_in": [("small", None), ("exchange", "mix")]}
BEHIND = {"dw_glu": [("exchange", "ffn2")], "s5_scan_bwd": [("exchange", "ffn2")]}
END_AT = {"mix_bwd_a": [("exchange", "ffn2", ["dw_mix_out", "dw_glu", "s5_scan_bwd"])],
          "dw_ffn1_in": [("swap", "mix", ["ffn1_bwd"])]}
LAST_GROUP = "ffn1"


class _Sched:
    def __init__(self, cmidx):
        self.bufs, self.cmidx = {}, cmidx
        self.W, self.G, self.raw, self.small_buf = {}, {}, None, None
        self.got1, self.p32, self.pbf, self.got2, self.half, self.theirs = {}, {}, {}, {}, {}, {}
        self._open, self._split, self.done = [], {}, {}

    def first_begin(self, bufs):
        self.bufs.update(bufs)
        p = _gather_ici_payload([bufs[n] for n in GATHER_FIRST])
        self._first = (p, _split_start(p, "gather_first_start"))
        return self._first[1][3]

    def first_end(self, bufs, after):
        self.bufs.update(bufs)
        p, handle = self._first
        _, landed = _split_wait(p, handle, after, "gather_first_wait")
        (outs,) = _comm_call("gather_first_pass", [_gather_pass_payload(landed)])
        self.W.update(zip(GATHER_FIRST, outs))

    def _payload(self, stage, key):
        if stage == "gather":
            return _gather_payload([self.bufs[n] for n in key])
        if stage == "small":
            return _allgather_payload(_pack([self.raw[k] for k in RAW_ORDER]))
        names = REDUCE_GROUP[key]
        if stage == "swap":
            return _swap_payload([self.G[n] for n in names])
        if stage == "exchange":
            for n in names:
                self.p32[n], self.pbf[n] = _pair_sum(self.cmidx, self.G[n], self.got1[n], "pair_sum_" + n)
            return _exchange_payload([self.pbf[n] for n in names])
        for n in names:
            self.half[n] = _chip_sum(self.p32[n], self.got2[n], "chip_sum_" + n)
        return _join_payload([self.half[n] for n in names])

    def _store(self, stages, got):
        for (stage, key), outs in zip(stages, got):
            if stage == "gather":
                self.W.update(zip(key, outs))
            elif stage == "small":
                self.small_buf = outs[0]
            else:
                {"swap": self.got1, "exchange": self.got2, "join": self.theirs}[stage].update(
                    zip(REDUCE_GROUP[key], outs))

    def _standalone(self, name, stages):
        self._store(stages, _comm_call(name, [self._payload(s, k) for s, k in stages]))

    def carry(self, name):
        for stage, key, behind in END_AT.get(name, []):
            self._end(stage, key, [self.done[b] for b in behind])
        tokens = [self._begin(stage, key) for stage, key in BEGIN_AT.get(name, [])]
        tokens += [self._split[sk][1][3] for sk in BEHIND.get(name, [])]
        self._open = [("gather", GATHER_AT[name])] if name in GATHER_AT else []
        self._open += REDUCE_AT.get(name, [])
        comm = [self._payload(s, k) for s, k in self._open]
        if tokens:
            comm.append(_Payload(tokens, [], {}, [], lambda *a: None, lambda *a: None))
        return tuple(comm)

    def landed(self, name, got):
        self._store(self._open, got)

    def grad(self, name, g4):
        self.G[name] = g4

    def small(self, raw):
        self.raw = raw

    def _begin(self, stage, key):
        p = self._payload(stage, key)
        self._split[stage, key] = (p, _split_start(p, "%s_%s_start" % (stage, key)))
        return self._split[stage, key][1][3]

    def _end(self, stage, key, after):
        p, handle = self._split.pop((stage, key))
        srcs, lands = _split_wait(p, handle, after, "%s_%s_wait" % (stage, key))
        if stage == "swap":
            self.G.update(zip(REDUCE_GROUP[key], srcs))
        self._store([(stage, key)], [lands])

    def tail_begin(self):
        return self._begin("swap", LAST_GROUP)

    def tail_mid(self, after):
        self._end("swap", LAST_GROUP, after)
        token = self._begin("exchange", LAST_GROUP)
        self._end("small", None, [token])
        self._end("exchange", "mix", [token])
        self._standalone("reduce_tail_join_mix", [("join", "mix")])
        return token

    def tail_end(self, after):
        self._end("exchange", LAST_GROUP, after)
        self._standalone("reduce_tail_join", [("join", LAST_GROUP)])


def _small_grads(raw_sum, sp):
    _, vjp = jax.vjp(_zoh, sp["ssm_lam_re"], sp["ssm_lam_im"], sp["ssm_log_step"], sp["ssm_b_re"], sp["ssm_b_im"])
    d_lre, d_lim, d_ls, d_bre, d_bim = vjp((raw_sum["a_re"], raw_sum["a_im"], raw_sum["bb_re"], raw_sum["bb_im"]))
    g = {k: raw_sum[k] for k in ("ln1_g", "ln1_b", "ln2_g", "ln2_b", "ln3_g", "ln3_b", "ln4_g", "ln4_b",
                                 "conv_w", "conv_b", "ssm_c_re", "ssm_c_im", "ssm_d")}
    g.update(ssm_lam_re=d_lre, ssm_lam_im=d_lim, ssm_log_step=d_ls, ssm_b_re=d_bre, ssm_b_im=d_bim)
    return g


def kernel(x, p, ffn1_w_in, ffn1_w_out, ln1_g, ln1_b, mix_w_in, conv_w, conv_b, conv_w_out, ssm_lam_re, ssm_lam_im, ssm_log_step, ssm_b_re, ssm_b_im, ssm_c_re, ssm_c_im, ssm_d, ssm_w_glu, mix_w_out, ln2_g, ln2_b, ffn2_w_in, ffn2_w_out, ln3_g, ln3_b, ple_w_in, ple_w_gate, ln4_g, ln4_b, loss_target, m_ffn1_w_in, m_ffn1_w_out, m_ln1_g, m_ln1_b, m_mix_w_in, m_conv_w, m_conv_b, m_conv_w_out, m_ssm_lam_re, m_ssm_lam_im, m_ssm_log_step, m_ssm_b_re, m_ssm_b_im, m_ssm_c_re, m_ssm_c_im, m_ssm_d, m_ssm_w_glu, m_mix_w_out, m_ln2_g, m_ln2_b, m_ffn2_w_in, m_ffn2_w_out, m_ln3_g, m_ln3_b, m_ple_w_in, m_ple_w_gate, m_ln4_g, m_ln4_b, v_ffn1_w_in, v_ffn1_w_out, v_ln1_g, v_ln1_b, v_mix_w_in, v_conv_w, v_conv_b, v_conv_w_out, v_ssm_lam_re, v_ssm_lam_im, v_ssm_log_step, v_ssm_b_re, v_ssm_b_im, v_ssm_c_re, v_ssm_c_im, v_ssm_d, v_ssm_w_glu, v_mix_w_out, v_ln2_g, v_ln2_b, v_ffn2_w_in, v_ffn2_w_out, v_ln3_g, v_ln3_b, v_ple_w_in, v_ple_w_gate, v_ln4_g, v_ln4_b):
    args = dict(locals())
    w = {n: args[n] for n in WEIGHTS}
    m = {n: args["m_" + n] for n in WEIGHTS}
    v = {n: args["v_" + n] for n in WEIGHTS}
    _, _, c, me = _where()
    cidx = jnp.stack([c, me]).astype(jnp.int32)
    meidx = jnp.reshape(me, (1,)).astype(jnp.int32)

    sched = _Sched(cidx)
    token = sched.first_begin({n: _slot_cast(meidx, w[n][0], BF16, "cast_" + n) for n in GATHER_FIRST})
    rest = {n: _slot_cast(meidx, w[n][0], BF16, "cast_" + n, (token,)) for n in BIG if n not in GATHER_FIRST}
    rest["conv_w"] = _slot_cast(meidx, jnp.pad(conv_w[0], ((0, 13), (0, 0))), F32, "cast_conv_w", (token,))
    sp = {n: (w[n] if w[n].ndim == 2 and n != "ssm_log_step" else w[n][0]) for n in SMALL if n != "conv_w"}
    ops = _s5_operands({**sp, "ssm_lam_re": sp["ssm_lam_re"] + token[0, 0]})
    sched.first_end(rest, list(rest.values()) + list(ops))
    loss_part, dx0 = _local_step(x[0], p[0, 0], loss_target[0], sp, ops, sched)
    out_g, out_d, out_m, out_v = {}, {}, {}, {}

    def big_adamw(names, token):
        for n in names:
            g, dl, mn, vn = _adamw_pair(cidx, w[n][0], sched.half[n], sched.theirs[n], m[n][0], v[n][0], token,
                                        "adamw_" + n)
            out_g[n], out_d[n], out_m[n], out_v[n] = g[None], dl[None], mn[None], vn[None]

    first = ["ple_w_gate", "ple_w_in", "ffn2_w_in"]
    big_adamw(first, sched.tail_begin())
    token = sched.tail_mid([out_v[n] for n in first])
    big_adamw(["ffn2_w_out"], token)

    raw_shapes = [sched.raw[k].shape for k in RAW_ORDER]
    raw_sum = dict(zip(RAW_ORDER, _unpack(_sum8(sched.small_buf, token), raw_shapes)))
    loss = raw_sum["loss"][0]
    sg = _small_grads(raw_sum, sp)
    sg["conv_w"] = lax.dynamic_slice_in_dim(sg["conv_w"], me * 128, 128, axis=1)
    small_shapes = [w[n].shape for n in SMALL]
    gp = _pack([sg[n] for n in SMALL])
    d_s, m_s, v_s = _adamw(_pack([w[n] for n in SMALL]), gp, _pack([m[n] for n in SMALL]),
                           _pack([v[n] for n in SMALL]), "adamw_small")

    for n, a, b_, c_, d_ in zip(SMALL, _unpack(gp, small_shapes), _unpack(d_s, small_shapes),
                                _unpack(m_s, small_shapes), _unpack(v_s, small_shapes)):
        out_g[n], out_d[n], out_m[n], out_v[n] = a, b_, c_, d_
    big_adamw(REDUCE_GROUP["mix"], token)
    sched.tail_end([d_s, out_v["ffn2_w_out"]] + [out_v[n] for n in REDUCE_GROUP["mix"]])
    big_adamw(REDUCE_GROUP[LAST_GROUP], token)

    return (loss, dx0[None], *[out_g[n] for n in WEIGHTS], *[out_d[n] for n in WEIGHTS],
            *[out_m[n] for n in WEIGHTS], *[out_v[n] for n in WEIGHTS])
```

```python
import functools
import math

import jax
import jax.numpy as jnp
import numpy as np
from jax import lax
from jax.experimental import pallas as pl
from jax.experimental.pallas import tpu as pltpu
from jax.experimental.pallas import tpu_sc as plsc

F32, BF16 = jnp.float32, jnp.bfloat16
D = 1024
FF = 2816
FFH = FF // 2
CONV = 512
SSM = 512
GROUPS = 32
STATE = 64
LANES = GROUPS * STATE
SCAN_W = 128
SCAN_PER = 512 // SCAN_W
SCAN_GR = SCAN_W // STATE
SCAN_R = 256
TOKEN_TILE = 256
ALPHA = 2.0 ** 0.25
LN_EPS = 1e-5
GELU_C = math.sqrt(2.0 / math.pi)
B1, B2, LR, EPS, WD, STEP = 0.9, 0.999, 0.001, 1e-8, 0.01, 10
MESH = pl.DeviceIdType.MESH
ANY = pl.BlockSpec(memory_space=pl.ANY)
VMEM_FULL = pl.BlockSpec(memory_space=pltpu.VMEM)


def _cp(vmem_mb=48, n_axes=1):
    return pltpu.CompilerParams(vmem_limit_bytes=vmem_mb << 20,
                                dimension_semantics=("arbitrary",) * n_axes)


def _hbm(*arrs):
    return [pltpu.with_memory_space_constraint(a, pltpu.HBM) for a in arrs]


def _hbm_out(shapes):
    if isinstance(shapes, (list, tuple)):
        return [pltpu.HBM(s.shape, s.dtype) for s in shapes]
    return pltpu.HBM(shapes.shape, shapes.dtype)


def _nn(a, b):
    return jnp.dot(a, b, preferred_element_type=F32)


def _nt(a, b):
    return lax.dot_general(a, b, (((1,), (1,)), ((), ())), preferred_element_type=F32)


def _tn(a, b):
    return lax.dot_general(a, b, (((0,), (0,)), ((), ())), preferred_element_type=F32)


def _sig(v):
    return jax.nn.sigmoid(v)


def _ln_stats(r):
    mu = jnp.mean(r, axis=-1, keepdims=True)
    xc = r - mu
    var = jnp.mean(xc * xc, axis=-1, keepdims=True)
    rstd = lax.rsqrt(var + LN_EPS)
    return xc * rstd, rstd


def _ln_bwd(dy, r, g):
    xhat, rstd = _ln_stats(r)
    dyg = dy * g
    m1 = jnp.mean(dyg, axis=-1, keepdims=True)
    m2 = jnp.mean(dyg * xhat, axis=-1, keepdims=True)
    return rstd * (dyg - m1 - xhat * m2), xhat


def _rowsum(v):
    return jnp.sum(v, axis=0, keepdims=True)


class _Payload:
    def __init__(self, operands, outs, aliases, sems, start, finish):
        self.operands, self.outs, self.aliases, self.sems = list(operands), list(outs), dict(aliases), list(sems)
        self.start, self.finish = start, finish


def _split(flat, comm, attr):
    out, i = [], 0
    for p in comm:
        n = len(getattr(p, attr))
        out.append(list(flat[i:i + n]))
        i += n
    return out


def _run_comm(comm, which, cin, cout, csem):
    for p, a, b, s in zip(comm, _split(cin, comm, "operands"), _split(cout, comm, "outs"), _split(csem, comm, "sems")):
        getattr(p, which)(a, b, s)


def _pcall(body, *, name, grid, in_specs, out_specs, out_shape, operands, scratch=(), vmem_mb=48, aliases=None,
           comm=()):
    ni, no, ns = len(in_specs), len(out_specs), len(scratch)
    c_ops = [a for p in comm for a in p.operands]
    c_outs = [s for p in comm for s in p.outs]
    c_sems = [s for p in comm for s in p.sems]
    io = dict(aliases or {})
    off_i, off_o = ni, no
    for p in comm:
        for a, b in p.aliases.items():
            io[off_i + a] = off_o + b
        off_i += len(p.operands)
        off_o += len(p.outs)

    def wrapped(*refs):
        ins, cin = refs[:ni], refs[ni:ni + len(c_ops)]
        o0 = ni + len(c_ops)
        outs, cout = refs[o0:o0 + no], refs[o0 + no:o0 + no + len(c_outs)]
        s0 = o0 + no + len(c_outs)
        scr, csem = refs[s0:s0 + ns], refs[s0 + ns:]
        if comm:
            first = functools.reduce(jnp.logical_and, [pl.program_id(a) == 0 for a in range(len(grid))])
            pl.when(first)(lambda: _run_comm(comm, "start", cin, cout, csem))
        body(*ins, *outs, *scr)
        if comm:
            last = functools.reduce(jnp.logical_and, [pl.program_id(a) == grid[a] - 1 for a in range(len(grid))])
            pl.when(last)(lambda: _run_comm(comm, "finish", cin, cout, csem))

    res = pl.pallas_call(
        wrapped, name=name, grid=grid,
        in_specs=list(in_specs) + [ANY] * len(c_ops), out_specs=list(out_specs) + [ANY] * len(c_outs),
        out_shape=_hbm_out(list(out_shape) + c_outs), scratch_shapes=list(scratch) + c_sems,
        input_output_aliases=io,
        compiler_params=pltpu.CompilerParams(vmem_limit_bytes=vmem_mb << 20,
                                             dimension_semantics=("arbitrary",) * len(grid),
                                             has_side_effects=bool(c_sems)),
    )(*_hbm(*operands, *c_ops))
    return list(res[:no]), _split(res[no:], comm, "outs")


def _comm_call(name, comm):
    c_ops = [a for p in comm for a in p.operands]
    c_outs = [s for p in comm for s in p.outs]
    c_sems = [s for p in comm for s in p.sems]
    io, off_i, off_o = {}, 0, 0
    for p in comm:
        for a, b in p.aliases.items():
            io[off_i + a] = off_o + b
        off_i += len(p.operands)
        off_o += len(p.outs)

    def body(*refs):
        cin, cout = refs[:len(c_ops)], refs[len(c_ops):len(c_ops) + len(c_outs)]
        csem = refs[len(c_ops) + len(c_outs):]
        _run_comm(comm, "start", cin, cout, csem)
        _run_comm(comm, "finish", cin, cout, csem)

    res = pl.pallas_call(
        body, name=name, in_specs=[ANY] * len(c_ops), out_specs=[ANY] * len(c_outs), out_shape=_hbm_out(c_outs),
        scratch_shapes=c_sems, input_output_aliases=io,
        compiler_params=pltpu.CompilerParams(has_side_effects=True),
    )(*_hbm(*c_ops))
    return _split(res, comm, "outs")


def _ffn_fwd(x, w_in4, w_out2, g, b, tm, name, comm=()):
    T = x.shape[0]

    def body(x_ref, win_ref, wo_ref, g_ref, b_ref, h_ref, r_ref, xo_ref, xob_ref, xib_ref):
        xf = x_ref[...]
        xv = xf.astype(BF16)
        xib_ref[...] = xv
        acc = ALPHA * xf
        for k in range(2):
            gt = _nn(xv, win_ref[k])
            up = _nn(xv, win_ref[k + 2])
            a = (gt * _sig(gt) * up).astype(BF16)
            h_ref[:, 2 * k * FFH:(2 * k + 1) * FFH] = gt.astype(BF16)
            h_ref[:, (2 * k + 1) * FFH:(2 * k + 2) * FFH] = up.astype(BF16)
            acc = acc + 0.5 * _nn(a, wo_ref[k])
        xhat, _ = _ln_stats(acc)
        xo = xhat * g_ref[...] + b_ref[...]
        r_ref[...] = acc
        xo_ref[...] = xo
        xob_ref[...] = xo.astype(BF16)

    tok = pl.BlockSpec((tm, D), lambda i: (i, 0))
    vec = pl.BlockSpec((1, D), lambda i: (0, 0))
    return _pcall(
        body, name=name, grid=(T // tm,),
        in_specs=[tok,
                  pl.BlockSpec((4, D, FFH), lambda i: (0, 0, 0), pipeline_mode=pl.Buffered(1)),
                  pl.BlockSpec((2, FFH, D), lambda i: (0, 0, 0), pipeline_mode=pl.Buffered(1)),
                  vec, vec],
        out_specs=[pl.BlockSpec((tm, 2 * FF), lambda i: (i, 0)), tok, tok, tok, tok],
        out_shape=[jax.ShapeDtypeStruct((T, 2 * FF), BF16), jax.ShapeDtypeStruct((T, D), F32),
                   jax.ShapeDtypeStruct((T, D), F32), jax.ShapeDtypeStruct((T, D), BF16),
                   jax.ShapeDtypeStruct((T, D), BF16)],
        vmem_mb=58, comm=comm, operands=(x, w_in4, w_out2, g, b))


def _ffn_bwd(dy, r, g, h, w_in4, w_out2, tm, name, comm=()):
    T = dy.shape[0]

    def body(dy_ref, r_ref, g_ref, h_ref, win_ref, wo_ref, dx_ref, dh_ref, a_ref, df_ref, dg_ref, db_ref):
        i = pl.program_id(0)
        dyv = dy_ref[...]
        dr, xhat = _ln_bwd(dyv, r_ref[...], g_ref[...])
        dg_ref[...] = jnp.where(i == 0, 0.0, dg_ref[...]) + _rowsum(dyv * xhat)
        db_ref[...] = jnp.where(i == 0, 0.0, db_ref[...]) + _rowsum(dyv)
        dfb = (0.5 * dr).astype(BF16)
        df_ref[...] = dfb
        acc = ALPHA * dr
        for k in range(2):
            da = _nt(dfb, wo_ref[k])
            gt = h_ref[:, 2 * k * FFH:(2 * k + 1) * FFH].astype(F32)
            up = h_ref[:, (2 * k + 1) * FFH:(2 * k + 2) * FFH].astype(F32)
            sg = _sig(gt)
            silu = gt * sg
            dgate = (da * up * (sg * (1.0 + gt * (1.0 - sg)))).astype(BF16)
            dup = (da * silu).astype(BF16)
            a_ref[:, k * FFH:(k + 1) * FFH] = (silu * up).astype(BF16)
            dh_ref[:, 2 * k * FFH:(2 * k + 1) * FFH] = dgate
            dh_ref[:, (2 * k + 1) * FFH:(2 * k + 2) * FFH] = dup
            acc = acc + _nt(dgate, win_ref[k]) + _nt(dup, win_ref[k + 2])
        dx_ref[...] = acc

    tok = pl.BlockSpec((tm, D), lambda i: (i, 0))
    vec = pl.BlockSpec((1, D), lambda i: (0, 0))
    wide = pl.BlockSpec((tm, 2 * FF), lambda i: (i, 0))
    return _pcall(
        body, name=name, grid=(T // tm,),
        in_specs=[tok, tok, vec, wide,
                  pl.BlockSpec((4, D, FFH), lambda i: (0, 0, 0), pipeline_mode=pl.Buffered(1)),
                  pl.BlockSpec((2, FFH, D), lambda i: (0, 0, 0), pipeline_mode=pl.Buffered(1))],
        out_specs=[tok, wide, pl.BlockSpec((tm, FF), lambda i: (i, 0)), tok, vec, vec],
        out_shape=[jax.ShapeDtypeStruct((T, D), F32), jax.ShapeDtypeStruct((T, 2 * FF), BF16),
                   jax.ShapeDtypeStruct((T, FF), BF16), jax.ShapeDtypeStruct((T, D), BF16),
                   jax.ShapeDtypeStruct((1, D), F32), jax.ShapeDtypeStruct((1, D), F32)],
        vmem_mb=58, comm=comm, operands=(dy, r, g, h, w_in4, w_out2))


def _mm_tn(a, b, tk, tn, name, shard_cols=None, interleaved=False, comm=()):
    T, K = a.shape
    N = b.shape[1]

    def body(a_ref, b_ref, o_ref):
        o_ref[...] = _tn(a_ref[...], b_ref[...])

    if shard_cols is None:
        out_shape = jax.ShapeDtypeStruct((K, N), F32)
        out_spec = pl.BlockSpec((tk, tn), lambda ki, nj: (ki, nj))
    else:
        per = shard_cols // tn

        def shard(nj):
            blk = nj // per
            return (blk % 2) * 2 + blk // 2 if interleaved else blk

        out_shape = jax.ShapeDtypeStruct((N // shard_cols, K, shard_cols), F32)
        out_spec = pl.BlockSpec((None, tk, tn), lambda ki, nj: (shard(nj), ki, nj % per))
    (out,), got = _pcall(
        body, name=name, grid=(K // tk, N // tn),
        in_specs=[pl.BlockSpec((T, tk), lambda ki, nj: (0, ki)), pl.BlockSpec((T, tn), lambda ki, nj: (0, nj))],
        out_specs=[out_spec], out_shape=[out_shape], comm=comm, operands=(a, b))
    return out, got


def _mix_fwd_a(xb, w_mix4, conv_w, conv_b, w_co4, tm, comm=()):
    T = xb.shape[0]

    def body(xb_ref, w_ref, cw_ref, cb_ref, wco_ref,
             pc_ref, z_ref, yin_ref, su_ref, sub_ref, gc_ref, gs_ref, yc_ref, qbuf):
        @pl.when(pl.program_id(0) == 0)
        def _():
            qbuf[pl.ds(0, 8), :] = jnp.zeros((8, CONV), F32)

        xv = xb_ref[...]
        p0 = _nn(xv, w_ref[0])
        p1 = _nn(xv, w_ref[1])
        gc_ref[...] = _nn(xv, w_ref[2]).astype(BF16)
        gs_ref[...] = _nn(xv, w_ref[3]).astype(BF16)
        cbv, ccv = p0[:, :CONV], p0[:, CONV:]
        chv, suv = p1[:, :CONV], p1[:, CONV:]
        q = ccv * chv
        qbuf[pl.ds(8, tm), :] = q
        cw = cw_ref[...]
        z = (cw[2:3] * q + cw[1:2] * qbuf[pl.ds(7, tm), :] + cw[0:1] * qbuf[pl.ds(6, tm), :]
             + cb_ref[...])
        qbuf[pl.ds(0, 8), :] = q[tm - 8:tm]
        yin = (cbv * z).astype(BF16)
        pc_ref[:, 0:CONV] = cbv.astype(BF16)
        pc_ref[:, CONV:2 * CONV] = ccv.astype(BF16)
        pc_ref[:, 2 * CONV:3 * CONV] = chv.astype(BF16)
        z_ref[...] = z.astype(BF16)
        yin_ref[...] = yin
        su_ref[...] = suv
        sub_ref[...] = suv.astype(BF16)
        for k in range(4):
            yc_ref[:, 256 * k:256 * (k + 1)] = _nn(yin, wco_ref[k]).astype(BF16)

    def tok(n):
        return pl.BlockSpec((tm, n), lambda i: (i, 0))

    def full(shape):
        return pl.BlockSpec(shape, lambda i: (0,) * len(shape))

    return _pcall(
        body, name="mix_fwd_a", grid=(T // tm,),
        in_specs=[tok(D), full((4, D, D)), full((3, CONV)), full((1, CONV)), full((4, CONV, 256))],
        out_specs=[tok(3 * CONV), tok(CONV), tok(CONV), tok(SSM), tok(SSM), tok(D), tok(D), tok(D)],
        out_shape=[jax.ShapeDtypeStruct((T, 3 * CONV), BF16), jax.ShapeDtypeStruct((T, CONV), BF16),
                   jax.ShapeDtypeStruct((T, CONV), BF16), jax.ShapeDtypeStruct((T, SSM), F32),
                   jax.ShapeDtypeStruct((T, SSM), BF16), jax.ShapeDtypeStruct((T, D), BF16),
                   jax.ShapeDtypeStruct((T, D), BF16), jax.ShapeDtypeStruct((T, D), BF16)],
        scratch=[pltpu.VMEM((tm + 8, CONV), F32)], vmem_mb=56, comm=comm,
        operands=(xb, w_mix4, conv_w, conv_b, w_co4))


def _scan_rows(bre, bim, ar, ai, T, rev, load):
    R, W, G = SCAN_R, bre.shape[1], T // 8
    if rev:
        ai = -ai

    def cmul(pr, pi, xr, xi):
        return pr * xr - pi * xi, pr * xi + pi * xr

    pw = [(ar, ai)]
    for _ in range(7):
        pw.append(cmul(ar, ai, *pw[-1]))

    def shifted(v, d, axis, n, idx):
        if rev:
            return jnp.where(idx < n - d, pltpu.roll(v, n - d, axis), 0.0)
        return jnp.where(idx >= d, pltpu.roll(v, d, axis), 0.0)

    sub8 = lax.broadcasted_iota(jnp.int32, (8, W), 0)
    inside = {d: (sub8 < 8 - d) if rev else (sub8 >= d) for d in (1, 2, 4)}
    pm = {d: (jnp.where(inside[d], pw[d - 1][0], 0.0)[None], jnp.where(inside[d], pw[d - 1][1], 0.0)[None])
          for d in (1, 2, 4)}

    def step(i, _):
        t0 = pl.multiple_of(i * R, R)
        vr, vi = load(t0)
        vr, vi = vr.reshape(R // 8, 8, W), vi.reshape(R // 8, 8, W)
        for d in (1, 2, 4):
            sh = (8 - d) if rev else d
            dr, di = cmul(pm[d][0], pm[d][1], pltpu.roll(vr, sh, 1), pltpu.roll(vi, sh, 1))
            vr, vi = vr + dr, vi + di
        bre[pl.ds(t0 + 8, R), :] = vr.reshape(R, W)
        bim[pl.ds(t0 + 8, R), :] = vi.reshape(R, W)
        return 0

    lax.fori_loop(0, T // R, step, 0)

    edge = 0 if rev else 7
    cr = bre[pl.ds(8 + edge, G, stride=8), :]
    ci = bim[pl.ds(8 + edge, G, stride=8), :]
    row = lax.broadcasted_iota(jnp.int32, (G, W), 0)
    qr, qi = pw[7]
    d = 1
    while d < G:
        dr, di = cmul(qr, qi, shifted(cr, d, 0, G, row), shifted(ci, d, 0, G, row))
        cr, ci = cr + dr, ci + di
        qr, qi = qr * qr - qi * qi, 2.0 * qr * qi
        d *= 2

    nr, ni = shifted(cr, 1, 0, G, row), shifted(ci, 1, 0, G, row)
    for r in range(8):
        pr, pi = pw[7 - r] if rev else pw[r]
        dr, di = cmul(pr, pi, nr, ni)
        bre[pl.ds(8 + r, G, stride=8), :] = bre[pl.ds(8 + r, G, stride=8), :] + dr
        bim[pl.ds(8 + r, G, stride=8), :] = bim[pl.ds(8 + r, G, stride=8), :] + di


def _scan_specs(T):
    W = SCAN_W
    lane = pl.BlockSpec((T, W), lambda j: (0, j))
    col = pl.BlockSpec((T, 128), lambda j: (0, j // SCAN_PER))
    wb = pl.BlockSpec((None, 128, W), lambda j: (j, 0, 0))
    wc = pl.BlockSpec((None, W, 128), lambda j: (j, 0, 0))
    vec = pl.BlockSpec((1, W), lambda j: (0, j))
    return lane, col, wb, wc, vec


def _s5_scan_fwd(su_b, wb_re, wb_im, a_re, a_im, comm=()):
    T = su_b.shape[0]
    W = SCAN_W

    def body(su_ref, wbr_ref, wbi_ref, ar_ref, ai_ref, sr_ref, si_ref, bre, bim):
        su = su_ref[...]
        bre[pl.ds(8, T), :] = _nn(su, wbr_ref[...])
        bim[pl.ds(8, T), :] = _nn(su, wbi_ref[...])
        _scan_rows(bre, bim, ar_ref[...], ai_ref[...], T, False,
                   lambda t0: (bre[pl.ds(t0 + 8, SCAN_R), :], bim[pl.ds(t0 + 8, SCAN_R), :]))
        sr_ref[...] = bre[pl.ds(8, T), :].astype(BF16)
        si_ref[...] = bim[pl.ds(8, T), :].astype(BF16)

    lane, col, wb, wc, vec = _scan_specs(T)
    return _pcall(
        body, name="s5_scan_fwd", grid=(LANES // W,),
        in_specs=[col, wb, wb, vec, vec],
        out_specs=[lane, lane],
        out_shape=[jax.ShapeDtypeStruct((T, LANES), BF16)] * 2,
        scratch=[pltpu.VMEM((T + 16, W), F32)] * 2, comm=comm,
        operands=(su_b, wb_re, wb_im, a_re, a_im))


def _gelu(s):
    th = jnp.tanh(GELU_C * (s + 0.044715 * s * s * s))
    return 0.5 * s * (1.0 + th), th


def _mix_fwd_b(st_re, st_im, wc_re4, wc_im4, su, dvec, w_glu4, g_conv, g_ssm, y_conv, w_mo, x1, g, b, tm, comm=()):
    T = su.shape[0]

    def body(sr_ref, si_ref, wcr_ref, wci_ref, su_ref, d_ref, wg_ref, gc_ref, gs_ref, yc_ref, wmo_ref,
             x_ref, g_ref, b_ref, s_ref, sgb_ref, ga_ref, gb_ref, mb_ref, r_ref, xo_ref):
        srb = sr_ref[...]
        sib = si_ref[...]
        ys = [_nn(srb[:, 512 * J:512 * (J + 1)], wcr_ref[J]) + _nn(sib[:, 512 * J:512 * (J + 1)], wci_ref[J])
              for J in range(4)]
        s = jnp.concatenate(ys, axis=1) + d_ref[...] * su_ref[...]
        sg, _ = _gelu(s)
        sgb = sg.astype(BF16)
        ga = jnp.concatenate([_nn(sgb, wg_ref[0]), _nn(sgb, wg_ref[1])], axis=1)
        gb = jnp.concatenate([_nn(sgb, wg_ref[2]), _nn(sgb, wg_ref[3])], axis=1)
        merged = (_sig(gc_ref[...].astype(F32)) * yc_ref[...].astype(F32)
                  + _sig(gs_ref[...].astype(F32)) * (ga * _sig(gb)))
        mb = merged.astype(BF16)
        r = ALPHA * x_ref[...] + _nn(mb, wmo_ref[...])
        xhat, _ = _ln_stats(r)
        xo = xhat * g_ref[...] + b_ref[...]
        s_ref[...] = s
        sgb_ref[...] = sgb
        ga_ref[...] = ga.astype(BF16)
        gb_ref[...] = gb.astype(BF16)
        mb_ref[...] = mb
        r_ref[...] = r
        xo_ref[...] = xo

    def tok(n):
        return pl.BlockSpec((tm, n), lambda i: (i, 0))

    def full(shape):
        return pl.BlockSpec(shape, lambda i: (0,) * len(shape))

    return _pcall(
        body, name="mix_fwd_b", grid=(T // tm,),
        in_specs=[tok(LANES), tok(LANES), full((4, 512, 128)), full((4, 512, 128)), tok(SSM), full((1, SSM)),
                  full((4, SSM, 512)), tok(D), tok(D), tok(D), full((D, D)), tok(D), full((1, D)), full((1, D))],
        out_specs=[tok(SSM), tok(SSM), tok(D), tok(D), tok(D), tok(D), tok(D)],
        out_shape=[jax.ShapeDtypeStruct((T, SSM), F32), jax.ShapeDtypeStruct((T, SSM), BF16),
                   jax.ShapeDtypeStruct((T, D), BF16), jax.ShapeDtypeStruct((T, D), BF16),
                   jax.ShapeDtypeStruct((T, D), BF16), jax.ShapeDtypeStruct((T, D), F32),
                   jax.ShapeDtypeStruct((T, D), F32)],
        vmem_mb=56, comm=comm,
        operands=(st_re, st_im, wc_re4, wc_im4, su, dvec, w_glu4, g_conv, g_ssm, y_conv, w_mo, x1, g, b))


def _ple_loss(x3, x3b, p, w_pi4, w_pg, g, b, target, tm):
    T = x3.shape[0]
    PD = p.shape[1]

    def body(x_ref, xb_ref, p_ref, wpi_ref, wpg_ref, g_ref, b_ref, t_ref,
             loss_ref, dx_ref, pb_ref, dpw_ref, dgt_ref, dg_ref, db_ref):
        i = pl.program_id(0)
        pb = p_ref[...].astype(BF16)
        pw = jnp.concatenate([_nn(pb, wpi_ref[k]) for k in range(4)], axis=1)
        gt = _nn(xb_ref[...], wpg_ref[...])
        sg = _sig(gt)
        r = ALPHA * x_ref[...] + pw * sg
        gv = g_ref[...]
        xhat, rstd = _ln_stats(r)
        err = xhat * gv + b_ref[...] - t_ref[...]
        lpart = jnp.zeros((1, 128), F32) + 0.5 * jnp.sum(jnp.mean(err * err, axis=-1, keepdims=True))
        dy = err * (1.0 / D)
        dyg = dy * gv
        m1 = jnp.mean(dyg, axis=-1, keepdims=True)
        m2 = jnp.mean(dyg * xhat, axis=-1, keepdims=True)
        dr = rstd * (dyg - m1 - xhat * m2)
        pg, pbias = _rowsum(dy * xhat), _rowsum(dy)

        @pl.when(i == 0)
        def _():
            loss_ref[...] = lpart
            dg_ref[...] = pg
            db_ref[...] = pbias

        @pl.when(i > 0)
        def _():
            loss_ref[...] += lpart
            dg_ref[...] += pg
            db_ref[...] += pbias

        dgt = (dr * pw * sg * (1.0 - sg)).astype(BF16)
        pb_ref[...] = pb
        dpw_ref[...] = (dr * sg).astype(BF16)
        dgt_ref[...] = dgt
        dx_ref[...] = ALPHA * dr + _nt(dgt, wpg_ref[...])

    def tok(n):
        return pl.BlockSpec((tm, n), lambda i: (i, 0))

    def full(shape):
        return pl.BlockSpec(shape, lambda i: (0,) * len(shape))

    return pl.pallas_call(
        body, name="ple_loss", grid=(T // tm,),
        in_specs=[tok(D), tok(D), tok(PD), full((4, PD, 256)), full((D, D)), full((1, D)), full((1, D)), tok(D)],
        out_specs=[full((1, 128)), tok(D), tok(PD), tok(D), tok(D), full((1, D)), full((1, D))],
        out_shape=_hbm_out([jax.ShapeDtypeStruct((1, 128), F32), jax.ShapeDtypeStruct((T, D), F32),
                            jax.ShapeDtypeStruct((T, PD), BF16), jax.ShapeDtypeStruct((T, D), BF16),
                            jax.ShapeDtypeStruct((T, D), BF16), jax.ShapeDtypeStruct((1, D), F32),
                            jax.ShapeDtypeStruct((1, D), F32)]),
        compiler_params=_cp(48, 1),
    )(*_hbm(x3, x3b, p, w_pi4, w_pg, g, b, target))


def _mix_bwd_b(dy, r2, g, w_mo, g_conv, g_ssm, y_conv, ga, gb, s, su, dvec, w_glu4, wc_re4, wc_im4, tm, comm=()):
    T = dy.shape[0]

    def body(dy_ref, r_ref, g_ref, wmo_ref, gc_ref, gs_ref, yc_ref, ga_ref, gb_ref, s_ref, su_ref, d_ref,
             wg_ref, wcr_ref, wci_ref,
             dres_ref, dmix_ref, dgl_ref, dsb_ref, dud_ref, gsr_ref, gsi_ref, dyc_ref, dp_ref,
             dg_ref, db_ref, dd_ref):
        i = pl.program_id(0)
        dyv = dy_ref[...]
        dr, xhat = _ln_bwd(dyv, r_ref[...], g_ref[...])
        dmix = dr.astype(BF16)
        dmerged = _nt(dmix, wmo_ref[...])
        sc, ss, sgb = (_sig(gc_ref[...].astype(F32)), _sig(gs_ref[...].astype(F32)),
                       _sig(gb_ref[...].astype(F32)))
        gav = ga_ref[...].astype(F32)
        yssm = gav * sgb
        dgc = dmerged * yc_ref[...].astype(F32) * sc * (1.0 - sc)
        dgss = dmerged * yssm * ss * (1.0 - ss)
        dyssm = dmerged * ss
        dgl = jnp.concatenate([dyssm * sgb, dyssm * gav * sgb * (1.0 - sgb)], axis=1).astype(BF16)
        dsg = (_nt(dgl[:, 0:512], wg_ref[0]) + _nt(dgl[:, 512:1024], wg_ref[1])
               + _nt(dgl[:, 1024:1536], wg_ref[2]) + _nt(dgl[:, 1536:2048], wg_ref[3]))
        sv = s_ref[...]
        _, th = _gelu(sv)
        dgelu = 0.5 * (1.0 + th) + 0.5 * sv * (1.0 - th * th) * GELU_C * (1.0 + 3.0 * 0.044715 * sv * sv)
        ds = dsg * dgelu
        dsb = ds.astype(BF16)
        pg, pb, pd = _rowsum(dyv * xhat), _rowsum(dyv), _rowsum(ds * su_ref[...])

        @pl.when(i == 0)
        def _():
            dg_ref[...] = pg
            db_ref[...] = pb
            dd_ref[...] = pd

        @pl.when(i > 0)
        def _():
            dg_ref[...] += pg
            db_ref[...] += pb
            dd_ref[...] += pd

        dres_ref[...] = ALPHA * dr
        dmix_ref[...] = dmix
        dgl_ref[...] = dgl
        dsb_ref[...] = dsb
        dud_ref[...] = ds * d_ref[...]
        for J in range(4):
            gsr_ref[:, 512 * J:512 * (J + 1)] = _nt(dsb[:, 128 * J:128 * (J + 1)], wcr_ref[J]).astype(BF16)
            gsi_ref[:, 512 * J:512 * (J + 1)] = _nt(dsb[:, 128 * J:128 * (J + 1)], wci_ref[J]).astype(BF16)
        dyc_ref[...] = (dmerged * sc).astype(BF16)
        dp_ref[:, 0:D] = dgc.astype(BF16)
        dp_ref[:, D:2 * D] = dgss.astype(BF16)

    def tok(n):
        return pl.BlockSpec((tm, n), lambda i: (i, 0))

    def full(shape):
        return pl.BlockSpec(shape, lambda i: (0,) * len(shape))

    return _pcall(
        body, name="mix_bwd_b", grid=(T // tm,),
        in_specs=[tok(D), tok(D), full((1, D)), full((D, D)), tok(D), tok(D), tok(D), tok(D), tok(D),
                  tok(SSM), tok(SSM), full((1, SSM)), full((4, SSM, 512)), full((4, 512, 128)), full((4, 512, 128))],
        out_specs=[tok(D), tok(D), tok(2 * D), tok(SSM), tok(SSM), tok(LANES), tok(LANES), tok(D),
                   pl.BlockSpec((tm, 2 * D), lambda i: (i, 1)), full((1, D)), full((1, D)), full((1, SSM))],
        out_shape=[jax.ShapeDtypeStruct((T, D), F32), jax.ShapeDtypeStruct((T, D), BF16),
                   jax.ShapeDtypeStruct((T, 2 * D), BF16), jax.ShapeDtypeStruct((T, SSM), BF16),
                   jax.ShapeDtypeStruct((T, SSM), F32), jax.ShapeDtypeStruct((T, LANES), BF16),
                   jax.ShapeDtypeStruct((T, LANES), BF16), jax.ShapeDtypeStruct((T, D), BF16),
                   jax.ShapeDtypeStruct((T, 4 * D), BF16), jax.ShapeDtypeStruct((1, D), F32),
                   jax.ShapeDtypeStruct((1, D), F32), jax.ShapeDtypeStruct((1, SSM), F32)],
        vmem_mb=56, comm=comm,
        operands=(dy, r2, g, w_mo, g_conv, g_ssm, y_conv, ga, gb, s, su, dvec, w_glu4, wc_re4, wc_im4))


def _s5_scan_bwd(gs_re, gs_im, st_re, st_im, su_b, ds_b, wb_re, wb_im, a_re, a_im, comm=()):
    T = su_b.shape[0]
    W = SCAN_W
    R = SCAN_R

    def body(gr_ref, gi_ref, sr_ref, si_ref, su_ref, ds_ref, wbr_ref, wbi_ref, ar_ref, ai_ref,
             dsu_ref, dwbr_ref, dwbi_ref, dwcr_ref, dwci_ref, dar_ref, dai_ref, gre, gim):
        j = pl.program_id(0)
        zero = jnp.zeros((8, W), F32)
        for buf in (gre, gim):
            buf[pl.ds(T + 8, 8), :] = zero
        _scan_rows(gre, gim, ar_ref[...], ai_ref[...], T, True,
                   lambda t0: (gr_ref[pl.ds(t0, R), :].astype(F32), gi_ref[pl.ds(t0, R), :].astype(F32)))
        grb = gre[pl.ds(8, T), :].astype(BF16)
        gib = gim[pl.ds(8, T), :].astype(BF16)
        part = _nt(grb, wbr_ref[...]) + _nt(gib, wbi_ref[...])

        @pl.when(j % SCAN_PER == 0)
        def _():
            dsu_ref[...] = part

        @pl.when(j % SCAN_PER > 0)
        def _():
            dsu_ref[...] += part

        su = su_ref[...]
        dwbr_ref[...] = _tn(su, grb)
        dwbi_ref[...] = _tn(su, gib)
        dsv = ds_ref[...]
        dwcr_ref[...] = _tn(sr_ref[...], dsv)
        dwci_ref[...] = _tn(si_ref[...], dsv)
        dar = jnp.zeros((1, W), F32)
        dai = jnp.zeros((1, W), F32)
        for c in range(T // R):
            xr = sr_ref[pl.ds(c * R, R), :].astype(F32)
            xi = si_ref[pl.ds(c * R, R), :].astype(F32)
            g1r = gre[pl.ds(c * R + 9, R), :]
            g1i = gim[pl.ds(c * R + 9, R), :]
            dar = dar + _rowsum(g1r * xr + g1i * xi)
            dai = dai + _rowsum(g1i * xr - g1r * xi)
        dar_ref[...] = dar
        dai_ref[...] = dai

    lane, col, wb, wc, vec = _scan_specs(T)
    return _pcall(
        body, name="s5_scan_bwd", grid=(LANES // W,),
        in_specs=[lane, lane, lane, lane, col, col, wb, wb, vec, vec],
        out_specs=[col, wb, wb, wc, wc, vec, vec],
        out_shape=[jax.ShapeDtypeStruct((T, SSM), F32),
                   jax.ShapeDtypeStruct((LANES // W, 128, W), F32), jax.ShapeDtypeStruct((LANES // W, 128, W), F32),
                   jax.ShapeDtypeStruct((LANES // W, W, 128), F32), jax.ShapeDtypeStruct((LANES // W, W, 128), F32),
                   jax.ShapeDtypeStruct((1, LANES), F32), jax.ShapeDtypeStruct((1, LANES), F32)],
        scratch=[pltpu.VMEM((T + 16, W), F32)] * 2, vmem_mb=56, comm=comm,
        operands=(gs_re, gs_im, st_re, st_im, su_b, ds_b, wb_re, wb_im, a_re, a_im))


def _mix_bwd_a(dyc_b, w_co4, pc, z_b, conv_w, dsu_ssm, du_dir, dproj, dres, w_mix4, tm, comm=()):
    T = dres.shape[0]
    nt = T // tm

    def body(dyc_ref, wco_ref, pc_ref, halo_ref, z_ref, cw_ref, dsu_ref, dud_ref, dpin_ref, dres_ref, w_ref,
             dp_ref, dx_ref, dcw_ref, dcb_ref, dzbuf, qbuf):
        i = pl.program_id(0)
        ii = nt - 1 - i

        @pl.when(i == 0)
        def _():
            dzbuf[pl.ds(tm, 8), :] = jnp.zeros((8, CONV), F32)

        dyc = dyc_ref[...]
        dyin = (_nt(dyc[:, 0:256], wco_ref[0]) + _nt(dyc[:, 256:512], wco_ref[1])
                + _nt(dyc[:, 512:768], wco_ref[2]) + _nt(dyc[:, 768:1024], wco_ref[3]))
        cbv = pc_ref[:, 0:CONV].astype(F32)
        ccv = pc_ref[:, CONV:2 * CONV].astype(F32)
        chv = pc_ref[:, 2 * CONV:3 * CONV].astype(F32)
        dcbv = dyin * z_ref[...].astype(F32)
        dz = dyin * cbv
        dzbuf[pl.ds(0, tm), :] = dz
        cw = cw_ref[...]
        dq = cw[2:3] * dz + cw[1:2] * dzbuf[pl.ds(1, tm), :] + cw[0:1] * dzbuf[pl.ds(2, tm), :]
        dzbuf[pl.ds(tm, 8), :] = dz[0:8]
        q = ccv * chv
        hq = halo_ref[:, CONV:2 * CONV].astype(F32) * halo_ref[:, 2 * CONV:3 * CONV].astype(F32)
        qbuf[pl.ds(0, 8), :] = jnp.where(ii > 0, hq, jnp.zeros_like(hq))
        qbuf[pl.ds(8, tm), :] = q
        pw = jnp.concatenate([_rowsum(dz * qbuf[pl.ds(6, tm), :]), _rowsum(dz * qbuf[pl.ds(7, tm), :]),
                              _rowsum(dz * q), jnp.zeros((5, CONV), F32)], axis=0)
        pbias = _rowsum(dz)

        @pl.when(i == 0)
        def _():
            dcw_ref[...] = pw
            dcb_ref[...] = pbias

        @pl.when(i > 0)
        def _():
            dcw_ref[...] += pw
            dcb_ref[...] += pbias

        dp0 = jnp.concatenate([dcbv, dq * chv], axis=1).astype(BF16)
        dp1 = jnp.concatenate([dq * ccv, dsu_ref[...] + dud_ref[...]], axis=1).astype(BF16)
        dp_ref[:, 0:D] = dp0
        dp_ref[:, D:2 * D] = dp1
        dx_ref[...] = (dres_ref[...] + _nt(dp0, w_ref[0]) + _nt(dp1, w_ref[1])
                       + _nt(dpin_ref[:, 0:D], w_ref[2]) + _nt(dpin_ref[:, D:2 * D], w_ref[3]))

    def tok(n):
        return pl.BlockSpec((tm, n), lambda i: (nt - 1 - i, 0))

    def full(shape):
        return pl.BlockSpec(shape, lambda i: (0,) * len(shape))

    halo = pl.BlockSpec((8, 3 * CONV), lambda i: (jnp.maximum((nt - 1 - i) * (tm // 8) - 1, 0), 0))
    return _pcall(
        body, name="mix_bwd_a", grid=(nt,),
        in_specs=[tok(D), full((4, CONV, 256)), tok(3 * CONV), halo, tok(CONV), full((3, CONV)),
                  tok(SSM), tok(SSM), pl.BlockSpec((tm, 2 * D), lambda i: (nt - 1 - i, 1)), tok(D),
                  full((4, D, D))],
        out_specs=[pl.BlockSpec((tm, 2 * D), lambda i: (nt - 1 - i, 0)), tok(D), full((8, CONV)), full((1, CONV))],
        out_shape=[jax.ShapeDtypeStruct((T, 4 * D), BF16), jax.ShapeDtypeStruct((T, D), F32),
                   jax.ShapeDtypeStruct((8, CONV), F32), jax.ShapeDtypeStruct((1, CONV), F32)],
        scratch=[pltpu.VMEM((tm + 8, CONV), F32), pltpu.VMEM((tm + 8, CONV), F32)],
        aliases={8: 0}, vmem_mb=56, comm=comm,
        operands=(dyc_b, w_co4, pc, pc, z_b, conv_w, dsu_ssm, du_dir, dproj, dres, w_mix4))


def _zoh(lam_re, lam_im, log_step, b_re, b_im):
    dt = jnp.exp(log_step)[:, None]
    mag = jnp.exp(lam_re * dt)
    abr, abi = mag * jnp.cos(lam_im * dt), mag * jnp.sin(lam_im * dt)
    nr, ni = abr - 1.0, abi
    den = lam_re * lam_re + lam_im * lam_im
    cr = (nr * lam_re + ni * lam_im) / den
    ci = (ni * lam_re - nr * lam_im) / den
    bbr = cr[..., None] * b_re - ci[..., None] * b_im
    bbi = cr[..., None] * b_im + ci[..., None] * b_re
    return abr, abi, bbr, bbi


_WB_MASK = (np.arange(8)[None, :, None]
            == SCAN_GR * np.arange(SCAN_PER)[:, None, None] + np.arange(SCAN_GR)[None, None, :]).astype(np.float32)
_EYE8 = np.eye(8, dtype=np.float32)


def _wb_blocks(bb):
    bt = bb.transpose(0, 2, 1).reshape(4, 1, 8, 16, 1, STATE)
    full = bt * _WB_MASK[None, :, :, None, :, None]
    return full.reshape(LANES // SCAN_W, 128, SCAN_W).astype(BF16)


def _wc_blocks(cc):
    ct = cc.transpose(0, 2, 1).reshape(4, 8, STATE, 1, 16)
    full = ct * _EYE8[None, :, None, :, None]
    return full.reshape(4, 512, 128).astype(BF16)


def _wb_diag(dwb):
    d6 = dwb.reshape(4, SCAN_PER, 8, 16, SCAN_GR, STATE) * _WB_MASK[None, :, :, None, :, None]
    return d6.sum(axis=(1, 4)).reshape(GROUPS, 16, STATE).transpose(0, 2, 1)


def _wc_diag(dwc):
    mask = _WB_MASK.transpose(0, 2, 1)
    d6 = dwc.reshape(4, SCAN_PER, SCAN_GR, STATE, 8, 16) * mask[None, :, :, None, :, None]
    return d6.sum(axis=4).reshape(GROUPS, STATE, 16).transpose(0, 2, 1)


def _where():
    x, y, c = lax.axis_index("x"), lax.axis_index("y"), lax.axis_index("c")
    return x, y, c, 2 * x + y


def _chip_dev(k, c):
    return (k // 2, k % 2, c)


def _slot_cast(meidx, w, dtype, name, token=()):
    R, C = w.shape
    tr = _row_tile(R)

    def body(m_ref, w_ref, *rest):
        rest[-1][...] = w_ref[...].astype(dtype)

    gs = pltpu.PrefetchScalarGridSpec(
        num_scalar_prefetch=1, grid=(R // tr,),
        in_specs=[pl.BlockSpec((tr, C), lambda i, m: (i, 0))] + [pl.BlockSpec((8, 128), lambda i, m: (0, 0))] * len(token),
        out_specs=pl.BlockSpec((None, tr, C), lambda i, m: (m[0], i, 0)))
    return pl.pallas_call(
        body, name=name, grid_spec=gs, out_shape=_hbm_out(jax.ShapeDtypeStruct((4, R, C), dtype)),
        compiler_params=_cp(32, 1),
    )(meidx, *_hbm(w), *token)


def _gather_ici_payload(bufs):
    def copies(ins, lnd, ss, rs):
        x, y, c, me = _where()
        cps = []
        for w, b in enumerate(bufs):
            h = b.shape[1] // 2
            mine = lnd[w].at[me, pl.ds(c * h, h)]
            for s in range(3):
                k = (me + 1 + s) % 4
                cps.append(pltpu.make_async_remote_copy(
                    src_ref=mine, dst_ref=mine, send_sem=ss.at[3 * w + s], recv_sem=rs.at[3 * w + s],
                    device_id=_chip_dev(k, c), device_id_type=MESH))
        return cps

    p = _sym_payload([], [jax.ShapeDtypeStruct(b.shape, b.dtype) for b in bufs], copies, 3 * len(bufs))
    p.lands = list(bufs)
    return p


def _gather_pass_payload(bufs):
    def copies(ins, outs, ss, rs):
        x, y, c, me = _where()
        cps = []
        for w, b in enumerate(bufs):
            h = b.shape[1] // 2
            for s in range(3):
                j = (me + 1 + s) % 4
                cps.append(pltpu.make_async_remote_copy(
                    src_ref=ins[w].at[j, pl.ds(c * h, h)], dst_ref=outs[w].at[j, pl.ds(c * h, h)],
                    send_sem=ss.at[3 * w + s], recv_sem=rs.at[3 * w + s], device_id=(x, y, 1 - c),
                    device_id_type=MESH))
        return cps

    p = _sym_payload(bufs, [jax.ShapeDtypeStruct(b.shape, b.dtype) for b in bufs], copies, 3 * len(bufs))
    p.aliases = {w: w for w in range(len(bufs))}
    return p


def _gather_payload(bufs):
    n = len(bufs)

    def half(ref, w, k, cc):
        h = bufs[w].shape[1] // 2
        return ref.at[k, pl.ds(cc * h, h)]

    def ici(ins, outs, sems, w, s):
        x, y, c, me = _where()
        k = (me + 1 + s) % 4
        return pltpu.make_async_remote_copy(
            src_ref=half(ins[w], w, me, c), dst_ref=half(outs[w], w, me, c), send_sem=sems[0].at[3 * w + s],
            recv_sem=sems[1].at[3 * w + s], device_id=_chip_dev(k, c), device_id_type=MESH)

    def landed(outs, sems, w, s):
        x, y, c, me = _where()
        j = (me + 3 - s) % 4
        return pltpu.make_async_remote_copy(
            src_ref=half(outs[w], w, j, c), dst_ref=half(outs[w], w, j, c), send_sem=sems[0].at[3 * w + s],
            recv_sem=sems[1].at[3 * w + s], device_id=(x, y, 1 - c), device_id_type=MESH)

    def passed(outs, sems, w, s, cc):
        x, y, c, me = _where()
        j = (me + 3 - s) % 4
        return pltpu.make_async_remote_copy(
            src_ref=half(outs[w], w, j, cc), dst_ref=half(outs[w], w, j, cc), send_sem=sems[2].at[3 * w + s],
            recv_sem=sems[3].at[3 * w + s], device_id=(x, y, 1 - c), device_id_type=MESH)

    pairs = [(w, s) for w in range(n) for s in range(3)]

    def start(ins, outs, sems):
        for w, s in pairs:
            ici(ins, outs, sems, w, s).start()

    def finish(ins, outs, sems):
        _, _, c, _ = _where()
        for w, s in pairs:
            landed(outs, sems, w, s).wait_recv()
            passed(outs, sems, w, s, c).start()
        for w, s in pairs:
            passed(outs, sems, w, s, 1 - c).wait_recv()
        for w, s in pairs:
            ici(ins, outs, sems, w, s).wait_send()
            passed(outs, sems, w, s, c).wait_send()

    return _Payload(bufs, [jax.ShapeDtypeStruct(b.shape, b.dtype) for b in bufs], {w: w for w in range(n)},
                    [pltpu.SemaphoreType.DMA((3 * n,))] * 4, start, finish)


def _sym_payload(operands, outs, copies, n_copies):
    def start(ins, outs_, sems):
        for cp in copies(ins, outs_, sems[0], sems[1]):
            cp.start()

    def finish(ins, outs_, sems):
        for cp in copies(ins, outs_, sems[0], sems[1]):
            cp.wait()

    p = _Payload(operands, outs, {}, [pltpu.SemaphoreType.DMA((n_copies,))] * 2, start, finish)
    p.copies, p.n_copies = copies, n_copies
    return p


def _swap_payload(g4s):
    def copies(ins, outs, ss, rs):
        x, y, c, me = _where()
        cps = []
        for w, g in enumerate(g4s):
            h = g.shape[1] // 2
            cps.append(pltpu.make_async_remote_copy(
                src_ref=ins[w].at[:, pl.ds((1 - c) * h, h)], dst_ref=outs[w], send_sem=ss.at[w],
                recv_sem=rs.at[w], device_id=(x, y, 1 - c), device_id_type=MESH))
        return cps

    outs = [jax.ShapeDtypeStruct((4, g.shape[1] // 2, g.shape[2]), g.dtype) for g in g4s]
    return _sym_payload(g4s, outs, copies, len(g4s))


def _exchange_payload(pbs):
    def copies(ins, outs, ss, rs):
        x, y, c, me = _where()
        cps = []
        for w in range(len(pbs)):
            for s in range(3):
                k = (me + 1 + s) % 4
                cps.append(pltpu.make_async_remote_copy(
                    src_ref=ins[w].at[k], dst_ref=outs[w].at[2 - s], send_sem=ss.at[3 * w + s],
                    recv_sem=rs.at[3 * w + s], device_id=_chip_dev(k, c), device_id_type=MESH))
        return cps

    outs = [jax.ShapeDtypeStruct((3,) + p.shape[1:], p.dtype) for p in pbs]
    return _sym_payload(pbs, outs, copies, 3 * len(pbs))


HBM_REF = pl.BlockSpec(memory_space=pltpu.HBM)
SEM_REF = pl.BlockSpec(memory_space=pltpu.SEMAPHORE)
DATAFLOW = pltpu.SideEffectType.DATAFLOW_SIDE_EFFECTING


class _SemList:
    def __init__(self, refs):
        self.refs = refs

    @property
    def at(self):
        return self.refs


def _split_start(p, name):
    n_in, n_out, nc = len(p.operands), len(p.outs), p.n_copies
    lands = getattr(p, "lands", None) or [lax.empty(s.shape, s.dtype) for s in p.outs]

    def body(*refs):
        ins, lnd = refs[:n_in], refs[n_in:n_in + n_out]
        sems = refs[n_in + n_out:n_in + n_out + 2 * nc]
        for cp in p.copies(ins, lnd, _SemList(sems[:nc]), _SemList(sems[nc:])):
            cp.start()
        refs[-1][...] = jnp.zeros((8, 128), F32)

    res = pl.pallas_call(
        body, name=name,
        in_specs=[HBM_REF] * (n_in + n_out),
        out_specs=[SEM_REF] * (2 * nc) + [HBM_REF] * (n_in + n_out) + [VMEM_FULL],
        out_shape=([pltpu.SemaphoreType.DMA(())] * (2 * nc) + _hbm_out(p.operands) + _hbm_out(lands)
                   + [jax.ShapeDtypeStruct((8, 128), F32)]),
        input_output_aliases={i: 2 * nc + i for i in range(n_in + n_out)},
        compiler_params=pltpu.CompilerParams(has_side_effects=DATAFLOW),
    )(*_hbm(*p.operands, *lands))
    k = 2 * nc
    return list(res[:k]), list(res[k:k + n_in]), list(res[k + n_in:k + n_in + n_out]), res[-1]


def _split_wait(p, handle, after, name):
    sems, srcs, lands, _ = handle
    n_in, n_out, nc = len(srcs), len(lands), p.n_copies

    def body(*refs):
        ins, lnd = refs[:n_in], refs[n_in:n_in + n_out]
        sm = refs[n_in + n_out:n_in + n_out + 2 * nc]
        for cp in p.copies(ins, lnd, _SemList(sm[:nc]), _SemList(sm[nc:])):
            cp.wait_send()
            cp.wait_recv()

    res = pl.pallas_call(
        body, name=name,
        in_specs=[HBM_REF] * (n_in + n_out) + [SEM_REF] * (2 * nc) + [ANY] * len(after),
        out_specs=[HBM_REF] * (n_in + n_out), out_shape=_hbm_out(srcs) + _hbm_out(lands),
        input_output_aliases={i: i for i in range(n_in + n_out)},
        compiler_params=pltpu.CompilerParams(has_side_effects=DATAFLOW),
    )(*srcs, *lands, *sems, *after)
    return list(res[:n_in]), list(res[n_in:])


def _join_payload(halves):
    def copies(ins, outs, ss, rs):
        x, y, c, me = _where()
        return [pltpu.make_async_remote_copy(
            src_ref=ins[w], dst_ref=outs[w], send_sem=ss.at[w], recv_sem=rs.at[w],
            device_id=(x, y, 1 - c), device_id_type=MESH) for w in range(len(halves))]

    outs = [jax.ShapeDtypeStruct(a.shape, a.dtype) for a in halves]
    return _sym_payload(halves, outs, copies, len(halves))


def _allgather_payload(v):
    def copies(ins, outs, ss, rs):
        x, y, c, me = _where()
        lin = 4 * x + 2 * y + c
        cps = []
        for o in range(1, 8):
            t = (lin + o) % 8
            cps.append(pltpu.make_async_remote_copy(
                src_ref=ins[0], dst_ref=outs[0].at[lin], send_sem=ss.at[o - 1], recv_sem=rs.at[o - 1],
                device_id=(t // 4, (t // 2) % 2, t % 2), device_id_type=MESH))
        return cps

    p = _sym_payload([v], [jax.ShapeDtypeStruct((8,) + v.shape, v.dtype)], copies, 7)
    x, y, c, _ = _where()
    p.lands = [lax.dynamic_update_slice(jnp.zeros((8,) + v.shape, v.dtype), v[None], (4 * x + 2 * y + c, 0, 0))]
    return p


def _sum8(buf, token):
    _, P, C = buf.shape

    def body(b_ref, t_ref, o_ref):
        acc = b_ref[0]
        for d in range(1, 8):
            acc = acc + b_ref[d]
        o_ref[...] = acc

    return pl.pallas_call(
        body, name="sum8", in_specs=[VMEM_FULL, VMEM_FULL], out_specs=VMEM_FULL,
        out_shape=jax.ShapeDtypeStruct((P, C), F32),
        compiler_params=pltpu.CompilerParams(vmem_limit_bytes=32 << 20),
    )(buf, token)


def _row_tile(h):
    for t in (256, 176, 128, 64, 32, 16, 8):
        if h % t == 0:
            return t
    raise ValueError(h)


def _pair_sum(cmidx, g4, got, name):
    _, R, C = g4.shape
    h = R // 2
    th = _row_tile(h)

    def body(cm_ref, a_ref, b_ref, o_ref, ob_ref):
        sm = a_ref[...] + b_ref[...]
        ob_ref[...] = sm.astype(BF16)

        @pl.when(pl.program_id(1) == cm_ref[1])
        def _():
            o_ref[...] = sm

    blk = pl.BlockSpec((None, th, C), lambda i, k, cm: (k, i, 0))
    gs = pltpu.PrefetchScalarGridSpec(
        num_scalar_prefetch=1, grid=(h // th, 4),
        in_specs=[pl.BlockSpec((None, None, th, C), lambda i, k, cm: (k, cm[0], i, 0)), blk],
        out_specs=[pl.BlockSpec((th, C), lambda i, k, cm: (i, 0)), blk])
    return pl.pallas_call(
        body, name=name, grid_spec=gs,
        out_shape=_hbm_out([jax.ShapeDtypeStruct((h, C), F32), jax.ShapeDtypeStruct((4, h, C), BF16)]),
        compiler_params=_cp(32, 2),
    )(cmidx, *_hbm(g4.reshape(4, 2, h, C), got))


def _chip_sum(own, got, name):
    h, C = own.shape
    th = _row_tile(h)

    def body(a_ref, b_ref, o_ref):
        o_ref[...] = ((a_ref[...] + b_ref[0].astype(F32)) + b_ref[1].astype(F32)) + b_ref[2].astype(F32)

    return pl.pallas_call(
        body, name=name, grid=(h // th,),
        in_specs=[pl.BlockSpec((th, C), lambda i: (i, 0)), pl.BlockSpec((3, th, C), lambda i: (0, i, 0))],
        out_specs=pl.BlockSpec((th, C), lambda i: (i, 0)),
        out_shape=_hbm_out(jax.ShapeDtypeStruct((h, C), F32)),
        compiler_params=_cp(32, 1),
    )(*_hbm(own, got))


def _adamw_math(w, g, m, v):
    m2 = B1 * m + (1.0 - B1) * g
    v2 = B2 * v + (1.0 - B2) * (g * g)
    m_hat = m2 / (1.0 - B1 ** STEP)
    v_hat = v2 / (1.0 - B2 ** STEP)
    delta = -LR * (m_hat / (jnp.sqrt(v_hat) + EPS) + WD * w)
    return delta, m2, v2


def _adamw_pair(cidx, w, mine, theirs, m, v, token, name):
    R, C = w.shape
    h = R // 2
    tr = _row_tile(h)
    nh = h // tr

    def body(c_ref, w_ref, a_ref, b_ref, m_ref, v_ref, t_ref, g_ref, d_ref, mo_ref, vo_ref):
        own = (pl.program_id(0) // nh) == c_ref[0]
        g = jnp.where(own, a_ref[...], b_ref[...])
        d, m2, v2 = _adamw_math(w_ref[...], g, m_ref[...], v_ref[...])
        g_ref[...] = g
        d_ref[...] = d
        mo_ref[...] = m2
        vo_ref[...] = v2

    blk = pl.BlockSpec((tr, C), lambda i, c: (i, 0))
    mine_blk = pl.BlockSpec((tr, C), lambda i, c: (jnp.clip(i - c[0] * nh, 0, nh - 1), 0))
    theirs_blk = pl.BlockSpec((tr, C), lambda i, c: (jnp.clip(i - (1 - c[0]) * nh, 0, nh - 1), 0))
    gs = pltpu.PrefetchScalarGridSpec(
        num_scalar_prefetch=1, grid=(R // tr,),
        in_specs=[blk, mine_blk, theirs_blk, blk, blk, pl.BlockSpec((8, 128), lambda i, c: (0, 0))],
        out_specs=[blk] * 4)
    return pl.pallas_call(
        body, name=name, grid_spec=gs, out_shape=_hbm_out([jax.ShapeDtypeStruct((R, C), F32)] * 4),
        compiler_params=_cp(32, 1),
    )(cidx, *_hbm(w, mine, theirs, m, v), token)


SC_TILES = 32
SC_ROWS = 8


def _adamw_pair_sc(w, mine, theirs, m, v, name):
    R, C = w.shape
    h = R // 2
    pieces = R // SC_ROWS

    def body(w_hbm, a_hbm, b_hbm, m_hbm, v_hbm, g_out, d_out, m_out, v_out, wb, gb, mb, vb):
        tile = lax.axis_index("sc_sub") * 2 + lax.axis_index("sc_core")
        c = lax.axis_index("c")

        @pl.loop(0, -(-pieces // SC_TILES))
        def _(k):
            piece = k * SC_TILES + tile

            @pl.when(piece < pieces)
            def _():
                row0 = piece * SC_ROWS
                rows = pl.ds(row0, SC_ROWS)
                half = jnp.where(row0 >= h, 1, 0)
                local = pl.ds(row0 - half * h, SC_ROWS)
                pltpu.sync_copy(w_hbm.at[rows], wb)
                pltpu.sync_copy(m_hbm.at[rows], mb)
                pltpu.sync_copy(v_hbm.at[rows], vb)

                @pl.when(half == c)
                def _():
                    pltpu.sync_copy(a_hbm.at[local], gb)

                @pl.when(half != c)
                def _():
                    pltpu.sync_copy(b_hbm.at[local], gb)

                @pl.loop(0, SC_ROWS)
                def _(i):
                    @pl.loop(0, C, step=16)
                    def _(j):
                        s = (i, pl.ds(j, 16))
                        d, m2, v2 = _adamw_math(wb[s], gb[s], mb[s], vb[s])
                        wb[s] = d
                        mb[s] = m2
                        vb[s] = v2

                pltpu.sync_copy(gb, g_out.at[rows])
                pltpu.sync_copy(wb, d_out.at[rows])
                pltpu.sync_copy(mb, m_out.at[rows])
                pltpu.sync_copy(vb, v_out.at[rows])

    return pl.kernel(
        body, name=name, out_type=[jax.ShapeDtypeStruct((R, C), F32)] * 4,
        mesh=plsc.VectorSubcoreMesh(core_axis_name="sc_core", subcore_axis_name="sc_sub"),
        scratch_types=[pltpu.VMEM((SC_ROWS, C), F32)] * 4,
    )(w, mine, theirs, m, v)


def _adamw(w, g, m, v, name):
    R, C = w.shape
    tr = _row_tile(R)

    def body(w_ref, g_ref, m_ref, v_ref, d_ref, mo_ref, vo_ref):
        d, m2, v2 = _adamw_math(w_ref[...], g_ref[...], m_ref[...], v_ref[...])
        d_ref[...] = d
        mo_ref[...] = m2
        vo_ref[...] = v2

    blk = pl.BlockSpec((tr, C), lambda i: (i, 0))
    return pl.pallas_call(
        body, name=name, grid=(R // tr,), in_specs=[blk] * 4, out_specs=[blk] * 3,
        out_shape=_hbm_out([jax.ShapeDtypeStruct((R, C), F32)] * 3),
        compiler_params=_cp(32, 1),
    )(*_hbm(w, g, m, v))


def _pack(arrs):
    flat = jnp.concatenate([a.reshape(-1).astype(F32) for a in arrs])
    rows = -(-flat.shape[0] // 1024)
    rows = -(-rows // 8) * 8
    return jnp.pad(flat, (0, rows * 1024 - flat.shape[0])).reshape(rows, 1024)


def _unpack(packed, shapes):
    flat = packed.reshape(-1)
    out, off = [], 0
    for s in shapes:
        n = math.prod(s)
        out.append(flat[off:off + n].reshape(s))
        off += n
    return out


BIG = ["ffn1_w_in", "ffn1_w_out", "mix_w_in", "conv_w_out", "ssm_w_glu", "mix_w_out",
       "ffn2_w_in", "ffn2_w_out", "ple_w_in", "ple_w_gate"]
SMALL = ["ln1_g", "ln1_b", "conv_w", "conv_b", "ssm_lam_re", "ssm_lam_im", "ssm_log_step", "ssm_b_re", "ssm_b_im",
         "ssm_c_re", "ssm_c_im", "ssm_d", "ln2_g", "ln2_b", "ln3_g", "ln3_b", "ln4_g", "ln4_b"]
WEIGHTS = ["ffn1_w_in", "ffn1_w_out", "ln1_g", "ln1_b", "mix_w_in", "conv_w", "conv_b", "conv_w_out",
           "ssm_lam_re", "ssm_lam_im", "ssm_log_step", "ssm_b_re", "ssm_b_im", "ssm_c_re", "ssm_c_im", "ssm_d",
           "ssm_w_glu", "mix_w_out", "ln2_g", "ln2_b", "ffn2_w_in", "ffn2_w_out", "ln3_g", "ln3_b",
           "ple_w_in", "ple_w_gate", "ln4_g", "ln4_b"]


def _s5_operands(sp):
    abr, abi, bbr, bbi = _zoh(sp["ssm_lam_re"], sp["ssm_lam_im"], sp["ssm_log_step"], sp["ssm_b_re"], sp["ssm_b_im"])
    return (_wb_blocks(bbr), _wb_blocks(bbi), _wc_blocks(sp["ssm_c_re"]), _wc_blocks(-sp["ssm_c_im"]),
            abr.reshape(1, LANES), abi.reshape(1, LANES), sp["ssm_d"].reshape(1, SSM))


def _local_step(x, p, target, sp, ops, sched):
    W = sched.W
    wb_re, wb_im, wc_re4, wc_im4, a_re, a_im, dvec = ops
    tm = TOKEN_TILE

    def run(fn, name, *args, **kw):
        outs, got = fn(*args, comm=sched.carry(name), **kw)
        sched.landed(name, got)
        sched.done[name] = outs[0]
        return outs

    def dw(name, wname, a, b, tk, tn, shape4, shard_cols=None, interleaved=False):
        out, got = _mm_tn(a, b, tk, tn, name, shard_cols=shard_cols, interleaved=interleaved,
                          comm=sched.carry(name))
        sched.landed(name, got)
        sched.done[name] = out
        sched.grad(wname, out.reshape(shape4))

    h1, r1, x1, x1b, xb = run(_ffn_fwd, "ffn1_fwd", x, W["ffn1_w_in"], W["ffn1_w_out"].reshape(2, FFH, D),
                              sp["ln1_g"], sp["ln1_b"], tm, "ffn1_fwd")
    conv_w = W["conv_w"][:, 0:3, :].transpose(1, 0, 2).reshape(3, CONV)
    pc, z_b, yin_b, su, su_b, g_conv, g_ssm, y_conv = run(
        _mix_fwd_a, "mix_fwd_a", x1b, W["mix_w_in"], conv_w, sp["conv_b"], W["conv_w_out"], tm)
    st_re, st_im = run(_s5_scan_fwd, "s5_scan_fwd", su_b, wb_re, wb_im, a_re, a_im)
    w_mo = W["mix_w_out"].reshape(D, D)
    s, sg_b, ga, gb, merged_b, r2, x2 = run(
        _mix_fwd_b, "mix_fwd_b", st_re, st_im, wc_re4, wc_im4, su, dvec, W["ssm_w_glu"], g_conv, g_ssm, y_conv,
        w_mo, x1, sp["ln2_g"], sp["ln2_b"], tm)
    w2o2 = W["ffn2_w_out"].reshape(2, FFH, D)
    h2, r3, x3, x3b, x2b = run(_ffn_fwd, "ffn2_fwd", x2, W["ffn2_w_in"], w2o2, sp["ln3_g"], sp["ln3_b"], tm,
                               "ffn2_fwd")
    loss_part, dx3, p_b, dpw_b, dgt_b, dg4, db4 = _ple_loss(
        x3, x3b, p, W["ple_w_in"], W["ple_w_gate"].reshape(D, D), sp["ln4_g"], sp["ln4_b"], target, tm)

    dw("dw_ple_gate", "ple_w_gate", x3b, dgt_b, 512, 1024, (4, 256, D))
    dw("dw_ple_in", "ple_w_in", p_b, dpw_b, 256, 256, (4, 256, 256), shard_cols=256)
    dx2, dh2, a2_b, df2_b, dg3, db3 = run(_ffn_bwd, "ffn2_bwd", dx3, r3, sp["ln3_g"], h2, W["ffn2_w_in"], w2o2,
                                          tm, "ffn2_bwd")
    dw("dw_ffn2_in", "ffn2_w_in", x2b, dh2, 512, FFH, (4, D, FFH), shard_cols=FFH, interleaved=True)
    dw("dw_ffn2_out", "ffn2_w_out", a2_b, df2_b, FFH, 1024, (4, FF // 4, D))
    (dres, dmix_b, dgl_b, ds_b, du_dir, gs_re, gs_im, dyc_b, dproj, dg2, db2, dd) = run(
        _mix_bwd_b, "mix_bwd_b", dx2, r2, sp["ln2_g"], w_mo, g_conv, g_ssm, y_conv, ga, gb, s, su, dvec,
        W["ssm_w_glu"], wc_re4, wc_im4, tm)
    dw("dw_mix_out", "mix_w_out", merged_b, dmix_b, 512, 1024, (4, 256, D))
    dw("dw_glu", "ssm_w_glu", sg_b, dgl_b, 512, 512, (4, SSM, 512), shard_cols=512)
    dsu_ssm, dwb_re, dwb_im, dwc_re, dwc_im, da_re, da_im = run(
        _s5_scan_bwd, "s5_scan_bwd", gs_re, gs_im, st_re, st_im, su_b, ds_b, wb_re, wb_im, a_re, a_im)
    dw("dw_conv_out", "conv_w_out", yin_b, dyc_b, 512, 256, (4, CONV, 256), shard_cols=256)
    dproj, dx1, dcw8, dcb = run(_mix_bwd_a, "mix_bwd_a", dyc_b, W["conv_w_out"], pc, z_b, conv_w, dsu_ssm,
                                du_dir, dproj, dres, W["mix_w_in"], tm)
    dw("dw_mix_in", "mix_w_in", x1b, dproj, 512, 1024, (4, D, D), shard_cols=1024)
    dx0, dh1, a1_b, df1_b, dg1, db1 = run(_ffn_bwd, "ffn1_bwd", dx1, r1, sp["ln1_g"], h1, W["ffn1_w_in"],
                                          W["ffn1_w_out"].reshape(2, FFH, D), tm, "ffn1_bwd")
    sched.small(dict(
        ln1_g=dg1, ln1_b=db1, ln2_g=dg2, ln2_b=db2, ln3_g=dg3, ln3_b=db3, ln4_g=dg4, ln4_b=db4,
        conv_w=dcw8[0:3], conv_b=dcb,
        a_re=da_re.reshape(GROUPS, STATE), a_im=da_im.reshape(GROUPS, STATE),
        bb_re=_wb_diag(dwb_re), bb_im=_wb_diag(dwb_im),
        ssm_c_re=_wc_diag(dwc_re), ssm_c_im=-_wc_diag(dwc_im), ssm_d=dd.reshape(GROUPS, 16),
        loss=loss_part[0:1, 0]))
    dw("dw_ffn1_in", "ffn1_w_in", xb, dh1, 512, FFH, (4, D, FFH), shard_cols=FFH, interleaved=True)
    dw("dw_ffn1_out", "ffn1_w_out", a1_b, df1_b, FFH, 1024, (4, FF // 4, D))
    return loss_part[0, 0], dx0


RAW_ORDER = ["ln1_g", "ln1_b", "ln2_g", "ln2_b", "ln3_g", "ln3_b", "ln4_g", "ln4_b", "conv_w", "conv_b",
             "a_re", "a_im", "bb_re", "bb_im", "ssm_c_re", "ssm_c_im", "ssm_d", "loss"]

GATHER_FIRST = ["ffn1_w_in", "ffn1_w_out"]
GATHER_AT = {"ffn1_fwd": ["mix_w_in", "conv_w_out", "conv_w"], "mix_fwd_a": ["ssm_w_glu", "mix_w_out"],
             "s5_scan_fwd": ["ffn2_w_in"], "mix_fwd_b": ["ffn2_w_out"], "ffn2_fwd": ["ple_w_in", "ple_w_gate"]}
REDUCE_GROUP = {"ffn2": ["ple_w_gate", "ple_w_in", "ffn2_w_in", "ffn2_w_out"],
                "mix": ["mix_w_out", "ssm_w_glu", "conv_w_out", "mix_w_in"], "ffn1": ["ffn1_w_in", "ffn1_w_out"]}
REDUCE_AT = {"mix_bwd_b": [("swap", "ffn2")], "mix_bwd_a": [("join", "ffn2")]}
BEGIN_AT = {"dw_mix_out": [("exchange", "ffn2")], "ffn1_bwd": [("swap", "mix")],
            "dw_ffn1_in": [("small", None), ("exchange", "mix")]}
BEHIND = {"dw_glu": [("exchange", "ffn2")], "s5_scan_bwd": [("exchange", "ffn2")]}
END_AT = {"mix_bwd_a": [("exchange", "ffn2", ["dw_mix_out", "dw_glu", "s5_scan_bwd"])],
          "dw_ffn1_in": [("swap", "mix", ["ffn1_bwd"])]}
LAST_GROUP = "ffn1"
SC_ADAMW = {"ffn2": ["ple_w_gate", "ple_w_in", "ffn2_w_out"]}


class _Sched:
    def __init__(self, cmidx, on_joined):
        self.bufs, self.cmidx, self.on_joined = {}, cmidx, on_joined
        self.W, self.G, self.raw, self.small_buf = {}, {}, None, None
        self.got1, self.p32, self.pbf, self.got2, self.half, self.theirs = {}, {}, {}, {}, {}, {}
        self._open, self._split, self.done = [], {}, {}

    def first_begin(self, bufs):
        self.bufs.update(bufs)
        p = _gather_ici_payload([bufs[n] for n in GATHER_FIRST])
        self._first = (p, _split_start(p, "gather_first_start"))
        return self._first[1][3]

    def first_end(self, bufs, after):
        self.bufs.update(bufs)
        p, handle = self._first
        _, landed = _split_wait(p, handle, after, "gather_first_wait")
        (outs,) = _comm_call("gather_first_pass", [_gather_pass_payload(landed)])
        self.W.update(zip(GATHER_FIRST, outs))

    def _payload(self, stage, key):
        if stage == "gather":
            return _gather_payload([self.bufs[n] for n in key])
        if stage == "small":
            return _allgather_payload(_pack([self.raw[k] for k in RAW_ORDER]))
        names = REDUCE_GROUP[key]
        if stage == "swap":
            return _swap_payload([self.G[n] for n in names])
        if stage == "exchange":
            for n in names:
                self.p32[n], self.pbf[n] = _pair_sum(self.cmidx, self.G[n], self.got1[n], "pair_sum_" + n)
            return _exchange_payload([self.pbf[n] for n in names])
        for n in names:
            self.half[n] = _chip_sum(self.p32[n], self.got2[n], "chip_sum_" + n)
        return _join_payload([self.half[n] for n in names])

    def _store(self, stages, got):
        for (stage, key), outs in zip(stages, got):
            if stage == "gather":
                self.W.update(zip(key, outs))
            elif stage == "small":
                self.small_buf = outs[0]
            else:
                {"swap": self.got1, "exchange": self.got2, "join": self.theirs}[stage].update(
                    zip(REDUCE_GROUP[key], outs))
                if stage == "join":
                    self.on_joined(key)

    def _standalone(self, name, stages):
        self._store(stages, _comm_call(name, [self._payload(s, k) for s, k in stages]))

    def carry(self, name):
        for stage, key, behind in END_AT.get(name, []):
            self._end(stage, key, [self.done[b] for b in behind])
        tokens = [self._begin(stage, key) for stage, key in BEGIN_AT.get(name, [])]
        tokens += [self._split[sk][1][3] for sk in BEHIND.get(name, [])]
        self._open = [("gather", GATHER_AT[name])] if name in GATHER_AT else []
        self._open += REDUCE_AT.get(name, [])
        comm = [self._payload(s, k) for s, k in self._open]
        if tokens:
            comm.append(_Payload(tokens, [], {}, [], lambda *a: None, lambda *a: None))
        return tuple(comm)

    def landed(self, name, got):
        self._store(self._open, got)

    def grad(self, name, g4):
        self.G[name] = g4

    def small(self, raw):
        self.raw = raw

    def _begin(self, stage, key):
        p = self._payload(stage, key)
        self._split[stage, key] = (p, _split_start(p, "%s_%s_start" % (stage, key)))
        return self._split[stage, key][1][3]

    def _end(self, stage, key, after):
        p, handle = self._split.pop((stage, key))
        srcs, lands = _split_wait(p, handle, after, "%s_%s_wait" % (stage, key))
        if stage == "swap":
            self.G.update(zip(REDUCE_GROUP[key], srcs))
        self._store([(stage, key)], [lands])

    def tail_begin(self):
        return self._begin("swap", LAST_GROUP)

    def tail_mid(self, after):
        self._end("swap", LAST_GROUP, after)
        token = self._begin("exchange", LAST_GROUP)
        self._end("small", None, [token])
        self._end("exchange", "mix", [token])
        self._standalone("reduce_tail_join_mix", [("join", "mix")])
        return token

    def tail_end(self, after):
        self._end("exchange", LAST_GROUP, after)
        self._standalone("reduce_tail_join", [("join", LAST_GROUP)])


def _small_grads(raw_sum, sp):
    _, vjp = jax.vjp(_zoh, sp["ssm_lam_re"], sp["ssm_lam_im"], sp["ssm_log_step"], sp["ssm_b_re"], sp["ssm_b_im"])
    d_lre, d_lim, d_ls, d_bre, d_bim = vjp((raw_sum["a_re"], raw_sum["a_im"], raw_sum["bb_re"], raw_sum["bb_im"]))
    g = {k: raw_sum[k] for k in ("ln1_g", "ln1_b", "ln2_g", "ln2_b", "ln3_g", "ln3_b", "ln4_g", "ln4_b",
                                 "conv_w", "conv_b", "ssm_c_re", "ssm_c_im", "ssm_d")}
    g.update(ssm_lam_re=d_lre, ssm_lam_im=d_lim, ssm_log_step=d_ls, ssm_b_re=d_bre, ssm_b_im=d_bim)
    return g


def kernel(x, p, ffn1_w_in, ffn1_w_out, ln1_g, ln1_b, mix_w_in, conv_w, conv_b, conv_w_out, ssm_lam_re, ssm_lam_im, ssm_log_step, ssm_b_re, ssm_b_im, ssm_c_re, ssm_c_im, ssm_d, ssm_w_glu, mix_w_out, ln2_g, ln2_b, ffn2_w_in, ffn2_w_out, ln3_g, ln3_b, ple_w_in, ple_w_gate, ln4_g, ln4_b, loss_target, m_ffn1_w_in, m_ffn1_w_out, m_ln1_g, m_ln1_b, m_mix_w_in, m_conv_w, m_conv_b, m_conv_w_out, m_ssm_lam_re, m_ssm_lam_im, m_ssm_log_step, m_ssm_b_re, m_ssm_b_im, m_ssm_c_re, m_ssm_c_im, m_ssm_d, m_ssm_w_glu, m_mix_w_out, m_ln2_g, m_ln2_b, m_ffn2_w_in, m_ffn2_w_out, m_ln3_g, m_ln3_b, m_ple_w_in, m_ple_w_gate, m_ln4_g, m_ln4_b, v_ffn1_w_in, v_ffn1_w_out, v_ln1_g, v_ln1_b, v_mix_w_in, v_conv_w, v_conv_b, v_conv_w_out, v_ssm_lam_re, v_ssm_lam_im, v_ssm_log_step, v_ssm_b_re, v_ssm_b_im, v_ssm_c_re, v_ssm_c_im, v_ssm_d, v_ssm_w_glu, v_mix_w_out, v_ln2_g, v_ln2_b, v_ffn2_w_in, v_ffn2_w_out, v_ln3_g, v_ln3_b, v_ple_w_in, v_ple_w_gate, v_ln4_g, v_ln4_b):
    args = dict(locals())
    w = {n: args[n] for n in WEIGHTS}
    m = {n: args["m_" + n] for n in WEIGHTS}
    v = {n: args["v_" + n] for n in WEIGHTS}
    _, _, c, me = _where()
    cidx = jnp.stack([c, me]).astype(jnp.int32)
    meidx = jnp.reshape(me, (1,)).astype(jnp.int32)

    out_g, out_d, out_m, out_v = {}, {}, {}, {}

    def on_joined(group):
        for n in SC_ADAMW.get(group, []):
            g, dl, mn, vn = _adamw_pair_sc(w[n][0], sched.half[n], sched.theirs[n], m[n][0], v[n][0],
                                           "adamw_sc_" + n)
            out_g[n], out_d[n], out_m[n], out_v[n] = g[None], dl[None], mn[None], vn[None]

    sched = _Sched(cidx, on_joined)
    token = sched.first_begin({n: _slot_cast(meidx, w[n][0], BF16, "cast_" + n) for n in GATHER_FIRST})
    rest = {n: _slot_cast(meidx, w[n][0], BF16, "cast_" + n, (token,)) for n in BIG if n not in GATHER_FIRST}
    rest["conv_w"] = _slot_cast(meidx, jnp.pad(conv_w[0], ((0, 13), (0, 0))), F32, "cast_conv_w", (token,))
    sp = {n: (w[n] if w[n].ndim == 2 and n != "ssm_log_step" else w[n][0]) for n in SMALL if n != "conv_w"}
    ops = _s5_operands({**sp, "ssm_lam_re": sp["ssm_lam_re"] + token[0, 0]})
    sched.first_end(rest, list(rest.values()) + list(ops))
    loss_part, dx0 = _local_step(x[0], p[0, 0], loss_target[0], sp, ops, sched)

    def big_adamw(names, token):
        for n in names:
            g, dl, mn, vn = _adamw_pair(cidx, w[n][0], sched.half[n], sched.theirs[n], m[n][0], v[n][0], token,
                                        "adamw_" + n)
            out_g[n], out_d[n], out_m[n], out_v[n] = g[None], dl[None], mn[None], vn[None]

    first = ["ffn2_w_in"]
    big_adamw(first, sched.tail_begin())
    token = sched.tail_mid([out_v[n] for n in first])

    raw_shapes = [sched.raw[k].shape for k in RAW_ORDER]
    raw_sum = dict(zip(RAW_ORDER, _unpack(_sum8(sched.small_buf, token), raw_shapes)))
    loss = raw_sum["loss"][0]
    sg = _small_grads(raw_sum, sp)
    sg["conv_w"] = lax.dynamic_slice_in_dim(sg["conv_w"], me * 128, 128, axis=1)
    small_shapes = [w[n].shape for n in SMALL]
    gp = _pack([sg[n] for n in SMALL])
    d_s, m_s, v_s = _adamw(_pack([w[n] for n in SMALL]), gp, _pack([m[n] for n in SMALL]),
                           _pack([v[n] for n in SMALL]), "adamw_small")

    for n, a, b_, c_, d_ in zip(SMALL, _unpack(gp, small_shapes), _unpack(d_s, small_shapes),
                                _unpack(m_s, small_shapes), _unpack(v_s, small_shapes)):
        out_g[n], out_d[n], out_m[n], out_v[n] = a, b_, c_, d_
    big_adamw(REDUCE_GROUP["mix"], token)
    sched.tail_end([d_s] + [out_v[n] for n in REDUCE_GROUP["mix"]])
    big_adamw(REDUCE_GROUP[LAST_GROUP], token)

    return (loss, dx0[None], *[out_g[n] for n in WEIGHTS], *[out_d[n] for n in WEIGHTS],
            *[out_m[n] for n in WEIGHTS], *[out_v[n] for n in WEIGHTS])
```

```python
import functools
import math

import jax
import jax.numpy as jnp
import numpy as np
from jax import lax
from jax.experimental import pallas as pl
from jax.experimental.pallas import tpu as pltpu

F32, BF16 = jnp.float32, jnp.bfloat16
D = 1024
FF = 2816
FFH = FF // 2
CONV = 512
SSM = 512
GROUPS = 32
STATE = 64
LANES = GROUPS * STATE
SCAN_W = 128
SCAN_PER = 512 // SCAN_W
SCAN_GR = SCAN_W // STATE
SCAN_R = 256
TOKEN_TILE = 256
ALPHA = 2.0 ** 0.25
LN_EPS = 1e-5
GELU_C = math.sqrt(2.0 / math.pi)
B1, B2, LR, EPS, WD, STEP = 0.9, 0.999, 0.001, 1e-8, 0.01, 10
MESH = pl.DeviceIdType.MESH
ANY = pl.BlockSpec(memory_space=pl.ANY)
VMEM_FULL = pl.BlockSpec(memory_space=pltpu.VMEM)


def _cp(vmem_mb=48, n_axes=1):
    return pltpu.CompilerParams(vmem_limit_bytes=vmem_mb << 20,
                                dimension_semantics=("arbitrary",) * n_axes)


def _hbm(*arrs):
    return [pltpu.with_memory_space_constraint(a, pltpu.HBM) for a in arrs]


def _hbm_out(shapes):
    if isinstance(shapes, (list, tuple)):
        return [pltpu.HBM(s.shape, s.dtype) for s in shapes]
    return pltpu.HBM(shapes.shape, shapes.dtype)


def _nn(a, b):
    return jnp.dot(a, b, preferred_element_type=F32)


def _nt(a, b):
    return lax.dot_general(a, b, (((1,), (1,)), ((), ())), preferred_element_type=F32)


def _tn(a, b):
    return lax.dot_general(a, b, (((0,), (0,)), ((), ())), preferred_element_type=F32)


def _sig(v):
    return jax.nn.sigmoid(v)


def _ln_stats(r):
    mu = jnp.mean(r, axis=-1, keepdims=True)
    xc = r - mu
    var = jnp.mean(xc * xc, axis=-1, keepdims=True)
    rstd = lax.rsqrt(var + LN_EPS)
    return xc * rstd, rstd


def _ln_bwd(dy, r, g):
    xhat, rstd = _ln_stats(r)
    dyg = dy * g
    m1 = jnp.mean(dyg, axis=-1, keepdims=True)
    m2 = jnp.mean(dyg * xhat, axis=-1, keepdims=True)
    return rstd * (dyg - m1 - xhat * m2), xhat


def _rowsum(v):
    return jnp.sum(v, axis=0, keepdims=True)


class _Payload:
    def __init__(self, operands, outs, aliases, sems, start, finish):
        self.operands, self.outs, self.aliases, self.sems = list(operands), list(outs), dict(aliases), list(sems)
        self.start, self.finish = start, finish


def _split(flat, comm, attr):
    out, i = [], 0
    for p in comm:
        n = len(getattr(p, attr))
        out.append(list(flat[i:i + n]))
        i += n
    return out


def _run_comm(comm, which, cin, cout, csem):
    for p, a, b, s in zip(comm, _split(cin, comm, "operands"), _split(cout, comm, "outs"), _split(csem, comm, "sems")):
        getattr(p, which)(a, b, s)


def _pcall(body, *, name, grid, in_specs, out_specs, out_shape, operands, scratch=(), vmem_mb=48, aliases=None,
           comm=()):
    ni, no, ns = len(in_specs), len(out_specs), len(scratch)
    c_ops = [a for p in comm for a in p.operands]
    c_outs = [s for p in comm for s in p.outs]
    c_sems = [s for p in comm for s in p.sems]
    io = dict(aliases or {})
    off_i, off_o = ni, no
    for p in comm:
        for a, b in p.aliases.items():
            io[off_i + a] = off_o + b
        off_i += len(p.operands)
        off_o += len(p.outs)

    def wrapped(*refs):
        ins, cin = refs[:ni], refs[ni:ni + len(c_ops)]
        o0 = ni + len(c_ops)
        outs, cout = refs[o0:o0 + no], refs[o0 + no:o0 + no + len(c_outs)]
        s0 = o0 + no + len(c_outs)
        scr, csem = refs[s0:s0 + ns], refs[s0 + ns:]
        if comm:
            first = functools.reduce(jnp.logical_and, [pl.program_id(a) == 0 for a in range(len(grid))])
            pl.when(first)(lambda: _run_comm(comm, "start", cin, cout, csem))
        body(*ins, *outs, *scr)
        if comm:
            last = functools.reduce(jnp.logical_and, [pl.program_id(a) == grid[a] - 1 for a in range(len(grid))])
            pl.when(last)(lambda: _run_comm(comm, "finish", cin, cout, csem))

    res = pl.pallas_call(
        wrapped, name=name, grid=grid,
        in_specs=list(in_specs) + [ANY] * len(c_ops), out_specs=list(out_specs) + [ANY] * len(c_outs),
        out_shape=_hbm_out(list(out_shape) + c_outs), scratch_shapes=list(scratch) + c_sems,
        input_output_aliases=io,
        compiler_params=pltpu.CompilerParams(vmem_limit_bytes=vmem_mb << 20,
                                             dimension_semantics=("arbitrary",) * len(grid),
                                             has_side_effects=bool(c_sems)),
    )(*_hbm(*operands, *c_ops))
    return list(res[:no]), _split(res[no:], comm, "outs")


def _comm_call(name, comm):
    c_ops = [a for p in comm for a in p.operands]
    c_outs = [s for p in comm for s in p.outs]
    c_sems = [s for p in comm for s in p.sems]
    io, off_i, off_o = {}, 0, 0
    for p in comm:
        for a, b in p.aliases.items():
            io[off_i + a] = off_o + b
        off_i += len(p.operands)
        off_o += len(p.outs)

    def body(*refs):
        cin, cout = refs[:len(c_ops)], refs[len(c_ops):len(c_ops) + len(c_outs)]
        csem = refs[len(c_ops) + len(c_outs):]
        _run_comm(comm, "start", cin, cout, csem)
        _run_comm(comm, "finish", cin, cout, csem)

    res = pl.pallas_call(
        body, name=name, in_specs=[ANY] * len(c_ops), out_specs=[ANY] * len(c_outs), out_shape=_hbm_out(c_outs),
        scratch_shapes=c_sems, input_output_aliases=io,
        compiler_params=pltpu.CompilerParams(has_side_effects=True),
    )(*_hbm(*c_ops))
    return _split(res, comm, "outs")


def _ffn_fwd(x, w_in4, w_out2, g, b, tm, name, comm=()):
    T = x.shape[0]

    def body(x_ref, win_ref, wo_ref, g_ref, b_ref, h_ref, r_ref, xo_ref, xob_ref, xib_ref):
        xf = x_ref[...]
        xv = xf.astype(BF16)
        xib_ref[...] = xv
        acc = ALPHA * xf
        for k in range(2):
            gt = _nn(xv, win_ref[k])
            up = _nn(xv, win_ref[k + 2])
            a = (gt * _sig(gt) * up).astype(BF16)
            h_ref[:, 2 * k * FFH:(2 * k + 1) * FFH] = gt.astype(BF16)
            h_ref[:, (2 * k + 1) * FFH:(2 * k + 2) * FFH] = up.astype(BF16)
            acc = acc + 0.5 * _nn(a, wo_ref[k])
        xhat, _ = _ln_stats(acc)
        xo = xhat * g_ref[...] + b_ref[...]
        r_ref[...] = acc
        xo_ref[...] = xo
        xob_ref[...] = xo.astype(BF16)

    tok = pl.BlockSpec((tm, D), lambda i: (i, 0))
    vec = pl.BlockSpec((1, D), lambda i: (0, 0))
    return _pcall(
        body, name=name, grid=(T // tm,),
        in_specs=[tok,
                  pl.BlockSpec((4, D, FFH), lambda i: (0, 0, 0), pipeline_mode=pl.Buffered(1)),
                  pl.BlockSpec((2, FFH, D), lambda i: (0, 0, 0), pipeline_mode=pl.Buffered(1)),
                  vec, vec],
        out_specs=[pl.BlockSpec((tm, 2 * FF), lambda i: (i, 0)), tok, tok, tok, tok],
        out_shape=[jax.ShapeDtypeStruct((T, 2 * FF), BF16), jax.ShapeDtypeStruct((T, D), F32),
                   jax.ShapeDtypeStruct((T, D), F32), jax.ShapeDtypeStruct((T, D), BF16),
                   jax.ShapeDtypeStruct((T, D), BF16)],
        vmem_mb=58, comm=comm, operands=(x, w_in4, w_out2, g, b))


def _ffn_bwd(dy, r, g, h, w_in4, w_out2, tm, name, comm=()):
    T = dy.shape[0]

    def body(dy_ref, r_ref, g_ref, h_ref, win_ref, wo_ref, dx_ref, dh_ref, a_ref, df_ref, dg_ref, db_ref):
        i = pl.program_id(0)
        dyv = dy_ref[...]
        dr, xhat = _ln_bwd(dyv, r_ref[...], g_ref[...])
        dg_ref[...] = jnp.where(i == 0, 0.0, dg_ref[...]) + _rowsum(dyv * xhat)
        db_ref[...] = jnp.where(i == 0, 0.0, db_ref[...]) + _rowsum(dyv)
        dfb = (0.5 * dr).astype(BF16)
        df_ref[...] = dfb
        acc = ALPHA * dr
        for k in range(2):
            da = _nt(dfb, wo_ref[k])
            gt = h_ref[:, 2 * k * FFH:(2 * k + 1) * FFH].astype(F32)
            up = h_ref[:, (2 * k + 1) * FFH:(2 * k + 2) * FFH].astype(F32)
            sg = _sig(gt)
            silu = gt * sg
            dgate = (da * up * (sg * (1.0 + gt * (1.0 - sg)))).astype(BF16)
            dup = (da * silu).astype(BF16)
            a_ref[:, k * FFH:(k + 1) * FFH] = (silu * up).astype(BF16)
            dh_ref[:, 2 * k * FFH:(2 * k + 1) * FFH] = dgate
            dh_ref[:, (2 * k + 1) * FFH:(2 * k + 2) * FFH] = dup
            acc = acc + _nt(dgate, win_ref[k]) + _nt(dup, win_ref[k + 2])
        dx_ref[...] = acc

    tok = pl.BlockSpec((tm, D), lambda i: (i, 0))
    vec = pl.BlockSpec((1, D), lambda i: (0, 0))
    wide = pl.BlockSpec((tm, 2 * FF), lambda i: (i, 0))
    return _pcall(
        body, name=name, grid=(T // tm,),
        in_specs=[tok, tok, vec, wide,
                  pl.BlockSpec((4, D, FFH), lambda i: (0, 0, 0), pipeline_mode=pl.Buffered(1)),
                  pl.BlockSpec((2, FFH, D), lambda i: (0, 0, 0), pipeline_mode=pl.Buffered(1))],
        out_specs=[tok, wide, pl.BlockSpec((tm, FF), lambda i: (i, 0)), tok, vec, vec],
        out_shape=[jax.ShapeDtypeStruct((T, D), F32), jax.ShapeDtypeStruct((T, 2 * FF), BF16),
                   jax.ShapeDtypeStruct((T, FF), BF16), jax.ShapeDtypeStruct((T, D), BF16),
                   jax.ShapeDtypeStruct((1, D), F32), jax.ShapeDtypeStruct((1, D), F32)],
        vmem_mb=58, comm=comm, operands=(dy, r, g, h, w_in4, w_out2))


def _mm_tn(a, b, tk, tn, name, shard_cols=None, interleaved=False, comm=()):
    T, K = a.shape
    N = b.shape[1]

    def body(a_ref, b_ref, o_ref):
        o_ref[...] = _tn(a_ref[...], b_ref[...])

    if shard_cols is None:
        out_shape = jax.ShapeDtypeStruct((K, N), F32)
        out_spec = pl.BlockSpec((tk, tn), lambda ki, nj: (ki, nj))
    else:
        per = shard_cols // tn

        def shard(nj):
            blk = nj // per
            return (blk % 2) * 2 + blk // 2 if interleaved else blk

        out_shape = jax.ShapeDtypeStruct((N // shard_cols, K, shard_cols), F32)
        out_spec = pl.BlockSpec((None, tk, tn), lambda ki, nj: (shard(nj), ki, nj % per))
    (out,), got = _pcall(
        body, name=name, grid=(K // tk, N // tn),
        in_specs=[pl.BlockSpec((T, tk), lambda ki, nj: (0, ki)), pl.BlockSpec((T, tn), lambda ki, nj: (0, nj))],
        out_specs=[out_spec], out_shape=[out_shape], comm=comm, operands=(a, b))
    return out, got


def _mix_fwd_a(xb, w_mix4, conv_w, conv_b, w_co4, tm, comm=()):
    T = xb.shape[0]

    def body(xb_ref, w_ref, cw_ref, cb_ref, wco_ref,
             pc_ref, z_ref, yin_ref, su_ref, sub_ref, gc_ref, gs_ref, yc_ref, qbuf):
        @pl.when(pl.program_id(0) == 0)
        def _():
            qbuf[pl.ds(0, 8), :] = jnp.zeros((8, CONV), F32)

        xv = xb_ref[...]
        p0 = _nn(xv, w_ref[0])
        p1 = _nn(xv, w_ref[1])
        gc_ref[...] = _nn(xv, w_ref[2]).astype(BF16)
        gs_ref[...] = _nn(xv, w_ref[3]).astype(BF16)
        cbv, ccv = p0[:, :CONV], p0[:, CONV:]
        chv, suv = p1[:, :CONV], p1[:, CONV:]
        q = ccv * chv
        qbuf[pl.ds(8, tm), :] = q
        cw = cw_ref[...]
        z = (cw[2:3] * q + cw[1:2] * qbuf[pl.ds(7, tm), :] + cw[0:1] * qbuf[pl.ds(6, tm), :]
             + cb_ref[...])
        qbuf[pl.ds(0, 8), :] = q[tm - 8:tm]
        yin = (cbv * z).astype(BF16)
        pc_ref[:, 0:CONV] = cbv.astype(BF16)
        pc_ref[:, CONV:2 * CONV] = ccv.astype(BF16)
        pc_ref[:, 2 * CONV:3 * CONV] = chv.astype(BF16)
        z_ref[...] = z.astype(BF16)
        yin_ref[...] = yin
        su_ref[...] = suv
        sub_ref[...] = suv.astype(BF16)
        for k in range(4):
            yc_ref[:, 256 * k:256 * (k + 1)] = _nn(yin, wco_ref[k]).astype(BF16)

    def tok(n):
        return pl.BlockSpec((tm, n), lambda i: (i, 0))

    def full(shape):
        return pl.BlockSpec(shape, lambda i: (0,) * len(shape))

    return _pcall(
        body, name="mix_fwd_a", grid=(T // tm,),
        in_specs=[tok(D), full((4, D, D)), full((3, CONV)), full((1, CONV)), full((4, CONV, 256))],
        out_specs=[tok(3 * CONV), tok(CONV), tok(CONV), tok(SSM), tok(SSM), tok(D), tok(D), tok(D)],
        out_shape=[jax.ShapeDtypeStruct((T, 3 * CONV), BF16), jax.ShapeDtypeStruct((T, CONV), BF16),
                   jax.ShapeDtypeStruct((T, CONV), BF16), jax.ShapeDtypeStruct((T, SSM), F32),
                   jax.ShapeDtypeStruct((T, SSM), BF16), jax.ShapeDtypeStruct((T, D), BF16),
                   jax.ShapeDtypeStruct((T, D), BF16), jax.ShapeDtypeStruct((T, D), BF16)],
        scratch=[pltpu.VMEM((tm + 8, CONV), F32)], vmem_mb=56, comm=comm,
        operands=(xb, w_mix4, conv_w, conv_b, w_co4))


def _scan_rows(bre, bim, ar, ai, T, rev, load, store):
    R, W, G = SCAN_R, bre.shape[1], T // 8
    if rev:
        ai = -ai

    def cmul(pr, pi, xr, xi):
        return pr * xr - pi * xi, pr * xi + pi * xr

    pw = [(ar, ai)]
    for _ in range(7):
        pw.append(cmul(ar, ai, *pw[-1]))

    def shifted(v, d, axis, n, idx):
        if rev:
            return jnp.where(idx < n - d, pltpu.roll(v, n - d, axis), 0.0)
        return jnp.where(idx >= d, pltpu.roll(v, d, axis), 0.0)

    sub8 = lax.broadcasted_iota(jnp.int32, (8, W), 0)
    inside = {d: (sub8 < 8 - d) if rev else (sub8 >= d) for d in (1, 2, 4)}
    pm = {d: (jnp.where(inside[d], pw[d - 1][0], 0.0)[None], jnp.where(inside[d], pw[d - 1][1], 0.0)[None])
          for d in (1, 2, 4)}

    def step(i, _):
        t0 = pl.multiple_of(i * R, R)
        vr, vi = load(t0)
        vr, vi = vr.reshape(R // 8, 8, W), vi.reshape(R // 8, 8, W)
        for d in (1, 2, 4):
            sh = (8 - d) if rev else d
            dr, di = cmul(pm[d][0], pm[d][1], pltpu.roll(vr, sh, 1), pltpu.roll(vi, sh, 1))
            vr, vi = vr + dr, vi + di
        bre[pl.ds(t0 + 8, R), :] = vr.reshape(R, W)
        bim[pl.ds(t0 + 8, R), :] = vi.reshape(R, W)
        return 0

    lax.fori_loop(0, T // R, step, 0)

    edge = 0 if rev else 7
    cr = bre[pl.ds(8 + edge, G, stride=8), :]
    ci = bim[pl.ds(8 + edge, G, stride=8), :]
    row = lax.broadcasted_iota(jnp.int32, (G, W), 0)
    qr, qi = pw[7]
    d = 1
    while d < G:
        dr, di = cmul(qr, qi, shifted(cr, d, 0, G, row), shifted(ci, d, 0, G, row))
        cr, ci = cr + dr, ci + di
        qr, qi = qr * qr - qi * qi, 2.0 * qr * qi
        d *= 2

    order = [7 - r for r in range(8)] if rev else list(range(8))
    p8r = jnp.concatenate([pw[k][0] for k in order], axis=0)[None]
    p8i = jnp.concatenate([pw[k][1] for k in order], axis=0)[None]

    def back(nre, nim):
        nre[...] = shifted(cr, 1, 0, G, row)
        nim[...] = shifted(ci, 1, 0, G, row)

        def step3(i, _):
            t0 = pl.multiple_of(i * R, R)
            g0 = pl.multiple_of(i * (R // 8), R // 8)
            br = jnp.broadcast_to(nre[pl.ds(g0, R // 8), :][:, None, :], (R // 8, 8, W))
            bi = jnp.broadcast_to(nim[pl.ds(g0, R // 8), :][:, None, :], (R // 8, 8, W))
            dr, di = cmul(p8r, p8i, br, bi)
            xr = bre[pl.ds(t0 + 8, R), :] + dr.reshape(R, W)
            xi = bim[pl.ds(t0 + 8, R), :] + di.reshape(R, W)
            store(t0, xr, xi)
            return 0

        lax.fori_loop(0, T // R, step3, 0)

    pl.run_scoped(back, pltpu.VMEM((G, W), F32), pltpu.VMEM((G, W), F32))


def _scan_specs(T):
    W = SCAN_W
    lane = pl.BlockSpec((T, W), lambda j: (0, j))
    col = pl.BlockSpec((T, 128), lambda j: (0, j // SCAN_PER))
    wb = pl.BlockSpec((None, 128, W), lambda j: (j, 0, 0))
    wc = pl.BlockSpec((None, W, 128), lambda j: (j, 0, 0))
    vec = pl.BlockSpec((1, W), lambda j: (0, j))
    return lane, col, wb, wc, vec


def _s5_scan_fwd(su_b, wb_re, wb_im, a_re, a_im, comm=()):
    T = su_b.shape[0]
    W = SCAN_W

    def body(su_ref, wbr_ref, wbi_ref, ar_ref, ai_ref, sr_ref, si_ref, bre, bim):
        su = su_ref[...]
        bre[pl.ds(8, T), :] = _nn(su, wbr_ref[...])
        bim[pl.ds(8, T), :] = _nn(su, wbi_ref[...])
        def store(t0, xr, xi):
            sr_ref[pl.ds(t0, SCAN_R), :] = xr.astype(BF16)
            si_ref[pl.ds(t0, SCAN_R), :] = xi.astype(BF16)

        _scan_rows(bre, bim, ar_ref[...], ai_ref[...], T, False,
                   lambda t0: (bre[pl.ds(t0 + 8, SCAN_R), :], bim[pl.ds(t0 + 8, SCAN_R), :]), store)

    lane, col, wb, wc, vec = _scan_specs(T)
    return _pcall(
        body, name="s5_scan_fwd", grid=(LANES // W,),
        in_specs=[col, wb, wb, vec, vec],
        out_specs=[lane, lane],
        out_shape=[jax.ShapeDtypeStruct((T, LANES), BF16)] * 2,
        scratch=[pltpu.VMEM((T + 16, W), F32)] * 2, comm=comm,
        operands=(su_b, wb_re, wb_im, a_re, a_im))


def _gelu(s):
    th = jnp.tanh(GELU_C * (s + 0.044715 * s * s * s))
    return 0.5 * s * (1.0 + th), th


def _mix_fwd_b(st_re, st_im, wc_re4, wc_im4, su, dvec, w_glu4, g_conv, g_ssm, y_conv, w_mo, x1, g, b, tm, comm=()):
    T = su.shape[0]

    def body(sr_ref, si_ref, wcr_ref, wci_ref, su_ref, d_ref, wg_ref, gc_ref, gs_ref, yc_ref, wmo_ref,
             x_ref, g_ref, b_ref, s_ref, sgb_ref, ga_ref, gb_ref, mb_ref, r_ref, xo_ref):
        srb = sr_ref[...]
        sib = si_ref[...]
        ys = [_nn(srb[:, 512 * J:512 * (J + 1)], wcr_ref[J]) + _nn(sib[:, 512 * J:512 * (J + 1)], wci_ref[J])
              for J in range(4)]
        s = jnp.concatenate(ys, axis=1) + d_ref[...] * su_ref[...]
        sg, _ = _gelu(s)
        sgb = sg.astype(BF16)
        ga = jnp.concatenate([_nn(sgb, wg_ref[0]), _nn(sgb, wg_ref[1])], axis=1)
        gb = jnp.concatenate([_nn(sgb, wg_ref[2]), _nn(sgb, wg_ref[3])], axis=1)
        merged = (_sig(gc_ref[...].astype(F32)) * yc_ref[...].astype(F32)
                  + _sig(gs_ref[...].astype(F32)) * (ga * _sig(gb)))
        mb = merged.astype(BF16)
        r = ALPHA * x_ref[...] + _nn(mb, wmo_ref[...])
        xhat, _ = _ln_stats(r)
        xo = xhat * g_ref[...] + b_ref[...]
        s_ref[...] = s
        sgb_ref[...] = sgb
        ga_ref[...] = ga.astype(BF16)
        gb_ref[...] = gb.astype(BF16)
        mb_ref[...] = mb
        r_ref[...] = r
        xo_ref[...] = xo

    def tok(n):
        return pl.BlockSpec((tm, n), lambda i: (i, 0))

    def full(shape):
        return pl.BlockSpec(shape, lambda i: (0,) * len(shape))

    return _pcall(
        body, name="mix_fwd_b", grid=(T // tm,),
        in_specs=[tok(LANES), tok(LANES), full((4, 512, 128)), full((4, 512, 128)), tok(SSM), full((1, SSM)),
                  full((4, SSM, 512)), tok(D), tok(D), tok(D), full((D, D)), tok(D), full((1, D)), full((1, D))],
        out_specs=[tok(SSM), tok(SSM), tok(D), tok(D), tok(D), tok(D), tok(D)],
        out_shape=[jax.ShapeDtypeStruct((T, SSM), F32), jax.ShapeDtypeStruct((T, SSM), BF16),
                   jax.ShapeDtypeStruct((T, D), BF16), jax.ShapeDtypeStruct((T, D), BF16),
                   jax.ShapeDtypeStruct((T, D), BF16), jax.ShapeDtypeStruct((T, D), F32),
                   jax.ShapeDtypeStruct((T, D), F32)],
        vmem_mb=56, comm=comm,
        operands=(st_re, st_im, wc_re4, wc_im4, su, dvec, w_glu4, g_conv, g_ssm, y_conv, w_mo, x1, g, b))


def _ple_loss(x3, x3b, p, w_pi4, w_pg, g, b, target, tm):
    T = x3.shape[0]
    PD = p.shape[1]

    def body(x_ref, xb_ref, p_ref, wpi_ref, wpg_ref, g_ref, b_ref, t_ref,
             loss_ref, dx_ref, pb_ref, dpw_ref, dgt_ref, dg_ref, db_ref):
        i = pl.program_id(0)
        pb = p_ref[...].astype(BF16)
        pw = jnp.concatenate([_nn(pb, wpi_ref[k]) for k in range(4)], axis=1)
        gt = _nn(xb_ref[...], wpg_ref[...])
        sg = _sig(gt)
        r = ALPHA * x_ref[...] + pw * sg
        gv = g_ref[...]
        xhat, rstd = _ln_stats(r)
        err = xhat * gv + b_ref[...] - t_ref[...]
        lpart = jnp.zeros((1, 128), F32) + 0.5 * jnp.sum(jnp.mean(err * err, axis=-1, keepdims=True))
        dy = err * (1.0 / D)
        dyg = dy * gv
        m1 = jnp.mean(dyg, axis=-1, keepdims=True)
        m2 = jnp.mean(dyg * xhat, axis=-1, keepdims=True)
        dr = rstd * (dyg - m1 - xhat * m2)
        pg, pbias = _rowsum(dy * xhat), _rowsum(dy)

        @pl.when(i == 0)
        def _():
            loss_ref[...] = lpart
            dg_ref[...] = pg
            db_ref[...] = pbias

        @pl.when(i > 0)
        def _():
            loss_ref[...] += lpart
            dg_ref[...] += pg
            db_ref[...] += pbias

        dgt = (dr * pw * sg * (1.0 - sg)).astype(BF16)
        pb_ref[...] = pb
        dpw_ref[...] = (dr * sg).astype(BF16)
        dgt_ref[...] = dgt
        dx_ref[...] = ALPHA * dr + _nt(dgt, wpg_ref[...])

    def tok(n):
        return pl.BlockSpec((tm, n), lambda i: (i, 0))

    def full(shape):
        return pl.BlockSpec(shape, lambda i: (0,) * len(shape))

    return pl.pallas_call(
        body, name="ple_loss", grid=(T // tm,),
        in_specs=[tok(D), tok(D), tok(PD), full((4, PD, 256)), full((D, D)), full((1, D)), full((1, D)), tok(D)],
        out_specs=[full((1, 128)), tok(D), tok(PD), tok(D), tok(D), full((1, D)), full((1, D))],
        out_shape=_hbm_out([jax.ShapeDtypeStruct((1, 128), F32), jax.ShapeDtypeStruct((T, D), F32),
                            jax.ShapeDtypeStruct((T, PD), BF16), jax.ShapeDtypeStruct((T, D), BF16),
                            jax.ShapeDtypeStruct((T, D), BF16), jax.ShapeDtypeStruct((1, D), F32),
                            jax.ShapeDtypeStruct((1, D), F32)]),
        compiler_params=_cp(48, 1),
    )(*_hbm(x3, x3b, p, w_pi4, w_pg, g, b, target))


def _mix_bwd_b(dy, r2, g, w_mo, g_conv, g_ssm, y_conv, ga, gb, s, su, dvec, w_glu4, wc_re4, wc_im4, tm, comm=()):
    T = dy.shape[0]

    def body(dy_ref, r_ref, g_ref, wmo_ref, gc_ref, gs_ref, yc_ref, ga_ref, gb_ref, s_ref, su_ref, d_ref,
             wg_ref, wcr_ref, wci_ref,
             dres_ref, dmix_ref, dgl_ref, dsb_ref, dud_ref, gsr_ref, gsi_ref, dyc_ref, dp_ref,
             dg_ref, db_ref, dd_ref):
        i = pl.program_id(0)
        dyv = dy_ref[...]
        dr, xhat = _ln_bwd(dyv, r_ref[...], g_ref[...])
        dmix = dr.astype(BF16)
        dmerged = _nt(dmix, wmo_ref[...])
        sc, ss, sgb = (_sig(gc_ref[...].astype(F32)), _sig(gs_ref[...].astype(F32)),
                       _sig(gb_ref[...].astype(F32)))
        gav = ga_ref[...].astype(F32)
        yssm = gav * sgb
        dgc = dmerged * yc_ref[...].astype(F32) * sc * (1.0 - sc)
        dgss = dmerged * yssm * ss * (1.0 - ss)
        dyssm = dmerged * ss
        dgl = jnp.concatenate([dyssm * sgb, dyssm * gav * sgb * (1.0 - sgb)], axis=1).astype(BF16)
        dsg = (_nt(dgl[:, 0:512], wg_ref[0]) + _nt(dgl[:, 512:1024], wg_ref[1])
               + _nt(dgl[:, 1024:1536], wg_ref[2]) + _nt(dgl[:, 1536:2048], wg_ref[3]))
        sv = s_ref[...]
        _, th = _gelu(sv)
        dgelu = 0.5 * (1.0 + th) + 0.5 * sv * (1.0 - th * th) * GELU_C * (1.0 + 3.0 * 0.044715 * sv * sv)
        ds = dsg * dgelu
        dsb = ds.astype(BF16)
        pg, pb, pd = _rowsum(dyv * xhat), _rowsum(dyv), _rowsum(ds * su_ref[...])

        @pl.when(i == 0)
        def _():
            dg_ref[...] = pg
            db_ref[...] = pb
            dd_ref[...] = pd

        @pl.when(i > 0)
        def _():
            dg_ref[...] += pg
            db_ref[...] += pb
            dd_ref[...] += pd

        dres_ref[...] = ALPHA * dr
        dmix_ref[...] = dmix
        dgl_ref[...] = dgl
        dsb_ref[...] = dsb
        dud_ref[...] = ds * d_ref[...]
        for J in range(4):
            gsr_ref[:, 512 * J:512 * (J + 1)] = _nt(dsb[:, 128 * J:128 * (J + 1)], wcr_ref[J]).astype(BF16)
            gsi_ref[:, 512 * J:512 * (J + 1)] = _nt(dsb[:, 128 * J:128 * (J + 1)], wci_ref[J]).astype(BF16)
        dyc_ref[...] = (dmerged * sc).astype(BF16)
        dp_ref[:, 0:D] = dgc.astype(BF16)
        dp_ref[:, D:2 * D] = dgss.astype(BF16)

    def tok(n):
        return pl.BlockSpec((tm, n), lambda i: (i, 0))

    def full(shape):
        return pl.BlockSpec(shape, lambda i: (0,) * len(shape))

    return _pcall(
        body, name="mix_bwd_b", grid=(T // tm,),
        in_specs=[tok(D), tok(D), full((1, D)), full((D, D)), tok(D), tok(D), tok(D), tok(D), tok(D),
                  tok(SSM), tok(SSM), full((1, SSM)), full((4, SSM, 512)), full((4, 512, 128)), full((4, 512, 128))],
        out_specs=[tok(D), tok(D), tok(2 * D), tok(SSM), tok(SSM), tok(LANES), tok(LANES), tok(D),
                   pl.BlockSpec((tm, 2 * D), lambda i: (i, 1)), full((1, D)), full((1, D)), full((1, SSM))],
        out_shape=[jax.ShapeDtypeStruct((T, D), F32), jax.ShapeDtypeStruct((T, D), BF16),
                   jax.ShapeDtypeStruct((T, 2 * D), BF16), jax.ShapeDtypeStruct((T, SSM), BF16),
                   jax.ShapeDtypeStruct((T, SSM), F32), jax.ShapeDtypeStruct((T, LANES), BF16),
                   jax.ShapeDtypeStruct((T, LANES), BF16), jax.ShapeDtypeStruct((T, D), BF16),
                   jax.ShapeDtypeStruct((T, 4 * D), BF16), jax.ShapeDtypeStruct((1, D), F32),
                   jax.ShapeDtypeStruct((1, D), F32), jax.ShapeDtypeStruct((1, SSM), F32)],
        vmem_mb=56, comm=comm,
        operands=(dy, r2, g, w_mo, g_conv, g_ssm, y_conv, ga, gb, s, su, dvec, w_glu4, wc_re4, wc_im4))


def _s5_scan_bwd(gs_re, gs_im, st_re, st_im, su_b, ds_b, wb_re, wb_im, a_re, a_im, comm=()):
    T = su_b.shape[0]
    W = SCAN_W
    R = SCAN_R

    def body(gr_ref, gi_ref, sr_ref, si_ref, su_ref, ds_ref, wbr_ref, wbi_ref, ar_ref, ai_ref,
             dsu_ref, dwbr_ref, dwbi_ref, dwcr_ref, dwci_ref, dar_ref, dai_ref, gre, gim):
        j = pl.program_id(0)
        zero = jnp.zeros((8, W), F32)
        for buf in (gre, gim):
            buf[pl.ds(T + 8, 8), :] = zero
        def store(t0, xr, xi):
            gre[pl.ds(t0 + 8, R), :] = xr
            gim[pl.ds(t0 + 8, R), :] = xi

        _scan_rows(gre, gim, ar_ref[...], ai_ref[...], T, True,
                   lambda t0: (gr_ref[pl.ds(t0, R), :].astype(F32), gi_ref[pl.ds(t0, R), :].astype(F32)), store)
        grb = gre[pl.ds(8, T), :].astype(BF16)
        gib = gim[pl.ds(8, T), :].astype(BF16)
        part = _nt(grb, wbr_ref[...]) + _nt(gib, wbi_ref[...])

        @pl.when(j % SCAN_PER == 0)
        def _():
            dsu_ref[...] = part

        @pl.when(j % SCAN_PER > 0)
        def _():
            dsu_ref[...] += part

        su = su_ref[...]
        dwbr_ref[...] = _tn(su, grb)
        dwbi_ref[...] = _tn(su, gib)
        dsv = ds_ref[...]
        dwcr_ref[...] = _tn(sr_ref[...], dsv)
        dwci_ref[...] = _tn(si_ref[...], dsv)
        dar = jnp.zeros((1, W), F32)
        dai = jnp.zeros((1, W), F32)
        for c in range(T // R):
            xr = sr_ref[pl.ds(c * R, R), :].astype(F32)
            xi = si_ref[pl.ds(c * R, R), :].astype(F32)
            g1r = gre[pl.ds(c * R + 9, R), :]
            g1i = gim[pl.ds(c * R + 9, R), :]
            dar = dar + _rowsum(g1r * xr + g1i * xi)
            dai = dai + _rowsum(g1i * xr - g1r * xi)
        dar_ref[...] = dar
        dai_ref[...] = dai

    lane, col, wb, wc, vec = _scan_specs(T)
    return _pcall(
        body, name="s5_scan_bwd", grid=(LANES // W,),
        in_specs=[lane, lane, lane, lane, col, col, wb, wb, vec, vec],
        out_specs=[col, wb, wb, wc, wc, vec, vec],
        out_shape=[jax.ShapeDtypeStruct((T, SSM), F32),
                   jax.ShapeDtypeStruct((LANES // W, 128, W), F32), jax.ShapeDtypeStruct((LANES // W, 128, W), F32),
                   jax.ShapeDtypeStruct((LANES // W, W, 128), F32), jax.ShapeDtypeStruct((LANES // W, W, 128), F32),
                   jax.ShapeDtypeStruct((1, LANES), F32), jax.ShapeDtypeStruct((1, LANES), F32)],
        scratch=[pltpu.VMEM((T + 16, W), F32)] * 2, vmem_mb=56, comm=comm,
        operands=(gs_re, gs_im, st_re, st_im, su_b, ds_b, wb_re, wb_im, a_re, a_im))


def _mix_bwd_a(dyc_b, w_co4, pc, z_b, conv_w, dsu_ssm, du_dir, dproj, dres, w_mix4, tm, comm=()):
    T = dres.shape[0]
    nt = T // tm

    def body(dyc_ref, wco_ref, pc_ref, halo_ref, z_ref, cw_ref, dsu_ref, dud_ref, dpin_ref, dres_ref, w_ref,
             dp_ref, dx_ref, dcw_ref, dcb_ref, dzbuf, qbuf):
        i = pl.program_id(0)
        ii = nt - 1 - i

        @pl.when(i == 0)
        def _():
            dzbuf[pl.ds(tm, 8), :] = jnp.zeros((8, CONV), F32)

        dyc = dyc_ref[...]
        dyin = (_nt(dyc[:, 0:256], wco_ref[0]) + _nt(dyc[:, 256:512], wco_ref[1])
                + _nt(dyc[:, 512:768], wco_ref[2]) + _nt(dyc[:, 768:1024], wco_ref[3]))
        cbv = pc_ref[:, 0:CONV].astype(F32)
        ccv = pc_ref[:, CONV:2 * CONV].astype(F32)
        chv = pc_ref[:, 2 * CONV:3 * CONV].astype(F32)
        dcbv = dyin * z_ref[...].astype(F32)
        dz = dyin * cbv
        dzbuf[pl.ds(0, tm), :] = dz
        cw = cw_ref[...]
        dq = cw[2:3] * dz + cw[1:2] * dzbuf[pl.ds(1, tm), :] + cw[0:1] * dzbuf[pl.ds(2, tm), :]
        dzbuf[pl.ds(tm, 8), :] = dz[0:8]
        q = ccv * chv
        hq = halo_ref[:, CONV:2 * CONV].astype(F32) * halo_ref[:, 2 * CONV:3 * CONV].astype(F32)
        qbuf[pl.ds(0, 8), :] = jnp.where(ii > 0, hq, jnp.zeros_like(hq))
        qbuf[pl.ds(8, tm), :] = q
        pw = jnp.concatenate([_rowsum(dz * qbuf[pl.ds(6, tm), :]), _rowsum(dz * qbuf[pl.ds(7, tm), :]),
                              _rowsum(dz * q), jnp.zeros((5, CONV), F32)], axis=0)
        pbias = _rowsum(dz)

        @pl.when(i == 0)
        def _():
            dcw_ref[...] = pw
            dcb_ref[...] = pbias

        @pl.when(i > 0)
        def _():
            dcw_ref[...] += pw
            dcb_ref[...] += pbias

        dp0 = jnp.concatenate([dcbv, dq * chv], axis=1).astype(BF16)
        dp1 = jnp.concatenate([dq * ccv, dsu_ref[...] + dud_ref[...]], axis=1).astype(BF16)
        dp_ref[:, 0:D] = dp0
        dp_ref[:, D:2 * D] = dp1
        dx_ref[...] = (dres_ref[...] + _nt(dp0, w_ref[0]) + _nt(dp1, w_ref[1])
                       + _nt(dpin_ref[:, 0:D], w_ref[2]) + _nt(dpin_ref[:, D:2 * D], w_ref[3]))

    def tok(n):
        return pl.BlockSpec((tm, n), lambda i: (nt - 1 - i, 0))

    def full(shape):
        return pl.BlockSpec(shape, lambda i: (0,) * len(shape))

    halo = pl.BlockSpec((8, 3 * CONV), lambda i: (jnp.maximum((nt - 1 - i) * (tm // 8) - 1, 0), 0))
    return _pcall(
        body, name="mix_bwd_a", grid=(nt,),
        in_specs=[tok(D), full((4, CONV, 256)), tok(3 * CONV), halo, tok(CONV), full((3, CONV)),
                  tok(SSM), tok(SSM), pl.BlockSpec((tm, 2 * D), lambda i: (nt - 1 - i, 1)), tok(D),
                  full((4, D, D))],
        out_specs=[pl.BlockSpec((tm, 2 * D), lambda i: (nt - 1 - i, 0)), tok(D), full((8, CONV)), full((1, CONV))],
        out_shape=[jax.ShapeDtypeStruct((T, 4 * D), BF16), jax.ShapeDtypeStruct((T, D), F32),
                   jax.ShapeDtypeStruct((8, CONV), F32), jax.ShapeDtypeStruct((1, CONV), F32)],
        scratch=[pltpu.VMEM((tm + 8, CONV), F32), pltpu.VMEM((tm + 8, CONV), F32)],
        aliases={8: 0}, vmem_mb=56, comm=comm,
        operands=(dyc_b, w_co4, pc, pc, z_b, conv_w, dsu_ssm, du_dir, dproj, dres, w_mix4))


def _zoh(lam_re, lam_im, log_step, b_re, b_im):
    dt = jnp.exp(log_step)[:, None]
    mag = jnp.exp(lam_re * dt)
    abr, abi = mag * jnp.cos(lam_im * dt), mag * jnp.sin(lam_im * dt)
    nr, ni = abr - 1.0, abi
    den = lam_re * lam_re + lam_im * lam_im
    cr = (nr * lam_re + ni * lam_im) / den
    ci = (ni * lam_re - nr * lam_im) / den
    bbr = cr[..., None] * b_re - ci[..., None] * b_im
    bbi = cr[..., None] * b_im + ci[..., None] * b_re
    return abr, abi, bbr, bbi


_WB_MASK = (np.arange(8)[None, :, None]
            == SCAN_GR * np.arange(SCAN_PER)[:, None, None] + np.arange(SCAN_GR)[None, None, :]).astype(np.float32)
_EYE8 = np.eye(8, dtype=np.float32)


def _wb_blocks(bb):
    bt = bb.transpose(0, 2, 1).reshape(4, 1, 8, 16, 1, STATE)
    full = bt * _WB_MASK[None, :, :, None, :, None]
    return full.reshape(LANES // SCAN_W, 128, SCAN_W).astype(BF16)


def _wc_blocks(cc):
    ct = cc.transpose(0, 2, 1).reshape(4, 8, STATE, 1, 16)
    full = ct * _EYE8[None, :, None, :, None]
    return full.reshape(4, 512, 128).astype(BF16)


def _wb_diag(dwb):
    d6 = dwb.reshape(4, SCAN_PER, 8, 16, SCAN_GR, STATE) * _WB_MASK[None, :, :, None, :, None]
    return d6.sum(axis=(1, 4)).reshape(GROUPS, 16, STATE).transpose(0, 2, 1)


def _wc_diag(dwc):
    mask = _WB_MASK.transpose(0, 2, 1)
    d6 = dwc.reshape(4, SCAN_PER, SCAN_GR, STATE, 8, 16) * mask[None, :, :, None, :, None]
    return d6.sum(axis=4).reshape(GROUPS, STATE, 16).transpose(0, 2, 1)


def _where():
    x, y, c = lax.axis_index("x"), lax.axis_index("y"), lax.axis_index("c")
    return x, y, c, 2 * x + y


def _chip_dev(k, c):
    return (k // 2, k % 2, c)


def _slot_cast(meidx, w, dtype, name, token=()):
    R, C = w.shape
    tr = _row_tile(R)

    def body(m_ref, w_ref, *rest):
        rest[-1][...] = w_ref[...].astype(dtype)

    gs = pltpu.PrefetchScalarGridSpec(
        num_scalar_prefetch=1, grid=(R // tr,),
        in_specs=[pl.BlockSpec((tr, C), lambda i, m: (i, 0))] + [pl.BlockSpec((8, 128), lambda i, m: (0, 0))] * len(token),
        out_specs=pl.BlockSpec((None, tr, C), lambda i, m: (m[0], i, 0)))
    return pl.pallas_call(
        body, name=name, grid_spec=gs, out_shape=_hbm_out(jax.ShapeDtypeStruct((4, R, C), dtype)),
        compiler_params=_cp(32, 1),
    )(meidx, *_hbm(w), *token)


def _gather_ici_payload(bufs):
    def copies(ins, lnd, ss, rs):
        x, y, c, me = _where()
        cps = []
        for w, b in enumerate(bufs):
            h = b.shape[1] // 2
            mine = lnd[w].at[me, pl.ds(c * h, h)]
            for s in range(3):
                k = (me + 1 + s) % 4
                cps.append(pltpu.make_async_remote_copy(
                    src_ref=mine, dst_ref=mine, send_sem=ss.at[3 * w + s], recv_sem=rs.at[3 * w + s],
                    device_id=_chip_dev(k, c), device_id_type=MESH))
        return cps

    p = _sym_payload([], [jax.ShapeDtypeStruct(b.shape, b.dtype) for b in bufs], copies, 3 * len(bufs))
    p.lands = list(bufs)
    return p


def _gather_pass_payload(bufs):
    def copies(ins, outs, ss, rs):
        x, y, c, me = _where()
        cps = []
        for w, b in enumerate(bufs):
            h = b.shape[1] // 2
            for s in range(3):
                j = (me + 1 + s) % 4
                cps.append(pltpu.make_async_remote_copy(
                    src_ref=ins[w].at[j, pl.ds(c * h, h)], dst_ref=outs[w].at[j, pl.ds(c * h, h)],
                    send_sem=ss.at[3 * w + s], recv_sem=rs.at[3 * w + s], device_id=(x, y, 1 - c),
                    device_id_type=MESH))
        return cps

    p = _sym_payload(bufs, [jax.ShapeDtypeStruct(b.shape, b.dtype) for b in bufs], copies, 3 * len(bufs))
    p.aliases = {w: w for w in range(len(bufs))}
    return p


def _gather_payload(bufs):
    n = len(bufs)

    def half(ref, w, k, cc):
        h = bufs[w].shape[1] // 2
        return ref.at[k, pl.ds(cc * h, h)]

    def ici(ins, outs, sems, w, s):
        x, y, c, me = _where()
        k = (me + 1 + s) % 4
        return pltpu.make_async_remote_copy(
            src_ref=half(ins[w], w, me, c), dst_ref=half(outs[w], w, me, c), send_sem=sems[0].at[3 * w + s],
            recv_sem=sems[1].at[3 * w + s], device_id=_chip_dev(k, c), device_id_type=MESH)

    def landed(outs, sems, w, s):
        x, y, c, me = _where()
        j = (me + 3 - s) % 4
        return pltpu.make_async_remote_copy(
            src_ref=half(outs[w], w, j, c), dst_ref=half(outs[w], w, j, c), send_sem=sems[0].at[3 * w + s],
            recv_sem=sems[1].at[3 * w + s], device_id=(x, y, 1 - c), device_id_type=MESH)

    def passed(outs, sems, w, s, cc):
        x, y, c, me = _where()
        j = (me + 3 - s) % 4
        return pltpu.make_async_remote_copy(
            src_ref=half(outs[w], w, j, cc), dst_ref=half(outs[w], w, j, cc), send_sem=sems[2].at[3 * w + s],
            recv_sem=sems[3].at[3 * w + s], device_id=(x, y, 1 - c), device_id_type=MESH)

    pairs = [(w, s) for w in range(n) for s in range(3)]

    def start(ins, outs, sems):
        for w, s in pairs:
            ici(ins, outs, sems, w, s).start()

    def finish(ins, outs, sems):
        _, _, c, _ = _where()
        for w, s in pairs:
            landed(outs, sems, w, s).wait_recv()
            passed(outs, sems, w, s, c).start()
        for w, s in pairs:
            passed(outs, sems, w, s, 1 - c).wait_recv()
        for w, s in pairs:
            ici(ins, outs, sems, w, s).wait_send()
            passed(outs, sems, w, s, c).wait_send()

    return _Payload(bufs, [jax.ShapeDtypeStruct(b.shape, b.dtype) for b in bufs], {w: w for w in range(n)},
                    [pltpu.SemaphoreType.DMA((3 * n,))] * 4, start, finish)


def _sym_payload(operands, outs, copies, n_copies):
    def start(ins, outs_, sems):
        for cp in copies(ins, outs_, sems[0], sems[1]):
            cp.start()

    def finish(ins, outs_, sems):
        for cp in copies(ins, outs_, sems[0], sems[1]):
            cp.wait()

    p = _Payload(operands, outs, {}, [pltpu.SemaphoreType.DMA((n_copies,))] * 2, start, finish)
    p.copies, p.n_copies = copies, n_copies
    return p


def _swap_payload(g4s):
    def copies(ins, outs, ss, rs):
        x, y, c, me = _where()
        cps = []
        for w, g in enumerate(g4s):
            h = g.shape[1] // 2
            cps.append(pltpu.make_async_remote_copy(
                src_ref=ins[w].at[:, pl.ds((1 - c) * h, h)], dst_ref=outs[w], send_sem=ss.at[w],
                recv_sem=rs.at[w], device_id=(x, y, 1 - c), device_id_type=MESH))
        return cps

    outs = [jax.ShapeDtypeStruct((4, g.shape[1] // 2, g.shape[2]), g.dtype) for g in g4s]
    return _sym_payload(g4s, outs, copies, len(g4s))


def _exchange_payload(pbs):
    def copies(ins, outs, ss, rs):
        x, y, c, me = _where()
        cps = []
        for w in range(len(pbs)):
            for s in range(3):
                k = (me + 1 + s) % 4
                cps.append(pltpu.make_async_remote_copy(
                    src_ref=ins[w].at[k], dst_ref=outs[w].at[2 - s], send_sem=ss.at[3 * w + s],
                    recv_sem=rs.at[3 * w + s], device_id=_chip_dev(k, c), device_id_type=MESH))
        return cps

    outs = [jax.ShapeDtypeStruct((3,) + p.shape[1:], p.dtype) for p in pbs]
    return _sym_payload(pbs, outs, copies, 3 * len(pbs))


HBM_REF = pl.BlockSpec(memory_space=pltpu.HBM)
SEM_REF = pl.BlockSpec(memory_space=pltpu.SEMAPHORE)
DATAFLOW = pltpu.SideEffectType.DATAFLOW_SIDE_EFFECTING


class _SemList:
    def __init__(self, refs):
        self.refs = refs

    @property
    def at(self):
        return self.refs


def _split_start(p, name):
    n_in, n_out, nc = len(p.operands), len(p.outs), p.n_copies
    lands = getattr(p, "lands", None) or [lax.empty(s.shape, s.dtype) for s in p.outs]

    def body(*refs):
        ins, lnd = refs[:n_in], refs[n_in:n_in + n_out]
        sems = refs[n_in + n_out:n_in + n_out + 2 * nc]
        for cp in p.copies(ins, lnd, _SemList(sems[:nc]), _SemList(sems[nc:])):
            cp.start()
        refs[-1][...] = jnp.zeros((8, 128), F32)

    res = pl.pallas_call(
        body, name=name,
        in_specs=[HBM_REF] * (n_in + n_out),
        out_specs=[SEM_REF] * (2 * nc) + [HBM_REF] * (n_in + n_out) + [VMEM_FULL],
        out_shape=([pltpu.SemaphoreType.DMA(())] * (2 * nc) + _hbm_out(p.operands) + _hbm_out(lands)
                   + [jax.ShapeDtypeStruct((8, 128), F32)]),
        input_output_aliases={i: 2 * nc + i for i in range(n_in + n_out)},
        compiler_params=pltpu.CompilerParams(has_side_effects=DATAFLOW),
    )(*_hbm(*p.operands, *lands))
    k = 2 * nc
    return list(res[:k]), list(res[k:k + n_in]), list(res[k + n_in:k + n_in + n_out]), res[-1]


def _split_wait(p, handle, after, name):
    sems, srcs, lands, _ = handle
    n_in, n_out, nc = len(srcs), len(lands), p.n_copies

    def body(*refs):
        ins, lnd = refs[:n_in], refs[n_in:n_in + n_out]
        sm = refs[n_in + n_out:n_in + n_out + 2 * nc]
        for cp in p.copies(ins, lnd, _SemList(sm[:nc]), _SemList(sm[nc:])):
            cp.wait_send()
            cp.wait_recv()

    res = pl.pallas_call(
        body, name=name,
        in_specs=[HBM_REF] * (n_in + n_out) + [SEM_REF] * (2 * nc) + [ANY] * len(after),
        out_specs=[HBM_REF] * (n_in + n_out), out_shape=_hbm_out(srcs) + _hbm_out(lands),
        input_output_aliases={i: i for i in range(n_in + n_out)},
        compiler_params=pltpu.CompilerParams(has_side_effects=DATAFLOW),
    )(*srcs, *lands, *sems, *after)
    return list(res[:n_in]), list(res[n_in:])


def _join_payload(halves):
    def copies(ins, outs, ss, rs):
        x, y, c, me = _where()
        return [pltpu.make_async_remote_copy(
            src_ref=ins[w], dst_ref=outs[w], send_sem=ss.at[w], recv_sem=rs.at[w],
            device_id=(x, y, 1 - c), device_id_type=MESH) for w in range(len(halves))]

    outs = [jax.ShapeDtypeStruct(a.shape, a.dtype) for a in halves]
    return _sym_payload(halves, outs, copies, len(halves))


def _allgather_payload(v):
    def copies(ins, outs, ss, rs):
        x, y, c, me = _where()
        lin = 4 * x + 2 * y + c
        cps = []
        for o in range(1, 8):
            t = (lin + o) % 8
            cps.append(pltpu.make_async_remote_copy(
                src_ref=ins[0], dst_ref=outs[0].at[lin], send_sem=ss.at[o - 1], recv_sem=rs.at[o - 1],
                device_id=(t // 4, (t // 2) % 2, t % 2), device_id_type=MESH))
        return cps

    p = _sym_payload([v], [jax.ShapeDtypeStruct((8,) + v.shape, v.dtype)], copies, 7)
    x, y, c, _ = _where()
    p.lands = [lax.dynamic_update_slice(jnp.zeros((8,) + v.shape, v.dtype), v[None], (4 * x + 2 * y + c, 0, 0))]
    return p


def _sum8(buf, token):
    _, P, C = buf.shape

    def body(b_ref, t_ref, o_ref):
        acc = b_ref[0]
        for d in range(1, 8):
            acc = acc + b_ref[d]
        o_ref[...] = acc

    return pl.pallas_call(
        body, name="sum8", in_specs=[VMEM_FULL, VMEM_FULL], out_specs=VMEM_FULL,
        out_shape=jax.ShapeDtypeStruct((P, C), F32),
        compiler_params=pltpu.CompilerParams(vmem_limit_bytes=32 << 20),
    )(buf, token)


def _row_tile(h):
    for t in (256, 176, 128, 64, 32, 16, 8):
        if h % t == 0:
            return t
    raise ValueError(h)


def _pair_sum(cmidx, g4, got, name):
    _, R, C = g4.shape
    h = R // 2
    th = _row_tile(h)

    def body(cm_ref, a_ref, b_ref, o_ref, ob_ref):
        sm = a_ref[...] + b_ref[...]
        ob_ref[...] = sm.astype(BF16)

        @pl.when(pl.program_id(1) == cm_ref[1])
        def _():
            o_ref[...] = sm

    blk = pl.BlockSpec((None, th, C), lambda i, k, cm: (k, i, 0))
    gs = pltpu.PrefetchScalarGridSpec(
        num_scalar_prefetch=1, grid=(h // th, 4),
        in_specs=[pl.BlockSpec((None, None, th, C), lambda i, k, cm: (k, cm[0], i, 0)), blk],
        out_specs=[pl.BlockSpec((th, C), lambda i, k, cm: (i, 0)), blk])
    return pl.pallas_call(
        body, name=name, grid_spec=gs,
        out_shape=_hbm_out([jax.ShapeDtypeStruct((h, C), F32), jax.ShapeDtypeStruct((4, h, C), BF16)]),
        compiler_params=_cp(32, 2),
    )(cmidx, *_hbm(g4.reshape(4, 2, h, C), got))


def _chip_sum(own, got, name):
    h, C = own.shape
    th = _row_tile(h)

    def body(a_ref, b_ref, o_ref):
        o_ref[...] = ((a_ref[...] + b_ref[0].astype(F32)) + b_ref[1].astype(F32)) + b_ref[2].astype(F32)

    return pl.pallas_call(
        body, name=name, grid=(h // th,),
        in_specs=[pl.BlockSpec((th, C), lambda i: (i, 0)), pl.BlockSpec((3, th, C), lambda i: (0, i, 0))],
        out_specs=pl.BlockSpec((th, C), lambda i: (i, 0)),
        out_shape=_hbm_out(jax.ShapeDtypeStruct((h, C), F32)),
        compiler_params=_cp(32, 1),
    )(*_hbm(own, got))


def _adamw_math(w, g, m, v):
    m2 = B1 * m + (1.0 - B1) * g
    v2 = B2 * v + (1.0 - B2) * (g * g)
    m_hat = m2 / (1.0 - B1 ** STEP)
    v_hat = v2 / (1.0 - B2 ** STEP)
    delta = -LR * (m_hat / (jnp.sqrt(v_hat) + EPS) + WD * w)
    return delta, m2, v2


def _adamw_pair(cidx, w, mine, theirs, m, v, token, name):
    R, C = w.shape
    h = R // 2
    tr = _row_tile(h)
    nh = h // tr

    def body(c_ref, w_ref, a_ref, b_ref, m_ref, v_ref, t_ref, g_ref, d_ref, mo_ref, vo_ref):
        own = (pl.program_id(0) // nh) == c_ref[0]
        g = jnp.where(own, a_ref[...], b_ref[...])
        d, m2, v2 = _adamw_math(w_ref[...], g, m_ref[...], v_ref[...])
        g_ref[...] = g
        d_ref[...] = d
        mo_ref[...] = m2
        vo_ref[...] = v2

    blk = pl.BlockSpec((tr, C), lambda i, c: (i, 0))
    mine_blk = pl.BlockSpec((tr, C), lambda i, c: (jnp.clip(i - c[0] * nh, 0, nh - 1), 0))
    theirs_blk = pl.BlockSpec((tr, C), lambda i, c: (jnp.clip(i - (1 - c[0]) * nh, 0, nh - 1), 0))
    gs = pltpu.PrefetchScalarGridSpec(
        num_scalar_prefetch=1, grid=(R // tr,),
        in_specs=[blk, mine_blk, theirs_blk, blk, blk, pl.BlockSpec((8, 128), lambda i, c: (0, 0))],
        out_specs=[blk] * 4)
    return pl.pallas_call(
        body, name=name, grid_spec=gs, out_shape=_hbm_out([jax.ShapeDtypeStruct((R, C), F32)] * 4),
        compiler_params=_cp(32, 1),
    )(cidx, *_hbm(w, mine, theirs, m, v), token)


def _adamw(w, g, m, v, name):
    R, C = w.shape
    tr = _row_tile(R)

    def body(w_ref, g_ref, m_ref, v_ref, d_ref, mo_ref, vo_ref):
        d, m2, v2 = _adamw_math(w_ref[...], g_ref[...], m_ref[...], v_ref[...])
        d_ref[...] = d
        mo_ref[...] = m2
        vo_ref[...] = v2

    blk = pl.BlockSpec((tr, C), lambda i: (i, 0))
    return pl.pallas_call(
        body, name=name, grid=(R // tr,), in_specs=[blk] * 4, out_specs=[blk] * 3,
        out_shape=_hbm_out([jax.ShapeDtypeStruct((R, C), F32)] * 3),
        compiler_params=_cp(32, 1),
    )(*_hbm(w, g, m, v))


def _pack(arrs):
    flat = jnp.concatenate([a.reshape(-1).astype(F32) for a in arrs])
    rows = -(-flat.shape[0] // 1024)
    rows = -(-rows // 8) * 8
    return jnp.pad(flat, (0, rows * 1024 - flat.shape[0])).reshape(rows, 1024)


def _unpack(packed, shapes):
    flat = packed.reshape(-1)
    out, off = [], 0
    for s in shapes:
        n = math.prod(s)
        out.append(flat[off:off + n].reshape(s))
        off += n
    return out


BIG = ["ffn1_w_in", "ffn1_w_out", "mix_w_in", "conv_w_out", "ssm_w_glu", "mix_w_out",
       "ffn2_w_in", "ffn2_w_out", "ple_w_in", "ple_w_gate"]
SMALL = ["ln1_g", "ln1_b", "conv_w", "conv_b", "ssm_lam_re", "ssm_lam_im", "ssm_log_step", "ssm_b_re", "ssm_b_im",
         "ssm_c_re", "ssm_c_im", "ssm_d", "ln2_g", "ln2_b", "ln3_g", "ln3_b", "ln4_g", "ln4_b"]
WEIGHTS = ["ffn1_w_in", "ffn1_w_out", "ln1_g", "ln1_b", "mix_w_in", "conv_w", "conv_b", "conv_w_out",
           "ssm_lam_re", "ssm_lam_im", "ssm_log_step", "ssm_b_re", "ssm_b_im", "ssm_c_re", "ssm_c_im", "ssm_d",
           "ssm_w_glu", "mix_w_out", "ln2_g", "ln2_b", "ffn2_w_in", "ffn2_w_out", "ln3_g", "ln3_b",
           "ple_w_in", "ple_w_gate", "ln4_g", "ln4_b"]


def _s5_operands(sp):
    abr, abi, bbr, bbi = _zoh(sp["ssm_lam_re"], sp["ssm_lam_im"], sp["ssm_log_step"], sp["ssm_b_re"], sp["ssm_b_im"])
    return (_wb_blocks(bbr), _wb_blocks(bbi), _wc_blocks(sp["ssm_c_re"]), _wc_blocks(-sp["ssm_c_im"]),
            abr.reshape(1, LANES), abi.reshape(1, LANES), sp["ssm_d"].reshape(1, SSM))


def _local_step(x, p, target, sp, ops, sched):
    W = sched.W
    wb_re, wb_im, wc_re4, wc_im4, a_re, a_im, dvec = ops
    tm = TOKEN_TILE

    def run(fn, name, *args, **kw):
        outs, got = fn(*args, comm=sched.carry(name), **kw)
        sched.landed(name, got)
        sched.done[name] = outs[0]
        return outs

    def dw(name, wname, a, b, tk, tn, shape4, shard_cols=None, interleaved=False):
        out, got = _mm_tn(a, b, tk, tn, name, shard_cols=shard_cols, interleaved=interleaved,
                          comm=sched.carry(name))
        sched.landed(name, got)
        sched.done[name] = out
        sched.grad(wname, out.reshape(shape4))

    h1, r1, x1, x1b, xb = run(_ffn_fwd, "ffn1_fwd", x, W["ffn1_w_in"], W["ffn1_w_out"].reshape(2, FFH, D),
                              sp["ln1_g"], sp["ln1_b"], tm, "ffn1_fwd")
    conv_w = W["conv_w"][:, 0:3, :].transpose(1, 0, 2).reshape(3, CONV)
    pc, z_b, yin_b, su, su_b, g_conv, g_ssm, y_conv = run(
        _mix_fwd_a, "mix_fwd_a", x1b, W["mix_w_in"], conv_w, sp["conv_b"], W["conv_w_out"], tm)
    st_re, st_im = run(_s5_scan_fwd, "s5_scan_fwd", su_b, wb_re, wb_im, a_re, a_im)
    w_mo = W["mix_w_out"].reshape(D, D)
    s, sg_b, ga, gb, merged_b, r2, x2 = run(
        _mix_fwd_b, "mix_fwd_b", st_re, st_im, wc_re4, wc_im4, su, dvec, W["ssm_w_glu"], g_conv, g_ssm, y_conv,
        w_mo, x1, sp["ln2_g"], sp["ln2_b"], tm)
    w2o2 = W["ffn2_w_out"].reshape(2, FFH, D)
    h2, r3, x3, x3b, x2b = run(_ffn_fwd, "ffn2_fwd", x2, W["ffn2_w_in"], w2o2, sp["ln3_g"], sp["ln3_b"], tm,
                               "ffn2_fwd")
    loss_part, dx3, p_b, dpw_b, dgt_b, dg4, db4 = _ple_loss(
        x3, x3b, p, W["ple_w_in"], W["ple_w_gate"].reshape(D, D), sp["ln4_g"], sp["ln4_b"], target, tm)

    dw("dw_ple_gate", "ple_w_gate", x3b, dgt_b, 512, 1024, (4, 256, D))
    dw("dw_ple_in", "ple_w_in", p_b, dpw_b, 256, 256, (4, 256, 256), shard_cols=256)
    dx2, dh2, a2_b, df2_b, dg3, db3 = run(_ffn_bwd, "ffn2_bwd", dx3, r3, sp["ln3_g"], h2, W["ffn2_w_in"], w2o2,
                                          tm, "ffn2_bwd")
    dw("dw_ffn2_in", "ffn2_w_in", x2b, dh2, 512, FFH, (4, D, FFH), shard_cols=FFH, interleaved=True)
    dw("dw_ffn2_out", "ffn2_w_out", a2_b, df2_b, FFH, 1024, (4, FF // 4, D))
    (dres, dmix_b, dgl_b, ds_b, du_dir, gs_re, gs_im, dyc_b, dproj, dg2, db2, dd) = run(
        _mix_bwd_b, "mix_bwd_b", dx2, r2, sp["ln2_g"], w_mo, g_conv, g_ssm, y_conv, ga, gb, s, su, dvec,
        W["ssm_w_glu"], wc_re4, wc_im4, tm)
    dw("dw_mix_out", "mix_w_out", merged_b, dmix_b, 512, 1024, (4, 256, D))
    dw("dw_glu", "ssm_w_glu", sg_b, dgl_b, 512, 512, (4, SSM, 512), shard_cols=512)
    dsu_ssm, dwb_re, dwb_im, dwc_re, dwc_im, da_re, da_im = run(
        _s5_scan_bwd, "s5_scan_bwd", gs_re, gs_im, st_re, st_im, su_b, ds_b, wb_re, wb_im, a_re, a_im)
    dw("dw_conv_out", "conv_w_out", yin_b, dyc_b, 512, 256, (4, CONV, 256), shard_cols=256)
    dproj, dx1, dcw8, dcb = run(_mix_bwd_a, "mix_bwd_a", dyc_b, W["conv_w_out"], pc, z_b, conv_w, dsu_ssm,
                                du_dir, dproj, dres, W["mix_w_in"], tm)
    dw("dw_mix_in", "mix_w_in", x1b, dproj, 512, 1024, (4, D, D), shard_cols=1024)
    dx0, dh1, a1_b, df1_b, dg1, db1 = run(_ffn_bwd, "ffn1_bwd", dx1, r1, sp["ln1_g"], h1, W["ffn1_w_in"],
                                          W["ffn1_w_out"].reshape(2, FFH, D), tm, "ffn1_bwd")
    sched.small(dict(
        ln1_g=dg1, ln1_b=db1, ln2_g=dg2, ln2_b=db2, ln3_g=dg3, ln3_b=db3, ln4_g=dg4, ln4_b=db4,
        conv_w=dcw8[0:3], conv_b=dcb,
        a_re=da_re.reshape(GROUPS, STATE), a_im=da_im.reshape(GROUPS, STATE),
        bb_re=_wb_diag(dwb_re), bb_im=_wb_diag(dwb_im),
        ssm_c_re=_wc_diag(dwc_re), ssm_c_im=-_wc_diag(dwc_im), ssm_d=dd.reshape(GROUPS, 16),
        loss=loss_part[0:1, 0]))
    dw("dw_ffn1_in", "ffn1_w_in", xb, dh1, 512, FFH, (4, D, FFH), shard_cols=FFH, interleaved=True)
    dw("dw_ffn1_out", "ffn1_w_out", a1_b, df1_b, FFH, 1024, (4, FF // 4, D))
    return loss_part[0, 0], dx0


RAW_ORDER = ["ln1_g", "ln1_b", "ln2_g", "ln2_b", "ln3_g", "ln3_b", "ln4_g", "ln4_b", "conv_w", "conv_b",
             "a_re", "a_im", "bb_re", "bb_im", "ssm_c_re", "ssm_c_im", "ssm_d", "loss"]

GATHER_FIRST = ["ffn1_w_in", "ffn1_w_out"]
GATHER_AT = {"ffn1_fwd": ["mix_w_in", "conv_w_out", "conv_w"], "mix_fwd_a": ["ssm_w_glu", "mix_w_out"],
             "s5_scan_fwd": ["ffn2_w_in"], "mix_fwd_b": ["ffn2_w_out"], "ffn2_fwd": ["ple_w_in", "ple_w_gate"]}
REDUCE_GROUP = {"ffn2": ["ple_w_gate", "ple_w_in", "ffn2_w_in", "ffn2_w_out"],
                "mix": ["mix_w_out", "ssm_w_glu", "conv_w_out", "mix_w_in"], "ffn1": ["ffn1_w_in", "ffn1_w_out"]}
REDUCE_AT = {"mix_bwd_b": [("swap", "ffn2")], "mix_bwd_a": [("join", "ffn2")]}
BEGIN_AT = {"dw_mix_out": [("exchange", "ffn2")], "ffn1_bwd": [("swap", "mix")],
            "dw_ffn1_in": [("small", None), ("exchange", "mix")]}
BEHIND = {"dw_glu": [("exchange", "ffn2")], "s5_scan_bwd": [("exchange", "ffn2")]}
END_AT = {"mix_bwd_a": [("exchange", "ffn2", ["dw_mix_out", "dw_glu", "s5_scan_bwd"])],
          "dw_ffn1_in": [("swap", "mix", ["ffn1_bwd"])]}
LAST_GROUP = "ffn1"


class _Sched:
    def __init__(self, cmidx):
        self.bufs, self.cmidx = {}, cmidx
        self.W, self.G, self.raw, self.small_buf = {}, {}, None, None
        self.got1, self.p32, self.pbf, self.got2, self.half, self.theirs = {}, {}, {}, {}, {}, {}
        self._open, self._split, self.done = [], {}, {}

    def first_begin(self, bufs):
        self.bufs.update(bufs)
        p = _gather_ici_payload([bufs[n] for n in GATHER_FIRST])
        self._first = (p, _split_start(p, "gather_first_start"))
        return self._first[1][3]

    def first_end(self, bufs, after):
        self.bufs.update(bufs)
        p, handle = self._first
        _, landed = _split_wait(p, handle, after, "gather_first_wait")
        (outs,) = _comm_call("gather_first_pass", [_gather_pass_payload(landed)])
        self.W.update(zip(GATHER_FIRST, outs))

    def _payload(self, stage, key):
        if stage == "gather":
            return _gather_payload([self.bufs[n] for n in key])
        if stage == "small":
            return _allgather_payload(_pack([self.raw[k] for k in RAW_ORDER]))
        names = REDUCE_GROUP[key]
        if stage == "swap":
            return _swap_payload([self.G[n] for n in names])
        if stage == "exchange":
            for n in names:
                self.p32[n], self.pbf[n] = _pair_sum(self.cmidx, self.G[n], self.got1[n], "pair_sum_" + n)
            return _exchange_payload([self.pbf[n] for n in names])
        for n in names:
            self.half[n] = _chip_sum(self.p32[n], self.got2[n], "chip_sum_" + n)
        return _join_payload([self.half[n] for n in names])

    def _store(self, stages, got):
        for (stage, key), outs in zip(stages, got):
            if stage == "gather":
                self.W.update(zip(key, outs))
            elif stage == "small":
                self.small_buf = outs[0]
            else:
                {"swap": self.got1, "exchange": self.got2, "join": self.theirs}[stage].update(
                    zip(REDUCE_GROUP[key], outs))

    def _standalone(self, name, stages):
        self._store(stages, _comm_call(name, [self._payload(s, k) for s, k in stages]))

    def carry(self, name):
        for stage, key, behind in END_AT.get(name, []):
            self._end(stage, key, [self.done[b] for b in behind])
        tokens = [self._begin(stage, key) for stage, key in BEGIN_AT.get(name, [])]
        tokens += [self._split[sk][1][3] for sk in BEHIND.get(name, [])]
        self._open = [("gather", GATHER_AT[name])] if name in GATHER_AT else []
        self._open += REDUCE_AT.get(name, [])
        comm = [self._payload(s, k) for s, k in self._open]
        if tokens:
            comm.append(_Payload(tokens, [], {}, [], lambda *a: None, lambda *a: None))
        return tuple(comm)

    def landed(self, name, got):
        self._store(self._open, got)

    def grad(self, name, g4):
        self.G[name] = g4

    def small(self, raw):
        self.raw = raw

    def _begin(self, stage, key):
        p = self._payload(stage, key)
        self._split[stage, key] = (p, _split_start(p, "%s_%s_start" % (stage, key)))
        return self._split[stage, key][1][3]

    def _end(self, stage, key, after):
        p, handle = self._split.pop((stage, key))
        srcs, lands = _split_wait(p, handle, after, "%s_%s_wait" % (stage, key))
        if stage == "swap":
            self.G.update(zip(REDUCE_GROUP[key], srcs))
        self._store([(stage, key)], [lands])

    def tail_begin(self):
        return self._begin("swap", LAST_GROUP)

    def tail_mid(self, after):
        self._end("swap", LAST_GROUP, after)
        token = self._begin("exchange", LAST_GROUP)
        self._end("small", None, [token])
        self._end("exchange", "mix", [token])
        self._standalone("reduce_tail_join_mix", [("join", "mix")])
        return token

    def tail_end(self, after):
        self._end("exchange", LAST_GROUP, after)
        self._standalone("reduce_tail_join", [("join", LAST_GROUP)])


def _small_grads(raw_sum, sp):
    _, vjp = jax.vjp(_zoh, sp["ssm_lam_re"], sp["ssm_lam_im"], sp["ssm_log_step"], sp["ssm_b_re"], sp["ssm_b_im"])
    d_lre, d_lim, d_ls, d_bre, d_bim = vjp((raw_sum["a_re"], raw_sum["a_im"], raw_sum["bb_re"], raw_sum["bb_im"]))
    g = {k: raw_sum[k] for k in ("ln1_g", "ln1_b", "ln2_g", "ln2_b", "ln3_g", "ln3_b", "ln4_g", "ln4_b",
                                 "conv_w", "conv_b", "ssm_c_re", "ssm_c_im", "ssm_d")}
    g.update(ssm_lam_re=d_lre, ssm_lam_im=d_lim, ssm_log_step=d_ls, ssm_b_re=d_bre, ssm_b_im=d_bim)
    return g


def kernel(x, p, ffn1_w_in, ffn1_w_out, ln1_g, ln1_b, mix_w_in, conv_w, conv_b, conv_w_out, ssm_lam_re, ssm_lam_im, ssm_log_step, ssm_b_re, ssm_b_im, ssm_c_re, ssm_c_im, ssm_d, ssm_w_glu, mix_w_out, ln2_g, ln2_b, ffn2_w_in, ffn2_w_out, ln3_g, ln3_b, ple_w_in, ple_w_gate, ln4_g, ln4_b, loss_target, m_ffn1_w_in, m_ffn1_w_out, m_ln1_g, m_ln1_b, m_mix_w_in, m_conv_w, m_conv_b, m_conv_w_out, m_ssm_lam_re, m_ssm_lam_im, m_ssm_log_step, m_ssm_b_re, m_ssm_b_im, m_ssm_c_re, m_ssm_c_im, m_ssm_d, m_ssm_w_glu, m_mix_w_out, m_ln2_g, m_ln2_b, m_ffn2_w_in, m_ffn2_w_out, m_ln3_g, m_ln3_b, m_ple_w_in, m_ple_w_gate, m_ln4_g, m_ln4_b, v_ffn1_w_in, v_ffn1_w_out, v_ln1_g, v_ln1_b, v_mix_w_in, v_conv_w, v_conv_b, v_conv_w_out, v_ssm_lam_re, v_ssm_lam_im, v_ssm_log_step, v_ssm_b_re, v_ssm_b_im, v_ssm_c_re, v_ssm_c_im, v_ssm_d, v_ssm_w_glu, v_mix_w_out, v_ln2_g, v_ln2_b, v_ffn2_w_in, v_ffn2_w_out, v_ln3_g, v_ln3_b, v_ple_w_in, v_ple_w_gate, v_ln4_g, v_ln4_b):
    args = dict(locals())
    w = {n: args[n] for n in WEIGHTS}
    m = {n: args["m_" + n] for n in WEIGHTS}
    v = {n: args["v_" + n] for n in WEIGHTS}
    _, _, c, me = _where()
    cidx = jnp.stack([c, me]).astype(jnp.int32)
    meidx = jnp.reshape(me, (1,)).astype(jnp.int32)

    sched = _Sched(cidx)
    token = sched.first_begin({n: _slot_cast(meidx, w[n][0], BF16, "cast_" + n) for n in GATHER_FIRST})
    rest = {n: _slot_cast(meidx, w[n][0], BF16, "cast_" + n, (token,)) for n in BIG if n not in GATHER_FIRST}
    rest["conv_w"] = _slot_cast(meidx, jnp.pad(conv_w[0], ((0, 13), (0, 0))), F32, "cast_conv_w", (token,))
    sp = {n: (w[n] if w[n].ndim == 2 and n != "ssm_log_step" else w[n][0]) for n in SMALL if n != "conv_w"}
    ops = _s5_operands({**sp, "ssm_lam_re": sp["ssm_lam_re"] + token[0, 0]})
    sched.first_end(rest, list(rest.values()) + list(ops))
    loss_part, dx0 = _local_step(x[0], p[0, 0], loss_target[0], sp, ops, sched)
    out_g, out_d, out_m, out_v = {}, {}, {}, {}

    def big_adamw(names, token):
        for n in names:
            g, dl, mn, vn = _adamw_pair(cidx, w[n][0], sched.half[n], sched.theirs[n], m[n][0], v[n][0], token,
                                        "adamw_" + n)
            out_g[n], out_d[n], out_m[n], out_v[n] = g[None], dl[None], mn[None], vn[None]

    first = ["ple_w_gate", "ple_w_in", "ffn2_w_in"]
    big_adamw(first, sched.tail_begin())
    token = sched.tail_mid([out_v[n] for n in first])
    big_adamw(["ffn2_w_out"], token)

    raw_shapes = [sched.raw[k].shape for k in RAW_ORDER]
    raw_sum = dict(zip(RAW_ORDER, _unpack(_sum8(sched.small_buf, token), raw_shapes)))
    loss = raw_sum["loss"][0]
    sg = _small_grads(raw_sum, sp)
    sg["conv_w"] = lax.dynamic_slice_in_dim(sg["conv_w"], me * 128, 128, axis=1)
    small_shapes = [w[n].shape for n in SMALL]
    gp = _pack([sg[n] for n in SMALL])
    d_s, m_s, v_s = _adamw(_pack([w[n] for n in SMALL]), gp, _pack([m[n] for n in SMALL]),
                           _pack([v[n] for n in SMALL]), "adamw_small")

    for n, a, b_, c_, d_ in zip(SMALL, _unpack(gp, small_shapes), _unpack(d_s, small_shapes),
                                _unpack(m_s, small_shapes), _unpack(v_s, small_shapes)):
        out_g[n], out_d[n], out_m[n], out_v[n] = a, b_, c_, d_
    big_adamw(REDUCE_GROUP["mix"], token)
    sched.tail_end([d_s, out_v["ffn2_w_out"]] + [out_v[n] for n in REDUCE_GROUP["mix"]])
    big_adamw(REDUCE_GROUP[LAST_GROUP], token)

    return (loss, dx0[None], *[out_g[n] for n in WEIGHTS], *[out_d[n] for n in WEIGHTS],
            *[out_m[n] for n in WEIGHTS], *[out_v[n] for n in WEIGHTS])
```

```python
import functools
import math

import jax
import jax.numpy as jnp
import numpy as np
from jax import lax
from jax.experimental import pallas as pl
from jax.experimental.pallas import tpu as pltpu

F32, BF16 = jnp.float32, jnp.bfloat16
D = 1024
FF = 2816
FFH = FF // 2
CONV = 512
SSM = 512
GROUPS = 32
STATE = 64
LANES = GROUPS * STATE
SCAN_W = 128
SCAN_PER = 512 // SCAN_W
SCAN_GR = SCAN_W // STATE
SCAN_R = 256
TOKEN_TILE = 256
ALPHA = 2.0 ** 0.25
LN_EPS = 1e-5
GELU_C = math.sqrt(2.0 / math.pi)
B1, B2, LR, EPS, WD, STEP = 0.9, 0.999, 0.001, 1e-8, 0.01, 10
MESH = pl.DeviceIdType.MESH
ANY = pl.BlockSpec(memory_space=pl.ANY)
VMEM_FULL = pl.BlockSpec(memory_space=pltpu.VMEM)


def _cp(vmem_mb=48, n_axes=1):
    return pltpu.CompilerParams(vmem_limit_bytes=vmem_mb << 20,
                                dimension_semantics=("arbitrary",) * n_axes)


def _hbm(*arrs):
    return [pltpu.with_memory_space_constraint(a, pltpu.HBM) for a in arrs]


def _hbm_out(shapes):
    if isinstance(shapes, (list, tuple)):
        return [pltpu.HBM(s.shape, s.dtype) for s in shapes]
    return pltpu.HBM(shapes.shape, shapes.dtype)


def _nn(a, b):
    return jnp.dot(a, b, preferred_element_type=F32)


def _nt(a, b):
    return lax.dot_general(a, b, (((1,), (1,)), ((), ())), preferred_element_type=F32)


def _tn(a, b):
    return lax.dot_general(a, b, (((0,), (0,)), ((), ())), preferred_element_type=F32)


def _sig(v):
    return jax.nn.sigmoid(v)


def _ln_stats(r):
    mu = jnp.mean(r, axis=-1, keepdims=True)
    xc = r - mu
    var = jnp.mean(xc * xc, axis=-1, keepdims=True)
    rstd = lax.rsqrt(var + LN_EPS)
    return xc * rstd, rstd


def _ln_bwd(dy, r, g):
    xhat, rstd = _ln_stats(r)
    dyg = dy * g
    m1 = jnp.mean(dyg, axis=-1, keepdims=True)
    m2 = jnp.mean(dyg * xhat, axis=-1, keepdims=True)
    return rstd * (dyg - m1 - xhat * m2), xhat


def _rowsum(v):
    return jnp.sum(v, axis=0, keepdims=True)


class _Payload:
    def __init__(self, operands, outs, aliases, sems, start, finish):
        self.operands, self.outs, self.aliases, self.sems = list(operands), list(outs), dict(aliases), list(sems)
        self.start, self.finish = start, finish


def _split(flat, comm, attr):
    out, i = [], 0
    for p in comm:
        n = len(getattr(p, attr))
        out.append(list(flat[i:i + n]))
        i += n
    return out


def _run_comm(comm, which, cin, cout, csem):
    for p, a, b, s in zip(comm, _split(cin, comm, "operands"), _split(cout, comm, "outs"), _split(csem, comm, "sems")):
        getattr(p, which)(a, b, s)


def _pcall(body, *, name, grid, in_specs, out_specs, out_shape, operands, scratch=(), vmem_mb=48, aliases=None,
           comm=()):
    ni, no, ns = len(in_specs), len(out_specs), len(scratch)
    c_ops = [a for p in comm for a in p.operands]
    c_outs = [s for p in comm for s in p.outs]
    c_sems = [s for p in comm for s in p.sems]
    io = dict(aliases or {})
    off_i, off_o = ni, no
    for p in comm:
        for a, b in p.aliases.items():
            io[off_i + a] = off_o + b
        off_i += len(p.operands)
        off_o += len(p.outs)

    def wrapped(*refs):
        ins, cin = refs[:ni], refs[ni:ni + len(c_ops)]
        o0 = ni + len(c_ops)
        outs, cout = refs[o0:o0 + no], refs[o0 + no:o0 + no + len(c_outs)]
        s0 = o0 + no + len(c_outs)
        scr, csem = refs[s0:s0 + ns], refs[s0 + ns:]
        if comm:
            first = functools.reduce(jnp.logical_and, [pl.program_id(a) == 0 for a in range(len(grid))])
            pl.when(first)(lambda: _run_comm(comm, "start", cin, cout, csem))
        body(*ins, *outs, *scr)
        if comm:
            last = functools.reduce(jnp.logical_and, [pl.program_id(a) == grid[a] - 1 for a in range(len(grid))])
            pl.when(last)(lambda: _run_comm(comm, "finish", cin, cout, csem))

    res = pl.pallas_call(
        wrapped, name=name, grid=grid,
        in_specs=list(in_specs) + [ANY] * len(c_ops), out_specs=list(out_specs) + [ANY] * len(c_outs),
        out_shape=_hbm_out(list(out_shape) + c_outs), scratch_shapes=list(scratch) + c_sems,
        input_output_aliases=io,
        compiler_params=pltpu.CompilerParams(vmem_limit_bytes=vmem_mb << 20,
                                             dimension_semantics=("arbitrary",) * len(grid),
                                             has_side_effects=bool(c_sems)),
    )(*_hbm(*operands, *c_ops))
    return list(res[:no]), _split(res[no:], comm, "outs")


def _comm_call(name, comm):
    c_ops = [a for p in comm for a in p.operands]
    c_outs = [s for p in comm for s in p.outs]
    c_sems = [s for p in comm for s in p.sems]
    io, off_i, off_o = {}, 0, 0
    for p in comm:
        for a, b in p.aliases.items():
            io[off_i + a] = off_o + b
        off_i += len(p.operands)
        off_o += len(p.outs)

    def body(*refs):
        cin, cout = refs[:len(c_ops)], refs[len(c_ops):len(c_ops) + len(c_outs)]
        csem = refs[len(c_ops) + len(c_outs):]
        _run_comm(comm, "start", cin, cout, csem)
        _run_comm(comm, "finish", cin, cout, csem)

    res = pl.pallas_call(
        body, name=name, in_specs=[ANY] * len(c_ops), out_specs=[ANY] * len(c_outs), out_shape=_hbm_out(c_outs),
        scratch_shapes=c_sems, input_output_aliases=io,
        compiler_params=pltpu.CompilerParams(has_side_effects=True),
    )(*_hbm(*c_ops))
    return _split(res, comm, "outs")


def _ffn_fwd(x, w_in4, w_out2, g, b, tm, name, comm=()):
    T = x.shape[0]

    def body(x_ref, win_ref, wo_ref, g_ref, b_ref, h_ref, r_ref, xo_ref, xob_ref, xib_ref):
        xf = x_ref[...]
        xv = xf.astype(BF16)
        xib_ref[...] = xv
        acc = ALPHA * xf
        for k in range(2):
            gt = _nn(xv, win_ref[k])
            up = _nn(xv, win_ref[k + 2])
            a = (gt * _sig(gt) * up).astype(BF16)
            h_ref[:, 2 * k * FFH:(2 * k + 1) * FFH] = gt.astype(BF16)
            h_ref[:, (2 * k + 1) * FFH:(2 * k + 2) * FFH] = up.astype(BF16)
            acc = acc + 0.5 * _nn(a, wo_ref[k])
        xhat, _ = _ln_stats(acc)
        xo = xhat * g_ref[...] + b_ref[...]
        r_ref[...] = acc
        xo_ref[...] = xo
        xob_ref[...] = xo.astype(BF16)

    tok = pl.BlockSpec((tm, D), lambda i: (i, 0))
    vec = pl.BlockSpec((1, D), lambda i: (0, 0))
    return _pcall(
        body, name=name, grid=(T // tm,),
        in_specs=[tok,
                  pl.BlockSpec((4, D, FFH), lambda i: (0, 0, 0), pipeline_mode=pl.Buffered(1)),
                  pl.BlockSpec((2, FFH, D), lambda i: (0, 0, 0), pipeline_mode=pl.Buffered(1)),
                  vec, vec],
        out_specs=[pl.BlockSpec((tm, 2 * FF), lambda i: (i, 0)), tok, tok, tok, tok],
        out_shape=[jax.ShapeDtypeStruct((T, 2 * FF), BF16), jax.ShapeDtypeStruct((T, D), F32),
                   jax.ShapeDtypeStruct((T, D), F32), jax.ShapeDtypeStruct((T, D), BF16),
                   jax.ShapeDtypeStruct((T, D), BF16)],
        vmem_mb=58, comm=comm, operands=(x, w_in4, w_out2, g, b))


def _ffn_bwd(dy, r, g, h, w_in4, w_out2, tm, name, comm=()):
    T = dy.shape[0]

    def body(dy_ref, r_ref, g_ref, h_ref, win_ref, wo_ref, dx_ref, dh_ref, a_ref, df_ref, dg_ref, db_ref):
        i = pl.program_id(0)
        dyv = dy_ref[...]
        dr, xhat = _ln_bwd(dyv, r_ref[...], g_ref[...])
        dg_ref[...] = jnp.where(i == 0, 0.0, dg_ref[...]) + _rowsum(dyv * xhat)
        db_ref[...] = jnp.where(i == 0, 0.0, db_ref[...]) + _rowsum(dyv)
        dfb = (0.5 * dr).astype(BF16)
        df_ref[...] = dfb
        acc = ALPHA * dr
        for k in range(2):
            da = _nt(dfb, wo_ref[k])
            gt = h_ref[:, 2 * k * FFH:(2 * k + 1) * FFH].astype(F32)
            up = h_ref[:, (2 * k + 1) * FFH:(2 * k + 2) * FFH].astype(F32)
            sg = _sig(gt)
            silu = gt * sg
            dgate = (da * up * (sg * (1.0 + gt * (1.0 - sg)))).astype(BF16)
            dup = (da * silu).astype(BF16)
            a_ref[:, k * FFH:(k + 1) * FFH] = (silu * up).astype(BF16)
            dh_ref[:, 2 * k * FFH:(2 * k + 1) * FFH] = dgate
            dh_ref[:, (2 * k + 1) * FFH:(2 * k + 2) * FFH] = dup
            acc = acc + _nt(dgate, win_ref[k]) + _nt(dup, win_ref[k + 2])
        dx_ref[...] = acc

    tok = pl.BlockSpec((tm, D), lambda i: (i, 0))
    vec = pl.BlockSpec((1, D), lambda i: (0, 0))
    wide = pl.BlockSpec((tm, 2 * FF), lambda i: (i, 0))
    return _pcall(
        body, name=name, grid=(T // tm,),
        in_specs=[tok, tok, vec, wide,
                  pl.BlockSpec((4, D, FFH), lambda i: (0, 0, 0), pipeline_mode=pl.Buffered(1)),
                  pl.BlockSpec((2, FFH, D), lambda i: (0, 0, 0), pipeline_mode=pl.Buffered(1))],
        out_specs=[tok, wide, pl.BlockSpec((tm, FF), lambda i: (i, 0)), tok, vec, vec],
        out_shape=[jax.ShapeDtypeStruct((T, D), F32), jax.ShapeDtypeStruct((T, 2 * FF), BF16),
                   jax.ShapeDtypeStruct((T, FF), BF16), jax.ShapeDtypeStruct((T, D), BF16),
                   jax.ShapeDtypeStruct((1, D), F32), jax.ShapeDtypeStruct((1, D), F32)],
        vmem_mb=58, comm=comm, operands=(dy, r, g, h, w_in4, w_out2))


def _mm_tn(a, b, tk, tn, name, shard_cols=None, interleaved=False, comm=()):
    T, K = a.shape
    N = b.shape[1]

    def body(a_ref, b_ref, o_ref):
        o_ref[...] = _tn(a_ref[...], b_ref[...])

    if shard_cols is None:
        out_shape = jax.ShapeDtypeStruct((K, N), F32)
        out_spec = pl.BlockSpec((tk, tn), lambda ki, nj: (ki, nj))
    else:
        per = shard_cols // tn

        def shard(nj):
            blk = nj // per
            return (blk % 2) * 2 + blk // 2 if interleaved else blk

        out_shape = jax.ShapeDtypeStruct((N // shard_cols, K, shard_cols), F32)
        out_spec = pl.BlockSpec((None, tk, tn), lambda ki, nj: (shard(nj), ki, nj % per))
    (out,), got = _pcall(
        body, name=name, grid=(K // tk, N // tn),
        in_specs=[pl.BlockSpec((T, tk), lambda ki, nj: (0, ki)), pl.BlockSpec((T, tn), lambda ki, nj: (0, nj))],
        out_specs=[out_spec], out_shape=[out_shape], comm=comm, operands=(a, b))
    return out, got


def _mix_fwd_a(xb, w_mix4, conv_w, conv_b, w_co4, tm, comm=()):
    T = xb.shape[0]

    def body(xb_ref, w_ref, cw_ref, cb_ref, wco_ref,
             pc_ref, z_ref, yin_ref, su_ref, sub_ref, gc_ref, gs_ref, yc_ref, qbuf):
        @pl.when(pl.program_id(0) == 0)
        def _():
            qbuf[pl.ds(0, 8), :] = jnp.zeros((8, CONV), F32)

        xv = xb_ref[...]
        p0 = _nn(xv, w_ref[0])
        p1 = _nn(xv, w_ref[1])
        gc_ref[...] = _nn(xv, w_ref[2]).astype(BF16)
        gs_ref[...] = _nn(xv, w_ref[3]).astype(BF16)
        cbv, ccv = p0[:, :CONV], p0[:, CONV:]
        chv, suv = p1[:, :CONV], p1[:, CONV:]
        q = ccv * chv
        qbuf[pl.ds(8, tm), :] = q
        cw = cw_ref[...]
        z = (cw[2:3] * q + cw[1:2] * qbuf[pl.ds(7, tm), :] + cw[0:1] * qbuf[pl.ds(6, tm), :]
             + cb_ref[...])
        qbuf[pl.ds(0, 8), :] = q[tm - 8:tm]
        yin = (cbv * z).astype(BF16)
        pc_ref[:, 0:CONV] = cbv.astype(BF16)
        pc_ref[:, CONV:2 * CONV] = ccv.astype(BF16)
        pc_ref[:, 2 * CONV:3 * CONV] = chv.astype(BF16)
        z_ref[...] = z.astype(BF16)
        yin_ref[...] = yin
        su_ref[...] = suv
        sub_ref[...] = suv.astype(BF16)
        for k in range(4):
            yc_ref[:, 256 * k:256 * (k + 1)] = _nn(yin, wco_ref[k]).astype(BF16)

    def tok(n):
        return pl.BlockSpec((tm, n), lambda i: (i, 0))

    def full(shape):
        return pl.BlockSpec(shape, lambda i: (0,) * len(shape))

    return _pcall(
        body, name="mix_fwd_a", grid=(T // tm,),
        in_specs=[tok(D), full((4, D, D)), full((3, CONV)), full((1, CONV)), full((4, CONV, 256))],
        out_specs=[tok(3 * CONV), tok(CONV), tok(CONV), tok(SSM), tok(SSM), tok(D), tok(D), tok(D)],
        out_shape=[jax.ShapeDtypeStruct((T, 3 * CONV), BF16), jax.ShapeDtypeStruct((T, CONV), BF16),
                   jax.ShapeDtypeStruct((T, CONV), BF16), jax.ShapeDtypeStruct((T, SSM), F32),
                   jax.ShapeDtypeStruct((T, SSM), BF16), jax.ShapeDtypeStruct((T, D), BF16),
                   jax.ShapeDtypeStruct((T, D), BF16), jax.ShapeDtypeStruct((T, D), BF16)],
        scratch=[pltpu.VMEM((tm + 8, CONV), F32)], vmem_mb=56, comm=comm,
        operands=(xb, w_mix4, conv_w, conv_b, w_co4))


def _scan_rows(bre, bim, ar, ai, T, rev, load, store=None):
    R, W, G = SCAN_R, bre.shape[1], T // 8
    if rev:
        ai = -ai

    def cmul(pr, pi, xr, xi):
        return pr * xr - pi * xi, pr * xi + pi * xr

    pw = [(ar, ai)]
    for _ in range(7):
        pw.append(cmul(ar, ai, *pw[-1]))

    def shifted(v, d, axis, n, idx):
        if rev:
            return jnp.where(idx < n - d, pltpu.roll(v, n - d, axis), 0.0)
        return jnp.where(idx >= d, pltpu.roll(v, d, axis), 0.0)

    sub8 = lax.broadcasted_iota(jnp.int32, (8, W), 0)
    inside = {d: (sub8 < 8 - d) if rev else (sub8 >= d) for d in (1, 2, 4)}
    pm = {d: (jnp.where(inside[d], pw[d - 1][0], 0.0)[None], jnp.where(inside[d], pw[d - 1][1], 0.0)[None])
          for d in (1, 2, 4)}

    def step(i, _):
        t0 = pl.multiple_of(i * R, R)
        vr, vi = load(t0)
        vr, vi = vr.reshape(R // 8, 8, W), vi.reshape(R // 8, 8, W)
        for d in (1, 2, 4):
            sh = (8 - d) if rev else d
            dr, di = cmul(pm[d][0], pm[d][1], pltpu.roll(vr, sh, 1), pltpu.roll(vi, sh, 1))
            vr, vi = vr + dr, vi + di
        bre[pl.ds(t0 + 8, R), :] = vr.reshape(R, W)
        bim[pl.ds(t0 + 8, R), :] = vi.reshape(R, W)
        return 0

    lax.fori_loop(0, T // R, step, 0)

    edge = 0 if rev else 7
    cr = bre[pl.ds(8 + edge, G, stride=8), :]
    ci = bim[pl.ds(8 + edge, G, stride=8), :]
    row = lax.broadcasted_iota(jnp.int32, (G, W), 0)
    qr, qi = pw[7]
    d = 1
    while d < G:
        dr, di = cmul(qr, qi, shifted(cr, d, 0, G, row), shifted(ci, d, 0, G, row))
        cr, ci = cr + dr, ci + di
        qr, qi = qr * qr - qi * qi, 2.0 * qr * qi
        d *= 2

    order = [7 - r for r in range(8)] if rev else list(range(8))
    if store is None:
        nr, ni = shifted(cr, 1, 0, G, row), shifted(ci, 1, 0, G, row)
        for r in range(8):
            dr, di = cmul(*pw[order[r]], nr, ni)
            bre[pl.ds(8 + r, G, stride=8), :] = bre[pl.ds(8 + r, G, stride=8), :] + dr
            bim[pl.ds(8 + r, G, stride=8), :] = bim[pl.ds(8 + r, G, stride=8), :] + di
        return
    p8r = jnp.concatenate([pw[k][0] for k in order], axis=0)[None]
    p8i = jnp.concatenate([pw[k][1] for k in order], axis=0)[None]

    def back(nre, nim):
        nre[...] = shifted(cr, 1, 0, G, row)
        nim[...] = shifted(ci, 1, 0, G, row)

        def step3(i, _):
            t0 = pl.multiple_of(i * R, R)
            g0 = pl.multiple_of(i * (R // 8), R // 8)
            br = jnp.broadcast_to(nre[pl.ds(g0, R // 8), :][:, None, :], (R // 8, 8, W))
            bi = jnp.broadcast_to(nim[pl.ds(g0, R // 8), :][:, None, :], (R // 8, 8, W))
            dr, di = cmul(p8r, p8i, br, bi)
            xr = bre[pl.ds(t0 + 8, R), :] + dr.reshape(R, W)
            xi = bim[pl.ds(t0 + 8, R), :] + di.reshape(R, W)
            store(t0, xr, xi)
            return 0

        lax.fori_loop(0, T // R, step3, 0)

    pl.run_scoped(back, pltpu.VMEM((G, W), F32), pltpu.VMEM((G, W), F32))


def _scan_specs(T):
    W = SCAN_W
    lane = pl.BlockSpec((T, W), lambda j: (0, j))
    col = pl.BlockSpec((T, 128), lambda j: (0, j // SCAN_PER))
    wb = pl.BlockSpec((None, 128, W), lambda j: (j, 0, 0))
    wc = pl.BlockSpec((None, W, 128), lambda j: (j, 0, 0))
    vec = pl.BlockSpec((1, W), lambda j: (0, j))
    return lane, col, wb, wc, vec


def _s5_scan_fwd(su_b, wb_re, wb_im, a_re, a_im, comm=()):
    T = su_b.shape[0]
    W = SCAN_W

    def body(su_ref, wbr_ref, wbi_ref, ar_ref, ai_ref, sr_ref, si_ref, bre, bim):
        su = su_ref[...]
        bre[pl.ds(8, T), :] = _nn(su, wbr_ref[...])
        bim[pl.ds(8, T), :] = _nn(su, wbi_ref[...])
        def store(t0, xr, xi):
            sr_ref[pl.ds(t0, SCAN_R), :] = xr.astype(BF16)
            si_ref[pl.ds(t0, SCAN_R), :] = xi.astype(BF16)

        _scan_rows(bre, bim, ar_ref[...], ai_ref[...], T, False,
                   lambda t0: (bre[pl.ds(t0 + 8, SCAN_R), :], bim[pl.ds(t0 + 8, SCAN_R), :]), store)

    lane, col, wb, wc, vec = _scan_specs(T)
    return _pcall(
        body, name="s5_scan_fwd", grid=(LANES // W,),
        in_specs=[col, wb, wb, vec, vec],
        out_specs=[lane, lane],
        out_shape=[jax.ShapeDtypeStruct((T, LANES), BF16)] * 2,
        scratch=[pltpu.VMEM((T + 16, W), F32)] * 2, comm=comm,
        operands=(su_b, wb_re, wb_im, a_re, a_im))


def _gelu(s):
    th = jnp.tanh(GELU_C * (s + 0.044715 * s * s * s))
    return 0.5 * s * (1.0 + th), th


def _mix_fwd_b(st_re, st_im, wc_re4, wc_im4, su, dvec, w_glu4, g_conv, g_ssm, y_conv, w_mo, x1, g, b, tm, comm=()):
    T = su.shape[0]

    def body(sr_ref, si_ref, wcr_ref, wci_ref, su_ref, d_ref, wg_ref, gc_ref, gs_ref, yc_ref, wmo_ref,
             x_ref, g_ref, b_ref, s_ref, sgb_ref, ga_ref, gb_ref, mb_ref, r_ref, xo_ref):
        srb = sr_ref[...]
        sib = si_ref[...]
        ys = [_nn(srb[:, 512 * J:512 * (J + 1)], wcr_ref[J]) + _nn(sib[:, 512 * J:512 * (J + 1)], wci_ref[J])
              for J in range(4)]
        s = jnp.concatenate(ys, axis=1) + d_ref[...] * su_ref[...]
        sg, _ = _gelu(s)
        sgb = sg.astype(BF16)
        ga = jnp.concatenate([_nn(sgb, wg_ref[0]), _nn(sgb, wg_ref[1])], axis=1)
        gb = jnp.concatenate([_nn(sgb, wg_ref[2]), _nn(sgb, wg_ref[3])], axis=1)
        merged = (_sig(gc_ref[...].astype(F32)) * yc_ref[...].astype(F32)
                  + _sig(gs_ref[...].astype(F32)) * (ga * _sig(gb)))
        mb = merged.astype(BF16)
        r = ALPHA * x_ref[...] + _nn(mb, wmo_ref[...])
        xhat, _ = _ln_stats(r)
        xo = xhat * g_ref[...] + b_ref[...]
        s_ref[...] = s
        sgb_ref[...] = sgb
        ga_ref[...] = ga.astype(BF16)
        gb_ref[...] = gb.astype(BF16)
        mb_ref[...] = mb
        r_ref[...] = r
        xo_ref[...] = xo

    def tok(n):
        return pl.BlockSpec((tm, n), lambda i: (i, 0))

    def full(shape):
        return pl.BlockSpec(shape, lambda i: (0,) * len(shape))

    return _pcall(
        body, name="mix_fwd_b", grid=(T // tm,),
        in_specs=[tok(LANES), tok(LANES), full((4, 512, 128)), full((4, 512, 128)), tok(SSM), full((1, SSM)),
                  full((4, SSM, 512)), tok(D), tok(D), tok(D), full((D, D)), tok(D), full((1, D)), full((1, D))],
        out_specs=[tok(SSM), tok(SSM), tok(D), tok(D), tok(D), tok(D), tok(D)],
        out_shape=[jax.ShapeDtypeStruct((T, SSM), F32), jax.ShapeDtypeStruct((T, SSM), BF16),
                   jax.ShapeDtypeStruct((T, D), BF16), jax.ShapeDtypeStruct((T, D), BF16),
                   jax.ShapeDtypeStruct((T, D), BF16), jax.ShapeDtypeStruct((T, D), F32),
                   jax.ShapeDtypeStruct((T, D), F32)],
        vmem_mb=56, comm=comm,
        operands=(st_re, st_im, wc_re4, wc_im4, su, dvec, w_glu4, g_conv, g_ssm, y_conv, w_mo, x1, g, b))


def _ple_loss(x3, x3b, p, w_pi4, w_pg, g, b, target, tm):
    T = x3.shape[0]
    PD = p.shape[1]

    def body(x_ref, xb_ref, p_ref, wpi_ref, wpg_ref, g_ref, b_ref, t_ref,
             loss_ref, dx_ref, pb_ref, dpw_ref, dgt_ref, dg_ref, db_ref):
        i = pl.program_id(0)
        pb = p_ref[...].astype(BF16)
        pw = jnp.concatenate([_nn(pb, wpi_ref[k]) for k in range(4)], axis=1)
        gt = _nn(xb_ref[...], wpg_ref[...])
        sg = _sig(gt)
        r = ALPHA * x_ref[...] + pw * sg
        gv = g_ref[...]
        xhat, rstd = _ln_stats(r)
        err = xhat * gv + b_ref[...] - t_ref[...]
        lpart = jnp.zeros((1, 128), F32) + 0.5 * jnp.sum(jnp.mean(err * err, axis=-1, keepdims=True))
        dy = err * (1.0 / D)
        dyg = dy * gv
        m1 = jnp.mean(dyg, axis=-1, keepdims=True)
        m2 = jnp.mean(dyg * xhat, axis=-1, keepdims=True)
        dr = rstd * (dyg - m1 - xhat * m2)
        pg, pbias = _rowsum(dy * xhat), _rowsum(dy)

        @pl.when(i == 0)
        def _():
            loss_ref[...] = lpart
            dg_ref[...] = pg
            db_ref[...] = pbias

        @pl.when(i > 0)
        def _():
            loss_ref[...] += lpart
            dg_ref[...] += pg
            db_ref[...] += pbias

        dgt = (dr * pw * sg * (1.0 - sg)).astype(BF16)
        pb_ref[...] = pb
        dpw_ref[...] = (dr * sg).astype(BF16)
        dgt_ref[...] = dgt
        dx_ref[...] = ALPHA * dr + _nt(dgt, wpg_ref[...])

    def tok(n):
        return pl.BlockSpec((tm, n), lambda i: (i, 0))

    def full(shape):
        return pl.BlockSpec(shape, lambda i: (0,) * len(shape))

    return pl.pallas_call(
        body, name="ple_loss", grid=(T // tm,),
        in_specs=[tok(D), tok(D), tok(PD), full((4, PD, 256)), full((D, D)), full((1, D)), full((1, D)), tok(D)],
        out_specs=[full((1, 128)), tok(D), tok(PD), tok(D), tok(D), full((1, D)), full((1, D))],
        out_shape=_hbm_out([jax.ShapeDtypeStruct((1, 128), F32), jax.ShapeDtypeStruct((T, D), F32),
                            jax.ShapeDtypeStruct((T, PD), BF16), jax.ShapeDtypeStruct((T, D), BF16),
                            jax.ShapeDtypeStruct((T, D), BF16), jax.ShapeDtypeStruct((1, D), F32),
                            jax.ShapeDtypeStruct((1, D), F32)]),
        compiler_params=_cp(48, 1),
    )(*_hbm(x3, x3b, p, w_pi4, w_pg, g, b, target))


def _mix_bwd_b(dy, r2, g, w_mo, g_conv, g_ssm, y_conv, ga, gb, s, su, dvec, w_glu4, wc_re4, wc_im4, tm, comm=()):
    T = dy.shape[0]

    def body(dy_ref, r_ref, g_ref, wmo_ref, gc_ref, gs_ref, yc_ref, ga_ref, gb_ref, s_ref, su_ref, d_ref,
             wg_ref, wcr_ref, wci_ref,
             dres_ref, dmix_ref, dgl_ref, dsb_ref, dud_ref, gsr_ref, gsi_ref, dyc_ref, dp_ref,
             dg_ref, db_ref, dd_ref):
        i = pl.program_id(0)
        dyv = dy_ref[...]
        dr, xhat = _ln_bwd(dyv, r_ref[...], g_ref[...])
        dmix = dr.astype(BF16)
        dmerged = _nt(dmix, wmo_ref[...])
        sc, ss, sgb = (_sig(gc_ref[...].astype(F32)), _sig(gs_ref[...].astype(F32)),
                       _sig(gb_ref[...].astype(F32)))
        gav = ga_ref[...].astype(F32)
        yssm = gav * sgb
        dgc = dmerged * yc_ref[...].astype(F32) * sc * (1.0 - sc)
        dgss = dmerged * yssm * ss * (1.0 - ss)
        dyssm = dmerged * ss
        dgl = jnp.concatenate([dyssm * sgb, dyssm * gav * sgb * (1.0 - sgb)], axis=1).astype(BF16)
        dsg = (_nt(dgl[:, 0:512], wg_ref[0]) + _nt(dgl[:, 512:1024], wg_ref[1])
               + _nt(dgl[:, 1024:1536], wg_ref[2]) + _nt(dgl[:, 1536:2048], wg_ref[3]))
        sv = s_ref[...]
        _, th = _gelu(sv)
        dgelu = 0.5 * (1.0 + th) + 0.5 * sv * (1.0 - th * th) * GELU_C * (1.0 + 3.0 * 0.044715 * sv * sv)
        ds = dsg * dgelu
        dsb = ds.astype(BF16)
        pg, pb, pd = _rowsum(dyv * xhat), _rowsum(dyv), _rowsum(ds * su_ref[...])

        @pl.when(i == 0)
        def _():
            dg_ref[...] = pg
            db_ref[...] = pb
            dd_ref[...] = pd

        @pl.when(i > 0)
        def _():
            dg_ref[...] += pg
            db_ref[...] += pb
            dd_ref[...] += pd

        dres_ref[...] = ALPHA * dr
        dmix_ref[...] = dmix
        dgl_ref[...] = dgl
        dsb_ref[...] = dsb
        dud_ref[...] = ds * d_ref[...]
        for J in range(4):
            gsr_ref[:, 512 * J:512 * (J + 1)] = _nt(dsb[:, 128 * J:128 * (J + 1)], wcr_ref[J]).astype(BF16)
            gsi_ref[:, 512 * J:512 * (J + 1)] = _nt(dsb[:, 128 * J:128 * (J + 1)], wci_ref[J]).astype(BF16)
        dyc_ref[...] = (dmerged * sc).astype(BF16)
        dp_ref[:, 0:D] = dgc.astype(BF16)
        dp_ref[:, D:2 * D] = dgss.astype(BF16)

    def tok(n):
        return pl.BlockSpec((tm, n), lambda i: (i, 0))

    def full(shape):
        return pl.BlockSpec(shape, lambda i: (0,) * len(shape))

    return _pcall(
        body, name="mix_bwd_b", grid=(T // tm,),
        in_specs=[tok(D), tok(D), full((1, D)), full((D, D)), tok(D), tok(D), tok(D), tok(D), tok(D),
                  tok(SSM), tok(SSM), full((1, SSM)), full((4, SSM, 512)), full((4, 512, 128)), full((4, 512, 128))],
        out_specs=[tok(D), tok(D), tok(2 * D), tok(SSM), tok(SSM), tok(LANES), tok(LANES), tok(D),
                   pl.BlockSpec((tm, 2 * D), lambda i: (i, 1)), full((1, D)), full((1, D)), full((1, SSM))],
        out_shape=[jax.ShapeDtypeStruct((T, D), F32), jax.ShapeDtypeStruct((T, D), BF16),
                   jax.ShapeDtypeStruct((T, 2 * D), BF16), jax.ShapeDtypeStruct((T, SSM), BF16),
                   jax.ShapeDtypeStruct((T, SSM), F32), jax.ShapeDtypeStruct((T, LANES), BF16),
                   jax.ShapeDtypeStruct((T, LANES), BF16), jax.ShapeDtypeStruct((T, D), BF16),
                   jax.ShapeDtypeStruct((T, 4 * D), BF16), jax.ShapeDtypeStruct((1, D), F32),
                   jax.ShapeDtypeStruct((1, D), F32), jax.ShapeDtypeStruct((1, SSM), F32)],
        vmem_mb=56, comm=comm,
        operands=(dy, r2, g, w_mo, g_conv, g_ssm, y_conv, ga, gb, s, su, dvec, w_glu4, wc_re4, wc_im4))


def _s5_scan_bwd(gs_re, gs_im, st_re, st_im, su_b, ds_b, wb_re, wb_im, a_re, a_im, comm=()):
    T = su_b.shape[0]
    W = SCAN_W
    R = SCAN_R

    def body(gr_ref, gi_ref, sr_ref, si_ref, su_ref, ds_ref, wbr_ref, wbi_ref, ar_ref, ai_ref,
             dsu_ref, dwbr_ref, dwbi_ref, dwcr_ref, dwci_ref, dar_ref, dai_ref, gre, gim):
        j = pl.program_id(0)
        zero = jnp.zeros((8, W), F32)
        for buf in (gre, gim):
            buf[pl.ds(T + 8, 8), :] = zero
        _scan_rows(gre, gim, ar_ref[...], ai_ref[...], T, True,
                   lambda t0: (gr_ref[pl.ds(t0, R), :].astype(F32), gi_ref[pl.ds(t0, R), :].astype(F32)))
        grb = gre[pl.ds(8, T), :].astype(BF16)
        gib = gim[pl.ds(8, T), :].astype(BF16)
        part = _nt(grb, wbr_ref[...]) + _nt(gib, wbi_ref[...])

        @pl.when(j % SCAN_PER == 0)
        def _():
            dsu_ref[...] = part

        @pl.when(j % SCAN_PER > 0)
        def _():
            dsu_ref[...] += part

        su = su_ref[...]
        dwbr_ref[...] = _tn(su, grb)
        dwbi_ref[...] = _tn(su, gib)
        dsv = ds_ref[...]
        dwcr_ref[...] = _tn(sr_ref[...], dsv)
        dwci_ref[...] = _tn(si_ref[...], dsv)
        dar = jnp.zeros((1, W), F32)
        dai = jnp.zeros((1, W), F32)
        for c in range(T // R):
            xr = sr_ref[pl.ds(c * R, R), :].astype(F32)
            xi = si_ref[pl.ds(c * R, R), :].astype(F32)
            g1r = gre[pl.ds(c * R + 9, R), :]
            g1i = gim[pl.ds(c * R + 9, R), :]
            dar = dar + _rowsum(g1r * xr + g1i * xi)
            dai = dai + _rowsum(g1i * xr - g1r * xi)
        dar_ref[...] = dar
        dai_ref[...] = dai

    lane, col, wb, wc, vec = _scan_specs(T)
    return _pcall(
        body, name="s5_scan_bwd", grid=(LANES // W,),
        in_specs=[lane, lane, lane, lane, col, col, wb, wb, vec, vec],
        out_specs=[col, wb, wb, wc, wc, vec, vec],
        out_shape=[jax.ShapeDtypeStruct((T, SSM), F32),
                   jax.ShapeDtypeStruct((LANES // W, 128, W), F32), jax.ShapeDtypeStruct((LANES // W, 128, W), F32),
                   jax.ShapeDtypeStruct((LANES // W, W, 128), F32), jax.ShapeDtypeStruct((LANES // W, W, 128), F32),
                   jax.ShapeDtypeStruct((1, LANES), F32), jax.ShapeDtypeStruct((1, LANES), F32)],
        scratch=[pltpu.VMEM((T + 16, W), F32)] * 2, vmem_mb=56, comm=comm,
        operands=(gs_re, gs_im, st_re, st_im, su_b, ds_b, wb_re, wb_im, a_re, a_im))


def _mix_bwd_a(dyc_b, w_co4, pc, z_b, conv_w, dsu_ssm, du_dir, dproj, dres, w_mix4, tm, comm=()):
    T = dres.shape[0]
    nt = T // tm

    def body(dyc_ref, wco_ref, pc_ref, halo_ref, z_ref, cw_ref, dsu_ref, dud_ref, dpin_ref, dres_ref, w_ref,
             dp_ref, dx_ref, dcw_ref, dcb_ref, dzbuf, qbuf):
        i = pl.program_id(0)
        ii = nt - 1 - i

        @pl.when(i == 0)
        def _():
            dzbuf[pl.ds(tm, 8), :] = jnp.zeros((8, CONV), F32)

        dyc = dyc_ref[...]
        dyin = (_nt(dyc[:, 0:256], wco_ref[0]) + _nt(dyc[:, 256:512], wco_ref[1])
                + _nt(dyc[:, 512:768], wco_ref[2]) + _nt(dyc[:, 768:1024], wco_ref[3]))
        cbv = pc_ref[:, 0:CONV].astype(F32)
        ccv = pc_ref[:, CONV:2 * CONV].astype(F32)
        chv = pc_ref[:, 2 * CONV:3 * CONV].astype(F32)
        dcbv = dyin * z_ref[...].astype(F32)
        dz = dyin * cbv
        dzbuf[pl.ds(0, tm), :] = dz
        cw = cw_ref[...]
        dq = cw[2:3] * dz + cw[1:2] * dzbuf[pl.ds(1, tm), :] + cw[0:1] * dzbuf[pl.ds(2, tm), :]
        dzbuf[pl.ds(tm, 8), :] = dz[0:8]
        q = ccv * chv
        hq = halo_ref[:, CONV:2 * CONV].astype(F32) * halo_ref[:, 2 * CONV:3 * CONV].astype(F32)
        qbuf[pl.ds(0, 8), :] = jnp.where(ii > 0, hq, jnp.zeros_like(hq))
        qbuf[pl.ds(8, tm), :] = q
        pw = jnp.concatenate([_rowsum(dz * qbuf[pl.ds(6, tm), :]), _rowsum(dz * qbuf[pl.ds(7, tm), :]),
                              _rowsum(dz * q), jnp.zeros((5, CONV), F32)], axis=0)
        pbias = _rowsum(dz)

        @pl.when(i == 0)
        def _():
            dcw_ref[...] = pw
            dcb_ref[...] = pbias

        @pl.when(i > 0)
        def _():
            dcw_ref[...] += pw
            dcb_ref[...] += pbias

        dp0 = jnp.concatenate([dcbv, dq * chv], axis=1).astype(BF16)
        dp1 = jnp.concatenate([dq * ccv, dsu_ref[...] + dud_ref[...]], axis=1).astype(BF16)
        dp_ref[:, 0:D] = dp0
        dp_ref[:, D:2 * D] = dp1
        dx_ref[...] = (dres_ref[...] + _nt(dp0, w_ref[0]) + _nt(dp1, w_ref[1])
                       + _nt(dpin_ref[:, 0:D], w_ref[2]) + _nt(dpin_ref[:, D:2 * D], w_ref[3]))

    def tok(n):
        return pl.BlockSpec((tm, n), lambda i: (nt - 1 - i, 0))

    def full(shape):
        return pl.BlockSpec(shape, lambda i: (0,) * len(shape))

    halo = pl.BlockSpec((8, 3 * CONV), lambda i: (jnp.maximum((nt - 1 - i) * (tm // 8) - 1, 0), 0))
    return _pcall(
        body, name="mix_bwd_a", grid=(nt,),
        in_specs=[tok(D), full((4, CONV, 256)), tok(3 * CONV), halo, tok(CONV), full((3, CONV)),
                  tok(SSM), tok(SSM), pl.BlockSpec((tm, 2 * D), lambda i: (nt - 1 - i, 1)), tok(D),
                  full((4, D, D))],
        out_specs=[pl.BlockSpec((tm, 2 * D), lambda i: (nt - 1 - i, 0)), tok(D), full((8, CONV)), full((1, CONV))],
        out_shape=[jax.ShapeDtypeStruct((T, 4 * D), BF16), jax.ShapeDtypeStruct((T, D), F32),
                   jax.ShapeDtypeStruct((8, CONV), F32), jax.ShapeDtypeStruct((1, CONV), F32)],
        scratch=[pltpu.VMEM((tm + 8, CONV), F32), pltpu.VMEM((tm + 8, CONV), F32)],
        aliases={8: 0}, vmem_mb=56, comm=comm,
        operands=(dyc_b, w_co4, pc, pc, z_b, conv_w, dsu_ssm, du_dir, dproj, dres, w_mix4))


def _zoh(lam_re, lam_im, log_step, b_re, b_im):
    dt = jnp.exp(log_step)[:, None]
    mag = jnp.exp(lam_re * dt)
    abr, abi = mag * jnp.cos(lam_im * dt), mag * jnp.sin(lam_im * dt)
    nr, ni = abr - 1.0, abi
    den = lam_re * lam_re + lam_im * lam_im
    cr = (nr * lam_re + ni * lam_im) / den
    ci = (ni * lam_re - nr * lam_im) / den
    bbr = cr[..., None] * b_re - ci[..., None] * b_im
    bbi = cr[..., None] * b_im + ci[..., None] * b_re
    return abr, abi, bbr, bbi


_WB_MASK = (np.arange(8)[None, :, None]
            == SCAN_GR * np.arange(SCAN_PER)[:, None, None] + np.arange(SCAN_GR)[None, None, :]).astype(np.float32)
_EYE8 = np.eye(8, dtype=np.float32)


def _wb_blocks(bb):
    bt = bb.transpose(0, 2, 1).reshape(4, 1, 8, 16, 1, STATE)
    full = bt * _WB_MASK[None, :, :, None, :, None]
    return full.reshape(LANES // SCAN_W, 128, SCAN_W).astype(BF16)


def _wc_blocks(cc):
    ct = cc.transpose(0, 2, 1).reshape(4, 8, STATE, 1, 16)
    full = ct * _EYE8[None, :, None, :, None]
    return full.reshape(4, 512, 128).astype(BF16)


def _wb_diag(dwb):
    d6 = dwb.reshape(4, SCAN_PER, 8, 16, SCAN_GR, STATE) * _WB_MASK[None, :, :, None, :, None]
    return d6.sum(axis=(1, 4)).reshape(GROUPS, 16, STATE).transpose(0, 2, 1)


def _wc_diag(dwc):
    mask = _WB_MASK.transpose(0, 2, 1)
    d6 = dwc.reshape(4, SCAN_PER, SCAN_GR, STATE, 8, 16) * mask[None, :, :, None, :, None]
    return d6.sum(axis=4).reshape(GROUPS, STATE, 16).transpose(0, 2, 1)


def _where():
    x, y, c = lax.axis_index("x"), lax.axis_index("y"), lax.axis_index("c")
    return x, y, c, 2 * x + y


def _chip_dev(k, c):
    return (k // 2, k % 2, c)


def _slot_cast(meidx, w, dtype, name, token=()):
    R, C = w.shape
    tr = _row_tile(R)

    def body(m_ref, w_ref, *rest):
        rest[-1][...] = w_ref[...].astype(dtype)

    gs = pltpu.PrefetchScalarGridSpec(
        num_scalar_prefetch=1, grid=(R // tr,),
        in_specs=[pl.BlockSpec((tr, C), lambda i, m: (i, 0))] + [pl.BlockSpec((8, 128), lambda i, m: (0, 0))] * len(token),
        out_specs=pl.BlockSpec((None, tr, C), lambda i, m: (m[0], i, 0)))
    return pl.pallas_call(
        body, name=name, grid_spec=gs, out_shape=_hbm_out(jax.ShapeDtypeStruct((4, R, C), dtype)),
        compiler_params=_cp(32, 1),
    )(meidx, *_hbm(w), *token)


def _gather_ici_payload(bufs):
    def copies(ins, lnd, ss, rs):
        x, y, c, me = _where()
        cps = []
        for w, b in enumerate(bufs):
            h = b.shape[1] // 2
            mine = lnd[w].at[me, pl.ds(c * h, h)]
            for s in range(3):
                k = (me + 1 + s) % 4
                cps.append(pltpu.make_async_remote_copy(
                    src_ref=mine, dst_ref=mine, send_sem=ss.at[3 * w + s], recv_sem=rs.at[3 * w + s],
                    device_id=_chip_dev(k, c), device_id_type=MESH))
        return cps

    p = _sym_payload([], [jax.ShapeDtypeStruct(b.shape, b.dtype) for b in bufs], copies, 3 * len(bufs))
    p.lands = list(bufs)
    return p


def _gather_pass_payload(bufs):
    def copies(ins, outs, ss, rs):
        x, y, c, me = _where()
        cps = []
        for w, b in enumerate(bufs):
            h = b.shape[1] // 2
            for s in range(3):
                j = (me + 1 + s) % 4
                cps.append(pltpu.make_async_remote_copy(
                    src_ref=ins[w].at[j, pl.ds(c * h, h)], dst_ref=outs[w].at[j, pl.ds(c * h, h)],
                    send_sem=ss.at[3 * w + s], recv_sem=rs.at[3 * w + s], device_id=(x, y, 1 - c),
                    device_id_type=MESH))
        return cps

    p = _sym_payload(bufs, [jax.ShapeDtypeStruct(b.shape, b.dtype) for b in bufs], copies, 3 * len(bufs))
    p.aliases = {w: w for w in range(len(bufs))}
    return p


def _gather_payload(bufs):
    n = len(bufs)

    def half(ref, w, k, cc):
        h = bufs[w].shape[1] // 2
        return ref.at[k, pl.ds(cc * h, h)]

    def ici(ins, outs, sems, w, s):
        x, y, c, me = _where()
        k = (me + 1 + s) % 4
        return pltpu.make_async_remote_copy(
            src_ref=half(ins[w], w, me, c), dst_ref=half(outs[w], w, me, c), send_sem=sems[0].at[3 * w + s],
            recv_sem=sems[1].at[3 * w + s], device_id=_chip_dev(k, c), device_id_type=MESH)

    def landed(outs, sems, w, s):
        x, y, c, me = _where()
        j = (me + 3 - s) % 4
        return pltpu.make_async_remote_copy(
            src_ref=half(outs[w], w, j, c), dst_ref=half(outs[w], w, j, c), send_sem=sems[0].at[3 * w + s],
            recv_sem=sems[1].at[3 * w + s], device_id=(x, y, 1 - c), device_id_type=MESH)

    def passed(outs, sems, w, s, cc):
        x, y, c, me = _where()
        j = (me + 3 - s) % 4
        return pltpu.make_async_remote_copy(
            src_ref=half(outs[w], w, j, cc), dst_ref=half(outs[w], w, j, cc), send_sem=sems[2].at[3 * w + s],
            recv_sem=sems[3].at[3 * w + s], device_id=(x, y, 1 - c), device_id_type=MESH)

    pairs = [(w, s) for w in range(n) for s in range(3)]

    def start(ins, outs, sems):
        for w, s in pairs:
            ici(ins, outs, sems, w, s).start()

    def finish(ins, outs, sems):
        _, _, c, _ = _where()
        for w, s in pairs:
            landed(outs, sems, w, s).wait_recv()
            passed(outs, sems, w, s, c).start()
        for w, s in pairs:
            passed(outs, sems, w, s, 1 - c).wait_recv()
        for w, s in pairs:
            ici(ins, outs, sems, w, s).wait_send()
            passed(outs, sems, w, s, c).wait_send()

    return _Payload(bufs, [jax.ShapeDtypeStruct(b.shape, b.dtype) for b in bufs], {w: w for w in range(n)},
                    [pltpu.SemaphoreType.DMA((3 * n,))] * 4, start, finish)


def _sym_payload(operands, outs, copies, n_copies):
    def start(ins, outs_, sems):
        for cp in copies(ins, outs_, sems[0], sems[1]):
            cp.start()

    def finish(ins, outs_, sems):
        for cp in copies(ins, outs_, sems[0], sems[1]):
            cp.wait()

    p = _Payload(operands, outs, {}, [pltpu.SemaphoreType.DMA((n_copies,))] * 2, start, finish)
    p.copies, p.n_copies = copies, n_copies
    return p


def _swap_payload(g4s):
    def copies(ins, outs, ss, rs):
        x, y, c, me = _where()
        cps = []
        for w, g in enumerate(g4s):
            h = g.shape[1] // 2
            cps.append(pltpu.make_async_remote_copy(
                src_ref=ins[w].at[:, pl.ds((1 - c) * h, h)], dst_ref=outs[w], send_sem=ss.at[w],
                recv_sem=rs.at[w], device_id=(x, y, 1 - c), device_id_type=MESH))
        return cps

    outs = [jax.ShapeDtypeStruct((4, g.shape[1] // 2, g.shape[2]), g.dtype) for g in g4s]
    return _sym_payload(g4s, outs, copies, len(g4s))


def _exchange_payload(pbs):
    def copies(ins, outs, ss, rs):
        x, y, c, me = _where()
        cps = []
        for w in range(len(pbs)):
            for s in range(3):
                k = (me + 1 + s) % 4
                cps.append(pltpu.make_async_remote_copy(
                    src_ref=ins[w].at[k], dst_ref=outs[w].at[2 - s], send_sem=ss.at[3 * w + s],
                    recv_sem=rs.at[3 * w + s], device_id=_chip_dev(k, c), device_id_type=MESH))
        return cps

    outs = [jax.ShapeDtypeStruct((3,) + p.shape[1:], p.dtype) for p in pbs]
    return _sym_payload(pbs, outs, copies, 3 * len(pbs))


HBM_REF = pl.BlockSpec(memory_space=pltpu.HBM)
SEM_REF = pl.BlockSpec(memory_space=pltpu.SEMAPHORE)
DATAFLOW = pltpu.SideEffectType.DATAFLOW_SIDE_EFFECTING


class _SemList:
    def __init__(self, refs):
        self.refs = refs

    @property
    def at(self):
        return self.refs


def _split_start(p, name):
    n_in, n_out, nc = len(p.operands), len(p.outs), p.n_copies
    lands = getattr(p, "lands", None) or [lax.empty(s.shape, s.dtype) for s in p.outs]

    def body(*refs):
        ins, lnd = refs[:n_in], refs[n_in:n_in + n_out]
        sems = refs[n_in + n_out:n_in + n_out + 2 * nc]
        for cp in p.copies(ins, lnd, _SemList(sems[:nc]), _SemList(sems[nc:])):
            cp.start()
        refs[-1][...] = jnp.zeros((8, 128), F32)

    res = pl.pallas_call(
        body, name=name,
        in_specs=[HBM_REF] * (n_in + n_out),
        out_specs=[SEM_REF] * (2 * nc) + [HBM_REF] * (n_in + n_out) + [VMEM_FULL],
        out_shape=([pltpu.SemaphoreType.DMA(())] * (2 * nc) + _hbm_out(p.operands) + _hbm_out(lands)
                   + [jax.ShapeDtypeStruct((8, 128), F32)]),
        input_output_aliases={i: 2 * nc + i for i in range(n_in + n_out)},
        compiler_params=pltpu.CompilerParams(has_side_effects=DATAFLOW),
    )(*_hbm(*p.operands, *lands))
    k = 2 * nc
    return list(res[:k]), list(res[k:k + n_in]), list(res[k + n_in:k + n_in + n_out]), res[-1]


def _split_wait(p, handle, after, name):
    sems, srcs, lands, _ = handle
    n_in, n_out, nc = len(srcs), len(lands), p.n_copies

    def body(*refs):
        ins, lnd = refs[:n_in], refs[n_in:n_in + n_out]
        sm = refs[n_in + n_out:n_in + n_out + 2 * nc]
        for cp in p.copies(ins, lnd, _SemList(sm[:nc]), _SemList(sm[nc:])):
            cp.wait_send()
            cp.wait_recv()

    res = pl.pallas_call(
        body, name=name,
        in_specs=[HBM_REF] * (n_in + n_out) + [SEM_REF] * (2 * nc) + [ANY] * len(after),
        out_specs=[HBM_REF] * (n_in + n_out), out_shape=_hbm_out(srcs) + _hbm_out(lands),
        input_output_aliases={i: i for i in range(n_in + n_out)},
        compiler_params=pltpu.CompilerParams(has_side_effects=DATAFLOW),
    )(*srcs, *lands, *sems, *after)
    return list(res[:n_in]), list(res[n_in:])


def _join_payload(halves):
    def copies(ins, outs, ss, rs):
        x, y, c, me = _where()
        return [pltpu.make_async_remote_copy(
            src_ref=ins[w], dst_ref=outs[w], send_sem=ss.at[w], recv_sem=rs.at[w],
            device_id=(x, y, 1 - c), device_id_type=MESH) for w in range(len(halves))]

    outs = [jax.ShapeDtypeStruct(a.shape, a.dtype) for a in halves]
    return _sym_payload(halves, outs, copies, len(halves))


def _allgather_payload(v):
    def copies(ins, outs, ss, rs):
        x, y, c, me = _where()
        lin = 4 * x + 2 * y + c
        cps = []
        for o in range(1, 8):
            t = (lin + o) % 8
            cps.append(pltpu.make_async_remote_copy(
                src_ref=ins[0], dst_ref=outs[0].at[lin], send_sem=ss.at[o - 1], recv_sem=rs.at[o - 1],
                device_id=(t // 4, (t // 2) % 2, t % 2), device_id_type=MESH))
        return cps

    p = _sym_payload([v], [jax.ShapeDtypeStruct((8,) + v.shape, v.dtype)], copies, 7)
    x, y, c, _ = _where()
    p.lands = [lax.dynamic_update_slice(jnp.zeros((8,) + v.shape, v.dtype), v[None], (4 * x + 2 * y + c, 0, 0))]
    return p


def _sum8(buf, token):
    _, P, C = buf.shape

    def body(b_ref, t_ref, o_ref):
        acc = b_ref[0]
        for d in range(1, 8):
            acc = acc + b_ref[d]
        o_ref[...] = acc

    return pl.pallas_call(
        body, name="sum8", in_specs=[VMEM_FULL, VMEM_FULL], out_specs=VMEM_FULL,
        out_shape=jax.ShapeDtypeStruct((P, C), F32),
        compiler_params=pltpu.CompilerParams(vmem_limit_bytes=32 << 20),
    )(buf, token)


def _row_tile(h):
    for t in (256, 176, 128, 64, 32, 16, 8):
        if h % t == 0:
            return t
    raise ValueError(h)


def _pair_sum(cmidx, g4, got, name):
    _, R, C = g4.shape
    h = R // 2
    th = _row_tile(h)

    def body(cm_ref, a_ref, b_ref, o_ref, ob_ref):
        sm = a_ref[...] + b_ref[...]
        ob_ref[...] = sm.astype(BF16)

        @pl.when(pl.program_id(1) == cm_ref[1])
        def _():
            o_ref[...] = sm

    blk = pl.BlockSpec((None, th, C), lambda i, k, cm: (k, i, 0))
    gs = pltpu.PrefetchScalarGridSpec(
        num_scalar_prefetch=1, grid=(h // th, 4),
        in_specs=[pl.BlockSpec((None, None, th, C), lambda i, k, cm: (k, cm[0], i, 0)), blk],
        out_specs=[pl.BlockSpec((th, C), lambda i, k, cm: (i, 0)), blk])
    return pl.pallas_call(
        body, name=name, grid_spec=gs,
        out_shape=_hbm_out([jax.ShapeDtypeStruct((h, C), F32), jax.ShapeDtypeStruct((4, h, C), BF16)]),
        compiler_params=_cp(32, 2),
    )(cmidx, *_hbm(g4.reshape(4, 2, h, C), got))


def _chip_sum(own, got, name):
    h, C = own.shape
    th = _row_tile(h)

    def body(a_ref, b_ref, o_ref):
        o_ref[...] = ((a_ref[...] + b_ref[0].astype(F32)) + b_ref[1].astype(F32)) + b_ref[2].astype(F32)

    return pl.pallas_call(
        body, name=name, grid=(h // th,),
        in_specs=[pl.BlockSpec((th, C), lambda i: (i, 0)), pl.BlockSpec((3, th, C), lambda i: (0, i, 0))],
        out_specs=pl.BlockSpec((th, C), lambda i: (i, 0)),
        out_shape=_hbm_out(jax.ShapeDtypeStruct((h, C), F32)),
        compiler_params=_cp(32, 1),
    )(*_hbm(own, got))


def _adamw_math(w, g, m, v):
    m2 = B1 * m + (1.0 - B1) * g
    v2 = B2 * v + (1.0 - B2) * (g * g)
    m_hat = m2 / (1.0 - B1 ** STEP)
    v_hat = v2 / (1.0 - B2 ** STEP)
    delta = -LR * (m_hat / (jnp.sqrt(v_hat) + EPS) + WD * w)
    return delta, m2, v2


def _adamw_pair(cidx, w, mine, theirs, m, v, token, name):
    R, C = w.shape
    h = R // 2
    tr = _row_tile(h)
    nh = h // tr

    def body(c_ref, w_ref, a_ref, b_ref, m_ref, v_ref, t_ref, g_ref, d_ref, mo_ref, vo_ref):
        own = (pl.program_id(0) // nh) == c_ref[0]
        g = jnp.where(own, a_ref[...], b_ref[...])
        d, m2, v2 = _adamw_math(w_ref[...], g, m_ref[...], v_ref[...])
        g_ref[...] = g
        d_ref[...] = d
        mo_ref[...] = m2
        vo_ref[...] = v2

    blk = pl.BlockSpec((tr, C), lambda i, c: (i, 0))
    mine_blk = pl.BlockSpec((tr, C), lambda i, c: (jnp.clip(i - c[0] * nh, 0, nh - 1), 0))
    theirs_blk = pl.BlockSpec((tr, C), lambda i, c: (jnp.clip(i - (1 - c[0]) * nh, 0, nh - 1), 0))
    gs = pltpu.PrefetchScalarGridSpec(
        num_scalar_prefetch=1, grid=(R // tr,),
        in_specs=[blk, mine_blk, theirs_blk, blk, blk, pl.BlockSpec((8, 128), lambda i, c: (0, 0))],
        out_specs=[blk] * 4)
    return pl.pallas_call(
        body, name=name, grid_spec=gs, out_shape=_hbm_out([jax.ShapeDtypeStruct((R, C), F32)] * 4),
        compiler_params=_cp(32, 1),
    )(cidx, *_hbm(w, mine, theirs, m, v), token)


def _adamw(w, g, m, v, name):
    R, C = w.shape
    tr = _row_tile(R)

    def body(w_ref, g_ref, m_ref, v_ref, d_ref, mo_ref, vo_ref):
        d, m2, v2 = _adamw_math(w_ref[...], g_ref[...], m_ref[...], v_ref[...])
        d_ref[...] = d
        mo_ref[...] = m2
        vo_ref[...] = v2

    blk = pl.BlockSpec((tr, C), lambda i: (i, 0))
    return pl.pallas_call(
        body, name=name, grid=(R // tr,), in_specs=[blk] * 4, out_specs=[blk] * 3,
        out_shape=_hbm_out([jax.ShapeDtypeStruct((R, C), F32)] * 3),
        compiler_params=_cp(32, 1),
    )(*_hbm(w, g, m, v))


def _pack(arrs):
    flat = jnp.concatenate([a.reshape(-1).astype(F32) for a in arrs])
    rows = -(-flat.shape[0] // 1024)
    rows = -(-rows // 8) * 8
    return jnp.pad(flat, (0, rows * 1024 - flat.shape[0])).reshape(rows, 1024)


def _unpack(packed, shapes):
    flat = packed.reshape(-1)
    out, off = [], 0
    for s in shapes:
        n = math.prod(s)
        out.append(flat[off:off + n].reshape(s))
        off += n
    return out


BIG = ["ffn1_w_in", "ffn1_w_out", "mix_w_in", "conv_w_out", "ssm_w_glu", "mix_w_out",
       "ffn2_w_in", "ffn2_w_out", "ple_w_in", "ple_w_gate"]
SMALL = ["ln1_g", "ln1_b", "conv_w", "conv_b", "ssm_lam_re", "ssm_lam_im", "ssm_log_step", "ssm_b_re", "ssm_b_im",
         "ssm_c_re", "ssm_c_im", "ssm_d", "ln2_g", "ln2_b", "ln3_g", "ln3_b", "ln4_g", "ln4_b"]
WEIGHTS = ["ffn1_w_in", "ffn1_w_out", "ln1_g", "ln1_b", "mix_w_in", "conv_w", "conv_b", "conv_w_out",
           "ssm_lam_re", "ssm_lam_im", "ssm_log_step", "ssm_b_re", "ssm_b_im", "ssm_c_re", "ssm_c_im", "ssm_d",
           "ssm_w_glu", "mix_w_out", "ln2_g", "ln2_b", "ffn2_w_in", "ffn2_w_out", "ln3_g", "ln3_b",
           "ple_w_in", "ple_w_gate", "ln4_g", "ln4_b"]


def _s5_operands(sp):
    abr, abi, bbr, bbi = _zoh(sp["ssm_lam_re"], sp["ssm_lam_im"], sp["ssm_log_step"], sp["ssm_b_re"], sp["ssm_b_im"])
    return (_wb_blocks(bbr), _wb_blocks(bbi), _wc_blocks(sp["ssm_c_re"]), _wc_blocks(-sp["ssm_c_im"]),
            abr.reshape(1, LANES), abi.reshape(1, LANES), sp["ssm_d"].reshape(1, SSM))


def _local_step(x, p, target, sp, ops, sched):
    W = sched.W
    wb_re, wb_im, wc_re4, wc_im4, a_re, a_im, dvec = ops
    tm = TOKEN_TILE

    def run(fn, name, *args, **kw):
        outs, got = fn(*args, comm=sched.carry(name), **kw)
        sched.landed(name, got)
        sched.done[name] = outs[0]
        return outs

    def dw(name, wname, a, b, tk, tn, shape4, shard_cols=None, interleaved=False):
        out, got = _mm_tn(a, b, tk, tn, name, shard_cols=shard_cols, interleaved=interleaved,
                          comm=sched.carry(name))
        sched.landed(name, got)
        sched.done[name] = out
        sched.grad(wname, out.reshape(shape4))

    h1, r1, x1, x1b, xb = run(_ffn_fwd, "ffn1_fwd", x, W["ffn1_w_in"], W["ffn1_w_out"].reshape(2, FFH, D),
                              sp["ln1_g"], sp["ln1_b"], tm, "ffn1_fwd")
    conv_w = W["conv_w"][:, 0:3, :].transpose(1, 0, 2).reshape(3, CONV)
    pc, z_b, yin_b, su, su_b, g_conv, g_ssm, y_conv = run(
        _mix_fwd_a, "mix_fwd_a", x1b, W["mix_w_in"], conv_w, sp["conv_b"], W["conv_w_out"], tm)
    st_re, st_im = run(_s5_scan_fwd, "s5_scan_fwd", su_b, wb_re, wb_im, a_re, a_im)
    w_mo = W["mix_w_out"].reshape(D, D)
    s, sg_b, ga, gb, merged_b, r2, x2 = run(
        _mix_fwd_b, "mix_fwd_b", st_re, st_im, wc_re4, wc_im4, su, dvec, W["ssm_w_glu"], g_conv, g_ssm, y_conv,
        w_mo, x1, sp["ln2_g"], sp["ln2_b"], tm)
    w2o2 = W["ffn2_w_out"].reshape(2, FFH, D)
    h2, r3, x3, x3b, x2b = run(_ffn_fwd, "ffn2_fwd", x2, W["ffn2_w_in"], w2o2, sp["ln3_g"], sp["ln3_b"], tm,
                               "ffn2_fwd")
    loss_part, dx3, p_b, dpw_b, dgt_b, dg4, db4 = _ple_loss(
        x3, x3b, p, W["ple_w_in"], W["ple_w_gate"].reshape(D, D), sp["ln4_g"], sp["ln4_b"], target, tm)

    dw("dw_ple_gate", "ple_w_gate", x3b, dgt_b, 512, 1024, (4, 256, D))
    dw("dw_ple_in", "ple_w_in", p_b, dpw_b, 256, 256, (4, 256, 256), shard_cols=256)
    dx2, dh2, a2_b, df2_b, dg3, db3 = run(_ffn_bwd, "ffn2_bwd", dx3, r3, sp["ln3_g"], h2, W["ffn2_w_in"], w2o2,
                                          tm, "ffn2_bwd")
    dw("dw_ffn2_in", "ffn2_w_in", x2b, dh2, 512, FFH, (4, D, FFH), shard_cols=FFH, interleaved=True)
    dw("dw_ffn2_out", "ffn2_w_out", a2_b, df2_b, FFH, 1024, (4, FF // 4, D))
    (dres, dmix_b, dgl_b, ds_b, du_dir, gs_re, gs_im, dyc_b, dproj, dg2, db2, dd) = run(
        _mix_bwd_b, "mix_bwd_b", dx2, r2, sp["ln2_g"], w_mo, g_conv, g_ssm, y_conv, ga, gb, s, su, dvec,
        W["ssm_w_glu"], wc_re4, wc_im4, tm)
    dw("dw_mix_out", "mix_w_out", merged_b, dmix_b, 512, 1024, (4, 256, D))
    dw("dw_glu", "ssm_w_glu", sg_b, dgl_b, 512, 512, (4, SSM, 512), shard_cols=512)
    dsu_ssm, dwb_re, dwb_im, dwc_re, dwc_im, da_re, da_im = run(
        _s5_scan_bwd, "s5_scan_bwd", gs_re, gs_im, st_re, st_im, su_b, ds_b, wb_re, wb_im, a_re, a_im)
    dw("dw_conv_out", "conv_w_out", yin_b, dyc_b, 512, 256, (4, CONV, 256), shard_cols=256)
    dproj, dx1, dcw8, dcb = run(_mix_bwd_a, "mix_bwd_a", dyc_b, W["conv_w_out"], pc, z_b, conv_w, dsu_ssm,
                                du_dir, dproj, dres, W["mix_w_in"], tm)
    dw("dw_mix_in", "mix_w_in", x1b, dproj, 512, 1024, (4, D, D), shard_cols=1024)
    dx0, dh1, a1_b, df1_b, dg1, db1 = run(_ffn_bwd, "ffn1_bwd", dx1, r1, sp["ln1_g"], h1, W["ffn1_w_in"],
                                          W["ffn1_w_out"].reshape(2, FFH, D), tm, "ffn1_bwd")
    sched.small(dict(
        ln1_g=dg1, ln1_b=db1, ln2_g=dg2, ln2_b=db2, ln3_g=dg3, ln3_b=db3, ln4_g=dg4, ln4_b=db4,
        conv_w=dcw8[0:3], conv_b=dcb,
        a_re=da_re.reshape(GROUPS, STATE), a_im=da_im.reshape(GROUPS, STATE),
        bb_re=_wb_diag(dwb_re), bb_im=_wb_diag(dwb_im),
        ssm_c_re=_wc_diag(dwc_re), ssm_c_im=-_wc_diag(dwc_im), ssm_d=dd.reshape(GROUPS, 16),
        loss=loss_part[0:1, 0]))
    dw("dw_ffn1_in", "ffn1_w_in", xb, dh1, 512, FFH, (4, D, FFH), shard_cols=FFH, interleaved=True)
    dw("dw_ffn1_out", "ffn1_w_out", a1_b, df1_b, FFH, 1024, (4, FF // 4, D))
    return loss_part[0, 0], dx0


RAW_ORDER = ["ln1_g", "ln1_b", "ln2_g", "ln2_b", "ln3_g", "ln3_b", "ln4_g", "ln4_b", "conv_w", "conv_b",
             "a_re", "a_im", "bb_re", "bb_im", "ssm_c_re", "ssm_c_im", "ssm_d", "loss"]

GATHER_FIRST = ["ffn1_w_in", "ffn1_w_out"]
GATHER_AT = {"ffn1_fwd": ["mix_w_in", "conv_w_out", "conv_w"], "mix_fwd_a": ["ssm_w_glu", "mix_w_out"],
             "s5_scan_fwd": ["ffn2_w_in"], "mix_fwd_b": ["ffn2_w_out"], "ffn2_fwd": ["ple_w_in", "ple_w_gate"]}
REDUCE_GROUP = {"ffn2": ["ple_w_gate", "ple_w_in", "ffn2_w_in", "ffn2_w_out"],
                "mix": ["mix_w_out", "ssm_w_glu", "conv_w_out", "mix_w_in"], "ffn1": ["ffn1_w_in", "ffn1_w_out"]}
REDUCE_AT = {"mix_bwd_b": [("swap", "ffn2")], "mix_bwd_a": [("join", "ffn2")]}
BEGIN_AT = {"dw_mix_out": [("exchange", "ffn2")], "ffn1_bwd": [("swap", "mix")],
            "dw_ffn1_in": [("small", None), ("exchange", "mix")]}
BEHIND = {"dw_glu": [("exchange", "ffn2")], "s5_scan_bwd": [("exchange", "ffn2")]}
END_AT = {"mix_bwd_a": [("exchange", "ffn2", ["dw_mix_out", "dw_glu", "s5_scan_bwd"])],
          "dw_ffn1_in": [("swap", "mix", ["ffn1_bwd"])]}
LAST_GROUP = "ffn1"


class _Sched:
    def __init__(self, cmidx):
        self.bufs, self.cmidx = {}, cmidx
        self.W, self.G, self.raw, self.small_buf = {}, {}, None, None
        self.got1, self.p32, self.pbf, self.got2, self.half, self.theirs = {}, {}, {}, {}, {}, {}
        self._open, self._split, self.done = [], {}, {}

    def first_begin(self, bufs):
        self.bufs.update(bufs)
        p = _gather_ici_payload([bufs[n] for n in GATHER_FIRST])
        self._first = (p, _split_start(p, "gather_first_start"))
        return self._first[1][3]

    def first_end(self, bufs, after):
        self.bufs.update(bufs)
        p, handle = self._first
        _, landed = _split_wait(p, handle, after, "gather_first_wait")
        (outs,) = _comm_call("gather_first_pass", [_gather_pass_payload(landed)])
        self.W.update(zip(GATHER_FIRST, outs))

    def _payload(self, stage, key):
        if stage == "gather":
            return _gather_payload([self.bufs[n] for n in key])
        if stage == "small":
            return _allgather_payload(_pack([self.raw[k] for k in RAW_ORDER]))
        names = REDUCE_GROUP[key]
        if stage == "swap":
            return _swap_payload([self.G[n] for n in names])
        if stage == "exchange":
            for n in names:
                self.p32[n], self.pbf[n] = _pair_sum(self.cmidx, self.G[n], self.got1[n], "pair_sum_" + n)
            return _exchange_payload([self.pbf[n] for n in names])
        for n in names:
            self.half[n] = _chip_sum(self.p32[n], self.got2[n], "chip_sum_" + n)
        return _join_payload([self.half[n] for n in names])

    def _store(self, stages, got):
        for (stage, key), outs in zip(stages, got):
            if stage == "gather":
                self.W.update(zip(key, outs))
            elif stage == "small":
                self.small_buf = outs[0]
            else:
                {"swap": self.got1, "exchange": self.got2, "join": self.theirs}[stage].update(
                    zip(REDUCE_GROUP[key], outs))

    def _standalone(self, name, stages):
        self._store(stages, _comm_call(name, [self._payload(s, k) for s, k in stages]))

    def carry(self, name):
        for stage, key, behind in END_AT.get(name, []):
            self._end(stage, key, [self.done[b] for b in behind])
        tokens = [self._begin(stage, key) for stage, key in BEGIN_AT.get(name, [])]
        tokens += [self._split[sk][1][3] for sk in BEHIND.get(name, [])]
        self._open = [("gather", GATHER_AT[name])] if name in GATHER_AT else []
        self._open += REDUCE_AT.get(name, [])
        comm = [self._payload(s, k) for s, k in self._open]
        if tokens:
            comm.append(_Payload(tokens, [], {}, [], lambda *a: None, lambda *a: None))
        return tuple(comm)

    def landed(self, name, got):
        self._store(self._open, got)

    def grad(self, name, g4):
        self.G[name] = g4

    def small(self, raw):
        self.raw = raw

    def _begin(self, stage, key):
        p = self._payload(stage, key)
        self._split[stage, key] = (p, _split_start(p, "%s_%s_start" % (stage, key)))
        return self._split[stage, key][1][3]

    def _end(self, stage, key, after):
        p, handle = self._split.pop((stage, key))
        srcs, lands = _split_wait(p, handle, after, "%s_%s_wait" % (stage, key))
        if stage == "swap":
            self.G.update(zip(REDUCE_GROUP[key], srcs))
        self._store([(stage, key)], [lands])

    def tail_begin(self):
        return self._begin("swap", LAST_GROUP)

    def tail_mid(self, after):
        self._end("swap", LAST_GROUP, after)
        token = self._begin("exchange", LAST_GROUP)
        self._end("small", None, [token])
        self._end("exchange", "mix", [token])
        self._standalone("reduce_tail_join_mix", [("join", "mix")])
        return token

    def tail_end(self, after):
        self._end("exchange", LAST_GROUP, after)
        self._standalone("reduce_tail_join", [("join", LAST_GROUP)])


def _small_grads(raw_sum, sp):
    _, vjp = jax.vjp(_zoh, sp["ssm_lam_re"], sp["ssm_lam_im"], sp["ssm_log_step"], sp["ssm_b_re"], sp["ssm_b_im"])
    d_lre, d_lim, d_ls, d_bre, d_bim = vjp((raw_sum["a_re"], raw_sum["a_im"], raw_sum["bb_re"], raw_sum["bb_im"]))
    g = {k: raw_sum[k] for k in ("ln1_g", "ln1_b", "ln2_g", "ln2_b", "ln3_g", "ln3_b", "ln4_g", "ln4_b",
                                 "conv_w", "conv_b", "ssm_c_re", "ssm_c_im", "ssm_d")}
    g.update(ssm_lam_re=d_lre, ssm_lam_im=d_lim, ssm_log_step=d_ls, ssm_b_re=d_bre, ssm_b_im=d_bim)
    return g


def kernel(x, p, ffn1_w_in, ffn1_w_out, ln1_g, ln1_b, mix_w_in, conv_w, conv_b, conv_w_out, ssm_lam_re, ssm_lam_im, ssm_log_step, ssm_b_re, ssm_b_im, ssm_c_re, ssm_c_im, ssm_d, ssm_w_glu, mix_w_out, ln2_g, ln2_b, ffn2_w_in, ffn2_w_out, ln3_g, ln3_b, ple_w_in, ple_w_gate, ln4_g, ln4_b, loss_target, m_ffn1_w_in, m_ffn1_w_out, m_ln1_g, m_ln1_b, m_mix_w_in, m_conv_w, m_conv_b, m_conv_w_out, m_ssm_lam_re, m_ssm_lam_im, m_ssm_log_step, m_ssm_b_re, m_ssm_b_im, m_ssm_c_re, m_ssm_c_im, m_ssm_d, m_ssm_w_glu, m_mix_w_out, m_ln2_g, m_ln2_b, m_ffn2_w_in, m_ffn2_w_out, m_ln3_g, m_ln3_b, m_ple_w_in, m_ple_w_gate, m_ln4_g, m_ln4_b, v_ffn1_w_in, v_ffn1_w_out, v_ln1_g, v_ln1_b, v_mix_w_in, v_conv_w, v_conv_b, v_conv_w_out, v_ssm_lam_re, v_ssm_lam_im, v_ssm_log_step, v_ssm_b_re, v_ssm_b_im, v_ssm_c_re, v_ssm_c_im, v_ssm_d, v_ssm_w_glu, v_mix_w_out, v_ln2_g, v_ln2_b, v_ffn2_w_in, v_ffn2_w_out, v_ln3_g, v_ln3_b, v_ple_w_in, v_ple_w_gate, v_ln4_g, v_ln4_b):
    args = dict(locals())
    w = {n: args[n] for n in WEIGHTS}
    m = {n: args["m_" + n] for n in WEIGHTS}
    v = {n: args["v_" + n] for n in WEIGHTS}
    _, _, c, me = _where()
    cidx = jnp.stack([c, me]).astype(jnp.int32)
    meidx = jnp.reshape(me, (1,)).astype(jnp.int32)

    sched = _Sched(cidx)
    token = sched.first_begin({n: _slot_cast(meidx, w[n][0], BF16, "cast_" + n) for n in GATHER_FIRST})
    rest = {n: _slot_cast(meidx, w[n][0], BF16, "cast_" + n, (token,)) for n in BIG if n not in GATHER_FIRST}
    rest["conv_w"] = _slot_cast(meidx, jnp.pad(conv_w[0], ((0, 13), (0, 0))), F32, "cast_conv_w", (token,))
    sp = {n: (w[n] if w[n].ndim == 2 and n != "ssm_log_step" else w[n][0]) for n in SMALL if n != "conv_w"}
    ops = _s5_operands({**sp, "ssm_lam_re": sp["ssm_lam_re"] + token[0, 0]})
    sched.first_end(rest, list(rest.values()) + list(ops))
    loss_part, dx0 = _local_step(x[0], p[0, 0], loss_target[0], sp, ops, sched)
    out_g, out_d, out_m, out_v = {}, {}, {}, {}

    def big_adamw(names, token):
        for n in names:
            g, dl, mn, vn = _adamw_pair(cidx, w[n][0], sched.half[n], sched.theirs[n], m[n][0], v[n][0], token,
                                        "adamw_" + n)
            out_g[n], out_d[n], out_m[n], out_v[n] = g[None], dl[None], mn[None], vn[None]

    first = ["ple_w_gate", "ple_w_in", "ffn2_w_in"]
    big_adamw(first, sched.tail_begin())
    token = sched.tail_mid([out_v[n] for n in first])
    big_adamw(["ffn2_w_out"], token)

    raw_shapes = [sched.raw[k].shape for k in RAW_ORDER]
    raw_sum = dict(zip(RAW_ORDER, _unpack(_sum8(sched.small_buf, token), raw_shapes)))
    loss = raw_sum["loss"][0]
    sg = _small_grads(raw_sum, sp)
    sg["conv_w"] = lax.dynamic_slice_in_dim(sg["conv_w"], me * 128, 128, axis=1)
    small_shapes = [w[n].shape for n in SMALL]
    gp = _pack([sg[n] for n in SMALL])
    d_s, m_s, v_s = _adamw(_pack([w[n] for n in SMALL]), gp, _pack([m[n] for n in SMALL]),
                           _pack([v[n] for n in SMALL]), "adamw_small")

    for n, a, b_, c_, d_ in zip(SMALL, _unpack(gp, small_shapes), _unpack(d_s, small_shapes),
                                _unpack(m_s, small_shapes), _unpack(v_s, small_shapes)):
        out_g[n], out_d[n], out_m[n], out_v[n] = a, b_, c_, d_
    big_adamw(REDUCE_GROUP["mix"], token)
    sched.tail_end([d_s, out_v["ffn2_w_out"]] + [out_v[n] for n in REDUCE_GROUP["mix"]])
    big_adamw(REDUCE_GROUP[LAST_GROUP], token)

    return (loss, dx0[None], *[out_g[n] for n in WEIGHTS], *[out_d[n] for n in WEIGHTS],
            *[out_m[n] for n in WEIGHTS], *[out_v[n] for n in WEIGHTS])
```

```python
import functools
import math

import jax
import jax.numpy as jnp
import numpy as np
from jax import lax
from jax.experimental import pallas as pl
from jax.experimental.pallas import tpu as pltpu

F32, BF16 = jnp.float32, jnp.bfloat16
D = 1024
FF = 2816
FFH = FF // 2
CONV = 512
SSM = 512
GROUPS = 32
STATE = 64
LANES = GROUPS * STATE
SCAN_W = 128
SCAN_PER = 512 // SCAN_W
SCAN_GR = SCAN_W // STATE
SCAN_R = 256
TOKEN_TILE = 256
ALPHA = 2.0 ** 0.25
LN_EPS = 1e-5
GELU_C = math.sqrt(2.0 / math.pi)
B1, B2, LR, EPS, WD, STEP = 0.9, 0.999, 0.001, 1e-8, 0.01, 10
MESH = pl.DeviceIdType.MESH
ANY = pl.BlockSpec(memory_space=pl.ANY)
VMEM_FULL = pl.BlockSpec(memory_space=pltpu.VMEM)


def _cp(vmem_mb=48, n_axes=1):
    return pltpu.CompilerParams(vmem_limit_bytes=vmem_mb << 20,
                                dimension_semantics=("arbitrary",) * n_axes)


def _hbm(*arrs):
    return [pltpu.with_memory_space_constraint(a, pltpu.HBM) for a in arrs]


def _hbm_out(shapes):
    if isinstance(shapes, (list, tuple)):
        return [pltpu.HBM(s.shape, s.dtype) for s in shapes]
    return pltpu.HBM(shapes.shape, shapes.dtype)


def _nn(a, b):
    return jnp.dot(a, b, preferred_element_type=F32)


def _nt(a, b):
    return lax.dot_general(a, b, (((1,), (1,)), ((), ())), preferred_element_type=F32)


def _tn(a, b):
    return lax.dot_general(a, b, (((0,), (0,)), ((), ())), preferred_element_type=F32)


def _sig(v):
    return jax.nn.sigmoid(v)


def _ln_stats(r):
    mu = jnp.mean(r, axis=-1, keepdims=True)
    xc = r - mu
    var = jnp.mean(xc * xc, axis=-1, keepdims=True)
    rstd = lax.rsqrt(var + LN_EPS)
    return xc * rstd, rstd


def _ln_bwd(dy, r, g):
    xhat, rstd = _ln_stats(r)
    dyg = dy * g
    m1 = jnp.mean(dyg, axis=-1, keepdims=True)
    m2 = jnp.mean(dyg * xhat, axis=-1, keepdims=True)
    return rstd * (dyg - m1 - xhat * m2), xhat


def _rowsum(v):
    return jnp.sum(v, axis=0, keepdims=True)


class _Payload:
    def __init__(self, operands, outs, aliases, sems, start, finish):
        self.operands, self.outs, self.aliases, self.sems = list(operands), list(outs), dict(aliases), list(sems)
        self.start, self.finish = start, finish


def _split(flat, comm, attr):
    out, i = [], 0
    for p in comm:
        n = len(getattr(p, attr))
        out.append(list(flat[i:i + n]))
        i += n
    return out


def _run_comm(comm, which, cin, cout, csem):
    for p, a, b, s in zip(comm, _split(cin, comm, "operands"), _split(cout, comm, "outs"), _split(csem, comm, "sems")):
        getattr(p, which)(a, b, s)


def _pcall(body, *, name, grid, in_specs, out_specs, out_shape, operands, scratch=(), vmem_mb=48, aliases=None,
           comm=()):
    ni, no, ns = len(in_specs), len(out_specs), len(scratch)
    c_ops = [a for p in comm for a in p.operands]
    c_outs = [s for p in comm for s in p.outs]
    c_sems = [s for p in comm for s in p.sems]
    io = dict(aliases or {})
    off_i, off_o = ni, no
    for p in comm:
        for a, b in p.aliases.items():
            io[off_i + a] = off_o + b
        off_i += len(p.operands)
        off_o += len(p.outs)

    def wrapped(*refs):
        ins, cin = refs[:ni], refs[ni:ni + len(c_ops)]
        o0 = ni + len(c_ops)
        outs, cout = refs[o0:o0 + no], refs[o0 + no:o0 + no + len(c_outs)]
        s0 = o0 + no + len(c_outs)
        scr, csem = refs[s0:s0 + ns], refs[s0 + ns:]
        if comm:
            first = functools.reduce(jnp.logical_and, [pl.program_id(a) == 0 for a in range(len(grid))])
            pl.when(first)(lambda: _run_comm(comm, "start", cin, cout, csem))
        body(*ins, *outs, *scr)
        if comm:
            last = functools.reduce(jnp.logical_and, [pl.program_id(a) == grid[a] - 1 for a in range(len(grid))])
            pl.when(last)(lambda: _run_comm(comm, "finish", cin, cout, csem))

    res = pl.pallas_call(
        wrapped, name=name, grid=grid,
        in_specs=list(in_specs) + [ANY] * len(c_ops), out_specs=list(out_specs) + [ANY] * len(c_outs),
        out_shape=_hbm_out(list(out_shape) + c_outs), scratch_shapes=list(scratch) + c_sems,
        input_output_aliases=io,
        compiler_params=pltpu.CompilerParams(vmem_limit_bytes=vmem_mb << 20,
                                             dimension_semantics=("arbitrary",) * len(grid),
                                             has_side_effects=bool(c_sems)),
    )(*_hbm(*operands, *c_ops))
    return list(res[:no]), _split(res[no:], comm, "outs")


def _comm_call(name, comm):
    c_ops = [a for p in comm for a in p.operands]
    c_outs = [s for p in comm for s in p.outs]
    c_sems = [s for p in comm for s in p.sems]
    io, off_i, off_o = {}, 0, 0
    for p in comm:
        for a, b in p.aliases.items():
            io[off_i + a] = off_o + b
        off_i += len(p.operands)
        off_o += len(p.outs)

    def body(*refs):
        cin, cout = refs[:len(c_ops)], refs[len(c_ops):len(c_ops) + len(c_outs)]
        csem = refs[len(c_ops) + len(c_outs):]
        _run_comm(comm, "start", cin, cout, csem)
        _run_comm(comm, "finish", cin, cout, csem)

    res = pl.pallas_call(
        body, name=name, in_specs=[ANY] * len(c_ops), out_specs=[ANY] * len(c_outs), out_shape=_hbm_out(c_outs),
        scratch_shapes=c_sems, input_output_aliases=io,
        compiler_params=pltpu.CompilerParams(has_side_effects=True),
    )(*_hbm(*c_ops))
    return _split(res, comm, "outs")


def _ffn_fwd(x, w_in4, w_out2, g, b, tm, name, comm=()):
    T = x.shape[0]

    def body(x_ref, win_ref, wo_ref, g_ref, b_ref, h_ref, r_ref, xo_ref, xob_ref, xib_ref):
        xf = x_ref[...]
        xv = xf.astype(BF16)
        xib_ref[...] = xv
        acc = ALPHA * xf
        for k in range(2):
            gt = _nn(xv, win_ref[k])
            up = _nn(xv, win_ref[k + 2])
            a = (gt * _sig(gt) * up).astype(BF16)
            h_ref[:, 2 * k * FFH:(2 * k + 1) * FFH] = gt.astype(BF16)
            h_ref[:, (2 * k + 1) * FFH:(2 * k + 2) * FFH] = up.astype(BF16)
            acc = acc + 0.5 * _nn(a, wo_ref[k])
        xhat, _ = _ln_stats(acc)
        xo = xhat * g_ref[...] + b_ref[...]
        r_ref[...] = acc
        xo_ref[...] = xo
        xob_ref[...] = xo.astype(BF16)

    tok = pl.BlockSpec((tm, D), lambda i: (i, 0))
    vec = pl.BlockSpec((1, D), lambda i: (0, 0))
    return _pcall(
        body, name=name, grid=(T // tm,),
        in_specs=[tok,
                  pl.BlockSpec((4, D, FFH), lambda i: (0, 0, 0), pipeline_mode=pl.Buffered(1)),
                  pl.BlockSpec((2, FFH, D), lambda i: (0, 0, 0), pipeline_mode=pl.Buffered(1)),
                  vec, vec],
        out_specs=[pl.BlockSpec((tm, 2 * FF), lambda i: (i, 0)), tok, tok, tok, tok],
        out_shape=[jax.ShapeDtypeStruct((T, 2 * FF), BF16), jax.ShapeDtypeStruct((T, D), F32),
                   jax.ShapeDtypeStruct((T, D), F32), jax.ShapeDtypeStruct((T, D), BF16),
                   jax.ShapeDtypeStruct((T, D), BF16)],
        vmem_mb=58, comm=comm, operands=(x, w_in4, w_out2, g, b))


def _ffn_bwd(dy, r, g, h, w_in4, w_out2, tm, name, comm=()):
    T = dy.shape[0]

    def body(dy_ref, r_ref, g_ref, h_ref, win_ref, wo_ref, dx_ref, dh_ref, a_ref, df_ref, dg_ref, db_ref):
        i = pl.program_id(0)
        dyv = dy_ref[...]
        dr, xhat = _ln_bwd(dyv, r_ref[...], g_ref[...])
        dg_ref[...] = jnp.where(i == 0, 0.0, dg_ref[...]) + _rowsum(dyv * xhat)
        db_ref[...] = jnp.where(i == 0, 0.0, db_ref[...]) + _rowsum(dyv)
        dfb = (0.5 * dr).astype(BF16)
        df_ref[...] = dfb
        acc = ALPHA * dr
        for k in range(2):
            da = _nt(dfb, wo_ref[k])
            gt = h_ref[:, 2 * k * FFH:(2 * k + 1) * FFH].astype(F32)
            up = h_ref[:, (2 * k + 1) * FFH:(2 * k + 2) * FFH].astype(F32)
            sg = _sig(gt)
            silu = gt * sg
            dgate = (da * up * (sg * (1.0 + gt * (1.0 - sg)))).astype(BF16)
            dup = (da * silu).astype(BF16)
            a_ref[:, k * FFH:(k + 1) * FFH] = (silu * up).astype(BF16)
            dh_ref[:, 2 * k * FFH:(2 * k + 1) * FFH] = dgate
            dh_ref[:, (2 * k + 1) * FFH:(2 * k + 2) * FFH] = dup
            acc = acc + _nt(dgate, win_ref[k]) + _nt(dup, win_ref[k + 2])
        dx_ref[...] = acc

    tok = pl.BlockSpec((tm, D), lambda i: (i, 0))
    vec = pl.BlockSpec((1, D), lambda i: (0, 0))
    wide = pl.BlockSpec((tm, 2 * FF), lambda i: (i, 0))
    return _pcall(
        body, name=name, grid=(T // tm,),
        in_specs=[tok, tok, vec, wide,
                  pl.BlockSpec((4, D, FFH), lambda i: (0, 0, 0), pipeline_mode=pl.Buffered(1)),
                  pl.BlockSpec((2, FFH, D), lambda i: (0, 0, 0), pipeline_mode=pl.Buffered(1))],
        out_specs=[tok, wide, pl.BlockSpec((tm, FF), lambda i: (i, 0)), tok, vec, vec],
        out_shape=[jax.ShapeDtypeStruct((T, D), F32), jax.ShapeDtypeStruct((T, 2 * FF), BF16),
                   jax.ShapeDtypeStruct((T, FF), BF16), jax.ShapeDtypeStruct((T, D), BF16),
                   jax.ShapeDtypeStruct((1, D), F32), jax.ShapeDtypeStruct((1, D), F32)],
        vmem_mb=58, comm=comm, operands=(dy, r, g, h, w_in4, w_out2))


def _mm_tn(a, b, tk, tn, name, shard_cols=None, interleaved=False, comm=()):
    T, K = a.shape
    N = b.shape[1]

    def body(a_ref, b_ref, o_ref):
        o_ref[...] = _tn(a_ref[...], b_ref[...])

    if shard_cols is None:
        out_shape = jax.ShapeDtypeStruct((K, N), F32)
        out_spec = pl.BlockSpec((tk, tn), lambda ki, nj: (ki, nj))
    else:
        per = shard_cols // tn

        def shard(nj):
            blk = nj // per
            return (blk % 2) * 2 + blk // 2 if interleaved else blk

        out_shape = jax.ShapeDtypeStruct((N // shard_cols, K, shard_cols), F32)
        out_spec = pl.BlockSpec((None, tk, tn), lambda ki, nj: (shard(nj), ki, nj % per))
    (out,), got = _pcall(
        body, name=name, grid=(K // tk, N // tn),
        in_specs=[pl.BlockSpec((T, tk), lambda ki, nj: (0, ki)), pl.BlockSpec((T, tn), lambda ki, nj: (0, nj))],
        out_specs=[out_spec], out_shape=[out_shape], comm=comm, operands=(a, b))
    return out, got


def _mix_fwd_a(xb, w_mix4, conv_w, conv_b, w_co4, tm, comm=()):
    T = xb.shape[0]

    def body(xb_ref, w_ref, cw_ref, cb_ref, wco_ref,
             pc_ref, z_ref, yin_ref, su_ref, sub_ref, gc_ref, gs_ref, yc_ref, qbuf):
        @pl.when(pl.program_id(0) == 0)
        def _():
            qbuf[pl.ds(0, 8), :] = jnp.zeros((8, CONV), F32)

        xv = xb_ref[...]
        p0 = _nn(xv, w_ref[0])
        p1 = _nn(xv, w_ref[1])
        gc_ref[...] = _nn(xv, w_ref[2]).astype(BF16)
        gs_ref[...] = _nn(xv, w_ref[3]).astype(BF16)
        cbv, ccv = p0[:, :CONV], p0[:, CONV:]
        chv, suv = p1[:, :CONV], p1[:, CONV:]
        q = ccv * chv
        qbuf[pl.ds(8, tm), :] = q
        cw = cw_ref[...]
        z = (cw[2:3] * q + cw[1:2] * qbuf[pl.ds(7, tm), :] + cw[0:1] * qbuf[pl.ds(6, tm), :]
             + cb_ref[...])
        qbuf[pl.ds(0, 8), :] = q[tm - 8:tm]
        yin = (cbv * z).astype(BF16)
        pc_ref[:, 0:CONV] = cbv.astype(BF16)
        pc_ref[:, CONV:2 * CONV] = ccv.astype(BF16)
        pc_ref[:, 2 * CONV:3 * CONV] = chv.astype(BF16)
        z_ref[...] = z.astype(BF16)
        yin_ref[...] = yin
        su_ref[...] = suv
        sub_ref[...] = suv.astype(BF16)
        for k in range(4):
            yc_ref[:, 256 * k:256 * (k + 1)] = _nn(yin, wco_ref[k]).astype(BF16)

    def tok(n):
        return pl.BlockSpec((tm, n), lambda i: (i, 0))

    def full(shape):
        return pl.BlockSpec(shape, lambda i: (0,) * len(shape))

    return _pcall(
        body, name="mix_fwd_a", grid=(T // tm,),
        in_specs=[tok(D), full((4, D, D)), full((3, CONV)), full((1, CONV)), full((4, CONV, 256))],
        out_specs=[tok(3 * CONV), tok(CONV), tok(CONV), tok(SSM), tok(SSM), tok(D), tok(D), tok(D)],
        out_shape=[jax.ShapeDtypeStruct((T, 3 * CONV), BF16), jax.ShapeDtypeStruct((T, CONV), BF16),
                   jax.ShapeDtypeStruct((T, CONV), BF16), jax.ShapeDtypeStruct((T, SSM), F32),
                   jax.ShapeDtypeStruct((T, SSM), BF16), jax.ShapeDtypeStruct((T, D), BF16),
                   jax.ShapeDtypeStruct((T, D), BF16), jax.ShapeDtypeStruct((T, D), BF16)],
        scratch=[pltpu.VMEM((tm + 8, CONV), F32)], vmem_mb=56, comm=comm,
        operands=(xb, w_mix4, conv_w, conv_b, w_co4))


def _scan_rows(bre, bim, ar, ai, T, rev, load, store=None):
    R, W, G = SCAN_R, bre.shape[1], T // 8
    if rev:
        ai = -ai

    def cmul(pr, pi, xr, xi):
        return pr * xr - pi * xi, pr * xi + pi * xr

    pw = [(ar, ai)]
    for _ in range(7):
        pw.append(cmul(ar, ai, *pw[-1]))

    def shifted(v, d, axis, n, idx):
        if rev:
            return jnp.where(idx < n - d, pltpu.roll(v, n - d, axis), 0.0)
        return jnp.where(idx >= d, pltpu.roll(v, d, axis), 0.0)

    sub8 = lax.broadcasted_iota(jnp.int32, (8, W), 0)
    inside = {d: (sub8 < 8 - d) if rev else (sub8 >= d) for d in (1, 2, 4)}
    pm = {d: (jnp.where(inside[d], pw[d - 1][0], 0.0)[None], jnp.where(inside[d], pw[d - 1][1], 0.0)[None])
          for d in (1, 2, 4)}

    def step(i, _):
        t0 = pl.multiple_of(i * R, R)
        vr, vi = load(t0)
        vr, vi = vr.reshape(R // 8, 8, W), vi.reshape(R // 8, 8, W)
        for d in (1, 2, 4):
            sh = (8 - d) if rev else d
            dr, di = cmul(pm[d][0], pm[d][1], pltpu.roll(vr, sh, 1), pltpu.roll(vi, sh, 1))
            vr, vi = vr + dr, vi + di
        bre[pl.ds(t0 + 8, R), :] = vr.reshape(R, W)
        bim[pl.ds(t0 + 8, R), :] = vi.reshape(R, W)
        return 0

    lax.fori_loop(0, T // R, step, 0)

    edge = 0 if rev else 7
    cr = bre[pl.ds(8 + edge, G, stride=8), :]
    ci = bim[pl.ds(8 + edge, G, stride=8), :]
    row = lax.broadcasted_iota(jnp.int32, (G, W), 0)
    qr, qi = pw[7]
    d = 1
    while d < G:
        dr, di = cmul(qr, qi, shifted(cr, d, 0, G, row), shifted(ci, d, 0, G, row))
        cr, ci = cr + dr, ci + di
        qr, qi = qr * qr - qi * qi, 2.0 * qr * qi
        d *= 2

    order = [7 - r for r in range(8)] if rev else list(range(8))
    if store is None:
        nr, ni = shifted(cr, 1, 0, G, row), shifted(ci, 1, 0, G, row)
        for r in range(8):
            dr, di = cmul(*pw[order[r]], nr, ni)
            bre[pl.ds(8 + r, G, stride=8), :] = bre[pl.ds(8 + r, G, stride=8), :] + dr
            bim[pl.ds(8 + r, G, stride=8), :] = bim[pl.ds(8 + r, G, stride=8), :] + di
        return
    p8r = jnp.concatenate([pw[k][0] for k in order], axis=0)[None]
    p8i = jnp.concatenate([pw[k][1] for k in order], axis=0)[None]

    def back(nre, nim):
        nre[...] = shifted(cr, 1, 0, G, row)
        nim[...] = shifted(ci, 1, 0, G, row)

        def step3(i, _):
            t0 = pl.multiple_of(i * R, R)
            g0 = pl.multiple_of(i * (R // 8), R // 8)
            br = jnp.broadcast_to(nre[pl.ds(g0, R // 8), :][:, None, :], (R // 8, 8, W))
            bi = jnp.broadcast_to(nim[pl.ds(g0, R // 8), :][:, None, :], (R // 8, 8, W))
            dr, di = cmul(p8r, p8i, br, bi)
            xr = bre[pl.ds(t0 + 8, R), :] + dr.reshape(R, W)
            xi = bim[pl.ds(t0 + 8, R), :] + di.reshape(R, W)
            store(t0, xr, xi)
            return 0

        lax.fori_loop(0, T // R, step3, 0)

    pl.run_scoped(back, pltpu.VMEM((G, W), F32), pltpu.VMEM((G, W), F32))


def _scan_specs(T):
    W = SCAN_W
    lane = pl.BlockSpec((T, W), lambda j: (0, j))
    col = pl.BlockSpec((T, 128), lambda j: (0, j // SCAN_PER))
    wb = pl.BlockSpec((None, 128, W), lambda j: (j, 0, 0))
    wc = pl.BlockSpec((None, W, 128), lambda j: (j, 0, 0))
    vec = pl.BlockSpec((1, W), lambda j: (0, j))
    return lane, col, wb, wc, vec


def _s5_scan_fwd(su_b, wb_re, wb_im, a_re, a_im, comm=()):
    T = su_b.shape[0]
    W = SCAN_W

    def body(su_ref, wbr_ref, wbi_ref, ar_ref, ai_ref, sr_ref, si_ref, bre, bim):
        su = su_ref[...]
        bre[pl.ds(8, T), :] = _nn(su, wbr_ref[...])
        bim[pl.ds(8, T), :] = _nn(su, wbi_ref[...])
        def store(t0, xr, xi):
            sr_ref[pl.ds(t0, SCAN_R), :] = xr.astype(BF16)
            si_ref[pl.ds(t0, SCAN_R), :] = xi.astype(BF16)

        _scan_rows(bre, bim, ar_ref[...], ai_ref[...], T, False,
                   lambda t0: (bre[pl.ds(t0 + 8, SCAN_R), :], bim[pl.ds(t0 + 8, SCAN_R), :]), store)

    lane, col, wb, wc, vec = _scan_specs(T)
    return _pcall(
        body, name="s5_scan_fwd", grid=(LANES // W,),
        in_specs=[col, wb, wb, vec, vec],
        out_specs=[lane, lane],
        out_shape=[jax.ShapeDtypeStruct((T, LANES), BF16)] * 2,
        scratch=[pltpu.VMEM((T + 16, W), F32)] * 2, comm=comm,
        operands=(su_b, wb_re, wb_im, a_re, a_im))


def _gelu(s):
    th = jnp.tanh(GELU_C * (s + 0.044715 * s * s * s))
    return 0.5 * s * (1.0 + th), th


def _mix_fwd_b(st_re, st_im, wc_re4, wc_im4, su, dvec, w_glu4, g_conv, g_ssm, y_conv, w_mo, x1, g, b, tm, comm=()):
    T = su.shape[0]

    def body(sr_ref, si_ref, wcr_ref, wci_ref, su_ref, d_ref, wg_ref, gc_ref, gs_ref, yc_ref, wmo_ref,
             x_ref, g_ref, b_ref, s_ref, sgb_ref, ga_ref, gb_ref, mb_ref, r_ref, xo_ref):
        srb = sr_ref[...]
        sib = si_ref[...]
        ys = [_nn(srb[:, 512 * J:512 * (J + 1)], wcr_ref[J]) + _nn(sib[:, 512 * J:512 * (J + 1)], wci_ref[J])
              for J in range(4)]
        s = jnp.concatenate(ys, axis=1) + d_ref[...] * su_ref[...]
        sg, _ = _gelu(s)
        sgb = sg.astype(BF16)
        ga = jnp.concatenate([_nn(sgb, wg_ref[0]), _nn(sgb, wg_ref[1])], axis=1)
        gb = jnp.concatenate([_nn(sgb, wg_ref[2]), _nn(sgb, wg_ref[3])], axis=1)
        merged = (_sig(gc_ref[...].astype(F32)) * yc_ref[...].astype(F32)
                  + _sig(gs_ref[...].astype(F32)) * (ga * _sig(gb)))
        mb = merged.astype(BF16)
        r = ALPHA * x_ref[...] + _nn(mb, wmo_ref[...])
        xhat, _ = _ln_stats(r)
        xo = xhat * g_ref[...] + b_ref[...]
        s_ref[...] = s
        sgb_ref[...] = sgb
        ga_ref[...] = ga.astype(BF16)
        gb_ref[...] = gb.astype(BF16)
        mb_ref[...] = mb
        r_ref[...] = r
        xo_ref[...] = xo

    def tok(n):
        return pl.BlockSpec((tm, n), lambda i: (i, 0))

    def full(shape):
        return pl.BlockSpec(shape, lambda i: (0,) * len(shape))

    return _pcall(
        body, name="mix_fwd_b", grid=(T // tm,),
        in_specs=[tok(LANES), tok(LANES), full((4, 512, 128)), full((4, 512, 128)), tok(SSM), full((1, SSM)),
                  full((4, SSM, 512)), tok(D), tok(D), tok(D), full((D, D)), tok(D), full((1, D)), full((1, D))],
        out_specs=[tok(SSM), tok(SSM), tok(D), tok(D), tok(D), tok(D), tok(D)],
        out_shape=[jax.ShapeDtypeStruct((T, SSM), F32), jax.ShapeDtypeStruct((T, SSM), BF16),
                   jax.ShapeDtypeStruct((T, D), BF16), jax.ShapeDtypeStruct((T, D), BF16),
                   jax.ShapeDtypeStruct((T, D), BF16), jax.ShapeDtypeStruct((T, D), F32),
                   jax.ShapeDtypeStruct((T, D), F32)],
        vmem_mb=56, comm=comm,
        operands=(st_re, st_im, wc_re4, wc_im4, su, dvec, w_glu4, g_conv, g_ssm, y_conv, w_mo, x1, g, b))


def _ple_loss(x3, x3b, p, w_pi4, w_pg, g, b, target, tm):
    T = x3.shape[0]
    PD = p.shape[1]

    def body(x_ref, xb_ref, p_ref, wpi_ref, wpg_ref, g_ref, b_ref, t_ref,
             loss_ref, dx_ref, pb_ref, dpw_ref, dgt_ref, dg_ref, db_ref):
        i = pl.program_id(0)
        pb = p_ref[...].astype(BF16)
        pw = jnp.concatenate([_nn(pb, wpi_ref[k]) for k in range(4)], axis=1)
        gt = _nn(xb_ref[...], wpg_ref[...])
        sg = _sig(gt)
        r = ALPHA * x_ref[...] + pw * sg
        gv = g_ref[...]
        xhat, rstd = _ln_stats(r)
        err = xhat * gv + b_ref[...] - t_ref[...]
        lpart = jnp.zeros((1, 128), F32) + 0.5 * jnp.sum(jnp.mean(err * err, axis=-1, keepdims=True))
        dy = err * (1.0 / D)
        dyg = dy * gv
        m1 = jnp.mean(dyg, axis=-1, keepdims=True)
        m2 = jnp.mean(dyg * xhat, axis=-1, keepdims=True)
        dr = rstd * (dyg - m1 - xhat * m2)
        pg, pbias = _rowsum(dy * xhat), _rowsum(dy)

        @pl.when(i == 0)
        def _():
            loss_ref[...] = lpart
            dg_ref[...] = pg
            db_ref[...] = pbias

        @pl.when(i > 0)
        def _():
            loss_ref[...] += lpart
            dg_ref[...] += pg
            db_ref[...] += pbias

        dgt = (dr * pw * sg * (1.0 - sg)).astype(BF16)
        pb_ref[...] = pb
        dpw_ref[...] = (dr * sg).astype(BF16)
        dgt_ref[...] = dgt
        dx_ref[...] = ALPHA * dr + _nt(dgt, wpg_ref[...])

    def tok(n):
        return pl.BlockSpec((tm, n), lambda i: (i, 0))

    def full(shape):
        return pl.BlockSpec(shape, lambda i: (0,) * len(shape))

    return pl.pallas_call(
        body, name="ple_loss", grid=(T // tm,),
        in_specs=[tok(D), tok(D), tok(PD), full((4, PD, 256)), full((D, D)), full((1, D)), full((1, D)), tok(D)],
        out_specs=[full((1, 128)), tok(D), tok(PD), tok(D), tok(D), full((1, D)), full((1, D))],
        out_shape=_hbm_out([jax.ShapeDtypeStruct((1, 128), F32), jax.ShapeDtypeStruct((T, D), F32),
                            jax.ShapeDtypeStruct((T, PD), BF16), jax.ShapeDtypeStruct((T, D), BF16),
                            jax.ShapeDtypeStruct((T, D), BF16), jax.ShapeDtypeStruct((1, D), F32),
                            jax.ShapeDtypeStruct((1, D), F32)]),
        compiler_params=_cp(48, 1),
    )(*_hbm(x3, x3b, p, w_pi4, w_pg, g, b, target))


def _mix_bwd_b(dy, r2, g, w_mo, g_conv, g_ssm, y_conv, ga, gb, s, su, dvec, w_glu4, wc_re4, wc_im4, tm, comm=()):
    T = dy.shape[0]

    def body(dy_ref, r_ref, g_ref, wmo_ref, gc_ref, gs_ref, yc_ref, ga_ref, gb_ref, s_ref, su_ref, d_ref,
             wg_ref, wcr_ref, wci_ref,
             dres_ref, dmix_ref, dgl_ref, dsb_ref, dud_ref, gsr_ref, gsi_ref, dyc_ref, dp_ref,
             dg_ref, db_ref, dd_ref):
        i = pl.program_id(0)
        dyv = dy_ref[...]
        dr, xhat = _ln_bwd(dyv, r_ref[...], g_ref[...])
        dmix = dr.astype(BF16)
        dmerged = _nt(dmix, wmo_ref[...])
        sc, ss, sgb = (_sig(gc_ref[...].astype(F32)), _sig(gs_ref[...].astype(F32)),
                       _sig(gb_ref[...].astype(F32)))
        gav = ga_ref[...].astype(F32)
        yssm = gav * sgb
        dgc = dmerged * yc_ref[...].astype(F32) * sc * (1.0 - sc)
        dgss = dmerged * yssm * ss * (1.0 - ss)
        dyssm = dmerged * ss
        dgl = jnp.concatenate([dyssm * sgb, dyssm * gav * sgb * (1.0 - sgb)], axis=1).astype(BF16)
        dsg = (_nt(dgl[:, 0:512], wg_ref[0]) + _nt(dgl[:, 512:1024], wg_ref[1])
               + _nt(dgl[:, 1024:1536], wg_ref[2]) + _nt(dgl[:, 1536:2048], wg_ref[3]))
        sv = s_ref[...]
        _, th = _gelu(sv)
        dgelu = 0.5 * (1.0 + th) + 0.5 * sv * (1.0 - th * th) * GELU_C * (1.0 + 3.0 * 0.044715 * sv * sv)
        ds = dsg * dgelu
        dsb = ds.astype(BF16)
        pg, pb, pd = _rowsum(dyv * xhat), _rowsum(dyv), _rowsum(ds * su_ref[...])

        @pl.when(i == 0)
        def _():
            dg_ref[...] = pg
            db_ref[...] = pb
            dd_ref[...] = pd

        @pl.when(i > 0)
        def _():
            dg_ref[...] += pg
            db_ref[...] += pb
            dd_ref[...] += pd

        dres_ref[...] = ALPHA * dr
        dmix_ref[...] = dmix
        dgl_ref[...] = dgl
        dsb_ref[...] = dsb
        dud_ref[...] = ds * d_ref[...]
        for J in range(4):
            gsr_ref[:, 512 * J:512 * (J + 1)] = _nt(dsb[:, 128 * J:128 * (J + 1)], wcr_ref[J]).astype(BF16)
            gsi_ref[:, 512 * J:512 * (J + 1)] = _nt(dsb[:, 128 * J:128 * (J + 1)], wci_ref[J]).astype(BF16)
        dyc_ref[...] = (dmerged * sc).astype(BF16)
        dp_ref[:, 0:D] = dgc.astype(BF16)
        dp_ref[:, D:2 * D] = dgss.astype(BF16)

    def tok(n):
        return pl.BlockSpec((tm, n), lambda i: (i, 0))

    def full(shape):
        return pl.BlockSpec(shape, lambda i: (0,) * len(shape))

    return _pcall(
        body, name="mix_bwd_b", grid=(T // tm,),
        in_specs=[tok(D), tok(D), full((1, D)), full((D, D)), tok(D), tok(D), tok(D), tok(D), tok(D),
                  tok(SSM), tok(SSM), full((1, SSM)), full((4, SSM, 512)), full((4, 512, 128)), full((4, 512, 128))],
        out_specs=[tok(D), tok(D), tok(2 * D), tok(SSM), tok(SSM), tok(LANES), tok(LANES), tok(D),
                   pl.BlockSpec((tm, 2 * D), lambda i: (i, 1)), full((1, D)), full((1, D)), full((1, SSM))],
        out_shape=[jax.ShapeDtypeStruct((T, D), F32), jax.ShapeDtypeStruct((T, D), BF16),
                   jax.ShapeDtypeStruct((T, 2 * D), BF16), jax.ShapeDtypeStruct((T, SSM), BF16),
                   jax.ShapeDtypeStruct((T, SSM), F32), jax.ShapeDtypeStruct((T, LANES), BF16),
                   jax.ShapeDtypeStruct((T, LANES), BF16), jax.ShapeDtypeStruct((T, D), BF16),
                   jax.ShapeDtypeStruct((T, 4 * D), BF16), jax.ShapeDtypeStruct((1, D), F32),
                   jax.ShapeDtypeStruct((1, D), F32), jax.ShapeDtypeStruct((1, SSM), F32)],
        vmem_mb=56, comm=comm,
        operands=(dy, r2, g, w_mo, g_conv, g_ssm, y_conv, ga, gb, s, su, dvec, w_glu4, wc_re4, wc_im4))


def _s5_scan_bwd(gs_re, gs_im, st_re, st_im, su_b, ds_b, wb_re, wb_im, a_re, a_im, comm=()):
    T = su_b.shape[0]
    W = SCAN_W
    R = SCAN_R

    def body(gr_ref, gi_ref, sr_ref, si_ref, su_ref, ds_ref, wbr_ref, wbi_ref, ar_ref, ai_ref,
             dsu_ref, dwbr_ref, dwbi_ref, dwcr_ref, dwci_ref, dar_ref, dai_ref, gre, gim):
        j = pl.program_id(0)
        zero = jnp.zeros((8, W), F32)
        for buf in (gre, gim):
            buf[pl.ds(T + 8, 8), :] = zero
        _scan_rows(gre, gim, ar_ref[...], ai_ref[...], T, True,
                   lambda t0: (gr_ref[pl.ds(t0, R), :].astype(F32), gi_ref[pl.ds(t0, R), :].astype(F32)))
        grb = gre[pl.ds(8, T), :].astype(BF16)
        gib = gim[pl.ds(8, T), :].astype(BF16)
        part = _nt(grb, wbr_ref[...]) + _nt(gib, wbi_ref[...])

        @pl.when(j % SCAN_PER == 0)
        def _():
            dsu_ref[...] = part

        @pl.when(j % SCAN_PER > 0)
        def _():
            dsu_ref[...] += part

        su = su_ref[...]
        dwbr_ref[...] = _tn(su, grb)
        dwbi_ref[...] = _tn(su, gib)
        dsv = ds_ref[...]
        dwcr_ref[...] = _tn(sr_ref[...], dsv)
        dwci_ref[...] = _tn(si_ref[...], dsv)
        dar = jnp.zeros((1, W), F32)
        dai = jnp.zeros((1, W), F32)
        for c in range(T // R):
            xr = sr_ref[pl.ds(c * R, R), :].astype(F32)
            xi = si_ref[pl.ds(c * R, R), :].astype(F32)
            g1r = gre[pl.ds(c * R + 9, R), :]
            g1i = gim[pl.ds(c * R + 9, R), :]
            dar = dar + _rowsum(g1r * xr + g1i * xi)
            dai = dai + _rowsum(g1i * xr - g1r * xi)
        dar_ref[...] = dar
        dai_ref[...] = dai

    lane, col, wb, wc, vec = _scan_specs(T)
    return _pcall(
        body, name="s5_scan_bwd", grid=(LANES // W,),
        in_specs=[lane, lane, lane, lane, col, col, wb, wb, vec, vec],
        out_specs=[col, wb, wb, wc, wc, vec, vec],
        out_shape=[jax.ShapeDtypeStruct((T, SSM), F32),
                   jax.ShapeDtypeStruct((LANES // W, 128, W), F32), jax.ShapeDtypeStruct((LANES // W, 128, W), F32),
                   jax.ShapeDtypeStruct((LANES // W, W, 128), F32), jax.ShapeDtypeStruct((LANES // W, W, 128), F32),
                   jax.ShapeDtypeStruct((1, LANES), F32), jax.ShapeDtypeStruct((1, LANES), F32)],
        scratch=[pltpu.VMEM((T + 16, W), F32)] * 2, vmem_mb=56, comm=comm,
        operands=(gs_re, gs_im, st_re, st_im, su_b, ds_b, wb_re, wb_im, a_re, a_im))


def _mix_bwd_a(dyc_b, w_co4, pc, z_b, conv_w, dsu_ssm, du_dir, dproj, dres, w_mix4, tm, comm=()):
    T = dres.shape[0]
    nt = T // tm

    def body(dyc_ref, wco_ref, pc_ref, halo_ref, z_ref, cw_ref, dsu_ref, dud_ref, dpin_ref, dres_ref, w_ref,
             dp_ref, dx_ref, dcw_ref, dcb_ref, dzbuf, qbuf):
        i = pl.program_id(0)
        ii = nt - 1 - i

        @pl.when(i == 0)
        def _():
            dzbuf[pl.ds(tm, 8), :] = jnp.zeros((8, CONV), F32)

        dyc = dyc_ref[...]
        dyin = (_nt(dyc[:, 0:256], wco_ref[0]) + _nt(dyc[:, 256:512], wco_ref[1])
                + _nt(dyc[:, 512:768], wco_ref[2]) + _nt(dyc[:, 768:1024], wco_ref[3]))
        cbv = pc_ref[:, 0:CONV].astype(F32)
        ccv = pc_ref[:, CONV:2 * CONV].astype(F32)
        chv = pc_ref[:, 2 * CONV:3 * CONV].astype(F32)
        dcbv = dyin * z_ref[...].astype(F32)
        dz = dyin * cbv
        dzbuf[pl.ds(0, tm), :] = dz
        cw = cw_ref[...]
        dq = cw[2:3] * dz + cw[1:2] * dzbuf[pl.ds(1, tm), :] + cw[0:1] * dzbuf[pl.ds(2, tm), :]
        dzbuf[pl.ds(tm, 8), :] = dz[0:8]
        q = ccv * chv
        hq = halo_ref[:, CONV:2 * CONV].astype(F32) * halo_ref[:, 2 * CONV:3 * CONV].astype(F32)
        qbuf[pl.ds(0, 8), :] = jnp.where(ii > 0, hq, jnp.zeros_like(hq))
        qbuf[pl.ds(8, tm), :] = q
        pw = jnp.concatenate([_rowsum(dz * qbuf[pl.ds(6, tm), :]), _rowsum(dz * qbuf[pl.ds(7, tm), :]),
                              _rowsum(dz * q), jnp.zeros((5, CONV), F32)], axis=0)
        pbias = _rowsum(dz)

        @pl.when(i == 0)
        def _():
            dcw_ref[...] = pw
            dcb_ref[...] = pbias

        @pl.when(i > 0)
        def _():
            dcw_ref[...] += pw
            dcb_ref[...] += pbias

        dp0 = jnp.concatenate([dcbv, dq * chv], axis=1).astype(BF16)
        dp1 = jnp.concatenate([dq * ccv, dsu_ref[...] + dud_ref[...]], axis=1).astype(BF16)
        dp_ref[:, 0:D] = dp0
        dp_ref[:, D:2 * D] = dp1
        dx_ref[...] = (dres_ref[...] + _nt(dp0, w_ref[0]) + _nt(dp1, w_ref[1])
                       + _nt(dpin_ref[:, 0:D], w_ref[2]) + _nt(dpin_ref[:, D:2 * D], w_ref[3]))

    def tok(n):
        return pl.BlockSpec((tm, n), lambda i: (nt - 1 - i, 0))

    def full(shape):
        return pl.BlockSpec(shape, lambda i: (0,) * len(shape))

    halo = pl.BlockSpec((8, 3 * CONV), lambda i: (jnp.maximum((nt - 1 - i) * (tm // 8) - 1, 0), 0))
    return _pcall(
        body, name="mix_bwd_a", grid=(nt,),
        in_specs=[tok(D), pl.BlockSpec((4, CONV, 256), lambda i: (0, 0, 0), pipeline_mode=pl.Buffered(1)),
                  tok(3 * CONV), halo, tok(CONV), full((3, CONV)),
                  tok(SSM), tok(SSM), pl.BlockSpec((tm, 2 * D), lambda i: (nt - 1 - i, 1)), tok(D),
                  pl.BlockSpec((4, D, D), lambda i: (0, 0, 0), pipeline_mode=pl.Buffered(1))],
        out_specs=[pl.BlockSpec((tm, 2 * D), lambda i: (nt - 1 - i, 0)), tok(D), full((8, CONV)), full((1, CONV))],
        out_shape=[jax.ShapeDtypeStruct((T, 4 * D), BF16), jax.ShapeDtypeStruct((T, D), F32),
                   jax.ShapeDtypeStruct((8, CONV), F32), jax.ShapeDtypeStruct((1, CONV), F32)],
        scratch=[pltpu.VMEM((tm + 8, CONV), F32), pltpu.VMEM((tm + 8, CONV), F32)],
        aliases={8: 0}, vmem_mb=56, comm=comm,
        operands=(dyc_b, w_co4, pc, pc, z_b, conv_w, dsu_ssm, du_dir, dproj, dres, w_mix4))


def _zoh(lam_re, lam_im, log_step, b_re, b_im):
    dt = jnp.exp(log_step)[:, None]
    mag = jnp.exp(lam_re * dt)
    abr, abi = mag * jnp.cos(lam_im * dt), mag * jnp.sin(lam_im * dt)
    nr, ni = abr - 1.0, abi
    den = lam_re * lam_re + lam_im * lam_im
    cr = (nr * lam_re + ni * lam_im) / den
    ci = (ni * lam_re - nr * lam_im) / den
    bbr = cr[..., None] * b_re - ci[..., None] * b_im
    bbi = cr[..., None] * b_im + ci[..., None] * b_re
    return abr, abi, bbr, bbi


_WB_MASK = (np.arange(8)[None, :, None]
            == SCAN_GR * np.arange(SCAN_PER)[:, None, None] + np.arange(SCAN_GR)[None, None, :]).astype(np.float32)
_EYE8 = np.eye(8, dtype=np.float32)


def _wb_blocks(bb):
    bt = bb.transpose(0, 2, 1).reshape(4, 1, 8, 16, 1, STATE)
    full = bt * _WB_MASK[None, :, :, None, :, None]
    return full.reshape(LANES // SCAN_W, 128, SCAN_W).astype(BF16)


def _wc_blocks(cc):
    ct = cc.transpose(0, 2, 1).reshape(4, 8, STATE, 1, 16)
    full = ct * _EYE8[None, :, None, :, None]
    return full.reshape(4, 512, 128).astype(BF16)


def _wb_diag(dwb):
    d6 = dwb.reshape(4, SCAN_PER, 8, 16, SCAN_GR, STATE) * _WB_MASK[None, :, :, None, :, None]
    return d6.sum(axis=(1, 4)).reshape(GROUPS, 16, STATE).transpose(0, 2, 1)


def _wc_diag(dwc):
    mask = _WB_MASK.transpose(0, 2, 1)
    d6 = dwc.reshape(4, SCAN_PER, SCAN_GR, STATE, 8, 16) * mask[None, :, :, None, :, None]
    return d6.sum(axis=4).reshape(GROUPS, STATE, 16).transpose(0, 2, 1)


def _where():
    x, y, c = lax.axis_index("x"), lax.axis_index("y"), lax.axis_index("c")
    return x, y, c, 2 * x + y


def _chip_dev(k, c):
    return (k // 2, k % 2, c)


def _slot_cast(meidx, w, dtype, name, token=()):
    R, C = w.shape
    tr = _row_tile(R)

    def body(m_ref, w_ref, *rest):
        rest[-1][...] = w_ref[...].astype(dtype)

    gs = pltpu.PrefetchScalarGridSpec(
        num_scalar_prefetch=1, grid=(R // tr,),
        in_specs=[pl.BlockSpec((tr, C), lambda i, m: (i, 0))] + [pl.BlockSpec((8, 128), lambda i, m: (0, 0))] * len(token),
        out_specs=pl.BlockSpec((None, tr, C), lambda i, m: (m[0], i, 0)))
    return pl.pallas_call(
        body, name=name, grid_spec=gs, out_shape=_hbm_out(jax.ShapeDtypeStruct((4, R, C), dtype)),
        compiler_params=_cp(32, 1),
    )(meidx, *_hbm(w), *token)


def _gather_ici_payload(bufs):
    def copies(ins, lnd, ss, rs):
        x, y, c, me = _where()
        cps = []
        for w, b in enumerate(bufs):
            h = b.shape[1] // 2
            mine = lnd[w].at[me, pl.ds(c * h, h)]
            for s in range(3):
                k = (me + 1 + s) % 4
                cps.append(pltpu.make_async_remote_copy(
                    src_ref=mine, dst_ref=mine, send_sem=ss.at[3 * w + s], recv_sem=rs.at[3 * w + s],
                    device_id=_chip_dev(k, c), device_id_type=MESH))
        return cps

    p = _sym_payload([], [jax.ShapeDtypeStruct(b.shape, b.dtype) for b in bufs], copies, 3 * len(bufs))
    p.lands = list(bufs)
    return p


def _gather_pass_payload(bufs):
    def copies(ins, outs, ss, rs):
        x, y, c, me = _where()
        cps = []
        for w, b in enumerate(bufs):
            h = b.shape[1] // 2
            for s in range(3):
                j = (me + 1 + s) % 4
                cps.append(pltpu.make_async_remote_copy(
                    src_ref=ins[w].at[j, pl.ds(c * h, h)], dst_ref=outs[w].at[j, pl.ds(c * h, h)],
                    send_sem=ss.at[3 * w + s], recv_sem=rs.at[3 * w + s], device_id=(x, y, 1 - c),
                    device_id_type=MESH))
        return cps

    p = _sym_payload(bufs, [jax.ShapeDtypeStruct(b.shape, b.dtype) for b in bufs], copies, 3 * len(bufs))
    p.aliases = {w: w for w in range(len(bufs))}
    return p


def _gather_payload(bufs):
    n = len(bufs)

    def half(ref, w, k, cc):
        h = bufs[w].shape[1] // 2
        return ref.at[k, pl.ds(cc * h, h)]

    def ici(ins, outs, sems, w, s):
        x, y, c, me = _where()
        k = (me + 1 + s) % 4
        return pltpu.make_async_remote_copy(
            src_ref=half(ins[w], w, me, c), dst_ref=half(outs[w], w, me, c), send_sem=sems[0].at[3 * w + s],
            recv_sem=sems[1].at[3 * w + s], device_id=_chip_dev(k, c), device_id_type=MESH)

    def landed(outs, sems, w, s):
        x, y, c, me = _where()
        j = (me + 3 - s) % 4
        return pltpu.make_async_remote_copy(
            src_ref=half(outs[w], w, j, c), dst_ref=half(outs[w], w, j, c), send_sem=sems[0].at[3 * w + s],
            recv_sem=sems[1].at[3 * w + s], device_id=(x, y, 1 - c), device_id_type=MESH)

    def passed(outs, sems, w, s, cc):
        x, y, c, me = _where()
        j = (me + 3 - s) % 4
        return pltpu.make_async_remote_copy(
            src_ref=half(outs[w], w, j, cc), dst_ref=half(outs[w], w, j, cc), send_sem=sems[2].at[3 * w + s],
            recv_sem=sems[3].at[3 * w + s], device_id=(x, y, 1 - c), device_id_type=MESH)

    pairs = [(w, s) for w in range(n) for s in range(3)]

    def start(ins, outs, sems):
        for w, s in pairs:
            ici(ins, outs, sems, w, s).start()

    def finish(ins, outs, sems):
        _, _, c, _ = _where()
        for w, s in pairs:
            landed(outs, sems, w, s).wait_recv()
            passed(outs, sems, w, s, c).start()
        for w, s in pairs:
            passed(outs, sems, w, s, 1 - c).wait_recv()
        for w, s in pairs:
            ici(ins, outs, sems, w, s).wait_send()
            passed(outs, sems, w, s, c).wait_send()

    return _Payload(bufs, [jax.ShapeDtypeStruct(b.shape, b.dtype) for b in bufs], {w: w for w in range(n)},
                    [pltpu.SemaphoreType.DMA((3 * n,))] * 4, start, finish)


def _sym_payload(operands, outs, copies, n_copies):
    def start(ins, outs_, sems):
        for cp in copies(ins, outs_, sems[0], sems[1]):
            cp.start()

    def finish(ins, outs_, sems):
        for cp in copies(ins, outs_, sems[0], sems[1]):
            cp.wait()

    p = _Payload(operands, outs, {}, [pltpu.SemaphoreType.DMA((n_copies,))] * 2, start, finish)
    p.copies, p.n_copies = copies, n_copies
    return p


def _swap_payload(g4s):
    def copies(ins, outs, ss, rs):
        x, y, c, me = _where()
        cps = []
        for w, g in enumerate(g4s):
            h = g.shape[1] // 2
            cps.append(pltpu.make_async_remote_copy(
                src_ref=ins[w].at[:, pl.ds((1 - c) * h, h)], dst_ref=outs[w], send_sem=ss.at[w],
                recv_sem=rs.at[w], device_id=(x, y, 1 - c), device_id_type=MESH))
        return cps

    outs = [jax.ShapeDtypeStruct((4, g.shape[1] // 2, g.shape[2]), g.dtype) for g in g4s]
    return _sym_payload(g4s, outs, copies, len(g4s))


def _exchange_payload(pbs):
    def copies(ins, outs, ss, rs):
        x, y, c, me = _where()
        cps = []
        for w in range(len(pbs)):
            for s in range(3):
                k = (me + 1 + s) % 4
                cps.append(pltpu.make_async_remote_copy(
                    src_ref=ins[w].at[k], dst_ref=outs[w].at[2 - s], send_sem=ss.at[3 * w + s],
                    recv_sem=rs.at[3 * w + s], device_id=_chip_dev(k, c), device_id_type=MESH))
        return cps

    outs = [jax.ShapeDtypeStruct((3,) + p.shape[1:], p.dtype) for p in pbs]
    return _sym_payload(pbs, outs, copies, 3 * len(pbs))


HBM_REF = pl.BlockSpec(memory_space=pltpu.HBM)
SEM_REF = pl.BlockSpec(memory_space=pltpu.SEMAPHORE)
DATAFLOW = pltpu.SideEffectType.DATAFLOW_SIDE_EFFECTING


class _SemList:
    def __init__(self, refs):
        self.refs = refs

    @property
    def at(self):
        return self.refs


def _split_start(p, name):
    n_in, n_out, nc = len(p.operands), len(p.outs), p.n_copies
    lands = getattr(p, "lands", None) or [lax.empty(s.shape, s.dtype) for s in p.outs]

    def body(*refs):
        ins, lnd = refs[:n_in], refs[n_in:n_in + n_out]
        sems = refs[n_in + n_out:n_in + n_out + 2 * nc]
        for cp in p.copies(ins, lnd, _SemList(sems[:nc]), _SemList(sems[nc:])):
            cp.start()
        refs[-1][...] = jnp.zeros((8, 128), F32)

    res = pl.pallas_call(
        body, name=name,
        in_specs=[HBM_REF] * (n_in + n_out),
        out_specs=[SEM_REF] * (2 * nc) + [HBM_REF] * (n_in + n_out) + [VMEM_FULL],
        out_shape=([pltpu.SemaphoreType.DMA(())] * (2 * nc) + _hbm_out(p.operands) + _hbm_out(lands)
                   + [jax.ShapeDtypeStruct((8, 128), F32)]),
        input_output_aliases={i: 2 * nc + i for i in range(n_in + n_out)},
        compiler_params=pltpu.CompilerParams(has_side_effects=DATAFLOW),
    )(*_hbm(*p.operands, *lands))
    k = 2 * nc
    return list(res[:k]), list(res[k:k + n_in]), list(res[k + n_in:k + n_in + n_out]), res[-1]


def _split_wait(p, handle, after, name):
    sems, srcs, lands, _ = handle
    n_in, n_out, nc = len(srcs), len(lands), p.n_copies

    def body(*refs):
        ins, lnd = refs[:n_in], refs[n_in:n_in + n_out]
        sm = refs[n_in + n_out:n_in + n_out + 2 * nc]
        for cp in p.copies(ins, lnd, _SemList(sm[:nc]), _SemList(sm[nc:])):
            cp.wait_send()
            cp.wait_recv()

    res = pl.pallas_call(
        body, name=name,
        in_specs=[HBM_REF] * (n_in + n_out) + [SEM_REF] * (2 * nc) + [ANY] * len(after),
        out_specs=[HBM_REF] * (n_in + n_out), out_shape=_hbm_out(srcs) + _hbm_out(lands),
        input_output_aliases={i: i for i in range(n_in + n_out)},
        compiler_params=pltpu.CompilerParams(has_side_effects=DATAFLOW),
    )(*srcs, *lands, *sems, *after)
    return list(res[:n_in]), list(res[n_in:])


def _join_payload(halves):
    def copies(ins, outs, ss, rs):
        x, y, c, me = _where()
        return [pltpu.make_async_remote_copy(
            src_ref=ins[w], dst_ref=outs[w], send_sem=ss.at[w], recv_sem=rs.at[w],
            device_id=(x, y, 1 - c), device_id_type=MESH) for w in range(len(halves))]

    outs = [jax.ShapeDtypeStruct(a.shape, a.dtype) for a in halves]
    return _sym_payload(halves, outs, copies, len(halves))


def _allgather_payload(v):
    def copies(ins, outs, ss, rs):
        x, y, c, me = _where()
        lin = 4 * x + 2 * y + c
        cps = []
        for o in range(1, 8):
            t = (lin + o) % 8
            cps.append(pltpu.make_async_remote_copy(
                src_ref=ins[0], dst_ref=outs[0].at[lin], send_sem=ss.at[o - 1], recv_sem=rs.at[o - 1],
                device_id=(t // 4, (t // 2) % 2, t % 2), device_id_type=MESH))
        return cps

    p = _sym_payload([v], [jax.ShapeDtypeStruct((8,) + v.shape, v.dtype)], copies, 7)
    x, y, c, _ = _where()
    p.lands = [lax.dynamic_update_slice(jnp.zeros((8,) + v.shape, v.dtype), v[None], (4 * x + 2 * y + c, 0, 0))]
    return p


def _sum8(buf, token):
    _, P, C = buf.shape

    def body(b_ref, t_ref, o_ref):
        acc = b_ref[0]
        for d in range(1, 8):
            acc = acc + b_ref[d]
        o_ref[...] = acc

    return pl.pallas_call(
        body, name="sum8", in_specs=[VMEM_FULL, VMEM_FULL], out_specs=VMEM_FULL,
        out_shape=jax.ShapeDtypeStruct((P, C), F32),
        compiler_params=pltpu.CompilerParams(vmem_limit_bytes=32 << 20),
    )(buf, token)


def _row_tile(h):
    for t in (256, 176, 128, 64, 32, 16, 8):
        if h % t == 0:
            return t
    raise ValueError(h)


def _pair_sum(cmidx, g4, got, name):
    _, R, C = g4.shape
    h = R // 2
    th = _row_tile(h)

    def body(cm_ref, a_ref, b_ref, o_ref, ob_ref):
        sm = a_ref[...] + b_ref[...]
        ob_ref[...] = sm.astype(BF16)

        @pl.when(pl.program_id(1) == cm_ref[1])
        def _():
            o_ref[...] = sm

    blk = pl.BlockSpec((None, th, C), lambda i, k, cm: (k, i, 0))
    gs = pltpu.PrefetchScalarGridSpec(
        num_scalar_prefetch=1, grid=(h // th, 4),
        in_specs=[pl.BlockSpec((None, None, th, C), lambda i, k, cm: (k, cm[0], i, 0)), blk],
        out_specs=[pl.BlockSpec((th, C), lambda i, k, cm: (i, 0)), blk])
    return pl.pallas_call(
        body, name=name, grid_spec=gs,
        out_shape=_hbm_out([jax.ShapeDtypeStruct((h, C), F32), jax.ShapeDtypeStruct((4, h, C), BF16)]),
        compiler_params=_cp(32, 2),
    )(cmidx, *_hbm(g4.reshape(4, 2, h, C), got))


def _chip_sum(own, got, name):
    h, C = own.shape
    th = _row_tile(h)

    def body(a_ref, b_ref, o_ref):
        o_ref[...] = ((a_ref[...] + b_ref[0].astype(F32)) + b_ref[1].astype(F32)) + b_ref[2].astype(F32)

    return pl.pallas_call(
        body, name=name, grid=(h // th,),
        in_specs=[pl.BlockSpec((th, C), lambda i: (i, 0)), pl.BlockSpec((3, th, C), lambda i: (0, i, 0))],
        out_specs=pl.BlockSpec((th, C), lambda i: (i, 0)),
        out_shape=_hbm_out(jax.ShapeDtypeStruct((h, C), F32)),
        compiler_params=_cp(32, 1),
    )(*_hbm(own, got))


def _adamw_math(w, g, m, v):
    m2 = B1 * m + (1.0 - B1) * g
    v2 = B2 * v + (1.0 - B2) * (g * g)
    m_hat = m2 / (1.0 - B1 ** STEP)
    v_hat = v2 / (1.0 - B2 ** STEP)
    delta = -LR * (m_hat / (jnp.sqrt(v_hat) + EPS) + WD * w)
    return delta, m2, v2


def _adamw_pair(cidx, w, mine, theirs, m, v, token, name):
    R, C = w.shape
    h = R // 2
    tr = _row_tile(h)
    nh = h // tr

    def body(c_ref, w_ref, a_ref, b_ref, m_ref, v_ref, t_ref, g_ref, d_ref, mo_ref, vo_ref):
        own = (pl.program_id(0) // nh) == c_ref[0]
        g = jnp.where(own, a_ref[...], b_ref[...])
        d, m2, v2 = _adamw_math(w_ref[...], g, m_ref[...], v_ref[...])
        g_ref[...] = g
        d_ref[...] = d
        mo_ref[...] = m2
        vo_ref[...] = v2

    blk = pl.BlockSpec((tr, C), lambda i, c: (i, 0))
    mine_blk = pl.BlockSpec((tr, C), lambda i, c: (jnp.clip(i - c[0] * nh, 0, nh - 1), 0))
    theirs_blk = pl.BlockSpec((tr, C), lambda i, c: (jnp.clip(i - (1 - c[0]) * nh, 0, nh - 1), 0))
    gs = pltpu.PrefetchScalarGridSpec(
        num_scalar_prefetch=1, grid=(R // tr,),
        in_specs=[blk, mine_blk, theirs_blk, blk, blk, pl.BlockSpec((8, 128), lambda i, c: (0, 0))],
        out_specs=[blk] * 4)
    return pl.pallas_call(
        body, name=name, grid_spec=gs, out_shape=_hbm_out([jax.ShapeDtypeStruct((R, C), F32)] * 4),
        compiler_params=_cp(32, 1),
    )(cidx, *_hbm(w, mine, theirs, m, v), token)


def _adamw(w, g, m, v, name):
    R, C = w.shape
    tr = _row_tile(R)

    def body(w_ref, g_ref, m_ref, v_ref, d_ref, mo_ref, vo_ref):
        d, m2, v2 = _adamw_math(w_ref[...], g_ref[...], m_ref[...], v_ref[...])
        d_ref[...] = d
        mo_ref[...] = m2
        vo_ref[...] = v2

    blk = pl.BlockSpec((tr, C), lambda i: (i, 0))
    return pl.pallas_call(
        body, name=name, grid=(R // tr,), in_specs=[blk] * 4, out_specs=[blk] * 3,
        out_shape=_hbm_out([jax.ShapeDtypeStruct((R, C), F32)] * 3),
        compiler_params=_cp(32, 1),
    )(*_hbm(w, g, m, v))


def _pack(arrs):
    flat = jnp.concatenate([a.reshape(-1).astype(F32) for a in arrs])
    rows = -(-flat.shape[0] // 1024)
    rows = -(-rows // 8) * 8
    return jnp.pad(flat, (0, rows * 1024 - flat.shape[0])).reshape(rows, 1024)


def _unpack(packed, shapes):
    flat = packed.reshape(-1)
    out, off = [], 0
    for s in shapes:
        n = math.prod(s)
        out.append(flat[off:off + n].reshape(s))
        off += n
    return out


BIG = ["ffn1_w_in", "ffn1_w_out", "mix_w_in", "conv_w_out", "ssm_w_glu", "mix_w_out",
       "ffn2_w_in", "ffn2_w_out", "ple_w_in", "ple_w_gate"]
SMALL = ["ln1_g", "ln1_b", "conv_w", "conv_b", "ssm_lam_re", "ssm_lam_im", "ssm_log_step", "ssm_b_re", "ssm_b_im",
         "ssm_c_re", "ssm_c_im", "ssm_d", "ln2_g", "ln2_b", "ln3_g", "ln3_b", "ln4_g", "ln4_b"]
WEIGHTS = ["ffn1_w_in", "ffn1_w_out", "ln1_g", "ln1_b", "mix_w_in", "conv_w", "conv_b", "conv_w_out",
           "ssm_lam_re", "ssm_lam_im", "ssm_log_step", "ssm_b_re", "ssm_b_im", "ssm_c_re", "ssm_c_im", "ssm_d",
           "ssm_w_glu", "mix_w_out", "ln2_g", "ln2_b", "ffn2_w_in", "ffn2_w_out", "ln3_g", "ln3_b",
           "ple_w_in", "ple_w_gate", "ln4_g", "ln4_b"]


def _s5_operands(sp):
    abr, abi, bbr, bbi = _zoh(sp["ssm_lam_re"], sp["ssm_lam_im"], sp["ssm_log_step"], sp["ssm_b_re"], sp["ssm_b_im"])
    return (_wb_blocks(bbr), _wb_blocks(bbi), _wc_blocks(sp["ssm_c_re"]), _wc_blocks(-sp["ssm_c_im"]),
            abr.reshape(1, LANES), abi.reshape(1, LANES), sp["ssm_d"].reshape(1, SSM))


def _local_step(x, p, target, sp, ops, sched):
    W = sched.W
    wb_re, wb_im, wc_re4, wc_im4, a_re, a_im, dvec = ops
    tm = TOKEN_TILE

    def run(fn, name, *args, **kw):
        outs, got = fn(*args, comm=sched.carry(name), **kw)
        sched.landed(name, got)
        sched.done[name] = outs[0]
        return outs

    def dw(name, wname, a, b, tk, tn, shape4, shard_cols=None, interleaved=False):
        out, got = _mm_tn(a, b, tk, tn, name, shard_cols=shard_cols, interleaved=interleaved,
                          comm=sched.carry(name))
        sched.landed(name, got)
        sched.done[name] = out
        sched.grad(wname, out.reshape(shape4))

    h1, r1, x1, x1b, xb = run(_ffn_fwd, "ffn1_fwd", x, W["ffn1_w_in"], W["ffn1_w_out"].reshape(2, FFH, D),
                              sp["ln1_g"], sp["ln1_b"], tm, "ffn1_fwd")
    conv_w = W["conv_w"][:, 0:3, :].transpose(1, 0, 2).reshape(3, CONV)
    pc, z_b, yin_b, su, su_b, g_conv, g_ssm, y_conv = run(
        _mix_fwd_a, "mix_fwd_a", x1b, W["mix_w_in"], conv_w, sp["conv_b"], W["conv_w_out"], tm)
    st_re, st_im = run(_s5_scan_fwd, "s5_scan_fwd", su_b, wb_re, wb_im, a_re, a_im)
    w_mo = W["mix_w_out"].reshape(D, D)
    s, sg_b, ga, gb, merged_b, r2, x2 = run(
        _mix_fwd_b, "mix_fwd_b", st_re, st_im, wc_re4, wc_im4, su, dvec, W["ssm_w_glu"], g_conv, g_ssm, y_conv,
        w_mo, x1, sp["ln2_g"], sp["ln2_b"], tm)
    w2o2 = W["ffn2_w_out"].reshape(2, FFH, D)
    h2, r3, x3, x3b, x2b = run(_ffn_fwd, "ffn2_fwd", x2, W["ffn2_w_in"], w2o2, sp["ln3_g"], sp["ln3_b"], tm,
                               "ffn2_fwd")
    loss_part, dx3, p_b, dpw_b, dgt_b, dg4, db4 = _ple_loss(
        x3, x3b, p, W["ple_w_in"], W["ple_w_gate"].reshape(D, D), sp["ln4_g"], sp["ln4_b"], target, tm)

    dw("dw_ple_gate", "ple_w_gate", x3b, dgt_b, 512, 1024, (4, 256, D))
    dw("dw_ple_in", "ple_w_in", p_b, dpw_b, 256, 256, (4, 256, 256), shard_cols=256)
    dx2, dh2, a2_b, df2_b, dg3, db3 = run(_ffn_bwd, "ffn2_bwd", dx3, r3, sp["ln3_g"], h2, W["ffn2_w_in"], w2o2,
                                          tm, "ffn2_bwd")
    dw("dw_ffn2_in", "ffn2_w_in", x2b, dh2, 512, FFH, (4, D, FFH), shard_cols=FFH, interleaved=True)
    dw("dw_ffn2_out", "ffn2_w_out", a2_b, df2_b, FFH, 1024, (4, FF // 4, D))
    (dres, dmix_b, dgl_b, ds_b, du_dir, gs_re, gs_im, dyc_b, dproj, dg2, db2, dd) = run(
        _mix_bwd_b, "mix_bwd_b", dx2, r2, sp["ln2_g"], w_mo, g_conv, g_ssm, y_conv, ga, gb, s, su, dvec,
        W["ssm_w_glu"], wc_re4, wc_im4, tm)
    dw("dw_mix_out", "mix_w_out", merged_b, dmix_b, 512, 1024, (4, 256, D))
    dw("dw_glu", "ssm_w_glu", sg_b, dgl_b, 512, 512, (4, SSM, 512), shard_cols=512)
    dsu_ssm, dwb_re, dwb_im, dwc_re, dwc_im, da_re, da_im = run(
        _s5_scan_bwd, "s5_scan_bwd", gs_re, gs_im, st_re, st_im, su_b, ds_b, wb_re, wb_im, a_re, a_im)
    dw("dw_conv_out", "conv_w_out", yin_b, dyc_b, 512, 256, (4, CONV, 256), shard_cols=256)
    dproj, dx1, dcw8, dcb = run(_mix_bwd_a, "mix_bwd_a", dyc_b, W["conv_w_out"], pc, z_b, conv_w, dsu_ssm,
                                du_dir, dproj, dres, W["mix_w_in"], 2 * tm)
    dw("dw_mix_in", "mix_w_in", x1b, dproj, 512, 1024, (4, D, D), shard_cols=1024)
    dx0, dh1, a1_b, df1_b, dg1, db1 = run(_ffn_bwd, "ffn1_bwd", dx1, r1, sp["ln1_g"], h1, W["ffn1_w_in"],
                                          W["ffn1_w_out"].reshape(2, FFH, D), tm, "ffn1_bwd")
    sched.small(dict(
        ln1_g=dg1, ln1_b=db1, ln2_g=dg2, ln2_b=db2, ln3_g=dg3, ln3_b=db3, ln4_g=dg4, ln4_b=db4,
        conv_w=dcw8[0:3], conv_b=dcb,
        a_re=da_re.reshape(GROUPS, STATE), a_im=da_im.reshape(GROUPS, STATE),
        bb_re=_wb_diag(dwb_re), bb_im=_wb_diag(dwb_im),
        ssm_c_re=_wc_diag(dwc_re), ssm_c_im=-_wc_diag(dwc_im), ssm_d=dd.reshape(GROUPS, 16),
        loss=loss_part[0:1, 0]))
    dw("dw_ffn1_in", "ffn1_w_in", xb, dh1, 512, FFH, (4, D, FFH), shard_cols=FFH, interleaved=True)
    dw("dw_ffn1_out", "ffn1_w_out", a1_b, df1_b, FFH, 1024, (4, FF // 4, D))
    return loss_part[0, 0], dx0


RAW_ORDER = ["ln1_g", "ln1_b", "ln2_g", "ln2_b", "ln3_g", "ln3_b", "ln4_g", "ln4_b", "conv_w", "conv_b",
             "a_re", "a_im", "bb_re", "bb_im", "ssm_c_re", "ssm_c_im", "ssm_d", "loss"]

GATHER_FIRST = ["ffn1_w_in", "ffn1_w_out"]
GATHER_AT = {"ffn1_fwd": ["mix_w_in", "conv_w_out", "conv_w"], "mix_fwd_a": ["ssm_w_glu", "mix_w_out"],
             "s5_scan_fwd": ["ffn2_w_in"], "mix_fwd_b": ["ffn2_w_out"], "ffn2_fwd": ["ple_w_in", "ple_w_gate"]}
REDUCE_GROUP = {"ffn2": ["ple_w_gate", "ple_w_in", "ffn2_w_in", "ffn2_w_out"],
                "mix": ["mix_w_out", "ssm_w_glu", "conv_w_out", "mix_w_in"], "ffn1": ["ffn1_w_in", "ffn1_w_out"]}
REDUCE_AT = {"mix_bwd_b": [("swap", "ffn2")], "mix_bwd_a": [("join", "ffn2")]}
BEGIN_AT = {"dw_mix_out": [("exchange", "ffn2")], "ffn1_bwd": [("swap", "mix")],
            "dw_ffn1_in": [("small", None), ("exchange", "mix")]}
BEHIND = {"dw_glu": [("exchange", "ffn2")], "s5_scan_bwd": [("exchange", "ffn2")]}
END_AT = {"mix_bwd_a": [("exchange", "ffn2", ["dw_mix_out", "dw_glu", "s5_scan_bwd"])],
          "dw_ffn1_in": [("swap", "mix", ["ffn1_bwd"])]}
LAST_GROUP = "ffn1"


class _Sched:
    def __init__(self, cmidx):
        self.bufs, self.cmidx = {}, cmidx
        self.W, self.G, self.raw, self.small_buf = {}, {}, None, None
        self.got1, self.p32, self.pbf, self.got2, self.half, self.theirs = {}, {}, {}, {}, {}, {}
        self._open, self._split, self.done = [], {}, {}

    def first_begin(self, bufs):
        self.bufs.update(bufs)
        p = _gather_ici_payload([bufs[n] for n in GATHER_FIRST])
        self._first = (p, _split_start(p, "gather_first_start"))
        return self._first[1][3]

    def first_end(self, bufs, after):
        self.bufs.update(bufs)
        p, handle = self._first
        _, landed = _split_wait(p, handle, after, "gather_first_wait")
        (outs,) = _comm_call("gather_first_pass", [_gather_pass_payload(landed)])
        self.W.update(zip(GATHER_FIRST, outs))

    def _payload(self, stage, key):
        if stage == "gather":
            return _gather_payload([self.bufs[n] for n in key])
        if stage == "small":
            return _allgather_payload(_pack([self.raw[k] for k in RAW_ORDER]))
        names = REDUCE_GROUP[key]
        if stage == "swap":
            return _swap_payload([self.G[n] for n in names])
        if stage == "exchange":
            for n in names:
                self.p32[n], self.pbf[n] = _pair_sum(self.cmidx, self.G[n], self.got1[n], "pair_sum_" + n)
            return _exchange_payload([self.pbf[n] for n in names])
        for n in names:
            self.half[n] = _chip_sum(self.p32[n], self.got2[n], "chip_sum_" + n)
        return _join_payload([self.half[n] for n in names])

    def _store(self, stages, got):
        for (stage, key), outs in zip(stages, got):
            if stage == "gather":
                self.W.update(zip(key, outs))
            elif stage == "small":
                self.small_buf = outs[0]
            else:
                {"swap": self.got1, "exchange": self.got2, "join": self.theirs}[stage].update(
                    zip(REDUCE_GROUP[key], outs))

    def _standalone(self, name, stages):
        self._store(stages, _comm_call(name, [self._payload(s, k) for s, k in stages]))

    def carry(self, name):
        for stage, key, behind in END_AT.get(name, []):
            self._end(stage, key, [self.done[b] for b in behind])
        tokens = [self._begin(stage, key) for stage, key in BEGIN_AT.get(name, [])]
        tokens += [self._split[sk][1][3] for sk in BEHIND.get(name, [])]
        self._open = [("gather", GATHER_AT[name])] if name in GATHER_AT else []
        self._open += REDUCE_AT.get(name, [])
        comm = [self._payload(s, k) for s, k in self._open]
        if tokens:
            comm.append(_Payload(tokens, [], {}, [], lambda *a: None, lambda *a: None))
        return tuple(comm)

    def landed(self, name, got):
        self._store(self._open, got)

    def grad(self, name, g4):
        self.G[name] = g4

    def small(self, raw):
        self.raw = raw

    def _begin(self, stage, key):
        p = self._payload(stage, key)
        self._split[stage, key] = (p, _split_start(p, "%s_%s_start" % (stage, key)))
        return self._split[stage, key][1][3]

    def _end(self, stage, key, after):
        p, handle = self._split.pop((stage, key))
        srcs, lands = _split_wait(p, handle, after, "%s_%s_wait" % (stage, key))
        if stage == "swap":
            self.G.update(zip(REDUCE_GROUP[key], srcs))
        self._store([(stage, key)], [lands])

    def tail_begin(self):
        return self._begin("swap", LAST_GROUP)

    def tail_mid(self, after):
        self._end("swap", LAST_GROUP, after)
        token = self._begin("exchange", LAST_GROUP)
        self._end("small", None, [token])
        self._end("exchange", "mix", [token])
        self._standalone("reduce_tail_join_mix", [("join", "mix")])
        return token

    def tail_end(self, after):
        self._end("exchange", LAST_GROUP, after)
        self._standalone("reduce_tail_join", [("join", LAST_GROUP)])


def _small_grads(raw_sum, sp):
    _, vjp = jax.vjp(_zoh, sp["ssm_lam_re"], sp["ssm_lam_im"], sp["ssm_log_step"], sp["ssm_b_re"], sp["ssm_b_im"])
    d_lre, d_lim, d_ls, d_bre, d_bim = vjp((raw_sum["a_re"], raw_sum["a_im"], raw_sum["bb_re"], raw_sum["bb_im"]))
    g = {k: raw_sum[k] for k in ("ln1_g", "ln1_b", "ln2_g", "ln2_b", "ln3_g", "ln3_b", "ln4_g", "ln4_b",
                                 "conv_w", "conv_b", "ssm_c_re", "ssm_c_im", "ssm_d")}
    g.update(ssm_lam_re=d_lre, ssm_lam_im=d_lim, ssm_log_step=d_ls, ssm_b_re=d_bre, ssm_b_im=d_bim)
    return g


def kernel(x, p, ffn1_w_in, ffn1_w_out, ln1_g, ln1_b, mix_w_in, conv_w, conv_b, conv_w_out, ssm_lam_re, ssm_lam_im, ssm_log_step, ssm_b_re, ssm_b_im, ssm_c_re, ssm_c_im, ssm_d, ssm_w_glu, mix_w_out, ln2_g, ln2_b, ffn2_w_in, ffn2_w_out, ln3_g, ln3_b, ple_w_in, ple_w_gate, ln4_g, ln4_b, loss_target, m_ffn1_w_in, m_ffn1_w_out, m_ln1_g, m_ln1_b, m_mix_w_in, m_conv_w, m_conv_b, m_conv_w_out, m_ssm_lam_re, m_ssm_lam_im, m_ssm_log_step, m_ssm_b_re, m_ssm_b_im, m_ssm_c_re, m_ssm_c_im, m_ssm_d, m_ssm_w_glu, m_mix_w_out, m_ln2_g, m_ln2_b, m_ffn2_w_in, m_ffn2_w_out, m_ln3_g, m_ln3_b, m_ple_w_in, m_ple_w_gate, m_ln4_g, m_ln4_b, v_ffn1_w_in, v_ffn1_w_out, v_ln1_g, v_ln1_b, v_mix_w_in, v_conv_w, v_conv_b, v_conv_w_out, v_ssm_lam_re, v_ssm_lam_im, v_ssm_log_step, v_ssm_b_re, v_ssm_b_im, v_ssm_c_re, v_ssm_c_im, v_ssm_d, v_ssm_w_glu, v_mix_w_out, v_ln2_g, v_ln2_b, v_ffn2_w_in, v_ffn2_w_out, v_ln3_g, v_ln3_b, v_ple_w_in, v_ple_w_gate, v_ln4_g, v_ln4_b):
    args = dict(locals())
    w = {n: args[n] for n in WEIGHTS}
    m = {n: args["m_" + n] for n in WEIGHTS}
    v = {n: args["v_" + n] for n in WEIGHTS}
    _, _, c, me = _where()
    cidx = jnp.stack([c, me]).astype(jnp.int32)
    meidx = jnp.reshape(me, (1,)).astype(jnp.int32)

    sched = _Sched(cidx)
    token = sched.first_begin({n: _slot_cast(meidx, w[n][0], BF16, "cast_" + n) for n in GATHER_FIRST})
    rest = {n: _slot_cast(meidx, w[n][0], BF16, "cast_" + n, (token,)) for n in BIG if n not in GATHER_FIRST}
    rest["conv_w"] = _slot_cast(meidx, jnp.pad(conv_w[0], ((0, 13), (0, 0))), F32, "cast_conv_w", (token,))
    sp = {n: (w[n] if w[n].ndim == 2 and n != "ssm_log_step" else w[n][0]) for n in SMALL if n != "conv_w"}
    ops = _s5_operands({**sp, "ssm_lam_re": sp["ssm_lam_re"] + token[0, 0]})
    sched.first_end(rest, list(rest.values()) + list(ops))
    loss_part, dx0 = _local_step(x[0], p[0, 0], loss_target[0], sp, ops, sched)
    out_g, out_d, out_m, out_v = {}, {}, {}, {}

    def big_adamw(names, token):
        for n in names:
            g, dl, mn, vn = _adamw_pair(cidx, w[n][0], sched.half[n], sched.theirs[n], m[n][0], v[n][0], token,
                                        "adamw_" + n)
            out_g[n], out_d[n], out_m[n], out_v[n] = g[None], dl[None], mn[None], vn[None]

    first = ["ple_w_gate", "ple_w_in", "ffn2_w_in"]
    big_adamw(first, sched.tail_begin())
    token = sched.tail_mid([out_v[n] for n in first])
    big_adamw(["ffn2_w_out"], token)

    raw_shapes = [sched.raw[k].shape for k in RAW_ORDER]
    raw_sum = dict(zip(RAW_ORDER, _unpack(_sum8(sched.small_buf, token), raw_shapes)))
    loss = raw_sum["loss"][0]
    sg = _small_grads(raw_sum, sp)
    sg["conv_w"] = lax.dynamic_slice_in_dim(sg["conv_w"], me * 128, 128, axis=1)
    small_shapes = [w[n].shape for n in SMALL]
    gp = _pack([sg[n] for n in SMALL])
    d_s, m_s, v_s = _adamw(_pack([w[n] for n in SMALL]), gp, _pack([m[n] for n in SMALL]),
                           _pack([v[n] for n in SMALL]), "adamw_small")

    for n, a, b_, c_, d_ in zip(SMALL, _unpack(gp, small_shapes), _unpack(d_s, small_shapes),
                                _unpack(m_s, small_shapes), _unpack(v_s, small_shapes)):
        out_g[n], out_d[n], out_m[n], out_v[n] = a, b_, c_, d_
    big_adamw(REDUCE_GROUP["mix"], token)
    sched.tail_end([d_s, out_v["ffn2_w_out"]] + [out_v[n] for n in REDUCE_GROUP["mix"]])
    big_adamw(REDUCE_GROUP[LAST_GROUP], token)

    return (loss, dx0[None], *[out_g[n] for n in WEIGHTS], *[out_d[n] for n in WEIGHTS],
            *[out_m[n] for n in WEIGHTS], *[out_v[n] for n in WEIGHTS])
```

```python
import functools
import math

import jax
import jax.numpy as jnp
import numpy as np
from jax import lax
from jax.experimental import pallas as pl
from jax.experimental.pallas import tpu as pltpu

F32, BF16 = jnp.float32, jnp.bfloat16
D = 1024
FF = 2816
FFH = FF // 2
CONV = 512
SSM = 512
GROUPS = 32
STATE = 64
LANES = GROUPS * STATE
SCAN_W = 128
SCAN_PER = 512 // SCAN_W
SCAN_GR = SCAN_W // STATE
SCAN_R = 256
TOKEN_TILE = 256
ALPHA = 2.0 ** 0.25
LN_EPS = 1e-5
GELU_C = math.sqrt(2.0 / math.pi)
B1, B2, LR, EPS, WD, STEP = 0.9, 0.999, 0.001, 1e-8, 0.01, 10
MESH = pl.DeviceIdType.MESH
ANY = pl.BlockSpec(memory_space=pl.ANY)
VMEM_FULL = pl.BlockSpec(memory_space=pltpu.VMEM)


def _cp(vmem_mb=48, n_axes=1):
    return pltpu.CompilerParams(vmem_limit_bytes=vmem_mb << 20,
                                dimension_semantics=("arbitrary",) * n_axes)


def _hbm(*arrs):
    return [pltpu.with_memory_space_constraint(a, pltpu.HBM) for a in arrs]


def _hbm_out(shapes):
    if isinstance(shapes, (list, tuple)):
        return [pltpu.HBM(s.shape, s.dtype) for s in shapes]
    return pltpu.HBM(shapes.shape, shapes.dtype)


def _nn(a, b):
    return jnp.dot(a, b, preferred_element_type=F32)


def _nt(a, b):
    return lax.dot_general(a, b, (((1,), (1,)), ((), ())), preferred_element_type=F32)


def _tn(a, b):
    return lax.dot_general(a, b, (((0,), (0,)), ((), ())), preferred_element_type=F32)


def _sig(v):
    return jax.nn.sigmoid(v)


def _ln_stats(r):
    mu = jnp.mean(r, axis=-1, keepdims=True)
    xc = r - mu
    var = jnp.mean(xc * xc, axis=-1, keepdims=True)
    rstd = lax.rsqrt(var + LN_EPS)
    return xc * rstd, rstd


def _ln_bwd(dy, r, g):
    xhat, rstd = _ln_stats(r)
    dyg = dy * g
    m1 = jnp.mean(dyg, axis=-1, keepdims=True)
    m2 = jnp.mean(dyg * xhat, axis=-1, keepdims=True)
    return rstd * (dyg - m1 - xhat * m2), xhat


def _rowsum(v):
    return jnp.sum(v, axis=0, keepdims=True)


class _Payload:
    def __init__(self, operands, outs, aliases, sems, start, finish):
        self.operands, self.outs, self.aliases, self.sems = list(operands), list(outs), dict(aliases), list(sems)
        self.start, self.finish = start, finish


def _split(flat, comm, attr):
    out, i = [], 0
    for p in comm:
        n = len(getattr(p, attr))
        out.append(list(flat[i:i + n]))
        i += n
    return out


def _run_comm(comm, which, cin, cout, csem):
    for p, a, b, s in zip(comm, _split(cin, comm, "operands"), _split(cout, comm, "outs"), _split(csem, comm, "sems")):
        getattr(p, which)(a, b, s)


def _pcall(body, *, name, grid, in_specs, out_specs, out_shape, operands, scratch=(), vmem_mb=48, aliases=None,
           comm=()):
    ni, no, ns = len(in_specs), len(out_specs), len(scratch)
    c_ops = [a for p in comm for a in p.operands]
    c_outs = [s for p in comm for s in p.outs]
    c_sems = [s for p in comm for s in p.sems]
    io = dict(aliases or {})
    off_i, off_o = ni, no
    for p in comm:
        for a, b in p.aliases.items():
            io[off_i + a] = off_o + b
        off_i += len(p.operands)
        off_o += len(p.outs)

    def wrapped(*refs):
        ins, cin = refs[:ni], refs[ni:ni + len(c_ops)]
        o0 = ni + len(c_ops)
        outs, cout = refs[o0:o0 + no], refs[o0 + no:o0 + no + len(c_outs)]
        s0 = o0 + no + len(c_outs)
        scr, csem = refs[s0:s0 + ns], refs[s0 + ns:]
        if comm:
            first = functools.reduce(jnp.logical_and, [pl.program_id(a) == 0 for a in range(len(grid))])
            pl.when(first)(lambda: _run_comm(comm, "start", cin, cout, csem))
        body(*ins, *outs, *scr)
        if comm:
            last = functools.reduce(jnp.logical_and, [pl.program_id(a) == grid[a] - 1 for a in range(len(grid))])
            pl.when(last)(lambda: _run_comm(comm, "finish", cin, cout, csem))

    res = pl.pallas_call(
        wrapped, name=name, grid=grid,
        in_specs=list(in_specs) + [ANY] * len(c_ops), out_specs=list(out_specs) + [ANY] * len(c_outs),
        out_shape=_hbm_out(list(out_shape) + c_outs), scratch_shapes=list(scratch) + c_sems,
        input_output_aliases=io,
        compiler_params=pltpu.CompilerParams(vmem_limit_bytes=vmem_mb << 20,
                                             dimension_semantics=("arbitrary",) * len(grid),
                                             has_side_effects=bool(c_sems)),
    )(*_hbm(*operands, *c_ops))
    return list(res[:no]), _split(res[no:], comm, "outs")


def _comm_call(name, comm):
    c_ops = [a for p in comm for a in p.operands]
    c_outs = [s for p in comm for s in p.outs]
    c_sems = [s for p in comm for s in p.sems]
    io, off_i, off_o = {}, 0, 0
    for p in comm:
        for a, b in p.aliases.items():
            io[off_i + a] = off_o + b
        off_i += len(p.operands)
        off_o += len(p.outs)

    def body(*refs):
        cin, cout = refs[:len(c_ops)], refs[len(c_ops):len(c_ops) + len(c_outs)]
        csem = refs[len(c_ops) + len(c_outs):]
        _run_comm(comm, "start", cin, cout, csem)
        _run_comm(comm, "finish", cin, cout, csem)

    res = pl.pallas_call(
        body, name=name, in_specs=[ANY] * len(c_ops), out_specs=[ANY] * len(c_outs), out_shape=_hbm_out(c_outs),
        scratch_shapes=c_sems, input_output_aliases=io,
        compiler_params=pltpu.CompilerParams(has_side_effects=True),
    )(*_hbm(*c_ops))
    return _split(res, comm, "outs")


def _ffn_fwd(x, w_in4, w_out2, g, b, tm, name, comm=()):
    T = x.shape[0]

    def body(x_ref, win_ref, wo_ref, g_ref, b_ref, h_ref, r_ref, xo_ref, xob_ref, xib_ref):
        xf = x_ref[...]
        xv = xf.astype(BF16)
        xib_ref[...] = xv
        acc = ALPHA * xf
        for k in range(2):
            gt = _nn(xv, win_ref[k])
            up = _nn(xv, win_ref[k + 2])
            a = (gt * _sig(gt) * up).astype(BF16)
            h_ref[:, 2 * k * FFH:(2 * k + 1) * FFH] = gt.astype(BF16)
            h_ref[:, (2 * k + 1) * FFH:(2 * k + 2) * FFH] = up.astype(BF16)
            acc = acc + 0.5 * _nn(a, wo_ref[k])
        xhat, _ = _ln_stats(acc)
        xo = xhat * g_ref[...] + b_ref[...]
        r_ref[...] = acc
        xo_ref[...] = xo
        xob_ref[...] = xo.astype(BF16)

    tok = pl.BlockSpec((tm, D), lambda i: (i, 0))
    vec = pl.BlockSpec((1, D), lambda i: (0, 0))
    return _pcall(
        body, name=name, grid=(T // tm,),
        in_specs=[tok,
                  pl.BlockSpec((4, D, FFH), lambda i: (0, 0, 0), pipeline_mode=pl.Buffered(1)),
                  pl.BlockSpec((2, FFH, D), lambda i: (0, 0, 0), pipeline_mode=pl.Buffered(1)),
                  vec, vec],
        out_specs=[pl.BlockSpec((tm, 2 * FF), lambda i: (i, 0)), tok, tok, tok, tok],
        out_shape=[jax.ShapeDtypeStruct((T, 2 * FF), BF16), jax.ShapeDtypeStruct((T, D), F32),
                   jax.ShapeDtypeStruct((T, D), F32), jax.ShapeDtypeStruct((T, D), BF16),
                   jax.ShapeDtypeStruct((T, D), BF16)],
        vmem_mb=58, comm=comm, operands=(x, w_in4, w_out2, g, b))


def _ffn_bwd(dy, r, g, h, w_in4, w_out2, tm, name, comm=()):
    T = dy.shape[0]

    def body(dy_ref, r_ref, g_ref, h_ref, win_ref, wo_ref, dx_ref, dh_ref, a_ref, df_ref, dg_ref, db_ref):
        i = pl.program_id(0)
        dyv = dy_ref[...]
        dr, xhat = _ln_bwd(dyv, r_ref[...], g_ref[...])
        dg_ref[...] = jnp.where(i == 0, 0.0, dg_ref[...]) + _rowsum(dyv * xhat)
        db_ref[...] = jnp.where(i == 0, 0.0, db_ref[...]) + _rowsum(dyv)
        dfb = (0.5 * dr).astype(BF16)
        df_ref[...] = dfb
        acc = ALPHA * dr
        for k in range(2):
            da = _nt(dfb, wo_ref[k])
            gt = h_ref[:, 2 * k * FFH:(2 * k + 1) * FFH].astype(F32)
            up = h_ref[:, (2 * k + 1) * FFH:(2 * k + 2) * FFH].astype(F32)
            sg = _sig(gt)
            silu = gt * sg
            dgate = (da * up * (sg * (1.0 + gt * (1.0 - sg)))).astype(BF16)
            dup = (da * silu).astype(BF16)
            a_ref[:, k * FFH:(k + 1) * FFH] = (silu * up).astype(BF16)
            dh_ref[:, 2 * k * FFH:(2 * k + 1) * FFH] = dgate
            dh_ref[:, (2 * k + 1) * FFH:(2 * k + 2) * FFH] = dup
            acc = acc + _nt(dgate, win_ref[k]) + _nt(dup, win_ref[k + 2])
        dx_ref[...] = acc

    tok = pl.BlockSpec((tm, D), lambda i: (i, 0))
    vec = pl.BlockSpec((1, D), lambda i: (0, 0))
    wide = pl.BlockSpec((tm, 2 * FF), lambda i: (i, 0))
    return _pcall(
        body, name=name, grid=(T // tm,),
        in_specs=[tok, tok, vec, wide,
                  pl.BlockSpec((4, D, FFH), lambda i: (0, 0, 0), pipeline_mode=pl.Buffered(1)),
                  pl.BlockSpec((2, FFH, D), lambda i: (0, 0, 0), pipeline_mode=pl.Buffered(1))],
        out_specs=[tok, wide, pl.BlockSpec((tm, FF), lambda i: (i, 0)), tok, vec, vec],
        out_shape=[jax.ShapeDtypeStruct((T, D), F32), jax.ShapeDtypeStruct((T, 2 * FF), BF16),
                   jax.ShapeDtypeStruct((T, FF), BF16), jax.ShapeDtypeStruct((T, D), BF16),
                   jax.ShapeDtypeStruct((1, D), F32), jax.ShapeDtypeStruct((1, D), F32)],
        vmem_mb=58, comm=comm, operands=(dy, r, g, h, w_in4, w_out2))


def _mm_tn(a, b, tk, tn, name, shard_cols=None, interleaved=False, comm=()):
    T, K = a.shape
    N = b.shape[1]

    def body(a_ref, b_ref, o_ref):
        o_ref[...] = _tn(a_ref[...], b_ref[...])

    if shard_cols is None:
        out_shape = jax.ShapeDtypeStruct((K, N), F32)
        out_spec = pl.BlockSpec((tk, tn), lambda ki, nj: (ki, nj))
    else:
        per = shard_cols // tn

        def shard(nj):
            blk = nj // per
            return (blk % 2) * 2 + blk // 2 if interleaved else blk

        out_shape = jax.ShapeDtypeStruct((N // shard_cols, K, shard_cols), F32)
        out_spec = pl.BlockSpec((None, tk, tn), lambda ki, nj: (shard(nj), ki, nj % per))
    (out,), got = _pcall(
        body, name=name, grid=(K // tk, N // tn),
        in_specs=[pl.BlockSpec((T, tk), lambda ki, nj: (0, ki)), pl.BlockSpec((T, tn), lambda ki, nj: (0, nj))],
        out_specs=[out_spec], out_shape=[out_shape], comm=comm, operands=(a, b))
    return out, got


def _mix_fwd_a(xb, w_mix4, conv_w, conv_b, w_co4, tm, comm=()):
    T = xb.shape[0]

    def body(xb_ref, w_ref, cw_ref, cb_ref, wco_ref,
             pc_ref, z_ref, yin_ref, su_ref, sub_ref, gc_ref, gs_ref, yc_ref, qbuf):
        @pl.when(pl.program_id(0) == 0)
        def _():
            qbuf[pl.ds(0, 8), :] = jnp.zeros((8, CONV), F32)

        xv = xb_ref[...]
        p0 = _nn(xv, w_ref[0])
        p1 = _nn(xv, w_ref[1])
        gc_ref[...] = _nn(xv, w_ref[2]).astype(BF16)
        gs_ref[...] = _nn(xv, w_ref[3]).astype(BF16)
        cbv, ccv = p0[:, :CONV], p0[:, CONV:]
        chv, suv = p1[:, :CONV], p1[:, CONV:]
        q = ccv * chv
        qbuf[pl.ds(8, tm), :] = q
        cw = cw_ref[...]
        z = (cw[2:3] * q + cw[1:2] * qbuf[pl.ds(7, tm), :] + cw[0:1] * qbuf[pl.ds(6, tm), :]
             + cb_ref[...])
        qbuf[pl.ds(0, 8), :] = q[tm - 8:tm]
        yin = (cbv * z).astype(BF16)
        pc_ref[:, 0:CONV] = cbv.astype(BF16)
        pc_ref[:, CONV:2 * CONV] = ccv.astype(BF16)
        pc_ref[:, 2 * CONV:3 * CONV] = chv.astype(BF16)
        z_ref[...] = z.astype(BF16)
        yin_ref[...] = yin
        su_ref[...] = suv
        sub_ref[...] = suv.astype(BF16)
        for k in range(4):
            yc_ref[:, 256 * k:256 * (k + 1)] = _nn(yin, wco_ref[k]).astype(BF16)

    def tok(n):
        return pl.BlockSpec((tm, n), lambda i: (i, 0))

    def full(shape):
        return pl.BlockSpec(shape, lambda i: (0,) * len(shape))

    return _pcall(
        body, name="mix_fwd_a", grid=(T // tm,),
        in_specs=[tok(D), full((4, D, D)), full((3, CONV)), full((1, CONV)), full((4, CONV, 256))],
        out_specs=[tok(3 * CONV), tok(CONV), tok(CONV), tok(SSM), tok(SSM), tok(D), tok(D), tok(D)],
        out_shape=[jax.ShapeDtypeStruct((T, 3 * CONV), BF16), jax.ShapeDtypeStruct((T, CONV), BF16),
                   jax.ShapeDtypeStruct((T, CONV), BF16), jax.ShapeDtypeStruct((T, SSM), F32),
                   jax.ShapeDtypeStruct((T, SSM), BF16), jax.ShapeDtypeStruct((T, D), BF16),
                   jax.ShapeDtypeStruct((T, D), BF16), jax.ShapeDtypeStruct((T, D), BF16)],
        scratch=[pltpu.VMEM((tm + 8, CONV), F32)], vmem_mb=56, comm=comm,
        operands=(xb, w_mix4, conv_w, conv_b, w_co4))


def _scan_rows(bre, bim, ar, ai, T, rev, load, store=None):
    R, W, G = SCAN_R, bre.shape[1], T // 8
    if rev:
        ai = -ai

    def cmul(pr, pi, xr, xi):
        return pr * xr - pi * xi, pr * xi + pi * xr

    pw = [(ar, ai)]
    for _ in range(7):
        pw.append(cmul(ar, ai, *pw[-1]))

    def shifted(v, d, axis, n, idx):
        if rev:
            return jnp.where(idx < n - d, pltpu.roll(v, n - d, axis), 0.0)
        return jnp.where(idx >= d, pltpu.roll(v, d, axis), 0.0)

    sub8 = lax.broadcasted_iota(jnp.int32, (8, W), 0)
    inside = {d: (sub8 < 8 - d) if rev else (sub8 >= d) for d in (1, 2, 4)}
    pm = {d: (jnp.where(inside[d], pw[d - 1][0], 0.0)[None], jnp.where(inside[d], pw[d - 1][1], 0.0)[None])
          for d in (1, 2, 4)}

    def step(i, _):
        t0 = pl.multiple_of(i * R, R)
        vr, vi = load(t0)
        vr, vi = vr.reshape(R // 8, 8, W), vi.reshape(R // 8, 8, W)
        for d in (1, 2, 4):
            sh = (8 - d) if rev else d
            dr, di = cmul(pm[d][0], pm[d][1], pltpu.roll(vr, sh, 1), pltpu.roll(vi, sh, 1))
            vr, vi = vr + dr, vi + di
        bre[pl.ds(t0 + 8, R), :] = vr.reshape(R, W)
        bim[pl.ds(t0 + 8, R), :] = vi.reshape(R, W)
        return 0

    lax.fori_loop(0, T // R, step, 0)

    edge = 0 if rev else 7
    cr = bre[pl.ds(8 + edge, G, stride=8), :]
    ci = bim[pl.ds(8 + edge, G, stride=8), :]
    row = lax.broadcasted_iota(jnp.int32, (G, W), 0)
    qr, qi = pw[7]
    d = 1
    while d < G:
        dr, di = cmul(qr, qi, shifted(cr, d, 0, G, row), shifted(ci, d, 0, G, row))
        cr, ci = cr + dr, ci + di
        qr, qi = qr * qr - qi * qi, 2.0 * qr * qi
        d *= 2

    order = [7 - r for r in range(8)] if rev else list(range(8))
    if store is None:
        nr, ni = shifted(cr, 1, 0, G, row), shifted(ci, 1, 0, G, row)
        for r in range(8):
            dr, di = cmul(*pw[order[r]], nr, ni)
            bre[pl.ds(8 + r, G, stride=8), :] = bre[pl.ds(8 + r, G, stride=8), :] + dr
            bim[pl.ds(8 + r, G, stride=8), :] = bim[pl.ds(8 + r, G, stride=8), :] + di
        return
    p8r = jnp.concatenate([pw[k][0] for k in order], axis=0)[None]
    p8i = jnp.concatenate([pw[k][1] for k in order], axis=0)[None]

    def back(nre, nim):
        nre[...] = shifted(cr, 1, 0, G, row)
        nim[...] = shifted(ci, 1, 0, G, row)

        def step3(i, _):
            t0 = pl.multiple_of(i * R, R)
            g0 = pl.multiple_of(i * (R // 8), R // 8)
            br = jnp.broadcast_to(nre[pl.ds(g0, R // 8), :][:, None, :], (R // 8, 8, W))
            bi = jnp.broadcast_to(nim[pl.ds(g0, R // 8), :][:, None, :], (R // 8, 8, W))
            dr, di = cmul(p8r, p8i, br, bi)
            xr = bre[pl.ds(t0 + 8, R), :] + dr.reshape(R, W)
            xi = bim[pl.ds(t0 + 8, R), :] + di.reshape(R, W)
            store(t0, xr, xi)
            return 0

        lax.fori_loop(0, T // R, step3, 0)

    pl.run_scoped(back, pltpu.VMEM((G, W), F32), pltpu.VMEM((G, W), F32))


def _scan_specs(T):
    W = SCAN_W
    lane = pl.BlockSpec((T, W), lambda j: (0, j))
    col = pl.BlockSpec((T, 128), lambda j: (0, j // SCAN_PER))
    wb = pl.BlockSpec((None, 128, W), lambda j: (j, 0, 0))
    wc = pl.BlockSpec((None, W, 128), lambda j: (j, 0, 0))
    vec = pl.BlockSpec((1, W), lambda j: (0, j))
    return lane, col, wb, wc, vec


def _s5_scan_fwd(su_b, wb_re, wb_im, a_re, a_im, comm=()):
    T = su_b.shape[0]
    W = SCAN_W

    def body(su_ref, wbr_ref, wbi_ref, ar_ref, ai_ref, sr_ref, si_ref, bre, bim):
        su = su_ref[...]
        bre[pl.ds(8, T), :] = _nn(su, wbr_ref[...])
        bim[pl.ds(8, T), :] = _nn(su, wbi_ref[...])
        def store(t0, xr, xi):
            sr_ref[pl.ds(t0, SCAN_R), :] = xr.astype(BF16)
            si_ref[pl.ds(t0, SCAN_R), :] = xi.astype(BF16)

        _scan_rows(bre, bim, ar_ref[...], ai_ref[...], T, False,
                   lambda t0: (bre[pl.ds(t0 + 8, SCAN_R), :], bim[pl.ds(t0 + 8, SCAN_R), :]), store)

    lane, col, wb, wc, vec = _scan_specs(T)
    return _pcall(
        body, name="s5_scan_fwd", grid=(LANES // W,),
        in_specs=[col, wb, wb, vec, vec],
        out_specs=[lane, lane],
        out_shape=[jax.ShapeDtypeStruct((T, LANES), BF16)] * 2,
        scratch=[pltpu.VMEM((T + 16, W), F32)] * 2, comm=comm,
        operands=(su_b, wb_re, wb_im, a_re, a_im))


def _gelu(s):
    th = jnp.tanh(GELU_C * (s + 0.044715 * s * s * s))
    return 0.5 * s * (1.0 + th), th


def _mix_fwd_b(st_re, st_im, wc_re4, wc_im4, su, dvec, w_glu4, g_conv, g_ssm, y_conv, w_mo, x1, g, b, tm, comm=()):
    T = su.shape[0]

    def body(sr_ref, si_ref, wcr_ref, wci_ref, su_ref, d_ref, wg_ref, gc_ref, gs_ref, yc_ref, wmo_ref,
             x_ref, g_ref, b_ref, s_ref, sgb_ref, ga_ref, gb_ref, mb_ref, r_ref, xo_ref):
        srb = sr_ref[...]
        sib = si_ref[...]
        ys = [_nn(srb[:, 512 * J:512 * (J + 1)], wcr_ref[J]) + _nn(sib[:, 512 * J:512 * (J + 1)], wci_ref[J])
              for J in range(4)]
        s = jnp.concatenate(ys, axis=1) + d_ref[...] * su_ref[...]
        sg, _ = _gelu(s)
        sgb = sg.astype(BF16)
        ga = jnp.concatenate([_nn(sgb, wg_ref[0]), _nn(sgb, wg_ref[1])], axis=1)
        gb = jnp.concatenate([_nn(sgb, wg_ref[2]), _nn(sgb, wg_ref[3])], axis=1)
        merged = (_sig(gc_ref[...].astype(F32)) * yc_ref[...].astype(F32)
                  + _sig(gs_ref[...].astype(F32)) * (ga * _sig(gb)))
        mb = merged.astype(BF16)
        r = ALPHA * x_ref[...] + _nn(mb, wmo_ref[...])
        xhat, _ = _ln_stats(r)
        xo = xhat * g_ref[...] + b_ref[...]
        s_ref[...] = s
        sgb_ref[...] = sgb
        ga_ref[...] = ga.astype(BF16)
        gb_ref[...] = gb.astype(BF16)
        mb_ref[...] = mb
        r_ref[...] = r
        xo_ref[...] = xo

    def tok(n):
        return pl.BlockSpec((tm, n), lambda i: (i, 0))

    def full(shape):
        return pl.BlockSpec(shape, lambda i: (0,) * len(shape))

    return _pcall(
        body, name="mix_fwd_b", grid=(T // tm,),
        in_specs=[tok(LANES), tok(LANES), full((4, 512, 128)), full((4, 512, 128)), tok(SSM), full((1, SSM)),
                  full((4, SSM, 512)), tok(D), tok(D), tok(D), full((D, D)), tok(D), full((1, D)), full((1, D))],
        out_specs=[tok(SSM), tok(SSM), tok(D), tok(D), tok(D), tok(D), tok(D)],
        out_shape=[jax.ShapeDtypeStruct((T, SSM), F32), jax.ShapeDtypeStruct((T, SSM), BF16),
                   jax.ShapeDtypeStruct((T, D), BF16), jax.ShapeDtypeStruct((T, D), BF16),
                   jax.ShapeDtypeStruct((T, D), BF16), jax.ShapeDtypeStruct((T, D), F32),
                   jax.ShapeDtypeStruct((T, D), F32)],
        vmem_mb=56, comm=comm,
        operands=(st_re, st_im, wc_re4, wc_im4, su, dvec, w_glu4, g_conv, g_ssm, y_conv, w_mo, x1, g, b))


def _ple_loss(x3, x3b, p, w_pi4, w_pg, g, b, target, tm):
    T = x3.shape[0]
    PD = p.shape[1]

    def body(x_ref, xb_ref, p_ref, wpi_ref, wpg_ref, g_ref, b_ref, t_ref,
             loss_ref, dx_ref, pb_ref, dpw_ref, dgt_ref, dg_ref, db_ref):
        i = pl.program_id(0)
        pb = p_ref[...].astype(BF16)
        pw = jnp.concatenate([_nn(pb, wpi_ref[k]) for k in range(4)], axis=1)
        gt = _nn(xb_ref[...], wpg_ref[...])
        sg = _sig(gt)
        r = ALPHA * x_ref[...] + pw * sg
        gv = g_ref[...]
        xhat, rstd = _ln_stats(r)
        err = xhat * gv + b_ref[...] - t_ref[...]
        lpart = jnp.zeros((1, 128), F32) + 0.5 * jnp.sum(jnp.mean(err * err, axis=-1, keepdims=True))
        dy = err * (1.0 / D)
        dyg = dy * gv
        m1 = jnp.mean(dyg, axis=-1, keepdims=True)
        m2 = jnp.mean(dyg * xhat, axis=-1, keepdims=True)
        dr = rstd * (dyg - m1 - xhat * m2)
        pg, pbias = _rowsum(dy * xhat), _rowsum(dy)

        @pl.when(i == 0)
        def _():
            loss_ref[...] = lpart
            dg_ref[...] = pg
            db_ref[...] = pbias

        @pl.when(i > 0)
        def _():
            loss_ref[...] += lpart
            dg_ref[...] += pg
            db_ref[...] += pbias

        dgt = (dr * pw * sg * (1.0 - sg)).astype(BF16)
        pb_ref[...] = pb
        dpw_ref[...] = (dr * sg).astype(BF16)
        dgt_ref[...] = dgt
        dx_ref[...] = ALPHA * dr + _nt(dgt, wpg_ref[...])

    def tok(n):
        return pl.BlockSpec((tm, n), lambda i: (i, 0))

    def full(shape):
        return pl.BlockSpec(shape, lambda i: (0,) * len(shape))

    return pl.pallas_call(
        body, name="ple_loss", grid=(T // tm,),
        in_specs=[tok(D), tok(D), tok(PD),
                  pl.BlockSpec((4, PD, 256), lambda i: (0, 0, 0), pipeline_mode=pl.Buffered(1)),
                  pl.BlockSpec((D, D), lambda i: (0, 0), pipeline_mode=pl.Buffered(1)),
                  full((1, D)), full((1, D)), tok(D)],
        out_specs=[full((1, 128)), tok(D), tok(PD), tok(D), tok(D), full((1, D)), full((1, D))],
        out_shape=_hbm_out([jax.ShapeDtypeStruct((1, 128), F32), jax.ShapeDtypeStruct((T, D), F32),
                            jax.ShapeDtypeStruct((T, PD), BF16), jax.ShapeDtypeStruct((T, D), BF16),
                            jax.ShapeDtypeStruct((T, D), BF16), jax.ShapeDtypeStruct((1, D), F32),
                            jax.ShapeDtypeStruct((1, D), F32)]),
        compiler_params=_cp(48, 1),
    )(*_hbm(x3, x3b, p, w_pi4, w_pg, g, b, target))


def _mix_bwd_b(dy, r2, g, w_mo, g_conv, g_ssm, y_conv, ga, gb, s, su, dvec, w_glu4, wc_re4, wc_im4, tm, comm=()):
    T = dy.shape[0]

    def body(dy_ref, r_ref, g_ref, wmo_ref, gc_ref, gs_ref, yc_ref, ga_ref, gb_ref, s_ref, su_ref, d_ref,
             wg_ref, wcr_ref, wci_ref,
             dres_ref, dmix_ref, dgl_ref, dsb_ref, dud_ref, gsr_ref, gsi_ref, dyc_ref, dp_ref,
             dg_ref, db_ref, dd_ref):
        i = pl.program_id(0)
        dyv = dy_ref[...]
        dr, xhat = _ln_bwd(dyv, r_ref[...], g_ref[...])
        dmix = dr.astype(BF16)
        dmerged = _nt(dmix, wmo_ref[...])
        sc, ss, sgb = (_sig(gc_ref[...].astype(F32)), _sig(gs_ref[...].astype(F32)),
                       _sig(gb_ref[...].astype(F32)))
        gav = ga_ref[...].astype(F32)
        yssm = gav * sgb
        dgc = dmerged * yc_ref[...].astype(F32) * sc * (1.0 - sc)
        dgss = dmerged * yssm * ss * (1.0 - ss)
        dyssm = dmerged * ss
        dgl = jnp.concatenate([dyssm * sgb, dyssm * gav * sgb * (1.0 - sgb)], axis=1).astype(BF16)
        dsg = (_nt(dgl[:, 0:512], wg_ref[0]) + _nt(dgl[:, 512:1024], wg_ref[1])
               + _nt(dgl[:, 1024:1536], wg_ref[2]) + _nt(dgl[:, 1536:2048], wg_ref[3]))
        sv = s_ref[...]
        _, th = _gelu(sv)
        dgelu = 0.5 * (1.0 + th) + 0.5 * sv * (1.0 - th * th) * GELU_C * (1.0 + 3.0 * 0.044715 * sv * sv)
        ds = dsg * dgelu
        dsb = ds.astype(BF16)
        pg, pb, pd = _rowsum(dyv * xhat), _rowsum(dyv), _rowsum(ds * su_ref[...])

        @pl.when(i == 0)
        def _():
            dg_ref[...] = pg
            db_ref[...] = pb
            dd_ref[...] = pd

        @pl.when(i > 0)
        def _():
            dg_ref[...] += pg
            db_ref[...] += pb
            dd_ref[...] += pd

        dres_ref[...] = ALPHA * dr
        dmix_ref[...] = dmix
        dgl_ref[...] = dgl
        dsb_ref[...] = dsb
        dud_ref[...] = ds * d_ref[...]
        for J in range(4):
            gsr_ref[:, 512 * J:512 * (J + 1)] = _nt(dsb[:, 128 * J:128 * (J + 1)], wcr_ref[J]).astype(BF16)
            gsi_ref[:, 512 * J:512 * (J + 1)] = _nt(dsb[:, 128 * J:128 * (J + 1)], wci_ref[J]).astype(BF16)
        dyc_ref[...] = (dmerged * sc).astype(BF16)
        dp_ref[:, 0:D] = dgc.astype(BF16)
        dp_ref[:, D:2 * D] = dgss.astype(BF16)

    def tok(n):
        return pl.BlockSpec((tm, n), lambda i: (i, 0))

    def full(shape):
        return pl.BlockSpec(shape, lambda i: (0,) * len(shape))

    return _pcall(
        body, name="mix_bwd_b", grid=(T // tm,),
        in_specs=[tok(D), tok(D), full((1, D)), full((D, D)), tok(D), tok(D), tok(D), tok(D), tok(D),
                  tok(SSM), tok(SSM), full((1, SSM)), full((4, SSM, 512)), full((4, 512, 128)), full((4, 512, 128))],
        out_specs=[tok(D), tok(D), tok(2 * D), tok(SSM), tok(SSM), tok(LANES), tok(LANES), tok(D),
                   pl.BlockSpec((tm, 2 * D), lambda i: (i, 1)), full((1, D)), full((1, D)), full((1, SSM))],
        out_shape=[jax.ShapeDtypeStruct((T, D), F32), jax.ShapeDtypeStruct((T, D), BF16),
                   jax.ShapeDtypeStruct((T, 2 * D), BF16), jax.ShapeDtypeStruct((T, SSM), BF16),
                   jax.ShapeDtypeStruct((T, SSM), F32), jax.ShapeDtypeStruct((T, LANES), BF16),
                   jax.ShapeDtypeStruct((T, LANES), BF16), jax.ShapeDtypeStruct((T, D), BF16),
                   jax.ShapeDtypeStruct((T, 4 * D), BF16), jax.ShapeDtypeStruct((1, D), F32),
                   jax.ShapeDtypeStruct((1, D), F32), jax.ShapeDtypeStruct((1, SSM), F32)],
        vmem_mb=56, comm=comm,
        operands=(dy, r2, g, w_mo, g_conv, g_ssm, y_conv, ga, gb, s, su, dvec, w_glu4, wc_re4, wc_im4))


def _s5_scan_bwd(gs_re, gs_im, st_re, st_im, su_b, ds_b, wb_re, wb_im, a_re, a_im, comm=()):
    T = su_b.shape[0]
    W = SCAN_W
    R = SCAN_R

    def body(gr_ref, gi_ref, sr_ref, si_ref, su_ref, ds_ref, wbr_ref, wbi_ref, ar_ref, ai_ref,
             dsu_ref, dwbr_ref, dwbi_ref, dwcr_ref, dwci_ref, dar_ref, dai_ref, gre, gim):
        j = pl.program_id(0)
        zero = jnp.zeros((8, W), F32)
        for buf in (gre, gim):
            buf[pl.ds(T + 8, 8), :] = zero
        _scan_rows(gre, gim, ar_ref[...], ai_ref[...], T, True,
                   lambda t0: (gr_ref[pl.ds(t0, R), :].astype(F32), gi_ref[pl.ds(t0, R), :].astype(F32)))
        grb = gre[pl.ds(8, T), :].astype(BF16)
        gib = gim[pl.ds(8, T), :].astype(BF16)
        part = _nt(grb, wbr_ref[...]) + _nt(gib, wbi_ref[...])

        @pl.when(j % SCAN_PER == 0)
        def _():
            dsu_ref[...] = part

        @pl.when(j % SCAN_PER > 0)
        def _():
            dsu_ref[...] += part

        su = su_ref[...]
        dwbr_ref[...] = _tn(su, grb)
        dwbi_ref[...] = _tn(su, gib)
        dsv = ds_ref[...]
        dwcr_ref[...] = _tn(sr_ref[...], dsv)
        dwci_ref[...] = _tn(si_ref[...], dsv)
        dar = jnp.zeros((1, W), F32)
        dai = jnp.zeros((1, W), F32)
        for c in range(T // R):
            xr = sr_ref[pl.ds(c * R, R), :].astype(F32)
            xi = si_ref[pl.ds(c * R, R), :].astype(F32)
            g1r = gre[pl.ds(c * R + 9, R), :]
            g1i = gim[pl.ds(c * R + 9, R), :]
            dar = dar + _rowsum(g1r * xr + g1i * xi)
            dai = dai + _rowsum(g1i * xr - g1r * xi)
        dar_ref[...] = dar
        dai_ref[...] = dai

    lane, col, wb, wc, vec = _scan_specs(T)
    return _pcall(
        body, name="s5_scan_bwd", grid=(LANES // W,),
        in_specs=[lane, lane, lane, lane, col, col, wb, wb, vec, vec],
        out_specs=[col, wb, wb, wc, wc, vec, vec],
        out_shape=[jax.ShapeDtypeStruct((T, SSM), F32),
                   jax.ShapeDtypeStruct((LANES // W, 128, W), F32), jax.ShapeDtypeStruct((LANES // W, 128, W), F32),
                   jax.ShapeDtypeStruct((LANES // W, W, 128), F32), jax.ShapeDtypeStruct((LANES // W, W, 128), F32),
                   jax.ShapeDtypeStruct((1, LANES), F32), jax.ShapeDtypeStruct((1, LANES), F32)],
        scratch=[pltpu.VMEM((T + 16, W), F32)] * 2, vmem_mb=56, comm=comm,
        operands=(gs_re, gs_im, st_re, st_im, su_b, ds_b, wb_re, wb_im, a_re, a_im))


def _mix_bwd_a(dyc_b, w_co4, pc, z_b, conv_w, dsu_ssm, du_dir, dproj, dres, w_mix4, tm, comm=()):
    T = dres.shape[0]
    nt = T // tm

    def body(dyc_ref, wco_ref, pc_ref, halo_ref, z_ref, cw_ref, dsu_ref, dud_ref, dpin_ref, dres_ref, w_ref,
             dp_ref, dx_ref, dcw_ref, dcb_ref, dzbuf, qbuf):
        i = pl.program_id(0)
        ii = nt - 1 - i

        @pl.when(i == 0)
        def _():
            dzbuf[pl.ds(tm, 8), :] = jnp.zeros((8, CONV), F32)

        dyc = dyc_ref[...]
        dyin = (_nt(dyc[:, 0:256], wco_ref[0]) + _nt(dyc[:, 256:512], wco_ref[1])
                + _nt(dyc[:, 512:768], wco_ref[2]) + _nt(dyc[:, 768:1024], wco_ref[3]))
        cbv = pc_ref[:, 0:CONV].astype(F32)
        ccv = pc_ref[:, CONV:2 * CONV].astype(F32)
        chv = pc_ref[:, 2 * CONV:3 * CONV].astype(F32)
        dcbv = dyin * z_ref[...].astype(F32)
        dz = dyin * cbv
        dzbuf[pl.ds(0, tm), :] = dz
        cw = cw_ref[...]
        dq = cw[2:3] * dz + cw[1:2] * dzbuf[pl.ds(1, tm), :] + cw[0:1] * dzbuf[pl.ds(2, tm), :]
        dzbuf[pl.ds(tm, 8), :] = dz[0:8]
        q = ccv * chv
        hq = halo_ref[:, CONV:2 * CONV].astype(F32) * halo_ref[:, 2 * CONV:3 * CONV].astype(F32)
        qbuf[pl.ds(0, 8), :] = jnp.where(ii > 0, hq, jnp.zeros_like(hq))
        qbuf[pl.ds(8, tm), :] = q
        pw = jnp.concatenate([_rowsum(dz * qbuf[pl.ds(6, tm), :]), _rowsum(dz * qbuf[pl.ds(7, tm), :]),
                              _rowsum(dz * q), jnp.zeros((5, CONV), F32)], axis=0)
        pbias = _rowsum(dz)

        @pl.when(i == 0)
        def _():
            dcw_ref[...] = pw
            dcb_ref[...] = pbias

        @pl.when(i > 0)
        def _():
            dcw_ref[...] += pw
            dcb_ref[...] += pbias

        dp0 = jnp.concatenate([dcbv, dq * chv], axis=1).astype(BF16)
        dp1 = jnp.concatenate([dq * ccv, dsu_ref[...] + dud_ref[...]], axis=1).astype(BF16)
        dp_ref[:, 0:D] = dp0
        dp_ref[:, D:2 * D] = dp1
        dx_ref[...] = (dres_ref[...] + _nt(dp0, w_ref[0]) + _nt(dp1, w_ref[1])
                       + _nt(dpin_ref[:, 0:D], w_ref[2]) + _nt(dpin_ref[:, D:2 * D], w_ref[3]))

    def tok(n):
        return pl.BlockSpec((tm, n), lambda i: (nt - 1 - i, 0))

    def full(shape):
        return pl.BlockSpec(shape, lambda i: (0,) * len(shape))

    halo = pl.BlockSpec((8, 3 * CONV), lambda i: (jnp.maximum((nt - 1 - i) * (tm // 8) - 1, 0), 0))
    return _pcall(
        body, name="mix_bwd_a", grid=(nt,),
        in_specs=[tok(D), pl.BlockSpec((4, CONV, 256), lambda i: (0, 0, 0), pipeline_mode=pl.Buffered(1)),
                  tok(3 * CONV), halo, tok(CONV), full((3, CONV)),
                  tok(SSM), tok(SSM), pl.BlockSpec((tm, 2 * D), lambda i: (nt - 1 - i, 1)), tok(D),
                  pl.BlockSpec((4, D, D), lambda i: (0, 0, 0), pipeline_mode=pl.Buffered(1))],
        out_specs=[pl.BlockSpec((tm, 2 * D), lambda i: (nt - 1 - i, 0)), tok(D), full((8, CONV)), full((1, CONV))],
        out_shape=[jax.ShapeDtypeStruct((T, 4 * D), BF16), jax.ShapeDtypeStruct((T, D), F32),
                   jax.ShapeDtypeStruct((8, CONV), F32), jax.ShapeDtypeStruct((1, CONV), F32)],
        scratch=[pltpu.VMEM((tm + 8, CONV), F32), pltpu.VMEM((tm + 8, CONV), F32)],
        aliases={8: 0}, vmem_mb=56, comm=comm,
        operands=(dyc_b, w_co4, pc, pc, z_b, conv_w, dsu_ssm, du_dir, dproj, dres, w_mix4))


def _zoh(lam_re, lam_im, log_step, b_re, b_im):
    dt = jnp.exp(log_step)[:, None]
    mag = jnp.exp(lam_re * dt)
    abr, abi = mag * jnp.cos(lam_im * dt), mag * jnp.sin(lam_im * dt)
    nr, ni = abr - 1.0, abi
    den = lam_re * lam_re + lam_im * lam_im
    cr = (nr * lam_re + ni * lam_im) / den
    ci = (ni * lam_re - nr * lam_im) / den
    bbr = cr[..., None] * b_re - ci[..., None] * b_im
    bbi = cr[..., None] * b_im + ci[..., None] * b_re
    return abr, abi, bbr, bbi


_WB_MASK = (np.arange(8)[None, :, None]
            == SCAN_GR * np.arange(SCAN_PER)[:, None, None] + np.arange(SCAN_GR)[None, None, :]).astype(np.float32)
_EYE8 = np.eye(8, dtype=np.float32)


def _wb_blocks(bb):
    bt = bb.transpose(0, 2, 1).reshape(4, 1, 8, 16, 1, STATE)
    full = bt * _WB_MASK[None, :, :, None, :, None]
    return full.reshape(LANES // SCAN_W, 128, SCAN_W).astype(BF16)


def _wc_blocks(cc):
    ct = cc.transpose(0, 2, 1).reshape(4, 8, STATE, 1, 16)
    full = ct * _EYE8[None, :, None, :, None]
    return full.reshape(4, 512, 128).astype(BF16)


def _wb_diag(dwb):
    d6 = dwb.reshape(4, SCAN_PER, 8, 16, SCAN_GR, STATE) * _WB_MASK[None, :, :, None, :, None]
    return d6.sum(axis=(1, 4)).reshape(GROUPS, 16, STATE).transpose(0, 2, 1)


def _wc_diag(dwc):
    mask = _WB_MASK.transpose(0, 2, 1)
    d6 = dwc.reshape(4, SCAN_PER, SCAN_GR, STATE, 8, 16) * mask[None, :, :, None, :, None]
    return d6.sum(axis=4).reshape(GROUPS, STATE, 16).transpose(0, 2, 1)


def _where():
    x, y, c = lax.axis_index("x"), lax.axis_index("y"), lax.axis_index("c")
    return x, y, c, 2 * x + y


def _chip_dev(k, c):
    return (k // 2, k % 2, c)


def _slot_cast(meidx, w, dtype, name, token=()):
    R, C = w.shape
    tr = _row_tile(R)

    def body(m_ref, w_ref, *rest):
        rest[-1][...] = w_ref[...].astype(dtype)

    gs = pltpu.PrefetchScalarGridSpec(
        num_scalar_prefetch=1, grid=(R // tr,),
        in_specs=[pl.BlockSpec((tr, C), lambda i, m: (i, 0))] + [pl.BlockSpec((8, 128), lambda i, m: (0, 0))] * len(token),
        out_specs=pl.BlockSpec((None, tr, C), lambda i, m: (m[0], i, 0)))
    return pl.pallas_call(
        body, name=name, grid_spec=gs, out_shape=_hbm_out(jax.ShapeDtypeStruct((4, R, C), dtype)),
        compiler_params=_cp(32, 1),
    )(meidx, *_hbm(w), *token)


def _gather_ici_payload(bufs):
    def copies(ins, lnd, ss, rs):
        x, y, c, me = _where()
        cps = []
        for w, b in enumerate(bufs):
            h = b.shape[1] // 2
            mine = lnd[w].at[me, pl.ds(c * h, h)]
            for s in range(3):
                k = (me + 1 + s) % 4
                cps.append(pltpu.make_async_remote_copy(
                    src_ref=mine, dst_ref=mine, send_sem=ss.at[3 * w + s], recv_sem=rs.at[3 * w + s],
                    device_id=_chip_dev(k, c), device_id_type=MESH))
        return cps

    p = _sym_payload([], [jax.ShapeDtypeStruct(b.shape, b.dtype) for b in bufs], copies, 3 * len(bufs))
    p.lands = list(bufs)
    return p


def _gather_pass_payload(bufs):
    def copies(ins, outs, ss, rs):
        x, y, c, me = _where()
        cps = []
        for w, b in enumerate(bufs):
            h = b.shape[1] // 2
            for s in range(3):
                j = (me + 1 + s) % 4
                cps.append(pltpu.make_async_remote_copy(
                    src_ref=ins[w].at[j, pl.ds(c * h, h)], dst_ref=outs[w].at[j, pl.ds(c * h, h)],
                    send_sem=ss.at[3 * w + s], recv_sem=rs.at[3 * w + s], device_id=(x, y, 1 - c),
                    device_id_type=MESH))
        return cps

    p = _sym_payload(bufs, [jax.ShapeDtypeStruct(b.shape, b.dtype) for b in bufs], copies, 3 * len(bufs))
    p.aliases = {w: w for w in range(len(bufs))}
    return p


def _gather_payload(bufs):
    n = len(bufs)

    def half(ref, w, k, cc):
        h = bufs[w].shape[1] // 2
        return ref.at[k, pl.ds(cc * h, h)]

    def ici(ins, outs, sems, w, s):
        x, y, c, me = _where()
        k = (me + 1 + s) % 4
        return pltpu.make_async_remote_copy(
            src_ref=half(ins[w], w, me, c), dst_ref=half(outs[w], w, me, c), send_sem=sems[0].at[3 * w + s],
            recv_sem=sems[1].at[3 * w + s], device_id=_chip_dev(k, c), device_id_type=MESH)

    def landed(outs, sems, w, s):
        x, y, c, me = _where()
        j = (me + 3 - s) % 4
        return pltpu.make_async_remote_copy(
            src_ref=half(outs[w], w, j, c), dst_ref=half(outs[w], w, j, c), send_sem=sems[0].at[3 * w + s],
            recv_sem=sems[1].at[3 * w + s], device_id=(x, y, 1 - c), device_id_type=MESH)

    def passed(outs, sems, w, s, cc):
        x, y, c, me = _where()
        j = (me + 3 - s) % 4
        return pltpu.make_async_remote_copy(
            src_ref=half(outs[w], w, j, cc), dst_ref=half(outs[w], w, j, cc), send_sem=sems[2].at[3 * w + s],
            recv_sem=sems[3].at[3 * w + s], device_id=(x, y, 1 - c), device_id_type=MESH)

    pairs = [(w, s) for w in range(n) for s in range(3)]

    def start(ins, outs, sems):
        for w, s in pairs:
            ici(ins, outs, sems, w, s).start()

    def finish(ins, outs, sems):
        _, _, c, _ = _where()
        for w, s in pairs:
            landed(outs, sems, w, s).wait_recv()
            passed(outs, sems, w, s, c).start()
        for w, s in pairs:
            passed(outs, sems, w, s, 1 - c).wait_recv()
        for w, s in pairs:
            ici(ins, outs, sems, w, s).wait_send()
            passed(outs, sems, w, s, c).wait_send()

    return _Payload(bufs, [jax.ShapeDtypeStruct(b.shape, b.dtype) for b in bufs], {w: w for w in range(n)},
                    [pltpu.SemaphoreType.DMA((3 * n,))] * 4, start, finish)


def _sym_payload(operands, outs, copies, n_copies):
    def start(ins, outs_, sems):
        for cp in copies(ins, outs_, sems[0], sems[1]):
            cp.start()

    def finish(ins, outs_, sems):
        for cp in copies(ins, outs_, sems[0], sems[1]):
            cp.wait()

    p = _Payload(operands, outs, {}, [pltpu.SemaphoreType.DMA((n_copies,))] * 2, start, finish)
    p.copies, p.n_copies = copies, n_copies
    return p


def _swap_payload(g4s):
    def copies(ins, outs, ss, rs):
        x, y, c, me = _where()
        cps = []
        for w, g in enumerate(g4s):
            h = g.shape[1] // 2
            cps.append(pltpu.make_async_remote_copy(
                src_ref=ins[w].at[:, pl.ds((1 - c) * h, h)], dst_ref=outs[w], send_sem=ss.at[w],
                recv_sem=rs.at[w], device_id=(x, y, 1 - c), device_id_type=MESH))
        return cps

    outs = [jax.ShapeDtypeStruct((4, g.shape[1] // 2, g.shape[2]), g.dtype) for g in g4s]
    return _sym_payload(g4s, outs, copies, len(g4s))


def _exchange_payload(pbs):
    def copies(ins, outs, ss, rs):
        x, y, c, me = _where()
        cps = []
        for w in range(len(pbs)):
            for s in range(3):
                k = (me + 1 + s) % 4
                cps.append(pltpu.make_async_remote_copy(
                    src_ref=ins[w].at[k], dst_ref=outs[w].at[2 - s], send_sem=ss.at[3 * w + s],
                    recv_sem=rs.at[3 * w + s], device_id=_chip_dev(k, c), device_id_type=MESH))
        return cps

    outs = [jax.ShapeDtypeStruct((3,) + p.shape[1:], p.dtype) for p in pbs]
    return _sym_payload(pbs, outs, copies, 3 * len(pbs))


HBM_REF = pl.BlockSpec(memory_space=pltpu.HBM)
SEM_REF = pl.BlockSpec(memory_space=pltpu.SEMAPHORE)
DATAFLOW = pltpu.SideEffectType.DATAFLOW_SIDE_EFFECTING


class _SemList:
    def __init__(self, refs):
        self.refs = refs

    @property
    def at(self):
        return self.refs


def _split_start(p, name):
    n_in, n_out, nc = len(p.operands), len(p.outs), p.n_copies
    lands = getattr(p, "lands", None) or [lax.empty(s.shape, s.dtype) for s in p.outs]

    def body(*refs):
        ins, lnd = refs[:n_in], refs[n_in:n_in + n_out]
        sems = refs[n_in + n_out:n_in + n_out + 2 * nc]
        for cp in p.copies(ins, lnd, _SemList(sems[:nc]), _SemList(sems[nc:])):
            cp.start()
        refs[-1][...] = jnp.zeros((8, 128), F32)

    res = pl.pallas_call(
        body, name=name,
        in_specs=[HBM_REF] * (n_in + n_out),
        out_specs=[SEM_REF] * (2 * nc) + [HBM_REF] * (n_in + n_out) + [VMEM_FULL],
        out_shape=([pltpu.SemaphoreType.DMA(())] * (2 * nc) + _hbm_out(p.operands) + _hbm_out(lands)
                   + [jax.ShapeDtypeStruct((8, 128), F32)]),
        input_output_aliases={i: 2 * nc + i for i in range(n_in + n_out)},
        compiler_params=pltpu.CompilerParams(has_side_effects=DATAFLOW),
    )(*_hbm(*p.operands, *lands))
    k = 2 * nc
    return list(res[:k]), list(res[k:k + n_in]), list(res[k + n_in:k + n_in + n_out]), res[-1]


def _split_wait(p, handle, after, name):
    sems, srcs, lands, _ = handle
    n_in, n_out, nc = len(srcs), len(lands), p.n_copies

    def body(*refs):
        ins, lnd = refs[:n_in], refs[n_in:n_in + n_out]
        sm = refs[n_in + n_out:n_in + n_out + 2 * nc]
        for cp in p.copies(ins, lnd, _SemList(sm[:nc]), _SemList(sm[nc:])):
            cp.wait_send()
            cp.wait_recv()

    res = pl.pallas_call(
        body, name=name,
        in_specs=[HBM_REF] * (n_in + n_out) + [SEM_REF] * (2 * nc) + [ANY] * len(after),
        out_specs=[HBM_REF] * (n_in + n_out), out_shape=_hbm_out(srcs) + _hbm_out(lands),
        input_output_aliases={i: i for i in range(n_in + n_out)},
        compiler_params=pltpu.CompilerParams(has_side_effects=DATAFLOW),
    )(*srcs, *lands, *sems, *after)
    return list(res[:n_in]), list(res[n_in:])


def _join_payload(halves):
    def copies(ins, outs, ss, rs):
        x, y, c, me = _where()
        return [pltpu.make_async_remote_copy(
            src_ref=ins[w], dst_ref=outs[w], send_sem=ss.at[w], recv_sem=rs.at[w],
            device_id=(x, y, 1 - c), device_id_type=MESH) for w in range(len(halves))]

    outs = [jax.ShapeDtypeStruct(a.shape, a.dtype) for a in halves]
    return _sym_payload(halves, outs, copies, len(halves))


def _allgather_payload(v):
    def copies(ins, outs, ss, rs):
        x, y, c, me = _where()
        lin = 4 * x + 2 * y + c
        cps = []
        for o in range(1, 8):
            t = (lin + o) % 8
            cps.append(pltpu.make_async_remote_copy(
                src_ref=ins[0], dst_ref=outs[0].at[lin], send_sem=ss.at[o - 1], recv_sem=rs.at[o - 1],
                device_id=(t // 4, (t // 2) % 2, t % 2), device_id_type=MESH))
        return cps

    p = _sym_payload([v], [jax.ShapeDtypeStruct((8,) + v.shape, v.dtype)], copies, 7)
    x, y, c, _ = _where()
    p.lands = [lax.dynamic_update_slice(jnp.zeros((8,) + v.shape, v.dtype), v[None], (4 * x + 2 * y + c, 0, 0))]
    return p


def _sum8(buf, token):
    _, P, C = buf.shape

    def body(b_ref, t_ref, o_ref):
        acc = b_ref[0]
        for d in range(1, 8):
            acc = acc + b_ref[d]
        o_ref[...] = acc

    return pl.pallas_call(
        body, name="sum8", in_specs=[VMEM_FULL, VMEM_FULL], out_specs=VMEM_FULL,
        out_shape=jax.ShapeDtypeStruct((P, C), F32),
        compiler_params=pltpu.CompilerParams(vmem_limit_bytes=32 << 20),
    )(buf, token)


def _row_tile(h):
    for t in (256, 176, 128, 64, 32, 16, 8):
        if h % t == 0:
            return t
    raise ValueError(h)


def _pair_sum(cmidx, g4, got, name):
    _, R, C = g4.shape
    h = R // 2
    th = _row_tile(h)

    def body(cm_ref, a_ref, b_ref, o_ref, ob_ref):
        sm = a_ref[...] + b_ref[...]
        ob_ref[...] = sm.astype(BF16)

        @pl.when(pl.program_id(1) == cm_ref[1])
        def _():
            o_ref[...] = sm

    blk = pl.BlockSpec((None, th, C), lambda i, k, cm: (k, i, 0))
    gs = pltpu.PrefetchScalarGridSpec(
        num_scalar_prefetch=1, grid=(h // th, 4),
        in_specs=[pl.BlockSpec((None, None, th, C), lambda i, k, cm: (k, cm[0], i, 0)), blk],
        out_specs=[pl.BlockSpec((th, C), lambda i, k, cm: (i, 0)), blk])
    return pl.pallas_call(
        body, name=name, grid_spec=gs,
        out_shape=_hbm_out([jax.ShapeDtypeStruct((h, C), F32), jax.ShapeDtypeStruct((4, h, C), BF16)]),
        compiler_params=_cp(32, 2),
    )(cmidx, *_hbm(g4.reshape(4, 2, h, C), got))


def _chip_sum(own, got, name):
    h, C = own.shape
    th = _row_tile(h)

    def body(a_ref, b_ref, o_ref):
        o_ref[...] = ((a_ref[...] + b_ref[0].astype(F32)) + b_ref[1].astype(F32)) + b_ref[2].astype(F32)

    return pl.pallas_call(
        body, name=name, grid=(h // th,),
        in_specs=[pl.BlockSpec((th, C), lambda i: (i, 0)), pl.BlockSpec((3, th, C), lambda i: (0, i, 0))],
        out_specs=pl.BlockSpec((th, C), lambda i: (i, 0)),
        out_shape=_hbm_out(jax.ShapeDtypeStruct((h, C), F32)),
        compiler_params=_cp(32, 1),
    )(*_hbm(own, got))


def _adamw_math(w, g, m, v):
    m2 = B1 * m + (1.0 - B1) * g
    v2 = B2 * v + (1.0 - B2) * (g * g)
    m_hat = m2 / (1.0 - B1 ** STEP)
    v_hat = v2 / (1.0 - B2 ** STEP)
    delta = -LR * (m_hat / (jnp.sqrt(v_hat) + EPS) + WD * w)
    return delta, m2, v2


def _adamw_pair(cidx, w, mine, theirs, m, v, token, name):
    R, C = w.shape
    h = R // 2
    tr = _row_tile(h)
    nh = h // tr

    def body(c_ref, w_ref, a_ref, b_ref, m_ref, v_ref, t_ref, g_ref, d_ref, mo_ref, vo_ref):
        own = (pl.program_id(0) // nh) == c_ref[0]
        g = jnp.where(own, a_ref[...], b_ref[...])
        d, m2, v2 = _adamw_math(w_ref[...], g, m_ref[...], v_ref[...])
        g_ref[...] = g
        d_ref[...] = d
        mo_ref[...] = m2
        vo_ref[...] = v2

    blk = pl.BlockSpec((tr, C), lambda i, c: (i, 0))
    mine_blk = pl.BlockSpec((tr, C), lambda i, c: (jnp.clip(i - c[0] * nh, 0, nh - 1), 0))
    theirs_blk = pl.BlockSpec((tr, C), lambda i, c: (jnp.clip(i - (1 - c[0]) * nh, 0, nh - 1), 0))
    gs = pltpu.PrefetchScalarGridSpec(
        num_scalar_prefetch=1, grid=(R // tr,),
        in_specs=[blk, mine_blk, theirs_blk, blk, blk, pl.BlockSpec((8, 128), lambda i, c: (0, 0))],
        out_specs=[blk] * 4)
    return pl.pallas_call(
        body, name=name, grid_spec=gs, out_shape=_hbm_out([jax.ShapeDtypeStruct((R, C), F32)] * 4),
        compiler_params=_cp(32, 1),
    )(cidx, *_hbm(w, mine, theirs, m, v), token)


def _adamw(w, g, m, v, name):
    R, C = w.shape
    tr = _row_tile(R)

    def body(w_ref, g_ref, m_ref, v_ref, d_ref, mo_ref, vo_ref):
        d, m2, v2 = _adamw_math(w_ref[...], g_ref[...], m_ref[...], v_ref[...])
        d_ref[...] = d
        mo_ref[...] = m2
        vo_ref[...] = v2

    blk = pl.BlockSpec((tr, C), lambda i: (i, 0))
    return pl.pallas_call(
        body, name=name, grid=(R // tr,), in_specs=[blk] * 4, out_specs=[blk] * 3,
        out_shape=_hbm_out([jax.ShapeDtypeStruct((R, C), F32)] * 3),
        compiler_params=_cp(32, 1),
    )(*_hbm(w, g, m, v))


def _pack(arrs):
    flat = jnp.concatenate([a.reshape(-1).astype(F32) for a in arrs])
    rows = -(-flat.shape[0] // 1024)
    rows = -(-rows // 8) * 8
    return jnp.pad(flat, (0, rows * 1024 - flat.shape[0])).reshape(rows, 1024)


def _unpack(packed, shapes):
    flat = packed.reshape(-1)
    out, off = [], 0
    for s in shapes:
        n = math.prod(s)
        out.append(flat[off:off + n].reshape(s))
        off += n
    return out


BIG = ["ffn1_w_in", "ffn1_w_out", "mix_w_in", "conv_w_out", "ssm_w_glu", "mix_w_out",
       "ffn2_w_in", "ffn2_w_out", "ple_w_in", "ple_w_gate"]
SMALL = ["ln1_g", "ln1_b", "conv_w", "conv_b", "ssm_lam_re", "ssm_lam_im", "ssm_log_step", "ssm_b_re", "ssm_b_im",
         "ssm_c_re", "ssm_c_im", "ssm_d", "ln2_g", "ln2_b", "ln3_g", "ln3_b", "ln4_g", "ln4_b"]
WEIGHTS = ["ffn1_w_in", "ffn1_w_out", "ln1_g", "ln1_b", "mix_w_in", "conv_w", "conv_b", "conv_w_out",
           "ssm_lam_re", "ssm_lam_im", "ssm_log_step", "ssm_b_re", "ssm_b_im", "ssm_c_re", "ssm_c_im", "ssm_d",
           "ssm_w_glu", "mix_w_out", "ln2_g", "ln2_b", "ffn2_w_in", "ffn2_w_out", "ln3_g", "ln3_b",
           "ple_w_in", "ple_w_gate", "ln4_g", "ln4_b"]


def _s5_operands(sp):
    abr, abi, bbr, bbi = _zoh(sp["ssm_lam_re"], sp["ssm_lam_im"], sp["ssm_log_step"], sp["ssm_b_re"], sp["ssm_b_im"])
    return (_wb_blocks(bbr), _wb_blocks(bbi), _wc_blocks(sp["ssm_c_re"]), _wc_blocks(-sp["ssm_c_im"]),
            abr.reshape(1, LANES), abi.reshape(1, LANES), sp["ssm_d"].reshape(1, SSM))


def _local_step(x, p, target, sp, ops, sched):
    W = sched.W
    wb_re, wb_im, wc_re4, wc_im4, a_re, a_im, dvec = ops
    tm = TOKEN_TILE

    def run(fn, name, *args, **kw):
        outs, got = fn(*args, comm=sched.carry(name), **kw)
        sched.landed(name, got)
        sched.done[name] = outs[0]
        return outs

    def dw(name, wname, a, b, tk, tn, shape4, shard_cols=None, interleaved=False):
        out, got = _mm_tn(a, b, tk, tn, name, shard_cols=shard_cols, interleaved=interleaved,
                          comm=sched.carry(name))
        sched.landed(name, got)
        sched.done[name] = out
        sched.grad(wname, out.reshape(shape4))

    h1, r1, x1, x1b, xb = run(_ffn_fwd, "ffn1_fwd", x, W["ffn1_w_in"], W["ffn1_w_out"].reshape(2, FFH, D),
                              sp["ln1_g"], sp["ln1_b"], tm, "ffn1_fwd")
    conv_w = W["conv_w"][:, 0:3, :].transpose(1, 0, 2).reshape(3, CONV)
    pc, z_b, yin_b, su, su_b, g_conv, g_ssm, y_conv = run(
        _mix_fwd_a, "mix_fwd_a", x1b, W["mix_w_in"], conv_w, sp["conv_b"], W["conv_w_out"], tm)
    st_re, st_im = run(_s5_scan_fwd, "s5_scan_fwd", su_b, wb_re, wb_im, a_re, a_im)
    w_mo = W["mix_w_out"].reshape(D, D)
    s, sg_b, ga, gb, merged_b, r2, x2 = run(
        _mix_fwd_b, "mix_fwd_b", st_re, st_im, wc_re4, wc_im4, su, dvec, W["ssm_w_glu"], g_conv, g_ssm, y_conv,
        w_mo, x1, sp["ln2_g"], sp["ln2_b"], tm)
    w2o2 = W["ffn2_w_out"].reshape(2, FFH, D)
    h2, r3, x3, x3b, x2b = run(_ffn_fwd, "ffn2_fwd", x2, W["ffn2_w_in"], w2o2, sp["ln3_g"], sp["ln3_b"], tm,
                               "ffn2_fwd")
    loss_part, dx3, p_b, dpw_b, dgt_b, dg4, db4 = _ple_loss(
        x3, x3b, p, W["ple_w_in"], W["ple_w_gate"].reshape(D, D), sp["ln4_g"], sp["ln4_b"], target, 2 * tm)

    dw("dw_ple_gate", "ple_w_gate", x3b, dgt_b, 512, 1024, (4, 256, D))
    dw("dw_ple_in", "ple_w_in", p_b, dpw_b, 256, 256, (4, 256, 256), shard_cols=256)
    dx2, dh2, a2_b, df2_b, dg3, db3 = run(_ffn_bwd, "ffn2_bwd", dx3, r3, sp["ln3_g"], h2, W["ffn2_w_in"], w2o2,
                                          tm, "ffn2_bwd")
    dw("dw_ffn2_in", "ffn2_w_in", x2b, dh2, 512, FFH, (4, D, FFH), shard_cols=FFH, interleaved=True)
    dw("dw_ffn2_out", "ffn2_w_out", a2_b, df2_b, FFH, 1024, (4, FF // 4, D))
    (dres, dmix_b, dgl_b, ds_b, du_dir, gs_re, gs_im, dyc_b, dproj, dg2, db2, dd) = run(
        _mix_bwd_b, "mix_bwd_b", dx2, r2, sp["ln2_g"], w_mo, g_conv, g_ssm, y_conv, ga, gb, s, su, dvec,
        W["ssm_w_glu"], wc_re4, wc_im4, tm)
    dw("dw_mix_out", "mix_w_out", merged_b, dmix_b, 512, 1024, (4, 256, D))
    dw("dw_glu", "ssm_w_glu", sg_b, dgl_b, 512, 512, (4, SSM, 512), shard_cols=512)
    dsu_ssm, dwb_re, dwb_im, dwc_re, dwc_im, da_re, da_im = run(
        _s5_scan_bwd, "s5_scan_bwd", gs_re, gs_im, st_re, st_im, su_b, ds_b, wb_re, wb_im, a_re, a_im)
    dw("dw_conv_out", "conv_w_out", yin_b, dyc_b, 512, 256, (4, CONV, 256), shard_cols=256)
    dproj, dx1, dcw8, dcb = run(_mix_bwd_a, "mix_bwd_a", dyc_b, W["conv_w_out"], pc, z_b, conv_w, dsu_ssm,
                                du_dir, dproj, dres, W["mix_w_in"], 2 * tm)
    dw("dw_mix_in", "mix_w_in", x1b, dproj, 512, 1024, (4, D, D), shard_cols=1024)
    dx0, dh1, a1_b, df1_b, dg1, db1 = run(_ffn_bwd, "ffn1_bwd", dx1, r1, sp["ln1_g"], h1, W["ffn1_w_in"],
                                          W["ffn1_w_out"].reshape(2, FFH, D), tm, "ffn1_bwd")
    sched.small(dict(
        ln1_g=dg1, ln1_b=db1, ln2_g=dg2, ln2_b=db2, ln3_g=dg3, ln3_b=db3, ln4_g=dg4, ln4_b=db4,
        conv_w=dcw8[0:3], conv_b=dcb,
        a_re=da_re.reshape(GROUPS, STATE), a_im=da_im.reshape(GROUPS, STATE),
        bb_re=_wb_diag(dwb_re), bb_im=_wb_diag(dwb_im),
        ssm_c_re=_wc_diag(dwc_re), ssm_c_im=-_wc_diag(dwc_im), ssm_d=dd.reshape(GROUPS, 16),
        loss=loss_part[0:1, 0]))
    dw("dw_ffn1_in", "ffn1_w_in", xb, dh1, 512, FFH, (4, D, FFH), shard_cols=FFH, interleaved=True)
    dw("dw_ffn1_out", "ffn1_w_out", a1_b, df1_b, FFH, 1024, (4, FF // 4, D))
    return loss_part[0, 0], dx0


RAW_ORDER = ["ln1_g", "ln1_b", "ln2_g", "ln2_b", "ln3_g", "ln3_b", "ln4_g", "ln4_b", "conv_w", "conv_b",
             "a_re", "a_im", "bb_re", "bb_im", "ssm_c_re", "ssm_c_im", "ssm_d", "loss"]

GATHER_FIRST = ["ffn1_w_in", "ffn1_w_out"]
GATHER_AT = {"ffn1_fwd": ["mix_w_in", "conv_w_out", "conv_w"], "mix_fwd_a": ["ssm_w_glu", "mix_w_out"],
             "s5_scan_fwd": ["ffn2_w_in"], "mix_fwd_b": ["ffn2_w_out"], "ffn2_fwd": ["ple_w_in", "ple_w_gate"]}
REDUCE_GROUP = {"ffn2": ["ple_w_gate", "ple_w_in", "ffn2_w_in", "ffn2_w_out"],
                "mix": ["mix_w_out", "ssm_w_glu", "conv_w_out", "mix_w_in"], "ffn1": ["ffn1_w_in", "ffn1_w_out"]}
REDUCE_AT = {"mix_bwd_b": [("swap", "ffn2")], "mix_bwd_a": [("join", "ffn2")]}
BEGIN_AT = {"dw_mix_out": [("exchange", "ffn2")], "ffn1_bwd": [("swap", "mix")],
            "dw_ffn1_in": [("small", None), ("exchange", "mix")]}
BEHIND = {"dw_glu": [("exchange", "ffn2")], "s5_scan_bwd": [("exchange", "ffn2")]}
END_AT = {"mix_bwd_a": [("exchange", "ffn2", ["dw_mix_out", "dw_glu", "s5_scan_bwd"])],
          "dw_ffn1_in": [("swap", "mix", ["ffn1_bwd"])]}
LAST_GROUP = "ffn1"


class _Sched:
    def __init__(self, cmidx):
        self.bufs, self.cmidx = {}, cmidx
        self.W, self.G, self.raw, self.small_buf = {}, {}, None, None
        self.got1, self.p32, self.pbf, self.got2, self.half, self.theirs = {}, {}, {}, {}, {}, {}
        self._open, self._split, self.done = [], {}, {}

    def first_begin(self, bufs):
        self.bufs.update(bufs)
        p = _gather_ici_payload([bufs[n] for n in GATHER_FIRST])
        self._first = (p, _split_start(p, "gather_first_start"))
        return self._first[1][3]

    def first_end(self, bufs, after):
        self.bufs.update(bufs)
        p, handle = self._first
        _, landed = _split_wait(p, handle, after, "gather_first_wait")
        (outs,) = _comm_call("gather_first_pass", [_gather_pass_payload(landed)])
        self.W.update(zip(GATHER_FIRST, outs))

    def _payload(self, stage, key):
        if stage == "gather":
            return _gather_payload([self.bufs[n] for n in key])
        if stage == "small":
            return _allgather_payload(_pack([self.raw[k] for k in RAW_ORDER]))
        names = REDUCE_GROUP[key]
        if stage == "swap":
            return _swap_payload([self.G[n] for n in names])
        if stage == "exchange":
            for n in names:
                self.p32[n], self.pbf[n] = _pair_sum(self.cmidx, self.G[n], self.got1[n], "pair_sum_" + n)
            return _exchange_payload([self.pbf[n] for n in names])
        for n in names:
            self.half[n] = _chip_sum(self.p32[n], self.got2[n], "chip_sum_" + n)
        return _join_payload([self.half[n] for n in names])

    def _store(self, stages, got):
        for (stage, key), outs in zip(stages, got):
            if stage == "gather":
                self.W.update(zip(key, outs))
            elif stage == "small":
                self.small_buf = outs[0]
            else:
                {"swap": self.got1, "exchange": self.got2, "join": self.theirs}[stage].update(
                    zip(REDUCE_GROUP[key], outs))

    def _standalone(self, name, stages):
        self._store(stages, _comm_call(name, [self._payload(s, k) for s, k in stages]))

    def carry(self, name):
        for stage, key, behind in END_AT.get(name, []):
            self._end(stage, key, [self.done[b] for b in behind])
        tokens = [self._begin(stage, key) for stage, key in BEGIN_AT.get(name, [])]
        tokens += [self._split[sk][1][3] for sk in BEHIND.get(name, [])]
        self._open = [("gather", GATHER_AT[name])] if name in GATHER_AT else []
        self._open += REDUCE_AT.get(name, [])
        comm = [self._payload(s, k) for s, k in self._open]
        if tokens:
            comm.append(_Payload(tokens, [], {}, [], lambda *a: None, lambda *a: None))
        return tuple(comm)

    def landed(self, name, got):
        self._store(self._open, got)

    def grad(self, name, g4):
        self.G[name] = g4

    def small(self, raw):
        self.raw = raw

    def _begin(self, stage, key):
        p = self._payload(stage, key)
        self._split[stage, key] = (p, _split_start(p, "%s_%s_start" % (stage, key)))
        return self._split[stage, key][1][3]

    def _end(self, stage, key, after):
        p, handle = self._split.pop((stage, key))
        srcs, lands = _split_wait(p, handle, after, "%s_%s_wait" % (stage, key))
        if stage == "swap":
            self.G.update(zip(REDUCE_GROUP[key], srcs))
        self._store([(stage, key)], [lands])

    def tail_begin(self):
        return self._begin("swap", LAST_GROUP)

    def tail_mid(self, after):
        self._end("swap", LAST_GROUP, after)
        token = self._begin("exchange", LAST_GROUP)
        self._end("small", None, [token])
        self._end("exchange", "mix", [token])
        self._standalone("reduce_tail_join_mix", [("join", "mix")])
        return token

    def tail_end(self, after):
        self._end("exchange", LAST_GROUP, after)
        self._standalone("reduce_tail_join", [("join", LAST_GROUP)])


def _small_grads(raw_sum, sp):
    _, vjp = jax.vjp(_zoh, sp["ssm_lam_re"], sp["ssm_lam_im"], sp["ssm_log_step"], sp["ssm_b_re"], sp["ssm_b_im"])
    d_lre, d_lim, d_ls, d_bre, d_bim = vjp((raw_sum["a_re"], raw_sum["a_im"], raw_sum["bb_re"], raw_sum["bb_im"]))
    g = {k: raw_sum[k] for k in ("ln1_g", "ln1_b", "ln2_g", "ln2_b", "ln3_g", "ln3_b", "ln4_g", "ln4_b",
                                 "conv_w", "conv_b", "ssm_c_re", "ssm_c_im", "ssm_d")}
    g.update(ssm_lam_re=d_lre, ssm_lam_im=d_lim, ssm_log_step=d_ls, ssm_b_re=d_bre, ssm_b_im=d_bim)
    return g


def kernel(x, p, ffn1_w_in, ffn1_w_out, ln1_g, ln1_b, mix_w_in, conv_w, conv_b, conv_w_out, ssm_lam_re, ssm_lam_im, ssm_log_step, ssm_b_re, ssm_b_im, ssm_c_re, ssm_c_im, ssm_d, ssm_w_glu, mix_w_out, ln2_g, ln2_b, ffn2_w_in, ffn2_w_out, ln3_g, ln3_b, ple_w_in, ple_w_gate, ln4_g, ln4_b, loss_target, m_ffn1_w_in, m_ffn1_w_out, m_ln1_g, m_ln1_b, m_mix_w_in, m_conv_w, m_conv_b, m_conv_w_out, m_ssm_lam_re, m_ssm_lam_im, m_ssm_log_step, m_ssm_b_re, m_ssm_b_im, m_ssm_c_re, m_ssm_c_im, m_ssm_d, m_ssm_w_glu, m_mix_w_out, m_ln2_g, m_ln2_b, m_ffn2_w_in, m_ffn2_w_out, m_ln3_g, m_ln3_b, m_ple_w_in, m_ple_w_gate, m_ln4_g, m_ln4_b, v_ffn1_w_in, v_ffn1_w_out, v_ln1_g, v_ln1_b, v_mix_w_in, v_conv_w, v_conv_b, v_conv_w_out, v_ssm_lam_re, v_ssm_lam_im, v_ssm_log_step, v_ssm_b_re, v_ssm_b_im, v_ssm_c_re, v_ssm_c_im, v_ssm_d, v_ssm_w_glu, v_mix_w_out, v_ln2_g, v_ln2_b, v_ffn2_w_in, v_ffn2_w_out, v_ln3_g, v_ln3_b, v_ple_w_in, v_ple_w_gate, v_ln4_g, v_ln4_b):
    args = dict(locals())
    w = {n: args[n] for n in WEIGHTS}
    m = {n: args["m_" + n] for n in WEIGHTS}
    v = {n: args["v_" + n] for n in WEIGHTS}
    _, _, c, me = _where()
    cidx = jnp.stack([c, me]).astype(jnp.int32)
    meidx = jnp.reshape(me, (1,)).astype(jnp.int32)

    sched = _Sched(cidx)
    token = sched.first_begin({n: _slot_cast(meidx, w[n][0], BF16, "cast_" + n) for n in GATHER_FIRST})
    rest = {n: _slot_cast(meidx, w[n][0], BF16, "cast_" + n, (token,)) for n in BIG if n not in GATHER_FIRST}
    rest["conv_w"] = _slot_cast(meidx, jnp.pad(conv_w[0], ((0, 13), (0, 0))), F32, "cast_conv_w", (token,))
    sp = {n: (w[n] if w[n].ndim == 2 and n != "ssm_log_step" else w[n][0]) for n in SMALL if n != "conv_w"}
    ops = _s5_operands({**sp, "ssm_lam_re": sp["ssm_lam_re"] + token[0, 0]})
    sched.first_end(rest, list(rest.values()) + list(ops))
    loss_part, dx0 = _local_step(x[0], p[0, 0], loss_target[0], sp, ops, sched)
    out_g, out_d, out_m, out_v = {}, {}, {}, {}

    def big_adamw(names, token):
        for n in names:
            g, dl, mn, vn = _adamw_pair(cidx, w[n][0], sched.half[n], sched.theirs[n], m[n][0], v[n][0], token,
                                        "adamw_" + n)
            out_g[n], out_d[n], out_m[n], out_v[n] = g[None], dl[None], mn[None], vn[None]

    first = ["ple_w_gate", "ple_w_in", "ffn2_w_in"]
    big_adamw(first, sched.tail_begin())
    token = sched.tail_mid([out_v[n] for n in first])
    big_adamw(["ffn2_w_out"], token)

    raw_shapes = [sched.raw[k].shape for k in RAW_ORDER]
    raw_sum = dict(zip(RAW_ORDER, _unpack(_sum8(sched.small_buf, token), raw_shapes)))
    loss = raw_sum["loss"][0]
    sg = _small_grads(raw_sum, sp)
    sg["conv_w"] = lax.dynamic_slice_in_dim(sg["conv_w"], me * 128, 128, axis=1)
    small_shapes = [w[n].shape for n in SMALL]
    gp = _pack([sg[n] for n in SMALL])
    d_s, m_s, v_s = _adamw(_pack([w[n] for n in SMALL]), gp, _pack([m[n] for n in SMALL]),
                           _pack([v[n] for n in SMALL]), "adamw_small")

    for n, a, b_, c_, d_ in zip(SMALL, _unpack(gp, small_shapes), _unpack(d_s, small_shapes),
                                _unpack(m_s, small_shapes), _unpack(v_s, small_shapes)):
        out_g[n], out_d[n], out_m[n], out_v[n] = a, b_, c_, d_
    big_adamw(REDUCE_GROUP["mix"], token)
    sched.tail_end([d_s, out_v["ffn2_w_out"]] + [out_v[n] for n in REDUCE_GROUP["mix"]])
    big_adamw(REDUCE_GROUP[LAST_GROUP], token)

    return (loss, dx0[None], *[out_g[n] for n in WEIGHTS], *[out_d[n] for n in WEIGHTS],
            *[out_m[n] for n in WEIGHTS], *[out_v[n] for n in WEIGHTS])
```
